```python
import jax
import jax.numpy as jnp
from jax import lax
import numpy as np

D_MODEL = 1024
BATCH = 16
SEQ = 2048
DEPTH = 2

D_MIX = D_MODEL
N_GROUPS = 4
GROUP_WIDTH = D_MIX // N_GROUPS
A_HEADS = 4
A_DK = GROUP_WIDTH // A_HEADS
A_DV = GROUP_WIDTH // A_HEADS
A_CHUNK = 16
LB_FLOOR = 1e-30
B_HEADS = 4
B_Q_LORA = 256
B_KV_LORA = 128
B_NOPE = 64
B_ROPE = 32
B_V = GROUP_WIDTH // B_HEADS
ROPE_THETA = 10000.0
C_HEADS = 4
C_HEAD_DIM = GROUP_WIDTH // C_HEADS
FOX_GATE_BIAS = 3.0
D_GROUPS = 4
D_GROUP_DIM = GROUP_WIDTH // D_GROUPS
D_CHUNK = 128
Q_BLOCK = 128
D_FF = 2816
N_MOD = 9
ALPHA = (2 * DEPTH) ** 0.25
BETA = (8 * DEPTH) ** -0.25
LN_EPS = 1e-5
RMS_EPS = 1e-6
MIX_SPLIT_SIZES = (GROUP_WIDTH, GROUP_WIDTH, GROUP_WIDTH, GROUP_WIDTH,
                   B_Q_LORA, B_KV_LORA, B_ROPE,
                   GROUP_WIDTH, GROUP_WIDTH, GROUP_WIDTH, C_HEADS,
                   GROUP_WIDTH, GROUP_WIDTH)
MIX_IN_COLS = sum(MIX_SPLIT_SIZES)

kernel_name = "hybrid_hgrn2_mla_fox_gmlp_deepnorm_block"


def layer_norm(x, g, b):
    xf = x.astype(jnp.float32)
    mu = jnp.mean(xf, axis=-1, keepdims=True)
    var = jnp.mean(jnp.square(xf - mu), axis=-1, keepdims=True)
    return ((xf - mu) * lax.rsqrt(var + LN_EPS)).astype(x.dtype) * g + b


def rms_norm(x, g):
    xf = x.astype(jnp.float32)
    return (xf * lax.rsqrt(jnp.mean(xf * xf, axis=-1, keepdims=True) + RMS_EPS)).astype(x.dtype) * g


def swiglu_ffn(h, w_in, w_out):
    gate, up = jnp.split(h @ w_in, 2, axis=-1)
    return (jax.nn.silu(gate) * up) @ w_out


def rope(x, pos):
    half = x.shape[-1] // 2
    inv_freq = ROPE_THETA ** (-jnp.arange(half, dtype=jnp.float32) / half)
    ang = pos.astype(jnp.float32)[:, None] * inv_freq[None, :]
    cos = jnp.cos(ang)[None, :, None, :].astype(x.dtype)
    sin = jnp.sin(ang)[None, :, None, :].astype(x.dtype)
    x1, x2 = x[..., :half], x[..., half:]
    return jnp.concatenate([x1 * cos - x2 * sin, x1 * sin + x2 * cos], axis=-1)


def causal_softmax_attention(q, k, v, cum_log_f=None):
    b, s, h, dk = q.shape
    dv = v.shape[-1]
    n_blocks = s // Q_BLOCK
    scale = dk ** -0.5
    k_pos = jnp.arange(s)
    cum_t = None if cum_log_f is None else jnp.swapaxes(cum_log_f, 1, 2)

    def one_block(i):
        start = i * Q_BLOCK
        q_i = lax.dynamic_slice_in_dim(q, start, Q_BLOCK, axis=1)
        logits = jnp.einsum('bqhd,bkhd->bhqk', q_i, k,
                            preferred_element_type=jnp.float32) * scale
        if cum_t is not None:
            f_i = lax.dynamic_slice_in_dim(cum_t, start, Q_BLOCK, axis=2)
            logits = logits + (f_i[..., :, None] - cum_t[..., None, :])
        q_pos = start + jnp.arange(Q_BLOCK)
        logits = jnp.where(k_pos[None, :] <= q_pos[:, None], logits, -jnp.inf)
        p = jax.nn.softmax(logits, axis=-1).astype(v.dtype)
        return jnp.einsum('bhqk,bkhd->bqhd', p, v)

    out = lax.map(one_block, jnp.arange(n_blocks))
    return jnp.moveaxis(out, 0, 1).reshape(b, s, h * dv)


def hgrn2_mixer(q, f_logit, inp, g_out, lb, norm_g):
    b, s, _ = q.shape
    dt = q.dtype
    n_chunks = s // A_CHUNK
    lbf = lb.astype(jnp.float32)
    log_f = jnp.logaddexp(jnp.log(jnp.maximum(lbf, LB_FLOOR)),
                          jnp.log1p(-lbf) + jax.nn.log_sigmoid(f_logit.astype(jnp.float32)))
    k = -jnp.expm1(log_f)
    qf = jax.nn.silu(q.astype(jnp.float32))
    shp_k = (b, n_chunks, A_CHUNK, A_HEADS, A_DK)
    qf, k, log_f = qf.reshape(shp_k), k.reshape(shp_k), log_f.reshape(shp_k)
    v = inp.astype(jnp.float32).reshape(b, n_chunks, A_CHUNK, A_HEADS, A_DV)
    G = jnp.cumsum(log_f, axis=2)
    G_last = G[:, :, -1:]
    causal = jnp.tril(jnp.ones((A_CHUNK, A_CHUNK), dtype=bool))[None, None, :, :, None, None]
    rel = jnp.where(causal, G[:, :, :, None] - G[:, :, None, :], -jnp.inf)
    decay = jnp.exp(rel)
    scores = jnp.einsum('bctha,bcsha,bctsha->bchts', qf, k, decay)
    o_intra = jnp.einsum('bchts,bcshv->bcthv', scores, v)
    q_dec = qf * jnp.exp(G)
    k_to_end = k * jnp.exp(G_last - G)
    chunk_kv = jnp.einsum('bcsha,bcshv->cbhav', k_to_end, v)
    chunk_decay = jnp.transpose(jnp.exp(G_last[:, :, 0]), (1, 0, 2, 3))

    def step(state, xs):
        dec, kv = xs
        return dec[..., None] * state + kv, state

    state0 = jnp.zeros((b, A_HEADS, A_DK, A_DV), jnp.float32)
    _, state_in = lax.scan(step, state0, (chunk_decay, chunk_kv))
    o_inter = jnp.einsum('bctha,cbhav->bcthv', q_dec, state_in)
    o = (o_intra + o_inter).reshape(b, s, A_HEADS, A_DV)
    o = rms_norm(o, norm_g.astype(jnp.float32).reshape(A_HEADS, A_DV)).reshape(b, s, GROUP_WIDTH)
    return (o * jax.nn.silu(g_out.astype(jnp.float32))).astype(dt)


def mla_mixer(c_q, c_kv, k_rope, q_norm_g, kv_norm_g, w_uq, w_ukv, pos):
    b, s, _ = c_q.shape
    q = (rms_norm(c_q, q_norm_g) @ w_uq).reshape(b, s, B_HEADS, B_NOPE + B_ROPE)
    q_nope, q_rope = q[..., :B_NOPE], q[..., B_NOPE:]
    kv = (rms_norm(c_kv, kv_norm_g) @ w_ukv).reshape(b, s, B_HEADS, B_NOPE + B_V)
    k_nope, v = kv[..., :B_NOPE], kv[..., B_NOPE:]
    k_r = rope(k_rope[:, :, None, :], pos)
    q = jnp.concatenate([q_nope, rope(q_rope, pos)], axis=-1)
    k = jnp.concatenate([k_nope, jnp.broadcast_to(k_r, (b, s, B_HEADS, B_ROPE))], axis=-1)
    return causal_softmax_attention(q, k, v)


def fox_mixer(q, k, v, f_logit, b_f):
    b, s, _ = q.shape
    shp = (b, s, C_HEADS, C_HEAD_DIM)
    log_f = jax.nn.log_sigmoid(f_logit.astype(jnp.float32) + b_f.astype(jnp.float32))
    cum = jnp.cumsum(log_f, axis=1)
    return causal_softmax_attention(q.reshape(shp), k.reshape(shp), v.reshape(shp), cum)


def gmlp_mixer(u, v, ln_g, ln_b, w_s, b_s):
    b, s, _ = u.shape
    n_chunks = s // D_CHUNK
    u = jax.nn.gelu(u)
    v = layer_norm(jax.nn.gelu(v), ln_g, ln_b).reshape(b, n_chunks, D_CHUNK, D_GROUPS, D_GROUP_DIM)
    causal = jnp.tril(jnp.ones((D_CHUNK, D_CHUNK), dtype=bool))
    w = jnp.where(causal, w_s, 0.0)
    mixed = jnp.einsum('gts,bcsgd->bctgd', w, v) + jnp.swapaxes(b_s, 0, 1)[:, :, None]
    return u * mixed.reshape(b, s, GROUP_WIDTH)


def hybrid_token_mixer(h, w_in, w_out, lb, hgrn_norm_g, mla_q_norm_g, mla_kv_norm_g,
                       mla_w_uq, mla_w_ukv, fox_b_f, gmlp_ln_g, gmlp_ln_b, gmlp_w_s, gmlp_b_s):
    split_idx = [int(i) for i in np.cumsum(MIX_SPLIT_SIZES)[:-1]]
    proj = h @ w_in
    (a_q, a_f, a_i, a_g, b_cq, b_ckv, b_kr,
     c_q, c_k, c_v, c_f, d_u, d_v) = jnp.split(proj, split_idx, axis=-1)
    pos = jnp.arange(h.shape[1])
    o_a = hgrn2_mixer(a_q, a_f, a_i, a_g, lb, hgrn_norm_g)
    o_b = mla_mixer(b_cq, b_ckv, b_kr, mla_q_norm_g, mla_kv_norm_g, mla_w_uq, mla_w_ukv, pos)
    o_c = fox_mixer(c_q, c_k, c_v, c_f, fox_b_f)
    o_d = gmlp_mixer(d_u, d_v, gmlp_ln_g, gmlp_ln_b, gmlp_w_s, gmlp_b_s)
    return jnp.concatenate([o_a, o_b.astype(h.dtype), o_c.astype(h.dtype), o_d], axis=-1) @ w_out


def _fwd_setup_inputs(seed: int = 0) -> dict:
    key = jax.random.key(seed)
    ks = jax.random.split(key, 23)
    L = DEPTH

    def nrm(k, shape, scale):
        return scale * jax.random.normal(k, shape, jnp.float32)

    return {
        'x': nrm(ks[0], (BATCH, SEQ, D_MODEL), 1.0),
        'c': nrm(ks[1], (BATCH, D_MODEL), 1.0),
        'ada_w': nrm(ks[2], (L, D_MODEL, N_MOD * D_MODEL), 0.1 * D_MODEL ** -0.5),
        'ada_b': nrm(ks[3], (L, N_MOD * D_MODEL), 0.01),
        'ln_g': 1.0 + nrm(ks[4], (L, 3, D_MODEL), 0.02),
        'ln_b': nrm(ks[5], (L, 3, D_MODEL), 0.02),
        'ffn1_w_in': nrm(ks[6], (L, D_MODEL, 2 * D_FF), D_MODEL ** -0.5),
        'ffn1_w_out': nrm(ks[7], (L, D_FF, D_MODEL), BETA * D_FF ** -0.5),
        'ffn2_w_in': nrm(ks[8], (L, D_MODEL, 2 * D_FF), D_MODEL ** -0.5),
        'ffn2_w_out': nrm(ks[9], (L, D_FF, D_MODEL), BETA * D_FF ** -0.5),
        'mix_w_in': nrm(ks[10], (L, D_MODEL, MIX_IN_COLS), D_MODEL ** -0.5),
        'mix_w_out': nrm(ks[11], (L, D_MIX, D_MODEL), BETA * D_MIX ** -0.5),
        'hgrn_lb_logits': nrm(ks[12], (L, GROUP_WIDTH), 0.5),
        'hgrn_norm_g': 1.0 + nrm(ks[13], (L, GROUP_WIDTH), 0.02),
        'mla_q_norm_g': 1.0 + nrm(ks[14], (L, B_Q_LORA), 0.02),
        'mla_kv_norm_g': 1.0 + nrm(ks[15], (L, B_KV_LORA), 0.02),
        'mla_w_uq': nrm(ks[16], (L, B_Q_LORA, B_HEADS * (B_NOPE + B_ROPE)), B_Q_LORA ** -0.5),
        'mla_w_ukv': nrm(ks[17], (L, B_KV_LORA, B_HEADS * (B_NOPE + B_V)), B_KV_LORA ** -0.5),
        'fox_b_f': FOX_GATE_BIAS + nrm(ks[18], (L, C_HEADS), 0.5),
        'gmlp_ln_g': 1.0 + nrm(ks[19], (L, GROUP_WIDTH), 0.02),
        'gmlp_ln_b': nrm(ks[20], (L, GROUP_WIDTH), 0.02),
        'gmlp_w_s': nrm(ks[21], (L, D_GROUPS, D_CHUNK, D_CHUNK), 0.5 * D_CHUNK ** -0.5),
        'gmlp_b_s': 1.0 + nrm(ks[22], (L, D_GROUPS, D_CHUNK), 0.02),
    }


def _fwd_reference(x, c, ada_w, ada_b, ln_g, ln_b, ffn1_w_in, ffn1_w_out, ffn2_w_in, ffn2_w_out,
              mix_w_in, mix_w_out, hgrn_lb_logits, hgrn_norm_g, mla_q_norm_g, mla_kv_norm_g,
              mla_w_uq, mla_w_ukv, fox_b_f, gmlp_ln_g, gmlp_ln_b, gmlp_w_s, gmlp_b_s):
    lb_sm = jax.nn.softmax(hgrn_lb_logits.astype(jnp.float32), axis=0)
    lb_all = jnp.cumsum(lb_sm, axis=0) - lb_sm[0]
    c_act = jax.nn.silu(c)
    for l in range(DEPTH):
        mod = (c_act @ ada_w[l] + ada_b[l])[:, None, :]
        sh1, sc1, g1, sh2, sc2, g2, sh3, sc3, g3 = jnp.split(mod, N_MOD, axis=-1)
        h = x * (1.0 + sc1) + sh1
        x = layer_norm(ALPHA * x + 0.5 * (1.0 + g1) * swiglu_ffn(h, ffn1_w_in[l], ffn1_w_out[l]),
                       ln_g[l, 0], ln_b[l, 0])
        h = x * (1.0 + sc2) + sh2
        mixed = hybrid_token_mixer(h, mix_w_in[l], mix_w_out[l], lb_all[l], hgrn_norm_g[l],
                                   mla_q_norm_g[l], mla_kv_norm_g[l], mla_w_uq[l], mla_w_ukv[l],
                                   fox_b_f[l], gmlp_ln_g[l], gmlp_ln_b[l], gmlp_w_s[l], gmlp_b_s[l])
        x = layer_norm(ALPHA * x + (1.0 + g2) * mixed, ln_g[l, 1], ln_b[l, 1])
        h = x * (1.0 + sc3) + sh3
        x = layer_norm(ALPHA * x + 0.5 * (1.0 + g3) * swiglu_ffn(h, ffn2_w_in[l], ffn2_w_out[l]),
                       ln_g[l, 2], ln_b[l, 2])
    return x


import jax as _jax
import jax.numpy as _jnp

TWIN_FORMAT = 'train_step'
FWD_PARAMS = ['x', 'c', 'ada_w', 'ada_b', 'ln_g', 'ln_b', 'ffn1_w_in', 'ffn1_w_out', 'ffn2_w_in', 'ffn2_w_out', 'mix_w_in', 'mix_w_out', 'hgrn_lb_logits', 'hgrn_norm_g', 'mla_q_norm_g', 'mla_kv_norm_g', 'mla_w_uq', 'mla_w_ukv', 'fox_b_f', 'gmlp_ln_g', 'gmlp_ln_b', 'gmlp_w_s', 'gmlp_b_s']
TWIN_WEIGHTS = ['ada_w', 'ada_b', 'ln_g', 'ln_b', 'ffn1_w_in', 'ffn1_w_out', 'ffn2_w_in', 'ffn2_w_out', 'mix_w_in', 'mix_w_out', 'hgrn_lb_logits', 'hgrn_norm_g', 'mla_q_norm_g', 'mla_kv_norm_g', 'mla_w_uq', 'mla_w_ukv', 'fox_b_f', 'gmlp_ln_g', 'gmlp_ln_b', 'gmlp_w_s', 'gmlp_b_s']
TWIN_DIFF_INPUT = 'x'
TWIN_INPUTS = ['x', 'c', 'ada_w', 'ada_b', 'ln_g', 'ln_b', 'ffn1_w_in', 'ffn1_w_out', 'ffn2_w_in', 'ffn2_w_out', 'mix_w_in', 'mix_w_out', 'hgrn_lb_logits', 'hgrn_norm_g', 'mla_q_norm_g', 'mla_kv_norm_g', 'mla_w_uq', 'mla_w_ukv', 'fox_b_f', 'gmlp_ln_g', 'gmlp_ln_b', 'gmlp_w_s', 'gmlp_b_s', 'loss_target', 'm_ada_w', 'm_ada_b', 'm_ln_g', 'm_ln_b', 'm_ffn1_w_in', 'm_ffn1_w_out', 'm_ffn2_w_in', 'm_ffn2_w_out', 'm_mix_w_in', 'm_mix_w_out', 'm_hgrn_lb_logits', 'm_hgrn_norm_g', 'm_mla_q_norm_g', 'm_mla_kv_norm_g', 'm_mla_w_uq', 'm_mla_w_ukv', 'm_fox_b_f', 'm_gmlp_ln_g', 'm_gmlp_ln_b', 'm_gmlp_w_s', 'm_gmlp_b_s', 'v_ada_w', 'v_ada_b', 'v_ln_g', 'v_ln_b', 'v_ffn1_w_in', 'v_ffn1_w_out', 'v_ffn2_w_in', 'v_ffn2_w_out', 'v_mix_w_in', 'v_mix_w_out', 'v_hgrn_lb_logits', 'v_hgrn_norm_g', 'v_mla_q_norm_g', 'v_mla_kv_norm_g', 'v_mla_w_uq', 'v_mla_w_ukv', 'v_fox_b_f', 'v_gmlp_ln_g', 'v_gmlp_ln_b', 'v_gmlp_w_s', 'v_gmlp_b_s']
TWIN_OUTPUTS = ['loss', 'grad_x', 'grad_ada_w', 'grad_ada_b', 'grad_ln_g', 'grad_ln_b', 'grad_ffn1_w_in', 'grad_ffn1_w_out', 'grad_ffn2_w_in', 'grad_ffn2_w_out', 'grad_mix_w_in', 'grad_mix_w_out', 'grad_hgrn_lb_logits', 'grad_hgrn_norm_g', 'grad_mla_q_norm_g', 'grad_mla_kv_norm_g', 'grad_mla_w_uq', 'grad_mla_w_ukv', 'grad_fox_b_f', 'grad_gmlp_ln_g', 'grad_gmlp_ln_b', 'grad_gmlp_w_s', 'grad_gmlp_b_s', 'delta_ada_w', 'delta_ada_b', 'delta_ln_g', 'delta_ln_b', 'delta_ffn1_w_in', 'delta_ffn1_w_out', 'delta_ffn2_w_in', 'delta_ffn2_w_out', 'delta_mix_w_in', 'delta_mix_w_out', 'delta_hgrn_lb_logits', 'delta_hgrn_norm_g', 'delta_mla_q_norm_g', 'delta_mla_kv_norm_g', 'delta_mla_w_uq', 'delta_mla_w_ukv', 'delta_fox_b_f', 'delta_gmlp_ln_g', 'delta_gmlp_ln_b', 'delta_gmlp_w_s', 'delta_gmlp_b_s', 'new_m_ada_w', 'new_m_ada_b', 'new_m_ln_g', 'new_m_ln_b', 'new_m_ffn1_w_in', 'new_m_ffn1_w_out', 'new_m_ffn2_w_in', 'new_m_ffn2_w_out', 'new_m_mix_w_in', 'new_m_mix_w_out', 'new_m_hgrn_lb_logits', 'new_m_hgrn_norm_g', 'new_m_mla_q_norm_g', 'new_m_mla_kv_norm_g', 'new_m_mla_w_uq', 'new_m_mla_w_ukv', 'new_m_fox_b_f', 'new_m_gmlp_ln_g', 'new_m_gmlp_ln_b', 'new_m_gmlp_w_s', 'new_m_gmlp_b_s', 'new_v_ada_w', 'new_v_ada_b', 'new_v_ln_g', 'new_v_ln_b', 'new_v_ffn1_w_in', 'new_v_ffn1_w_out', 'new_v_ffn2_w_in', 'new_v_ffn2_w_out', 'new_v_mix_w_in', 'new_v_mix_w_out', 'new_v_hgrn_lb_logits', 'new_v_hgrn_norm_g', 'new_v_mla_q_norm_g', 'new_v_mla_kv_norm_g', 'new_v_mla_w_uq', 'new_v_mla_w_ukv', 'new_v_fox_b_f', 'new_v_gmlp_ln_g', 'new_v_gmlp_ln_b', 'new_v_gmlp_w_s', 'new_v_gmlp_b_s']
TWIN_LEAF_KINDS = {'loss': 'loss', 'grad_x': 'grad_x', 'grad_ada_w': 'grad_w', 'grad_ada_b': 'grad_w', 'grad_ln_g': 'grad_w', 'grad_ln_b': 'grad_w', 'grad_ffn1_w_in': 'grad_w', 'grad_ffn1_w_out': 'grad_w', 'grad_ffn2_w_in': 'grad_w', 'grad_ffn2_w_out': 'grad_w', 'grad_mix_w_in': 'grad_w', 'grad_mix_w_out': 'grad_w', 'grad_hgrn_lb_logits': 'grad_w', 'grad_hgrn_norm_g': 'grad_w', 'grad_mla_q_norm_g': 'grad_w', 'grad_mla_kv_norm_g': 'grad_w', 'grad_mla_w_uq': 'grad_w', 'grad_mla_w_ukv': 'grad_w', 'grad_fox_b_f': 'grad_w', 'grad_gmlp_ln_g': 'grad_w', 'grad_gmlp_ln_b': 'grad_w', 'grad_gmlp_w_s': 'grad_w', 'grad_gmlp_b_s': 'grad_w', 'delta_ada_w': 'delta_w', 'delta_ada_b': 'delta_w', 'delta_ln_g': 'delta_w', 'delta_ln_b': 'delta_w', 'delta_ffn1_w_in': 'delta_w', 'delta_ffn1_w_out': 'delta_w', 'delta_ffn2_w_in': 'delta_w', 'delta_ffn2_w_out': 'delta_w', 'delta_mix_w_in': 'delta_w', 'delta_mix_w_out': 'delta_w', 'delta_hgrn_lb_logits': 'delta_w', 'delta_hgrn_norm_g': 'delta_w', 'delta_mla_q_norm_g': 'delta_w', 'delta_mla_kv_norm_g': 'delta_w', 'delta_mla_w_uq': 'delta_w', 'delta_mla_w_ukv': 'delta_w', 'delta_fox_b_f': 'delta_w', 'delta_gmlp_ln_g': 'delta_w', 'delta_gmlp_ln_b': 'delta_w', 'delta_gmlp_w_s': 'delta_w', 'delta_gmlp_b_s': 'delta_w', 'new_m_ada_w': 'new_m', 'new_m_ada_b': 'new_m', 'new_m_ln_g': 'new_m', 'new_m_ln_b': 'new_m', 'new_m_ffn1_w_in': 'new_m', 'new_m_ffn1_w_out': 'new_m', 'new_m_ffn2_w_in': 'new_m', 'new_m_ffn2_w_out': 'new_m', 'new_m_mix_w_in': 'new_m', 'new_m_mix_w_out': 'new_m', 'new_m_hgrn_lb_logits': 'new_m', 'new_m_hgrn_norm_g': 'new_m', 'new_m_mla_q_norm_g': 'new_m', 'new_m_mla_kv_norm_g': 'new_m', 'new_m_mla_w_uq': 'new_m', 'new_m_mla_w_ukv': 'new_m', 'new_m_fox_b_f': 'new_m', 'new_m_gmlp_ln_g': 'new_m', 'new_m_gmlp_ln_b': 'new_m', 'new_m_gmlp_w_s': 'new_m', 'new_m_gmlp_b_s': 'new_m', 'new_v_ada_w': 'new_v', 'new_v_ada_b': 'new_v', 'new_v_ln_g': 'new_v', 'new_v_ln_b': 'new_v', 'new_v_ffn1_w_in': 'new_v', 'new_v_ffn1_w_out': 'new_v', 'new_v_ffn2_w_in': 'new_v', 'new_v_ffn2_w_out': 'new_v', 'new_v_mix_w_in': 'new_v', 'new_v_mix_w_out': 'new_v', 'new_v_hgrn_lb_logits': 'new_v', 'new_v_hgrn_norm_g': 'new_v', 'new_v_mla_q_norm_g': 'new_v', 'new_v_mla_kv_norm_g': 'new_v', 'new_v_mla_w_uq': 'new_v', 'new_v_mla_w_ukv': 'new_v', 'new_v_fox_b_f': 'new_v', 'new_v_gmlp_ln_g': 'new_v', 'new_v_gmlp_ln_b': 'new_v', 'new_v_gmlp_w_s': 'new_v', 'new_v_gmlp_b_s': 'new_v'}


def _forward(args):
    return _fwd_reference(*[args[k] for k in FWD_PARAMS])


def _output_shape():
    out = _jax.eval_shape(lambda: _forward(_fwd_setup_inputs(0)))
    return out.shape, out.dtype

N_MICROBATCH = 1
ADAM_LR = 0.001
ADAM_B1 = 0.9
ADAM_B2 = 0.999
ADAM_EPS = 1e-08
ADAM_WD = 0.01
ADAM_STEP = 10
PER_EXAMPLE_BATCH_AXIS = {'x': 0, 'c': 0, 'loss_target': 0}
SHARED_INPUTS = []
_WEIGHT_DTYPES = {'ada_w': _jnp.float32, 'ada_b': _jnp.float32, 'ln_g': _jnp.float32, 'ln_b': _jnp.float32, 'ffn1_w_in': _jnp.float32, 'ffn1_w_out': _jnp.float32, 'ffn2_w_in': _jnp.float32, 'ffn2_w_out': _jnp.float32, 'mix_w_in': _jnp.float32, 'mix_w_out': _jnp.float32, 'hgrn_lb_logits': _jnp.float32, 'hgrn_norm_g': _jnp.float32, 'mla_q_norm_g': _jnp.float32, 'mla_kv_norm_g': _jnp.float32, 'mla_w_uq': _jnp.float32, 'mla_w_ukv': _jnp.float32, 'fox_b_f': _jnp.float32, 'gmlp_ln_g': _jnp.float32, 'gmlp_ln_b': _jnp.float32, 'gmlp_w_s': _jnp.float32, 'gmlp_b_s': _jnp.float32}
MOMENT_SCALE = {'ada_w': 2.439095e-02, 'ada_b': 4.750921e-02, 'ln_g': 1.312517e+01, 'ln_b': 1.074798e+00, 'ffn1_w_in': 1.180309e-02, 'ffn1_w_out': 3.845825e-02, 'ffn2_w_in': 1.161439e-02, 'ffn2_w_out': 3.790634e-02, 'mix_w_in': 2.543735e-02, 'mix_w_out': 7.018983e-02, 'hgrn_lb_logits': 5.816925e-03, 'hgrn_norm_g': 3.695170e-02, 'mla_q_norm_g': 1.149177e-02, 'mla_kv_norm_g': 2.807453e-02, 'mla_w_uq': 9.575770e-03, 'mla_w_ukv': 1.304867e-02, 'fox_b_f': 9.250279e-02, 'gmlp_ln_g': 1.519501e-02, 'gmlp_ln_b': 1.474052e-02, 'gmlp_w_s': 2.068949e-02, 'gmlp_b_s': 2.979234e-02}


def _to_microbatches(a, axis):
    t = _jnp.moveaxis(a, axis, 0)
    t = t.reshape((N_MICROBATCH, t.shape[0] // N_MICROBATCH) + t.shape[1:])
    return _jnp.moveaxis(t, 1, axis + 1)


def setup_inputs(seed: int = 0) -> dict:
    inp = _fwd_setup_inputs(seed)
    key = _jax.random.fold_in(_jax.random.key(seed), 7919)
    shape, _ = _output_shape()
    out = dict(inp)
    out["loss_target"] = _jax.random.normal(_jax.random.fold_in(key, 0), shape, _jnp.float32)
    for i, name in enumerate(TWIN_WEIGHTS):
        w = inp[name].astype(_jnp.float32)
        if MOMENT_SCALE is None:
            s = _jnp.sqrt(_jnp.mean(_jnp.square(w)) + 1e-30)
        else:
            s = MOMENT_SCALE[name]
        km, kv = _jax.random.split(_jax.random.fold_in(key, i + 1))
        out[name] = w
        out["m_" + name] = s * _jax.random.normal(km, w.shape, _jnp.float32)
        out["v_" + name] = (s * s) * _jax.random.uniform(kv, w.shape, _jnp.float32, 0.5, 1.5)
    if N_MICROBATCH > 1:
        for name, axis in PER_EXAMPLE_BATCH_AXIS.items():
            out[name] = _to_microbatches(out[name], axis)
    return {'x': out['x'], 'c': out['c'], 'ada_w': out['ada_w'], 'ada_b': out['ada_b'], 'ln_g': out['ln_g'], 'ln_b': out['ln_b'], 'ffn1_w_in': out['ffn1_w_in'], 'ffn1_w_out': out['ffn1_w_out'], 'ffn2_w_in': out['ffn2_w_in'], 'ffn2_w_out': out['ffn2_w_out'], 'mix_w_in': out['mix_w_in'], 'mix_w_out': out['mix_w_out'], 'hgrn_lb_logits': out['hgrn_lb_logits'], 'hgrn_norm_g': out['hgrn_norm_g'], 'mla_q_norm_g': out['mla_q_norm_g'], 'mla_kv_norm_g': out['mla_kv_norm_g'], 'mla_w_uq': out['mla_w_uq'], 'mla_w_ukv': out['mla_w_ukv'], 'fox_b_f': out['fox_b_f'], 'gmlp_ln_g': out['gmlp_ln_g'], 'gmlp_ln_b': out['gmlp_ln_b'], 'gmlp_w_s': out['gmlp_w_s'], 'gmlp_b_s': out['gmlp_b_s'], 'loss_target': out['loss_target'], 'm_ada_w': out['m_ada_w'], 'm_ada_b': out['m_ada_b'], 'm_ln_g': out['m_ln_g'], 'm_ln_b': out['m_ln_b'], 'm_ffn1_w_in': out['m_ffn1_w_in'], 'm_ffn1_w_out': out['m_ffn1_w_out'], 'm_ffn2_w_in': out['m_ffn2_w_in'], 'm_ffn2_w_out': out['m_ffn2_w_out'], 'm_mix_w_in': out['m_mix_w_in'], 'm_mix_w_out': out['m_mix_w_out'], 'm_hgrn_lb_logits': out['m_hgrn_lb_logits'], 'm_hgrn_norm_g': out['m_hgrn_norm_g'], 'm_mla_q_norm_g': out['m_mla_q_norm_g'], 'm_mla_kv_norm_g': out['m_mla_kv_norm_g'], 'm_mla_w_uq': out['m_mla_w_uq'], 'm_mla_w_ukv': out['m_mla_w_ukv'], 'm_fox_b_f': out['m_fox_b_f'], 'm_gmlp_ln_g': out['m_gmlp_ln_g'], 'm_gmlp_ln_b': out['m_gmlp_ln_b'], 'm_gmlp_w_s': out['m_gmlp_w_s'], 'm_gmlp_b_s': out['m_gmlp_b_s'], 'v_ada_w': out['v_ada_w'], 'v_ada_b': out['v_ada_b'], 'v_ln_g': out['v_ln_g'], 'v_ln_b': out['v_ln_b'], 'v_ffn1_w_in': out['v_ffn1_w_in'], 'v_ffn1_w_out': out['v_ffn1_w_out'], 'v_ffn2_w_in': out['v_ffn2_w_in'], 'v_ffn2_w_out': out['v_ffn2_w_out'], 'v_mix_w_in': out['v_mix_w_in'], 'v_mix_w_out': out['v_mix_w_out'], 'v_hgrn_lb_logits': out['v_hgrn_lb_logits'], 'v_hgrn_norm_g': out['v_hgrn_norm_g'], 'v_mla_q_norm_g': out['v_mla_q_norm_g'], 'v_mla_kv_norm_g': out['v_mla_kv_norm_g'], 'v_mla_w_uq': out['v_mla_w_uq'], 'v_mla_w_ukv': out['v_mla_w_ukv'], 'v_fox_b_f': out['v_fox_b_f'], 'v_gmlp_ln_g': out['v_gmlp_ln_g'], 'v_gmlp_ln_b': out['v_gmlp_ln_b'], 'v_gmlp_w_s': out['v_gmlp_w_s'], 'v_gmlp_b_s': out['v_gmlp_b_s']}


def _loss(weights, diff, rest, loss_target):
    with _jax.named_scope("forward"):
        args = {**rest, TWIN_DIFF_INPUT: diff, **{k: w.astype(_WEIGHT_DTYPES[k]) for k, w in weights.items()}}
        y = _forward(args)
    with _jax.named_scope("loss_head"):
        err = _jnp.square(y.astype(_jnp.float32) - loss_target)
        return 0.5 * _jnp.sum(_jnp.mean(err, axis=-1)) if err.ndim else 0.5 * err


def _adamw(w, g, m, v):
    m = ADAM_B1 * m + (1.0 - ADAM_B1) * g
    v = ADAM_B2 * v + (1.0 - ADAM_B2) * _jnp.square(g)
    m_hat = m / (1.0 - ADAM_B1 ** ADAM_STEP)
    v_hat = v / (1.0 - ADAM_B2 ** ADAM_STEP)
    delta = -ADAM_LR * (m_hat / (_jnp.sqrt(v_hat) + ADAM_EPS) + ADAM_WD * w)
    return delta, m, v


def reference(x, c, ada_w, ada_b, ln_g, ln_b, ffn1_w_in, ffn1_w_out, ffn2_w_in, ffn2_w_out, mix_w_in, mix_w_out, hgrn_lb_logits, hgrn_norm_g, mla_q_norm_g, mla_kv_norm_g, mla_w_uq, mla_w_ukv, fox_b_f, gmlp_ln_g, gmlp_ln_b, gmlp_w_s, gmlp_b_s, loss_target, m_ada_w, m_ada_b, m_ln_g, m_ln_b, m_ffn1_w_in, m_ffn1_w_out, m_ffn2_w_in, m_ffn2_w_out, m_mix_w_in, m_mix_w_out, m_hgrn_lb_logits, m_hgrn_norm_g, m_mla_q_norm_g, m_mla_kv_norm_g, m_mla_w_uq, m_mla_w_ukv, m_fox_b_f, m_gmlp_ln_g, m_gmlp_ln_b, m_gmlp_w_s, m_gmlp_b_s, v_ada_w, v_ada_b, v_ln_g, v_ln_b, v_ffn1_w_in, v_ffn1_w_out, v_ffn2_w_in, v_ffn2_w_out, v_mix_w_in, v_mix_w_out, v_hgrn_lb_logits, v_hgrn_norm_g, v_mla_q_norm_g, v_mla_kv_norm_g, v_mla_w_uq, v_mla_w_ukv, v_fox_b_f, v_gmlp_ln_g, v_gmlp_ln_b, v_gmlp_w_s, v_gmlp_b_s):
    given = dict(x=x, c=c, ada_w=ada_w, ada_b=ada_b, ln_g=ln_g, ln_b=ln_b, ffn1_w_in=ffn1_w_in, ffn1_w_out=ffn1_w_out, ffn2_w_in=ffn2_w_in, ffn2_w_out=ffn2_w_out, mix_w_in=mix_w_in, mix_w_out=mix_w_out, hgrn_lb_logits=hgrn_lb_logits, hgrn_norm_g=hgrn_norm_g, mla_q_norm_g=mla_q_norm_g, mla_kv_norm_g=mla_kv_norm_g, mla_w_uq=mla_w_uq, mla_w_ukv=mla_w_ukv, fox_b_f=fox_b_f, gmlp_ln_g=gmlp_ln_g, gmlp_ln_b=gmlp_ln_b, gmlp_w_s=gmlp_w_s, gmlp_b_s=gmlp_b_s, loss_target=loss_target, m_ada_w=m_ada_w, m_ada_b=m_ada_b, m_ln_g=m_ln_g, m_ln_b=m_ln_b, m_ffn1_w_in=m_ffn1_w_in, m_ffn1_w_out=m_ffn1_w_out, m_ffn2_w_in=m_ffn2_w_in, m_ffn2_w_out=m_ffn2_w_out, m_mix_w_in=m_mix_w_in, m_mix_w_out=m_mix_w_out, m_hgrn_lb_logits=m_hgrn_lb_logits, m_hgrn_norm_g=m_hgrn_norm_g, m_mla_q_norm_g=m_mla_q_norm_g, m_mla_kv_norm_g=m_mla_kv_norm_g, m_mla_w_uq=m_mla_w_uq, m_mla_w_ukv=m_mla_w_ukv, m_fox_b_f=m_fox_b_f, m_gmlp_ln_g=m_gmlp_ln_g, m_gmlp_ln_b=m_gmlp_ln_b, m_gmlp_w_s=m_gmlp_w_s, m_gmlp_b_s=m_gmlp_b_s, v_ada_w=v_ada_w, v_ada_b=v_ada_b, v_ln_g=v_ln_g, v_ln_b=v_ln_b, v_ffn1_w_in=v_ffn1_w_in, v_ffn1_w_out=v_ffn1_w_out, v_ffn2_w_in=v_ffn2_w_in, v_ffn2_w_out=v_ffn2_w_out, v_mix_w_in=v_mix_w_in, v_mix_w_out=v_mix_w_out, v_hgrn_lb_logits=v_hgrn_lb_logits, v_hgrn_norm_g=v_hgrn_norm_g, v_mla_q_norm_g=v_mla_q_norm_g, v_mla_kv_norm_g=v_mla_kv_norm_g, v_mla_w_uq=v_mla_w_uq, v_mla_w_ukv=v_mla_w_ukv, v_fox_b_f=v_fox_b_f, v_gmlp_ln_g=v_gmlp_ln_g, v_gmlp_ln_b=v_gmlp_ln_b, v_gmlp_w_s=v_gmlp_w_s, v_gmlp_b_s=v_gmlp_b_s)
    weights = {n: given[n] for n in TWIN_WEIGHTS}
    shared = {n: given[n] for n in SHARED_INPUTS}
    per_example = {n: given[n] for n in ['x', 'c']}
    grad_fn = _jax.value_and_grad(_loss, argnums=(0, 1))

    def one_microbatch(ex, loss_target):
        ex = dict(ex)
        diff = ex.pop(TWIN_DIFF_INPUT)
        return grad_fn(weights, diff, {**shared, **ex}, loss_target)

    if N_MICROBATCH == 1:
        loss, (grad_w, grad_x) = one_microbatch(per_example, given["loss_target"])
    else:
        def body(carry, xs):
            loss_sum, grad_sum = carry
            l_k, (gw_k, gx_k) = one_microbatch(xs[0], xs[1])
            with _jax.named_scope("update"):
                return (loss_sum + l_k, _jax.tree.map(_jnp.add, grad_sum, gw_k)), gx_k

        init = (_jnp.zeros((), _jnp.float32), _jax.tree.map(_jnp.zeros_like, weights))
        (loss, grad_w), grad_x = _jax.lax.scan(body, init, (per_example, given["loss_target"]))
    with _jax.named_scope("update"):
        delta_w, new_m, new_v = {}, {}, {}
        for n in TWIN_WEIGHTS:
            delta_w[n], new_m[n], new_v[n] = _adamw(weights[n], grad_w[n], given["m_" + n], given["v_" + n])
    return (loss, grad_x, *[grad_w[n] for n in TWIN_WEIGHTS], *[delta_w[n] for n in TWIN_WEIGHTS],
            *[new_m[n] for n in TWIN_WEIGHTS], *[new_v[n] for n in TWIN_WEIGHTS])
```

```python
import functools

import jax
import jax.numpy as jnp
import numpy as np
from jax import lax
from jax.experimental import pallas as pl
from jax.experimental.pallas import tpu as pltpu

F32, BF16 = jnp.float32, jnp.bfloat16
MESH = pl.DeviceIdType.MESH

D = 1024
DEPTH = 2
FF = 2816
N_MOD = 9
GW = 256
HEADS = 4
HD = 64
HP = 128
A_CHUNK = 16
LB_FLOOR = 1e-30
B_Q_LORA, B_KV_LORA, B_NOPE, B_ROPE = 256, 128, 64, 32
ROPE_THETA = 10000.0
D_CHUNK = 128
MIX_COLS = 2724
ALPHA = (2 * DEPTH) ** 0.25
LN_EPS = 1e-5
RMS_EPS = 1e-6
ADAM_LR, ADAM_B1, ADAM_B2, ADAM_EPS, ADAM_WD, ADAM_STEP = 0.001, 0.9, 0.999, 1e-08, 0.01, 10

NP = 3840
NP_TILE = 1920
C_A, C_B, C_CQ, C_CK, C_CV, C_D, C_CF = 0, 1024, 1536, 2048, 2560, 3072, 3584
NCAT = 1536
O_A, O_B, O_C, O_D = 0, 256, 768, 1280

VMEM_BIG = 48 << 20


def _cparams(vmem=None):
    return pltpu.CompilerParams(vmem_limit_bytes=vmem) if vmem else pltpu.CompilerParams()


def _sds(shape, dtype):
    return jax.ShapeDtypeStruct(tuple(shape), dtype)


@jax.custom_vjp
def _mm(a, w):
    return jnp.dot(a.astype(BF16), w.astype(BF16), preferred_element_type=F32)


def _mm_f(a, w):
    return _mm(a, w), (a, w)


def _mm_b(res, g):
    a, w = res
    gb = g.astype(BF16)
    da = lax.dot_general(gb, w.astype(BF16), (((1,), (1,)), ((), ())), preferred_element_type=F32)
    dw = lax.dot_general(a.astype(BF16), gb, (((0,), (0,)), ((), ())), preferred_element_type=F32)
    return da.astype(a.dtype), dw.astype(w.dtype)


_mm.defvjp(_mm_f, _mm_b)


@jax.custom_vjp
def _mm_nt(a, b):
    return lax.dot_general(a.astype(BF16), b.astype(BF16), (((1,), (1,)), ((), ())), preferred_element_type=F32)


def _mm_nt_f(a, b):
    return _mm_nt(a, b), (a, b)


def _mm_nt_b(res, g):
    a, b = res
    gb = g.astype(BF16)
    da = jnp.dot(gb, b.astype(BF16), preferred_element_type=F32)
    db = lax.dot_general(gb, a.astype(BF16), (((0,), (0,)), ((), ())), preferred_element_type=F32)
    return da.astype(a.dtype), db.astype(b.dtype)


_mm_nt.defvjp(_mm_nt_f, _mm_nt_b)


@jax.custom_vjp
def _mm_tn(a, b):
    return lax.dot_general(a.astype(BF16), b.astype(BF16), (((0,), (0,)), ((), ())), preferred_element_type=F32)


def _mm_tn_f(a, b):
    return _mm_tn(a, b), (a, b)


def _mm_tn_b(res, g):
    a, b = res
    gb = g.astype(BF16)
    da = lax.dot_general(b.astype(BF16), gb, (((1,), (1,)), ((), ())), preferred_element_type=F32)
    db = jnp.dot(a.astype(BF16), gb, preferred_element_type=F32)
    return da.astype(a.dtype), db.astype(b.dtype)


_mm_tn.defvjp(_mm_tn_f, _mm_tn_b)


def _mm_hi(a, w):
    return jnp.dot(a, w, precision=lax.Precision.HIGHEST, preferred_element_type=F32)


def _iota(shape, dim):
    return lax.broadcasted_iota(jnp.int32, shape, dim)


def _head_sum_mats():
    e = (_iota((GW, HP), 0) // HD == _iota((GW, HP), 1)).astype(F32)
    et = (_iota((HP, GW), 1) // HD == _iota((HP, GW), 0)).astype(F32)
    return e, et


def _modulate(x, mod_ref, sh, sc):
    return x * (1.0 + mod_ref[sc:sc + 1, :]) + mod_ref[sh:sh + 1, :]


def _ln_res(x, y, gate, lg, lb, gs):
    r = ALPHA * x + gs * (1.0 + gate) * y
    mu = jnp.mean(r, axis=-1, keepdims=True)
    var = jnp.mean(jnp.square(r - mu), axis=-1, keepdims=True)
    return (r - mu) * lax.rsqrt(var + LN_EPS) * lg + lb


def _rms(x, g):
    return x * lax.rsqrt(jnp.mean(x * x, axis=-1, keepdims=True) + RMS_EPS) * g


def _tile(n, pref):
    return pref if n % pref == 0 else n


def mod_fwd(c8, w, b):
    n = w.shape[1]
    tn = 2304

    def body(c_ref, w_ref, b_ref, o_ref):
        h = jax.nn.silu(c_ref[...]).astype(BF16)
        o_ref[...] = jnp.dot(h, w_ref[...], preferred_element_type=F32) + b_ref[...]

    return pl.pallas_call(
        body, name="mod_fwd", grid=(n // tn,),
        in_specs=[pl.BlockSpec((8, D), lambda j: (0, 0)), pl.BlockSpec((D, tn), lambda j: (0, j)),
                  pl.BlockSpec((1, tn), lambda j: (0, j))],
        out_specs=pl.BlockSpec((8, tn), lambda j: (0, j)), out_shape=_sds((8, n), F32),
        compiler_params=_cparams(VMEM_BIG))(c8, w, b)


def ffn_in_fwd(x, mod, w_in, sh, sc, S):
    T = x.shape[0]
    tm, tn = _tile(S, 512), 1408
    tpb, nj = S // tm, FF // tn

    def body(x_ref, mod_ref, wg_ref, wu_ref, zg_ref, zu_ref, act_ref, h_ref):
        @pl.when(pl.program_id(1) == 0)
        def _():
            h_ref[...] = _modulate(x_ref[...], mod_ref, sh, sc).astype(BF16)
        g = jnp.dot(h_ref[...], wg_ref[...], preferred_element_type=F32)
        u = jnp.dot(h_ref[...], wu_ref[...], preferred_element_type=F32)
        zg_ref[...] = g
        zu_ref[...] = u
        act_ref[...] = (jax.nn.silu(g) * u).astype(BF16)

    return pl.pallas_call(
        body, name="ffn_in_fwd", grid=(T // tm, nj),
        in_specs=[pl.BlockSpec((tm, D), lambda i, j: (i, 0)),
                  pl.BlockSpec((None, N_MOD, D), lambda i, j: (i // tpb, 0, 0)),
                  pl.BlockSpec((D, tn), lambda i, j: (0, j)),
                  pl.BlockSpec((D, tn), lambda i, j: (0, j + nj))],
        out_specs=[pl.BlockSpec((tm, tn), lambda i, j: (i, j))] * 3,
        out_shape=[_sds((T, FF), F32), _sds((T, FF), F32), _sds((T, FF), BF16)],
        scratch_shapes=[pltpu.VMEM((tm, D), BF16)],
        compiler_params=_cparams(VMEM_BIG))(x, mod, w_in, w_in)


def mix_in_fwd(x, mod, w, sh, sc, S):
    T = x.shape[0]
    n = w.shape[1]
    tm, tn = _tile(S, 512), NP_TILE
    tpb = S // tm

    def body(x_ref, mod_ref, w_ref, o_ref, h_ref):
        @pl.when(pl.program_id(1) == 0)
        def _():
            h_ref[...] = _modulate(x_ref[...], mod_ref, sh, sc).astype(BF16)
        o_ref[...] = jnp.dot(h_ref[...], w_ref[...], preferred_element_type=F32)

    return pl.pallas_call(
        body, name="mix_in_fwd", grid=(T // tm, n // tn),
        in_specs=[pl.BlockSpec((tm, D), lambda i, j: (i, 0)),
                  pl.BlockSpec((None, N_MOD, D), lambda i, j: (i // tpb, 0, 0)),
                  pl.BlockSpec((D, tn), lambda i, j: (0, j))],
        out_specs=pl.BlockSpec((tm, tn), lambda i, j: (i, j)), out_shape=_sds((T, n), F32),
        scratch_shapes=[pltpu.VMEM((tm, D), BF16)],
        compiler_params=_cparams(VMEM_BIG))(x, mod, w)


def out_ln_fwd(act, w_out, x, mod, lg, lb, gate, gs, S):
    T, K = act.shape
    tm = _tile(S, 512)
    tpb = S // tm

    def body(a_ref, w_ref, x_ref, mod_ref, lg_ref, lb_ref, y_ref, xn_ref):
        y = jnp.dot(a_ref[...], w_ref[...], preferred_element_type=F32)
        y_ref[...] = y
        xn_ref[...] = _ln_res(x_ref[...], y, mod_ref[gate:gate + 1, :], lg_ref[...], lb_ref[...], gs)

    return pl.pallas_call(
        body, name="out_ln_fwd", grid=(T // tm,),
        in_specs=[pl.BlockSpec((tm, K), lambda i: (i, 0)), pl.BlockSpec((K, D), lambda i: (0, 0)),
                  pl.BlockSpec((tm, D), lambda i: (i, 0)),
                  pl.BlockSpec((None, N_MOD, D), lambda i: (i // tpb, 0, 0)),
                  pl.BlockSpec((1, D), lambda i: (0, 0)), pl.BlockSpec((1, D), lambda i: (0, 0))],
        out_specs=[pl.BlockSpec((tm, D), lambda i: (i, 0))] * 2,
        out_shape=[_sds((T, D), F32), _sds((T, D), F32)],
        compiler_params=_cparams(VMEM_BIG))(act, w_out, x, mod, lg, lb)


def ln_res_bwd(dxn, x, y, mod, lg, lb, gate, gs, S):
    T = x.shape[0]
    B = T // S
    tm = _tile(S, 512)
    tpb = S // tm

    def body(d_ref, x_ref, y_ref, mod_ref, lg_ref, lb_ref, dx_ref, dy_ref, dg_ref, dlg_ref, dlb_ref):
        i = pl.program_id(0)
        f = functools.partial(_ln_res, gs=gs)
        _, vjp = jax.vjp(f, x_ref[...], y_ref[...], mod_ref[gate:gate + 1, :], lg_ref[...], lb_ref[...])
        dx, dy, dg, dlg, dlb = vjp(d_ref[...])
        dx_ref[...] = dx
        dy_ref[...] = dy.astype(BF16)

        @pl.when(i % tpb == 0)
        def _():
            dg_ref[...] = jnp.zeros_like(dg_ref)

        @pl.when(i == 0)
        def _():
            dlg_ref[...] = jnp.zeros_like(dlg_ref)
            dlb_ref[...] = jnp.zeros_like(dlb_ref)

        dg_ref[...] += dg
        dlg_ref[...] += dlg
        dlb_ref[...] += dlb

    tok = pl.BlockSpec((tm, D), lambda i: (i, 0))
    vec = pl.BlockSpec((1, D), lambda i: (0, 0))
    return pl.pallas_call(
        body, name="ln_res_bwd", grid=(T // tm,),
        in_specs=[tok, tok, tok, pl.BlockSpec((None, N_MOD, D), lambda i: (i // tpb, 0, 0)), vec, vec],
        out_specs=[tok, tok, pl.BlockSpec((None, 1, D), lambda i: (i // tpb, 0, 0)), vec, vec],
        out_shape=[_sds((T, D), F32), _sds((T, D), BF16), _sds((B, 1, D), F32), _sds((1, D), F32), _sds((1, D), F32)],
        compiler_params=_cparams(VMEM_BIG))(dxn, x, y, mod, lg, lb)


def swiglu_bwd(dy, w_out, zg, zu, S):
    T = dy.shape[0]
    tm, tn = _tile(S, 512), 1408

    def body(dy_ref, w_ref, zg_ref, zu_ref, dg_ref, du_ref):
        da = lax.dot_general(dy_ref[...], w_ref[...], (((1,), (1,)), ((), ())), preferred_element_type=F32)
        g, u = zg_ref[...], zu_ref[...]
        sg = jax.nn.sigmoid(g)
        dg_ref[...] = (da * u * (sg * (1.0 + g * (1.0 - sg)))).astype(BF16)
        du_ref[...] = (da * (g * sg)).astype(BF16)

    zt = pl.BlockSpec((tm, tn), lambda i, j: (i, j))
    return pl.pallas_call(
        body, name="swiglu_bwd", grid=(T // tm, FF // tn),
        in_specs=[pl.BlockSpec((tm, D), lambda i, j: (i, 0)), pl.BlockSpec((tn, D), lambda i, j: (j, 0)), zt, zt],
        out_specs=[zt, zt], out_shape=[_sds((T, FF), BF16), _sds((T, FF), BF16)],
        compiler_params=_cparams(VMEM_BIG))(dy, w_out, zg, zu)


def nt_plain(dy, w):
    T = dy.shape[0]
    K = w.shape[0]
    tm = _tile(T, 512)

    def body(dy_ref, w_ref, o_ref):
        o_ref[...] = lax.dot_general(dy_ref[...], w_ref[...], (((1,), (1,)), ((), ())), preferred_element_type=F32)

    return pl.pallas_call(
        body, name="nt_plain", grid=(T // tm,),
        in_specs=[pl.BlockSpec((tm, D), lambda i: (i, 0)), pl.BlockSpec((K, D), lambda i: (0, 0))],
        out_specs=pl.BlockSpec((tm, K), lambda i: (i, 0)), out_shape=_sds((T, K), F32),
        compiler_params=_cparams(VMEM_BIG))(dy, w)


def tn_mm(a, b, tk):
    T, K = a.shape
    N = b.shape[1]
    tt = _tile(T, 512)

    def body(a_ref, b_ref, o_ref):
        @pl.when(pl.program_id(1) == 0)
        def _():
            o_ref[...] = jnp.zeros_like(o_ref)
        o_ref[...] += lax.dot_general(a_ref[...], b_ref[...], (((0,), (0,)), ((), ())), preferred_element_type=F32)

    return pl.pallas_call(
        body, name="tn_mm", grid=(K // tk, T // tt),
        in_specs=[pl.BlockSpec((tt, tk), lambda k, t: (t, k)), pl.BlockSpec((tt, N), lambda k, t: (t, 0))],
        out_specs=pl.BlockSpec((tk, N), lambda k, t: (k, 0)), out_shape=_sds((K, N), F32),
        compiler_params=_cparams(VMEM_BIG))(a, b)


def tn_mm_mod(x, mod, b, sh, sc, S, tn):
    T = x.shape[0]
    N = b.shape[1]
    tt = _tile(S, 512)
    tpb = S // tt

    def body(x_ref, mod_ref, b_ref, o_ref):
        @pl.when(pl.program_id(1) == 0)
        def _():
            o_ref[...] = jnp.zeros_like(o_ref)
        h = _modulate(x_ref[...], mod_ref, sh, sc).astype(BF16)
        o_ref[...] += lax.dot_general(h, b_ref[...], (((0,), (0,)), ((), ())), preferred_element_type=F32)

    return pl.pallas_call(
        body, name="tn_mm_mod", grid=(N // tn, T // tt),
        in_specs=[pl.BlockSpec((tt, D), lambda j, t: (t, 0)),
                  pl.BlockSpec((None, N_MOD, D), lambda j, t: (t // tpb, 0, 0)),
                  pl.BlockSpec((tt, tn), lambda j, t: (t, j))],
        out_specs=pl.BlockSpec((D, tn), lambda j, t: (0, j)), out_shape=_sds((D, N), F32),
        compiler_params=_cparams(VMEM_BIG))(x, mod, b)


def nt_mod_bwd(ds, w, offs, x, mod, dres, sc, S, tk):
    T = x.shape[0]
    B = T // S
    tm = _tile(S, 512)
    tpb = S // tm
    Kd = ds[0].shape[1]
    nk = Kd // tk
    n_in = len(ds)

    def body(*refs):
        d_refs, w_refs = refs[:n_in], refs[n_in:2 * n_in]
        x_ref, mod_ref, r_ref, dx_ref, dsh_ref, dsc_ref, acc = refs[2 * n_in:]
        i, k = pl.program_id(0), pl.program_id(1)

        @pl.when(k == 0)
        def _():
            acc[...] = jnp.zeros_like(acc)

        for d_ref, w_ref in zip(d_refs, w_refs):
            acc[...] += lax.dot_general(d_ref[...], w_ref[...], (((1,), (1,)), ((), ())), preferred_element_type=F32)

        @pl.when(k == nk - 1)
        def _():
            dh = acc[...]
            dx_ref[...] = dh * (1.0 + mod_ref[sc:sc + 1, :]) + r_ref[...]

            @pl.when(i % tpb == 0)
            def _():
                dsh_ref[...] = jnp.zeros_like(dsh_ref)
                dsc_ref[...] = jnp.zeros_like(dsc_ref)

            dsh_ref[...] += jnp.sum(dh, axis=0, keepdims=True)
            dsc_ref[...] += jnp.sum(dh * x_ref[...], axis=0, keepdims=True)

    tok = pl.BlockSpec((tm, D), lambda i, k: (i, 0))
    vec = pl.BlockSpec((None, 1, D), lambda i, k: (i // tpb, 0, 0))
    in_specs = [pl.BlockSpec((tm, tk), lambda i, k: (i, k)) for _ in ds]
    in_specs += [pl.BlockSpec((D, tk), functools.partial(lambda i, k, o: (0, k + o), o=off // tk)) for off in offs]
    in_specs += [tok, pl.BlockSpec((None, N_MOD, D), lambda i, k: (i // tpb, 0, 0)), tok]
    return pl.pallas_call(
        body, name="nt_mod_bwd", grid=(T // tm, nk), in_specs=in_specs,
        out_specs=[tok, vec, vec],
        out_shape=[_sds((T, D), F32), _sds((B, 1, D), F32), _sds((B, 1, D), F32)],
        scratch_shapes=[pltpu.VMEM((tm, D), F32)],
        compiler_params=_cparams(VMEM_BIG))(*ds, *([w] * n_in), x, mod, dres)


def loss_head(y, tgt):
    T = y.shape[0]
    tm = _tile(T, 512)

    def body(y_ref, t_ref, l_ref, d_ref):
        @pl.when(pl.program_id(0) == 0)
        def _():
            l_ref[...] = jnp.zeros_like(l_ref)
        e = y_ref[...] - t_ref[...]
        d_ref[...] = e * (1.0 / D)
        l_ref[...] += 0.5 * jnp.sum(jnp.sum(e * e, axis=1, keepdims=True) * (1.0 / D))

    tok = pl.BlockSpec((tm, D), lambda i: (i, 0))
    return pl.pallas_call(
        body, name="loss_head", grid=(T // tm,), in_specs=[tok, tok],
        out_specs=[pl.BlockSpec((8, 128), lambda i: (0, 0)), tok],
        out_shape=[_sds((8, 128), F32), _sds((T, D), F32)],
        compiler_params=_cparams(VMEM_BIG))(y, tgt)


def _hgrn_block(q, fz, inp, go, st, lb, ng, blk):
    nc = blk // A_CHUNK
    lb_eff = jnp.maximum(lb, LB_FLOOR)
    log_f = jnp.logaddexp(jnp.log(lb_eff), jnp.log1p(-lb) + jax.nn.log_sigmoid(fz))
    k = (1.0 - lb) * jax.nn.sigmoid(-fz) - (lb_eff - lb)
    qf = jax.nn.silu(q)
    same_chunk = _iota((blk, blk), 0) // A_CHUNK == _iota((blk, blk), 1) // A_CHUNK
    tril = (same_chunk & (_iota((blk, blk), 1) <= _iota((blk, blk), 0))).astype(F32)
    G = _mm_hi(tril, log_f)
    e_mat, et_mat = _head_sum_mats()
    G4, q4, k4, v4 = (z.reshape(nc, A_CHUNK, GW) for z in (G, qf, k, inp))
    shp = (nc, A_CHUNK, A_CHUNK, GW)
    causal = _iota(shp, 2) <= _iota(shp, 1)
    decay = jnp.exp(jnp.where(causal, G4[:, :, None, :] - G4[:, None, :, :], -jnp.inf))
    prod = q4[:, :, None, :] * k4[:, None, :, :] * decay
    scores = _mm(prod.reshape(nc * A_CHUNK * A_CHUNK, GW), e_mat.astype(BF16))
    spread = _mm(scores, et_mat.astype(BF16)).reshape(shp)
    o_intra = jnp.sum(spread * v4[:, None, :, :], axis=2).reshape(blk, GW)
    head_diag = (_iota((GW, GW), 0) // HD == _iota((GW, GW), 1) // HD).astype(F32)
    g_last = [jnp.sum(log_f[c * A_CHUNK:(c + 1) * A_CHUNK], axis=0, keepdims=True) for c in range(nc)]
    g_last_b = jnp.concatenate([jnp.broadcast_to(g, (A_CHUNK, GW)) for g in g_last], axis=0)
    q_dec = qf * jnp.exp(G)
    k_end = k * jnp.exp(g_last_b - G)
    outs = []
    for c in range(nc):
        rows = slice(c * A_CHUNK, (c + 1) * A_CHUNK)
        outs.append(_mm_nt(q_dec[rows], st))
        st = st * jnp.exp(g_last[c]) + _mm_tn(inp[rows], k_end[rows]) * head_diag
    o = o_intra + jnp.concatenate(outs, axis=0)
    ms = _mm_hi(o * o, e_mat) * (1.0 / HD)
    o = o * _mm_hi(lax.rsqrt(ms + RMS_EPS), et_mat) * ng
    return o * jax.nn.silu(go), st


HGRN_BLK = 128


def hgrn_fwd(proj, lb, ng, S):
    T = proj.shape[0]
    B = T // S
    blk = min(HGRN_BLK, S)
    nb = S // blk

    def body(p_ref, lb_ref, ng_ref, o_ref, st_out_ref, st_ref):
        @pl.when(pl.program_id(1) == 0)
        def _():
            st_ref[...] = jnp.zeros_like(st_ref)
        st_out_ref[...] = st_ref[...]
        p = p_ref[...]
        o, st = _hgrn_block(p[:, 0:GW], p[:, GW:2 * GW], p[:, 2 * GW:3 * GW], p[:, 3 * GW:4 * GW],
                            st_ref[...], lb_ref[...], ng_ref[...], blk)
        o_ref[...] = o.astype(BF16)
        st_ref[...] = st

    vec = pl.BlockSpec((1, GW), lambda b, j: (0, 0))
    return pl.pallas_call(
        body, name="hgrn_fwd", grid=(B, nb),
        in_specs=[pl.BlockSpec((blk, 4 * GW), lambda b, j: (b * nb + j, C_A // (4 * GW))), vec, vec],
        out_specs=[pl.BlockSpec((blk, GW), lambda b, j: (b * nb + j, 0)),
                   pl.BlockSpec((None, GW, GW), lambda b, j: (b * nb + j, 0, 0))],
        out_shape=[_sds((T, GW), BF16), _sds((B * nb, GW, GW), F32)],
        scratch_shapes=[pltpu.VMEM((GW, GW), F32)],
        compiler_params=_cparams(VMEM_BIG))(proj, lb, ng)


def hgrn_bwd(proj, states, dcat, lb, ng, S):
    T = proj.shape[0]
    B = T // S
    blk = min(HGRN_BLK, S)
    nb = S // blk

    def body(p_ref, st_in_ref, do_ref, lb_ref, ng_ref, dp_ref, dlb_ref, dng_ref, dst_ref):
        b, j = pl.program_id(0), pl.program_id(1)

        @pl.when(j == 0)
        def _():
            dst_ref[...] = jnp.zeros_like(dst_ref)

        @pl.when((b == 0) & (j == 0))
        def _():
            dlb_ref[...] = jnp.zeros_like(dlb_ref)
            dng_ref[...] = jnp.zeros_like(dng_ref)

        p = p_ref[...]
        f = functools.partial(_hgrn_block, blk=blk)
        _, vjp = jax.vjp(f, p[:, 0:GW], p[:, GW:2 * GW], p[:, 2 * GW:3 * GW], p[:, 3 * GW:4 * GW],
                         st_in_ref[...], lb_ref[...], ng_ref[...])
        dq, df, di, dg, dst, dlb, dng = vjp((do_ref[...], dst_ref[...]))
        dp_ref[...] = jnp.concatenate([dq, df, di, dg], axis=1).astype(BF16)
        dst_ref[...] = dst
        dlb_ref[...] += dlb
        dng_ref[...] += dng

    def rev(b, j):
        return b * nb + (nb - 1 - j)

    vec = pl.BlockSpec((1, GW), lambda b, j: (0, 0))
    return pl.pallas_call(
        body, name="hgrn_bwd", grid=(B, nb),
        in_specs=[pl.BlockSpec((blk, 4 * GW), lambda b, j: (rev(b, j), C_A // (4 * GW))),
                  pl.BlockSpec((None, GW, GW), lambda b, j: (rev(b, j), 0, 0)),
                  pl.BlockSpec((blk, GW), lambda b, j: (rev(b, j), O_A // GW)), vec, vec],
        out_specs=[pl.BlockSpec((blk, 4 * GW), lambda b, j: (rev(b, j), 0)), vec, vec],
        out_shape=[_sds((T, 4 * GW), BF16), _sds((1, GW), F32), _sds((1, GW), F32)],
        scratch_shapes=[pltpu.VMEM((GW, GW), F32)],
        compiler_params=_cparams(VMEM_BIG))(proj, states, dcat, lb, ng)


ATT_TQ = 256


def _attn_block(q, k, v, cum, qpos0, scale, use_cum):
    s = _mm_nt(q, k) * scale
    if use_cum:
        s = s - cum
    qpos = qpos0 + _iota(s.shape, 0)
    s = jnp.where(_iota(s.shape, 1) <= qpos, s, -jnp.inf)
    e = jnp.exp(s - jnp.max(s, axis=-1, keepdims=True))
    p = e / jnp.sum(e, axis=-1, keepdims=True)
    return _mm(p, v)


def attn_fwd(qa, qo, ka, ko, va, vo, cum, scale, S):
    T = qa.shape[0]
    B = T // S
    tq = min(ATT_TQ, S)
    nq = S // tq
    use_cum = cum is not None

    def body(*refs):
        if use_cum:
            q_ref, k_ref, v_ref, c_ref, o_ref = refs
            crow = c_ref[pl.ds(pl.program_id(1), 1), :]
        else:
            q_ref, k_ref, v_ref, o_ref = refs
            crow = None
        o = _attn_block(q_ref[...], k_ref[...], v_ref[...], crow, pl.program_id(2) * tq, scale, use_cum)
        o_ref[...] = o.astype(BF16)

    in_specs = [pl.BlockSpec((tq, HP), lambda b, h, i: (b * nq + i, qo + h)),
                pl.BlockSpec((S, HP), lambda b, h, i: (b, ko + h)),
                pl.BlockSpec((S, HP), lambda b, h, i: (b, vo + h))]
    args = [qa, ka, va]
    if use_cum:
        in_specs.append(pl.BlockSpec((None, 8, S), lambda b, h, i: (b, 0, 0)))
        args.append(cum)
    return pl.pallas_call(
        body, name="attn_fwd", grid=(B, HEADS, nq), in_specs=in_specs,
        out_specs=pl.BlockSpec((tq, HP), lambda b, h, i: (b * nq + i, h)),
        out_shape=_sds((T, HEADS * HP), BF16),
        compiler_params=_cparams(VMEM_BIG))(*args)


def attn_bwd(qa, qo, ka, ko, va, vo, cum, dcat, do_off, scale, S, out_dtype):
    T = qa.shape[0]
    B = T // S
    tq = min(ATT_TQ, S)
    nq = S // tq
    use_cum = cum is not None

    def body(*refs):
        if use_cum:
            q_ref, k_ref, v_ref, do_ref, c_ref, dq_ref, dk_ref, dv_ref, dc_ref, dk_acc, dv_acc = refs
            crow = c_ref[pl.ds(pl.program_id(1), 1), :]
        else:
            q_ref, k_ref, v_ref, do_ref, dq_ref, dk_ref, dv_ref, dk_acc, dv_acc = refs
            crow = jnp.zeros((1, S), F32)
        i = pl.program_id(2)

        @pl.when(i == 0)
        def _():
            dk_acc[...] = jnp.zeros_like(dk_acc)
            dv_acc[...] = jnp.zeros_like(dv_acc)
            if use_cum:
                dc_ref[...] = jnp.zeros_like(dc_ref)

        f = functools.partial(_attn_block, qpos0=i * tq, scale=scale, use_cum=use_cum)
        _, vjp = jax.vjp(f, q_ref[...], k_ref[...], v_ref[...], crow)
        dq, dk, dv, dc = vjp(do_ref[...])
        dq_ref[...] = dq.astype(out_dtype)
        dk_acc[...] += dk
        dv_acc[...] += dv
        if use_cum:
            dc_ref[...] += dc

        @pl.when(i == nq - 1)
        def _():
            dk_ref[...] = dk_acc[...].astype(out_dtype)
            dv_ref[...] = dv_acc[...].astype(out_dtype)

    qspec = pl.BlockSpec((tq, HP), lambda b, h, i: (b * nq + i, qo + h))
    in_specs = [qspec, pl.BlockSpec((S, HP), lambda b, h, i: (b, ko + h)),
                pl.BlockSpec((S, HP), lambda b, h, i: (b, vo + h)),
                pl.BlockSpec((tq, HP), lambda b, h, i: (b * nq + i, do_off + h))]
    args = [qa, ka, va, dcat]
    kv_out = pl.BlockSpec((S, HP), lambda b, h, i: (b, h))
    out_specs = [pl.BlockSpec((tq, HP), lambda b, h, i: (b * nq + i, h)), kv_out, kv_out]
    out_shape = [_sds((T, HEADS * HP), out_dtype)] * 3
    if use_cum:
        in_specs.append(pl.BlockSpec((None, 8, S), lambda b, h, i: (b, 0, 0)))
        args.append(cum)
        out_specs.append(pl.BlockSpec((None, 1, S), lambda b, h, i: (b * HEADS + h, 0, 0)))
        out_shape.append(_sds((B * HEADS, 1, S), F32))
    return pl.pallas_call(
        body, name="attn_bwd", grid=(B, HEADS, nq), in_specs=in_specs, out_specs=out_specs, out_shape=out_shape,
        scratch_shapes=[pltpu.VMEM((S, HP), F32), pltpu.VMEM((S, HP), F32)],
        compiler_params=_cparams(VMEM_BIG))(*args)


def _tri(n, upper):
    r, c = _iota((n, n), 0), _iota((n, n), 1)
    return ((r <= c) if upper else (r >= c)).astype(F32)


def fox_gate_fwd(proj, bcol, S):
    T = proj.shape[0]
    B = T // S
    ts = _tile(S, 512)
    nt = S // ts

    def body(p_ref, b_ref, o_ref, carry):
        @pl.when(pl.program_id(1) == 0)
        def _():
            carry[...] = jnp.zeros_like(carry)
        cf = jnp.transpose(p_ref[...])[0:8, :]
        lf = jax.nn.log_sigmoid(cf + b_ref[...])
        cum = _mm_hi(lf, _tri(ts, True)) + carry[...]
        o_ref[...] = cum
        carry[...] += jnp.sum(lf, axis=1, keepdims=True)

    return pl.pallas_call(
        body, name="fox_gate_fwd", grid=(B, nt),
        in_specs=[pl.BlockSpec((ts, HP), lambda b, j: (b * nt + j, C_CF // HP)), pl.BlockSpec((8, 1), lambda b, j: (0, 0))],
        out_specs=pl.BlockSpec((None, 8, ts), lambda b, j: (b, 0, j)), out_shape=_sds((B, 8, S), F32),
        scratch_shapes=[pltpu.VMEM((8, 1), F32)],
        compiler_params=_cparams(VMEM_BIG))(proj, bcol)


def fox_gate_bwd(proj, bcol, dcum, S):
    T = proj.shape[0]
    B = T // S
    ts = _tile(S, 512)
    nt = S // ts

    def body(p_ref, b_ref, dc_ref, dp_ref, db_ref, carry):
        b, j = pl.program_id(0), pl.program_id(1)

        @pl.when(j == 0)
        def _():
            carry[...] = jnp.zeros_like(carry)

        @pl.when((b == 0) & (j == 0))
        def _():
            db_ref[...] = jnp.zeros_like(db_ref)

        cf = jnp.transpose(p_ref[...])[0:8, :]
        dc = dc_ref[...]
        dlf = _mm_hi(dc, _tri(ts, False)) + carry[...]
        carry[...] += jnp.sum(dc, axis=1, keepdims=True)
        dcf = dlf * jax.nn.sigmoid(-(cf + b_ref[...]))
        db_ref[...] += jnp.sum(dcf, axis=1, keepdims=True)
        full = jnp.concatenate([dcf, jnp.zeros((HP - 8, ts), F32)], axis=0)
        dp_ref[...] = jnp.transpose(full).astype(BF16)

    def rev(b, j):
        return nt - 1 - j

    return pl.pallas_call(
        body, name="fox_gate_bwd", grid=(B, nt),
        in_specs=[pl.BlockSpec((ts, HP), lambda b, j: (b * nt + rev(b, j), C_CF // HP)),
                  pl.BlockSpec((8, 1), lambda b, j: (0, 0)),
                  pl.BlockSpec((None, 8, ts), lambda b, j: (b, 0, rev(b, j)))],
        out_specs=[pl.BlockSpec((ts, HP), lambda b, j: (b * nt + rev(b, j), 0)), pl.BlockSpec((8, 1), lambda b, j: (0, 0))],
        out_shape=[_sds((T, HP), BF16), _sds((8, 1), F32)],
        scratch_shapes=[pltpu.VMEM((8, 1), F32)],
        compiler_params=_cparams(VMEM_BIG))(proj, bcol, dcum)


def _mla_pre(blk, gq, gkv, wq, wkv, place, cos_q, sin_q, cs_k):
    nq = _rms(blk[:, 0:B_Q_LORA], gq)
    nkv = _rms(blk[:, B_Q_LORA:B_Q_LORA + B_KV_LORA], gkv)
    qq = _mm(nq, wq)
    q = qq[:, 0:HEADS * HP] * cos_q + qq[:, HEADS * HP:] * sin_q
    kv = _mm(nkv, wkv)
    k = kv[:, 0:HEADS * HP] + _mm(blk[:, B_Q_LORA + B_KV_LORA:] * cs_k, place)
    return q, k, kv[:, HEADS * HP:]


def mla_pre_fwd(proj, gq, gkv, wq, wkv, place, cos_q, sin_q, cs_k, S):
    T = proj.shape[0]
    tm = _tile(S, 512)
    tpb = S // tm
    W = HEADS * HP

    def body(p_ref, gq_ref, gkv_ref, wq_ref, wkv_ref, pl_ref, cq_ref, sq_ref, ck_ref, q_ref, k_ref, v_ref):
        q, k, v = _mla_pre(p_ref[...], gq_ref[...], gkv_ref[...], wq_ref[...], wkv_ref[...], pl_ref[...],
                           cq_ref[...], sq_ref[...], ck_ref[...])
        q_ref[...] = q
        k_ref[...] = k
        v_ref[...] = v

    def full(a):
        return pl.BlockSpec(a.shape, lambda i: (0,) * a.ndim)

    tok = pl.BlockSpec((tm, W), lambda i: (i, 0))
    return pl.pallas_call(
        body, name="mla_pre_fwd", grid=(T // tm,),
        in_specs=[pl.BlockSpec((tm, W), lambda i: (i, C_B // W)), full(gq), full(gkv), full(wq), full(wkv), full(place),
                  pl.BlockSpec((tm, W), lambda i: (i % tpb, 0)), pl.BlockSpec((tm, W), lambda i: (i % tpb, 0)),
                  pl.BlockSpec((tm, HP), lambda i: (i % tpb, 0))],
        out_specs=[tok] * 3, out_shape=[_sds((T, W), F32)] * 3,
        compiler_params=_cparams(VMEM_BIG))(proj, gq, gkv, wq, wkv, place, cos_q, sin_q, cs_k)


def mla_pre_bwd(proj, gq, gkv, wq, wkv, place, cos_q, sin_q, cs_k, dq, dk, dv, S):
    T = proj.shape[0]
    tm = _tile(S, 512)
    tpb = S // tm
    W = HEADS * HP

    def body(p_ref, gq_ref, gkv_ref, wq_ref, wkv_ref, pl_ref, cq_ref, sq_ref, ck_ref, dq_ref, dk_ref, dv_ref,
             dp_ref, dgq_ref, dgkv_ref, dwq_ref, dwkv_ref):
        @pl.when(pl.program_id(0) == 0)
        def _():
            for r in (dgq_ref, dgkv_ref, dwq_ref, dwkv_ref):
                r[...] = jnp.zeros_like(r)

        f = functools.partial(_mla_pre, place=pl_ref[...], cos_q=cq_ref[...], sin_q=sq_ref[...], cs_k=ck_ref[...])
        _, vjp = jax.vjp(f, p_ref[...], gq_ref[...], gkv_ref[...], wq_ref[...], wkv_ref[...])
        dp, dgq, dgkv, dwq, dwkv = vjp((dq_ref[...], dk_ref[...], dv_ref[...]))
        dp_ref[...] = dp.astype(BF16)
        dgq_ref[...] += dgq
        dgkv_ref[...] += dgkv
        dwq_ref[...] += dwq
        dwkv_ref[...] += dwkv

    def full(a):
        return pl.BlockSpec(a.shape, lambda i: (0,) * a.ndim)

    tok = pl.BlockSpec((tm, W), lambda i: (i, 0))
    return pl.pallas_call(
        body, name="mla_pre_bwd", grid=(T // tm,),
        in_specs=[pl.BlockSpec((tm, W), lambda i: (i, C_B // W)), full(gq), full(gkv), full(wq), full(wkv), full(place),
                  pl.BlockSpec((tm, W), lambda i: (i % tpb, 0)), pl.BlockSpec((tm, W), lambda i: (i % tpb, 0)),
                  pl.BlockSpec((tm, HP), lambda i: (i % tpb, 0)), tok, tok, tok],
        out_specs=[tok, full(gq), full(gkv), full(wq), full(wkv)],
        out_shape=[_sds((T, W), BF16), _sds(gq.shape, F32), _sds(gkv.shape, F32), _sds(wq.shape, F32), _sds(wkv.shape, F32)],
        compiler_params=_cparams(VMEM_BIG))(proj, gq, gkv, wq, wkv, place, cos_q, sin_q, cs_k, dq, dk, dv)


def _gmlp_block(blk, lg, lb, ws, bs):
    u = jax.nn.gelu(blk[:, 0:GW])
    v = jax.nn.gelu(blk[:, GW:2 * GW])
    mu = jnp.mean(v, axis=-1, keepdims=True)
    var = jnp.mean(jnp.square(v - mu), axis=-1, keepdims=True)
    vn = (v - mu) * lax.rsqrt(var + LN_EPS) * lg + lb
    causal = _iota((D_CHUNK, D_CHUNK), 1) <= _iota((D_CHUNK, D_CHUNK), 0)
    group = _iota((1, GW), 1) // HD
    mixed = jnp.zeros((D_CHUNK, GW), F32)
    for g in range(HEADS):
        part = _mm(jnp.where(causal, ws[g], 0.0), vn) + bs[g]
        mixed = mixed + jnp.where(group == g, part, 0.0)
    return u * mixed


def gmlp_fwd(proj, lg, lb, ws, bs):
    T = proj.shape[0]

    def body(p_ref, lg_ref, lb_ref, ws_ref, bs_ref, o_ref):
        o_ref[...] = _gmlp_block(p_ref[...], lg_ref[...], lb_ref[...], ws_ref[...], bs_ref[...]).astype(BF16)

    def full(a):
        return pl.BlockSpec(a.shape, lambda i: (0,) * a.ndim)

    return pl.pallas_call(
        body, name="gmlp_fwd", grid=(T // D_CHUNK,),
        in_specs=[pl.BlockSpec((D_CHUNK, 2 * GW), lambda i: (i, C_D // (2 * GW))), full(lg), full(lb), full(ws), full(bs)],
        out_specs=pl.BlockSpec((D_CHUNK, GW), lambda i: (i, 0)), out_shape=_sds((T, GW), BF16),
        compiler_params=_cparams(VMEM_BIG))(proj, lg, lb, ws, bs)


def gmlp_bwd(proj, lg, lb, ws, bs, dcat):
    T = proj.shape[0]

    def body(p_ref, lg_ref, lb_ref, ws_ref, bs_ref, do_ref, dp_ref, dlg_ref, dlb_ref, dws_ref, dbs_ref):
        @pl.when(pl.program_id(0) == 0)
        def _():
            for r in (dlg_ref, dlb_ref, dws_ref, dbs_ref):
                r[...] = jnp.zeros_like(r)

        _, vjp = jax.vjp(_gmlp_block, p_ref[...], lg_ref[...], lb_ref[...], ws_ref[...], bs_ref[...])
        dp, dlg, dlb, dws, dbs = vjp(do_ref[...])
        dp_ref[...] = dp.astype(BF16)
        dlg_ref[...] += dlg
        dlb_ref[...] += dlb
        dws_ref[...] += dws
        dbs_ref[...] += dbs

    def full(a):
        return pl.BlockSpec(a.shape, lambda i: (0,) * a.ndim)

    return pl.pallas_call(
        body, name="gmlp_bwd", grid=(T // D_CHUNK,),
        in_specs=[pl.BlockSpec((D_CHUNK, 2 * GW), lambda i: (i, C_D // (2 * GW))), full(lg), full(lb), full(ws), full(bs),
                  pl.BlockSpec((D_CHUNK, GW), lambda i: (i, O_D // GW))],
        out_specs=[pl.BlockSpec((D_CHUNK, 2 * GW), lambda i: (i, 0)), full(lg), full(lb), full(ws), full(bs)],
        out_shape=[_sds((T, 2 * GW), BF16), _sds(lg.shape, F32), _sds(lb.shape, F32), _sds(ws.shape, F32), _sds(bs.shape, F32)],
        compiler_params=_cparams(VMEM_BIG))(proj, lg, lb, ws, bs, dcat)


def _lb_all(logits):
    m = jnp.max(logits, axis=0, keepdims=True)
    e = jnp.exp(logits - m)
    sm = e / jnp.sum(e, axis=0, keepdims=True)
    return jnp.concatenate([sm[0:1] - sm[0:1], (sm[0:1] + sm[1:2]) - sm[0:1]], axis=0)


def lb_fwd(logits):
    def body(l_ref, o_ref):
        o_ref[...] = _lb_all(l_ref[...])

    return pl.pallas_call(body, name="lb_fwd", out_shape=_sds(logits.shape, F32))(logits)


def lb_bwd(logits, dlb):
    def body(l_ref, d_ref, o_ref):
        _, vjp = jax.vjp(_lb_all, l_ref[...])
        o_ref[...] = vjp(d_ref[...])[0]

    return pl.pallas_call(body, name="lb_bwd", out_shape=_sds(logits.shape, F32))(logits, dlb)


def ada_grad(c_all, dmod_cols):
    N = dmod_cols.shape[1]
    tn = _tile(N, 1152)

    def body(c_ref, d_ref, o_ref):
        h = jax.nn.silu(c_ref[...]).astype(BF16)
        o_ref[...] = lax.dot_general(h, d_ref[...].astype(BF16), (((0,), (0,)), ((), ())), preferred_element_type=F32)

    nb = c_all.shape[0]
    return pl.pallas_call(
        body, name="ada_grad", grid=(N // tn,),
        in_specs=[pl.BlockSpec((nb, D), lambda j: (0, 0)), pl.BlockSpec((nb, tn), lambda j: (0, j))],
        out_specs=pl.BlockSpec((D, tn), lambda j: (0, j)), out_shape=_sds((D, N), F32),
        compiler_params=_cparams(VMEM_BIG))(c_all, dmod_cols)


def sum_slots(a, n, name):
    _, R, C = a.shape
    tr = _row_tile(R, C, n)

    def body(a_ref, o_ref):
        acc = a_ref[0]
        for k in range(1, n):
            acc = acc + a_ref[k]
        o_ref[...] = acc

    return pl.pallas_call(
        body, name=name, grid=(R // tr,),
        in_specs=[pl.BlockSpec((n, tr, C), lambda i: (0, i, 0))],
        out_specs=pl.BlockSpec((tr, C), lambda i: (i, 0)), out_shape=_sds((R, C), F32),
        compiler_params=_cparams(VMEM_BIG))(a)


def add2(a, b, name):
    shp = a.shape
    C = shp[-1]
    a2, b2 = a.reshape(-1, C), b.reshape(-1, C)
    R = a2.shape[0]
    tr = _row_tile(R, C)

    def body(a_ref, b_ref, o_ref):
        o_ref[...] = a_ref[...] + b_ref[...]

    spec = pl.BlockSpec((tr, C), lambda i: (i, 0))
    return pl.pallas_call(body, name=name, grid=(R // tr,), in_specs=[spec, spec], out_specs=spec,
                          out_shape=_sds((R, C), F32), compiler_params=_cparams(VMEM_BIG))(a2, b2).reshape(shp)


def _row_tile(R, C=D, n=1):
    limit = max(8, (1 << 18) // (C * n))
    for t in range(limit - limit % 8, 7, -8):
        if R % t == 0:
            return t
    return R


def adamw(w, g, m, v, name):
    R, C = w.shape
    tr = _row_tile(R, C)
    c1 = 1.0 - ADAM_B1 ** ADAM_STEP
    c2 = 1.0 - ADAM_B2 ** ADAM_STEP

    def body(w_ref, g_ref, m_ref, v_ref, d_ref, nm_ref, nv_ref):
        g_ = g_ref[...]
        nm = ADAM_B1 * m_ref[...] + (1.0 - ADAM_B1) * g_
        nv = ADAM_B2 * v_ref[...] + (1.0 - ADAM_B2) * jnp.square(g_)
        d_ref[...] = -ADAM_LR * ((nm / c1) / (jnp.sqrt(nv / c2) + ADAM_EPS) + ADAM_WD * w_ref[...])
        nm_ref[...] = nm
        nv_ref[...] = nv

    spec = pl.BlockSpec((tr, C), lambda i: (i, 0))
    return pl.pallas_call(body, name=name, grid=(R // tr,), in_specs=[spec] * 4, out_specs=[spec] * 3,
                          out_shape=[_sds((R, C), F32)] * 3, compiler_params=_cparams(VMEM_BIG))(w, g, m, v)


def _rot_cols(w):
    return jnp.concatenate([-w[:, 16:32], w[:, 0:16]], axis=1)


def _fold_rot(d):
    return jnp.concatenate([d[:, 16:32], -d[:, 0:16]], axis=1)


def _pad_heads(w, off, axis):
    parts = []
    for h in range(HEADS):
        piece = lax.slice_in_dim(w, off + HD * h, off + HD * (h + 1), axis=axis)
        parts += [piece, jnp.zeros_like(piece)]
    return parts


def _unpad_heads(d, off, axis):
    return [lax.slice_in_dim(d, off + HP * h, off + HP * h + HD, axis=axis) for h in range(HEADS)]


def mix_in_ext(w):
    z = lambda n: jnp.zeros((w.shape[0], n), w.dtype)
    kr = w[:, 1408:1440]
    cols = [w[:, 0:1408], kr, _rot_cols(kr), z(64)]
    cols += _pad_heads(w, 1440, 1) + _pad_heads(w, 1696, 1) + _pad_heads(w, 1952, 1)
    cols += [w[:, 2212:2724], w[:, 2208:2212], z(NP - C_CF - HEADS)]
    return jnp.concatenate(cols, axis=1)


def mix_in_unext(d):
    kr = d[:, 1408:1440] + _fold_rot(d[:, 1440:1472])
    cols = [d[:, 0:1408], kr] + _unpad_heads(d, C_CQ, 1) + _unpad_heads(d, C_CK, 1) + _unpad_heads(d, C_CV, 1)
    cols += [d[:, C_CF:C_CF + HEADS], d[:, C_D:C_D + 2 * GW]]
    return jnp.concatenate(cols, axis=1)


def mix_out_ext(w):
    return jnp.concatenate([w[0:GW]] + _pad_heads(w, GW, 0) + _pad_heads(w, 2 * GW, 0) + [w[3 * GW:4 * GW]], axis=0)


def mix_out_unext(d):
    return jnp.concatenate([d[0:GW]] + _unpad_heads(d, O_B, 0) + _unpad_heads(d, O_C, 0) + [d[O_D:O_D + GW]], axis=0)


def uq_ext(w):
    z = lambda n: jnp.zeros((w.shape[0], n), w.dtype)
    a, b = [], []
    for h in range(HEADS):
        o = (B_NOPE + B_ROPE) * h
        a += [w[:, o:o + B_NOPE + B_ROPE], z(32)]
        b += [z(B_NOPE), _rot_cols(w[:, o + B_NOPE:o + B_NOPE + B_ROPE]), z(32)]
    return jnp.concatenate(a + b, axis=1)


def uq_unext(d):
    cols = []
    for h in range(HEADS):
        o = HP * h
        cols += [d[:, o:o + B_NOPE], d[:, o + B_NOPE:o + B_NOPE + B_ROPE]
                 + _fold_rot(d[:, HEADS * HP + o + B_NOPE:HEADS * HP + o + B_NOPE + B_ROPE])]
    return jnp.concatenate(cols, axis=1)


def ukv_ext(w):
    z = jnp.zeros((w.shape[0], HD), w.dtype)
    k, v = [], []
    for h in range(HEADS):
        k += [w[:, 2 * HD * h:2 * HD * h + HD], z]
        v += [w[:, 2 * HD * h + HD:2 * HD * (h + 1)], z]
    return jnp.concatenate(k + v, axis=1)


def ukv_unext(d):
    cols = []
    for h in range(HEADS):
        cols += [d[:, HP * h:HP * h + HD], d[:, HEADS * HP + HP * h:HEADS * HP + HP * h + HD]]
    return jnp.concatenate(cols, axis=1)


def rope_tables(S):
    half = B_ROPE // 2
    inv_freq = ROPE_THETA ** (-jnp.arange(half, dtype=F32) / half)
    ang = jnp.arange(S).astype(F32)[:, None] * inv_freq[None, :]
    cos = jnp.tile(jnp.cos(ang), (1, 2))
    sin = jnp.tile(jnp.sin(ang), (1, 2))
    one, zero = jnp.ones((S, B_NOPE), F32), jnp.zeros((S, B_NOPE), F32)
    z32 = jnp.zeros((S, 32), F32)
    cos_q = jnp.tile(jnp.concatenate([one, cos, z32], axis=1), (1, HEADS))
    sin_q = jnp.tile(jnp.concatenate([zero, sin, z32], axis=1), (1, HEADS))
    cs_k = jnp.concatenate([cos, sin, zero], axis=1)
    place = np.zeros((HP, HEADS * HP), np.float32)
    for h in range(HEADS):
        for j in range(B_ROPE):
            place[j, h * HP + B_NOPE + j] = 1.0
            place[B_ROPE + j, h * HP + B_NOPE + j] = 1.0
    return cos_q, sin_q, cs_k, jnp.asarray(place, BF16)


def layer_fwd(x, mod, p, tabs, S):
    cos_q, sin_q, cs_k, place = tabs
    zg1, zu1, act1 = ffn_in_fwd(x, mod, p["ffn1_in"], 0, 1, S)
    y1, x1 = out_ln_fwd(act1, p["ffn1_out"], x, mod, p["ln_g"][0:1], p["ln_b"][0:1], 2, 0.5, S)
    proj = mix_in_fwd(x1, mod, p["mix_in"], 3, 4, S)
    o_a, states = hgrn_fwd(proj, p["lb"], p["ng"], S)
    q_b, k_b, v_b = mla_pre_fwd(proj, p["gq"], p["gkv"], p["wq"], p["wkv"], place, cos_q, sin_q, cs_k, S)
    o_b = attn_fwd(q_b, 0, k_b, 0, v_b, 0, None, (B_NOPE + B_ROPE) ** -0.5, S)
    cum = fox_gate_fwd(proj, p["bcol"], S)
    o_c = attn_fwd(proj, C_CQ // HP, proj, C_CK // HP, proj, C_CV // HP, cum, HD ** -0.5, S)
    o_d = gmlp_fwd(proj, p["g_lg"], p["g_lb"], p["ws"], p["bs"])
    cat = jnp.concatenate([o_a, o_b, o_c, o_d], axis=1)
    y2, x2 = out_ln_fwd(cat, p["mix_out"], x1, mod, p["ln_g"][1:2], p["ln_b"][1:2], 5, 1.0, S)
    zg3, zu3, act3 = ffn_in_fwd(x2, mod, p["ffn2_in"], 6, 7, S)
    y3, x3 = out_ln_fwd(act3, p["ffn2_out"], x2, mod, p["ln_g"][2:3], p["ln_b"][2:3], 8, 0.5, S)
    saved = dict(x=x, zg1=zg1, zu1=zu1, act1=act1, y1=y1, x1=x1, proj=proj, states=states, q_b=q_b, k_b=k_b, v_b=v_b,
                 cum=cum, cat=cat, y2=y2, x2=x2, zg3=zg3, zu3=zu3, act3=act3, y3=y3)
    return x3, saved


def _ffn_bwd(dxn, x_in, y, zg, zu, act, mod, w_in, w_out, lg, lb, idx, S):
    sh, sc, gate = idx
    dres, dy, dgate, dlg, dlb = ln_res_bwd(dxn, x_in, y, mod, lg, lb, gate, 0.5, S)
    dzg, dzu = swiglu_bwd(dy, w_out, zg, zu, S)
    dw_out = tn_mm(act, dy, 1408)
    dw_in = (tn_mm_mod(x_in, mod, dzg, sh, sc, S, 1408), tn_mm_mod(x_in, mod, dzu, sh, sc, S, 1408))
    dx, dsh, dsc = nt_mod_bwd([dzg, dzu], w_in, [0, FF], x_in, mod, dres, sc, S, 1408)
    return dx, dw_in, dw_out, dlg, dlb, {sh: dsh, sc: dsc, gate: dgate}


def layer_bwd(dx3, mod, p, sv, tabs, S):
    cos_q, sin_q, cs_k, place = tabs
    g = {}
    dm = {}
    dx2, g["ffn2_in"], g["ffn2_out"], dlg2, dlb2, d = _ffn_bwd(
        dx3, sv["x2"], sv["y3"], sv["zg3"], sv["zu3"], sv["act3"], mod, p["ffn2_in"], p["ffn2_out"],
        p["ln_g"][2:3], p["ln_b"][2:3], (6, 7, 8), S)
    dm.update(d)
    dres, dy2, dm[5], dlg1, dlb1 = ln_res_bwd(dx2, sv["x1"], sv["y2"], mod, p["ln_g"][1:2], p["ln_b"][1:2], 5, 1.0, S)
    dcat = nt_plain(dy2, p["mix_out"])
    g["mix_out"] = tn_mm(sv["cat"], dy2, 768)
    proj = sv["proj"]
    d_a, g["lb"], g["ng"] = hgrn_bwd(proj, sv["states"], dcat, p["lb"], p["ng"], S)
    dq_c, dk_c, dv_c, dcum = attn_bwd(proj, C_CQ // HP, proj, C_CK // HP, proj, C_CV // HP, sv["cum"], dcat,
                                      O_C // HP, HD ** -0.5, S, BF16)
    B = proj.shape[0] // S
    dcum = jnp.concatenate([dcum.reshape(B, HEADS, S), jnp.zeros((B, 8 - HEADS, S), F32)], axis=1)
    d_cf, g["bcol"] = fox_gate_bwd(proj, p["bcol"], dcum, S)
    dq_b, dk_b, dv_b = attn_bwd(sv["q_b"], 0, sv["k_b"], 0, sv["v_b"], 0, None, dcat, O_B // HP,
                                (B_NOPE + B_ROPE) ** -0.5, S, F32)
    d_b, g["gq"], g["gkv"], g["wq"], g["wkv"] = mla_pre_bwd(
        proj, p["gq"], p["gkv"], p["wq"], p["wkv"], place, cos_q, sin_q, cs_k, dq_b, dk_b, dv_b, S)
    d_d, g["g_lg"], g["g_lb"], g["ws"], g["bs"] = gmlp_bwd(proj, p["g_lg"], p["g_lb"], p["ws"], p["bs"], dcat)
    dproj = jnp.concatenate([d_a, d_b, dq_c, dk_c, dv_c, d_d, d_cf, jnp.zeros_like(d_cf)], axis=1)
    g["mix_in"] = tn_mm_mod(sv["x1"], mod, dproj, 3, 4, S, NP_TILE)
    dx1, dm[3], dm[4] = nt_mod_bwd([dproj], p["mix_in"], [0], sv["x1"], mod, dres, 4, S, NP_TILE)
    dx0, g["ffn1_in"], g["ffn1_out"], dlg0, dlb0, d = _ffn_bwd(
        dx1, sv["x"], sv["y1"], sv["zg1"], sv["zu1"], sv["act1"], mod, p["ffn1_in"], p["ffn1_out"],
        p["ln_g"][0:1], p["ln_b"][0:1], (0, 1, 2), S)
    dm.update(d)
    g["ln_g"] = jnp.concatenate([dlg0, dlg1, dlg2], axis=0)
    g["ln_b"] = jnp.concatenate([dlb0, dlb1, dlb2], axis=0)
    dmod = jnp.concatenate([dm[i] for i in range(N_MOD)], axis=1)
    return dx0, dmod, g


def local_step(x, c8, tgt, ada_w, ada_b, layers, lb_logits, S):
    B = x.shape[0] // S
    tabs = rope_tables(S)
    lb_all = lb_fwd(lb_logits)
    mods, saved = [], []
    h = x
    for l in range(DEPTH):
        mod = mod_fwd(c8, ada_w[l], ada_b[l])[0:B].reshape(B, N_MOD, D)
        p = dict(layers[l], lb=lb_all[l:l + 1])
        h, sv = layer_fwd(h, mod, p, tabs, S)
        mods.append(mod)
        saved.append(sv)
    loss_tile, dh = loss_head(h, tgt)
    grads, dmods, dlb = [None] * DEPTH, [None] * DEPTH, [None] * DEPTH
    for l in reversed(range(DEPTH)):
        p = dict(layers[l], lb=lb_all[l:l + 1])
        dh, dmods[l], grads[l] = layer_bwd(dh, mods[l], p, saved[l], tabs, S)
        dlb[l] = grads[l].pop("lb")
    d_logits = lb_bwd(lb_logits, jnp.concatenate(dlb, axis=0))
    return loss_tile, dh, dmods, grads, d_logits


ANY = pl.BlockSpec(memory_space=pl.ANY)


def _place():
    x, y, c = lax.axis_index("x"), lax.axis_index("y"), lax.axis_index("c")
    chips = [(1 - x, y), (x, 1 - y), (1 - x, 1 - y)]
    return x, y, c, chips


def _rcopy(src, dst, sems, k, to):
    send_sems, recv_sems = sems
    return pltpu.make_async_remote_copy(src_ref=src, dst_ref=dst, send_sem=send_sems.at[k], recv_sem=recv_sems.at[k],
                                        device_id=to, device_id_type=MESH)


def ag_shards(pk):
    R, C = pk.shape
    Rh = R // 2

    def body(src, out, send_sems, recv_sems, loc_sem):
        x, y, c, chips = _place()
        sems = (send_sems, recv_sems)
        me = 2 * x + y
        sibling = (x, y, 1 - c)

        def half(k, hc):
            return out.at[k, pl.ds(hc * Rh, Rh), :]

        mine = pltpu.make_async_copy(src, out.at[me], loc_sem)
        mine.start()
        first = [_rcopy(src.at[pl.ds(c * Rh, Rh), :], half(me, c), sems, j, (px, py, c)) for j, (px, py) in enumerate(chips)]
        for cp in first:
            cp.start()
        passed = [_rcopy(half(2 * px + py, c), half(2 * px + py, c), sems, 3 + j, sibling) for j, (px, py) in enumerate(chips)]
        for j, (px, py) in enumerate(chips):
            _rcopy(half(2 * px + py, c), half(2 * px + py, c), sems, j, (px, py, c)).wait_recv()
            passed[j].start()
        for j, (px, py) in enumerate(chips):
            _rcopy(half(2 * px + py, 1 - c), half(2 * px + py, 1 - c), sems, 3 + j, sibling).wait_recv()
        for cp in first + passed:
            cp.wait_send()
        mine.wait()

    return pl.pallas_call(
        body, name="ag_shards", out_shape=_sds((4, R, C), pk.dtype), in_specs=[ANY], out_specs=ANY,
        scratch_shapes=[pltpu.SemaphoreType.DMA((6,)), pltpu.SemaphoreType.DMA((6,)), pltpu.SemaphoreType.DMA(())])(pk)


def sibling_swap(a):
    def body(src, out, send_sems, recv_sems):
        x, y, c, _ = _place()
        cp = _rcopy(src, out, (send_sems, recv_sems), 0, (x, y, 1 - c))
        cp.start()
        cp.wait()

    return pl.pallas_call(
        body, name="sibling_swap", out_shape=_sds(a.shape, a.dtype), in_specs=[ANY], out_specs=ANY,
        scratch_shapes=[pltpu.SemaphoreType.DMA((1,)), pltpu.SemaphoreType.DMA((1,))])(a)


def chip_exchange(h):
    def body(src, out, send_sems, recv_sems, loc_sem):
        x, y, c, chips = _place()
        sems = (send_sems, recv_sems)
        me = 2 * x + y
        mine = pltpu.make_async_copy(src.at[me], out.at[me], loc_sem)
        mine.start()
        sends = [_rcopy(src.at[2 * px + py], out.at[me], sems, j, (px, py, c)) for j, (px, py) in enumerate(chips)]
        for cp in sends:
            cp.start()
        for j, (px, py) in enumerate(chips):
            _rcopy(src.at[2 * px + py], out.at[2 * px + py], sems, j, (px, py, c)).wait_recv()
        for cp in sends:
            cp.wait_send()
        mine.wait()

    return pl.pallas_call(
        body, name="chip_exchange", out_shape=_sds(h.shape, h.dtype), in_specs=[ANY], out_specs=ANY,
        scratch_shapes=[pltpu.SemaphoreType.DMA((3,)), pltpu.SemaphoreType.DMA((3,)), pltpu.SemaphoreType.DMA(())])(h)


def sibling_join(red):
    Rh, C = red.shape

    def body(src, out, send_sems, recv_sems, loc_sem):
        x, y, c, _ = _place()
        mine = pltpu.make_async_copy(src, out.at[pl.ds(c * Rh, Rh), :], loc_sem)
        mine.start()
        cp = _rcopy(src, out.at[pl.ds(c * Rh, Rh), :], (send_sems, recv_sems), 0, (x, y, 1 - c))
        cp.start()
        _rcopy(src, out.at[pl.ds((1 - c) * Rh, Rh), :], (send_sems, recv_sems), 0, (x, y, 1 - c)).wait_recv()
        cp.wait_send()
        mine.wait()

    return pl.pallas_call(
        body, name="sibling_join", out_shape=_sds((2 * Rh, C), red.dtype), in_specs=[ANY], out_specs=ANY,
        scratch_shapes=[pltpu.SemaphoreType.DMA((1,)), pltpu.SemaphoreType.DMA((1,)), pltpu.SemaphoreType.DMA(())])(red)


def ag_all(blk):
    M, C = blk.shape

    def body(x_ref, out_ref, send_sems, recv_sems, loc_sem):
        x, y, c, chips = _place()
        sems = (send_sems, recv_sems)
        me, sibling = (x, y, c), (x, y, 1 - c)

        def slot(px, py, pc):
            return out_ref.at[4 * px + 2 * py + pc]

        mine = pltpu.make_async_copy(x_ref, slot(*me), loc_sem)
        mine.start()
        first = [_rcopy(x_ref, slot(*me), sems, 0, sibling)]
        first += [_rcopy(x_ref, slot(*me), sems, 1 + j, (*chip, c)) for j, chip in enumerate(chips)]
        for cp in first:
            cp.start()
        passed = [_rcopy(slot(*chip, c), slot(*chip, c), sems, 4 + j, sibling) for j, chip in enumerate(chips)]
        for j, chip in enumerate(chips):
            _rcopy(slot(*chip, c), slot(*chip, c), sems, 1 + j, me).wait_recv()
            passed[j].start()
        _rcopy(slot(*sibling), slot(*sibling), sems, 0, me).wait_recv()
        for j, chip in enumerate(chips):
            _rcopy(slot(*chip, 1 - c), slot(*chip, 1 - c), sems, 4 + j, me).wait_recv()
        for cp in first + passed:
            cp.wait_send()
        mine.wait()

    return pl.pallas_call(
        body, name="ag_all", out_shape=_sds((8, M, C), blk.dtype),
        in_specs=[pl.BlockSpec(memory_space=pltpu.VMEM)], out_specs=pl.BlockSpec(memory_space=pltpu.VMEM),
        scratch_shapes=[pltpu.SemaphoreType.DMA((7,)), pltpu.SemaphoreType.DMA((7,)), pltpu.SemaphoreType.DMA(())],
        compiler_params=_cparams(VMEM_BIG))(blk)


WEIGHTS = ["ada_w", "ada_b", "ln_g", "ln_b", "ffn1_w_in", "ffn1_w_out", "ffn2_w_in", "ffn2_w_out", "mix_w_in", "mix_w_out",
           "hgrn_lb_logits", "hgrn_norm_g", "mla_q_norm_g", "mla_kv_norm_g", "mla_w_uq", "mla_w_ukv", "fox_b_f",
           "gmlp_ln_g", "gmlp_ln_b", "gmlp_w_s", "gmlp_b_s"]
SHARDED = {"ffn1_w_in": 1, "ffn1_w_out": 0, "ffn2_w_in": 1, "ffn2_w_out": 0, "mix_w_in": 1, "mix_w_out": 0,
           "mla_w_uq": 1, "mla_w_ukv": 1}
SMALL = ["hgrn_lb_logits", "hgrn_norm_g", "mla_q_norm_g", "mla_kv_norm_g", "fox_b_f", "gmlp_ln_g", "gmlp_ln_b",
         "gmlp_w_s", "gmlp_b_s"]
N_CHIPS = 4


def _rows(parts, n_rows, dtype):
    flat = jnp.concatenate([p.reshape(-1) for p in parts])
    pad = n_rows * D - flat.shape[0]
    return jnp.concatenate([flat, jnp.zeros((pad,), dtype)]).reshape(n_rows, D)


def _take(flat, shapes):
    out, o = [], 0
    for shp in shapes:
        n = int(np.prod(shp))
        out.append(flat[o:o + n].reshape(shp))
        o += n
    return out


def _round_up(n, m):
    return -(-n // m) * m


def pack_shard(w):
    parts = [w[n][l] for l in range(DEPTH) for n in SHARDED] + [w[n][l] for l in range(DEPTH) for n in ("ln_g", "ln_b")]
    n = sum(int(np.prod(p.shape)) for p in parts)
    return _rows(parts, _round_up(-(-n // D), 16), F32)


def unpack_shard(pk, like):
    shapes = [like[n].shape[1:] for l in range(DEPTH) for n in SHARDED] + [like[n].shape[1:] for l in range(DEPTH) for n in ("ln_g", "ln_b")]
    pieces = _take(pk.reshape(-1), shapes)
    names = [n for l in range(DEPTH) for n in SHARDED] + [n for l in range(DEPTH) for n in ("ln_g", "ln_b")]
    out = {}
    for n in list(SHARDED) + ["ln_g", "ln_b"]:
        out[n] = jnp.stack([p for p, m in zip(pieces, names) if m == n])
    return out


def pack_small(w):
    parts = [w[n][l] for l in range(DEPTH) for n in SMALL]
    n = sum(int(np.prod(p.shape)) for p in parts)
    return _rows(parts, _round_up(-(-n // D), 8), F32)


def unpack_small(pk, like):
    shapes = [like[n].shape[1:] for l in range(DEPTH) for n in SMALL]
    pieces = _take(pk.reshape(-1), shapes)
    names = [n for l in range(DEPTH) for n in SMALL]
    return {n: jnp.stack([p for p, m in zip(pieces, names) if m == n]) for n in SMALL}


def pack_gather(w):
    parts = [w[n][l].astype(BF16) for l in range(DEPTH) for n in ["ada_w"] + list(SHARDED)]
    ln = jnp.concatenate([w[n][l].reshape(-1) for l in range(DEPTH) for n in ("ln_g", "ln_b")])
    parts.append(lax.bitcast_convert_type(ln, BF16))
    n = sum(int(np.prod(p.shape)) for p in parts)
    return _rows(parts, _round_up(-(-n // D), 16), BF16)


def unpack_gather(g, w):
    names = ["ada_w"] + list(SHARDED)
    shapes = [w[n].shape[1:] for l in range(DEPTH) for n in names]
    n_ln = DEPTH * 2 * 3 * (D // N_CHIPS)
    flat = g.reshape(N_CHIPS, -1)
    per_chip = [_take(flat[k], shapes + [(n_ln, 2)]) for k in range(N_CHIPS)]
    layers = [dict() for _ in range(DEPTH)]
    i = 0
    for l in range(DEPTH):
        for n in names:
            axis = 1 if n == "ada_w" else SHARDED[n]
            layers[l][n] = jnp.concatenate([per_chip[k][i] for k in range(N_CHIPS)], axis=axis)
            i += 1
    ln = [lax.bitcast_convert_type(per_chip[k][i], F32).reshape(DEPTH, 2, 3, D // N_CHIPS) for k in range(N_CHIPS)]
    ln = jnp.concatenate(ln, axis=3)
    for l in range(DEPTH):
        layers[l]["ln_g"], layers[l]["ln_b"] = ln[l, 0], ln[l, 1]
    return layers


def pack_grads(grads, k):
    parts = []
    for l in range(DEPTH):
        g = grads[l]
        full = {"ffn1_w_out": g["ffn1_out"], "ffn2_w_out": g["ffn2_out"], "mix_w_in": mix_in_unext(g["mix_in"]),
                "mix_w_out": mix_out_unext(g["mix_out"]), "mla_w_uq": uq_unext(g["wq"]), "mla_w_ukv": ukv_unext(g["wkv"])}
        for n, axis in SHARDED.items():
            if n in ("ffn1_w_in", "ffn2_w_in"):
                half = g[n.replace("_w_in", "_in")][k // 2]
                parts.append(half[:, (k % 2) * (FF // 2):(k % 2 + 1) * (FF // 2)])
            else:
                sz = full[n].shape[axis] // N_CHIPS
                parts.append(lax.slice_in_dim(full[n], k * sz, (k + 1) * sz, axis=axis))
    for l in range(DEPTH):
        for n in ("ln_g", "ln_b"):
            parts.append(grads[l][n][:, k * (D // N_CHIPS):(k + 1) * (D // N_CHIPS)])
    n = sum(int(np.prod(p.shape)) for p in parts)
    return _rows(parts, _round_up(-(-n // D), 16), F32)


def kernel(x, c, ada_w, ada_b, ln_g, ln_b, ffn1_w_in, ffn1_w_out, ffn2_w_in, ffn2_w_out, mix_w_in, mix_w_out, hgrn_lb_logits, hgrn_norm_g, mla_q_norm_g, mla_kv_norm_g, mla_w_uq, mla_w_ukv, fox_b_f, gmlp_ln_g, gmlp_ln_b, gmlp_w_s, gmlp_b_s, loss_target, m_ada_w, m_ada_b, m_ln_g, m_ln_b, m_ffn1_w_in, m_ffn1_w_out, m_ffn2_w_in, m_ffn2_w_out, m_mix_w_in, m_mix_w_out, m_hgrn_lb_logits, m_hgrn_norm_g, m_mla_q_norm_g, m_mla_kv_norm_g, m_mla_w_uq, m_mla_w_ukv, m_fox_b_f, m_gmlp_ln_g, m_gmlp_ln_b, m_gmlp_w_s, m_gmlp_b_s, v_ada_w, v_ada_b, v_ln_g, v_ln_b, v_ffn1_w_in, v_ffn1_w_out, v_ffn2_w_in, v_ffn2_w_out, v_mix_w_in, v_mix_w_out, v_hgrn_lb_logits, v_hgrn_norm_g, v_mla_q_norm_g, v_mla_kv_norm_g, v_mla_w_uq, v_mla_w_ukv, v_fox_b_f, v_gmlp_ln_g, v_gmlp_ln_b, v_gmlp_w_s, v_gmlp_b_s):
    w = dict(zip(WEIGHTS, (ada_w, ada_b, ln_g, ln_b, ffn1_w_in, ffn1_w_out, ffn2_w_in, ffn2_w_out, mix_w_in, mix_w_out, hgrn_lb_logits, hgrn_norm_g, mla_q_norm_g, mla_kv_norm_g, mla_w_uq, mla_w_ukv, fox_b_f, gmlp_ln_g, gmlp_ln_b, gmlp_w_s, gmlp_b_s)))
    m = dict(zip(WEIGHTS, (m_ada_w, m_ada_b, m_ln_g, m_ln_b, m_ffn1_w_in, m_ffn1_w_out, m_ffn2_w_in, m_ffn2_w_out, m_mix_w_in, m_mix_w_out, m_hgrn_lb_logits, m_hgrn_norm_g, m_mla_q_norm_g, m_mla_kv_norm_g, m_mla_w_uq, m_mla_w_ukv, m_fox_b_f, m_gmlp_ln_g, m_gmlp_ln_b, m_gmlp_w_s, m_gmlp_b_s)))
    v = dict(zip(WEIGHTS, (v_ada_w, v_ada_b, v_ln_g, v_ln_b, v_ffn1_w_in, v_ffn1_w_out, v_ffn2_w_in, v_ffn2_w_out, v_mix_w_in, v_mix_w_out, v_hgrn_lb_logits, v_hgrn_norm_g, v_mla_q_norm_g, v_mla_kv_norm_g, v_mla_w_uq, v_mla_w_ukv, v_fox_b_f, v_gmlp_ln_g, v_gmlp_ln_b, v_gmlp_w_s, v_gmlp_b_s)))
    Bl, S, _ = x.shape
    T = Bl * S
    core = lax.axis_index("c")
    chip = 2 * lax.axis_index("x") + lax.axis_index("y")

    full = unpack_gather(ag_shards(pack_gather(w)), w)
    layers = []
    for l in range(DEPTH):
        f = full[l]
        layers.append(dict(
            ffn1_in=f["ffn1_w_in"], ffn1_out=f["ffn1_w_out"], ffn2_in=f["ffn2_w_in"], ffn2_out=f["ffn2_w_out"],
            mix_in=mix_in_ext(f["mix_w_in"]), mix_out=mix_out_ext(f["mix_w_out"]),
            wq=uq_ext(f["mla_w_uq"]).astype(F32), wkv=ukv_ext(f["mla_w_ukv"]).astype(F32),
            ln_g=f["ln_g"], ln_b=f["ln_b"], ng=hgrn_norm_g[l][None], gq=mla_q_norm_g[l][None], gkv=mla_kv_norm_g[l][None],
            bcol=jnp.concatenate([fox_b_f[l], jnp.zeros((8 - HEADS,), F32)])[:, None],
            g_lg=gmlp_ln_g[l][None], g_lb=gmlp_ln_b[l][None], ws=gmlp_w_s[l], bs=gmlp_b_s[l][:, :, None]))
    c8 = jnp.concatenate([c, jnp.zeros((8 - Bl, D), F32)], axis=0)
    loss_tile, dx, dmods, grads, d_logits = local_step(
        x.reshape(T, D), c8, loss_target.reshape(T, D), [full[l]["ada_w"] for l in range(DEPTH)],
        [ada_b[l][None] for l in range(DEPTH)], layers, hgrn_lb_logits, S)
    loss = lax.psum(loss_tile[0, 0], ("x", "y", "c"))

    small_g = {"hgrn_lb_logits": d_logits,
               "hgrn_norm_g": jnp.stack([grads[l]["ng"][0] for l in range(DEPTH)]),
               "mla_q_norm_g": jnp.stack([grads[l]["gq"][0] for l in range(DEPTH)]),
               "mla_kv_norm_g": jnp.stack([grads[l]["gkv"][0] for l in range(DEPTH)]),
               "fox_b_f": jnp.stack([grads[l]["bcol"][0:HEADS, 0] for l in range(DEPTH)]),
               "gmlp_ln_g": jnp.stack([grads[l]["g_lg"][0] for l in range(DEPTH)]),
               "gmlp_ln_b": jnp.stack([grads[l]["g_lb"][0] for l in range(DEPTH)]),
               "gmlp_w_s": jnp.stack([grads[l]["ws"] for l in range(DEPTH)]),
               "gmlp_b_s": jnp.stack([grads[l]["bs"][:, :, 0] for l in range(DEPTH)])}
    pk_small = pack_small(small_g)
    n_small = pk_small.shape[0]
    extras = [dmods[l] for l in range(DEPTH)] + [c]
    n_extra = _round_up(-(-sum(int(np.prod(e.shape)) for e in extras) // D), 8)
    gathered = ag_all(jnp.concatenate([pk_small, _rows(extras, n_extra, F32)], axis=0))
    g_small = unpack_small(sum_slots(gathered[:, 0:n_small], 8, "sum_small"), small_g)
    ext = gathered[:, n_small:].reshape(8, -1)
    n_dmod = DEPTH * Bl * N_MOD * D
    dmod_all = ext[:, 0:n_dmod].reshape(8, DEPTH, Bl, N_MOD * D)
    c_all = ext[:, n_dmod:n_dmod + Bl * D].reshape(8 * Bl, D)
    g_ada_w, g_ada_b = [], []
    ncol = N_MOD * D // N_CHIPS
    for l in range(DEPTH):
        dm = dmod_all[:, l].reshape(8 * Bl, N_MOD * D)
        g_ada_w.append(ada_grad(c_all, lax.dynamic_slice_in_dim(dm, chip * ncol, ncol, axis=1)))
        g_ada_b.append(sum_slots(dm.reshape(8 * Bl, N_MOD, D), 8 * Bl, "sum_ada_b").reshape(N_MOD * D))
    g_ada_w, g_ada_b = jnp.stack(g_ada_w), jnp.stack(g_ada_b)

    G = jnp.stack([pack_grads(grads, k) for k in range(N_CHIPS)])
    Rh = G.shape[1] // 2
    keep = lax.dynamic_slice_in_dim(G, core * Rh, Rh, axis=1)
    give = lax.dynamic_slice_in_dim(G, (1 - core) * Rh, Rh, axis=1)
    chip_sum = add2(keep, sibling_swap(give), "add_sibling")
    reduced = sibling_join(sum_slots(chip_exchange(chip_sum), N_CHIPS, "sum_chips"))
    g_shard = unpack_shard(reduced, w)

    out = {}
    d_pk, m_pk, v_pk = adamw(pack_shard(w), reduced, pack_shard(m), pack_shard(v), "adamw_shard")
    for key, pk in (("delta", d_pk), ("new_m", m_pk), ("new_v", v_pk)):
        out[key] = unpack_shard(pk, w)
    out["grad"] = dict(g_shard)
    d_pk, m_pk, v_pk = adamw(pack_small(w), pack_small(g_small), pack_small(m), pack_small(v), "adamw_small")
    for key, pk in (("delta", d_pk), ("new_m", m_pk), ("new_v", v_pk)):
        out[key].update(unpack_small(pk, w))
    out["grad"].update(g_small)
    for n, g_n, cols in (("ada_w", g_ada_w, ncol), ("ada_b", g_ada_b, D)):
        res = adamw(w[n].reshape(-1, cols), g_n.reshape(-1, cols), m[n].reshape(-1, cols), v[n].reshape(-1, cols), "adamw_" + n)
        for key, r in zip(("delta", "new_m", "new_v"), res):
            out[key][n] = r.reshape(w[n].shape)
        out["grad"][n] = g_n.reshape(w[n].shape)
    outs = [loss, dx.reshape(Bl, S, D)]
    for key in ("grad", "delta", "new_m", "new_v"):
        outs += [out[key][n] for n in WEIGHTS]
    return tuple(outs)
```

```python
import functools

import jax
import jax.numpy as jnp
import numpy as np
from jax import lax
from jax.experimental import pallas as pl
from jax.experimental.pallas import tpu as pltpu

F32, BF16 = jnp.float32, jnp.bfloat16
MESH = pl.DeviceIdType.MESH

N_CHIPS = 4
D = 1024
DEPTH = 2
FF = 2816
N_MOD = 9
GW = 256
HEADS = 4
HD = 64
HP = 128
A_CHUNK = 16
LB_FLOOR = 1e-30
B_Q_LORA, B_KV_LORA, B_NOPE, B_ROPE = 256, 128, 64, 32
ROPE_THETA = 10000.0
D_CHUNK = 128
MIX_COLS = 2724
ALPHA = (2 * DEPTH) ** 0.25
LN_EPS = 1e-5
RMS_EPS = 1e-6
ADAM_LR, ADAM_B1, ADAM_B2, ADAM_EPS, ADAM_WD, ADAM_STEP = 0.001, 0.9, 0.999, 1e-08, 0.01, 10

NP = 3840
NP_TILE = 1920
C_A, C_B, C_CQ, C_CK, C_CV, C_D, C_CF = 0, 1024, 1536, 2048, 2560, 3072, 3584
NCAT = 1536
O_A, O_B, O_C, O_D = 0, 256, 768, 1280

VMEM_BIG = 48 << 20


def _cparams(vmem=None):
    return pltpu.CompilerParams(vmem_limit_bytes=vmem) if vmem else pltpu.CompilerParams()


def _sds(shape, dtype):
    return jax.ShapeDtypeStruct(tuple(shape), dtype)


@jax.custom_vjp
def _mm(a, w):
    return jnp.dot(a.astype(BF16), w.astype(BF16), preferred_element_type=F32)


def _mm_f(a, w):
    return _mm(a, w), (a, w)


def _mm_b(res, g):
    a, w = res
    gb = g.astype(BF16)
    da = lax.dot_general(gb, w.astype(BF16), (((1,), (1,)), ((), ())), preferred_element_type=F32)
    dw = lax.dot_general(a.astype(BF16), gb, (((0,), (0,)), ((), ())), preferred_element_type=F32)
    return da.astype(a.dtype), dw.astype(w.dtype)


_mm.defvjp(_mm_f, _mm_b)


@jax.custom_vjp
def _mm_nt(a, b):
    return lax.dot_general(a.astype(BF16), b.astype(BF16), (((1,), (1,)), ((), ())), preferred_element_type=F32)


def _mm_nt_f(a, b):
    return _mm_nt(a, b), (a, b)


def _mm_nt_b(res, g):
    a, b = res
    gb = g.astype(BF16)
    da = jnp.dot(gb, b.astype(BF16), preferred_element_type=F32)
    db = lax.dot_general(gb, a.astype(BF16), (((0,), (0,)), ((), ())), preferred_element_type=F32)
    return da.astype(a.dtype), db.astype(b.dtype)


_mm_nt.defvjp(_mm_nt_f, _mm_nt_b)


@jax.custom_vjp
def _mm_tn(a, b):
    return lax.dot_general(a.astype(BF16), b.astype(BF16), (((0,), (0,)), ((), ())), preferred_element_type=F32)


def _mm_tn_f(a, b):
    return _mm_tn(a, b), (a, b)


def _mm_tn_b(res, g):
    a, b = res
    gb = g.astype(BF16)
    da = lax.dot_general(b.astype(BF16), gb, (((1,), (1,)), ((), ())), preferred_element_type=F32)
    db = jnp.dot(a.astype(BF16), gb, preferred_element_type=F32)
    return da.astype(a.dtype), db.astype(b.dtype)


_mm_tn.defvjp(_mm_tn_f, _mm_tn_b)


def _mm_hi(a, w):
    return jnp.dot(a, w, precision=lax.Precision.HIGHEST, preferred_element_type=F32)


def _iota(shape, dim):
    return lax.broadcasted_iota(jnp.int32, shape, dim)


def _head_sum_mats():
    e = (_iota((GW, HP), 0) // HD == _iota((GW, HP), 1)).astype(F32)
    et = (_iota((HP, GW), 1) // HD == _iota((HP, GW), 0)).astype(F32)
    return e, et


def _modulate(x, mod_ref, sh, sc):
    return x * (1.0 + mod_ref[sc:sc + 1, :]) + mod_ref[sh:sh + 1, :]


def _ln_res(x, y, gate, lg, lb, gs):
    r = ALPHA * x + gs * (1.0 + gate) * y
    mu = jnp.mean(r, axis=-1, keepdims=True)
    var = jnp.mean(jnp.square(r - mu), axis=-1, keepdims=True)
    return (r - mu) * lax.rsqrt(var + LN_EPS) * lg + lb


def _rms(x, g):
    return x * lax.rsqrt(jnp.mean(x * x, axis=-1, keepdims=True) + RMS_EPS) * g


def _tile(n, pref):
    return pref if n % pref == 0 else n


def mod_fwd(c8, w, l, b):
    tn = w.shape[3]
    n = N_CHIPS * tn

    def body(c_ref, w_ref, b_ref, o_ref):
        h = jax.nn.silu(c_ref[...]).astype(BF16)
        o_ref[...] = jnp.dot(h, w_ref[...], preferred_element_type=F32) + b_ref[...]

    return pl.pallas_call(
        body, name="mod_fwd", grid=(N_CHIPS,),
        in_specs=[pl.BlockSpec((8, D), lambda j: (0, 0)), pl.BlockSpec((None, None, D, tn), lambda j: (l, j, 0, 0)),
                  pl.BlockSpec((1, tn), lambda j: (0, j))],
        out_specs=pl.BlockSpec((8, tn), lambda j: (0, j)), out_shape=_sds((8, n), F32),
        compiler_params=_cparams(VMEM_BIG))(c8, w, b)


def ffn_in_fwd(x, mod, w_in, l, sh, sc, S):
    T = x.shape[0]
    tm, tn = _tile(S, 512), FF // 2
    tpb, nj = S // tm, 2

    def body(x_ref, mod_ref, wg_ref, wu_ref, zg_ref, zu_ref, act_ref, h_ref):
        @pl.when(pl.program_id(1) == 0)
        def _():
            h_ref[...] = _modulate(x_ref[...], mod_ref, sh, sc).astype(BF16)
        g = jnp.dot(h_ref[...], wg_ref[...], preferred_element_type=F32)
        u = jnp.dot(h_ref[...], wu_ref[...], preferred_element_type=F32)
        zg_ref[...] = g
        zu_ref[...] = u
        act_ref[...] = (jax.nn.silu(g) * u).astype(BF16)

    return pl.pallas_call(
        body, name="ffn_in_fwd", grid=(T // tm, nj),
        in_specs=[pl.BlockSpec((tm, D), lambda i, j: (i, 0)),
                  pl.BlockSpec((None, N_MOD, D), lambda i, j: (i // tpb, 0, 0)),
                  pl.BlockSpec((None, None, D, tn), lambda i, j: (l, j, 0, 0)),
                  pl.BlockSpec((None, None, D, tn), lambda i, j: (l, j + nj, 0, 0))],
        out_specs=[pl.BlockSpec((tm, tn), lambda i, j: (i, j))] * 3,
        out_shape=[_sds((T, FF), F32), _sds((T, FF), F32), _sds((T, FF), BF16)],
        scratch_shapes=[pltpu.VMEM((tm, D), BF16)],
        compiler_params=_cparams(VMEM_BIG))(x, mod, w_in, w_in)


def mix_in_fwd(x, mod, w, sh, sc, S):
    T = x.shape[0]
    n = w.shape[1]
    tm, tn = _tile(S, 512), NP_TILE
    tpb = S // tm

    def body(x_ref, mod_ref, w_ref, o_ref, h_ref):
        @pl.when(pl.program_id(1) == 0)
        def _():
            h_ref[...] = _modulate(x_ref[...], mod_ref, sh, sc).astype(BF16)
        o_ref[...] = jnp.dot(h_ref[...], w_ref[...], preferred_element_type=F32)

    return pl.pallas_call(
        body, name="mix_in_fwd", grid=(T // tm, n // tn),
        in_specs=[pl.BlockSpec((tm, D), lambda i, j: (i, 0)),
                  pl.BlockSpec((None, N_MOD, D), lambda i, j: (i // tpb, 0, 0)),
                  pl.BlockSpec((D, tn), lambda i, j: (0, j))],
        out_specs=pl.BlockSpec((tm, tn), lambda i, j: (i, j)), out_shape=_sds((T, n), F32),
        scratch_shapes=[pltpu.VMEM((tm, D), BF16)],
        compiler_params=_cparams(VMEM_BIG))(x, mod, w)


def out_ln_fwd(act, w_out, x, mod, lg, lb, gate, gs, S, l=None):
    T, K = act.shape
    tm = _tile(S, 512)
    tpb = S // tm

    def body(a_ref, w_ref, x_ref, mod_ref, lg_ref, lb_ref, y_ref, xn_ref):
        y = jnp.dot(a_ref[...], w_ref[...].reshape(K, D), preferred_element_type=F32)
        y_ref[...] = y
        xn_ref[...] = _ln_res(x_ref[...], y, mod_ref[gate:gate + 1, :], lg_ref[...], lb_ref[...], gs)

    if l is None:
        w_spec = pl.BlockSpec((K, D), lambda i: (0, 0))
    else:
        w_spec = pl.BlockSpec((None, N_CHIPS, K // N_CHIPS, D), lambda i: (l, 0, 0, 0))
    return pl.pallas_call(
        body, name="out_ln_fwd", grid=(T // tm,),
        in_specs=[pl.BlockSpec((tm, K), lambda i: (i, 0)), w_spec,
                  pl.BlockSpec((tm, D), lambda i: (i, 0)),
                  pl.BlockSpec((None, N_MOD, D), lambda i: (i // tpb, 0, 0)),
                  pl.BlockSpec((1, D), lambda i: (0, 0)), pl.BlockSpec((1, D), lambda i: (0, 0))],
        out_specs=[pl.BlockSpec((tm, D), lambda i: (i, 0))] * 2,
        out_shape=[_sds((T, D), F32), _sds((T, D), F32)],
        compiler_params=_cparams(VMEM_BIG))(act, w_out, x, mod, lg, lb)


def ln_res_bwd(dxn, x, y, mod, lg, lb, gate, gs, S):
    T = x.shape[0]
    B = T // S
    tm = _tile(S, 512)
    tpb = S // tm

    def body(d_ref, x_ref, y_ref, mod_ref, lg_ref, lb_ref, dx_ref, dy_ref, dg_ref, dlg_ref, dlb_ref):
        i = pl.program_id(0)
        f = functools.partial(_ln_res, gs=gs)
        _, vjp = jax.vjp(f, x_ref[...], y_ref[...], mod_ref[gate:gate + 1, :], lg_ref[...], lb_ref[...])
        dx, dy, dg, dlg, dlb = vjp(d_ref[...])
        dx_ref[...] = dx
        dy_ref[...] = dy.astype(BF16)

        @pl.when(i % tpb == 0)
        def _():
            dg_ref[...] = jnp.zeros_like(dg_ref)

        @pl.when(i == 0)
        def _():
            dlg_ref[...] = jnp.zeros_like(dlg_ref)
            dlb_ref[...] = jnp.zeros_like(dlb_ref)

        dg_ref[...] += dg
        dlg_ref[...] += dlg
        dlb_ref[...] += dlb

    tok = pl.BlockSpec((tm, D), lambda i: (i, 0))
    vec = pl.BlockSpec((1, D), lambda i: (0, 0))
    return pl.pallas_call(
        body, name="ln_res_bwd", grid=(T // tm,),
        in_specs=[tok, tok, tok, pl.BlockSpec((None, N_MOD, D), lambda i: (i // tpb, 0, 0)), vec, vec],
        out_specs=[tok, tok, pl.BlockSpec((None, 1, D), lambda i: (i // tpb, 0, 0)), vec, vec],
        out_shape=[_sds((T, D), F32), _sds((T, D), BF16), _sds((B, 1, D), F32), _sds((1, D), F32), _sds((1, D), F32)],
        compiler_params=_cparams(VMEM_BIG))(dxn, x, y, mod, lg, lb)


def swiglu_bwd(dy, w_out, l, zg, zu, S):
    T = dy.shape[0]
    tm, tn = _tile(S, 512), FF // 2

    def body(dy_ref, w_ref, zg_ref, zu_ref, dg_ref, du_ref):
        da = lax.dot_general(dy_ref[...], w_ref[...].reshape(tn, D), (((1,), (1,)), ((), ())), preferred_element_type=F32)
        g, u = zg_ref[...], zu_ref[...]
        sg = jax.nn.sigmoid(g)
        dg_ref[...] = (da * u * (sg * (1.0 + g * (1.0 - sg)))).astype(BF16)
        du_ref[...] = (da * (g * sg)).astype(BF16)

    zt = pl.BlockSpec((tm, tn), lambda i, j: (i, j))
    return pl.pallas_call(
        body, name="swiglu_bwd", grid=(T // tm, FF // tn),
        in_specs=[pl.BlockSpec((tm, D), lambda i, j: (i, 0)),
                  pl.BlockSpec((None, 2, FF // N_CHIPS, D), lambda i, j: (l, j, 0, 0)), zt, zt],
        out_specs=[zt, zt], out_shape=[_sds((T, FF), BF16), _sds((T, FF), BF16)],
        compiler_params=_cparams(VMEM_BIG))(dy, w_out, zg, zu)


def nt_plain(dy, w):
    T = dy.shape[0]
    K = w.shape[0]
    tm = _tile(T, 512)

    def body(dy_ref, w_ref, o_ref):
        o_ref[...] = lax.dot_general(dy_ref[...], w_ref[...], (((1,), (1,)), ((), ())), preferred_element_type=F32)

    return pl.pallas_call(
        body, name="nt_plain", grid=(T // tm,),
        in_specs=[pl.BlockSpec((tm, D), lambda i: (i, 0)), pl.BlockSpec((K, D), lambda i: (0, 0))],
        out_specs=pl.BlockSpec((tm, K), lambda i: (i, 0)), out_shape=_sds((T, K), F32),
        compiler_params=_cparams(VMEM_BIG))(dy, w)


def tn_mm(a, b, tk):
    T, K = a.shape
    N = b.shape[1]
    tt = _tile(T, 512)

    def body(a_ref, b_ref, o_ref):
        @pl.when(pl.program_id(1) == 0)
        def _():
            o_ref[...] = jnp.zeros_like(o_ref)
        o_ref[...] += lax.dot_general(a_ref[...], b_ref[...], (((0,), (0,)), ((), ())), preferred_element_type=F32)

    return pl.pallas_call(
        body, name="tn_mm", grid=(K // tk, T // tt),
        in_specs=[pl.BlockSpec((tt, tk), lambda k, t: (t, k)), pl.BlockSpec((tt, N), lambda k, t: (t, 0))],
        out_specs=pl.BlockSpec((tk, N), lambda k, t: (k, 0)), out_shape=_sds((K, N), F32),
        compiler_params=_cparams(VMEM_BIG))(a, b)


def tn_mm_mod(x, mod, b, sh, sc, S, tn):
    T = x.shape[0]
    N = b.shape[1]
    tt = _tile(S, 512)
    tpb = S // tt

    def body(x_ref, mod_ref, b_ref, o_ref):
        @pl.when(pl.program_id(1) == 0)
        def _():
            o_ref[...] = jnp.zeros_like(o_ref)
        h = _modulate(x_ref[...], mod_ref, sh, sc).astype(BF16)
        o_ref[...] += lax.dot_general(h, b_ref[...], (((0,), (0,)), ((), ())), preferred_element_type=F32)

    return pl.pallas_call(
        body, name="tn_mm_mod", grid=(N // tn, T // tt),
        in_specs=[pl.BlockSpec((tt, D), lambda j, t: (t, 0)),
                  pl.BlockSpec((None, N_MOD, D), lambda j, t: (t // tpb, 0, 0)),
                  pl.BlockSpec((tt, tn), lambda j, t: (t, j))],
        out_specs=pl.BlockSpec((D, tn), lambda j, t: (0, j)), out_shape=_sds((D, N), F32),
        compiler_params=_cparams(VMEM_BIG))(x, mod, b)


def tn_mm_mod_shards(x, mod, bg, bu, sh, sc, S):
    T = x.shape[0]
    tn = FF // 2
    tt = _tile(S, 512)
    tpb = S // tt

    def body(x_ref, mod_ref, bg_ref, bu_ref, o_ref):
        j = pl.program_id(0)

        @pl.when(pl.program_id(1) == 0)
        def _():
            o_ref[...] = jnp.zeros_like(o_ref)
        h = _modulate(x_ref[...], mod_ref, sh, sc).astype(BF16)

        @pl.when(j < 2)
        def _():
            o_ref[...] += lax.dot_general(h, bg_ref[...], (((0,), (0,)), ((), ())), preferred_element_type=F32)

        @pl.when(j >= 2)
        def _():
            o_ref[...] += lax.dot_general(h, bu_ref[...], (((0,), (0,)), ((), ())), preferred_element_type=F32)

    return pl.pallas_call(
        body, name="tn_mm_mod_shards", grid=(N_CHIPS, T // tt),
        in_specs=[pl.BlockSpec((tt, D), lambda j, t: (t, 0)),
                  pl.BlockSpec((None, N_MOD, D), lambda j, t: (t // tpb, 0, 0)),
                  pl.BlockSpec((tt, tn), lambda j, t: (t, jnp.minimum(j, 1))),
                  pl.BlockSpec((tt, tn), lambda j, t: (t, jnp.maximum(j - 2, 0)))],
        out_specs=pl.BlockSpec((None, D, tn), lambda j, t: (j, 0, 0)), out_shape=_sds((N_CHIPS, D, tn), F32),
        compiler_params=_cparams(VMEM_BIG))(x, mod, bg, bu)


def nt_mod_bwd(ds, w, offs, x, mod, dres, sc, S, tk, l=None):
    T = x.shape[0]
    B = T // S
    tm = _tile(S, 512)
    tpb = S // tm
    Kd = ds[0].shape[1]
    nk = Kd // tk
    n_in = len(ds)

    def body(*refs):
        d_refs, w_refs = refs[:n_in], refs[n_in:2 * n_in]
        x_ref, mod_ref, r_ref, dx_ref, dsh_ref, dsc_ref, acc = refs[2 * n_in:]
        i, k = pl.program_id(0), pl.program_id(1)

        @pl.when(k == 0)
        def _():
            acc[...] = jnp.zeros_like(acc)

        for d_ref, w_ref in zip(d_refs, w_refs):
            acc[...] += lax.dot_general(d_ref[...], w_ref[...], (((1,), (1,)), ((), ())), preferred_element_type=F32)

        @pl.when(k == nk - 1)
        def _():
            dh = acc[...]
            dx_ref[...] = dh * (1.0 + mod_ref[sc:sc + 1, :]) + r_ref[...]

            @pl.when(i % tpb == 0)
            def _():
                dsh_ref[...] = jnp.zeros_like(dsh_ref)
                dsc_ref[...] = jnp.zeros_like(dsc_ref)

            dsh_ref[...] += jnp.sum(dh, axis=0, keepdims=True)
            dsc_ref[...] += jnp.sum(dh * x_ref[...], axis=0, keepdims=True)

    tok = pl.BlockSpec((tm, D), lambda i, k: (i, 0))
    vec = pl.BlockSpec((None, 1, D), lambda i, k: (i // tpb, 0, 0))
    in_specs = [pl.BlockSpec((tm, tk), lambda i, k: (i, k)) for _ in ds]
    if l is None:
        in_specs += [pl.BlockSpec((D, tk), functools.partial(lambda i, k, o: (0, k + o), o=off // tk)) for off in offs]
    else:
        in_specs += [pl.BlockSpec((None, None, D, tk), functools.partial(lambda i, k, o: (l, k + o, 0, 0), o=off)) for off in offs]
    in_specs += [tok, pl.BlockSpec((None, N_MOD, D), lambda i, k: (i // tpb, 0, 0)), tok]
    return pl.pallas_call(
        body, name="nt_mod_bwd", grid=(T // tm, nk), in_specs=in_specs,
        out_specs=[tok, vec, vec],
        out_shape=[_sds((T, D), F32), _sds((B, 1, D), F32), _sds((B, 1, D), F32)],
        scratch_shapes=[pltpu.VMEM((tm, D), F32)],
        compiler_params=_cparams(VMEM_BIG))(*ds, *([w] * n_in), x, mod, dres)


def loss_head(y, tgt):
    T = y.shape[0]
    tm = _tile(T, 512)

    def body(y_ref, t_ref, l_ref, d_ref):
        @pl.when(pl.program_id(0) == 0)
        def _():
            l_ref[...] = jnp.zeros_like(l_ref)
        e = y_ref[...] - t_ref[...]
        d_ref[...] = e * (1.0 / D)
        l_ref[...] += 0.5 * jnp.sum(jnp.sum(e * e, axis=1, keepdims=True) * (1.0 / D))

    tok = pl.BlockSpec((tm, D), lambda i: (i, 0))
    return pl.pallas_call(
        body, name="loss_head", grid=(T // tm,), in_specs=[tok, tok],
        out_specs=[pl.BlockSpec((8, 128), lambda i: (0, 0)), tok],
        out_shape=[_sds((8, 128), F32), _sds((T, D), F32)],
        compiler_params=_cparams(VMEM_BIG))(y, tgt)


def _hgrn_block(q, fz, inp, go, st, lb, ng, blk):
    nc = blk // A_CHUNK
    lb_eff = jnp.maximum(lb, LB_FLOOR)
    log_f = jnp.logaddexp(jnp.log(lb_eff), jnp.log1p(-lb) + jax.nn.log_sigmoid(fz))
    k = (1.0 - lb) * jax.nn.sigmoid(-fz) - (lb_eff - lb)
    qf = jax.nn.silu(q)
    same_chunk = _iota((blk, blk), 0) // A_CHUNK == _iota((blk, blk), 1) // A_CHUNK
    tril = (same_chunk & (_iota((blk, blk), 1) <= _iota((blk, blk), 0))).astype(F32)
    G = _mm_hi(tril, log_f)
    e_mat, et_mat = _head_sum_mats()
    G4, q4, k4, v4 = (z.reshape(nc, A_CHUNK, GW) for z in (G, qf, k, inp))
    shp = (nc, A_CHUNK, A_CHUNK, GW)
    causal = _iota(shp, 2) <= _iota(shp, 1)
    decay = jnp.exp(jnp.where(causal, G4[:, :, None, :] - G4[:, None, :, :], -jnp.inf))
    prod = q4[:, :, None, :] * k4[:, None, :, :] * decay
    scores = _mm(prod.reshape(nc * A_CHUNK * A_CHUNK, GW), e_mat.astype(BF16))
    spread = _mm(scores, et_mat.astype(BF16)).reshape(shp)
    o_intra = jnp.sum(spread * v4[:, None, :, :], axis=2).reshape(blk, GW)
    head_diag = (_iota((GW, GW), 0) // HD == _iota((GW, GW), 1) // HD).astype(F32)
    g_last = [jnp.sum(log_f[c * A_CHUNK:(c + 1) * A_CHUNK], axis=0, keepdims=True) for c in range(nc)]
    g_last_b = jnp.concatenate([jnp.broadcast_to(g, (A_CHUNK, GW)) for g in g_last], axis=0)
    q_dec = qf * jnp.exp(G)
    k_end = k * jnp.exp(g_last_b - G)
    outs = []
    for c in range(nc):
        rows = slice(c * A_CHUNK, (c + 1) * A_CHUNK)
        outs.append(_mm_nt(q_dec[rows], st))
        st = st * jnp.exp(g_last[c]) + _mm_tn(inp[rows], k_end[rows]) * head_diag
    o = o_intra + jnp.concatenate(outs, axis=0)
    ms = _mm_hi(o * o, e_mat) * (1.0 / HD)
    o = o * _mm_hi(lax.rsqrt(ms + RMS_EPS), et_mat) * ng
    return o * jax.nn.silu(go), st


HGRN_BLK = 128


def hgrn_fwd(proj, lb, ng, S):
    T = proj.shape[0]
    B = T // S
    blk = min(HGRN_BLK, S)
    nb = S // blk

    def body(p_ref, lb_ref, ng_ref, o_ref, st_out_ref, st_ref):
        @pl.when(pl.program_id(1) == 0)
        def _():
            st_ref[...] = jnp.zeros_like(st_ref)
        st_out_ref[...] = st_ref[...]
        p = p_ref[...]
        o, st = _hgrn_block(p[:, 0:GW], p[:, GW:2 * GW], p[:, 2 * GW:3 * GW], p[:, 3 * GW:4 * GW],
                            st_ref[...], lb_ref[...], ng_ref[...], blk)
        o_ref[...] = o.astype(BF16)
        st_ref[...] = st

    vec = pl.BlockSpec((1, GW), lambda b, j: (0, 0))
    return pl.pallas_call(
        body, name="hgrn_fwd", grid=(B, nb),
        in_specs=[pl.BlockSpec((blk, 4 * GW), lambda b, j: (b * nb + j, C_A // (4 * GW))), vec, vec],
        out_specs=[pl.BlockSpec((blk, GW), lambda b, j: (b * nb + j, 0)),
                   pl.BlockSpec((None, GW, GW), lambda b, j: (b * nb + j, 0, 0))],
        out_shape=[_sds((T, GW), BF16), _sds((B * nb, GW, GW), F32)],
        scratch_shapes=[pltpu.VMEM((GW, GW), F32)],
        compiler_params=_cparams(VMEM_BIG))(proj, lb, ng)


def hgrn_bwd(proj, states, dcat, lb, ng, S):
    T = proj.shape[0]
    B = T // S
    blk = min(HGRN_BLK, S)
    nb = S // blk

    def body(p_ref, st_in_ref, do_ref, lb_ref, ng_ref, dp_ref, dlb_ref, dng_ref, dst_ref):
        b, j = pl.program_id(0), pl.program_id(1)

        @pl.when(j == 0)
        def _():
            dst_ref[...] = jnp.zeros_like(dst_ref)

        @pl.when((b == 0) & (j == 0))
        def _():
            dlb_ref[...] = jnp.zeros_like(dlb_ref)
            dng_ref[...] = jnp.zeros_like(dng_ref)

        p = p_ref[...]
        f = functools.partial(_hgrn_block, blk=blk)
        _, vjp = jax.vjp(f, p[:, 0:GW], p[:, GW:2 * GW], p[:, 2 * GW:3 * GW], p[:, 3 * GW:4 * GW],
                         st_in_ref[...], lb_ref[...], ng_ref[...])
        dq, df, di, dg, dst, dlb, dng = vjp((do_ref[...], dst_ref[...]))
        dp_ref[...] = jnp.concatenate([dq, df, di, dg], axis=1).astype(BF16)
        dst_ref[...] = dst
        dlb_ref[...] += dlb
        dng_ref[...] += dng

    def rev(b, j):
        return b * nb + (nb - 1 - j)

    vec = pl.BlockSpec((1, GW), lambda b, j: (0, 0))
    return pl.pallas_call(
        body, name="hgrn_bwd", grid=(B, nb),
        in_specs=[pl.BlockSpec((blk, 4 * GW), lambda b, j: (rev(b, j), C_A // (4 * GW))),
                  pl.BlockSpec((None, GW, GW), lambda b, j: (rev(b, j), 0, 0)),
                  pl.BlockSpec((blk, GW), lambda b, j: (rev(b, j), O_A // GW)), vec, vec],
        out_specs=[pl.BlockSpec((blk, 4 * GW), lambda b, j: (rev(b, j), 0)), vec, vec],
        out_shape=[_sds((T, 4 * GW), BF16), _sds((1, GW), F32), _sds((1, GW), F32)],
        scratch_shapes=[pltpu.VMEM((GW, GW), F32)],
        compiler_params=_cparams(VMEM_BIG))(proj, states, dcat, lb, ng)


ATT_TQ = 256


def _attn_block(q, k, v, cum, qpos0, scale, use_cum):
    s = _mm_nt(q, k) * scale
    if use_cum:
        s = s - cum
    qpos = qpos0 + _iota(s.shape, 0)
    s = jnp.where(_iota(s.shape, 1) <= qpos, s, -jnp.inf)
    e = jnp.exp(s - jnp.max(s, axis=-1, keepdims=True))
    p = e / jnp.sum(e, axis=-1, keepdims=True)
    return _mm(p, v)


def attn_fwd(qa, qo, ka, ko, va, vo, cum, scale, S):
    T = qa.shape[0]
    B = T // S
    tq = min(ATT_TQ, S)
    nq = S // tq
    use_cum = cum is not None

    def body(*refs):
        if use_cum:
            q_ref, k_ref, v_ref, c_ref, o_ref = refs
            crow = c_ref[pl.ds(pl.program_id(1), 1), :]
        else:
            q_ref, k_ref, v_ref, o_ref = refs
            crow = None
        o = _attn_block(q_ref[...], k_ref[...], v_ref[...], crow, pl.program_id(2) * tq, scale, use_cum)
        o_ref[...] = o.astype(BF16)

    in_specs = [pl.BlockSpec((tq, HP), lambda b, h, i: (b * nq + i, qo + h)),
                pl.BlockSpec((S, HP), lambda b, h, i: (b, ko + h)),
                pl.BlockSpec((S, HP), lambda b, h, i: (b, vo + h))]
    args = [qa, ka, va]
    if use_cum:
        in_specs.append(pl.BlockSpec((None, 8, S), lambda b, h, i: (b, 0, 0)))
        args.append(cum)
    return pl.pallas_call(
        body, name="attn_fwd", grid=(B, HEADS, nq), in_specs=in_specs,
        out_specs=pl.BlockSpec((tq, HP), lambda b, h, i: (b * nq + i, h)),
        out_shape=_sds((T, HEADS * HP), BF16),
        compiler_params=_cparams(VMEM_BIG))(*args)


def attn_bwd(qa, qo, ka, ko, va, vo, cum, dcat, do_off, scale, S, out_dtype):
    T = qa.shape[0]
    B = T // S
    tq = min(ATT_TQ, S)
    nq = S // tq
    use_cum = cum is not None

    def body(*refs):
        if use_cum:
            q_ref, k_ref, v_ref, do_ref, c_ref, dq_ref, dk_ref, dv_ref, dc_ref, dk_acc, dv_acc = refs
            crow = c_ref[pl.ds(pl.program_id(1), 1), :]
        else:
            q_ref, k_ref, v_ref, do_ref, dq_ref, dk_ref, dv_ref, dk_acc, dv_acc = refs
            crow = jnp.zeros((1, S), F32)
        i = pl.program_id(2)

        @pl.when(i == 0)
        def _():
            dk_acc[...] = jnp.zeros_like(dk_acc)
            dv_acc[...] = jnp.zeros_like(dv_acc)
            if use_cum:
                dc_ref[...] = jnp.zeros_like(dc_ref)

        f = functools.partial(_attn_block, qpos0=i * tq, scale=scale, use_cum=use_cum)
        _, vjp = jax.vjp(f, q_ref[...], k_ref[...], v_ref[...], crow)
        dq, dk, dv, dc = vjp(do_ref[...])
        dq_ref[...] = dq.astype(out_dtype)
        dk_acc[...] += dk
        dv_acc[...] += dv
        if use_cum:
            dc_ref[...] += dc

        @pl.when(i == nq - 1)
        def _():
            dk_ref[...] = dk_acc[...].astype(out_dtype)
            dv_ref[...] = dv_acc[...].astype(out_dtype)

    qspec = pl.BlockSpec((tq, HP), lambda b, h, i: (b * nq + i, qo + h))
    in_specs = [qspec, pl.BlockSpec((S, HP), lambda b, h, i: (b, ko + h)),
                pl.BlockSpec((S, HP), lambda b, h, i: (b, vo + h)),
                pl.BlockSpec((tq, HP), lambda b, h, i: (b * nq + i, do_off + h))]
    args = [qa, ka, va, dcat]
    kv_out = pl.BlockSpec((S, HP), lambda b, h, i: (b, h))
    out_specs = [pl.BlockSpec((tq, HP), lambda b, h, i: (b * nq + i, h)), kv_out, kv_out]
    out_shape = [_sds((T, HEADS * HP), out_dtype)] * 3
    if use_cum:
        in_specs.append(pl.BlockSpec((None, 8, S), lambda b, h, i: (b, 0, 0)))
        args.append(cum)
        out_specs.append(pl.BlockSpec((None, 1, S), lambda b, h, i: (b * HEADS + h, 0, 0)))
        out_shape.append(_sds((B * HEADS, 1, S), F32))
    return pl.pallas_call(
        body, name="attn_bwd", grid=(B, HEADS, nq), in_specs=in_specs, out_specs=out_specs, out_shape=out_shape,
        scratch_shapes=[pltpu.VMEM((S, HP), F32), pltpu.VMEM((S, HP), F32)],
        compiler_params=_cparams(VMEM_BIG))(*args)


def _tri(n, upper):
    r, c = _iota((n, n), 0), _iota((n, n), 1)
    return ((r <= c) if upper else (r >= c)).astype(F32)


def fox_gate_fwd(proj, bcol, S):
    T = proj.shape[0]
    B = T // S
    ts = _tile(S, 512)
    nt = S // ts

    def body(p_ref, b_ref, o_ref, carry):
        @pl.when(pl.program_id(1) == 0)
        def _():
            carry[...] = jnp.zeros_like(carry)
        cf = jnp.transpose(p_ref[...])[0:8, :]
        lf = jax.nn.log_sigmoid(cf + b_ref[...])
        cum = _mm_hi(lf, _tri(ts, True)) + carry[...]
        o_ref[...] = cum
        carry[...] += jnp.sum(lf, axis=1, keepdims=True)

    return pl.pallas_call(
        body, name="fox_gate_fwd", grid=(B, nt),
        in_specs=[pl.BlockSpec((ts, HP), lambda b, j: (b * nt + j, C_CF // HP)), pl.BlockSpec((8, 1), lambda b, j: (0, 0))],
        out_specs=pl.BlockSpec((None, 8, ts), lambda b, j: (b, 0, j)), out_shape=_sds((B, 8, S), F32),
        scratch_shapes=[pltpu.VMEM((8, 1), F32)],
        compiler_params=_cparams(VMEM_BIG))(proj, bcol)


def fox_gate_bwd(proj, bcol, dcum, S):
    T = proj.shape[0]
    B = T // S
    ts = _tile(S, 512)
    nt = S // ts

    def body(p_ref, b_ref, dc_ref, dp_ref, db_ref, carry):
        b, j = pl.program_id(0), pl.program_id(1)

        @pl.when(j == 0)
        def _():
            carry[...] = jnp.zeros_like(carry)

        @pl.when((b == 0) & (j == 0))
        def _():
            db_ref[...] = jnp.zeros_like(db_ref)

        cf = jnp.transpose(p_ref[...])[0:8, :]
        dc = dc_ref[...]
        dlf = _mm_hi(dc, _tri(ts, False)) + carry[...]
        carry[...] += jnp.sum(dc, axis=1, keepdims=True)
        dcf = dlf * jax.nn.sigmoid(-(cf + b_ref[...]))
        db_ref[...] += jnp.sum(dcf, axis=1, keepdims=True)
        full = jnp.concatenate([dcf, jnp.zeros((HP - 8, ts), F32)], axis=0)
        dp_ref[...] = jnp.transpose(full).astype(BF16)

    def rev(b, j):
        return nt - 1 - j

    return pl.pallas_call(
        body, name="fox_gate_bwd", grid=(B, nt),
        in_specs=[pl.BlockSpec((ts, HP), lambda b, j: (b * nt + rev(b, j), C_CF // HP)),
                  pl.BlockSpec((8, 1), lambda b, j: (0, 0)),
                  pl.BlockSpec((None, 8, ts), lambda b, j: (b, 0, rev(b, j)))],
        out_specs=[pl.BlockSpec((ts, HP), lambda b, j: (b * nt + rev(b, j), 0)), pl.BlockSpec((8, 1), lambda b, j: (0, 0))],
        out_shape=[_sds((T, HP), BF16), _sds((8, 1), F32)],
        scratch_shapes=[pltpu.VMEM((8, 1), F32)],
        compiler_params=_cparams(VMEM_BIG))(proj, bcol, dcum)


def _mla_pre(blk, gq, gkv, wq, wkv, place, cos_q, sin_q, cs_k):
    nq = _rms(blk[:, 0:B_Q_LORA], gq)
    nkv = _rms(blk[:, B_Q_LORA:B_Q_LORA + B_KV_LORA], gkv)
    qq = _mm(nq, wq)
    q = qq[:, 0:HEADS * HP] * cos_q + qq[:, HEADS * HP:] * sin_q
    kv = _mm(nkv, wkv)
    k = kv[:, 0:HEADS * HP] + _mm(blk[:, B_Q_LORA + B_KV_LORA:] * cs_k, place)
    return q, k, kv[:, HEADS * HP:]


def mla_pre_fwd(proj, gq, gkv, wq, wkv, place, cos_q, sin_q, cs_k, S):
    T = proj.shape[0]
    tm = _tile(S, 512)
    tpb = S // tm
    W = HEADS * HP

    def body(p_ref, gq_ref, gkv_ref, wq_ref, wkv_ref, pl_ref, cq_ref, sq_ref, ck_ref, q_ref, k_ref, v_ref):
        q, k, v = _mla_pre(p_ref[...], gq_ref[...], gkv_ref[...], wq_ref[...], wkv_ref[...], pl_ref[...],
                           cq_ref[...], sq_ref[...], ck_ref[...])
        q_ref[...] = q
        k_ref[...] = k
        v_ref[...] = v

    def full(a):
        return pl.BlockSpec(a.shape, lambda i: (0,) * a.ndim)

    tok = pl.BlockSpec((tm, W), lambda i: (i, 0))
    return pl.pallas_call(
        body, name="mla_pre_fwd", grid=(T // tm,),
        in_specs=[pl.BlockSpec((tm, W), lambda i: (i, C_B // W)), full(gq), full(gkv), full(wq), full(wkv), full(place),
                  pl.BlockSpec((tm, W), lambda i: (i % tpb, 0)), pl.BlockSpec((tm, W), lambda i: (i % tpb, 0)),
                  pl.BlockSpec((tm, HP), lambda i: (i % tpb, 0))],
        out_specs=[tok] * 3, out_shape=[_sds((T, W), F32)] * 3,
        compiler_params=_cparams(VMEM_BIG))(proj, gq, gkv, wq, wkv, place, cos_q, sin_q, cs_k)


def mla_pre_bwd(proj, gq, gkv, wq, wkv, place, cos_q, sin_q, cs_k, dq, dk, dv, S):
    T = proj.shape[0]
    tm = _tile(S, 512)
    tpb = S // tm
    W = HEADS * HP

    def body(p_ref, gq_ref, gkv_ref, wq_ref, wkv_ref, pl_ref, cq_ref, sq_ref, ck_ref, dq_ref, dk_ref, dv_ref,
             dp_ref, dgq_ref, dgkv_ref, dwq_ref, dwkv_ref):
        @pl.when(pl.program_id(0) == 0)
        def _():
            for r in (dgq_ref, dgkv_ref, dwq_ref, dwkv_ref):
                r[...] = jnp.zeros_like(r)

        f = functools.partial(_mla_pre, place=pl_ref[...], cos_q=cq_ref[...], sin_q=sq_ref[...], cs_k=ck_ref[...])
        _, vjp = jax.vjp(f, p_ref[...], gq_ref[...], gkv_ref[...], wq_ref[...], wkv_ref[...])
        dp, dgq, dgkv, dwq, dwkv = vjp((dq_ref[...], dk_ref[...], dv_ref[...]))
        dp_ref[...] = dp.astype(BF16)
        dgq_ref[...] += dgq
        dgkv_ref[...] += dgkv
        dwq_ref[...] += dwq
        dwkv_ref[...] += dwkv

    def full(a):
        return pl.BlockSpec(a.shape, lambda i: (0,) * a.ndim)

    tok = pl.BlockSpec((tm, W), lambda i: (i, 0))
    return pl.pallas_call(
        body, name="mla_pre_bwd", grid=(T // tm,),
        in_specs=[pl.BlockSpec((tm, W), lambda i: (i, C_B // W)), full(gq), full(gkv), full(wq), full(wkv), full(place),
                  pl.BlockSpec((tm, W), lambda i: (i % tpb, 0)), pl.BlockSpec((tm, W), lambda i: (i % tpb, 0)),
                  pl.BlockSpec((tm, HP), lambda i: (i % tpb, 0)), tok, tok, tok],
        out_specs=[tok, full(gq), full(gkv), full(wq), full(wkv)],
        out_shape=[_sds((T, W), BF16), _sds(gq.shape, F32), _sds(gkv.shape, F32), _sds(wq.shape, F32), _sds(wkv.shape, F32)],
        compiler_params=_cparams(VMEM_BIG))(proj, gq, gkv, wq, wkv, place, cos_q, sin_q, cs_k, dq, dk, dv)


def _gmlp_block(blk, lg, lb, ws, bs):
    u = jax.nn.gelu(blk[:, 0:GW])
    v = jax.nn.gelu(blk[:, GW:2 * GW])
    mu = jnp.mean(v, axis=-1, keepdims=True)
    var = jnp.mean(jnp.square(v - mu), axis=-1, keepdims=True)
    vn = (v - mu) * lax.rsqrt(var + LN_EPS) * lg + lb
    causal = _iota((D_CHUNK, D_CHUNK), 1) <= _iota((D_CHUNK, D_CHUNK), 0)
    group = _iota((1, GW), 1) // HD
    mixed = jnp.zeros((D_CHUNK, GW), F32)
    for g in range(HEADS):
        part = _mm(jnp.where(causal, ws[g], 0.0), vn) + bs[g]
        mixed = mixed + jnp.where(group == g, part, 0.0)
    return u * mixed


def gmlp_fwd(proj, lg, lb, ws, bs):
    T = proj.shape[0]

    def body(p_ref, lg_ref, lb_ref, ws_ref, bs_ref, o_ref):
        o_ref[...] = _gmlp_block(p_ref[...], lg_ref[...], lb_ref[...], ws_ref[...], bs_ref[...]).astype(BF16)

    def full(a):
        return pl.BlockSpec(a.shape, lambda i: (0,) * a.ndim)

    return pl.pallas_call(
        body, name="gmlp_fwd", grid=(T // D_CHUNK,),
        in_specs=[pl.BlockSpec((D_CHUNK, 2 * GW), lambda i: (i, C_D // (2 * GW))), full(lg), full(lb), full(ws), full(bs)],
        out_specs=pl.BlockSpec((D_CHUNK, GW), lambda i: (i, 0)), out_shape=_sds((T, GW), BF16),
        compiler_params=_cparams(VMEM_BIG))(proj, lg, lb, ws, bs)


def gmlp_bwd(proj, lg, lb, ws, bs, dcat):
    T = proj.shape[0]

    def body(p_ref, lg_ref, lb_ref, ws_ref, bs_ref, do_ref, dp_ref, dlg_ref, dlb_ref, dws_ref, dbs_ref):
        @pl.when(pl.program_id(0) == 0)
        def _():
            for r in (dlg_ref, dlb_ref, dws_ref, dbs_ref):
                r[...] = jnp.zeros_like(r)

        _, vjp = jax.vjp(_gmlp_block, p_ref[...], lg_ref[...], lb_ref[...], ws_ref[...], bs_ref[...])
        dp, dlg, dlb, dws, dbs = vjp(do_ref[...])
        dp_ref[...] = dp.astype(BF16)
        dlg_ref[...] += dlg
        dlb_ref[...] += dlb
        dws_ref[...] += dws
        dbs_ref[...] += dbs

    def full(a):
        return pl.BlockSpec(a.shape, lambda i: (0,) * a.ndim)

    return pl.pallas_call(
        body, name="gmlp_bwd", grid=(T // D_CHUNK,),
        in_specs=[pl.BlockSpec((D_CHUNK, 2 * GW), lambda i: (i, C_D // (2 * GW))), full(lg), full(lb), full(ws), full(bs),
                  pl.BlockSpec((D_CHUNK, GW), lambda i: (i, O_D // GW))],
        out_specs=[pl.BlockSpec((D_CHUNK, 2 * GW), lambda i: (i, 0)), full(lg), full(lb), full(ws), full(bs)],
        out_shape=[_sds((T, 2 * GW), BF16), _sds(lg.shape, F32), _sds(lb.shape, F32), _sds(ws.shape, F32), _sds(bs.shape, F32)],
        compiler_params=_cparams(VMEM_BIG))(proj, lg, lb, ws, bs, dcat)


def _lb_all(logits):
    m = jnp.max(logits, axis=0, keepdims=True)
    e = jnp.exp(logits - m)
    sm = e / jnp.sum(e, axis=0, keepdims=True)
    return jnp.concatenate([sm[0:1] - sm[0:1], (sm[0:1] + sm[1:2]) - sm[0:1]], axis=0)


def lb_fwd(logits):
    def body(l_ref, o_ref):
        o_ref[...] = _lb_all(l_ref[...])

    return pl.pallas_call(body, name="lb_fwd", out_shape=_sds(logits.shape, F32))(logits)


def lb_bwd(logits, dlb):
    def body(l_ref, d_ref, o_ref):
        _, vjp = jax.vjp(_lb_all, l_ref[...])
        o_ref[...] = vjp(d_ref[...])[0]

    return pl.pallas_call(body, name="lb_bwd", out_shape=_sds(logits.shape, F32))(logits, dlb)


def ada_grad(c_all, dmod_cols):
    N = dmod_cols.shape[1]
    tn = _tile(N, 1152)

    def body(c_ref, d_ref, o_ref):
        h = jax.nn.silu(c_ref[...]).astype(BF16)
        o_ref[...] = lax.dot_general(h, d_ref[...].astype(BF16), (((0,), (0,)), ((), ())), preferred_element_type=F32)

    nb = c_all.shape[0]
    return pl.pallas_call(
        body, name="ada_grad", grid=(N // tn,),
        in_specs=[pl.BlockSpec((nb, D), lambda j: (0, 0)), pl.BlockSpec((nb, tn), lambda j: (0, j))],
        out_specs=pl.BlockSpec((D, tn), lambda j: (0, j)), out_shape=_sds((D, N), F32),
        compiler_params=_cparams(VMEM_BIG))(c_all, dmod_cols)


def sum_slots(a, n, name):
    _, R, C = a.shape
    tr = _row_tile(R, C, n)

    def body(a_ref, o_ref):
        acc = a_ref[0]
        for k in range(1, n):
            acc = acc + a_ref[k]
        o_ref[...] = acc

    return pl.pallas_call(
        body, name=name, grid=(R // tr,),
        in_specs=[pl.BlockSpec((n, tr, C), lambda i: (0, i, 0))],
        out_specs=pl.BlockSpec((tr, C), lambda i: (i, 0)), out_shape=_sds((R, C), F32),
        compiler_params=_cparams(VMEM_BIG))(a)


def add2(a, b, name):
    shp = a.shape
    C = shp[-1]
    a2, b2 = a.reshape(-1, C), b.reshape(-1, C)
    R = a2.shape[0]
    tr = _row_tile(R, C)

    def body(a_ref, b_ref, o_ref):
        o_ref[...] = a_ref[...] + b_ref[...]

    spec = pl.BlockSpec((tr, C), lambda i: (i, 0))
    return pl.pallas_call(body, name=name, grid=(R // tr,), in_specs=[spec, spec], out_specs=spec,
                          out_shape=_sds((R, C), F32), compiler_params=_cparams(VMEM_BIG))(a2, b2).reshape(shp)


def _row_tile(R, C=D, n=1):
    limit = max(8, (1 << 18) // (C * n))
    for t in range(limit - limit % 8, 7, -8):
        if R % t == 0:
            return t
    return R


def adamw(w, g, m, v, name):
    R, C = w.shape
    tr = _row_tile(R, C)
    c1 = 1.0 - ADAM_B1 ** ADAM_STEP
    c2 = 1.0 - ADAM_B2 ** ADAM_STEP

    def body(w_ref, g_ref, m_ref, v_ref, d_ref, nm_ref, nv_ref):
        g_ = g_ref[...]
        nm = ADAM_B1 * m_ref[...] + (1.0 - ADAM_B1) * g_
        nv = ADAM_B2 * v_ref[...] + (1.0 - ADAM_B2) * jnp.square(g_)
        d_ref[...] = -ADAM_LR * ((nm / c1) / (jnp.sqrt(nv / c2) + ADAM_EPS) + ADAM_WD * w_ref[...])
        nm_ref[...] = nm
        nv_ref[...] = nv

    spec = pl.BlockSpec((tr, C), lambda i: (i, 0))
    return pl.pallas_call(body, name=name, grid=(R // tr,), in_specs=[spec] * 4, out_specs=[spec] * 3,
                          out_shape=[_sds((R, C), F32)] * 3, compiler_params=_cparams(VMEM_BIG))(w, g, m, v)


def _rot_cols(w):
    return jnp.concatenate([-w[:, 16:32], w[:, 0:16]], axis=1)


def _fold_rot(d):
    return jnp.concatenate([d[:, 16:32], -d[:, 0:16]], axis=1)


def _pad_heads(w, off, axis):
    parts = []
    for h in range(HEADS):
        piece = lax.slice_in_dim(w, off + HD * h, off + HD * (h + 1), axis=axis)
        parts += [piece, jnp.zeros_like(piece)]
    return parts


def _unpad_heads(d, off, axis):
    return [lax.slice_in_dim(d, off + HP * h, off + HP * h + HD, axis=axis) for h in range(HEADS)]


def mix_in_ext(w):
    z = lambda n: jnp.zeros((w.shape[0], n), w.dtype)
    kr = w[:, 1408:1440]
    cols = [w[:, 0:1408], kr, _rot_cols(kr), z(64)]
    cols += _pad_heads(w, 1440, 1) + _pad_heads(w, 1696, 1) + _pad_heads(w, 1952, 1)
    cols += [w[:, 2212:2724], w[:, 2208:2212], z(NP - C_CF - HEADS)]
    return jnp.concatenate(cols, axis=1)


def mix_in_unext(d):
    kr = d[:, 1408:1440] + _fold_rot(d[:, 1440:1472])
    cols = [d[:, 0:1408], kr] + _unpad_heads(d, C_CQ, 1) + _unpad_heads(d, C_CK, 1) + _unpad_heads(d, C_CV, 1)
    cols += [d[:, C_CF:C_CF + HEADS], d[:, C_D:C_D + 2 * GW]]
    return jnp.concatenate(cols, axis=1)


def mix_out_ext(w):
    return jnp.concatenate([w[0:GW]] + _pad_heads(w, GW, 0) + _pad_heads(w, 2 * GW, 0) + [w[3 * GW:4 * GW]], axis=0)


def mix_out_unext(d):
    return jnp.concatenate([d[0:GW]] + _unpad_heads(d, O_B, 0) + _unpad_heads(d, O_C, 0) + [d[O_D:O_D + GW]], axis=0)


def uq_ext(w):
    z = lambda n: jnp.zeros((w.shape[0], n), w.dtype)
    a, b = [], []
    for h in range(HEADS):
        o = (B_NOPE + B_ROPE) * h
        a += [w[:, o:o + B_NOPE + B_ROPE], z(32)]
        b += [z(B_NOPE), _rot_cols(w[:, o + B_NOPE:o + B_NOPE + B_ROPE]), z(32)]
    return jnp.concatenate(a + b, axis=1)


def uq_unext(d):
    cols = []
    for h in range(HEADS):
        o = HP * h
        cols += [d[:, o:o + B_NOPE], d[:, o + B_NOPE:o + B_NOPE + B_ROPE]
                 + _fold_rot(d[:, HEADS * HP + o + B_NOPE:HEADS * HP + o + B_NOPE + B_ROPE])]
    return jnp.concatenate(cols, axis=1)


def ukv_ext(w):
    z = jnp.zeros((w.shape[0], HD), w.dtype)
    k, v = [], []
    for h in range(HEADS):
        k += [w[:, 2 * HD * h:2 * HD * h + HD], z]
        v += [w[:, 2 * HD * h + HD:2 * HD * (h + 1)], z]
    return jnp.concatenate(k + v, axis=1)


def ukv_unext(d):
    cols = []
    for h in range(HEADS):
        cols += [d[:, HP * h:HP * h + HD], d[:, HEADS * HP + HP * h:HEADS * HP + HP * h + HD]]
    return jnp.concatenate(cols, axis=1)


def rope_tables(S):
    half = B_ROPE // 2
    inv_freq = ROPE_THETA ** (-jnp.arange(half, dtype=F32) / half)
    ang = jnp.arange(S).astype(F32)[:, None] * inv_freq[None, :]
    cos = jnp.tile(jnp.cos(ang), (1, 2))
    sin = jnp.tile(jnp.sin(ang), (1, 2))
    one, zero = jnp.ones((S, B_NOPE), F32), jnp.zeros((S, B_NOPE), F32)
    z32 = jnp.zeros((S, 32), F32)
    cos_q = jnp.tile(jnp.concatenate([one, cos, z32], axis=1), (1, HEADS))
    sin_q = jnp.tile(jnp.concatenate([zero, sin, z32], axis=1), (1, HEADS))
    cs_k = jnp.concatenate([cos, sin, zero], axis=1)
    place = np.zeros((HP, HEADS * HP), np.float32)
    for h in range(HEADS):
        for j in range(B_ROPE):
            place[j, h * HP + B_NOPE + j] = 1.0
            place[B_ROPE + j, h * HP + B_NOPE + j] = 1.0
    return cos_q, sin_q, cs_k, jnp.asarray(place, BF16)


def layer_fwd(x, mod, p, l, tabs, S):
    cos_q, sin_q, cs_k, place = tabs
    zg1, zu1, act1 = ffn_in_fwd(x, mod, p["ffn1_in"], l, 0, 1, S)
    y1, x1 = out_ln_fwd(act1, p["ffn1_out"], x, mod, p["ln_g"][0:1], p["ln_b"][0:1], 2, 0.5, S, l)
    proj = mix_in_fwd(x1, mod, p["mix_in"], 3, 4, S)
    o_a, states = hgrn_fwd(proj, p["lb"], p["ng"], S)
    q_b, k_b, v_b = mla_pre_fwd(proj, p["gq"], p["gkv"], p["wq"], p["wkv"], place, cos_q, sin_q, cs_k, S)
    o_b = attn_fwd(q_b, 0, k_b, 0, v_b, 0, None, (B_NOPE + B_ROPE) ** -0.5, S)
    cum = fox_gate_fwd(proj, p["bcol"], S)
    o_c = attn_fwd(proj, C_CQ // HP, proj, C_CK // HP, proj, C_CV // HP, cum, HD ** -0.5, S)
    o_d = gmlp_fwd(proj, p["g_lg"], p["g_lb"], p["ws"], p["bs"])
    cat = jnp.concatenate([o_a, o_b, o_c, o_d], axis=1)
    y2, x2 = out_ln_fwd(cat, p["mix_out"], x1, mod, p["ln_g"][1:2], p["ln_b"][1:2], 5, 1.0, S)
    zg3, zu3, act3 = ffn_in_fwd(x2, mod, p["ffn2_in"], l, 6, 7, S)
    y3, x3 = out_ln_fwd(act3, p["ffn2_out"], x2, mod, p["ln_g"][2:3], p["ln_b"][2:3], 8, 0.5, S, l)
    saved = dict(x=x, zg1=zg1, zu1=zu1, act1=act1, y1=y1, x1=x1, proj=proj, states=states, q_b=q_b, k_b=k_b, v_b=v_b,
                 cum=cum, cat=cat, y2=y2, x2=x2, zg3=zg3, zu3=zu3, act3=act3, y3=y3)
    return x3, saved


def _ffn_bwd(dxn, x_in, y, zg, zu, act, mod, w_in, w_out, l, lg, lb, idx, S):
    sh, sc, gate = idx
    dres, dy, dgate, dlg, dlb = ln_res_bwd(dxn, x_in, y, mod, lg, lb, gate, 0.5, S)
    dzg, dzu = swiglu_bwd(dy, w_out, l, zg, zu, S)
    dw_out = tn_mm(act, dy, FF // 2).reshape(N_CHIPS, FF // N_CHIPS, D)
    dw_in = tn_mm_mod_shards(x_in, mod, dzg, dzu, sh, sc, S)
    dx, dsh, dsc = nt_mod_bwd([dzg, dzu], w_in, [0, 2], x_in, mod, dres, sc, S, FF // 2, l)
    return dx, dw_in, dw_out, dlg, dlb, {sh: dsh, sc: dsc, gate: dgate}


def layer_bwd(dx3, mod, p, l, sv, tabs, S):
    cos_q, sin_q, cs_k, place = tabs
    g = {}
    dm = {}
    dx2, g["ffn2_in"], g["ffn2_out"], dlg2, dlb2, d = _ffn_bwd(
        dx3, sv["x2"], sv["y3"], sv["zg3"], sv["zu3"], sv["act3"], mod, p["ffn2_in"], p["ffn2_out"], l,
        p["ln_g"][2:3], p["ln_b"][2:3], (6, 7, 8), S)
    dm.update(d)
    dres, dy2, dm[5], dlg1, dlb1 = ln_res_bwd(dx2, sv["x1"], sv["y2"], mod, p["ln_g"][1:2], p["ln_b"][1:2], 5, 1.0, S)
    dcat = nt_plain(dy2, p["mix_out"])
    g["mix_out"] = tn_mm(sv["cat"], dy2, 768)
    proj = sv["proj"]
    d_a, g["lb"], g["ng"] = hgrn_bwd(proj, sv["states"], dcat, p["lb"], p["ng"], S)
    dq_c, dk_c, dv_c, dcum = attn_bwd(proj, C_CQ // HP, proj, C_CK // HP, proj, C_CV // HP, sv["cum"], dcat,
                                      O_C // HP, HD ** -0.5, S, BF16)
    B = proj.shape[0] // S
    dcum = jnp.concatenate([dcum.reshape(B, HEADS, S), jnp.zeros((B, 8 - HEADS, S), F32)], axis=1)
    d_cf, g["bcol"] = fox_gate_bwd(proj, p["bcol"], dcum, S)
    dq_b, dk_b, dv_b = attn_bwd(sv["q_b"], 0, sv["k_b"], 0, sv["v_b"], 0, None, dcat, O_B // HP,
                                (B_NOPE + B_ROPE) ** -0.5, S, F32)
    d_b, g["gq"], g["gkv"], g["wq"], g["wkv"] = mla_pre_bwd(
        proj, p["gq"], p["gkv"], p["wq"], p["wkv"], place, cos_q, sin_q, cs_k, dq_b, dk_b, dv_b, S)
    d_d, g["g_lg"], g["g_lb"], g["ws"], g["bs"] = gmlp_bwd(proj, p["g_lg"], p["g_lb"], p["ws"], p["bs"], dcat)
    dproj = jnp.concatenate([d_a, d_b, dq_c, dk_c, dv_c, d_d, d_cf, jnp.zeros_like(d_cf)], axis=1)
    g["mix_in"] = tn_mm_mod(sv["x1"], mod, dproj, 3, 4, S, NP_TILE)
    dx1, dm[3], dm[4] = nt_mod_bwd([dproj], p["mix_in"], [0], sv["x1"], mod, dres, 4, S, NP_TILE)
    dx0, g["ffn1_in"], g["ffn1_out"], dlg0, dlb0, d = _ffn_bwd(
        dx1, sv["x"], sv["y1"], sv["zg1"], sv["zu1"], sv["act1"], mod, p["ffn1_in"], p["ffn1_out"], l,
        p["ln_g"][0:1], p["ln_b"][0:1], (0, 1, 2), S)
    dm.update(d)
    g["ln_g"] = jnp.concatenate([dlg0, dlg1, dlg2], axis=0)
    g["ln_b"] = jnp.concatenate([dlb0, dlb1, dlb2], axis=0)
    dmod = jnp.concatenate([dm[i] for i in range(N_MOD)], axis=1)
    return dx0, dmod, g


def local_step(x, c8, tgt, ada_w, ada_b, layers, lb_logits, S):
    B = x.shape[0] // S
    tabs = rope_tables(S)
    lb_all = lb_fwd(lb_logits)
    mods, saved = [], []
    h = x
    for l in range(DEPTH):
        mod = mod_fwd(c8, ada_w, l, ada_b[l])[0:B].reshape(B, N_MOD, D)
        p = dict(layers[l], lb=lb_all[l:l + 1])
        h, sv = layer_fwd(h, mod, p, l, tabs, S)
        mods.append(mod)
        saved.append(sv)
    loss_tile, dh = loss_head(h, tgt)
    grads, dmods, dlb = [None] * DEPTH, [None] * DEPTH, [None] * DEPTH
    for l in reversed(range(DEPTH)):
        p = dict(layers[l], lb=lb_all[l:l + 1])
        dh, dmods[l], grads[l] = layer_bwd(dh, mods[l], p, l, saved[l], tabs, S)
        dlb[l] = grads[l].pop("lb")
    d_logits = lb_bwd(lb_logits, jnp.concatenate(dlb, axis=0))
    return loss_tile, dh, dmods, grads, d_logits


ANY = pl.BlockSpec(memory_space=pl.ANY)


def _place():
    x, y, c = lax.axis_index("x"), lax.axis_index("y"), lax.axis_index("c")
    chips = [(1 - x, y), (x, 1 - y), (1 - x, 1 - y)]
    return x, y, c, chips


def _rcopy(src, dst, sems, k, to):
    send_sems, recv_sems = sems
    return pltpu.make_async_remote_copy(src_ref=src, dst_ref=dst, send_sem=send_sems.at[k], recv_sem=recv_sems.at[k],
                                        device_id=to, device_id_type=MESH)


def _dma_sems(n_remote, n_local):
    return [pltpu.SemaphoreType.DMA((n_remote,)), pltpu.SemaphoreType.DMA((n_remote,)), pltpu.SemaphoreType.DMA((n_local,))]


def ag_shards(arrs):
    n = len(arrs)
    rh = [a.shape[1] // 2 for a in arrs]

    def body(*refs):
        srcs, outs = refs[:n], refs[n:2 * n]
        send_sems, recv_sems, loc_sems = refs[2 * n:]
        x, y, c, chips = _place()
        sems = (send_sems, recv_sems)
        me = 2 * x + y
        sibling = (x, y, 1 - c)

        def part(i, k, hc):
            return outs[i].at[:, k, pl.ds(hc * rh[i], rh[i]), :]

        mine = [pltpu.make_async_copy(srcs[i], outs[i].at[:, me], loc_sems.at[i]) for i in range(n)]
        for cp in mine:
            cp.start()
        started = []
        for j, (px, py) in enumerate(chips):
            for i in range(n):
                cp = _rcopy(srcs[i].at[:, pl.ds(c * rh[i], rh[i]), :], part(i, me, c), sems, 6 * i + j, (px, py, c))
                cp.start()
                started.append(cp)
        for j, (px, py) in enumerate(chips):
            k = 2 * px + py
            for i in range(n):
                _rcopy(part(i, k, c), part(i, k, c), sems, 6 * i + j, (px, py, c)).wait_recv()
                cp = _rcopy(part(i, k, c), part(i, k, c), sems, 6 * i + 3 + j, sibling)
                cp.start()
                started.append(cp)
        for j, (px, py) in enumerate(chips):
            k = 2 * px + py
            for i in range(n):
                _rcopy(part(i, k, 1 - c), part(i, k, 1 - c), sems, 6 * i + 3 + j, sibling).wait_recv()
        for cp in started:
            cp.wait_send()
        for cp in mine:
            cp.wait()

    return pl.pallas_call(
        body, name="ag_shards", out_shape=[_sds((a.shape[0], N_CHIPS) + a.shape[1:], a.dtype) for a in arrs],
        in_specs=[ANY] * n, out_specs=[ANY] * n, scratch_shapes=_dma_sems(6 * n, n))(*arrs)


def sibling_swap(arrs):
    n = len(arrs)
    rh = [a.shape[1] // 2 for a in arrs]

    def body(*refs):
        srcs, outs = refs[:n], refs[n:2 * n]
        send_sems, recv_sems = refs[2 * n:]
        x, y, c, _ = _place()
        cps = [_rcopy(srcs[i].at[:, pl.ds((1 - c) * rh[i], rh[i]), :], outs[i], (send_sems, recv_sems), i, (x, y, 1 - c))
               for i in range(n)]
        for cp in cps:
            cp.start()
        for cp in cps:
            cp.wait()

    return pl.pallas_call(
        body, name="sibling_swap", out_shape=[_sds((N_CHIPS, r, a.shape[2]), a.dtype) for a, r in zip(arrs, rh)],
        in_specs=[ANY] * n, out_specs=[ANY] * n, scratch_shapes=_dma_sems(n, 1)[:2])(*arrs)


def chip_exchange(hs):
    n = len(hs)

    def body(*refs):
        srcs, outs = refs[:n], refs[n:2 * n]
        send_sems, recv_sems, loc_sems = refs[2 * n:]
        x, y, c, chips = _place()
        sems = (send_sems, recv_sems)
        me = 2 * x + y
        mine = [pltpu.make_async_copy(srcs[i].at[me], outs[i].at[me], loc_sems.at[i]) for i in range(n)]
        for cp in mine:
            cp.start()
        sends = []
        for j, (px, py) in enumerate(chips):
            for i in range(n):
                cp = _rcopy(srcs[i].at[2 * px + py], outs[i].at[me], sems, 3 * i + j, (px, py, c))
                cp.start()
                sends.append(cp)
        for j, (px, py) in enumerate(chips):
            for i in range(n):
                _rcopy(srcs[i].at[2 * px + py], outs[i].at[2 * px + py], sems, 3 * i + j, (px, py, c)).wait_recv()
        for cp in sends:
            cp.wait_send()
        for cp in mine:
            cp.wait()

    return pl.pallas_call(
        body, name="chip_exchange", out_shape=[_sds(h.shape, h.dtype) for h in hs],
        in_specs=[ANY] * n, out_specs=[ANY] * n, scratch_shapes=_dma_sems(3 * n, n))(*hs)


def sibling_join(reds):
    L, n = len(reds), len(reds[0])
    rh = [r.shape[0] for r in reds[0]]
    flat = [r for layer in reds for r in layer]

    def body(*refs):
        srcs, outs = refs[:L * n], refs[L * n:L * n + n]
        send_sems, recv_sems, loc_sems = refs[L * n + n:]
        x, y, c, _ = _place()
        sems = (send_sems, recv_sems)
        mine, sends = [], []
        for l in range(L):
            for i in range(n):
                k = l * n + i
                dst = outs[i].at[l, pl.ds(c * rh[i], rh[i]), :]
                mine.append(pltpu.make_async_copy(srcs[k], dst, loc_sems.at[k]))
                sends.append(_rcopy(srcs[k], dst, sems, k, (x, y, 1 - c)))
        for cp in mine + sends:
            cp.start()
        for l in range(L):
            for i in range(n):
                k = l * n + i
                _rcopy(srcs[k], outs[i].at[l, pl.ds((1 - c) * rh[i], rh[i]), :], sems, k, (x, y, 1 - c)).wait_recv()
        for cp in sends:
            cp.wait_send()
        for cp in mine:
            cp.wait()

    return pl.pallas_call(
        body, name="sibling_join", out_shape=[_sds((L, 2 * r.shape[0], r.shape[1]), r.dtype) for r in reds[0]],
        in_specs=[ANY] * (L * n), out_specs=[ANY] * n, scratch_shapes=_dma_sems(L * n, L * n))(*flat)


def ag_all(blk):
    M, C = blk.shape

    def body(x_ref, out_ref, send_sems, recv_sems, loc_sem):
        x, y, c, chips = _place()
        sems = (send_sems, recv_sems)
        me, sibling = (x, y, c), (x, y, 1 - c)

        def slot(px, py, pc):
            return out_ref.at[4 * px + 2 * py + pc]

        mine = pltpu.make_async_copy(x_ref, slot(*me), loc_sem)
        mine.start()
        first = [_rcopy(x_ref, slot(*me), sems, 0, sibling)]
        first += [_rcopy(x_ref, slot(*me), sems, 1 + j, (*chip, c)) for j, chip in enumerate(chips)]
        for cp in first:
            cp.start()
        passed = [_rcopy(slot(*chip, c), slot(*chip, c), sems, 4 + j, sibling) for j, chip in enumerate(chips)]
        for j, chip in enumerate(chips):
            _rcopy(slot(*chip, c), slot(*chip, c), sems, 1 + j, me).wait_recv()
            passed[j].start()
        _rcopy(slot(*sibling), slot(*sibling), sems, 0, me).wait_recv()
        for j, chip in enumerate(chips):
            _rcopy(slot(*chip, 1 - c), slot(*chip, 1 - c), sems, 4 + j, me).wait_recv()
        for cp in first + passed:
            cp.wait_send()
        mine.wait()

    return pl.pallas_call(
        body, name="ag_all", out_shape=_sds((8, M, C), blk.dtype),
        in_specs=[pl.BlockSpec(memory_space=pltpu.VMEM)], out_specs=pl.BlockSpec(memory_space=pltpu.VMEM),
        scratch_shapes=[pltpu.SemaphoreType.DMA((7,)), pltpu.SemaphoreType.DMA((7,)), pltpu.SemaphoreType.DMA(())],
        compiler_params=_cparams(VMEM_BIG))(blk)


WEIGHTS = ["ada_w", "ada_b", "ln_g", "ln_b", "ffn1_w_in", "ffn1_w_out", "ffn2_w_in", "ffn2_w_out", "mix_w_in", "mix_w_out",
           "hgrn_lb_logits", "hgrn_norm_g", "mla_q_norm_g", "mla_kv_norm_g", "mla_w_uq", "mla_w_ukv", "fox_b_f",
           "gmlp_ln_g", "gmlp_ln_b", "gmlp_w_s", "gmlp_b_s"]
SHARDED = {"ffn1_w_in": 1, "ffn1_w_out": 0, "ffn2_w_in": 1, "ffn2_w_out": 0, "mix_w_in": 1, "mix_w_out": 0,
           "mla_w_uq": 1, "mla_w_ukv": 1}
SMALL = ["hgrn_lb_logits", "hgrn_norm_g", "mla_q_norm_g", "mla_kv_norm_g", "fox_b_f", "gmlp_ln_g", "gmlp_ln_b",
         "gmlp_w_s", "gmlp_b_s", "ln_g", "ln_b"]
GATHERED = ["ada_w", "ffn1_w_in", "ffn1_w_out", "ffn2_w_in", "ffn2_w_out", "mix_w_in", "mix_w_out", "mla_w_uq", "mla_w_ukv"]
REDUCED = GATHERED[1:]


def _col_shards(a):
    cols = a.shape[1] // N_CHIPS
    return jnp.stack([a[:, k * cols:(k + 1) * cols] for k in range(N_CHIPS)])


def add_kept_half(a, got, core, name):
    _, R, C = a.shape
    rh = R // 2
    tr = _row_tile(rh, C)
    nr = rh // tr

    def body(core_ref, a_ref, b_ref, o_ref):
        o_ref[...] = a_ref[...] + b_ref[...]

    half = pl.BlockSpec((None, tr, C), lambda k, r, core_ref: (k, r, 0))
    grid_spec = pltpu.PrefetchScalarGridSpec(
        num_scalar_prefetch=1, grid=(N_CHIPS, nr),
        in_specs=[pl.BlockSpec((None, tr, C), lambda k, r, core_ref: (k, core_ref[0] * nr + r, 0)), half],
        out_specs=half)
    return pl.pallas_call(body, name=name, grid_spec=grid_spec, out_shape=_sds((N_CHIPS, rh, C), F32),
                          compiler_params=_cparams(VMEM_BIG))(core.reshape(1).astype(jnp.int32), a, got)


def _rows(parts, n_rows, dtype):
    flat = jnp.concatenate([p.reshape(-1) for p in parts])
    pad = n_rows * D - flat.shape[0]
    return jnp.concatenate([flat, jnp.zeros((pad,), dtype)]).reshape(n_rows, D)


def _take(flat, shapes):
    out, o = [], 0
    for shp in shapes:
        n = int(np.prod(shp))
        out.append(flat[o:o + n].reshape(shp))
        o += n
    return out


def _round_up(n, m):
    return -(-n // m) * m


def pack_shard(w):
    parts = [w[n][l] for l in range(DEPTH) for n in SHARDED] + [w[n][l] for l in range(DEPTH) for n in ("ln_g", "ln_b")]
    n = sum(int(np.prod(p.shape)) for p in parts)
    return _rows(parts, _round_up(-(-n // D), 16), F32)


def unpack_shard(pk, like):
    shapes = [like[n].shape[1:] for l in range(DEPTH) for n in SHARDED] + [like[n].shape[1:] for l in range(DEPTH) for n in ("ln_g", "ln_b")]
    pieces = _take(pk.reshape(-1), shapes)
    names = [n for l in range(DEPTH) for n in SHARDED] + [n for l in range(DEPTH) for n in ("ln_g", "ln_b")]
    out = {}
    for n in list(SHARDED) + ["ln_g", "ln_b"]:
        out[n] = jnp.stack([p for p, m in zip(pieces, names) if m == n])
    return out


def pack_small(w):
    parts = [w[n][l] for l in range(DEPTH) for n in SMALL]
    n = sum(int(np.prod(p.shape)) for p in parts)
    return _rows(parts, _round_up(-(-n // D), 8), F32)


def unpack_small(pk, like):
    shapes = [like[n].shape[1:] for l in range(DEPTH) for n in SMALL]
    pieces = _take(pk.reshape(-1), shapes)
    names = [n for l in range(DEPTH) for n in SMALL]
    return {n: jnp.stack([p for p, m in zip(pieces, names) if m == n]) for n in SMALL}


def pack_gather(w):
    parts = [w[n][l].astype(BF16) for l in range(DEPTH) for n in ["ada_w"] + list(SHARDED)]
    ln = jnp.concatenate([w[n][l].reshape(-1) for l in range(DEPTH) for n in ("ln_g", "ln_b")])
    parts.append(lax.bitcast_convert_type(ln, BF16))
    n = sum(int(np.prod(p.shape)) for p in parts)
    return _rows(parts, _round_up(-(-n // D), 16), BF16)


def unpack_gather(g, w):
    names = ["ada_w"] + list(SHARDED)
    shapes = [w[n].shape[1:] for l in range(DEPTH) for n in names]
    n_ln = DEPTH * 2 * 3 * (D // N_CHIPS)
    flat = g.reshape(N_CHIPS, -1)
    per_chip = [_take(flat[k], shapes + [(n_ln, 2)]) for k in range(N_CHIPS)]
    layers = [dict() for _ in range(DEPTH)]
    i = 0
    for l in range(DEPTH):
        for n in names:
            axis = 1 if n == "ada_w" else SHARDED[n]
            layers[l][n] = jnp.concatenate([per_chip[k][i] for k in range(N_CHIPS)], axis=axis)
            i += 1
    ln = [lax.bitcast_convert_type(per_chip[k][i], F32).reshape(DEPTH, 2, 3, D // N_CHIPS) for k in range(N_CHIPS)]
    ln = jnp.concatenate(ln, axis=3)
    for l in range(DEPTH):
        layers[l]["ln_g"], layers[l]["ln_b"] = ln[l, 0], ln[l, 1]
    return layers


def pack_grads(grads, k):
    parts = []
    for l in range(DEPTH):
        g = grads[l]
        full = {"ffn1_w_out": g["ffn1_out"], "ffn2_w_out": g["ffn2_out"], "mix_w_in": mix_in_unext(g["mix_in"]),
                "mix_w_out": mix_out_unext(g["mix_out"]), "mla_w_uq": uq_unext(g["wq"]), "mla_w_ukv": ukv_unext(g["wkv"])}
        for n, axis in SHARDED.items():
            if n in ("ffn1_w_in", "ffn2_w_in"):
                half = g[n.replace("_w_in", "_in")][k // 2]
                parts.append(half[:, (k % 2) * (FF // 2):(k % 2 + 1) * (FF // 2)])
            else:
                sz = full[n].shape[axis] // N_CHIPS
                parts.append(lax.slice_in_dim(full[n], k * sz, (k + 1) * sz, axis=axis))
    for l in range(DEPTH):
        for n in ("ln_g", "ln_b"):
            parts.append(grads[l][n][:, k * (D // N_CHIPS):(k + 1) * (D // N_CHIPS)])
    n = sum(int(np.prod(p.shape)) for p in parts)
    return _rows(parts, _round_up(-(-n // D), 16), F32)


def kernel(x, c, ada_w, ada_b, ln_g, ln_b, ffn1_w_in, ffn1_w_out, ffn2_w_in, ffn2_w_out, mix_w_in, mix_w_out, hgrn_lb_logits, hgrn_norm_g, mla_q_norm_g, mla_kv_norm_g, mla_w_uq, mla_w_ukv, fox_b_f, gmlp_ln_g, gmlp_ln_b, gmlp_w_s, gmlp_b_s, loss_target, m_ada_w, m_ada_b, m_ln_g, m_ln_b, m_ffn1_w_in, m_ffn1_w_out, m_ffn2_w_in, m_ffn2_w_out, m_mix_w_in, m_mix_w_out, m_hgrn_lb_logits, m_hgrn_norm_g, m_mla_q_norm_g, m_mla_kv_norm_g, m_mla_w_uq, m_mla_w_ukv, m_fox_b_f, m_gmlp_ln_g, m_gmlp_ln_b, m_gmlp_w_s, m_gmlp_b_s, v_ada_w, v_ada_b, v_ln_g, v_ln_b, v_ffn1_w_in, v_ffn1_w_out, v_ffn2_w_in, v_ffn2_w_out, v_mix_w_in, v_mix_w_out, v_hgrn_lb_logits, v_hgrn_norm_g, v_mla_q_norm_g, v_mla_kv_norm_g, v_mla_w_uq, v_mla_w_ukv, v_fox_b_f, v_gmlp_ln_g, v_gmlp_ln_b, v_gmlp_w_s, v_gmlp_b_s):
    w = dict(zip(WEIGHTS, (ada_w, ada_b, ln_g, ln_b, ffn1_w_in, ffn1_w_out, ffn2_w_in, ffn2_w_out, mix_w_in, mix_w_out, hgrn_lb_logits, hgrn_norm_g, mla_q_norm_g, mla_kv_norm_g, mla_w_uq, mla_w_ukv, fox_b_f, gmlp_ln_g, gmlp_ln_b, gmlp_w_s, gmlp_b_s)))
    m = dict(zip(WEIGHTS, (m_ada_w, m_ada_b, m_ln_g, m_ln_b, m_ffn1_w_in, m_ffn1_w_out, m_ffn2_w_in, m_ffn2_w_out, m_mix_w_in, m_mix_w_out, m_hgrn_lb_logits, m_hgrn_norm_g, m_mla_q_norm_g, m_mla_kv_norm_g, m_mla_w_uq, m_mla_w_ukv, m_fox_b_f, m_gmlp_ln_g, m_gmlp_ln_b, m_gmlp_w_s, m_gmlp_b_s)))
    v = dict(zip(WEIGHTS, (v_ada_w, v_ada_b, v_ln_g, v_ln_b, v_ffn1_w_in, v_ffn1_w_out, v_ffn2_w_in, v_ffn2_w_out, v_mix_w_in, v_mix_w_out, v_hgrn_lb_logits, v_hgrn_norm_g, v_mla_q_norm_g, v_mla_kv_norm_g, v_mla_w_uq, v_mla_w_ukv, v_fox_b_f, v_gmlp_ln_g, v_gmlp_ln_b, v_gmlp_w_s, v_gmlp_b_s)))
    Bl, S, _ = x.shape
    T = Bl * S
    core = lax.axis_index("c")
    chip = 2 * lax.axis_index("x") + lax.axis_index("y")

    ln_pack = jnp.concatenate([ln_g, ln_b, jnp.zeros((DEPTH, 2, D // N_CHIPS), F32)], axis=1)
    gathered = ag_shards([w[n].astype(BF16) for n in GATHERED] + [ln_pack])
    full = dict(zip(GATHERED, gathered))
    ln_full = jnp.moveaxis(gathered[-1], 1, 2).reshape(DEPTH, 8, D)

    def cat_cols(a, l):
        return jnp.concatenate([a[l, k] for k in range(N_CHIPS)], axis=1)

    layers = []
    for l in range(DEPTH):
        layers.append(dict(
            ffn1_in=full["ffn1_w_in"], ffn1_out=full["ffn1_w_out"], ffn2_in=full["ffn2_w_in"], ffn2_out=full["ffn2_w_out"],
            mix_in=mix_in_ext(cat_cols(full["mix_w_in"], l)), mix_out=mix_out_ext(full["mix_w_out"][l].reshape(D, D)),
            wq=uq_ext(cat_cols(full["mla_w_uq"], l)).astype(F32), wkv=ukv_ext(cat_cols(full["mla_w_ukv"], l)).astype(F32),
            ln_g=ln_full[l, 0:3], ln_b=ln_full[l, 3:6], ng=hgrn_norm_g[l][None], gq=mla_q_norm_g[l][None],
            gkv=mla_kv_norm_g[l][None], bcol=jnp.concatenate([fox_b_f[l], jnp.zeros((8 - HEADS,), F32)])[:, None],
            g_lg=gmlp_ln_g[l][None], g_lb=gmlp_ln_b[l][None], ws=gmlp_w_s[l], bs=gmlp_b_s[l][:, :, None]))
    c8 = jnp.concatenate([c, jnp.zeros((8 - Bl, D), F32)], axis=0)
    loss_tile, dx, dmods, grads, d_logits = local_step(
        x.reshape(T, D), c8, loss_target.reshape(T, D), full["ada_w"], [ada_b[l][None] for l in range(DEPTH)],
        layers, hgrn_lb_logits, S)
    loss = lax.psum(loss_tile[0, 0], ("x", "y", "c"))

    small_g = {"hgrn_lb_logits": d_logits,
               "hgrn_norm_g": jnp.stack([grads[l]["ng"][0] for l in range(DEPTH)]),
               "mla_q_norm_g": jnp.stack([grads[l]["gq"][0] for l in range(DEPTH)]),
               "mla_kv_norm_g": jnp.stack([grads[l]["gkv"][0] for l in range(DEPTH)]),
               "fox_b_f": jnp.stack([grads[l]["bcol"][0:HEADS, 0] for l in range(DEPTH)]),
               "gmlp_ln_g": jnp.stack([grads[l]["g_lg"][0] for l in range(DEPTH)]),
               "gmlp_ln_b": jnp.stack([grads[l]["g_lb"][0] for l in range(DEPTH)]),
               "gmlp_w_s": jnp.stack([grads[l]["ws"] for l in range(DEPTH)]),
               "gmlp_b_s": jnp.stack([grads[l]["bs"][:, :, 0] for l in range(DEPTH)])}
    small_g["ln_g"] = jnp.stack([grads[l]["ln_g"] for l in range(DEPTH)])
    small_g["ln_b"] = jnp.stack([grads[l]["ln_b"] for l in range(DEPTH)])
    pk_small = pack_small(small_g)
    n_small = pk_small.shape[0]
    extras = [dmods[l] for l in range(DEPTH)] + [c]
    n_extra = _round_up(-(-sum(int(np.prod(e.shape)) for e in extras) // D), 8)
    gathered = ag_all(jnp.concatenate([pk_small, _rows(extras, n_extra, F32)], axis=0))
    g_small = unpack_small(sum_slots(gathered[:, 0:n_small], 8, "sum_small"), small_g)
    ext = gathered[:, n_small:].reshape(8, -1)
    n_dmod = DEPTH * Bl * N_MOD * D
    dmod_all = ext[:, 0:n_dmod].reshape(8, DEPTH, Bl, N_MOD * D)
    c_all = ext[:, n_dmod:n_dmod + Bl * D].reshape(8 * Bl, D)
    g_ada_w, g_ada_b = [], []
    ncol = N_MOD * D // N_CHIPS
    for l in range(DEPTH):
        dm = dmod_all[:, l].reshape(8 * Bl, N_MOD * D)
        g_ada_w.append(ada_grad(c_all, lax.dynamic_slice_in_dim(dm, chip * ncol, ncol, axis=1)))
        g_ada_b.append(sum_slots(dm.reshape(8 * Bl, N_MOD, D), 8 * Bl, "sum_ada_b").reshape(N_MOD * D))
    g_ada_w, g_ada_b = jnp.stack(g_ada_w), jnp.stack(g_ada_b)

    reds = []
    for l in range(DEPTH):
        g = grads[l]
        by_chip = [g["ffn1_in"], g["ffn1_out"], g["ffn2_in"], g["ffn2_out"],
                   _col_shards(mix_in_unext(g["mix_in"])), mix_out_unext(g["mix_out"]).reshape(N_CHIPS, D // N_CHIPS, D),
                   _col_shards(uq_unext(g["wq"])), _col_shards(ukv_unext(g["wkv"]))]
        got = sibling_swap(by_chip)
        chip_sum = [add_kept_half(a, r, core, "add_sibling") for a, r in zip(by_chip, got)]
        reds.append([sum_slots(s, N_CHIPS, "sum_chips") for s in chip_exchange(chip_sum)])
    g_shard = dict(zip(REDUCED, sibling_join(reds)))

    grad = dict(g_shard)
    grad.update(g_small)
    grad["ada_w"], grad["ada_b"] = g_ada_w, g_ada_b
    for n in ("ln_g", "ln_b"):
        grad[n] = lax.dynamic_slice_in_dim(g_small[n], chip * (D // N_CHIPS), D // N_CHIPS, axis=2)
    out = {"grad": grad, "delta": {}, "new_m": {}, "new_v": {}}
    for n in WEIGHTS:
        shp = w[n].shape
        two_d = (-1, shp[-1])
        res = adamw(w[n].reshape(two_d), grad[n].reshape(two_d), m[n].reshape(two_d), v[n].reshape(two_d), "adamw_" + n)
        grad[n] = grad[n].reshape(shp)
        for key, r in zip(("delta", "new_m", "new_v"), res):
            out[key][n] = r.reshape(shp)
    outs = [loss, dx.reshape(Bl, S, D)]
    for key in ("grad", "delta", "new_m", "new_v"):
        outs += [out[key][n] for n in WEIGHTS]
    return tuple(outs)
```

```python
import functools

import jax
import jax.numpy as jnp
import numpy as np
from jax import lax
from jax.experimental import pallas as pl
from jax.experimental.pallas import tpu as pltpu

F32, BF16 = jnp.float32, jnp.bfloat16
MESH = pl.DeviceIdType.MESH

N_CHIPS = 4
D = 1024
DEPTH = 2
FF = 2816
N_MOD = 9
GW = 256
HEADS = 4
HD = 64
HP = 128
A_CHUNK = 16
LB_FLOOR = 1e-30
B_Q_LORA, B_KV_LORA, B_NOPE, B_ROPE = 256, 128, 64, 32
ROPE_THETA = 10000.0
D_CHUNK = 128
MIX_COLS = 2724
ALPHA = (2 * DEPTH) ** 0.25
LN_EPS = 1e-5
RMS_EPS = 1e-6
ADAM_LR, ADAM_B1, ADAM_B2, ADAM_EPS, ADAM_WD, ADAM_STEP = 0.001, 0.9, 0.999, 1e-08, 0.01, 10

NP = 3840
NP_TILE = 1920
C_A, C_B, C_CQ, C_CK, C_CV, C_D, C_CF = 0, 1024, 1536, 2048, 2560, 3072, 3584
NCAT = 1536
O_A, O_B, O_C, O_D = 0, 256, 768, 1280

VMEM_BIG = 48 << 20


def _cparams(vmem=None):
    return pltpu.CompilerParams(vmem_limit_bytes=vmem) if vmem else pltpu.CompilerParams()


def _sds(shape, dtype):
    return jax.ShapeDtypeStruct(tuple(shape), dtype)


@jax.custom_vjp
def _mm(a, w):
    return jnp.dot(a.astype(BF16), w.astype(BF16), preferred_element_type=F32)


def _mm_f(a, w):
    return _mm(a, w), (a, w)


def _mm_b(res, g):
    a, w = res
    gb = g.astype(BF16)
    da = lax.dot_general(gb, w.astype(BF16), (((1,), (1,)), ((), ())), preferred_element_type=F32)
    dw = lax.dot_general(a.astype(BF16), gb, (((0,), (0,)), ((), ())), preferred_element_type=F32)
    return da.astype(a.dtype), dw.astype(w.dtype)


_mm.defvjp(_mm_f, _mm_b)


@jax.custom_vjp
def _mm_nt(a, b):
    return lax.dot_general(a.astype(BF16), b.astype(BF16), (((1,), (1,)), ((), ())), preferred_element_type=F32)


def _mm_nt_f(a, b):
    return _mm_nt(a, b), (a, b)


def _mm_nt_b(res, g):
    a, b = res
    gb = g.astype(BF16)
    da = jnp.dot(gb, b.astype(BF16), preferred_element_type=F32)
    db = lax.dot_general(gb, a.astype(BF16), (((0,), (0,)), ((), ())), preferred_element_type=F32)
    return da.astype(a.dtype), db.astype(b.dtype)


_mm_nt.defvjp(_mm_nt_f, _mm_nt_b)


@jax.custom_vjp
def _mm_tn(a, b):
    return lax.dot_general(a.astype(BF16), b.astype(BF16), (((0,), (0,)), ((), ())), preferred_element_type=F32)


def _mm_tn_f(a, b):
    return _mm_tn(a, b), (a, b)


def _mm_tn_b(res, g):
    a, b = res
    gb = g.astype(BF16)
    da = lax.dot_general(b.astype(BF16), gb, (((1,), (1,)), ((), ())), preferred_element_type=F32)
    db = jnp.dot(a.astype(BF16), gb, preferred_element_type=F32)
    return da.astype(a.dtype), db.astype(b.dtype)


_mm_tn.defvjp(_mm_tn_f, _mm_tn_b)


def _mm_hi(a, w):
    return jnp.dot(a, w, precision=lax.Precision.HIGHEST, preferred_element_type=F32)


def _iota(shape, dim):
    return lax.broadcasted_iota(jnp.int32, shape, dim)


def _head_sum_mats():
    e = (_iota((GW, HP), 0) // HD == _iota((GW, HP), 1)).astype(F32)
    et = (_iota((HP, GW), 1) // HD == _iota((HP, GW), 0)).astype(F32)
    return e, et


def _modulate(x, mod_ref, sh, sc):
    return x * (1.0 + mod_ref[sc:sc + 1, :]) + mod_ref[sh:sh + 1, :]


def _ln_res(x, y, gate, lg, lb, gs):
    r = ALPHA * x + gs * (1.0 + gate) * y
    mu = jnp.mean(r, axis=-1, keepdims=True)
    var = jnp.mean(jnp.square(r - mu), axis=-1, keepdims=True)
    return (r - mu) * lax.rsqrt(var + LN_EPS) * lg + lb


def _rms(x, g):
    return x * lax.rsqrt(jnp.mean(x * x, axis=-1, keepdims=True) + RMS_EPS) * g


def _tile(n, pref):
    return pref if n % pref == 0 else n


def mod_fwd(c8, w, l, b):
    tn = w.shape[3]
    n = N_CHIPS * tn

    def body(c_ref, w_ref, b_ref, o_ref):
        h = jax.nn.silu(c_ref[...]).astype(BF16)
        o_ref[...] = jnp.dot(h, w_ref[...], preferred_element_type=F32) + b_ref[...]

    return pl.pallas_call(
        body, name="mod_fwd", grid=(N_CHIPS,),
        in_specs=[pl.BlockSpec((8, D), lambda j: (0, 0)), pl.BlockSpec((None, None, D, tn), lambda j: (l, j, 0, 0)),
                  pl.BlockSpec((1, tn), lambda j: (0, j))],
        out_specs=pl.BlockSpec((8, tn), lambda j: (0, j)), out_shape=_sds((8, n), F32),
        compiler_params=_cparams(VMEM_BIG))(c8, w, b)


def ffn_in_fwd(x, mod, w_in, l, sh, sc, S):
    T = x.shape[0]
    tm, tn = _tile(S, 512), FF // 2
    tpb, nj = S // tm, 2

    def body(x_ref, mod_ref, wg_ref, wu_ref, zg_ref, zu_ref, act_ref, h_ref):
        @pl.when(pl.program_id(1) == 0)
        def _():
            h_ref[...] = _modulate(x_ref[...], mod_ref, sh, sc).astype(BF16)
        g = jnp.dot(h_ref[...], wg_ref[...], preferred_element_type=F32)
        u = jnp.dot(h_ref[...], wu_ref[...], preferred_element_type=F32)
        zg_ref[...] = g
        zu_ref[...] = u
        act_ref[...] = (jax.nn.silu(g) * u).astype(BF16)

    return pl.pallas_call(
        body, name="ffn_in_fwd", grid=(T // tm, nj),
        in_specs=[pl.BlockSpec((tm, D), lambda i, j: (i, 0)),
                  pl.BlockSpec((None, N_MOD, D), lambda i, j: (i // tpb, 0, 0)),
                  pl.BlockSpec((None, None, D, tn), lambda i, j: (l, j, 0, 0)),
                  pl.BlockSpec((None, None, D, tn), lambda i, j: (l, j + nj, 0, 0))],
        out_specs=[pl.BlockSpec((tm, tn), lambda i, j: (i, j))] * 3,
        out_shape=[_sds((T, FF), F32), _sds((T, FF), F32), _sds((T, FF), BF16)],
        scratch_shapes=[pltpu.VMEM((tm, D), BF16)],
        compiler_params=_cparams(VMEM_BIG))(x, mod, w_in, w_in)


def mix_in_fwd(x, mod, w, sh, sc, S):
    T = x.shape[0]
    n = w.shape[1]
    tm, tn = _tile(S, 512), NP_TILE
    tpb = S // tm

    def body(x_ref, mod_ref, w_ref, o_ref, h_ref):
        @pl.when(pl.program_id(1) == 0)
        def _():
            h_ref[...] = _modulate(x_ref[...], mod_ref, sh, sc).astype(BF16)
        o_ref[...] = jnp.dot(h_ref[...], w_ref[...], preferred_element_type=F32)

    return pl.pallas_call(
        body, name="mix_in_fwd", grid=(T // tm, n // tn),
        in_specs=[pl.BlockSpec((tm, D), lambda i, j: (i, 0)),
                  pl.BlockSpec((None, N_MOD, D), lambda i, j: (i // tpb, 0, 0)),
                  pl.BlockSpec((D, tn), lambda i, j: (0, j))],
        out_specs=pl.BlockSpec((tm, tn), lambda i, j: (i, j)), out_shape=_sds((T, n), F32),
        scratch_shapes=[pltpu.VMEM((tm, D), BF16)],
        compiler_params=_cparams(VMEM_BIG))(x, mod, w)


def out_ln_fwd(act, w_out, x, mod, lg, lb, gate, gs, S, l=None):
    T, K = act.shape
    tm = _tile(S, 512)
    tpb = S // tm

    def body(a_ref, w_ref, x_ref, mod_ref, lg_ref, lb_ref, y_ref, xn_ref):
        y = jnp.dot(a_ref[...], w_ref[...].reshape(K, D), preferred_element_type=F32)
        y_ref[...] = y
        xn_ref[...] = _ln_res(x_ref[...], y, mod_ref[gate:gate + 1, :], lg_ref[...], lb_ref[...], gs)

    if l is None:
        w_spec = pl.BlockSpec((K, D), lambda i: (0, 0))
    else:
        w_spec = pl.BlockSpec((None, N_CHIPS, K // N_CHIPS, D), lambda i: (l, 0, 0, 0))
    return pl.pallas_call(
        body, name="out_ln_fwd", grid=(T // tm,),
        in_specs=[pl.BlockSpec((tm, K), lambda i: (i, 0)), w_spec,
                  pl.BlockSpec((tm, D), lambda i: (i, 0)),
                  pl.BlockSpec((None, N_MOD, D), lambda i: (i // tpb, 0, 0)),
                  pl.BlockSpec((1, D), lambda i: (0, 0)), pl.BlockSpec((1, D), lambda i: (0, 0))],
        out_specs=[pl.BlockSpec((tm, D), lambda i: (i, 0))] * 2,
        out_shape=[_sds((T, D), F32), _sds((T, D), F32)],
        compiler_params=_cparams(VMEM_BIG))(act, w_out, x, mod, lg, lb)


def ln_res_bwd(dxn, x, y, mod, lg, lb, gate, gs, S):
    T = x.shape[0]
    B = T // S
    tm = _tile(S, 512)
    tpb = S // tm

    def body(d_ref, x_ref, y_ref, mod_ref, lg_ref, lb_ref, dx_ref, dy_ref, dg_ref, dlg_ref, dlb_ref):
        i = pl.program_id(0)
        f = functools.partial(_ln_res, gs=gs)
        _, vjp = jax.vjp(f, x_ref[...], y_ref[...], mod_ref[gate:gate + 1, :], lg_ref[...], lb_ref[...])
        dx, dy, dg, dlg, dlb = vjp(d_ref[...])
        dx_ref[...] = dx
        dy_ref[...] = dy.astype(BF16)

        @pl.when(i % tpb == 0)
        def _():
            dg_ref[...] = jnp.zeros_like(dg_ref)

        @pl.when(i == 0)
        def _():
            dlg_ref[...] = jnp.zeros_like(dlg_ref)
            dlb_ref[...] = jnp.zeros_like(dlb_ref)

        dg_ref[...] += dg
        dlg_ref[...] += dlg
        dlb_ref[...] += dlb

    tok = pl.BlockSpec((tm, D), lambda i: (i, 0))
    vec = pl.BlockSpec((1, D), lambda i: (0, 0))
    return pl.pallas_call(
        body, name="ln_res_bwd", grid=(T // tm,),
        in_specs=[tok, tok, tok, pl.BlockSpec((None, N_MOD, D), lambda i: (i // tpb, 0, 0)), vec, vec],
        out_specs=[tok, tok, pl.BlockSpec((None, 1, D), lambda i: (i // tpb, 0, 0)), vec, vec],
        out_shape=[_sds((T, D), F32), _sds((T, D), BF16), _sds((B, 1, D), F32), _sds((1, D), F32), _sds((1, D), F32)],
        compiler_params=_cparams(VMEM_BIG))(dxn, x, y, mod, lg, lb)


def swiglu_bwd(dy, w_out, l, zg, zu, S):
    T = dy.shape[0]
    tm, tn = _tile(S, 512), FF // 2

    def body(dy_ref, w_ref, zg_ref, zu_ref, dg_ref, du_ref):
        da = lax.dot_general(dy_ref[...], w_ref[...].reshape(tn, D), (((1,), (1,)), ((), ())), preferred_element_type=F32)
        g, u = zg_ref[...], zu_ref[...]
        sg = jax.nn.sigmoid(g)
        dg_ref[...] = (da * u * (sg * (1.0 + g * (1.0 - sg)))).astype(BF16)
        du_ref[...] = (da * (g * sg)).astype(BF16)

    zt = pl.BlockSpec((tm, tn), lambda i, j: (i, j))
    return pl.pallas_call(
        body, name="swiglu_bwd", grid=(T // tm, FF // tn),
        in_specs=[pl.BlockSpec((tm, D), lambda i, j: (i, 0)),
                  pl.BlockSpec((None, 2, FF // N_CHIPS, D), lambda i, j: (l, j, 0, 0)), zt, zt],
        out_specs=[zt, zt], out_shape=[_sds((T, FF), BF16), _sds((T, FF), BF16)],
        compiler_params=_cparams(VMEM_BIG))(dy, w_out, zg, zu)


def nt_plain(dy, w):
    T = dy.shape[0]
    K = w.shape[0]
    tm = _tile(T, 512)

    def body(dy_ref, w_ref, o_ref):
        o_ref[...] = lax.dot_general(dy_ref[...], w_ref[...], (((1,), (1,)), ((), ())), preferred_element_type=F32)

    return pl.pallas_call(
        body, name="nt_plain", grid=(T // tm,),
        in_specs=[pl.BlockSpec((tm, D), lambda i: (i, 0)), pl.BlockSpec((K, D), lambda i: (0, 0))],
        out_specs=pl.BlockSpec((tm, K), lambda i: (i, 0)), out_shape=_sds((T, K), F32),
        compiler_params=_cparams(VMEM_BIG))(dy, w)


def tn_mm(a, b, tk):
    T, K = a.shape
    N = b.shape[1]
    tt = _tile(T, 512)

    def body(a_ref, b_ref, o_ref):
        @pl.when(pl.program_id(1) == 0)
        def _():
            o_ref[...] = jnp.zeros_like(o_ref)
        o_ref[...] += lax.dot_general(a_ref[...], b_ref[...], (((0,), (0,)), ((), ())), preferred_element_type=F32)

    return pl.pallas_call(
        body, name="tn_mm", grid=(K // tk, T // tt),
        in_specs=[pl.BlockSpec((tt, tk), lambda k, t: (t, k)), pl.BlockSpec((tt, N), lambda k, t: (t, 0))],
        out_specs=pl.BlockSpec((tk, N), lambda k, t: (k, 0)), out_shape=_sds((K, N), F32),
        compiler_params=_cparams(VMEM_BIG))(a, b)


def tn_mm_mod(x, mod, b, sh, sc, S, tn):
    T = x.shape[0]
    N = b.shape[1]
    tt = _tile(S, 512)
    tpb = S // tt

    def body(x_ref, mod_ref, b_ref, o_ref):
        @pl.when(pl.program_id(1) == 0)
        def _():
            o_ref[...] = jnp.zeros_like(o_ref)
        h = _modulate(x_ref[...], mod_ref, sh, sc).astype(BF16)
        o_ref[...] += lax.dot_general(h, b_ref[...], (((0,), (0,)), ((), ())), preferred_element_type=F32)

    return pl.pallas_call(
        body, name="tn_mm_mod", grid=(N // tn, T // tt),
        in_specs=[pl.BlockSpec((tt, D), lambda j, t: (t, 0)),
                  pl.BlockSpec((None, N_MOD, D), lambda j, t: (t // tpb, 0, 0)),
                  pl.BlockSpec((tt, tn), lambda j, t: (t, j))],
        out_specs=pl.BlockSpec((D, tn), lambda j, t: (0, j)), out_shape=_sds((D, N), F32),
        compiler_params=_cparams(VMEM_BIG))(x, mod, b)


def tn_mm_mod_shards(x, mod, bg, bu, sh, sc, S):
    T = x.shape[0]
    tn = FF // 2
    tt = _tile(S, 512)
    tpb = S // tt

    def body(x_ref, mod_ref, bg_ref, bu_ref, o_ref):
        j = pl.program_id(0)

        @pl.when(pl.program_id(1) == 0)
        def _():
            o_ref[...] = jnp.zeros_like(o_ref)
        h = _modulate(x_ref[...], mod_ref, sh, sc).astype(BF16)

        @pl.when(j < 2)
        def _():
            o_ref[...] += lax.dot_general(h, bg_ref[...], (((0,), (0,)), ((), ())), preferred_element_type=F32)

        @pl.when(j >= 2)
        def _():
            o_ref[...] += lax.dot_general(h, bu_ref[...], (((0,), (0,)), ((), ())), preferred_element_type=F32)

    return pl.pallas_call(
        body, name="tn_mm_mod_shards", grid=(N_CHIPS, T // tt),
        in_specs=[pl.BlockSpec((tt, D), lambda j, t: (t, 0)),
                  pl.BlockSpec((None, N_MOD, D), lambda j, t: (t // tpb, 0, 0)),
                  pl.BlockSpec((tt, tn), lambda j, t: (t, jnp.minimum(j, 1))),
                  pl.BlockSpec((tt, tn), lambda j, t: (t, jnp.maximum(j - 2, 0)))],
        out_specs=pl.BlockSpec((None, D, tn), lambda j, t: (j, 0, 0)), out_shape=_sds((N_CHIPS, D, tn), F32),
        compiler_params=_cparams(VMEM_BIG))(x, mod, bg, bu)


def nt_mod_bwd(ds, w, offs, x, mod, dres, sc, S, tk, l=None):
    T = x.shape[0]
    B = T // S
    tm = _tile(S, 512)
    tpb = S // tm
    Kd = ds[0].shape[1]
    nk = Kd // tk
    n_in = len(ds)

    def body(*refs):
        d_refs, w_refs = refs[:n_in], refs[n_in:2 * n_in]
        x_ref, mod_ref, r_ref, dx_ref, dsh_ref, dsc_ref, acc = refs[2 * n_in:]
        i, k = pl.program_id(0), pl.program_id(1)

        @pl.when(k == 0)
        def _():
            acc[...] = jnp.zeros_like(acc)

        for d_ref, w_ref in zip(d_refs, w_refs):
            acc[...] += lax.dot_general(d_ref[...], w_ref[...], (((1,), (1,)), ((), ())), preferred_element_type=F32)

        @pl.when(k == nk - 1)
        def _():
            dh = acc[...]
            dx_ref[...] = dh * (1.0 + mod_ref[sc:sc + 1, :]) + r_ref[...]

            @pl.when(i % tpb == 0)
            def _():
                dsh_ref[...] = jnp.zeros_like(dsh_ref)
                dsc_ref[...] = jnp.zeros_like(dsc_ref)

            dsh_ref[...] += jnp.sum(dh, axis=0, keepdims=True)
            dsc_ref[...] += jnp.sum(dh * x_ref[...], axis=0, keepdims=True)

    tok = pl.BlockSpec((tm, D), lambda i, k: (i, 0))
    vec = pl.BlockSpec((None, 1, D), lambda i, k: (i // tpb, 0, 0))
    in_specs = [pl.BlockSpec((tm, tk), lambda i, k: (i, k)) for _ in ds]
    if l is None:
        in_specs += [pl.BlockSpec((D, tk), functools.partial(lambda i, k, o: (0, k + o), o=off // tk)) for off in offs]
    else:
        in_specs += [pl.BlockSpec((None, None, D, tk), functools.partial(lambda i, k, o: (l, k + o, 0, 0), o=off)) for off in offs]
    in_specs += [tok, pl.BlockSpec((None, N_MOD, D), lambda i, k: (i // tpb, 0, 0)), tok]
    return pl.pallas_call(
        body, name="nt_mod_bwd", grid=(T // tm, nk), in_specs=in_specs,
        out_specs=[tok, vec, vec],
        out_shape=[_sds((T, D), F32), _sds((B, 1, D), F32), _sds((B, 1, D), F32)],
        scratch_shapes=[pltpu.VMEM((tm, D), F32)],
        compiler_params=_cparams(VMEM_BIG))(*ds, *([w] * n_in), x, mod, dres)


def loss_head(y, tgt):
    T = y.shape[0]
    tm = _tile(T, 512)

    def body(y_ref, t_ref, l_ref, d_ref):
        @pl.when(pl.program_id(0) == 0)
        def _():
            l_ref[...] = jnp.zeros_like(l_ref)
        e = y_ref[...] - t_ref[...]
        d_ref[...] = e * (1.0 / D)
        l_ref[...] += 0.5 * jnp.sum(jnp.sum(e * e, axis=1, keepdims=True) * (1.0 / D))

    tok = pl.BlockSpec((tm, D), lambda i: (i, 0))
    return pl.pallas_call(
        body, name="loss_head", grid=(T // tm,), in_specs=[tok, tok],
        out_specs=[pl.BlockSpec((8, 128), lambda i: (0, 0)), tok],
        out_shape=[_sds((8, 128), F32), _sds((T, D), F32)],
        compiler_params=_cparams(VMEM_BIG))(y, tgt)


def _hgrn_block(q, fz, inp, go, st, lb, ng, blk):
    nc = blk // A_CHUNK
    lb_eff = jnp.maximum(lb, LB_FLOOR)
    log_f = jnp.logaddexp(jnp.log(lb_eff), jnp.log1p(-lb) + jax.nn.log_sigmoid(fz))
    k = (1.0 - lb) * jax.nn.sigmoid(-fz) - (lb_eff - lb)
    qf = jax.nn.silu(q)
    same_chunk = _iota((blk, blk), 0) // A_CHUNK == _iota((blk, blk), 1) // A_CHUNK
    tril = (same_chunk & (_iota((blk, blk), 1) <= _iota((blk, blk), 0))).astype(F32)
    G = _mm_hi(tril, log_f)
    e_mat, et_mat = _head_sum_mats()
    G4, q4, k4, v4 = (z.reshape(nc, A_CHUNK, GW) for z in (G, qf, k, inp))
    shp = (nc, A_CHUNK, A_CHUNK, GW)
    causal = _iota(shp, 2) <= _iota(shp, 1)
    decay = jnp.exp(jnp.where(causal, G4[:, :, None, :] - G4[:, None, :, :], -jnp.inf))
    prod = q4[:, :, None, :] * k4[:, None, :, :] * decay
    scores = _mm(prod.reshape(nc * A_CHUNK * A_CHUNK, GW), e_mat.astype(BF16))
    spread = _mm(scores, et_mat.astype(BF16)).reshape(shp)
    o_intra = jnp.sum(spread * v4[:, None, :, :], axis=2).reshape(blk, GW)
    head_diag = (_iota((GW, GW), 0) // HD == _iota((GW, GW), 1) // HD).astype(F32)
    g_last = [jnp.sum(log_f[c * A_CHUNK:(c + 1) * A_CHUNK], axis=0, keepdims=True) for c in range(nc)]
    g_last_b = jnp.concatenate([jnp.broadcast_to(g, (A_CHUNK, GW)) for g in g_last], axis=0)
    q_dec = qf * jnp.exp(G)
    k_end = k * jnp.exp(g_last_b - G)
    outs = []
    for c in range(nc):
        rows = slice(c * A_CHUNK, (c + 1) * A_CHUNK)
        outs.append(_mm_nt(q_dec[rows], st))
        st = st * jnp.exp(g_last[c]) + _mm_tn(inp[rows], k_end[rows]) * head_diag
    o = o_intra + jnp.concatenate(outs, axis=0)
    ms = _mm_hi(o * o, e_mat) * (1.0 / HD)
    o = o * _mm_hi(lax.rsqrt(ms + RMS_EPS), et_mat) * ng
    return o * jax.nn.silu(go), st


HGRN_BLK = 128


def hgrn_fwd(proj, lb, ng, S):
    T = proj.shape[0]
    B = T // S
    blk = min(HGRN_BLK, S)
    nb = S // blk

    def body(p_ref, lb_ref, ng_ref, o_ref, st_out_ref, st_ref):
        @pl.when(pl.program_id(1) == 0)
        def _():
            st_ref[...] = jnp.zeros_like(st_ref)
        st_out_ref[...] = st_ref[...]
        p = p_ref[...]
        o, st = _hgrn_block(p[:, 0:GW], p[:, GW:2 * GW], p[:, 2 * GW:3 * GW], p[:, 3 * GW:4 * GW],
                            st_ref[...], lb_ref[...], ng_ref[...], blk)
        o_ref[...] = o.astype(BF16)
        st_ref[...] = st

    vec = pl.BlockSpec((1, GW), lambda b, j: (0, 0))
    return pl.pallas_call(
        body, name="hgrn_fwd", grid=(B, nb),
        in_specs=[pl.BlockSpec((blk, 4 * GW), lambda b, j: (b * nb + j, C_A // (4 * GW))), vec, vec],
        out_specs=[pl.BlockSpec((blk, GW), lambda b, j: (b * nb + j, 0)),
                   pl.BlockSpec((None, GW, GW), lambda b, j: (b * nb + j, 0, 0))],
        out_shape=[_sds((T, GW), BF16), _sds((B * nb, GW, GW), F32)],
        scratch_shapes=[pltpu.VMEM((GW, GW), F32)],
        compiler_params=_cparams(VMEM_BIG))(proj, lb, ng)


def hgrn_bwd(proj, states, dcat, lb, ng, S):
    T = proj.shape[0]
    B = T // S
    blk = min(HGRN_BLK, S)
    nb = S // blk

    def body(p_ref, st_in_ref, do_ref, lb_ref, ng_ref, dp_ref, dlb_ref, dng_ref, dst_ref):
        b, j = pl.program_id(0), pl.program_id(1)

        @pl.when(j == 0)
        def _():
            dst_ref[...] = jnp.zeros_like(dst_ref)

        @pl.when((b == 0) & (j == 0))
        def _():
            dlb_ref[...] = jnp.zeros_like(dlb_ref)
            dng_ref[...] = jnp.zeros_like(dng_ref)

        p = p_ref[...]
        f = functools.partial(_hgrn_block, blk=blk)
        _, vjp = jax.vjp(f, p[:, 0:GW], p[:, GW:2 * GW], p[:, 2 * GW:3 * GW], p[:, 3 * GW:4 * GW],
                         st_in_ref[...], lb_ref[...], ng_ref[...])
        dq, df, di, dg, dst, dlb, dng = vjp((do_ref[...], dst_ref[...]))
        dp_ref[...] = jnp.concatenate([dq, df, di, dg], axis=1).astype(BF16)
        dst_ref[...] = dst
        dlb_ref[...] += dlb
        dng_ref[...] += dng

    def rev(b, j):
        return b * nb + (nb - 1 - j)

    vec = pl.BlockSpec((1, GW), lambda b, j: (0, 0))
    return pl.pallas_call(
        body, name="hgrn_bwd", grid=(B, nb),
        in_specs=[pl.BlockSpec((blk, 4 * GW), lambda b, j: (rev(b, j), C_A // (4 * GW))),
                  pl.BlockSpec((None, GW, GW), lambda b, j: (rev(b, j), 0, 0)),
                  pl.BlockSpec((blk, GW), lambda b, j: (rev(b, j), O_A // GW)), vec, vec],
        out_specs=[pl.BlockSpec((blk, 4 * GW), lambda b, j: (rev(b, j), 0)), vec, vec],
        out_shape=[_sds((T, 4 * GW), BF16), _sds((1, GW), F32), _sds((1, GW), F32)],
        scratch_shapes=[pltpu.VMEM((GW, GW), F32)],
        compiler_params=_cparams(VMEM_BIG))(proj, states, dcat, lb, ng)


ATT_TQ = 256


def _attn_block(q, k, v, cum, qpos0, scale, use_cum):
    s = _mm_nt(q, k) * scale
    if use_cum:
        s = s - cum
    qpos = qpos0 + _iota(s.shape, 0)
    s = jnp.where(_iota(s.shape, 1) <= qpos, s, -jnp.inf)
    e = jnp.exp(s - jnp.max(s, axis=-1, keepdims=True))
    p = e / jnp.sum(e, axis=-1, keepdims=True)
    return _mm(p, v)


def attn_fwd(qa, qo, ka, ko, va, vo, cum, scale, S):
    T = qa.shape[0]
    B = T // S
    tq = min(ATT_TQ, S)
    nq = S // tq
    use_cum = cum is not None

    def body(*refs):
        if use_cum:
            q_ref, k_ref, v_ref, c_ref, o_ref = refs
            crow = c_ref[pl.ds(pl.program_id(1), 1), :]
        else:
            q_ref, k_ref, v_ref, o_ref = refs
            crow = None
        o = _attn_block(q_ref[...], k_ref[...], v_ref[...], crow, pl.program_id(2) * tq, scale, use_cum)
        o_ref[...] = o.astype(BF16)

    in_specs = [pl.BlockSpec((tq, HP), lambda b, h, i: (b * nq + i, qo + h)),
                pl.BlockSpec((S, HP), lambda b, h, i: (b, ko + h)),
                pl.BlockSpec((S, HP), lambda b, h, i: (b, vo + h))]
    args = [qa, ka, va]
    if use_cum:
        in_specs.append(pl.BlockSpec((None, 8, S), lambda b, h, i: (b, 0, 0)))
        args.append(cum)
    return pl.pallas_call(
        body, name="attn_fwd", grid=(B, HEADS, nq), in_specs=in_specs,
        out_specs=pl.BlockSpec((tq, HP), lambda b, h, i: (b * nq + i, h)),
        out_shape=_sds((T, HEADS * HP), BF16),
        compiler_params=_cparams(VMEM_BIG))(*args)


def attn_bwd(qa, qo, ka, ko, va, vo, cum, dcat, do_off, scale, S, out_dtype):
    T = qa.shape[0]
    B = T // S
    tq = min(ATT_TQ, S)
    nq = S // tq
    use_cum = cum is not None

    def body(*refs):
        if use_cum:
            q_ref, k_ref, v_ref, do_ref, c_ref, dq_ref, dk_ref, dv_ref, dc_ref, dk_acc, dv_acc = refs
            crow = c_ref[pl.ds(pl.program_id(1), 1), :]
        else:
            q_ref, k_ref, v_ref, do_ref, dq_ref, dk_ref, dv_ref, dk_acc, dv_acc = refs
            crow = jnp.zeros((1, S), F32)
        i = pl.program_id(2)

        @pl.when(i == 0)
        def _():
            dk_acc[...] = jnp.zeros_like(dk_acc)
            dv_acc[...] = jnp.zeros_like(dv_acc)
            if use_cum:
                dc_ref[...] = jnp.zeros_like(dc_ref)

        f = functools.partial(_attn_block, qpos0=i * tq, scale=scale, use_cum=use_cum)
        _, vjp = jax.vjp(f, q_ref[...], k_ref[...], v_ref[...], crow)
        dq, dk, dv, dc = vjp(do_ref[...])
        dq_ref[...] = dq.astype(out_dtype)
        dk_acc[...] += dk
        dv_acc[...] += dv
        if use_cum:
            dc_ref[...] += dc

        @pl.when(i == nq - 1)
        def _():
            dk_ref[...] = dk_acc[...].astype(out_dtype)
            dv_ref[...] = dv_acc[...].astype(out_dtype)

    qspec = pl.BlockSpec((tq, HP), lambda b, h, i: (b * nq + i, qo + h))
    in_specs = [qspec, pl.BlockSpec((S, HP), lambda b, h, i: (b, ko + h)),
                pl.BlockSpec((S, HP), lambda b, h, i: (b, vo + h)),
                pl.BlockSpec((tq, HP), lambda b, h, i: (b * nq + i, do_off + h))]
    args = [qa, ka, va, dcat]
    kv_out = pl.BlockSpec((S, HP), lambda b, h, i: (b, h))
    out_specs = [pl.BlockSpec((tq, HP), lambda b, h, i: (b * nq + i, h)), kv_out, kv_out]
    out_shape = [_sds((T, HEADS * HP), out_dtype)] * 3
    if use_cum:
        in_specs.append(pl.BlockSpec((None, 8, S), lambda b, h, i: (b, 0, 0)))
        args.append(cum)
        out_specs.append(pl.BlockSpec((None, 1, S), lambda b, h, i: (b * HEADS + h, 0, 0)))
        out_shape.append(_sds((B * HEADS, 1, S), F32))
    return pl.pallas_call(
        body, name="attn_bwd", grid=(B, HEADS, nq), in_specs=in_specs, out_specs=out_specs, out_shape=out_shape,
        scratch_shapes=[pltpu.VMEM((S, HP), F32), pltpu.VMEM((S, HP), F32)],
        compiler_params=_cparams(VMEM_BIG))(*args)


def _tri(n, upper):
    r, c = _iota((n, n), 0), _iota((n, n), 1)
    return ((r <= c) if upper else (r >= c)).astype(F32)


def fox_gate_fwd(proj, bcol, S):
    T = proj.shape[0]
    B = T // S
    ts = _tile(S, 512)
    nt = S // ts

    def body(p_ref, b_ref, o_ref, carry):
        @pl.when(pl.program_id(1) == 0)
        def _():
            carry[...] = jnp.zeros_like(carry)
        cf = jnp.transpose(p_ref[...])[0:8, :]
        lf = jax.nn.log_sigmoid(cf + b_ref[...])
        cum = _mm_hi(lf, _tri(ts, True)) + carry[...]
        o_ref[...] = cum
        carry[...] += jnp.sum(lf, axis=1, keepdims=True)

    return pl.pallas_call(
        body, name="fox_gate_fwd", grid=(B, nt),
        in_specs=[pl.BlockSpec((ts, HP), lambda b, j: (b * nt + j, C_CF // HP)), pl.BlockSpec((8, 1), lambda b, j: (0, 0))],
        out_specs=pl.BlockSpec((None, 8, ts), lambda b, j: (b, 0, j)), out_shape=_sds((B, 8, S), F32),
        scratch_shapes=[pltpu.VMEM((8, 1), F32)],
        compiler_params=_cparams(VMEM_BIG))(proj, bcol)


def fox_gate_bwd(proj, bcol, dcum, S):
    T = proj.shape[0]
    B = T // S
    ts = _tile(S, 512)
    nt = S // ts

    def body(p_ref, b_ref, dc_ref, dp_ref, db_ref, carry):
        b, j = pl.program_id(0), pl.program_id(1)

        @pl.when(j == 0)
        def _():
            carry[...] = jnp.zeros_like(carry)

        @pl.when((b == 0) & (j == 0))
        def _():
            db_ref[...] = jnp.zeros_like(db_ref)

        cf = jnp.transpose(p_ref[...])[0:8, :]
        dc = dc_ref[...]
        dlf = _mm_hi(dc, _tri(ts, False)) + carry[...]
        carry[...] += jnp.sum(dc, axis=1, keepdims=True)
        dcf = dlf * jax.nn.sigmoid(-(cf + b_ref[...]))
        db_ref[...] += jnp.sum(dcf, axis=1, keepdims=True)
        full = jnp.concatenate([dcf, jnp.zeros((HP - 8, ts), F32)], axis=0)
        dp_ref[...] = jnp.transpose(full).astype(BF16)

    def rev(b, j):
        return nt - 1 - j

    return pl.pallas_call(
        body, name="fox_gate_bwd", grid=(B, nt),
        in_specs=[pl.BlockSpec((ts, HP), lambda b, j: (b * nt + rev(b, j), C_CF // HP)),
                  pl.BlockSpec((8, 1), lambda b, j: (0, 0)),
                  pl.BlockSpec((None, 8, ts), lambda b, j: (b, 0, rev(b, j)))],
        out_specs=[pl.BlockSpec((ts, HP), lambda b, j: (b * nt + rev(b, j), 0)), pl.BlockSpec((8, 1), lambda b, j: (0, 0))],
        out_shape=[_sds((T, HP), BF16), _sds((8, 1), F32)],
        scratch_shapes=[pltpu.VMEM((8, 1), F32)],
        compiler_params=_cparams(VMEM_BIG))(proj, bcol, dcum)


def _mla_pre(blk, gq, gkv, wq, wkv, place, cos_q, sin_q, cs_k):
    nq = _rms(blk[:, 0:B_Q_LORA], gq)
    nkv = _rms(blk[:, B_Q_LORA:B_Q_LORA + B_KV_LORA], gkv)
    qq = _mm(nq, wq)
    q = qq[:, 0:HEADS * HP] * cos_q + qq[:, HEADS * HP:] * sin_q
    kv = _mm(nkv, wkv)
    k = kv[:, 0:HEADS * HP] + _mm(blk[:, B_Q_LORA + B_KV_LORA:] * cs_k, place)
    return q, k, kv[:, HEADS * HP:]


def mla_pre_fwd(proj, gq, gkv, wq, wkv, place, cos_q, sin_q, cs_k, S):
    T = proj.shape[0]
    tm = _tile(S, 512)
    tpb = S // tm
    W = HEADS * HP

    def body(p_ref, gq_ref, gkv_ref, wq_ref, wkv_ref, pl_ref, cq_ref, sq_ref, ck_ref, q_ref, k_ref, v_ref):
        q, k, v = _mla_pre(p_ref[...], gq_ref[...], gkv_ref[...], wq_ref[...], wkv_ref[...], pl_ref[...],
                           cq_ref[...], sq_ref[...], ck_ref[...])
        q_ref[...] = q
        k_ref[...] = k
        v_ref[...] = v

    def full(a):
        return pl.BlockSpec(a.shape, lambda i: (0,) * a.ndim)

    tok = pl.BlockSpec((tm, W), lambda i: (i, 0))
    return pl.pallas_call(
        body, name="mla_pre_fwd", grid=(T // tm,),
        in_specs=[pl.BlockSpec((tm, W), lambda i: (i, C_B // W)), full(gq), full(gkv), full(wq), full(wkv), full(place),
                  pl.BlockSpec((tm, W), lambda i: (i % tpb, 0)), pl.BlockSpec((tm, W), lambda i: (i % tpb, 0)),
                  pl.BlockSpec((tm, HP), lambda i: (i % tpb, 0))],
        out_specs=[tok] * 3, out_shape=[_sds((T, W), F32)] * 3,
        compiler_params=_cparams(VMEM_BIG))(proj, gq, gkv, wq, wkv, place, cos_q, sin_q, cs_k)


def mla_pre_bwd(proj, gq, gkv, wq, wkv, place, cos_q, sin_q, cs_k, dq, dk, dv, S):
    T = proj.shape[0]
    tm = _tile(S, 512)
    tpb = S // tm
    W = HEADS * HP

    def body(p_ref, gq_ref, gkv_ref, wq_ref, wkv_ref, pl_ref, cq_ref, sq_ref, ck_ref, dq_ref, dk_ref, dv_ref,
             dp_ref, dgq_ref, dgkv_ref, dwq_ref, dwkv_ref):
        @pl.when(pl.program_id(0) == 0)
        def _():
            for r in (dgq_ref, dgkv_ref, dwq_ref, dwkv_ref):
                r[...] = jnp.zeros_like(r)

        f = functools.partial(_mla_pre, place=pl_ref[...], cos_q=cq_ref[...], sin_q=sq_ref[...], cs_k=ck_ref[...])
        _, vjp = jax.vjp(f, p_ref[...], gq_ref[...], gkv_ref[...], wq_ref[...], wkv_ref[...])
        dp, dgq, dgkv, dwq, dwkv = vjp((dq_ref[...], dk_ref[...], dv_ref[...]))
        dp_ref[...] = dp.astype(BF16)
        dgq_ref[...] += dgq
        dgkv_ref[...] += dgkv
        dwq_ref[...] += dwq
        dwkv_ref[...] += dwkv

    def full(a):
        return pl.BlockSpec(a.shape, lambda i: (0,) * a.ndim)

    tok = pl.BlockSpec((tm, W), lambda i: (i, 0))
    return pl.pallas_call(
        body, name="mla_pre_bwd", grid=(T // tm,),
        in_specs=[pl.BlockSpec((tm, W), lambda i: (i, C_B // W)), full(gq), full(gkv), full(wq), full(wkv), full(place),
                  pl.BlockSpec((tm, W), lambda i: (i % tpb, 0)), pl.BlockSpec((tm, W), lambda i: (i % tpb, 0)),
                  pl.BlockSpec((tm, HP), lambda i: (i % tpb, 0)), tok, tok, tok],
        out_specs=[tok, full(gq), full(gkv), full(wq), full(wkv)],
        out_shape=[_sds((T, W), BF16), _sds(gq.shape, F32), _sds(gkv.shape, F32), _sds(wq.shape, F32), _sds(wkv.shape, F32)],
        compiler_params=_cparams(VMEM_BIG))(proj, gq, gkv, wq, wkv, place, cos_q, sin_q, cs_k, dq, dk, dv)


def _gmlp_block(blk, lg, lb, ws, bs):
    u = jax.nn.gelu(blk[:, 0:GW])
    v = jax.nn.gelu(blk[:, GW:2 * GW])
    mu = jnp.mean(v, axis=-1, keepdims=True)
    var = jnp.mean(jnp.square(v - mu), axis=-1, keepdims=True)
    vn = (v - mu) * lax.rsqrt(var + LN_EPS) * lg + lb
    causal = _iota((D_CHUNK, D_CHUNK), 1) <= _iota((D_CHUNK, D_CHUNK), 0)
    group = _iota((1, GW), 1) // HD
    mixed = jnp.zeros((D_CHUNK, GW), F32)
    for g in range(HEADS):
        part = _mm(jnp.where(causal, ws[g], 0.0), vn) + bs[g]
        mixed = mixed + jnp.where(group == g, part, 0.0)
    return u * mixed


def gmlp_fwd(proj, lg, lb, ws, bs):
    T = proj.shape[0]

    def body(p_ref, lg_ref, lb_ref, ws_ref, bs_ref, o_ref):
        o_ref[...] = _gmlp_block(p_ref[...], lg_ref[...], lb_ref[...], ws_ref[...], bs_ref[...]).astype(BF16)

    def full(a):
        return pl.BlockSpec(a.shape, lambda i: (0,) * a.ndim)

    return pl.pallas_call(
        body, name="gmlp_fwd", grid=(T // D_CHUNK,),
        in_specs=[pl.BlockSpec((D_CHUNK, 2 * GW), lambda i: (i, C_D // (2 * GW))), full(lg), full(lb), full(ws), full(bs)],
        out_specs=pl.BlockSpec((D_CHUNK, GW), lambda i: (i, 0)), out_shape=_sds((T, GW), BF16),
        compiler_params=_cparams(VMEM_BIG))(proj, lg, lb, ws, bs)


def gmlp_bwd(proj, lg, lb, ws, bs, dcat):
    T = proj.shape[0]

    def body(p_ref, lg_ref, lb_ref, ws_ref, bs_ref, do_ref, dp_ref, dlg_ref, dlb_ref, dws_ref, dbs_ref):
        @pl.when(pl.program_id(0) == 0)
        def _():
            for r in (dlg_ref, dlb_ref, dws_ref, dbs_ref):
                r[...] = jnp.zeros_like(r)

        _, vjp = jax.vjp(_gmlp_block, p_ref[...], lg_ref[...], lb_ref[...], ws_ref[...], bs_ref[...])
        dp, dlg, dlb, dws, dbs = vjp(do_ref[...])
        dp_ref[...] = dp.astype(BF16)
        dlg_ref[...] += dlg
        dlb_ref[...] += dlb
        dws_ref[...] += dws
        dbs_ref[...] += dbs

    def full(a):
        return pl.BlockSpec(a.shape, lambda i: (0,) * a.ndim)

    return pl.pallas_call(
        body, name="gmlp_bwd", grid=(T // D_CHUNK,),
        in_specs=[pl.BlockSpec((D_CHUNK, 2 * GW), lambda i: (i, C_D // (2 * GW))), full(lg), full(lb), full(ws), full(bs),
                  pl.BlockSpec((D_CHUNK, GW), lambda i: (i, O_D // GW))],
        out_specs=[pl.BlockSpec((D_CHUNK, 2 * GW), lambda i: (i, 0)), full(lg), full(lb), full(ws), full(bs)],
        out_shape=[_sds((T, 2 * GW), BF16), _sds(lg.shape, F32), _sds(lb.shape, F32), _sds(ws.shape, F32), _sds(bs.shape, F32)],
        compiler_params=_cparams(VMEM_BIG))(proj, lg, lb, ws, bs, dcat)


def _lb_all(logits):
    m = jnp.max(logits, axis=0, keepdims=True)
    e = jnp.exp(logits - m)
    sm = e / jnp.sum(e, axis=0, keepdims=True)
    return jnp.concatenate([sm[0:1] - sm[0:1], (sm[0:1] + sm[1:2]) - sm[0:1]], axis=0)


def lb_fwd(logits):
    def body(l_ref, o_ref):
        o_ref[...] = _lb_all(l_ref[...])

    return pl.pallas_call(body, name="lb_fwd", out_shape=_sds(logits.shape, F32))(logits)


def lb_bwd(logits, dlb):
    def body(l_ref, d_ref, o_ref):
        _, vjp = jax.vjp(_lb_all, l_ref[...])
        o_ref[...] = vjp(d_ref[...])[0]

    return pl.pallas_call(body, name="lb_bwd", out_shape=_sds(logits.shape, F32))(logits, dlb)


def ada_grad(c_all, dmod_cols):
    N = dmod_cols.shape[1]
    tn = _tile(N, 1152)

    def body(c_ref, d_ref, o_ref):
        h = jax.nn.silu(c_ref[...]).astype(BF16)
        o_ref[...] = lax.dot_general(h, d_ref[...].astype(BF16), (((0,), (0,)), ((), ())), preferred_element_type=F32)

    nb = c_all.shape[0]
    return pl.pallas_call(
        body, name="ada_grad", grid=(N // tn,),
        in_specs=[pl.BlockSpec((nb, D), lambda j: (0, 0)), pl.BlockSpec((nb, tn), lambda j: (0, j))],
        out_specs=pl.BlockSpec((D, tn), lambda j: (0, j)), out_shape=_sds((D, N), F32),
        compiler_params=_cparams(VMEM_BIG))(c_all, dmod_cols)


def sum_slots(a, n, name):
    _, R, C = a.shape
    tr = _row_tile(R, C, n)

    def body(a_ref, o_ref):
        acc = a_ref[0]
        for k in range(1, n):
            acc = acc + a_ref[k]
        o_ref[...] = acc

    return pl.pallas_call(
        body, name=name, grid=(R // tr,),
        in_specs=[pl.BlockSpec((n, tr, C), lambda i: (0, i, 0))],
        out_specs=pl.BlockSpec((tr, C), lambda i: (i, 0)), out_shape=_sds((R, C), F32),
        compiler_params=_cparams(VMEM_BIG))(a)


def add2(a, b, name):
    shp = a.shape
    C = shp[-1]
    a2, b2 = a.reshape(-1, C), b.reshape(-1, C)
    R = a2.shape[0]
    tr = _row_tile(R, C)

    def body(a_ref, b_ref, o_ref):
        o_ref[...] = a_ref[...] + b_ref[...]

    spec = pl.BlockSpec((tr, C), lambda i: (i, 0))
    return pl.pallas_call(body, name=name, grid=(R // tr,), in_specs=[spec, spec], out_specs=spec,
                          out_shape=_sds((R, C), F32), compiler_params=_cparams(VMEM_BIG))(a2, b2).reshape(shp)


def _row_tile(R, C=D, n=1):
    limit = max(8, (1 << 18) // (C * n))
    for t in range(limit - limit % 8, 7, -8):
        if R % t == 0:
            return t
    return R


def adamw(w, g, m, v, name):
    R, C = w.shape
    tr = _row_tile(R, C)
    c1 = 1.0 - ADAM_B1 ** ADAM_STEP
    c2 = 1.0 - ADAM_B2 ** ADAM_STEP

    def body(w_ref, g_ref, m_ref, v_ref, d_ref, nm_ref, nv_ref):
        g_ = g_ref[...]
        nm = ADAM_B1 * m_ref[...] + (1.0 - ADAM_B1) * g_
        nv = ADAM_B2 * v_ref[...] + (1.0 - ADAM_B2) * jnp.square(g_)
        d_ref[...] = -ADAM_LR * ((nm / c1) / (jnp.sqrt(nv / c2) + ADAM_EPS) + ADAM_WD * w_ref[...])
        nm_ref[...] = nm
        nv_ref[...] = nv

    spec = pl.BlockSpec((tr, C), lambda i: (i, 0))
    return pl.pallas_call(body, name=name, grid=(R // tr,), in_specs=[spec] * 4, out_specs=[spec] * 3,
                          out_shape=[_sds((R, C), F32)] * 3, compiler_params=_cparams(VMEM_BIG))(w, g, m, v)


def _rot_cols(w):
    return jnp.concatenate([-w[:, 16:32], w[:, 0:16]], axis=1)


def _fold_rot(d):
    return jnp.concatenate([d[:, 16:32], -d[:, 0:16]], axis=1)


def _pad_heads(w, off, axis):
    parts = []
    for h in range(HEADS):
        piece = lax.slice_in_dim(w, off + HD * h, off + HD * (h + 1), axis=axis)
        parts += [piece, jnp.zeros_like(piece)]
    return parts


def _unpad_heads(d, off, axis):
    return [lax.slice_in_dim(d, off + HP * h, off + HP * h + HD, axis=axis) for h in range(HEADS)]


def mix_in_ext(w):
    z = lambda n: jnp.zeros((w.shape[0], n), w.dtype)
    kr = w[:, 1408:1440]
    cols = [w[:, 0:1408], kr, _rot_cols(kr), z(64)]
    cols += _pad_heads(w, 1440, 1) + _pad_heads(w, 1696, 1) + _pad_heads(w, 1952, 1)
    cols += [w[:, 2212:2724], w[:, 2208:2212], z(NP - C_CF - HEADS)]
    return jnp.concatenate(cols, axis=1)


def mix_in_unext(d):
    kr = d[:, 1408:1440] + _fold_rot(d[:, 1440:1472])
    cols = [d[:, 0:1408], kr] + _unpad_heads(d, C_CQ, 1) + _unpad_heads(d, C_CK, 1) + _unpad_heads(d, C_CV, 1)
    cols += [d[:, C_CF:C_CF + HEADS], d[:, C_D:C_D + 2 * GW]]
    return jnp.concatenate(cols, axis=1)


def mix_out_ext(w):
    return jnp.concatenate([w[0:GW]] + _pad_heads(w, GW, 0) + _pad_heads(w, 2 * GW, 0) + [w[3 * GW:4 * GW]], axis=0)


def mix_out_unext(d):
    return jnp.concatenate([d[0:GW]] + _unpad_heads(d, O_B, 0) + _unpad_heads(d, O_C, 0) + [d[O_D:O_D + GW]], axis=0)


def uq_ext(w):
    z = lambda n: jnp.zeros((w.shape[0], n), w.dtype)
    a, b = [], []
    for h in range(HEADS):
        o = (B_NOPE + B_ROPE) * h
        a += [w[:, o:o + B_NOPE + B_ROPE], z(32)]
        b += [z(B_NOPE), _rot_cols(w[:, o + B_NOPE:o + B_NOPE + B_ROPE]), z(32)]
    return jnp.concatenate(a + b, axis=1)


def uq_unext(d):
    cols = []
    for h in range(HEADS):
        o = HP * h
        cols += [d[:, o:o + B_NOPE], d[:, o + B_NOPE:o + B_NOPE + B_ROPE]
                 + _fold_rot(d[:, HEADS * HP + o + B_NOPE:HEADS * HP + o + B_NOPE + B_ROPE])]
    return jnp.concatenate(cols, axis=1)


def ukv_ext(w):
    z = jnp.zeros((w.shape[0], HD), w.dtype)
    k, v = [], []
    for h in range(HEADS):
        k += [w[:, 2 * HD * h:2 * HD * h + HD], z]
        v += [w[:, 2 * HD * h + HD:2 * HD * (h + 1)], z]
    return jnp.concatenate(k + v, axis=1)


def ukv_unext(d):
    cols = []
    for h in range(HEADS):
        cols += [d[:, HP * h:HP * h + HD], d[:, HEADS * HP + HP * h:HEADS * HP + HP * h + HD]]
    return jnp.concatenate(cols, axis=1)


def rope_tables(S):
    half = B_ROPE // 2
    inv_freq = ROPE_THETA ** (-jnp.arange(half, dtype=F32) / half)
    ang = jnp.arange(S).astype(F32)[:, None] * inv_freq[None, :]
    cos = jnp.tile(jnp.cos(ang), (1, 2))
    sin = jnp.tile(jnp.sin(ang), (1, 2))
    one, zero = jnp.ones((S, B_NOPE), F32), jnp.zeros((S, B_NOPE), F32)
    z32 = jnp.zeros((S, 32), F32)
    cos_q = jnp.tile(jnp.concatenate([one, cos, z32], axis=1), (1, HEADS))
    sin_q = jnp.tile(jnp.concatenate([zero, sin, z32], axis=1), (1, HEADS))
    cs_k = jnp.concatenate([cos, sin, zero], axis=1)
    place = np.zeros((HP, HEADS * HP), np.float32)
    for h in range(HEADS):
        for j in range(B_ROPE):
            place[j, h * HP + B_NOPE + j] = 1.0
            place[B_ROPE + j, h * HP + B_NOPE + j] = 1.0
    return cos_q, sin_q, cs_k, jnp.asarray(place, BF16)


def layer_fwd(x, mod, get, tabs, S):
    cos_q, sin_q, cs_k, place = tabs
    p = dict(get("ffn1", x))
    l = p["wl"]
    zg1, zu1, act1 = ffn_in_fwd(x, mod, p["ffn1_in"], l, 0, 1, S)
    y1, x1 = out_ln_fwd(act1, p["ffn1_out"], x, mod, p["ln_g"][0:1], p["ln_b"][0:1], 2, 0.5, S, l)
    p.update(get("mix", x1))
    proj = mix_in_fwd(x1, mod, p["mix_in"], 3, 4, S)
    o_a, states = hgrn_fwd(proj, p["lb"], p["ng"], S)
    q_b, k_b, v_b = mla_pre_fwd(proj, p["gq"], p["gkv"], p["wq"], p["wkv"], place, cos_q, sin_q, cs_k, S)
    o_b = attn_fwd(q_b, 0, k_b, 0, v_b, 0, None, (B_NOPE + B_ROPE) ** -0.5, S)
    cum = fox_gate_fwd(proj, p["bcol"], S)
    o_c = attn_fwd(proj, C_CQ // HP, proj, C_CK // HP, proj, C_CV // HP, cum, HD ** -0.5, S)
    o_d = gmlp_fwd(proj, p["g_lg"], p["g_lb"], p["ws"], p["bs"])
    cat = jnp.concatenate([o_a, o_b, o_c, o_d], axis=1)
    y2, x2 = out_ln_fwd(cat, p["mix_out"], x1, mod, p["ln_g"][1:2], p["ln_b"][1:2], 5, 1.0, S)
    p.update(get("ffn2", x2))
    zg3, zu3, act3 = ffn_in_fwd(x2, mod, p["ffn2_in"], l, 6, 7, S)
    y3, x3 = out_ln_fwd(act3, p["ffn2_out"], x2, mod, p["ln_g"][2:3], p["ln_b"][2:3], 8, 0.5, S, l)
    saved = dict(x=x, zg1=zg1, zu1=zu1, act1=act1, y1=y1, x1=x1, proj=proj, states=states, q_b=q_b, k_b=k_b, v_b=v_b,
                 cum=cum, cat=cat, y2=y2, x2=x2, zg3=zg3, zu3=zu3, act3=act3, y3=y3, p=p)
    return x3, saved


def _ffn_bwd(dxn, x_in, y, zg, zu, act, mod, w_in, w_out, l, lg, lb, idx, S):
    sh, sc, gate = idx
    dres, dy, dgate, dlg, dlb = ln_res_bwd(dxn, x_in, y, mod, lg, lb, gate, 0.5, S)
    dzg, dzu = swiglu_bwd(dy, w_out, l, zg, zu, S)
    dw_out = tn_mm(act, dy, FF // 2).reshape(N_CHIPS, FF // N_CHIPS, D)
    dw_in = tn_mm_mod_shards(x_in, mod, dzg, dzu, sh, sc, S)
    dx, dsh, dsc = nt_mod_bwd([dzg, dzu], w_in, [0, 2], x_in, mod, dres, sc, S, FF // 2, l)
    return dx, dw_in, dw_out, dlg, dlb, {sh: dsh, sc: dsc, gate: dgate}


def layer_bwd(dx3, mod, sv, tabs, S, emit):
    cos_q, sin_q, cs_k, place = tabs
    p = sv["p"]
    l = p["wl"]
    g = {}
    dm = {}
    dx2, g["ffn2_in"], g["ffn2_out"], dlg2, dlb2, d = _ffn_bwd(
        dx3, sv["x2"], sv["y3"], sv["zg3"], sv["zu3"], sv["act3"], mod, p["ffn2_in"], p["ffn2_out"], l,
        p["ln_g"][2:3], p["ln_b"][2:3], (6, 7, 8), S)
    dm.update(d)
    mod = emit("ffn2", g, mod)
    dres, dy2, dm[5], dlg1, dlb1 = ln_res_bwd(dx2, sv["x1"], sv["y2"], mod, p["ln_g"][1:2], p["ln_b"][1:2], 5, 1.0, S)
    dcat = nt_plain(dy2, p["mix_out"])
    g["mix_out"] = tn_mm(sv["cat"], dy2, 768)
    proj = sv["proj"]
    d_a, g["lb"], g["ng"] = hgrn_bwd(proj, sv["states"], dcat, p["lb"], p["ng"], S)
    dq_c, dk_c, dv_c, dcum = attn_bwd(proj, C_CQ // HP, proj, C_CK // HP, proj, C_CV // HP, sv["cum"], dcat,
                                      O_C // HP, HD ** -0.5, S, BF16)
    B = proj.shape[0] // S
    dcum = jnp.concatenate([dcum.reshape(B, HEADS, S), jnp.zeros((B, 8 - HEADS, S), F32)], axis=1)
    d_cf, g["bcol"] = fox_gate_bwd(proj, p["bcol"], dcum, S)
    dq_b, dk_b, dv_b = attn_bwd(sv["q_b"], 0, sv["k_b"], 0, sv["v_b"], 0, None, dcat, O_B // HP,
                                (B_NOPE + B_ROPE) ** -0.5, S, F32)
    d_b, g["gq"], g["gkv"], g["wq"], g["wkv"] = mla_pre_bwd(
        proj, p["gq"], p["gkv"], p["wq"], p["wkv"], place, cos_q, sin_q, cs_k, dq_b, dk_b, dv_b, S)
    d_d, g["g_lg"], g["g_lb"], g["ws"], g["bs"] = gmlp_bwd(proj, p["g_lg"], p["g_lb"], p["ws"], p["bs"], dcat)
    dproj = jnp.concatenate([d_a, d_b, dq_c, dk_c, dv_c, d_d, d_cf, jnp.zeros_like(d_cf)], axis=1)
    g["mix_in"] = tn_mm_mod(sv["x1"], mod, dproj, 3, 4, S, NP_TILE)
    dx1, dm[3], dm[4] = nt_mod_bwd([dproj], p["mix_in"], [0], sv["x1"], mod, dres, 4, S, NP_TILE)
    mod = emit("mix", g, mod)
    dx0, g["ffn1_in"], g["ffn1_out"], dlg0, dlb0, d = _ffn_bwd(
        dx1, sv["x"], sv["y1"], sv["zg1"], sv["zu1"], sv["act1"], mod, p["ffn1_in"], p["ffn1_out"], l,
        p["ln_g"][0:1], p["ln_b"][0:1], (0, 1, 2), S)
    dm.update(d)
    emit("ffn1", g, mod)
    g["ln_g"] = jnp.concatenate([dlg0, dlg1, dlg2], axis=0)
    g["ln_b"] = jnp.concatenate([dlb0, dlb1, dlb2], axis=0)
    dmod = jnp.concatenate([dm[i] for i in range(N_MOD)], axis=1)
    return dx0, dmod, g


def local_step(x, c8, tgt, get, lb_logits, S, emit=None):
    B = x.shape[0] // S
    tabs = rope_tables(S)
    lb_all = lb_fwd(lb_logits)
    mods, saved = [], []
    h = x
    for l in range(DEPTH):
        pa = get(l, "ada", h)
        mod = mod_fwd(c8, pa["ada_w"], pa["wl"], pa["ada_b"])[0:B].reshape(B, N_MOD, D)

        def get_l(part, after, l=l):
            p = dict(get(l, part, after))
            if part == "mix":
                p["lb"] = lb_all[l:l + 1]
            return p

        h, sv = layer_fwd(h, mod, get_l, tabs, S)
        mods.append(mod)
        saved.append(sv)
    loss_tile, dh = loss_head(h, tgt)
    grads, dmods, dlb = [None] * DEPTH, [None] * DEPTH, [None] * DEPTH
    for l in reversed(range(DEPTH)):
        emit_l = (lambda part, g, mod: mod) if emit is None else functools.partial(emit, l)
        dh, dmods[l], grads[l] = layer_bwd(dh, mods[l], saved[l], tabs, S, emit_l)
        dlb[l] = grads[l].pop("lb")
    d_logits = lb_bwd(lb_logits, jnp.concatenate(dlb, axis=0))
    return loss_tile, dh, dmods, grads, d_logits


ANY = pl.BlockSpec(memory_space=pl.ANY)


def _place():
    x, y, c = lax.axis_index("x"), lax.axis_index("y"), lax.axis_index("c")
    chips = [(1 - x, y), (x, 1 - y), (1 - x, 1 - y)]
    return x, y, c, chips


def _rcopy(src, dst, sems, k, to):
    send_sems, recv_sems = sems
    return pltpu.make_async_remote_copy(src_ref=src, dst_ref=dst, send_sem=send_sems.at[k], recv_sem=recv_sems.at[k],
                                        device_id=to, device_id_type=MESH)


def _dma_sems(n_remote, n_local):
    return [pltpu.SemaphoreType.DMA((n_remote,)), pltpu.SemaphoreType.DMA((n_remote,)), pltpu.SemaphoreType.DMA((n_local,))]


def ag_shards(arrs):
    n = len(arrs)
    rh = [a.shape[1] // 2 for a in arrs]

    def body(*refs):
        srcs, outs = refs[:n], refs[n:2 * n]
        send_sems, recv_sems, loc_sems = refs[2 * n:]
        x, y, c, chips = _place()
        sems = (send_sems, recv_sems)
        me = 2 * x + y
        sibling = (x, y, 1 - c)

        def part(i, k, hc):
            return outs[i].at[:, k, pl.ds(hc * rh[i], rh[i]), :]

        mine = [pltpu.make_async_copy(srcs[i], outs[i].at[:, me], loc_sems.at[i]) for i in range(n)]
        for cp in mine:
            cp.start()
        started = []
        for j, (px, py) in enumerate(chips):
            for i in range(n):
                cp = _rcopy(srcs[i].at[:, pl.ds(c * rh[i], rh[i]), :], part(i, me, c), sems, 6 * i + j, (px, py, c))
                cp.start()
                started.append(cp)
        for j, (px, py) in enumerate(chips):
            k = 2 * px + py
            for i in range(n):
                _rcopy(part(i, k, c), part(i, k, c), sems, 6 * i + j, (px, py, c)).wait_recv()
                cp = _rcopy(part(i, k, c), part(i, k, c), sems, 6 * i + 3 + j, sibling)
                cp.start()
                started.append(cp)
        for j, (px, py) in enumerate(chips):
            k = 2 * px + py
            for i in range(n):
                _rcopy(part(i, k, 1 - c), part(i, k, 1 - c), sems, 6 * i + 3 + j, sibling).wait_recv()
        for cp in started:
            cp.wait_send()
        for cp in mine:
            cp.wait()

    return pl.pallas_call(
        body, name="ag_shards", out_shape=[_sds((a.shape[0], N_CHIPS) + a.shape[1:], a.dtype) for a in arrs],
        in_specs=[ANY] * n, out_specs=[ANY] * n, scratch_shapes=_dma_sems(6 * n, n))(*arrs)


HBM_SPEC = pl.BlockSpec(memory_space=pltpu.HBM)
SEM_SPEC = pl.BlockSpec(memory_space=pltpu.SEMAPHORE)
DATAFLOW = pltpu.SideEffectType.DATAFLOW_SIDE_EFFECTING


def _after(x, dep):
    return lax.optimization_barrier((x, dep))[0]


def _split_start(srcs, lands, copies, name):
    n, m = len(srcs), len(lands)

    def body(*refs):
        ins = refs[:n + m]
        send_sems, recv_sems = refs[n + m], refs[n + m + 1]
        token = refs[-1]
        for k, (src, dst, to) in enumerate(copies(ins[:n], ins[n:], _place())):
            pltpu.make_async_remote_copy(src_ref=src, dst_ref=dst, send_sem=send_sems.at[k], recv_sem=recv_sems.at[k],
                                         device_id=to, device_id_type=MESH).start()
        token[...] = jnp.zeros_like(token)

    n_copies = 3 * n
    arrs = list(srcs) + list(lands)
    outs = pl.pallas_call(
        body, name=name,
        out_shape=(pltpu.SemaphoreType.DMA((n_copies,)), pltpu.SemaphoreType.DMA((n_copies,)),
                   *[pltpu.HBM(a.shape, a.dtype) for a in arrs], _sds((8, 128), F32)),
        in_specs=[HBM_SPEC] * (n + m),
        out_specs=(SEM_SPEC, SEM_SPEC, *[HBM_SPEC] * (n + m), pl.BlockSpec(memory_space=pltpu.VMEM)),
        input_output_aliases={i: 2 + i for i in range(n + m)},
        compiler_params=pltpu.CompilerParams(has_side_effects=DATAFLOW),
    )(*[pltpu.with_memory_space_constraint(a, pltpu.HBM) for a in arrs])
    return outs[0], outs[1], list(outs[2:2 + n]), list(outs[2 + n:2 + n + m]), outs[-1]


def _split_wait(handle, arrivals, after, name):
    send_sems, recv_sems, srcs, lands, _ = handle
    n, m = len(srcs), len(lands)

    def body(*refs):
        ins = refs[:n + m]
        send_sems, recv_sems = refs[n + m], refs[n + m + 1]
        x, y, c, chips = place = _place()
        for k, (src, dst) in enumerate(arrivals(ins[:n], ins[n:], place)):
            cp = pltpu.make_async_remote_copy(src_ref=src, dst_ref=dst, send_sem=send_sems.at[k], recv_sem=recv_sems.at[k],
                                              device_id=(x, y, 1 - c), device_id_type=MESH)
            cp.wait_send()
            cp.wait_recv()

    arrs = list(srcs) + list(lands)
    outs = pl.pallas_call(
        body, name=name, out_shape=[pltpu.HBM(a.shape, a.dtype) for a in arrs],
        in_specs=[HBM_SPEC] * (n + m) + [SEM_SPEC, SEM_SPEC, ANY], out_specs=[HBM_SPEC] * (n + m),
        input_output_aliases={i: i for i in range(n + m)},
        compiler_params=pltpu.CompilerParams(has_side_effects=DATAFLOW),
    )(*arrs, send_sems, recv_sems, after)
    return list(outs[n:])


def _ag_part(ref, k, hc):
    rh = ref.shape[2] // 2
    return ref.at[:, k, pl.ds(hc * rh, rh), :]


def ag_start(srcs, lands, name):
    def copies(s, d, place):
        x, y, c, chips = place
        out = []
        for j, (px, py) in enumerate(chips):
            for i in range(len(s)):
                rh = s[i].shape[1] // 2
                out.append((s[i].at[:, pl.ds(c * rh, rh), :], _ag_part(d[i], 2 * x + y, c), (px, py, c)))
        return out

    return _split_start(srcs, lands, copies, name)


def ag_wait(handle, after, name):
    def arrivals(s, d, place):
        x, y, c, chips = place
        out = []
        for j, (px, py) in enumerate(chips):
            for i in range(len(s)):
                rh = s[i].shape[1] // 2
                out.append((s[i].at[:, pl.ds(c * rh, rh), :], _ag_part(d[i], 2 * px + py, c)))
        return out

    return _split_wait(handle, arrivals, after, name)


def ag_forward(lands, name):
    n = len(lands)

    def body(*refs):
        bufs = refs[n:2 * n]
        send_sems, recv_sems = refs[2 * n:]
        x, y, c, chips = _place()
        sems = (send_sems, recv_sems)
        cps = []
        for j, (px, py) in enumerate(chips):
            for i in range(n):
                part = _ag_part(bufs[i], 2 * px + py, c)
                cps.append(_rcopy(part, part, sems, 3 * i + j, (x, y, 1 - c)))
        for cp in cps:
            cp.start()
        for j, (px, py) in enumerate(chips):
            for i in range(n):
                part = _ag_part(bufs[i], 2 * px + py, 1 - c)
                _rcopy(part, part, sems, 3 * i + j, (x, y, 1 - c)).wait_recv()
        for cp in cps:
            cp.wait_send()

    return pl.pallas_call(
        body, name=name, out_shape=[_sds(a.shape, a.dtype) for a in lands], in_specs=[ANY] * n, out_specs=[ANY] * n,
        input_output_aliases={i: i for i in range(n)}, scratch_shapes=_dma_sems(3 * n, 1)[:2])(*lands)


def rs_start(hs, lands, name):
    def copies(s, d, place):
        x, y, c, chips = place
        return [(s[i].at[2 * px + py], d[i].at[2 * x + y], (px, py, c)) for j, (px, py) in enumerate(chips) for i in range(len(s))]

    return _split_start(hs, lands, copies, name)


def rs_wait(handle, after, name):
    def arrivals(s, d, place):
        x, y, c, chips = place
        return [(s[i].at[2 * px + py], d[i].at[2 * px + py]) for j, (px, py) in enumerate(chips) for i in range(len(s))]

    return _split_wait(handle, arrivals, after, name)


def sibling_swap(arrs, name):
    n = len(arrs)
    rh = [a.shape[1] // 2 for a in arrs]

    def body(*refs):
        srcs, outs = refs[:n], refs[n:2 * n]
        send_sems, recv_sems = refs[2 * n:]
        x, y, c, _ = _place()
        cps = [_rcopy(srcs[i].at[:, pl.ds((1 - c) * rh[i], rh[i]), :], outs[i], (send_sems, recv_sems), i, (x, y, 1 - c))
               for i in range(n)]
        for cp in cps:
            cp.start()
        for cp in cps:
            cp.wait()

    return pl.pallas_call(
        body, name=name, out_shape=[_sds((N_CHIPS, r, a.shape[2]), a.dtype) for a, r in zip(arrs, rh)],
        in_specs=[ANY] * n, out_specs=[ANY] * n, scratch_shapes=_dma_sems(n, 1)[:2])(*arrs)


def chip_exchange(hs):
    n = len(hs)

    def body(*refs):
        srcs, outs = refs[:n], refs[n:2 * n]
        send_sems, recv_sems, loc_sems = refs[2 * n:]
        x, y, c, chips = _place()
        sems = (send_sems, recv_sems)
        me = 2 * x + y
        mine = [pltpu.make_async_copy(srcs[i].at[me], outs[i].at[me], loc_sems.at[i]) for i in range(n)]
        for cp in mine:
            cp.start()
        sends = []
        for j, (px, py) in enumerate(chips):
            for i in range(n):
                cp = _rcopy(srcs[i].at[2 * px + py], outs[i].at[me], sems, 3 * i + j, (px, py, c))
                cp.start()
                sends.append(cp)
        for j, (px, py) in enumerate(chips):
            for i in range(n):
                _rcopy(srcs[i].at[2 * px + py], outs[i].at[2 * px + py], sems, 3 * i + j, (px, py, c)).wait_recv()
        for cp in sends:
            cp.wait_send()
        for cp in mine:
            cp.wait()

    return pl.pallas_call(
        body, name="chip_exchange", out_shape=[_sds(h.shape, h.dtype) for h in hs],
        in_specs=[ANY] * n, out_specs=[ANY] * n, scratch_shapes=_dma_sems(3 * n, n))(*hs)


def sibling_join(reds):
    L, n = len(reds), len(reds[0])
    rh = [r.shape[0] for r in reds[0]]
    flat = [r for layer in reds for r in layer]

    def body(*refs):
        srcs, outs = refs[:L * n], refs[L * n:L * n + n]
        send_sems, recv_sems, loc_sems = refs[L * n + n:]
        x, y, c, _ = _place()
        sems = (send_sems, recv_sems)
        mine, sends = [], []
        for l in range(L):
            for i in range(n):
                k = l * n + i
                dst = outs[i].at[l, pl.ds(c * rh[i], rh[i]), :]
                mine.append(pltpu.make_async_copy(srcs[k], dst, loc_sems.at[k]))
                sends.append(_rcopy(srcs[k], dst, sems, k, (x, y, 1 - c)))
        for cp in mine + sends:
            cp.start()
        for l in range(L):
            for i in range(n):
                k = l * n + i
                _rcopy(srcs[k], outs[i].at[l, pl.ds((1 - c) * rh[i], rh[i]), :], sems, k, (x, y, 1 - c)).wait_recv()
        for cp in sends:
            cp.wait_send()
        for cp in mine:
            cp.wait()

    return pl.pallas_call(
        body, name="sibling_join", out_shape=[_sds((L, 2 * r.shape[0], r.shape[1]), r.dtype) for r in reds[0]],
        in_specs=[ANY] * (L * n), out_specs=[ANY] * n, scratch_shapes=_dma_sems(L * n, L * n))(*flat)


def ag_all(blk):
    M, C = blk.shape

    def body(x_ref, out_ref, send_sems, recv_sems, loc_sem):
        x, y, c, chips = _place()
        sems = (send_sems, recv_sems)
        me, sibling = (x, y, c), (x, y, 1 - c)

        def slot(px, py, pc):
            return out_ref.at[4 * px + 2 * py + pc]

        mine = pltpu.make_async_copy(x_ref, slot(*me), loc_sem)
        mine.start()
        first = [_rcopy(x_ref, slot(*me), sems, 0, sibling)]
        first += [_rcopy(x_ref, slot(*me), sems, 1 + j, (*chip, c)) for j, chip in enumerate(chips)]
        for cp in first:
            cp.start()
        passed = [_rcopy(slot(*chip, c), slot(*chip, c), sems, 4 + j, sibling) for j, chip in enumerate(chips)]
        for j, chip in enumerate(chips):
            _rcopy(slot(*chip, c), slot(*chip, c), sems, 1 + j, me).wait_recv()
            passed[j].start()
        _rcopy(slot(*sibling), slot(*sibling), sems, 0, me).wait_recv()
        for j, chip in enumerate(chips):
            _rcopy(slot(*chip, 1 - c), slot(*chip, 1 - c), sems, 4 + j, me).wait_recv()
        for cp in first + passed:
            cp.wait_send()
        mine.wait()

    return pl.pallas_call(
        body, name="ag_all", out_shape=_sds((8, M, C), blk.dtype),
        in_specs=[pl.BlockSpec(memory_space=pltpu.VMEM)], out_specs=pl.BlockSpec(memory_space=pltpu.VMEM),
        scratch_shapes=[pltpu.SemaphoreType.DMA((7,)), pltpu.SemaphoreType.DMA((7,)), pltpu.SemaphoreType.DMA(())],
        compiler_params=_cparams(VMEM_BIG))(blk)


WEIGHTS = ["ada_w", "ada_b", "ln_g", "ln_b", "ffn1_w_in", "ffn1_w_out", "ffn2_w_in", "ffn2_w_out", "mix_w_in", "mix_w_out",
           "hgrn_lb_logits", "hgrn_norm_g", "mla_q_norm_g", "mla_kv_norm_g", "mla_w_uq", "mla_w_ukv", "fox_b_f",
           "gmlp_ln_g", "gmlp_ln_b", "gmlp_w_s", "gmlp_b_s"]
SHARDED = {"ffn1_w_in": 1, "ffn1_w_out": 0, "ffn2_w_in": 1, "ffn2_w_out": 0, "mix_w_in": 1, "mix_w_out": 0,
           "mla_w_uq": 1, "mla_w_ukv": 1}
SMALL = ["hgrn_lb_logits", "hgrn_norm_g", "mla_q_norm_g", "mla_kv_norm_g", "fox_b_f", "gmlp_ln_g", "gmlp_ln_b",
         "gmlp_w_s", "gmlp_b_s", "ln_g", "ln_b"]
GATHERED = ["ada_w", "ffn1_w_in", "ffn1_w_out", "ffn2_w_in", "ffn2_w_out", "mix_w_in", "mix_w_out", "mla_w_uq", "mla_w_ukv"]
REDUCED = GATHERED[1:]


def _col_shards(a):
    cols = a.shape[1] // N_CHIPS
    return jnp.stack([a[:, k * cols:(k + 1) * cols] for k in range(N_CHIPS)])


def add_kept_half(a, got, core, name):
    _, R, C = a.shape
    rh = R // 2
    tr = _row_tile(rh, C)
    nr = rh // tr

    def body(core_ref, a_ref, b_ref, o_ref):
        o_ref[...] = a_ref[...] + b_ref[...]

    half = pl.BlockSpec((None, tr, C), lambda k, r, core_ref: (k, r, 0))
    grid_spec = pltpu.PrefetchScalarGridSpec(
        num_scalar_prefetch=1, grid=(N_CHIPS, nr),
        in_specs=[pl.BlockSpec((None, tr, C), lambda k, r, core_ref: (k, core_ref[0] * nr + r, 0)), half],
        out_specs=half)
    return pl.pallas_call(body, name=name, grid_spec=grid_spec, out_shape=_sds((N_CHIPS, rh, C), F32),
                          compiler_params=_cparams(VMEM_BIG))(core.reshape(1).astype(jnp.int32), a, got)


def _rows(parts, n_rows, dtype):
    flat = jnp.concatenate([p.reshape(-1) for p in parts])
    pad = n_rows * D - flat.shape[0]
    return jnp.concatenate([flat, jnp.zeros((pad,), dtype)]).reshape(n_rows, D)


def _take(flat, shapes):
    out, o = [], 0
    for shp in shapes:
        n = int(np.prod(shp))
        out.append(flat[o:o + n].reshape(shp))
        o += n
    return out


def _round_up(n, m):
    return -(-n // m) * m


def pack_shard(w):
    parts = [w[n][l] for l in range(DEPTH) for n in SHARDED] + [w[n][l] for l in range(DEPTH) for n in ("ln_g", "ln_b")]
    n = sum(int(np.prod(p.shape)) for p in parts)
    return _rows(parts, _round_up(-(-n // D), 16), F32)


def unpack_shard(pk, like):
    shapes = [like[n].shape[1:] for l in range(DEPTH) for n in SHARDED] + [like[n].shape[1:] for l in range(DEPTH) for n in ("ln_g", "ln_b")]
    pieces = _take(pk.reshape(-1), shapes)
    names = [n for l in range(DEPTH) for n in SHARDED] + [n for l in range(DEPTH) for n in ("ln_g", "ln_b")]
    out = {}
    for n in list(SHARDED) + ["ln_g", "ln_b"]:
        out[n] = jnp.stack([p for p, m in zip(pieces, names) if m == n])
    return out


def pack_small(w):
    parts = [w[n][l] for l in range(DEPTH) for n in SMALL]
    n = sum(int(np.prod(p.shape)) for p in parts)
    return _rows(parts, _round_up(-(-n // D), 8), F32)


def unpack_small(pk, like):
    shapes = [like[n].shape[1:] for l in range(DEPTH) for n in SMALL]
    pieces = _take(pk.reshape(-1), shapes)
    names = [n for l in range(DEPTH) for n in SMALL]
    return {n: jnp.stack([p for p, m in zip(pieces, names) if m == n]) for n in SMALL}


def pack_gather(w):
    parts = [w[n][l].astype(BF16) for l in range(DEPTH) for n in ["ada_w"] + list(SHARDED)]
    ln = jnp.concatenate([w[n][l].reshape(-1) for l in range(DEPTH) for n in ("ln_g", "ln_b")])
    parts.append(lax.bitcast_convert_type(ln, BF16))
    n = sum(int(np.prod(p.shape)) for p in parts)
    return _rows(parts, _round_up(-(-n // D), 16), BF16)


def unpack_gather(g, w):
    names = ["ada_w"] + list(SHARDED)
    shapes = [w[n].shape[1:] for l in range(DEPTH) for n in names]
    n_ln = DEPTH * 2 * 3 * (D // N_CHIPS)
    flat = g.reshape(N_CHIPS, -1)
    per_chip = [_take(flat[k], shapes + [(n_ln, 2)]) for k in range(N_CHIPS)]
    layers = [dict() for _ in range(DEPTH)]
    i = 0
    for l in range(DEPTH):
        for n in names:
            axis = 1 if n == "ada_w" else SHARDED[n]
            layers[l][n] = jnp.concatenate([per_chip[k][i] for k in range(N_CHIPS)], axis=axis)
            i += 1
    ln = [lax.bitcast_convert_type(per_chip[k][i], F32).reshape(DEPTH, 2, 3, D // N_CHIPS) for k in range(N_CHIPS)]
    ln = jnp.concatenate(ln, axis=3)
    for l in range(DEPTH):
        layers[l]["ln_g"], layers[l]["ln_b"] = ln[l, 0], ln[l, 1]
    return layers


def pack_grads(grads, k):
    parts = []
    for l in range(DEPTH):
        g = grads[l]
        full = {"ffn1_w_out": g["ffn1_out"], "ffn2_w_out": g["ffn2_out"], "mix_w_in": mix_in_unext(g["mix_in"]),
                "mix_w_out": mix_out_unext(g["mix_out"]), "mla_w_uq": uq_unext(g["wq"]), "mla_w_ukv": ukv_unext(g["wkv"])}
        for n, axis in SHARDED.items():
            if n in ("ffn1_w_in", "ffn2_w_in"):
                half = g[n.replace("_w_in", "_in")][k // 2]
                parts.append(half[:, (k % 2) * (FF // 2):(k % 2 + 1) * (FF // 2)])
            else:
                sz = full[n].shape[axis] // N_CHIPS
                parts.append(lax.slice_in_dim(full[n], k * sz, (k + 1) * sz, axis=axis))
    for l in range(DEPTH):
        for n in ("ln_g", "ln_b"):
            parts.append(grads[l][n][:, k * (D // N_CHIPS):(k + 1) * (D // N_CHIPS)])
    n = sum(int(np.prod(p.shape)) for p in parts)
    return _rows(parts, _round_up(-(-n // D), 16), F32)


def kernel(x, c, ada_w, ada_b, ln_g, ln_b, ffn1_w_in, ffn1_w_out, ffn2_w_in, ffn2_w_out, mix_w_in, mix_w_out, hgrn_lb_logits, hgrn_norm_g, mla_q_norm_g, mla_kv_norm_g, mla_w_uq, mla_w_ukv, fox_b_f, gmlp_ln_g, gmlp_ln_b, gmlp_w_s, gmlp_b_s, loss_target, m_ada_w, m_ada_b, m_ln_g, m_ln_b, m_ffn1_w_in, m_ffn1_w_out, m_ffn2_w_in, m_ffn2_w_out, m_mix_w_in, m_mix_w_out, m_hgrn_lb_logits, m_hgrn_norm_g, m_mla_q_norm_g, m_mla_kv_norm_g, m_mla_w_uq, m_mla_w_ukv, m_fox_b_f, m_gmlp_ln_g, m_gmlp_ln_b, m_gmlp_w_s, m_gmlp_b_s, v_ada_w, v_ada_b, v_ln_g, v_ln_b, v_ffn1_w_in, v_ffn1_w_out, v_ffn2_w_in, v_ffn2_w_out, v_mix_w_in, v_mix_w_out, v_hgrn_lb_logits, v_hgrn_norm_g, v_mla_q_norm_g, v_mla_kv_norm_g, v_mla_w_uq, v_mla_w_ukv, v_fox_b_f, v_gmlp_ln_g, v_gmlp_ln_b, v_gmlp_w_s, v_gmlp_b_s):
    w = dict(zip(WEIGHTS, (ada_w, ada_b, ln_g, ln_b, ffn1_w_in, ffn1_w_out, ffn2_w_in, ffn2_w_out, mix_w_in, mix_w_out, hgrn_lb_logits, hgrn_norm_g, mla_q_norm_g, mla_kv_norm_g, mla_w_uq, mla_w_ukv, fox_b_f, gmlp_ln_g, gmlp_ln_b, gmlp_w_s, gmlp_b_s)))
    m = dict(zip(WEIGHTS, (m_ada_w, m_ada_b, m_ln_g, m_ln_b, m_ffn1_w_in, m_ffn1_w_out, m_ffn2_w_in, m_ffn2_w_out, m_mix_w_in, m_mix_w_out, m_hgrn_lb_logits, m_hgrn_norm_g, m_mla_q_norm_g, m_mla_kv_norm_g, m_mla_w_uq, m_mla_w_ukv, m_fox_b_f, m_gmlp_ln_g, m_gmlp_ln_b, m_gmlp_w_s, m_gmlp_b_s)))
    v = dict(zip(WEIGHTS, (v_ada_w, v_ada_b, v_ln_g, v_ln_b, v_ffn1_w_in, v_ffn1_w_out, v_ffn2_w_in, v_ffn2_w_out, v_mix_w_in, v_mix_w_out, v_hgrn_lb_logits, v_hgrn_norm_g, v_mla_q_norm_g, v_mla_kv_norm_g, v_mla_w_uq, v_mla_w_ukv, v_fox_b_f, v_gmlp_ln_g, v_gmlp_ln_b, v_gmlp_w_s, v_gmlp_b_s)))
    Bl, S, _ = x.shape
    T = Bl * S
    core = lax.axis_index("c")
    chip = 2 * lax.axis_index("x") + lax.axis_index("y")

    def shard(key):
        n, l = key
        if n == "ln":
            return jnp.concatenate([ln_g[l:l + 1], ln_b[l:l + 1], jnp.zeros((1, 2, D // N_CHIPS), F32)], axis=1)
        return w[n][l:l + 1].astype(BF16)

    mixers = ["mix_w_in", "mix_w_out", "mla_w_uq", "mla_w_ukv"]
    groups = [[("ada_w", 0), ("ffn1_w_in", 0), ("ffn1_w_out", 0), ("ln", 0)],
              [(n, 0) for n in mixers + ["ffn2_w_in", "ffn2_w_out"]],
              [(n, 1) for n in GATHERED + ["ln"]]]
    first = ag_shards([shard(k) for k in groups[0]])
    have = dict(zip(groups[0], first))
    handles = {}
    for gi in (1, 2):
        srcs = _after([shard(k) for k in groups[gi]], first[0])
        lands = [lax.dynamic_update_slice(jnp.zeros((1, N_CHIPS) + s.shape[1:], s.dtype), s[:, None], (0, chip, 0, 0))
                 for s in srcs]
        handles[gi] = ag_start(srcs, lands, "ag_start_%d" % gi)
    c8 = jnp.concatenate([c, jnp.zeros((8 - Bl, D), F32)], axis=0)
    c8 = _after(c8, (handles[1][-1], handles[2][-1]))

    def cat_cols(a):
        return jnp.concatenate([a[0, k] for k in range(N_CHIPS)], axis=1)

    def get(l, part, after):
        gi = 2 if l == 1 else (0 if part in ("ada", "ffn1") else 1)
        if gi in handles:
            arrived = ag_forward(ag_wait(handles.pop(gi), after, "ag_wait_%d" % gi), "ag_forward_%d" % gi)
            have.update(zip(groups[gi], arrived))
        if part == "ada":
            return dict(ada_w=have[("ada_w", l)], wl=0, ada_b=ada_b[l][None])
        if part == "ffn1":
            ln_full = jnp.moveaxis(have[("ln", l)][0], 0, 1).reshape(8, D)
            return dict(ffn1_in=have[("ffn1_w_in", l)], ffn1_out=have[("ffn1_w_out", l)], wl=0,
                        ln_g=ln_full[0:3], ln_b=ln_full[3:6])
        if part == "ffn2":
            return dict(ffn2_in=have[("ffn2_w_in", l)], ffn2_out=have[("ffn2_w_out", l)])
        return dict(
            mix_in=mix_in_ext(cat_cols(have[("mix_w_in", l)])), mix_out=mix_out_ext(have[("mix_w_out", l)].reshape(D, D)),
            wq=uq_ext(cat_cols(have[("mla_w_uq", l)])).astype(F32), wkv=ukv_ext(cat_cols(have[("mla_w_ukv", l)])).astype(F32),
            ng=hgrn_norm_g[l][None], gq=mla_q_norm_g[l][None], gkv=mla_kv_norm_g[l][None],
            bcol=jnp.concatenate([fox_b_f[l], jnp.zeros((8 - HEADS,), F32)])[:, None],
            g_lg=gmlp_ln_g[l][None], g_lb=gmlp_ln_b[l][None], ws=gmlp_w_s[l], bs=gmlp_b_s[l][:, :, None])

    pending = []

    def emit(l, part, g, mod):
        if part == "mix":
            names = mixers
            by_chip = [_col_shards(mix_in_unext(g["mix_in"])), mix_out_unext(g["mix_out"]).reshape(N_CHIPS, D // N_CHIPS, D),
                       _col_shards(uq_unext(g["wq"])), _col_shards(ukv_unext(g["wkv"]))]
        else:
            names = [part + "_w_in", part + "_w_out"]
            by_chip = [g[part + "_in"], g[part + "_out"]]
        tag = "%d_%s" % (l, part)
        got = sibling_swap(by_chip, "sibling_swap_" + tag)
        chip_sum = [add_kept_half(a, r, core, "add_sibling") for a, r in zip(by_chip, got)]
        slot = lax.broadcasted_iota(jnp.int32, (N_CHIPS, 1, 1), 0)
        lands = [jnp.where(slot == chip, h, 0.0) for h in chip_sum]
        handle = rs_start(chip_sum, lands, "rs_start_" + tag)
        pending.append((l, names, handle, tag))
        return _after(mod, handle[-1])

    loss_tile, dx, dmods, grads, d_logits = local_step(
        x.reshape(T, D), c8, loss_target.reshape(T, D), get, hgrn_lb_logits, S, emit)
    loss = lax.psum(loss_tile[0, 0], ("x", "y", "c"))

    small_g = {"hgrn_lb_logits": d_logits,
               "hgrn_norm_g": jnp.stack([grads[l]["ng"][0] for l in range(DEPTH)]),
               "mla_q_norm_g": jnp.stack([grads[l]["gq"][0] for l in range(DEPTH)]),
               "mla_kv_norm_g": jnp.stack([grads[l]["gkv"][0] for l in range(DEPTH)]),
               "fox_b_f": jnp.stack([grads[l]["bcol"][0:HEADS, 0] for l in range(DEPTH)]),
               "gmlp_ln_g": jnp.stack([grads[l]["g_lg"][0] for l in range(DEPTH)]),
               "gmlp_ln_b": jnp.stack([grads[l]["g_lb"][0] for l in range(DEPTH)]),
               "gmlp_w_s": jnp.stack([grads[l]["ws"] for l in range(DEPTH)]),
               "gmlp_b_s": jnp.stack([grads[l]["bs"][:, :, 0] for l in range(DEPTH)])}
    small_g["ln_g"] = jnp.stack([grads[l]["ln_g"] for l in range(DEPTH)])
    small_g["ln_b"] = jnp.stack([grads[l]["ln_b"] for l in range(DEPTH)])
    pk_small = pack_small(small_g)
    n_small = pk_small.shape[0]
    extras = [dmods[l] for l in range(DEPTH)] + [c]
    n_extra = _round_up(-(-sum(int(np.prod(e.shape)) for e in extras) // D), 8)
    gathered = ag_all(jnp.concatenate([pk_small, _rows(extras, n_extra, F32)], axis=0))
    g_small = unpack_small(sum_slots(gathered[:, 0:n_small], 8, "sum_small"), small_g)
    ext = gathered[:, n_small:].reshape(8, -1)
    n_dmod = DEPTH * Bl * N_MOD * D
    dmod_all = ext[:, 0:n_dmod].reshape(8, DEPTH, Bl, N_MOD * D)
    c_all = ext[:, n_dmod:n_dmod + Bl * D].reshape(8 * Bl, D)
    g_ada_w, g_ada_b = [], []
    ncol = N_MOD * D // N_CHIPS
    for l in range(DEPTH):
        dm = dmod_all[:, l].reshape(8 * Bl, N_MOD * D)
        g_ada_w.append(ada_grad(c_all, lax.dynamic_slice_in_dim(dm, chip * ncol, ncol, axis=1)))
        g_ada_b.append(sum_slots(dm.reshape(8 * Bl, N_MOD, D), 8 * Bl, "sum_ada_b").reshape(N_MOD * D))
    g_ada_w, g_ada_b = jnp.stack(g_ada_w), jnp.stack(g_ada_b)

    red = [dict() for _ in range(DEPTH)]
    for l, names, handle, tag in pending:
        for n, land in zip(names, rs_wait(handle, dx, "rs_wait_" + tag)):
            red[l][n] = sum_slots(land, N_CHIPS, "sum_chips")
    g_shard = dict(zip(REDUCED, sibling_join([[red[l][n] for n in REDUCED] for l in range(DEPTH)])))

    grad = dict(g_shard)
    grad.update(g_small)
    grad["ada_w"], grad["ada_b"] = g_ada_w, g_ada_b
    for n in ("ln_g", "ln_b"):
        grad[n] = lax.dynamic_slice_in_dim(g_small[n], chip * (D // N_CHIPS), D // N_CHIPS, axis=2)
    out = {"grad": grad, "delta": {}, "new_m": {}, "new_v": {}}
    for n in WEIGHTS:
        shp = w[n].shape
        two_d = (-1, shp[-1])
        res = adamw(w[n].reshape(two_d), grad[n].reshape(two_d), m[n].reshape(two_d), v[n].reshape(two_d), "adamw_" + n)
        grad[n] = grad[n].reshape(shp)
        for key, r in zip(("delta", "new_m", "new_v"), res):
            out[key][n] = r.reshape(shp)
    outs = [loss, dx.reshape(Bl, S, D)]
    for key in ("grad", "delta", "new_m", "new_v"):
        outs += [out[key][n] for n in WEIGHTS]
    return tuple(outs)
```

```python
import functools

import jax
import jax.numpy as jnp
import numpy as np
from jax import lax
from jax.experimental import pallas as pl
from jax.experimental.pallas import tpu as pltpu

F32, BF16 = jnp.float32, jnp.bfloat16
MESH = pl.DeviceIdType.MESH

N_CHIPS = 4
D = 1024
DEPTH = 2
FF = 2816
N_MOD = 9
GW = 256
HEADS = 4
HD = 64
HP = 128
A_CHUNK = 16
LB_FLOOR = 1e-30
B_Q_LORA, B_KV_LORA, B_NOPE, B_ROPE = 256, 128, 64, 32
ROPE_THETA = 10000.0
D_CHUNK = 128
MIX_COLS = 2724
ALPHA = (2 * DEPTH) ** 0.25
LN_EPS = 1e-5
RMS_EPS = 1e-6
ADAM_LR, ADAM_B1, ADAM_B2, ADAM_EPS, ADAM_WD, ADAM_STEP = 0.001, 0.9, 0.999, 1e-08, 0.01, 10

NP = 3840
NP_TILE = 1920
C_A, C_B, C_CQ, C_CK, C_CV, C_D, C_CF = 0, 1024, 1536, 2048, 2560, 3072, 3584
NCAT = 1536
O_A, O_B, O_C, O_D = 0, 256, 768, 1280

VMEM_BIG = 48 << 20


def _cparams(vmem=None):
    return pltpu.CompilerParams(vmem_limit_bytes=vmem) if vmem else pltpu.CompilerParams()


def _sds(shape, dtype):
    return jax.ShapeDtypeStruct(tuple(shape), dtype)


@jax.custom_vjp
def _mm(a, w):
    return jnp.dot(a.astype(BF16), w.astype(BF16), preferred_element_type=F32)


def _mm_f(a, w):
    return _mm(a, w), (a, w)


def _mm_b(res, g):
    a, w = res
    gb = g.astype(BF16)
    da = lax.dot_general(gb, w.astype(BF16), (((1,), (1,)), ((), ())), preferred_element_type=F32)
    dw = lax.dot_general(a.astype(BF16), gb, (((0,), (0,)), ((), ())), preferred_element_type=F32)
    return da.astype(a.dtype), dw.astype(w.dtype)


_mm.defvjp(_mm_f, _mm_b)


@jax.custom_vjp
def _mm_nt(a, b):
    return lax.dot_general(a.astype(BF16), b.astype(BF16), (((1,), (1,)), ((), ())), preferred_element_type=F32)


def _mm_nt_f(a, b):
    return _mm_nt(a, b), (a, b)


def _mm_nt_b(res, g):
    a, b = res
    gb = g.astype(BF16)
    da = jnp.dot(gb, b.astype(BF16), preferred_element_type=F32)
    db = lax.dot_general(gb, a.astype(BF16), (((0,), (0,)), ((), ())), preferred_element_type=F32)
    return da.astype(a.dtype), db.astype(b.dtype)


_mm_nt.defvjp(_mm_nt_f, _mm_nt_b)


@jax.custom_vjp
def _mm_tn(a, b):
    return lax.dot_general(a.astype(BF16), b.astype(BF16), (((0,), (0,)), ((), ())), preferred_element_type=F32)


def _mm_tn_f(a, b):
    return _mm_tn(a, b), (a, b)


def _mm_tn_b(res, g):
    a, b = res
    gb = g.astype(BF16)
    da = lax.dot_general(b.astype(BF16), gb, (((1,), (1,)), ((), ())), preferred_element_type=F32)
    db = jnp.dot(a.astype(BF16), gb, preferred_element_type=F32)
    return da.astype(a.dtype), db.astype(b.dtype)


_mm_tn.defvjp(_mm_tn_f, _mm_tn_b)


def _mm_hi(a, w):
    return jnp.dot(a, w, precision=lax.Precision.HIGHEST, preferred_element_type=F32)


def _iota(shape, dim):
    return lax.broadcasted_iota(jnp.int32, shape, dim)


def _head_sum_mats():
    e = (_iota((GW, HP), 0) // HD == _iota((GW, HP), 1)).astype(F32)
    et = (_iota((HP, GW), 1) // HD == _iota((HP, GW), 0)).astype(F32)
    return e, et


def _modulate(x, mod_ref, sh, sc):
    return x * (1.0 + mod_ref[sc:sc + 1, :]) + mod_ref[sh:sh + 1, :]


def _ln_res(x, y, gate, lg, lb, gs):
    r = ALPHA * x + gs * (1.0 + gate) * y
    mu = jnp.mean(r, axis=-1, keepdims=True)
    var = jnp.mean(jnp.square(r - mu), axis=-1, keepdims=True)
    return (r - mu) * lax.rsqrt(var + LN_EPS) * lg + lb


def _rms(x, g):
    return x * lax.rsqrt(jnp.mean(x * x, axis=-1, keepdims=True) + RMS_EPS) * g


def _tile(n, pref):
    return pref if n % pref == 0 else n


def mod_fwd(c8, w, l, b):
    tn = w.shape[3]
    n = N_CHIPS * tn

    def body(c_ref, w_ref, b_ref, o_ref):
        h = jax.nn.silu(c_ref[...]).astype(BF16)
        o_ref[...] = jnp.dot(h, w_ref[...], preferred_element_type=F32) + b_ref[...]

    return pl.pallas_call(
        body, name="mod_fwd", grid=(N_CHIPS,),
        in_specs=[pl.BlockSpec((8, D), lambda j: (0, 0)), pl.BlockSpec((None, None, D, tn), lambda j: (l, j, 0, 0)),
                  pl.BlockSpec((1, tn), lambda j: (0, j))],
        out_specs=pl.BlockSpec((8, tn), lambda j: (0, j)), out_shape=_sds((8, n), F32),
        compiler_params=_cparams(VMEM_BIG))(c8, w, b)


def ffn_in_fwd(x, mod, w_in, l, sh, sc, S):
    T = x.shape[0]
    tm, tn = _tile(S, 512), FF // 2
    tpb, nj = S // tm, 2

    def body(x_ref, mod_ref, wg_ref, wu_ref, zg_ref, zu_ref, act_ref, h_ref):
        @pl.when(pl.program_id(1) == 0)
        def _():
            h_ref[...] = _modulate(x_ref[...], mod_ref, sh, sc).astype(BF16)
        g = jnp.dot(h_ref[...], wg_ref[...], preferred_element_type=F32)
        u = jnp.dot(h_ref[...], wu_ref[...], preferred_element_type=F32)
        zg_ref[...] = g
        zu_ref[...] = u
        act_ref[...] = (jax.nn.silu(g) * u).astype(BF16)

    return pl.pallas_call(
        body, name="ffn_in_fwd", grid=(T // tm, nj),
        in_specs=[pl.BlockSpec((tm, D), lambda i, j: (i, 0)),
                  pl.BlockSpec((None, N_MOD, D), lambda i, j: (i // tpb, 0, 0)),
                  pl.BlockSpec((None, None, D, tn), lambda i, j: (l, j, 0, 0)),
                  pl.BlockSpec((None, None, D, tn), lambda i, j: (l, j + nj, 0, 0))],
        out_specs=[pl.BlockSpec((tm, tn), lambda i, j: (i, j))] * 3,
        out_shape=[_sds((T, FF), F32), _sds((T, FF), F32), _sds((T, FF), BF16)],
        scratch_shapes=[pltpu.VMEM((tm, D), BF16)],
        compiler_params=_cparams(VMEM_BIG))(x, mod, w_in, w_in)


def mix_in_fwd(x, mod, w, sh, sc, S):
    T = x.shape[0]
    n = w.shape[1]
    tm, tn = _tile(S, 512), NP_TILE
    tpb = S // tm

    def body(x_ref, mod_ref, w_ref, o_ref, h_ref):
        @pl.when(pl.program_id(1) == 0)
        def _():
            h_ref[...] = _modulate(x_ref[...], mod_ref, sh, sc).astype(BF16)
        o_ref[...] = jnp.dot(h_ref[...], w_ref[...], preferred_element_type=F32)

    return pl.pallas_call(
        body, name="mix_in_fwd", grid=(T // tm, n // tn),
        in_specs=[pl.BlockSpec((tm, D), lambda i, j: (i, 0)),
                  pl.BlockSpec((None, N_MOD, D), lambda i, j: (i // tpb, 0, 0)),
                  pl.BlockSpec((D, tn), lambda i, j: (0, j))],
        out_specs=pl.BlockSpec((tm, tn), lambda i, j: (i, j)), out_shape=_sds((T, n), F32),
        scratch_shapes=[pltpu.VMEM((tm, D), BF16)],
        compiler_params=_cparams(VMEM_BIG))(x, mod, w)


def out_ln_fwd(act, w_out, x, mod, lg, lb, gate, gs, S, l=None):
    T, K = act.shape
    tm = _tile(S, 512)
    tpb = S // tm

    def body(a_ref, w_ref, x_ref, mod_ref, lg_ref, lb_ref, y_ref, xn_ref):
        y = jnp.dot(a_ref[...], w_ref[...].reshape(K, D), preferred_element_type=F32)
        y_ref[...] = y
        xn_ref[...] = _ln_res(x_ref[...], y, mod_ref[gate:gate + 1, :], lg_ref[...], lb_ref[...], gs)

    if l is None:
        w_spec = pl.BlockSpec((K, D), lambda i: (0, 0))
    else:
        w_spec = pl.BlockSpec((None, N_CHIPS, K // N_CHIPS, D), lambda i: (l, 0, 0, 0))
    return pl.pallas_call(
        body, name="out_ln_fwd", grid=(T // tm,),
        in_specs=[pl.BlockSpec((tm, K), lambda i: (i, 0)), w_spec,
                  pl.BlockSpec((tm, D), lambda i: (i, 0)),
                  pl.BlockSpec((None, N_MOD, D), lambda i: (i // tpb, 0, 0)),
                  pl.BlockSpec((1, D), lambda i: (0, 0)), pl.BlockSpec((1, D), lambda i: (0, 0))],
        out_specs=[pl.BlockSpec((tm, D), lambda i: (i, 0))] * 2,
        out_shape=[_sds((T, D), F32), _sds((T, D), F32)],
        compiler_params=_cparams(VMEM_BIG))(act, w_out, x, mod, lg, lb)


def ln_res_bwd(dxn, x, y, mod, lg, lb, gate, gs, S):
    T = x.shape[0]
    B = T // S
    tm = _tile(S, 512)
    tpb = S // tm

    def body(d_ref, x_ref, y_ref, mod_ref, lg_ref, lb_ref, dx_ref, dy_ref, dg_ref, dlg_ref, dlb_ref):
        i = pl.program_id(0)
        f = functools.partial(_ln_res, gs=gs)
        _, vjp = jax.vjp(f, x_ref[...], y_ref[...], mod_ref[gate:gate + 1, :], lg_ref[...], lb_ref[...])
        dx, dy, dg, dlg, dlb = vjp(d_ref[...])
        dx_ref[...] = dx
        dy_ref[...] = dy.astype(BF16)

        @pl.when(i % tpb == 0)
        def _():
            dg_ref[...] = jnp.zeros_like(dg_ref)

        @pl.when(i == 0)
        def _():
            dlg_ref[...] = jnp.zeros_like(dlg_ref)
            dlb_ref[...] = jnp.zeros_like(dlb_ref)

        dg_ref[...] += dg
        dlg_ref[...] += dlg
        dlb_ref[...] += dlb

    tok = pl.BlockSpec((tm, D), lambda i: (i, 0))
    vec = pl.BlockSpec((1, D), lambda i: (0, 0))
    return pl.pallas_call(
        body, name="ln_res_bwd", grid=(T // tm,),
        in_specs=[tok, tok, tok, pl.BlockSpec((None, N_MOD, D), lambda i: (i // tpb, 0, 0)), vec, vec],
        out_specs=[tok, tok, pl.BlockSpec((None, 1, D), lambda i: (i // tpb, 0, 0)), vec, vec],
        out_shape=[_sds((T, D), F32), _sds((T, D), BF16), _sds((B, 1, D), F32), _sds((1, D), F32), _sds((1, D), F32)],
        compiler_params=_cparams(VMEM_BIG))(dxn, x, y, mod, lg, lb)


def swiglu_bwd(dy, w_out, l, zg, zu, S):
    T = dy.shape[0]
    tm, tn = _tile(S, 512), FF // 2

    def body(dy_ref, w_ref, zg_ref, zu_ref, dg_ref, du_ref):
        da = lax.dot_general(dy_ref[...], w_ref[...].reshape(tn, D), (((1,), (1,)), ((), ())), preferred_element_type=F32)
        g, u = zg_ref[...], zu_ref[...]
        sg = jax.nn.sigmoid(g)
        dg_ref[...] = (da * u * (sg * (1.0 + g * (1.0 - sg)))).astype(BF16)
        du_ref[...] = (da * (g * sg)).astype(BF16)

    zt = pl.BlockSpec((tm, tn), lambda i, j: (i, j))
    return pl.pallas_call(
        body, name="swiglu_bwd", grid=(T // tm, FF // tn),
        in_specs=[pl.BlockSpec((tm, D), lambda i, j: (i, 0)),
                  pl.BlockSpec((None, 2, FF // N_CHIPS, D), lambda i, j: (l, j, 0, 0)), zt, zt],
        out_specs=[zt, zt], out_shape=[_sds((T, FF), BF16), _sds((T, FF), BF16)],
        compiler_params=_cparams(VMEM_BIG))(dy, w_out, zg, zu)


def nt_plain(dy, w):
    T = dy.shape[0]
    K = w.shape[0]
    tm = _tile(T, 512)

    def body(dy_ref, w_ref, o_ref):
        o_ref[...] = lax.dot_general(dy_ref[...], w_ref[...], (((1,), (1,)), ((), ())), preferred_element_type=F32)

    return pl.pallas_call(
        body, name="nt_plain", grid=(T // tm,),
        in_specs=[pl.BlockSpec((tm, D), lambda i: (i, 0)), pl.BlockSpec((K, D), lambda i: (0, 0))],
        out_specs=pl.BlockSpec((tm, K), lambda i: (i, 0)), out_shape=_sds((T, K), F32),
        compiler_params=_cparams(VMEM_BIG))(dy, w)


def tn_mm(a, b, tk):
    T, K = a.shape
    N = b.shape[1]
    tt = _tile(T, 512)

    def body(a_ref, b_ref, o_ref):
        @pl.when(pl.program_id(1) == 0)
        def _():
            o_ref[...] = jnp.zeros_like(o_ref)
        o_ref[...] += lax.dot_general(a_ref[...], b_ref[...], (((0,), (0,)), ((), ())), preferred_element_type=F32)

    return pl.pallas_call(
        body, name="tn_mm", grid=(K // tk, T // tt),
        in_specs=[pl.BlockSpec((tt, tk), lambda k, t: (t, k)), pl.BlockSpec((tt, N), lambda k, t: (t, 0))],
        out_specs=pl.BlockSpec((tk, N), lambda k, t: (k, 0)), out_shape=_sds((K, N), F32),
        compiler_params=_cparams(VMEM_BIG))(a, b)


def tn_mm_mod(x, mod, b, sh, sc, S, tn):
    T = x.shape[0]
    N = b.shape[1]
    tt = _tile(S, 512)
    tpb = S // tt

    def body(x_ref, mod_ref, b_ref, o_ref):
        @pl.when(pl.program_id(1) == 0)
        def _():
            o_ref[...] = jnp.zeros_like(o_ref)
        h = _modulate(x_ref[...], mod_ref, sh, sc).astype(BF16)
        o_ref[...] += lax.dot_general(h, b_ref[...], (((0,), (0,)), ((), ())), preferred_element_type=F32)

    return pl.pallas_call(
        body, name="tn_mm_mod", grid=(N // tn, T // tt),
        in_specs=[pl.BlockSpec((tt, D), lambda j, t: (t, 0)),
                  pl.BlockSpec((None, N_MOD, D), lambda j, t: (t // tpb, 0, 0)),
                  pl.BlockSpec((tt, tn), lambda j, t: (t, j))],
        out_specs=pl.BlockSpec((D, tn), lambda j, t: (0, j)), out_shape=_sds((D, N), F32),
        compiler_params=_cparams(VMEM_BIG))(x, mod, b)


def tn_mm_mod_shards(x, mod, bg, bu, sh, sc, S):
    T = x.shape[0]
    tn = FF // 2
    tt = _tile(S, 512)
    tpb = S // tt

    def body(x_ref, mod_ref, bg_ref, bu_ref, o_ref):
        j = pl.program_id(0)

        @pl.when(pl.program_id(1) == 0)
        def _():
            o_ref[...] = jnp.zeros_like(o_ref)
        h = _modulate(x_ref[...], mod_ref, sh, sc).astype(BF16)

        @pl.when(j < 2)
        def _():
            o_ref[...] += lax.dot_general(h, bg_ref[...], (((0,), (0,)), ((), ())), preferred_element_type=F32)

        @pl.when(j >= 2)
        def _():
            o_ref[...] += lax.dot_general(h, bu_ref[...], (((0,), (0,)), ((), ())), preferred_element_type=F32)

    return pl.pallas_call(
        body, name="tn_mm_mod_shards", grid=(N_CHIPS, T // tt),
        in_specs=[pl.BlockSpec((tt, D), lambda j, t: (t, 0)),
                  pl.BlockSpec((None, N_MOD, D), lambda j, t: (t // tpb, 0, 0)),
                  pl.BlockSpec((tt, tn), lambda j, t: (t, jnp.minimum(j, 1))),
                  pl.BlockSpec((tt, tn), lambda j, t: (t, jnp.maximum(j - 2, 0)))],
        out_specs=pl.BlockSpec((None, D, tn), lambda j, t: (j, 0, 0)), out_shape=_sds((N_CHIPS, D, tn), F32),
        compiler_params=_cparams(VMEM_BIG))(x, mod, bg, bu)


def nt_mod_bwd(ds, w, offs, x, mod, dres, sc, S, tk, l=None):
    T = x.shape[0]
    B = T // S
    tm = _tile(S, 512)
    tpb = S // tm
    Kd = ds[0].shape[1]
    nk = Kd // tk
    n_in = len(ds)

    def body(*refs):
        d_refs, w_refs = refs[:n_in], refs[n_in:2 * n_in]
        x_ref, mod_ref, r_ref, dx_ref, dsh_ref, dsc_ref, acc = refs[2 * n_in:]
        i, k = pl.program_id(0), pl.program_id(1)

        @pl.when(k == 0)
        def _():
            acc[...] = jnp.zeros_like(acc)

        for d_ref, w_ref in zip(d_refs, w_refs):
            acc[...] += lax.dot_general(d_ref[...], w_ref[...], (((1,), (1,)), ((), ())), preferred_element_type=F32)

        @pl.when(k == nk - 1)
        def _():
            dh = acc[...]
            dx_ref[...] = dh * (1.0 + mod_ref[sc:sc + 1, :]) + r_ref[...]

            @pl.when(i % tpb == 0)
            def _():
                dsh_ref[...] = jnp.zeros_like(dsh_ref)
                dsc_ref[...] = jnp.zeros_like(dsc_ref)

            dsh_ref[...] += jnp.sum(dh, axis=0, keepdims=True)
            dsc_ref[...] += jnp.sum(dh * x_ref[...], axis=0, keepdims=True)

    tok = pl.BlockSpec((tm, D), lambda i, k: (i, 0))
    vec = pl.BlockSpec((None, 1, D), lambda i, k: (i // tpb, 0, 0))
    in_specs = [pl.BlockSpec((tm, tk), lambda i, k: (i, k)) for _ in ds]
    if l is None:
        in_specs += [pl.BlockSpec((D, tk), functools.partial(lambda i, k, o: (0, k + o), o=off // tk)) for off in offs]
    else:
        in_specs += [pl.BlockSpec((None, None, D, tk), functools.partial(lambda i, k, o: (l, k + o, 0, 0), o=off)) for off in offs]
    in_specs += [tok, pl.BlockSpec((None, N_MOD, D), lambda i, k: (i // tpb, 0, 0)), tok]
    return pl.pallas_call(
        body, name="nt_mod_bwd", grid=(T // tm, nk), in_specs=in_specs,
        out_specs=[tok, vec, vec],
        out_shape=[_sds((T, D), F32), _sds((B, 1, D), F32), _sds((B, 1, D), F32)],
        scratch_shapes=[pltpu.VMEM((tm, D), F32)],
        compiler_params=_cparams(VMEM_BIG))(*ds, *([w] * n_in), x, mod, dres)


def loss_head(y, tgt):
    T = y.shape[0]
    tm = _tile(T, 512)

    def body(y_ref, t_ref, l_ref, d_ref):
        @pl.when(pl.program_id(0) == 0)
        def _():
            l_ref[...] = jnp.zeros_like(l_ref)
        e = y_ref[...] - t_ref[...]
        d_ref[...] = e * (1.0 / D)
        l_ref[...] += 0.5 * jnp.sum(jnp.sum(e * e, axis=1, keepdims=True) * (1.0 / D))

    tok = pl.BlockSpec((tm, D), lambda i: (i, 0))
    return pl.pallas_call(
        body, name="loss_head", grid=(T // tm,), in_specs=[tok, tok],
        out_specs=[pl.BlockSpec((8, 128), lambda i: (0, 0)), tok],
        out_shape=[_sds((8, 128), F32), _sds((T, D), F32)],
        compiler_params=_cparams(VMEM_BIG))(y, tgt)


def _hgrn_block(q, fz, inp, go, st, lb, ng, blk):
    nc = blk // A_CHUNK
    lb_eff = jnp.maximum(lb, LB_FLOOR)
    log_f = jnp.logaddexp(jnp.log(lb_eff), jnp.log1p(-lb) + jax.nn.log_sigmoid(fz))
    k = (1.0 - lb) * jax.nn.sigmoid(-fz) - (lb_eff - lb)
    qf = jax.nn.silu(q)
    same_chunk = _iota((blk, blk), 0) // A_CHUNK == _iota((blk, blk), 1) // A_CHUNK
    tril = (same_chunk & (_iota((blk, blk), 1) <= _iota((blk, blk), 0))).astype(F32)
    G = _mm_hi(tril, log_f)
    e_mat, et_mat = _head_sum_mats()
    G4, q4, k4, v4 = (z.reshape(nc, A_CHUNK, GW) for z in (G, qf, k, inp))
    shp = (nc, A_CHUNK, A_CHUNK, GW)
    causal = _iota(shp, 2) <= _iota(shp, 1)
    decay = jnp.exp(jnp.where(causal, G4[:, :, None, :] - G4[:, None, :, :], -jnp.inf))
    prod = q4[:, :, None, :] * k4[:, None, :, :] * decay
    scores = _mm(prod.reshape(nc * A_CHUNK * A_CHUNK, GW), e_mat.astype(BF16))
    spread = _mm(scores, et_mat.astype(BF16)).reshape(shp)
    o_intra = jnp.sum(spread * v4[:, None, :, :], axis=2).reshape(blk, GW)
    head_diag = (_iota((GW, GW), 0) // HD == _iota((GW, GW), 1) // HD).astype(F32)
    g_last = [jnp.sum(log_f[c * A_CHUNK:(c + 1) * A_CHUNK], axis=0, keepdims=True) for c in range(nc)]
    g_last_b = jnp.concatenate([jnp.broadcast_to(g, (A_CHUNK, GW)) for g in g_last], axis=0)
    q_dec = qf * jnp.exp(G)
    k_end = k * jnp.exp(g_last_b - G)
    outs = []
    for c in range(nc):
        rows = slice(c * A_CHUNK, (c + 1) * A_CHUNK)
        outs.append(_mm_nt(q_dec[rows], st))
        st = st * jnp.exp(g_last[c]) + _mm_tn(inp[rows], k_end[rows]) * head_diag
    o = o_intra + jnp.concatenate(outs, axis=0)
    ms = _mm_hi(o * o, e_mat) * (1.0 / HD)
    o = o * _mm_hi(lax.rsqrt(ms + RMS_EPS), et_mat) * ng
    return o * jax.nn.silu(go), st


HGRN_BLK = 128


def hgrn_fwd(proj, lb, ng, S):
    T = proj.shape[0]
    B = T // S
    blk = min(HGRN_BLK, S)
    nb = S // blk

    def body(p_ref, lb_ref, ng_ref, o_ref, st_out_ref, st_ref):
        @pl.when(pl.program_id(1) == 0)
        def _():
            st_ref[...] = jnp.zeros_like(st_ref)
        st_out_ref[...] = st_ref[...]
        p = p_ref[...]
        o, st = _hgrn_block(p[:, 0:GW], p[:, GW:2 * GW], p[:, 2 * GW:3 * GW], p[:, 3 * GW:4 * GW],
                            st_ref[...], lb_ref[...], ng_ref[...], blk)
        o_ref[...] = o.astype(BF16)
        st_ref[...] = st

    vec = pl.BlockSpec((1, GW), lambda b, j: (0, 0))
    return pl.pallas_call(
        body, name="hgrn_fwd", grid=(B, nb),
        in_specs=[pl.BlockSpec((blk, 4 * GW), lambda b, j: (b * nb + j, C_A // (4 * GW))), vec, vec],
        out_specs=[pl.BlockSpec((blk, GW), lambda b, j: (b * nb + j, 0)),
                   pl.BlockSpec((None, GW, GW), lambda b, j: (b * nb + j, 0, 0))],
        out_shape=[_sds((T, GW), BF16), _sds((B * nb, GW, GW), F32)],
        scratch_shapes=[pltpu.VMEM((GW, GW), F32)],
        compiler_params=_cparams(VMEM_BIG))(proj, lb, ng)


def hgrn_bwd(proj, states, dcat, lb, ng, S):
    T = proj.shape[0]
    B = T // S
    blk = min(HGRN_BLK, S)
    nb = S // blk

    def body(p_ref, st_in_ref, do_ref, lb_ref, ng_ref, dp_ref, dlb_ref, dng_ref, dst_ref):
        b, j = pl.program_id(0), pl.program_id(1)

        @pl.when(j == 0)
        def _():
            dst_ref[...] = jnp.zeros_like(dst_ref)

        @pl.when((b == 0) & (j == 0))
        def _():
            dlb_ref[...] = jnp.zeros_like(dlb_ref)
            dng_ref[...] = jnp.zeros_like(dng_ref)

        p = p_ref[...]
        f = functools.partial(_hgrn_block, blk=blk)
        _, vjp = jax.vjp(f, p[:, 0:GW], p[:, GW:2 * GW], p[:, 2 * GW:3 * GW], p[:, 3 * GW:4 * GW],
                         st_in_ref[...], lb_ref[...], ng_ref[...])
        dq, df, di, dg, dst, dlb, dng = vjp((do_ref[...], dst_ref[...]))
        dp_ref[...] = jnp.concatenate([dq, df, di, dg], axis=1).astype(BF16)
        dst_ref[...] = dst
        dlb_ref[...] += dlb
        dng_ref[...] += dng

    def rev(b, j):
        return b * nb + (nb - 1 - j)

    vec = pl.BlockSpec((1, GW), lambda b, j: (0, 0))
    return pl.pallas_call(
        body, name="hgrn_bwd", grid=(B, nb),
        in_specs=[pl.BlockSpec((blk, 4 * GW), lambda b, j: (rev(b, j), C_A // (4 * GW))),
                  pl.BlockSpec((None, GW, GW), lambda b, j: (rev(b, j), 0, 0)),
                  pl.BlockSpec((blk, GW), lambda b, j: (rev(b, j), O_A // GW)), vec, vec],
        out_specs=[pl.BlockSpec((blk, 4 * GW), lambda b, j: (rev(b, j), 0)), vec, vec],
        out_shape=[_sds((T, 4 * GW), BF16), _sds((1, GW), F32), _sds((1, GW), F32)],
        scratch_shapes=[pltpu.VMEM((GW, GW), F32)],
        compiler_params=_cparams(VMEM_BIG))(proj, states, dcat, lb, ng)


ATT_TQ = 256


def _attn_block(q, k, v, cum, qpos0, scale, use_cum):
    s = _mm_nt(q, k) * scale
    if use_cum:
        s = s - cum
    qpos = qpos0 + _iota(s.shape, 0)
    s = jnp.where(_iota(s.shape, 1) <= qpos, s, -jnp.inf)
    e = jnp.exp(s - jnp.max(s, axis=-1, keepdims=True))
    p = e / jnp.sum(e, axis=-1, keepdims=True)
    return _mm(p, v)


def attn_fwd(qa, qo, ka, ko, va, vo, cum, scale, S):
    T = qa.shape[0]
    B = T // S
    tq = min(ATT_TQ, S)
    nq = S // tq
    use_cum = cum is not None

    def body(*refs):
        if use_cum:
            q_ref, k_ref, v_ref, c_ref, o_ref = refs
            crow = c_ref[pl.ds(pl.program_id(1), 1), :]
        else:
            q_ref, k_ref, v_ref, o_ref = refs
            crow = None
        o = _attn_block(q_ref[...], k_ref[...], v_ref[...], crow, pl.program_id(2) * tq, scale, use_cum)
        o_ref[...] = o.astype(BF16)

    in_specs = [pl.BlockSpec((tq, HP), lambda b, h, i: (b * nq + i, qo + h)),
                pl.BlockSpec((S, HP), lambda b, h, i: (b, ko + h)),
                pl.BlockSpec((S, HP), lambda b, h, i: (b, vo + h))]
    args = [qa, ka, va]
    if use_cum:
        in_specs.append(pl.BlockSpec((None, 8, S), lambda b, h, i: (b, 0, 0)))
        args.append(cum)
    return pl.pallas_call(
        body, name="attn_fwd", grid=(B, HEADS, nq), in_specs=in_specs,
        out_specs=pl.BlockSpec((tq, HP), lambda b, h, i: (b * nq + i, h)),
        out_shape=_sds((T, HEADS * HP), BF16),
        compiler_params=_cparams(VMEM_BIG))(*args)


def attn_bwd(qa, qo, ka, ko, va, vo, cum, dcat, do_off, scale, S, out_dtype):
    T = qa.shape[0]
    B = T // S
    tq = min(ATT_TQ, S)
    nq = S // tq
    use_cum = cum is not None

    def body(*refs):
        if use_cum:
            q_ref, k_ref, v_ref, do_ref, c_ref, dq_ref, dk_ref, dv_ref, dc_ref, dk_acc, dv_acc = refs
            crow = c_ref[pl.ds(pl.program_id(1), 1), :]
        else:
            q_ref, k_ref, v_ref, do_ref, dq_ref, dk_ref, dv_ref, dk_acc, dv_acc = refs
            crow = jnp.zeros((1, S), F32)
        i = pl.program_id(2)

        @pl.when(i == 0)
        def _():
            dk_acc[...] = jnp.zeros_like(dk_acc)
            dv_acc[...] = jnp.zeros_like(dv_acc)
            if use_cum:
                dc_ref[...] = jnp.zeros_like(dc_ref)

        f = functools.partial(_attn_block, qpos0=i * tq, scale=scale, use_cum=use_cum)
        _, vjp = jax.vjp(f, q_ref[...], k_ref[...], v_ref[...], crow)
        dq, dk, dv, dc = vjp(do_ref[...])
        dq_ref[...] = dq.astype(out_dtype)
        dk_acc[...] += dk
        dv_acc[...] += dv
        if use_cum:
            dc_ref[...] += dc

        @pl.when(i == nq - 1)
        def _():
            dk_ref[...] = dk_acc[...].astype(out_dtype)
            dv_ref[...] = dv_acc[...].astype(out_dtype)

    qspec = pl.BlockSpec((tq, HP), lambda b, h, i: (b * nq + i, qo + h))
    in_specs = [qspec, pl.BlockSpec((S, HP), lambda b, h, i: (b, ko + h)),
                pl.BlockSpec((S, HP), lambda b, h, i: (b, vo + h)),
                pl.BlockSpec((tq, HP), lambda b, h, i: (b * nq + i, do_off + h))]
    args = [qa, ka, va, dcat]
    kv_out = pl.BlockSpec((S, HP), lambda b, h, i: (b, h))
    out_specs = [pl.BlockSpec((tq, HP), lambda b, h, i: (b * nq + i, h)), kv_out, kv_out]
    out_shape = [_sds((T, HEADS * HP), out_dtype)] * 3
    if use_cum:
        in_specs.append(pl.BlockSpec((None, 8, S), lambda b, h, i: (b, 0, 0)))
        args.append(cum)
        out_specs.append(pl.BlockSpec((None, 1, S), lambda b, h, i: (b * HEADS + h, 0, 0)))
        out_shape.append(_sds((B * HEADS, 1, S), F32))
    return pl.pallas_call(
        body, name="attn_bwd", grid=(B, HEADS, nq), in_specs=in_specs, out_specs=out_specs, out_shape=out_shape,
        scratch_shapes=[pltpu.VMEM((S, HP), F32), pltpu.VMEM((S, HP), F32)],
        compiler_params=_cparams(VMEM_BIG))(*args)


def _tri(n, upper):
    r, c = _iota((n, n), 0), _iota((n, n), 1)
    return ((r <= c) if upper else (r >= c)).astype(F32)


def fox_gate_fwd(proj, bcol, S):
    T = proj.shape[0]
    B = T // S
    ts = _tile(S, 512)
    nt = S // ts

    def body(p_ref, b_ref, o_ref, carry):
        @pl.when(pl.program_id(1) == 0)
        def _():
            carry[...] = jnp.zeros_like(carry)
        cf = jnp.transpose(p_ref[...])[0:8, :]
        lf = jax.nn.log_sigmoid(cf + b_ref[...])
        cum = _mm_hi(lf, _tri(ts, True)) + carry[...]
        o_ref[...] = cum
        carry[...] += jnp.sum(lf, axis=1, keepdims=True)

    return pl.pallas_call(
        body, name="fox_gate_fwd", grid=(B, nt),
        in_specs=[pl.BlockSpec((ts, HP), lambda b, j: (b * nt + j, C_CF // HP)), pl.BlockSpec((8, 1), lambda b, j: (0, 0))],
        out_specs=pl.BlockSpec((None, 8, ts), lambda b, j: (b, 0, j)), out_shape=_sds((B, 8, S), F32),
        scratch_shapes=[pltpu.VMEM((8, 1), F32)],
        compiler_params=_cparams(VMEM_BIG))(proj, bcol)


def fox_gate_bwd(proj, bcol, dcum, S):
    T = proj.shape[0]
    B = T // S
    ts = _tile(S, 512)
    nt = S // ts

    def body(p_ref, b_ref, dc_ref, dp_ref, db_ref, carry):
        b, j = pl.program_id(0), pl.program_id(1)

        @pl.when(j == 0)
        def _():
            carry[...] = jnp.zeros_like(carry)

        @pl.when((b == 0) & (j == 0))
        def _():
            db_ref[...] = jnp.zeros_like(db_ref)

        cf = jnp.transpose(p_ref[...])[0:8, :]
        dc = dc_ref[...]
        dlf = _mm_hi(dc, _tri(ts, False)) + carry[...]
        carry[...] += jnp.sum(dc, axis=1, keepdims=True)
        dcf = dlf * jax.nn.sigmoid(-(cf + b_ref[...]))
        db_ref[...] += jnp.sum(dcf, axis=1, keepdims=True)
        full = jnp.concatenate([dcf, jnp.zeros((HP - 8, ts), F32)], axis=0)
        dp_ref[...] = jnp.transpose(full).astype(BF16)

    def rev(b, j):
        return nt - 1 - j

    return pl.pallas_call(
        body, name="fox_gate_bwd", grid=(B, nt),
        in_specs=[pl.BlockSpec((ts, HP), lambda b, j: (b * nt + rev(b, j), C_CF // HP)),
                  pl.BlockSpec((8, 1), lambda b, j: (0, 0)),
                  pl.BlockSpec((None, 8, ts), lambda b, j: (b, 0, rev(b, j)))],
        out_specs=[pl.BlockSpec((ts, HP), lambda b, j: (b * nt + rev(b, j), 0)), pl.BlockSpec((8, 1), lambda b, j: (0, 0))],
        out_shape=[_sds((T, HP), BF16), _sds((8, 1), F32)],
        scratch_shapes=[pltpu.VMEM((8, 1), F32)],
        compiler_params=_cparams(VMEM_BIG))(proj, bcol, dcum)


def _mla_pre(blk, gq, gkv, wq, wkv, place, cos_q, sin_q, cs_k):
    nq = _rms(blk[:, 0:B_Q_LORA], gq)
    nkv = _rms(blk[:, B_Q_LORA:B_Q_LORA + B_KV_LORA], gkv)
    qq = _mm(nq, wq)
    q = qq[:, 0:HEADS * HP] * cos_q + qq[:, HEADS * HP:] * sin_q
    kv = _mm(nkv, wkv)
    k = kv[:, 0:HEADS * HP] + _mm(blk[:, B_Q_LORA + B_KV_LORA:] * cs_k, place)
    return q, k, kv[:, HEADS * HP:]


def mla_pre_fwd(proj, gq, gkv, wq, wkv, place, cos_q, sin_q, cs_k, S):
    T = proj.shape[0]
    tm = _tile(S, 512)
    tpb = S // tm
    W = HEADS * HP

    def body(p_ref, gq_ref, gkv_ref, wq_ref, wkv_ref, pl_ref, cq_ref, sq_ref, ck_ref, q_ref, k_ref, v_ref):
        q, k, v = _mla_pre(p_ref[...], gq_ref[...], gkv_ref[...], wq_ref[...], wkv_ref[...], pl_ref[...],
                           cq_ref[...], sq_ref[...], ck_ref[...])
        q_ref[...] = q
        k_ref[...] = k
        v_ref[...] = v

    def full(a):
        return pl.BlockSpec(a.shape, lambda i: (0,) * a.ndim)

    tok = pl.BlockSpec((tm, W), lambda i: (i, 0))
    return pl.pallas_call(
        body, name="mla_pre_fwd", grid=(T // tm,),
        in_specs=[pl.BlockSpec((tm, W), lambda i: (i, C_B // W)), full(gq), full(gkv), full(wq), full(wkv), full(place),
                  pl.BlockSpec((tm, W), lambda i: (i % tpb, 0)), pl.BlockSpec((tm, W), lambda i: (i % tpb, 0)),
                  pl.BlockSpec((tm, HP), lambda i: (i % tpb, 0))],
        out_specs=[tok] * 3, out_shape=[_sds((T, W), F32)] * 3,
        compiler_params=_cparams(VMEM_BIG))(proj, gq, gkv, wq, wkv, place, cos_q, sin_q, cs_k)


def mla_pre_bwd(proj, gq, gkv, wq, wkv, place, cos_q, sin_q, cs_k, dq, dk, dv, S):
    T = proj.shape[0]
    tm = _tile(S, 512)
    tpb = S // tm
    W = HEADS * HP

    def body(p_ref, gq_ref, gkv_ref, wq_ref, wkv_ref, pl_ref, cq_ref, sq_ref, ck_ref, dq_ref, dk_ref, dv_ref,
             dp_ref, dgq_ref, dgkv_ref, dwq_ref, dwkv_ref):
        @pl.when(pl.program_id(0) == 0)
        def _():
            for r in (dgq_ref, dgkv_ref, dwq_ref, dwkv_ref):
                r[...] = jnp.zeros_like(r)

        f = functools.partial(_mla_pre, place=pl_ref[...], cos_q=cq_ref[...], sin_q=sq_ref[...], cs_k=ck_ref[...])
        _, vjp = jax.vjp(f, p_ref[...], gq_ref[...], gkv_ref[...], wq_ref[...], wkv_ref[...])
        dp, dgq, dgkv, dwq, dwkv = vjp((dq_ref[...], dk_ref[...], dv_ref[...]))
        dp_ref[...] = dp.astype(BF16)
        dgq_ref[...] += dgq
        dgkv_ref[...] += dgkv
        dwq_ref[...] += dwq
        dwkv_ref[...] += dwkv

    def full(a):
        return pl.BlockSpec(a.shape, lambda i: (0,) * a.ndim)

    tok = pl.BlockSpec((tm, W), lambda i: (i, 0))
    return pl.pallas_call(
        body, name="mla_pre_bwd", grid=(T // tm,),
        in_specs=[pl.BlockSpec((tm, W), lambda i: (i, C_B // W)), full(gq), full(gkv), full(wq), full(wkv), full(place),
                  pl.BlockSpec((tm, W), lambda i: (i % tpb, 0)), pl.BlockSpec((tm, W), lambda i: (i % tpb, 0)),
                  pl.BlockSpec((tm, HP), lambda i: (i % tpb, 0)), tok, tok, tok],
        out_specs=[tok, full(gq), full(gkv), full(wq), full(wkv)],
        out_shape=[_sds((T, W), BF16), _sds(gq.shape, F32), _sds(gkv.shape, F32), _sds(wq.shape, F32), _sds(wkv.shape, F32)],
        compiler_params=_cparams(VMEM_BIG))(proj, gq, gkv, wq, wkv, place, cos_q, sin_q, cs_k, dq, dk, dv)


def _gmlp_block(blk, lg, lb, ws, bs):
    u = jax.nn.gelu(blk[:, 0:GW])
    v = jax.nn.gelu(blk[:, GW:2 * GW])
    mu = jnp.mean(v, axis=-1, keepdims=True)
    var = jnp.mean(jnp.square(v - mu), axis=-1, keepdims=True)
    vn = (v - mu) * lax.rsqrt(var + LN_EPS) * lg + lb
    causal = _iota((D_CHUNK, D_CHUNK), 1) <= _iota((D_CHUNK, D_CHUNK), 0)
    group = _iota((1, GW), 1) // HD
    mixed = jnp.zeros((D_CHUNK, GW), F32)
    for g in range(HEADS):
        part = _mm(jnp.where(causal, ws[g], 0.0), vn) + bs[g]
        mixed = mixed + jnp.where(group == g, part, 0.0)
    return u * mixed


def gmlp_fwd(proj, lg, lb, ws, bs):
    T = proj.shape[0]

    def body(p_ref, lg_ref, lb_ref, ws_ref, bs_ref, o_ref):
        o_ref[...] = _gmlp_block(p_ref[...], lg_ref[...], lb_ref[...], ws_ref[...], bs_ref[...]).astype(BF16)

    def full(a):
        return pl.BlockSpec(a.shape, lambda i: (0,) * a.ndim)

    return pl.pallas_call(
        body, name="gmlp_fwd", grid=(T // D_CHUNK,),
        in_specs=[pl.BlockSpec((D_CHUNK, 2 * GW), lambda i: (i, C_D // (2 * GW))), full(lg), full(lb), full(ws), full(bs)],
        out_specs=pl.BlockSpec((D_CHUNK, GW), lambda i: (i, 0)), out_shape=_sds((T, GW), BF16),
        compiler_params=_cparams(VMEM_BIG))(proj, lg, lb, ws, bs)


def gmlp_bwd(proj, lg, lb, ws, bs, dcat):
    T = proj.shape[0]

    def body(p_ref, lg_ref, lb_ref, ws_ref, bs_ref, do_ref, dp_ref, dlg_ref, dlb_ref, dws_ref, dbs_ref):
        @pl.when(pl.program_id(0) == 0)
        def _():
            for r in (dlg_ref, dlb_ref, dws_ref, dbs_ref):
                r[...] = jnp.zeros_like(r)

        _, vjp = jax.vjp(_gmlp_block, p_ref[...], lg_ref[...], lb_ref[...], ws_ref[...], bs_ref[...])
        dp, dlg, dlb, dws, dbs = vjp(do_ref[...])
        dp_ref[...] = dp.astype(BF16)
        dlg_ref[...] += dlg
        dlb_ref[...] += dlb
        dws_ref[...] += dws
        dbs_ref[...] += dbs

    def full(a):
        return pl.BlockSpec(a.shape, lambda i: (0,) * a.ndim)

    return pl.pallas_call(
        body, name="gmlp_bwd", grid=(T // D_CHUNK,),
        in_specs=[pl.BlockSpec((D_CHUNK, 2 * GW), lambda i: (i, C_D // (2 * GW))), full(lg), full(lb), full(ws), full(bs),
                  pl.BlockSpec((D_CHUNK, GW), lambda i: (i, O_D // GW))],
        out_specs=[pl.BlockSpec((D_CHUNK, 2 * GW), lambda i: (i, 0)), full(lg), full(lb), full(ws), full(bs)],
        out_shape=[_sds((T, 2 * GW), BF16), _sds(lg.shape, F32), _sds(lb.shape, F32), _sds(ws.shape, F32), _sds(bs.shape, F32)],
        compiler_params=_cparams(VMEM_BIG))(proj, lg, lb, ws, bs, dcat)


def _lb_all(logits):
    m = jnp.max(logits, axis=0, keepdims=True)
    e = jnp.exp(logits - m)
    sm = e / jnp.sum(e, axis=0, keepdims=True)
    return jnp.concatenate([sm[0:1] - sm[0:1], (sm[0:1] + sm[1:2]) - sm[0:1]], axis=0)


def lb_fwd(logits):
    def body(l_ref, o_ref):
        o_ref[...] = _lb_all(l_ref[...])

    return pl.pallas_call(body, name="lb_fwd", out_shape=_sds(logits.shape, F32))(logits)


def lb_bwd(logits, dlb):
    def body(l_ref, d_ref, o_ref):
        _, vjp = jax.vjp(_lb_all, l_ref[...])
        o_ref[...] = vjp(d_ref[...])[0]

    return pl.pallas_call(body, name="lb_bwd", out_shape=_sds(logits.shape, F32))(logits, dlb)


def ada_grad(c_all, dmod_cols):
    N = dmod_cols.shape[1]
    tn = _tile(N, 1152)

    def body(c_ref, d_ref, o_ref):
        h = jax.nn.silu(c_ref[...]).astype(BF16)
        o_ref[...] = lax.dot_general(h, d_ref[...].astype(BF16), (((0,), (0,)), ((), ())), preferred_element_type=F32)

    nb = c_all.shape[0]
    return pl.pallas_call(
        body, name="ada_grad", grid=(N // tn,),
        in_specs=[pl.BlockSpec((nb, D), lambda j: (0, 0)), pl.BlockSpec((nb, tn), lambda j: (0, j))],
        out_specs=pl.BlockSpec((D, tn), lambda j: (0, j)), out_shape=_sds((D, N), F32),
        compiler_params=_cparams(VMEM_BIG))(c_all, dmod_cols)


def sum_slots(a, n, name):
    _, R, C = a.shape
    tr = _row_tile(R, C, n)

    def body(a_ref, o_ref):
        acc = a_ref[0]
        for k in range(1, n):
            acc = acc + a_ref[k]
        o_ref[...] = acc

    return pl.pallas_call(
        body, name=name, grid=(R // tr,),
        in_specs=[pl.BlockSpec((n, tr, C), lambda i: (0, i, 0))],
        out_specs=pl.BlockSpec((tr, C), lambda i: (i, 0)), out_shape=_sds((R, C), F32),
        compiler_params=_cparams(VMEM_BIG))(a)


def add2(a, b, name):
    shp = a.shape
    C = shp[-1]
    a2, b2 = a.reshape(-1, C), b.reshape(-1, C)
    R = a2.shape[0]
    tr = _row_tile(R, C)

    def body(a_ref, b_ref, o_ref):
        o_ref[...] = a_ref[...] + b_ref[...]

    spec = pl.BlockSpec((tr, C), lambda i: (i, 0))
    return pl.pallas_call(body, name=name, grid=(R // tr,), in_specs=[spec, spec], out_specs=spec,
                          out_shape=_sds((R, C), F32), compiler_params=_cparams(VMEM_BIG))(a2, b2).reshape(shp)


def _row_tile(R, C=D, n=1):
    limit = max(8, (1 << 18) // (C * n))
    for t in range(limit - limit % 8, 7, -8):
        if R % t == 0:
            return t
    return R


def adamw(w, g, m, v, name):
    R, C = w.shape
    tr = _row_tile(R, C)
    c1 = 1.0 - ADAM_B1 ** ADAM_STEP
    c2 = 1.0 - ADAM_B2 ** ADAM_STEP

    def body(w_ref, g_ref, m_ref, v_ref, d_ref, nm_ref, nv_ref):
        g_ = g_ref[...]
        nm = ADAM_B1 * m_ref[...] + (1.0 - ADAM_B1) * g_
        nv = ADAM_B2 * v_ref[...] + (1.0 - ADAM_B2) * jnp.square(g_)
        d_ref[...] = -ADAM_LR * ((nm / c1) / (jnp.sqrt(nv / c2) + ADAM_EPS) + ADAM_WD * w_ref[...])
        nm_ref[...] = nm
        nv_ref[...] = nv

    spec = pl.BlockSpec((tr, C), lambda i: (i, 0))
    return pl.pallas_call(body, name=name, grid=(R // tr,), in_specs=[spec] * 4, out_specs=[spec] * 3,
                          out_shape=[_sds((R, C), F32)] * 3, compiler_params=_cparams(VMEM_BIG))(w, g, m, v)


def _rot_cols(w):
    return jnp.concatenate([-w[:, 16:32], w[:, 0:16]], axis=1)


def _fold_rot(d):
    return jnp.concatenate([d[:, 16:32], -d[:, 0:16]], axis=1)


def _pad_heads(w, off, axis):
    parts = []
    for h in range(HEADS):
        piece = lax.slice_in_dim(w, off + HD * h, off + HD * (h + 1), axis=axis)
        parts += [piece, jnp.zeros_like(piece)]
    return parts


def _unpad_heads(d, off, axis):
    return [lax.slice_in_dim(d, off + HP * h, off + HP * h + HD, axis=axis) for h in range(HEADS)]


def mix_in_ext(w):
    z = lambda n: jnp.zeros((w.shape[0], n), w.dtype)
    kr = w[:, 1408:1440]
    cols = [w[:, 0:1408], kr, _rot_cols(kr), z(64)]
    cols += _pad_heads(w, 1440, 1) + _pad_heads(w, 1696, 1) + _pad_heads(w, 1952, 1)
    cols += [w[:, 2212:2724], w[:, 2208:2212], z(NP - C_CF - HEADS)]
    return jnp.concatenate(cols, axis=1)


def mix_in_unext(d):
    kr = d[:, 1408:1440] + _fold_rot(d[:, 1440:1472])
    cols = [d[:, 0:1408], kr] + _unpad_heads(d, C_CQ, 1) + _unpad_heads(d, C_CK, 1) + _unpad_heads(d, C_CV, 1)
    cols += [d[:, C_CF:C_CF + HEADS], d[:, C_D:C_D + 2 * GW]]
    return jnp.concatenate(cols, axis=1)


def mix_out_ext(w):
    return jnp.concatenate([w[0:GW]] + _pad_heads(w, GW, 0) + _pad_heads(w, 2 * GW, 0) + [w[3 * GW:4 * GW]], axis=0)


def mix_out_unext(d):
    return jnp.concatenate([d[0:GW]] + _unpad_heads(d, O_B, 0) + _unpad_heads(d, O_C, 0) + [d[O_D:O_D + GW]], axis=0)


def uq_ext(w):
    z = lambda n: jnp.zeros((w.shape[0], n), w.dtype)
    a, b = [], []
    for h in range(HEADS):
        o = (B_NOPE + B_ROPE) * h
        a += [w[:, o:o + B_NOPE + B_ROPE], z(32)]
        b += [z(B_NOPE), _rot_cols(w[:, o + B_NOPE:o + B_NOPE + B_ROPE]), z(32)]
    return jnp.concatenate(a + b, axis=1)


def uq_unext(d):
    cols = []
    for h in range(HEADS):
        o = HP * h
        cols += [d[:, o:o + B_NOPE], d[:, o + B_NOPE:o + B_NOPE + B_ROPE]
                 + _fold_rot(d[:, HEADS * HP + o + B_NOPE:HEADS * HP + o + B_NOPE + B_ROPE])]
    return jnp.concatenate(cols, axis=1)


def ukv_ext(w):
    z = jnp.zeros((w.shape[0], HD), w.dtype)
    k, v = [], []
    for h in range(HEADS):
        k += [w[:, 2 * HD * h:2 * HD * h + HD], z]
        v += [w[:, 2 * HD * h + HD:2 * HD * (h + 1)], z]
    return jnp.concatenate(k + v, axis=1)


def ukv_unext(d):
    cols = []
    for h in range(HEADS):
        cols += [d[:, HP * h:HP * h + HD], d[:, HEADS * HP + HP * h:HEADS * HP + HP * h + HD]]
    return jnp.concatenate(cols, axis=1)


def rope_tables(S):
    half = B_ROPE // 2
    inv_freq = ROPE_THETA ** (-jnp.arange(half, dtype=F32) / half)
    ang = jnp.arange(S).astype(F32)[:, None] * inv_freq[None, :]
    cos = jnp.tile(jnp.cos(ang), (1, 2))
    sin = jnp.tile(jnp.sin(ang), (1, 2))
    one, zero = jnp.ones((S, B_NOPE), F32), jnp.zeros((S, B_NOPE), F32)
    z32 = jnp.zeros((S, 32), F32)
    cos_q = jnp.tile(jnp.concatenate([one, cos, z32], axis=1), (1, HEADS))
    sin_q = jnp.tile(jnp.concatenate([zero, sin, z32], axis=1), (1, HEADS))
    cs_k = jnp.concatenate([cos, sin, zero], axis=1)
    place = np.zeros((HP, HEADS * HP), np.float32)
    for h in range(HEADS):
        for j in range(B_ROPE):
            place[j, h * HP + B_NOPE + j] = 1.0
            place[B_ROPE + j, h * HP + B_NOPE + j] = 1.0
    return cos_q, sin_q, cs_k, jnp.asarray(place, BF16)


def layer_fwd(x, mod, get, tabs, S):
    cos_q, sin_q, cs_k, place = tabs
    p = dict(get("ffn1", x))
    l = p["wl"]
    zg1, zu1, act1 = ffn_in_fwd(x, mod, p["ffn1_in"], l, 0, 1, S)
    y1, x1 = out_ln_fwd(act1, p["ffn1_out"], x, mod, p["ln_g"][0:1], p["ln_b"][0:1], 2, 0.5, S, l)
    p.update(get("mix", x1))
    proj = mix_in_fwd(x1, mod, p["mix_in"], 3, 4, S)
    o_a, states = hgrn_fwd(proj, p["lb"], p["ng"], S)
    q_b, k_b, v_b = mla_pre_fwd(proj, p["gq"], p["gkv"], p["wq"], p["wkv"], place, cos_q, sin_q, cs_k, S)
    o_b = attn_fwd(q_b, 0, k_b, 0, v_b, 0, None, (B_NOPE + B_ROPE) ** -0.5, S)
    cum = fox_gate_fwd(proj, p["bcol"], S)
    o_c = attn_fwd(proj, C_CQ // HP, proj, C_CK // HP, proj, C_CV // HP, cum, HD ** -0.5, S)
    o_d = gmlp_fwd(proj, p["g_lg"], p["g_lb"], p["ws"], p["bs"])
    cat = jnp.concatenate([o_a, o_b, o_c, o_d], axis=1)
    y2, x2 = out_ln_fwd(cat, p["mix_out"], x1, mod, p["ln_g"][1:2], p["ln_b"][1:2], 5, 1.0, S)
    p.update(get("ffn2", x2))
    zg3, zu3, act3 = ffn_in_fwd(x2, mod, p["ffn2_in"], l, 6, 7, S)
    y3, x3 = out_ln_fwd(act3, p["ffn2_out"], x2, mod, p["ln_g"][2:3], p["ln_b"][2:3], 8, 0.5, S, l)
    saved = dict(x=x, zg1=zg1, zu1=zu1, act1=act1, y1=y1, x1=x1, proj=proj, states=states, q_b=q_b, k_b=k_b, v_b=v_b,
                 cum=cum, cat=cat, y2=y2, x2=x2, zg3=zg3, zu3=zu3, act3=act3, y3=y3, p=p)
    return x3, saved


def _ffn_bwd(dxn, x_in, y, zg, zu, act, mod, w_in, w_out, l, lg, lb, idx, S):
    sh, sc, gate = idx
    dres, dy, dgate, dlg, dlb = ln_res_bwd(dxn, x_in, y, mod, lg, lb, gate, 0.5, S)
    dzg, dzu = swiglu_bwd(dy, w_out, l, zg, zu, S)
    dw_out = tn_mm(act, dy, FF // 2).reshape(N_CHIPS, FF // N_CHIPS, D)
    dw_in = tn_mm_mod_shards(x_in, mod, dzg, dzu, sh, sc, S)
    dx, dsh, dsc = nt_mod_bwd([dzg, dzu], w_in, [0, 2], x_in, mod, dres, sc, S, FF // 2, l)
    return dx, dw_in, dw_out, dlg, dlb, {sh: dsh, sc: dsc, gate: dgate}


def layer_bwd(dx3, mod, sv, tabs, S, emit):
    cos_q, sin_q, cs_k, place = tabs
    p = sv["p"]
    l = p["wl"]
    g = {}
    dm = {}
    dx2, g["ffn2_in"], g["ffn2_out"], dlg2, dlb2, d = _ffn_bwd(
        dx3, sv["x2"], sv["y3"], sv["zg3"], sv["zu3"], sv["act3"], mod, p["ffn2_in"], p["ffn2_out"], l,
        p["ln_g"][2:3], p["ln_b"][2:3], (6, 7, 8), S)
    dm.update(d)
    mod = emit("ffn2", g, mod)
    dres, dy2, dm[5], dlg1, dlb1 = ln_res_bwd(dx2, sv["x1"], sv["y2"], mod, p["ln_g"][1:2], p["ln_b"][1:2], 5, 1.0, S)
    dcat = nt_plain(dy2, p["mix_out"])
    g["mix_out"] = tn_mm(sv["cat"], dy2, 768)
    proj = sv["proj"]
    d_a, g["lb"], g["ng"] = hgrn_bwd(proj, sv["states"], dcat, p["lb"], p["ng"], S)
    dq_c, dk_c, dv_c, dcum = attn_bwd(proj, C_CQ // HP, proj, C_CK // HP, proj, C_CV // HP, sv["cum"], dcat,
                                      O_C // HP, HD ** -0.5, S, BF16)
    B = proj.shape[0] // S
    dcum = jnp.concatenate([dcum.reshape(B, HEADS, S), jnp.zeros((B, 8 - HEADS, S), F32)], axis=1)
    d_cf, g["bcol"] = fox_gate_bwd(proj, p["bcol"], dcum, S)
    dq_b, dk_b, dv_b = attn_bwd(sv["q_b"], 0, sv["k_b"], 0, sv["v_b"], 0, None, dcat, O_B // HP,
                                (B_NOPE + B_ROPE) ** -0.5, S, F32)
    d_b, g["gq"], g["gkv"], g["wq"], g["wkv"] = mla_pre_bwd(
        proj, p["gq"], p["gkv"], p["wq"], p["wkv"], place, cos_q, sin_q, cs_k, dq_b, dk_b, dv_b, S)
    d_d, g["g_lg"], g["g_lb"], g["ws"], g["bs"] = gmlp_bwd(proj, p["g_lg"], p["g_lb"], p["ws"], p["bs"], dcat)
    dproj = jnp.concatenate([d_a, d_b, dq_c, dk_c, dv_c, d_d, d_cf, jnp.zeros_like(d_cf)], axis=1)
    g["mix_in"] = tn_mm_mod(sv["x1"], mod, dproj, 3, 4, S, NP_TILE)
    dx1, dm[3], dm[4] = nt_mod_bwd([dproj], p["mix_in"], [0], sv["x1"], mod, dres, 4, S, NP_TILE)
    mod = emit("mix", g, mod)
    dx0, g["ffn1_in"], g["ffn1_out"], dlg0, dlb0, d = _ffn_bwd(
        dx1, sv["x"], sv["y1"], sv["zg1"], sv["zu1"], sv["act1"], mod, p["ffn1_in"], p["ffn1_out"], l,
        p["ln_g"][0:1], p["ln_b"][0:1], (0, 1, 2), S)
    dm.update(d)
    emit("ffn1", g, mod)
    g["ln_g"] = jnp.concatenate([dlg0, dlg1, dlg2], axis=0)
    g["ln_b"] = jnp.concatenate([dlb0, dlb1, dlb2], axis=0)
    dmod = jnp.concatenate([dm[i] for i in range(N_MOD)], axis=1)
    return dx0, dmod, g


def local_step(x, c8, tgt, get, lb_logits, S, emit=None):
    B = x.shape[0] // S
    tabs = rope_tables(S)
    lb_all = lb_fwd(lb_logits)
    mods, saved = [], []
    h = x
    for l in range(DEPTH):
        pa = get(l, "ada", h)
        mod = mod_fwd(c8, pa["ada_w"], pa["wl"], pa["ada_b"])[0:B].reshape(B, N_MOD, D)

        def get_l(part, after, l=l):
            p = dict(get(l, part, after))
            if part == "mix":
                p["lb"] = lb_all[l:l + 1]
            return p

        h, sv = layer_fwd(h, mod, get_l, tabs, S)
        mods.append(mod)
        saved.append(sv)
    loss_tile, dh = loss_head(h, tgt)
    grads, dmods, dlb = [None] * DEPTH, [None] * DEPTH, [None] * DEPTH
    for l in reversed(range(DEPTH)):
        emit_l = (lambda part, g, mod: mod) if emit is None else functools.partial(emit, l)
        dh, dmods[l], grads[l] = layer_bwd(dh, mods[l], saved[l], tabs, S, emit_l)
        dlb[l] = grads[l].pop("lb")
    d_logits = lb_bwd(lb_logits, jnp.concatenate(dlb, axis=0))
    return loss_tile, dh, dmods, grads, d_logits


ANY = pl.BlockSpec(memory_space=pl.ANY)


def _place():
    x, y, c = lax.axis_index("x"), lax.axis_index("y"), lax.axis_index("c")
    chips = [(1 - x, y), (x, 1 - y), (1 - x, 1 - y)]
    return x, y, c, chips


def _rcopy(src, dst, sems, k, to):
    send_sems, recv_sems = sems
    return pltpu.make_async_remote_copy(src_ref=src, dst_ref=dst, send_sem=send_sems.at[k], recv_sem=recv_sems.at[k],
                                        device_id=to, device_id_type=MESH)


def _dma_sems(n_remote, n_local):
    return [pltpu.SemaphoreType.DMA((n_remote,)), pltpu.SemaphoreType.DMA((n_remote,)), pltpu.SemaphoreType.DMA((n_local,))]


def own_slot(src, chip):
    L = src.shape[0]
    return lax.dynamic_update_slice(jnp.zeros((L, N_CHIPS) + src.shape[1:], src.dtype), src[:, None], (0, chip, 0, 0))


def ag_shards(arrs, lands):
    n = len(arrs)
    rh = [a.shape[1] // 2 for a in arrs]

    def body(*refs):
        srcs, outs, token = refs[:n], refs[2 * n:3 * n], refs[3 * n]
        send_sems, recv_sems = refs[3 * n + 1:]
        x, y, c, chips = _place()
        sems = (send_sems, recv_sems)
        me = 2 * x + y
        sibling = (x, y, 1 - c)
        token[...] = jnp.zeros_like(token)

        def part(i, k, hc):
            return outs[i].at[:, k, pl.ds(hc * rh[i], rh[i]), :]

        started = []
        for j, (px, py) in enumerate(chips):
            for i in range(n):
                cp = _rcopy(srcs[i].at[:, pl.ds(c * rh[i], rh[i]), :], part(i, me, c), sems, 6 * i + j, (px, py, c))
                cp.start()
                started.append(cp)
        for j, (px, py) in enumerate(chips):
            k = 2 * px + py
            for i in range(n):
                _rcopy(part(i, k, c), part(i, k, c), sems, 6 * i + j, (px, py, c)).wait_recv()
                cp = _rcopy(part(i, k, c), part(i, k, c), sems, 6 * i + 3 + j, sibling)
                cp.start()
                started.append(cp)
        for j, (px, py) in enumerate(chips):
            k = 2 * px + py
            for i in range(n):
                _rcopy(part(i, k, 1 - c), part(i, k, 1 - c), sems, 6 * i + 3 + j, sibling).wait_recv()
        for cp in started:
            cp.wait_send()

    outs = pl.pallas_call(
        body, name="ag_shards", out_shape=[_sds(a.shape, a.dtype) for a in lands] + [_sds((8, 128), F32)],
        in_specs=[ANY] * (2 * n), out_specs=[ANY] * n + [pl.BlockSpec(memory_space=pltpu.VMEM)],
        input_output_aliases={n + i: i for i in range(n)}, scratch_shapes=_dma_sems(6 * n, 1)[:2])(*arrs, *lands)
    return list(outs[:n]), outs[n]


HBM_SPEC = pl.BlockSpec(memory_space=pltpu.HBM)
SEM_SPEC = pl.BlockSpec(memory_space=pltpu.SEMAPHORE)
DATAFLOW = pltpu.SideEffectType.DATAFLOW_SIDE_EFFECTING


def _after(x, dep):
    return lax.optimization_barrier((x, dep))[0]


def _split_start(srcs, lands, copies, name):
    n, m = len(srcs), len(lands)

    def body(*refs):
        ins = refs[:n + m]
        send_sems, recv_sems = refs[n + m], refs[n + m + 1]
        token = refs[-1]
        for k, (src, dst, to) in enumerate(copies(ins[:n], ins[n:], _place())):
            pltpu.make_async_remote_copy(src_ref=src, dst_ref=dst, send_sem=send_sems.at[k], recv_sem=recv_sems.at[k],
                                         device_id=to, device_id_type=MESH).start()
        token[...] = jnp.zeros_like(token)

    n_copies = 3 * n
    arrs = list(srcs) + list(lands)
    outs = pl.pallas_call(
        body, name=name,
        out_shape=(pltpu.SemaphoreType.DMA((n_copies,)), pltpu.SemaphoreType.DMA((n_copies,)),
                   *[pltpu.HBM(a.shape, a.dtype) for a in arrs], _sds((8, 128), F32)),
        in_specs=[HBM_SPEC] * (n + m),
        out_specs=(SEM_SPEC, SEM_SPEC, *[HBM_SPEC] * (n + m), pl.BlockSpec(memory_space=pltpu.VMEM)),
        input_output_aliases={i: 2 + i for i in range(n + m)},
        compiler_params=pltpu.CompilerParams(has_side_effects=DATAFLOW),
    )(*[pltpu.with_memory_space_constraint(a, pltpu.HBM) for a in arrs])
    return outs[0], outs[1], list(outs[2:2 + n]), list(outs[2 + n:2 + n + m]), outs[-1]


def _split_wait(handle, arrivals, after, name):
    send_sems, recv_sems, srcs, lands, _ = handle
    n, m = len(srcs), len(lands)

    def body(*refs):
        ins = refs[:n + m]
        send_sems, recv_sems = refs[n + m], refs[n + m + 1]
        x, y, c, chips = place = _place()
        for k, (src, dst) in enumerate(arrivals(ins[:n], ins[n:], place)):
            cp = pltpu.make_async_remote_copy(src_ref=src, dst_ref=dst, send_sem=send_sems.at[k], recv_sem=recv_sems.at[k],
                                              device_id=(x, y, 1 - c), device_id_type=MESH)
            cp.wait_send()
            cp.wait_recv()

    arrs = list(srcs) + list(lands)
    outs = pl.pallas_call(
        body, name=name, out_shape=[pltpu.HBM(a.shape, a.dtype) for a in arrs],
        in_specs=[HBM_SPEC] * (n + m) + [SEM_SPEC, SEM_SPEC, ANY], out_specs=[HBM_SPEC] * (n + m),
        input_output_aliases={i: i for i in range(n + m)},
        compiler_params=pltpu.CompilerParams(has_side_effects=DATAFLOW),
    )(*arrs, send_sems, recv_sems, after)
    return list(outs[n:])


def _ag_part(ref, k, hc):
    rh = ref.shape[2] // 2
    return ref.at[:, k, pl.ds(hc * rh, rh), :]


def ag_start(srcs, lands, name):
    def copies(s, d, place):
        x, y, c, chips = place
        out = []
        for j, (px, py) in enumerate(chips):
            for i in range(len(s)):
                rh = s[i].shape[1] // 2
                out.append((s[i].at[:, pl.ds(c * rh, rh), :], _ag_part(d[i], 2 * x + y, c), (px, py, c)))
        return out

    return _split_start(srcs, lands, copies, name)


def ag_wait(handle, after, name):
    def arrivals(s, d, place):
        x, y, c, chips = place
        out = []
        for j, (px, py) in enumerate(chips):
            for i in range(len(s)):
                rh = s[i].shape[1] // 2
                out.append((s[i].at[:, pl.ds(c * rh, rh), :], _ag_part(d[i], 2 * px + py, c)))
        return out

    return _split_wait(handle, arrivals, after, name)


def ag_forward(lands, name):
    n = len(lands)

    def body(*refs):
        bufs = refs[n:2 * n]
        send_sems, recv_sems = refs[2 * n:]
        x, y, c, chips = _place()
        sems = (send_sems, recv_sems)
        cps = []
        for j, (px, py) in enumerate(chips):
            for i in range(n):
                part = _ag_part(bufs[i], 2 * px + py, c)
                cps.append(_rcopy(part, part, sems, 3 * i + j, (x, y, 1 - c)))
        for cp in cps:
            cp.start()
        for j, (px, py) in enumerate(chips):
            for i in range(n):
                part = _ag_part(bufs[i], 2 * px + py, 1 - c)
                _rcopy(part, part, sems, 3 * i + j, (x, y, 1 - c)).wait_recv()
        for cp in cps:
            cp.wait_send()

    return pl.pallas_call(
        body, name=name, out_shape=[_sds(a.shape, a.dtype) for a in lands], in_specs=[ANY] * n, out_specs=[ANY] * n,
        input_output_aliases={i: i for i in range(n)}, scratch_shapes=_dma_sems(3 * n, 1)[:2])(*lands)


def rs_start(hs, lands, name):
    def copies(s, d, place):
        x, y, c, chips = place
        return [(s[i].at[2 * px + py], d[i].at[2 * x + y], (px, py, c)) for j, (px, py) in enumerate(chips) for i in range(len(s))]

    return _split_start(hs, lands, copies, name)


def rs_wait(handle, after, name):
    def arrivals(s, d, place):
        x, y, c, chips = place
        return [(s[i].at[2 * px + py], d[i].at[2 * px + py]) for j, (px, py) in enumerate(chips) for i in range(len(s))]

    return _split_wait(handle, arrivals, after, name)


def sibling_swap(arrs, name):
    n = len(arrs)
    rh = [a.shape[1] // 2 for a in arrs]

    def body(*refs):
        srcs, outs = refs[:n], refs[n:2 * n]
        send_sems, recv_sems = refs[2 * n:]
        x, y, c, _ = _place()
        cps = [_rcopy(srcs[i].at[:, pl.ds((1 - c) * rh[i], rh[i]), :], outs[i], (send_sems, recv_sems), i, (x, y, 1 - c))
               for i in range(n)]
        for cp in cps:
            cp.start()
        for cp in cps:
            cp.wait()

    return pl.pallas_call(
        body, name=name, out_shape=[_sds((N_CHIPS, r, a.shape[2]), a.dtype) for a, r in zip(arrs, rh)],
        in_specs=[ANY] * n, out_specs=[ANY] * n, scratch_shapes=_dma_sems(n, 1)[:2])(*arrs)


def chip_exchange(hs):
    n = len(hs)

    def body(*refs):
        srcs, outs = refs[:n], refs[n:2 * n]
        send_sems, recv_sems, loc_sems = refs[2 * n:]
        x, y, c, chips = _place()
        sems = (send_sems, recv_sems)
        me = 2 * x + y
        mine = [pltpu.make_async_copy(srcs[i].at[me], outs[i].at[me], loc_sems.at[i]) for i in range(n)]
        for cp in mine:
            cp.start()
        sends = []
        for j, (px, py) in enumerate(chips):
            for i in range(n):
                cp = _rcopy(srcs[i].at[2 * px + py], outs[i].at[me], sems, 3 * i + j, (px, py, c))
                cp.start()
                sends.append(cp)
        for j, (px, py) in enumerate(chips):
            for i in range(n):
                _rcopy(srcs[i].at[2 * px + py], outs[i].at[2 * px + py], sems, 3 * i + j, (px, py, c)).wait_recv()
        for cp in sends:
            cp.wait_send()
        for cp in mine:
            cp.wait()

    return pl.pallas_call(
        body, name="chip_exchange", out_shape=[_sds(h.shape, h.dtype) for h in hs],
        in_specs=[ANY] * n, out_specs=[ANY] * n, scratch_shapes=_dma_sems(3 * n, n))(*hs)


def sum_into(land, base, l, core, name):
    _, rh, C = land.shape
    tr = _row_tile(rh, C, N_CHIPS)
    nr = rh // tr

    def body(core_ref, land_ref, base_ref, o_ref):
        acc = land_ref[0]
        for k in range(1, N_CHIPS):
            acc = acc + land_ref[k]
        o_ref[...] = acc

    grid_spec = pltpu.PrefetchScalarGridSpec(
        num_scalar_prefetch=1, grid=(nr,),
        in_specs=[pl.BlockSpec((N_CHIPS, tr, C), lambda r, core_ref: (0, r, 0)), ANY],
        out_specs=pl.BlockSpec((None, tr, C), lambda r, core_ref: (l, core_ref[0] * nr + r, 0)))
    return pl.pallas_call(body, name=name, grid_spec=grid_spec, out_shape=_sds(base.shape, base.dtype),
                          input_output_aliases={2: 0}, compiler_params=_cparams(VMEM_BIG))(
        core.reshape(1).astype(jnp.int32), land, base)


def sibling_join(bases):
    n = len(bases)

    def body(*refs):
        bufs = refs[n:2 * n]
        send_sems, recv_sems = refs[2 * n:]
        x, y, c, _ = _place()
        sems = (send_sems, recv_sems)

        def half(i, hc):
            rh = bufs[i].shape[1] // 2
            return bufs[i].at[:, pl.ds(hc * rh, rh), :]

        sends = [_rcopy(half(i, c), half(i, c), sems, i, (x, y, 1 - c)) for i in range(n)]
        for cp in sends:
            cp.start()
        for i in range(n):
            _rcopy(half(i, 1 - c), half(i, 1 - c), sems, i, (x, y, 1 - c)).wait_recv()
        for cp in sends:
            cp.wait_send()

    return pl.pallas_call(
        body, name="sibling_join", out_shape=[_sds(b.shape, b.dtype) for b in bases], in_specs=[ANY] * n, out_specs=[ANY] * n,
        input_output_aliases={i: i for i in range(n)}, scratch_shapes=_dma_sems(n, 1)[:2])(*bases)


def ag_all(blk):
    M, C = blk.shape

    def body(x_ref, out_ref, send_sems, recv_sems, loc_sem):
        x, y, c, chips = _place()
        sems = (send_sems, recv_sems)
        me, sibling = (x, y, c), (x, y, 1 - c)

        def slot(px, py, pc):
            return out_ref.at[4 * px + 2 * py + pc]

        mine = pltpu.make_async_copy(x_ref, slot(*me), loc_sem)
        mine.start()
        first = [_rcopy(x_ref, slot(*me), sems, 0, sibling)]
        first += [_rcopy(x_ref, slot(*me), sems, 1 + j, (*chip, c)) for j, chip in enumerate(chips)]
        for cp in first:
            cp.start()
        passed = [_rcopy(slot(*chip, c), slot(*chip, c), sems, 4 + j, sibling) for j, chip in enumerate(chips)]
        for j, chip in enumerate(chips):
            _rcopy(slot(*chip, c), slot(*chip, c), sems, 1 + j, me).wait_recv()
            passed[j].start()
        _rcopy(slot(*sibling), slot(*sibling), sems, 0, me).wait_recv()
        for j, chip in enumerate(chips):
            _rcopy(slot(*chip, 1 - c), slot(*chip, 1 - c), sems, 4 + j, me).wait_recv()
        for cp in first + passed:
            cp.wait_send()
        mine.wait()

    return pl.pallas_call(
        body, name="ag_all", out_shape=_sds((8, M, C), blk.dtype),
        in_specs=[pl.BlockSpec(memory_space=pltpu.VMEM)], out_specs=pl.BlockSpec(memory_space=pltpu.VMEM),
        scratch_shapes=[pltpu.SemaphoreType.DMA((7,)), pltpu.SemaphoreType.DMA((7,)), pltpu.SemaphoreType.DMA(())],
        compiler_params=_cparams(VMEM_BIG))(blk)


WEIGHTS = ["ada_w", "ada_b", "ln_g", "ln_b", "ffn1_w_in", "ffn1_w_out", "ffn2_w_in", "ffn2_w_out", "mix_w_in", "mix_w_out",
           "hgrn_lb_logits", "hgrn_norm_g", "mla_q_norm_g", "mla_kv_norm_g", "mla_w_uq", "mla_w_ukv", "fox_b_f",
           "gmlp_ln_g", "gmlp_ln_b", "gmlp_w_s", "gmlp_b_s"]
SHARDED = {"ffn1_w_in": 1, "ffn1_w_out": 0, "ffn2_w_in": 1, "ffn2_w_out": 0, "mix_w_in": 1, "mix_w_out": 0,
           "mla_w_uq": 1, "mla_w_ukv": 1}
SMALL = ["hgrn_lb_logits", "hgrn_norm_g", "mla_q_norm_g", "mla_kv_norm_g", "fox_b_f", "gmlp_ln_g", "gmlp_ln_b",
         "gmlp_w_s", "gmlp_b_s", "ln_g", "ln_b"]
GATHERED = ["ada_w", "ffn1_w_in", "ffn1_w_out", "ffn2_w_in", "ffn2_w_out", "mix_w_in", "mix_w_out", "mla_w_uq", "mla_w_ukv"]
REDUCED = GATHERED[1:]


def _col_shards(a):
    cols = a.shape[1] // N_CHIPS
    return jnp.stack([a[:, k * cols:(k + 1) * cols] for k in range(N_CHIPS)])


def add_kept_half(a, got, core, name):
    _, R, C = a.shape
    rh = R // 2
    tr = _row_tile(rh, C)
    nr = rh // tr

    def body(core_ref, a_ref, b_ref, o_ref):
        o_ref[...] = a_ref[...] + b_ref[...]

    half = pl.BlockSpec((None, tr, C), lambda k, r, core_ref: (k, r, 0))
    grid_spec = pltpu.PrefetchScalarGridSpec(
        num_scalar_prefetch=1, grid=(N_CHIPS, nr),
        in_specs=[pl.BlockSpec((None, tr, C), lambda k, r, core_ref: (k, core_ref[0] * nr + r, 0)), half],
        out_specs=half)
    return pl.pallas_call(body, name=name, grid_spec=grid_spec, out_shape=_sds((N_CHIPS, rh, C), F32),
                          compiler_params=_cparams(VMEM_BIG))(core.reshape(1).astype(jnp.int32), a, got)


def _rows(parts, n_rows, dtype):
    flat = jnp.concatenate([p.reshape(-1) for p in parts])
    pad = n_rows * D - flat.shape[0]
    return jnp.concatenate([flat, jnp.zeros((pad,), dtype)]).reshape(n_rows, D)


def _take(flat, shapes):
    out, o = [], 0
    for shp in shapes:
        n = int(np.prod(shp))
        out.append(flat[o:o + n].reshape(shp))
        o += n
    return out


def _round_up(n, m):
    return -(-n // m) * m


def pack_shard(w):
    parts = [w[n][l] for l in range(DEPTH) for n in SHARDED] + [w[n][l] for l in range(DEPTH) for n in ("ln_g", "ln_b")]
    n = sum(int(np.prod(p.shape)) for p in parts)
    return _rows(parts, _round_up(-(-n // D), 16), F32)


def unpack_shard(pk, like):
    shapes = [like[n].shape[1:] for l in range(DEPTH) for n in SHARDED] + [like[n].shape[1:] for l in range(DEPTH) for n in ("ln_g", "ln_b")]
    pieces = _take(pk.reshape(-1), shapes)
    names = [n for l in range(DEPTH) for n in SHARDED] + [n for l in range(DEPTH) for n in ("ln_g", "ln_b")]
    out = {}
    for n in list(SHARDED) + ["ln_g", "ln_b"]:
        out[n] = jnp.stack([p for p, m in zip(pieces, names) if m == n])
    return out


def pack_small(w):
    parts = [w[n][l] for l in range(DEPTH) for n in SMALL]
    n = sum(int(np.prod(p.shape)) for p in parts)
    return _rows(parts, _round_up(-(-n // D), 8), F32)


def unpack_small(pk, like):
    shapes = [like[n].shape[1:] for l in range(DEPTH) for n in SMALL]
    pieces = _take(pk.reshape(-1), shapes)
    names = [n for l in range(DEPTH) for n in SMALL]
    return {n: jnp.stack([p for p, m in zip(pieces, names) if m == n]) for n in SMALL}


def pack_gather(w):
    parts = [w[n][l].astype(BF16) for l in range(DEPTH) for n in ["ada_w"] + list(SHARDED)]
    ln = jnp.concatenate([w[n][l].reshape(-1) for l in range(DEPTH) for n in ("ln_g", "ln_b")])
    parts.append(lax.bitcast_convert_type(ln, BF16))
    n = sum(int(np.prod(p.shape)) for p in parts)
    return _rows(parts, _round_up(-(-n // D), 16), BF16)


def unpack_gather(g, w):
    names = ["ada_w"] + list(SHARDED)
    shapes = [w[n].shape[1:] for l in range(DEPTH) for n in names]
    n_ln = DEPTH * 2 * 3 * (D // N_CHIPS)
    flat = g.reshape(N_CHIPS, -1)
    per_chip = [_take(flat[k], shapes + [(n_ln, 2)]) for k in range(N_CHIPS)]
    layers = [dict() for _ in range(DEPTH)]
    i = 0
    for l in range(DEPTH):
        for n in names:
            axis = 1 if n == "ada_w" else SHARDED[n]
            layers[l][n] = jnp.concatenate([per_chip[k][i] for k in range(N_CHIPS)], axis=axis)
            i += 1
    ln = [lax.bitcast_convert_type(per_chip[k][i], F32).reshape(DEPTH, 2, 3, D // N_CHIPS) for k in range(N_CHIPS)]
    ln = jnp.concatenate(ln, axis=3)
    for l in range(DEPTH):
        layers[l]["ln_g"], layers[l]["ln_b"] = ln[l, 0], ln[l, 1]
    return layers


def pack_grads(grads, k):
    parts = []
    for l in range(DEPTH):
        g = grads[l]
        full = {"ffn1_w_out": g["ffn1_out"], "ffn2_w_out": g["ffn2_out"], "mix_w_in": mix_in_unext(g["mix_in"]),
                "mix_w_out": mix_out_unext(g["mix_out"]), "mla_w_uq": uq_unext(g["wq"]), "mla_w_ukv": ukv_unext(g["wkv"])}
        for n, axis in SHARDED.items():
            if n in ("ffn1_w_in", "ffn2_w_in"):
                half = g[n.replace("_w_in", "_in")][k // 2]
                parts.append(half[:, (k % 2) * (FF // 2):(k % 2 + 1) * (FF // 2)])
            else:
                sz = full[n].shape[axis] // N_CHIPS
                parts.append(lax.slice_in_dim(full[n], k * sz, (k + 1) * sz, axis=axis))
    for l in range(DEPTH):
        for n in ("ln_g", "ln_b"):
            parts.append(grads[l][n][:, k * (D // N_CHIPS):(k + 1) * (D // N_CHIPS)])
    n = sum(int(np.prod(p.shape)) for p in parts)
    return _rows(parts, _round_up(-(-n // D), 16), F32)


def kernel(x, c, ada_w, ada_b, ln_g, ln_b, ffn1_w_in, ffn1_w_out, ffn2_w_in, ffn2_w_out, mix_w_in, mix_w_out, hgrn_lb_logits, hgrn_norm_g, mla_q_norm_g, mla_kv_norm_g, mla_w_uq, mla_w_ukv, fox_b_f, gmlp_ln_g, gmlp_ln_b, gmlp_w_s, gmlp_b_s, loss_target, m_ada_w, m_ada_b, m_ln_g, m_ln_b, m_ffn1_w_in, m_ffn1_w_out, m_ffn2_w_in, m_ffn2_w_out, m_mix_w_in, m_mix_w_out, m_hgrn_lb_logits, m_hgrn_norm_g, m_mla_q_norm_g, m_mla_kv_norm_g, m_mla_w_uq, m_mla_w_ukv, m_fox_b_f, m_gmlp_ln_g, m_gmlp_ln_b, m_gmlp_w_s, m_gmlp_b_s, v_ada_w, v_ada_b, v_ln_g, v_ln_b, v_ffn1_w_in, v_ffn1_w_out, v_ffn2_w_in, v_ffn2_w_out, v_mix_w_in, v_mix_w_out, v_hgrn_lb_logits, v_hgrn_norm_g, v_mla_q_norm_g, v_mla_kv_norm_g, v_mla_w_uq, v_mla_w_ukv, v_fox_b_f, v_gmlp_ln_g, v_gmlp_ln_b, v_gmlp_w_s, v_gmlp_b_s):
    w = dict(zip(WEIGHTS, (ada_w, ada_b, ln_g, ln_b, ffn1_w_in, ffn1_w_out, ffn2_w_in, ffn2_w_out, mix_w_in, mix_w_out, hgrn_lb_logits, hgrn_norm_g, mla_q_norm_g, mla_kv_norm_g, mla_w_uq, mla_w_ukv, fox_b_f, gmlp_ln_g, gmlp_ln_b, gmlp_w_s, gmlp_b_s)))
    m = dict(zip(WEIGHTS, (m_ada_w, m_ada_b, m_ln_g, m_ln_b, m_ffn1_w_in, m_ffn1_w_out, m_ffn2_w_in, m_ffn2_w_out, m_mix_w_in, m_mix_w_out, m_hgrn_lb_logits, m_hgrn_norm_g, m_mla_q_norm_g, m_mla_kv_norm_g, m_mla_w_uq, m_mla_w_ukv, m_fox_b_f, m_gmlp_ln_g, m_gmlp_ln_b, m_gmlp_w_s, m_gmlp_b_s)))
    v = dict(zip(WEIGHTS, (v_ada_w, v_ada_b, v_ln_g, v_ln_b, v_ffn1_w_in, v_ffn1_w_out, v_ffn2_w_in, v_ffn2_w_out, v_mix_w_in, v_mix_w_out, v_hgrn_lb_logits, v_hgrn_norm_g, v_mla_q_norm_g, v_mla_kv_norm_g, v_mla_w_uq, v_mla_w_ukv, v_fox_b_f, v_gmlp_ln_g, v_gmlp_ln_b, v_gmlp_w_s, v_gmlp_b_s)))
    Bl, S, _ = x.shape
    T = Bl * S
    core = lax.axis_index("c")
    chip = 2 * lax.axis_index("x") + lax.axis_index("y")

    def shard(key):
        n, l = key
        if n == "ln":
            return jnp.concatenate([ln_g[l:l + 1], ln_b[l:l + 1], jnp.zeros((1, 2, D // N_CHIPS), F32)], axis=1)
        return w[n][l:l + 1].astype(BF16)

    mixers = ["mix_w_in", "mix_w_out", "mla_w_uq", "mla_w_ukv"]
    groups = [[("ada_w", 0), ("ffn1_w_in", 0), ("ffn1_w_out", 0), ("ln", 0)],
              [(n, 0) for n in mixers + ["ffn2_w_in", "ffn2_w_out"]],
              [(n, 1) for n in GATHERED + ["ln"]]]
    srcs = [shard(k) for k in groups[0]]
    first, token = ag_shards(srcs, [own_slot(s, chip) for s in srcs])
    have = dict(zip(groups[0], first))
    handles = {}
    for gi in (1, 2):
        srcs = [s + token[0, 0].astype(s.dtype) for s in (shard(k) for k in groups[gi])]
        handles[gi] = ag_start(srcs, [own_slot(s, chip) for s in srcs], "ag_start_%d" % gi)
    c8 = jnp.concatenate([c, jnp.zeros((8 - Bl, D), F32)], axis=0)
    c8 = c8 + (handles[1][-1][0, 0] + handles[2][-1][0, 0])

    def cat_cols(a):
        return jnp.concatenate([a[0, k] for k in range(N_CHIPS)], axis=1)

    def get(l, part, after):
        gi = 2 if l == 1 else (0 if part in ("ada", "ffn1") else 1)
        if gi in handles:
            arrived = ag_forward(ag_wait(handles.pop(gi), after, "ag_wait_%d" % gi), "ag_forward_%d" % gi)
            have.update(zip(groups[gi], arrived))
        if part == "ada":
            return dict(ada_w=have[("ada_w", l)], wl=0, ada_b=ada_b[l][None])
        if part == "ffn1":
            ln_full = jnp.moveaxis(have[("ln", l)][0], 0, 1).reshape(8, D)
            return dict(ffn1_in=have[("ffn1_w_in", l)], ffn1_out=have[("ffn1_w_out", l)], wl=0,
                        ln_g=ln_full[0:3], ln_b=ln_full[3:6])
        if part == "ffn2":
            return dict(ffn2_in=have[("ffn2_w_in", l)], ffn2_out=have[("ffn2_w_out", l)])
        return dict(
            mix_in=mix_in_ext(cat_cols(have[("mix_w_in", l)])), mix_out=mix_out_ext(have[("mix_w_out", l)].reshape(D, D)),
            wq=uq_ext(cat_cols(have[("mla_w_uq", l)])).astype(F32), wkv=ukv_ext(cat_cols(have[("mla_w_ukv", l)])).astype(F32),
            ng=hgrn_norm_g[l][None], gq=mla_q_norm_g[l][None], gkv=mla_kv_norm_g[l][None],
            bcol=jnp.concatenate([fox_b_f[l], jnp.zeros((8 - HEADS,), F32)])[:, None],
            g_lg=gmlp_ln_g[l][None], g_lb=gmlp_ln_b[l][None], ws=gmlp_w_s[l], bs=gmlp_b_s[l][:, :, None])

    pending = []

    def emit(l, part, g, mod):
        if part == "mix":
            names = mixers
            by_chip = [_col_shards(mix_in_unext(g["mix_in"])), mix_out_unext(g["mix_out"]).reshape(N_CHIPS, D // N_CHIPS, D),
                       _col_shards(uq_unext(g["wq"])), _col_shards(ukv_unext(g["wkv"]))]
        else:
            names = [part + "_w_in", part + "_w_out"]
            by_chip = [g[part + "_in"], g[part + "_out"]]
        tag = "%d_%s" % (l, part)
        got = sibling_swap(by_chip, "sibling_swap_" + tag)
        chip_sum = [add_kept_half(a, r, core, "add_sibling") for a, r in zip(by_chip, got)]
        slot = lax.broadcasted_iota(jnp.int32, (N_CHIPS, 1, 1), 0)
        lands = [jnp.where(slot == chip, h, 0.0) for h in chip_sum]
        handle = rs_start(chip_sum, lands, "rs_start_" + tag)
        pending.append((l, names, handle, tag))
        return mod + handle[-1][0, 0]

    loss_tile, dx, dmods, grads, d_logits = local_step(
        x.reshape(T, D), c8, loss_target.reshape(T, D), get, hgrn_lb_logits, S, emit)
    loss = lax.psum(loss_tile[0, 0], ("x", "y", "c"))

    small_g = {"hgrn_lb_logits": d_logits,
               "hgrn_norm_g": jnp.stack([grads[l]["ng"][0] for l in range(DEPTH)]),
               "mla_q_norm_g": jnp.stack([grads[l]["gq"][0] for l in range(DEPTH)]),
               "mla_kv_norm_g": jnp.stack([grads[l]["gkv"][0] for l in range(DEPTH)]),
               "fox_b_f": jnp.stack([grads[l]["bcol"][0:HEADS, 0] for l in range(DEPTH)]),
               "gmlp_ln_g": jnp.stack([grads[l]["g_lg"][0] for l in range(DEPTH)]),
               "gmlp_ln_b": jnp.stack([grads[l]["g_lb"][0] for l in range(DEPTH)]),
               "gmlp_w_s": jnp.stack([grads[l]["ws"] for l in range(DEPTH)]),
               "gmlp_b_s": jnp.stack([grads[l]["bs"][:, :, 0] for l in range(DEPTH)])}
    small_g["ln_g"] = jnp.stack([grads[l]["ln_g"] for l in range(DEPTH)])
    small_g["ln_b"] = jnp.stack([grads[l]["ln_b"] for l in range(DEPTH)])
    pk_small = pack_small(small_g)
    n_small = pk_small.shape[0]
    extras = [dmods[l] for l in range(DEPTH)] + [c]
    n_extra = _round_up(-(-sum(int(np.prod(e.shape)) for e in extras) // D), 8)
    gathered = ag_all(jnp.concatenate([pk_small, _rows(extras, n_extra, F32)], axis=0))
    g_small = unpack_small(sum_slots(gathered[:, 0:n_small], 8, "sum_small"), small_g)
    ext = gathered[:, n_small:].reshape(8, -1)
    n_dmod = DEPTH * Bl * N_MOD * D
    dmod_all = ext[:, 0:n_dmod].reshape(8, DEPTH, Bl, N_MOD * D)
    c_all = ext[:, n_dmod:n_dmod + Bl * D].reshape(8 * Bl, D)
    g_ada_w, g_ada_b = [], []
    ncol = N_MOD * D // N_CHIPS
    for l in range(DEPTH):
        dm = dmod_all[:, l].reshape(8 * Bl, N_MOD * D)
        g_ada_w.append(ada_grad(c_all, lax.dynamic_slice_in_dim(dm, chip * ncol, ncol, axis=1)))
        g_ada_b.append(sum_slots(dm.reshape(8 * Bl, N_MOD, D), 8 * Bl, "sum_ada_b").reshape(N_MOD * D))
    g_ada_w, g_ada_b = jnp.stack(g_ada_w), jnp.stack(g_ada_b)

    red = {n: jnp.zeros(w[n].shape, F32) for n in REDUCED}
    for l, names, handle, tag in pending:
        for n, land in zip(names, rs_wait(handle, dx, "rs_wait_" + tag)):
            red[n] = sum_into(land, red[n], l, core, "sum_chips")
    g_shard = dict(zip(REDUCED, sibling_join([red[n] for n in REDUCED])))

    grad = dict(g_shard)
    grad.update(g_small)
    grad["ada_w"], grad["ada_b"] = g_ada_w, g_ada_b
    for n in ("ln_g", "ln_b"):
        grad[n] = lax.dynamic_slice_in_dim(g_small[n], chip * (D // N_CHIPS), D // N_CHIPS, axis=2)
    out = {"grad": grad, "delta": {}, "new_m": {}, "new_v": {}}
    for n in WEIGHTS:
        shp = w[n].shape
        two_d = (-1, shp[-1])
        res = adamw(w[n].reshape(two_d), grad[n].reshape(two_d), m[n].reshape(two_d), v[n].reshape(two_d), "adamw_" + n)
        grad[n] = grad[n].reshape(shp)
        for key, r in zip(("delta", "new_m", "new_v"), res):
            out[key][n] = r.reshape(shp)
    outs = [loss, dx.reshape(Bl, S, D)]
    for key in ("grad", "delta", "new_m", "new_v"):
        outs += [out[key][n] for n in WEIGHTS]
    return tuple(outs)
```

```python
import functools

import jax
import jax.numpy as jnp
import numpy as np
from jax import lax
from jax.experimental import pallas as pl
from jax.experimental.pallas import tpu as pltpu

F32, BF16 = jnp.float32, jnp.bfloat16
MESH = pl.DeviceIdType.MESH

N_CHIPS = 4
D = 1024
DEPTH = 2
FF = 2816
N_MOD = 9
GW = 256
HEADS = 4
HD = 64
HP = 128
A_CHUNK = 16
LB_FLOOR = 1e-30
B_Q_LORA, B_KV_LORA, B_NOPE, B_ROPE = 256, 128, 64, 32
ROPE_THETA = 10000.0
D_CHUNK = 128
MIX_COLS = 2724
ALPHA = (2 * DEPTH) ** 0.25
LN_EPS = 1e-5
RMS_EPS = 1e-6
ADAM_LR, ADAM_B1, ADAM_B2, ADAM_EPS, ADAM_WD, ADAM_STEP = 0.001, 0.9, 0.999, 1e-08, 0.01, 10

NP = 3840
NP_TILE = 1920
C_A, C_B, C_CQ, C_CK, C_CV, C_D, C_CF = 0, 1024, 1536, 2048, 2560, 3072, 3584
NCAT = 1536
O_A, O_B, O_C, O_D = 0, 256, 768, 1280

VMEM_BIG = 48 << 20


def _cparams(vmem=None):
    return pltpu.CompilerParams(vmem_limit_bytes=vmem) if vmem else pltpu.CompilerParams()


def _sds(shape, dtype):
    return jax.ShapeDtypeStruct(tuple(shape), dtype)


@jax.custom_vjp
def _mm(a, w):
    return jnp.dot(a.astype(BF16), w.astype(BF16), preferred_element_type=F32)


def _mm_f(a, w):
    return _mm(a, w), (a, w)


def _mm_b(res, g):
    a, w = res
    gb = g.astype(BF16)
    da = lax.dot_general(gb, w.astype(BF16), (((1,), (1,)), ((), ())), preferred_element_type=F32)
    dw = lax.dot_general(a.astype(BF16), gb, (((0,), (0,)), ((), ())), preferred_element_type=F32)
    return da.astype(a.dtype), dw.astype(w.dtype)


_mm.defvjp(_mm_f, _mm_b)


@jax.custom_vjp
def _mm_nt(a, b):
    return lax.dot_general(a.astype(BF16), b.astype(BF16), (((1,), (1,)), ((), ())), preferred_element_type=F32)


def _mm_nt_f(a, b):
    return _mm_nt(a, b), (a, b)


def _mm_nt_b(res, g):
    a, b = res
    gb = g.astype(BF16)
    da = jnp.dot(gb, b.astype(BF16), preferred_element_type=F32)
    db = lax.dot_general(gb, a.astype(BF16), (((0,), (0,)), ((), ())), preferred_element_type=F32)
    return da.astype(a.dtype), db.astype(b.dtype)


_mm_nt.defvjp(_mm_nt_f, _mm_nt_b)


@jax.custom_vjp
def _mm_tn(a, b):
    return lax.dot_general(a.astype(BF16), b.astype(BF16), (((0,), (0,)), ((), ())), preferred_element_type=F32)


def _mm_tn_f(a, b):
    return _mm_tn(a, b), (a, b)


def _mm_tn_b(res, g):
    a, b = res
    gb = g.astype(BF16)
    da = lax.dot_general(b.astype(BF16), gb, (((1,), (1,)), ((), ())), preferred_element_type=F32)
    db = jnp.dot(a.astype(BF16), gb, preferred_element_type=F32)
    return da.astype(a.dtype), db.astype(b.dtype)


_mm_tn.defvjp(_mm_tn_f, _mm_tn_b)


def _mm_hi(a, w):
    return jnp.dot(a, w, precision=lax.Precision.HIGHEST, preferred_element_type=F32)


def _iota(shape, dim):
    return lax.broadcasted_iota(jnp.int32, shape, dim)


def _head_sum_mats():
    e = (_iota((GW, HP), 0) // HD == _iota((GW, HP), 1)).astype(F32)
    et = (_iota((HP, GW), 1) // HD == _iota((HP, GW), 0)).astype(F32)
    return e, et


def _modulate(x, mod_ref, sh, sc):
    return x * (1.0 + mod_ref[sc:sc + 1, :]) + mod_ref[sh:sh + 1, :]


def _ln_res(x, y, gate, lg, lb, gs):
    r = ALPHA * x + gs * (1.0 + gate) * y
    mu = jnp.mean(r, axis=-1, keepdims=True)
    var = jnp.mean(jnp.square(r - mu), axis=-1, keepdims=True)
    return (r - mu) * lax.rsqrt(var + LN_EPS) * lg + lb


def _rms(x, g):
    return x * lax.rsqrt(jnp.mean(x * x, axis=-1, keepdims=True) + RMS_EPS) * g


def _tile(n, pref):
    return pref if n % pref == 0 else n


def mod_fwd(c8, w, l, b):
    tn = w.shape[3]
    n = N_CHIPS * tn

    def body(c_ref, w_ref, b_ref, o_ref):
        h = jax.nn.silu(c_ref[...]).astype(BF16)
        o_ref[...] = jnp.dot(h, w_ref[...], preferred_element_type=F32) + b_ref[...]

    return pl.pallas_call(
        body, name="mod_fwd", grid=(N_CHIPS,),
        in_specs=[pl.BlockSpec((8, D), lambda j: (0, 0)), pl.BlockSpec((None, None, D, tn), lambda j: (l, j, 0, 0)),
                  pl.BlockSpec((1, tn), lambda j: (0, j))],
        out_specs=pl.BlockSpec((8, tn), lambda j: (0, j)), out_shape=_sds((8, n), F32),
        compiler_params=_cparams(VMEM_BIG))(c8, w, b)


def ffn_in_fwd(x, mod, w_in, l, sh, sc, S):
    T = x.shape[0]
    tm, tn = _tile(S, 512), FF // 2
    tpb, nj = S // tm, 2

    def body(x_ref, mod_ref, wg_ref, wu_ref, zg_ref, zu_ref, act_ref, h_ref):
        @pl.when(pl.program_id(1) == 0)
        def _():
            h_ref[...] = _modulate(x_ref[...], mod_ref, sh, sc).astype(BF16)
        g = jnp.dot(h_ref[...], wg_ref[...], preferred_element_type=F32)
        u = jnp.dot(h_ref[...], wu_ref[...], preferred_element_type=F32)
        zg_ref[...] = g
        zu_ref[...] = u
        act_ref[...] = (jax.nn.silu(g) * u).astype(BF16)

    return pl.pallas_call(
        body, name="ffn_in_fwd", grid=(T // tm, nj),
        in_specs=[pl.BlockSpec((tm, D), lambda i, j: (i, 0)),
                  pl.BlockSpec((None, N_MOD, D), lambda i, j: (i // tpb, 0, 0)),
                  pl.BlockSpec((None, None, D, tn), lambda i, j: (l, j, 0, 0)),
                  pl.BlockSpec((None, None, D, tn), lambda i, j: (l, j + nj, 0, 0))],
        out_specs=[pl.BlockSpec((tm, tn), lambda i, j: (i, j))] * 3,
        out_shape=[_sds((T, FF), F32), _sds((T, FF), F32), _sds((T, FF), BF16)],
        scratch_shapes=[pltpu.VMEM((tm, D), BF16)],
        compiler_params=_cparams(VMEM_BIG))(x, mod, w_in, w_in)


def mix_in_fwd(x, mod, w, sh, sc, S):
    T = x.shape[0]
    n = w.shape[1]
    tm, tn = _tile(S, 512), NP_TILE
    tpb = S // tm

    def body(x_ref, mod_ref, w_ref, o_ref, h_ref):
        @pl.when(pl.program_id(1) == 0)
        def _():
            h_ref[...] = _modulate(x_ref[...], mod_ref, sh, sc).astype(BF16)
        o_ref[...] = jnp.dot(h_ref[...], w_ref[...], preferred_element_type=F32)

    return pl.pallas_call(
        body, name="mix_in_fwd", grid=(T // tm, n // tn),
        in_specs=[pl.BlockSpec((tm, D), lambda i, j: (i, 0)),
                  pl.BlockSpec((None, N_MOD, D), lambda i, j: (i // tpb, 0, 0)),
                  pl.BlockSpec((D, tn), lambda i, j: (0, j))],
        out_specs=pl.BlockSpec((tm, tn), lambda i, j: (i, j)), out_shape=_sds((T, n), F32),
        scratch_shapes=[pltpu.VMEM((tm, D), BF16)],
        compiler_params=_cparams(VMEM_BIG))(x, mod, w)


def out_ln_fwd(act, w_out, x, mod, lg, lb, gate, gs, S, l=None):
    T, K = act.shape
    tm = _tile(S, 512)
    tpb = S // tm

    def body(a_ref, w_ref, x_ref, mod_ref, lg_ref, lb_ref, y_ref, xn_ref):
        y = jnp.dot(a_ref[...], w_ref[...].reshape(K, D), preferred_element_type=F32)
        y_ref[...] = y
        xn_ref[...] = _ln_res(x_ref[...], y, mod_ref[gate:gate + 1, :], lg_ref[...], lb_ref[...], gs)

    if l is None:
        w_spec = pl.BlockSpec((K, D), lambda i: (0, 0))
    else:
        w_spec = pl.BlockSpec((None, N_CHIPS, K // N_CHIPS, D), lambda i: (l, 0, 0, 0))
    return pl.pallas_call(
        body, name="out_ln_fwd", grid=(T // tm,),
        in_specs=[pl.BlockSpec((tm, K), lambda i: (i, 0)), w_spec,
                  pl.BlockSpec((tm, D), lambda i: (i, 0)),
                  pl.BlockSpec((None, N_MOD, D), lambda i: (i // tpb, 0, 0)),
                  pl.BlockSpec((1, D), lambda i: (0, 0)), pl.BlockSpec((1, D), lambda i: (0, 0))],
        out_specs=[pl.BlockSpec((tm, D), lambda i: (i, 0))] * 2,
        out_shape=[_sds((T, D), F32), _sds((T, D), F32)],
        compiler_params=_cparams(VMEM_BIG))(act, w_out, x, mod, lg, lb)


def ln_res_bwd(dxn, x, y, mod, lg, lb, gate, gs, S):
    T = x.shape[0]
    B = T // S
    tm = _tile(S, 512)
    tpb = S // tm

    def body(d_ref, x_ref, y_ref, mod_ref, lg_ref, lb_ref, dx_ref, dy_ref, dg_ref, dlg_ref, dlb_ref):
        i = pl.program_id(0)
        f = functools.partial(_ln_res, gs=gs)
        _, vjp = jax.vjp(f, x_ref[...], y_ref[...], mod_ref[gate:gate + 1, :], lg_ref[...], lb_ref[...])
        dx, dy, dg, dlg, dlb = vjp(d_ref[...])
        dx_ref[...] = dx
        dy_ref[...] = dy.astype(BF16)

        @pl.when(i % tpb == 0)
        def _():
            dg_ref[...] = jnp.zeros_like(dg_ref)

        @pl.when(i == 0)
        def _():
            dlg_ref[...] = jnp.zeros_like(dlg_ref)
            dlb_ref[...] = jnp.zeros_like(dlb_ref)

        dg_ref[...] += dg
        dlg_ref[...] += dlg
        dlb_ref[...] += dlb

    tok = pl.BlockSpec((tm, D), lambda i: (i, 0))
    vec = pl.BlockSpec((1, D), lambda i: (0, 0))
    return pl.pallas_call(
        body, name="ln_res_bwd", grid=(T // tm,),
        in_specs=[tok, tok, tok, pl.BlockSpec((None, N_MOD, D), lambda i: (i // tpb, 0, 0)), vec, vec],
        out_specs=[tok, tok, pl.BlockSpec((None, 1, D), lambda i: (i // tpb, 0, 0)), vec, vec],
        out_shape=[_sds((T, D), F32), _sds((T, D), BF16), _sds((B, 1, D), F32), _sds((1, D), F32), _sds((1, D), F32)],
        compiler_params=_cparams(VMEM_BIG))(dxn, x, y, mod, lg, lb)


def swiglu_bwd(dy, w_out, l, zg, zu, S):
    T = dy.shape[0]
    tm, tn = _tile(S, 512), FF // 2

    def body(dy_ref, w_ref, zg_ref, zu_ref, dg_ref, du_ref):
        da = lax.dot_general(dy_ref[...], w_ref[...].reshape(tn, D), (((1,), (1,)), ((), ())), preferred_element_type=F32)
        g, u = zg_ref[...], zu_ref[...]
        sg = jax.nn.sigmoid(g)
        dg_ref[...] = (da * u * (sg * (1.0 + g * (1.0 - sg)))).astype(BF16)
        du_ref[...] = (da * (g * sg)).astype(BF16)

    zt = pl.BlockSpec((tm, tn), lambda i, j: (i, j))
    return pl.pallas_call(
        body, name="swiglu_bwd", grid=(T // tm, FF // tn),
        in_specs=[pl.BlockSpec((tm, D), lambda i, j: (i, 0)),
                  pl.BlockSpec((None, 2, FF // N_CHIPS, D), lambda i, j: (l, j, 0, 0)), zt, zt],
        out_specs=[zt, zt], out_shape=[_sds((T, FF), BF16), _sds((T, FF), BF16)],
        compiler_params=_cparams(VMEM_BIG))(dy, w_out, zg, zu)


def nt_plain(dy, w):
    T = dy.shape[0]
    K = w.shape[0]
    tm = _tile(T, 512)

    def body(dy_ref, w_ref, o_ref):
        o_ref[...] = lax.dot_general(dy_ref[...], w_ref[...], (((1,), (1,)), ((), ())), preferred_element_type=F32)

    return pl.pallas_call(
        body, name="nt_plain", grid=(T // tm,),
        in_specs=[pl.BlockSpec((tm, D), lambda i: (i, 0)), pl.BlockSpec((K, D), lambda i: (0, 0))],
        out_specs=pl.BlockSpec((tm, K), lambda i: (i, 0)), out_shape=_sds((T, K), F32),
        compiler_params=_cparams(VMEM_BIG))(dy, w)


def tn_mm(a, b, tk):
    T, K = a.shape
    N = b.shape[1]
    tt = _tile(T, 512)

    def body(a_ref, b_ref, o_ref):
        @pl.when(pl.program_id(1) == 0)
        def _():
            o_ref[...] = jnp.zeros_like(o_ref)
        o_ref[...] += lax.dot_general(a_ref[...], b_ref[...], (((0,), (0,)), ((), ())), preferred_element_type=F32)

    return pl.pallas_call(
        body, name="tn_mm", grid=(K // tk, T // tt),
        in_specs=[pl.BlockSpec((tt, tk), lambda k, t: (t, k)), pl.BlockSpec((tt, N), lambda k, t: (t, 0))],
        out_specs=pl.BlockSpec((tk, N), lambda k, t: (k, 0)), out_shape=_sds((K, N), F32),
        compiler_params=_cparams(VMEM_BIG))(a, b)


def tn_mm_mod(x, mod, b, sh, sc, S, tn):
    T = x.shape[0]
    N = b.shape[1]
    tt = _tile(S, 512)
    tpb = S // tt

    def body(x_ref, mod_ref, b_ref, o_ref):
        @pl.when(pl.program_id(1) == 0)
        def _():
            o_ref[...] = jnp.zeros_like(o_ref)
        h = _modulate(x_ref[...], mod_ref, sh, sc).astype(BF16)
        o_ref[...] += lax.dot_general(h, b_ref[...], (((0,), (0,)), ((), ())), preferred_element_type=F32)

    return pl.pallas_call(
        body, name="tn_mm_mod", grid=(N // tn, T // tt),
        in_specs=[pl.BlockSpec((tt, D), lambda j, t: (t, 0)),
                  pl.BlockSpec((None, N_MOD, D), lambda j, t: (t // tpb, 0, 0)),
                  pl.BlockSpec((tt, tn), lambda j, t: (t, j))],
        out_specs=pl.BlockSpec((D, tn), lambda j, t: (0, j)), out_shape=_sds((D, N), F32),
        compiler_params=_cparams(VMEM_BIG))(x, mod, b)


def tn_mm_mod_shards(x, mod, bg, bu, sh, sc, S):
    T = x.shape[0]
    tn = FF // 2
    tt = _tile(S, 512)
    tpb = S // tt

    def body(x_ref, mod_ref, bg_ref, bu_ref, o_ref):
        j = pl.program_id(0)

        @pl.when(pl.program_id(1) == 0)
        def _():
            o_ref[...] = jnp.zeros_like(o_ref)
        h = _modulate(x_ref[...], mod_ref, sh, sc).astype(BF16)

        @pl.when(j < 2)
        def _():
            o_ref[...] += lax.dot_general(h, bg_ref[...], (((0,), (0,)), ((), ())), preferred_element_type=F32)

        @pl.when(j >= 2)
        def _():
            o_ref[...] += lax.dot_general(h, bu_ref[...], (((0,), (0,)), ((), ())), preferred_element_type=F32)

    return pl.pallas_call(
        body, name="tn_mm_mod_shards", grid=(N_CHIPS, T // tt),
        in_specs=[pl.BlockSpec((tt, D), lambda j, t: (t, 0)),
                  pl.BlockSpec((None, N_MOD, D), lambda j, t: (t // tpb, 0, 0)),
                  pl.BlockSpec((tt, tn), lambda j, t: (t, jnp.minimum(j, 1))),
                  pl.BlockSpec((tt, tn), lambda j, t: (t, jnp.maximum(j - 2, 0)))],
        out_specs=pl.BlockSpec((None, D, tn), lambda j, t: (j, 0, 0)), out_shape=_sds((N_CHIPS, D, tn), F32),
        compiler_params=_cparams(VMEM_BIG))(x, mod, bg, bu)


def nt_mod_bwd(ds, w, offs, x, mod, dres, sc, S, tk, l=None):
    T = x.shape[0]
    B = T // S
    tm = _tile(S, 512)
    tpb = S // tm
    Kd = ds[0].shape[1]
    nk = Kd // tk
    n_in = len(ds)

    def body(*refs):
        d_refs, w_refs = refs[:n_in], refs[n_in:2 * n_in]
        x_ref, mod_ref, r_ref, dx_ref, dsh_ref, dsc_ref, acc = refs[2 * n_in:]
        i, k = pl.program_id(0), pl.program_id(1)

        @pl.when(k == 0)
        def _():
            acc[...] = jnp.zeros_like(acc)

        for d_ref, w_ref in zip(d_refs, w_refs):
            acc[...] += lax.dot_general(d_ref[...], w_ref[...], (((1,), (1,)), ((), ())), preferred_element_type=F32)

        @pl.when(k == nk - 1)
        def _():
            dh = acc[...]
            dx_ref[...] = dh * (1.0 + mod_ref[sc:sc + 1, :]) + r_ref[...]

            @pl.when(i % tpb == 0)
            def _():
                dsh_ref[...] = jnp.zeros_like(dsh_ref)
                dsc_ref[...] = jnp.zeros_like(dsc_ref)

            dsh_ref[...] += jnp.sum(dh, axis=0, keepdims=True)
            dsc_ref[...] += jnp.sum(dh * x_ref[...], axis=0, keepdims=True)

    tok = pl.BlockSpec((tm, D), lambda i, k: (i, 0))
    vec = pl.BlockSpec((None, 1, D), lambda i, k: (i // tpb, 0, 0))
    in_specs = [pl.BlockSpec((tm, tk), lambda i, k: (i, k)) for _ in ds]
    if l is None:
        in_specs += [pl.BlockSpec((D, tk), functools.partial(lambda i, k, o: (0, k + o), o=off // tk)) for off in offs]
    else:
        in_specs += [pl.BlockSpec((None, None, D, tk), functools.partial(lambda i, k, o: (l, k + o, 0, 0), o=off)) for off in offs]
    in_specs += [tok, pl.BlockSpec((None, N_MOD, D), lambda i, k: (i // tpb, 0, 0)), tok]
    return pl.pallas_call(
        body, name="nt_mod_bwd", grid=(T // tm, nk), in_specs=in_specs,
        out_specs=[tok, vec, vec],
        out_shape=[_sds((T, D), F32), _sds((B, 1, D), F32), _sds((B, 1, D), F32)],
        scratch_shapes=[pltpu.VMEM((tm, D), F32)],
        compiler_params=_cparams(VMEM_BIG))(*ds, *([w] * n_in), x, mod, dres)


def loss_head(y, tgt):
    T = y.shape[0]
    tm = _tile(T, 512)

    def body(y_ref, t_ref, l_ref, d_ref):
        @pl.when(pl.program_id(0) == 0)
        def _():
            l_ref[...] = jnp.zeros_like(l_ref)
        e = y_ref[...] - t_ref[...]
        d_ref[...] = e * (1.0 / D)
        l_ref[...] += 0.5 * jnp.sum(jnp.sum(e * e, axis=1, keepdims=True) * (1.0 / D))

    tok = pl.BlockSpec((tm, D), lambda i: (i, 0))
    return pl.pallas_call(
        body, name="loss_head", grid=(T // tm,), in_specs=[tok, tok],
        out_specs=[pl.BlockSpec((8, 128), lambda i: (0, 0)), tok],
        out_shape=[_sds((8, 128), F32), _sds((T, D), F32)],
        compiler_params=_cparams(VMEM_BIG))(y, tgt)


def _hgrn_block(q, fz, inp, go, st, lb, ng, blk):
    nc = blk // A_CHUNK
    lb_eff = jnp.maximum(lb, LB_FLOOR)
    log_f = jnp.logaddexp(jnp.log(lb_eff), jnp.log1p(-lb) + jax.nn.log_sigmoid(fz))
    k = (1.0 - lb) * jax.nn.sigmoid(-fz) - (lb_eff - lb)
    qf = jax.nn.silu(q)
    same_chunk = _iota((blk, blk), 0) // A_CHUNK == _iota((blk, blk), 1) // A_CHUNK
    tril = (same_chunk & (_iota((blk, blk), 1) <= _iota((blk, blk), 0))).astype(F32)
    G = _mm_hi(tril, log_f)
    e_mat, et_mat = _head_sum_mats()
    G4, q4, k4, v4 = (z.reshape(nc, A_CHUNK, GW) for z in (G, qf, k, inp))
    shp = (nc, A_CHUNK, A_CHUNK, GW)
    causal = _iota(shp, 2) <= _iota(shp, 1)
    decay = jnp.exp(jnp.where(causal, G4[:, :, None, :] - G4[:, None, :, :], -jnp.inf))
    prod = q4[:, :, None, :] * k4[:, None, :, :] * decay
    scores = _mm(prod.reshape(nc * A_CHUNK * A_CHUNK, GW), e_mat.astype(BF16))
    spread = _mm(scores, et_mat.astype(BF16)).reshape(shp)
    o_intra = jnp.sum(spread * v4[:, None, :, :], axis=2).reshape(blk, GW)
    head_diag = (_iota((GW, GW), 0) // HD == _iota((GW, GW), 1) // HD).astype(F32)
    g_last = [jnp.sum(log_f[c * A_CHUNK:(c + 1) * A_CHUNK], axis=0, keepdims=True) for c in range(nc)]
    g_last_b = jnp.concatenate([jnp.broadcast_to(g, (A_CHUNK, GW)) for g in g_last], axis=0)
    q_dec = qf * jnp.exp(G)
    k_end = k * jnp.exp(g_last_b - G)
    outs = []
    for c in range(nc):
        rows = slice(c * A_CHUNK, (c + 1) * A_CHUNK)
        outs.append(_mm_nt(q_dec[rows], st))
        st = st * jnp.exp(g_last[c]) + _mm_tn(inp[rows], k_end[rows]) * head_diag
    o = o_intra + jnp.concatenate(outs, axis=0)
    ms = _mm_hi(o * o, e_mat) * (1.0 / HD)
    o = o * _mm_hi(lax.rsqrt(ms + RMS_EPS), et_mat) * ng
    return o * jax.nn.silu(go), st


HGRN_BLK = 128


def hgrn_fwd(proj, lb, ng, S):
    T = proj.shape[0]
    B = T // S
    blk = min(HGRN_BLK, S)
    nb = S // blk

    def body(p_ref, lb_ref, ng_ref, o_ref, st_out_ref, st_ref):
        @pl.when(pl.program_id(1) == 0)
        def _():
            st_ref[...] = jnp.zeros_like(st_ref)
        st_out_ref[...] = st_ref[...]
        p = p_ref[...]
        o, st = _hgrn_block(p[:, 0:GW], p[:, GW:2 * GW], p[:, 2 * GW:3 * GW], p[:, 3 * GW:4 * GW],
                            st_ref[...], lb_ref[...], ng_ref[...], blk)
        o_ref[...] = o.astype(BF16)
        st_ref[...] = st

    vec = pl.BlockSpec((1, GW), lambda b, j: (0, 0))
    return pl.pallas_call(
        body, name="hgrn_fwd", grid=(B, nb),
        in_specs=[pl.BlockSpec((blk, 4 * GW), lambda b, j: (b * nb + j, C_A // (4 * GW))), vec, vec],
        out_specs=[pl.BlockSpec((blk, GW), lambda b, j: (b * nb + j, 0)),
                   pl.BlockSpec((None, GW, GW), lambda b, j: (b * nb + j, 0, 0))],
        out_shape=[_sds((T, GW), BF16), _sds((B * nb, GW, GW), F32)],
        scratch_shapes=[pltpu.VMEM((GW, GW), F32)],
        compiler_params=_cparams(VMEM_BIG))(proj, lb, ng)


def hgrn_bwd(proj, states, dcat, lb, ng, S):
    T = proj.shape[0]
    B = T // S
    blk = min(HGRN_BLK, S)
    nb = S // blk

    def body(p_ref, st_in_ref, do_ref, lb_ref, ng_ref, dp_ref, dlb_ref, dng_ref, dst_ref):
        b, j = pl.program_id(0), pl.program_id(1)

        @pl.when(j == 0)
        def _():
            dst_ref[...] = jnp.zeros_like(dst_ref)

        @pl.when((b == 0) & (j == 0))
        def _():
            dlb_ref[...] = jnp.zeros_like(dlb_ref)
            dng_ref[...] = jnp.zeros_like(dng_ref)

        p = p_ref[...]
        f = functools.partial(_hgrn_block, blk=blk)
        _, vjp = jax.vjp(f, p[:, 0:GW], p[:, GW:2 * GW], p[:, 2 * GW:3 * GW], p[:, 3 * GW:4 * GW],
                         st_in_ref[...], lb_ref[...], ng_ref[...])
        dq, df, di, dg, dst, dlb, dng = vjp((do_ref[...], dst_ref[...]))
        dp_ref[...] = jnp.concatenate([dq, df, di, dg], axis=1).astype(BF16)
        dst_ref[...] = dst
        dlb_ref[...] += dlb
        dng_ref[...] += dng

    def rev(b, j):
        return b * nb + (nb - 1 - j)

    vec = pl.BlockSpec((1, GW), lambda b, j: (0, 0))
    return pl.pallas_call(
        body, name="hgrn_bwd", grid=(B, nb),
        in_specs=[pl.BlockSpec((blk, 4 * GW), lambda b, j: (rev(b, j), C_A // (4 * GW))),
                  pl.BlockSpec((None, GW, GW), lambda b, j: (rev(b, j), 0, 0)),
                  pl.BlockSpec((blk, GW), lambda b, j: (rev(b, j), O_A // GW)), vec, vec],
        out_specs=[pl.BlockSpec((blk, 4 * GW), lambda b, j: (rev(b, j), 0)), vec, vec],
        out_shape=[_sds((T, 4 * GW), BF16), _sds((1, GW), F32), _sds((1, GW), F32)],
        scratch_shapes=[pltpu.VMEM((GW, GW), F32)],
        compiler_params=_cparams(VMEM_BIG))(proj, states, dcat, lb, ng)


ATT_TQ = 256


def _attn_block(q, k, v, cum, qpos0, scale, use_cum):
    s = _mm_nt(q, k) * scale
    if use_cum:
        s = s - cum
    qpos = qpos0 + _iota(s.shape, 0)
    s = jnp.where(_iota(s.shape, 1) <= qpos, s, -jnp.inf)
    e = jnp.exp(s - jnp.max(s, axis=-1, keepdims=True))
    p = e / jnp.sum(e, axis=-1, keepdims=True)
    return _mm(p, v)


def attn_fwd(qa, qo, ka, ko, va, vo, cum, scale, S):
    T = qa.shape[0]
    B = T // S
    tq = min(ATT_TQ, S)
    nq = S // tq
    use_cum = cum is not None

    def body(*refs):
        if use_cum:
            q_ref, k_ref, v_ref, c_ref, o_ref = refs
            crow = c_ref[pl.ds(pl.program_id(1), 1), :]
        else:
            q_ref, k_ref, v_ref, o_ref = refs
            crow = None
        o = _attn_block(q_ref[...], k_ref[...], v_ref[...], crow, pl.program_id(2) * tq, scale, use_cum)
        o_ref[...] = o.astype(BF16)

    in_specs = [pl.BlockSpec((tq, HP), lambda b, h, i: (b * nq + i, qo + h)),
                pl.BlockSpec((S, HP), lambda b, h, i: (b, ko + h)),
                pl.BlockSpec((S, HP), lambda b, h, i: (b, vo + h))]
    args = [qa, ka, va]
    if use_cum:
        in_specs.append(pl.BlockSpec((None, 8, S), lambda b, h, i: (b, 0, 0)))
        args.append(cum)
    return pl.pallas_call(
        body, name="attn_fwd", grid=(B, HEADS, nq), in_specs=in_specs,
        out_specs=pl.BlockSpec((tq, HP), lambda b, h, i: (b * nq + i, h)),
        out_shape=_sds((T, HEADS * HP), BF16),
        compiler_params=_cparams(VMEM_BIG))(*args)


def attn_bwd(qa, qo, ka, ko, va, vo, cum, dcat, do_off, scale, S, out_dtype):
    T = qa.shape[0]
    B = T // S
    tq = min(ATT_TQ, S)
    nq = S // tq
    use_cum = cum is not None

    def body(*refs):
        if use_cum:
            q_ref, k_ref, v_ref, do_ref, c_ref, dq_ref, dk_ref, dv_ref, dc_ref, dk_acc, dv_acc = refs
            crow = c_ref[pl.ds(pl.program_id(1), 1), :]
        else:
            q_ref, k_ref, v_ref, do_ref, dq_ref, dk_ref, dv_ref, dk_acc, dv_acc = refs
            crow = jnp.zeros((1, S), F32)
        i = pl.program_id(2)

        @pl.when(i == 0)
        def _():
            dk_acc[...] = jnp.zeros_like(dk_acc)
            dv_acc[...] = jnp.zeros_like(dv_acc)
            if use_cum:
                dc_ref[...] = jnp.zeros_like(dc_ref)

        f = functools.partial(_attn_block, qpos0=i * tq, scale=scale, use_cum=use_cum)
        _, vjp = jax.vjp(f, q_ref[...], k_ref[...], v_ref[...], crow)
        dq, dk, dv, dc = vjp(do_ref[...])
        dq_ref[...] = dq.astype(out_dtype)
        dk_acc[...] += dk
        dv_acc[...] += dv
        if use_cum:
            dc_ref[...] += dc

        @pl.when(i == nq - 1)
        def _():
            dk_ref[...] = dk_acc[...].astype(out_dtype)
            dv_ref[...] = dv_acc[...].astype(out_dtype)

    qspec = pl.BlockSpec((tq, HP), lambda b, h, i: (b * nq + i, qo + h))
    in_specs = [qspec, pl.BlockSpec((S, HP), lambda b, h, i: (b, ko + h)),
                pl.BlockSpec((S, HP), lambda b, h, i: (b, vo + h)),
                pl.BlockSpec((tq, HP), lambda b, h, i: (b * nq + i, do_off + h))]
    args = [qa, ka, va, dcat]
    kv_out = pl.BlockSpec((S, HP), lambda b, h, i: (b, h))
    out_specs = [pl.BlockSpec((tq, HP), lambda b, h, i: (b * nq + i, h)), kv_out, kv_out]
    out_shape = [_sds((T, HEADS * HP), out_dtype)] * 3
    if use_cum:
        in_specs.append(pl.BlockSpec((None, 8, S), lambda b, h, i: (b, 0, 0)))
        args.append(cum)
        out_specs.append(pl.BlockSpec((None, 1, S), lambda b, h, i: (b * HEADS + h, 0, 0)))
        out_shape.append(_sds((B * HEADS, 1, S), F32))
    return pl.pallas_call(
        body, name="attn_bwd", grid=(B, HEADS, nq), in_specs=in_specs, out_specs=out_specs, out_shape=out_shape,
        scratch_shapes=[pltpu.VMEM((S, HP), F32), pltpu.VMEM((S, HP), F32)],
        compiler_params=_cparams(VMEM_BIG))(*args)


def _tri(n, upper):
    r, c = _iota((n, n), 0), _iota((n, n), 1)
    return ((r <= c) if upper else (r >= c)).astype(F32)


def fox_gate_fwd(proj, bcol, S):
    T = proj.shape[0]
    B = T // S
    ts = _tile(S, 512)
    nt = S // ts

    def body(p_ref, b_ref, o_ref, carry):
        @pl.when(pl.program_id(1) == 0)
        def _():
            carry[...] = jnp.zeros_like(carry)
        cf = jnp.transpose(p_ref[...])[0:8, :]
        lf = jax.nn.log_sigmoid(cf + b_ref[...])
        cum = _mm_hi(lf, _tri(ts, True)) + carry[...]
        o_ref[...] = cum
        carry[...] += jnp.sum(lf, axis=1, keepdims=True)

    return pl.pallas_call(
        body, name="fox_gate_fwd", grid=(B, nt),
        in_specs=[pl.BlockSpec((ts, HP), lambda b, j: (b * nt + j, C_CF // HP)), pl.BlockSpec((8, 1), lambda b, j: (0, 0))],
        out_specs=pl.BlockSpec((None, 8, ts), lambda b, j: (b, 0, j)), out_shape=_sds((B, 8, S), F32),
        scratch_shapes=[pltpu.VMEM((8, 1), F32)],
        compiler_params=_cparams(VMEM_BIG))(proj, bcol)


def fox_gate_bwd(proj, bcol, dcum, S):
    T = proj.shape[0]
    B = T // S
    ts = _tile(S, 512)
    nt = S // ts

    def body(p_ref, b_ref, dc_ref, dp_ref, db_ref, carry):
        b, j = pl.program_id(0), pl.program_id(1)

        @pl.when(j == 0)
        def _():
            carry[...] = jnp.zeros_like(carry)

        @pl.when((b == 0) & (j == 0))
        def _():
            db_ref[...] = jnp.zeros_like(db_ref)

        cf = jnp.transpose(p_ref[...])[0:8, :]
        dc = dc_ref[...]
        dlf = _mm_hi(dc, _tri(ts, False)) + carry[...]
        carry[...] += jnp.sum(dc, axis=1, keepdims=True)
        dcf = dlf * jax.nn.sigmoid(-(cf + b_ref[...]))
        db_ref[...] += jnp.sum(dcf, axis=1, keepdims=True)
        full = jnp.concatenate([dcf, jnp.zeros((HP - 8, ts), F32)], axis=0)
        dp_ref[...] = jnp.transpose(full).astype(BF16)

    def rev(b, j):
        return nt - 1 - j

    return pl.pallas_call(
        body, name="fox_gate_bwd", grid=(B, nt),
        in_specs=[pl.BlockSpec((ts, HP), lambda b, j: (b * nt + rev(b, j), C_CF // HP)),
                  pl.BlockSpec((8, 1), lambda b, j: (0, 0)),
                  pl.BlockSpec((None, 8, ts), lambda b, j: (b, 0, rev(b, j)))],
        out_specs=[pl.BlockSpec((ts, HP), lambda b, j: (b * nt + rev(b, j), 0)), pl.BlockSpec((8, 1), lambda b, j: (0, 0))],
        out_shape=[_sds((T, HP), BF16), _sds((8, 1), F32)],
        scratch_shapes=[pltpu.VMEM((8, 1), F32)],
        compiler_params=_cparams(VMEM_BIG))(proj, bcol, dcum)


def _mla_pre(blk, gq, gkv, wq, wkv, place, cos_q, sin_q, cs_k):
    nq = _rms(blk[:, 0:B_Q_LORA], gq)
    nkv = _rms(blk[:, B_Q_LORA:B_Q_LORA + B_KV_LORA], gkv)
    qq = _mm(nq, wq)
    q = qq[:, 0:HEADS * HP] * cos_q + qq[:, HEADS * HP:] * sin_q
    kv = _mm(nkv, wkv)
    k = kv[:, 0:HEADS * HP] + _mm(blk[:, B_Q_LORA + B_KV_LORA:] * cs_k, place)
    return q, k, kv[:, HEADS * HP:]


def mla_pre_fwd(proj, gq, gkv, wq, wkv, place, cos_q, sin_q, cs_k, S):
    T = proj.shape[0]
    tm = _tile(S, 512)
    tpb = S // tm
    W = HEADS * HP

    def body(p_ref, gq_ref, gkv_ref, wq_ref, wkv_ref, pl_ref, cq_ref, sq_ref, ck_ref, q_ref, k_ref, v_ref):
        q, k, v = _mla_pre(p_ref[...], gq_ref[...], gkv_ref[...], wq_ref[...], wkv_ref[...], pl_ref[...],
                           cq_ref[...], sq_ref[...], ck_ref[...])
        q_ref[...] = q
        k_ref[...] = k
        v_ref[...] = v

    def full(a):
        return pl.BlockSpec(a.shape, lambda i: (0,) * a.ndim)

    tok = pl.BlockSpec((tm, W), lambda i: (i, 0))
    return pl.pallas_call(
        body, name="mla_pre_fwd", grid=(T // tm,),
        in_specs=[pl.BlockSpec((tm, W), lambda i: (i, C_B // W)), full(gq), full(gkv), full(wq), full(wkv), full(place),
                  pl.BlockSpec((tm, W), lambda i: (i % tpb, 0)), pl.BlockSpec((tm, W), lambda i: (i % tpb, 0)),
                  pl.BlockSpec((tm, HP), lambda i: (i % tpb, 0))],
        out_specs=[tok] * 3, out_shape=[_sds((T, W), F32)] * 3,
        compiler_params=_cparams(VMEM_BIG))(proj, gq, gkv, wq, wkv, place, cos_q, sin_q, cs_k)


def mla_pre_bwd(proj, gq, gkv, wq, wkv, place, cos_q, sin_q, cs_k, dq, dk, dv, S):
    T = proj.shape[0]
    tm = _tile(S, 512)
    tpb = S // tm
    W = HEADS * HP

    def body(p_ref, gq_ref, gkv_ref, wq_ref, wkv_ref, pl_ref, cq_ref, sq_ref, ck_ref, dq_ref, dk_ref, dv_ref,
             dp_ref, dgq_ref, dgkv_ref, dwq_ref, dwkv_ref):
        @pl.when(pl.program_id(0) == 0)
        def _():
            for r in (dgq_ref, dgkv_ref, dwq_ref, dwkv_ref):
                r[...] = jnp.zeros_like(r)

        f = functools.partial(_mla_pre, place=pl_ref[...], cos_q=cq_ref[...], sin_q=sq_ref[...], cs_k=ck_ref[...])
        _, vjp = jax.vjp(f, p_ref[...], gq_ref[...], gkv_ref[...], wq_ref[...], wkv_ref[...])
        dp, dgq, dgkv, dwq, dwkv = vjp((dq_ref[...], dk_ref[...], dv_ref[...]))
        dp_ref[...] = dp.astype(BF16)
        dgq_ref[...] += dgq
        dgkv_ref[...] += dgkv
        dwq_ref[...] += dwq
        dwkv_ref[...] += dwkv

    def full(a):
        return pl.BlockSpec(a.shape, lambda i: (0,) * a.ndim)

    tok = pl.BlockSpec((tm, W), lambda i: (i, 0))
    return pl.pallas_call(
        body, name="mla_pre_bwd", grid=(T // tm,),
        in_specs=[pl.BlockSpec((tm, W), lambda i: (i, C_B // W)), full(gq), full(gkv), full(wq), full(wkv), full(place),
                  pl.BlockSpec((tm, W), lambda i: (i % tpb, 0)), pl.BlockSpec((tm, W), lambda i: (i % tpb, 0)),
                  pl.BlockSpec((tm, HP), lambda i: (i % tpb, 0)), tok, tok, tok],
        out_specs=[tok, full(gq), full(gkv), full(wq), full(wkv)],
        out_shape=[_sds((T, W), BF16), _sds(gq.shape, F32), _sds(gkv.shape, F32), _sds(wq.shape, F32), _sds(wkv.shape, F32)],
        compiler_params=_cparams(VMEM_BIG))(proj, gq, gkv, wq, wkv, place, cos_q, sin_q, cs_k, dq, dk, dv)


def _gmlp_block(blk, lg, lb, ws, bs):
    u = jax.nn.gelu(blk[:, 0:GW])
    v = jax.nn.gelu(blk[:, GW:2 * GW])
    mu = jnp.mean(v, axis=-1, keepdims=True)
    var = jnp.mean(jnp.square(v - mu), axis=-1, keepdims=True)
    vn = (v - mu) * lax.rsqrt(var + LN_EPS) * lg + lb
    causal = _iota((D_CHUNK, D_CHUNK), 1) <= _iota((D_CHUNK, D_CHUNK), 0)
    group = _iota((1, GW), 1) // HD
    mixed = jnp.zeros((D_CHUNK, GW), F32)
    for g in range(HEADS):
        part = _mm(jnp.where(causal, ws[g], 0.0), vn) + bs[g]
        mixed = mixed + jnp.where(group == g, part, 0.0)
    return u * mixed


def gmlp_fwd(proj, lg, lb, ws, bs):
    T = proj.shape[0]

    def body(p_ref, lg_ref, lb_ref, ws_ref, bs_ref, o_ref):
        o_ref[...] = _gmlp_block(p_ref[...], lg_ref[...], lb_ref[...], ws_ref[...], bs_ref[...]).astype(BF16)

    def full(a):
        return pl.BlockSpec(a.shape, lambda i: (0,) * a.ndim)

    return pl.pallas_call(
        body, name="gmlp_fwd", grid=(T // D_CHUNK,),
        in_specs=[pl.BlockSpec((D_CHUNK, 2 * GW), lambda i: (i, C_D // (2 * GW))), full(lg), full(lb), full(ws), full(bs)],
        out_specs=pl.BlockSpec((D_CHUNK, GW), lambda i: (i, 0)), out_shape=_sds((T, GW), BF16),
        compiler_params=_cparams(VMEM_BIG))(proj, lg, lb, ws, bs)


def gmlp_bwd(proj, lg, lb, ws, bs, dcat):
    T = proj.shape[0]

    def body(p_ref, lg_ref, lb_ref, ws_ref, bs_ref, do_ref, dp_ref, dlg_ref, dlb_ref, dws_ref, dbs_ref):
        @pl.when(pl.program_id(0) == 0)
        def _():
            for r in (dlg_ref, dlb_ref, dws_ref, dbs_ref):
                r[...] = jnp.zeros_like(r)

        _, vjp = jax.vjp(_gmlp_block, p_ref[...], lg_ref[...], lb_ref[...], ws_ref[...], bs_ref[...])
        dp, dlg, dlb, dws, dbs = vjp(do_ref[...])
        dp_ref[...] = dp.astype(BF16)
        dlg_ref[...] += dlg
        dlb_ref[...] += dlb
        dws_ref[...] += dws
        dbs_ref[...] += dbs

    def full(a):
        return pl.BlockSpec(a.shape, lambda i: (0,) * a.ndim)

    return pl.pallas_call(
        body, name="gmlp_bwd", grid=(T // D_CHUNK,),
        in_specs=[pl.BlockSpec((D_CHUNK, 2 * GW), lambda i: (i, C_D // (2 * GW))), full(lg), full(lb), full(ws), full(bs),
                  pl.BlockSpec((D_CHUNK, GW), lambda i: (i, O_D // GW))],
        out_specs=[pl.BlockSpec((D_CHUNK, 2 * GW), lambda i: (i, 0)), full(lg), full(lb), full(ws), full(bs)],
        out_shape=[_sds((T, 2 * GW), BF16), _sds(lg.shape, F32), _sds(lb.shape, F32), _sds(ws.shape, F32), _sds(bs.shape, F32)],
        compiler_params=_cparams(VMEM_BIG))(proj, lg, lb, ws, bs, dcat)


def _lb_all(logits):
    m = jnp.max(logits, axis=0, keepdims=True)
    e = jnp.exp(logits - m)
    sm = e / jnp.sum(e, axis=0, keepdims=True)
    return jnp.concatenate([sm[0:1] - sm[0:1], (sm[0:1] + sm[1:2]) - sm[0:1]], axis=0)


def lb_fwd(logits):
    def body(l_ref, o_ref):
        o_ref[...] = _lb_all(l_ref[...])

    return pl.pallas_call(body, name="lb_fwd", out_shape=_sds(logits.shape, F32))(logits)


def lb_bwd(logits, dlb):
    def body(l_ref, d_ref, o_ref):
        _, vjp = jax.vjp(_lb_all, l_ref[...])
        o_ref[...] = vjp(d_ref[...])[0]

    return pl.pallas_call(body, name="lb_bwd", out_shape=_sds(logits.shape, F32))(logits, dlb)


def ada_grad(c_all, dmod_cols):
    N = dmod_cols.shape[1]
    tn = _tile(N, 1152)

    def body(c_ref, d_ref, o_ref):
        h = jax.nn.silu(c_ref[...]).astype(BF16)
        o_ref[...] = lax.dot_general(h, d_ref[...].astype(BF16), (((0,), (0,)), ((), ())), preferred_element_type=F32)

    nb = c_all.shape[0]
    return pl.pallas_call(
        body, name="ada_grad", grid=(N // tn,),
        in_specs=[pl.BlockSpec((nb, D), lambda j: (0, 0)), pl.BlockSpec((nb, tn), lambda j: (0, j))],
        out_specs=pl.BlockSpec((D, tn), lambda j: (0, j)), out_shape=_sds((D, N), F32),
        compiler_params=_cparams(VMEM_BIG))(c_all, dmod_cols)


def sum_slots(a, n, name):
    _, R, C = a.shape
    tr = _row_tile(R, C, n)

    def body(a_ref, o_ref):
        acc = a_ref[0]
        for k in range(1, n):
            acc = acc + a_ref[k]
        o_ref[...] = acc

    return pl.pallas_call(
        body, name=name, grid=(R // tr,),
        in_specs=[pl.BlockSpec((n, tr, C), lambda i: (0, i, 0))],
        out_specs=pl.BlockSpec((tr, C), lambda i: (i, 0)), out_shape=_sds((R, C), F32),
        compiler_params=_cparams(VMEM_BIG))(a)


def add2(a, b, name):
    shp = a.shape
    C = shp[-1]
    a2, b2 = a.reshape(-1, C), b.reshape(-1, C)
    R = a2.shape[0]
    tr = _row_tile(R, C)

    def body(a_ref, b_ref, o_ref):
        o_ref[...] = a_ref[...] + b_ref[...]

    spec = pl.BlockSpec((tr, C), lambda i: (i, 0))
    return pl.pallas_call(body, name=name, grid=(R // tr,), in_specs=[spec, spec], out_specs=spec,
                          out_shape=_sds((R, C), F32), compiler_params=_cparams(VMEM_BIG))(a2, b2).reshape(shp)


def _row_tile(R, C=D, n=1):
    limit = max(8, (1 << 18) // (C * n))
    for t in range(limit - limit % 8, 7, -8):
        if R % t == 0:
            return t
    return R


def adamw(w, g, m, v, name):
    R, C = w.shape
    tr = _row_tile(R, C)
    c1 = 1.0 - ADAM_B1 ** ADAM_STEP
    c2 = 1.0 - ADAM_B2 ** ADAM_STEP

    def body(w_ref, g_ref, m_ref, v_ref, d_ref, nm_ref, nv_ref):
        g_ = g_ref[...]
        nm = ADAM_B1 * m_ref[...] + (1.0 - ADAM_B1) * g_
        nv = ADAM_B2 * v_ref[...] + (1.0 - ADAM_B2) * jnp.square(g_)
        d_ref[...] = -ADAM_LR * ((nm / c1) / (jnp.sqrt(nv / c2) + ADAM_EPS) + ADAM_WD * w_ref[...])
        nm_ref[...] = nm
        nv_ref[...] = nv

    spec = pl.BlockSpec((tr, C), lambda i: (i, 0))
    return pl.pallas_call(body, name=name, grid=(R // tr,), in_specs=[spec] * 4, out_specs=[spec] * 3,
                          out_shape=[_sds((R, C), F32)] * 3, compiler_params=_cparams(VMEM_BIG))(w, g, m, v)


def _rot_cols(w):
    return jnp.concatenate([-w[:, 16:32], w[:, 0:16]], axis=1)


def _fold_rot(d):
    return jnp.concatenate([d[:, 16:32], -d[:, 0:16]], axis=1)


def _pad_heads(w, off, axis):
    parts = []
    for h in range(HEADS):
        piece = lax.slice_in_dim(w, off + HD * h, off + HD * (h + 1), axis=axis)
        parts += [piece, jnp.zeros_like(piece)]
    return parts


def _unpad_heads(d, off, axis):
    return [lax.slice_in_dim(d, off + HP * h, off + HP * h + HD, axis=axis) for h in range(HEADS)]


def mix_in_ext(w):
    z = lambda n: jnp.zeros((w.shape[0], n), w.dtype)
    kr = w[:, 1408:1440]
    cols = [w[:, 0:1408], kr, _rot_cols(kr), z(64)]
    cols += _pad_heads(w, 1440, 1) + _pad_heads(w, 1696, 1) + _pad_heads(w, 1952, 1)
    cols += [w[:, 2212:2724], w[:, 2208:2212], z(NP - C_CF - HEADS)]
    return jnp.concatenate(cols, axis=1)


def mix_in_unext(d):
    kr = d[:, 1408:1440] + _fold_rot(d[:, 1440:1472])
    cols = [d[:, 0:1408], kr] + _unpad_heads(d, C_CQ, 1) + _unpad_heads(d, C_CK, 1) + _unpad_heads(d, C_CV, 1)
    cols += [d[:, C_CF:C_CF + HEADS], d[:, C_D:C_D + 2 * GW]]
    return jnp.concatenate(cols, axis=1)


def mix_out_ext(w):
    return jnp.concatenate([w[0:GW]] + _pad_heads(w, GW, 0) + _pad_heads(w, 2 * GW, 0) + [w[3 * GW:4 * GW]], axis=0)


def mix_out_unext(d):
    return jnp.concatenate([d[0:GW]] + _unpad_heads(d, O_B, 0) + _unpad_heads(d, O_C, 0) + [d[O_D:O_D + GW]], axis=0)


def uq_ext(w):
    z = lambda n: jnp.zeros((w.shape[0], n), w.dtype)
    a, b = [], []
    for h in range(HEADS):
        o = (B_NOPE + B_ROPE) * h
        a += [w[:, o:o + B_NOPE + B_ROPE], z(32)]
        b += [z(B_NOPE), _rot_cols(w[:, o + B_NOPE:o + B_NOPE + B_ROPE]), z(32)]
    return jnp.concatenate(a + b, axis=1)


def uq_unext(d):
    cols = []
    for h in range(HEADS):
        o = HP * h
        cols += [d[:, o:o + B_NOPE], d[:, o + B_NOPE:o + B_NOPE + B_ROPE]
                 + _fold_rot(d[:, HEADS * HP + o + B_NOPE:HEADS * HP + o + B_NOPE + B_ROPE])]
    return jnp.concatenate(cols, axis=1)


def ukv_ext(w):
    z = jnp.zeros((w.shape[0], HD), w.dtype)
    k, v = [], []
    for h in range(HEADS):
        k += [w[:, 2 * HD * h:2 * HD * h + HD], z]
        v += [w[:, 2 * HD * h + HD:2 * HD * (h + 1)], z]
    return jnp.concatenate(k + v, axis=1)


def ukv_unext(d):
    cols = []
    for h in range(HEADS):
        cols += [d[:, HP * h:HP * h + HD], d[:, HEADS * HP + HP * h:HEADS * HP + HP * h + HD]]
    return jnp.concatenate(cols, axis=1)


def rope_tables(S):
    half = B_ROPE // 2
    inv_freq = ROPE_THETA ** (-jnp.arange(half, dtype=F32) / half)
    ang = jnp.arange(S).astype(F32)[:, None] * inv_freq[None, :]
    cos = jnp.tile(jnp.cos(ang), (1, 2))
    sin = jnp.tile(jnp.sin(ang), (1, 2))
    one, zero = jnp.ones((S, B_NOPE), F32), jnp.zeros((S, B_NOPE), F32)
    z32 = jnp.zeros((S, 32), F32)
    cos_q = jnp.tile(jnp.concatenate([one, cos, z32], axis=1), (1, HEADS))
    sin_q = jnp.tile(jnp.concatenate([zero, sin, z32], axis=1), (1, HEADS))
    cs_k = jnp.concatenate([cos, sin, zero], axis=1)
    place = np.zeros((HP, HEADS * HP), np.float32)
    for h in range(HEADS):
        for j in range(B_ROPE):
            place[j, h * HP + B_NOPE + j] = 1.0
            place[B_ROPE + j, h * HP + B_NOPE + j] = 1.0
    return cos_q, sin_q, cs_k, jnp.asarray(place, BF16)


def layer_fwd(x, mod, get, tabs, S):
    cos_q, sin_q, cs_k, place = tabs
    p = dict(get("ffn1", x))
    l = p["wl"]
    zg1, zu1, act1 = ffn_in_fwd(x, mod, p["ffn1_in"], l, 0, 1, S)
    y1, x1 = out_ln_fwd(act1, p["ffn1_out"], x, mod, p["ln_g"][0:1], p["ln_b"][0:1], 2, 0.5, S, l)
    p.update(get("mix", x1))
    proj = mix_in_fwd(x1, mod, p["mix_in"], 3, 4, S)
    o_a, states = hgrn_fwd(proj, p["lb"], p["ng"], S)
    q_b, k_b, v_b = mla_pre_fwd(proj, p["gq"], p["gkv"], p["wq"], p["wkv"], place, cos_q, sin_q, cs_k, S)
    o_b = attn_fwd(q_b, 0, k_b, 0, v_b, 0, None, (B_NOPE + B_ROPE) ** -0.5, S)
    cum = fox_gate_fwd(proj, p["bcol"], S)
    o_c = attn_fwd(proj, C_CQ // HP, proj, C_CK // HP, proj, C_CV // HP, cum, HD ** -0.5, S)
    o_d = gmlp_fwd(proj, p["g_lg"], p["g_lb"], p["ws"], p["bs"])
    cat = jnp.concatenate([o_a, o_b, o_c, o_d], axis=1)
    y2, x2 = out_ln_fwd(cat, p["mix_out"], x1, mod, p["ln_g"][1:2], p["ln_b"][1:2], 5, 1.0, S)
    p.update(get("ffn2", x2))
    zg3, zu3, act3 = ffn_in_fwd(x2, mod, p["ffn2_in"], l, 6, 7, S)
    y3, x3 = out_ln_fwd(act3, p["ffn2_out"], x2, mod, p["ln_g"][2:3], p["ln_b"][2:3], 8, 0.5, S, l)
    saved = dict(x=x, zg1=zg1, zu1=zu1, act1=act1, y1=y1, x1=x1, proj=proj, states=states, q_b=q_b, k_b=k_b, v_b=v_b,
                 cum=cum, cat=cat, y2=y2, x2=x2, zg3=zg3, zu3=zu3, act3=act3, y3=y3, p=p)
    return x3, saved


def _ffn_bwd(dxn, x_in, y, zg, zu, act, mod, w_in, w_out, l, lg, lb, idx, S, emit):
    sh, sc, gate = idx
    dres, dy, dgate, dlg, dlb = ln_res_bwd(dxn, x_in, y, mod, lg, lb, gate, 0.5, S)
    dzg, dzu = swiglu_bwd(dy, w_out, l, zg, zu, S)
    dw_out = tn_mm(act, dy, FF // 2).reshape(N_CHIPS, FF // N_CHIPS, D)
    dw_in = tn_mm_mod_shards(x_in, mod, dzg, dzu, sh, sc, S)
    mod = mod + emit(dw_in, dw_out)
    dx, dsh, dsc = nt_mod_bwd([dzg, dzu], w_in, [0, 2], x_in, mod, dres, sc, S, FF // 2, l)
    return dx, dw_in, dw_out, dlg, dlb, {sh: dsh, sc: dsc, gate: dgate}, mod


def layer_bwd(dx3, mod, sv, tabs, S, emit):
    cos_q, sin_q, cs_k, place = tabs
    p = sv["p"]
    l = p["wl"]
    g = {}
    dm = {}

    def emit_ffn(part):
        def f(dw_in, dw_out):
            g[part + "_in"], g[part + "_out"] = dw_in, dw_out
            return emit(part, g)
        return f

    dx2, _, _, dlg2, dlb2, d, mod = _ffn_bwd(
        dx3, sv["x2"], sv["y3"], sv["zg3"], sv["zu3"], sv["act3"], mod, p["ffn2_in"], p["ffn2_out"], l,
        p["ln_g"][2:3], p["ln_b"][2:3], (6, 7, 8), S, emit_ffn("ffn2"))
    dm.update(d)
    dres, dy2, dm[5], dlg1, dlb1 = ln_res_bwd(dx2, sv["x1"], sv["y2"], mod, p["ln_g"][1:2], p["ln_b"][1:2], 5, 1.0, S)
    dcat = nt_plain(dy2, p["mix_out"])
    g["mix_out"] = tn_mm(sv["cat"], dy2, 768)
    proj = sv["proj"]
    d_a, g["lb"], g["ng"] = hgrn_bwd(proj, sv["states"], dcat, p["lb"], p["ng"], S)
    dq_c, dk_c, dv_c, dcum = attn_bwd(proj, C_CQ // HP, proj, C_CK // HP, proj, C_CV // HP, sv["cum"], dcat,
                                      O_C // HP, HD ** -0.5, S, BF16)
    B = proj.shape[0] // S
    dcum = jnp.concatenate([dcum.reshape(B, HEADS, S), jnp.zeros((B, 8 - HEADS, S), F32)], axis=1)
    d_cf, g["bcol"] = fox_gate_bwd(proj, p["bcol"], dcum, S)
    dq_b, dk_b, dv_b = attn_bwd(sv["q_b"], 0, sv["k_b"], 0, sv["v_b"], 0, None, dcat, O_B // HP,
                                (B_NOPE + B_ROPE) ** -0.5, S, F32)
    d_b, g["gq"], g["gkv"], g["wq"], g["wkv"] = mla_pre_bwd(
        proj, p["gq"], p["gkv"], p["wq"], p["wkv"], place, cos_q, sin_q, cs_k, dq_b, dk_b, dv_b, S)
    d_d, g["g_lg"], g["g_lb"], g["ws"], g["bs"] = gmlp_bwd(proj, p["g_lg"], p["g_lb"], p["ws"], p["bs"], dcat)
    dproj = jnp.concatenate([d_a, d_b, dq_c, dk_c, dv_c, d_d, d_cf, jnp.zeros_like(d_cf)], axis=1)
    g["mix_in"] = tn_mm_mod(sv["x1"], mod, dproj, 3, 4, S, NP_TILE)
    mod = mod + emit("mix", g)
    dx1, dm[3], dm[4] = nt_mod_bwd([dproj], p["mix_in"], [0], sv["x1"], mod, dres, 4, S, NP_TILE)
    last = []

    def emit_last(dw_in, dw_out):
        last.append(emit_ffn("ffn1")(dw_in, dw_out))
        return last[0]

    dx0, _, _, dlg0, dlb0, d, mod = _ffn_bwd(
        dx1, sv["x"], sv["y1"], sv["zg1"], sv["zu1"], sv["act1"], mod, p["ffn1_in"], p["ffn1_out"], l,
        p["ln_g"][0:1], p["ln_b"][0:1], (0, 1, 2), S, emit_last)
    dm.update(d)
    g["ln_g"] = jnp.concatenate([dlg0, dlg1, dlg2], axis=0)
    g["ln_b"] = jnp.concatenate([dlb0, dlb1, dlb2], axis=0)
    dmod = jnp.concatenate([dm[i] for i in range(N_MOD)], axis=1)
    return dx0, dmod, g, last[0]


def local_step(x, c8, tgt, get, lb_logits, S, emit=None):
    B = x.shape[0] // S
    tabs = rope_tables(S)
    lb_all = lb_fwd(lb_logits)
    mods, saved = [], []
    h = x
    for l in range(DEPTH):
        pa = get(l, "ada", h)
        mod = mod_fwd(c8, pa["ada_w"], pa["wl"], pa["ada_b"])[0:B].reshape(B, N_MOD, D)

        def get_l(part, after, l=l):
            p = dict(get(l, part, after))
            if part == "mix":
                p["lb"] = lb_all[l:l + 1]
            return p

        h, sv = layer_fwd(h, mod, get_l, tabs, S)
        mods.append(mod)
        saved.append(sv)
    loss_tile, dh = loss_head(h, tgt)
    grads, dmods, dlb = [None] * DEPTH, [None] * DEPTH, [None] * DEPTH
    tie = jnp.zeros((), F32)
    for l in reversed(range(DEPTH)):
        emit_l = (lambda part, g: jnp.zeros((), F32)) if emit is None else functools.partial(emit, l)
        dh, dmods[l], grads[l], tie = layer_bwd(dh, mods[l] + tie, saved[l], tabs, S, emit_l)
        dlb[l] = grads[l].pop("lb")
    d_logits = lb_bwd(lb_logits, jnp.concatenate(dlb, axis=0))
    return loss_tile, dh, dmods, grads, d_logits


ANY = pl.BlockSpec(memory_space=pl.ANY)


def _place():
    x, y, c = lax.axis_index("x"), lax.axis_index("y"), lax.axis_index("c")
    chips = [(1 - x, y), (x, 1 - y), (1 - x, 1 - y)]
    return x, y, c, chips


def _rcopy(src, dst, sems, k, to):
    send_sems, recv_sems = sems
    return pltpu.make_async_remote_copy(src_ref=src, dst_ref=dst, send_sem=send_sems.at[k], recv_sem=recv_sems.at[k],
                                        device_id=to, device_id_type=MESH)


def _dma_sems(n_remote, n_local):
    return [pltpu.SemaphoreType.DMA((n_remote,)), pltpu.SemaphoreType.DMA((n_remote,)), pltpu.SemaphoreType.DMA((n_local,))]


def own_slot(src, chip):
    L = src.shape[0]
    return lax.dynamic_update_slice(jnp.zeros((L, N_CHIPS) + src.shape[1:], src.dtype), src[:, None], (0, chip, 0, 0))


def ag_shards(arrs, lands):
    n = len(arrs)
    rh = [a.shape[1] // 2 for a in arrs]

    def body(*refs):
        srcs, outs, token = refs[:n], refs[2 * n:3 * n], refs[3 * n]
        send_sems, recv_sems = refs[3 * n + 1:]
        x, y, c, chips = _place()
        sems = (send_sems, recv_sems)
        me = 2 * x + y
        sibling = (x, y, 1 - c)
        token[...] = jnp.zeros_like(token)

        def part(i, k, hc):
            return outs[i].at[:, k, pl.ds(hc * rh[i], rh[i]), :]

        started = []
        for j, (px, py) in enumerate(chips):
            for i in range(n):
                cp = _rcopy(srcs[i].at[:, pl.ds(c * rh[i], rh[i]), :], part(i, me, c), sems, 6 * i + j, (px, py, c))
                cp.start()
                started.append(cp)
        for j, (px, py) in enumerate(chips):
            k = 2 * px + py
            for i in range(n):
                _rcopy(part(i, k, c), part(i, k, c), sems, 6 * i + j, (px, py, c)).wait_recv()
                cp = _rcopy(part(i, k, c), part(i, k, c), sems, 6 * i + 3 + j, sibling)
                cp.start()
                started.append(cp)
        for j, (px, py) in enumerate(chips):
            k = 2 * px + py
            for i in range(n):
                _rcopy(part(i, k, 1 - c), part(i, k, 1 - c), sems, 6 * i + 3 + j, sibling).wait_recv()
        for cp in started:
            cp.wait_send()

    outs = pl.pallas_call(
        body, name="ag_shards", out_shape=[_sds(a.shape, a.dtype) for a in lands] + [_sds((8, 128), F32)],
        in_specs=[ANY] * (2 * n), out_specs=[ANY] * n + [pl.BlockSpec(memory_space=pltpu.VMEM)],
        input_output_aliases={n + i: i for i in range(n)}, scratch_shapes=_dma_sems(6 * n, 1)[:2])(*arrs, *lands)
    return list(outs[:n]), outs[n]


HBM_SPEC = pl.BlockSpec(memory_space=pltpu.HBM)
SEM_SPEC = pl.BlockSpec(memory_space=pltpu.SEMAPHORE)
DATAFLOW = pltpu.SideEffectType.DATAFLOW_SIDE_EFFECTING


def _after(x, dep):
    return lax.optimization_barrier((x, dep))[0]


def _split_start(srcs, lands, copies, name):
    n, m = len(srcs), len(lands)

    def body(*refs):
        ins = refs[:n + m]
        send_sems, recv_sems = refs[n + m], refs[n + m + 1]
        token = refs[-1]
        for k, (src, dst, to) in enumerate(copies(ins[:n], ins[n:], _place())):
            pltpu.make_async_remote_copy(src_ref=src, dst_ref=dst, send_sem=send_sems.at[k], recv_sem=recv_sems.at[k],
                                         device_id=to, device_id_type=MESH).start()
        token[...] = jnp.zeros_like(token)

    n_copies = 3 * n
    arrs = list(srcs) + list(lands)
    outs = pl.pallas_call(
        body, name=name,
        out_shape=(pltpu.SemaphoreType.DMA((n_copies,)), pltpu.SemaphoreType.DMA((n_copies,)),
                   *[pltpu.HBM(a.shape, a.dtype) for a in arrs], _sds((8, 128), F32)),
        in_specs=[HBM_SPEC] * (n + m),
        out_specs=(SEM_SPEC, SEM_SPEC, *[HBM_SPEC] * (n + m), pl.BlockSpec(memory_space=pltpu.VMEM)),
        input_output_aliases={i: 2 + i for i in range(n + m)},
        compiler_params=pltpu.CompilerParams(has_side_effects=DATAFLOW),
    )(*[pltpu.with_memory_space_constraint(a, pltpu.HBM) for a in arrs])
    return outs[0], outs[1], list(outs[2:2 + n]), list(outs[2 + n:2 + n + m]), outs[-1]


def _split_wait(handle, arrivals, after, name):
    send_sems, recv_sems, srcs, lands, _ = handle
    n, m = len(srcs), len(lands)

    def body(*refs):
        ins = refs[:n + m]
        send_sems, recv_sems = refs[n + m], refs[n + m + 1]
        x, y, c, chips = place = _place()
        for k, (src, dst) in enumerate(arrivals(ins[:n], ins[n:], place)):
            cp = pltpu.make_async_remote_copy(src_ref=src, dst_ref=dst, send_sem=send_sems.at[k], recv_sem=recv_sems.at[k],
                                              device_id=(x, y, 1 - c), device_id_type=MESH)
            cp.wait_send()
            cp.wait_recv()

    arrs = list(srcs) + list(lands)
    outs = pl.pallas_call(
        body, name=name, out_shape=[pltpu.HBM(a.shape, a.dtype) for a in arrs],
        in_specs=[HBM_SPEC] * (n + m) + [SEM_SPEC, SEM_SPEC, ANY], out_specs=[HBM_SPEC] * (n + m),
        input_output_aliases={i: i for i in range(n + m)},
        compiler_params=pltpu.CompilerParams(has_side_effects=DATAFLOW),
    )(*arrs, send_sems, recv_sems, after)
    return list(outs[n:])


def _ag_part(ref, k, hc):
    rh = ref.shape[2] // 2
    return ref.at[:, k, pl.ds(hc * rh, rh), :]


def ag_start(srcs, lands, name):
    def copies(s, d, place):
        x, y, c, chips = place
        out = []
        for j, (px, py) in enumerate(chips):
            for i in range(len(s)):
                rh = s[i].shape[1] // 2
                out.append((s[i].at[:, pl.ds(c * rh, rh), :], _ag_part(d[i], 2 * x + y, c), (px, py, c)))
        return out

    return _split_start(srcs, lands, copies, name)


def ag_wait(handle, after, name):
    def arrivals(s, d, place):
        x, y, c, chips = place
        out = []
        for j, (px, py) in enumerate(chips):
            for i in range(len(s)):
                rh = s[i].shape[1] // 2
                out.append((s[i].at[:, pl.ds(c * rh, rh), :], _ag_part(d[i], 2 * px + py, c)))
        return out

    return _split_wait(handle, arrivals, after, name)


def ag_forward(lands, name):
    n = len(lands)

    def body(*refs):
        bufs = refs[n:2 * n]
        send_sems, recv_sems = refs[2 * n:]
        x, y, c, chips = _place()
        sems = (send_sems, recv_sems)
        cps = []
        for j, (px, py) in enumerate(chips):
            for i in range(n):
                part = _ag_part(bufs[i], 2 * px + py, c)
                cps.append(_rcopy(part, part, sems, 3 * i + j, (x, y, 1 - c)))
        for cp in cps:
            cp.start()
        for j, (px, py) in enumerate(chips):
            for i in range(n):
                part = _ag_part(bufs[i], 2 * px + py, 1 - c)
                _rcopy(part, part, sems, 3 * i + j, (x, y, 1 - c)).wait_recv()
        for cp in cps:
            cp.wait_send()

    return pl.pallas_call(
        body, name=name, out_shape=[_sds(a.shape, a.dtype) for a in lands], in_specs=[ANY] * n, out_specs=[ANY] * n,
        input_output_aliases={i: i for i in range(n)}, scratch_shapes=_dma_sems(3 * n, 1)[:2])(*lands)


def rs_start(hs, lands, name):
    def copies(s, d, place):
        x, y, c, chips = place
        return [(s[i].at[2 * px + py], d[i].at[2 * x + y], (px, py, c)) for j, (px, py) in enumerate(chips) for i in range(len(s))]

    return _split_start(hs, lands, copies, name)


def rs_wait(handle, after, name):
    def arrivals(s, d, place):
        x, y, c, chips = place
        return [(s[i].at[2 * px + py], d[i].at[2 * px + py]) for j, (px, py) in enumerate(chips) for i in range(len(s))]

    return _split_wait(handle, arrivals, after, name)


def sibling_swap(arrs, name):
    n = len(arrs)
    rh = [a.shape[1] // 2 for a in arrs]

    def body(*refs):
        srcs, outs = refs[:n], refs[n:2 * n]
        send_sems, recv_sems = refs[2 * n:]
        x, y, c, _ = _place()
        cps = [_rcopy(srcs[i].at[:, pl.ds((1 - c) * rh[i], rh[i]), :], outs[i], (send_sems, recv_sems), i, (x, y, 1 - c))
               for i in range(n)]
        for cp in cps:
            cp.start()
        for cp in cps:
            cp.wait()

    return pl.pallas_call(
        body, name=name, out_shape=[_sds((N_CHIPS, r, a.shape[2]), a.dtype) for a, r in zip(arrs, rh)],
        in_specs=[ANY] * n, out_specs=[ANY] * n, scratch_shapes=_dma_sems(n, 1)[:2])(*arrs)


def chip_exchange(hs):
    n = len(hs)

    def body(*refs):
        srcs, outs = refs[:n], refs[n:2 * n]
        send_sems, recv_sems, loc_sems = refs[2 * n:]
        x, y, c, chips = _place()
        sems = (send_sems, recv_sems)
        me = 2 * x + y
        mine = [pltpu.make_async_copy(srcs[i].at[me], outs[i].at[me], loc_sems.at[i]) for i in range(n)]
        for cp in mine:
            cp.start()
        sends = []
        for j, (px, py) in enumerate(chips):
            for i in range(n):
                cp = _rcopy(srcs[i].at[2 * px + py], outs[i].at[me], sems, 3 * i + j, (px, py, c))
                cp.start()
                sends.append(cp)
        for j, (px, py) in enumerate(chips):
            for i in range(n):
                _rcopy(srcs[i].at[2 * px + py], outs[i].at[2 * px + py], sems, 3 * i + j, (px, py, c)).wait_recv()
        for cp in sends:
            cp.wait_send()
        for cp in mine:
            cp.wait()

    return pl.pallas_call(
        body, name="chip_exchange", out_shape=[_sds(h.shape, h.dtype) for h in hs],
        in_specs=[ANY] * n, out_specs=[ANY] * n, scratch_shapes=_dma_sems(3 * n, n))(*hs)


def sum_into(land, base, l, core, name):
    _, rh, C = land.shape
    tr = _row_tile(rh, C, N_CHIPS)
    nr = rh // tr

    def body(core_ref, land_ref, base_ref, o_ref):
        acc = land_ref[0]
        for k in range(1, N_CHIPS):
            acc = acc + land_ref[k]
        o_ref[...] = acc

    grid_spec = pltpu.PrefetchScalarGridSpec(
        num_scalar_prefetch=1, grid=(nr,),
        in_specs=[pl.BlockSpec((N_CHIPS, tr, C), lambda r, core_ref: (0, r, 0)), ANY],
        out_specs=pl.BlockSpec((None, tr, C), lambda r, core_ref: (l, core_ref[0] * nr + r, 0)))
    return pl.pallas_call(body, name=name, grid_spec=grid_spec, out_shape=_sds(base.shape, base.dtype),
                          input_output_aliases={2: 0}, compiler_params=_cparams(VMEM_BIG))(
        core.reshape(1).astype(jnp.int32), land, base)


def sibling_join(bases, name):
    n = len(bases)

    def body(*refs):
        bufs = refs[n:2 * n]
        send_sems, recv_sems = refs[2 * n:]
        x, y, c, _ = _place()
        sems = (send_sems, recv_sems)

        def half(i, hc):
            rh = bufs[i].shape[1] // 2
            return bufs[i].at[:, pl.ds(hc * rh, rh), :]

        sends = [_rcopy(half(i, c), half(i, c), sems, i, (x, y, 1 - c)) for i in range(n)]
        for cp in sends:
            cp.start()
        for i in range(n):
            _rcopy(half(i, 1 - c), half(i, 1 - c), sems, i, (x, y, 1 - c)).wait_recv()
        for cp in sends:
            cp.wait_send()

    return pl.pallas_call(
        body, name=name, out_shape=[_sds(b.shape, b.dtype) for b in bases], in_specs=[ANY] * n, out_specs=[ANY] * n,
        input_output_aliases={i: i for i in range(n)}, scratch_shapes=_dma_sems(n, 1)[:2])(*bases)


def ag_all(blk):
    M, C = blk.shape

    def body(x_ref, out_ref, send_sems, recv_sems, loc_sem):
        x, y, c, chips = _place()
        sems = (send_sems, recv_sems)
        me, sibling = (x, y, c), (x, y, 1 - c)

        def slot(px, py, pc):
            return out_ref.at[4 * px + 2 * py + pc]

        mine = pltpu.make_async_copy(x_ref, slot(*me), loc_sem)
        mine.start()
        first = [_rcopy(x_ref, slot(*me), sems, 0, sibling)]
        first += [_rcopy(x_ref, slot(*me), sems, 1 + j, (*chip, c)) for j, chip in enumerate(chips)]
        for cp in first:
            cp.start()
        passed = [_rcopy(slot(*chip, c), slot(*chip, c), sems, 4 + j, sibling) for j, chip in enumerate(chips)]
        for j, chip in enumerate(chips):
            _rcopy(slot(*chip, c), slot(*chip, c), sems, 1 + j, me).wait_recv()
            passed[j].start()
        _rcopy(slot(*sibling), slot(*sibling), sems, 0, me).wait_recv()
        for j, chip in enumerate(chips):
            _rcopy(slot(*chip, 1 - c), slot(*chip, 1 - c), sems, 4 + j, me).wait_recv()
        for cp in first + passed:
            cp.wait_send()
        mine.wait()

    return pl.pallas_call(
        body, name="ag_all", out_shape=_sds((8, M, C), blk.dtype),
        in_specs=[pl.BlockSpec(memory_space=pltpu.VMEM)], out_specs=pl.BlockSpec(memory_space=pltpu.VMEM),
        scratch_shapes=[pltpu.SemaphoreType.DMA((7,)), pltpu.SemaphoreType.DMA((7,)), pltpu.SemaphoreType.DMA(())],
        compiler_params=_cparams(VMEM_BIG))(blk)


WEIGHTS = ["ada_w", "ada_b", "ln_g", "ln_b", "ffn1_w_in", "ffn1_w_out", "ffn2_w_in", "ffn2_w_out", "mix_w_in", "mix_w_out",
           "hgrn_lb_logits", "hgrn_norm_g", "mla_q_norm_g", "mla_kv_norm_g", "mla_w_uq", "mla_w_ukv", "fox_b_f",
           "gmlp_ln_g", "gmlp_ln_b", "gmlp_w_s", "gmlp_b_s"]
SHARDED = {"ffn1_w_in": 1, "ffn1_w_out": 0, "ffn2_w_in": 1, "ffn2_w_out": 0, "mix_w_in": 1, "mix_w_out": 0,
           "mla_w_uq": 1, "mla_w_ukv": 1}
SMALL = ["hgrn_lb_logits", "hgrn_norm_g", "mla_q_norm_g", "mla_kv_norm_g", "fox_b_f", "gmlp_ln_g", "gmlp_ln_b",
         "gmlp_w_s", "gmlp_b_s", "ln_g", "ln_b"]
GATHERED = ["ada_w", "ffn1_w_in", "ffn1_w_out", "ffn2_w_in", "ffn2_w_out", "mix_w_in", "mix_w_out", "mla_w_uq", "mla_w_ukv"]
REDUCED = GATHERED[1:]


def _col_shards(a):
    cols = a.shape[1] // N_CHIPS
    return jnp.stack([a[:, k * cols:(k + 1) * cols] for k in range(N_CHIPS)])


def add_kept_half(a, got, core, name):
    _, R, C = a.shape
    rh = R // 2
    tr = _row_tile(rh, C)
    nr = rh // tr

    def body(core_ref, a_ref, b_ref, o_ref):
        o_ref[...] = a_ref[...] + b_ref[...]

    half = pl.BlockSpec((None, tr, C), lambda k, r, core_ref: (k, r, 0))
    grid_spec = pltpu.PrefetchScalarGridSpec(
        num_scalar_prefetch=1, grid=(N_CHIPS, nr),
        in_specs=[pl.BlockSpec((None, tr, C), lambda k, r, core_ref: (k, core_ref[0] * nr + r, 0)), half],
        out_specs=half)
    return pl.pallas_call(body, name=name, grid_spec=grid_spec, out_shape=_sds((N_CHIPS, rh, C), F32),
                          compiler_params=_cparams(VMEM_BIG))(core.reshape(1).astype(jnp.int32), a, got)


def _rows(parts, n_rows, dtype):
    flat = jnp.concatenate([p.reshape(-1) for p in parts])
    pad = n_rows * D - flat.shape[0]
    return jnp.concatenate([flat, jnp.zeros((pad,), dtype)]).reshape(n_rows, D)


def _take(flat, shapes):
    out, o = [], 0
    for shp in shapes:
        n = int(np.prod(shp))
        out.append(flat[o:o + n].reshape(shp))
        o += n
    return out


def _round_up(n, m):
    return -(-n // m) * m


def pack_shard(w):
    parts = [w[n][l] for l in range(DEPTH) for n in SHARDED] + [w[n][l] for l in range(DEPTH) for n in ("ln_g", "ln_b")]
    n = sum(int(np.prod(p.shape)) for p in parts)
    return _rows(parts, _round_up(-(-n // D), 16), F32)


def unpack_shard(pk, like):
    shapes = [like[n].shape[1:] for l in range(DEPTH) for n in SHARDED] + [like[n].shape[1:] for l in range(DEPTH) for n in ("ln_g", "ln_b")]
    pieces = _take(pk.reshape(-1), shapes)
    names = [n for l in range(DEPTH) for n in SHARDED] + [n for l in range(DEPTH) for n in ("ln_g", "ln_b")]
    out = {}
    for n in list(SHARDED) + ["ln_g", "ln_b"]:
        out[n] = jnp.stack([p for p, m in zip(pieces, names) if m == n])
    return out


def pack_small(w):
    parts = [w[n][l] for l in range(DEPTH) for n in SMALL]
    n = sum(int(np.prod(p.shape)) for p in parts)
    return _rows(parts, _round_up(-(-n // D), 8), F32)


def unpack_small(pk, like):
    shapes = [like[n].shape[1:] for l in range(DEPTH) for n in SMALL]
    pieces = _take(pk.reshape(-1), shapes)
    names = [n for l in range(DEPTH) for n in SMALL]
    return {n: jnp.stack([p for p, m in zip(pieces, names) if m == n]) for n in SMALL}


def pack_gather(w):
    parts = [w[n][l].astype(BF16) for l in range(DEPTH) for n in ["ada_w"] + list(SHARDED)]
    ln = jnp.concatenate([w[n][l].reshape(-1) for l in range(DEPTH) for n in ("ln_g", "ln_b")])
    parts.append(lax.bitcast_convert_type(ln, BF16))
    n = sum(int(np.prod(p.shape)) for p in parts)
    return _rows(parts, _round_up(-(-n // D), 16), BF16)


def unpack_gather(g, w):
    names = ["ada_w"] + list(SHARDED)
    shapes = [w[n].shape[1:] for l in range(DEPTH) for n in names]
    n_ln = DEPTH * 2 * 3 * (D // N_CHIPS)
    flat = g.reshape(N_CHIPS, -1)
    per_chip = [_take(flat[k], shapes + [(n_ln, 2)]) for k in range(N_CHIPS)]
    layers = [dict() for _ in range(DEPTH)]
    i = 0
    for l in range(DEPTH):
        for n in names:
            axis = 1 if n == "ada_w" else SHARDED[n]
            layers[l][n] = jnp.concatenate([per_chip[k][i] for k in range(N_CHIPS)], axis=axis)
            i += 1
    ln = [lax.bitcast_convert_type(per_chip[k][i], F32).reshape(DEPTH, 2, 3, D // N_CHIPS) for k in range(N_CHIPS)]
    ln = jnp.concatenate(ln, axis=3)
    for l in range(DEPTH):
        layers[l]["ln_g"], layers[l]["ln_b"] = ln[l, 0], ln[l, 1]
    return layers


def pack_grads(grads, k):
    parts = []
    for l in range(DEPTH):
        g = grads[l]
        full = {"ffn1_w_out": g["ffn1_out"], "ffn2_w_out": g["ffn2_out"], "mix_w_in": mix_in_unext(g["mix_in"]),
                "mix_w_out": mix_out_unext(g["mix_out"]), "mla_w_uq": uq_unext(g["wq"]), "mla_w_ukv": ukv_unext(g["wkv"])}
        for n, axis in SHARDED.items():
            if n in ("ffn1_w_in", "ffn2_w_in"):
                half = g[n.replace("_w_in", "_in")][k // 2]
                parts.append(half[:, (k % 2) * (FF // 2):(k % 2 + 1) * (FF // 2)])
            else:
                sz = full[n].shape[axis] // N_CHIPS
                parts.append(lax.slice_in_dim(full[n], k * sz, (k + 1) * sz, axis=axis))
    for l in range(DEPTH):
        for n in ("ln_g", "ln_b"):
            parts.append(grads[l][n][:, k * (D // N_CHIPS):(k + 1) * (D // N_CHIPS)])
    n = sum(int(np.prod(p.shape)) for p in parts)
    return _rows(parts, _round_up(-(-n // D), 16), F32)


def kernel(x, c, ada_w, ada_b, ln_g, ln_b, ffn1_w_in, ffn1_w_out, ffn2_w_in, ffn2_w_out, mix_w_in, mix_w_out, hgrn_lb_logits, hgrn_norm_g, mla_q_norm_g, mla_kv_norm_g, mla_w_uq, mla_w_ukv, fox_b_f, gmlp_ln_g, gmlp_ln_b, gmlp_w_s, gmlp_b_s, loss_target, m_ada_w, m_ada_b, m_ln_g, m_ln_b, m_ffn1_w_in, m_ffn1_w_out, m_ffn2_w_in, m_ffn2_w_out, m_mix_w_in, m_mix_w_out, m_hgrn_lb_logits, m_hgrn_norm_g, m_mla_q_norm_g, m_mla_kv_norm_g, m_mla_w_uq, m_mla_w_ukv, m_fox_b_f, m_gmlp_ln_g, m_gmlp_ln_b, m_gmlp_w_s, m_gmlp_b_s, v_ada_w, v_ada_b, v_ln_g, v_ln_b, v_ffn1_w_in, v_ffn1_w_out, v_ffn2_w_in, v_ffn2_w_out, v_mix_w_in, v_mix_w_out, v_hgrn_lb_logits, v_hgrn_norm_g, v_mla_q_norm_g, v_mla_kv_norm_g, v_mla_w_uq, v_mla_w_ukv, v_fox_b_f, v_gmlp_ln_g, v_gmlp_ln_b, v_gmlp_w_s, v_gmlp_b_s):
    w = dict(zip(WEIGHTS, (ada_w, ada_b, ln_g, ln_b, ffn1_w_in, ffn1_w_out, ffn2_w_in, ffn2_w_out, mix_w_in, mix_w_out, hgrn_lb_logits, hgrn_norm_g, mla_q_norm_g, mla_kv_norm_g, mla_w_uq, mla_w_ukv, fox_b_f, gmlp_ln_g, gmlp_ln_b, gmlp_w_s, gmlp_b_s)))
    m = dict(zip(WEIGHTS, (m_ada_w, m_ada_b, m_ln_g, m_ln_b, m_ffn1_w_in, m_ffn1_w_out, m_ffn2_w_in, m_ffn2_w_out, m_mix_w_in, m_mix_w_out, m_hgrn_lb_logits, m_hgrn_norm_g, m_mla_q_norm_g, m_mla_kv_norm_g, m_mla_w_uq, m_mla_w_ukv, m_fox_b_f, m_gmlp_ln_g, m_gmlp_ln_b, m_gmlp_w_s, m_gmlp_b_s)))
    v = dict(zip(WEIGHTS, (v_ada_w, v_ada_b, v_ln_g, v_ln_b, v_ffn1_w_in, v_ffn1_w_out, v_ffn2_w_in, v_ffn2_w_out, v_mix_w_in, v_mix_w_out, v_hgrn_lb_logits, v_hgrn_norm_g, v_mla_q_norm_g, v_mla_kv_norm_g, v_mla_w_uq, v_mla_w_ukv, v_fox_b_f, v_gmlp_ln_g, v_gmlp_ln_b, v_gmlp_w_s, v_gmlp_b_s)))
    Bl, S, _ = x.shape
    T = Bl * S
    core = lax.axis_index("c")
    chip = 2 * lax.axis_index("x") + lax.axis_index("y")

    def shard(key):
        n, l = key
        if n == "ln":
            return jnp.concatenate([ln_g[l:l + 1], ln_b[l:l + 1], jnp.zeros((1, 2, D // N_CHIPS), F32)], axis=1)
        return w[n][l:l + 1].astype(BF16)

    mixers = ["mix_w_in", "mix_w_out", "mla_w_uq", "mla_w_ukv"]
    groups = [[("ada_w", 0), ("ffn1_w_in", 0), ("ffn1_w_out", 0), ("ln", 0)],
              [(n, 0) for n in mixers + ["ffn2_w_in", "ffn2_w_out"]],
              [(n, 1) for n in GATHERED + ["ln"]]]
    srcs = [shard(k) for k in groups[0]]
    first, token = ag_shards(srcs, [own_slot(s, chip) for s in srcs])
    have = dict(zip(groups[0], first))
    handles = {}
    for gi in (1, 2):
        srcs = [s + token[0, 0].astype(s.dtype) for s in (shard(k) for k in groups[gi])]
        handles[gi] = ag_start(srcs, [own_slot(s, chip) for s in srcs], "ag_start_%d" % gi)
        token = handles[gi][-1]
    c8 = jnp.concatenate([c, jnp.zeros((8 - Bl, D), F32)], axis=0)
    c8 = c8 + token[0, 0]

    def cat_cols(a):
        return jnp.concatenate([a[0, k] for k in range(N_CHIPS)], axis=1)

    def get(l, part, after):
        gi = 2 if l == 1 else (0 if part in ("ada", "ffn1") else 1)
        if gi in handles:
            arrived = ag_forward(ag_wait(handles.pop(gi), after, "ag_wait_%d" % gi), "ag_forward_%d" % gi)
            have.update(zip(groups[gi], arrived))
        if part == "ada":
            return dict(ada_w=have[("ada_w", l)], wl=0, ada_b=ada_b[l][None])
        if part == "ffn1":
            ln_full = jnp.moveaxis(have[("ln", l)][0], 0, 1).reshape(8, D)
            return dict(ffn1_in=have[("ffn1_w_in", l)], ffn1_out=have[("ffn1_w_out", l)], wl=0,
                        ln_g=ln_full[0:3], ln_b=ln_full[3:6])
        if part == "ffn2":
            return dict(ffn2_in=have[("ffn2_w_in", l)], ffn2_out=have[("ffn2_w_out", l)])
        return dict(
            mix_in=mix_in_ext(cat_cols(have[("mix_w_in", l)])), mix_out=mix_out_ext(have[("mix_w_out", l)].reshape(D, D)),
            wq=uq_ext(cat_cols(have[("mla_w_uq", l)])).astype(F32), wkv=ukv_ext(cat_cols(have[("mla_w_ukv", l)])).astype(F32),
            ng=hgrn_norm_g[l][None], gq=mla_q_norm_g[l][None], gkv=mla_kv_norm_g[l][None],
            bcol=jnp.concatenate([fox_b_f[l], jnp.zeros((8 - HEADS,), F32)])[:, None],
            g_lg=gmlp_ln_g[l][None], g_lb=gmlp_ln_b[l][None], ws=gmlp_w_s[l], bs=gmlp_b_s[l][:, :, None])

    pending = []

    def emit(l, part, g):
        if part == "mix":
            names = mixers
            by_chip = [_col_shards(mix_in_unext(g["mix_in"])), mix_out_unext(g["mix_out"]).reshape(N_CHIPS, D // N_CHIPS, D),
                       _col_shards(uq_unext(g["wq"])), _col_shards(ukv_unext(g["wkv"]))]
        else:
            names = [part + "_w_in", part + "_w_out"]
            by_chip = [g[part + "_in"], g[part + "_out"]]
        tag = "%d_%s" % (l, part)
        got = sibling_swap(by_chip, "sibling_swap_" + tag)
        chip_sum = [add_kept_half(a, r, core, "add_sibling") for a, r in zip(by_chip, got)]
        slot = lax.broadcasted_iota(jnp.int32, (N_CHIPS, 1, 1), 0)
        lands = [jnp.where(slot == chip, h, 0.0) for h in chip_sum]
        handle = rs_start(chip_sum, lands, "rs_start_" + tag)
        pending.append((l, names, handle, tag))
        return handle[-1][0, 0]

    loss_tile, dx, dmods, grads, d_logits = local_step(
        x.reshape(T, D), c8, loss_target.reshape(T, D), get, hgrn_lb_logits, S, emit)
    loss = lax.psum(loss_tile[0, 0], ("x", "y", "c"))

    small_g = {"hgrn_lb_logits": d_logits,
               "hgrn_norm_g": jnp.stack([grads[l]["ng"][0] for l in range(DEPTH)]),
               "mla_q_norm_g": jnp.stack([grads[l]["gq"][0] for l in range(DEPTH)]),
               "mla_kv_norm_g": jnp.stack([grads[l]["gkv"][0] for l in range(DEPTH)]),
               "fox_b_f": jnp.stack([grads[l]["bcol"][0:HEADS, 0] for l in range(DEPTH)]),
               "gmlp_ln_g": jnp.stack([grads[l]["g_lg"][0] for l in range(DEPTH)]),
               "gmlp_ln_b": jnp.stack([grads[l]["g_lb"][0] for l in range(DEPTH)]),
               "gmlp_w_s": jnp.stack([grads[l]["ws"] for l in range(DEPTH)]),
               "gmlp_b_s": jnp.stack([grads[l]["bs"][:, :, 0] for l in range(DEPTH)])}
    small_g["ln_g"] = jnp.stack([grads[l]["ln_g"] for l in range(DEPTH)])
    small_g["ln_b"] = jnp.stack([grads[l]["ln_b"] for l in range(DEPTH)])
    pk_small = pack_small(small_g)
    n_small = pk_small.shape[0]
    extras = [dmods[l] for l in range(DEPTH)] + [c]
    n_extra = _round_up(-(-sum(int(np.prod(e.shape)) for e in extras) // D), 8)
    gathered = ag_all(jnp.concatenate([pk_small, _rows(extras, n_extra, F32)], axis=0))
    g_small = unpack_small(sum_slots(gathered[:, 0:n_small], 8, "sum_small"), small_g)
    ext = gathered[:, n_small:].reshape(8, -1)
    n_dmod = DEPTH * Bl * N_MOD * D
    dmod_all = ext[:, 0:n_dmod].reshape(8, DEPTH, Bl, N_MOD * D)
    c_all = ext[:, n_dmod:n_dmod + Bl * D].reshape(8 * Bl, D)
    g_ada_w, g_ada_b = [], []
    ncol = N_MOD * D // N_CHIPS
    for l in range(DEPTH):
        dm = dmod_all[:, l].reshape(8 * Bl, N_MOD * D)
        g_ada_w.append(ada_grad(c_all, lax.dynamic_slice_in_dim(dm, chip * ncol, ncol, axis=1)))
        g_ada_b.append(sum_slots(dm.reshape(8 * Bl, N_MOD, D), 8 * Bl, "sum_ada_b").reshape(N_MOD * D))
    g_ada_w, g_ada_b = jnp.stack(g_ada_w), jnp.stack(g_ada_b)

    red = {n: jnp.zeros(w[n].shape, F32) for n in REDUCED}

    def arrive(entry, after):
        l, names, handle, tag = entry
        for n, land in zip(names, rs_wait(handle, after, "rs_wait_" + tag)):
            red[n] = sum_into(land, red[n], l, core, "sum_chips")

    for entry in pending[:-1]:
        arrive(entry, dx)
    late = pending[-1][1]
    early = [n for n in REDUCED if n not in late]
    grad = dict(zip(early, sibling_join([red[n] for n in early], "sibling_join_a")))
    grad.update(g_small)
    grad["ada_w"], grad["ada_b"] = g_ada_w, g_ada_b
    for n in ("ln_g", "ln_b"):
        grad[n] = lax.dynamic_slice_in_dim(g_small[n], chip * (D // N_CHIPS), D // N_CHIPS, axis=2)
    out = {"grad": grad, "delta": {}, "new_m": {}, "new_v": {}}

    def update(n):
        shp = w[n].shape
        two_d = (-1, shp[-1])
        res = adamw(w[n].reshape(two_d), grad[n].reshape(two_d), m[n].reshape(two_d), v[n].reshape(two_d), "adamw_" + n)
        grad[n] = grad[n].reshape(shp)
        for key, r in zip(("delta", "new_m", "new_v"), res):
            out[key][n] = r.reshape(shp)

    for n in WEIGHTS:
        if n not in late:
            update(n)
    arrive(pending[-1], out["delta"]["ffn2_w_in"])
    grad.update(zip(late, sibling_join([red[n] for n in late], "sibling_join_b")))
    for n in late:
        update(n)
    outs = [loss, dx.reshape(Bl, S, D)]
    for key in ("grad", "delta", "new_m", "new_v"):
        outs += [out[key][n] for n in WEIGHTS]
    return tuple(outs)
```

```python
import functools

import jax
import jax.numpy as jnp
import numpy as np
from jax import lax
from jax.experimental import pallas as pl
from jax.experimental.pallas import tpu as pltpu

F32, BF16 = jnp.float32, jnp.bfloat16
MESH = pl.DeviceIdType.MESH

N_CHIPS = 4
D = 1024
DEPTH = 2
FF = 2816
N_MOD = 9
GW = 256
HEADS = 4
HD = 64
HP = 128
A_CHUNK = 16
LB_FLOOR = 1e-30
B_Q_LORA, B_KV_LORA, B_NOPE, B_ROPE = 256, 128, 64, 32
ROPE_THETA = 10000.0
D_CHUNK = 128
MIX_COLS = 2724
ALPHA = (2 * DEPTH) ** 0.25
LN_EPS = 1e-5
RMS_EPS = 1e-6
ADAM_LR, ADAM_B1, ADAM_B2, ADAM_EPS, ADAM_WD, ADAM_STEP = 0.001, 0.9, 0.999, 1e-08, 0.01, 10

NP = 3840
NP_TILE = 1920
C_A, C_B, C_CQ, C_CK, C_CV, C_D, C_CF = 0, 1024, 1536, 2048, 2560, 3072, 3584
NCAT = 1536
O_A, O_B, O_C, O_D = 0, 256, 768, 1280

VMEM_BIG = 48 << 20


def _cparams(vmem=None):
    return pltpu.CompilerParams(vmem_limit_bytes=vmem) if vmem else pltpu.CompilerParams()


def _sds(shape, dtype):
    return jax.ShapeDtypeStruct(tuple(shape), dtype)


@jax.custom_vjp
def _mm(a, w):
    return jnp.dot(a.astype(BF16), w.astype(BF16), preferred_element_type=F32)


def _mm_f(a, w):
    return _mm(a, w), (a, w)


def _mm_b(res, g):
    a, w = res
    gb = g.astype(BF16)
    da = lax.dot_general(gb, w.astype(BF16), (((1,), (1,)), ((), ())), preferred_element_type=F32)
    dw = lax.dot_general(a.astype(BF16), gb, (((0,), (0,)), ((), ())), preferred_element_type=F32)
    return da.astype(a.dtype), dw.astype(w.dtype)


_mm.defvjp(_mm_f, _mm_b)


@jax.custom_vjp
def _mm_nt(a, b):
    return lax.dot_general(a.astype(BF16), b.astype(BF16), (((1,), (1,)), ((), ())), preferred_element_type=F32)


def _mm_nt_f(a, b):
    return _mm_nt(a, b), (a, b)


def _mm_nt_b(res, g):
    a, b = res
    gb = g.astype(BF16)
    da = jnp.dot(gb, b.astype(BF16), preferred_element_type=F32)
    db = lax.dot_general(gb, a.astype(BF16), (((0,), (0,)), ((), ())), preferred_element_type=F32)
    return da.astype(a.dtype), db.astype(b.dtype)


_mm_nt.defvjp(_mm_nt_f, _mm_nt_b)


@jax.custom_vjp
def _mm_tn(a, b):
    return lax.dot_general(a.astype(BF16), b.astype(BF16), (((0,), (0,)), ((), ())), preferred_element_type=F32)


def _mm_tn_f(a, b):
    return _mm_tn(a, b), (a, b)


def _mm_tn_b(res, g):
    a, b = res
    gb = g.astype(BF16)
    da = lax.dot_general(b.astype(BF16), gb, (((1,), (1,)), ((), ())), preferred_element_type=F32)
    db = jnp.dot(a.astype(BF16), gb, preferred_element_type=F32)
    return da.astype(a.dtype), db.astype(b.dtype)


_mm_tn.defvjp(_mm_tn_f, _mm_tn_b)


def _mm_hi(a, w):
    return jnp.dot(a, w, precision=lax.Precision.HIGHEST, preferred_element_type=F32)


def _iota(shape, dim):
    return lax.broadcasted_iota(jnp.int32, shape, dim)


def _head_sum_mats():
    e = (_iota((GW, HP), 0) // HD == _iota((GW, HP), 1)).astype(F32)
    et = (_iota((HP, GW), 1) // HD == _iota((HP, GW), 0)).astype(F32)
    return e, et


def _modulate(x, mod_ref, sh, sc):
    return x * (1.0 + mod_ref[sc:sc + 1, :]) + mod_ref[sh:sh + 1, :]


def _ln_res(x, y, gate, lg, lb, gs):
    r = ALPHA * x + gs * (1.0 + gate) * y
    mu = jnp.mean(r, axis=-1, keepdims=True)
    var = jnp.mean(jnp.square(r - mu), axis=-1, keepdims=True)
    return (r - mu) * lax.rsqrt(var + LN_EPS) * lg + lb


def _rms(x, g):
    return x * lax.rsqrt(jnp.mean(x * x, axis=-1, keepdims=True) + RMS_EPS) * g


def _tile(n, pref):
    return pref if n % pref == 0 else n


def mod_fwd(c8, w, l, b):
    tn = w.shape[3]
    n = N_CHIPS * tn

    def body(c_ref, w_ref, b_ref, o_ref):
        h = jax.nn.silu(c_ref[...]).astype(BF16)
        o_ref[...] = jnp.dot(h, w_ref[...], preferred_element_type=F32) + b_ref[...]

    return pl.pallas_call(
        body, name="mod_fwd", grid=(N_CHIPS,),
        in_specs=[pl.BlockSpec((8, D), lambda j: (0, 0)), pl.BlockSpec((None, None, D, tn), lambda j: (l, j, 0, 0)),
                  pl.BlockSpec((1, tn), lambda j: (0, j))],
        out_specs=pl.BlockSpec((8, tn), lambda j: (0, j)), out_shape=_sds((8, n), F32),
        compiler_params=_cparams(VMEM_BIG))(c8, w, b)


def ffn_in_fwd(x, mod, w_in, l, sh, sc, S):
    T = x.shape[0]
    tm, tn = _tile(S, 512), FF // 2
    tpb, nj = S // tm, 2

    def body(x_ref, mod_ref, wg_ref, wu_ref, zg_ref, zu_ref, act_ref, h_ref):
        @pl.when(pl.program_id(1) == 0)
        def _():
            h_ref[...] = _modulate(x_ref[...], mod_ref, sh, sc).astype(BF16)
        g = jnp.dot(h_ref[...], wg_ref[...], preferred_element_type=F32)
        u = jnp.dot(h_ref[...], wu_ref[...], preferred_element_type=F32)
        zg_ref[...] = g
        zu_ref[...] = u
        act_ref[...] = (jax.nn.silu(g) * u).astype(BF16)

    return pl.pallas_call(
        body, name="ffn_in_fwd", grid=(T // tm, nj),
        in_specs=[pl.BlockSpec((tm, D), lambda i, j: (i, 0)),
                  pl.BlockSpec((None, N_MOD, D), lambda i, j: (i // tpb, 0, 0)),
                  pl.BlockSpec((None, None, D, tn), lambda i, j: (l, j, 0, 0)),
                  pl.BlockSpec((None, None, D, tn), lambda i, j: (l, j + nj, 0, 0))],
        out_specs=[pl.BlockSpec((tm, tn), lambda i, j: (i, j))] * 3,
        out_shape=[_sds((T, FF), F32), _sds((T, FF), F32), _sds((T, FF), BF16)],
        scratch_shapes=[pltpu.VMEM((tm, D), BF16)],
        compiler_params=_cparams(VMEM_BIG))(x, mod, w_in, w_in)


def mix_in_fwd(x, mod, w, sh, sc, S):
    T = x.shape[0]
    n = w.shape[1]
    tm, tn = _tile(S, 512), NP_TILE
    tpb = S // tm

    def body(x_ref, mod_ref, w_ref, o_ref, h_ref):
        @pl.when(pl.program_id(1) == 0)
        def _():
            h_ref[...] = _modulate(x_ref[...], mod_ref, sh, sc).astype(BF16)
        o_ref[...] = jnp.dot(h_ref[...], w_ref[...], preferred_element_type=F32)

    return pl.pallas_call(
        body, name="mix_in_fwd", grid=(T // tm, n // tn),
        in_specs=[pl.BlockSpec((tm, D), lambda i, j: (i, 0)),
                  pl.BlockSpec((None, N_MOD, D), lambda i, j: (i // tpb, 0, 0)),
                  pl.BlockSpec((D, tn), lambda i, j: (0, j))],
        out_specs=pl.BlockSpec((tm, tn), lambda i, j: (i, j)), out_shape=_sds((T, n), F32),
        scratch_shapes=[pltpu.VMEM((tm, D), BF16)],
        compiler_params=_cparams(VMEM_BIG))(x, mod, w)


def out_ln_fwd(act, w_out, x, mod, lg, lb, gate, gs, S, l=None):
    T, K = act.shape
    tm = _tile(S, 512)
    tpb = S // tm

    def body(a_ref, w_ref, x_ref, mod_ref, lg_ref, lb_ref, y_ref, xn_ref):
        y = jnp.dot(a_ref[...], w_ref[...].reshape(K, D), preferred_element_type=F32)
        y_ref[...] = y
        xn_ref[...] = _ln_res(x_ref[...], y, mod_ref[gate:gate + 1, :], lg_ref[...], lb_ref[...], gs)

    if l is None:
        w_spec = pl.BlockSpec((K, D), lambda i: (0, 0))
    else:
        w_spec = pl.BlockSpec((None, N_CHIPS, K // N_CHIPS, D), lambda i: (l, 0, 0, 0))
    return pl.pallas_call(
        body, name="out_ln_fwd", grid=(T // tm,),
        in_specs=[pl.BlockSpec((tm, K), lambda i: (i, 0)), w_spec,
                  pl.BlockSpec((tm, D), lambda i: (i, 0)),
                  pl.BlockSpec((None, N_MOD, D), lambda i: (i // tpb, 0, 0)),
                  pl.BlockSpec((1, D), lambda i: (0, 0)), pl.BlockSpec((1, D), lambda i: (0, 0))],
        out_specs=[pl.BlockSpec((tm, D), lambda i: (i, 0))] * 2,
        out_shape=[_sds((T, D), F32), _sds((T, D), F32)],
        compiler_params=_cparams(VMEM_BIG))(act, w_out, x, mod, lg, lb)


def ln_res_bwd(dxn, x, y, mod, lg, lb, gate, gs, S):
    T = x.shape[0]
    B = T // S
    tm = _tile(S, 512)
    tpb = S // tm

    def body(d_ref, x_ref, y_ref, mod_ref, lg_ref, lb_ref, dx_ref, dy_ref, dg_ref, dlg_ref, dlb_ref):
        i = pl.program_id(0)
        f = functools.partial(_ln_res, gs=gs)
        _, vjp = jax.vjp(f, x_ref[...], y_ref[...], mod_ref[gate:gate + 1, :], lg_ref[...], lb_ref[...])
        dx, dy, dg, dlg, dlb = vjp(d_ref[...])
        dx_ref[...] = dx
        dy_ref[...] = dy.astype(BF16)

        @pl.when(i % tpb == 0)
        def _():
            dg_ref[...] = jnp.zeros_like(dg_ref)

        @pl.when(i == 0)
        def _():
            dlg_ref[...] = jnp.zeros_like(dlg_ref)
            dlb_ref[...] = jnp.zeros_like(dlb_ref)

        dg_ref[...] += dg
        dlg_ref[...] += dlg
        dlb_ref[...] += dlb

    tok = pl.BlockSpec((tm, D), lambda i: (i, 0))
    vec = pl.BlockSpec((1, D), lambda i: (0, 0))
    return pl.pallas_call(
        body, name="ln_res_bwd", grid=(T // tm,),
        in_specs=[tok, tok, tok, pl.BlockSpec((None, N_MOD, D), lambda i: (i // tpb, 0, 0)), vec, vec],
        out_specs=[tok, tok, pl.BlockSpec((None, 1, D), lambda i: (i // tpb, 0, 0)), vec, vec],
        out_shape=[_sds((T, D), F32), _sds((T, D), BF16), _sds((B, 1, D), F32), _sds((1, D), F32), _sds((1, D), F32)],
        compiler_params=_cparams(VMEM_BIG))(dxn, x, y, mod, lg, lb)


def swiglu_bwd(dy, w_out, l, zg, zu, S):
    T = dy.shape[0]
    tm, tn = _tile(S, 512), FF // 2

    def body(dy_ref, w_ref, zg_ref, zu_ref, dg_ref, du_ref):
        da = lax.dot_general(dy_ref[...], w_ref[...].reshape(tn, D), (((1,), (1,)), ((), ())), preferred_element_type=F32)
        g, u = zg_ref[...], zu_ref[...]
        sg = jax.nn.sigmoid(g)
        dg_ref[...] = (da * u * (sg * (1.0 + g * (1.0 - sg)))).astype(BF16)
        du_ref[...] = (da * (g * sg)).astype(BF16)

    zt = pl.BlockSpec((tm, tn), lambda i, j: (i, j))
    return pl.pallas_call(
        body, name="swiglu_bwd", grid=(T // tm, FF // tn),
        in_specs=[pl.BlockSpec((tm, D), lambda i, j: (i, 0)),
                  pl.BlockSpec((None, 2, FF // N_CHIPS, D), lambda i, j: (l, j, 0, 0)), zt, zt],
        out_specs=[zt, zt], out_shape=[_sds((T, FF), BF16), _sds((T, FF), BF16)],
        compiler_params=_cparams(VMEM_BIG))(dy, w_out, zg, zu)


def nt_plain(dy, w):
    T = dy.shape[0]
    K = w.shape[0]
    tm = _tile(T, 512)

    def body(dy_ref, w_ref, o_ref):
        o_ref[...] = lax.dot_general(dy_ref[...], w_ref[...], (((1,), (1,)), ((), ())), preferred_element_type=F32)

    return pl.pallas_call(
        body, name="nt_plain", grid=(T // tm,),
        in_specs=[pl.BlockSpec((tm, D), lambda i: (i, 0)), pl.BlockSpec((K, D), lambda i: (0, 0))],
        out_specs=pl.BlockSpec((tm, K), lambda i: (i, 0)), out_shape=_sds((T, K), F32),
        compiler_params=_cparams(VMEM_BIG))(dy, w)


def tn_mm(a, b, tk):
    T, K = a.shape
    N = b.shape[1]
    tt = _tile(T, 512)

    def body(a_ref, b_ref, o_ref):
        @pl.when(pl.program_id(1) == 0)
        def _():
            o_ref[...] = jnp.zeros_like(o_ref)
        o_ref[...] += lax.dot_general(a_ref[...], b_ref[...], (((0,), (0,)), ((), ())), preferred_element_type=F32)

    return pl.pallas_call(
        body, name="tn_mm", grid=(K // tk, T // tt),
        in_specs=[pl.BlockSpec((tt, tk), lambda k, t: (t, k)), pl.BlockSpec((tt, N), lambda k, t: (t, 0))],
        out_specs=pl.BlockSpec((tk, N), lambda k, t: (k, 0)), out_shape=_sds((K, N), F32),
        compiler_params=_cparams(VMEM_BIG))(a, b)


def tn_mm_mod(x, mod, b, sh, sc, S, tn):
    T = x.shape[0]
    N = b.shape[1]
    tt = _tile(S, 512)
    tpb = S // tt

    def body(x_ref, mod_ref, b_ref, o_ref):
        @pl.when(pl.program_id(1) == 0)
        def _():
            o_ref[...] = jnp.zeros_like(o_ref)
        h = _modulate(x_ref[...], mod_ref, sh, sc).astype(BF16)
        o_ref[...] += lax.dot_general(h, b_ref[...], (((0,), (0,)), ((), ())), preferred_element_type=F32)

    return pl.pallas_call(
        body, name="tn_mm_mod", grid=(N // tn, T // tt),
        in_specs=[pl.BlockSpec((tt, D), lambda j, t: (t, 0)),
                  pl.BlockSpec((None, N_MOD, D), lambda j, t: (t // tpb, 0, 0)),
                  pl.BlockSpec((tt, tn), lambda j, t: (t, j))],
        out_specs=pl.BlockSpec((D, tn), lambda j, t: (0, j)), out_shape=_sds((D, N), F32),
        compiler_params=_cparams(VMEM_BIG))(x, mod, b)


def tn_mm_mod_shards(x, mod, bg, bu, sh, sc, S):
    T = x.shape[0]
    tn = FF // 2
    tt = _tile(S, 512)
    tpb = S // tt

    def body(x_ref, mod_ref, bg_ref, bu_ref, o_ref):
        j = pl.program_id(0)

        @pl.when(pl.program_id(1) == 0)
        def _():
            o_ref[...] = jnp.zeros_like(o_ref)
        h = _modulate(x_ref[...], mod_ref, sh, sc).astype(BF16)

        @pl.when(j < 2)
        def _():
            o_ref[...] += lax.dot_general(h, bg_ref[...], (((0,), (0,)), ((), ())), preferred_element_type=F32)

        @pl.when(j >= 2)
        def _():
            o_ref[...] += lax.dot_general(h, bu_ref[...], (((0,), (0,)), ((), ())), preferred_element_type=F32)

    return pl.pallas_call(
        body, name="tn_mm_mod_shards", grid=(N_CHIPS, T // tt),
        in_specs=[pl.BlockSpec((tt, D), lambda j, t: (t, 0)),
                  pl.BlockSpec((None, N_MOD, D), lambda j, t: (t // tpb, 0, 0)),
                  pl.BlockSpec((tt, tn), lambda j, t: (t, jnp.minimum(j, 1))),
                  pl.BlockSpec((tt, tn), lambda j, t: (t, jnp.maximum(j - 2, 0)))],
        out_specs=pl.BlockSpec((None, D, tn), lambda j, t: (j, 0, 0)), out_shape=_sds((N_CHIPS, D, tn), F32),
        compiler_params=_cparams(VMEM_BIG))(x, mod, bg, bu)


def nt_mod_bwd(ds, w, offs, x, mod, dres, sc, S, tk, l=None):
    T = x.shape[0]
    B = T // S
    tm = _tile(S, 512)
    tpb = S // tm
    Kd = ds[0].shape[1]
    nk = Kd // tk
    n_in = len(ds)

    def body(*refs):
        d_refs, w_refs = refs[:n_in], refs[n_in:2 * n_in]
        x_ref, mod_ref, r_ref, dx_ref, dsh_ref, dsc_ref, acc = refs[2 * n_in:]
        i, k = pl.program_id(0), pl.program_id(1)

        @pl.when(k == 0)
        def _():
            acc[...] = jnp.zeros_like(acc)

        for d_ref, w_ref in zip(d_refs, w_refs):
            acc[...] += lax.dot_general(d_ref[...], w_ref[...], (((1,), (1,)), ((), ())), preferred_element_type=F32)

        @pl.when(k == nk - 1)
        def _():
            dh = acc[...]
            dx_ref[...] = dh * (1.0 + mod_ref[sc:sc + 1, :]) + r_ref[...]

            @pl.when(i % tpb == 0)
            def _():
                dsh_ref[...] = jnp.zeros_like(dsh_ref)
                dsc_ref[...] = jnp.zeros_like(dsc_ref)

            dsh_ref[...] += jnp.sum(dh, axis=0, keepdims=True)
            dsc_ref[...] += jnp.sum(dh * x_ref[...], axis=0, keepdims=True)

    tok = pl.BlockSpec((tm, D), lambda i, k: (i, 0))
    vec = pl.BlockSpec((None, 1, D), lambda i, k: (i // tpb, 0, 0))
    in_specs = [pl.BlockSpec((tm, tk), lambda i, k: (i, k)) for _ in ds]
    if l is None:
        in_specs += [pl.BlockSpec((D, tk), functools.partial(lambda i, k, o: (0, k + o), o=off // tk)) for off in offs]
    else:
        in_specs += [pl.BlockSpec((None, None, D, tk), functools.partial(lambda i, k, o: (l, k + o, 0, 0), o=off)) for off in offs]
    in_specs += [tok, pl.BlockSpec((None, N_MOD, D), lambda i, k: (i // tpb, 0, 0)), tok]
    return pl.pallas_call(
        body, name="nt_mod_bwd", grid=(T // tm, nk), in_specs=in_specs,
        out_specs=[tok, vec, vec],
        out_shape=[_sds((T, D), F32), _sds((B, 1, D), F32), _sds((B, 1, D), F32)],
        scratch_shapes=[pltpu.VMEM((tm, D), F32)],
        compiler_params=_cparams(VMEM_BIG))(*ds, *([w] * n_in), x, mod, dres)


def loss_head(y, tgt):
    T = y.shape[0]
    tm = _tile(T, 512)

    def body(y_ref, t_ref, l_ref, d_ref):
        @pl.when(pl.program_id(0) == 0)
        def _():
            l_ref[...] = jnp.zeros_like(l_ref)
        e = y_ref[...] - t_ref[...]
        d_ref[...] = e * (1.0 / D)
        l_ref[...] += 0.5 * jnp.sum(jnp.sum(e * e, axis=1, keepdims=True) * (1.0 / D))

    tok = pl.BlockSpec((tm, D), lambda i: (i, 0))
    return pl.pallas_call(
        body, name="loss_head", grid=(T // tm,), in_specs=[tok, tok],
        out_specs=[pl.BlockSpec((8, 128), lambda i: (0, 0)), tok],
        out_shape=[_sds((8, 128), F32), _sds((T, D), F32)],
        compiler_params=_cparams(VMEM_BIG))(y, tgt)


def _hgrn_block(q, fz, inp, go, st, lb, ng, blk):
    nc = blk // A_CHUNK
    lb_eff = jnp.maximum(lb, LB_FLOOR)
    log_f = jnp.logaddexp(jnp.log(lb_eff), jnp.log1p(-lb) + jax.nn.log_sigmoid(fz))
    k = (1.0 - lb) * jax.nn.sigmoid(-fz) - (lb_eff - lb)
    qf = jax.nn.silu(q)
    same_chunk = _iota((blk, blk), 0) // A_CHUNK == _iota((blk, blk), 1) // A_CHUNK
    tril = (same_chunk & (_iota((blk, blk), 1) <= _iota((blk, blk), 0))).astype(F32)
    G = _mm_hi(tril, log_f)
    e_mat, et_mat = _head_sum_mats()
    G4, q4, k4, v4 = (z.reshape(nc, A_CHUNK, GW) for z in (G, qf, k, inp))
    shp = (nc, A_CHUNK, A_CHUNK, GW)
    causal = _iota(shp, 2) <= _iota(shp, 1)
    decay = jnp.exp(jnp.where(causal, G4[:, :, None, :] - G4[:, None, :, :], -jnp.inf))
    prod = q4[:, :, None, :] * k4[:, None, :, :] * decay
    scores = _mm(prod.reshape(nc * A_CHUNK * A_CHUNK, GW), e_mat.astype(BF16))
    spread = _mm(scores, et_mat.astype(BF16)).reshape(shp)
    o_intra = jnp.sum(spread * v4[:, None, :, :], axis=2).reshape(blk, GW)
    head_diag = (_iota((GW, GW), 0) // HD == _iota((GW, GW), 1) // HD).astype(F32)
    g_last = [jnp.sum(log_f[c * A_CHUNK:(c + 1) * A_CHUNK], axis=0, keepdims=True) for c in range(nc)]
    g_last_b = jnp.concatenate([jnp.broadcast_to(g, (A_CHUNK, GW)) for g in g_last], axis=0)
    q_dec = qf * jnp.exp(G)
    k_end = k * jnp.exp(g_last_b - G)
    outs = []
    for c in range(nc):
        rows = slice(c * A_CHUNK, (c + 1) * A_CHUNK)
        outs.append(_mm_nt(q_dec[rows], st))
        st = st * jnp.exp(g_last[c]) + _mm_tn(inp[rows], k_end[rows]) * head_diag
    o = o_intra + jnp.concatenate(outs, axis=0)
    ms = _mm_hi(o * o, e_mat) * (1.0 / HD)
    o = o * _mm_hi(lax.rsqrt(ms + RMS_EPS), et_mat) * ng
    return o * jax.nn.silu(go), st


HGRN_BLK = 128


def hgrn_fwd(proj, lb, ng, S):
    T = proj.shape[0]
    B = T // S
    blk = min(HGRN_BLK, S)
    nb = S // blk

    def body(p_ref, lb_ref, ng_ref, o_ref, st_out_ref, st_ref):
        @pl.when(pl.program_id(1) == 0)
        def _():
            st_ref[...] = jnp.zeros_like(st_ref)
        st_out_ref[...] = st_ref[...]
        p = p_ref[...]
        o, st = _hgrn_block(p[:, 0:GW], p[:, GW:2 * GW], p[:, 2 * GW:3 * GW], p[:, 3 * GW:4 * GW],
                            st_ref[...], lb_ref[...], ng_ref[...], blk)
        o_ref[...] = o.astype(BF16)
        st_ref[...] = st

    vec = pl.BlockSpec((1, GW), lambda b, j: (0, 0))
    return pl.pallas_call(
        body, name="hgrn_fwd", grid=(B, nb),
        in_specs=[pl.BlockSpec((blk, 4 * GW), lambda b, j: (b * nb + j, C_A // (4 * GW))), vec, vec],
        out_specs=[pl.BlockSpec((blk, GW), lambda b, j: (b * nb + j, 0)),
                   pl.BlockSpec((None, GW, GW), lambda b, j: (b * nb + j, 0, 0))],
        out_shape=[_sds((T, GW), BF16), _sds((B * nb, GW, GW), F32)],
        scratch_shapes=[pltpu.VMEM((GW, GW), F32)],
        compiler_params=_cparams(VMEM_BIG))(proj, lb, ng)


def hgrn_bwd(proj, states, dcat, lb, ng, S):
    T = proj.shape[0]
    B = T // S
    blk = min(HGRN_BLK, S)
    nb = S // blk

    def body(p_ref, st_in_ref, do_ref, lb_ref, ng_ref, dp_ref, dlb_ref, dng_ref, dst_ref):
        b, j = pl.program_id(0), pl.program_id(1)

        @pl.when(j == 0)
        def _():
            dst_ref[...] = jnp.zeros_like(dst_ref)

        @pl.when((b == 0) & (j == 0))
        def _():
            dlb_ref[...] = jnp.zeros_like(dlb_ref)
            dng_ref[...] = jnp.zeros_like(dng_ref)

        p = p_ref[...]
        f = functools.partial(_hgrn_block, blk=blk)
        _, vjp = jax.vjp(f, p[:, 0:GW], p[:, GW:2 * GW], p[:, 2 * GW:3 * GW], p[:, 3 * GW:4 * GW],
                         st_in_ref[...], lb_ref[...], ng_ref[...])
        dq, df, di, dg, dst, dlb, dng = vjp((do_ref[...], dst_ref[...]))
        dp_ref[...] = jnp.concatenate([dq, df, di, dg], axis=1).astype(BF16)
        dst_ref[...] = dst
        dlb_ref[...] += dlb
        dng_ref[...] += dng

    def rev(b, j):
        return b * nb + (nb - 1 - j)

    vec = pl.BlockSpec((1, GW), lambda b, j: (0, 0))
    return pl.pallas_call(
        body, name="hgrn_bwd", grid=(B, nb),
        in_specs=[pl.BlockSpec((blk, 4 * GW), lambda b, j: (rev(b, j), C_A // (4 * GW))),
                  pl.BlockSpec((None, GW, GW), lambda b, j: (rev(b, j), 0, 0)),
                  pl.BlockSpec((blk, GW), lambda b, j: (rev(b, j), O_A // GW)), vec, vec],
        out_specs=[pl.BlockSpec((blk, 4 * GW), lambda b, j: (rev(b, j), 0)), vec, vec],
        out_shape=[_sds((T, 4 * GW), BF16), _sds((1, GW), F32), _sds((1, GW), F32)],
        scratch_shapes=[pltpu.VMEM((GW, GW), F32)],
        compiler_params=_cparams(VMEM_BIG))(proj, states, dcat, lb, ng)


ATT_TQ = 256


def _logits(q, k_ref, c_ref, h, i, j, t, scale):
    kj = k_ref[pl.ds(pl.multiple_of(j * t, t), t), :].astype(BF16)
    s = lax.dot_general(q, kj, (((1,), (1,)), ((), ())), preferred_element_type=F32) * scale
    if c_ref is not None:
        s = s - c_ref[j, pl.ds(h, 1), :]
    shape = (t, t)
    return jnp.where(j * t + _iota(shape, 1) <= i * t + _iota(shape, 0), s, -jnp.inf), kj


def attn_fwd(qa, qo, ka, ko, va, vo, cum, scale, S):
    T = qa.shape[0]
    B = T // S
    t = min(ATT_TQ, S)
    nq = S // t
    use_cum = cum is not None

    def body(*refs):
        if use_cum:
            q_ref, k_ref, v_ref, c_ref, o_ref, lse_ref = refs
        else:
            (q_ref, k_ref, v_ref, o_ref, lse_ref), c_ref = refs, None
        h, i = pl.program_id(1), pl.program_id(2)
        q = q_ref[...].astype(BF16)

        def step(j, carry):
            m, l, acc = carry
            s, _ = _logits(q, k_ref, c_ref, h, i, j, t, scale)
            m_new = jnp.maximum(m, jnp.max(s, axis=-1, keepdims=True))
            alpha = jnp.exp(m - m_new)
            p = jnp.exp(s - m_new)
            vj = v_ref[pl.ds(pl.multiple_of(j * t, t), t), :].astype(BF16)
            acc = alpha * acc + jnp.dot(p.astype(BF16), vj, preferred_element_type=F32)
            return m_new, alpha * l + jnp.sum(p, axis=-1, keepdims=True), acc

        init = (jnp.full((t, 1), -jnp.inf, F32), jnp.zeros((t, 1), F32), jnp.zeros((t, HP), F32))
        m, l, acc = lax.fori_loop(0, i + 1, step, init)
        o_ref[...] = acc / l
        lse_ref[...] = m + jnp.log(l)

    in_specs = [pl.BlockSpec((t, HP), lambda b, h, i: (b * nq + i, qo + h)),
                pl.BlockSpec((S, HP), lambda b, h, i: (b, ko + h)),
                pl.BlockSpec((S, HP), lambda b, h, i: (b, vo + h))]
    args = [qa, ka, va]
    if use_cum:
        in_specs.append(pl.BlockSpec((None, nq, 8, t), lambda b, h, i: (b, 0, 0, 0)))
        args.append(cum)
    return pl.pallas_call(
        body, name="attn_fwd", grid=(B, HEADS, nq), in_specs=in_specs,
        out_specs=[pl.BlockSpec((t, HP), lambda b, h, i: (b * nq + i, h)),
                   pl.BlockSpec((None, t, 1), lambda b, h, i: (b * HEADS + h, i, 0))],
        out_shape=[_sds((T, HEADS * HP), F32), _sds((B * HEADS, S, 1), F32)],
        compiler_params=_cparams(VMEM_BIG))(*args)


def attn_bwd(qa, qo, ka, ko, va, vo, cum, o, dcat, o_off, lse, scale, S, out_dtype):
    T = qa.shape[0]
    B = T // S
    t = min(ATT_TQ, S)
    nq = S // t
    use_cum = cum is not None
    nt = (((0,), (0,)), ((), ()))

    def body(*refs):
        if use_cum:
            q_ref, k_ref, v_ref, o_ref, do_ref, lse_ref, c_ref, dq_ref, dk_ref, dv_ref, dc_ref, dk_acc, dv_acc, dc_acc = refs
        else:
            q_ref, k_ref, v_ref, o_ref, do_ref, lse_ref, dq_ref, dk_ref, dv_ref, dk_acc, dv_acc = refs
            c_ref = None
        h, i = pl.program_id(1), pl.program_id(2)

        @pl.when(i == 0)
        def _():
            dk_acc[...] = jnp.zeros_like(dk_acc)
            dv_acc[...] = jnp.zeros_like(dv_acc)
            if use_cum:
                dc_acc[...] = jnp.zeros_like(dc_acc)

        q = q_ref[...].astype(BF16)
        do = do_ref[...]
        delta = jnp.sum(do * o_ref[...], axis=-1, keepdims=True)
        do = do.astype(BF16)
        lse_i = lse_ref[...]
        if use_cum:
            def row_dot(j, acc):
                s, _ = _logits(q, k_ref, c_ref, h, i, j, t, scale)
                vj = v_ref[pl.ds(pl.multiple_of(j * t, t), t), :].astype(BF16)
                dp = lax.dot_general(do, vj, (((1,), (1,)), ((), ())), preferred_element_type=F32)
                return acc + jnp.sum(jnp.exp(s - lse_i) * dp, axis=-1, keepdims=True)

            delta = lax.fori_loop(0, i + 1, row_dot, jnp.zeros((t, 1), F32))

        def step(j, dq):
            s, kj = _logits(q, k_ref, c_ref, h, i, j, t, scale)
            p = jnp.exp(s - lse_i)
            rows = pl.ds(pl.multiple_of(j * t, t), t)
            vj = v_ref[rows, :].astype(BF16)
            dv_acc[rows, :] += lax.dot_general(p.astype(BF16), do, nt, preferred_element_type=F32)
            dp = lax.dot_general(do, vj, (((1,), (1,)), ((), ())), preferred_element_type=F32)
            ds = p * (dp - delta)
            if use_cum:
                dc_acc[j] -= jnp.sum(ds, axis=0, keepdims=True)
            ds = ds.astype(BF16)
            dk_acc[rows, :] += lax.dot_general(ds, q, nt, preferred_element_type=F32) * scale
            return dq + jnp.dot(ds, kj, preferred_element_type=F32) * scale

        dq_ref[...] = lax.fori_loop(0, i + 1, step, jnp.zeros((t, HP), F32)).astype(out_dtype)

        @pl.when(i == nq - 1)
        def _():
            dk_ref[...] = dk_acc[...].astype(out_dtype)
            dv_ref[...] = dv_acc[...].astype(out_dtype)
            if use_cum:
                dc_ref[...] = dc_acc[...]

    def qblock(off):
        return pl.BlockSpec((t, HP), lambda b, h, i: (b * nq + i, off + h))

    in_specs = [qblock(qo), pl.BlockSpec((S, HP), lambda b, h, i: (b, ko + h)),
                pl.BlockSpec((S, HP), lambda b, h, i: (b, vo + h)), qblock(0), qblock(o_off),
                pl.BlockSpec((None, t, 1), lambda b, h, i: (b * HEADS + h, i, 0))]
    args = [qa, ka, va, o, dcat, lse]
    kv_out = pl.BlockSpec((S, HP), lambda b, h, i: (b, h))
    out_specs = [qblock(0), kv_out, kv_out]
    out_shape = [_sds((T, HEADS * HP), out_dtype)] * 3
    scratch = [pltpu.VMEM((S, HP), F32), pltpu.VMEM((S, HP), F32)]
    if use_cum:
        in_specs.append(pl.BlockSpec((None, nq, 8, t), lambda b, h, i: (b, 0, 0, 0)))
        args.append(cum)
        out_specs.append(pl.BlockSpec((None, nq, 1, t), lambda b, h, i: (b * HEADS + h, 0, 0, 0)))
        out_shape.append(_sds((B * HEADS, nq, 1, t), F32))
        scratch.append(pltpu.VMEM((nq, 1, t), F32))
    return pl.pallas_call(
        body, name="attn_bwd", grid=(B, HEADS, nq), in_specs=in_specs, out_specs=out_specs, out_shape=out_shape,
        scratch_shapes=scratch, compiler_params=_cparams(VMEM_BIG))(*args)


def _tri(n, upper):
    r, c = _iota((n, n), 0), _iota((n, n), 1)
    return ((r <= c) if upper else (r >= c)).astype(F32)


def fox_gate_fwd(proj, bcol, S):
    T = proj.shape[0]
    B = T // S
    ts = min(ATT_TQ, S)
    nt = S // ts

    def body(p_ref, b_ref, o_ref, carry):
        @pl.when(pl.program_id(1) == 0)
        def _():
            carry[...] = jnp.zeros_like(carry)
        cf = jnp.transpose(p_ref[...])[0:8, :]
        lf = jax.nn.log_sigmoid(cf + b_ref[...])
        cum = _mm_hi(lf, _tri(ts, True)) + carry[...]
        o_ref[...] = cum
        carry[...] += jnp.sum(lf, axis=1, keepdims=True)

    return pl.pallas_call(
        body, name="fox_gate_fwd", grid=(B, nt),
        in_specs=[pl.BlockSpec((ts, HP), lambda b, j: (b * nt + j, C_CF // HP)), pl.BlockSpec((8, 1), lambda b, j: (0, 0))],
        out_specs=pl.BlockSpec((None, None, 8, ts), lambda b, j: (b, j, 0, 0)), out_shape=_sds((B, nt, 8, ts), F32),
        scratch_shapes=[pltpu.VMEM((8, 1), F32)],
        compiler_params=_cparams(VMEM_BIG))(proj, bcol)


def fox_gate_bwd(proj, bcol, dcum, S):
    T = proj.shape[0]
    B = T // S
    ts = _tile(S, 512)
    nt = S // ts

    def body(p_ref, b_ref, dc_ref, dp_ref, db_ref, carry):
        b, j = pl.program_id(0), pl.program_id(1)

        @pl.when(j == 0)
        def _():
            carry[...] = jnp.zeros_like(carry)

        @pl.when((b == 0) & (j == 0))
        def _():
            db_ref[...] = jnp.zeros_like(db_ref)

        cf = jnp.transpose(p_ref[...])[0:8, :]
        dc = dc_ref[...]
        dlf = _mm_hi(dc, _tri(ts, False)) + carry[...]
        carry[...] += jnp.sum(dc, axis=1, keepdims=True)
        dcf = dlf * jax.nn.sigmoid(-(cf + b_ref[...]))
        db_ref[...] += jnp.sum(dcf, axis=1, keepdims=True)
        full = jnp.concatenate([dcf, jnp.zeros((HP - 8, ts), F32)], axis=0)
        dp_ref[...] = jnp.transpose(full).astype(BF16)

    def rev(b, j):
        return nt - 1 - j

    return pl.pallas_call(
        body, name="fox_gate_bwd", grid=(B, nt),
        in_specs=[pl.BlockSpec((ts, HP), lambda b, j: (b * nt + rev(b, j), C_CF // HP)),
                  pl.BlockSpec((8, 1), lambda b, j: (0, 0)),
                  pl.BlockSpec((None, 8, ts), lambda b, j: (b, 0, rev(b, j)))],
        out_specs=[pl.BlockSpec((ts, HP), lambda b, j: (b * nt + rev(b, j), 0)), pl.BlockSpec((8, 1), lambda b, j: (0, 0))],
        out_shape=[_sds((T, HP), BF16), _sds((8, 1), F32)],
        scratch_shapes=[pltpu.VMEM((8, 1), F32)],
        compiler_params=_cparams(VMEM_BIG))(proj, bcol, dcum)


def _mla_pre(blk, gq, gkv, wq, wkv, place, cos_q, sin_q, cs_k):
    nq = _rms(blk[:, 0:B_Q_LORA], gq)
    nkv = _rms(blk[:, B_Q_LORA:B_Q_LORA + B_KV_LORA], gkv)
    qq = _mm(nq, wq)
    q = qq[:, 0:HEADS * HP] * cos_q + qq[:, HEADS * HP:] * sin_q
    kv = _mm(nkv, wkv)
    k = kv[:, 0:HEADS * HP] + _mm(blk[:, B_Q_LORA + B_KV_LORA:] * cs_k, place)
    return q, k, kv[:, HEADS * HP:]


def mla_pre_fwd(proj, gq, gkv, wq, wkv, place, cos_q, sin_q, cs_k, S):
    T = proj.shape[0]
    tm = _tile(S, 512)
    tpb = S // tm
    W = HEADS * HP

    def body(p_ref, gq_ref, gkv_ref, wq_ref, wkv_ref, pl_ref, cq_ref, sq_ref, ck_ref, q_ref, k_ref, v_ref):
        q, k, v = _mla_pre(p_ref[...], gq_ref[...], gkv_ref[...], wq_ref[...], wkv_ref[...], pl_ref[...],
                           cq_ref[...], sq_ref[...], ck_ref[...])
        q_ref[...] = q
        k_ref[...] = k
        v_ref[...] = v

    def full(a):
        return pl.BlockSpec(a.shape, lambda i: (0,) * a.ndim)

    tok = pl.BlockSpec((tm, W), lambda i: (i, 0))
    return pl.pallas_call(
        body, name="mla_pre_fwd", grid=(T // tm,),
        in_specs=[pl.BlockSpec((tm, W), lambda i: (i, C_B // W)), full(gq), full(gkv), full(wq), full(wkv), full(place),
                  pl.BlockSpec((tm, W), lambda i: (i % tpb, 0)), pl.BlockSpec((tm, W), lambda i: (i % tpb, 0)),
                  pl.BlockSpec((tm, HP), lambda i: (i % tpb, 0))],
        out_specs=[tok] * 3, out_shape=[_sds((T, W), F32)] * 3,
        compiler_params=_cparams(VMEM_BIG))(proj, gq, gkv, wq, wkv, place, cos_q, sin_q, cs_k)


def mla_pre_bwd(proj, gq, gkv, wq, wkv, place, cos_q, sin_q, cs_k, dq, dk, dv, S):
    T = proj.shape[0]
    tm = _tile(S, 512)
    tpb = S // tm
    W = HEADS * HP

    def body(p_ref, gq_ref, gkv_ref, wq_ref, wkv_ref, pl_ref, cq_ref, sq_ref, ck_ref, dq_ref, dk_ref, dv_ref,
             dp_ref, dgq_ref, dgkv_ref, dwq_ref, dwkv_ref):
        @pl.when(pl.program_id(0) == 0)
        def _():
            for r in (dgq_ref, dgkv_ref, dwq_ref, dwkv_ref):
                r[...] = jnp.zeros_like(r)

        f = functools.partial(_mla_pre, place=pl_ref[...], cos_q=cq_ref[...], sin_q=sq_ref[...], cs_k=ck_ref[...])
        _, vjp = jax.vjp(f, p_ref[...], gq_ref[...], gkv_ref[...], wq_ref[...], wkv_ref[...])
        dp, dgq, dgkv, dwq, dwkv = vjp((dq_ref[...], dk_ref[...], dv_ref[...]))
        dp_ref[...] = dp.astype(BF16)
        dgq_ref[...] += dgq
        dgkv_ref[...] += dgkv
        dwq_ref[...] += dwq
        dwkv_ref[...] += dwkv

    def full(a):
        return pl.BlockSpec(a.shape, lambda i: (0,) * a.ndim)

    tok = pl.BlockSpec((tm, W), lambda i: (i, 0))
    return pl.pallas_call(
        body, name="mla_pre_bwd", grid=(T // tm,),
        in_specs=[pl.BlockSpec((tm, W), lambda i: (i, C_B // W)), full(gq), full(gkv), full(wq), full(wkv), full(place),
                  pl.BlockSpec((tm, W), lambda i: (i % tpb, 0)), pl.BlockSpec((tm, W), lambda i: (i % tpb, 0)),
                  pl.BlockSpec((tm, HP), lambda i: (i % tpb, 0)), tok, tok, tok],
        out_specs=[tok, full(gq), full(gkv), full(wq), full(wkv)],
        out_shape=[_sds((T, W), BF16), _sds(gq.shape, F32), _sds(gkv.shape, F32), _sds(wq.shape, F32), _sds(wkv.shape, F32)],
        compiler_params=_cparams(VMEM_BIG))(proj, gq, gkv, wq, wkv, place, cos_q, sin_q, cs_k, dq, dk, dv)


def _gmlp_block(blk, lg, lb, ws, bs):
    u = jax.nn.gelu(blk[:, 0:GW])
    v = jax.nn.gelu(blk[:, GW:2 * GW])
    mu = jnp.mean(v, axis=-1, keepdims=True)
    var = jnp.mean(jnp.square(v - mu), axis=-1, keepdims=True)
    vn = (v - mu) * lax.rsqrt(var + LN_EPS) * lg + lb
    causal = _iota((D_CHUNK, D_CHUNK), 1) <= _iota((D_CHUNK, D_CHUNK), 0)
    group = _iota((1, GW), 1) // HD
    mixed = jnp.zeros((D_CHUNK, GW), F32)
    for g in range(HEADS):
        part = _mm(jnp.where(causal, ws[g], 0.0), vn) + bs[g]
        mixed = mixed + jnp.where(group == g, part, 0.0)
    return u * mixed


def gmlp_fwd(proj, lg, lb, ws, bs):
    T = proj.shape[0]

    def body(p_ref, lg_ref, lb_ref, ws_ref, bs_ref, o_ref):
        o_ref[...] = _gmlp_block(p_ref[...], lg_ref[...], lb_ref[...], ws_ref[...], bs_ref[...]).astype(BF16)

    def full(a):
        return pl.BlockSpec(a.shape, lambda i: (0,) * a.ndim)

    return pl.pallas_call(
        body, name="gmlp_fwd", grid=(T // D_CHUNK,),
        in_specs=[pl.BlockSpec((D_CHUNK, 2 * GW), lambda i: (i, C_D // (2 * GW))), full(lg), full(lb), full(ws), full(bs)],
        out_specs=pl.BlockSpec((D_CHUNK, GW), lambda i: (i, 0)), out_shape=_sds((T, GW), BF16),
        compiler_params=_cparams(VMEM_BIG))(proj, lg, lb, ws, bs)


def gmlp_bwd(proj, lg, lb, ws, bs, dcat):
    T = proj.shape[0]

    def body(p_ref, lg_ref, lb_ref, ws_ref, bs_ref, do_ref, dp_ref, dlg_ref, dlb_ref, dws_ref, dbs_ref):
        @pl.when(pl.program_id(0) == 0)
        def _():
            for r in (dlg_ref, dlb_ref, dws_ref, dbs_ref):
                r[...] = jnp.zeros_like(r)

        _, vjp = jax.vjp(_gmlp_block, p_ref[...], lg_ref[...], lb_ref[...], ws_ref[...], bs_ref[...])
        dp, dlg, dlb, dws, dbs = vjp(do_ref[...])
        dp_ref[...] = dp.astype(BF16)
        dlg_ref[...] += dlg
        dlb_ref[...] += dlb
        dws_ref[...] += dws
        dbs_ref[...] += dbs

    def full(a):
        return pl.BlockSpec(a.shape, lambda i: (0,) * a.ndim)

    return pl.pallas_call(
        body, name="gmlp_bwd", grid=(T // D_CHUNK,),
        in_specs=[pl.BlockSpec((D_CHUNK, 2 * GW), lambda i: (i, C_D // (2 * GW))), full(lg), full(lb), full(ws), full(bs),
                  pl.BlockSpec((D_CHUNK, GW), lambda i: (i, O_D // GW))],
        out_specs=[pl.BlockSpec((D_CHUNK, 2 * GW), lambda i: (i, 0)), full(lg), full(lb), full(ws), full(bs)],
        out_shape=[_sds((T, 2 * GW), BF16), _sds(lg.shape, F32), _sds(lb.shape, F32), _sds(ws.shape, F32), _sds(bs.shape, F32)],
        compiler_params=_cparams(VMEM_BIG))(proj, lg, lb, ws, bs, dcat)


def _lb_all(logits):
    m = jnp.max(logits, axis=0, keepdims=True)
    e = jnp.exp(logits - m)
    sm = e / jnp.sum(e, axis=0, keepdims=True)
    return jnp.concatenate([sm[0:1] - sm[0:1], (sm[0:1] + sm[1:2]) - sm[0:1]], axis=0)


def lb_fwd(logits):
    def body(l_ref, o_ref):
        o_ref[...] = _lb_all(l_ref[...])

    return pl.pallas_call(body, name="lb_fwd", out_shape=_sds(logits.shape, F32))(logits)


def lb_bwd(logits, dlb):
    def body(l_ref, d_ref, o_ref):
        _, vjp = jax.vjp(_lb_all, l_ref[...])
        o_ref[...] = vjp(d_ref[...])[0]

    return pl.pallas_call(body, name="lb_bwd", out_shape=_sds(logits.shape, F32))(logits, dlb)


def ada_grad(c_all, dmod_cols):
    N = dmod_cols.shape[1]
    tn = _tile(N, 1152)

    def body(c_ref, d_ref, o_ref):
        h = jax.nn.silu(c_ref[...]).astype(BF16)
        o_ref[...] = lax.dot_general(h, d_ref[...].astype(BF16), (((0,), (0,)), ((), ())), preferred_element_type=F32)

    nb = c_all.shape[0]
    return pl.pallas_call(
        body, name="ada_grad", grid=(N // tn,),
        in_specs=[pl.BlockSpec((nb, D), lambda j: (0, 0)), pl.BlockSpec((nb, tn), lambda j: (0, j))],
        out_specs=pl.BlockSpec((D, tn), lambda j: (0, j)), out_shape=_sds((D, N), F32),
        compiler_params=_cparams(VMEM_BIG))(c_all, dmod_cols)


def sum_slots(a, n, name):
    _, R, C = a.shape
    tr = _row_tile(R, C, n)

    def body(a_ref, o_ref):
        acc = a_ref[0]
        for k in range(1, n):
            acc = acc + a_ref[k]
        o_ref[...] = acc

    return pl.pallas_call(
        body, name=name, grid=(R // tr,),
        in_specs=[pl.BlockSpec((n, tr, C), lambda i: (0, i, 0))],
        out_specs=pl.BlockSpec((tr, C), lambda i: (i, 0)), out_shape=_sds((R, C), F32),
        compiler_params=_cparams(VMEM_BIG))(a)


def add2(a, b, name):
    shp = a.shape
    C = shp[-1]
    a2, b2 = a.reshape(-1, C), b.reshape(-1, C)
    R = a2.shape[0]
    tr = _row_tile(R, C)

    def body(a_ref, b_ref, o_ref):
        o_ref[...] = a_ref[...] + b_ref[...]

    spec = pl.BlockSpec((tr, C), lambda i: (i, 0))
    return pl.pallas_call(body, name=name, grid=(R // tr,), in_specs=[spec, spec], out_specs=spec,
                          out_shape=_sds((R, C), F32), compiler_params=_cparams(VMEM_BIG))(a2, b2).reshape(shp)


def _row_tile(R, C=D, n=1):
    limit = max(8, (1 << 18) // (C * n))
    for t in range(limit - limit % 8, 7, -8):
        if R % t == 0:
            return t
    return R


def adamw(w, g, m, v, name):
    R, C = w.shape
    tr = _row_tile(R, C)
    c1 = 1.0 - ADAM_B1 ** ADAM_STEP
    c2 = 1.0 - ADAM_B2 ** ADAM_STEP

    def body(w_ref, g_ref, m_ref, v_ref, d_ref, nm_ref, nv_ref):
        g_ = g_ref[...]
        nm = ADAM_B1 * m_ref[...] + (1.0 - ADAM_B1) * g_
        nv = ADAM_B2 * v_ref[...] + (1.0 - ADAM_B2) * jnp.square(g_)
        d_ref[...] = -ADAM_LR * ((nm / c1) / (jnp.sqrt(nv / c2) + ADAM_EPS) + ADAM_WD * w_ref[...])
        nm_ref[...] = nm
        nv_ref[...] = nv

    spec = pl.BlockSpec((tr, C), lambda i: (i, 0))
    return pl.pallas_call(body, name=name, grid=(R // tr,), in_specs=[spec] * 4, out_specs=[spec] * 3,
                          out_shape=[_sds((R, C), F32)] * 3, compiler_params=_cparams(VMEM_BIG))(w, g, m, v)


def _rot_cols(w):
    return jnp.concatenate([-w[:, 16:32], w[:, 0:16]], axis=1)


def _fold_rot(d):
    return jnp.concatenate([d[:, 16:32], -d[:, 0:16]], axis=1)


def _pad_heads(w, off, axis):
    parts = []
    for h in range(HEADS):
        piece = lax.slice_in_dim(w, off + HD * h, off + HD * (h + 1), axis=axis)
        parts += [piece, jnp.zeros_like(piece)]
    return parts


def _unpad_heads(d, off, axis):
    return [lax.slice_in_dim(d, off + HP * h, off + HP * h + HD, axis=axis) for h in range(HEADS)]


def mix_in_ext(w):
    z = lambda n: jnp.zeros((w.shape[0], n), w.dtype)
    kr = w[:, 1408:1440]
    cols = [w[:, 0:1408], kr, _rot_cols(kr), z(64)]
    cols += _pad_heads(w, 1440, 1) + _pad_heads(w, 1696, 1) + _pad_heads(w, 1952, 1)
    cols += [w[:, 2212:2724], w[:, 2208:2212], z(NP - C_CF - HEADS)]
    return jnp.concatenate(cols, axis=1)


def mix_in_unext(d):
    kr = d[:, 1408:1440] + _fold_rot(d[:, 1440:1472])
    cols = [d[:, 0:1408], kr] + _unpad_heads(d, C_CQ, 1) + _unpad_heads(d, C_CK, 1) + _unpad_heads(d, C_CV, 1)
    cols += [d[:, C_CF:C_CF + HEADS], d[:, C_D:C_D + 2 * GW]]
    return jnp.concatenate(cols, axis=1)


def mix_out_ext(w):
    return jnp.concatenate([w[0:GW]] + _pad_heads(w, GW, 0) + _pad_heads(w, 2 * GW, 0) + [w[3 * GW:4 * GW]], axis=0)


def mix_out_unext(d):
    return jnp.concatenate([d[0:GW]] + _unpad_heads(d, O_B, 0) + _unpad_heads(d, O_C, 0) + [d[O_D:O_D + GW]], axis=0)


def uq_ext(w):
    z = lambda n: jnp.zeros((w.shape[0], n), w.dtype)
    a, b = [], []
    for h in range(HEADS):
        o = (B_NOPE + B_ROPE) * h
        a += [w[:, o:o + B_NOPE + B_ROPE], z(32)]
        b += [z(B_NOPE), _rot_cols(w[:, o + B_NOPE:o + B_NOPE + B_ROPE]), z(32)]
    return jnp.concatenate(a + b, axis=1)


def uq_unext(d):
    cols = []
    for h in range(HEADS):
        o = HP * h
        cols += [d[:, o:o + B_NOPE], d[:, o + B_NOPE:o + B_NOPE + B_ROPE]
                 + _fold_rot(d[:, HEADS * HP + o + B_NOPE:HEADS * HP + o + B_NOPE + B_ROPE])]
    return jnp.concatenate(cols, axis=1)


def ukv_ext(w):
    z = jnp.zeros((w.shape[0], HD), w.dtype)
    k, v = [], []
    for h in range(HEADS):
        k += [w[:, 2 * HD * h:2 * HD * h + HD], z]
        v += [w[:, 2 * HD * h + HD:2 * HD * (h + 1)], z]
    return jnp.concatenate(k + v, axis=1)


def ukv_unext(d):
    cols = []
    for h in range(HEADS):
        cols += [d[:, HP * h:HP * h + HD], d[:, HEADS * HP + HP * h:HEADS * HP + HP * h + HD]]
    return jnp.concatenate(cols, axis=1)


def rope_tables(S):
    half = B_ROPE // 2
    inv_freq = ROPE_THETA ** (-jnp.arange(half, dtype=F32) / half)
    ang = jnp.arange(S).astype(F32)[:, None] * inv_freq[None, :]
    cos = jnp.tile(jnp.cos(ang), (1, 2))
    sin = jnp.tile(jnp.sin(ang), (1, 2))
    one, zero = jnp.ones((S, B_NOPE), F32), jnp.zeros((S, B_NOPE), F32)
    z32 = jnp.zeros((S, 32), F32)
    cos_q = jnp.tile(jnp.concatenate([one, cos, z32], axis=1), (1, HEADS))
    sin_q = jnp.tile(jnp.concatenate([zero, sin, z32], axis=1), (1, HEADS))
    cs_k = jnp.concatenate([cos, sin, zero], axis=1)
    place = np.zeros((HP, HEADS * HP), np.float32)
    for h in range(HEADS):
        for j in range(B_ROPE):
            place[j, h * HP + B_NOPE + j] = 1.0
            place[B_ROPE + j, h * HP + B_NOPE + j] = 1.0
    return cos_q, sin_q, cs_k, jnp.asarray(place, BF16)


def layer_fwd(x, mod, get, tabs, S):
    cos_q, sin_q, cs_k, place = tabs
    p = dict(get("ffn1", x))
    l = p["wl"]
    zg1, zu1, act1 = ffn_in_fwd(x, mod, p["ffn1_in"], l, 0, 1, S)
    y1, x1 = out_ln_fwd(act1, p["ffn1_out"], x, mod, p["ln_g"][0:1], p["ln_b"][0:1], 2, 0.5, S, l)
    p.update(get("mix", x1))
    proj = mix_in_fwd(x1, mod, p["mix_in"], 3, 4, S)
    o_a, states = hgrn_fwd(proj, p["lb"], p["ng"], S)
    q_b, k_b, v_b = mla_pre_fwd(proj, p["gq"], p["gkv"], p["wq"], p["wkv"], place, cos_q, sin_q, cs_k, S)
    o_b, lse_b = attn_fwd(q_b, 0, k_b, 0, v_b, 0, None, (B_NOPE + B_ROPE) ** -0.5, S)
    cum = fox_gate_fwd(proj, p["bcol"], S)
    o_c, lse_c = attn_fwd(proj, C_CQ // HP, proj, C_CK // HP, proj, C_CV // HP, cum, HD ** -0.5, S)
    o_d = gmlp_fwd(proj, p["g_lg"], p["g_lb"], p["ws"], p["bs"])
    cat = jnp.concatenate([o_a, o_b.astype(BF16), o_c.astype(BF16), o_d], axis=1)
    y2, x2 = out_ln_fwd(cat, p["mix_out"], x1, mod, p["ln_g"][1:2], p["ln_b"][1:2], 5, 1.0, S)
    p.update(get("ffn2", x2))
    zg3, zu3, act3 = ffn_in_fwd(x2, mod, p["ffn2_in"], l, 6, 7, S)
    y3, x3 = out_ln_fwd(act3, p["ffn2_out"], x2, mod, p["ln_g"][2:3], p["ln_b"][2:3], 8, 0.5, S, l)
    saved = dict(x=x, zg1=zg1, zu1=zu1, act1=act1, y1=y1, x1=x1, proj=proj, states=states, q_b=q_b, k_b=k_b, v_b=v_b,
                 cum=cum, lse_b=lse_b, lse_c=lse_c, o_b=o_b, o_c=o_c, cat=cat, y2=y2, x2=x2, zg3=zg3, zu3=zu3, act3=act3, y3=y3, p=p)
    return x3, saved


def _ffn_bwd(dxn, x_in, y, zg, zu, act, mod, w_in, w_out, l, lg, lb, idx, S, emit):
    sh, sc, gate = idx
    dres, dy, dgate, dlg, dlb = ln_res_bwd(dxn, x_in, y, mod, lg, lb, gate, 0.5, S)
    dzg, dzu = swiglu_bwd(dy, w_out, l, zg, zu, S)
    dw_out = tn_mm(act, dy, FF // 2).reshape(N_CHIPS, FF // N_CHIPS, D)
    dw_in = tn_mm_mod_shards(x_in, mod, dzg, dzu, sh, sc, S)
    mod = mod + emit(dw_in, dw_out)
    dx, dsh, dsc = nt_mod_bwd([dzg, dzu], w_in, [0, 2], x_in, mod, dres, sc, S, FF // 2, l)
    return dx, dw_in, dw_out, dlg, dlb, {sh: dsh, sc: dsc, gate: dgate}, mod


def layer_bwd(dx3, mod, sv, tabs, S, emit):
    cos_q, sin_q, cs_k, place = tabs
    p = sv["p"]
    l = p["wl"]
    g = {}
    dm = {}

    def emit_ffn(part):
        def f(dw_in, dw_out):
            g[part + "_in"], g[part + "_out"] = dw_in, dw_out
            return emit(part, g)
        return f

    dx2, _, _, dlg2, dlb2, d, mod = _ffn_bwd(
        dx3, sv["x2"], sv["y3"], sv["zg3"], sv["zu3"], sv["act3"], mod, p["ffn2_in"], p["ffn2_out"], l,
        p["ln_g"][2:3], p["ln_b"][2:3], (6, 7, 8), S, emit_ffn("ffn2"))
    dm.update(d)
    dres, dy2, dm[5], dlg1, dlb1 = ln_res_bwd(dx2, sv["x1"], sv["y2"], mod, p["ln_g"][1:2], p["ln_b"][1:2], 5, 1.0, S)
    dcat = nt_plain(dy2, p["mix_out"])
    g["mix_out"] = tn_mm(sv["cat"], dy2, 768)
    proj = sv["proj"]
    d_a, g["lb"], g["ng"] = hgrn_bwd(proj, sv["states"], dcat, p["lb"], p["ng"], S)
    dq_c, dk_c, dv_c, dcum = attn_bwd(proj, C_CQ // HP, proj, C_CK // HP, proj, C_CV // HP, sv["cum"], sv["o_c"], dcat,
                                      O_C // HP, sv["lse_c"], HD ** -0.5, S, BF16)
    B = proj.shape[0] // S
    dcum = jnp.concatenate([dcum.reshape(B, HEADS, S), jnp.zeros((B, 8 - HEADS, S), F32)], axis=1)
    d_cf, g["bcol"] = fox_gate_bwd(proj, p["bcol"], dcum, S)
    dq_b, dk_b, dv_b = attn_bwd(sv["q_b"], 0, sv["k_b"], 0, sv["v_b"], 0, None, sv["o_b"], dcat, O_B // HP,
                                sv["lse_b"], (B_NOPE + B_ROPE) ** -0.5, S, F32)
    d_b, g["gq"], g["gkv"], g["wq"], g["wkv"] = mla_pre_bwd(
        proj, p["gq"], p["gkv"], p["wq"], p["wkv"], place, cos_q, sin_q, cs_k, dq_b, dk_b, dv_b, S)
    d_d, g["g_lg"], g["g_lb"], g["ws"], g["bs"] = gmlp_bwd(proj, p["g_lg"], p["g_lb"], p["ws"], p["bs"], dcat)
    dproj = jnp.concatenate([d_a, d_b, dq_c, dk_c, dv_c, d_d, d_cf, jnp.zeros_like(d_cf)], axis=1)
    g["mix_in"] = tn_mm_mod(sv["x1"], mod, dproj, 3, 4, S, NP_TILE)
    mod = mod + emit("mix", g)
    dx1, dm[3], dm[4] = nt_mod_bwd([dproj], p["mix_in"], [0], sv["x1"], mod, dres, 4, S, NP_TILE)
    last = []

    def emit_last(dw_in, dw_out):
        last.append(emit_ffn("ffn1")(dw_in, dw_out))
        return last[0]

    dx0, _, _, dlg0, dlb0, d, mod = _ffn_bwd(
        dx1, sv["x"], sv["y1"], sv["zg1"], sv["zu1"], sv["act1"], mod, p["ffn1_in"], p["ffn1_out"], l,
        p["ln_g"][0:1], p["ln_b"][0:1], (0, 1, 2), S, emit_last)
    dm.update(d)
    g["ln_g"] = jnp.concatenate([dlg0, dlg1, dlg2], axis=0)
    g["ln_b"] = jnp.concatenate([dlb0, dlb1, dlb2], axis=0)
    dmod = jnp.concatenate([dm[i] for i in range(N_MOD)], axis=1)
    return dx0, dmod, g, last[0]


def local_step(x, c8, tgt, get, lb_logits, S, emit=None):
    B = x.shape[0] // S
    tabs = rope_tables(S)
    lb_all = lb_fwd(lb_logits)
    mods, saved = [], []
    h = x
    for l in range(DEPTH):
        pa = get(l, "ada", h)
        mod = mod_fwd(c8, pa["ada_w"], pa["wl"], pa["ada_b"])[0:B].reshape(B, N_MOD, D)

        def get_l(part, after, l=l):
            p = dict(get(l, part, after))
            if part == "mix":
                p["lb"] = lb_all[l:l + 1]
            return p

        h, sv = layer_fwd(h, mod, get_l, tabs, S)
        mods.append(mod)
        saved.append(sv)
    loss_tile, dh = loss_head(h, tgt)
    grads, dmods, dlb = [None] * DEPTH, [None] * DEPTH, [None] * DEPTH
    tie = jnp.zeros((), F32)
    for l in reversed(range(DEPTH)):
        emit_l = (lambda part, g: jnp.zeros((), F32)) if emit is None else functools.partial(emit, l)
        dh, dmods[l], grads[l], tie = layer_bwd(dh, mods[l] + tie, saved[l], tabs, S, emit_l)
        dlb[l] = grads[l].pop("lb")
    d_logits = lb_bwd(lb_logits, jnp.concatenate(dlb, axis=0))
    return loss_tile, dh, dmods, grads, d_logits


ANY = pl.BlockSpec(memory_space=pl.ANY)


def _place():
    x, y, c = lax.axis_index("x"), lax.axis_index("y"), lax.axis_index("c")
    chips = [(1 - x, y), (x, 1 - y), (1 - x, 1 - y)]
    return x, y, c, chips


def _rcopy(src, dst, sems, k, to):
    send_sems, recv_sems = sems
    return pltpu.make_async_remote_copy(src_ref=src, dst_ref=dst, send_sem=send_sems.at[k], recv_sem=recv_sems.at[k],
                                        device_id=to, device_id_type=MESH)


def _dma_sems(n_remote, n_local):
    return [pltpu.SemaphoreType.DMA((n_remote,)), pltpu.SemaphoreType.DMA((n_remote,)), pltpu.SemaphoreType.DMA((n_local,))]


def own_slot(src, chip):
    L = src.shape[0]
    return lax.dynamic_update_slice(jnp.zeros((L, N_CHIPS) + src.shape[1:], src.dtype), src[:, None], (0, chip, 0, 0))


def ag_shards(arrs, lands):
    n = len(arrs)
    rh = [a.shape[1] // 2 for a in arrs]

    def body(*refs):
        srcs, outs, token = refs[:n], refs[2 * n:3 * n], refs[3 * n]
        send_sems, recv_sems = refs[3 * n + 1:]
        x, y, c, chips = _place()
        sems = (send_sems, recv_sems)
        me = 2 * x + y
        sibling = (x, y, 1 - c)
        token[...] = jnp.zeros_like(token)

        def part(i, k, hc):
            return outs[i].at[:, k, pl.ds(hc * rh[i], rh[i]), :]

        started = []
        for j, (px, py) in enumerate(chips):
            for i in range(n):
                cp = _rcopy(srcs[i].at[:, pl.ds(c * rh[i], rh[i]), :], part(i, me, c), sems, 6 * i + j, (px, py, c))
                cp.start()
                started.append(cp)
        for j, (px, py) in enumerate(chips):
            k = 2 * px + py
            for i in range(n):
                _rcopy(part(i, k, c), part(i, k, c), sems, 6 * i + j, (px, py, c)).wait_recv()
                cp = _rcopy(part(i, k, c), part(i, k, c), sems, 6 * i + 3 + j, sibling)
                cp.start()
                started.append(cp)
        for j, (px, py) in enumerate(chips):
            k = 2 * px + py
            for i in range(n):
                _rcopy(part(i, k, 1 - c), part(i, k, 1 - c), sems, 6 * i + 3 + j, sibling).wait_recv()
        for cp in started:
            cp.wait_send()

    outs = pl.pallas_call(
        body, name="ag_shards", out_shape=[_sds(a.shape, a.dtype) for a in lands] + [_sds((8, 128), F32)],
        in_specs=[ANY] * (2 * n), out_specs=[ANY] * n + [pl.BlockSpec(memory_space=pltpu.VMEM)],
        input_output_aliases={n + i: i for i in range(n)}, scratch_shapes=_dma_sems(6 * n, 1)[:2])(*arrs, *lands)
    return list(outs[:n]), outs[n]


HBM_SPEC = pl.BlockSpec(memory_space=pltpu.HBM)
SEM_SPEC = pl.BlockSpec(memory_space=pltpu.SEMAPHORE)
DATAFLOW = pltpu.SideEffectType.DATAFLOW_SIDE_EFFECTING


def _after(x, dep):
    return lax.optimization_barrier((x, dep))[0]


def _split_start(srcs, lands, copies, name):
    n, m = len(srcs), len(lands)

    def body(*refs):
        ins = refs[:n + m]
        send_sems, recv_sems = refs[n + m], refs[n + m + 1]
        token = refs[-1]
        for k, (src, dst, to) in enumerate(copies(ins[:n], ins[n:], _place())):
            pltpu.make_async_remote_copy(src_ref=src, dst_ref=dst, send_sem=send_sems.at[k], recv_sem=recv_sems.at[k],
                                         device_id=to, device_id_type=MESH).start()
        token[...] = jnp.zeros_like(token)

    n_copies = 3 * n
    arrs = list(srcs) + list(lands)
    outs = pl.pallas_call(
        body, name=name,
        out_shape=(pltpu.SemaphoreType.DMA((n_copies,)), pltpu.SemaphoreType.DMA((n_copies,)),
                   *[pltpu.HBM(a.shape, a.dtype) for a in arrs], _sds((8, 128), F32)),
        in_specs=[HBM_SPEC] * (n + m),
        out_specs=(SEM_SPEC, SEM_SPEC, *[HBM_SPEC] * (n + m), pl.BlockSpec(memory_space=pltpu.VMEM)),
        input_output_aliases={i: 2 + i for i in range(n + m)},
        compiler_params=pltpu.CompilerParams(has_side_effects=DATAFLOW),
    )(*[pltpu.with_memory_space_constraint(a, pltpu.HBM) for a in arrs])
    return outs[0], outs[1], list(outs[2:2 + n]), list(outs[2 + n:2 + n + m]), outs[-1]


def _split_wait(handle, arrivals, after, name):
    send_sems, recv_sems, srcs, lands, _ = handle
    n, m = len(srcs), len(lands)

    def body(*refs):
        ins = refs[:n + m]
        send_sems, recv_sems = refs[n + m], refs[n + m + 1]
        x, y, c, chips = place = _place()
        for k, (src, dst) in enumerate(arrivals(ins[:n], ins[n:], place)):
            cp = pltpu.make_async_remote_copy(src_ref=src, dst_ref=dst, send_sem=send_sems.at[k], recv_sem=recv_sems.at[k],
                                              device_id=(x, y, 1 - c), device_id_type=MESH)
            cp.wait_send()
            cp.wait_recv()

    arrs = list(srcs) + list(lands)
    outs = pl.pallas_call(
        body, name=name, out_shape=[pltpu.HBM(a.shape, a.dtype) for a in arrs],
        in_specs=[HBM_SPEC] * (n + m) + [SEM_SPEC, SEM_SPEC, ANY], out_specs=[HBM_SPEC] * (n + m),
        input_output_aliases={i: i for i in range(n + m)},
        compiler_params=pltpu.CompilerParams(has_side_effects=DATAFLOW),
    )(*arrs, send_sems, recv_sems, after)
    return list(outs[n:])


def _ag_part(ref, k, hc):
    rh = ref.shape[2] // 2
    return ref.at[:, k, pl.ds(hc * rh, rh), :]


def ag_start(srcs, lands, name):
    def copies(s, d, place):
        x, y, c, chips = place
        out = []
        for j, (px, py) in enumerate(chips):
            for i in range(len(s)):
                rh = s[i].shape[1] // 2
                out.append((s[i].at[:, pl.ds(c * rh, rh), :], _ag_part(d[i], 2 * x + y, c), (px, py, c)))
        return out

    return _split_start(srcs, lands, copies, name)


def ag_wait(handle, after, name):
    def arrivals(s, d, place):
        x, y, c, chips = place
        out = []
        for j, (px, py) in enumerate(chips):
            for i in range(len(s)):
                rh = s[i].shape[1] // 2
                out.append((s[i].at[:, pl.ds(c * rh, rh), :], _ag_part(d[i], 2 * px + py, c)))
        return out

    return _split_wait(handle, arrivals, after, name)


def ag_forward(lands, name):
    n = len(lands)

    def body(*refs):
        bufs = refs[n:2 * n]
        send_sems, recv_sems = refs[2 * n:]
        x, y, c, chips = _place()
        sems = (send_sems, recv_sems)
        cps = []
        for j, (px, py) in enumerate(chips):
            for i in range(n):
                part = _ag_part(bufs[i], 2 * px + py, c)
                cps.append(_rcopy(part, part, sems, 3 * i + j, (x, y, 1 - c)))
        for cp in cps:
            cp.start()
        for j, (px, py) in enumerate(chips):
            for i in range(n):
                part = _ag_part(bufs[i], 2 * px + py, 1 - c)
                _rcopy(part, part, sems, 3 * i + j, (x, y, 1 - c)).wait_recv()
        for cp in cps:
            cp.wait_send()

    return pl.pallas_call(
        body, name=name, out_shape=[_sds(a.shape, a.dtype) for a in lands], in_specs=[ANY] * n, out_specs=[ANY] * n,
        input_output_aliases={i: i for i in range(n)}, scratch_shapes=_dma_sems(3 * n, 1)[:2])(*lands)


def rs_start(hs, lands, name):
    def copies(s, d, place):
        x, y, c, chips = place
        return [(s[i].at[2 * px + py], d[i].at[2 * x + y], (px, py, c)) for j, (px, py) in enumerate(chips) for i in range(len(s))]

    return _split_start(hs, lands, copies, name)


def rs_wait(handle, after, name):
    def arrivals(s, d, place):
        x, y, c, chips = place
        return [(s[i].at[2 * px + py], d[i].at[2 * px + py]) for j, (px, py) in enumerate(chips) for i in range(len(s))]

    return _split_wait(handle, arrivals, after, name)


def sibling_swap(arrs, name):
    n = len(arrs)
    rh = [a.shape[1] // 2 for a in arrs]

    def body(*refs):
        srcs, outs = refs[:n], refs[n:2 * n]
        send_sems, recv_sems = refs[2 * n:]
        x, y, c, _ = _place()
        cps = [_rcopy(srcs[i].at[:, pl.ds((1 - c) * rh[i], rh[i]), :], outs[i], (send_sems, recv_sems), i, (x, y, 1 - c))
               for i in range(n)]
        for cp in cps:
            cp.start()
        for cp in cps:
            cp.wait()

    return pl.pallas_call(
        body, name=name, out_shape=[_sds((N_CHIPS, r, a.shape[2]), a.dtype) for a, r in zip(arrs, rh)],
        in_specs=[ANY] * n, out_specs=[ANY] * n, scratch_shapes=_dma_sems(n, 1)[:2])(*arrs)


def chip_exchange(hs):
    n = len(hs)

    def body(*refs):
        srcs, outs = refs[:n], refs[n:2 * n]
        send_sems, recv_sems, loc_sems = refs[2 * n:]
        x, y, c, chips = _place()
        sems = (send_sems, recv_sems)
        me = 2 * x + y
        mine = [pltpu.make_async_copy(srcs[i].at[me], outs[i].at[me], loc_sems.at[i]) for i in range(n)]
        for cp in mine:
            cp.start()
        sends = []
        for j, (px, py) in enumerate(chips):
            for i in range(n):
                cp = _rcopy(srcs[i].at[2 * px + py], outs[i].at[me], sems, 3 * i + j, (px, py, c))
                cp.start()
                sends.append(cp)
        for j, (px, py) in enumerate(chips):
            for i in range(n):
                _rcopy(srcs[i].at[2 * px + py], outs[i].at[2 * px + py], sems, 3 * i + j, (px, py, c)).wait_recv()
        for cp in sends:
            cp.wait_send()
        for cp in mine:
            cp.wait()

    return pl.pallas_call(
        body, name="chip_exchange", out_shape=[_sds(h.shape, h.dtype) for h in hs],
        in_specs=[ANY] * n, out_specs=[ANY] * n, scratch_shapes=_dma_sems(3 * n, n))(*hs)


def sum_into(land, base, l, core, name):
    _, rh, C = land.shape
    tr = _row_tile(rh, C, N_CHIPS)
    nr = rh // tr

    def body(core_ref, land_ref, base_ref, o_ref):
        acc = land_ref[0]
        for k in range(1, N_CHIPS):
            acc = acc + land_ref[k]
        o_ref[...] = acc

    grid_spec = pltpu.PrefetchScalarGridSpec(
        num_scalar_prefetch=1, grid=(nr,),
        in_specs=[pl.BlockSpec((N_CHIPS, tr, C), lambda r, core_ref: (0, r, 0)), ANY],
        out_specs=pl.BlockSpec((None, tr, C), lambda r, core_ref: (l, core_ref[0] * nr + r, 0)))
    return pl.pallas_call(body, name=name, grid_spec=grid_spec, out_shape=_sds(base.shape, base.dtype),
                          input_output_aliases={2: 0}, compiler_params=_cparams(VMEM_BIG))(
        core.reshape(1).astype(jnp.int32), land, base)


def sibling_join(bases, name):
    n = len(bases)

    def body(*refs):
        bufs = refs[n:2 * n]
        send_sems, recv_sems = refs[2 * n:]
        x, y, c, _ = _place()
        sems = (send_sems, recv_sems)

        def half(i, hc):
            rh = bufs[i].shape[1] // 2
            return bufs[i].at[:, pl.ds(hc * rh, rh), :]

        sends = [_rcopy(half(i, c), half(i, c), sems, i, (x, y, 1 - c)) for i in range(n)]
        for cp in sends:
            cp.start()
        for i in range(n):
            _rcopy(half(i, 1 - c), half(i, 1 - c), sems, i, (x, y, 1 - c)).wait_recv()
        for cp in sends:
            cp.wait_send()

    return pl.pallas_call(
        body, name=name, out_shape=[_sds(b.shape, b.dtype) for b in bases], in_specs=[ANY] * n, out_specs=[ANY] * n,
        input_output_aliases={i: i for i in range(n)}, scratch_shapes=_dma_sems(n, 1)[:2])(*bases)


def ag_all(blk):
    M, C = blk.shape

    def body(x_ref, out_ref, send_sems, recv_sems, loc_sem):
        x, y, c, chips = _place()
        sems = (send_sems, recv_sems)
        me, sibling = (x, y, c), (x, y, 1 - c)

        def slot(px, py, pc):
            return out_ref.at[4 * px + 2 * py + pc]

        mine = pltpu.make_async_copy(x_ref, slot(*me), loc_sem)
        mine.start()
        first = [_rcopy(x_ref, slot(*me), sems, 0, sibling)]
        first += [_rcopy(x_ref, slot(*me), sems, 1 + j, (*chip, c)) for j, chip in enumerate(chips)]
        for cp in first:
            cp.start()
        passed = [_rcopy(slot(*chip, c), slot(*chip, c), sems, 4 + j, sibling) for j, chip in enumerate(chips)]
        for j, chip in enumerate(chips):
            _rcopy(slot(*chip, c), slot(*chip, c), sems, 1 + j, me).wait_recv()
            passed[j].start()
        _rcopy(slot(*sibling), slot(*sibling), sems, 0, me).wait_recv()
        for j, chip in enumerate(chips):
            _rcopy(slot(*chip, 1 - c), slot(*chip, 1 - c), sems, 4 + j, me).wait_recv()
        for cp in first + passed:
            cp.wait_send()
        mine.wait()

    return pl.pallas_call(
        body, name="ag_all", out_shape=_sds((8, M, C), blk.dtype),
        in_specs=[pl.BlockSpec(memory_space=pltpu.VMEM)], out_specs=pl.BlockSpec(memory_space=pltpu.VMEM),
        scratch_shapes=[pltpu.SemaphoreType.DMA((7,)), pltpu.SemaphoreType.DMA((7,)), pltpu.SemaphoreType.DMA(())],
        compiler_params=_cparams(VMEM_BIG))(blk)


WEIGHTS = ["ada_w", "ada_b", "ln_g", "ln_b", "ffn1_w_in", "ffn1_w_out", "ffn2_w_in", "ffn2_w_out", "mix_w_in", "mix_w_out",
           "hgrn_lb_logits", "hgrn_norm_g", "mla_q_norm_g", "mla_kv_norm_g", "mla_w_uq", "mla_w_ukv", "fox_b_f",
           "gmlp_ln_g", "gmlp_ln_b", "gmlp_w_s", "gmlp_b_s"]
SHARDED = {"ffn1_w_in": 1, "ffn1_w_out": 0, "ffn2_w_in": 1, "ffn2_w_out": 0, "mix_w_in": 1, "mix_w_out": 0,
           "mla_w_uq": 1, "mla_w_ukv": 1}
SMALL = ["hgrn_lb_logits", "hgrn_norm_g", "mla_q_norm_g", "mla_kv_norm_g", "fox_b_f", "gmlp_ln_g", "gmlp_ln_b",
         "gmlp_w_s", "gmlp_b_s", "ln_g", "ln_b"]
GATHERED = ["ada_w", "ffn1_w_in", "ffn1_w_out", "ffn2_w_in", "ffn2_w_out", "mix_w_in", "mix_w_out", "mla_w_uq", "mla_w_ukv"]
REDUCED = GATHERED[1:]


def _col_shards(a):
    cols = a.shape[1] // N_CHIPS
    return jnp.stack([a[:, k * cols:(k + 1) * cols] for k in range(N_CHIPS)])


def add_kept_half(a, got, core, name):
    _, R, C = a.shape
    rh = R // 2
    tr = _row_tile(rh, C)
    nr = rh // tr

    def body(core_ref, a_ref, b_ref, o_ref):
        o_ref[...] = a_ref[...] + b_ref[...]

    half = pl.BlockSpec((None, tr, C), lambda k, r, core_ref: (k, r, 0))
    grid_spec = pltpu.PrefetchScalarGridSpec(
        num_scalar_prefetch=1, grid=(N_CHIPS, nr),
        in_specs=[pl.BlockSpec((None, tr, C), lambda k, r, core_ref: (k, core_ref[0] * nr + r, 0)), half],
        out_specs=half)
    return pl.pallas_call(body, name=name, grid_spec=grid_spec, out_shape=_sds((N_CHIPS, rh, C), F32),
                          compiler_params=_cparams(VMEM_BIG))(core.reshape(1).astype(jnp.int32), a, got)


def _rows(parts, n_rows, dtype):
    flat = jnp.concatenate([p.reshape(-1) for p in parts])
    pad = n_rows * D - flat.shape[0]
    return jnp.concatenate([flat, jnp.zeros((pad,), dtype)]).reshape(n_rows, D)


def _take(flat, shapes):
    out, o = [], 0
    for shp in shapes:
        n = int(np.prod(shp))
        out.append(flat[o:o + n].reshape(shp))
        o += n
    return out


def _round_up(n, m):
    return -(-n // m) * m


def pack_shard(w):
    parts = [w[n][l] for l in range(DEPTH) for n in SHARDED] + [w[n][l] for l in range(DEPTH) for n in ("ln_g", "ln_b")]
    n = sum(int(np.prod(p.shape)) for p in parts)
    return _rows(parts, _round_up(-(-n // D), 16), F32)


def unpack_shard(pk, like):
    shapes = [like[n].shape[1:] for l in range(DEPTH) for n in SHARDED] + [like[n].shape[1:] for l in range(DEPTH) for n in ("ln_g", "ln_b")]
    pieces = _take(pk.reshape(-1), shapes)
    names = [n for l in range(DEPTH) for n in SHARDED] + [n for l in range(DEPTH) for n in ("ln_g", "ln_b")]
    out = {}
    for n in list(SHARDED) + ["ln_g", "ln_b"]:
        out[n] = jnp.stack([p for p, m in zip(pieces, names) if m == n])
    return out


def pack_small(w):
    parts = [w[n][l] for l in range(DEPTH) for n in SMALL]
    n = sum(int(np.prod(p.shape)) for p in parts)
    return _rows(parts, _round_up(-(-n // D), 8), F32)


def unpack_small(pk, like):
    shapes = [like[n].shape[1:] for l in range(DEPTH) for n in SMALL]
    pieces = _take(pk.reshape(-1), shapes)
    names = [n for l in range(DEPTH) for n in SMALL]
    return {n: jnp.stack([p for p, m in zip(pieces, names) if m == n]) for n in SMALL}


def pack_gather(w):
    parts = [w[n][l].astype(BF16) for l in range(DEPTH) for n in ["ada_w"] + list(SHARDED)]
    ln = jnp.concatenate([w[n][l].reshape(-1) for l in range(DEPTH) for n in ("ln_g", "ln_b")])
    parts.append(lax.bitcast_convert_type(ln, BF16))
    n = sum(int(np.prod(p.shape)) for p in parts)
    return _rows(parts, _round_up(-(-n // D), 16), BF16)


def unpack_gather(g, w):
    names = ["ada_w"] + list(SHARDED)
    shapes = [w[n].shape[1:] for l in range(DEPTH) for n in names]
    n_ln = DEPTH * 2 * 3 * (D // N_CHIPS)
    flat = g.reshape(N_CHIPS, -1)
    per_chip = [_take(flat[k], shapes + [(n_ln, 2)]) for k in range(N_CHIPS)]
    layers = [dict() for _ in range(DEPTH)]
    i = 0
    for l in range(DEPTH):
        for n in names:
            axis = 1 if n == "ada_w" else SHARDED[n]
            layers[l][n] = jnp.concatenate([per_chip[k][i] for k in range(N_CHIPS)], axis=axis)
            i += 1
    ln = [lax.bitcast_convert_type(per_chip[k][i], F32).reshape(DEPTH, 2, 3, D // N_CHIPS) for k in range(N_CHIPS)]
    ln = jnp.concatenate(ln, axis=3)
    for l in range(DEPTH):
        layers[l]["ln_g"], layers[l]["ln_b"] = ln[l, 0], ln[l, 1]
    return layers


def pack_grads(grads, k):
    parts = []
    for l in range(DEPTH):
        g = grads[l]
        full = {"ffn1_w_out": g["ffn1_out"], "ffn2_w_out": g["ffn2_out"], "mix_w_in": mix_in_unext(g["mix_in"]),
                "mix_w_out": mix_out_unext(g["mix_out"]), "mla_w_uq": uq_unext(g["wq"]), "mla_w_ukv": ukv_unext(g["wkv"])}
        for n, axis in SHARDED.items():
            if n in ("ffn1_w_in", "ffn2_w_in"):
                half = g[n.replace("_w_in", "_in")][k // 2]
                parts.append(half[:, (k % 2) * (FF // 2):(k % 2 + 1) * (FF // 2)])
            else:
                sz = full[n].shape[axis] // N_CHIPS
                parts.append(lax.slice_in_dim(full[n], k * sz, (k + 1) * sz, axis=axis))
    for l in range(DEPTH):
        for n in ("ln_g", "ln_b"):
            parts.append(grads[l][n][:, k * (D // N_CHIPS):(k + 1) * (D // N_CHIPS)])
    n = sum(int(np.prod(p.shape)) for p in parts)
    return _rows(parts, _round_up(-(-n // D), 16), F32)


def kernel(x, c, ada_w, ada_b, ln_g, ln_b, ffn1_w_in, ffn1_w_out, ffn2_w_in, ffn2_w_out, mix_w_in, mix_w_out, hgrn_lb_logits, hgrn_norm_g, mla_q_norm_g, mla_kv_norm_g, mla_w_uq, mla_w_ukv, fox_b_f, gmlp_ln_g, gmlp_ln_b, gmlp_w_s, gmlp_b_s, loss_target, m_ada_w, m_ada_b, m_ln_g, m_ln_b, m_ffn1_w_in, m_ffn1_w_out, m_ffn2_w_in, m_ffn2_w_out, m_mix_w_in, m_mix_w_out, m_hgrn_lb_logits, m_hgrn_norm_g, m_mla_q_norm_g, m_mla_kv_norm_g, m_mla_w_uq, m_mla_w_ukv, m_fox_b_f, m_gmlp_ln_g, m_gmlp_ln_b, m_gmlp_w_s, m_gmlp_b_s, v_ada_w, v_ada_b, v_ln_g, v_ln_b, v_ffn1_w_in, v_ffn1_w_out, v_ffn2_w_in, v_ffn2_w_out, v_mix_w_in, v_mix_w_out, v_hgrn_lb_logits, v_hgrn_norm_g, v_mla_q_norm_g, v_mla_kv_norm_g, v_mla_w_uq, v_mla_w_ukv, v_fox_b_f, v_gmlp_ln_g, v_gmlp_ln_b, v_gmlp_w_s, v_gmlp_b_s):
    w = dict(zip(WEIGHTS, (ada_w, ada_b, ln_g, ln_b, ffn1_w_in, ffn1_w_out, ffn2_w_in, ffn2_w_out, mix_w_in, mix_w_out, hgrn_lb_logits, hgrn_norm_g, mla_q_norm_g, mla_kv_norm_g, mla_w_uq, mla_w_ukv, fox_b_f, gmlp_ln_g, gmlp_ln_b, gmlp_w_s, gmlp_b_s)))
    m = dict(zip(WEIGHTS, (m_ada_w, m_ada_b, m_ln_g, m_ln_b, m_ffn1_w_in, m_ffn1_w_out, m_ffn2_w_in, m_ffn2_w_out, m_mix_w_in, m_mix_w_out, m_hgrn_lb_logits, m_hgrn_norm_g, m_mla_q_norm_g, m_mla_kv_norm_g, m_mla_w_uq, m_mla_w_ukv, m_fox_b_f, m_gmlp_ln_g, m_gmlp_ln_b, m_gmlp_w_s, m_gmlp_b_s)))
    v = dict(zip(WEIGHTS, (v_ada_w, v_ada_b, v_ln_g, v_ln_b, v_ffn1_w_in, v_ffn1_w_out, v_ffn2_w_in, v_ffn2_w_out, v_mix_w_in, v_mix_w_out, v_hgrn_lb_logits, v_hgrn_norm_g, v_mla_q_norm_g, v_mla_kv_norm_g, v_mla_w_uq, v_mla_w_ukv, v_fox_b_f, v_gmlp_ln_g, v_gmlp_ln_b, v_gmlp_w_s, v_gmlp_b_s)))
    Bl, S, _ = x.shape
    T = Bl * S
    core = lax.axis_index("c")
    chip = 2 * lax.axis_index("x") + lax.axis_index("y")

    def shard(key):
        n, l = key
        if n == "ln":
            return jnp.concatenate([ln_g[l:l + 1], ln_b[l:l + 1], jnp.zeros((1, 2, D // N_CHIPS), F32)], axis=1)
        return w[n][l:l + 1].astype(BF16)

    mixers = ["mix_w_in", "mix_w_out", "mla_w_uq", "mla_w_ukv"]
    groups = [[("ada_w", 0), ("ffn1_w_in", 0), ("ffn1_w_out", 0), ("ln", 0)],
              [(n, 0) for n in mixers + ["ffn2_w_in", "ffn2_w_out"]],
              [(n, 1) for n in GATHERED + ["ln"]]]
    srcs = [shard(k) for k in groups[0]]
    first, token = ag_shards(srcs, [own_slot(s, chip) for s in srcs])
    have = dict(zip(groups[0], first))
    handles = {}
    for gi in (1, 2):
        srcs = [s + token[0, 0].astype(s.dtype) for s in (shard(k) for k in groups[gi])]
        handles[gi] = ag_start(srcs, [own_slot(s, chip) for s in srcs], "ag_start_%d" % gi)
        token = handles[gi][-1]
    c8 = jnp.concatenate([c, jnp.zeros((8 - Bl, D), F32)], axis=0)
    c8 = c8 + token[0, 0]

    def cat_cols(a):
        return jnp.concatenate([a[0, k] for k in range(N_CHIPS)], axis=1)

    def get(l, part, after):
        gi = 2 if l == 1 else (0 if part in ("ada", "ffn1") else 1)
        if gi in handles:
            arrived = ag_forward(ag_wait(handles.pop(gi), after, "ag_wait_%d" % gi), "ag_forward_%d" % gi)
            have.update(zip(groups[gi], arrived))
        if part == "ada":
            return dict(ada_w=have[("ada_w", l)], wl=0, ada_b=ada_b[l][None])
        if part == "ffn1":
            ln_full = jnp.moveaxis(have[("ln", l)][0], 0, 1).reshape(8, D)
            return dict(ffn1_in=have[("ffn1_w_in", l)], ffn1_out=have[("ffn1_w_out", l)], wl=0,
                        ln_g=ln_full[0:3], ln_b=ln_full[3:6])
        if part == "ffn2":
            return dict(ffn2_in=have[("ffn2_w_in", l)], ffn2_out=have[("ffn2_w_out", l)])
        return dict(
            mix_in=mix_in_ext(cat_cols(have[("mix_w_in", l)])), mix_out=mix_out_ext(have[("mix_w_out", l)].reshape(D, D)),
            wq=uq_ext(cat_cols(have[("mla_w_uq", l)])).astype(F32), wkv=ukv_ext(cat_cols(have[("mla_w_ukv", l)])).astype(F32),
            ng=hgrn_norm_g[l][None], gq=mla_q_norm_g[l][None], gkv=mla_kv_norm_g[l][None],
            bcol=jnp.concatenate([fox_b_f[l], jnp.zeros((8 - HEADS,), F32)])[:, None],
            g_lg=gmlp_ln_g[l][None], g_lb=gmlp_ln_b[l][None], ws=gmlp_w_s[l], bs=gmlp_b_s[l][:, :, None])

    pending = []

    def emit(l, part, g):
        if part == "mix":
            names = mixers
            by_chip = [_col_shards(mix_in_unext(g["mix_in"])), mix_out_unext(g["mix_out"]).reshape(N_CHIPS, D // N_CHIPS, D),
                       _col_shards(uq_unext(g["wq"])), _col_shards(ukv_unext(g["wkv"]))]
        else:
            names = [part + "_w_in", part + "_w_out"]
            by_chip = [g[part + "_in"], g[part + "_out"]]
        tag = "%d_%s" % (l, part)
        got = sibling_swap(by_chip, "sibling_swap_" + tag)
        chip_sum = [add_kept_half(a, r, core, "add_sibling") for a, r in zip(by_chip, got)]
        slot = lax.broadcasted_iota(jnp.int32, (N_CHIPS, 1, 1), 0)
        lands = [jnp.where(slot == chip, h, 0.0) for h in chip_sum]
        handle = rs_start(chip_sum, lands, "rs_start_" + tag)
        pending.append((l, names, handle, tag))
        return handle[-1][0, 0]

    loss_tile, dx, dmods, grads, d_logits = local_step(
        x.reshape(T, D), c8, loss_target.reshape(T, D), get, hgrn_lb_logits, S, emit)
    loss = lax.psum(loss_tile[0, 0], ("x", "y", "c"))

    small_g = {"hgrn_lb_logits": d_logits,
               "hgrn_norm_g": jnp.stack([grads[l]["ng"][0] for l in range(DEPTH)]),
               "mla_q_norm_g": jnp.stack([grads[l]["gq"][0] for l in range(DEPTH)]),
               "mla_kv_norm_g": jnp.stack([grads[l]["gkv"][0] for l in range(DEPTH)]),
               "fox_b_f": jnp.stack([grads[l]["bcol"][0:HEADS, 0] for l in range(DEPTH)]),
               "gmlp_ln_g": jnp.stack([grads[l]["g_lg"][0] for l in range(DEPTH)]),
               "gmlp_ln_b": jnp.stack([grads[l]["g_lb"][0] for l in range(DEPTH)]),
               "gmlp_w_s": jnp.stack([grads[l]["ws"] for l in range(DEPTH)]),
               "gmlp_b_s": jnp.stack([grads[l]["bs"][:, :, 0] for l in range(DEPTH)])}
    small_g["ln_g"] = jnp.stack([grads[l]["ln_g"] for l in range(DEPTH)])
    small_g["ln_b"] = jnp.stack([grads[l]["ln_b"] for l in range(DEPTH)])
    pk_small = pack_small(small_g)
    n_small = pk_small.shape[0]
    extras = [dmods[l] for l in range(DEPTH)] + [c]
    n_extra = _round_up(-(-sum(int(np.prod(e.shape)) for e in extras) // D), 8)
    gathered = ag_all(jnp.concatenate([pk_small, _rows(extras, n_extra, F32)], axis=0))
    g_small = unpack_small(sum_slots(gathered[:, 0:n_small], 8, "sum_small"), small_g)
    ext = gathered[:, n_small:].reshape(8, -1)
    n_dmod = DEPTH * Bl * N_MOD * D
    dmod_all = ext[:, 0:n_dmod].reshape(8, DEPTH, Bl, N_MOD * D)
    c_all = ext[:, n_dmod:n_dmod + Bl * D].reshape(8 * Bl, D)
    g_ada_w, g_ada_b = [], []
    ncol = N_MOD * D // N_CHIPS
    for l in range(DEPTH):
        dm = dmod_all[:, l].reshape(8 * Bl, N_MOD * D)
        g_ada_w.append(ada_grad(c_all, lax.dynamic_slice_in_dim(dm, chip * ncol, ncol, axis=1)))
        g_ada_b.append(sum_slots(dm.reshape(8 * Bl, N_MOD, D), 8 * Bl, "sum_ada_b").reshape(N_MOD * D))
    g_ada_w, g_ada_b = jnp.stack(g_ada_w), jnp.stack(g_ada_b)

    red = {n: jnp.zeros(w[n].shape, F32) for n in REDUCED}

    def arrive(entry, after):
        l, names, handle, tag = entry
        for n, land in zip(names, rs_wait(handle, after, "rs_wait_" + tag)):
            red[n] = sum_into(land, red[n], l, core, "sum_chips")

    for entry in pending[:-1]:
        arrive(entry, dx)
    late = pending[-1][1]
    early = [n for n in REDUCED if n not in late]
    grad = dict(zip(early, sibling_join([red[n] for n in early], "sibling_join_a")))
    grad.update(g_small)
    grad["ada_w"], grad["ada_b"] = g_ada_w, g_ada_b
    for n in ("ln_g", "ln_b"):
        grad[n] = lax.dynamic_slice_in_dim(g_small[n], chip * (D // N_CHIPS), D // N_CHIPS, axis=2)
    out = {"grad": grad, "delta": {}, "new_m": {}, "new_v": {}}

    def update(n):
        shp = w[n].shape
        two_d = (-1, shp[-1])
        res = adamw(w[n].reshape(two_d), grad[n].reshape(two_d), m[n].reshape(two_d), v[n].reshape(two_d), "adamw_" + n)
        grad[n] = grad[n].reshape(shp)
        for key, r in zip(("delta", "new_m", "new_v"), res):
            out[key][n] = r.reshape(shp)

    for n in WEIGHTS:
        if n not in late:
            update(n)
    arrive(pending[-1], out["delta"]["ffn2_w_in"])
    grad.update(zip(late, sibling_join([red[n] for n in late], "sibling_join_b")))
    for n in late:
        update(n)
    outs = [loss, dx.reshape(Bl, S, D)]
    for key in ("grad", "delta", "new_m", "new_v"):
        outs += [out[key][n] for n in WEIGHTS]
    return tuple(outs)
```

```python
import functools

import jax
import jax.numpy as jnp
import numpy as np
from jax import lax
from jax.experimental import pallas as pl
from jax.experimental.pallas import tpu as pltpu

F32, BF16 = jnp.float32, jnp.bfloat16
MESH = pl.DeviceIdType.MESH

N_CHIPS = 4
D = 1024
DEPTH = 2
FF = 2816
N_MOD = 9
GW = 256
HEADS = 4
HD = 64
HP = 128
A_CHUNK = 16
LB_FLOOR = 1e-30
B_Q_LORA, B_KV_LORA, B_NOPE, B_ROPE = 256, 128, 64, 32
ROPE_THETA = 10000.0
D_CHUNK = 128
MIX_COLS = 2724
ALPHA = (2 * DEPTH) ** 0.25
LN_EPS = 1e-5
RMS_EPS = 1e-6
ADAM_LR, ADAM_B1, ADAM_B2, ADAM_EPS, ADAM_WD, ADAM_STEP = 0.001, 0.9, 0.999, 1e-08, 0.01, 10

NP = 3840
NP_TILE = 1920
C_A, C_B, C_CQ, C_CK, C_CV, C_D, C_CF = 0, 1024, 1536, 2048, 2560, 3072, 3584
NCAT = 1536
O_A, O_B, O_C, O_D = 0, 256, 768, 1280

VMEM_BIG = 48 << 20


def _cparams(vmem=None):
    return pltpu.CompilerParams(vmem_limit_bytes=vmem) if vmem else pltpu.CompilerParams()


def _sds(shape, dtype):
    return jax.ShapeDtypeStruct(tuple(shape), dtype)


@jax.custom_vjp
def _mm(a, w):
    return jnp.dot(a.astype(BF16), w.astype(BF16), preferred_element_type=F32)


def _mm_f(a, w):
    return _mm(a, w), (a, w)


def _mm_b(res, g):
    a, w = res
    gb = g.astype(BF16)
    da = lax.dot_general(gb, w.astype(BF16), (((1,), (1,)), ((), ())), preferred_element_type=F32)
    dw = lax.dot_general(a.astype(BF16), gb, (((0,), (0,)), ((), ())), preferred_element_type=F32)
    return da.astype(a.dtype), dw.astype(w.dtype)


_mm.defvjp(_mm_f, _mm_b)


@jax.custom_vjp
def _mm_nt(a, b):
    return lax.dot_general(a.astype(BF16), b.astype(BF16), (((1,), (1,)), ((), ())), preferred_element_type=F32)


def _mm_nt_f(a, b):
    return _mm_nt(a, b), (a, b)


def _mm_nt_b(res, g):
    a, b = res
    gb = g.astype(BF16)
    da = jnp.dot(gb, b.astype(BF16), preferred_element_type=F32)
    db = lax.dot_general(gb, a.astype(BF16), (((0,), (0,)), ((), ())), preferred_element_type=F32)
    return da.astype(a.dtype), db.astype(b.dtype)


_mm_nt.defvjp(_mm_nt_f, _mm_nt_b)


@jax.custom_vjp
def _mm_tn(a, b):
    return lax.dot_general(a.astype(BF16), b.astype(BF16), (((0,), (0,)), ((), ())), preferred_element_type=F32)


def _mm_tn_f(a, b):
    return _mm_tn(a, b), (a, b)


def _mm_tn_b(res, g):
    a, b = res
    gb = g.astype(BF16)
    da = lax.dot_general(b.astype(BF16), gb, (((1,), (1,)), ((), ())), preferred_element_type=F32)
    db = jnp.dot(a.astype(BF16), gb, preferred_element_type=F32)
    return da.astype(a.dtype), db.astype(b.dtype)


_mm_tn.defvjp(_mm_tn_f, _mm_tn_b)


def _mm_hi(a, w):
    return jnp.dot(a, w, precision=lax.Precision.HIGHEST, preferred_element_type=F32)


def _iota(shape, dim):
    return lax.broadcasted_iota(jnp.int32, shape, dim)


def _head_sum_mats():
    e = (_iota((GW, HP), 0) // HD == _iota((GW, HP), 1)).astype(F32)
    et = (_iota((HP, GW), 1) // HD == _iota((HP, GW), 0)).astype(F32)
    return e, et


def _modulate(x, mod_ref, sh, sc):
    return x * (1.0 + mod_ref[sc:sc + 1, :]) + mod_ref[sh:sh + 1, :]


def _ln_res(x, y, gate, lg, lb, gs):
    r = ALPHA * x + gs * (1.0 + gate) * y
    mu = jnp.mean(r, axis=-1, keepdims=True)
    var = jnp.mean(jnp.square(r - mu), axis=-1, keepdims=True)
    return (r - mu) * lax.rsqrt(var + LN_EPS) * lg + lb


def _rms(x, g):
    return x * lax.rsqrt(jnp.mean(x * x, axis=-1, keepdims=True) + RMS_EPS) * g


def _tile(n, pref):
    return pref if n % pref == 0 else n


def mod_fwd(c8, w, l, b):
    tn = w.shape[3]
    n = N_CHIPS * tn

    def body(c_ref, w_ref, b_ref, o_ref):
        h = jax.nn.silu(c_ref[...]).astype(BF16)
        o_ref[...] = jnp.dot(h, w_ref[...], preferred_element_type=F32) + b_ref[...]

    return pl.pallas_call(
        body, name="mod_fwd", grid=(N_CHIPS,),
        in_specs=[pl.BlockSpec((8, D), lambda j: (0, 0)), pl.BlockSpec((None, None, D, tn), lambda j: (l, j, 0, 0)),
                  pl.BlockSpec((1, tn), lambda j: (0, j))],
        out_specs=pl.BlockSpec((8, tn), lambda j: (0, j)), out_shape=_sds((8, n), F32),
        compiler_params=_cparams(VMEM_BIG))(c8, w, b)


def ffn_in_fwd(x, mod, w_in, l, sh, sc, S):
    T = x.shape[0]
    tm, tn = _tile(S, 512), FF // 2
    tpb, nj = S // tm, 2

    def body(x_ref, mod_ref, wg_ref, wu_ref, zg_ref, zu_ref, act_ref, h_ref):
        @pl.when(pl.program_id(1) == 0)
        def _():
            h_ref[...] = _modulate(x_ref[...], mod_ref, sh, sc).astype(BF16)
        g = jnp.dot(h_ref[...], wg_ref[...], preferred_element_type=F32)
        u = jnp.dot(h_ref[...], wu_ref[...], preferred_element_type=F32)
        zg_ref[...] = g
        zu_ref[...] = u
        act_ref[...] = (jax.nn.silu(g) * u).astype(BF16)

    return pl.pallas_call(
        body, name="ffn_in_fwd", grid=(T // tm, nj),
        in_specs=[pl.BlockSpec((tm, D), lambda i, j: (i, 0)),
                  pl.BlockSpec((None, N_MOD, D), lambda i, j: (i // tpb, 0, 0)),
                  pl.BlockSpec((None, None, D, tn), lambda i, j: (l, j, 0, 0)),
                  pl.BlockSpec((None, None, D, tn), lambda i, j: (l, j + nj, 0, 0))],
        out_specs=[pl.BlockSpec((tm, tn), lambda i, j: (i, j))] * 3,
        out_shape=[_sds((T, FF), F32), _sds((T, FF), F32), _sds((T, FF), BF16)],
        scratch_shapes=[pltpu.VMEM((tm, D), BF16)],
        compiler_params=_cparams(VMEM_BIG))(x, mod, w_in, w_in)


def mix_in_fwd(x, mod, w, sh, sc, S):
    T = x.shape[0]
    n = w.shape[1]
    tm, tn = _tile(S, 512), NP_TILE
    tpb = S // tm

    def body(x_ref, mod_ref, w_ref, o_ref, h_ref):
        @pl.when(pl.program_id(1) == 0)
        def _():
            h_ref[...] = _modulate(x_ref[...], mod_ref, sh, sc).astype(BF16)
        o_ref[...] = jnp.dot(h_ref[...], w_ref[...], preferred_element_type=F32)

    return pl.pallas_call(
        body, name="mix_in_fwd", grid=(T // tm, n // tn),
        in_specs=[pl.BlockSpec((tm, D), lambda i, j: (i, 0)),
                  pl.BlockSpec((None, N_MOD, D), lambda i, j: (i // tpb, 0, 0)),
                  pl.BlockSpec((D, tn), lambda i, j: (0, j))],
        out_specs=pl.BlockSpec((tm, tn), lambda i, j: (i, j)), out_shape=_sds((T, n), F32),
        scratch_shapes=[pltpu.VMEM((tm, D), BF16)],
        compiler_params=_cparams(VMEM_BIG))(x, mod, w)


def out_ln_fwd(act, w_out, x, mod, lg, lb, gate, gs, S, l=None):
    T, K = act.shape
    tm = _tile(S, 512)
    tpb = S // tm

    def body(a_ref, w_ref, x_ref, mod_ref, lg_ref, lb_ref, y_ref, xn_ref):
        y = jnp.dot(a_ref[...], w_ref[...].reshape(K, D), preferred_element_type=F32)
        y_ref[...] = y
        xn_ref[...] = _ln_res(x_ref[...], y, mod_ref[gate:gate + 1, :], lg_ref[...], lb_ref[...], gs)

    if l is None:
        w_spec = pl.BlockSpec((K, D), lambda i: (0, 0))
    else:
        w_spec = pl.BlockSpec((None, N_CHIPS, K // N_CHIPS, D), lambda i: (l, 0, 0, 0))
    return pl.pallas_call(
        body, name="out_ln_fwd", grid=(T // tm,),
        in_specs=[pl.BlockSpec((tm, K), lambda i: (i, 0)), w_spec,
                  pl.BlockSpec((tm, D), lambda i: (i, 0)),
                  pl.BlockSpec((None, N_MOD, D), lambda i: (i // tpb, 0, 0)),
                  pl.BlockSpec((1, D), lambda i: (0, 0)), pl.BlockSpec((1, D), lambda i: (0, 0))],
        out_specs=[pl.BlockSpec((tm, D), lambda i: (i, 0))] * 2,
        out_shape=[_sds((T, D), F32), _sds((T, D), F32)],
        compiler_params=_cparams(VMEM_BIG))(act, w_out, x, mod, lg, lb)


def ln_res_bwd(dxn, x, y, mod, lg, lb, gate, gs, S):
    T = x.shape[0]
    B = T // S
    tm = _tile(S, 512)
    tpb = S // tm

    def body(d_ref, x_ref, y_ref, mod_ref, lg_ref, lb_ref, dx_ref, dy_ref, dg_ref, dlg_ref, dlb_ref):
        i = pl.program_id(0)
        f = functools.partial(_ln_res, gs=gs)
        _, vjp = jax.vjp(f, x_ref[...], y_ref[...], mod_ref[gate:gate + 1, :], lg_ref[...], lb_ref[...])
        dx, dy, dg, dlg, dlb = vjp(d_ref[...])
        dx_ref[...] = dx
        dy_ref[...] = dy.astype(BF16)

        @pl.when(i % tpb == 0)
        def _():
            dg_ref[...] = jnp.zeros_like(dg_ref)

        @pl.when(i == 0)
        def _():
            dlg_ref[...] = jnp.zeros_like(dlg_ref)
            dlb_ref[...] = jnp.zeros_like(dlb_ref)

        dg_ref[...] += dg
        dlg_ref[...] += dlg
        dlb_ref[...] += dlb

    tok = pl.BlockSpec((tm, D), lambda i: (i, 0))
    vec = pl.BlockSpec((1, D), lambda i: (0, 0))
    return pl.pallas_call(
        body, name="ln_res_bwd", grid=(T // tm,),
        in_specs=[tok, tok, tok, pl.BlockSpec((None, N_MOD, D), lambda i: (i // tpb, 0, 0)), vec, vec],
        out_specs=[tok, tok, pl.BlockSpec((None, 1, D), lambda i: (i // tpb, 0, 0)), vec, vec],
        out_shape=[_sds((T, D), F32), _sds((T, D), BF16), _sds((B, 1, D), F32), _sds((1, D), F32), _sds((1, D), F32)],
        compiler_params=_cparams(VMEM_BIG))(dxn, x, y, mod, lg, lb)


def swiglu_bwd(dy, w_out, l, zg, zu, S):
    T = dy.shape[0]
    tm, tn = _tile(S, 512), FF // 2

    def body(dy_ref, w_ref, zg_ref, zu_ref, dg_ref, du_ref):
        da = lax.dot_general(dy_ref[...], w_ref[...].reshape(tn, D), (((1,), (1,)), ((), ())), preferred_element_type=F32)
        g, u = zg_ref[...], zu_ref[...]
        sg = jax.nn.sigmoid(g)
        dg_ref[...] = (da * u * (sg * (1.0 + g * (1.0 - sg)))).astype(BF16)
        du_ref[...] = (da * (g * sg)).astype(BF16)

    zt = pl.BlockSpec((tm, tn), lambda i, j: (i, j))
    return pl.pallas_call(
        body, name="swiglu_bwd", grid=(T // tm, FF // tn),
        in_specs=[pl.BlockSpec((tm, D), lambda i, j: (i, 0)),
                  pl.BlockSpec((None, 2, FF // N_CHIPS, D), lambda i, j: (l, j, 0, 0)), zt, zt],
        out_specs=[zt, zt], out_shape=[_sds((T, FF), BF16), _sds((T, FF), BF16)],
        compiler_params=_cparams(VMEM_BIG))(dy, w_out, zg, zu)


def nt_plain(dy, w):
    T = dy.shape[0]
    K = w.shape[0]
    tm = _tile(T, 512)

    def body(dy_ref, w_ref, o_ref):
        o_ref[...] = lax.dot_general(dy_ref[...], w_ref[...], (((1,), (1,)), ((), ())), preferred_element_type=F32)

    return pl.pallas_call(
        body, name="nt_plain", grid=(T // tm,),
        in_specs=[pl.BlockSpec((tm, D), lambda i: (i, 0)), pl.BlockSpec((K, D), lambda i: (0, 0))],
        out_specs=pl.BlockSpec((tm, K), lambda i: (i, 0)), out_shape=_sds((T, K), F32),
        compiler_params=_cparams(VMEM_BIG))(dy, w)


def tn_mm(a, b, tk):
    T, K = a.shape
    N = b.shape[1]
    tt = _tile(T, 512)

    def body(a_ref, b_ref, o_ref):
        @pl.when(pl.program_id(1) == 0)
        def _():
            o_ref[...] = jnp.zeros_like(o_ref)
        o_ref[...] += lax.dot_general(a_ref[...], b_ref[...], (((0,), (0,)), ((), ())), preferred_element_type=F32)

    return pl.pallas_call(
        body, name="tn_mm", grid=(K // tk, T // tt),
        in_specs=[pl.BlockSpec((tt, tk), lambda k, t: (t, k)), pl.BlockSpec((tt, N), lambda k, t: (t, 0))],
        out_specs=pl.BlockSpec((tk, N), lambda k, t: (k, 0)), out_shape=_sds((K, N), F32),
        compiler_params=_cparams(VMEM_BIG))(a, b)


def tn_mm_mod(x, mod, b, sh, sc, S, tn):
    T = x.shape[0]
    N = b.shape[1]
    tt = _tile(S, 512)
    tpb = S // tt

    def body(x_ref, mod_ref, b_ref, o_ref):
        @pl.when(pl.program_id(1) == 0)
        def _():
            o_ref[...] = jnp.zeros_like(o_ref)
        h = _modulate(x_ref[...], mod_ref, sh, sc).astype(BF16)
        o_ref[...] += lax.dot_general(h, b_ref[...], (((0,), (0,)), ((), ())), preferred_element_type=F32)

    return pl.pallas_call(
        body, name="tn_mm_mod", grid=(N // tn, T // tt),
        in_specs=[pl.BlockSpec((tt, D), lambda j, t: (t, 0)),
                  pl.BlockSpec((None, N_MOD, D), lambda j, t: (t // tpb, 0, 0)),
                  pl.BlockSpec((tt, tn), lambda j, t: (t, j))],
        out_specs=pl.BlockSpec((D, tn), lambda j, t: (0, j)), out_shape=_sds((D, N), F32),
        compiler_params=_cparams(VMEM_BIG))(x, mod, b)


def tn_mm_mod_shards(x, mod, bg, bu, sh, sc, S):
    T = x.shape[0]
    tn = FF // 2
    tt = _tile(S, 512)
    tpb = S // tt

    def body(x_ref, mod_ref, bg_ref, bu_ref, o_ref):
        j = pl.program_id(0)

        @pl.when(pl.program_id(1) == 0)
        def _():
            o_ref[...] = jnp.zeros_like(o_ref)
        h = _modulate(x_ref[...], mod_ref, sh, sc).astype(BF16)

        @pl.when(j < 2)
        def _():
            o_ref[...] += lax.dot_general(h, bg_ref[...], (((0,), (0,)), ((), ())), preferred_element_type=F32)

        @pl.when(j >= 2)
        def _():
            o_ref[...] += lax.dot_general(h, bu_ref[...], (((0,), (0,)), ((), ())), preferred_element_type=F32)

    return pl.pallas_call(
        body, name="tn_mm_mod_shards", grid=(N_CHIPS, T // tt),
        in_specs=[pl.BlockSpec((tt, D), lambda j, t: (t, 0)),
                  pl.BlockSpec((None, N_MOD, D), lambda j, t: (t // tpb, 0, 0)),
                  pl.BlockSpec((tt, tn), lambda j, t: (t, jnp.minimum(j, 1))),
                  pl.BlockSpec((tt, tn), lambda j, t: (t, jnp.maximum(j - 2, 0)))],
        out_specs=pl.BlockSpec((None, D, tn), lambda j, t: (j, 0, 0)), out_shape=_sds((N_CHIPS, D, tn), F32),
        compiler_params=_cparams(VMEM_BIG))(x, mod, bg, bu)


def nt_mod_bwd(ds, w, offs, x, mod, dres, sc, S, tk, l=None):
    T = x.shape[0]
    B = T // S
    tm = _tile(S, 512)
    tpb = S // tm
    Kd = ds[0].shape[1]
    nk = Kd // tk
    n_in = len(ds)

    def body(*refs):
        d_refs, w_refs = refs[:n_in], refs[n_in:2 * n_in]
        x_ref, mod_ref, r_ref, dx_ref, dsh_ref, dsc_ref, acc = refs[2 * n_in:]
        i, k = pl.program_id(0), pl.program_id(1)

        @pl.when(k == 0)
        def _():
            acc[...] = jnp.zeros_like(acc)

        for d_ref, w_ref in zip(d_refs, w_refs):
            acc[...] += lax.dot_general(d_ref[...], w_ref[...], (((1,), (1,)), ((), ())), preferred_element_type=F32)

        @pl.when(k == nk - 1)
        def _():
            dh = acc[...]
            dx_ref[...] = dh * (1.0 + mod_ref[sc:sc + 1, :]) + r_ref[...]

            @pl.when(i % tpb == 0)
            def _():
                dsh_ref[...] = jnp.zeros_like(dsh_ref)
                dsc_ref[...] = jnp.zeros_like(dsc_ref)

            dsh_ref[...] += jnp.sum(dh, axis=0, keepdims=True)
            dsc_ref[...] += jnp.sum(dh * x_ref[...], axis=0, keepdims=True)

    tok = pl.BlockSpec((tm, D), lambda i, k: (i, 0))
    vec = pl.BlockSpec((None, 1, D), lambda i, k: (i // tpb, 0, 0))
    in_specs = [pl.BlockSpec((tm, tk), lambda i, k: (i, k)) for _ in ds]
    if l is None:
        in_specs += [pl.BlockSpec((D, tk), functools.partial(lambda i, k, o: (0, k + o), o=off // tk)) for off in offs]
    else:
        in_specs += [pl.BlockSpec((None, None, D, tk), functools.partial(lambda i, k, o: (l, k + o, 0, 0), o=off)) for off in offs]
    in_specs += [tok, pl.BlockSpec((None, N_MOD, D), lambda i, k: (i // tpb, 0, 0)), tok]
    return pl.pallas_call(
        body, name="nt_mod_bwd", grid=(T // tm, nk), in_specs=in_specs,
        out_specs=[tok, vec, vec],
        out_shape=[_sds((T, D), F32), _sds((B, 1, D), F32), _sds((B, 1, D), F32)],
        scratch_shapes=[pltpu.VMEM((tm, D), F32)],
        compiler_params=_cparams(VMEM_BIG))(*ds, *([w] * n_in), x, mod, dres)


def loss_head(y, tgt):
    T = y.shape[0]
    tm = _tile(T, 512)

    def body(y_ref, t_ref, l_ref, d_ref):
        @pl.when(pl.program_id(0) == 0)
        def _():
            l_ref[...] = jnp.zeros_like(l_ref)
        e = y_ref[...] - t_ref[...]
        d_ref[...] = e * (1.0 / D)
        l_ref[...] += 0.5 * jnp.sum(jnp.sum(e * e, axis=1, keepdims=True) * (1.0 / D))

    tok = pl.BlockSpec((tm, D), lambda i: (i, 0))
    return pl.pallas_call(
        body, name="loss_head", grid=(T // tm,), in_specs=[tok, tok],
        out_specs=[pl.BlockSpec((8, 128), lambda i: (0, 0)), tok],
        out_shape=[_sds((8, 128), F32), _sds((T, D), F32)],
        compiler_params=_cparams(VMEM_BIG))(y, tgt)


def _hgrn_block(q, fz, inp, go, st, lb, ng, blk):
    nc = blk // A_CHUNK
    lb_eff = jnp.maximum(lb, LB_FLOOR)
    log_f = jnp.logaddexp(jnp.log(lb_eff), jnp.log1p(-lb) + jax.nn.log_sigmoid(fz))
    k = (1.0 - lb) * jax.nn.sigmoid(-fz) - (lb_eff - lb)
    qf = jax.nn.silu(q)
    same_chunk = _iota((blk, blk), 0) // A_CHUNK == _iota((blk, blk), 1) // A_CHUNK
    tril = (same_chunk & (_iota((blk, blk), 1) <= _iota((blk, blk), 0))).astype(F32)
    G = _mm_hi(tril, log_f)
    e_mat, et_mat = _head_sum_mats()
    G4, q4, k4, v4 = (z.reshape(nc, A_CHUNK, GW) for z in (G, qf, k, inp))
    shp = (nc, A_CHUNK, A_CHUNK, GW)
    causal = _iota(shp, 2) <= _iota(shp, 1)
    decay = jnp.exp(jnp.where(causal, G4[:, :, None, :] - G4[:, None, :, :], -jnp.inf))
    prod = q4[:, :, None, :] * k4[:, None, :, :] * decay
    scores = _mm(prod.reshape(nc * A_CHUNK * A_CHUNK, GW), e_mat.astype(BF16))
    spread = _mm(scores, et_mat.astype(BF16)).reshape(shp)
    o_intra = jnp.sum(spread * v4[:, None, :, :], axis=2).reshape(blk, GW)
    head_diag = (_iota((GW, GW), 0) // HD == _iota((GW, GW), 1) // HD).astype(F32)
    g_last = [jnp.sum(log_f[c * A_CHUNK:(c + 1) * A_CHUNK], axis=0, keepdims=True) for c in range(nc)]
    g_last_b = jnp.concatenate([jnp.broadcast_to(g, (A_CHUNK, GW)) for g in g_last], axis=0)
    q_dec = qf * jnp.exp(G)
    k_end = k * jnp.exp(g_last_b - G)
    outs = []
    for c in range(nc):
        rows = slice(c * A_CHUNK, (c + 1) * A_CHUNK)
        outs.append(_mm_nt(q_dec[rows], st))
        st = st * jnp.exp(g_last[c]) + _mm_tn(inp[rows], k_end[rows]) * head_diag
    o = o_intra + jnp.concatenate(outs, axis=0)
    ms = _mm_hi(o * o, e_mat) * (1.0 / HD)
    o = o * _mm_hi(lax.rsqrt(ms + RMS_EPS), et_mat) * ng
    return o * jax.nn.silu(go), st


HGRN_BLK = 128


def hgrn_fwd(proj, lb, ng, S):
    T = proj.shape[0]
    B = T // S
    blk = min(HGRN_BLK, S)
    nb = S // blk

    def body(p_ref, lb_ref, ng_ref, o_ref, st_out_ref, st_ref):
        @pl.when(pl.program_id(1) == 0)
        def _():
            st_ref[...] = jnp.zeros_like(st_ref)
        st_out_ref[...] = st_ref[...]
        p = p_ref[...]
        o, st = _hgrn_block(p[:, 0:GW], p[:, GW:2 * GW], p[:, 2 * GW:3 * GW], p[:, 3 * GW:4 * GW],
                            st_ref[...], lb_ref[...], ng_ref[...], blk)
        o_ref[...] = o.astype(BF16)
        st_ref[...] = st

    vec = pl.BlockSpec((1, GW), lambda b, j: (0, 0))
    return pl.pallas_call(
        body, name="hgrn_fwd", grid=(B, nb),
        in_specs=[pl.BlockSpec((blk, 4 * GW), lambda b, j: (b * nb + j, C_A // (4 * GW))), vec, vec],
        out_specs=[pl.BlockSpec((blk, GW), lambda b, j: (b * nb + j, 0)),
                   pl.BlockSpec((None, GW, GW), lambda b, j: (b * nb + j, 0, 0))],
        out_shape=[_sds((T, GW), BF16), _sds((B * nb, GW, GW), F32)],
        scratch_shapes=[pltpu.VMEM((GW, GW), F32)],
        compiler_params=_cparams(VMEM_BIG))(proj, lb, ng)


def hgrn_bwd(proj, states, dcat, lb, ng, S):
    T = proj.shape[0]
    B = T // S
    blk = min(HGRN_BLK, S)
    nb = S // blk

    def body(p_ref, st_in_ref, do_ref, lb_ref, ng_ref, dp_ref, dlb_ref, dng_ref, dst_ref):
        b, j = pl.program_id(0), pl.program_id(1)

        @pl.when(j == 0)
        def _():
            dst_ref[...] = jnp.zeros_like(dst_ref)

        @pl.when((b == 0) & (j == 0))
        def _():
            dlb_ref[...] = jnp.zeros_like(dlb_ref)
            dng_ref[...] = jnp.zeros_like(dng_ref)

        p = p_ref[...]
        f = functools.partial(_hgrn_block, blk=blk)
        _, vjp = jax.vjp(f, p[:, 0:GW], p[:, GW:2 * GW], p[:, 2 * GW:3 * GW], p[:, 3 * GW:4 * GW],
                         st_in_ref[...], lb_ref[...], ng_ref[...])
        dq, df, di, dg, dst, dlb, dng = vjp((do_ref[...], dst_ref[...]))
        dp_ref[...] = jnp.concatenate([dq, df, di, dg], axis=1).astype(BF16)
        dst_ref[...] = dst
        dlb_ref[...] += dlb
        dng_ref[...] += dng

    def rev(b, j):
        return b * nb + (nb - 1 - j)

    vec = pl.BlockSpec((1, GW), lambda b, j: (0, 0))
    return pl.pallas_call(
        body, name="hgrn_bwd", grid=(B, nb),
        in_specs=[pl.BlockSpec((blk, 4 * GW), lambda b, j: (rev(b, j), C_A // (4 * GW))),
                  pl.BlockSpec((None, GW, GW), lambda b, j: (rev(b, j), 0, 0)),
                  pl.BlockSpec((blk, GW), lambda b, j: (rev(b, j), O_A // GW)), vec, vec],
        out_specs=[pl.BlockSpec((blk, 4 * GW), lambda b, j: (rev(b, j), 0)), vec, vec],
        out_shape=[_sds((T, 4 * GW), BF16), _sds((1, GW), F32), _sds((1, GW), F32)],
        scratch_shapes=[pltpu.VMEM((GW, GW), F32)],
        compiler_params=_cparams(VMEM_BIG))(proj, states, dcat, lb, ng)


ATT_TQ = 256


ATT_BANDS = 4


def _attn_block(q, k, v, cum, qpos0, scale, use_cum):
    s = _mm_nt(q, k) * scale
    if use_cum:
        s = s - cum
    qpos = qpos0 + _iota(s.shape, 0)
    s = jnp.where(_iota(s.shape, 1) <= qpos, s, -jnp.inf)
    e = jnp.exp(s - jnp.max(s, axis=-1, keepdims=True))
    p = e / jnp.sum(e, axis=-1, keepdims=True)
    return _mm(p, v)


def _bands(S, tq):
    nq = S // tq
    nb = min(ATT_BANDS, nq)
    per = nq // nb
    return [(r * per, (r + 1) * per, (r + 1) * per * tq) for r in range(nb)]


def attn_fwd(qa, qo, ka, ko, va, vo, cum, scale, S):
    T = qa.shape[0]
    B = T // S
    tq = min(ATT_TQ, S)
    nq = S // tq
    use_cum = cum is not None

    def body(*refs):
        if use_cum:
            q_ref, k_ref, v_ref, c_ref, o_ref = refs
        else:
            (q_ref, k_ref, v_ref, o_ref), c_ref = refs, None
        h, i = pl.program_id(1), pl.program_id(2)
        for lo, hi, kw in _bands(S, tq):
            @pl.when((i >= lo) & (i < hi))
            def _():
                crow = c_ref[pl.ds(h, 1), 0:kw] if use_cum else None
                o = _attn_block(q_ref[...], k_ref[0:kw, :], v_ref[0:kw, :], crow, i * tq, scale, use_cum)
                o_ref[...] = o.astype(BF16)

    in_specs = [pl.BlockSpec((tq, HP), lambda b, h, i: (b * nq + i, qo + h)),
                pl.BlockSpec((S, HP), lambda b, h, i: (b, ko + h)),
                pl.BlockSpec((S, HP), lambda b, h, i: (b, vo + h))]
    args = [qa, ka, va]
    if use_cum:
        in_specs.append(pl.BlockSpec((None, 8, S), lambda b, h, i: (b, 0, 0)))
        args.append(cum)
    return pl.pallas_call(
        body, name="attn_fwd", grid=(B, HEADS, nq), in_specs=in_specs,
        out_specs=pl.BlockSpec((tq, HP), lambda b, h, i: (b * nq + i, h)),
        out_shape=_sds((T, HEADS * HP), BF16),
        compiler_params=_cparams(VMEM_BIG))(*args)


def attn_bwd(qa, qo, ka, ko, va, vo, cum, dcat, do_off, scale, S, out_dtype):
    T = qa.shape[0]
    B = T // S
    tq = min(ATT_TQ, S)
    nq = S // tq
    use_cum = cum is not None

    def body(*refs):
        if use_cum:
            q_ref, k_ref, v_ref, do_ref, c_ref, dq_ref, dk_ref, dv_ref, dc_ref, dk_acc, dv_acc = refs
        else:
            q_ref, k_ref, v_ref, do_ref, dq_ref, dk_ref, dv_ref, dk_acc, dv_acc = refs
        h, i = pl.program_id(1), pl.program_id(2)

        @pl.when(i == 0)
        def _():
            dk_acc[...] = jnp.zeros_like(dk_acc)
            dv_acc[...] = jnp.zeros_like(dv_acc)
            if use_cum:
                dc_ref[...] = jnp.zeros_like(dc_ref)

        for lo, hi, kw in _bands(S, tq):
            @pl.when((i >= lo) & (i < hi))
            def _():
                crow = c_ref[pl.ds(h, 1), 0:kw] if use_cum else jnp.zeros((1, kw), F32)
                f = functools.partial(_attn_block, qpos0=i * tq, scale=scale, use_cum=use_cum)
                _, vjp = jax.vjp(f, q_ref[...], k_ref[0:kw, :], v_ref[0:kw, :], crow)
                dq, dk, dv, dc = vjp(do_ref[...])
                dq_ref[...] = dq.astype(out_dtype)
                dk_acc[0:kw, :] += dk
                dv_acc[0:kw, :] += dv
                if use_cum:
                    dc_ref[:, 0:kw] += dc

        @pl.when(i == nq - 1)
        def _():
            dk_ref[...] = dk_acc[...].astype(out_dtype)
            dv_ref[...] = dv_acc[...].astype(out_dtype)

    qspec = pl.BlockSpec((tq, HP), lambda b, h, i: (b * nq + i, qo + h))
    in_specs = [qspec, pl.BlockSpec((S, HP), lambda b, h, i: (b, ko + h)),
                pl.BlockSpec((S, HP), lambda b, h, i: (b, vo + h)),
                pl.BlockSpec((tq, HP), lambda b, h, i: (b * nq + i, do_off + h))]
    args = [qa, ka, va, dcat]
    kv_out = pl.BlockSpec((S, HP), lambda b, h, i: (b, h))
    out_specs = [pl.BlockSpec((tq, HP), lambda b, h, i: (b * nq + i, h)), kv_out, kv_out]
    out_shape = [_sds((T, HEADS * HP), out_dtype)] * 3
    if use_cum:
        in_specs.append(pl.BlockSpec((None, 8, S), lambda b, h, i: (b, 0, 0)))
        args.append(cum)
        out_specs.append(pl.BlockSpec((None, 1, S), lambda b, h, i: (b * HEADS + h, 0, 0)))
        out_shape.append(_sds((B * HEADS, 1, S), F32))
    return pl.pallas_call(
        body, name="attn_bwd", grid=(B, HEADS, nq), in_specs=in_specs, out_specs=out_specs, out_shape=out_shape,
        scratch_shapes=[pltpu.VMEM((S, HP), F32), pltpu.VMEM((S, HP), F32)],
        compiler_params=_cparams(VMEM_BIG))(*args)


def _tri(n, upper):
    r, c = _iota((n, n), 0), _iota((n, n), 1)
    return ((r <= c) if upper else (r >= c)).astype(F32)


def fox_gate_fwd(proj, bcol, S):
    T = proj.shape[0]
    B = T // S
    ts = _tile(S, 512)
    nt = S // ts

    def body(p_ref, b_ref, o_ref, carry):
        @pl.when(pl.program_id(1) == 0)
        def _():
            carry[...] = jnp.zeros_like(carry)
        cf = jnp.transpose(p_ref[...])[0:8, :]
        lf = jax.nn.log_sigmoid(cf + b_ref[...])
        cum = _mm_hi(lf, _tri(ts, True)) + carry[...]
        o_ref[...] = cum
        carry[...] += jnp.sum(lf, axis=1, keepdims=True)

    return pl.pallas_call(
        body, name="fox_gate_fwd", grid=(B, nt),
        in_specs=[pl.BlockSpec((ts, HP), lambda b, j: (b * nt + j, C_CF // HP)), pl.BlockSpec((8, 1), lambda b, j: (0, 0))],
        out_specs=pl.BlockSpec((None, 8, ts), lambda b, j: (b, 0, j)), out_shape=_sds((B, 8, S), F32),
        scratch_shapes=[pltpu.VMEM((8, 1), F32)],
        compiler_params=_cparams(VMEM_BIG))(proj, bcol)


def fox_gate_bwd(proj, bcol, dcum, S):
    T = proj.shape[0]
    B = T // S
    ts = _tile(S, 512)
    nt = S // ts

    def body(p_ref, b_ref, dc_ref, dp_ref, db_ref, carry):
        b, j = pl.program_id(0), pl.program_id(1)

        @pl.when(j == 0)
        def _():
            carry[...] = jnp.zeros_like(carry)

        @pl.when((b == 0) & (j == 0))
        def _():
            db_ref[...] = jnp.zeros_like(db_ref)

        cf = jnp.transpose(p_ref[...])[0:8, :]
        dc = dc_ref[...]
        dlf = _mm_hi(dc, _tri(ts, False)) + carry[...]
        carry[...] += jnp.sum(dc, axis=1, keepdims=True)
        dcf = dlf * jax.nn.sigmoid(-(cf + b_ref[...]))
        db_ref[...] += jnp.sum(dcf, axis=1, keepdims=True)
        full = jnp.concatenate([dcf, jnp.zeros((HP - 8, ts), F32)], axis=0)
        dp_ref[...] = jnp.transpose(full).astype(BF16)

    def rev(b, j):
        return nt - 1 - j

    return pl.pallas_call(
        body, name="fox_gate_bwd", grid=(B, nt),
        in_specs=[pl.BlockSpec((ts, HP), lambda b, j: (b * nt + rev(b, j), C_CF // HP)),
                  pl.BlockSpec((8, 1), lambda b, j: (0, 0)),
                  pl.BlockSpec((None, 8, ts), lambda b, j: (b, 0, rev(b, j)))],
        out_specs=[pl.BlockSpec((ts, HP), lambda b, j: (b * nt + rev(b, j), 0)), pl.BlockSpec((8, 1), lambda b, j: (0, 0))],
        out_shape=[_sds((T, HP), BF16), _sds((8, 1), F32)],
        scratch_shapes=[pltpu.VMEM((8, 1), F32)],
        compiler_params=_cparams(VMEM_BIG))(proj, bcol, dcum)


def _mla_pre(blk, gq, gkv, wq, wkv, place, cos_q, sin_q, cs_k):
    nq = _rms(blk[:, 0:B_Q_LORA], gq)
    nkv = _rms(blk[:, B_Q_LORA:B_Q_LORA + B_KV_LORA], gkv)
    qq = _mm(nq, wq)
    q = qq[:, 0:HEADS * HP] * cos_q + qq[:, HEADS * HP:] * sin_q
    kv = _mm(nkv, wkv)
    k = kv[:, 0:HEADS * HP] + _mm(blk[:, B_Q_LORA + B_KV_LORA:] * cs_k, place)
    return q, k, kv[:, HEADS * HP:]


def mla_pre_fwd(proj, gq, gkv, wq, wkv, place, cos_q, sin_q, cs_k, S):
    T = proj.shape[0]
    tm = _tile(S, 512)
    tpb = S // tm
    W = HEADS * HP

    def body(p_ref, gq_ref, gkv_ref, wq_ref, wkv_ref, pl_ref, cq_ref, sq_ref, ck_ref, q_ref, k_ref, v_ref):
        q, k, v = _mla_pre(p_ref[...], gq_ref[...], gkv_ref[...], wq_ref[...], wkv_ref[...], pl_ref[...],
                           cq_ref[...], sq_ref[...], ck_ref[...])
        q_ref[...] = q
        k_ref[...] = k
        v_ref[...] = v

    def full(a):
        return pl.BlockSpec(a.shape, lambda i: (0,) * a.ndim)

    tok = pl.BlockSpec((tm, W), lambda i: (i, 0))
    return pl.pallas_call(
        body, name="mla_pre_fwd", grid=(T // tm,),
        in_specs=[pl.BlockSpec((tm, W), lambda i: (i, C_B // W)), full(gq), full(gkv), full(wq), full(wkv), full(place),
                  pl.BlockSpec((tm, W), lambda i: (i % tpb, 0)), pl.BlockSpec((tm, W), lambda i: (i % tpb, 0)),
                  pl.BlockSpec((tm, HP), lambda i: (i % tpb, 0))],
        out_specs=[tok] * 3, out_shape=[_sds((T, W), F32)] * 3,
        compiler_params=_cparams(VMEM_BIG))(proj, gq, gkv, wq, wkv, place, cos_q, sin_q, cs_k)


def mla_pre_bwd(proj, gq, gkv, wq, wkv, place, cos_q, sin_q, cs_k, dq, dk, dv, S):
    T = proj.shape[0]
    tm = _tile(S, 512)
    tpb = S // tm
    W = HEADS * HP

    def body(p_ref, gq_ref, gkv_ref, wq_ref, wkv_ref, pl_ref, cq_ref, sq_ref, ck_ref, dq_ref, dk_ref, dv_ref,
             dp_ref, dgq_ref, dgkv_ref, dwq_ref, dwkv_ref):
        @pl.when(pl.program_id(0) == 0)
        def _():
            for r in (dgq_ref, dgkv_ref, dwq_ref, dwkv_ref):
                r[...] = jnp.zeros_like(r)

        f = functools.partial(_mla_pre, place=pl_ref[...], cos_q=cq_ref[...], sin_q=sq_ref[...], cs_k=ck_ref[...])
        _, vjp = jax.vjp(f, p_ref[...], gq_ref[...], gkv_ref[...], wq_ref[...], wkv_ref[...])
        dp, dgq, dgkv, dwq, dwkv = vjp((dq_ref[...], dk_ref[...], dv_ref[...]))
        dp_ref[...] = dp.astype(BF16)
        dgq_ref[...] += dgq
        dgkv_ref[...] += dgkv
        dwq_ref[...] += dwq
        dwkv_ref[...] += dwkv

    def full(a):
        return pl.BlockSpec(a.shape, lambda i: (0,) * a.ndim)

    tok = pl.BlockSpec((tm, W), lambda i: (i, 0))
    return pl.pallas_call(
        body, name="mla_pre_bwd", grid=(T // tm,),
        in_specs=[pl.BlockSpec((tm, W), lambda i: (i, C_B // W)), full(gq), full(gkv), full(wq), full(wkv), full(place),
                  pl.BlockSpec((tm, W), lambda i: (i % tpb, 0)), pl.BlockSpec((tm, W), lambda i: (i % tpb, 0)),
                  pl.BlockSpec((tm, HP), lambda i: (i % tpb, 0)), tok, tok, tok],
        out_specs=[tok, full(gq), full(gkv), full(wq), full(wkv)],
        out_shape=[_sds((T, W), BF16), _sds(gq.shape, F32), _sds(gkv.shape, F32), _sds(wq.shape, F32), _sds(wkv.shape, F32)],
        compiler_params=_cparams(VMEM_BIG))(proj, gq, gkv, wq, wkv, place, cos_q, sin_q, cs_k, dq, dk, dv)


def _gmlp_block(blk, lg, lb, ws, bs):
    u = jax.nn.gelu(blk[:, 0:GW])
    v = jax.nn.gelu(blk[:, GW:2 * GW])
    mu = jnp.mean(v, axis=-1, keepdims=True)
    var = jnp.mean(jnp.square(v - mu), axis=-1, keepdims=True)
    vn = (v - mu) * lax.rsqrt(var + LN_EPS) * lg + lb
    causal = _iota((D_CHUNK, D_CHUNK), 1) <= _iota((D_CHUNK, D_CHUNK), 0)
    group = _iota((1, GW), 1) // HD
    mixed = jnp.zeros((D_CHUNK, GW), F32)
    for g in range(HEADS):
        part = _mm(jnp.where(causal, ws[g], 0.0), vn) + bs[g]
        mixed = mixed + jnp.where(group == g, part, 0.0)
    return u * mixed


def gmlp_fwd(proj, lg, lb, ws, bs):
    T = proj.shape[0]

    def body(p_ref, lg_ref, lb_ref, ws_ref, bs_ref, o_ref):
        o_ref[...] = _gmlp_block(p_ref[...], lg_ref[...], lb_ref[...], ws_ref[...], bs_ref[...]).astype(BF16)

    def full(a):
        return pl.BlockSpec(a.shape, lambda i: (0,) * a.ndim)

    return pl.pallas_call(
        body, name="gmlp_fwd", grid=(T // D_CHUNK,),
        in_specs=[pl.BlockSpec((D_CHUNK, 2 * GW), lambda i: (i, C_D // (2 * GW))), full(lg), full(lb), full(ws), full(bs)],
        out_specs=pl.BlockSpec((D_CHUNK, GW), lambda i: (i, 0)), out_shape=_sds((T, GW), BF16),
        compiler_params=_cparams(VMEM_BIG))(proj, lg, lb, ws, bs)


def gmlp_bwd(proj, lg, lb, ws, bs, dcat):
    T = proj.shape[0]

    def body(p_ref, lg_ref, lb_ref, ws_ref, bs_ref, do_ref, dp_ref, dlg_ref, dlb_ref, dws_ref, dbs_ref):
        @pl.when(pl.program_id(0) == 0)
        def _():
            for r in (dlg_ref, dlb_ref, dws_ref, dbs_ref):
                r[...] = jnp.zeros_like(r)

        _, vjp = jax.vjp(_gmlp_block, p_ref[...], lg_ref[...], lb_ref[...], ws_ref[...], bs_ref[...])
        dp, dlg, dlb, dws, dbs = vjp(do_ref[...])
        dp_ref[...] = dp.astype(BF16)
        dlg_ref[...] += dlg
        dlb_ref[...] += dlb
        dws_ref[...] += dws
        dbs_ref[...] += dbs

    def full(a):
        return pl.BlockSpec(a.shape, lambda i: (0,) * a.ndim)

    return pl.pallas_call(
        body, name="gmlp_bwd", grid=(T // D_CHUNK,),
        in_specs=[pl.BlockSpec((D_CHUNK, 2 * GW), lambda i: (i, C_D // (2 * GW))), full(lg), full(lb), full(ws), full(bs),
                  pl.BlockSpec((D_CHUNK, GW), lambda i: (i, O_D // GW))],
        out_specs=[pl.BlockSpec((D_CHUNK, 2 * GW), lambda i: (i, 0)), full(lg), full(lb), full(ws), full(bs)],
        out_shape=[_sds((T, 2 * GW), BF16), _sds(lg.shape, F32), _sds(lb.shape, F32), _sds(ws.shape, F32), _sds(bs.shape, F32)],
        compiler_params=_cparams(VMEM_BIG))(proj, lg, lb, ws, bs, dcat)


def _lb_all(logits):
    m = jnp.max(logits, axis=0, keepdims=True)
    e = jnp.exp(logits - m)
    sm = e / jnp.sum(e, axis=0, keepdims=True)
    return jnp.concatenate([sm[0:1] - sm[0:1], (sm[0:1] + sm[1:2]) - sm[0:1]], axis=0)


def lb_fwd(logits):
    def body(l_ref, o_ref):
        o_ref[...] = _lb_all(l_ref[...])

    return pl.pallas_call(body, name="lb_fwd", out_shape=_sds(logits.shape, F32))(logits)


def lb_bwd(logits, dlb):
    def body(l_ref, d_ref, o_ref):
        _, vjp = jax.vjp(_lb_all, l_ref[...])
        o_ref[...] = vjp(d_ref[...])[0]

    return pl.pallas_call(body, name="lb_bwd", out_shape=_sds(logits.shape, F32))(logits, dlb)


def ada_grad(c_all, dmod_cols):
    N = dmod_cols.shape[1]
    tn = _tile(N, 1152)

    def body(c_ref, d_ref, o_ref):
        h = jax.nn.silu(c_ref[...]).astype(BF16)
        o_ref[...] = lax.dot_general(h, d_ref[...].astype(BF16), (((0,), (0,)), ((), ())), preferred_element_type=F32)

    nb = c_all.shape[0]
    return pl.pallas_call(
        body, name="ada_grad", grid=(N // tn,),
        in_specs=[pl.BlockSpec((nb, D), lambda j: (0, 0)), pl.BlockSpec((nb, tn), lambda j: (0, j))],
        out_specs=pl.BlockSpec((D, tn), lambda j: (0, j)), out_shape=_sds((D, N), F32),
        compiler_params=_cparams(VMEM_BIG))(c_all, dmod_cols)


def sum_slots(a, n, name):
    _, R, C = a.shape
    tr = _row_tile(R, C, n)

    def body(a_ref, o_ref):
        acc = a_ref[0]
        for k in range(1, n):
            acc = acc + a_ref[k]
        o_ref[...] = acc

    return pl.pallas_call(
        body, name=name, grid=(R // tr,),
        in_specs=[pl.BlockSpec((n, tr, C), lambda i: (0, i, 0))],
        out_specs=pl.BlockSpec((tr, C), lambda i: (i, 0)), out_shape=_sds((R, C), F32),
        compiler_params=_cparams(VMEM_BIG))(a)


def add2(a, b, name):
    shp = a.shape
    C = shp[-1]
    a2, b2 = a.reshape(-1, C), b.reshape(-1, C)
    R = a2.shape[0]
    tr = _row_tile(R, C)

    def body(a_ref, b_ref, o_ref):
        o_ref[...] = a_ref[...] + b_ref[...]

    spec = pl.BlockSpec((tr, C), lambda i: (i, 0))
    return pl.pallas_call(body, name=name, grid=(R // tr,), in_specs=[spec, spec], out_specs=spec,
                          out_shape=_sds((R, C), F32), compiler_params=_cparams(VMEM_BIG))(a2, b2).reshape(shp)


def _row_tile(R, C=D, n=1):
    limit = max(8, (1 << 18) // (C * n))
    for t in range(limit - limit % 8, 7, -8):
        if R % t == 0:
            return t
    return R


def adamw(w, g, m, v, name):
    R, C = w.shape
    tr = _row_tile(R, C)
    c1 = 1.0 - ADAM_B1 ** ADAM_STEP
    c2 = 1.0 - ADAM_B2 ** ADAM_STEP

    def body(w_ref, g_ref, m_ref, v_ref, d_ref, nm_ref, nv_ref):
        g_ = g_ref[...]
        nm = ADAM_B1 * m_ref[...] + (1.0 - ADAM_B1) * g_
        nv = ADAM_B2 * v_ref[...] + (1.0 - ADAM_B2) * jnp.square(g_)
        d_ref[...] = -ADAM_LR * ((nm / c1) / (jnp.sqrt(nv / c2) + ADAM_EPS) + ADAM_WD * w_ref[...])
        nm_ref[...] = nm
        nv_ref[...] = nv

    spec = pl.BlockSpec((tr, C), lambda i: (i, 0))
    return pl.pallas_call(body, name=name, grid=(R // tr,), in_specs=[spec] * 4, out_specs=[spec] * 3,
                          out_shape=[_sds((R, C), F32)] * 3, compiler_params=_cparams(VMEM_BIG))(w, g, m, v)


def _rot_cols(w):
    return jnp.concatenate([-w[:, 16:32], w[:, 0:16]], axis=1)


def _fold_rot(d):
    return jnp.concatenate([d[:, 16:32], -d[:, 0:16]], axis=1)


def _pad_heads(w, off, axis):
    parts = []
    for h in range(HEADS):
        piece = lax.slice_in_dim(w, off + HD * h, off + HD * (h + 1), axis=axis)
        parts += [piece, jnp.zeros_like(piece)]
    return parts


def _unpad_heads(d, off, axis):
    return [lax.slice_in_dim(d, off + HP * h, off + HP * h + HD, axis=axis) for h in range(HEADS)]


def mix_in_ext(w):
    z = lambda n: jnp.zeros((w.shape[0], n), w.dtype)
    kr = w[:, 1408:1440]
    cols = [w[:, 0:1408], kr, _rot_cols(kr), z(64)]
    cols += _pad_heads(w, 1440, 1) + _pad_heads(w, 1696, 1) + _pad_heads(w, 1952, 1)
    cols += [w[:, 2212:2724], w[:, 2208:2212], z(NP - C_CF - HEADS)]
    return jnp.concatenate(cols, axis=1)


def mix_in_unext(d):
    kr = d[:, 1408:1440] + _fold_rot(d[:, 1440:1472])
    cols = [d[:, 0:1408], kr] + _unpad_heads(d, C_CQ, 1) + _unpad_heads(d, C_CK, 1) + _unpad_heads(d, C_CV, 1)
    cols += [d[:, C_CF:C_CF + HEADS], d[:, C_D:C_D + 2 * GW]]
    return jnp.concatenate(cols, axis=1)


def mix_out_ext(w):
    return jnp.concatenate([w[0:GW]] + _pad_heads(w, GW, 0) + _pad_heads(w, 2 * GW, 0) + [w[3 * GW:4 * GW]], axis=0)


def mix_out_unext(d):
    return jnp.concatenate([d[0:GW]] + _unpad_heads(d, O_B, 0) + _unpad_heads(d, O_C, 0) + [d[O_D:O_D + GW]], axis=0)


def uq_ext(w):
    z = lambda n: jnp.zeros((w.shape[0], n), w.dtype)
    a, b = [], []
    for h in range(HEADS):
        o = (B_NOPE + B_ROPE) * h
        a += [w[:, o:o + B_NOPE + B_ROPE], z(32)]
        b += [z(B_NOPE), _rot_cols(w[:, o + B_NOPE:o + B_NOPE + B_ROPE]), z(32)]
    return jnp.concatenate(a + b, axis=1)


def uq_unext(d):
    cols = []
    for h in range(HEADS):
        o = HP * h
        cols += [d[:, o:o + B_NOPE], d[:, o + B_NOPE:o + B_NOPE + B_ROPE]
                 + _fold_rot(d[:, HEADS * HP + o + B_NOPE:HEADS * HP + o + B_NOPE + B_ROPE])]
    return jnp.concatenate(cols, axis=1)


def ukv_ext(w):
    z = jnp.zeros((w.shape[0], HD), w.dtype)
    k, v = [], []
    for h in range(HEADS):
        k += [w[:, 2 * HD * h:2 * HD * h + HD], z]
        v += [w[:, 2 * HD * h + HD:2 * HD * (h + 1)], z]
    return jnp.concatenate(k + v, axis=1)


def ukv_unext(d):
    cols = []
    for h in range(HEADS):
        cols += [d[:, HP * h:HP * h + HD], d[:, HEADS * HP + HP * h:HEADS * HP + HP * h + HD]]
    return jnp.concatenate(cols, axis=1)


def rope_tables(S):
    half = B_ROPE // 2
    inv_freq = ROPE_THETA ** (-jnp.arange(half, dtype=F32) / half)
    ang = jnp.arange(S).astype(F32)[:, None] * inv_freq[None, :]
    cos = jnp.tile(jnp.cos(ang), (1, 2))
    sin = jnp.tile(jnp.sin(ang), (1, 2))
    one, zero = jnp.ones((S, B_NOPE), F32), jnp.zeros((S, B_NOPE), F32)
    z32 = jnp.zeros((S, 32), F32)
    cos_q = jnp.tile(jnp.concatenate([one, cos, z32], axis=1), (1, HEADS))
    sin_q = jnp.tile(jnp.concatenate([zero, sin, z32], axis=1), (1, HEADS))
    cs_k = jnp.concatenate([cos, sin, zero], axis=1)
    place = np.zeros((HP, HEADS * HP), np.float32)
    for h in range(HEADS):
        for j in range(B_ROPE):
            place[j, h * HP + B_NOPE + j] = 1.0
            place[B_ROPE + j, h * HP + B_NOPE + j] = 1.0
    return cos_q, sin_q, cs_k, jnp.asarray(place, BF16)


def layer_fwd(x, mod, get, tabs, S):
    cos_q, sin_q, cs_k, place = tabs
    p = dict(get("ffn1", x))
    l = p["wl"]
    zg1, zu1, act1 = ffn_in_fwd(x, mod, p["ffn1_in"], l, 0, 1, S)
    y1, x1 = out_ln_fwd(act1, p["ffn1_out"], x, mod, p["ln_g"][0:1], p["ln_b"][0:1], 2, 0.5, S, l)
    p.update(get("mix", x1))
    proj = mix_in_fwd(x1, mod, p["mix_in"], 3, 4, S)
    o_a, states = hgrn_fwd(proj, p["lb"], p["ng"], S)
    q_b, k_b, v_b = mla_pre_fwd(proj, p["gq"], p["gkv"], p["wq"], p["wkv"], place, cos_q, sin_q, cs_k, S)
    o_b = attn_fwd(q_b, 0, k_b, 0, v_b, 0, None, (B_NOPE + B_ROPE) ** -0.5, S)
    cum = fox_gate_fwd(proj, p["bcol"], S)
    o_c = attn_fwd(proj, C_CQ // HP, proj, C_CK // HP, proj, C_CV // HP, cum, HD ** -0.5, S)
    o_d = gmlp_fwd(proj, p["g_lg"], p["g_lb"], p["ws"], p["bs"])
    cat = jnp.concatenate([o_a, o_b, o_c, o_d], axis=1)
    y2, x2 = out_ln_fwd(cat, p["mix_out"], x1, mod, p["ln_g"][1:2], p["ln_b"][1:2], 5, 1.0, S)
    p.update(get("ffn2", x2))
    zg3, zu3, act3 = ffn_in_fwd(x2, mod, p["ffn2_in"], l, 6, 7, S)
    y3, x3 = out_ln_fwd(act3, p["ffn2_out"], x2, mod, p["ln_g"][2:3], p["ln_b"][2:3], 8, 0.5, S, l)
    saved = dict(x=x, zg1=zg1, zu1=zu1, act1=act1, y1=y1, x1=x1, proj=proj, states=states, q_b=q_b, k_b=k_b, v_b=v_b,
                 cum=cum, cat=cat, y2=y2, x2=x2, zg3=zg3, zu3=zu3, act3=act3, y3=y3, p=p)
    return x3, saved


def _ffn_bwd(dxn, x_in, y, zg, zu, act, mod, w_in, w_out, l, lg, lb, idx, S, emit):
    sh, sc, gate = idx
    dres, dy, dgate, dlg, dlb = ln_res_bwd(dxn, x_in, y, mod, lg, lb, gate, 0.5, S)
    dzg, dzu = swiglu_bwd(dy, w_out, l, zg, zu, S)
    dw_out = tn_mm(act, dy, FF // 2).reshape(N_CHIPS, FF // N_CHIPS, D)
    dw_in = tn_mm_mod_shards(x_in, mod, dzg, dzu, sh, sc, S)
    mod = mod + emit(dw_in, dw_out)
    dx, dsh, dsc = nt_mod_bwd([dzg, dzu], w_in, [0, 2], x_in, mod, dres, sc, S, FF // 2, l)
    return dx, dw_in, dw_out, dlg, dlb, {sh: dsh, sc: dsc, gate: dgate}, mod


def layer_bwd(dx3, mod, sv, tabs, S, emit):
    cos_q, sin_q, cs_k, place = tabs
    p = sv["p"]
    l = p["wl"]
    g = {}
    dm = {}

    def emit_ffn(part):
        def f(dw_in, dw_out):
            g[part + "_in"], g[part + "_out"] = dw_in, dw_out
            return emit(part, g)
        return f

    dx2, _, _, dlg2, dlb2, d, mod = _ffn_bwd(
        dx3, sv["x2"], sv["y3"], sv["zg3"], sv["zu3"], sv["act3"], mod, p["ffn2_in"], p["ffn2_out"], l,
        p["ln_g"][2:3], p["ln_b"][2:3], (6, 7, 8), S, emit_ffn("ffn2"))
    dm.update(d)
    dres, dy2, dm[5], dlg1, dlb1 = ln_res_bwd(dx2, sv["x1"], sv["y2"], mod, p["ln_g"][1:2], p["ln_b"][1:2], 5, 1.0, S)
    dcat = nt_plain(dy2, p["mix_out"])
    g["mix_out"] = tn_mm(sv["cat"], dy2, 768)
    proj = sv["proj"]
    d_a, g["lb"], g["ng"] = hgrn_bwd(proj, sv["states"], dcat, p["lb"], p["ng"], S)
    dq_c, dk_c, dv_c, dcum = attn_bwd(proj, C_CQ // HP, proj, C_CK // HP, proj, C_CV // HP, sv["cum"], dcat,
                                      O_C // HP, HD ** -0.5, S, BF16)
    B = proj.shape[0] // S
    dcum = jnp.concatenate([dcum.reshape(B, HEADS, S), jnp.zeros((B, 8 - HEADS, S), F32)], axis=1)
    d_cf, g["bcol"] = fox_gate_bwd(proj, p["bcol"], dcum, S)
    dq_b, dk_b, dv_b = attn_bwd(sv["q_b"], 0, sv["k_b"], 0, sv["v_b"], 0, None, dcat, O_B // HP,
                                (B_NOPE + B_ROPE) ** -0.5, S, F32)
    d_b, g["gq"], g["gkv"], g["wq"], g["wkv"] = mla_pre_bwd(
        proj, p["gq"], p["gkv"], p["wq"], p["wkv"], place, cos_q, sin_q, cs_k, dq_b, dk_b, dv_b, S)
    d_d, g["g_lg"], g["g_lb"], g["ws"], g["bs"] = gmlp_bwd(proj, p["g_lg"], p["g_lb"], p["ws"], p["bs"], dcat)
    dproj = jnp.concatenate([d_a, d_b, dq_c, dk_c, dv_c, d_d, d_cf, jnp.zeros_like(d_cf)], axis=1)
    g["mix_in"] = tn_mm_mod(sv["x1"], mod, dproj, 3, 4, S, NP_TILE)
    mod = mod + emit("mix", g)
    dx1, dm[3], dm[4] = nt_mod_bwd([dproj], p["mix_in"], [0], sv["x1"], mod, dres, 4, S, NP_TILE)
    last = []

    def emit_last(dw_in, dw_out):
        last.append(emit_ffn("ffn1")(dw_in, dw_out))
        return last[0]

    dx0, _, _, dlg0, dlb0, d, mod = _ffn_bwd(
        dx1, sv["x"], sv["y1"], sv["zg1"], sv["zu1"], sv["act1"], mod, p["ffn1_in"], p["ffn1_out"], l,
        p["ln_g"][0:1], p["ln_b"][0:1], (0, 1, 2), S, emit_last)
    dm.update(d)
    g["ln_g"] = jnp.concatenate([dlg0, dlg1, dlg2], axis=0)
    g["ln_b"] = jnp.concatenate([dlb0, dlb1, dlb2], axis=0)
    dmod = jnp.concatenate([dm[i] for i in range(N_MOD)], axis=1)
    return dx0, dmod, g, last[0]


def local_step(x, c8, tgt, get, lb_logits, S, emit=None):
    B = x.shape[0] // S
    tabs = rope_tables(S)
    lb_all = lb_fwd(lb_logits)
    mods, saved = [], []
    h = x
    for l in range(DEPTH):
        pa = get(l, "ada", h)
        mod = mod_fwd(c8, pa["ada_w"], pa["wl"], pa["ada_b"])[0:B].reshape(B, N_MOD, D)

        def get_l(part, after, l=l):
            p = dict(get(l, part, after))
            if part == "mix":
                p["lb"] = lb_all[l:l + 1]
            return p

        h, sv = layer_fwd(h, mod, get_l, tabs, S)
        mods.append(mod)
        saved.append(sv)
    loss_tile, dh = loss_head(h, tgt)
    grads, dmods, dlb = [None] * DEPTH, [None] * DEPTH, [None] * DEPTH
    tie = jnp.zeros((), F32)
    for l in reversed(range(DEPTH)):
        emit_l = (lambda part, g: jnp.zeros((), F32)) if emit is None else functools.partial(emit, l)
        dh, dmods[l], grads[l], tie = layer_bwd(dh, mods[l] + tie, saved[l], tabs, S, emit_l)
        dlb[l] = grads[l].pop("lb")
    d_logits = lb_bwd(lb_logits, jnp.concatenate(dlb, axis=0))
    return loss_tile, dh, dmods, grads, d_logits


ANY = pl.BlockSpec(memory_space=pl.ANY)


def _place():
    x, y, c = lax.axis_index("x"), lax.axis_index("y"), lax.axis_index("c")
    chips = [(1 - x, y), (x, 1 - y), (1 - x, 1 - y)]
    return x, y, c, chips


def _rcopy(src, dst, sems, k, to):
    send_sems, recv_sems = sems
    return pltpu.make_async_remote_copy(src_ref=src, dst_ref=dst, send_sem=send_sems.at[k], recv_sem=recv_sems.at[k],
                                        device_id=to, device_id_type=MESH)


def _dma_sems(n_remote, n_local):
    return [pltpu.SemaphoreType.DMA((n_remote,)), pltpu.SemaphoreType.DMA((n_remote,)), pltpu.SemaphoreType.DMA((n_local,))]


def own_slot(src, chip):
    L = src.shape[0]
    return lax.dynamic_update_slice(jnp.zeros((L, N_CHIPS) + src.shape[1:], src.dtype), src[:, None], (0, chip, 0, 0))


def ag_shards(arrs, lands):
    n = len(arrs)
    rh = [a.shape[1] // 2 for a in arrs]

    def body(*refs):
        srcs, outs, token = refs[:n], refs[2 * n:3 * n], refs[3 * n]
        send_sems, recv_sems = refs[3 * n + 1:]
        x, y, c, chips = _place()
        sems = (send_sems, recv_sems)
        me = 2 * x + y
        sibling = (x, y, 1 - c)
        token[...] = jnp.zeros_like(token)

        def part(i, k, hc):
            return outs[i].at[:, k, pl.ds(hc * rh[i], rh[i]), :]

        started = []
        for j, (px, py) in enumerate(chips):
            for i in range(n):
                cp = _rcopy(srcs[i].at[:, pl.ds(c * rh[i], rh[i]), :], part(i, me, c), sems, 6 * i + j, (px, py, c))
                cp.start()
                started.append(cp)
        for j, (px, py) in enumerate(chips):
            k = 2 * px + py
            for i in range(n):
                _rcopy(part(i, k, c), part(i, k, c), sems, 6 * i + j, (px, py, c)).wait_recv()
                cp = _rcopy(part(i, k, c), part(i, k, c), sems, 6 * i + 3 + j, sibling)
                cp.start()
                started.append(cp)
        for j, (px, py) in enumerate(chips):
            k = 2 * px + py
            for i in range(n):
                _rcopy(part(i, k, 1 - c), part(i, k, 1 - c), sems, 6 * i + 3 + j, sibling).wait_recv()
        for cp in started:
            cp.wait_send()

    outs = pl.pallas_call(
        body, name="ag_shards", out_shape=[_sds(a.shape, a.dtype) for a in lands] + [_sds((8, 128), F32)],
        in_specs=[ANY] * (2 * n), out_specs=[ANY] * n + [pl.BlockSpec(memory_space=pltpu.VMEM)],
        input_output_aliases={n + i: i for i in range(n)}, scratch_shapes=_dma_sems(6 * n, 1)[:2])(*arrs, *lands)
    return list(outs[:n]), outs[n]


HBM_SPEC = pl.BlockSpec(memory_space=pltpu.HBM)
SEM_SPEC = pl.BlockSpec(memory_space=pltpu.SEMAPHORE)
DATAFLOW = pltpu.SideEffectType.DATAFLOW_SIDE_EFFECTING


def _after(x, dep):
    return lax.optimization_barrier((x, dep))[0]


def _split_start(srcs, lands, copies, name):
    n, m = len(srcs), len(lands)

    def body(*refs):
        ins = refs[:n + m]
        send_sems, recv_sems = refs[n + m], refs[n + m + 1]
        token = refs[-1]
        for k, (src, dst, to) in enumerate(copies(ins[:n], ins[n:], _place())):
            pltpu.make_async_remote_copy(src_ref=src, dst_ref=dst, send_sem=send_sems.at[k], recv_sem=recv_sems.at[k],
                                         device_id=to, device_id_type=MESH).start()
        token[...] = jnp.zeros_like(token)

    n_copies = 3 * n
    arrs = list(srcs) + list(lands)
    outs = pl.pallas_call(
        body, name=name,
        out_shape=(pltpu.SemaphoreType.DMA((n_copies,)), pltpu.SemaphoreType.DMA((n_copies,)),
                   *[pltpu.HBM(a.shape, a.dtype) for a in arrs], _sds((8, 128), F32)),
        in_specs=[HBM_SPEC] * (n + m),
        out_specs=(SEM_SPEC, SEM_SPEC, *[HBM_SPEC] * (n + m), pl.BlockSpec(memory_space=pltpu.VMEM)),
        input_output_aliases={i: 2 + i for i in range(n + m)},
        compiler_params=pltpu.CompilerParams(has_side_effects=DATAFLOW),
    )(*[pltpu.with_memory_space_constraint(a, pltpu.HBM) for a in arrs])
    return outs[0], outs[1], list(outs[2:2 + n]), list(outs[2 + n:2 + n + m]), outs[-1]


def _split_wait(handle, arrivals, after, name):
    send_sems, recv_sems, srcs, lands, _ = handle
    n, m = len(srcs), len(lands)

    def body(*refs):
        ins = refs[:n + m]
        send_sems, recv_sems = refs[n + m], refs[n + m + 1]
        x, y, c, chips = place = _place()
        for k, (src, dst) in enumerate(arrivals(ins[:n], ins[n:], place)):
            cp = pltpu.make_async_remote_copy(src_ref=src, dst_ref=dst, send_sem=send_sems.at[k], recv_sem=recv_sems.at[k],
                                              device_id=(x, y, 1 - c), device_id_type=MESH)
            cp.wait_send()
            cp.wait_recv()

    arrs = list(srcs) + list(lands)
    outs = pl.pallas_call(
        body, name=name, out_shape=[pltpu.HBM(a.shape, a.dtype) for a in arrs],
        in_specs=[HBM_SPEC] * (n + m) + [SEM_SPEC, SEM_SPEC, ANY], out_specs=[HBM_SPEC] * (n + m),
        input_output_aliases={i: i for i in range(n + m)},
        compiler_params=pltpu.CompilerParams(has_side_effects=DATAFLOW),
    )(*arrs, send_sems, recv_sems, after)
    return list(outs[n:])


def _ag_part(ref, k, hc):
    rh = ref.shape[2] // 2
    return ref.at[:, k, pl.ds(hc * rh, rh), :]


def ag_start(srcs, lands, name):
    def copies(s, d, place):
        x, y, c, chips = place
        out = []
        for j, (px, py) in enumerate(chips):
            for i in range(len(s)):
                rh = s[i].shape[1] // 2
                out.append((s[i].at[:, pl.ds(c * rh, rh), :], _ag_part(d[i], 2 * x + y, c), (px, py, c)))
        return out

    return _split_start(srcs, lands, copies, name)


def ag_wait(handle, after, name):
    def arrivals(s, d, place):
        x, y, c, chips = place
        out = []
        for j, (px, py) in enumerate(chips):
            for i in range(len(s)):
                rh = s[i].shape[1] // 2
                out.append((s[i].at[:, pl.ds(c * rh, rh), :], _ag_part(d[i], 2 * px + py, c)))
        return out

    return _split_wait(handle, arrivals, after, name)


def ag_forward(lands, name):
    n = len(lands)

    def body(*refs):
        bufs = refs[n:2 * n]
        send_sems, recv_sems = refs[2 * n:]
        x, y, c, chips = _place()
        sems = (send_sems, recv_sems)
        cps = []
        for j, (px, py) in enumerate(chips):
            for i in range(n):
                part = _ag_part(bufs[i], 2 * px + py, c)
                cps.append(_rcopy(part, part, sems, 3 * i + j, (x, y, 1 - c)))
        for cp in cps:
            cp.start()
        for j, (px, py) in enumerate(chips):
            for i in range(n):
                part = _ag_part(bufs[i], 2 * px + py, 1 - c)
                _rcopy(part, part, sems, 3 * i + j, (x, y, 1 - c)).wait_recv()
        for cp in cps:
            cp.wait_send()

    return pl.pallas_call(
        body, name=name, out_shape=[_sds(a.shape, a.dtype) for a in lands], in_specs=[ANY] * n, out_specs=[ANY] * n,
        input_output_aliases={i: i for i in range(n)}, scratch_shapes=_dma_sems(3 * n, 1)[:2])(*lands)


def rs_start(hs, lands, name):
    def copies(s, d, place):
        x, y, c, chips = place
        return [(s[i].at[2 * px + py], d[i].at[2 * x + y], (px, py, c)) for j, (px, py) in enumerate(chips) for i in range(len(s))]

    return _split_start(hs, lands, copies, name)


def rs_wait(handle, after, name):
    def arrivals(s, d, place):
        x, y, c, chips = place
        return [(s[i].at[2 * px + py], d[i].at[2 * px + py]) for j, (px, py) in enumerate(chips) for i in range(len(s))]

    return _split_wait(handle, arrivals, after, name)


def sibling_swap(arrs, name):
    n = len(arrs)
    rh = [a.shape[1] // 2 for a in arrs]

    def body(*refs):
        srcs, outs = refs[:n], refs[n:2 * n]
        send_sems, recv_sems = refs[2 * n:]
        x, y, c, _ = _place()
        cps = [_rcopy(srcs[i].at[:, pl.ds((1 - c) * rh[i], rh[i]), :], outs[i], (send_sems, recv_sems), i, (x, y, 1 - c))
               for i in range(n)]
        for cp in cps:
            cp.start()
        for cp in cps:
            cp.wait()

    return pl.pallas_call(
        body, name=name, out_shape=[_sds((N_CHIPS, r, a.shape[2]), a.dtype) for a, r in zip(arrs, rh)],
        in_specs=[ANY] * n, out_specs=[ANY] * n, scratch_shapes=_dma_sems(n, 1)[:2])(*arrs)


def chip_exchange(hs):
    n = len(hs)

    def body(*refs):
        srcs, outs = refs[:n], refs[n:2 * n]
        send_sems, recv_sems, loc_sems = refs[2 * n:]
        x, y, c, chips = _place()
        sems = (send_sems, recv_sems)
        me = 2 * x + y
        mine = [pltpu.make_async_copy(srcs[i].at[me], outs[i].at[me], loc_sems.at[i]) for i in range(n)]
        for cp in mine:
            cp.start()
        sends = []
        for j, (px, py) in enumerate(chips):
            for i in range(n):
                cp = _rcopy(srcs[i].at[2 * px + py], outs[i].at[me], sems, 3 * i + j, (px, py, c))
                cp.start()
                sends.append(cp)
        for j, (px, py) in enumerate(chips):
            for i in range(n):
                _rcopy(srcs[i].at[2 * px + py], outs[i].at[2 * px + py], sems, 3 * i + j, (px, py, c)).wait_recv()
        for cp in sends:
            cp.wait_send()
        for cp in mine:
            cp.wait()

    return pl.pallas_call(
        body, name="chip_exchange", out_shape=[_sds(h.shape, h.dtype) for h in hs],
        in_specs=[ANY] * n, out_specs=[ANY] * n, scratch_shapes=_dma_sems(3 * n, n))(*hs)


def sum_into(land, base, l, core, name):
    _, rh, C = land.shape
    tr = _row_tile(rh, C, N_CHIPS)
    nr = rh // tr

    def body(core_ref, land_ref, base_ref, o_ref):
        acc = land_ref[0]
        for k in range(1, N_CHIPS):
            acc = acc + land_ref[k]
        o_ref[...] = acc

    grid_spec = pltpu.PrefetchScalarGridSpec(
        num_scalar_prefetch=1, grid=(nr,),
        in_specs=[pl.BlockSpec((N_CHIPS, tr, C), lambda r, core_ref: (0, r, 0)), ANY],
        out_specs=pl.BlockSpec((None, tr, C), lambda r, core_ref: (l, core_ref[0] * nr + r, 0)))
    return pl.pallas_call(body, name=name, grid_spec=grid_spec, out_shape=_sds(base.shape, base.dtype),
                          input_output_aliases={2: 0}, compiler_params=_cparams(VMEM_BIG))(
        core.reshape(1).astype(jnp.int32), land, base)


def sibling_join(bases, name):
    n = len(bases)

    def body(*refs):
        bufs = refs[n:2 * n]
        send_sems, recv_sems = refs[2 * n:]
        x, y, c, _ = _place()
        sems = (send_sems, recv_sems)

        def half(i, hc):
            rh = bufs[i].shape[1] // 2
            return bufs[i].at[:, pl.ds(hc * rh, rh), :]

        sends = [_rcopy(half(i, c), half(i, c), sems, i, (x, y, 1 - c)) for i in range(n)]
        for cp in sends:
            cp.start()
        for i in range(n):
            _rcopy(half(i, 1 - c), half(i, 1 - c), sems, i, (x, y, 1 - c)).wait_recv()
        for cp in sends:
            cp.wait_send()

    return pl.pallas_call(
        body, name=name, out_shape=[_sds(b.shape, b.dtype) for b in bases], in_specs=[ANY] * n, out_specs=[ANY] * n,
        input_output_aliases={i: i for i in range(n)}, scratch_shapes=_dma_sems(n, 1)[:2])(*bases)


def ag_all(blk):
    M, C = blk.shape

    def body(x_ref, out_ref, send_sems, recv_sems, loc_sem):
        x, y, c, chips = _place()
        sems = (send_sems, recv_sems)
        me, sibling = (x, y, c), (x, y, 1 - c)

        def slot(px, py, pc):
            return out_ref.at[4 * px + 2 * py + pc]

        mine = pltpu.make_async_copy(x_ref, slot(*me), loc_sem)
        mine.start()
        first = [_rcopy(x_ref, slot(*me), sems, 0, sibling)]
        first += [_rcopy(x_ref, slot(*me), sems, 1 + j, (*chip, c)) for j, chip in enumerate(chips)]
        for cp in first:
            cp.start()
        passed = [_rcopy(slot(*chip, c), slot(*chip, c), sems, 4 + j, sibling) for j, chip in enumerate(chips)]
        for j, chip in enumerate(chips):
            _rcopy(slot(*chip, c), slot(*chip, c), sems, 1 + j, me).wait_recv()
            passed[j].start()
        _rcopy(slot(*sibling), slot(*sibling), sems, 0, me).wait_recv()
        for j, chip in enumerate(chips):
            _rcopy(slot(*chip, 1 - c), slot(*chip, 1 - c), sems, 4 + j, me).wait_recv()
        for cp in first + passed:
            cp.wait_send()
        mine.wait()

    return pl.pallas_call(
        body, name="ag_all", out_shape=_sds((8, M, C), blk.dtype),
        in_specs=[pl.BlockSpec(memory_space=pltpu.VMEM)], out_specs=pl.BlockSpec(memory_space=pltpu.VMEM),
        scratch_shapes=[pltpu.SemaphoreType.DMA((7,)), pltpu.SemaphoreType.DMA((7,)), pltpu.SemaphoreType.DMA(())],
        compiler_params=_cparams(VMEM_BIG))(blk)


WEIGHTS = ["ada_w", "ada_b", "ln_g", "ln_b", "ffn1_w_in", "ffn1_w_out", "ffn2_w_in", "ffn2_w_out", "mix_w_in", "mix_w_out",
           "hgrn_lb_logits", "hgrn_norm_g", "mla_q_norm_g", "mla_kv_norm_g", "mla_w_uq", "mla_w_ukv", "fox_b_f",
           "gmlp_ln_g", "gmlp_ln_b", "gmlp_w_s", "gmlp_b_s"]
SHARDED = {"ffn1_w_in": 1, "ffn1_w_out": 0, "ffn2_w_in": 1, "ffn2_w_out": 0, "mix_w_in": 1, "mix_w_out": 0,
           "mla_w_uq": 1, "mla_w_ukv": 1}
SMALL = ["hgrn_lb_logits", "hgrn_norm_g", "mla_q_norm_g", "mla_kv_norm_g", "fox_b_f", "gmlp_ln_g", "gmlp_ln_b",
         "gmlp_w_s", "gmlp_b_s", "ln_g", "ln_b"]
GATHERED = ["ada_w", "ffn1_w_in", "ffn1_w_out", "ffn2_w_in", "ffn2_w_out", "mix_w_in", "mix_w_out", "mla_w_uq", "mla_w_ukv"]
REDUCED = GATHERED[1:]


def _col_shards(a):
    cols = a.shape[1] // N_CHIPS
    return jnp.stack([a[:, k * cols:(k + 1) * cols] for k in range(N_CHIPS)])


def add_kept_half(a, got, core, name):
    _, R, C = a.shape
    rh = R // 2
    tr = _row_tile(rh, C)
    nr = rh // tr

    def body(core_ref, a_ref, b_ref, o_ref):
        o_ref[...] = a_ref[...] + b_ref[...]

    half = pl.BlockSpec((None, tr, C), lambda k, r, core_ref: (k, r, 0))
    grid_spec = pltpu.PrefetchScalarGridSpec(
        num_scalar_prefetch=1, grid=(N_CHIPS, nr),
        in_specs=[pl.BlockSpec((None, tr, C), lambda k, r, core_ref: (k, core_ref[0] * nr + r, 0)), half],
        out_specs=half)
    return pl.pallas_call(body, name=name, grid_spec=grid_spec, out_shape=_sds((N_CHIPS, rh, C), F32),
                          compiler_params=_cparams(VMEM_BIG))(core.reshape(1).astype(jnp.int32), a, got)


def _rows(parts, n_rows, dtype):
    flat = jnp.concatenate([p.reshape(-1) for p in parts])
    pad = n_rows * D - flat.shape[0]
    return jnp.concatenate([flat, jnp.zeros((pad,), dtype)]).reshape(n_rows, D)


def _take(flat, shapes):
    out, o = [], 0
    for shp in shapes:
        n = int(np.prod(shp))
        out.append(flat[o:o + n].reshape(shp))
        o += n
    return out


def _round_up(n, m):
    return -(-n // m) * m


def pack_shard(w):
    parts = [w[n][l] for l in range(DEPTH) for n in SHARDED] + [w[n][l] for l in range(DEPTH) for n in ("ln_g", "ln_b")]
    n = sum(int(np.prod(p.shape)) for p in parts)
    return _rows(parts, _round_up(-(-n // D), 16), F32)


def unpack_shard(pk, like):
    shapes = [like[n].shape[1:] for l in range(DEPTH) for n in SHARDED] + [like[n].shape[1:] for l in range(DEPTH) for n in ("ln_g", "ln_b")]
    pieces = _take(pk.reshape(-1), shapes)
    names = [n for l in range(DEPTH) for n in SHARDED] + [n for l in range(DEPTH) for n in ("ln_g", "ln_b")]
    out = {}
    for n in list(SHARDED) + ["ln_g", "ln_b"]:
        out[n] = jnp.stack([p for p, m in zip(pieces, names) if m == n])
    return out


def pack_small(w):
    parts = [w[n][l] for l in range(DEPTH) for n in SMALL]
    n = sum(int(np.prod(p.shape)) for p in parts)
    return _rows(parts, _round_up(-(-n // D), 8), F32)


def unpack_small(pk, like):
    shapes = [like[n].shape[1:] for l in range(DEPTH) for n in SMALL]
    pieces = _take(pk.reshape(-1), shapes)
    names = [n for l in range(DEPTH) for n in SMALL]
    return {n: jnp.stack([p for p, m in zip(pieces, names) if m == n]) for n in SMALL}


def pack_gather(w):
    parts = [w[n][l].astype(BF16) for l in range(DEPTH) for n in ["ada_w"] + list(SHARDED)]
    ln = jnp.concatenate([w[n][l].reshape(-1) for l in range(DEPTH) for n in ("ln_g", "ln_b")])
    parts.append(lax.bitcast_convert_type(ln, BF16))
    n = sum(int(np.prod(p.shape)) for p in parts)
    return _rows(parts, _round_up(-(-n // D), 16), BF16)


def unpack_gather(g, w):
    names = ["ada_w"] + list(SHARDED)
    shapes = [w[n].shape[1:] for l in range(DEPTH) for n in names]
    n_ln = DEPTH * 2 * 3 * (D // N_CHIPS)
    flat = g.reshape(N_CHIPS, -1)
    per_chip = [_take(flat[k], shapes + [(n_ln, 2)]) for k in range(N_CHIPS)]
    layers = [dict() for _ in range(DEPTH)]
    i = 0
    for l in range(DEPTH):
        for n in names:
            axis = 1 if n == "ada_w" else SHARDED[n]
            layers[l][n] = jnp.concatenate([per_chip[k][i] for k in range(N_CHIPS)], axis=axis)
            i += 1
    ln = [lax.bitcast_convert_type(per_chip[k][i], F32).reshape(DEPTH, 2, 3, D // N_CHIPS) for k in range(N_CHIPS)]
    ln = jnp.concatenate(ln, axis=3)
    for l in range(DEPTH):
        layers[l]["ln_g"], layers[l]["ln_b"] = ln[l, 0], ln[l, 1]
    return layers


def pack_grads(grads, k):
    parts = []
    for l in range(DEPTH):
        g = grads[l]
        full = {"ffn1_w_out": g["ffn1_out"], "ffn2_w_out": g["ffn2_out"], "mix_w_in": mix_in_unext(g["mix_in"]),
                "mix_w_out": mix_out_unext(g["mix_out"]), "mla_w_uq": uq_unext(g["wq"]), "mla_w_ukv": ukv_unext(g["wkv"])}
        for n, axis in SHARDED.items():
            if n in ("ffn1_w_in", "ffn2_w_in"):
                half = g[n.replace("_w_in", "_in")][k // 2]
                parts.append(half[:, (k % 2) * (FF // 2):(k % 2 + 1) * (FF // 2)])
            else:
                sz = full[n].shape[axis] // N_CHIPS
                parts.append(lax.slice_in_dim(full[n], k * sz, (k + 1) * sz, axis=axis))
    for l in range(DEPTH):
        for n in ("ln_g", "ln_b"):
            parts.append(grads[l][n][:, k * (D // N_CHIPS):(k + 1) * (D // N_CHIPS)])
    n = sum(int(np.prod(p.shape)) for p in parts)
    return _rows(parts, _round_up(-(-n // D), 16), F32)


def kernel(x, c, ada_w, ada_b, ln_g, ln_b, ffn1_w_in, ffn1_w_out, ffn2_w_in, ffn2_w_out, mix_w_in, mix_w_out, hgrn_lb_logits, hgrn_norm_g, mla_q_norm_g, mla_kv_norm_g, mla_w_uq, mla_w_ukv, fox_b_f, gmlp_ln_g, gmlp_ln_b, gmlp_w_s, gmlp_b_s, loss_target, m_ada_w, m_ada_b, m_ln_g, m_ln_b, m_ffn1_w_in, m_ffn1_w_out, m_ffn2_w_in, m_ffn2_w_out, m_mix_w_in, m_mix_w_out, m_hgrn_lb_logits, m_hgrn_norm_g, m_mla_q_norm_g, m_mla_kv_norm_g, m_mla_w_uq, m_mla_w_ukv, m_fox_b_f, m_gmlp_ln_g, m_gmlp_ln_b, m_gmlp_w_s, m_gmlp_b_s, v_ada_w, v_ada_b, v_ln_g, v_ln_b, v_ffn1_w_in, v_ffn1_w_out, v_ffn2_w_in, v_ffn2_w_out, v_mix_w_in, v_mix_w_out, v_hgrn_lb_logits, v_hgrn_norm_g, v_mla_q_norm_g, v_mla_kv_norm_g, v_mla_w_uq, v_mla_w_ukv, v_fox_b_f, v_gmlp_ln_g, v_gmlp_ln_b, v_gmlp_w_s, v_gmlp_b_s):
    w = dict(zip(WEIGHTS, (ada_w, ada_b, ln_g, ln_b, ffn1_w_in, ffn1_w_out, ffn2_w_in, ffn2_w_out, mix_w_in, mix_w_out, hgrn_lb_logits, hgrn_norm_g, mla_q_norm_g, mla_kv_norm_g, mla_w_uq, mla_w_ukv, fox_b_f, gmlp_ln_g, gmlp_ln_b, gmlp_w_s, gmlp_b_s)))
    m = dict(zip(WEIGHTS, (m_ada_w, m_ada_b, m_ln_g, m_ln_b, m_ffn1_w_in, m_ffn1_w_out, m_ffn2_w_in, m_ffn2_w_out, m_mix_w_in, m_mix_w_out, m_hgrn_lb_logits, m_hgrn_norm_g, m_mla_q_norm_g, m_mla_kv_norm_g, m_mla_w_uq, m_mla_w_ukv, m_fox_b_f, m_gmlp_ln_g, m_gmlp_ln_b, m_gmlp_w_s, m_gmlp_b_s)))
    v = dict(zip(WEIGHTS, (v_ada_w, v_ada_b, v_ln_g, v_ln_b, v_ffn1_w_in, v_ffn1_w_out, v_ffn2_w_in, v_ffn2_w_out, v_mix_w_in, v_mix_w_out, v_hgrn_lb_logits, v_hgrn_norm_g, v_mla_q_norm_g, v_mla_kv_norm_g, v_mla_w_uq, v_mla_w_ukv, v_fox_b_f, v_gmlp_ln_g, v_gmlp_ln_b, v_gmlp_w_s, v_gmlp_b_s)))
    Bl, S, _ = x.shape
    T = Bl * S
    core = lax.axis_index("c")
    chip = 2 * lax.axis_index("x") + lax.axis_index("y")

    def shard(key):
        n, l = key
        if n == "ln":
            return jnp.concatenate([ln_g[l:l + 1], ln_b[l:l + 1], jnp.zeros((1, 2, D // N_CHIPS), F32)], axis=1)
        return w[n][l:l + 1].astype(BF16)

    mixers = ["mix_w_in", "mix_w_out", "mla_w_uq", "mla_w_ukv"]
    groups = [[("ada_w", 0), ("ffn1_w_in", 0), ("ffn1_w_out", 0), ("ln", 0)],
              [(n, 0) for n in mixers + ["ffn2_w_in", "ffn2_w_out"]],
              [(n, 1) for n in GATHERED + ["ln"]]]
    srcs = [shard(k) for k in groups[0]]
    first, token = ag_shards(srcs, [own_slot(s, chip) for s in srcs])
    have = dict(zip(groups[0], first))
    handles = {}
    for gi in (1, 2):
        srcs = [s + token[0, 0].astype(s.dtype) for s in (shard(k) for k in groups[gi])]
        handles[gi] = ag_start(srcs, [own_slot(s, chip) for s in srcs], "ag_start_%d" % gi)
        token = handles[gi][-1]
    c8 = jnp.concatenate([c, jnp.zeros((8 - Bl, D), F32)], axis=0)
    c8 = c8 + token[0, 0]

    def cat_cols(a):
        return jnp.concatenate([a[0, k] for k in range(N_CHIPS)], axis=1)

    def get(l, part, after):
        gi = 2 if l == 1 else (0 if part in ("ada", "ffn1") else 1)
        if gi in handles:
            arrived = ag_forward(ag_wait(handles.pop(gi), after, "ag_wait_%d" % gi), "ag_forward_%d" % gi)
            have.update(zip(groups[gi], arrived))
        if part == "ada":
            return dict(ada_w=have[("ada_w", l)], wl=0, ada_b=ada_b[l][None])
        if part == "ffn1":
            ln_full = jnp.moveaxis(have[("ln", l)][0], 0, 1).reshape(8, D)
            return dict(ffn1_in=have[("ffn1_w_in", l)], ffn1_out=have[("ffn1_w_out", l)], wl=0,
                        ln_g=ln_full[0:3], ln_b=ln_full[3:6])
        if part == "ffn2":
            return dict(ffn2_in=have[("ffn2_w_in", l)], ffn2_out=have[("ffn2_w_out", l)])
        return dict(
            mix_in=mix_in_ext(cat_cols(have[("mix_w_in", l)])), mix_out=mix_out_ext(have[("mix_w_out", l)].reshape(D, D)),
            wq=uq_ext(cat_cols(have[("mla_w_uq", l)])).astype(F32), wkv=ukv_ext(cat_cols(have[("mla_w_ukv", l)])).astype(F32),
            ng=hgrn_norm_g[l][None], gq=mla_q_norm_g[l][None], gkv=mla_kv_norm_g[l][None],
            bcol=jnp.concatenate([fox_b_f[l], jnp.zeros((8 - HEADS,), F32)])[:, None],
            g_lg=gmlp_ln_g[l][None], g_lb=gmlp_ln_b[l][None], ws=gmlp_w_s[l], bs=gmlp_b_s[l][:, :, None])

    pending = []

    def emit(l, part, g):
        if part == "mix":
            names = mixers
            by_chip = [_col_shards(mix_in_unext(g["mix_in"])), mix_out_unext(g["mix_out"]).reshape(N_CHIPS, D // N_CHIPS, D),
                       _col_shards(uq_unext(g["wq"])), _col_shards(ukv_unext(g["wkv"]))]
        else:
            names = [part + "_w_in", part + "_w_out"]
            by_chip = [g[part + "_in"], g[part + "_out"]]
        tag = "%d_%s" % (l, part)
        got = sibling_swap(by_chip, "sibling_swap_" + tag)
        chip_sum = [add_kept_half(a, r, core, "add_sibling") for a, r in zip(by_chip, got)]
        slot = lax.broadcasted_iota(jnp.int32, (N_CHIPS, 1, 1), 0)
        lands = [jnp.where(slot == chip, h, 0.0) for h in chip_sum]
        handle = rs_start(chip_sum, lands, "rs_start_" + tag)
        pending.append((l, names, handle, tag))
        return handle[-1][0, 0]

    loss_tile, dx, dmods, grads, d_logits = local_step(
        x.reshape(T, D), c8, loss_target.reshape(T, D), get, hgrn_lb_logits, S, emit)
    loss = lax.psum(loss_tile[0, 0], ("x", "y", "c"))

    small_g = {"hgrn_lb_logits": d_logits,
               "hgrn_norm_g": jnp.stack([grads[l]["ng"][0] for l in range(DEPTH)]),
               "mla_q_norm_g": jnp.stack([grads[l]["gq"][0] for l in range(DEPTH)]),
               "mla_kv_norm_g": jnp.stack([grads[l]["gkv"][0] for l in range(DEPTH)]),
               "fox_b_f": jnp.stack([grads[l]["bcol"][0:HEADS, 0] for l in range(DEPTH)]),
               "gmlp_ln_g": jnp.stack([grads[l]["g_lg"][0] for l in range(DEPTH)]),
               "gmlp_ln_b": jnp.stack([grads[l]["g_lb"][0] for l in range(DEPTH)]),
               "gmlp_w_s": jnp.stack([grads[l]["ws"] for l in range(DEPTH)]),
               "gmlp_b_s": jnp.stack([grads[l]["bs"][:, :, 0] for l in range(DEPTH)])}
    small_g["ln_g"] = jnp.stack([grads[l]["ln_g"] for l in range(DEPTH)])
    small_g["ln_b"] = jnp.stack([grads[l]["ln_b"] for l in range(DEPTH)])
    pk_small = pack_small(small_g)
    n_small = pk_small.shape[0]
    extras = [dmods[l] for l in range(DEPTH)] + [c]
    n_extra = _round_up(-(-sum(int(np.prod(e.shape)) for e in extras) // D), 8)
    gathered = ag_all(jnp.concatenate([pk_small, _rows(extras, n_extra, F32)], axis=0))
    g_small = unpack_small(sum_slots(gathered[:, 0:n_small], 8, "sum_small"), small_g)
    ext = gathered[:, n_small:].reshape(8, -1)
    n_dmod = DEPTH * Bl * N_MOD * D
    dmod_all = ext[:, 0:n_dmod].reshape(8, DEPTH, Bl, N_MOD * D)
    c_all = ext[:, n_dmod:n_dmod + Bl * D].reshape(8 * Bl, D)
    g_ada_w, g_ada_b = [], []
    ncol = N_MOD * D // N_CHIPS
    for l in range(DEPTH):
        dm = dmod_all[:, l].reshape(8 * Bl, N_MOD * D)
        g_ada_w.append(ada_grad(c_all, lax.dynamic_slice_in_dim(dm, chip * ncol, ncol, axis=1)))
        g_ada_b.append(sum_slots(dm.reshape(8 * Bl, N_MOD, D), 8 * Bl, "sum_ada_b").reshape(N_MOD * D))
    g_ada_w, g_ada_b = jnp.stack(g_ada_w), jnp.stack(g_ada_b)

    red = {n: jnp.zeros(w[n].shape, F32) for n in REDUCED}

    def arrive(entry, after):
        l, names, handle, tag = entry
        for n, land in zip(names, rs_wait(handle, after, "rs_wait_" + tag)):
            red[n] = sum_into(land, red[n], l, core, "sum_chips")

    for entry in pending[:-1]:
        arrive(entry, dx)
    late = pending[-1][1]
    early = [n for n in REDUCED if n not in late]
    grad = dict(zip(early, sibling_join([red[n] for n in early], "sibling_join_a")))
    grad.update(g_small)
    grad["ada_w"], grad["ada_b"] = g_ada_w, g_ada_b
    for n in ("ln_g", "ln_b"):
        grad[n] = lax.dynamic_slice_in_dim(g_small[n], chip * (D // N_CHIPS), D // N_CHIPS, axis=2)
    out = {"grad": grad, "delta": {}, "new_m": {}, "new_v": {}}

    def update(n):
        shp = w[n].shape
        two_d = (-1, shp[-1])
        res = adamw(w[n].reshape(two_d), grad[n].reshape(two_d), m[n].reshape(two_d), v[n].reshape(two_d), "adamw_" + n)
        grad[n] = grad[n].reshape(shp)
        for key, r in zip(("delta", "new_m", "new_v"), res):
            out[key][n] = r.reshape(shp)

    for n in WEIGHTS:
        if n not in late:
            update(n)
    arrive(pending[-1], out["delta"]["ffn2_w_in"])
    grad.update(zip(late, sibling_join([red[n] for n in late], "sibling_join_b")))
    for n in late:
        update(n)
    outs = [loss, dx.reshape(Bl, S, D)]
    for key in ("grad", "delta", "new_m", "new_v"):
        outs += [out[key][n] for n in WEIGHTS]
    return tuple(outs)
```

```python
import functools

import jax
import jax.numpy as jnp
import numpy as np
from jax import lax
from jax.experimental import pallas as pl
from jax.experimental.pallas import tpu as pltpu

F32, BF16 = jnp.float32, jnp.bfloat16
MESH = pl.DeviceIdType.MESH

N_CHIPS = 4
D = 1024
DEPTH = 2
FF = 2816
N_MOD = 9
GW = 256
HEADS = 4
HD = 64
HP = 128
A_CHUNK = 16
LB_FLOOR = 1e-30
B_Q_LORA, B_KV_LORA, B_NOPE, B_ROPE = 256, 128, 64, 32
ROPE_THETA = 10000.0
D_CHUNK = 128
MIX_COLS = 2724
ALPHA = (2 * DEPTH) ** 0.25
LN_EPS = 1e-5
RMS_EPS = 1e-6
ADAM_LR, ADAM_B1, ADAM_B2, ADAM_EPS, ADAM_WD, ADAM_STEP = 0.001, 0.9, 0.999, 1e-08, 0.01, 10

NP = 3840
NP_TILE = 1920
C_A, C_B, C_CQ, C_CK, C_CV, C_D, C_CF = 0, 1024, 1536, 2048, 2560, 3072, 3584
NCAT = 1536
O_A, O_B, O_C, O_D = 0, 256, 768, 1280

VMEM_BIG = 48 << 20


def _cparams(vmem=None):
    return pltpu.CompilerParams(vmem_limit_bytes=vmem) if vmem else pltpu.CompilerParams()


def _sds(shape, dtype):
    return jax.ShapeDtypeStruct(tuple(shape), dtype)


@jax.custom_vjp
def _mm(a, w):
    return jnp.dot(a.astype(BF16), w.astype(BF16), preferred_element_type=F32)


def _mm_f(a, w):
    return _mm(a, w), (a, w)


def _mm_b(res, g):
    a, w = res
    gb = g.astype(BF16)
    da = lax.dot_general(gb, w.astype(BF16), (((1,), (1,)), ((), ())), preferred_element_type=F32)
    dw = lax.dot_general(a.astype(BF16), gb, (((0,), (0,)), ((), ())), preferred_element_type=F32)
    return da.astype(a.dtype), dw.astype(w.dtype)


_mm.defvjp(_mm_f, _mm_b)


@jax.custom_vjp
def _mm_nt(a, b):
    return lax.dot_general(a.astype(BF16), b.astype(BF16), (((1,), (1,)), ((), ())), preferred_element_type=F32)


def _mm_nt_f(a, b):
    return _mm_nt(a, b), (a, b)


def _mm_nt_b(res, g):
    a, b = res
    gb = g.astype(BF16)
    da = jnp.dot(gb, b.astype(BF16), preferred_element_type=F32)
    db = lax.dot_general(gb, a.astype(BF16), (((0,), (0,)), ((), ())), preferred_element_type=F32)
    return da.astype(a.dtype), db.astype(b.dtype)


_mm_nt.defvjp(_mm_nt_f, _mm_nt_b)


@jax.custom_vjp
def _mm_tn(a, b):
    return lax.dot_general(a.astype(BF16), b.astype(BF16), (((0,), (0,)), ((), ())), preferred_element_type=F32)


def _mm_tn_f(a, b):
    return _mm_tn(a, b), (a, b)


def _mm_tn_b(res, g):
    a, b = res
    gb = g.astype(BF16)
    da = lax.dot_general(b.astype(BF16), gb, (((1,), (1,)), ((), ())), preferred_element_type=F32)
    db = jnp.dot(a.astype(BF16), gb, preferred_element_type=F32)
    return da.astype(a.dtype), db.astype(b.dtype)


_mm_tn.defvjp(_mm_tn_f, _mm_tn_b)


def _mm_hi(a, w):
    return jnp.dot(a, w, precision=lax.Precision.HIGHEST, preferred_element_type=F32)


def _iota(shape, dim):
    return lax.broadcasted_iota(jnp.int32, shape, dim)


def _head_sum_mats():
    e = (_iota((GW, HP), 0) // HD == _iota((GW, HP), 1)).astype(F32)
    et = (_iota((HP, GW), 1) // HD == _iota((HP, GW), 0)).astype(F32)
    return e, et


def _modulate(x, mod_ref, sh, sc):
    return x * (1.0 + mod_ref[sc:sc + 1, :]) + mod_ref[sh:sh + 1, :]


def _ln_res(x, y, gate, lg, lb, gs):
    r = ALPHA * x + gs * (1.0 + gate) * y
    mu = jnp.mean(r, axis=-1, keepdims=True)
    var = jnp.mean(jnp.square(r - mu), axis=-1, keepdims=True)
    return (r - mu) * lax.rsqrt(var + LN_EPS) * lg + lb


def _rms(x, g):
    return x * lax.rsqrt(jnp.mean(x * x, axis=-1, keepdims=True) + RMS_EPS) * g


def _tile(n, pref):
    return pref if n % pref == 0 else n


def mod_fwd(c8, w, l, b):
    tn = w.shape[3]
    n = N_CHIPS * tn

    def body(c_ref, w_ref, b_ref, o_ref):
        h = jax.nn.silu(c_ref[...]).astype(BF16)
        o_ref[...] = jnp.dot(h, w_ref[...], preferred_element_type=F32) + b_ref[...]

    return pl.pallas_call(
        body, name="mod_fwd", grid=(N_CHIPS,),
        in_specs=[pl.BlockSpec((8, D), lambda j: (0, 0)), pl.BlockSpec((None, None, D, tn), lambda j: (l, j, 0, 0)),
                  pl.BlockSpec((1, tn), lambda j: (0, j))],
        out_specs=pl.BlockSpec((8, tn), lambda j: (0, j)), out_shape=_sds((8, n), F32),
        compiler_params=_cparams(VMEM_BIG))(c8, w, b)


def ffn_in_fwd(x, mod, w_in, l, sh, sc, S):
    T = x.shape[0]
    tm, tn = _tile(S, 512), FF // 2
    tpb, nj = S // tm, 2

    def body(x_ref, mod_ref, wg_ref, wu_ref, zg_ref, zu_ref, act_ref, h_ref):
        @pl.when(pl.program_id(1) == 0)
        def _():
            h_ref[...] = _modulate(x_ref[...], mod_ref, sh, sc).astype(BF16)
        g = jnp.dot(h_ref[...], wg_ref[...], preferred_element_type=F32)
        u = jnp.dot(h_ref[...], wu_ref[...], preferred_element_type=F32)
        zg_ref[...] = g.astype(BF16)
        zu_ref[...] = u.astype(BF16)
        act_ref[...] = (jax.nn.silu(g) * u).astype(BF16)

    return pl.pallas_call(
        body, name="ffn_in_fwd", grid=(T // tm, nj),
        in_specs=[pl.BlockSpec((tm, D), lambda i, j: (i, 0)),
                  pl.BlockSpec((None, N_MOD, D), lambda i, j: (i // tpb, 0, 0)),
                  pl.BlockSpec((None, None, D, tn), lambda i, j: (l, j, 0, 0)),
                  pl.BlockSpec((None, None, D, tn), lambda i, j: (l, j + nj, 0, 0))],
        out_specs=[pl.BlockSpec((tm, tn), lambda i, j: (i, j))] * 3,
        out_shape=[_sds((T, FF), BF16)] * 3,
        scratch_shapes=[pltpu.VMEM((tm, D), BF16)],
        compiler_params=_cparams(VMEM_BIG))(x, mod, w_in, w_in)


def mix_in_fwd(x, mod, w, sh, sc, S):
    T = x.shape[0]
    n = w.shape[1]
    tm, tn = _tile(S, 512), NP_TILE
    tpb = S // tm

    def body(x_ref, mod_ref, w_ref, o_ref, h_ref):
        @pl.when(pl.program_id(1) == 0)
        def _():
            h_ref[...] = _modulate(x_ref[...], mod_ref, sh, sc).astype(BF16)
        o_ref[...] = jnp.dot(h_ref[...], w_ref[...], preferred_element_type=F32)

    return pl.pallas_call(
        body, name="mix_in_fwd", grid=(T // tm, n // tn),
        in_specs=[pl.BlockSpec((tm, D), lambda i, j: (i, 0)),
                  pl.BlockSpec((None, N_MOD, D), lambda i, j: (i // tpb, 0, 0)),
                  pl.BlockSpec((D, tn), lambda i, j: (0, j))],
        out_specs=pl.BlockSpec((tm, tn), lambda i, j: (i, j)), out_shape=_sds((T, n), F32),
        scratch_shapes=[pltpu.VMEM((tm, D), BF16)],
        compiler_params=_cparams(VMEM_BIG))(x, mod, w)


def out_ln_fwd(act, w_out, x, mod, lg, lb, gate, gs, S, l=None):
    T, K = act.shape
    tm = _tile(S, 512)
    tpb = S // tm

    def body(a_ref, w_ref, x_ref, mod_ref, lg_ref, lb_ref, y_ref, xn_ref):
        y = jnp.dot(a_ref[...], w_ref[...].reshape(K, D), preferred_element_type=F32)
        y_ref[...] = y
        xn_ref[...] = _ln_res(x_ref[...], y, mod_ref[gate:gate + 1, :], lg_ref[...], lb_ref[...], gs)

    if l is None:
        w_spec = pl.BlockSpec((K, D), lambda i: (0, 0))
    else:
        w_spec = pl.BlockSpec((None, N_CHIPS, K // N_CHIPS, D), lambda i: (l, 0, 0, 0))
    return pl.pallas_call(
        body, name="out_ln_fwd", grid=(T // tm,),
        in_specs=[pl.BlockSpec((tm, K), lambda i: (i, 0)), w_spec,
                  pl.BlockSpec((tm, D), lambda i: (i, 0)),
                  pl.BlockSpec((None, N_MOD, D), lambda i: (i // tpb, 0, 0)),
                  pl.BlockSpec((1, D), lambda i: (0, 0)), pl.BlockSpec((1, D), lambda i: (0, 0))],
        out_specs=[pl.BlockSpec((tm, D), lambda i: (i, 0))] * 2,
        out_shape=[_sds((T, D), F32), _sds((T, D), F32)],
        compiler_params=_cparams(VMEM_BIG))(act, w_out, x, mod, lg, lb)


def ln_res_bwd(dxn, x, y, mod, lg, lb, gate, gs, S):
    T = x.shape[0]
    B = T // S
    tm = _tile(S, 512)
    tpb = S // tm

    def body(d_ref, x_ref, y_ref, mod_ref, lg_ref, lb_ref, dx_ref, dy_ref, dg_ref, dlg_ref, dlb_ref):
        i = pl.program_id(0)
        f = functools.partial(_ln_res, gs=gs)
        _, vjp = jax.vjp(f, x_ref[...], y_ref[...], mod_ref[gate:gate + 1, :], lg_ref[...], lb_ref[...])
        dx, dy, dg, dlg, dlb = vjp(d_ref[...])
        dx_ref[...] = dx
        dy_ref[...] = dy.astype(BF16)

        @pl.when(i % tpb == 0)
        def _():
            dg_ref[...] = jnp.zeros_like(dg_ref)

        @pl.when(i == 0)
        def _():
            dlg_ref[...] = jnp.zeros_like(dlg_ref)
            dlb_ref[...] = jnp.zeros_like(dlb_ref)

        dg_ref[...] += dg
        dlg_ref[...] += dlg
        dlb_ref[...] += dlb

    tok = pl.BlockSpec((tm, D), lambda i: (i, 0))
    vec = pl.BlockSpec((1, D), lambda i: (0, 0))
    return pl.pallas_call(
        body, name="ln_res_bwd", grid=(T // tm,),
        in_specs=[tok, tok, tok, pl.BlockSpec((None, N_MOD, D), lambda i: (i // tpb, 0, 0)), vec, vec],
        out_specs=[tok, tok, pl.BlockSpec((None, 1, D), lambda i: (i // tpb, 0, 0)), vec, vec],
        out_shape=[_sds((T, D), F32), _sds((T, D), BF16), _sds((B, 1, D), F32), _sds((1, D), F32), _sds((1, D), F32)],
        compiler_params=_cparams(VMEM_BIG))(dxn, x, y, mod, lg, lb)


def swiglu_bwd(dy, w_out, l, zg, zu, S):
    T = dy.shape[0]
    tm, tn = _tile(S, 512), FF // 2

    def body(dy_ref, w_ref, zg_ref, zu_ref, dg_ref, du_ref):
        da = lax.dot_general(dy_ref[...], w_ref[...].reshape(tn, D), (((1,), (1,)), ((), ())), preferred_element_type=F32)
        g, u = zg_ref[...].astype(F32), zu_ref[...].astype(F32)
        sg = jax.nn.sigmoid(g)
        dg_ref[...] = (da * u * (sg * (1.0 + g * (1.0 - sg)))).astype(BF16)
        du_ref[...] = (da * (g * sg)).astype(BF16)

    zt = pl.BlockSpec((tm, tn), lambda i, j: (i, j))
    return pl.pallas_call(
        body, name="swiglu_bwd", grid=(T // tm, FF // tn),
        in_specs=[pl.BlockSpec((tm, D), lambda i, j: (i, 0)),
                  pl.BlockSpec((None, 2, FF // N_CHIPS, D), lambda i, j: (l, j, 0, 0)), zt, zt],
        out_specs=[zt, zt], out_shape=[_sds((T, FF), BF16), _sds((T, FF), BF16)],
        compiler_params=_cparams(VMEM_BIG))(dy, w_out, zg, zu)


def nt_plain(dy, w):
    T = dy.shape[0]
    K = w.shape[0]
    tm = _tile(T, 512)

    def body(dy_ref, w_ref, o_ref):
        o_ref[...] = lax.dot_general(dy_ref[...], w_ref[...], (((1,), (1,)), ((), ())), preferred_element_type=F32)

    return pl.pallas_call(
        body, name="nt_plain", grid=(T // tm,),
        in_specs=[pl.BlockSpec((tm, D), lambda i: (i, 0)), pl.BlockSpec((K, D), lambda i: (0, 0))],
        out_specs=pl.BlockSpec((tm, K), lambda i: (i, 0)), out_shape=_sds((T, K), F32),
        compiler_params=_cparams(VMEM_BIG))(dy, w)


def tn_mm(a, b, tk):
    T, K = a.shape
    N = b.shape[1]
    tt = _tile(T, 512)

    def body(a_ref, b_ref, o_ref):
        @pl.when(pl.program_id(1) == 0)
        def _():
            o_ref[...] = jnp.zeros_like(o_ref)
        o_ref[...] += lax.dot_general(a_ref[...], b_ref[...], (((0,), (0,)), ((), ())), preferred_element_type=F32)

    return pl.pallas_call(
        body, name="tn_mm", grid=(K // tk, T // tt),
        in_specs=[pl.BlockSpec((tt, tk), lambda k, t: (t, k)), pl.BlockSpec((tt, N), lambda k, t: (t, 0))],
        out_specs=pl.BlockSpec((tk, N), lambda k, t: (k, 0)), out_shape=_sds((K, N), F32),
        compiler_params=_cparams(VMEM_BIG))(a, b)


def tn_mm_mod(x, mod, b, sh, sc, S, tn):
    T = x.shape[0]
    N = b.shape[1]
    tt = _tile(S, 512)
    tpb = S // tt

    def body(x_ref, mod_ref, b_ref, o_ref):
        @pl.when(pl.program_id(1) == 0)
        def _():
            o_ref[...] = jnp.zeros_like(o_ref)
        h = _modulate(x_ref[...], mod_ref, sh, sc).astype(BF16)
        o_ref[...] += lax.dot_general(h, b_ref[...], (((0,), (0,)), ((), ())), preferred_element_type=F32)

    return pl.pallas_call(
        body, name="tn_mm_mod", grid=(N // tn, T // tt),
        in_specs=[pl.BlockSpec((tt, D), lambda j, t: (t, 0)),
                  pl.BlockSpec((None, N_MOD, D), lambda j, t: (t // tpb, 0, 0)),
                  pl.BlockSpec((tt, tn), lambda j, t: (t, j))],
        out_specs=pl.BlockSpec((D, tn), lambda j, t: (0, j)), out_shape=_sds((D, N), F32),
        compiler_params=_cparams(VMEM_BIG))(x, mod, b)


def tn_mm_mod_shards(x, mod, bg, bu, sh, sc, S):
    T = x.shape[0]
    tn = FF // 2
    tt = _tile(S, 512)
    tpb = S // tt

    def body(x_ref, mod_ref, bg_ref, bu_ref, o_ref):
        j = pl.program_id(0)

        @pl.when(pl.program_id(1) == 0)
        def _():
            o_ref[...] = jnp.zeros_like(o_ref)
        h = _modulate(x_ref[...], mod_ref, sh, sc).astype(BF16)

        @pl.when(j < 2)
        def _():
            o_ref[...] += lax.dot_general(h, bg_ref[...], (((0,), (0,)), ((), ())), preferred_element_type=F32)

        @pl.when(j >= 2)
        def _():
            o_ref[...] += lax.dot_general(h, bu_ref[...], (((0,), (0,)), ((), ())), preferred_element_type=F32)

    return pl.pallas_call(
        body, name="tn_mm_mod_shards", grid=(N_CHIPS, T // tt),
        in_specs=[pl.BlockSpec((tt, D), lambda j, t: (t, 0)),
                  pl.BlockSpec((None, N_MOD, D), lambda j, t: (t // tpb, 0, 0)),
                  pl.BlockSpec((tt, tn), lambda j, t: (jnp.where(j < 2, t, 0), jnp.minimum(j, 1))),
                  pl.BlockSpec((tt, tn), lambda j, t: (jnp.where(j < 2, 0, t), jnp.maximum(j - 2, 0)))],
        out_specs=pl.BlockSpec((None, D, tn), lambda j, t: (j, 0, 0)), out_shape=_sds((N_CHIPS, D, tn), F32),
        compiler_params=_cparams(VMEM_BIG))(x, mod, bg, bu)


def nt_mod_bwd(ds, w, offs, x, mod, dres, sc, S, tk, l=None):
    T = x.shape[0]
    B = T // S
    tm = _tile(S, 512)
    tpb = S // tm
    Kd = ds[0].shape[1]
    nk = Kd // tk
    n_in = len(ds)

    def body(*refs):
        d_refs, w_refs = refs[:n_in], refs[n_in:2 * n_in]
        x_ref, mod_ref, r_ref, dx_ref, dsh_ref, dsc_ref, acc = refs[2 * n_in:]
        i, k = pl.program_id(0), pl.program_id(1)

        @pl.when(k == 0)
        def _():
            acc[...] = jnp.zeros_like(acc)

        for d_ref, w_ref in zip(d_refs, w_refs):
            acc[...] += lax.dot_general(d_ref[...], w_ref[...], (((1,), (1,)), ((), ())), preferred_element_type=F32)

        @pl.when(k == nk - 1)
        def _():
            dh = acc[...]
            dx_ref[...] = dh * (1.0 + mod_ref[sc:sc + 1, :]) + r_ref[...]

            @pl.when(i % tpb == 0)
            def _():
                dsh_ref[...] = jnp.zeros_like(dsh_ref)
                dsc_ref[...] = jnp.zeros_like(dsc_ref)

            dsh_ref[...] += jnp.sum(dh, axis=0, keepdims=True)
            dsc_ref[...] += jnp.sum(dh * x_ref[...], axis=0, keepdims=True)

    tok = pl.BlockSpec((tm, D), lambda i, k: (i, 0))
    vec = pl.BlockSpec((None, 1, D), lambda i, k: (i // tpb, 0, 0))
    in_specs = [pl.BlockSpec((tm, tk), lambda i, k: (i, k)) for _ in ds]
    if l is None:
        in_specs += [pl.BlockSpec((D, tk), functools.partial(lambda i, k, o: (0, k + o), o=off // tk)) for off in offs]
    else:
        in_specs += [pl.BlockSpec((None, None, D, tk), functools.partial(lambda i, k, o: (l, k + o, 0, 0), o=off)) for off in offs]
    in_specs += [tok, pl.BlockSpec((None, N_MOD, D), lambda i, k: (i // tpb, 0, 0)), tok]
    return pl.pallas_call(
        body, name="nt_mod_bwd", grid=(T // tm, nk), in_specs=in_specs,
        out_specs=[tok, vec, vec],
        out_shape=[_sds((T, D), F32), _sds((B, 1, D), F32), _sds((B, 1, D), F32)],
        scratch_shapes=[pltpu.VMEM((tm, D), F32)],
        compiler_params=_cparams(VMEM_BIG))(*ds, *([w] * n_in), x, mod, dres)


def loss_head(y, tgt):
    T = y.shape[0]
    tm = _tile(T, 512)

    def body(y_ref, t_ref, l_ref, d_ref):
        @pl.when(pl.program_id(0) == 0)
        def _():
            l_ref[...] = jnp.zeros_like(l_ref)
        e = y_ref[...] - t_ref[...]
        d_ref[...] = e * (1.0 / D)
        l_ref[...] += 0.5 * jnp.sum(jnp.sum(e * e, axis=1, keepdims=True) * (1.0 / D))

    tok = pl.BlockSpec((tm, D), lambda i: (i, 0))
    return pl.pallas_call(
        body, name="loss_head", grid=(T // tm,), in_specs=[tok, tok],
        out_specs=[pl.BlockSpec((8, 128), lambda i: (0, 0)), tok],
        out_shape=[_sds((8, 128), F32), _sds((T, D), F32)],
        compiler_params=_cparams(VMEM_BIG))(y, tgt)


def _hgrn_block(q, fz, inp, go, st, lb, ng, blk):
    nc = blk // A_CHUNK
    lb_eff = jnp.maximum(lb, LB_FLOOR)
    log_f = jnp.logaddexp(jnp.log(lb_eff), jnp.log1p(-lb) + jax.nn.log_sigmoid(fz))
    k = (1.0 - lb) * jax.nn.sigmoid(-fz) - (lb_eff - lb)
    qf = jax.nn.silu(q)
    same_chunk = _iota((blk, blk), 0) // A_CHUNK == _iota((blk, blk), 1) // A_CHUNK
    tril = (same_chunk & (_iota((blk, blk), 1) <= _iota((blk, blk), 0))).astype(F32)
    G = _mm_hi(tril, log_f)
    e_mat, et_mat = _head_sum_mats()
    G4, q4, k4, v4 = (z.reshape(nc, A_CHUNK, GW) for z in (G, qf, k, inp))
    shp = (nc, A_CHUNK, A_CHUNK, GW)
    one = (1, A_CHUNK, A_CHUNK, GW)
    mask = jnp.where(_iota(one, 2) <= _iota(one, 1), 0.0, -jnp.inf)
    decay = jnp.exp((G4[:, :, None, :] - G4[:, None, :, :]) + mask)
    prod = q4[:, :, None, :] * k4[:, None, :, :] * decay
    scores = _mm(prod.reshape(nc * A_CHUNK * A_CHUNK, GW), e_mat.astype(BF16))
    spread = _mm(scores, et_mat.astype(BF16)).reshape(shp)
    o_intra = jnp.sum(spread * v4[:, None, :, :], axis=2).reshape(blk, GW)
    head_diag = (_iota((GW, GW), 0) // HD == _iota((GW, GW), 1) // HD).astype(F32)
    g_last = [jnp.sum(log_f[c * A_CHUNK:(c + 1) * A_CHUNK], axis=0, keepdims=True) for c in range(nc)]
    g_last_b = jnp.concatenate([jnp.broadcast_to(g, (A_CHUNK, GW)) for g in g_last], axis=0)
    q_dec = qf * jnp.exp(G)
    k_end = k * jnp.exp(g_last_b - G)
    outs = []
    for c in range(nc):
        rows = slice(c * A_CHUNK, (c + 1) * A_CHUNK)
        outs.append(_mm_nt(q_dec[rows], st))
        st = st * jnp.exp(g_last[c]) + _mm_tn(inp[rows], k_end[rows]) * head_diag
    o = o_intra + jnp.concatenate(outs, axis=0)
    ms = _mm_hi(o * o, e_mat) * (1.0 / HD)
    o = o * _mm_hi(lax.rsqrt(ms + RMS_EPS), et_mat) * ng
    return o * jax.nn.silu(go), st


HGRN_BLK = 128


def hgrn_fwd(proj, lb, ng, S):
    T = proj.shape[0]
    B = T // S
    blk = min(HGRN_BLK, S)
    nb = S // blk

    def body(p_ref, lb_ref, ng_ref, o_ref, st_out_ref, st_ref):
        @pl.when(pl.program_id(1) == 0)
        def _():
            st_ref[...] = jnp.zeros_like(st_ref)
        st_out_ref[...] = st_ref[...]
        p = p_ref[...]
        o, st = _hgrn_block(p[:, 0:GW], p[:, GW:2 * GW], p[:, 2 * GW:3 * GW], p[:, 3 * GW:4 * GW],
                            st_ref[...], lb_ref[...], ng_ref[...], blk)
        o_ref[...] = o.astype(BF16)
        st_ref[...] = st

    vec = pl.BlockSpec((1, GW), lambda b, j: (0, 0))
    return pl.pallas_call(
        body, name="hgrn_fwd", grid=(B, nb),
        in_specs=[pl.BlockSpec((blk, 4 * GW), lambda b, j: (b * nb + j, C_A // (4 * GW))), vec, vec],
        out_specs=[pl.BlockSpec((blk, GW), lambda b, j: (b * nb + j, 0)),
                   pl.BlockSpec((None, GW, GW), lambda b, j: (b * nb + j, 0, 0))],
        out_shape=[_sds((T, GW), BF16), _sds((B * nb, GW, GW), F32)],
        scratch_shapes=[pltpu.VMEM((GW, GW), F32)],
        compiler_params=_cparams(VMEM_BIG))(proj, lb, ng)


def hgrn_bwd(proj, states, dcat, lb, ng, S):
    T = proj.shape[0]
    B = T // S
    blk = min(HGRN_BLK, S)
    nb = S // blk

    def body(p_ref, st_in_ref, do_ref, lb_ref, ng_ref, dp_ref, dlb_ref, dng_ref, dst_ref):
        b, j = pl.program_id(0), pl.program_id(1)

        @pl.when(j == 0)
        def _():
            dst_ref[...] = jnp.zeros_like(dst_ref)

        @pl.when((b == 0) & (j == 0))
        def _():
            dlb_ref[...] = jnp.zeros_like(dlb_ref)
            dng_ref[...] = jnp.zeros_like(dng_ref)

        p = p_ref[...]
        f = functools.partial(_hgrn_block, blk=blk)
        _, vjp = jax.vjp(f, p[:, 0:GW], p[:, GW:2 * GW], p[:, 2 * GW:3 * GW], p[:, 3 * GW:4 * GW],
                         st_in_ref[...], lb_ref[...], ng_ref[...])
        dq, df, di, dg, dst, dlb, dng = vjp((do_ref[...], dst_ref[...]))
        dp_ref[...] = jnp.concatenate([dq, df, di, dg], axis=1).astype(BF16)
        dst_ref[...] = dst
        dlb_ref[...] += dlb
        dng_ref[...] += dng

    def rev(b, j):
        return b * nb + (nb - 1 - j)

    vec = pl.BlockSpec((1, GW), lambda b, j: (0, 0))
    return pl.pallas_call(
        body, name="hgrn_bwd", grid=(B, nb),
        in_specs=[pl.BlockSpec((blk, 4 * GW), lambda b, j: (rev(b, j), C_A // (4 * GW))),
                  pl.BlockSpec((None, GW, GW), lambda b, j: (rev(b, j), 0, 0)),
                  pl.BlockSpec((blk, GW), lambda b, j: (rev(b, j), O_A // GW)), vec, vec],
        out_specs=[pl.BlockSpec((blk, 4 * GW), lambda b, j: (rev(b, j), 0)), vec, vec],
        out_shape=[_sds((T, 4 * GW), BF16), _sds((1, GW), F32), _sds((1, GW), F32)],
        scratch_shapes=[pltpu.VMEM((GW, GW), F32)],
        compiler_params=_cparams(VMEM_BIG))(proj, states, dcat, lb, ng)


ATT_TQ = 256


ATT_BANDS = 4


def _attn_block(q, k, v, cum, qpos0, scale, use_cum):
    s = _mm_nt(q, k) * scale
    if use_cum:
        s = s - cum
    qpos = qpos0 + _iota(s.shape, 0)
    s = jnp.where(_iota(s.shape, 1) <= qpos, s, -jnp.inf)
    e = jnp.exp(s - jnp.max(s, axis=-1, keepdims=True))
    p = e * (1.0 / jnp.sum(e, axis=-1, keepdims=True))
    return _mm(p, v)


def _bands(S, tq):
    nq = S // tq
    nb = min(ATT_BANDS, nq)
    per = nq // nb
    return [(r * per, (r + 1) * per, (r + 1) * per * tq) for r in range(nb)]


def attn_fwd(qa, qo, ka, ko, va, vo, cum, scale, S):
    T = qa.shape[0]
    B = T // S
    tq = min(ATT_TQ, S)
    nq = S // tq
    use_cum = cum is not None

    def body(*refs):
        if use_cum:
            q_ref, k_ref, v_ref, c_ref, o_ref = refs
        else:
            (q_ref, k_ref, v_ref, o_ref), c_ref = refs, None
        h, i = pl.program_id(1), pl.program_id(2)
        for lo, hi, kw in _bands(S, tq):
            @pl.when((i >= lo) & (i < hi))
            def _():
                crow = c_ref[pl.ds(h, 1), 0:kw] if use_cum else None
                o = _attn_block(q_ref[...], k_ref[0:kw, :], v_ref[0:kw, :], crow, i * tq, scale, use_cum)
                o_ref[...] = o.astype(BF16)

    in_specs = [pl.BlockSpec((tq, HP), lambda b, h, i: (b * nq + i, qo + h)),
                pl.BlockSpec((S, HP), lambda b, h, i: (b, ko + h)),
                pl.BlockSpec((S, HP), lambda b, h, i: (b, vo + h))]
    args = [qa, ka, va]
    if use_cum:
        in_specs.append(pl.BlockSpec((None, 8, S), lambda b, h, i: (b, 0, 0)))
        args.append(cum)
    return pl.pallas_call(
        body, name="attn_fwd", grid=(B, HEADS, nq), in_specs=in_specs,
        out_specs=pl.BlockSpec((tq, HP), lambda b, h, i: (b * nq + i, h)),
        out_shape=_sds((T, HEADS * HP), BF16),
        compiler_params=_cparams(VMEM_BIG))(*args)


def attn_bwd(qa, qo, ka, ko, va, vo, cum, dcat, do_off, scale, S, out_dtype):
    T = qa.shape[0]
    B = T // S
    tq = min(ATT_TQ, S)
    nq = S // tq
    use_cum = cum is not None

    def body(*refs):
        if use_cum:
            q_ref, k_ref, v_ref, do_ref, c_ref, dq_ref, dk_ref, dv_ref, dc_ref, dk_acc, dv_acc = refs
        else:
            q_ref, k_ref, v_ref, do_ref, dq_ref, dk_ref, dv_ref, dk_acc, dv_acc = refs
        h, i = pl.program_id(1), pl.program_id(2)

        @pl.when(i == 0)
        def _():
            dk_acc[...] = jnp.zeros_like(dk_acc)
            dv_acc[...] = jnp.zeros_like(dv_acc)
            if use_cum:
                dc_ref[...] = jnp.zeros_like(dc_ref)

        for lo, hi, kw in _bands(S, tq):
            @pl.when((i >= lo) & (i < hi))
            def _():
                crow = c_ref[pl.ds(h, 1), 0:kw] if use_cum else jnp.zeros((1, kw), F32)
                f = functools.partial(_attn_block, qpos0=i * tq, scale=scale, use_cum=use_cum)
                _, vjp = jax.vjp(f, q_ref[...], k_ref[0:kw, :], v_ref[0:kw, :], crow)
                dq, dk, dv, dc = vjp(do_ref[...])
                dq_ref[...] = dq.astype(out_dtype)
                dk_acc[0:kw, :] += dk
                dv_acc[0:kw, :] += dv
                if use_cum:
                    dc_ref[:, 0:kw] += dc

        @pl.when(i == nq - 1)
        def _():
            dk_ref[...] = dk_acc[...].astype(out_dtype)
            dv_ref[...] = dv_acc[...].astype(out_dtype)

    qspec = pl.BlockSpec((tq, HP), lambda b, h, i: (b * nq + i, qo + h))
    in_specs = [qspec, pl.BlockSpec((S, HP), lambda b, h, i: (b, ko + h)),
                pl.BlockSpec((S, HP), lambda b, h, i: (b, vo + h)),
                pl.BlockSpec((tq, HP), lambda b, h, i: (b * nq + i, do_off + h))]
    args = [qa, ka, va, dcat]
    kv_out = pl.BlockSpec((S, HP), lambda b, h, i: (b, h))
    out_specs = [pl.BlockSpec((tq, HP), lambda b, h, i: (b * nq + i, h)), kv_out, kv_out]
    out_shape = [_sds((T, HEADS * HP), out_dtype)] * 3
    if use_cum:
        in_specs.append(pl.BlockSpec((None, 8, S), lambda b, h, i: (b, 0, 0)))
        args.append(cum)
        out_specs.append(pl.BlockSpec((None, 1, S), lambda b, h, i: (b * HEADS + h, 0, 0)))
        out_shape.append(_sds((B * HEADS, 1, S), F32))
    return pl.pallas_call(
        body, name="attn_bwd", grid=(B, HEADS, nq), in_specs=in_specs, out_specs=out_specs, out_shape=out_shape,
        scratch_shapes=[pltpu.VMEM((S, HP), F32), pltpu.VMEM((S, HP), F32)],
        compiler_params=_cparams(VMEM_BIG))(*args)


def _tri(n, upper):
    r, c = _iota((n, n), 0), _iota((n, n), 1)
    return ((r <= c) if upper else (r >= c)).astype(F32)


def fox_gate_fwd(proj, bcol, S):
    T = proj.shape[0]
    B = T // S
    ts = _tile(S, 512)
    nt = S // ts

    def body(p_ref, b_ref, o_ref, carry):
        @pl.when(pl.program_id(1) == 0)
        def _():
            carry[...] = jnp.zeros_like(carry)
        cf = jnp.transpose(p_ref[...])[0:8, :]
        lf = jax.nn.log_sigmoid(cf + b_ref[...])
        cum = _mm_hi(lf, _tri(ts, True)) + carry[...]
        o_ref[...] = cum
        carry[...] += jnp.sum(lf, axis=1, keepdims=True)

    return pl.pallas_call(
        body, name="fox_gate_fwd", grid=(B, nt),
        in_specs=[pl.BlockSpec((ts, HP), lambda b, j: (b * nt + j, C_CF // HP)), pl.BlockSpec((8, 1), lambda b, j: (0, 0))],
        out_specs=pl.BlockSpec((None, 8, ts), lambda b, j: (b, 0, j)), out_shape=_sds((B, 8, S), F32),
        scratch_shapes=[pltpu.VMEM((8, 1), F32)],
        compiler_params=_cparams(VMEM_BIG))(proj, bcol)


def fox_gate_bwd(proj, bcol, dcum, S):
    T = proj.shape[0]
    B = T // S
    ts = _tile(S, 512)
    nt = S // ts

    def body(p_ref, b_ref, dc_ref, dp_ref, db_ref, carry):
        b, j = pl.program_id(0), pl.program_id(1)

        @pl.when(j == 0)
        def _():
            carry[...] = jnp.zeros_like(carry)

        @pl.when((b == 0) & (j == 0))
        def _():
            db_ref[...] = jnp.zeros_like(db_ref)

        cf = jnp.transpose(p_ref[...])[0:8, :]
        dc = dc_ref[...]
        dlf = _mm_hi(dc, _tri(ts, False)) + carry[...]
        carry[...] += jnp.sum(dc, axis=1, keepdims=True)
        dcf = dlf * jax.nn.sigmoid(-(cf + b_ref[...]))
        db_ref[...] += jnp.sum(dcf, axis=1, keepdims=True)
        full = jnp.concatenate([dcf, jnp.zeros((HP - 8, ts), F32)], axis=0)
        dp_ref[...] = jnp.transpose(full).astype(BF16)

    def rev(b, j):
        return nt - 1 - j

    return pl.pallas_call(
        body, name="fox_gate_bwd", grid=(B, nt),
        in_specs=[pl.BlockSpec((ts, HP), lambda b, j: (b * nt + rev(b, j), C_CF // HP)),
                  pl.BlockSpec((8, 1), lambda b, j: (0, 0)),
                  pl.BlockSpec((None, 8, ts), lambda b, j: (b, 0, rev(b, j)))],
        out_specs=[pl.BlockSpec((ts, HP), lambda b, j: (b * nt + rev(b, j), 0)), pl.BlockSpec((8, 1), lambda b, j: (0, 0))],
        out_shape=[_sds((T, HP), BF16), _sds((8, 1), F32)],
        scratch_shapes=[pltpu.VMEM((8, 1), F32)],
        compiler_params=_cparams(VMEM_BIG))(proj, bcol, dcum)


def _mla_pre(blk, gq, gkv, wq, wkv, place, cos_q, sin_q, cs_k):
    nq = _rms(blk[:, 0:B_Q_LORA], gq)
    nkv = _rms(blk[:, B_Q_LORA:B_Q_LORA + B_KV_LORA], gkv)
    qq = _mm(nq, wq)
    q = qq[:, 0:HEADS * HP] * cos_q + qq[:, HEADS * HP:] * sin_q
    kv = _mm(nkv, wkv)
    k = kv[:, 0:HEADS * HP] + _mm(blk[:, B_Q_LORA + B_KV_LORA:] * cs_k, place)
    return q, k, kv[:, HEADS * HP:]


def mla_pre_fwd(proj, gq, gkv, wq, wkv, place, cos_q, sin_q, cs_k, S):
    T = proj.shape[0]
    tm = _tile(S, 512)
    tpb = S // tm
    W = HEADS * HP

    def body(p_ref, gq_ref, gkv_ref, wq_ref, wkv_ref, pl_ref, cq_ref, sq_ref, ck_ref, q_ref, k_ref, v_ref):
        q, k, v = _mla_pre(p_ref[...], gq_ref[...], gkv_ref[...], wq_ref[...], wkv_ref[...], pl_ref[...],
                           cq_ref[...], sq_ref[...], ck_ref[...])
        q_ref[...] = q
        k_ref[...] = k
        v_ref[...] = v

    def full(a):
        return pl.BlockSpec(a.shape, lambda i: (0,) * a.ndim)

    tok = pl.BlockSpec((tm, W), lambda i: (i, 0))
    return pl.pallas_call(
        body, name="mla_pre_fwd", grid=(T // tm,),
        in_specs=[pl.BlockSpec((tm, W), lambda i: (i, C_B // W)), full(gq), full(gkv), full(wq), full(wkv), full(place),
                  pl.BlockSpec((tm, W), lambda i: (i % tpb, 0)), pl.BlockSpec((tm, W), lambda i: (i % tpb, 0)),
                  pl.BlockSpec((tm, HP), lambda i: (i % tpb, 0))],
        out_specs=[tok] * 3, out_shape=[_sds((T, W), F32)] * 3,
        compiler_params=_cparams(VMEM_BIG))(proj, gq, gkv, wq, wkv, place, cos_q, sin_q, cs_k)


def mla_pre_bwd(proj, gq, gkv, wq, wkv, place, cos_q, sin_q, cs_k, dq, dk, dv, S):
    T = proj.shape[0]
    tm = _tile(S, 512)
    tpb = S // tm
    W = HEADS * HP

    def body(p_ref, gq_ref, gkv_ref, wq_ref, wkv_ref, pl_ref, cq_ref, sq_ref, ck_ref, dq_ref, dk_ref, dv_ref,
             dp_ref, dgq_ref, dgkv_ref, dwq_ref, dwkv_ref):
        @pl.when(pl.program_id(0) == 0)
        def _():
            for r in (dgq_ref, dgkv_ref, dwq_ref, dwkv_ref):
                r[...] = jnp.zeros_like(r)

        f = functools.partial(_mla_pre, place=pl_ref[...], cos_q=cq_ref[...], sin_q=sq_ref[...], cs_k=ck_ref[...])
        _, vjp = jax.vjp(f, p_ref[...], gq_ref[...], gkv_ref[...], wq_ref[...], wkv_ref[...])
        dp, dgq, dgkv, dwq, dwkv = vjp((dq_ref[...], dk_ref[...], dv_ref[...]))
        dp_ref[...] = dp.astype(BF16)
        dgq_ref[...] += dgq
        dgkv_ref[...] += dgkv
        dwq_ref[...] += dwq
        dwkv_ref[...] += dwkv

    def full(a):
        return pl.BlockSpec(a.shape, lambda i: (0,) * a.ndim)

    tok = pl.BlockSpec((tm, W), lambda i: (i, 0))
    return pl.pallas_call(
        body, name="mla_pre_bwd", grid=(T // tm,),
        in_specs=[pl.BlockSpec((tm, W), lambda i: (i, C_B // W)), full(gq), full(gkv), full(wq), full(wkv), full(place),
                  pl.BlockSpec((tm, W), lambda i: (i % tpb, 0)), pl.BlockSpec((tm, W), lambda i: (i % tpb, 0)),
                  pl.BlockSpec((tm, HP), lambda i: (i % tpb, 0)), tok, tok, tok],
        out_specs=[tok, full(gq), full(gkv), full(wq), full(wkv)],
        out_shape=[_sds((T, W), BF16), _sds(gq.shape, F32), _sds(gkv.shape, F32), _sds(wq.shape, F32), _sds(wkv.shape, F32)],
        compiler_params=_cparams(VMEM_BIG))(proj, gq, gkv, wq, wkv, place, cos_q, sin_q, cs_k, dq, dk, dv)


def _gmlp_block(blk, lg, lb, ws, bs):
    u = jax.nn.gelu(blk[:, 0:GW])
    v = jax.nn.gelu(blk[:, GW:2 * GW])
    mu = jnp.mean(v, axis=-1, keepdims=True)
    var = jnp.mean(jnp.square(v - mu), axis=-1, keepdims=True)
    vn = (v - mu) * lax.rsqrt(var + LN_EPS) * lg + lb
    causal = _iota((D_CHUNK, D_CHUNK), 1) <= _iota((D_CHUNK, D_CHUNK), 0)
    group = _iota((1, GW), 1) // HD
    mixed = jnp.zeros((D_CHUNK, GW), F32)
    for g in range(HEADS):
        part = _mm(jnp.where(causal, ws[g], 0.0), vn) + bs[g]
        mixed = mixed + jnp.where(group == g, part, 0.0)
    return u * mixed


def gmlp_fwd(proj, lg, lb, ws, bs):
    T = proj.shape[0]

    def body(p_ref, lg_ref, lb_ref, ws_ref, bs_ref, o_ref):
        o_ref[...] = _gmlp_block(p_ref[...], lg_ref[...], lb_ref[...], ws_ref[...], bs_ref[...]).astype(BF16)

    def full(a):
        return pl.BlockSpec(a.shape, lambda i: (0,) * a.ndim)

    return pl.pallas_call(
        body, name="gmlp_fwd", grid=(T // D_CHUNK,),
        in_specs=[pl.BlockSpec((D_CHUNK, 2 * GW), lambda i: (i, C_D // (2 * GW))), full(lg), full(lb), full(ws), full(bs)],
        out_specs=pl.BlockSpec((D_CHUNK, GW), lambda i: (i, 0)), out_shape=_sds((T, GW), BF16),
        compiler_params=_cparams(VMEM_BIG))(proj, lg, lb, ws, bs)


def gmlp_bwd(proj, lg, lb, ws, bs, dcat):
    T = proj.shape[0]

    def body(p_ref, lg_ref, lb_ref, ws_ref, bs_ref, do_ref, dp_ref, dlg_ref, dlb_ref, dws_ref, dbs_ref):
        @pl.when(pl.program_id(0) == 0)
        def _():
            for r in (dlg_ref, dlb_ref, dws_ref, dbs_ref):
                r[...] = jnp.zeros_like(r)

        _, vjp = jax.vjp(_gmlp_block, p_ref[...], lg_ref[...], lb_ref[...], ws_ref[...], bs_ref[...])
        dp, dlg, dlb, dws, dbs = vjp(do_ref[...])
        dp_ref[...] = dp.astype(BF16)
        dlg_ref[...] += dlg
        dlb_ref[...] += dlb
        dws_ref[...] += dws
        dbs_ref[...] += dbs

    def full(a):
        return pl.BlockSpec(a.shape, lambda i: (0,) * a.ndim)

    return pl.pallas_call(
        body, name="gmlp_bwd", grid=(T // D_CHUNK,),
        in_specs=[pl.BlockSpec((D_CHUNK, 2 * GW), lambda i: (i, C_D // (2 * GW))), full(lg), full(lb), full(ws), full(bs),
                  pl.BlockSpec((D_CHUNK, GW), lambda i: (i, O_D // GW))],
        out_specs=[pl.BlockSpec((D_CHUNK, 2 * GW), lambda i: (i, 0)), full(lg), full(lb), full(ws), full(bs)],
        out_shape=[_sds((T, 2 * GW), BF16), _sds(lg.shape, F32), _sds(lb.shape, F32), _sds(ws.shape, F32), _sds(bs.shape, F32)],
        compiler_params=_cparams(VMEM_BIG))(proj, lg, lb, ws, bs, dcat)


def _lb_all(logits):
    m = jnp.max(logits, axis=0, keepdims=True)
    e = jnp.exp(logits - m)
    sm = e / jnp.sum(e, axis=0, keepdims=True)
    return jnp.concatenate([sm[0:1] - sm[0:1], (sm[0:1] + sm[1:2]) - sm[0:1]], axis=0)


def lb_fwd(logits):
    def body(l_ref, o_ref):
        o_ref[...] = _lb_all(l_ref[...])

    return pl.pallas_call(body, name="lb_fwd", out_shape=_sds(logits.shape, F32))(logits)


def lb_bwd(logits, dlb):
    def body(l_ref, d_ref, o_ref):
        _, vjp = jax.vjp(_lb_all, l_ref[...])
        o_ref[...] = vjp(d_ref[...])[0]

    return pl.pallas_call(body, name="lb_bwd", out_shape=_sds(logits.shape, F32))(logits, dlb)


def ada_grad(c_all, dmod_cols):
    N = dmod_cols.shape[1]
    tn = _tile(N, 1152)

    def body(c_ref, d_ref, o_ref):
        h = jax.nn.silu(c_ref[...]).astype(BF16)
        o_ref[...] = lax.dot_general(h, d_ref[...].astype(BF16), (((0,), (0,)), ((), ())), preferred_element_type=F32)

    nb = c_all.shape[0]
    return pl.pallas_call(
        body, name="ada_grad", grid=(N // tn,),
        in_specs=[pl.BlockSpec((nb, D), lambda j: (0, 0)), pl.BlockSpec((nb, tn), lambda j: (0, j))],
        out_specs=pl.BlockSpec((D, tn), lambda j: (0, j)), out_shape=_sds((D, N), F32),
        compiler_params=_cparams(VMEM_BIG))(c_all, dmod_cols)


def sum_slots(a, n, name):
    _, R, C = a.shape
    tr = _row_tile(R, C, n)

    def body(a_ref, o_ref):
        acc = a_ref[0]
        for k in range(1, n):
            acc = acc + a_ref[k]
        o_ref[...] = acc

    return pl.pallas_call(
        body, name=name, grid=(R // tr,),
        in_specs=[pl.BlockSpec((n, tr, C), lambda i: (0, i, 0))],
        out_specs=pl.BlockSpec((tr, C), lambda i: (i, 0)), out_shape=_sds((R, C), F32),
        compiler_params=_cparams(VMEM_BIG))(a)


def add2(a, b, name):
    shp = a.shape
    C = shp[-1]
    a2, b2 = a.reshape(-1, C), b.reshape(-1, C)
    R = a2.shape[0]
    tr = _row_tile(R, C)

    def body(a_ref, b_ref, o_ref):
        o_ref[...] = a_ref[...] + b_ref[...]

    spec = pl.BlockSpec((tr, C), lambda i: (i, 0))
    return pl.pallas_call(body, name=name, grid=(R // tr,), in_specs=[spec, spec], out_specs=spec,
                          out_shape=_sds((R, C), F32), compiler_params=_cparams(VMEM_BIG))(a2, b2).reshape(shp)


def _row_tile(R, C=D, n=1, mult=8):
    limit = max(mult, (1 << 18) // (C * n))
    for t in range(limit - limit % mult, mult - 1, -mult):
        if R % t == 0:
            return t
    return R


def adamw(w, g, m, v, name):
    R, C = w.shape
    tr = _row_tile(R, C)
    c1 = 1.0 - ADAM_B1 ** ADAM_STEP
    c2 = 1.0 - ADAM_B2 ** ADAM_STEP

    def body(w_ref, g_ref, m_ref, v_ref, d_ref, nm_ref, nv_ref):
        g_ = g_ref[...]
        nm = ADAM_B1 * m_ref[...] + (1.0 - ADAM_B1) * g_
        nv = ADAM_B2 * v_ref[...] + (1.0 - ADAM_B2) * jnp.square(g_)
        d_ref[...] = -ADAM_LR * ((nm / c1) / (jnp.sqrt(nv / c2) + ADAM_EPS) + ADAM_WD * w_ref[...])
        nm_ref[...] = nm
        nv_ref[...] = nv

    spec = pl.BlockSpec((tr, C), lambda i: (i, 0))
    return pl.pallas_call(body, name=name, grid=(R // tr,), in_specs=[spec] * 4, out_specs=[spec] * 3,
                          out_shape=[_sds((R, C), F32)] * 3, compiler_params=_cparams(VMEM_BIG))(w, g, m, v)


def _rot_cols(w):
    return jnp.concatenate([-w[:, 16:32], w[:, 0:16]], axis=1)


def _fold_rot(d):
    return jnp.concatenate([d[:, 16:32], -d[:, 0:16]], axis=1)


def _pad_heads(w, off, axis):
    parts = []
    for h in range(HEADS):
        piece = lax.slice_in_dim(w, off + HD * h, off + HD * (h + 1), axis=axis)
        parts += [piece, jnp.zeros_like(piece)]
    return parts


def _unpad_heads(d, off, axis):
    return [lax.slice_in_dim(d, off + HP * h, off + HP * h + HD, axis=axis) for h in range(HEADS)]


def mix_in_ext(w):
    z = lambda n: jnp.zeros((w.shape[0], n), w.dtype)
    kr = w[:, 1408:1440]
    cols = [w[:, 0:1408], kr, _rot_cols(kr), z(64)]
    cols += _pad_heads(w, 1440, 1) + _pad_heads(w, 1696, 1) + _pad_heads(w, 1952, 1)
    cols += [w[:, 2212:2724], w[:, 2208:2212], z(NP - C_CF - HEADS)]
    return jnp.concatenate(cols, axis=1)


def mix_in_unext(d):
    kr = d[:, 1408:1440] + _fold_rot(d[:, 1440:1472])
    cols = [d[:, 0:1408], kr] + _unpad_heads(d, C_CQ, 1) + _unpad_heads(d, C_CK, 1) + _unpad_heads(d, C_CV, 1)
    cols += [d[:, C_CF:C_CF + HEADS], d[:, C_D:C_D + 2 * GW]]
    return jnp.concatenate(cols, axis=1)


def mix_out_ext(w):
    return jnp.concatenate([w[0:GW]] + _pad_heads(w, GW, 0) + _pad_heads(w, 2 * GW, 0) + [w[3 * GW:4 * GW]], axis=0)


def mix_out_unext(d):
    return jnp.concatenate([d[0:GW]] + _unpad_heads(d, O_B, 0) + _unpad_heads(d, O_C, 0) + [d[O_D:O_D + GW]], axis=0)


def uq_ext(w):
    z = lambda n: jnp.zeros((w.shape[0], n), w.dtype)
    a, b = [], []
    for h in range(HEADS):
        o = (B_NOPE + B_ROPE) * h
        a += [w[:, o:o + B_NOPE + B_ROPE], z(32)]
        b += [z(B_NOPE), _rot_cols(w[:, o + B_NOPE:o + B_NOPE + B_ROPE]), z(32)]
    return jnp.concatenate(a + b, axis=1)


def uq_unext(d):
    cols = []
    for h in range(HEADS):
        o = HP * h
        cols += [d[:, o:o + B_NOPE], d[:, o + B_NOPE:o + B_NOPE + B_ROPE]
                 + _fold_rot(d[:, HEADS * HP + o + B_NOPE:HEADS * HP + o + B_NOPE + B_ROPE])]
    return jnp.concatenate(cols, axis=1)


def ukv_ext(w):
    z = jnp.zeros((w.shape[0], HD), w.dtype)
    k, v = [], []
    for h in range(HEADS):
        k += [w[:, 2 * HD * h:2 * HD * h + HD], z]
        v += [w[:, 2 * HD * h + HD:2 * HD * (h + 1)], z]
    return jnp.concatenate(k + v, axis=1)


def ukv_unext(d):
    cols = []
    for h in range(HEADS):
        cols += [d[:, HP * h:HP * h + HD], d[:, HEADS * HP + HP * h:HEADS * HP + HP * h + HD]]
    return jnp.concatenate(cols, axis=1)


def rope_tables(S):
    half = B_ROPE // 2
    inv_freq = ROPE_THETA ** (-jnp.arange(half, dtype=F32) / half)
    ang = jnp.arange(S).astype(F32)[:, None] * inv_freq[None, :]
    cos = jnp.tile(jnp.cos(ang), (1, 2))
    sin = jnp.tile(jnp.sin(ang), (1, 2))
    one, zero = jnp.ones((S, B_NOPE), F32), jnp.zeros((S, B_NOPE), F32)
    z32 = jnp.zeros((S, 32), F32)
    cos_q = jnp.tile(jnp.concatenate([one, cos, z32], axis=1), (1, HEADS))
    sin_q = jnp.tile(jnp.concatenate([zero, sin, z32], axis=1), (1, HEADS))
    cs_k = jnp.concatenate([cos, sin, zero], axis=1)
    place = np.zeros((HP, HEADS * HP), np.float32)
    for h in range(HEADS):
        for j in range(B_ROPE):
            place[j, h * HP + B_NOPE + j] = 1.0
            place[B_ROPE + j, h * HP + B_NOPE + j] = 1.0
    return cos_q, sin_q, cs_k, jnp.asarray(place, BF16)


def layer_fwd(x, mod, get, tabs, S):
    cos_q, sin_q, cs_k, place = tabs
    p = dict(get("ffn1", x))
    l = p["wl"]
    zg1, zu1, act1 = ffn_in_fwd(x, mod, p["ffn1_in"], l, 0, 1, S)
    y1, x1 = out_ln_fwd(act1, p["ffn1_out"], x, mod, p["ln_g"][0:1], p["ln_b"][0:1], 2, 0.5, S, l)
    p.update(get("mix", x1))
    proj = mix_in_fwd(x1, mod, p["mix_in"], 3, 4, S)
    o_a, states = hgrn_fwd(proj, p["lb"], p["ng"], S)
    q_b, k_b, v_b = mla_pre_fwd(proj, p["gq"], p["gkv"], p["wq"], p["wkv"], place, cos_q, sin_q, cs_k, S)
    o_b = attn_fwd(q_b, 0, k_b, 0, v_b, 0, None, (B_NOPE + B_ROPE) ** -0.5, S)
    cum = fox_gate_fwd(proj, p["bcol"], S)
    o_c = attn_fwd(proj, C_CQ // HP, proj, C_CK // HP, proj, C_CV // HP, cum, HD ** -0.5, S)
    o_d = gmlp_fwd(proj, p["g_lg"], p["g_lb"], p["ws"], p["bs"])
    cat = jnp.concatenate([o_a, o_b, o_c, o_d], axis=1)
    y2, x2 = out_ln_fwd(cat, p["mix_out"], x1, mod, p["ln_g"][1:2], p["ln_b"][1:2], 5, 1.0, S)
    p.update(get("ffn2", x2))
    zg3, zu3, act3 = ffn_in_fwd(x2, mod, p["ffn2_in"], l, 6, 7, S)
    y3, x3 = out_ln_fwd(act3, p["ffn2_out"], x2, mod, p["ln_g"][2:3], p["ln_b"][2:3], 8, 0.5, S, l)
    saved = dict(x=x, zg1=zg1, zu1=zu1, act1=act1, y1=y1, x1=x1, proj=proj, states=states, q_b=q_b, k_b=k_b, v_b=v_b,
                 cum=cum, cat=cat, y2=y2, x2=x2, zg3=zg3, zu3=zu3, act3=act3, y3=y3, p=p)
    return x3, saved


def _ffn_bwd(dxn, x_in, y, zg, zu, act, mod, w_in, w_out, l, lg, lb, idx, S, emit):
    sh, sc, gate = idx
    dres, dy, dgate, dlg, dlb = ln_res_bwd(dxn, x_in, y, mod, lg, lb, gate, 0.5, S)
    dzg, dzu = swiglu_bwd(dy, w_out, l, zg, zu, S)
    dw_out = tn_mm(act, dy, FF // 2).reshape(N_CHIPS, FF // N_CHIPS, D)
    dw_in = tn_mm_mod_shards(x_in, mod, dzg, dzu, sh, sc, S)
    mod = mod + emit(dw_in, dw_out)
    dx, dsh, dsc = nt_mod_bwd([dzg, dzu], w_in, [0, 2], x_in, mod, dres, sc, S, FF // 2, l)
    return dx, dw_in, dw_out, dlg, dlb, {sh: dsh, sc: dsc, gate: dgate}, mod


def layer_bwd(dx3, mod, sv, tabs, S, emit):
    cos_q, sin_q, cs_k, place = tabs
    p = sv["p"]
    l = p["wl"]
    g = {}
    dm = {}

    def emit_ffn(part):
        def f(dw_in, dw_out):
            g[part + "_in"], g[part + "_out"] = dw_in, dw_out
            return emit(part, g)
        return f

    dx2, _, _, dlg2, dlb2, d, mod = _ffn_bwd(
        dx3, sv["x2"], sv["y3"], sv["zg3"], sv["zu3"], sv["act3"], mod, p["ffn2_in"], p["ffn2_out"], l,
        p["ln_g"][2:3], p["ln_b"][2:3], (6, 7, 8), S, emit_ffn("ffn2"))
    dm.update(d)
    dres, dy2, dm[5], dlg1, dlb1 = ln_res_bwd(dx2, sv["x1"], sv["y2"], mod, p["ln_g"][1:2], p["ln_b"][1:2], 5, 1.0, S)
    dcat = nt_plain(dy2, p["mix_out"])
    g["mix_out"] = tn_mm(sv["cat"], dy2, 768)
    proj = sv["proj"]
    d_a, g["lb"], g["ng"] = hgrn_bwd(proj, sv["states"], dcat, p["lb"], p["ng"], S)
    dq_c, dk_c, dv_c, dcum = attn_bwd(proj, C_CQ // HP, proj, C_CK // HP, proj, C_CV // HP, sv["cum"], dcat,
                                      O_C // HP, HD ** -0.5, S, BF16)
    B = proj.shape[0] // S
    dcum = jnp.concatenate([dcum.reshape(B, HEADS, S), jnp.zeros((B, 8 - HEADS, S), F32)], axis=1)
    d_cf, g["bcol"] = fox_gate_bwd(proj, p["bcol"], dcum, S)
    dq_b, dk_b, dv_b = attn_bwd(sv["q_b"], 0, sv["k_b"], 0, sv["v_b"], 0, None, dcat, O_B // HP,
                                (B_NOPE + B_ROPE) ** -0.5, S, F32)
    d_b, g["gq"], g["gkv"], g["wq"], g["wkv"] = mla_pre_bwd(
        proj, p["gq"], p["gkv"], p["wq"], p["wkv"], place, cos_q, sin_q, cs_k, dq_b, dk_b, dv_b, S)
    d_d, g["g_lg"], g["g_lb"], g["ws"], g["bs"] = gmlp_bwd(proj, p["g_lg"], p["g_lb"], p["ws"], p["bs"], dcat)
    dproj = jnp.concatenate([d_a, d_b, dq_c, dk_c, dv_c, d_d, d_cf, jnp.zeros_like(d_cf)], axis=1)
    g["mix_in"] = tn_mm_mod(sv["x1"], mod, dproj, 3, 4, S, NP_TILE)
    mod = mod + emit("mix", g)
    dx1, dm[3], dm[4] = nt_mod_bwd([dproj], p["mix_in"], [0], sv["x1"], mod, dres, 4, S, NP_TILE)
    last = []

    def emit_last(dw_in, dw_out):
        last.append(emit_ffn("ffn1")(dw_in, dw_out))
        return last[0]

    dx0, _, _, dlg0, dlb0, d, mod = _ffn_bwd(
        dx1, sv["x"], sv["y1"], sv["zg1"], sv["zu1"], sv["act1"], mod, p["ffn1_in"], p["ffn1_out"], l,
        p["ln_g"][0:1], p["ln_b"][0:1], (0, 1, 2), S, emit_last)
    dm.update(d)
    g["ln_g"] = jnp.concatenate([dlg0, dlg1, dlg2], axis=0)
    g["ln_b"] = jnp.concatenate([dlb0, dlb1, dlb2], axis=0)
    dmod = jnp.concatenate([dm[i] for i in range(N_MOD)], axis=1)
    return dx0, dmod, g, last[0]


def local_step(x, c8, tgt, get, lb_logits, S, emit=None):
    B = x.shape[0] // S
    tabs = rope_tables(S)
    lb_all = lb_fwd(lb_logits)
    mods, saved = [], []
    h = x
    for l in range(DEPTH):
        pa = get(l, "ada", h)
        mod = mod_fwd(c8, pa["ada_w"], pa["wl"], pa["ada_b"])[0:B].reshape(B, N_MOD, D)

        def get_l(part, after, l=l):
            p = dict(get(l, part, after))
            if part == "mix":
                p["lb"] = lb_all[l:l + 1]
            return p

        h, sv = layer_fwd(h, mod, get_l, tabs, S)
        mods.append(mod)
        saved.append(sv)
    loss_tile, dh = loss_head(h, tgt)
    grads, dmods, dlb = [None] * DEPTH, [None] * DEPTH, [None] * DEPTH
    tie = jnp.zeros((), F32)
    for l in reversed(range(DEPTH)):
        emit_l = (lambda part, g: jnp.zeros((), F32)) if emit is None else functools.partial(emit, l)
        dh, dmods[l], grads[l], tie = layer_bwd(dh, mods[l] + tie, saved[l], tabs, S, emit_l)
        dlb[l] = grads[l].pop("lb")
    d_logits = lb_bwd(lb_logits, jnp.concatenate(dlb, axis=0))
    return loss_tile, dh, dmods, grads, d_logits


ANY = pl.BlockSpec(memory_space=pl.ANY)


def _place():
    x, y, c = lax.axis_index("x"), lax.axis_index("y"), lax.axis_index("c")
    chips = [(1 - x, y), (x, 1 - y), (1 - x, 1 - y)]
    return x, y, c, chips


def _rcopy(src, dst, sems, k, to):
    send_sems, recv_sems = sems
    return pltpu.make_async_remote_copy(src_ref=src, dst_ref=dst, send_sem=send_sems.at[k], recv_sem=recv_sems.at[k],
                                        device_id=to, device_id_type=MESH)


def _dma_sems(n_remote, n_local):
    return [pltpu.SemaphoreType.DMA((n_remote,)), pltpu.SemaphoreType.DMA((n_remote,)), pltpu.SemaphoreType.DMA((n_local,))]


def own_slot(src, chip):
    L = src.shape[0]
    return lax.dynamic_update_slice(jnp.zeros((L, N_CHIPS) + src.shape[1:], src.dtype), src[:, None], (0, chip, 0, 0))


def ag_shards(arrs, lands):
    n = len(arrs)
    rh = [a.shape[1] // 2 for a in arrs]

    def body(*refs):
        srcs, outs, token = refs[:n], refs[2 * n:3 * n], refs[3 * n]
        send_sems, recv_sems = refs[3 * n + 1:]
        x, y, c, chips = _place()
        sems = (send_sems, recv_sems)
        me = 2 * x + y
        sibling = (x, y, 1 - c)
        token[...] = jnp.zeros_like(token)

        def part(i, k, hc):
            return outs[i].at[:, k, pl.ds(hc * rh[i], rh[i]), :]

        started = []
        for j, (px, py) in enumerate(chips):
            for i in range(n):
                cp = _rcopy(srcs[i].at[:, pl.ds(c * rh[i], rh[i]), :], part(i, me, c), sems, 6 * i + j, (px, py, c))
                cp.start()
                started.append(cp)
        for j, (px, py) in enumerate(chips):
            k = 2 * px + py
            for i in range(n):
                _rcopy(part(i, k, c), part(i, k, c), sems, 6 * i + j, (px, py, c)).wait_recv()
                cp = _rcopy(part(i, k, c), part(i, k, c), sems, 6 * i + 3 + j, sibling)
                cp.start()
                started.append(cp)
        for j, (px, py) in enumerate(chips):
            k = 2 * px + py
            for i in range(n):
                _rcopy(part(i, k, 1 - c), part(i, k, 1 - c), sems, 6 * i + 3 + j, sibling).wait_recv()
        for cp in started:
            cp.wait_send()

    outs = pl.pallas_call(
        body, name="ag_shards", out_shape=[_sds(a.shape, a.dtype) for a in lands] + [_sds((8, 128), F32)],
        in_specs=[ANY] * (2 * n), out_specs=[ANY] * n + [pl.BlockSpec(memory_space=pltpu.VMEM)],
        input_output_aliases={n + i: i for i in range(n)}, scratch_shapes=_dma_sems(6 * n, 1)[:2])(*arrs, *lands)
    return list(outs[:n]), outs[n]


HBM_SPEC = pl.BlockSpec(memory_space=pltpu.HBM)
SEM_SPEC = pl.BlockSpec(memory_space=pltpu.SEMAPHORE)
DATAFLOW = pltpu.SideEffectType.DATAFLOW_SIDE_EFFECTING


def _after(x, dep):
    return lax.optimization_barrier((x, dep))[0]


def _split_start(srcs, lands, copies, name):
    n, m = len(srcs), len(lands)

    def body(*refs):
        ins = refs[:n + m]
        send_sems, recv_sems = refs[n + m], refs[n + m + 1]
        token = refs[-1]
        for k, (src, dst, to) in enumerate(copies(ins[:n], ins[n:], _place())):
            pltpu.make_async_remote_copy(src_ref=src, dst_ref=dst, send_sem=send_sems.at[k], recv_sem=recv_sems.at[k],
                                         device_id=to, device_id_type=MESH).start()
        token[...] = jnp.zeros_like(token)

    n_copies = 3 * n
    arrs = list(srcs) + list(lands)
    outs = pl.pallas_call(
        body, name=name,
        out_shape=(pltpu.SemaphoreType.DMA((n_copies,)), pltpu.SemaphoreType.DMA((n_copies,)),
                   *[pltpu.HBM(a.shape, a.dtype) for a in arrs], _sds((8, 128), F32)),
        in_specs=[HBM_SPEC] * (n + m),
        out_specs=(SEM_SPEC, SEM_SPEC, *[HBM_SPEC] * (n + m), pl.BlockSpec(memory_space=pltpu.VMEM)),
        input_output_aliases={i: 2 + i for i in range(n + m)},
        compiler_params=pltpu.CompilerParams(has_side_effects=DATAFLOW),
    )(*[pltpu.with_memory_space_constraint(a, pltpu.HBM) for a in arrs])
    return outs[0], outs[1], list(outs[2:2 + n]), list(outs[2 + n:2 + n + m]), outs[-1]


def _split_wait(handle, arrivals, after, name):
    send_sems, recv_sems, srcs, lands, _ = handle
    n, m = len(srcs), len(lands)

    def body(*refs):
        ins = refs[:n + m]
        send_sems, recv_sems = refs[n + m], refs[n + m + 1]
        x, y, c, chips = place = _place()
        for k, (src, dst) in enumerate(arrivals(ins[:n], ins[n:], place)):
            cp = pltpu.make_async_remote_copy(src_ref=src, dst_ref=dst, send_sem=send_sems.at[k], recv_sem=recv_sems.at[k],
                                              device_id=(x, y, 1 - c), device_id_type=MESH)
            cp.wait_send()
            cp.wait_recv()

    arrs = list(srcs) + list(lands)
    outs = pl.pallas_call(
        body, name=name, out_shape=[pltpu.HBM(a.shape, a.dtype) for a in arrs],
        in_specs=[HBM_SPEC] * (n + m) + [SEM_SPEC, SEM_SPEC, ANY], out_specs=[HBM_SPEC] * (n + m),
        input_output_aliases={i: i for i in range(n + m)},
        compiler_params=pltpu.CompilerParams(has_side_effects=DATAFLOW),
    )(*arrs, send_sems, recv_sems, after)
    return list(outs[n:])


def _ag_part(ref, k, hc):
    rh = ref.shape[2] // 2
    return ref.at[:, k, pl.ds(hc * rh, rh), :]


def ag_start(srcs, lands, name):
    def copies(s, d, place):
        x, y, c, chips = place
        out = []
        for j, (px, py) in enumerate(chips):
            for i in range(len(s)):
                rh = s[i].shape[1] // 2
                out.append((s[i].at[:, pl.ds(c * rh, rh), :], _ag_part(d[i], 2 * x + y, c), (px, py, c)))
        return out

    return _split_start(srcs, lands, copies, name)


def ag_wait(handle, after, name):
    def arrivals(s, d, place):
        x, y, c, chips = place
        out = []
        for j, (px, py) in enumerate(chips):
            for i in range(len(s)):
                rh = s[i].shape[1] // 2
                out.append((s[i].at[:, pl.ds(c * rh, rh), :], _ag_part(d[i], 2 * px + py, c)))
        return out

    return _split_wait(handle, arrivals, after, name)


def ag_forward(lands, name):
    n = len(lands)

    def body(*refs):
        bufs = refs[n:2 * n]
        send_sems, recv_sems = refs[2 * n:]
        x, y, c, chips = _place()
        sems = (send_sems, recv_sems)
        cps = []
        for j, (px, py) in enumerate(chips):
            for i in range(n):
                part = _ag_part(bufs[i], 2 * px + py, c)
                cps.append(_rcopy(part, part, sems, 3 * i + j, (x, y, 1 - c)))
        for cp in cps:
            cp.start()
        for j, (px, py) in enumerate(chips):
            for i in range(n):
                part = _ag_part(bufs[i], 2 * px + py, 1 - c)
                _rcopy(part, part, sems, 3 * i + j, (x, y, 1 - c)).wait_recv()
        for cp in cps:
            cp.wait_send()

    return pl.pallas_call(
        body, name=name, out_shape=[_sds(a.shape, a.dtype) for a in lands], in_specs=[ANY] * n, out_specs=[ANY] * n,
        input_output_aliases={i: i for i in range(n)}, scratch_shapes=_dma_sems(3 * n, 1)[:2])(*lands)


def rs_start(hs, lands, name):
    def copies(s, d, place):
        x, y, c, chips = place
        return [(s[i].at[2 * px + py], d[i].at[2 * x + y], (px, py, c)) for j, (px, py) in enumerate(chips) for i in range(len(s))]

    return _split_start(hs, lands, copies, name)


def rs_wait(handle, after, name):
    def arrivals(s, d, place):
        x, y, c, chips = place
        return [(s[i].at[2 * px + py], d[i].at[2 * px + py]) for j, (px, py) in enumerate(chips) for i in range(len(s))]

    return _split_wait(handle, arrivals, after, name)


def sibling_swap(arrs, name):
    n = len(arrs)
    rh = [a.shape[1] // 2 for a in arrs]

    def body(*refs):
        srcs, outs = refs[:n], refs[n:2 * n]
        send_sems, recv_sems = refs[2 * n:]
        x, y, c, _ = _place()
        cps = [_rcopy(srcs[i].at[:, pl.ds((1 - c) * rh[i], rh[i]), :], outs[i], (send_sems, recv_sems), i, (x, y, 1 - c))
               for i in range(n)]
        for cp in cps:
            cp.start()
        for cp in cps:
            cp.wait()

    return pl.pallas_call(
        body, name=name, out_shape=[_sds((N_CHIPS, r, a.shape[2]), a.dtype) for a, r in zip(arrs, rh)],
        in_specs=[ANY] * n, out_specs=[ANY] * n, scratch_shapes=_dma_sems(n, 1)[:2])(*arrs)


def chip_exchange(hs):
    n = len(hs)

    def body(*refs):
        srcs, outs = refs[:n], refs[n:2 * n]
        send_sems, recv_sems, loc_sems = refs[2 * n:]
        x, y, c, chips = _place()
        sems = (send_sems, recv_sems)
        me = 2 * x + y
        mine = [pltpu.make_async_copy(srcs[i].at[me], outs[i].at[me], loc_sems.at[i]) for i in range(n)]
        for cp in mine:
            cp.start()
        sends = []
        for j, (px, py) in enumerate(chips):
            for i in range(n):
                cp = _rcopy(srcs[i].at[2 * px + py], outs[i].at[me], sems, 3 * i + j, (px, py, c))
                cp.start()
                sends.append(cp)
        for j, (px, py) in enumerate(chips):
            for i in range(n):
                _rcopy(srcs[i].at[2 * px + py], outs[i].at[2 * px + py], sems, 3 * i + j, (px, py, c)).wait_recv()
        for cp in sends:
            cp.wait_send()
        for cp in mine:
            cp.wait()

    return pl.pallas_call(
        body, name="chip_exchange", out_shape=[_sds(h.shape, h.dtype) for h in hs],
        in_specs=[ANY] * n, out_specs=[ANY] * n, scratch_shapes=_dma_sems(3 * n, n))(*hs)


def sum_into(land, base, l, core, name):
    _, rh, C = land.shape
    tr = _row_tile(rh, C, N_CHIPS, mult=16)
    nr = rh // tr

    def body(core_ref, land_ref, base_ref, o_ref):
        acc = land_ref[0].astype(F32)
        for k in range(1, N_CHIPS):
            acc = acc + land_ref[k].astype(F32)
        o_ref[...] = acc

    grid_spec = pltpu.PrefetchScalarGridSpec(
        num_scalar_prefetch=1, grid=(nr,),
        in_specs=[pl.BlockSpec((N_CHIPS, tr, C), lambda r, core_ref: (0, r, 0)), ANY],
        out_specs=pl.BlockSpec((None, tr, C), lambda r, core_ref: (l, core_ref[0] * nr + r, 0)))
    return pl.pallas_call(body, name=name, grid_spec=grid_spec, out_shape=_sds(base.shape, base.dtype),
                          input_output_aliases={2: 0}, compiler_params=_cparams(VMEM_BIG))(
        core.reshape(1).astype(jnp.int32), land, base)


def sibling_join(bases, name):
    n = len(bases)

    def body(*refs):
        bufs = refs[n:2 * n]
        send_sems, recv_sems = refs[2 * n:]
        x, y, c, _ = _place()
        sems = (send_sems, recv_sems)

        def half(i, hc):
            rh = bufs[i].shape[1] // 2
            return bufs[i].at[:, pl.ds(hc * rh, rh), :]

        sends = [_rcopy(half(i, c), half(i, c), sems, i, (x, y, 1 - c)) for i in range(n)]
        for cp in sends:
            cp.start()
        for i in range(n):
            _rcopy(half(i, 1 - c), half(i, 1 - c), sems, i, (x, y, 1 - c)).wait_recv()
        for cp in sends:
            cp.wait_send()

    return pl.pallas_call(
        body, name=name, out_shape=[_sds(b.shape, b.dtype) for b in bases], in_specs=[ANY] * n, out_specs=[ANY] * n,
        input_output_aliases={i: i for i in range(n)}, scratch_shapes=_dma_sems(n, 1)[:2])(*bases)


def ag_all(blk):
    M, C = blk.shape

    def body(x_ref, out_ref, send_sems, recv_sems, loc_sem):
        x, y, c, chips = _place()
        sems = (send_sems, recv_sems)
        me, sibling = (x, y, c), (x, y, 1 - c)

        def slot(px, py, pc):
            return out_ref.at[4 * px + 2 * py + pc]

        mine = pltpu.make_async_copy(x_ref, slot(*me), loc_sem)
        mine.start()
        first = [_rcopy(x_ref, slot(*me), sems, 0, sibling)]
        first += [_rcopy(x_ref, slot(*me), sems, 1 + j, (*chip, c)) for j, chip in enumerate(chips)]
        for cp in first:
            cp.start()
        passed = [_rcopy(slot(*chip, c), slot(*chip, c), sems, 4 + j, sibling) for j, chip in enumerate(chips)]
        for j, chip in enumerate(chips):
            _rcopy(slot(*chip, c), slot(*chip, c), sems, 1 + j, me).wait_recv()
            passed[j].start()
        _rcopy(slot(*sibling), slot(*sibling), sems, 0, me).wait_recv()
        for j, chip in enumerate(chips):
            _rcopy(slot(*chip, 1 - c), slot(*chip, 1 - c), sems, 4 + j, me).wait_recv()
        for cp in first + passed:
            cp.wait_send()
        mine.wait()

    return pl.pallas_call(
        body, name="ag_all", out_shape=_sds((8, M, C), blk.dtype),
        in_specs=[pl.BlockSpec(memory_space=pltpu.VMEM)], out_specs=pl.BlockSpec(memory_space=pltpu.VMEM),
        scratch_shapes=[pltpu.SemaphoreType.DMA((7,)), pltpu.SemaphoreType.DMA((7,)), pltpu.SemaphoreType.DMA(())],
        compiler_params=_cparams(VMEM_BIG))(blk)


WEIGHTS = ["ada_w", "ada_b", "ln_g", "ln_b", "ffn1_w_in", "ffn1_w_out", "ffn2_w_in", "ffn2_w_out", "mix_w_in", "mix_w_out",
           "hgrn_lb_logits", "hgrn_norm_g", "mla_q_norm_g", "mla_kv_norm_g", "mla_w_uq", "mla_w_ukv", "fox_b_f",
           "gmlp_ln_g", "gmlp_ln_b", "gmlp_w_s", "gmlp_b_s"]
SHARDED = {"ffn1_w_in": 1, "ffn1_w_out": 0, "ffn2_w_in": 1, "ffn2_w_out": 0, "mix_w_in": 1, "mix_w_out": 0,
           "mla_w_uq": 1, "mla_w_ukv": 1}
SMALL = ["hgrn_lb_logits", "hgrn_norm_g", "mla_q_norm_g", "mla_kv_norm_g", "fox_b_f", "gmlp_ln_g", "gmlp_ln_b",
         "gmlp_w_s", "gmlp_b_s", "ln_g", "ln_b"]
GATHERED = ["ada_w", "ffn1_w_in", "ffn1_w_out", "ffn2_w_in", "ffn2_w_out", "mix_w_in", "mix_w_out", "mla_w_uq", "mla_w_ukv"]
REDUCED = GATHERED[1:]


def _col_shards(a):
    cols = a.shape[1] // N_CHIPS
    return jnp.stack([a[:, k * cols:(k + 1) * cols] for k in range(N_CHIPS)])


def add_kept_half(a, got, core, name):
    _, R, C = a.shape
    rh = R // 2
    tr = _row_tile(rh, C, mult=16)
    nr = rh // tr

    def body(core_ref, a_ref, b_ref, o_ref):
        o_ref[...] = (a_ref[...] + b_ref[...]).astype(o_ref.dtype)

    half = pl.BlockSpec((None, tr, C), lambda k, r, core_ref: (k, r, 0))
    grid_spec = pltpu.PrefetchScalarGridSpec(
        num_scalar_prefetch=1, grid=(N_CHIPS, nr),
        in_specs=[pl.BlockSpec((None, tr, C), lambda k, r, core_ref: (k, core_ref[0] * nr + r, 0)), half],
        out_specs=half)
    return pl.pallas_call(body, name=name, grid_spec=grid_spec, out_shape=_sds((N_CHIPS, rh, C), BF16),
                          compiler_params=_cparams(VMEM_BIG))(core.reshape(1).astype(jnp.int32), a, got)


def _rows(parts, n_rows, dtype):
    flat = jnp.concatenate([p.reshape(-1) for p in parts])
    pad = n_rows * D - flat.shape[0]
    return jnp.concatenate([flat, jnp.zeros((pad,), dtype)]).reshape(n_rows, D)


def _take(flat, shapes):
    out, o = [], 0
    for shp in shapes:
        n = int(np.prod(shp))
        out.append(flat[o:o + n].reshape(shp))
        o += n
    return out


def _round_up(n, m):
    return -(-n // m) * m


def pack_shard(w):
    parts = [w[n][l] for l in range(DEPTH) for n in SHARDED] + [w[n][l] for l in range(DEPTH) for n in ("ln_g", "ln_b")]
    n = sum(int(np.prod(p.shape)) for p in parts)
    return _rows(parts, _round_up(-(-n // D), 16), F32)


def unpack_shard(pk, like):
    shapes = [like[n].shape[1:] for l in range(DEPTH) for n in SHARDED] + [like[n].shape[1:] for l in range(DEPTH) for n in ("ln_g", "ln_b")]
    pieces = _take(pk.reshape(-1), shapes)
    names = [n for l in range(DEPTH) for n in SHARDED] + [n for l in range(DEPTH) for n in ("ln_g", "ln_b")]
    out = {}
    for n in list(SHARDED) + ["ln_g", "ln_b"]:
        out[n] = jnp.stack([p for p, m in zip(pieces, names) if m == n])
    return out


def pack_small(w):
    parts = [w[n][l] for l in range(DEPTH) for n in SMALL]
    n = sum(int(np.prod(p.shape)) for p in parts)
    return _rows(parts, _round_up(-(-n // D), 8), F32)


def unpack_small(pk, like):
    shapes = [like[n].shape[1:] for l in range(DEPTH) for n in SMALL]
    pieces = _take(pk.reshape(-1), shapes)
    names = [n for l in range(DEPTH) for n in SMALL]
    return {n: jnp.stack([p for p, m in zip(pieces, names) if m == n]) for n in SMALL}


def pack_gather(w):
    parts = [w[n][l].astype(BF16) for l in range(DEPTH) for n in ["ada_w"] + list(SHARDED)]
    ln = jnp.concatenate([w[n][l].reshape(-1) for l in range(DEPTH) for n in ("ln_g", "ln_b")])
    parts.append(lax.bitcast_convert_type(ln, BF16))
    n = sum(int(np.prod(p.shape)) for p in parts)
    return _rows(parts, _round_up(-(-n // D), 16), BF16)


def unpack_gather(g, w):
    names = ["ada_w"] + list(SHARDED)
    shapes = [w[n].shape[1:] for l in range(DEPTH) for n in names]
    n_ln = DEPTH * 2 * 3 * (D // N_CHIPS)
    flat = g.reshape(N_CHIPS, -1)
    per_chip = [_take(flat[k], shapes + [(n_ln, 2)]) for k in range(N_CHIPS)]
    layers = [dict() for _ in range(DEPTH)]
    i = 0
    for l in range(DEPTH):
        for n in names:
            axis = 1 if n == "ada_w" else SHARDED[n]
            layers[l][n] = jnp.concatenate([per_chip[k][i] for k in range(N_CHIPS)], axis=axis)
            i += 1
    ln = [lax.bitcast_convert_type(per_chip[k][i], F32).reshape(DEPTH, 2, 3, D // N_CHIPS) for k in range(N_CHIPS)]
    ln = jnp.concatenate(ln, axis=3)
    for l in range(DEPTH):
        layers[l]["ln_g"], layers[l]["ln_b"] = ln[l, 0], ln[l, 1]
    return layers


def pack_grads(grads, k):
    parts = []
    for l in range(DEPTH):
        g = grads[l]
        full = {"ffn1_w_out": g["ffn1_out"], "ffn2_w_out": g["ffn2_out"], "mix_w_in": mix_in_unext(g["mix_in"]),
                "mix_w_out": mix_out_unext(g["mix_out"]), "mla_w_uq": uq_unext(g["wq"]), "mla_w_ukv": ukv_unext(g["wkv"])}
        for n, axis in SHARDED.items():
            if n in ("ffn1_w_in", "ffn2_w_in"):
                half = g[n.replace("_w_in", "_in")][k // 2]
                parts.append(half[:, (k % 2) * (FF // 2):(k % 2 + 1) * (FF // 2)])
            else:
                sz = full[n].shape[axis] // N_CHIPS
                parts.append(lax.slice_in_dim(full[n], k * sz, (k + 1) * sz, axis=axis))
    for l in range(DEPTH):
        for n in ("ln_g", "ln_b"):
            parts.append(grads[l][n][:, k * (D // N_CHIPS):(k + 1) * (D // N_CHIPS)])
    n = sum(int(np.prod(p.shape)) for p in parts)
    return _rows(parts, _round_up(-(-n // D), 16), F32)


def kernel(x, c, ada_w, ada_b, ln_g, ln_b, ffn1_w_in, ffn1_w_out, ffn2_w_in, ffn2_w_out, mix_w_in, mix_w_out, hgrn_lb_logits, hgrn_norm_g, mla_q_norm_g, mla_kv_norm_g, mla_w_uq, mla_w_ukv, fox_b_f, gmlp_ln_g, gmlp_ln_b, gmlp_w_s, gmlp_b_s, loss_target, m_ada_w, m_ada_b, m_ln_g, m_ln_b, m_ffn1_w_in, m_ffn1_w_out, m_ffn2_w_in, m_ffn2_w_out, m_mix_w_in, m_mix_w_out, m_hgrn_lb_logits, m_hgrn_norm_g, m_mla_q_norm_g, m_mla_kv_norm_g, m_mla_w_uq, m_mla_w_ukv, m_fox_b_f, m_gmlp_ln_g, m_gmlp_ln_b, m_gmlp_w_s, m_gmlp_b_s, v_ada_w, v_ada_b, v_ln_g, v_ln_b, v_ffn1_w_in, v_ffn1_w_out, v_ffn2_w_in, v_ffn2_w_out, v_mix_w_in, v_mix_w_out, v_hgrn_lb_logits, v_hgrn_norm_g, v_mla_q_norm_g, v_mla_kv_norm_g, v_mla_w_uq, v_mla_w_ukv, v_fox_b_f, v_gmlp_ln_g, v_gmlp_ln_b, v_gmlp_w_s, v_gmlp_b_s):
    w = dict(zip(WEIGHTS, (ada_w, ada_b, ln_g, ln_b, ffn1_w_in, ffn1_w_out, ffn2_w_in, ffn2_w_out, mix_w_in, mix_w_out, hgrn_lb_logits, hgrn_norm_g, mla_q_norm_g, mla_kv_norm_g, mla_w_uq, mla_w_ukv, fox_b_f, gmlp_ln_g, gmlp_ln_b, gmlp_w_s, gmlp_b_s)))
    m = dict(zip(WEIGHTS, (m_ada_w, m_ada_b, m_ln_g, m_ln_b, m_ffn1_w_in, m_ffn1_w_out, m_ffn2_w_in, m_ffn2_w_out, m_mix_w_in, m_mix_w_out, m_hgrn_lb_logits, m_hgrn_norm_g, m_mla_q_norm_g, m_mla_kv_norm_g, m_mla_w_uq, m_mla_w_ukv, m_fox_b_f, m_gmlp_ln_g, m_gmlp_ln_b, m_gmlp_w_s, m_gmlp_b_s)))
    v = dict(zip(WEIGHTS, (v_ada_w, v_ada_b, v_ln_g, v_ln_b, v_ffn1_w_in, v_ffn1_w_out, v_ffn2_w_in, v_ffn2_w_out, v_mix_w_in, v_mix_w_out, v_hgrn_lb_logits, v_hgrn_norm_g, v_mla_q_norm_g, v_mla_kv_norm_g, v_mla_w_uq, v_mla_w_ukv, v_fox_b_f, v_gmlp_ln_g, v_gmlp_ln_b, v_gmlp_w_s, v_gmlp_b_s)))
    Bl, S, _ = x.shape
    T = Bl * S
    core = lax.axis_index("c")
    chip = 2 * lax.axis_index("x") + lax.axis_index("y")

    def shard(key):
        n, l = key
        if n == "ln":
            return jnp.concatenate([ln_g[l:l + 1], ln_b[l:l + 1], jnp.zeros((1, 2, D // N_CHIPS), F32)], axis=1)
        return w[n][l:l + 1].astype(BF16)

    mixers = ["mix_w_in", "mix_w_out", "mla_w_uq", "mla_w_ukv"]
    groups = [[("ada_w", 0), ("ffn1_w_in", 0), ("ffn1_w_out", 0), ("ln", 0)],
              [(n, 0) for n in mixers + ["ffn2_w_in", "ffn2_w_out"]],
              [(n, 1) for n in GATHERED + ["ln"]]]
    srcs = [shard(k) for k in groups[0]]
    first, token = ag_shards(srcs, [own_slot(s, chip) for s in srcs])
    have = dict(zip(groups[0], first))
    handles = {}
    for gi in (1, 2):
        srcs = [s + token[0, 0].astype(s.dtype) for s in (shard(k) for k in groups[gi])]
        handles[gi] = ag_start(srcs, [own_slot(s, chip) for s in srcs], "ag_start_%d" % gi)
        token = handles[gi][-1]
    c8 = jnp.concatenate([c, jnp.zeros((8 - Bl, D), F32)], axis=0)
    c8 = c8 + token[0, 0]

    def cat_cols(a):
        return jnp.concatenate([a[0, k] for k in range(N_CHIPS)], axis=1)

    def get(l, part, after):
        gi = 2 if l == 1 else (0 if part in ("ada", "ffn1") else 1)
        if gi in handles:
            arrived = ag_forward(ag_wait(handles.pop(gi), after, "ag_wait_%d" % gi), "ag_forward_%d" % gi)
            have.update(zip(groups[gi], arrived))
        if part == "ada":
            return dict(ada_w=have[("ada_w", l)], wl=0, ada_b=ada_b[l][None])
        if part == "ffn1":
            ln_full = jnp.moveaxis(have[("ln", l)][0], 0, 1).reshape(8, D)
            return dict(ffn1_in=have[("ffn1_w_in", l)], ffn1_out=have[("ffn1_w_out", l)], wl=0,
                        ln_g=ln_full[0:3], ln_b=ln_full[3:6])
        if part == "ffn2":
            return dict(ffn2_in=have[("ffn2_w_in", l)], ffn2_out=have[("ffn2_w_out", l)])
        return dict(
            mix_in=mix_in_ext(cat_cols(have[("mix_w_in", l)])), mix_out=mix_out_ext(have[("mix_w_out", l)].reshape(D, D)),
            wq=uq_ext(cat_cols(have[("mla_w_uq", l)])).astype(F32), wkv=ukv_ext(cat_cols(have[("mla_w_ukv", l)])).astype(F32),
            ng=hgrn_norm_g[l][None], gq=mla_q_norm_g[l][None], gkv=mla_kv_norm_g[l][None],
            bcol=jnp.concatenate([fox_b_f[l], jnp.zeros((8 - HEADS,), F32)])[:, None],
            g_lg=gmlp_ln_g[l][None], g_lb=gmlp_ln_b[l][None], ws=gmlp_w_s[l], bs=gmlp_b_s[l][:, :, None])

    pending = []

    def emit(l, part, g):
        if part == "mix":
            names = mixers
            by_chip = [_col_shards(mix_in_unext(g["mix_in"])), mix_out_unext(g["mix_out"]).reshape(N_CHIPS, D // N_CHIPS, D),
                       _col_shards(uq_unext(g["wq"])), _col_shards(ukv_unext(g["wkv"]))]
        else:
            names = [part + "_w_in", part + "_w_out"]
            by_chip = [g[part + "_in"], g[part + "_out"]]
        tag = "%d_%s" % (l, part)
        got = sibling_swap(by_chip, "sibling_swap_" + tag)
        chip_sum = [add_kept_half(a, r, core, "add_sibling") for a, r in zip(by_chip, got)]
        slot = lax.broadcasted_iota(jnp.int32, (N_CHIPS, 1, 1), 0)
        lands = [jnp.where(slot == chip, h, 0.0) for h in chip_sum]
        handle = rs_start(chip_sum, lands, "rs_start_" + tag)
        pending.append((l, names, handle, tag))
        return handle[-1][0, 0]

    loss_tile, dx, dmods, grads, d_logits = local_step(
        x.reshape(T, D), c8, loss_target.reshape(T, D), get, hgrn_lb_logits, S, emit)
    loss = lax.psum(loss_tile[0, 0], ("x", "y", "c"))

    small_g = {"hgrn_lb_logits": d_logits,
               "hgrn_norm_g": jnp.stack([grads[l]["ng"][0] for l in range(DEPTH)]),
               "mla_q_norm_g": jnp.stack([grads[l]["gq"][0] for l in range(DEPTH)]),
               "mla_kv_norm_g": jnp.stack([grads[l]["gkv"][0] for l in range(DEPTH)]),
               "fox_b_f": jnp.stack([grads[l]["bcol"][0:HEADS, 0] for l in range(DEPTH)]),
               "gmlp_ln_g": jnp.stack([grads[l]["g_lg"][0] for l in range(DEPTH)]),
               "gmlp_ln_b": jnp.stack([grads[l]["g_lb"][0] for l in range(DEPTH)]),
               "gmlp_w_s": jnp.stack([grads[l]["ws"] for l in range(DEPTH)]),
               "gmlp_b_s": jnp.stack([grads[l]["bs"][:, :, 0] for l in range(DEPTH)])}
    small_g["ln_g"] = jnp.stack([grads[l]["ln_g"] for l in range(DEPTH)])
    small_g["ln_b"] = jnp.stack([grads[l]["ln_b"] for l in range(DEPTH)])
    pk_small = pack_small(small_g)
    n_small = pk_small.shape[0]
    extras = [dmods[l] for l in range(DEPTH)] + [c]
    n_extra = _round_up(-(-sum(int(np.prod(e.shape)) for e in extras) // D), 8)
    gathered = ag_all(jnp.concatenate([pk_small, _rows(extras, n_extra, F32)], axis=0))
    g_small = unpack_small(sum_slots(gathered[:, 0:n_small], 8, "sum_small"), small_g)
    ext = gathered[:, n_small:].reshape(8, -1)
    n_dmod = DEPTH * Bl * N_MOD * D
    dmod_all = ext[:, 0:n_dmod].reshape(8, DEPTH, Bl, N_MOD * D)
    c_all = ext[:, n_dmod:n_dmod + Bl * D].reshape(8 * Bl, D)
    g_ada_w, g_ada_b = [], []
    ncol = N_MOD * D // N_CHIPS
    for l in range(DEPTH):
        dm = dmod_all[:, l].reshape(8 * Bl, N_MOD * D)
        g_ada_w.append(ada_grad(c_all, lax.dynamic_slice_in_dim(dm, chip * ncol, ncol, axis=1)))
        g_ada_b.append(sum_slots(dm.reshape(8 * Bl, N_MOD, D), 8 * Bl, "sum_ada_b").reshape(N_MOD * D))
    g_ada_w, g_ada_b = jnp.stack(g_ada_w), jnp.stack(g_ada_b)

    red = {n: jnp.zeros(w[n].shape, F32) for n in REDUCED}

    def arrive(entry, after):
        l, names, handle, tag = entry
        for n, land in zip(names, rs_wait(handle, after, "rs_wait_" + tag)):
            red[n] = sum_into(land, red[n], l, core, "sum_chips")

    for entry in pending[:-1]:
        arrive(entry, dx)
    late = pending[-1][1]
    early = [n for n in REDUCED if n not in late]
    grad = dict(zip(early, sibling_join([red[n] for n in early], "sibling_join_a")))
    grad.update(g_small)
    grad["ada_w"], grad["ada_b"] = g_ada_w, g_ada_b
    for n in ("ln_g", "ln_b"):
        grad[n] = lax.dynamic_slice_in_dim(g_small[n], chip * (D // N_CHIPS), D // N_CHIPS, axis=2)
    out = {"grad": grad, "delta": {}, "new_m": {}, "new_v": {}}

    def update(n):
        shp = w[n].shape
        two_d = (-1, shp[-1])
        res = adamw(w[n].reshape(two_d), grad[n].reshape(two_d), m[n].reshape(two_d), v[n].reshape(two_d), "adamw_" + n)
        grad[n] = grad[n].reshape(shp)
        for key, r in zip(("delta", "new_m", "new_v"), res):
            out[key][n] = r.reshape(shp)

    for n in WEIGHTS:
        if n not in late:
            update(n)
    arrive(pending[-1], out["delta"]["ffn2_w_in"])
    grad.update(zip(late, sibling_join([red[n] for n in late], "sibling_join_b")))
    for n in late:
        update(n)
    outs = [loss, dx.reshape(Bl, S, D)]
    for key in ("grad", "delta", "new_m", "new_v"):
        outs += [out[key][n] for n in WEIGHTS]
    return tuple(outs)
```

```python
import functools

import jax
import jax.numpy as jnp
import numpy as np
from jax import lax
from jax.experimental import pallas as pl
from jax.experimental.pallas import tpu as pltpu

F32, BF16 = jnp.float32, jnp.bfloat16
MESH = pl.DeviceIdType.MESH

N_CHIPS = 4
D = 1024
DEPTH = 2
FF = 2816
N_MOD = 9
GW = 256
HEADS = 4
HD = 64
HP = 128
A_CHUNK = 16
LB_FLOOR = 1e-30
B_Q_LORA, B_KV_LORA, B_NOPE, B_ROPE = 256, 128, 64, 32
ROPE_THETA = 10000.0
D_CHUNK = 128
MIX_COLS = 2724
ALPHA = (2 * DEPTH) ** 0.25
LN_EPS = 1e-5
RMS_EPS = 1e-6
ADAM_LR, ADAM_B1, ADAM_B2, ADAM_EPS, ADAM_WD, ADAM_STEP = 0.001, 0.9, 0.999, 1e-08, 0.01, 10

NP = 3840
NP_TILE = 1920
C_A, C_B, C_CQ, C_CK, C_CV, C_D, C_CF = 0, 1024, 1536, 2048, 2560, 3072, 3584
NCAT = 1536
O_A, O_B, O_C, O_D = 0, 256, 768, 1280

VMEM_BIG = 48 << 20


def _cparams(vmem=None):
    return pltpu.CompilerParams(vmem_limit_bytes=vmem) if vmem else pltpu.CompilerParams()


def _sds(shape, dtype):
    return jax.ShapeDtypeStruct(tuple(shape), dtype)


@jax.custom_vjp
def _mm(a, w):
    return jnp.dot(a.astype(BF16), w.astype(BF16), preferred_element_type=F32)


def _mm_f(a, w):
    return _mm(a, w), (a, w)


def _mm_b(res, g):
    a, w = res
    gb = g.astype(BF16)
    da = lax.dot_general(gb, w.astype(BF16), (((1,), (1,)), ((), ())), preferred_element_type=F32)
    dw = lax.dot_general(a.astype(BF16), gb, (((0,), (0,)), ((), ())), preferred_element_type=F32)
    return da.astype(a.dtype), dw.astype(w.dtype)


_mm.defvjp(_mm_f, _mm_b)


@jax.custom_vjp
def _mm_nt(a, b):
    return lax.dot_general(a.astype(BF16), b.astype(BF16), (((1,), (1,)), ((), ())), preferred_element_type=F32)


def _mm_nt_f(a, b):
    return _mm_nt(a, b), (a, b)


def _mm_nt_b(res, g):
    a, b = res
    gb = g.astype(BF16)
    da = jnp.dot(gb, b.astype(BF16), preferred_element_type=F32)
    db = lax.dot_general(gb, a.astype(BF16), (((0,), (0,)), ((), ())), preferred_element_type=F32)
    return da.astype(a.dtype), db.astype(b.dtype)


_mm_nt.defvjp(_mm_nt_f, _mm_nt_b)


@jax.custom_vjp
def _mm_tn(a, b):
    return lax.dot_general(a.astype(BF16), b.astype(BF16), (((0,), (0,)), ((), ())), preferred_element_type=F32)


def _mm_tn_f(a, b):
    return _mm_tn(a, b), (a, b)


def _mm_tn_b(res, g):
    a, b = res
    gb = g.astype(BF16)
    da = lax.dot_general(b.astype(BF16), gb, (((1,), (1,)), ((), ())), preferred_element_type=F32)
    db = jnp.dot(a.astype(BF16), gb, preferred_element_type=F32)
    return da.astype(a.dtype), db.astype(b.dtype)


_mm_tn.defvjp(_mm_tn_f, _mm_tn_b)


def _mm_hi(a, w):
    return jnp.dot(a, w, precision=lax.Precision.HIGHEST, preferred_element_type=F32)


def _iota(shape, dim):
    return lax.broadcasted_iota(jnp.int32, shape, dim)


def _head_sum_mats():
    e = (_iota((GW, HP), 0) // HD == _iota((GW, HP), 1)).astype(F32)
    et = (_iota((HP, GW), 1) // HD == _iota((HP, GW), 0)).astype(F32)
    return e, et


def _modulate(x, mod_ref, sh, sc):
    return x * (1.0 + mod_ref[sc:sc + 1, :]) + mod_ref[sh:sh + 1, :]


def _ln_res(x, y, gate, lg, lb, gs):
    r = ALPHA * x + gs * (1.0 + gate) * y
    mu = jnp.mean(r, axis=-1, keepdims=True)
    var = jnp.mean(jnp.square(r - mu), axis=-1, keepdims=True)
    return (r - mu) * lax.rsqrt(var + LN_EPS) * lg + lb


def _rms(x, g):
    return x * lax.rsqrt(jnp.mean(x * x, axis=-1, keepdims=True) + RMS_EPS) * g


def _tile(n, pref):
    return pref if n % pref == 0 else n


def mod_fwd(c8, w, l, b):
    tn = w.shape[3]
    n = N_CHIPS * tn

    def body(c_ref, w_ref, b_ref, o_ref):
        h = jax.nn.silu(c_ref[...]).astype(BF16)
        o_ref[...] = jnp.dot(h, w_ref[...], preferred_element_type=F32) + b_ref[...]

    return pl.pallas_call(
        body, name="mod_fwd", grid=(N_CHIPS,),
        in_specs=[pl.BlockSpec((8, D), lambda j: (0, 0)), pl.BlockSpec((None, None, D, tn), lambda j: (l, j, 0, 0)),
                  pl.BlockSpec((1, tn), lambda j: (0, j))],
        out_specs=pl.BlockSpec((8, tn), lambda j: (0, j)), out_shape=_sds((8, n), F32),
        compiler_params=_cparams(VMEM_BIG))(c8, w, b)


def ffn_in_fwd(x, mod, w_in, l, sh, sc, S):
    T = x.shape[0]
    tm, tn = _tile(S, 512), FF // 2
    tpb, nj = S // tm, 2

    def body(x_ref, mod_ref, wg_ref, wu_ref, zg_ref, zu_ref, act_ref, h_ref):
        @pl.when(pl.program_id(1) == 0)
        def _():
            h_ref[...] = _modulate(x_ref[...], mod_ref, sh, sc).astype(BF16)
        g = jnp.dot(h_ref[...], wg_ref[...], preferred_element_type=F32)
        u = jnp.dot(h_ref[...], wu_ref[...], preferred_element_type=F32)
        zg_ref[...] = g.astype(BF16)
        zu_ref[...] = u.astype(BF16)
        act_ref[...] = (jax.nn.silu(g) * u).astype(BF16)

    return pl.pallas_call(
        body, name="ffn_in_fwd", grid=(T // tm, nj),
        in_specs=[pl.BlockSpec((tm, D), lambda i, j: (i, 0)),
                  pl.BlockSpec((None, N_MOD, D), lambda i, j: (i // tpb, 0, 0)),
                  pl.BlockSpec((None, None, D, tn), lambda i, j: (l, j, 0, 0)),
                  pl.BlockSpec((None, None, D, tn), lambda i, j: (l, j + nj, 0, 0))],
        out_specs=[pl.BlockSpec((tm, tn), lambda i, j: (i, j))] * 3,
        out_shape=[_sds((T, FF), BF16)] * 3,
        scratch_shapes=[pltpu.VMEM((tm, D), BF16)],
        compiler_params=_cparams(VMEM_BIG))(x, mod, w_in, w_in)


def mix_in_fwd(x, mod, w, sh, sc, S):
    T = x.shape[0]
    n = w.shape[1]
    tm, tn = _tile(S, 512), NP_TILE
    tpb = S // tm

    def body(x_ref, mod_ref, w_ref, o_ref, h_ref):
        @pl.when(pl.program_id(1) == 0)
        def _():
            h_ref[...] = _modulate(x_ref[...], mod_ref, sh, sc).astype(BF16)
        o_ref[...] = jnp.dot(h_ref[...], w_ref[...], preferred_element_type=F32)

    return pl.pallas_call(
        body, name="mix_in_fwd", grid=(T // tm, n // tn),
        in_specs=[pl.BlockSpec((tm, D), lambda i, j: (i, 0)),
                  pl.BlockSpec((None, N_MOD, D), lambda i, j: (i // tpb, 0, 0)),
                  pl.BlockSpec((D, tn), lambda i, j: (0, j))],
        out_specs=pl.BlockSpec((tm, tn), lambda i, j: (i, j)), out_shape=_sds((T, n), F32),
        scratch_shapes=[pltpu.VMEM((tm, D), BF16)],
        compiler_params=_cparams(VMEM_BIG))(x, mod, w)


def out_ln_fwd(act, w_out, x, mod, lg, lb, gate, gs, S, l=None):
    T, K = act.shape
    tm = _tile(S, 512)
    tpb = S // tm

    def body(a_ref, w_ref, x_ref, mod_ref, lg_ref, lb_ref, y_ref, xn_ref):
        y = jnp.dot(a_ref[...], w_ref[...].reshape(K, D), preferred_element_type=F32)
        y_ref[...] = y
        xn_ref[...] = _ln_res(x_ref[...], y, mod_ref[gate:gate + 1, :], lg_ref[...], lb_ref[...], gs)

    if l is None:
        w_spec = pl.BlockSpec((K, D), lambda i: (0, 0))
    else:
        w_spec = pl.BlockSpec((None, N_CHIPS, K // N_CHIPS, D), lambda i: (l, 0, 0, 0))
    return pl.pallas_call(
        body, name="out_ln_fwd", grid=(T // tm,),
        in_specs=[pl.BlockSpec((tm, K), lambda i: (i, 0)), w_spec,
                  pl.BlockSpec((tm, D), lambda i: (i, 0)),
                  pl.BlockSpec((None, N_MOD, D), lambda i: (i // tpb, 0, 0)),
                  pl.BlockSpec((1, D), lambda i: (0, 0)), pl.BlockSpec((1, D), lambda i: (0, 0))],
        out_specs=[pl.BlockSpec((tm, D), lambda i: (i, 0))] * 2,
        out_shape=[_sds((T, D), F32), _sds((T, D), F32)],
        compiler_params=_cparams(VMEM_BIG))(act, w_out, x, mod, lg, lb)


def ln_res_bwd(dxn, x, y, mod, lg, lb, gate, gs, S):
    T = x.shape[0]
    B = T // S
    tm = _tile(S, 512)
    tpb = S // tm

    def body(d_ref, x_ref, y_ref, mod_ref, lg_ref, lb_ref, dx_ref, dy_ref, dg_ref, dlg_ref, dlb_ref):
        i = pl.program_id(0)
        f = functools.partial(_ln_res, gs=gs)
        _, vjp = jax.vjp(f, x_ref[...], y_ref[...], mod_ref[gate:gate + 1, :], lg_ref[...], lb_ref[...])
        dx, dy, dg, dlg, dlb = vjp(d_ref[...])
        dx_ref[...] = dx
        dy_ref[...] = dy.astype(BF16)

        @pl.when(i % tpb == 0)
        def _():
            dg_ref[...] = jnp.zeros_like(dg_ref)

        @pl.when(i == 0)
        def _():
            dlg_ref[...] = jnp.zeros_like(dlg_ref)
            dlb_ref[...] = jnp.zeros_like(dlb_ref)

        dg_ref[...] += dg
        dlg_ref[...] += dlg
        dlb_ref[...] += dlb

    tok = pl.BlockSpec((tm, D), lambda i: (i, 0))
    vec = pl.BlockSpec((1, D), lambda i: (0, 0))
    return pl.pallas_call(
        body, name="ln_res_bwd", grid=(T // tm,),
        in_specs=[tok, tok, tok, pl.BlockSpec((None, N_MOD, D), lambda i: (i // tpb, 0, 0)), vec, vec],
        out_specs=[tok, tok, pl.BlockSpec((None, 1, D), lambda i: (i // tpb, 0, 0)), vec, vec],
        out_shape=[_sds((T, D), F32), _sds((T, D), BF16), _sds((B, 1, D), F32), _sds((1, D), F32), _sds((1, D), F32)],
        compiler_params=_cparams(VMEM_BIG))(dxn, x, y, mod, lg, lb)


def swiglu_bwd(dy, w_out, l, zg, zu, S):
    T = dy.shape[0]
    tm, tn = _tile(S, 512), FF // 2

    def body(dy_ref, w_ref, zg_ref, zu_ref, dg_ref, du_ref):
        da = lax.dot_general(dy_ref[...], w_ref[...].reshape(tn, D), (((1,), (1,)), ((), ())), preferred_element_type=F32)
        g, u = zg_ref[...].astype(F32), zu_ref[...].astype(F32)
        sg = jax.nn.sigmoid(g)
        dg_ref[...] = (da * u * (sg * (1.0 + g * (1.0 - sg)))).astype(BF16)
        du_ref[...] = (da * (g * sg)).astype(BF16)

    zt = pl.BlockSpec((tm, tn), lambda i, j: (i, j))
    return pl.pallas_call(
        body, name="swiglu_bwd", grid=(T // tm, FF // tn),
        in_specs=[pl.BlockSpec((tm, D), lambda i, j: (i, 0)),
                  pl.BlockSpec((None, 2, FF // N_CHIPS, D), lambda i, j: (l, j, 0, 0)), zt, zt],
        out_specs=[zt, zt], out_shape=[_sds((T, FF), BF16), _sds((T, FF), BF16)],
        compiler_params=_cparams(VMEM_BIG))(dy, w_out, zg, zu)


def nt_plain(dy, w):
    T = dy.shape[0]
    K = w.shape[0]
    tm = _tile(T, 512)

    def body(dy_ref, w_ref, o_ref):
        o_ref[...] = lax.dot_general(dy_ref[...], w_ref[...], (((1,), (1,)), ((), ())), preferred_element_type=F32)

    return pl.pallas_call(
        body, name="nt_plain", grid=(T // tm,),
        in_specs=[pl.BlockSpec((tm, D), lambda i: (i, 0)), pl.BlockSpec((K, D), lambda i: (0, 0))],
        out_specs=pl.BlockSpec((tm, K), lambda i: (i, 0)), out_shape=_sds((T, K), F32),
        compiler_params=_cparams(VMEM_BIG))(dy, w)


def tn_mm(a, b, tk):
    T, K = a.shape
    N = b.shape[1]
    tt = _tile(T, 512)

    def body(a_ref, b_ref, o_ref):
        @pl.when(pl.program_id(1) == 0)
        def _():
            o_ref[...] = jnp.zeros_like(o_ref)
        o_ref[...] += lax.dot_general(a_ref[...], b_ref[...], (((0,), (0,)), ((), ())), preferred_element_type=F32)

    return pl.pallas_call(
        body, name="tn_mm", grid=(K // tk, T // tt),
        in_specs=[pl.BlockSpec((tt, tk), lambda k, t: (t, k)), pl.BlockSpec((tt, N), lambda k, t: (t, 0))],
        out_specs=pl.BlockSpec((tk, N), lambda k, t: (k, 0)), out_shape=_sds((K, N), F32),
        compiler_params=_cparams(VMEM_BIG))(a, b)


def tn_mm_mod(x, mod, b, sh, sc, S, tn):
    T = x.shape[0]
    N = b.shape[1]
    tt = _tile(S, 512)
    tpb = S // tt

    def body(x_ref, mod_ref, b_ref, o_ref):
        @pl.when(pl.program_id(1) == 0)
        def _():
            o_ref[...] = jnp.zeros_like(o_ref)
        h = _modulate(x_ref[...], mod_ref, sh, sc).astype(BF16)
        o_ref[...] += lax.dot_general(h, b_ref[...], (((0,), (0,)), ((), ())), preferred_element_type=F32)

    return pl.pallas_call(
        body, name="tn_mm_mod", grid=(N // tn, T // tt),
        in_specs=[pl.BlockSpec((tt, D), lambda j, t: (t, 0)),
                  pl.BlockSpec((None, N_MOD, D), lambda j, t: (t // tpb, 0, 0)),
                  pl.BlockSpec((tt, tn), lambda j, t: (t, j))],
        out_specs=pl.BlockSpec((D, tn), lambda j, t: (0, j)), out_shape=_sds((D, N), F32),
        compiler_params=_cparams(VMEM_BIG))(x, mod, b)


def tn_mm_mod_shards(x, mod, bg, bu, sh, sc, S):
    T = x.shape[0]
    tn = FF // 2
    tt = _tile(S, 512)
    tpb = S // tt

    def body(x_ref, mod_ref, bg_ref, bu_ref, o_ref):
        j = pl.program_id(0)

        @pl.when(pl.program_id(1) == 0)
        def _():
            o_ref[...] = jnp.zeros_like(o_ref)
        h = _modulate(x_ref[...], mod_ref, sh, sc).astype(BF16)

        @pl.when(j < 2)
        def _():
            o_ref[...] += lax.dot_general(h, bg_ref[...], (((0,), (0,)), ((), ())), preferred_element_type=F32)

        @pl.when(j >= 2)
        def _():
            o_ref[...] += lax.dot_general(h, bu_ref[...], (((0,), (0,)), ((), ())), preferred_element_type=F32)

    return pl.pallas_call(
        body, name="tn_mm_mod_shards", grid=(N_CHIPS, T // tt),
        in_specs=[pl.BlockSpec((tt, D), lambda j, t: (t, 0)),
                  pl.BlockSpec((None, N_MOD, D), lambda j, t: (t // tpb, 0, 0)),
                  pl.BlockSpec((tt, tn), lambda j, t: (jnp.where(j < 2, t, 0), jnp.minimum(j, 1))),
                  pl.BlockSpec((tt, tn), lambda j, t: (jnp.where(j < 2, 0, t), jnp.maximum(j - 2, 0)))],
        out_specs=pl.BlockSpec((None, D, tn), lambda j, t: (j, 0, 0)), out_shape=_sds((N_CHIPS, D, tn), F32),
        compiler_params=_cparams(VMEM_BIG))(x, mod, bg, bu)


def nt_mod_bwd(ds, w, offs, x, mod, dres, sc, S, tk, l=None):
    T = x.shape[0]
    B = T // S
    tm = _tile(S, 512)
    tpb = S // tm
    Kd = ds[0].shape[1]
    nk = Kd // tk
    n_in = len(ds)

    def body(*refs):
        d_refs, w_refs = refs[:n_in], refs[n_in:2 * n_in]
        x_ref, mod_ref, r_ref, dx_ref, dsh_ref, dsc_ref, acc = refs[2 * n_in:]
        i, k = pl.program_id(0), pl.program_id(1)

        @pl.when(k == 0)
        def _():
            acc[...] = jnp.zeros_like(acc)

        for d_ref, w_ref in zip(d_refs, w_refs):
            acc[...] += lax.dot_general(d_ref[...], w_ref[...], (((1,), (1,)), ((), ())), preferred_element_type=F32)

        @pl.when(k == nk - 1)
        def _():
            dh = acc[...]
            dx_ref[...] = dh * (1.0 + mod_ref[sc:sc + 1, :]) + r_ref[...]

            @pl.when(i % tpb == 0)
            def _():
                dsh_ref[...] = jnp.zeros_like(dsh_ref)
                dsc_ref[...] = jnp.zeros_like(dsc_ref)

            dsh_ref[...] += jnp.sum(dh, axis=0, keepdims=True)
            dsc_ref[...] += jnp.sum(dh * x_ref[...], axis=0, keepdims=True)

    tok = pl.BlockSpec((tm, D), lambda i, k: (i, 0))
    vec = pl.BlockSpec((None, 1, D), lambda i, k: (i // tpb, 0, 0))
    in_specs = [pl.BlockSpec((tm, tk), lambda i, k: (i, k)) for _ in ds]
    if l is None:
        in_specs += [pl.BlockSpec((D, tk), functools.partial(lambda i, k, o: (0, k + o), o=off // tk)) for off in offs]
    else:
        in_specs += [pl.BlockSpec((None, None, D, tk), functools.partial(lambda i, k, o: (l, k + o, 0, 0), o=off)) for off in offs]
    in_specs += [tok, pl.BlockSpec((None, N_MOD, D), lambda i, k: (i // tpb, 0, 0)), tok]
    return pl.pallas_call(
        body, name="nt_mod_bwd", grid=(T // tm, nk), in_specs=in_specs,
        out_specs=[tok, vec, vec],
        out_shape=[_sds((T, D), F32), _sds((B, 1, D), F32), _sds((B, 1, D), F32)],
        scratch_shapes=[pltpu.VMEM((tm, D), F32)],
        compiler_params=_cparams(VMEM_BIG))(*ds, *([w] * n_in), x, mod, dres)


def loss_head(y, tgt):
    T = y.shape[0]
    tm = _tile(T, 512)

    def body(y_ref, t_ref, l_ref, d_ref):
        @pl.when(pl.program_id(0) == 0)
        def _():
            l_ref[...] = jnp.zeros_like(l_ref)
        e = y_ref[...] - t_ref[...]
        d_ref[...] = e * (1.0 / D)
        l_ref[...] += 0.5 * jnp.sum(jnp.sum(e * e, axis=1, keepdims=True) * (1.0 / D))

    tok = pl.BlockSpec((tm, D), lambda i: (i, 0))
    return pl.pallas_call(
        body, name="loss_head", grid=(T // tm,), in_specs=[tok, tok],
        out_specs=[pl.BlockSpec((8, 128), lambda i: (0, 0)), tok],
        out_shape=[_sds((8, 128), F32), _sds((T, D), F32)],
        compiler_params=_cparams(VMEM_BIG))(y, tgt)


def _hgrn_block(q, fz, inp, go, st, lb, ng, blk):
    nc = blk // A_CHUNK
    lb_eff = jnp.maximum(lb, LB_FLOOR)
    log_f = jnp.logaddexp(jnp.log(lb_eff), jnp.log1p(-lb) + jax.nn.log_sigmoid(fz))
    k = (1.0 - lb) * jax.nn.sigmoid(-fz) - (lb_eff - lb)
    qf = jax.nn.silu(q)
    same_chunk = _iota((blk, blk), 0) // A_CHUNK == _iota((blk, blk), 1) // A_CHUNK
    tril = (same_chunk & (_iota((blk, blk), 1) <= _iota((blk, blk), 0))).astype(F32)
    G = _mm_hi(tril, log_f)
    e_mat, et_mat = _head_sum_mats()
    G4, q4, k4, v4 = (z.reshape(nc, A_CHUNK, GW) for z in (G, qf, k, inp))
    shp = (nc, A_CHUNK, A_CHUNK, GW)
    one = (1, A_CHUNK, A_CHUNK, GW)
    mask = jnp.where(_iota(one, 2) <= _iota(one, 1), 0.0, -jnp.inf)
    decay = jnp.exp((G4[:, :, None, :] - G4[:, None, :, :]) + mask)
    prod = q4[:, :, None, :] * k4[:, None, :, :] * decay
    scores = _mm(prod.reshape(nc * A_CHUNK * A_CHUNK, GW), e_mat.astype(BF16))
    spread = _mm(scores, et_mat.astype(BF16)).reshape(shp)
    o_intra = jnp.sum(spread * v4[:, None, :, :], axis=2).reshape(blk, GW)
    head_diag = (_iota((GW, GW), 0) // HD == _iota((GW, GW), 1) // HD).astype(F32)
    g_last = [jnp.sum(log_f[c * A_CHUNK:(c + 1) * A_CHUNK], axis=0, keepdims=True) for c in range(nc)]
    g_last_b = jnp.concatenate([jnp.broadcast_to(g, (A_CHUNK, GW)) for g in g_last], axis=0)
    q_dec = qf * jnp.exp(G)
    k_end = k * jnp.exp(g_last_b - G)
    outs = []
    for c in range(nc):
        rows = slice(c * A_CHUNK, (c + 1) * A_CHUNK)
        outs.append(_mm_nt(q_dec[rows], st))
        st = st * jnp.exp(g_last[c]) + _mm_tn(inp[rows], k_end[rows]) * head_diag
    o = o_intra + jnp.concatenate(outs, axis=0)
    ms = _mm_hi(o * o, e_mat) * (1.0 / HD)
    o = o * _mm_hi(lax.rsqrt(ms + RMS_EPS), et_mat) * ng
    return o * jax.nn.silu(go), st


HGRN_BLK = 128


def hgrn_fwd(proj, lb, ng, S):
    T = proj.shape[0]
    B = T // S
    blk = min(HGRN_BLK, S)
    nb = S // blk

    def body(p_ref, lb_ref, ng_ref, o_ref, st_out_ref, st_ref):
        @pl.when(pl.program_id(1) == 0)
        def _():
            st_ref[...] = jnp.zeros_like(st_ref)
        st_out_ref[...] = st_ref[...]
        p = p_ref[...]
        o, st = _hgrn_block(p[:, 0:GW], p[:, GW:2 * GW], p[:, 2 * GW:3 * GW], p[:, 3 * GW:4 * GW],
                            st_ref[...], lb_ref[...], ng_ref[...], blk)
        o_ref[...] = o.astype(BF16)
        st_ref[...] = st

    vec = pl.BlockSpec((1, GW), lambda b, j: (0, 0))
    return pl.pallas_call(
        body, name="hgrn_fwd", grid=(B, nb),
        in_specs=[pl.BlockSpec((blk, 4 * GW), lambda b, j: (b * nb + j, C_A // (4 * GW))), vec, vec],
        out_specs=[pl.BlockSpec((blk, GW), lambda b, j: (b * nb + j, 0)),
                   pl.BlockSpec((None, GW, GW), lambda b, j: (b * nb + j, 0, 0))],
        out_shape=[_sds((T, GW), BF16), _sds((B * nb, GW, GW), F32)],
        scratch_shapes=[pltpu.VMEM((GW, GW), F32)],
        compiler_params=_cparams(VMEM_BIG))(proj, lb, ng)


def hgrn_bwd(proj, states, dcat, lb, ng, S):
    T = proj.shape[0]
    B = T // S
    blk = min(HGRN_BLK, S)
    nb = S // blk

    def body(p_ref, st_in_ref, do_ref, lb_ref, ng_ref, dp_ref, dlb_ref, dng_ref, dst_ref):
        b, j = pl.program_id(0), pl.program_id(1)

        @pl.when(j == 0)
        def _():
            dst_ref[...] = jnp.zeros_like(dst_ref)

        @pl.when((b == 0) & (j == 0))
        def _():
            dlb_ref[...] = jnp.zeros_like(dlb_ref)
            dng_ref[...] = jnp.zeros_like(dng_ref)

        p = p_ref[...]
        f = functools.partial(_hgrn_block, blk=blk)
        _, vjp = jax.vjp(f, p[:, 0:GW], p[:, GW:2 * GW], p[:, 2 * GW:3 * GW], p[:, 3 * GW:4 * GW],
                         st_in_ref[...], lb_ref[...], ng_ref[...])
        dq, df, di, dg, dst, dlb, dng = vjp((do_ref[...], dst_ref[...]))
        dp_ref[...] = jnp.concatenate([dq, df, di, dg], axis=1).astype(BF16)
        dst_ref[...] = dst
        dlb_ref[...] += dlb
        dng_ref[...] += dng

    def rev(b, j):
        return b * nb + (nb - 1 - j)

    vec = pl.BlockSpec((1, GW), lambda b, j: (0, 0))
    return pl.pallas_call(
        body, name="hgrn_bwd", grid=(B, nb),
        in_specs=[pl.BlockSpec((blk, 4 * GW), lambda b, j: (rev(b, j), C_A // (4 * GW))),
                  pl.BlockSpec((None, GW, GW), lambda b, j: (rev(b, j), 0, 0)),
                  pl.BlockSpec((blk, GW), lambda b, j: (rev(b, j), O_A // GW)), vec, vec],
        out_specs=[pl.BlockSpec((blk, 4 * GW), lambda b, j: (rev(b, j), 0)), vec, vec],
        out_shape=[_sds((T, 4 * GW), BF16), _sds((1, GW), F32), _sds((1, GW), F32)],
        scratch_shapes=[pltpu.VMEM((GW, GW), F32)],
        compiler_params=_cparams(VMEM_BIG))(proj, states, dcat, lb, ng)


ATT_TQ = 256


ATT_BANDS = 4


def _attn_block(q, k, v, cum, qpos0, scale, use_cum, n_free):
    s = _mm_nt(q, k) * scale
    if use_cum:
        s = s - cum
    band = s[:, n_free:]
    visible = _iota(band.shape, 1) <= (qpos0 - n_free) + _iota(band.shape, 0)
    band = jnp.where(visible, band, -jnp.inf)
    m = jnp.max(band, axis=-1, keepdims=True)
    if n_free:
        free = s[:, :n_free]
        m = jnp.maximum(m, jnp.max(free, axis=-1, keepdims=True))
    e = jnp.exp(band - m)
    denom = jnp.sum(e, axis=-1, keepdims=True)
    o = _mm(e, v[n_free:])
    if n_free:
        e = jnp.exp(free - m)
        denom = denom + jnp.sum(e, axis=-1, keepdims=True)
        o = o + _mm(e, v[:n_free])
    return o * (1.0 / denom)


def _bands(S, tq):
    nq = S // tq
    nb = min(ATT_BANDS, nq)
    per = nq // nb
    return [(r * per, (r + 1) * per, (r + 1) * per * tq) for r in range(nb)]


def attn_fwd(qa, qo, ka, ko, va, vo, cum, scale, S):
    T = qa.shape[0]
    B = T // S
    tq = min(ATT_TQ, S)
    nq = S // tq
    use_cum = cum is not None

    def body(*refs):
        if use_cum:
            q_ref, k_ref, v_ref, c_ref, o_ref = refs
        else:
            (q_ref, k_ref, v_ref, o_ref), c_ref = refs, None
        h, i = pl.program_id(1), pl.program_id(2)
        for lo, hi, kw in _bands(S, tq):
            @pl.when((i >= lo) & (i < hi))
            def _():
                crow = c_ref[pl.ds(h, 1), 0:kw] if use_cum else None
                o = _attn_block(q_ref[...], k_ref[0:kw, :], v_ref[0:kw, :], crow, i * tq, scale, use_cum, lo * tq)
                o_ref[...] = o.astype(BF16)

    in_specs = [pl.BlockSpec((tq, HP), lambda b, h, i: (b * nq + i, qo + h)),
                pl.BlockSpec((S, HP), lambda b, h, i: (b, ko + h)),
                pl.BlockSpec((S, HP), lambda b, h, i: (b, vo + h))]
    args = [qa, ka, va]
    if use_cum:
        in_specs.append(pl.BlockSpec((None, 8, S), lambda b, h, i: (b, 0, 0)))
        args.append(cum)
    return pl.pallas_call(
        body, name="attn_fwd", grid=(B, HEADS, nq), in_specs=in_specs,
        out_specs=pl.BlockSpec((tq, HP), lambda b, h, i: (b * nq + i, h)),
        out_shape=_sds((T, HEADS * HP), BF16),
        compiler_params=_cparams(VMEM_BIG))(*args)


def attn_bwd(qa, qo, ka, ko, va, vo, cum, dcat, do_off, scale, S, out_dtype):
    T = qa.shape[0]
    B = T // S
    tq = min(ATT_TQ, S)
    nq = S // tq
    use_cum = cum is not None

    def body(*refs):
        if use_cum:
            q_ref, k_ref, v_ref, do_ref, c_ref, dq_ref, dk_ref, dv_ref, dc_ref, dk_acc, dv_acc = refs
        else:
            q_ref, k_ref, v_ref, do_ref, dq_ref, dk_ref, dv_ref, dk_acc, dv_acc = refs
        h, i = pl.program_id(1), pl.program_id(2)

        @pl.when(i == 0)
        def _():
            dk_acc[...] = jnp.zeros_like(dk_acc)
            dv_acc[...] = jnp.zeros_like(dv_acc)
            if use_cum:
                dc_ref[...] = jnp.zeros_like(dc_ref)

        for lo, hi, kw in _bands(S, tq):
            @pl.when((i >= lo) & (i < hi))
            def _():
                crow = c_ref[pl.ds(h, 1), 0:kw] if use_cum else jnp.zeros((1, kw), F32)
                f = functools.partial(_attn_block, qpos0=i * tq, scale=scale, use_cum=use_cum, n_free=lo * tq)
                _, vjp = jax.vjp(f, q_ref[...], k_ref[0:kw, :], v_ref[0:kw, :], crow)
                dq, dk, dv, dc = vjp(do_ref[...])
                dq_ref[...] = dq.astype(out_dtype)
                dk_acc[0:kw, :] += dk
                dv_acc[0:kw, :] += dv
                if use_cum:
                    dc_ref[:, 0:kw] += dc

        @pl.when(i == nq - 1)
        def _():
            dk_ref[...] = dk_acc[...].astype(out_dtype)
            dv_ref[...] = dv_acc[...].astype(out_dtype)

    qspec = pl.BlockSpec((tq, HP), lambda b, h, i: (b * nq + i, qo + h))
    in_specs = [qspec, pl.BlockSpec((S, HP), lambda b, h, i: (b, ko + h)),
                pl.BlockSpec((S, HP), lambda b, h, i: (b, vo + h)),
                pl.BlockSpec((tq, HP), lambda b, h, i: (b * nq + i, do_off + h))]
    args = [qa, ka, va, dcat]
    kv_out = pl.BlockSpec((S, HP), lambda b, h, i: (b, h))
    out_specs = [pl.BlockSpec((tq, HP), lambda b, h, i: (b * nq + i, h)), kv_out, kv_out]
    out_shape = [_sds((T, HEADS * HP), out_dtype)] * 3
    if use_cum:
        in_specs.append(pl.BlockSpec((None, 8, S), lambda b, h, i: (b, 0, 0)))
        args.append(cum)
        out_specs.append(pl.BlockSpec((None, 1, S), lambda b, h, i: (b * HEADS + h, 0, 0)))
        out_shape.append(_sds((B * HEADS, 1, S), F32))
    return pl.pallas_call(
        body, name="attn_bwd", grid=(B, HEADS, nq), in_specs=in_specs, out_specs=out_specs, out_shape=out_shape,
        scratch_shapes=[pltpu.VMEM((S, HP), F32), pltpu.VMEM((S, HP), F32)],
        compiler_params=_cparams(VMEM_BIG))(*args)


def _tri(n, upper):
    r, c = _iota((n, n), 0), _iota((n, n), 1)
    return ((r <= c) if upper else (r >= c)).astype(F32)


def fox_gate_fwd(proj, bcol, S):
    T = proj.shape[0]
    B = T // S
    ts = _tile(S, 512)
    nt = S // ts

    def body(p_ref, b_ref, o_ref, carry):
        @pl.when(pl.program_id(1) == 0)
        def _():
            carry[...] = jnp.zeros_like(carry)
        cf = jnp.transpose(p_ref[...])[0:8, :]
        lf = jax.nn.log_sigmoid(cf + b_ref[...])
        cum = _mm_hi(lf, _tri(ts, True)) + carry[...]
        o_ref[...] = cum
        carry[...] += jnp.sum(lf, axis=1, keepdims=True)

    return pl.pallas_call(
        body, name="fox_gate_fwd", grid=(B, nt),
        in_specs=[pl.BlockSpec((ts, HP), lambda b, j: (b * nt + j, C_CF // HP)), pl.BlockSpec((8, 1), lambda b, j: (0, 0))],
        out_specs=pl.BlockSpec((None, 8, ts), lambda b, j: (b, 0, j)), out_shape=_sds((B, 8, S), F32),
        scratch_shapes=[pltpu.VMEM((8, 1), F32)],
        compiler_params=_cparams(VMEM_BIG))(proj, bcol)


def fox_gate_bwd(proj, bcol, dcum, S):
    T = proj.shape[0]
    B = T // S
    ts = _tile(S, 512)
    nt = S // ts

    def body(p_ref, b_ref, dc_ref, dp_ref, db_ref, carry):
        b, j = pl.program_id(0), pl.program_id(1)

        @pl.when(j == 0)
        def _():
            carry[...] = jnp.zeros_like(carry)

        @pl.when((b == 0) & (j == 0))
        def _():
            db_ref[...] = jnp.zeros_like(db_ref)

        cf = jnp.transpose(p_ref[...])[0:8, :]
        dc = dc_ref[...]
        dlf = _mm_hi(dc, _tri(ts, False)) + carry[...]
        carry[...] += jnp.sum(dc, axis=1, keepdims=True)
        dcf = dlf * jax.nn.sigmoid(-(cf + b_ref[...]))
        db_ref[...] += jnp.sum(dcf, axis=1, keepdims=True)
        full = jnp.concatenate([dcf, jnp.zeros((HP - 8, ts), F32)], axis=0)
        dp_ref[...] = jnp.transpose(full).astype(BF16)

    def rev(b, j):
        return nt - 1 - j

    return pl.pallas_call(
        body, name="fox_gate_bwd", grid=(B, nt),
        in_specs=[pl.BlockSpec((ts, HP), lambda b, j: (b * nt + rev(b, j), C_CF // HP)),
                  pl.BlockSpec((8, 1), lambda b, j: (0, 0)),
                  pl.BlockSpec((None, 8, ts), lambda b, j: (b, 0, rev(b, j)))],
        out_specs=[pl.BlockSpec((ts, HP), lambda b, j: (b * nt + rev(b, j), 0)), pl.BlockSpec((8, 1), lambda b, j: (0, 0))],
        out_shape=[_sds((T, HP), BF16), _sds((8, 1), F32)],
        scratch_shapes=[pltpu.VMEM((8, 1), F32)],
        compiler_params=_cparams(VMEM_BIG))(proj, bcol, dcum)


def _mla_pre(blk, gq, gkv, wq, wkv, place, cos_q, sin_q, cs_k):
    nq = _rms(blk[:, 0:B_Q_LORA], gq)
    nkv = _rms(blk[:, B_Q_LORA:B_Q_LORA + B_KV_LORA], gkv)
    qq = _mm(nq, wq)
    q = qq[:, 0:HEADS * HP] * cos_q + qq[:, HEADS * HP:] * sin_q
    kv = _mm(nkv, wkv)
    k = kv[:, 0:HEADS * HP] + _mm(blk[:, B_Q_LORA + B_KV_LORA:] * cs_k, place)
    return q, k, kv[:, HEADS * HP:]


def mla_pre_fwd(proj, gq, gkv, wq, wkv, place, cos_q, sin_q, cs_k, S):
    T = proj.shape[0]
    tm = _tile(S, 512)
    tpb = S // tm
    W = HEADS * HP

    def body(p_ref, gq_ref, gkv_ref, wq_ref, wkv_ref, pl_ref, cq_ref, sq_ref, ck_ref, q_ref, k_ref, v_ref):
        q, k, v = _mla_pre(p_ref[...], gq_ref[...], gkv_ref[...], wq_ref[...], wkv_ref[...], pl_ref[...],
                           cq_ref[...], sq_ref[...], ck_ref[...])
        q_ref[...] = q
        k_ref[...] = k
        v_ref[...] = v

    def full(a):
        return pl.BlockSpec(a.shape, lambda i: (0,) * a.ndim)

    tok = pl.BlockSpec((tm, W), lambda i: (i, 0))
    return pl.pallas_call(
        body, name="mla_pre_fwd", grid=(T // tm,),
        in_specs=[pl.BlockSpec((tm, W), lambda i: (i, C_B // W)), full(gq), full(gkv), full(wq), full(wkv), full(place),
                  pl.BlockSpec((tm, W), lambda i: (i % tpb, 0)), pl.BlockSpec((tm, W), lambda i: (i % tpb, 0)),
                  pl.BlockSpec((tm, HP), lambda i: (i % tpb, 0))],
        out_specs=[tok] * 3, out_shape=[_sds((T, W), F32)] * 3,
        compiler_params=_cparams(VMEM_BIG))(proj, gq, gkv, wq, wkv, place, cos_q, sin_q, cs_k)


def mla_pre_bwd(proj, gq, gkv, wq, wkv, place, cos_q, sin_q, cs_k, dq, dk, dv, S):
    T = proj.shape[0]
    tm = _tile(S, 512)
    tpb = S // tm
    W = HEADS * HP

    def body(p_ref, gq_ref, gkv_ref, wq_ref, wkv_ref, pl_ref, cq_ref, sq_ref, ck_ref, dq_ref, dk_ref, dv_ref,
             dp_ref, dgq_ref, dgkv_ref, dwq_ref, dwkv_ref):
        @pl.when(pl.program_id(0) == 0)
        def _():
            for r in (dgq_ref, dgkv_ref, dwq_ref, dwkv_ref):
                r[...] = jnp.zeros_like(r)

        f = functools.partial(_mla_pre, place=pl_ref[...], cos_q=cq_ref[...], sin_q=sq_ref[...], cs_k=ck_ref[...])
        _, vjp = jax.vjp(f, p_ref[...], gq_ref[...], gkv_ref[...], wq_ref[...], wkv_ref[...])
        dp, dgq, dgkv, dwq, dwkv = vjp((dq_ref[...], dk_ref[...], dv_ref[...]))
        dp_ref[...] = dp.astype(BF16)
        dgq_ref[...] += dgq
        dgkv_ref[...] += dgkv
        dwq_ref[...] += dwq
        dwkv_ref[...] += dwkv

    def full(a):
        return pl.BlockSpec(a.shape, lambda i: (0,) * a.ndim)

    tok = pl.BlockSpec((tm, W), lambda i: (i, 0))
    return pl.pallas_call(
        body, name="mla_pre_bwd", grid=(T // tm,),
        in_specs=[pl.BlockSpec((tm, W), lambda i: (i, C_B // W)), full(gq), full(gkv), full(wq), full(wkv), full(place),
                  pl.BlockSpec((tm, W), lambda i: (i % tpb, 0)), pl.BlockSpec((tm, W), lambda i: (i % tpb, 0)),
                  pl.BlockSpec((tm, HP), lambda i: (i % tpb, 0)), tok, tok, tok],
        out_specs=[tok, full(gq), full(gkv), full(wq), full(wkv)],
        out_shape=[_sds((T, W), BF16), _sds(gq.shape, F32), _sds(gkv.shape, F32), _sds(wq.shape, F32), _sds(wkv.shape, F32)],
        compiler_params=_cparams(VMEM_BIG))(proj, gq, gkv, wq, wkv, place, cos_q, sin_q, cs_k, dq, dk, dv)


def _gmlp_block(blk, lg, lb, ws, bs):
    u = jax.nn.gelu(blk[:, 0:GW])
    v = jax.nn.gelu(blk[:, GW:2 * GW])
    mu = jnp.mean(v, axis=-1, keepdims=True)
    var = jnp.mean(jnp.square(v - mu), axis=-1, keepdims=True)
    vn = (v - mu) * lax.rsqrt(var + LN_EPS) * lg + lb
    causal = _iota((D_CHUNK, D_CHUNK), 1) <= _iota((D_CHUNK, D_CHUNK), 0)
    group = _iota((1, GW), 1) // HD
    mixed = jnp.zeros((D_CHUNK, GW), F32)
    for g in range(HEADS):
        part = _mm(jnp.where(causal, ws[g], 0.0), vn) + bs[g]
        mixed = mixed + jnp.where(group == g, part, 0.0)
    return u * mixed


def gmlp_fwd(proj, lg, lb, ws, bs):
    T = proj.shape[0]

    def body(p_ref, lg_ref, lb_ref, ws_ref, bs_ref, o_ref):
        o_ref[...] = _gmlp_block(p_ref[...], lg_ref[...], lb_ref[...], ws_ref[...], bs_ref[...]).astype(BF16)

    def full(a):
        return pl.BlockSpec(a.shape, lambda i: (0,) * a.ndim)

    return pl.pallas_call(
        body, name="gmlp_fwd", grid=(T // D_CHUNK,),
        in_specs=[pl.BlockSpec((D_CHUNK, 2 * GW), lambda i: (i, C_D // (2 * GW))), full(lg), full(lb), full(ws), full(bs)],
        out_specs=pl.BlockSpec((D_CHUNK, GW), lambda i: (i, 0)), out_shape=_sds((T, GW), BF16),
        compiler_params=_cparams(VMEM_BIG))(proj, lg, lb, ws, bs)


def gmlp_bwd(proj, lg, lb, ws, bs, dcat):
    T = proj.shape[0]

    def body(p_ref, lg_ref, lb_ref, ws_ref, bs_ref, do_ref, dp_ref, dlg_ref, dlb_ref, dws_ref, dbs_ref):
        @pl.when(pl.program_id(0) == 0)
        def _():
            for r in (dlg_ref, dlb_ref, dws_ref, dbs_ref):
                r[...] = jnp.zeros_like(r)

        _, vjp = jax.vjp(_gmlp_block, p_ref[...], lg_ref[...], lb_ref[...], ws_ref[...], bs_ref[...])
        dp, dlg, dlb, dws, dbs = vjp(do_ref[...])
        dp_ref[...] = dp.astype(BF16)
        dlg_ref[...] += dlg
        dlb_ref[...] += dlb
        dws_ref[...] += dws
        dbs_ref[...] += dbs

    def full(a):
        return pl.BlockSpec(a.shape, lambda i: (0,) * a.ndim)

    return pl.pallas_call(
        body, name="gmlp_bwd", grid=(T // D_CHUNK,),
        in_specs=[pl.BlockSpec((D_CHUNK, 2 * GW), lambda i: (i, C_D // (2 * GW))), full(lg), full(lb), full(ws), full(bs),
                  pl.BlockSpec((D_CHUNK, GW), lambda i: (i, O_D // GW))],
        out_specs=[pl.BlockSpec((D_CHUNK, 2 * GW), lambda i: (i, 0)), full(lg), full(lb), full(ws), full(bs)],
        out_shape=[_sds((T, 2 * GW), BF16), _sds(lg.shape, F32), _sds(lb.shape, F32), _sds(ws.shape, F32), _sds(bs.shape, F32)],
        compiler_params=_cparams(VMEM_BIG))(proj, lg, lb, ws, bs, dcat)


def _lb_all(logits):
    m = jnp.max(logits, axis=0, keepdims=True)
    e = jnp.exp(logits - m)
    sm = e / jnp.sum(e, axis=0, keepdims=True)
    return jnp.concatenate([sm[0:1] - sm[0:1], (sm[0:1] + sm[1:2]) - sm[0:1]], axis=0)


def lb_fwd(logits):
    def body(l_ref, o_ref):
        o_ref[...] = _lb_all(l_ref[...])

    return pl.pallas_call(body, name="lb_fwd", out_shape=_sds(logits.shape, F32))(logits)


def lb_bwd(logits, dlb):
    def body(l_ref, d_ref, o_ref):
        _, vjp = jax.vjp(_lb_all, l_ref[...])
        o_ref[...] = vjp(d_ref[...])[0]

    return pl.pallas_call(body, name="lb_bwd", out_shape=_sds(logits.shape, F32))(logits, dlb)


def ada_grad(c_all, dmod_cols):
    N = dmod_cols.shape[1]
    tn = _tile(N, 1152)

    def body(c_ref, d_ref, o_ref):
        h = jax.nn.silu(c_ref[...]).astype(BF16)
        o_ref[...] = lax.dot_general(h, d_ref[...].astype(BF16), (((0,), (0,)), ((), ())), preferred_element_type=F32)

    nb = c_all.shape[0]
    return pl.pallas_call(
        body, name="ada_grad", grid=(N // tn,),
        in_specs=[pl.BlockSpec((nb, D), lambda j: (0, 0)), pl.BlockSpec((nb, tn), lambda j: (0, j))],
        out_specs=pl.BlockSpec((D, tn), lambda j: (0, j)), out_shape=_sds((D, N), F32),
        compiler_params=_cparams(VMEM_BIG))(c_all, dmod_cols)


def sum_slots(a, n, name):
    _, R, C = a.shape
    tr = _row_tile(R, C, n)

    def body(a_ref, o_ref):
        acc = a_ref[0]
        for k in range(1, n):
            acc = acc + a_ref[k]
        o_ref[...] = acc

    return pl.pallas_call(
        body, name=name, grid=(R // tr,),
        in_specs=[pl.BlockSpec((n, tr, C), lambda i: (0, i, 0))],
        out_specs=pl.BlockSpec((tr, C), lambda i: (i, 0)), out_shape=_sds((R, C), F32),
        compiler_params=_cparams(VMEM_BIG))(a)


def add2(a, b, name):
    shp = a.shape
    C = shp[-1]
    a2, b2 = a.reshape(-1, C), b.reshape(-1, C)
    R = a2.shape[0]
    tr = _row_tile(R, C)

    def body(a_ref, b_ref, o_ref):
        o_ref[...] = a_ref[...] + b_ref[...]

    spec = pl.BlockSpec((tr, C), lambda i: (i, 0))
    return pl.pallas_call(body, name=name, grid=(R // tr,), in_specs=[spec, spec], out_specs=spec,
                          out_shape=_sds((R, C), F32), compiler_params=_cparams(VMEM_BIG))(a2, b2).reshape(shp)


def _row_tile(R, C=D, n=1, mult=8):
    limit = max(mult, (1 << 18) // (C * n))
    for t in range(limit - limit % mult, mult - 1, -mult):
        if R % t == 0:
            return t
    return R


def adamw(w, g, m, v, name):
    R, C = w.shape
    tr = _row_tile(R, C)
    c1 = 1.0 - ADAM_B1 ** ADAM_STEP
    c2 = 1.0 - ADAM_B2 ** ADAM_STEP

    def body(w_ref, g_ref, m_ref, v_ref, d_ref, nm_ref, nv_ref):
        g_ = g_ref[...]
        nm = ADAM_B1 * m_ref[...] + (1.0 - ADAM_B1) * g_
        nv = ADAM_B2 * v_ref[...] + (1.0 - ADAM_B2) * jnp.square(g_)
        d_ref[...] = -ADAM_LR * ((nm / c1) / (jnp.sqrt(nv / c2) + ADAM_EPS) + ADAM_WD * w_ref[...])
        nm_ref[...] = nm
        nv_ref[...] = nv

    spec = pl.BlockSpec((tr, C), lambda i: (i, 0))
    return pl.pallas_call(body, name=name, grid=(R // tr,), in_specs=[spec] * 4, out_specs=[spec] * 3,
                          out_shape=[_sds((R, C), F32)] * 3, compiler_params=_cparams(VMEM_BIG))(w, g, m, v)


def _rot_cols(w):
    return jnp.concatenate([-w[:, 16:32], w[:, 0:16]], axis=1)


def _fold_rot(d):
    return jnp.concatenate([d[:, 16:32], -d[:, 0:16]], axis=1)


def _pad_heads(w, off, axis):
    parts = []
    for h in range(HEADS):
        piece = lax.slice_in_dim(w, off + HD * h, off + HD * (h + 1), axis=axis)
        parts += [piece, jnp.zeros_like(piece)]
    return parts


def _unpad_heads(d, off, axis):
    return [lax.slice_in_dim(d, off + HP * h, off + HP * h + HD, axis=axis) for h in range(HEADS)]


def mix_in_ext(w):
    z = lambda n: jnp.zeros((w.shape[0], n), w.dtype)
    kr = w[:, 1408:1440]
    cols = [w[:, 0:1408], kr, _rot_cols(kr), z(64)]
    cols += _pad_heads(w, 1440, 1) + _pad_heads(w, 1696, 1) + _pad_heads(w, 1952, 1)
    cols += [w[:, 2212:2724], w[:, 2208:2212], z(NP - C_CF - HEADS)]
    return jnp.concatenate(cols, axis=1)


def mix_in_unext(d):
    kr = d[:, 1408:1440] + _fold_rot(d[:, 1440:1472])
    cols = [d[:, 0:1408], kr] + _unpad_heads(d, C_CQ, 1) + _unpad_heads(d, C_CK, 1) + _unpad_heads(d, C_CV, 1)
    cols += [d[:, C_CF:C_CF + HEADS], d[:, C_D:C_D + 2 * GW]]
    return jnp.concatenate(cols, axis=1)


def mix_out_ext(w):
    return jnp.concatenate([w[0:GW]] + _pad_heads(w, GW, 0) + _pad_heads(w, 2 * GW, 0) + [w[3 * GW:4 * GW]], axis=0)


def mix_out_unext(d):
    return jnp.concatenate([d[0:GW]] + _unpad_heads(d, O_B, 0) + _unpad_heads(d, O_C, 0) + [d[O_D:O_D + GW]], axis=0)


def uq_ext(w):
    z = lambda n: jnp.zeros((w.shape[0], n), w.dtype)
    a, b = [], []
    for h in range(HEADS):
        o = (B_NOPE + B_ROPE) * h
        a += [w[:, o:o + B_NOPE + B_ROPE], z(32)]
        b += [z(B_NOPE), _rot_cols(w[:, o + B_NOPE:o + B_NOPE + B_ROPE]), z(32)]
    return jnp.concatenate(a + b, axis=1)


def uq_unext(d):
    cols = []
    for h in range(HEADS):
        o = HP * h
        cols += [d[:, o:o + B_NOPE], d[:, o + B_NOPE:o + B_NOPE + B_ROPE]
                 + _fold_rot(d[:, HEADS * HP + o + B_NOPE:HEADS * HP + o + B_NOPE + B_ROPE])]
    return jnp.concatenate(cols, axis=1)


def ukv_ext(w):
    z = jnp.zeros((w.shape[0], HD), w.dtype)
    k, v = [], []
    for h in range(HEADS):
        k += [w[:, 2 * HD * h:2 * HD * h + HD], z]
        v += [w[:, 2 * HD * h + HD:2 * HD * (h + 1)], z]
    return jnp.concatenate(k + v, axis=1)


def ukv_unext(d):
    cols = []
    for h in range(HEADS):
        cols += [d[:, HP * h:HP * h + HD], d[:, HEADS * HP + HP * h:HEADS * HP + HP * h + HD]]
    return jnp.concatenate(cols, axis=1)


def rope_tables(S):
    half = B_ROPE // 2
    inv_freq = ROPE_THETA ** (-jnp.arange(half, dtype=F32) / half)
    ang = jnp.arange(S).astype(F32)[:, None] * inv_freq[None, :]
    cos = jnp.tile(jnp.cos(ang), (1, 2))
    sin = jnp.tile(jnp.sin(ang), (1, 2))
    one, zero = jnp.ones((S, B_NOPE), F32), jnp.zeros((S, B_NOPE), F32)
    z32 = jnp.zeros((S, 32), F32)
    cos_q = jnp.tile(jnp.concatenate([one, cos, z32], axis=1), (1, HEADS))
    sin_q = jnp.tile(jnp.concatenate([zero, sin, z32], axis=1), (1, HEADS))
    cs_k = jnp.concatenate([cos, sin, zero], axis=1)
    place = np.zeros((HP, HEADS * HP), np.float32)
    for h in range(HEADS):
        for j in range(B_ROPE):
            place[j, h * HP + B_NOPE + j] = 1.0
            place[B_ROPE + j, h * HP + B_NOPE + j] = 1.0
    return cos_q, sin_q, cs_k, jnp.asarray(place, BF16)


def layer_fwd(x, mod, get, tabs, S):
    cos_q, sin_q, cs_k, place = tabs
    p = dict(get("ffn1", x))
    l = p["wl"]
    zg1, zu1, act1 = ffn_in_fwd(x, mod, p["ffn1_in"], l, 0, 1, S)
    y1, x1 = out_ln_fwd(act1, p["ffn1_out"], x, mod, p["ln_g"][0:1], p["ln_b"][0:1], 2, 0.5, S, l)
    p.update(get("mix", x1))
    proj = mix_in_fwd(x1, mod, p["mix_in"], 3, 4, S)
    o_a, states = hgrn_fwd(proj, p["lb"], p["ng"], S)
    q_b, k_b, v_b = mla_pre_fwd(proj, p["gq"], p["gkv"], p["wq"], p["wkv"], place, cos_q, sin_q, cs_k, S)
    o_b = attn_fwd(q_b, 0, k_b, 0, v_b, 0, None, (B_NOPE + B_ROPE) ** -0.5, S)
    cum = fox_gate_fwd(proj, p["bcol"], S)
    o_c = attn_fwd(proj, C_CQ // HP, proj, C_CK // HP, proj, C_CV // HP, cum, HD ** -0.5, S)
    o_d = gmlp_fwd(proj, p["g_lg"], p["g_lb"], p["ws"], p["bs"])
    cat = jnp.concatenate([o_a, o_b, o_c, o_d], axis=1)
    y2, x2 = out_ln_fwd(cat, p["mix_out"], x1, mod, p["ln_g"][1:2], p["ln_b"][1:2], 5, 1.0, S)
    p.update(get("ffn2", x2))
    zg3, zu3, act3 = ffn_in_fwd(x2, mod, p["ffn2_in"], l, 6, 7, S)
    y3, x3 = out_ln_fwd(act3, p["ffn2_out"], x2, mod, p["ln_g"][2:3], p["ln_b"][2:3], 8, 0.5, S, l)
    saved = dict(x=x, zg1=zg1, zu1=zu1, act1=act1, y1=y1, x1=x1, proj=proj, states=states, q_b=q_b, k_b=k_b, v_b=v_b,
                 cum=cum, cat=cat, y2=y2, x2=x2, zg3=zg3, zu3=zu3, act3=act3, y3=y3, p=p)
    return x3, saved


def _ffn_bwd(dxn, x_in, y, zg, zu, act, mod, w_in, w_out, l, lg, lb, idx, S, emit):
    sh, sc, gate = idx
    dres, dy, dgate, dlg, dlb = ln_res_bwd(dxn, x_in, y, mod, lg, lb, gate, 0.5, S)
    dzg, dzu = swiglu_bwd(dy, w_out, l, zg, zu, S)
    dw_out = tn_mm(act, dy, FF // 2).reshape(N_CHIPS, FF // N_CHIPS, D)
    dw_in = tn_mm_mod_shards(x_in, mod, dzg, dzu, sh, sc, S)
    mod = mod + emit(dw_in, dw_out)
    dx, dsh, dsc = nt_mod_bwd([dzg, dzu], w_in, [0, 2], x_in, mod, dres, sc, S, FF // 2, l)
    return dx, dw_in, dw_out, dlg, dlb, {sh: dsh, sc: dsc, gate: dgate}, mod


def layer_bwd(dx3, mod, sv, tabs, S, emit):
    cos_q, sin_q, cs_k, place = tabs
    p = sv["p"]
    l = p["wl"]
    g = {}
    dm = {}

    def emit_ffn(part):
        def f(dw_in, dw_out):
            g[part + "_in"], g[part + "_out"] = dw_in, dw_out
            return emit(part, g)
        return f

    dx2, _, _, dlg2, dlb2, d, mod = _ffn_bwd(
        dx3, sv["x2"], sv["y3"], sv["zg3"], sv["zu3"], sv["act3"], mod, p["ffn2_in"], p["ffn2_out"], l,
        p["ln_g"][2:3], p["ln_b"][2:3], (6, 7, 8), S, emit_ffn("ffn2"))
    dm.update(d)
    dres, dy2, dm[5], dlg1, dlb1 = ln_res_bwd(dx2, sv["x1"], sv["y2"], mod, p["ln_g"][1:2], p["ln_b"][1:2], 5, 1.0, S)
    dcat = nt_plain(dy2, p["mix_out"])
    g["mix_out"] = tn_mm(sv["cat"], dy2, 768)
    proj = sv["proj"]
    d_a, g["lb"], g["ng"] = hgrn_bwd(proj, sv["states"], dcat, p["lb"], p["ng"], S)
    dq_c, dk_c, dv_c, dcum = attn_bwd(proj, C_CQ // HP, proj, C_CK // HP, proj, C_CV // HP, sv["cum"], dcat,
                                      O_C // HP, HD ** -0.5, S, BF16)
    B = proj.shape[0] // S
    dcum = jnp.concatenate([dcum.reshape(B, HEADS, S), jnp.zeros((B, 8 - HEADS, S), F32)], axis=1)
    d_cf, g["bcol"] = fox_gate_bwd(proj, p["bcol"], dcum, S)
    dq_b, dk_b, dv_b = attn_bwd(sv["q_b"], 0, sv["k_b"], 0, sv["v_b"], 0, None, dcat, O_B // HP,
                                (B_NOPE + B_ROPE) ** -0.5, S, F32)
    d_b, g["gq"], g["gkv"], g["wq"], g["wkv"] = mla_pre_bwd(
        proj, p["gq"], p["gkv"], p["wq"], p["wkv"], place, cos_q, sin_q, cs_k, dq_b, dk_b, dv_b, S)
    d_d, g["g_lg"], g["g_lb"], g["ws"], g["bs"] = gmlp_bwd(proj, p["g_lg"], p["g_lb"], p["ws"], p["bs"], dcat)
    dproj = jnp.concatenate([d_a, d_b, dq_c, dk_c, dv_c, d_d, d_cf, jnp.zeros_like(d_cf)], axis=1)
    g["mix_in"] = tn_mm_mod(sv["x1"], mod, dproj, 3, 4, S, NP_TILE)
    mod = mod + emit("mix", g)
    dx1, dm[3], dm[4] = nt_mod_bwd([dproj], p["mix_in"], [0], sv["x1"], mod, dres, 4, S, NP_TILE)
    last = []

    def emit_last(dw_in, dw_out):
        last.append(emit_ffn("ffn1")(dw_in, dw_out))
        return last[0]

    dx0, _, _, dlg0, dlb0, d, mod = _ffn_bwd(
        dx1, sv["x"], sv["y1"], sv["zg1"], sv["zu1"], sv["act1"], mod, p["ffn1_in"], p["ffn1_out"], l,
        p["ln_g"][0:1], p["ln_b"][0:1], (0, 1, 2), S, emit_last)
    dm.update(d)
    g["ln_g"] = jnp.concatenate([dlg0, dlg1, dlg2], axis=0)
    g["ln_b"] = jnp.concatenate([dlb0, dlb1, dlb2], axis=0)
    dmod = jnp.concatenate([dm[i] for i in range(N_MOD)], axis=1)
    return dx0, dmod, g, last[0]


def local_step(x, c8, tgt, get, lb_logits, S, emit=None):
    B = x.shape[0] // S
    tabs = rope_tables(S)
    lb_all = lb_fwd(lb_logits)
    mods, saved = [], []
    h = x
    for l in range(DEPTH):
        pa = get(l, "ada", h)
        mod = mod_fwd(c8, pa["ada_w"], pa["wl"], pa["ada_b"])[0:B].reshape(B, N_MOD, D)

        def get_l(part, after, l=l):
            p = dict(get(l, part, after))
            if part == "mix":
                p["lb"] = lb_all[l:l + 1]
            return p

        h, sv = layer_fwd(h, mod, get_l, tabs, S)
        mods.append(mod)
        saved.append(sv)
    loss_tile, dh = loss_head(h, tgt)
    grads, dmods, dlb = [None] * DEPTH, [None] * DEPTH, [None] * DEPTH
    tie = jnp.zeros((), F32)
    for l in reversed(range(DEPTH)):
        emit_l = (lambda part, g: jnp.zeros((), F32)) if emit is None else functools.partial(emit, l)
        dh, dmods[l], grads[l], tie = layer_bwd(dh, mods[l] + tie, saved[l], tabs, S, emit_l)
        dlb[l] = grads[l].pop("lb")
    d_logits = lb_bwd(lb_logits, jnp.concatenate(dlb, axis=0))
    return loss_tile, dh, dmods, grads, d_logits


ANY = pl.BlockSpec(memory_space=pl.ANY)


def _place():
    x, y, c = lax.axis_index("x"), lax.axis_index("y"), lax.axis_index("c")
    chips = [(1 - x, y), (x, 1 - y), (1 - x, 1 - y)]
    return x, y, c, chips


def _rcopy(src, dst, sems, k, to):
    send_sems, recv_sems = sems
    return pltpu.make_async_remote_copy(src_ref=src, dst_ref=dst, send_sem=send_sems.at[k], recv_sem=recv_sems.at[k],
                                        device_id=to, device_id_type=MESH)


def _dma_sems(n_remote, n_local):
    return [pltpu.SemaphoreType.DMA((n_remote,)), pltpu.SemaphoreType.DMA((n_remote,)), pltpu.SemaphoreType.DMA((n_local,))]


def own_slot(src, chip):
    L = src.shape[0]
    return lax.dynamic_update_slice(jnp.zeros((L, N_CHIPS) + src.shape[1:], src.dtype), src[:, None], (0, chip, 0, 0))


def ag_shards(arrs, lands):
    n = len(arrs)
    rh = [a.shape[1] // 2 for a in arrs]

    def body(*refs):
        srcs, outs, token = refs[:n], refs[2 * n:3 * n], refs[3 * n]
        send_sems, recv_sems = refs[3 * n + 1:]
        x, y, c, chips = _place()
        sems = (send_sems, recv_sems)
        me = 2 * x + y
        sibling = (x, y, 1 - c)
        token[...] = jnp.zeros_like(token)

        def part(i, k, hc):
            return outs[i].at[:, k, pl.ds(hc * rh[i], rh[i]), :]

        started = []
        for j, (px, py) in enumerate(chips):
            for i in range(n):
                cp = _rcopy(srcs[i].at[:, pl.ds(c * rh[i], rh[i]), :], part(i, me, c), sems, 6 * i + j, (px, py, c))
                cp.start()
                started.append(cp)
        for j, (px, py) in enumerate(chips):
            k = 2 * px + py
            for i in range(n):
                _rcopy(part(i, k, c), part(i, k, c), sems, 6 * i + j, (px, py, c)).wait_recv()
                cp = _rcopy(part(i, k, c), part(i, k, c), sems, 6 * i + 3 + j, sibling)
                cp.start()
                started.append(cp)
        for j, (px, py) in enumerate(chips):
            k = 2 * px + py
            for i in range(n):
                _rcopy(part(i, k, 1 - c), part(i, k, 1 - c), sems, 6 * i + 3 + j, sibling).wait_recv()
        for cp in started:
            cp.wait_send()

    outs = pl.pallas_call(
        body, name="ag_shards", out_shape=[_sds(a.shape, a.dtype) for a in lands] + [_sds((8, 128), F32)],
        in_specs=[ANY] * (2 * n), out_specs=[ANY] * n + [pl.BlockSpec(memory_space=pltpu.VMEM)],
        input_output_aliases={n + i: i for i in range(n)}, scratch_shapes=_dma_sems(6 * n, 1)[:2])(*arrs, *lands)
    return list(outs[:n]), outs[n]


HBM_SPEC = pl.BlockSpec(memory_space=pltpu.HBM)
SEM_SPEC = pl.BlockSpec(memory_space=pltpu.SEMAPHORE)
DATAFLOW = pltpu.SideEffectType.DATAFLOW_SIDE_EFFECTING


def _after(x, dep):
    return lax.optimization_barrier((x, dep))[0]


def _split_start(srcs, lands, copies, n_copies, dep, name):
    n, m = len(srcs), len(lands)

    def body(*refs):
        ins = refs[:n + m]
        send_sems, recv_sems = refs[n + m + 1], refs[n + m + 2]
        token = refs[-1]
        for k, (src, dst, to) in enumerate(copies(ins[:n], ins[n:], _place())):
            pltpu.make_async_remote_copy(src_ref=src, dst_ref=dst, send_sem=send_sems.at[k], recv_sem=recv_sems.at[k],
                                         device_id=to, device_id_type=MESH).start()
        token[...] = jnp.zeros_like(token)

    arrs = list(srcs) + list(lands)
    outs = pl.pallas_call(
        body, name=name,
        out_shape=(pltpu.SemaphoreType.DMA((n_copies,)), pltpu.SemaphoreType.DMA((n_copies,)),
                   *[pltpu.HBM(a.shape, a.dtype) for a in arrs], _sds((8, 128), F32)),
        in_specs=[HBM_SPEC] * (n + m) + [ANY],
        out_specs=(SEM_SPEC, SEM_SPEC, *[HBM_SPEC] * (n + m), pl.BlockSpec(memory_space=pltpu.VMEM)),
        input_output_aliases={i: 2 + i for i in range(n + m)},
        compiler_params=pltpu.CompilerParams(has_side_effects=DATAFLOW),
    )(*[pltpu.with_memory_space_constraint(a, pltpu.HBM) for a in arrs], dep)
    return outs[0], outs[1], list(outs[2:2 + n]), list(outs[2 + n:2 + n + m]), outs[-1]


def _split_wait(handle, arrivals, after, name):
    send_sems, recv_sems, srcs, lands, _ = handle
    n, m = len(srcs), len(lands)

    def body(*refs):
        ins = refs[:n + m]
        send_sems, recv_sems = refs[n + m], refs[n + m + 1]
        x, y, c, chips = place = _place()
        for k, (src, dst) in enumerate(arrivals(ins[:n], ins[n:], place)):
            cp = pltpu.make_async_remote_copy(src_ref=src, dst_ref=dst, send_sem=send_sems.at[k], recv_sem=recv_sems.at[k],
                                              device_id=(x, y, 1 - c), device_id_type=MESH)
            cp.wait_send()
            cp.wait_recv()

    arrs = list(srcs) + list(lands)
    outs = pl.pallas_call(
        body, name=name, out_shape=[pltpu.HBM(a.shape, a.dtype) for a in arrs],
        in_specs=[HBM_SPEC] * (n + m) + [SEM_SPEC, SEM_SPEC, ANY], out_specs=[HBM_SPEC] * (n + m),
        input_output_aliases={i: i for i in range(n + m)},
        compiler_params=pltpu.CompilerParams(has_side_effects=DATAFLOW),
    )(*arrs, send_sems, recv_sems, after)
    return list(outs[:n]), list(outs[n:])


def _ag_part(ref, k, hc):
    rh = ref.shape[2] // 2
    return ref.at[:, k, pl.ds(hc * rh, rh), :]


def ag_start(srcs, lands, dep, name):
    def copies(s, d, place):
        x, y, c, chips = place
        out = []
        for j, (px, py) in enumerate(chips):
            for i in range(len(s)):
                rh = s[i].shape[1] // 2
                out.append((s[i].at[:, pl.ds(c * rh, rh), :], _ag_part(d[i], 2 * x + y, c), (px, py, c)))
        return out

    return _split_start(srcs, lands, copies, 3 * len(srcs), dep, name)


def ag_wait(handle, after, name):
    def arrivals(s, d, place):
        x, y, c, chips = place
        out = []
        for j, (px, py) in enumerate(chips):
            for i in range(len(s)):
                rh = s[i].shape[1] // 2
                out.append((s[i].at[:, pl.ds(c * rh, rh), :], _ag_part(d[i], 2 * px + py, c)))
        return out

    return _split_wait(handle, arrivals, after, name)


def ag_forward(lands, name):
    n = len(lands)

    def body(*refs):
        bufs, token = refs[n:2 * n], refs[2 * n]
        send_sems, recv_sems = refs[2 * n + 1:]
        x, y, c, chips = _place()
        sems = (send_sems, recv_sems)
        token[...] = jnp.zeros_like(token)
        cps = []
        for j, (px, py) in enumerate(chips):
            for i in range(n):
                part = _ag_part(bufs[i], 2 * px + py, c)
                cps.append(_rcopy(part, part, sems, 3 * i + j, (x, y, 1 - c)))
        for cp in cps:
            cp.start()
        for j, (px, py) in enumerate(chips):
            for i in range(n):
                part = _ag_part(bufs[i], 2 * px + py, 1 - c)
                _rcopy(part, part, sems, 3 * i + j, (x, y, 1 - c)).wait_recv()
        for cp in cps:
            cp.wait_send()

    outs = pl.pallas_call(
        body, name=name, out_shape=[_sds(a.shape, a.dtype) for a in lands] + [_sds((8, 128), F32)],
        in_specs=[ANY] * n, out_specs=[ANY] * n + [pl.BlockSpec(memory_space=pltpu.VMEM)],
        input_output_aliases={i: i for i in range(n)}, scratch_shapes=_dma_sems(3 * n, 1)[:2])(*lands)
    return list(outs[:n]), outs[n]


def rs_start(hs, lands, dep, name):
    def copies(s, d, place):
        x, y, c, chips = place
        return [(s[i].at[2 * px + py], d[i].at[2 * x + y], (px, py, c)) for j, (px, py) in enumerate(chips) for i in range(len(s))]

    return _split_start(hs, lands, copies, 3 * len(hs), dep, name)


def _kept_out(ref, c):
    rh = ref.shape[1] // 2
    return ref.at[:, pl.ds((1 - c) * rh, rh), :]


def swap_start(arrs, lands, dep, name):
    def copies(s, d, place):
        x, y, c, _ = place
        return [(_kept_out(s[i], c), d[i], (x, y, 1 - c)) for i in range(len(s))]

    return _split_start(arrs, lands, copies, len(arrs), dep, name)


def swap_wait(handle, after, name):
    def arrivals(s, d, place):
        x, y, c, _ = place
        return [(_kept_out(s[i], c), d[i]) for i in range(len(s))]

    return _split_wait(handle, arrivals, after, name)


def rs_wait(handle, after, name):
    def arrivals(s, d, place):
        x, y, c, chips = place
        return [(s[i].at[2 * px + py], d[i].at[2 * px + py]) for j, (px, py) in enumerate(chips) for i in range(len(s))]

    return _split_wait(handle, arrivals, after, name)


def sibling_swap(arrs, name):
    n = len(arrs)
    rh = [a.shape[1] // 2 for a in arrs]

    def body(*refs):
        srcs, outs = refs[:n], refs[n:2 * n]
        send_sems, recv_sems = refs[2 * n:]
        x, y, c, _ = _place()
        cps = [_rcopy(srcs[i].at[:, pl.ds((1 - c) * rh[i], rh[i]), :], outs[i], (send_sems, recv_sems), i, (x, y, 1 - c))
               for i in range(n)]
        for cp in cps:
            cp.start()
        for cp in cps:
            cp.wait()

    return pl.pallas_call(
        body, name=name, out_shape=[_sds((N_CHIPS, r, a.shape[2]), a.dtype) for a, r in zip(arrs, rh)],
        in_specs=[ANY] * n, out_specs=[ANY] * n, scratch_shapes=_dma_sems(n, 1)[:2])(*arrs)


def chip_exchange(hs):
    n = len(hs)

    def body(*refs):
        srcs, outs = refs[:n], refs[n:2 * n]
        send_sems, recv_sems, loc_sems = refs[2 * n:]
        x, y, c, chips = _place()
        sems = (send_sems, recv_sems)
        me = 2 * x + y
        mine = [pltpu.make_async_copy(srcs[i].at[me], outs[i].at[me], loc_sems.at[i]) for i in range(n)]
        for cp in mine:
            cp.start()
        sends = []
        for j, (px, py) in enumerate(chips):
            for i in range(n):
                cp = _rcopy(srcs[i].at[2 * px + py], outs[i].at[me], sems, 3 * i + j, (px, py, c))
                cp.start()
                sends.append(cp)
        for j, (px, py) in enumerate(chips):
            for i in range(n):
                _rcopy(srcs[i].at[2 * px + py], outs[i].at[2 * px + py], sems, 3 * i + j, (px, py, c)).wait_recv()
        for cp in sends:
            cp.wait_send()
        for cp in mine:
            cp.wait()

    return pl.pallas_call(
        body, name="chip_exchange", out_shape=[_sds(h.shape, h.dtype) for h in hs],
        in_specs=[ANY] * n, out_specs=[ANY] * n, scratch_shapes=_dma_sems(3 * n, n))(*hs)


def sum_into(land, base, l, core, name):
    _, rh, C = land.shape
    tr = _row_tile(rh, C, N_CHIPS, mult=16)
    nr = rh // tr

    def body(core_ref, land_ref, base_ref, o_ref):
        acc = land_ref[0].astype(F32)
        for k in range(1, N_CHIPS):
            acc = acc + land_ref[k].astype(F32)
        o_ref[...] = acc

    grid_spec = pltpu.PrefetchScalarGridSpec(
        num_scalar_prefetch=1, grid=(nr,),
        in_specs=[pl.BlockSpec((N_CHIPS, tr, C), lambda r, core_ref: (0, r, 0)), ANY],
        out_specs=pl.BlockSpec((None, tr, C), lambda r, core_ref: (l, core_ref[0] * nr + r, 0)))
    return pl.pallas_call(body, name=name, grid_spec=grid_spec, out_shape=_sds(base.shape, base.dtype),
                          input_output_aliases={2: 0}, compiler_params=_cparams(VMEM_BIG))(
        core.reshape(1).astype(jnp.int32), land, base)


def sibling_join(bases, name):
    n = len(bases)

    def body(*refs):
        bufs = refs[n:2 * n]
        send_sems, recv_sems = refs[2 * n:]
        x, y, c, _ = _place()
        sems = (send_sems, recv_sems)

        def half(i, hc):
            rh = bufs[i].shape[1] // 2
            return bufs[i].at[:, pl.ds(hc * rh, rh), :]

        sends = [_rcopy(half(i, c), half(i, c), sems, i, (x, y, 1 - c)) for i in range(n)]
        for cp in sends:
            cp.start()
        for i in range(n):
            _rcopy(half(i, 1 - c), half(i, 1 - c), sems, i, (x, y, 1 - c)).wait_recv()
        for cp in sends:
            cp.wait_send()

    return pl.pallas_call(
        body, name=name, out_shape=[_sds(b.shape, b.dtype) for b in bases], in_specs=[ANY] * n, out_specs=[ANY] * n,
        input_output_aliases={i: i for i in range(n)}, scratch_shapes=_dma_sems(n, 1)[:2])(*bases)


def ag_all(blk):
    M, C = blk.shape

    def body(x_ref, out_ref, send_sems, recv_sems, loc_sem):
        x, y, c, chips = _place()
        sems = (send_sems, recv_sems)
        me, sibling = (x, y, c), (x, y, 1 - c)

        def slot(px, py, pc):
            return out_ref.at[4 * px + 2 * py + pc]

        mine = pltpu.make_async_copy(x_ref, slot(*me), loc_sem)
        mine.start()
        first = [_rcopy(x_ref, slot(*me), sems, 0, sibling)]
        first += [_rcopy(x_ref, slot(*me), sems, 1 + j, (*chip, c)) for j, chip in enumerate(chips)]
        for cp in first:
            cp.start()
        passed = [_rcopy(slot(*chip, c), slot(*chip, c), sems, 4 + j, sibling) for j, chip in enumerate(chips)]
        for j, chip in enumerate(chips):
            _rcopy(slot(*chip, c), slot(*chip, c), sems, 1 + j, me).wait_recv()
            passed[j].start()
        _rcopy(slot(*sibling), slot(*sibling), sems, 0, me).wait_recv()
        for j, chip in enumerate(chips):
            _rcopy(slot(*chip, 1 - c), slot(*chip, 1 - c), sems, 4 + j, me).wait_recv()
        for cp in first + passed:
            cp.wait_send()
        mine.wait()

    return pl.pallas_call(
        body, name="ag_all", out_shape=_sds((8, M, C), blk.dtype),
        in_specs=[pl.BlockSpec(memory_space=pltpu.VMEM)], out_specs=pl.BlockSpec(memory_space=pltpu.VMEM),
        scratch_shapes=[pltpu.SemaphoreType.DMA((7,)), pltpu.SemaphoreType.DMA((7,)), pltpu.SemaphoreType.DMA(())],
        compiler_params=_cparams(VMEM_BIG))(blk)


WEIGHTS = ["ada_w", "ada_b", "ln_g", "ln_b", "ffn1_w_in", "ffn1_w_out", "ffn2_w_in", "ffn2_w_out", "mix_w_in", "mix_w_out",
           "hgrn_lb_logits", "hgrn_norm_g", "mla_q_norm_g", "mla_kv_norm_g", "mla_w_uq", "mla_w_ukv", "fox_b_f",
           "gmlp_ln_g", "gmlp_ln_b", "gmlp_w_s", "gmlp_b_s"]
SHARDED = {"ffn1_w_in": 1, "ffn1_w_out": 0, "ffn2_w_in": 1, "ffn2_w_out": 0, "mix_w_in": 1, "mix_w_out": 0,
           "mla_w_uq": 1, "mla_w_ukv": 1}
SMALL = ["hgrn_lb_logits", "hgrn_norm_g", "mla_q_norm_g", "mla_kv_norm_g", "fox_b_f", "gmlp_ln_g", "gmlp_ln_b",
         "gmlp_w_s", "gmlp_b_s", "ln_g", "ln_b"]
GATHERED = ["ada_w", "ffn1_w_in", "ffn1_w_out", "ffn2_w_in", "ffn2_w_out", "mix_w_in", "mix_w_out", "mla_w_uq", "mla_w_ukv"]
REDUCED = GATHERED[1:]


def _col_shards(a):
    cols = a.shape[1] // N_CHIPS
    return jnp.stack([a[:, k * cols:(k + 1) * cols] for k in range(N_CHIPS)])


def add_kept_half(a, got, core, name):
    _, R, C = a.shape
    rh = R // 2
    tr = _row_tile(rh, C, mult=16)
    nr = rh // tr

    def body(core_ref, a_ref, b_ref, o_ref):
        o_ref[...] = (a_ref[...] + b_ref[...]).astype(o_ref.dtype)

    half = pl.BlockSpec((None, tr, C), lambda k, r, core_ref: (k, r, 0))
    grid_spec = pltpu.PrefetchScalarGridSpec(
        num_scalar_prefetch=1, grid=(N_CHIPS, nr),
        in_specs=[pl.BlockSpec((None, tr, C), lambda k, r, core_ref: (k, core_ref[0] * nr + r, 0)), half],
        out_specs=half)
    return pl.pallas_call(body, name=name, grid_spec=grid_spec, out_shape=_sds((N_CHIPS, rh, C), BF16),
                          compiler_params=_cparams(VMEM_BIG))(core.reshape(1).astype(jnp.int32), a, got)


def _rows(parts, n_rows, dtype):
    flat = jnp.concatenate([p.reshape(-1) for p in parts])
    pad = n_rows * D - flat.shape[0]
    return jnp.concatenate([flat, jnp.zeros((pad,), dtype)]).reshape(n_rows, D)


def _take(flat, shapes):
    out, o = [], 0
    for shp in shapes:
        n = int(np.prod(shp))
        out.append(flat[o:o + n].reshape(shp))
        o += n
    return out


def _round_up(n, m):
    return -(-n // m) * m


def pack_shard(w):
    parts = [w[n][l] for l in range(DEPTH) for n in SHARDED] + [w[n][l] for l in range(DEPTH) for n in ("ln_g", "ln_b")]
    n = sum(int(np.prod(p.shape)) for p in parts)
    return _rows(parts, _round_up(-(-n // D), 16), F32)


def unpack_shard(pk, like):
    shapes = [like[n].shape[1:] for l in range(DEPTH) for n in SHARDED] + [like[n].shape[1:] for l in range(DEPTH) for n in ("ln_g", "ln_b")]
    pieces = _take(pk.reshape(-1), shapes)
    names = [n for l in range(DEPTH) for n in SHARDED] + [n for l in range(DEPTH) for n in ("ln_g", "ln_b")]
    out = {}
    for n in list(SHARDED) + ["ln_g", "ln_b"]:
        out[n] = jnp.stack([p for p, m in zip(pieces, names) if m == n])
    return out


def pack_small(w):
    parts = [w[n][l] for l in range(DEPTH) for n in SMALL]
    n = sum(int(np.prod(p.shape)) for p in parts)
    return _rows(parts, _round_up(-(-n // D), 8), F32)


def unpack_small(pk, like):
    shapes = [like[n].shape[1:] for l in range(DEPTH) for n in SMALL]
    pieces = _take(pk.reshape(-1), shapes)
    names = [n for l in range(DEPTH) for n in SMALL]
    return {n: jnp.stack([p for p, m in zip(pieces, names) if m == n]) for n in SMALL}


def pack_gather(w):
    parts = [w[n][l].astype(BF16) for l in range(DEPTH) for n in ["ada_w"] + list(SHARDED)]
    ln = jnp.concatenate([w[n][l].reshape(-1) for l in range(DEPTH) for n in ("ln_g", "ln_b")])
    parts.append(lax.bitcast_convert_type(ln, BF16))
    n = sum(int(np.prod(p.shape)) for p in parts)
    return _rows(parts, _round_up(-(-n // D), 16), BF16)


def unpack_gather(g, w):
    names = ["ada_w"] + list(SHARDED)
    shapes = [w[n].shape[1:] for l in range(DEPTH) for n in names]
    n_ln = DEPTH * 2 * 3 * (D // N_CHIPS)
    flat = g.reshape(N_CHIPS, -1)
    per_chip = [_take(flat[k], shapes + [(n_ln, 2)]) for k in range(N_CHIPS)]
    layers = [dict() for _ in range(DEPTH)]
    i = 0
    for l in range(DEPTH):
        for n in names:
            axis = 1 if n == "ada_w" else SHARDED[n]
            layers[l][n] = jnp.concatenate([per_chip[k][i] for k in range(N_CHIPS)], axis=axis)
            i += 1
    ln = [lax.bitcast_convert_type(per_chip[k][i], F32).reshape(DEPTH, 2, 3, D // N_CHIPS) for k in range(N_CHIPS)]
    ln = jnp.concatenate(ln, axis=3)
    for l in range(DEPTH):
        layers[l]["ln_g"], layers[l]["ln_b"] = ln[l, 0], ln[l, 1]
    return layers


def pack_grads(grads, k):
    parts = []
    for l in range(DEPTH):
        g = grads[l]
        full = {"ffn1_w_out": g["ffn1_out"], "ffn2_w_out": g["ffn2_out"], "mix_w_in": mix_in_unext(g["mix_in"]),
                "mix_w_out": mix_out_unext(g["mix_out"]), "mla_w_uq": uq_unext(g["wq"]), "mla_w_ukv": ukv_unext(g["wkv"])}
        for n, axis in SHARDED.items():
            if n in ("ffn1_w_in", "ffn2_w_in"):
                half = g[n.replace("_w_in", "_in")][k // 2]
                parts.append(half[:, (k % 2) * (FF // 2):(k % 2 + 1) * (FF // 2)])
            else:
                sz = full[n].shape[axis] // N_CHIPS
                parts.append(lax.slice_in_dim(full[n], k * sz, (k + 1) * sz, axis=axis))
    for l in range(DEPTH):
        for n in ("ln_g", "ln_b"):
            parts.append(grads[l][n][:, k * (D // N_CHIPS):(k + 1) * (D // N_CHIPS)])
    n = sum(int(np.prod(p.shape)) for p in parts)
    return _rows(parts, _round_up(-(-n // D), 16), F32)


def kernel(x, c, ada_w, ada_b, ln_g, ln_b, ffn1_w_in, ffn1_w_out, ffn2_w_in, ffn2_w_out, mix_w_in, mix_w_out, hgrn_lb_logits, hgrn_norm_g, mla_q_norm_g, mla_kv_norm_g, mla_w_uq, mla_w_ukv, fox_b_f, gmlp_ln_g, gmlp_ln_b, gmlp_w_s, gmlp_b_s, loss_target, m_ada_w, m_ada_b, m_ln_g, m_ln_b, m_ffn1_w_in, m_ffn1_w_out, m_ffn2_w_in, m_ffn2_w_out, m_mix_w_in, m_mix_w_out, m_hgrn_lb_logits, m_hgrn_norm_g, m_mla_q_norm_g, m_mla_kv_norm_g, m_mla_w_uq, m_mla_w_ukv, m_fox_b_f, m_gmlp_ln_g, m_gmlp_ln_b, m_gmlp_w_s, m_gmlp_b_s, v_ada_w, v_ada_b, v_ln_g, v_ln_b, v_ffn1_w_in, v_ffn1_w_out, v_ffn2_w_in, v_ffn2_w_out, v_mix_w_in, v_mix_w_out, v_hgrn_lb_logits, v_hgrn_norm_g, v_mla_q_norm_g, v_mla_kv_norm_g, v_mla_w_uq, v_mla_w_ukv, v_fox_b_f, v_gmlp_ln_g, v_gmlp_ln_b, v_gmlp_w_s, v_gmlp_b_s):
    w = dict(zip(WEIGHTS, (ada_w, ada_b, ln_g, ln_b, ffn1_w_in, ffn1_w_out, ffn2_w_in, ffn2_w_out, mix_w_in, mix_w_out, hgrn_lb_logits, hgrn_norm_g, mla_q_norm_g, mla_kv_norm_g, mla_w_uq, mla_w_ukv, fox_b_f, gmlp_ln_g, gmlp_ln_b, gmlp_w_s, gmlp_b_s)))
    m = dict(zip(WEIGHTS, (m_ada_w, m_ada_b, m_ln_g, m_ln_b, m_ffn1_w_in, m_ffn1_w_out, m_ffn2_w_in, m_ffn2_w_out, m_mix_w_in, m_mix_w_out, m_hgrn_lb_logits, m_hgrn_norm_g, m_mla_q_norm_g, m_mla_kv_norm_g, m_mla_w_uq, m_mla_w_ukv, m_fox_b_f, m_gmlp_ln_g, m_gmlp_ln_b, m_gmlp_w_s, m_gmlp_b_s)))
    v = dict(zip(WEIGHTS, (v_ada_w, v_ada_b, v_ln_g, v_ln_b, v_ffn1_w_in, v_ffn1_w_out, v_ffn2_w_in, v_ffn2_w_out, v_mix_w_in, v_mix_w_out, v_hgrn_lb_logits, v_hgrn_norm_g, v_mla_q_norm_g, v_mla_kv_norm_g, v_mla_w_uq, v_mla_w_ukv, v_fox_b_f, v_gmlp_ln_g, v_gmlp_ln_b, v_gmlp_w_s, v_gmlp_b_s)))
    Bl, S, _ = x.shape
    T = Bl * S
    core = lax.axis_index("c")
    chip = 2 * lax.axis_index("x") + lax.axis_index("y")

    def shard(key):
        n, l = key
        if n == "ln":
            return jnp.concatenate([ln_g[l:l + 1], ln_b[l:l + 1], jnp.zeros((1, 2, D // N_CHIPS), F32)], axis=1)
        return w[n][l:l + 1].astype(BF16)

    mixers = ["mix_w_in", "mix_w_out", "mla_w_uq", "mla_w_ukv"]
    groups = [[("ada_w", 0), ("ffn1_w_in", 0), ("ffn1_w_out", 0), ("ln", 0)],
              [(n, 0) for n in mixers + ["ffn2_w_in", "ffn2_w_out"]],
              [(n, 1) for n in GATHERED + ["ln"]]]
    srcs = [[shard(k) for k in grp] for grp in groups]
    lands = [[own_slot(s, chip) for s in grp] for grp in srcs]
    handle0 = ag_start(srcs[0], lands[0], jnp.zeros((8, 128), F32), "ag_start_0")
    first, token = ag_forward(ag_wait(handle0, lands[2][0], "ag_wait_0")[1], "ag_forward_0")
    have = dict(zip(groups[0], first))
    handles = {}
    for gi in (1, 2):
        handles[gi] = ag_start(srcs[gi], lands[gi], token, "ag_start_%d" % gi)
        token = handles[gi][-1]
    c8 = jnp.concatenate([c, jnp.zeros((8 - Bl, D), F32)], axis=0)
    c8 = c8 + token[0, 0]

    def cat_cols(a):
        return jnp.concatenate([a[0, k] for k in range(N_CHIPS)], axis=1)

    def get(l, part, after):
        gi = 2 if l == 1 else (0 if part in ("ada", "ffn1") else 1)
        if gi in handles:
            arrived, _ = ag_forward(ag_wait(handles.pop(gi), after, "ag_wait_%d" % gi)[1], "ag_forward_%d" % gi)
            have.update(zip(groups[gi], arrived))
        if part == "ada":
            return dict(ada_w=have[("ada_w", l)], wl=0, ada_b=ada_b[l][None])
        if part == "ffn1":
            ln_full = jnp.moveaxis(have[("ln", l)][0], 0, 1).reshape(8, D)
            return dict(ffn1_in=have[("ffn1_w_in", l)], ffn1_out=have[("ffn1_w_out", l)], wl=0,
                        ln_g=ln_full[0:3], ln_b=ln_full[3:6])
        if part == "ffn2":
            return dict(ffn2_in=have[("ffn2_w_in", l)], ffn2_out=have[("ffn2_w_out", l)])
        return dict(
            mix_in=mix_in_ext(cat_cols(have[("mix_w_in", l)])), mix_out=mix_out_ext(have[("mix_w_out", l)].reshape(D, D)),
            wq=uq_ext(cat_cols(have[("mla_w_uq", l)])).astype(F32), wkv=ukv_ext(cat_cols(have[("mla_w_ukv", l)])).astype(F32),
            ng=hgrn_norm_g[l][None], gq=mla_q_norm_g[l][None], gkv=mla_kv_norm_g[l][None],
            bcol=jnp.concatenate([fox_b_f[l], jnp.zeros((8 - HEADS,), F32)])[:, None],
            g_lg=gmlp_ln_g[l][None], g_lb=gmlp_ln_b[l][None], ws=gmlp_w_s[l], bs=gmlp_b_s[l][:, :, None])

    pending, swapping = [], []

    def chip_stage(after):
        l, names, handle, tag = swapping.pop()
        by_chip, got = swap_wait(handle, after, "swap_wait_" + tag)
        chip_sum = [add_kept_half(a, r, core, "add_sibling") for a, r in zip(by_chip, got)]
        slot = lax.broadcasted_iota(jnp.int32, (N_CHIPS, 1, 1), 0)
        zones = [jnp.where(slot == chip, h, 0.0) for h in chip_sum]
        handle = rs_start(chip_sum, zones, chip_sum[0], "rs_start_" + tag)
        pending.append((l, names, handle, tag))
        return handle[-1][0, 0]

    def emit(l, part, g):
        if part == "mix":
            names = mixers
            by_chip = [_col_shards(mix_in_unext(g["mix_in"])), mix_out_unext(g["mix_out"]).reshape(N_CHIPS, D // N_CHIPS, D),
                       _col_shards(uq_unext(g["wq"])), _col_shards(ukv_unext(g["wkv"]))]
        else:
            names = [part + "_w_in", part + "_w_out"]
            by_chip = [g[part + "_in"], g[part + "_out"]]
        tag = "%d_%s" % (l, part)
        zones = [lax.empty((N_CHIPS, a.shape[1] // 2, a.shape[2]), F32) for a in by_chip]
        handle = swap_start(by_chip, zones, by_chip[0], "swap_start_" + tag)
        tie = handle[-1][0, 0]
        if swapping:
            tie = tie + chip_stage(handle[2][0])
        swapping.append((l, names, handle, tag))
        return tie

    loss_tile, dx, dmods, grads, d_logits = local_step(
        x.reshape(T, D), c8, loss_target.reshape(T, D), get, hgrn_lb_logits, S, emit)
    chip_stage(dx)
    loss = lax.psum(loss_tile[0, 0], ("x", "y", "c"))

    small_g = {"hgrn_lb_logits": d_logits,
               "hgrn_norm_g": jnp.stack([grads[l]["ng"][0] for l in range(DEPTH)]),
               "mla_q_norm_g": jnp.stack([grads[l]["gq"][0] for l in range(DEPTH)]),
               "mla_kv_norm_g": jnp.stack([grads[l]["gkv"][0] for l in range(DEPTH)]),
               "fox_b_f": jnp.stack([grads[l]["bcol"][0:HEADS, 0] for l in range(DEPTH)]),
               "gmlp_ln_g": jnp.stack([grads[l]["g_lg"][0] for l in range(DEPTH)]),
               "gmlp_ln_b": jnp.stack([grads[l]["g_lb"][0] for l in range(DEPTH)]),
               "gmlp_w_s": jnp.stack([grads[l]["ws"] for l in range(DEPTH)]),
               "gmlp_b_s": jnp.stack([grads[l]["bs"][:, :, 0] for l in range(DEPTH)])}
    small_g["ln_g"] = jnp.stack([grads[l]["ln_g"] for l in range(DEPTH)])
    small_g["ln_b"] = jnp.stack([grads[l]["ln_b"] for l in range(DEPTH)])
    pk_small = pack_small(small_g)
    n_small = pk_small.shape[0]
    extras = [dmods[l] for l in range(DEPTH)] + [c]
    n_extra = _round_up(-(-sum(int(np.prod(e.shape)) for e in extras) // D), 8)
    gathered = ag_all(jnp.concatenate([pk_small, _rows(extras, n_extra, F32)], axis=0))
    g_small = unpack_small(sum_slots(gathered[:, 0:n_small], 8, "sum_small"), small_g)
    ext = gathered[:, n_small:].reshape(8, -1)
    n_dmod = DEPTH * Bl * N_MOD * D
    dmod_all = ext[:, 0:n_dmod].reshape(8, DEPTH, Bl, N_MOD * D)
    c_all = ext[:, n_dmod:n_dmod + Bl * D].reshape(8 * Bl, D)
    g_ada_w, g_ada_b = [], []
    ncol = N_MOD * D // N_CHIPS
    for l in range(DEPTH):
        dm = dmod_all[:, l].reshape(8 * Bl, N_MOD * D)
        g_ada_w.append(ada_grad(c_all, lax.dynamic_slice_in_dim(dm, chip * ncol, ncol, axis=1)))
        g_ada_b.append(sum_slots(dm.reshape(8 * Bl, N_MOD, D), 8 * Bl, "sum_ada_b").reshape(N_MOD * D))
    g_ada_w, g_ada_b = jnp.stack(g_ada_w), jnp.stack(g_ada_b)

    red = {n: jnp.zeros(w[n].shape, F32) for n in REDUCED}

    def arrive(entry, after):
        l, names, handle, tag = entry
        for n, land in zip(names, rs_wait(handle, after, "rs_wait_" + tag)[1]):
            red[n] = sum_into(land, red[n], l, core, "sum_chips")

    for entry in pending[:-1]:
        arrive(entry, dx)
    late = pending[-1][1]
    early = [n for n in REDUCED if n not in late]
    grad = dict(zip(early, sibling_join([red[n] for n in early], "sibling_join_a")))
    grad.update(g_small)
    grad["ada_w"], grad["ada_b"] = g_ada_w, g_ada_b
    for n in ("ln_g", "ln_b"):
        grad[n] = lax.dynamic_slice_in_dim(g_small[n], chip * (D // N_CHIPS), D // N_CHIPS, axis=2)
    out = {"grad": grad, "delta": {}, "new_m": {}, "new_v": {}}

    def update(n):
        shp = w[n].shape
        two_d = (-1, shp[-1])
        res = adamw(w[n].reshape(two_d), grad[n].reshape(two_d), m[n].reshape(two_d), v[n].reshape(two_d), "adamw_" + n)
        grad[n] = grad[n].reshape(shp)
        for key, r in zip(("delta", "new_m", "new_v"), res):
            out[key][n] = r.reshape(shp)

    for n in WEIGHTS:
        if n not in late:
            update(n)
    arrive(pending[-1], out["delta"]["ffn2_w_in"])
    grad.update(zip(late, sibling_join([red[n] for n in late], "sibling_join_b")))
    for n in late:
        update(n)
    outs = [loss, dx.reshape(Bl, S, D)]
    for key in ("grad", "delta", "new_m", "new_v"):
        outs += [out[key][n] for n in WEIGHTS]
    return tuple(outs)
```

```python
import functools

import jax
import jax.numpy as jnp
import numpy as np
from jax import lax
from jax.experimental import pallas as pl
from jax.experimental.pallas import tpu as pltpu

F32, BF16 = jnp.float32, jnp.bfloat16
MESH = pl.DeviceIdType.MESH

N_CHIPS = 4
D = 1024
DEPTH = 2
FF = 2816
N_MOD = 9
GW = 256
HEADS = 4
HD = 64
HP = 128
A_CHUNK = 16
LB_FLOOR = 1e-30
B_Q_LORA, B_KV_LORA, B_NOPE, B_ROPE = 256, 128, 64, 32
ROPE_THETA = 10000.0
D_CHUNK = 128
MIX_COLS = 2724
ALPHA = (2 * DEPTH) ** 0.25
LN_EPS = 1e-5
RMS_EPS = 1e-6
ADAM_LR, ADAM_B1, ADAM_B2, ADAM_EPS, ADAM_WD, ADAM_STEP = 0.001, 0.9, 0.999, 1e-08, 0.01, 10

NP = 3840
NP_TILE = 1920
C_A, C_B, C_CQ, C_CK, C_CV, C_D, C_CF = 0, 1024, 1536, 2048, 2560, 3072, 3584
NCAT = 1536
O_A, O_B, O_C, O_D = 0, 256, 768, 1280

VMEM_BIG = 48 << 20


def _cparams(vmem=None):
    return pltpu.CompilerParams(vmem_limit_bytes=vmem) if vmem else pltpu.CompilerParams()


def _sds(shape, dtype):
    return jax.ShapeDtypeStruct(tuple(shape), dtype)


@jax.custom_vjp
def _mm(a, w):
    return jnp.dot(a.astype(BF16), w.astype(BF16), preferred_element_type=F32)


def _mm_f(a, w):
    return _mm(a, w), (a, w)


def _mm_b(res, g):
    a, w = res
    gb = g.astype(BF16)
    da = lax.dot_general(gb, w.astype(BF16), (((1,), (1,)), ((), ())), preferred_element_type=F32)
    dw = lax.dot_general(a.astype(BF16), gb, (((0,), (0,)), ((), ())), preferred_element_type=F32)
    return da.astype(a.dtype), dw.astype(w.dtype)


_mm.defvjp(_mm_f, _mm_b)


@jax.custom_vjp
def _mm_nt(a, b):
    return lax.dot_general(a.astype(BF16), b.astype(BF16), (((1,), (1,)), ((), ())), preferred_element_type=F32)


def _mm_nt_f(a, b):
    return _mm_nt(a, b), (a, b)


def _mm_nt_b(res, g):
    a, b = res
    gb = g.astype(BF16)
    da = jnp.dot(gb, b.astype(BF16), preferred_element_type=F32)
    db = lax.dot_general(gb, a.astype(BF16), (((0,), (0,)), ((), ())), preferred_element_type=F32)
    return da.astype(a.dtype), db.astype(b.dtype)


_mm_nt.defvjp(_mm_nt_f, _mm_nt_b)


@jax.custom_vjp
def _mm_tn(a, b):
    return lax.dot_general(a.astype(BF16), b.astype(BF16), (((0,), (0,)), ((), ())), preferred_element_type=F32)


def _mm_tn_f(a, b):
    return _mm_tn(a, b), (a, b)


def _mm_tn_b(res, g):
    a, b = res
    gb = g.astype(BF16)
    da = lax.dot_general(b.astype(BF16), gb, (((1,), (1,)), ((), ())), preferred_element_type=F32)
    db = jnp.dot(a.astype(BF16), gb, preferred_element_type=F32)
    return da.astype(a.dtype), db.astype(b.dtype)


_mm_tn.defvjp(_mm_tn_f, _mm_tn_b)


def _mm_hi(a, w):
    return jnp.dot(a, w, precision=lax.Precision.HIGHEST, preferred_element_type=F32)


def _iota(shape, dim):
    return lax.broadcasted_iota(jnp.int32, shape, dim)


def _head_sum_mats():
    e = (_iota((GW, HP), 0) // HD == _iota((GW, HP), 1)).astype(F32)
    et = (_iota((HP, GW), 1) // HD == _iota((HP, GW), 0)).astype(F32)
    return e, et


def _modulate(x, mod_ref, sh, sc):
    return x * (1.0 + mod_ref[sc:sc + 1, :]) + mod_ref[sh:sh + 1, :]


def _ln_res(x, y, gate, lg, lb, gs):
    r = ALPHA * x + gs * (1.0 + gate) * y
    mu = jnp.mean(r, axis=-1, keepdims=True)
    var = jnp.mean(jnp.square(r - mu), axis=-1, keepdims=True)
    return (r - mu) * lax.rsqrt(var + LN_EPS) * lg + lb


def _rms(x, g):
    return x * lax.rsqrt(jnp.mean(x * x, axis=-1, keepdims=True) + RMS_EPS) * g


def _tile(n, pref):
    return pref if n % pref == 0 else n


def mod_fwd(c8, w, l, b):
    tn = w.shape[3]
    n = N_CHIPS * tn

    def body(c_ref, w_ref, b_ref, o_ref):
        h = jax.nn.silu(c_ref[...]).astype(BF16)
        o_ref[...] = jnp.dot(h, w_ref[...], preferred_element_type=F32) + b_ref[...]

    return pl.pallas_call(
        body, name="mod_fwd", grid=(N_CHIPS,),
        in_specs=[pl.BlockSpec((8, D), lambda j: (0, 0)), pl.BlockSpec((None, None, D, tn), lambda j: (l, j, 0, 0)),
                  pl.BlockSpec((1, tn), lambda j: (0, j))],
        out_specs=pl.BlockSpec((8, tn), lambda j: (0, j)), out_shape=_sds((8, n), F32),
        compiler_params=_cparams(VMEM_BIG))(c8, w, b)


def ffn_in_fwd(x, mod, w_in, l, sh, sc, S):
    T = x.shape[0]
    tm, tn = _tile(S, 512), FF // 2
    tpb, nj = S // tm, 2

    def body(x_ref, mod_ref, wg_ref, wu_ref, zg_ref, zu_ref, act_ref, h_ref):
        @pl.when(pl.program_id(1) == 0)
        def _():
            h_ref[...] = _modulate(x_ref[...], mod_ref, sh, sc).astype(BF16)
        g = jnp.dot(h_ref[...], wg_ref[...], preferred_element_type=F32)
        u = jnp.dot(h_ref[...], wu_ref[...], preferred_element_type=F32)
        zg_ref[...] = g.astype(BF16)
        zu_ref[...] = u.astype(BF16)
        act_ref[...] = (jax.nn.silu(g) * u).astype(BF16)

    return pl.pallas_call(
        body, name="ffn_in_fwd", grid=(T // tm, nj),
        in_specs=[pl.BlockSpec((tm, D), lambda i, j: (i, 0)),
                  pl.BlockSpec((None, N_MOD, D), lambda i, j: (i // tpb, 0, 0)),
                  pl.BlockSpec((None, None, D, tn), lambda i, j: (l, j, 0, 0)),
                  pl.BlockSpec((None, None, D, tn), lambda i, j: (l, j + nj, 0, 0))],
        out_specs=[pl.BlockSpec((tm, tn), lambda i, j: (i, j))] * 3,
        out_shape=[_sds((T, FF), BF16)] * 3,
        scratch_shapes=[pltpu.VMEM((tm, D), BF16)],
        compiler_params=_cparams(VMEM_BIG))(x, mod, w_in, w_in)


def mix_in_fwd(x, mod, w, sh, sc, S):
    T = x.shape[0]
    n = w.shape[1]
    tm, tn = _tile(S, 512), NP_TILE
    tpb = S // tm

    def body(x_ref, mod_ref, w_ref, o_ref, h_ref):
        @pl.when(pl.program_id(1) == 0)
        def _():
            h_ref[...] = _modulate(x_ref[...], mod_ref, sh, sc).astype(BF16)
        o_ref[...] = jnp.dot(h_ref[...], w_ref[...], preferred_element_type=F32)

    return pl.pallas_call(
        body, name="mix_in_fwd", grid=(T // tm, n // tn),
        in_specs=[pl.BlockSpec((tm, D), lambda i, j: (i, 0)),
                  pl.BlockSpec((None, N_MOD, D), lambda i, j: (i // tpb, 0, 0)),
                  pl.BlockSpec((D, tn), lambda i, j: (0, j))],
        out_specs=pl.BlockSpec((tm, tn), lambda i, j: (i, j)), out_shape=_sds((T, n), F32),
        scratch_shapes=[pltpu.VMEM((tm, D), BF16)],
        compiler_params=_cparams(VMEM_BIG))(x, mod, w)


def out_ln_fwd(act, w_out, x, mod, lg, lb, gate, gs, S, l=None):
    T, K = act.shape
    tm = _tile(S, 512)
    tpb = S // tm

    def body(a_ref, w_ref, x_ref, mod_ref, lg_ref, lb_ref, y_ref, xn_ref):
        y = jnp.dot(a_ref[...], w_ref[...].reshape(K, D), preferred_element_type=F32)
        y_ref[...] = y
        xn_ref[...] = _ln_res(x_ref[...], y, mod_ref[gate:gate + 1, :], lg_ref[...], lb_ref[...], gs)

    if l is None:
        w_spec = pl.BlockSpec((K, D), lambda i: (0, 0))
    else:
        w_spec = pl.BlockSpec((None, N_CHIPS, K // N_CHIPS, D), lambda i: (l, 0, 0, 0))
    return pl.pallas_call(
        body, name="out_ln_fwd", grid=(T // tm,),
        in_specs=[pl.BlockSpec((tm, K), lambda i: (i, 0)), w_spec,
                  pl.BlockSpec((tm, D), lambda i: (i, 0)),
                  pl.BlockSpec((None, N_MOD, D), lambda i: (i // tpb, 0, 0)),
                  pl.BlockSpec((1, D), lambda i: (0, 0)), pl.BlockSpec((1, D), lambda i: (0, 0))],
        out_specs=[pl.BlockSpec((tm, D), lambda i: (i, 0))] * 2,
        out_shape=[_sds((T, D), F32), _sds((T, D), F32)],
        compiler_params=_cparams(VMEM_BIG))(act, w_out, x, mod, lg, lb)


def ln_res_bwd(dxn, x, y, mod, lg, lb, gate, gs, S):
    T = x.shape[0]
    B = T // S
    tm = _tile(S, 512)
    tpb = S // tm

    def body(d_ref, x_ref, y_ref, mod_ref, lg_ref, lb_ref, dx_ref, dy_ref, dg_ref, dlg_ref, dlb_ref):
        i = pl.program_id(0)
        f = functools.partial(_ln_res, gs=gs)
        _, vjp = jax.vjp(f, x_ref[...], y_ref[...], mod_ref[gate:gate + 1, :], lg_ref[...], lb_ref[...])
        dx, dy, dg, dlg, dlb = vjp(d_ref[...])
        dx_ref[...] = dx
        dy_ref[...] = dy.astype(BF16)

        @pl.when(i % tpb == 0)
        def _():
            dg_ref[...] = jnp.zeros_like(dg_ref)

        @pl.when(i == 0)
        def _():
            dlg_ref[...] = jnp.zeros_like(dlg_ref)
            dlb_ref[...] = jnp.zeros_like(dlb_ref)

        dg_ref[...] += dg
        dlg_ref[...] += dlg
        dlb_ref[...] += dlb

    tok = pl.BlockSpec((tm, D), lambda i: (i, 0))
    vec = pl.BlockSpec((1, D), lambda i: (0, 0))
    return pl.pallas_call(
        body, name="ln_res_bwd", grid=(T // tm,),
        in_specs=[tok, tok, tok, pl.BlockSpec((None, N_MOD, D), lambda i: (i // tpb, 0, 0)), vec, vec],
        out_specs=[tok, tok, pl.BlockSpec((None, 1, D), lambda i: (i // tpb, 0, 0)), vec, vec],
        out_shape=[_sds((T, D), F32), _sds((T, D), BF16), _sds((B, 1, D), F32), _sds((1, D), F32), _sds((1, D), F32)],
        compiler_params=_cparams(VMEM_BIG))(dxn, x, y, mod, lg, lb)


def swiglu_bwd(dy, w_out, l, zg, zu, S):
    T = dy.shape[0]
    tm, tn = _tile(S, 512), FF // 2

    def body(dy_ref, w_ref, zg_ref, zu_ref, dg_ref, du_ref):
        da = lax.dot_general(dy_ref[...], w_ref[...].reshape(tn, D), (((1,), (1,)), ((), ())), preferred_element_type=F32)
        g, u = zg_ref[...].astype(F32), zu_ref[...].astype(F32)
        sg = jax.nn.sigmoid(g)
        dg_ref[...] = (da * u * (sg * (1.0 + g * (1.0 - sg)))).astype(BF16)
        du_ref[...] = (da * (g * sg)).astype(BF16)

    zt = pl.BlockSpec((tm, tn), lambda i, j: (i, j))
    return pl.pallas_call(
        body, name="swiglu_bwd", grid=(T // tm, FF // tn),
        in_specs=[pl.BlockSpec((tm, D), lambda i, j: (i, 0)),
                  pl.BlockSpec((None, 2, FF // N_CHIPS, D), lambda i, j: (l, j, 0, 0)), zt, zt],
        out_specs=[zt, zt], out_shape=[_sds((T, FF), BF16), _sds((T, FF), BF16)],
        compiler_params=_cparams(VMEM_BIG))(dy, w_out, zg, zu)


def nt_plain(dy, w):
    T = dy.shape[0]
    K = w.shape[0]
    tm = _tile(T, 512)

    def body(dy_ref, w_ref, o_ref):
        o_ref[...] = lax.dot_general(dy_ref[...], w_ref[...], (((1,), (1,)), ((), ())), preferred_element_type=F32)

    return pl.pallas_call(
        body, name="nt_plain", grid=(T // tm,),
        in_specs=[pl.BlockSpec((tm, D), lambda i: (i, 0)), pl.BlockSpec((K, D), lambda i: (0, 0))],
        out_specs=pl.BlockSpec((tm, K), lambda i: (i, 0)), out_shape=_sds((T, K), F32),
        compiler_params=_cparams(VMEM_BIG))(dy, w)


def tn_mm(a, b, tk):
    T, K = a.shape
    N = b.shape[1]
    tt = _tile(T, 512)

    def body(a_ref, b_ref, o_ref):
        @pl.when(pl.program_id(1) == 0)
        def _():
            o_ref[...] = jnp.zeros_like(o_ref)
        o_ref[...] += lax.dot_general(a_ref[...], b_ref[...], (((0,), (0,)), ((), ())), preferred_element_type=F32)

    return pl.pallas_call(
        body, name="tn_mm", grid=(K // tk, T // tt),
        in_specs=[pl.BlockSpec((tt, tk), lambda k, t: (t, k)), pl.BlockSpec((tt, N), lambda k, t: (t, 0))],
        out_specs=pl.BlockSpec((tk, N), lambda k, t: (k, 0)), out_shape=_sds((K, N), F32),
        compiler_params=_cparams(VMEM_BIG))(a, b)


def tn_mm_mod(x, mod, b, sh, sc, S, tn):
    T = x.shape[0]
    N = b.shape[1]
    tt = _tile(S, 512)
    tpb = S // tt

    def body(x_ref, mod_ref, b_ref, o_ref):
        @pl.when(pl.program_id(1) == 0)
        def _():
            o_ref[...] = jnp.zeros_like(o_ref)
        h = _modulate(x_ref[...], mod_ref, sh, sc).astype(BF16)
        o_ref[...] += lax.dot_general(h, b_ref[...], (((0,), (0,)), ((), ())), preferred_element_type=F32)

    return pl.pallas_call(
        body, name="tn_mm_mod", grid=(N // tn, T // tt),
        in_specs=[pl.BlockSpec((tt, D), lambda j, t: (t, 0)),
                  pl.BlockSpec((None, N_MOD, D), lambda j, t: (t // tpb, 0, 0)),
                  pl.BlockSpec((tt, tn), lambda j, t: (t, j))],
        out_specs=pl.BlockSpec((D, tn), lambda j, t: (0, j)), out_shape=_sds((D, N), F32),
        compiler_params=_cparams(VMEM_BIG))(x, mod, b)


def tn_mm_mod_shards(x, mod, bg, bu, sh, sc, S):
    T = x.shape[0]
    tn = FF // 2
    tt = _tile(S, 512)
    tpb = S // tt

    def body(x_ref, mod_ref, bg_ref, bu_ref, o_ref):
        j = pl.program_id(0)

        @pl.when(pl.program_id(1) == 0)
        def _():
            o_ref[...] = jnp.zeros_like(o_ref)
        h = _modulate(x_ref[...], mod_ref, sh, sc).astype(BF16)

        @pl.when(j < 2)
        def _():
            o_ref[...] += lax.dot_general(h, bg_ref[...], (((0,), (0,)), ((), ())), preferred_element_type=F32)

        @pl.when(j >= 2)
        def _():
            o_ref[...] += lax.dot_general(h, bu_ref[...], (((0,), (0,)), ((), ())), preferred_element_type=F32)

    return pl.pallas_call(
        body, name="tn_mm_mod_shards", grid=(N_CHIPS, T // tt),
        in_specs=[pl.BlockSpec((tt, D), lambda j, t: (t, 0)),
                  pl.BlockSpec((None, N_MOD, D), lambda j, t: (t // tpb, 0, 0)),
                  pl.BlockSpec((tt, tn), lambda j, t: (jnp.where(j < 2, t, 0), jnp.minimum(j, 1))),
                  pl.BlockSpec((tt, tn), lambda j, t: (jnp.where(j < 2, 0, t), jnp.maximum(j - 2, 0)))],
        out_specs=pl.BlockSpec((None, D, tn), lambda j, t: (j, 0, 0)), out_shape=_sds((N_CHIPS, D, tn), F32),
        compiler_params=_cparams(VMEM_BIG))(x, mod, bg, bu)


def nt_mod_bwd(ds, w, offs, x, mod, dres, sc, S, tk, l=None):
    T = x.shape[0]
    B = T // S
    tm = _tile(S, 512)
    tpb = S // tm
    Kd = ds[0].shape[1]
    nk = Kd // tk
    n_in = len(ds)

    def body(*refs):
        d_refs, w_refs = refs[:n_in], refs[n_in:2 * n_in]
        x_ref, mod_ref, r_ref, dx_ref, dsh_ref, dsc_ref, acc = refs[2 * n_in:]
        i, k = pl.program_id(0), pl.program_id(1)

        @pl.when(k == 0)
        def _():
            acc[...] = jnp.zeros_like(acc)

        for d_ref, w_ref in zip(d_refs, w_refs):
            acc[...] += lax.dot_general(d_ref[...], w_ref[...], (((1,), (1,)), ((), ())), preferred_element_type=F32)

        @pl.when(k == nk - 1)
        def _():
            dh = acc[...]
            dx_ref[...] = dh * (1.0 + mod_ref[sc:sc + 1, :]) + r_ref[...]

            @pl.when(i % tpb == 0)
            def _():
                dsh_ref[...] = jnp.zeros_like(dsh_ref)
                dsc_ref[...] = jnp.zeros_like(dsc_ref)

            dsh_ref[...] += jnp.sum(dh, axis=0, keepdims=True)
            dsc_ref[...] += jnp.sum(dh * x_ref[...], axis=0, keepdims=True)

    tok = pl.BlockSpec((tm, D), lambda i, k: (i, 0))
    vec = pl.BlockSpec((None, 1, D), lambda i, k: (i // tpb, 0, 0))
    in_specs = [pl.BlockSpec((tm, tk), lambda i, k: (i, k)) for _ in ds]
    if l is None:
        in_specs += [pl.BlockSpec((D, tk), functools.partial(lambda i, k, o: (0, k + o), o=off // tk)) for off in offs]
    else:
        in_specs += [pl.BlockSpec((None, None, D, tk), functools.partial(lambda i, k, o: (l, k + o, 0, 0), o=off)) for off in offs]
    in_specs += [tok, pl.BlockSpec((None, N_MOD, D), lambda i, k: (i // tpb, 0, 0)), tok]
    return pl.pallas_call(
        body, name="nt_mod_bwd", grid=(T // tm, nk), in_specs=in_specs,
        out_specs=[tok, vec, vec],
        out_shape=[_sds((T, D), F32), _sds((B, 1, D), F32), _sds((B, 1, D), F32)],
        scratch_shapes=[pltpu.VMEM((tm, D), F32)],
        compiler_params=_cparams(VMEM_BIG))(*ds, *([w] * n_in), x, mod, dres)


def loss_head(y, tgt):
    T = y.shape[0]
    tm = _tile(T, 512)

    def body(y_ref, t_ref, l_ref, d_ref):
        @pl.when(pl.program_id(0) == 0)
        def _():
            l_ref[...] = jnp.zeros_like(l_ref)
        e = y_ref[...] - t_ref[...]
        d_ref[...] = e * (1.0 / D)
        l_ref[...] += 0.5 * jnp.sum(jnp.sum(e * e, axis=1, keepdims=True) * (1.0 / D))

    tok = pl.BlockSpec((tm, D), lambda i: (i, 0))
    return pl.pallas_call(
        body, name="loss_head", grid=(T // tm,), in_specs=[tok, tok],
        out_specs=[pl.BlockSpec((8, 128), lambda i: (0, 0)), tok],
        out_shape=[_sds((8, 128), F32), _sds((T, D), F32)],
        compiler_params=_cparams(VMEM_BIG))(y, tgt)


def _hgrn_block(q, fz, inp, go, st, lb, ng, blk):
    nc = blk // A_CHUNK
    lb_eff = jnp.maximum(lb, LB_FLOOR)
    log_f = jnp.logaddexp(jnp.log(lb_eff), jnp.log1p(-lb) + jax.nn.log_sigmoid(fz))
    k = (1.0 - lb) * jax.nn.sigmoid(-fz) - (lb_eff - lb)
    qf = jax.nn.silu(q)
    same_chunk = _iota((blk, blk), 0) // A_CHUNK == _iota((blk, blk), 1) // A_CHUNK
    tril = (same_chunk & (_iota((blk, blk), 1) <= _iota((blk, blk), 0))).astype(F32)
    G = _mm_hi(tril, log_f)
    e_mat, et_mat = _head_sum_mats()
    G4, q4, k4, v4 = (z.reshape(nc, A_CHUNK, GW) for z in (G, qf, k, inp))
    shp = (nc, A_CHUNK, A_CHUNK, GW)
    one = (1, A_CHUNK, A_CHUNK, GW)
    mask = jnp.where(_iota(one, 2) <= _iota(one, 1), 0.0, -jnp.inf)
    decay = jnp.exp((G4[:, :, None, :] - G4[:, None, :, :]) + mask)
    prod = q4[:, :, None, :] * k4[:, None, :, :] * decay
    scores = _mm(prod.reshape(nc * A_CHUNK * A_CHUNK, GW), e_mat.astype(BF16))
    spread = _mm(scores, et_mat.astype(BF16)).reshape(shp)
    o_intra = jnp.sum(spread * v4[:, None, :, :], axis=2).reshape(blk, GW)
    head_diag = (_iota((GW, GW), 0) // HD == _iota((GW, GW), 1) // HD).astype(F32)
    g_last = [jnp.sum(log_f[c * A_CHUNK:(c + 1) * A_CHUNK], axis=0, keepdims=True) for c in range(nc)]
    g_last_b = jnp.concatenate([jnp.broadcast_to(g, (A_CHUNK, GW)) for g in g_last], axis=0)
    q_dec = qf * jnp.exp(G)
    k_end = k * jnp.exp(g_last_b - G)
    outs = []
    for c in range(nc):
        rows = slice(c * A_CHUNK, (c + 1) * A_CHUNK)
        outs.append(_mm_nt(q_dec[rows], st))
        st = st * jnp.exp(g_last[c]) + _mm_tn(inp[rows], k_end[rows]) * head_diag
    o = o_intra + jnp.concatenate(outs, axis=0)
    ms = _mm_hi(o * o, e_mat) * (1.0 / HD)
    o = o * _mm_hi(lax.rsqrt(ms + RMS_EPS), et_mat) * ng
    return o * jax.nn.silu(go), st


HGRN_BLK = 128


def hgrn_fwd(proj, lb, ng, S):
    T = proj.shape[0]
    B = T // S
    blk = min(HGRN_BLK, S)
    nb = S // blk

    def body(p_ref, lb_ref, ng_ref, o_ref, st_out_ref, st_ref):
        @pl.when(pl.program_id(1) == 0)
        def _():
            st_ref[...] = jnp.zeros_like(st_ref)
        st_out_ref[...] = st_ref[...]
        p = p_ref[...]
        o, st = _hgrn_block(p[:, 0:GW], p[:, GW:2 * GW], p[:, 2 * GW:3 * GW], p[:, 3 * GW:4 * GW],
                            st_ref[...], lb_ref[...], ng_ref[...], blk)
        o_ref[...] = o.astype(BF16)
        st_ref[...] = st

    vec = pl.BlockSpec((1, GW), lambda b, j: (0, 0))
    return pl.pallas_call(
        body, name="hgrn_fwd", grid=(B, nb),
        in_specs=[pl.BlockSpec((blk, 4 * GW), lambda b, j: (b * nb + j, C_A // (4 * GW))), vec, vec],
        out_specs=[pl.BlockSpec((blk, GW), lambda b, j: (b * nb + j, 0)),
                   pl.BlockSpec((None, GW, GW), lambda b, j: (b * nb + j, 0, 0))],
        out_shape=[_sds((T, GW), BF16), _sds((B * nb, GW, GW), F32)],
        scratch_shapes=[pltpu.VMEM((GW, GW), F32)],
        compiler_params=_cparams(VMEM_BIG))(proj, lb, ng)


def hgrn_bwd(proj, states, dcat, lb, ng, S):
    T = proj.shape[0]
    B = T // S
    blk = min(HGRN_BLK, S)
    nb = S // blk

    def body(p_ref, st_in_ref, do_ref, lb_ref, ng_ref, dp_ref, dlb_ref, dng_ref, dst_ref):
        b, j = pl.program_id(0), pl.program_id(1)

        @pl.when(j == 0)
        def _():
            dst_ref[...] = jnp.zeros_like(dst_ref)

        @pl.when((b == 0) & (j == 0))
        def _():
            dlb_ref[...] = jnp.zeros_like(dlb_ref)
            dng_ref[...] = jnp.zeros_like(dng_ref)

        p = p_ref[...]
        f = functools.partial(_hgrn_block, blk=blk)
        _, vjp = jax.vjp(f, p[:, 0:GW], p[:, GW:2 * GW], p[:, 2 * GW:3 * GW], p[:, 3 * GW:4 * GW],
                         st_in_ref[...], lb_ref[...], ng_ref[...])
        dq, df, di, dg, dst, dlb, dng = vjp((do_ref[...], dst_ref[...]))
        dp_ref[...] = jnp.concatenate([dq, df, di, dg], axis=1).astype(BF16)
        dst_ref[...] = dst
        dlb_ref[...] += dlb
        dng_ref[...] += dng

    def rev(b, j):
        return b * nb + (nb - 1 - j)

    vec = pl.BlockSpec((1, GW), lambda b, j: (0, 0))
    return pl.pallas_call(
        body, name="hgrn_bwd", grid=(B, nb),
        in_specs=[pl.BlockSpec((blk, 4 * GW), lambda b, j: (rev(b, j), C_A // (4 * GW))),
                  pl.BlockSpec((None, GW, GW), lambda b, j: (rev(b, j), 0, 0)),
                  pl.BlockSpec((blk, GW), lambda b, j: (rev(b, j), O_A // GW)), vec, vec],
        out_specs=[pl.BlockSpec((blk, 4 * GW), lambda b, j: (rev(b, j), 0)), vec, vec],
        out_shape=[_sds((T, 4 * GW), BF16), _sds((1, GW), F32), _sds((1, GW), F32)],
        scratch_shapes=[pltpu.VMEM((GW, GW), F32)],
        compiler_params=_cparams(VMEM_BIG))(proj, states, dcat, lb, ng)


ATT_TQ = 256


ATT_BANDS = 4


def _attn_block(q, k, v, cum, qpos0, scale, use_cum, n_free):
    s = _mm_nt(q, k) * scale
    if use_cum:
        s = s - cum
    band = s[:, n_free:]
    visible = _iota(band.shape, 1) <= (qpos0 - n_free) + _iota(band.shape, 0)
    band = jnp.where(visible, band, -jnp.inf)
    m = jnp.max(band, axis=-1, keepdims=True)
    if n_free:
        free = s[:, :n_free]
        m = jnp.maximum(m, jnp.max(free, axis=-1, keepdims=True))
    if not use_cum:
        m = lax.stop_gradient(m)
    e = jnp.exp(band - m)
    denom = jnp.sum(e, axis=-1, keepdims=True)
    o = _mm(e, v[n_free:])
    if n_free:
        e = jnp.exp(free - m)
        denom = denom + jnp.sum(e, axis=-1, keepdims=True)
        o = o + _mm(e, v[:n_free])
    return o * (1.0 / denom)


def _bands(S, tq):
    nq = S // tq
    nb = min(ATT_BANDS, nq)
    per = nq // nb
    return [(r * per, (r + 1) * per, (r + 1) * per * tq) for r in range(nb)]


def attn_fwd(qa, qo, ka, ko, va, vo, cum, scale, S):
    T = qa.shape[0]
    B = T // S
    tq = min(ATT_TQ, S)
    nq = S // tq
    use_cum = cum is not None

    def body(*refs):
        if use_cum:
            q_ref, k_ref, v_ref, c_ref, o_ref = refs
        else:
            (q_ref, k_ref, v_ref, o_ref), c_ref = refs, None
        h, i = pl.program_id(1), pl.program_id(2)
        for lo, hi, kw in _bands(S, tq):
            @pl.when((i >= lo) & (i < hi))
            def _():
                crow = c_ref[pl.ds(h, 1), 0:kw] if use_cum else None
                o = _attn_block(q_ref[...], k_ref[0:kw, :], v_ref[0:kw, :], crow, i * tq, scale, use_cum, lo * tq)
                o_ref[...] = o.astype(BF16)

    in_specs = [pl.BlockSpec((tq, HP), lambda b, h, i: (b * nq + i, qo + h)),
                pl.BlockSpec((S, HP), lambda b, h, i: (b, ko + h)),
                pl.BlockSpec((S, HP), lambda b, h, i: (b, vo + h))]
    args = [qa, ka, va]
    if use_cum:
        in_specs.append(pl.BlockSpec((None, 8, S), lambda b, h, i: (b, 0, 0)))
        args.append(cum)
    return pl.pallas_call(
        body, name="attn_fwd", grid=(B, HEADS, nq), in_specs=in_specs,
        out_specs=pl.BlockSpec((tq, HP), lambda b, h, i: (b * nq + i, h)),
        out_shape=_sds((T, HEADS * HP), BF16),
        compiler_params=_cparams(VMEM_BIG))(*args)


def attn_bwd(qa, qo, ka, ko, va, vo, cum, dcat, do_off, scale, S, out_dtype):
    T = qa.shape[0]
    B = T // S
    tq = min(ATT_TQ, S)
    nq = S // tq
    use_cum = cum is not None

    def body(*refs):
        if use_cum:
            q_ref, k_ref, v_ref, do_ref, c_ref, dq_ref, dk_ref, dv_ref, dc_ref, dk_acc, dv_acc = refs
        else:
            q_ref, k_ref, v_ref, do_ref, dq_ref, dk_ref, dv_ref, dk_acc, dv_acc = refs
        h, i = pl.program_id(1), pl.program_id(2)

        @pl.when(i == 0)
        def _():
            dk_acc[...] = jnp.zeros_like(dk_acc)
            dv_acc[...] = jnp.zeros_like(dv_acc)
            if use_cum:
                dc_ref[...] = jnp.zeros_like(dc_ref)

        for lo, hi, kw in _bands(S, tq):
            @pl.when((i >= lo) & (i < hi))
            def _():
                crow = c_ref[pl.ds(h, 1), 0:kw] if use_cum else jnp.zeros((1, kw), F32)
                f = functools.partial(_attn_block, qpos0=i * tq, scale=scale, use_cum=use_cum, n_free=lo * tq)
                _, vjp = jax.vjp(f, q_ref[...], k_ref[0:kw, :], v_ref[0:kw, :], crow)
                dq, dk, dv, dc = vjp(do_ref[...])
                dq_ref[...] = dq.astype(out_dtype)
                dk_acc[0:kw, :] += dk
                dv_acc[0:kw, :] += dv
                if use_cum:
                    dc_ref[:, 0:kw] += dc

        @pl.when(i == nq - 1)
        def _():
            dk_ref[...] = dk_acc[...].astype(out_dtype)
            dv_ref[...] = dv_acc[...].astype(out_dtype)

    qspec = pl.BlockSpec((tq, HP), lambda b, h, i: (b * nq + i, qo + h))
    in_specs = [qspec, pl.BlockSpec((S, HP), lambda b, h, i: (b, ko + h)),
                pl.BlockSpec((S, HP), lambda b, h, i: (b, vo + h)),
                pl.BlockSpec((tq, HP), lambda b, h, i: (b * nq + i, do_off + h))]
    args = [qa, ka, va, dcat]
    kv_out = pl.BlockSpec((S, HP), lambda b, h, i: (b, h))
    out_specs = [pl.BlockSpec((tq, HP), lambda b, h, i: (b * nq + i, h)), kv_out, kv_out]
    out_shape = [_sds((T, HEADS * HP), out_dtype)] * 3
    if use_cum:
        in_specs.append(pl.BlockSpec((None, 8, S), lambda b, h, i: (b, 0, 0)))
        args.append(cum)
        out_specs.append(pl.BlockSpec((None, 1, S), lambda b, h, i: (b * HEADS + h, 0, 0)))
        out_shape.append(_sds((B * HEADS, 1, S), F32))
    return pl.pallas_call(
        body, name="attn_bwd", grid=(B, HEADS, nq), in_specs=in_specs, out_specs=out_specs, out_shape=out_shape,
        scratch_shapes=[pltpu.VMEM((S, HP), F32), pltpu.VMEM((S, HP), F32)],
        compiler_params=_cparams(VMEM_BIG))(*args)


def _tri(n, upper):
    r, c = _iota((n, n), 0), _iota((n, n), 1)
    return ((r <= c) if upper else (r >= c)).astype(F32)


def fox_gate_fwd(proj, bcol, S):
    T = proj.shape[0]
    B = T // S
    ts = _tile(S, 512)
    nt = S // ts

    def body(p_ref, b_ref, o_ref, carry):
        @pl.when(pl.program_id(1) == 0)
        def _():
            carry[...] = jnp.zeros_like(carry)
        cf = jnp.transpose(p_ref[...])[0:8, :]
        lf = jax.nn.log_sigmoid(cf + b_ref[...])
        cum = _mm_hi(lf, _tri(ts, True)) + carry[...]
        o_ref[...] = cum
        carry[...] += jnp.sum(lf, axis=1, keepdims=True)

    return pl.pallas_call(
        body, name="fox_gate_fwd", grid=(B, nt),
        in_specs=[pl.BlockSpec((ts, HP), lambda b, j: (b * nt + j, C_CF // HP)), pl.BlockSpec((8, 1), lambda b, j: (0, 0))],
        out_specs=pl.BlockSpec((None, 8, ts), lambda b, j: (b, 0, j)), out_shape=_sds((B, 8, S), F32),
        scratch_shapes=[pltpu.VMEM((8, 1), F32)],
        compiler_params=_cparams(VMEM_BIG))(proj, bcol)


def fox_gate_bwd(proj, bcol, dcum, S):
    T = proj.shape[0]
    B = T // S
    ts = _tile(S, 512)
    nt = S // ts

    def body(p_ref, b_ref, dc_ref, dp_ref, db_ref, carry):
        b, j = pl.program_id(0), pl.program_id(1)

        @pl.when(j == 0)
        def _():
            carry[...] = jnp.zeros_like(carry)

        @pl.when((b == 0) & (j == 0))
        def _():
            db_ref[...] = jnp.zeros_like(db_ref)

        cf = jnp.transpose(p_ref[...])[0:8, :]
        dc = dc_ref[...]
        dlf = _mm_hi(dc, _tri(ts, False)) + carry[...]
        carry[...] += jnp.sum(dc, axis=1, keepdims=True)
        dcf = dlf * jax.nn.sigmoid(-(cf + b_ref[...]))
        db_ref[...] += jnp.sum(dcf, axis=1, keepdims=True)
        full = jnp.concatenate([dcf, jnp.zeros((HP - 8, ts), F32)], axis=0)
        dp_ref[...] = jnp.transpose(full).astype(BF16)

    def rev(b, j):
        return nt - 1 - j

    return pl.pallas_call(
        body, name="fox_gate_bwd", grid=(B, nt),
        in_specs=[pl.BlockSpec((ts, HP), lambda b, j: (b * nt + rev(b, j), C_CF // HP)),
                  pl.BlockSpec((8, 1), lambda b, j: (0, 0)),
                  pl.BlockSpec((None, 8, ts), lambda b, j: (b, 0, rev(b, j)))],
        out_specs=[pl.BlockSpec((ts, HP), lambda b, j: (b * nt + rev(b, j), 0)), pl.BlockSpec((8, 1), lambda b, j: (0, 0))],
        out_shape=[_sds((T, HP), BF16), _sds((8, 1), F32)],
        scratch_shapes=[pltpu.VMEM((8, 1), F32)],
        compiler_params=_cparams(VMEM_BIG))(proj, bcol, dcum)


def _mla_pre(blk, gq, gkv, wq, wkv, place, cos_q, sin_q, cs_k):
    nq = _rms(blk[:, 0:B_Q_LORA], gq)
    nkv = _rms(blk[:, B_Q_LORA:B_Q_LORA + B_KV_LORA], gkv)
    qq = _mm(nq, wq)
    q = qq[:, 0:HEADS * HP] * cos_q + qq[:, HEADS * HP:] * sin_q
    kv = _mm(nkv, wkv)
    k = kv[:, 0:HEADS * HP] + _mm(blk[:, B_Q_LORA + B_KV_LORA:] * cs_k, place)
    return q, k, kv[:, HEADS * HP:]


def mla_pre_fwd(proj, gq, gkv, wq, wkv, place, cos_q, sin_q, cs_k, S):
    T = proj.shape[0]
    tm = _tile(S, 512)
    tpb = S // tm
    W = HEADS * HP

    def body(p_ref, gq_ref, gkv_ref, wq_ref, wkv_ref, pl_ref, cq_ref, sq_ref, ck_ref, q_ref, k_ref, v_ref):
        q, k, v = _mla_pre(p_ref[...], gq_ref[...], gkv_ref[...], wq_ref[...], wkv_ref[...], pl_ref[...],
                           cq_ref[...], sq_ref[...], ck_ref[...])
        q_ref[...] = q
        k_ref[...] = k
        v_ref[...] = v

    def full(a):
        return pl.BlockSpec(a.shape, lambda i: (0,) * a.ndim)

    tok = pl.BlockSpec((tm, W), lambda i: (i, 0))
    return pl.pallas_call(
        body, name="mla_pre_fwd", grid=(T // tm,),
        in_specs=[pl.BlockSpec((tm, W), lambda i: (i, C_B // W)), full(gq), full(gkv), full(wq), full(wkv), full(place),
                  pl.BlockSpec((tm, W), lambda i: (i % tpb, 0)), pl.BlockSpec((tm, W), lambda i: (i % tpb, 0)),
                  pl.BlockSpec((tm, HP), lambda i: (i % tpb, 0))],
        out_specs=[tok] * 3, out_shape=[_sds((T, W), F32)] * 3,
        compiler_params=_cparams(VMEM_BIG))(proj, gq, gkv, wq, wkv, place, cos_q, sin_q, cs_k)


def mla_pre_bwd(proj, gq, gkv, wq, wkv, place, cos_q, sin_q, cs_k, dq, dk, dv, S):
    T = proj.shape[0]
    tm = _tile(S, 512)
    tpb = S // tm
    W = HEADS * HP

    def body(p_ref, gq_ref, gkv_ref, wq_ref, wkv_ref, pl_ref, cq_ref, sq_ref, ck_ref, dq_ref, dk_ref, dv_ref,
             dp_ref, dgq_ref, dgkv_ref, dwq_ref, dwkv_ref):
        @pl.when(pl.program_id(0) == 0)
        def _():
            for r in (dgq_ref, dgkv_ref, dwq_ref, dwkv_ref):
                r[...] = jnp.zeros_like(r)

        f = functools.partial(_mla_pre, place=pl_ref[...], cos_q=cq_ref[...], sin_q=sq_ref[...], cs_k=ck_ref[...])
        _, vjp = jax.vjp(f, p_ref[...], gq_ref[...], gkv_ref[...], wq_ref[...], wkv_ref[...])
        dp, dgq, dgkv, dwq, dwkv = vjp((dq_ref[...], dk_ref[...], dv_ref[...]))
        dp_ref[...] = dp.astype(BF16)
        dgq_ref[...] += dgq
        dgkv_ref[...] += dgkv
        dwq_ref[...] += dwq
        dwkv_ref[...] += dwkv

    def full(a):
        return pl.BlockSpec(a.shape, lambda i: (0,) * a.ndim)

    tok = pl.BlockSpec((tm, W), lambda i: (i, 0))
    return pl.pallas_call(
        body, name="mla_pre_bwd", grid=(T // tm,),
        in_specs=[pl.BlockSpec((tm, W), lambda i: (i, C_B // W)), full(gq), full(gkv), full(wq), full(wkv), full(place),
                  pl.BlockSpec((tm, W), lambda i: (i % tpb, 0)), pl.BlockSpec((tm, W), lambda i: (i % tpb, 0)),
                  pl.BlockSpec((tm, HP), lambda i: (i % tpb, 0)), tok, tok, tok],
        out_specs=[tok, full(gq), full(gkv), full(wq), full(wkv)],
        out_shape=[_sds((T, W), BF16), _sds(gq.shape, F32), _sds(gkv.shape, F32), _sds(wq.shape, F32), _sds(wkv.shape, F32)],
        compiler_params=_cparams(VMEM_BIG))(proj, gq, gkv, wq, wkv, place, cos_q, sin_q, cs_k, dq, dk, dv)


def _gmlp_block(blk, lg, lb, ws, bs):
    u = jax.nn.gelu(blk[:, 0:GW])
    v = jax.nn.gelu(blk[:, GW:2 * GW])
    mu = jnp.mean(v, axis=-1, keepdims=True)
    var = jnp.mean(jnp.square(v - mu), axis=-1, keepdims=True)
    vn = (v - mu) * lax.rsqrt(var + LN_EPS) * lg + lb
    causal = _iota((D_CHUNK, D_CHUNK), 1) <= _iota((D_CHUNK, D_CHUNK), 0)
    group = _iota((1, GW), 1) // HD
    mixed = jnp.zeros((D_CHUNK, GW), F32)
    for g in range(HEADS):
        part = _mm(jnp.where(causal, ws[g], 0.0), vn) + bs[g]
        mixed = mixed + jnp.where(group == g, part, 0.0)
    return u * mixed


def gmlp_fwd(proj, lg, lb, ws, bs):
    T = proj.shape[0]

    def body(p_ref, lg_ref, lb_ref, ws_ref, bs_ref, o_ref):
        o_ref[...] = _gmlp_block(p_ref[...], lg_ref[...], lb_ref[...], ws_ref[...], bs_ref[...]).astype(BF16)

    def full(a):
        return pl.BlockSpec(a.shape, lambda i: (0,) * a.ndim)

    return pl.pallas_call(
        body, name="gmlp_fwd", grid=(T // D_CHUNK,),
        in_specs=[pl.BlockSpec((D_CHUNK, 2 * GW), lambda i: (i, C_D // (2 * GW))), full(lg), full(lb), full(ws), full(bs)],
        out_specs=pl.BlockSpec((D_CHUNK, GW), lambda i: (i, 0)), out_shape=_sds((T, GW), BF16),
        compiler_params=_cparams(VMEM_BIG))(proj, lg, lb, ws, bs)


def gmlp_bwd(proj, lg, lb, ws, bs, dcat):
    T = proj.shape[0]

    def body(p_ref, lg_ref, lb_ref, ws_ref, bs_ref, do_ref, dp_ref, dlg_ref, dlb_ref, dws_ref, dbs_ref):
        @pl.when(pl.program_id(0) == 0)
        def _():
            for r in (dlg_ref, dlb_ref, dws_ref, dbs_ref):
                r[...] = jnp.zeros_like(r)

        _, vjp = jax.vjp(_gmlp_block, p_ref[...], lg_ref[...], lb_ref[...], ws_ref[...], bs_ref[...])
        dp, dlg, dlb, dws, dbs = vjp(do_ref[...])
        dp_ref[...] = dp.astype(BF16)
        dlg_ref[...] += dlg
        dlb_ref[...] += dlb
        dws_ref[...] += dws
        dbs_ref[...] += dbs

    def full(a):
        return pl.BlockSpec(a.shape, lambda i: (0,) * a.ndim)

    return pl.pallas_call(
        body, name="gmlp_bwd", grid=(T // D_CHUNK,),
        in_specs=[pl.BlockSpec((D_CHUNK, 2 * GW), lambda i: (i, C_D // (2 * GW))), full(lg), full(lb), full(ws), full(bs),
                  pl.BlockSpec((D_CHUNK, GW), lambda i: (i, O_D // GW))],
        out_specs=[pl.BlockSpec((D_CHUNK, 2 * GW), lambda i: (i, 0)), full(lg), full(lb), full(ws), full(bs)],
        out_shape=[_sds((T, 2 * GW), BF16), _sds(lg.shape, F32), _sds(lb.shape, F32), _sds(ws.shape, F32), _sds(bs.shape, F32)],
        compiler_params=_cparams(VMEM_BIG))(proj, lg, lb, ws, bs, dcat)


def _lb_all(logits):
    m = jnp.max(logits, axis=0, keepdims=True)
    e = jnp.exp(logits - m)
    sm = e / jnp.sum(e, axis=0, keepdims=True)
    return jnp.concatenate([sm[0:1] - sm[0:1], (sm[0:1] + sm[1:2]) - sm[0:1]], axis=0)


def lb_fwd(logits):
    def body(l_ref, o_ref):
        o_ref[...] = _lb_all(l_ref[...])

    return pl.pallas_call(body, name="lb_fwd", out_shape=_sds(logits.shape, F32))(logits)


def lb_bwd(logits, dlb):
    def body(l_ref, d_ref, o_ref):
        _, vjp = jax.vjp(_lb_all, l_ref[...])
        o_ref[...] = vjp(d_ref[...])[0]

    return pl.pallas_call(body, name="lb_bwd", out_shape=_sds(logits.shape, F32))(logits, dlb)


def ada_grad(c_all, dmod_cols):
    N = dmod_cols.shape[1]
    tn = _tile(N, 1152)

    def body(c_ref, d_ref, o_ref):
        h = jax.nn.silu(c_ref[...]).astype(BF16)
        o_ref[...] = lax.dot_general(h, d_ref[...].astype(BF16), (((0,), (0,)), ((), ())), preferred_element_type=F32)

    nb = c_all.shape[0]
    return pl.pallas_call(
        body, name="ada_grad", grid=(N // tn,),
        in_specs=[pl.BlockSpec((nb, D), lambda j: (0, 0)), pl.BlockSpec((nb, tn), lambda j: (0, j))],
        out_specs=pl.BlockSpec((D, tn), lambda j: (0, j)), out_shape=_sds((D, N), F32),
        compiler_params=_cparams(VMEM_BIG))(c_all, dmod_cols)


def sum_slots(a, n, name):
    _, R, C = a.shape
    tr = _row_tile(R, C, n)

    def body(a_ref, o_ref):
        acc = a_ref[0]
        for k in range(1, n):
            acc = acc + a_ref[k]
        o_ref[...] = acc

    return pl.pallas_call(
        body, name=name, grid=(R // tr,),
        in_specs=[pl.BlockSpec((n, tr, C), lambda i: (0, i, 0))],
        out_specs=pl.BlockSpec((tr, C), lambda i: (i, 0)), out_shape=_sds((R, C), F32),
        compiler_params=_cparams(VMEM_BIG))(a)


def add2(a, b, name):
    shp = a.shape
    C = shp[-1]
    a2, b2 = a.reshape(-1, C), b.reshape(-1, C)
    R = a2.shape[0]
    tr = _row_tile(R, C)

    def body(a_ref, b_ref, o_ref):
        o_ref[...] = a_ref[...] + b_ref[...]

    spec = pl.BlockSpec((tr, C), lambda i: (i, 0))
    return pl.pallas_call(body, name=name, grid=(R // tr,), in_specs=[spec, spec], out_specs=spec,
                          out_shape=_sds((R, C), F32), compiler_params=_cparams(VMEM_BIG))(a2, b2).reshape(shp)


def _row_tile(R, C=D, n=1, mult=8):
    limit = max(mult, (1 << 18) // (C * n))
    for t in range(limit - limit % mult, mult - 1, -mult):
        if R % t == 0:
            return t
    return R


def adamw(w, g, m, v, name):
    R, C = w.shape
    tr = _row_tile(R, C)
    c1 = 1.0 - ADAM_B1 ** ADAM_STEP
    c2 = 1.0 - ADAM_B2 ** ADAM_STEP

    def body(w_ref, g_ref, m_ref, v_ref, d_ref, nm_ref, nv_ref):
        g_ = g_ref[...]
        nm = ADAM_B1 * m_ref[...] + (1.0 - ADAM_B1) * g_
        nv = ADAM_B2 * v_ref[...] + (1.0 - ADAM_B2) * jnp.square(g_)
        d_ref[...] = -ADAM_LR * ((nm / c1) / (jnp.sqrt(nv / c2) + ADAM_EPS) + ADAM_WD * w_ref[...])
        nm_ref[...] = nm
        nv_ref[...] = nv

    spec = pl.BlockSpec((tr, C), lambda i: (i, 0))
    return pl.pallas_call(body, name=name, grid=(R // tr,), in_specs=[spec] * 4, out_specs=[spec] * 3,
                          out_shape=[_sds((R, C), F32)] * 3, compiler_params=_cparams(VMEM_BIG))(w, g, m, v)


def _rot_cols(w):
    return jnp.concatenate([-w[:, 16:32], w[:, 0:16]], axis=1)


def _fold_rot(d):
    return jnp.concatenate([d[:, 16:32], -d[:, 0:16]], axis=1)


def _pad_heads(w, off, axis):
    parts = []
    for h in range(HEADS):
        piece = lax.slice_in_dim(w, off + HD * h, off + HD * (h + 1), axis=axis)
        parts += [piece, jnp.zeros_like(piece)]
    return parts


def _unpad_heads(d, off, axis):
    return [lax.slice_in_dim(d, off + HP * h, off + HP * h + HD, axis=axis) for h in range(HEADS)]


def mix_in_ext(w):
    z = lambda n: jnp.zeros((w.shape[0], n), w.dtype)
    kr = w[:, 1408:1440]
    cols = [w[:, 0:1408], kr, _rot_cols(kr), z(64)]
    cols += _pad_heads(w, 1440, 1) + _pad_heads(w, 1696, 1) + _pad_heads(w, 1952, 1)
    cols += [w[:, 2212:2724], w[:, 2208:2212], z(NP - C_CF - HEADS)]
    return jnp.concatenate(cols, axis=1)


def mix_in_unext(d):
    kr = d[:, 1408:1440] + _fold_rot(d[:, 1440:1472])
    cols = [d[:, 0:1408], kr] + _unpad_heads(d, C_CQ, 1) + _unpad_heads(d, C_CK, 1) + _unpad_heads(d, C_CV, 1)
    cols += [d[:, C_CF:C_CF + HEADS], d[:, C_D:C_D + 2 * GW]]
    return jnp.concatenate(cols, axis=1)


def mix_out_ext(w):
    return jnp.concatenate([w[0:GW]] + _pad_heads(w, GW, 0) + _pad_heads(w, 2 * GW, 0) + [w[3 * GW:4 * GW]], axis=0)


def mix_out_unext(d):
    return jnp.concatenate([d[0:GW]] + _unpad_heads(d, O_B, 0) + _unpad_heads(d, O_C, 0) + [d[O_D:O_D + GW]], axis=0)


def uq_ext(w):
    z = lambda n: jnp.zeros((w.shape[0], n), w.dtype)
    a, b = [], []
    for h in range(HEADS):
        o = (B_NOPE + B_ROPE) * h
        a += [w[:, o:o + B_NOPE + B_ROPE], z(32)]
        b += [z(B_NOPE), _rot_cols(w[:, o + B_NOPE:o + B_NOPE + B_ROPE]), z(32)]
    return jnp.concatenate(a + b, axis=1)


def uq_unext(d):
    cols = []
    for h in range(HEADS):
        o = HP * h
        cols += [d[:, o:o + B_NOPE], d[:, o + B_NOPE:o + B_NOPE + B_ROPE]
                 + _fold_rot(d[:, HEADS * HP + o + B_NOPE:HEADS * HP + o + B_NOPE + B_ROPE])]
    return jnp.concatenate(cols, axis=1)


def ukv_ext(w):
    z = jnp.zeros((w.shape[0], HD), w.dtype)
    k, v = [], []
    for h in range(HEADS):
        k += [w[:, 2 * HD * h:2 * HD * h + HD], z]
        v += [w[:, 2 * HD * h + HD:2 * HD * (h + 1)], z]
    return jnp.concatenate(k + v, axis=1)


def ukv_unext(d):
    cols = []
    for h in range(HEADS):
        cols += [d[:, HP * h:HP * h + HD], d[:, HEADS * HP + HP * h:HEADS * HP + HP * h + HD]]
    return jnp.concatenate(cols, axis=1)


def rope_tables(S):
    half = B_ROPE // 2
    inv_freq = ROPE_THETA ** (-jnp.arange(half, dtype=F32) / half)
    ang = jnp.arange(S).astype(F32)[:, None] * inv_freq[None, :]
    cos = jnp.tile(jnp.cos(ang), (1, 2))
    sin = jnp.tile(jnp.sin(ang), (1, 2))
    one, zero = jnp.ones((S, B_NOPE), F32), jnp.zeros((S, B_NOPE), F32)
    z32 = jnp.zeros((S, 32), F32)
    cos_q = jnp.tile(jnp.concatenate([one, cos, z32], axis=1), (1, HEADS))
    sin_q = jnp.tile(jnp.concatenate([zero, sin, z32], axis=1), (1, HEADS))
    cs_k = jnp.concatenate([cos, sin, zero], axis=1)
    place = np.zeros((HP, HEADS * HP), np.float32)
    for h in range(HEADS):
        for j in range(B_ROPE):
            place[j, h * HP + B_NOPE + j] = 1.0
            place[B_ROPE + j, h * HP + B_NOPE + j] = 1.0
    return cos_q, sin_q, cs_k, jnp.asarray(place, BF16)


def layer_fwd(x, mod, get, tabs, S):
    cos_q, sin_q, cs_k, place = tabs
    p = dict(get("ffn1", x))
    l = p["wl"]
    zg1, zu1, act1 = ffn_in_fwd(x, mod, p["ffn1_in"], l, 0, 1, S)
    y1, x1 = out_ln_fwd(act1, p["ffn1_out"], x, mod, p["ln_g"][0:1], p["ln_b"][0:1], 2, 0.5, S, l)
    p.update(get("mix", x1))
    proj = mix_in_fwd(x1, mod, p["mix_in"], 3, 4, S)
    o_a, states = hgrn_fwd(proj, p["lb"], p["ng"], S)
    q_b, k_b, v_b = mla_pre_fwd(proj, p["gq"], p["gkv"], p["wq"], p["wkv"], place, cos_q, sin_q, cs_k, S)
    o_b = attn_fwd(q_b, 0, k_b, 0, v_b, 0, None, (B_NOPE + B_ROPE) ** -0.5, S)
    cum = fox_gate_fwd(proj, p["bcol"], S)
    o_c = attn_fwd(proj, C_CQ // HP, proj, C_CK // HP, proj, C_CV // HP, cum, HD ** -0.5, S)
    o_d = gmlp_fwd(proj, p["g_lg"], p["g_lb"], p["ws"], p["bs"])
    cat = jnp.concatenate([o_a, o_b, o_c, o_d], axis=1)
    y2, x2 = out_ln_fwd(cat, p["mix_out"], x1, mod, p["ln_g"][1:2], p["ln_b"][1:2], 5, 1.0, S)
    p.update(get("ffn2", x2))
    zg3, zu3, act3 = ffn_in_fwd(x2, mod, p["ffn2_in"], l, 6, 7, S)
    y3, x3 = out_ln_fwd(act3, p["ffn2_out"], x2, mod, p["ln_g"][2:3], p["ln_b"][2:3], 8, 0.5, S, l)
    saved = dict(x=x, zg1=zg1, zu1=zu1, act1=act1, y1=y1, x1=x1, proj=proj, states=states, q_b=q_b, k_b=k_b, v_b=v_b,
                 cum=cum, cat=cat, y2=y2, x2=x2, zg3=zg3, zu3=zu3, act3=act3, y3=y3, p=p)
    return x3, saved


def _ffn_bwd(dxn, x_in, y, zg, zu, act, mod, w_in, w_out, l, lg, lb, idx, S, emit):
    sh, sc, gate = idx
    dres, dy, dgate, dlg, dlb = ln_res_bwd(dxn, x_in, y, mod, lg, lb, gate, 0.5, S)
    dzg, dzu = swiglu_bwd(dy, w_out, l, zg, zu, S)
    dw_out = tn_mm(act, dy, FF // 2).reshape(N_CHIPS, FF // N_CHIPS, D)
    dw_in = tn_mm_mod_shards(x_in, mod, dzg, dzu, sh, sc, S)
    mod = mod + emit(dw_in, dw_out)
    dx, dsh, dsc = nt_mod_bwd([dzg, dzu], w_in, [0, 2], x_in, mod, dres, sc, S, FF // 2, l)
    return dx, dw_in, dw_out, dlg, dlb, {sh: dsh, sc: dsc, gate: dgate}, mod


def layer_bwd(dx3, mod, sv, tabs, S, emit):
    cos_q, sin_q, cs_k, place = tabs
    p = sv["p"]
    l = p["wl"]
    g = {}
    dm = {}

    def emit_ffn(part):
        def f(dw_in, dw_out):
            g[part + "_in"], g[part + "_out"] = dw_in, dw_out
            return emit(part, g)
        return f

    dx2, _, _, dlg2, dlb2, d, mod = _ffn_bwd(
        dx3, sv["x2"], sv["y3"], sv["zg3"], sv["zu3"], sv["act3"], mod, p["ffn2_in"], p["ffn2_out"], l,
        p["ln_g"][2:3], p["ln_b"][2:3], (6, 7, 8), S, emit_ffn("ffn2"))
    dm.update(d)
    dres, dy2, dm[5], dlg1, dlb1 = ln_res_bwd(dx2, sv["x1"], sv["y2"], mod, p["ln_g"][1:2], p["ln_b"][1:2], 5, 1.0, S)
    dcat = nt_plain(dy2, p["mix_out"])
    g["mix_out"] = tn_mm(sv["cat"], dy2, 768)
    proj = sv["proj"]
    d_a, g["lb"], g["ng"] = hgrn_bwd(proj, sv["states"], dcat, p["lb"], p["ng"], S)
    dq_c, dk_c, dv_c, dcum = attn_bwd(proj, C_CQ // HP, proj, C_CK // HP, proj, C_CV // HP, sv["cum"], dcat,
                                      O_C // HP, HD ** -0.5, S, BF16)
    B = proj.shape[0] // S
    dcum = jnp.concatenate([dcum.reshape(B, HEADS, S), jnp.zeros((B, 8 - HEADS, S), F32)], axis=1)
    d_cf, g["bcol"] = fox_gate_bwd(proj, p["bcol"], dcum, S)
    dq_b, dk_b, dv_b = attn_bwd(sv["q_b"], 0, sv["k_b"], 0, sv["v_b"], 0, None, dcat, O_B // HP,
                                (B_NOPE + B_ROPE) ** -0.5, S, F32)
    d_b, g["gq"], g["gkv"], g["wq"], g["wkv"] = mla_pre_bwd(
        proj, p["gq"], p["gkv"], p["wq"], p["wkv"], place, cos_q, sin_q, cs_k, dq_b, dk_b, dv_b, S)
    d_d, g["g_lg"], g["g_lb"], g["ws"], g["bs"] = gmlp_bwd(proj, p["g_lg"], p["g_lb"], p["ws"], p["bs"], dcat)
    dproj = jnp.concatenate([d_a, d_b, dq_c, dk_c, dv_c, d_d, d_cf, jnp.zeros_like(d_cf)], axis=1)
    g["mix_in"] = tn_mm_mod(sv["x1"], mod, dproj, 3, 4, S, NP_TILE)
    mod = mod + emit("mix", g)
    dx1, dm[3], dm[4] = nt_mod_bwd([dproj], p["mix_in"], [0], sv["x1"], mod, dres, 4, S, NP_TILE)
    last = []

    def emit_last(dw_in, dw_out):
        last.append(emit_ffn("ffn1")(dw_in, dw_out))
        return last[0]

    dx0, _, _, dlg0, dlb0, d, mod = _ffn_bwd(
        dx1, sv["x"], sv["y1"], sv["zg1"], sv["zu1"], sv["act1"], mod, p["ffn1_in"], p["ffn1_out"], l,
        p["ln_g"][0:1], p["ln_b"][0:1], (0, 1, 2), S, emit_last)
    dm.update(d)
    g["ln_g"] = jnp.concatenate([dlg0, dlg1, dlg2], axis=0)
    g["ln_b"] = jnp.concatenate([dlb0, dlb1, dlb2], axis=0)
    dmod = jnp.concatenate([dm[i] for i in range(N_MOD)], axis=1)
    return dx0, dmod, g, last[0]


def local_step(x, c8, tgt, get, lb_logits, S, emit=None):
    B = x.shape[0] // S
    tabs = rope_tables(S)
    lb_all = lb_fwd(lb_logits)
    mods, saved = [], []
    h = x
    for l in range(DEPTH):
        pa = get(l, "ada", h)
        mod = mod_fwd(c8, pa["ada_w"], pa["wl"], pa["ada_b"])[0:B].reshape(B, N_MOD, D)

        def get_l(part, after, l=l):
            p = dict(get(l, part, after))
            if part == "mix":
                p["lb"] = lb_all[l:l + 1]
            return p

        h, sv = layer_fwd(h, mod, get_l, tabs, S)
        mods.append(mod)
        saved.append(sv)
    loss_tile, dh = loss_head(h, tgt)
    grads, dmods, dlb = [None] * DEPTH, [None] * DEPTH, [None] * DEPTH
    tie = jnp.zeros((), F32)
    for l in reversed(range(DEPTH)):
        emit_l = (lambda part, g: jnp.zeros((), F32)) if emit is None else functools.partial(emit, l)
        dh, dmods[l], grads[l], tie = layer_bwd(dh, mods[l] + tie, saved[l], tabs, S, emit_l)
        dlb[l] = grads[l].pop("lb")
    d_logits = lb_bwd(lb_logits, jnp.concatenate(dlb, axis=0))
    return loss_tile, dh, dmods, grads, d_logits


ANY = pl.BlockSpec(memory_space=pl.ANY)


def _place():
    x, y, c = lax.axis_index("x"), lax.axis_index("y"), lax.axis_index("c")
    chips = [(1 - x, y), (x, 1 - y), (1 - x, 1 - y)]
    return x, y, c, chips


def _rcopy(src, dst, sems, k, to):
    send_sems, recv_sems = sems
    return pltpu.make_async_remote_copy(src_ref=src, dst_ref=dst, send_sem=send_sems.at[k], recv_sem=recv_sems.at[k],
                                        device_id=to, device_id_type=MESH)


def _dma_sems(n_remote, n_local):
    return [pltpu.SemaphoreType.DMA((n_remote,)), pltpu.SemaphoreType.DMA((n_remote,)), pltpu.SemaphoreType.DMA((n_local,))]


def own_slot(src, chip):
    L = src.shape[0]
    return lax.dynamic_update_slice(jnp.zeros((L, N_CHIPS) + src.shape[1:], src.dtype), src[:, None], (0, chip, 0, 0))


def ag_shards(arrs, lands):
    n = len(arrs)
    rh = [a.shape[1] // 2 for a in arrs]

    def body(*refs):
        srcs, outs, token = refs[:n], refs[2 * n:3 * n], refs[3 * n]
        send_sems, recv_sems = refs[3 * n + 1:]
        x, y, c, chips = _place()
        sems = (send_sems, recv_sems)
        me = 2 * x + y
        sibling = (x, y, 1 - c)
        token[...] = jnp.zeros_like(token)

        def part(i, k, hc):
            return outs[i].at[:, k, pl.ds(hc * rh[i], rh[i]), :]

        started = []
        for j, (px, py) in enumerate(chips):
            for i in range(n):
                cp = _rcopy(srcs[i].at[:, pl.ds(c * rh[i], rh[i]), :], part(i, me, c), sems, 6 * i + j, (px, py, c))
                cp.start()
                started.append(cp)
        for j, (px, py) in enumerate(chips):
            k = 2 * px + py
            for i in range(n):
                _rcopy(part(i, k, c), part(i, k, c), sems, 6 * i + j, (px, py, c)).wait_recv()
                cp = _rcopy(part(i, k, c), part(i, k, c), sems, 6 * i + 3 + j, sibling)
                cp.start()
                started.append(cp)
        for j, (px, py) in enumerate(chips):
            k = 2 * px + py
            for i in range(n):
                _rcopy(part(i, k, 1 - c), part(i, k, 1 - c), sems, 6 * i + 3 + j, sibling).wait_recv()
        for cp in started:
            cp.wait_send()

    outs = pl.pallas_call(
        body, name="ag_shards", out_shape=[_sds(a.shape, a.dtype) for a in lands] + [_sds((8, 128), F32)],
        in_specs=[ANY] * (2 * n), out_specs=[ANY] * n + [pl.BlockSpec(memory_space=pltpu.VMEM)],
        input_output_aliases={n + i: i for i in range(n)}, scratch_shapes=_dma_sems(6 * n, 1)[:2])(*arrs, *lands)
    return list(outs[:n]), outs[n]


HBM_SPEC = pl.BlockSpec(memory_space=pltpu.HBM)
SEM_SPEC = pl.BlockSpec(memory_space=pltpu.SEMAPHORE)
DATAFLOW = pltpu.SideEffectType.DATAFLOW_SIDE_EFFECTING


def _after(x, dep):
    return lax.optimization_barrier((x, dep))[0]


def _split_start(srcs, lands, copies, n_copies, dep, name):
    n, m = len(srcs), len(lands)

    def body(*refs):
        ins = refs[:n + m]
        send_sems, recv_sems = refs[n + m + 1], refs[n + m + 2]
        token = refs[-1]
        for k, (src, dst, to) in enumerate(copies(ins[:n], ins[n:], _place())):
            pltpu.make_async_remote_copy(src_ref=src, dst_ref=dst, send_sem=send_sems.at[k], recv_sem=recv_sems.at[k],
                                         device_id=to, device_id_type=MESH).start()
        token[...] = jnp.zeros_like(token)

    arrs = list(srcs) + list(lands)
    outs = pl.pallas_call(
        body, name=name,
        out_shape=(pltpu.SemaphoreType.DMA((n_copies,)), pltpu.SemaphoreType.DMA((n_copies,)),
                   *[pltpu.HBM(a.shape, a.dtype) for a in arrs], _sds((8, 128), F32)),
        in_specs=[HBM_SPEC] * (n + m) + [ANY],
        out_specs=(SEM_SPEC, SEM_SPEC, *[HBM_SPEC] * (n + m), pl.BlockSpec(memory_space=pltpu.VMEM)),
        input_output_aliases={i: 2 + i for i in range(n + m)},
        compiler_params=pltpu.CompilerParams(has_side_effects=DATAFLOW),
    )(*[pltpu.with_memory_space_constraint(a, pltpu.HBM) for a in arrs], dep)
    return outs[0], outs[1], list(outs[2:2 + n]), list(outs[2 + n:2 + n + m]), outs[-1]


def _split_wait(handle, arrivals, after, name):
    send_sems, recv_sems, srcs, lands, _ = handle
    n, m = len(srcs), len(lands)

    def body(*refs):
        ins = refs[:n + m]
        send_sems, recv_sems = refs[n + m], refs[n + m + 1]
        x, y, c, chips = place = _place()
        for k, (src, dst) in enumerate(arrivals(ins[:n], ins[n:], place)):
            cp = pltpu.make_async_remote_copy(src_ref=src, dst_ref=dst, send_sem=send_sems.at[k], recv_sem=recv_sems.at[k],
                                              device_id=(x, y, 1 - c), device_id_type=MESH)
            cp.wait_send()
            cp.wait_recv()

    arrs = list(srcs) + list(lands)
    outs = pl.pallas_call(
        body, name=name, out_shape=[pltpu.HBM(a.shape, a.dtype) for a in arrs],
        in_specs=[HBM_SPEC] * (n + m) + [SEM_SPEC, SEM_SPEC, ANY], out_specs=[HBM_SPEC] * (n + m),
        input_output_aliases={i: i for i in range(n + m)},
        compiler_params=pltpu.CompilerParams(has_side_effects=DATAFLOW),
    )(*arrs, send_sems, recv_sems, after)
    return list(outs[:n]), list(outs[n:])


def _ag_part(ref, k, hc):
    rh = ref.shape[2] // 2
    return ref.at[:, k, pl.ds(hc * rh, rh), :]


def ag_start(srcs, lands, dep, name):
    def copies(s, d, place):
        x, y, c, chips = place
        out = []
        for j, (px, py) in enumerate(chips):
            for i in range(len(s)):
                rh = s[i].shape[1] // 2
                out.append((s[i].at[:, pl.ds(c * rh, rh), :], _ag_part(d[i], 2 * x + y, c), (px, py, c)))
        return out

    return _split_start(srcs, lands, copies, 3 * len(srcs), dep, name)


def ag_wait(handle, after, name):
    def arrivals(s, d, place):
        x, y, c, chips = place
        out = []
        for j, (px, py) in enumerate(chips):
            for i in range(len(s)):
                rh = s[i].shape[1] // 2
                out.append((s[i].at[:, pl.ds(c * rh, rh), :], _ag_part(d[i], 2 * px + py, c)))
        return out

    return _split_wait(handle, arrivals, after, name)


def ag_forward(lands, name):
    n = len(lands)

    def body(*refs):
        bufs, token = refs[n:2 * n], refs[2 * n]
        send_sems, recv_sems = refs[2 * n + 1:]
        x, y, c, chips = _place()
        sems = (send_sems, recv_sems)
        token[...] = jnp.zeros_like(token)
        cps = []
        for j, (px, py) in enumerate(chips):
            for i in range(n):
                part = _ag_part(bufs[i], 2 * px + py, c)
                cps.append(_rcopy(part, part, sems, 3 * i + j, (x, y, 1 - c)))
        for cp in cps:
            cp.start()
        for j, (px, py) in enumerate(chips):
            for i in range(n):
                part = _ag_part(bufs[i], 2 * px + py, 1 - c)
                _rcopy(part, part, sems, 3 * i + j, (x, y, 1 - c)).wait_recv()
        for cp in cps:
            cp.wait_send()

    outs = pl.pallas_call(
        body, name=name, out_shape=[_sds(a.shape, a.dtype) for a in lands] + [_sds((8, 128), F32)],
        in_specs=[ANY] * n, out_specs=[ANY] * n + [pl.BlockSpec(memory_space=pltpu.VMEM)],
        input_output_aliases={i: i for i in range(n)}, scratch_shapes=_dma_sems(3 * n, 1)[:2])(*lands)
    return list(outs[:n]), outs[n]


def rs_start(hs, lands, dep, name):
    def copies(s, d, place):
        x, y, c, chips = place
        return [(s[i].at[2 * px + py], d[i].at[2 * x + y], (px, py, c)) for j, (px, py) in enumerate(chips) for i in range(len(s))]

    return _split_start(hs, lands, copies, 3 * len(hs), dep, name)


def _kept_out(ref, c):
    rh = ref.shape[1] // 2
    return ref.at[:, pl.ds((1 - c) * rh, rh), :]


def swap_start(arrs, lands, dep, name):
    def copies(s, d, place):
        x, y, c, _ = place
        return [(_kept_out(s[i], c), d[i], (x, y, 1 - c)) for i in range(len(s))]

    return _split_start(arrs, lands, copies, len(arrs), dep, name)


def swap_wait(handle, after, name):
    def arrivals(s, d, place):
        x, y, c, _ = place
        return [(_kept_out(s[i], c), d[i]) for i in range(len(s))]

    return _split_wait(handle, arrivals, after, name)


def rs_wait(handle, after, name):
    def arrivals(s, d, place):
        x, y, c, chips = place
        return [(s[i].at[2 * px + py], d[i].at[2 * px + py]) for j, (px, py) in enumerate(chips) for i in range(len(s))]

    return _split_wait(handle, arrivals, after, name)


def sibling_swap(arrs, name):
    n = len(arrs)
    rh = [a.shape[1] // 2 for a in arrs]

    def body(*refs):
        srcs, outs = refs[:n], refs[n:2 * n]
        send_sems, recv_sems = refs[2 * n:]
        x, y, c, _ = _place()
        cps = [_rcopy(srcs[i].at[:, pl.ds((1 - c) * rh[i], rh[i]), :], outs[i], (send_sems, recv_sems), i, (x, y, 1 - c))
               for i in range(n)]
        for cp in cps:
            cp.start()
        for cp in cps:
            cp.wait()

    return pl.pallas_call(
        body, name=name, out_shape=[_sds((N_CHIPS, r, a.shape[2]), a.dtype) for a, r in zip(arrs, rh)],
        in_specs=[ANY] * n, out_specs=[ANY] * n, scratch_shapes=_dma_sems(n, 1)[:2])(*arrs)


def chip_exchange(hs):
    n = len(hs)

    def body(*refs):
        srcs, outs = refs[:n], refs[n:2 * n]
        send_sems, recv_sems, loc_sems = refs[2 * n:]
        x, y, c, chips = _place()
        sems = (send_sems, recv_sems)
        me = 2 * x + y
        mine = [pltpu.make_async_copy(srcs[i].at[me], outs[i].at[me], loc_sems.at[i]) for i in range(n)]
        for cp in mine:
            cp.start()
        sends = []
        for j, (px, py) in enumerate(chips):
            for i in range(n):
                cp = _rcopy(srcs[i].at[2 * px + py], outs[i].at[me], sems, 3 * i + j, (px, py, c))
                cp.start()
                sends.append(cp)
        for j, (px, py) in enumerate(chips):
            for i in range(n):
                _rcopy(srcs[i].at[2 * px + py], outs[i].at[2 * px + py], sems, 3 * i + j, (px, py, c)).wait_recv()
        for cp in sends:
            cp.wait_send()
        for cp in mine:
            cp.wait()

    return pl.pallas_call(
        body, name="chip_exchange", out_shape=[_sds(h.shape, h.dtype) for h in hs],
        in_specs=[ANY] * n, out_specs=[ANY] * n, scratch_shapes=_dma_sems(3 * n, n))(*hs)


def sum_into(land, base, l, core, name):
    _, rh, C = land.shape
    tr = _row_tile(rh, C, N_CHIPS, mult=16)
    nr = rh // tr

    def body(core_ref, land_ref, base_ref, o_ref):
        acc = land_ref[0].astype(F32)
        for k in range(1, N_CHIPS):
            acc = acc + land_ref[k].astype(F32)
        o_ref[...] = acc

    grid_spec = pltpu.PrefetchScalarGridSpec(
        num_scalar_prefetch=1, grid=(nr,),
        in_specs=[pl.BlockSpec((N_CHIPS, tr, C), lambda r, core_ref: (0, r, 0)), ANY],
        out_specs=pl.BlockSpec((None, tr, C), lambda r, core_ref: (l, core_ref[0] * nr + r, 0)))
    return pl.pallas_call(body, name=name, grid_spec=grid_spec, out_shape=_sds(base.shape, base.dtype),
                          input_output_aliases={2: 0}, compiler_params=_cparams(VMEM_BIG))(
        core.reshape(1).astype(jnp.int32), land, base)


def sibling_join(bases, name):
    n = len(bases)

    def body(*refs):
        bufs = refs[n:2 * n]
        send_sems, recv_sems = refs[2 * n:]
        x, y, c, _ = _place()
        sems = (send_sems, recv_sems)

        def half(i, hc):
            rh = bufs[i].shape[1] // 2
            return bufs[i].at[:, pl.ds(hc * rh, rh), :]

        sends = [_rcopy(half(i, c), half(i, c), sems, i, (x, y, 1 - c)) for i in range(n)]
        for cp in sends:
            cp.start()
        for i in range(n):
            _rcopy(half(i, 1 - c), half(i, 1 - c), sems, i, (x, y, 1 - c)).wait_recv()
        for cp in sends:
            cp.wait_send()

    return pl.pallas_call(
        body, name=name, out_shape=[_sds(b.shape, b.dtype) for b in bases], in_specs=[ANY] * n, out_specs=[ANY] * n,
        input_output_aliases={i: i for i in range(n)}, scratch_shapes=_dma_sems(n, 1)[:2])(*bases)


def ag_all(blk):
    M, C = blk.shape

    def body(x_ref, out_ref, send_sems, recv_sems, loc_sem):
        x, y, c, chips = _place()
        sems = (send_sems, recv_sems)
        me, sibling = (x, y, c), (x, y, 1 - c)

        def slot(px, py, pc):
            return out_ref.at[4 * px + 2 * py + pc]

        mine = pltpu.make_async_copy(x_ref, slot(*me), loc_sem)
        mine.start()
        first = [_rcopy(x_ref, slot(*me), sems, 0, sibling)]
        first += [_rcopy(x_ref, slot(*me), sems, 1 + j, (*chip, c)) for j, chip in enumerate(chips)]
        for cp in first:
            cp.start()
        passed = [_rcopy(slot(*chip, c), slot(*chip, c), sems, 4 + j, sibling) for j, chip in enumerate(chips)]
        for j, chip in enumerate(chips):
            _rcopy(slot(*chip, c), slot(*chip, c), sems, 1 + j, me).wait_recv()
            passed[j].start()
        _rcopy(slot(*sibling), slot(*sibling), sems, 0, me).wait_recv()
        for j, chip in enumerate(chips):
            _rcopy(slot(*chip, 1 - c), slot(*chip, 1 - c), sems, 4 + j, me).wait_recv()
        for cp in first + passed:
            cp.wait_send()
        mine.wait()

    return pl.pallas_call(
        body, name="ag_all", out_shape=_sds((8, M, C), blk.dtype),
        in_specs=[pl.BlockSpec(memory_space=pltpu.VMEM)], out_specs=pl.BlockSpec(memory_space=pltpu.VMEM),
        scratch_shapes=[pltpu.SemaphoreType.DMA((7,)), pltpu.SemaphoreType.DMA((7,)), pltpu.SemaphoreType.DMA(())],
        compiler_params=_cparams(VMEM_BIG))(blk)


WEIGHTS = ["ada_w", "ada_b", "ln_g", "ln_b", "ffn1_w_in", "ffn1_w_out", "ffn2_w_in", "ffn2_w_out", "mix_w_in", "mix_w_out",
           "hgrn_lb_logits", "hgrn_norm_g", "mla_q_norm_g", "mla_kv_norm_g", "mla_w_uq", "mla_w_ukv", "fox_b_f",
           "gmlp_ln_g", "gmlp_ln_b", "gmlp_w_s", "gmlp_b_s"]
SHARDED = {"ffn1_w_in": 1, "ffn1_w_out": 0, "ffn2_w_in": 1, "ffn2_w_out": 0, "mix_w_in": 1, "mix_w_out": 0,
           "mla_w_uq": 1, "mla_w_ukv": 1}
SMALL = ["hgrn_lb_logits", "hgrn_norm_g", "mla_q_norm_g", "mla_kv_norm_g", "fox_b_f", "gmlp_ln_g", "gmlp_ln_b",
         "gmlp_w_s", "gmlp_b_s", "ln_g", "ln_b"]
GATHERED = ["ada_w", "ffn1_w_in", "ffn1_w_out", "ffn2_w_in", "ffn2_w_out", "mix_w_in", "mix_w_out", "mla_w_uq", "mla_w_ukv"]
REDUCED = GATHERED[1:]


def _col_shards(a):
    cols = a.shape[1] // N_CHIPS
    return jnp.stack([a[:, k * cols:(k + 1) * cols] for k in range(N_CHIPS)])


def add_kept_half(a, got, core, name):
    _, R, C = a.shape
    rh = R // 2
    tr = _row_tile(rh, C, mult=16)
    nr = rh // tr

    def body(core_ref, a_ref, b_ref, o_ref):
        o_ref[...] = (a_ref[...] + b_ref[...]).astype(o_ref.dtype)

    half = pl.BlockSpec((None, tr, C), lambda k, r, core_ref: (k, r, 0))
    grid_spec = pltpu.PrefetchScalarGridSpec(
        num_scalar_prefetch=1, grid=(N_CHIPS, nr),
        in_specs=[pl.BlockSpec((None, tr, C), lambda k, r, core_ref: (k, core_ref[0] * nr + r, 0)), half],
        out_specs=half)
    return pl.pallas_call(body, name=name, grid_spec=grid_spec, out_shape=_sds((N_CHIPS, rh, C), BF16),
                          compiler_params=_cparams(VMEM_BIG))(core.reshape(1).astype(jnp.int32), a, got)


def _rows(parts, n_rows, dtype):
    flat = jnp.concatenate([p.reshape(-1) for p in parts])
    pad = n_rows * D - flat.shape[0]
    return jnp.concatenate([flat, jnp.zeros((pad,), dtype)]).reshape(n_rows, D)


def _take(flat, shapes):
    out, o = [], 0
    for shp in shapes:
        n = int(np.prod(shp))
        out.append(flat[o:o + n].reshape(shp))
        o += n
    return out


def _round_up(n, m):
    return -(-n // m) * m


def pack_shard(w):
    parts = [w[n][l] for l in range(DEPTH) for n in SHARDED] + [w[n][l] for l in range(DEPTH) for n in ("ln_g", "ln_b")]
    n = sum(int(np.prod(p.shape)) for p in parts)
    return _rows(parts, _round_up(-(-n // D), 16), F32)


def unpack_shard(pk, like):
    shapes = [like[n].shape[1:] for l in range(DEPTH) for n in SHARDED] + [like[n].shape[1:] for l in range(DEPTH) for n in ("ln_g", "ln_b")]
    pieces = _take(pk.reshape(-1), shapes)
    names = [n for l in range(DEPTH) for n in SHARDED] + [n for l in range(DEPTH) for n in ("ln_g", "ln_b")]
    out = {}
    for n in list(SHARDED) + ["ln_g", "ln_b"]:
        out[n] = jnp.stack([p for p, m in zip(pieces, names) if m == n])
    return out


def pack_small(w):
    parts = [w[n][l] for l in range(DEPTH) for n in SMALL]
    n = sum(int(np.prod(p.shape)) for p in parts)
    return _rows(parts, _round_up(-(-n // D), 8), F32)


def unpack_small(pk, like):
    shapes = [like[n].shape[1:] for l in range(DEPTH) for n in SMALL]
    pieces = _take(pk.reshape(-1), shapes)
    names = [n for l in range(DEPTH) for n in SMALL]
    return {n: jnp.stack([p for p, m in zip(pieces, names) if m == n]) for n in SMALL}


def pack_gather(w):
    parts = [w[n][l].astype(BF16) for l in range(DEPTH) for n in ["ada_w"] + list(SHARDED)]
    ln = jnp.concatenate([w[n][l].reshape(-1) for l in range(DEPTH) for n in ("ln_g", "ln_b")])
    parts.append(lax.bitcast_convert_type(ln, BF16))
    n = sum(int(np.prod(p.shape)) for p in parts)
    return _rows(parts, _round_up(-(-n // D), 16), BF16)


def unpack_gather(g, w):
    names = ["ada_w"] + list(SHARDED)
    shapes = [w[n].shape[1:] for l in range(DEPTH) for n in names]
    n_ln = DEPTH * 2 * 3 * (D // N_CHIPS)
    flat = g.reshape(N_CHIPS, -1)
    per_chip = [_take(flat[k], shapes + [(n_ln, 2)]) for k in range(N_CHIPS)]
    layers = [dict() for _ in range(DEPTH)]
    i = 0
    for l in range(DEPTH):
        for n in names:
            axis = 1 if n == "ada_w" else SHARDED[n]
            layers[l][n] = jnp.concatenate([per_chip[k][i] for k in range(N_CHIPS)], axis=axis)
            i += 1
    ln = [lax.bitcast_convert_type(per_chip[k][i], F32).reshape(DEPTH, 2, 3, D // N_CHIPS) for k in range(N_CHIPS)]
    ln = jnp.concatenate(ln, axis=3)
    for l in range(DEPTH):
        layers[l]["ln_g"], layers[l]["ln_b"] = ln[l, 0], ln[l, 1]
    return layers


def pack_grads(grads, k):
    parts = []
    for l in range(DEPTH):
        g = grads[l]
        full = {"ffn1_w_out": g["ffn1_out"], "ffn2_w_out": g["ffn2_out"], "mix_w_in": mix_in_unext(g["mix_in"]),
                "mix_w_out": mix_out_unext(g["mix_out"]), "mla_w_uq": uq_unext(g["wq"]), "mla_w_ukv": ukv_unext(g["wkv"])}
        for n, axis in SHARDED.items():
            if n in ("ffn1_w_in", "ffn2_w_in"):
                half = g[n.replace("_w_in", "_in")][k // 2]
                parts.append(half[:, (k % 2) * (FF // 2):(k % 2 + 1) * (FF // 2)])
            else:
                sz = full[n].shape[axis] // N_CHIPS
                parts.append(lax.slice_in_dim(full[n], k * sz, (k + 1) * sz, axis=axis))
    for l in range(DEPTH):
        for n in ("ln_g", "ln_b"):
            parts.append(grads[l][n][:, k * (D // N_CHIPS):(k + 1) * (D // N_CHIPS)])
    n = sum(int(np.prod(p.shape)) for p in parts)
    return _rows(parts, _round_up(-(-n // D), 16), F32)


def kernel(x, c, ada_w, ada_b, ln_g, ln_b, ffn1_w_in, ffn1_w_out, ffn2_w_in, ffn2_w_out, mix_w_in, mix_w_out, hgrn_lb_logits, hgrn_norm_g, mla_q_norm_g, mla_kv_norm_g, mla_w_uq, mla_w_ukv, fox_b_f, gmlp_ln_g, gmlp_ln_b, gmlp_w_s, gmlp_b_s, loss_target, m_ada_w, m_ada_b, m_ln_g, m_ln_b, m_ffn1_w_in, m_ffn1_w_out, m_ffn2_w_in, m_ffn2_w_out, m_mix_w_in, m_mix_w_out, m_hgrn_lb_logits, m_hgrn_norm_g, m_mla_q_norm_g, m_mla_kv_norm_g, m_mla_w_uq, m_mla_w_ukv, m_fox_b_f, m_gmlp_ln_g, m_gmlp_ln_b, m_gmlp_w_s, m_gmlp_b_s, v_ada_w, v_ada_b, v_ln_g, v_ln_b, v_ffn1_w_in, v_ffn1_w_out, v_ffn2_w_in, v_ffn2_w_out, v_mix_w_in, v_mix_w_out, v_hgrn_lb_logits, v_hgrn_norm_g, v_mla_q_norm_g, v_mla_kv_norm_g, v_mla_w_uq, v_mla_w_ukv, v_fox_b_f, v_gmlp_ln_g, v_gmlp_ln_b, v_gmlp_w_s, v_gmlp_b_s):
    w = dict(zip(WEIGHTS, (ada_w, ada_b, ln_g, ln_b, ffn1_w_in, ffn1_w_out, ffn2_w_in, ffn2_w_out, mix_w_in, mix_w_out, hgrn_lb_logits, hgrn_norm_g, mla_q_norm_g, mla_kv_norm_g, mla_w_uq, mla_w_ukv, fox_b_f, gmlp_ln_g, gmlp_ln_b, gmlp_w_s, gmlp_b_s)))
    m = dict(zip(WEIGHTS, (m_ada_w, m_ada_b, m_ln_g, m_ln_b, m_ffn1_w_in, m_ffn1_w_out, m_ffn2_w_in, m_ffn2_w_out, m_mix_w_in, m_mix_w_out, m_hgrn_lb_logits, m_hgrn_norm_g, m_mla_q_norm_g, m_mla_kv_norm_g, m_mla_w_uq, m_mla_w_ukv, m_fox_b_f, m_gmlp_ln_g, m_gmlp_ln_b, m_gmlp_w_s, m_gmlp_b_s)))
    v = dict(zip(WEIGHTS, (v_ada_w, v_ada_b, v_ln_g, v_ln_b, v_ffn1_w_in, v_ffn1_w_out, v_ffn2_w_in, v_ffn2_w_out, v_mix_w_in, v_mix_w_out, v_hgrn_lb_logits, v_hgrn_norm_g, v_mla_q_norm_g, v_mla_kv_norm_g, v_mla_w_uq, v_mla_w_ukv, v_fox_b_f, v_gmlp_ln_g, v_gmlp_ln_b, v_gmlp_w_s, v_gmlp_b_s)))
    Bl, S, _ = x.shape
    T = Bl * S
    core = lax.axis_index("c")
    chip = 2 * lax.axis_index("x") + lax.axis_index("y")

    def shard(key):
        n, l = key
        if n == "ln":
            return jnp.concatenate([ln_g[l:l + 1], ln_b[l:l + 1], jnp.zeros((1, 2, D // N_CHIPS), F32)], axis=1)
        return w[n][l:l + 1].astype(BF16)

    mixers = ["mix_w_in", "mix_w_out", "mla_w_uq", "mla_w_ukv"]
    groups = [[("ada_w", 0), ("ffn1_w_in", 0), ("ffn1_w_out", 0), ("ln", 0)],
              [(n, 0) for n in mixers + ["ffn2_w_in", "ffn2_w_out"]],
              [(n, 1) for n in GATHERED + ["ln"]]]
    srcs = [[shard(k) for k in grp] for grp in groups]
    lands = [[own_slot(s, chip) for s in grp] for grp in srcs]
    handle0 = ag_start(srcs[0], lands[0], jnp.zeros((8, 128), F32), "ag_start_0")
    first, token = ag_forward(ag_wait(handle0, lands[2][0], "ag_wait_0")[1], "ag_forward_0")
    have = dict(zip(groups[0], first))
    handles = {}
    for gi in (1, 2):
        handles[gi] = ag_start(srcs[gi], lands[gi], token, "ag_start_%d" % gi)
        token = handles[gi][-1]
    c8 = jnp.concatenate([c, jnp.zeros((8 - Bl, D), F32)], axis=0)
    c8 = c8 + token[0, 0]

    def cat_cols(a):
        return jnp.concatenate([a[0, k] for k in range(N_CHIPS)], axis=1)

    def get(l, part, after):
        gi = 2 if l == 1 else (0 if part in ("ada", "ffn1") else 1)
        if gi in handles:
            arrived, _ = ag_forward(ag_wait(handles.pop(gi), after, "ag_wait_%d" % gi)[1], "ag_forward_%d" % gi)
            have.update(zip(groups[gi], arrived))
        if part == "ada":
            return dict(ada_w=have[("ada_w", l)], wl=0, ada_b=ada_b[l][None])
        if part == "ffn1":
            ln_full = jnp.moveaxis(have[("ln", l)][0], 0, 1).reshape(8, D)
            return dict(ffn1_in=have[("ffn1_w_in", l)], ffn1_out=have[("ffn1_w_out", l)], wl=0,
                        ln_g=ln_full[0:3], ln_b=ln_full[3:6])
        if part == "ffn2":
            return dict(ffn2_in=have[("ffn2_w_in", l)], ffn2_out=have[("ffn2_w_out", l)])
        return dict(
            mix_in=mix_in_ext(cat_cols(have[("mix_w_in", l)])), mix_out=mix_out_ext(have[("mix_w_out", l)].reshape(D, D)),
            wq=uq_ext(cat_cols(have[("mla_w_uq", l)])).astype(F32), wkv=ukv_ext(cat_cols(have[("mla_w_ukv", l)])).astype(F32),
            ng=hgrn_norm_g[l][None], gq=mla_q_norm_g[l][None], gkv=mla_kv_norm_g[l][None],
            bcol=jnp.concatenate([fox_b_f[l], jnp.zeros((8 - HEADS,), F32)])[:, None],
            g_lg=gmlp_ln_g[l][None], g_lb=gmlp_ln_b[l][None], ws=gmlp_w_s[l], bs=gmlp_b_s[l][:, :, None])

    pending = []

    def emit(l, part, g):
        if part == "mix":
            names = mixers
            by_chip = [_col_shards(mix_in_unext(g["mix_in"])), mix_out_unext(g["mix_out"]).reshape(N_CHIPS, D // N_CHIPS, D),
                       _col_shards(uq_unext(g["wq"])), _col_shards(ukv_unext(g["wkv"]))]
        else:
            names = [part + "_w_in", part + "_w_out"]
            by_chip = [g[part + "_in"], g[part + "_out"]]
        tag = "%d_%s" % (l, part)
        got = sibling_swap(by_chip, "sibling_swap_" + tag)
        chip_sum = [add_kept_half(a, r, core, "add_sibling") for a, r in zip(by_chip, got)]
        slot = lax.broadcasted_iota(jnp.int32, (N_CHIPS, 1, 1), 0)
        zones = [jnp.where(slot == chip, h, 0.0) for h in chip_sum]
        handle = rs_start(chip_sum, zones, chip_sum[0], "rs_start_" + tag)
        pending.append((l, names, handle, tag))
        return handle[-1][0, 0]

    loss_tile, dx, dmods, grads, d_logits = local_step(
        x.reshape(T, D), c8, loss_target.reshape(T, D), get, hgrn_lb_logits, S, emit)
    loss = lax.psum(loss_tile[0, 0], ("x", "y", "c"))

    small_g = {"hgrn_lb_logits": d_logits,
               "hgrn_norm_g": jnp.stack([grads[l]["ng"][0] for l in range(DEPTH)]),
               "mla_q_norm_g": jnp.stack([grads[l]["gq"][0] for l in range(DEPTH)]),
               "mla_kv_norm_g": jnp.stack([grads[l]["gkv"][0] for l in range(DEPTH)]),
               "fox_b_f": jnp.stack([grads[l]["bcol"][0:HEADS, 0] for l in range(DEPTH)]),
               "gmlp_ln_g": jnp.stack([grads[l]["g_lg"][0] for l in range(DEPTH)]),
               "gmlp_ln_b": jnp.stack([grads[l]["g_lb"][0] for l in range(DEPTH)]),
               "gmlp_w_s": jnp.stack([grads[l]["ws"] for l in range(DEPTH)]),
               "gmlp_b_s": jnp.stack([grads[l]["bs"][:, :, 0] for l in range(DEPTH)])}
    small_g["ln_g"] = jnp.stack([grads[l]["ln_g"] for l in range(DEPTH)])
    small_g["ln_b"] = jnp.stack([grads[l]["ln_b"] for l in range(DEPTH)])
    pk_small = pack_small(small_g)
    n_small = pk_small.shape[0]
    extras = [dmods[l] for l in range(DEPTH)] + [c]
    n_extra = _round_up(-(-sum(int(np.prod(e.shape)) for e in extras) // D), 8)
    gathered = ag_all(jnp.concatenate([pk_small, _rows(extras, n_extra, F32)], axis=0))
    g_small = unpack_small(sum_slots(gathered[:, 0:n_small], 8, "sum_small"), small_g)
    ext = gathered[:, n_small:].reshape(8, -1)
    n_dmod = DEPTH * Bl * N_MOD * D
    dmod_all = ext[:, 0:n_dmod].reshape(8, DEPTH, Bl, N_MOD * D)
    c_all = ext[:, n_dmod:n_dmod + Bl * D].reshape(8 * Bl, D)
    g_ada_w, g_ada_b = [], []
    ncol = N_MOD * D // N_CHIPS
    for l in range(DEPTH):
        dm = dmod_all[:, l].reshape(8 * Bl, N_MOD * D)
        g_ada_w.append(ada_grad(c_all, lax.dynamic_slice_in_dim(dm, chip * ncol, ncol, axis=1)))
        g_ada_b.append(sum_slots(dm.reshape(8 * Bl, N_MOD, D), 8 * Bl, "sum_ada_b").reshape(N_MOD * D))
    g_ada_w, g_ada_b = jnp.stack(g_ada_w), jnp.stack(g_ada_b)

    red = {n: jnp.zeros(w[n].shape, F32) for n in REDUCED}

    def arrive(entry, after):
        l, names, handle, tag = entry
        for n, land in zip(names, rs_wait(handle, after, "rs_wait_" + tag)[1]):
            red[n] = sum_into(land, red[n], l, core, "sum_chips")

    for entry in pending[:-1]:
        arrive(entry, dx)
    late = pending[-1][1]
    early = [n for n in REDUCED if n not in late]
    grad = dict(zip(early, sibling_join([red[n] for n in early], "sibling_join_a")))
    grad.update(g_small)
    grad["ada_w"], grad["ada_b"] = g_ada_w, g_ada_b
    for n in ("ln_g", "ln_b"):
        grad[n] = lax.dynamic_slice_in_dim(g_small[n], chip * (D // N_CHIPS), D // N_CHIPS, axis=2)
    out = {"grad": grad, "delta": {}, "new_m": {}, "new_v": {}}

    def update(n):
        shp = w[n].shape
        two_d = (-1, shp[-1])
        res = adamw(w[n].reshape(two_d), grad[n].reshape(two_d), m[n].reshape(two_d), v[n].reshape(two_d), "adamw_" + n)
        grad[n] = grad[n].reshape(shp)
        for key, r in zip(("delta", "new_m", "new_v"), res):
            out[key][n] = r.reshape(shp)

    for n in WEIGHTS:
        if n not in late:
            update(n)
    arrive(pending[-1], out["delta"]["ffn2_w_in"])
    grad.update(zip(late, sibling_join([red[n] for n in late], "sibling_join_b")))
    for n in late:
        update(n)
    outs = [loss, dx.reshape(Bl, S, D)]
    for key in ("grad", "delta", "new_m", "new_v"):
        outs += [out[key][n] for n in WEIGHTS]
    return tuple(outs)
```

```python
import functools

import jax
import jax.numpy as jnp
import numpy as np
from jax import lax
from jax.experimental import pallas as pl
from jax.experimental.pallas import tpu as pltpu

F32, BF16 = jnp.float32, jnp.bfloat16
MESH = pl.DeviceIdType.MESH

N_CHIPS = 4
D = 1024
DEPTH = 2
FF = 2816
N_MOD = 9
GW = 256
HEADS = 4
HD = 64
HP = 128
A_CHUNK = 16
LB_FLOOR = 1e-30
B_Q_LORA, B_KV_LORA, B_NOPE, B_ROPE = 256, 128, 64, 32
ROPE_THETA = 10000.0
D_CHUNK = 128
MIX_COLS = 2724
ALPHA = (2 * DEPTH) ** 0.25
LN_EPS = 1e-5
RMS_EPS = 1e-6
ADAM_LR, ADAM_B1, ADAM_B2, ADAM_EPS, ADAM_WD, ADAM_STEP = 0.001, 0.9, 0.999, 1e-08, 0.01, 10

NP = 3840
NP_TILE = 1920
C_A, C_B, C_CQ, C_CK, C_CV, C_D, C_CF = 0, 1024, 1536, 2048, 2560, 3072, 3584
NCAT = 1536
O_A, O_B, O_C, O_D = 0, 256, 768, 1280

VMEM_BIG = 48 << 20


def _cparams(vmem=None):
    return pltpu.CompilerParams(vmem_limit_bytes=vmem) if vmem else pltpu.CompilerParams()


def _sds(shape, dtype):
    return jax.ShapeDtypeStruct(tuple(shape), dtype)


@jax.custom_vjp
def _mm(a, w):
    return jnp.dot(a.astype(BF16), w.astype(BF16), preferred_element_type=F32)


def _mm_f(a, w):
    return _mm(a, w), (a, w)


def _mm_b(res, g):
    a, w = res
    gb = g.astype(BF16)
    da = lax.dot_general(gb, w.astype(BF16), (((1,), (1,)), ((), ())), preferred_element_type=F32)
    dw = lax.dot_general(a.astype(BF16), gb, (((0,), (0,)), ((), ())), preferred_element_type=F32)
    return da.astype(a.dtype), dw.astype(w.dtype)


_mm.defvjp(_mm_f, _mm_b)


@jax.custom_vjp
def _mm_nt(a, b):
    return lax.dot_general(a.astype(BF16), b.astype(BF16), (((1,), (1,)), ((), ())), preferred_element_type=F32)


def _mm_nt_f(a, b):
    return _mm_nt(a, b), (a, b)


def _mm_nt_b(res, g):
    a, b = res
    gb = g.astype(BF16)
    da = jnp.dot(gb, b.astype(BF16), preferred_element_type=F32)
    db = lax.dot_general(gb, a.astype(BF16), (((0,), (0,)), ((), ())), preferred_element_type=F32)
    return da.astype(a.dtype), db.astype(b.dtype)


_mm_nt.defvjp(_mm_nt_f, _mm_nt_b)


@jax.custom_vjp
def _mm_tn(a, b):
    return lax.dot_general(a.astype(BF16), b.astype(BF16), (((0,), (0,)), ((), ())), preferred_element_type=F32)


def _mm_tn_f(a, b):
    return _mm_tn(a, b), (a, b)


def _mm_tn_b(res, g):
    a, b = res
    gb = g.astype(BF16)
    da = lax.dot_general(b.astype(BF16), gb, (((1,), (1,)), ((), ())), preferred_element_type=F32)
    db = jnp.dot(a.astype(BF16), gb, preferred_element_type=F32)
    return da.astype(a.dtype), db.astype(b.dtype)


_mm_tn.defvjp(_mm_tn_f, _mm_tn_b)


def _mm_hi(a, w):
    return jnp.dot(a, w, precision=lax.Precision.HIGHEST, preferred_element_type=F32)


def _iota(shape, dim):
    return lax.broadcasted_iota(jnp.int32, shape, dim)


def _head_sum_mats():
    e = (_iota((GW, HP), 0) // HD == _iota((GW, HP), 1)).astype(F32)
    et = (_iota((HP, GW), 1) // HD == _iota((HP, GW), 0)).astype(F32)
    return e, et


def _modulate(x, mod_ref, sh, sc):
    return x * (1.0 + mod_ref[sc:sc + 1, :]) + mod_ref[sh:sh + 1, :]


def _ln_res(x, y, gate, lg, lb, gs):
    r = ALPHA * x + gs * (1.0 + gate) * y
    mu = jnp.mean(r, axis=-1, keepdims=True)
    var = jnp.mean(jnp.square(r - mu), axis=-1, keepdims=True)
    return (r - mu) * lax.rsqrt(var + LN_EPS) * lg + lb


def _rms(x, g):
    return x * lax.rsqrt(jnp.mean(x * x, axis=-1, keepdims=True) + RMS_EPS) * g


def _tile(n, pref):
    return pref if n % pref == 0 else n


def mod_fwd(c8, w, l, b):
    tn = w.shape[3]
    n = N_CHIPS * tn

    def body(c_ref, w_ref, b_ref, o_ref):
        h = jax.nn.silu(c_ref[...]).astype(BF16)
        o_ref[...] = jnp.dot(h, w_ref[...], preferred_element_type=F32) + b_ref[...]

    return pl.pallas_call(
        body, name="mod_fwd", grid=(N_CHIPS,),
        in_specs=[pl.BlockSpec((8, D), lambda j: (0, 0)), pl.BlockSpec((None, None, D, tn), lambda j: (l, j, 0, 0)),
                  pl.BlockSpec((1, tn), lambda j: (0, j))],
        out_specs=pl.BlockSpec((8, tn), lambda j: (0, j)), out_shape=_sds((8, n), F32),
        compiler_params=_cparams(VMEM_BIG))(c8, w, b)


def ffn_in_fwd(x, mod, w_in, l, sh, sc, S):
    T = x.shape[0]
    tm, tn = _tile(S, 512), FF // 2
    tpb, nj = S // tm, 2

    def body(x_ref, mod_ref, wg_ref, wu_ref, zg_ref, zu_ref, act_ref, h_ref):
        @pl.when(pl.program_id(1) == 0)
        def _():
            h_ref[...] = _modulate(x_ref[...], mod_ref, sh, sc).astype(BF16)
        g = jnp.dot(h_ref[...], wg_ref[...], preferred_element_type=F32)
        u = jnp.dot(h_ref[...], wu_ref[...], preferred_element_type=F32)
        zg_ref[...] = g.astype(BF16)
        zu_ref[...] = u.astype(BF16)
        act_ref[...] = (jax.nn.silu(g) * u).astype(BF16)

    return pl.pallas_call(
        body, name="ffn_in_fwd", grid=(T // tm, nj),
        in_specs=[pl.BlockSpec((tm, D), lambda i, j: (i, 0)),
                  pl.BlockSpec((None, N_MOD, D), lambda i, j: (i // tpb, 0, 0)),
                  pl.BlockSpec((None, None, D, tn), lambda i, j: (l, j, 0, 0)),
                  pl.BlockSpec((None, None, D, tn), lambda i, j: (l, j + nj, 0, 0))],
        out_specs=[pl.BlockSpec((tm, tn), lambda i, j: (i, j))] * 3,
        out_shape=[_sds((T, FF), BF16)] * 3,
        scratch_shapes=[pltpu.VMEM((tm, D), BF16)],
        compiler_params=_cparams(VMEM_BIG))(x, mod, w_in, w_in)


def mix_in_fwd(x, mod, w, sh, sc, S):
    T = x.shape[0]
    n = w.shape[1]
    tm, tn = _tile(S, 512), NP_TILE
    tpb = S // tm

    def body(x_ref, mod_ref, w_ref, o_ref, h_ref):
        @pl.when(pl.program_id(1) == 0)
        def _():
            h_ref[...] = _modulate(x_ref[...], mod_ref, sh, sc).astype(BF16)
        o_ref[...] = jnp.dot(h_ref[...], w_ref[...], preferred_element_type=F32)

    return pl.pallas_call(
        body, name="mix_in_fwd", grid=(T // tm, n // tn),
        in_specs=[pl.BlockSpec((tm, D), lambda i, j: (i, 0)),
                  pl.BlockSpec((None, N_MOD, D), lambda i, j: (i // tpb, 0, 0)),
                  pl.BlockSpec((D, tn), lambda i, j: (0, j))],
        out_specs=pl.BlockSpec((tm, tn), lambda i, j: (i, j)), out_shape=_sds((T, n), F32),
        scratch_shapes=[pltpu.VMEM((tm, D), BF16)],
        compiler_params=_cparams(VMEM_BIG))(x, mod, w)


def out_ln_fwd(act, w_out, x, mod, lg, lb, gate, gs, S, l=None):
    T, K = act.shape
    tm = _tile(S, 512)
    tpb = S // tm

    def body(a_ref, w_ref, x_ref, mod_ref, lg_ref, lb_ref, y_ref, xn_ref):
        y = jnp.dot(a_ref[...], w_ref[...].reshape(K, D), preferred_element_type=F32)
        y_ref[...] = y
        xn_ref[...] = _ln_res(x_ref[...], y, mod_ref[gate:gate + 1, :], lg_ref[...], lb_ref[...], gs)

    if l is None:
        w_spec = pl.BlockSpec((K, D), lambda i: (0, 0))
    else:
        w_spec = pl.BlockSpec((None, N_CHIPS, K // N_CHIPS, D), lambda i: (l, 0, 0, 0))
    return pl.pallas_call(
        body, name="out_ln_fwd", grid=(T // tm,),
        in_specs=[pl.BlockSpec((tm, K), lambda i: (i, 0)), w_spec,
                  pl.BlockSpec((tm, D), lambda i: (i, 0)),
                  pl.BlockSpec((None, N_MOD, D), lambda i: (i // tpb, 0, 0)),
                  pl.BlockSpec((1, D), lambda i: (0, 0)), pl.BlockSpec((1, D), lambda i: (0, 0))],
        out_specs=[pl.BlockSpec((tm, D), lambda i: (i, 0))] * 2,
        out_shape=[_sds((T, D), F32), _sds((T, D), F32)],
        compiler_params=_cparams(VMEM_BIG))(act, w_out, x, mod, lg, lb)


def ln_res_bwd(dxn, x, y, mod, lg, lb, gate, gs, S):
    T = x.shape[0]
    B = T // S
    tm = _tile(S, 512)
    tpb = S // tm

    def body(d_ref, x_ref, y_ref, mod_ref, lg_ref, lb_ref, dx_ref, dy_ref, dg_ref, dlg_ref, dlb_ref):
        i = pl.program_id(0)
        f = functools.partial(_ln_res, gs=gs)
        _, vjp = jax.vjp(f, x_ref[...], y_ref[...], mod_ref[gate:gate + 1, :], lg_ref[...], lb_ref[...])
        dx, dy, dg, dlg, dlb = vjp(d_ref[...])
        dx_ref[...] = dx
        dy_ref[...] = dy.astype(BF16)

        @pl.when(i % tpb == 0)
        def _():
            dg_ref[...] = jnp.zeros_like(dg_ref)

        @pl.when(i == 0)
        def _():
            dlg_ref[...] = jnp.zeros_like(dlg_ref)
            dlb_ref[...] = jnp.zeros_like(dlb_ref)

        dg_ref[...] += dg
        dlg_ref[...] += dlg
        dlb_ref[...] += dlb

    tok = pl.BlockSpec((tm, D), lambda i: (i, 0))
    vec = pl.BlockSpec((1, D), lambda i: (0, 0))
    return pl.pallas_call(
        body, name="ln_res_bwd", grid=(T // tm,),
        in_specs=[tok, tok, tok, pl.BlockSpec((None, N_MOD, D), lambda i: (i // tpb, 0, 0)), vec, vec],
        out_specs=[tok, tok, pl.BlockSpec((None, 1, D), lambda i: (i // tpb, 0, 0)), vec, vec],
        out_shape=[_sds((T, D), F32), _sds((T, D), BF16), _sds((B, 1, D), F32), _sds((1, D), F32), _sds((1, D), F32)],
        compiler_params=_cparams(VMEM_BIG))(dxn, x, y, mod, lg, lb)


def swiglu_bwd(dy, w_out, l, zg, zu, S):
    T = dy.shape[0]
    tm, tn = _tile(S, 512), FF // 2

    def body(dy_ref, w_ref, zg_ref, zu_ref, dg_ref, du_ref):
        da = lax.dot_general(dy_ref[...], w_ref[...].reshape(tn, D), (((1,), (1,)), ((), ())), preferred_element_type=F32)
        g, u = zg_ref[...].astype(F32), zu_ref[...].astype(F32)
        sg = jax.nn.sigmoid(g)
        dg_ref[...] = (da * u * (sg * (1.0 + g * (1.0 - sg)))).astype(BF16)
        du_ref[...] = (da * (g * sg)).astype(BF16)

    zt = pl.BlockSpec((tm, tn), lambda i, j: (i, j))
    return pl.pallas_call(
        body, name="swiglu_bwd", grid=(T // tm, FF // tn),
        in_specs=[pl.BlockSpec((tm, D), lambda i, j: (i, 0)),
                  pl.BlockSpec((None, 2, FF // N_CHIPS, D), lambda i, j: (l, j, 0, 0)), zt, zt],
        out_specs=[zt, zt], out_shape=[_sds((T, FF), BF16), _sds((T, FF), BF16)],
        compiler_params=_cparams(VMEM_BIG))(dy, w_out, zg, zu)


def nt_plain(dy, w):
    T = dy.shape[0]
    K = w.shape[0]
    tm = _tile(T, 512)

    def body(dy_ref, w_ref, o_ref):
        o_ref[...] = lax.dot_general(dy_ref[...], w_ref[...], (((1,), (1,)), ((), ())), preferred_element_type=F32)

    return pl.pallas_call(
        body, name="nt_plain", grid=(T // tm,),
        in_specs=[pl.BlockSpec((tm, D), lambda i: (i, 0)), pl.BlockSpec((K, D), lambda i: (0, 0))],
        out_specs=pl.BlockSpec((tm, K), lambda i: (i, 0)), out_shape=_sds((T, K), F32),
        compiler_params=_cparams(VMEM_BIG))(dy, w)


def tn_mm(a, b, tk):
    T, K = a.shape
    N = b.shape[1]
    tt = _tile(T, 512)

    def body(a_ref, b_ref, o_ref):
        @pl.when(pl.program_id(1) == 0)
        def _():
            o_ref[...] = jnp.zeros_like(o_ref)
        o_ref[...] += lax.dot_general(a_ref[...], b_ref[...], (((0,), (0,)), ((), ())), preferred_element_type=F32)

    return pl.pallas_call(
        body, name="tn_mm", grid=(K // tk, T // tt),
        in_specs=[pl.BlockSpec((tt, tk), lambda k, t: (t, k)), pl.BlockSpec((tt, N), lambda k, t: (t, 0))],
        out_specs=pl.BlockSpec((tk, N), lambda k, t: (k, 0)), out_shape=_sds((K, N), F32),
        compiler_params=_cparams(VMEM_BIG))(a, b)


def tn_mm_mod(x, mod, b, sh, sc, S, tn):
    T = x.shape[0]
    N = b.shape[1]
    tt = _tile(S, 512)
    tpb = S // tt

    def body(x_ref, mod_ref, b_ref, o_ref):
        @pl.when(pl.program_id(1) == 0)
        def _():
            o_ref[...] = jnp.zeros_like(o_ref)
        h = _modulate(x_ref[...], mod_ref, sh, sc).astype(BF16)
        o_ref[...] += lax.dot_general(h, b_ref[...], (((0,), (0,)), ((), ())), preferred_element_type=F32)

    return pl.pallas_call(
        body, name="tn_mm_mod", grid=(N // tn, T // tt),
        in_specs=[pl.BlockSpec((tt, D), lambda j, t: (t, 0)),
                  pl.BlockSpec((None, N_MOD, D), lambda j, t: (t // tpb, 0, 0)),
                  pl.BlockSpec((tt, tn), lambda j, t: (t, j))],
        out_specs=pl.BlockSpec((D, tn), lambda j, t: (0, j)), out_shape=_sds((D, N), F32),
        compiler_params=_cparams(VMEM_BIG))(x, mod, b)


def tn_mm_mod_shards(x, mod, bg, bu, sh, sc, S):
    T = x.shape[0]
    tn = FF // 2
    tt = _tile(S, 512)
    tpb = S // tt

    def body(x_ref, mod_ref, bg_ref, bu_ref, o_ref):
        j = pl.program_id(0)

        @pl.when(pl.program_id(1) == 0)
        def _():
            o_ref[...] = jnp.zeros_like(o_ref)
        h = _modulate(x_ref[...], mod_ref, sh, sc).astype(BF16)

        @pl.when(j < 2)
        def _():
            o_ref[...] += lax.dot_general(h, bg_ref[...], (((0,), (0,)), ((), ())), preferred_element_type=F32)

        @pl.when(j >= 2)
        def _():
            o_ref[...] += lax.dot_general(h, bu_ref[...], (((0,), (0,)), ((), ())), preferred_element_type=F32)

    return pl.pallas_call(
        body, name="tn_mm_mod_shards", grid=(N_CHIPS, T // tt),
        in_specs=[pl.BlockSpec((tt, D), lambda j, t: (t, 0)),
                  pl.BlockSpec((None, N_MOD, D), lambda j, t: (t // tpb, 0, 0)),
                  pl.BlockSpec((tt, tn), lambda j, t: (jnp.where(j < 2, t, 0), jnp.minimum(j, 1))),
                  pl.BlockSpec((tt, tn), lambda j, t: (jnp.where(j < 2, 0, t), jnp.maximum(j - 2, 0)))],
        out_specs=pl.BlockSpec((None, D, tn), lambda j, t: (j, 0, 0)), out_shape=_sds((N_CHIPS, D, tn), F32),
        compiler_params=_cparams(VMEM_BIG))(x, mod, bg, bu)


def nt_mod_bwd(ds, w, offs, x, mod, dres, sc, S, tk, l=None):
    T = x.shape[0]
    B = T // S
    tm = _tile(S, 512)
    tpb = S // tm
    Kd = ds[0].shape[1]
    nk = Kd // tk
    n_in = len(ds)

    def body(*refs):
        d_refs, w_refs = refs[:n_in], refs[n_in:2 * n_in]
        x_ref, mod_ref, r_ref, dx_ref, dsh_ref, dsc_ref, acc = refs[2 * n_in:]
        i, k = pl.program_id(0), pl.program_id(1)

        part = sum(lax.dot_general(d_ref[...], w_ref[...], (((1,), (1,)), ((), ())), preferred_element_type=F32)
                   for d_ref, w_ref in zip(d_refs, w_refs))

        @pl.when(k == 0)
        def _():
            acc[...] = part

        @pl.when(k > 0)
        def _():
            acc[...] += part

        @pl.when(k == nk - 1)
        def _():
            dh = acc[...]
            dx_ref[...] = dh * (1.0 + mod_ref[sc:sc + 1, :]) + r_ref[...]

            @pl.when(i % tpb == 0)
            def _():
                dsh_ref[...] = jnp.zeros_like(dsh_ref)
                dsc_ref[...] = jnp.zeros_like(dsc_ref)

            dsh_ref[...] += jnp.sum(dh, axis=0, keepdims=True)
            dsc_ref[...] += jnp.sum(dh * x_ref[...], axis=0, keepdims=True)

    tok = pl.BlockSpec((tm, D), lambda i, k: (i, 0))
    vec = pl.BlockSpec((None, 1, D), lambda i, k: (i // tpb, 0, 0))
    in_specs = [pl.BlockSpec((tm, tk), lambda i, k: (i, k)) for _ in ds]
    if l is None:
        in_specs += [pl.BlockSpec((D, tk), functools.partial(lambda i, k, o: (0, k + o), o=off // tk)) for off in offs]
    else:
        in_specs += [pl.BlockSpec((None, None, D, tk), functools.partial(lambda i, k, o: (l, k + o, 0, 0), o=off)) for off in offs]
    in_specs += [tok, pl.BlockSpec((None, N_MOD, D), lambda i, k: (i // tpb, 0, 0)), tok]
    return pl.pallas_call(
        body, name="nt_mod_bwd", grid=(T // tm, nk), in_specs=in_specs,
        out_specs=[tok, vec, vec],
        out_shape=[_sds((T, D), F32), _sds((B, 1, D), F32), _sds((B, 1, D), F32)],
        scratch_shapes=[pltpu.VMEM((tm, D), F32)],
        compiler_params=_cparams(VMEM_BIG))(*ds, *([w] * n_in), x, mod, dres)


def loss_head(y, tgt):
    T = y.shape[0]
    tm = _tile(T, 512)

    def body(y_ref, t_ref, l_ref, d_ref):
        @pl.when(pl.program_id(0) == 0)
        def _():
            l_ref[...] = jnp.zeros_like(l_ref)
        e = y_ref[...] - t_ref[...]
        d_ref[...] = e * (1.0 / D)
        l_ref[...] += 0.5 * jnp.sum(jnp.sum(e * e, axis=1, keepdims=True) * (1.0 / D))

    tok = pl.BlockSpec((tm, D), lambda i: (i, 0))
    return pl.pallas_call(
        body, name="loss_head", grid=(T // tm,), in_specs=[tok, tok],
        out_specs=[pl.BlockSpec((8, 128), lambda i: (0, 0)), tok],
        out_shape=[_sds((8, 128), F32), _sds((T, D), F32)],
        compiler_params=_cparams(VMEM_BIG))(y, tgt)


def _hgrn_block(q, fz, inp, go, st, lb, ng, blk):
    nc = blk // A_CHUNK
    lb_eff = jnp.maximum(lb, LB_FLOOR)
    log_f = jnp.logaddexp(jnp.log(lb_eff), jnp.log1p(-lb) + jax.nn.log_sigmoid(fz))
    k = (1.0 - lb) * jax.nn.sigmoid(-fz) - (lb_eff - lb)
    qf = jax.nn.silu(q)
    same_chunk = _iota((blk, blk), 0) // A_CHUNK == _iota((blk, blk), 1) // A_CHUNK
    tril = (same_chunk & (_iota((blk, blk), 1) <= _iota((blk, blk), 0))).astype(F32)
    G = _mm_hi(tril, log_f)
    e_mat, et_mat = _head_sum_mats()
    G4, q4, k4, v4 = (z.reshape(nc, A_CHUNK, GW) for z in (G, qf, k, inp))
    shp = (nc, A_CHUNK, A_CHUNK, GW)
    one = (1, A_CHUNK, A_CHUNK, GW)
    mask = jnp.where(_iota(one, 2) <= _iota(one, 1), 0.0, -jnp.inf)
    decay = jnp.exp((G4[:, :, None, :] - G4[:, None, :, :]) + mask)
    prod = q4[:, :, None, :] * k4[:, None, :, :] * decay
    scores = _mm(prod.reshape(nc * A_CHUNK * A_CHUNK, GW), e_mat.astype(BF16))
    spread = _mm(scores, et_mat.astype(BF16)).reshape(shp)
    o_intra = jnp.sum(spread * v4[:, None, :, :], axis=2).reshape(blk, GW)
    head_diag = (_iota((GW, GW), 0) // HD == _iota((GW, GW), 1) // HD).astype(F32)
    g_last = [jnp.sum(log_f[c * A_CHUNK:(c + 1) * A_CHUNK], axis=0, keepdims=True) for c in range(nc)]
    g_last_b = jnp.concatenate([jnp.broadcast_to(g, (A_CHUNK, GW)) for g in g_last], axis=0)
    q_dec = qf * jnp.exp(G)
    k_end = k * jnp.exp(g_last_b - G)
    outs = []
    for c in range(nc):
        rows = slice(c * A_CHUNK, (c + 1) * A_CHUNK)
        outs.append(_mm_nt(q_dec[rows], st))
        st = st * jnp.exp(g_last[c]) + _mm_tn(inp[rows], k_end[rows]) * head_diag
    o = o_intra + jnp.concatenate(outs, axis=0)
    ms = _mm_hi(o * o, e_mat) * (1.0 / HD)
    o = o * _mm_hi(lax.rsqrt(ms + RMS_EPS), et_mat) * ng
    return o * jax.nn.silu(go), st


HGRN_BLK = 128


def hgrn_fwd(proj, lb, ng, S):
    T = proj.shape[0]
    B = T // S
    blk = min(HGRN_BLK, S)
    nb = S // blk

    def body(p_ref, lb_ref, ng_ref, o_ref, st_out_ref, st_ref):
        @pl.when(pl.program_id(1) == 0)
        def _():
            st_ref[...] = jnp.zeros_like(st_ref)
        st_out_ref[...] = st_ref[...]
        p = p_ref[...]
        o, st = _hgrn_block(p[:, 0:GW], p[:, GW:2 * GW], p[:, 2 * GW:3 * GW], p[:, 3 * GW:4 * GW],
                            st_ref[...], lb_ref[...], ng_ref[...], blk)
        o_ref[...] = o.astype(BF16)
        st_ref[...] = st

    vec = pl.BlockSpec((1, GW), lambda b, j: (0, 0))
    return pl.pallas_call(
        body, name="hgrn_fwd", grid=(B, nb),
        in_specs=[pl.BlockSpec((blk, 4 * GW), lambda b, j: (b * nb + j, C_A // (4 * GW))), vec, vec],
        out_specs=[pl.BlockSpec((blk, GW), lambda b, j: (b * nb + j, 0)),
                   pl.BlockSpec((None, GW, GW), lambda b, j: (b * nb + j, 0, 0))],
        out_shape=[_sds((T, GW), BF16), _sds((B * nb, GW, GW), F32)],
        scratch_shapes=[pltpu.VMEM((GW, GW), F32)],
        compiler_params=_cparams(VMEM_BIG))(proj, lb, ng)


def hgrn_bwd(proj, states, dcat, lb, ng, S):
    T = proj.shape[0]
    B = T // S
    blk = min(HGRN_BLK, S)
    nb = S // blk

    def body(p_ref, st_in_ref, do_ref, lb_ref, ng_ref, dp_ref, dlb_ref, dng_ref, dst_ref):
        b, j = pl.program_id(0), pl.program_id(1)

        @pl.when(j == 0)
        def _():
            dst_ref[...] = jnp.zeros_like(dst_ref)

        @pl.when((b == 0) & (j == 0))
        def _():
            dlb_ref[...] = jnp.zeros_like(dlb_ref)
            dng_ref[...] = jnp.zeros_like(dng_ref)

        p = p_ref[...]
        f = functools.partial(_hgrn_block, blk=blk)
        _, vjp = jax.vjp(f, p[:, 0:GW], p[:, GW:2 * GW], p[:, 2 * GW:3 * GW], p[:, 3 * GW:4 * GW],
                         st_in_ref[...], lb_ref[...], ng_ref[...])
        dq, df, di, dg, dst, dlb, dng = vjp((do_ref[...], dst_ref[...]))
        dp_ref[...] = jnp.concatenate([dq, df, di, dg], axis=1).astype(BF16)
        dst_ref[...] = dst
        dlb_ref[...] += dlb
        dng_ref[...] += dng

    def rev(b, j):
        return b * nb + (nb - 1 - j)

    vec = pl.BlockSpec((1, GW), lambda b, j: (0, 0))
    return pl.pallas_call(
        body, name="hgrn_bwd", grid=(B, nb),
        in_specs=[pl.BlockSpec((blk, 4 * GW), lambda b, j: (rev(b, j), C_A // (4 * GW))),
                  pl.BlockSpec((None, GW, GW), lambda b, j: (rev(b, j), 0, 0)),
                  pl.BlockSpec((blk, GW), lambda b, j: (rev(b, j), O_A // GW)), vec, vec],
        out_specs=[pl.BlockSpec((blk, 4 * GW), lambda b, j: (rev(b, j), 0)), vec, vec],
        out_shape=[_sds((T, 4 * GW), BF16), _sds((1, GW), F32), _sds((1, GW), F32)],
        scratch_shapes=[pltpu.VMEM((GW, GW), F32)],
        compiler_params=_cparams(VMEM_BIG))(proj, states, dcat, lb, ng)


ATT_TQ = 256


ATT_BANDS = 8


def _attn_block(q, k, v, cum, qpos0, scale, use_cum, n_free):
    s = _mm_nt(q, k) * scale
    if use_cum:
        s = s - cum
    band = s[:, n_free:]
    visible = _iota(band.shape, 1) <= (qpos0 - n_free) + _iota(band.shape, 0)
    band = jnp.where(visible, band, -jnp.inf)
    m = jnp.max(band, axis=-1, keepdims=True)
    if n_free:
        free = s[:, :n_free]
        m = jnp.maximum(m, jnp.max(free, axis=-1, keepdims=True))
    if not use_cum:
        m = lax.stop_gradient(m)
    e = jnp.exp(band - m)
    denom = jnp.sum(e, axis=-1, keepdims=True)
    o = _mm(e, v[n_free:])
    if n_free:
        e = jnp.exp(free - m)
        denom = denom + jnp.sum(e, axis=-1, keepdims=True)
        o = o + _mm(e, v[:n_free])
    return o * (1.0 / denom)


def _bands(S, tq):
    nq = S // tq
    nb = min(ATT_BANDS, nq)
    per = nq // nb
    return [(r * per, (r + 1) * per, (r + 1) * per * tq) for r in range(nb)]


def attn_fwd(qa, qo, ka, ko, va, vo, cum, scale, S):
    T = qa.shape[0]
    B = T // S
    tq = min(ATT_TQ, S)
    nq = S // tq
    use_cum = cum is not None

    def body(*refs):
        if use_cum:
            q_ref, k_ref, v_ref, c_ref, o_ref = refs
        else:
            (q_ref, k_ref, v_ref, o_ref), c_ref = refs, None
        h, i = pl.program_id(1), pl.program_id(2)
        for lo, hi, kw in _bands(S, tq):
            @pl.when((i >= lo) & (i < hi))
            def _():
                crow = c_ref[pl.ds(h, 1), 0:kw] if use_cum else None
                o = _attn_block(q_ref[...], k_ref[0:kw, :], v_ref[0:kw, :], crow, i * tq, scale, use_cum, lo * tq)
                o_ref[...] = o.astype(BF16)

    in_specs = [pl.BlockSpec((tq, HP), lambda b, h, i: (b * nq + i, qo + h)),
                pl.BlockSpec((S, HP), lambda b, h, i: (b, ko + h)),
                pl.BlockSpec((S, HP), lambda b, h, i: (b, vo + h))]
    args = [qa, ka, va]
    if use_cum:
        in_specs.append(pl.BlockSpec((None, 8, S), lambda b, h, i: (b, 0, 0)))
        args.append(cum)
    return pl.pallas_call(
        body, name="attn_fwd", grid=(B, HEADS, nq), in_specs=in_specs,
        out_specs=pl.BlockSpec((tq, HP), lambda b, h, i: (b * nq + i, h)),
        out_shape=_sds((T, HEADS * HP), BF16),
        compiler_params=_cparams(VMEM_BIG))(*args)


def attn_bwd(qa, qo, ka, ko, va, vo, cum, dcat, do_off, scale, S, out_dtype):
    T = qa.shape[0]
    B = T // S
    tq = min(ATT_TQ, S)
    nq = S // tq
    use_cum = cum is not None

    def body(*refs):
        if use_cum:
            q_ref, k_ref, v_ref, do_ref, c_ref, dq_ref, dk_ref, dv_ref, dc_ref, dk_acc, dv_acc = refs
        else:
            q_ref, k_ref, v_ref, do_ref, dq_ref, dk_ref, dv_ref, dk_acc, dv_acc = refs
        h, i = pl.program_id(1), pl.program_id(2)

        @pl.when(i == 0)
        def _():
            dk_acc[...] = jnp.zeros_like(dk_acc)
            dv_acc[...] = jnp.zeros_like(dv_acc)
            if use_cum:
                dc_ref[...] = jnp.zeros_like(dc_ref)

        for lo, hi, kw in _bands(S, tq):
            @pl.when((i >= lo) & (i < hi))
            def _():
                crow = c_ref[pl.ds(h, 1), 0:kw] if use_cum else jnp.zeros((1, kw), F32)
                f = functools.partial(_attn_block, qpos0=i * tq, scale=scale, use_cum=use_cum, n_free=lo * tq)
                _, vjp = jax.vjp(f, q_ref[...], k_ref[0:kw, :], v_ref[0:kw, :], crow)
                dq, dk, dv, dc = vjp(do_ref[...])
                dq_ref[...] = dq.astype(out_dtype)
                dk_acc[0:kw, :] += dk
                dv_acc[0:kw, :] += dv
                if use_cum:
                    dc_ref[:, 0:kw] += dc

        @pl.when(i == nq - 1)
        def _():
            dk_ref[...] = dk_acc[...].astype(out_dtype)
            dv_ref[...] = dv_acc[...].astype(out_dtype)

    qspec = pl.BlockSpec((tq, HP), lambda b, h, i: (b * nq + i, qo + h))
    in_specs = [qspec, pl.BlockSpec((S, HP), lambda b, h, i: (b, ko + h)),
                pl.BlockSpec((S, HP), lambda b, h, i: (b, vo + h)),
                pl.BlockSpec((tq, HP), lambda b, h, i: (b * nq + i, do_off + h))]
    args = [qa, ka, va, dcat]
    kv_out = pl.BlockSpec((S, HP), lambda b, h, i: (b, h))
    out_specs = [pl.BlockSpec((tq, HP), lambda b, h, i: (b * nq + i, h)), kv_out, kv_out]
    out_shape = [_sds((T, HEADS * HP), out_dtype)] * 3
    if use_cum:
        in_specs.append(pl.BlockSpec((None, 8, S), lambda b, h, i: (b, 0, 0)))
        args.append(cum)
        out_specs.append(pl.BlockSpec((None, 1, S), lambda b, h, i: (b * HEADS + h, 0, 0)))
        out_shape.append(_sds((B * HEADS, 1, S), F32))
    return pl.pallas_call(
        body, name="attn_bwd", grid=(B, HEADS, nq), in_specs=in_specs, out_specs=out_specs, out_shape=out_shape,
        scratch_shapes=[pltpu.VMEM((S, HP), F32), pltpu.VMEM((S, HP), F32)],
        compiler_params=_cparams(VMEM_BIG))(*args)


def _tri(n, upper):
    r, c = _iota((n, n), 0), _iota((n, n), 1)
    return ((r <= c) if upper else (r >= c)).astype(F32)


def fox_gate_fwd(proj, bcol, S):
    T = proj.shape[0]
    B = T // S
    ts = _tile(S, 512)
    nt = S // ts

    def body(p_ref, b_ref, o_ref, carry):
        @pl.when(pl.program_id(1) == 0)
        def _():
            carry[...] = jnp.zeros_like(carry)
        cf = jnp.transpose(p_ref[...])[0:8, :]
        lf = jax.nn.log_sigmoid(cf + b_ref[...])
        cum = _mm_hi(lf, _tri(ts, True)) + carry[...]
        o_ref[...] = cum
        carry[...] += jnp.sum(lf, axis=1, keepdims=True)

    return pl.pallas_call(
        body, name="fox_gate_fwd", grid=(B, nt),
        in_specs=[pl.BlockSpec((ts, HP), lambda b, j: (b * nt + j, C_CF // HP)), pl.BlockSpec((8, 1), lambda b, j: (0, 0))],
        out_specs=pl.BlockSpec((None, 8, ts), lambda b, j: (b, 0, j)), out_shape=_sds((B, 8, S), F32),
        scratch_shapes=[pltpu.VMEM((8, 1), F32)],
        compiler_params=_cparams(VMEM_BIG))(proj, bcol)


def fox_gate_bwd(proj, bcol, dcum, S):
    T = proj.shape[0]
    B = T // S
    ts = _tile(S, 512)
    nt = S // ts

    def body(p_ref, b_ref, dc_ref, dp_ref, db_ref, carry):
        b, j = pl.program_id(0), pl.program_id(1)

        @pl.when(j == 0)
        def _():
            carry[...] = jnp.zeros_like(carry)

        @pl.when((b == 0) & (j == 0))
        def _():
            db_ref[...] = jnp.zeros_like(db_ref)

        cf = jnp.transpose(p_ref[...])[0:8, :]
        dc = dc_ref[...]
        dlf = _mm_hi(dc, _tri(ts, False)) + carry[...]
        carry[...] += jnp.sum(dc, axis=1, keepdims=True)
        dcf = dlf * jax.nn.sigmoid(-(cf + b_ref[...]))
        db_ref[...] += jnp.sum(dcf, axis=1, keepdims=True)
        full = jnp.concatenate([dcf, jnp.zeros((HP - 8, ts), F32)], axis=0)
        dp_ref[...] = jnp.transpose(full).astype(BF16)

    def rev(b, j):
        return nt - 1 - j

    return pl.pallas_call(
        body, name="fox_gate_bwd", grid=(B, nt),
        in_specs=[pl.BlockSpec((ts, HP), lambda b, j: (b * nt + rev(b, j), C_CF // HP)),
                  pl.BlockSpec((8, 1), lambda b, j: (0, 0)),
                  pl.BlockSpec((None, 8, ts), lambda b, j: (b, 0, rev(b, j)))],
        out_specs=[pl.BlockSpec((ts, HP), lambda b, j: (b * nt + rev(b, j), 0)), pl.BlockSpec((8, 1), lambda b, j: (0, 0))],
        out_shape=[_sds((T, HP), BF16), _sds((8, 1), F32)],
        scratch_shapes=[pltpu.VMEM((8, 1), F32)],
        compiler_params=_cparams(VMEM_BIG))(proj, bcol, dcum)


def _mla_pre(blk, gq, gkv, wq, wkv, place, cos_q, sin_q, cs_k):
    nq = _rms(blk[:, 0:B_Q_LORA], gq)
    nkv = _rms(blk[:, B_Q_LORA:B_Q_LORA + B_KV_LORA], gkv)
    qq = _mm(nq, wq)
    q = qq[:, 0:HEADS * HP] * cos_q + qq[:, HEADS * HP:] * sin_q
    kv = _mm(nkv, wkv)
    k = kv[:, 0:HEADS * HP] + _mm(blk[:, B_Q_LORA + B_KV_LORA:] * cs_k, place)
    return q, k, kv[:, HEADS * HP:]


def mla_pre_fwd(proj, gq, gkv, wq, wkv, place, cos_q, sin_q, cs_k, S):
    T = proj.shape[0]
    tm = _tile(S, 512)
    tpb = S // tm
    W = HEADS * HP

    def body(p_ref, gq_ref, gkv_ref, wq_ref, wkv_ref, pl_ref, cq_ref, sq_ref, ck_ref, q_ref, k_ref, v_ref):
        q, k, v = _mla_pre(p_ref[...], gq_ref[...], gkv_ref[...], wq_ref[...], wkv_ref[...], pl_ref[...],
                           cq_ref[...], sq_ref[...], ck_ref[...])
        q_ref[...] = q
        k_ref[...] = k
        v_ref[...] = v

    def full(a):
        return pl.BlockSpec(a.shape, lambda i: (0,) * a.ndim)

    tok = pl.BlockSpec((tm, W), lambda i: (i, 0))
    return pl.pallas_call(
        body, name="mla_pre_fwd", grid=(T // tm,),
        in_specs=[pl.BlockSpec((tm, W), lambda i: (i, C_B // W)), full(gq), full(gkv), full(wq), full(wkv), full(place),
                  pl.BlockSpec((tm, W), lambda i: (i % tpb, 0)), pl.BlockSpec((tm, W), lambda i: (i % tpb, 0)),
                  pl.BlockSpec((tm, HP), lambda i: (i % tpb, 0))],
        out_specs=[tok] * 3, out_shape=[_sds((T, W), F32)] * 3,
        compiler_params=_cparams(VMEM_BIG))(proj, gq, gkv, wq, wkv, place, cos_q, sin_q, cs_k)


def mla_pre_bwd(proj, gq, gkv, wq, wkv, place, cos_q, sin_q, cs_k, dq, dk, dv, S):
    T = proj.shape[0]
    tm = _tile(S, 512)
    tpb = S // tm
    W = HEADS * HP

    def body(p_ref, gq_ref, gkv_ref, wq_ref, wkv_ref, pl_ref, cq_ref, sq_ref, ck_ref, dq_ref, dk_ref, dv_ref,
             dp_ref, dgq_ref, dgkv_ref, dwq_ref, dwkv_ref):
        @pl.when(pl.program_id(0) == 0)
        def _():
            for r in (dgq_ref, dgkv_ref, dwq_ref, dwkv_ref):
                r[...] = jnp.zeros_like(r)

        f = functools.partial(_mla_pre, place=pl_ref[...], cos_q=cq_ref[...], sin_q=sq_ref[...], cs_k=ck_ref[...])
        _, vjp = jax.vjp(f, p_ref[...], gq_ref[...], gkv_ref[...], wq_ref[...], wkv_ref[...])
        dp, dgq, dgkv, dwq, dwkv = vjp((dq_ref[...], dk_ref[...], dv_ref[...]))
        dp_ref[...] = dp.astype(BF16)
        dgq_ref[...] += dgq
        dgkv_ref[...] += dgkv
        dwq_ref[...] += dwq
        dwkv_ref[...] += dwkv

    def full(a):
        return pl.BlockSpec(a.shape, lambda i: (0,) * a.ndim)

    tok = pl.BlockSpec((tm, W), lambda i: (i, 0))
    return pl.pallas_call(
        body, name="mla_pre_bwd", grid=(T // tm,),
        in_specs=[pl.BlockSpec((tm, W), lambda i: (i, C_B // W)), full(gq), full(gkv), full(wq), full(wkv), full(place),
                  pl.BlockSpec((tm, W), lambda i: (i % tpb, 0)), pl.BlockSpec((tm, W), lambda i: (i % tpb, 0)),
                  pl.BlockSpec((tm, HP), lambda i: (i % tpb, 0)), tok, tok, tok],
        out_specs=[tok, full(gq), full(gkv), full(wq), full(wkv)],
        out_shape=[_sds((T, W), BF16), _sds(gq.shape, F32), _sds(gkv.shape, F32), _sds(wq.shape, F32), _sds(wkv.shape, F32)],
        compiler_params=_cparams(VMEM_BIG))(proj, gq, gkv, wq, wkv, place, cos_q, sin_q, cs_k, dq, dk, dv)


def _gmlp_block(blk, lg, lb, ws, bs):
    u = jax.nn.gelu(blk[:, 0:GW])
    v = jax.nn.gelu(blk[:, GW:2 * GW])
    mu = jnp.mean(v, axis=-1, keepdims=True)
    var = jnp.mean(jnp.square(v - mu), axis=-1, keepdims=True)
    vn = (v - mu) * lax.rsqrt(var + LN_EPS) * lg + lb
    causal = _iota((D_CHUNK, D_CHUNK), 1) <= _iota((D_CHUNK, D_CHUNK), 0)
    group = _iota((1, GW), 1) // HD
    mixed = jnp.zeros((D_CHUNK, GW), F32)
    for g in range(HEADS):
        part = _mm(jnp.where(causal, ws[g], 0.0), vn) + bs[g]
        mixed = mixed + jnp.where(group == g, part, 0.0)
    return u * mixed


def gmlp_fwd(proj, lg, lb, ws, bs):
    T = proj.shape[0]

    def body(p_ref, lg_ref, lb_ref, ws_ref, bs_ref, o_ref):
        o_ref[...] = _gmlp_block(p_ref[...], lg_ref[...], lb_ref[...], ws_ref[...], bs_ref[...]).astype(BF16)

    def full(a):
        return pl.BlockSpec(a.shape, lambda i: (0,) * a.ndim)

    return pl.pallas_call(
        body, name="gmlp_fwd", grid=(T // D_CHUNK,),
        in_specs=[pl.BlockSpec((D_CHUNK, 2 * GW), lambda i: (i, C_D // (2 * GW))), full(lg), full(lb), full(ws), full(bs)],
        out_specs=pl.BlockSpec((D_CHUNK, GW), lambda i: (i, 0)), out_shape=_sds((T, GW), BF16),
        compiler_params=_cparams(VMEM_BIG))(proj, lg, lb, ws, bs)


def gmlp_bwd(proj, lg, lb, ws, bs, dcat):
    T = proj.shape[0]

    def body(p_ref, lg_ref, lb_ref, ws_ref, bs_ref, do_ref, dp_ref, dlg_ref, dlb_ref, dws_ref, dbs_ref):
        @pl.when(pl.program_id(0) == 0)
        def _():
            for r in (dlg_ref, dlb_ref, dws_ref, dbs_ref):
                r[...] = jnp.zeros_like(r)

        _, vjp = jax.vjp(_gmlp_block, p_ref[...], lg_ref[...], lb_ref[...], ws_ref[...], bs_ref[...])
        dp, dlg, dlb, dws, dbs = vjp(do_ref[...])
        dp_ref[...] = dp.astype(BF16)
        dlg_ref[...] += dlg
        dlb_ref[...] += dlb
        dws_ref[...] += dws
        dbs_ref[...] += dbs

    def full(a):
        return pl.BlockSpec(a.shape, lambda i: (0,) * a.ndim)

    return pl.pallas_call(
        body, name="gmlp_bwd", grid=(T // D_CHUNK,),
        in_specs=[pl.BlockSpec((D_CHUNK, 2 * GW), lambda i: (i, C_D // (2 * GW))), full(lg), full(lb), full(ws), full(bs),
                  pl.BlockSpec((D_CHUNK, GW), lambda i: (i, O_D // GW))],
        out_specs=[pl.BlockSpec((D_CHUNK, 2 * GW), lambda i: (i, 0)), full(lg), full(lb), full(ws), full(bs)],
        out_shape=[_sds((T, 2 * GW), BF16), _sds(lg.shape, F32), _sds(lb.shape, F32), _sds(ws.shape, F32), _sds(bs.shape, F32)],
        compiler_params=_cparams(VMEM_BIG))(proj, lg, lb, ws, bs, dcat)


def _lb_all(logits):
    m = jnp.max(logits, axis=0, keepdims=True)
    e = jnp.exp(logits - m)
    sm = e / jnp.sum(e, axis=0, keepdims=True)
    return jnp.concatenate([sm[0:1] - sm[0:1], (sm[0:1] + sm[1:2]) - sm[0:1]], axis=0)


def lb_fwd(logits):
    def body(l_ref, o_ref):
        o_ref[...] = _lb_all(l_ref[...])

    return pl.pallas_call(body, name="lb_fwd", out_shape=_sds(logits.shape, F32))(logits)


def lb_bwd(logits, dlb):
    def body(l_ref, d_ref, o_ref):
        _, vjp = jax.vjp(_lb_all, l_ref[...])
        o_ref[...] = vjp(d_ref[...])[0]

    return pl.pallas_call(body, name="lb_bwd", out_shape=_sds(logits.shape, F32))(logits, dlb)


def ada_grad(c_all, dmod_cols):
    N = dmod_cols.shape[1]
    tn = _tile(N, 1152)

    def body(c_ref, d_ref, o_ref):
        h = jax.nn.silu(c_ref[...]).astype(BF16)
        o_ref[...] = lax.dot_general(h, d_ref[...].astype(BF16), (((0,), (0,)), ((), ())), preferred_element_type=F32)

    nb = c_all.shape[0]
    return pl.pallas_call(
        body, name="ada_grad", grid=(N // tn,),
        in_specs=[pl.BlockSpec((nb, D), lambda j: (0, 0)), pl.BlockSpec((nb, tn), lambda j: (0, j))],
        out_specs=pl.BlockSpec((D, tn), lambda j: (0, j)), out_shape=_sds((D, N), F32),
        compiler_params=_cparams(VMEM_BIG))(c_all, dmod_cols)


def sum_slots(a, n, name):
    _, R, C = a.shape
    tr = _row_tile(R, C, n)

    def body(a_ref, o_ref):
        acc = a_ref[0]
        for k in range(1, n):
            acc = acc + a_ref[k]
        o_ref[...] = acc

    return pl.pallas_call(
        body, name=name, grid=(R // tr,),
        in_specs=[pl.BlockSpec((n, tr, C), lambda i: (0, i, 0))],
        out_specs=pl.BlockSpec((tr, C), lambda i: (i, 0)), out_shape=_sds((R, C), F32),
        compiler_params=_cparams(VMEM_BIG))(a)


def add2(a, b, name):
    shp = a.shape
    C = shp[-1]
    a2, b2 = a.reshape(-1, C), b.reshape(-1, C)
    R = a2.shape[0]
    tr = _row_tile(R, C)

    def body(a_ref, b_ref, o_ref):
        o_ref[...] = a_ref[...] + b_ref[...]

    spec = pl.BlockSpec((tr, C), lambda i: (i, 0))
    return pl.pallas_call(body, name=name, grid=(R // tr,), in_specs=[spec, spec], out_specs=spec,
                          out_shape=_sds((R, C), F32), compiler_params=_cparams(VMEM_BIG))(a2, b2).reshape(shp)


def _row_tile(R, C=D, n=1, mult=8, elems=1 << 18):
    limit = max(mult, elems // (C * n))
    for t in range(limit - limit % mult, mult - 1, -mult):
        if R % t == 0:
            return t
    return R


def adamw(w, g, m, v, name):
    R, C = w.shape
    tr = _row_tile(R, C, elems=1 << 19)
    c1 = 1.0 - ADAM_B1 ** ADAM_STEP
    c2 = 1.0 - ADAM_B2 ** ADAM_STEP

    def body(w_ref, g_ref, m_ref, v_ref, d_ref, nm_ref, nv_ref):
        g_ = g_ref[...]
        nm = ADAM_B1 * m_ref[...] + (1.0 - ADAM_B1) * g_
        nv = ADAM_B2 * v_ref[...] + (1.0 - ADAM_B2) * jnp.square(g_)
        d_ref[...] = -ADAM_LR * ((nm / c1) / (jnp.sqrt(nv / c2) + ADAM_EPS) + ADAM_WD * w_ref[...])
        nm_ref[...] = nm
        nv_ref[...] = nv

    spec = pl.BlockSpec((tr, C), lambda i: (i, 0))
    return pl.pallas_call(body, name=name, grid=(R // tr,), in_specs=[spec] * 4, out_specs=[spec] * 3,
                          out_shape=[_sds((R, C), F32)] * 3, compiler_params=_cparams(VMEM_BIG))(w, g, m, v)


def _rot_cols(w):
    return jnp.concatenate([-w[:, 16:32], w[:, 0:16]], axis=1)


def _fold_rot(d):
    return jnp.concatenate([d[:, 16:32], -d[:, 0:16]], axis=1)


def _pad_heads(w, off, axis):
    parts = []
    for h in range(HEADS):
        piece = lax.slice_in_dim(w, off + HD * h, off + HD * (h + 1), axis=axis)
        parts += [piece, jnp.zeros_like(piece)]
    return parts


def _unpad_heads(d, off, axis):
    return [lax.slice_in_dim(d, off + HP * h, off + HP * h + HD, axis=axis) for h in range(HEADS)]


def mix_in_ext(w):
    z = lambda n: jnp.zeros((w.shape[0], n), w.dtype)
    kr = w[:, 1408:1440]
    cols = [w[:, 0:1408], kr, _rot_cols(kr), z(64)]
    cols += _pad_heads(w, 1440, 1) + _pad_heads(w, 1696, 1) + _pad_heads(w, 1952, 1)
    cols += [w[:, 2212:2724], w[:, 2208:2212], z(NP - C_CF - HEADS)]
    return jnp.concatenate(cols, axis=1)


def mix_in_unext(d):
    kr = d[:, 1408:1440] + _fold_rot(d[:, 1440:1472])
    cols = [d[:, 0:1408], kr] + _unpad_heads(d, C_CQ, 1) + _unpad_heads(d, C_CK, 1) + _unpad_heads(d, C_CV, 1)
    cols += [d[:, C_CF:C_CF + HEADS], d[:, C_D:C_D + 2 * GW]]
    return jnp.concatenate(cols, axis=1)


def mix_out_ext(w):
    return jnp.concatenate([w[0:GW]] + _pad_heads(w, GW, 0) + _pad_heads(w, 2 * GW, 0) + [w[3 * GW:4 * GW]], axis=0)


def mix_out_unext(d):
    return jnp.concatenate([d[0:GW]] + _unpad_heads(d, O_B, 0) + _unpad_heads(d, O_C, 0) + [d[O_D:O_D + GW]], axis=0)


def uq_ext(w):
    z = lambda n: jnp.zeros((w.shape[0], n), w.dtype)
    a, b = [], []
    for h in range(HEADS):
        o = (B_NOPE + B_ROPE) * h
        a += [w[:, o:o + B_NOPE + B_ROPE], z(32)]
        b += [z(B_NOPE), _rot_cols(w[:, o + B_NOPE:o + B_NOPE + B_ROPE]), z(32)]
    return jnp.concatenate(a + b, axis=1)


def uq_unext(d):
    cols = []
    for h in range(HEADS):
        o = HP * h
        cols += [d[:, o:o + B_NOPE], d[:, o + B_NOPE:o + B_NOPE + B_ROPE]
                 + _fold_rot(d[:, HEADS * HP + o + B_NOPE:HEADS * HP + o + B_NOPE + B_ROPE])]
    return jnp.concatenate(cols, axis=1)


def ukv_ext(w):
    z = jnp.zeros((w.shape[0], HD), w.dtype)
    k, v = [], []
    for h in range(HEADS):
        k += [w[:, 2 * HD * h:2 * HD * h + HD], z]
        v += [w[:, 2 * HD * h + HD:2 * HD * (h + 1)], z]
    return jnp.concatenate(k + v, axis=1)


def ukv_unext(d):
    cols = []
    for h in range(HEADS):
        cols += [d[:, HP * h:HP * h + HD], d[:, HEADS * HP + HP * h:HEADS * HP + HP * h + HD]]
    return jnp.concatenate(cols, axis=1)


def rope_tables(S):
    half = B_ROPE // 2
    inv_freq = ROPE_THETA ** (-jnp.arange(half, dtype=F32) / half)
    ang = jnp.arange(S).astype(F32)[:, None] * inv_freq[None, :]
    cos = jnp.tile(jnp.cos(ang), (1, 2))
    sin = jnp.tile(jnp.sin(ang), (1, 2))
    one, zero = jnp.ones((S, B_NOPE), F32), jnp.zeros((S, B_NOPE), F32)
    z32 = jnp.zeros((S, 32), F32)
    cos_q = jnp.tile(jnp.concatenate([one, cos, z32], axis=1), (1, HEADS))
    sin_q = jnp.tile(jnp.concatenate([zero, sin, z32], axis=1), (1, HEADS))
    cs_k = jnp.concatenate([cos, sin, zero], axis=1)
    place = np.zeros((HP, HEADS * HP), np.float32)
    for h in range(HEADS):
        for j in range(B_ROPE):
            place[j, h * HP + B_NOPE + j] = 1.0
            place[B_ROPE + j, h * HP + B_NOPE + j] = 1.0
    return cos_q, sin_q, cs_k, jnp.asarray(place, BF16)


def layer_fwd(x, mod, get, tabs, S):
    cos_q, sin_q, cs_k, place = tabs
    p = dict(get("ffn1", x))
    l = p["wl"]
    zg1, zu1, act1 = ffn_in_fwd(x, mod, p["ffn1_in"], l, 0, 1, S)
    y1, x1 = out_ln_fwd(act1, p["ffn1_out"], x, mod, p["ln_g"][0:1], p["ln_b"][0:1], 2, 0.5, S, l)
    p.update(get("mix", x1))
    proj = mix_in_fwd(x1, mod, p["mix_in"], 3, 4, S)
    o_a, states = hgrn_fwd(proj, p["lb"], p["ng"], S)
    q_b, k_b, v_b = mla_pre_fwd(proj, p["gq"], p["gkv"], p["wq"], p["wkv"], place, cos_q, sin_q, cs_k, S)
    o_b = attn_fwd(q_b, 0, k_b, 0, v_b, 0, None, (B_NOPE + B_ROPE) ** -0.5, S)
    cum = fox_gate_fwd(proj, p["bcol"], S)
    o_c = attn_fwd(proj, C_CQ // HP, proj, C_CK // HP, proj, C_CV // HP, cum, HD ** -0.5, S)
    o_d = gmlp_fwd(proj, p["g_lg"], p["g_lb"], p["ws"], p["bs"])
    cat = jnp.concatenate([o_a, o_b, o_c, o_d], axis=1)
    y2, x2 = out_ln_fwd(cat, p["mix_out"], x1, mod, p["ln_g"][1:2], p["ln_b"][1:2], 5, 1.0, S)
    p.update(get("ffn2", x2))
    zg3, zu3, act3 = ffn_in_fwd(x2, mod, p["ffn2_in"], l, 6, 7, S)
    y3, x3 = out_ln_fwd(act3, p["ffn2_out"], x2, mod, p["ln_g"][2:3], p["ln_b"][2:3], 8, 0.5, S, l)
    saved = dict(x=x, zg1=zg1, zu1=zu1, act1=act1, y1=y1, x1=x1, proj=proj, states=states, q_b=q_b, k_b=k_b, v_b=v_b,
                 cum=cum, cat=cat, y2=y2, x2=x2, zg3=zg3, zu3=zu3, act3=act3, y3=y3, p=p)
    return x3, saved


def _ffn_bwd(dxn, x_in, y, zg, zu, act, mod, w_in, w_out, l, lg, lb, idx, S, emit):
    sh, sc, gate = idx
    dres, dy, dgate, dlg, dlb = ln_res_bwd(dxn, x_in, y, mod, lg, lb, gate, 0.5, S)
    dzg, dzu = swiglu_bwd(dy, w_out, l, zg, zu, S)
    dw_out = tn_mm(act, dy, FF // 2).reshape(N_CHIPS, FF // N_CHIPS, D)
    dw_in = tn_mm_mod_shards(x_in, mod, dzg, dzu, sh, sc, S)
    mod = mod + emit(dw_in, dw_out)
    dx, dsh, dsc = nt_mod_bwd([dzg, dzu], w_in, [0, 2], x_in, mod, dres, sc, S, FF // 2, l)
    return dx, dw_in, dw_out, dlg, dlb, {sh: dsh, sc: dsc, gate: dgate}, mod


def layer_bwd(dx3, mod, sv, tabs, S, emit):
    cos_q, sin_q, cs_k, place = tabs
    p = sv["p"]
    l = p["wl"]
    g = {}
    dm = {}

    def emit_ffn(part):
        def f(dw_in, dw_out):
            g[part + "_in"], g[part + "_out"] = dw_in, dw_out
            return emit(part, g)
        return f

    dx2, _, _, dlg2, dlb2, d, mod = _ffn_bwd(
        dx3, sv["x2"], sv["y3"], sv["zg3"], sv["zu3"], sv["act3"], mod, p["ffn2_in"], p["ffn2_out"], l,
        p["ln_g"][2:3], p["ln_b"][2:3], (6, 7, 8), S, emit_ffn("ffn2"))
    dm.update(d)
    dres, dy2, dm[5], dlg1, dlb1 = ln_res_bwd(dx2, sv["x1"], sv["y2"], mod, p["ln_g"][1:2], p["ln_b"][1:2], 5, 1.0, S)
    dcat = nt_plain(dy2, p["mix_out"])
    g["mix_out"] = tn_mm(sv["cat"], dy2, 768)
    proj = sv["proj"]
    d_a, g["lb"], g["ng"] = hgrn_bwd(proj, sv["states"], dcat, p["lb"], p["ng"], S)
    dq_c, dk_c, dv_c, dcum = attn_bwd(proj, C_CQ // HP, proj, C_CK // HP, proj, C_CV // HP, sv["cum"], dcat,
                                      O_C // HP, HD ** -0.5, S, BF16)
    B = proj.shape[0] // S
    dcum = jnp.concatenate([dcum.reshape(B, HEADS, S), jnp.zeros((B, 8 - HEADS, S), F32)], axis=1)
    d_cf, g["bcol"] = fox_gate_bwd(proj, p["bcol"], dcum, S)
    dq_b, dk_b, dv_b = attn_bwd(sv["q_b"], 0, sv["k_b"], 0, sv["v_b"], 0, None, dcat, O_B // HP,
                                (B_NOPE + B_ROPE) ** -0.5, S, F32)
    d_b, g["gq"], g["gkv"], g["wq"], g["wkv"] = mla_pre_bwd(
        proj, p["gq"], p["gkv"], p["wq"], p["wkv"], place, cos_q, sin_q, cs_k, dq_b, dk_b, dv_b, S)
    d_d, g["g_lg"], g["g_lb"], g["ws"], g["bs"] = gmlp_bwd(proj, p["g_lg"], p["g_lb"], p["ws"], p["bs"], dcat)
    dproj = jnp.concatenate([d_a, d_b, dq_c, dk_c, dv_c, d_d, d_cf, jnp.zeros_like(d_cf)], axis=1)
    g["mix_in"] = tn_mm_mod(sv["x1"], mod, dproj, 3, 4, S, NP_TILE)
    mod = mod + emit("mix", g)
    dx1, dm[3], dm[4] = nt_mod_bwd([dproj], p["mix_in"], [0], sv["x1"], mod, dres, 4, S, NP_TILE)
    last = []

    def emit_last(dw_in, dw_out):
        last.append(emit_ffn("ffn1")(dw_in, dw_out))
        return last[0]

    dx0, _, _, dlg0, dlb0, d, mod = _ffn_bwd(
        dx1, sv["x"], sv["y1"], sv["zg1"], sv["zu1"], sv["act1"], mod, p["ffn1_in"], p["ffn1_out"], l,
        p["ln_g"][0:1], p["ln_b"][0:1], (0, 1, 2), S, emit_last)
    dm.update(d)
    g["ln_g"] = jnp.concatenate([dlg0, dlg1, dlg2], axis=0)
    g["ln_b"] = jnp.concatenate([dlb0, dlb1, dlb2], axis=0)
    dmod = jnp.concatenate([dm[i] for i in range(N_MOD)], axis=1)
    return dx0, dmod, g, last[0]


def local_step(x, c8, tgt, get, lb_logits, S, emit=None):
    B = x.shape[0] // S
    tabs = rope_tables(S)
    lb_all = lb_fwd(lb_logits)
    mods, saved = [], []
    h = x
    for l in range(DEPTH):
        pa = get(l, "ada", h)
        mod = mod_fwd(c8, pa["ada_w"], pa["wl"], pa["ada_b"])[0:B].reshape(B, N_MOD, D)

        def get_l(part, after, l=l):
            p = dict(get(l, part, after))
            if part == "mix":
                p["lb"] = lb_all[l:l + 1]
            return p

        h, sv = layer_fwd(h, mod, get_l, tabs, S)
        mods.append(mod)
        saved.append(sv)
    loss_tile, dh = loss_head(h, tgt)
    grads, dmods, dlb = [None] * DEPTH, [None] * DEPTH, [None] * DEPTH
    tie = jnp.zeros((), F32)
    for l in reversed(range(DEPTH)):
        emit_l = (lambda part, g: jnp.zeros((), F32)) if emit is None else functools.partial(emit, l)
        dh, dmods[l], grads[l], tie = layer_bwd(dh, mods[l] + tie, saved[l], tabs, S, emit_l)
        dlb[l] = grads[l].pop("lb")
    d_logits = lb_bwd(lb_logits, jnp.concatenate(dlb, axis=0))
    return loss_tile, dh, dmods, grads, d_logits


ANY = pl.BlockSpec(memory_space=pl.ANY)


def _place():
    x, y, c = lax.axis_index("x"), lax.axis_index("y"), lax.axis_index("c")
    chips = [(1 - x, y), (x, 1 - y), (1 - x, 1 - y)]
    return x, y, c, chips


def _rcopy(src, dst, sems, k, to):
    send_sems, recv_sems = sems
    return pltpu.make_async_remote_copy(src_ref=src, dst_ref=dst, send_sem=send_sems.at[k], recv_sem=recv_sems.at[k],
                                        device_id=to, device_id_type=MESH)


def _dma_sems(n_remote, n_local):
    return [pltpu.SemaphoreType.DMA((n_remote,)), pltpu.SemaphoreType.DMA((n_remote,)), pltpu.SemaphoreType.DMA((n_local,))]


def own_slot(src, chip):
    L = src.shape[0]
    return lax.dynamic_update_slice(lax.empty((L, N_CHIPS) + src.shape[1:], src.dtype), src[:, None], (0, chip, 0, 0))


def ag_shards(arrs, lands):
    n = len(arrs)
    rh = [a.shape[1] // 2 for a in arrs]

    def body(*refs):
        srcs, outs, token = refs[:n], refs[2 * n:3 * n], refs[3 * n]
        send_sems, recv_sems = refs[3 * n + 1:]
        x, y, c, chips = _place()
        sems = (send_sems, recv_sems)
        me = 2 * x + y
        sibling = (x, y, 1 - c)
        token[...] = jnp.zeros_like(token)

        def part(i, k, hc):
            return outs[i].at[:, k, pl.ds(hc * rh[i], rh[i]), :]

        started = []
        for j, (px, py) in enumerate(chips):
            for i in range(n):
                cp = _rcopy(srcs[i].at[:, pl.ds(c * rh[i], rh[i]), :], part(i, me, c), sems, 6 * i + j, (px, py, c))
                cp.start()
                started.append(cp)
        for j, (px, py) in enumerate(chips):
            k = 2 * px + py
            for i in range(n):
                _rcopy(part(i, k, c), part(i, k, c), sems, 6 * i + j, (px, py, c)).wait_recv()
                cp = _rcopy(part(i, k, c), part(i, k, c), sems, 6 * i + 3 + j, sibling)
                cp.start()
                started.append(cp)
        for j, (px, py) in enumerate(chips):
            k = 2 * px + py
            for i in range(n):
                _rcopy(part(i, k, 1 - c), part(i, k, 1 - c), sems, 6 * i + 3 + j, sibling).wait_recv()
        for cp in started:
            cp.wait_send()

    outs = pl.pallas_call(
        body, name="ag_shards", out_shape=[_sds(a.shape, a.dtype) for a in lands] + [_sds((8, 128), F32)],
        in_specs=[ANY] * (2 * n), out_specs=[ANY] * n + [pl.BlockSpec(memory_space=pltpu.VMEM)],
        input_output_aliases={n + i: i for i in range(n)}, scratch_shapes=_dma_sems(6 * n, 1)[:2])(*arrs, *lands)
    return list(outs[:n]), outs[n]


HBM_SPEC = pl.BlockSpec(memory_space=pltpu.HBM)
SEM_SPEC = pl.BlockSpec(memory_space=pltpu.SEMAPHORE)
DATAFLOW = pltpu.SideEffectType.DATAFLOW_SIDE_EFFECTING


def _after(x, dep):
    return lax.optimization_barrier((x, dep))[0]


def _split_start(srcs, lands, copies, n_copies, dep, name):
    n, m = len(srcs), len(lands)

    def body(*refs):
        ins = refs[:n + m]
        send_sems, recv_sems = refs[n + m + 1], refs[n + m + 2]
        token = refs[-1]
        for k, (src, dst, to) in enumerate(copies(ins[:n], ins[n:], _place())):
            pltpu.make_async_remote_copy(src_ref=src, dst_ref=dst, send_sem=send_sems.at[k], recv_sem=recv_sems.at[k],
                                         device_id=to, device_id_type=MESH).start()
        token[...] = jnp.zeros_like(token)

    arrs = list(srcs) + list(lands)
    outs = pl.pallas_call(
        body, name=name,
        out_shape=(pltpu.SemaphoreType.DMA((n_copies,)), pltpu.SemaphoreType.DMA((n_copies,)),
                   *[pltpu.HBM(a.shape, a.dtype) for a in arrs], _sds((8, 128), F32)),
        in_specs=[HBM_SPEC] * (n + m) + [ANY],
        out_specs=(SEM_SPEC, SEM_SPEC, *[HBM_SPEC] * (n + m), pl.BlockSpec(memory_space=pltpu.VMEM)),
        input_output_aliases={i: 2 + i for i in range(n + m)},
        compiler_params=pltpu.CompilerParams(has_side_effects=DATAFLOW),
    )(*[pltpu.with_memory_space_constraint(a, pltpu.HBM) for a in arrs], dep)
    return outs[0], outs[1], list(outs[2:2 + n]), list(outs[2 + n:2 + n + m]), outs[-1]


def _split_wait(handle, arrivals, after, name):
    send_sems, recv_sems, srcs, lands, _ = handle
    n, m = len(srcs), len(lands)

    def body(*refs):
        ins = refs[:n + m]
        send_sems, recv_sems = refs[n + m], refs[n + m + 1]
        x, y, c, chips = place = _place()
        for k, (src, dst) in enumerate(arrivals(ins[:n], ins[n:], place)):
            cp = pltpu.make_async_remote_copy(src_ref=src, dst_ref=dst, send_sem=send_sems.at[k], recv_sem=recv_sems.at[k],
                                              device_id=(x, y, 1 - c), device_id_type=MESH)
            cp.wait_send()
            cp.wait_recv()

    arrs = list(srcs) + list(lands)
    outs = pl.pallas_call(
        body, name=name, out_shape=[pltpu.HBM(a.shape, a.dtype) for a in arrs],
        in_specs=[HBM_SPEC] * (n + m) + [SEM_SPEC, SEM_SPEC, ANY], out_specs=[HBM_SPEC] * (n + m),
        input_output_aliases={i: i for i in range(n + m)},
        compiler_params=pltpu.CompilerParams(has_side_effects=DATAFLOW),
    )(*arrs, send_sems, recv_sems, after)
    return list(outs[:n]), list(outs[n:])


def _ag_part(ref, k, hc):
    rh = ref.shape[2] // 2
    return ref.at[:, k, pl.ds(hc * rh, rh), :]


def ag_start(srcs, lands, dep, name):
    def copies(s, d, place):
        x, y, c, chips = place
        out = []
        for j, (px, py) in enumerate(chips):
            for i in range(len(s)):
                rh = s[i].shape[1] // 2
                out.append((s[i].at[:, pl.ds(c * rh, rh), :], _ag_part(d[i], 2 * x + y, c), (px, py, c)))
        return out

    return _split_start(srcs, lands, copies, 3 * len(srcs), dep, name)


def ag_wait(handle, after, name):
    def arrivals(s, d, place):
        x, y, c, chips = place
        out = []
        for j, (px, py) in enumerate(chips):
            for i in range(len(s)):
                rh = s[i].shape[1] // 2
                out.append((s[i].at[:, pl.ds(c * rh, rh), :], _ag_part(d[i], 2 * px + py, c)))
        return out

    return _split_wait(handle, arrivals, after, name)


def ag_forward(lands, name):
    n = len(lands)

    def body(*refs):
        bufs, token = refs[n:2 * n], refs[2 * n]
        send_sems, recv_sems = refs[2 * n + 1:]
        x, y, c, chips = _place()
        sems = (send_sems, recv_sems)
        token[...] = jnp.zeros_like(token)
        cps = []
        for j, (px, py) in enumerate(chips):
            for i in range(n):
                part = _ag_part(bufs[i], 2 * px + py, c)
                cps.append(_rcopy(part, part, sems, 3 * i + j, (x, y, 1 - c)))
        for cp in cps:
            cp.start()
        for j, (px, py) in enumerate(chips):
            for i in range(n):
                part = _ag_part(bufs[i], 2 * px + py, 1 - c)
                _rcopy(part, part, sems, 3 * i + j, (x, y, 1 - c)).wait_recv()
        for cp in cps:
            cp.wait_send()

    outs = pl.pallas_call(
        body, name=name, out_shape=[_sds(a.shape, a.dtype) for a in lands] + [_sds((8, 128), F32)],
        in_specs=[ANY] * n, out_specs=[ANY] * n + [pl.BlockSpec(memory_space=pltpu.VMEM)],
        input_output_aliases={i: i for i in range(n)}, scratch_shapes=_dma_sems(3 * n, 1)[:2])(*lands)
    return list(outs[:n]), outs[n]


def rs_start(hs, lands, dep, name):
    def copies(s, d, place):
        x, y, c, chips = place
        return [(s[i].at[2 * px + py], d[i].at[2 * x + y], (px, py, c)) for j, (px, py) in enumerate(chips) for i in range(len(s))]

    return _split_start(hs, lands, copies, 3 * len(hs), dep, name)


def _kept_out(ref, c):
    rh = ref.shape[1] // 2
    return ref.at[:, pl.ds((1 - c) * rh, rh), :]


def swap_start(arrs, lands, dep, name):
    def copies(s, d, place):
        x, y, c, _ = place
        return [(_kept_out(s[i], c), d[i], (x, y, 1 - c)) for i in range(len(s))]

    return _split_start(arrs, lands, copies, len(arrs), dep, name)


def swap_wait(handle, after, name):
    def arrivals(s, d, place):
        x, y, c, _ = place
        return [(_kept_out(s[i], c), d[i]) for i in range(len(s))]

    return _split_wait(handle, arrivals, after, name)


def rs_wait(handle, after, name):
    def arrivals(s, d, place):
        x, y, c, chips = place
        return [(s[i].at[2 * px + py], d[i].at[2 * px + py]) for j, (px, py) in enumerate(chips) for i in range(len(s))]

    return _split_wait(handle, arrivals, after, name)


def sibling_swap(arrs, name):
    n = len(arrs)
    rh = [a.shape[1] // 2 for a in arrs]

    def body(*refs):
        srcs, outs = refs[:n], refs[n:2 * n]
        send_sems, recv_sems = refs[2 * n:]
        x, y, c, _ = _place()
        cps = [_rcopy(srcs[i].at[:, pl.ds((1 - c) * rh[i], rh[i]), :], outs[i], (send_sems, recv_sems), i, (x, y, 1 - c))
               for i in range(n)]
        for cp in cps:
            cp.start()
        for cp in cps:
            cp.wait()

    return pl.pallas_call(
        body, name=name, out_shape=[_sds((N_CHIPS, r, a.shape[2]), a.dtype) for a, r in zip(arrs, rh)],
        in_specs=[ANY] * n, out_specs=[ANY] * n, scratch_shapes=_dma_sems(n, 1)[:2])(*arrs)


def chip_exchange(hs):
    n = len(hs)

    def body(*refs):
        srcs, outs = refs[:n], refs[n:2 * n]
        send_sems, recv_sems, loc_sems = refs[2 * n:]
        x, y, c, chips = _place()
        sems = (send_sems, recv_sems)
        me = 2 * x + y
        mine = [pltpu.make_async_copy(srcs[i].at[me], outs[i].at[me], loc_sems.at[i]) for i in range(n)]
        for cp in mine:
            cp.start()
        sends = []
        for j, (px, py) in enumerate(chips):
            for i in range(n):
                cp = _rcopy(srcs[i].at[2 * px + py], outs[i].at[me], sems, 3 * i + j, (px, py, c))
                cp.start()
                sends.append(cp)
        for j, (px, py) in enumerate(chips):
            for i in range(n):
                _rcopy(srcs[i].at[2 * px + py], outs[i].at[2 * px + py], sems, 3 * i + j, (px, py, c)).wait_recv()
        for cp in sends:
            cp.wait_send()
        for cp in mine:
            cp.wait()

    return pl.pallas_call(
        body, name="chip_exchange", out_shape=[_sds(h.shape, h.dtype) for h in hs],
        in_specs=[ANY] * n, out_specs=[ANY] * n, scratch_shapes=_dma_sems(3 * n, n))(*hs)


def sum_into(land, base, l, core, name):
    _, rh, C = land.shape
    tr = _row_tile(rh, C, N_CHIPS, mult=16)
    nr = rh // tr

    def body(core_ref, land_ref, base_ref, o_ref):
        acc = land_ref[0].astype(F32)
        for k in range(1, N_CHIPS):
            acc = acc + land_ref[k].astype(F32)
        o_ref[...] = acc

    grid_spec = pltpu.PrefetchScalarGridSpec(
        num_scalar_prefetch=1, grid=(nr,),
        in_specs=[pl.BlockSpec((N_CHIPS, tr, C), lambda r, core_ref: (0, r, 0)), ANY],
        out_specs=pl.BlockSpec((None, tr, C), lambda r, core_ref: (l, core_ref[0] * nr + r, 0)))
    return pl.pallas_call(body, name=name, grid_spec=grid_spec, out_shape=_sds(base.shape, base.dtype),
                          input_output_aliases={2: 0}, compiler_params=_cparams(VMEM_BIG))(
        core.reshape(1).astype(jnp.int32), land, base)


def sibling_join(bases, name):
    n = len(bases)

    def body(*refs):
        bufs = refs[n:2 * n]
        send_sems, recv_sems = refs[2 * n:]
        x, y, c, _ = _place()
        sems = (send_sems, recv_sems)

        def half(i, hc):
            rh = bufs[i].shape[1] // 2
            return bufs[i].at[:, pl.ds(hc * rh, rh), :]

        sends = [_rcopy(half(i, c), half(i, c), sems, i, (x, y, 1 - c)) for i in range(n)]
        for cp in sends:
            cp.start()
        for i in range(n):
            _rcopy(half(i, 1 - c), half(i, 1 - c), sems, i, (x, y, 1 - c)).wait_recv()
        for cp in sends:
            cp.wait_send()

    return pl.pallas_call(
        body, name=name, out_shape=[_sds(b.shape, b.dtype) for b in bases], in_specs=[ANY] * n, out_specs=[ANY] * n,
        input_output_aliases={i: i for i in range(n)}, scratch_shapes=_dma_sems(n, 1)[:2])(*bases)


def ag_all(blk):
    M, C = blk.shape

    def body(x_ref, out_ref, send_sems, recv_sems, loc_sem):
        x, y, c, chips = _place()
        sems = (send_sems, recv_sems)
        me, sibling = (x, y, c), (x, y, 1 - c)

        def slot(px, py, pc):
            return out_ref.at[4 * px + 2 * py + pc]

        mine = pltpu.make_async_copy(x_ref, slot(*me), loc_sem)
        mine.start()
        first = [_rcopy(x_ref, slot(*me), sems, 0, sibling)]
        first += [_rcopy(x_ref, slot(*me), sems, 1 + j, (*chip, c)) for j, chip in enumerate(chips)]
        for cp in first:
            cp.start()
        passed = [_rcopy(slot(*chip, c), slot(*chip, c), sems, 4 + j, sibling) for j, chip in enumerate(chips)]
        for j, chip in enumerate(chips):
            _rcopy(slot(*chip, c), slot(*chip, c), sems, 1 + j, me).wait_recv()
            passed[j].start()
        _rcopy(slot(*sibling), slot(*sibling), sems, 0, me).wait_recv()
        for j, chip in enumerate(chips):
            _rcopy(slot(*chip, 1 - c), slot(*chip, 1 - c), sems, 4 + j, me).wait_recv()
        for cp in first + passed:
            cp.wait_send()
        mine.wait()

    return pl.pallas_call(
        body, name="ag_all", out_shape=_sds((8, M, C), blk.dtype),
        in_specs=[pl.BlockSpec(memory_space=pltpu.VMEM)], out_specs=pl.BlockSpec(memory_space=pltpu.VMEM),
        scratch_shapes=[pltpu.SemaphoreType.DMA((7,)), pltpu.SemaphoreType.DMA((7,)), pltpu.SemaphoreType.DMA(())],
        compiler_params=_cparams(VMEM_BIG))(blk)


WEIGHTS = ["ada_w", "ada_b", "ln_g", "ln_b", "ffn1_w_in", "ffn1_w_out", "ffn2_w_in", "ffn2_w_out", "mix_w_in", "mix_w_out",
           "hgrn_lb_logits", "hgrn_norm_g", "mla_q_norm_g", "mla_kv_norm_g", "mla_w_uq", "mla_w_ukv", "fox_b_f",
           "gmlp_ln_g", "gmlp_ln_b", "gmlp_w_s", "gmlp_b_s"]
SHARDED = {"ffn1_w_in": 1, "ffn1_w_out": 0, "ffn2_w_in": 1, "ffn2_w_out": 0, "mix_w_in": 1, "mix_w_out": 0,
           "mla_w_uq": 1, "mla_w_ukv": 1}
SMALL = ["hgrn_lb_logits", "hgrn_norm_g", "mla_q_norm_g", "mla_kv_norm_g", "fox_b_f", "gmlp_ln_g", "gmlp_ln_b",
         "gmlp_w_s", "gmlp_b_s", "ln_g", "ln_b"]
GATHERED = ["ada_w", "ffn1_w_in", "ffn1_w_out", "ffn2_w_in", "ffn2_w_out", "mix_w_in", "mix_w_out", "mla_w_uq", "mla_w_ukv"]
REDUCED = GATHERED[1:]


def _col_shards(a):
    cols = a.shape[1] // N_CHIPS
    return jnp.stack([a[:, k * cols:(k + 1) * cols] for k in range(N_CHIPS)])


def add_kept_half(a, got, core, name):
    _, R, C = a.shape
    rh = R // 2
    tr = _row_tile(rh, C, mult=16)
    nr = rh // tr

    def body(core_ref, a_ref, b_ref, o_ref):
        o_ref[...] = (a_ref[...] + b_ref[...]).astype(o_ref.dtype)

    half = pl.BlockSpec((None, tr, C), lambda k, r, core_ref: (k, r, 0))
    grid_spec = pltpu.PrefetchScalarGridSpec(
        num_scalar_prefetch=1, grid=(N_CHIPS, nr),
        in_specs=[pl.BlockSpec((None, tr, C), lambda k, r, core_ref: (k, core_ref[0] * nr + r, 0)), half],
        out_specs=half)
    return pl.pallas_call(body, name=name, grid_spec=grid_spec, out_shape=_sds((N_CHIPS, rh, C), BF16),
                          compiler_params=_cparams(VMEM_BIG))(core.reshape(1).astype(jnp.int32), a, got)


def _rows(parts, n_rows, dtype):
    flat = jnp.concatenate([p.reshape(-1) for p in parts])
    pad = n_rows * D - flat.shape[0]
    return jnp.concatenate([flat, jnp.zeros((pad,), dtype)]).reshape(n_rows, D)


def _take(flat, shapes):
    out, o = [], 0
    for shp in shapes:
        n = int(np.prod(shp))
        out.append(flat[o:o + n].reshape(shp))
        o += n
    return out


def _round_up(n, m):
    return -(-n // m) * m


def pack_shard(w):
    parts = [w[n][l] for l in range(DEPTH) for n in SHARDED] + [w[n][l] for l in range(DEPTH) for n in ("ln_g", "ln_b")]
    n = sum(int(np.prod(p.shape)) for p in parts)
    return _rows(parts, _round_up(-(-n // D), 16), F32)


def unpack_shard(pk, like):
    shapes = [like[n].shape[1:] for l in range(DEPTH) for n in SHARDED] + [like[n].shape[1:] for l in range(DEPTH) for n in ("ln_g", "ln_b")]
    pieces = _take(pk.reshape(-1), shapes)
    names = [n for l in range(DEPTH) for n in SHARDED] + [n for l in range(DEPTH) for n in ("ln_g", "ln_b")]
    out = {}
    for n in list(SHARDED) + ["ln_g", "ln_b"]:
        out[n] = jnp.stack([p for p, m in zip(pieces, names) if m == n])
    return out


def pack_small(w):
    parts = [w[n][l] for l in range(DEPTH) for n in SMALL]
    n = sum(int(np.prod(p.shape)) for p in parts)
    return _rows(parts, _round_up(-(-n // D), 8), F32)


def unpack_small(pk, like):
    shapes = [like[n].shape[1:] for l in range(DEPTH) for n in SMALL]
    pieces = _take(pk.reshape(-1), shapes)
    names = [n for l in range(DEPTH) for n in SMALL]
    return {n: jnp.stack([p for p, m in zip(pieces, names) if m == n]) for n in SMALL}


def pack_gather(w):
    parts = [w[n][l].astype(BF16) for l in range(DEPTH) for n in ["ada_w"] + list(SHARDED)]
    ln = jnp.concatenate([w[n][l].reshape(-1) for l in range(DEPTH) for n in ("ln_g", "ln_b")])
    parts.append(lax.bitcast_convert_type(ln, BF16))
    n = sum(int(np.prod(p.shape)) for p in parts)
    return _rows(parts, _round_up(-(-n // D), 16), BF16)


def unpack_gather(g, w):
    names = ["ada_w"] + list(SHARDED)
    shapes = [w[n].shape[1:] for l in range(DEPTH) for n in names]
    n_ln = DEPTH * 2 * 3 * (D // N_CHIPS)
    flat = g.reshape(N_CHIPS, -1)
    per_chip = [_take(flat[k], shapes + [(n_ln, 2)]) for k in range(N_CHIPS)]
    layers = [dict() for _ in range(DEPTH)]
    i = 0
    for l in range(DEPTH):
        for n in names:
            axis = 1 if n == "ada_w" else SHARDED[n]
            layers[l][n] = jnp.concatenate([per_chip[k][i] for k in range(N_CHIPS)], axis=axis)
            i += 1
    ln = [lax.bitcast_convert_type(per_chip[k][i], F32).reshape(DEPTH, 2, 3, D // N_CHIPS) for k in range(N_CHIPS)]
    ln = jnp.concatenate(ln, axis=3)
    for l in range(DEPTH):
        layers[l]["ln_g"], layers[l]["ln_b"] = ln[l, 0], ln[l, 1]
    return layers


def pack_grads(grads, k):
    parts = []
    for l in range(DEPTH):
        g = grads[l]
        full = {"ffn1_w_out": g["ffn1_out"], "ffn2_w_out": g["ffn2_out"], "mix_w_in": mix_in_unext(g["mix_in"]),
                "mix_w_out": mix_out_unext(g["mix_out"]), "mla_w_uq": uq_unext(g["wq"]), "mla_w_ukv": ukv_unext(g["wkv"])}
        for n, axis in SHARDED.items():
            if n in ("ffn1_w_in", "ffn2_w_in"):
                half = g[n.replace("_w_in", "_in")][k // 2]
                parts.append(half[:, (k % 2) * (FF // 2):(k % 2 + 1) * (FF // 2)])
            else:
                sz = full[n].shape[axis] // N_CHIPS
                parts.append(lax.slice_in_dim(full[n], k * sz, (k + 1) * sz, axis=axis))
    for l in range(DEPTH):
        for n in ("ln_g", "ln_b"):
            parts.append(grads[l][n][:, k * (D // N_CHIPS):(k + 1) * (D // N_CHIPS)])
    n = sum(int(np.prod(p.shape)) for p in parts)
    return _rows(parts, _round_up(-(-n // D), 16), F32)


def kernel(x, c, ada_w, ada_b, ln_g, ln_b, ffn1_w_in, ffn1_w_out, ffn2_w_in, ffn2_w_out, mix_w_in, mix_w_out, hgrn_lb_logits, hgrn_norm_g, mla_q_norm_g, mla_kv_norm_g, mla_w_uq, mla_w_ukv, fox_b_f, gmlp_ln_g, gmlp_ln_b, gmlp_w_s, gmlp_b_s, loss_target, m_ada_w, m_ada_b, m_ln_g, m_ln_b, m_ffn1_w_in, m_ffn1_w_out, m_ffn2_w_in, m_ffn2_w_out, m_mix_w_in, m_mix_w_out, m_hgrn_lb_logits, m_hgrn_norm_g, m_mla_q_norm_g, m_mla_kv_norm_g, m_mla_w_uq, m_mla_w_ukv, m_fox_b_f, m_gmlp_ln_g, m_gmlp_ln_b, m_gmlp_w_s, m_gmlp_b_s, v_ada_w, v_ada_b, v_ln_g, v_ln_b, v_ffn1_w_in, v_ffn1_w_out, v_ffn2_w_in, v_ffn2_w_out, v_mix_w_in, v_mix_w_out, v_hgrn_lb_logits, v_hgrn_norm_g, v_mla_q_norm_g, v_mla_kv_norm_g, v_mla_w_uq, v_mla_w_ukv, v_fox_b_f, v_gmlp_ln_g, v_gmlp_ln_b, v_gmlp_w_s, v_gmlp_b_s):
    w = dict(zip(WEIGHTS, (ada_w, ada_b, ln_g, ln_b, ffn1_w_in, ffn1_w_out, ffn2_w_in, ffn2_w_out, mix_w_in, mix_w_out, hgrn_lb_logits, hgrn_norm_g, mla_q_norm_g, mla_kv_norm_g, mla_w_uq, mla_w_ukv, fox_b_f, gmlp_ln_g, gmlp_ln_b, gmlp_w_s, gmlp_b_s)))
    m = dict(zip(WEIGHTS, (m_ada_w, m_ada_b, m_ln_g, m_ln_b, m_ffn1_w_in, m_ffn1_w_out, m_ffn2_w_in, m_ffn2_w_out, m_mix_w_in, m_mix_w_out, m_hgrn_lb_logits, m_hgrn_norm_g, m_mla_q_norm_g, m_mla_kv_norm_g, m_mla_w_uq, m_mla_w_ukv, m_fox_b_f, m_gmlp_ln_g, m_gmlp_ln_b, m_gmlp_w_s, m_gmlp_b_s)))
    v = dict(zip(WEIGHTS, (v_ada_w, v_ada_b, v_ln_g, v_ln_b, v_ffn1_w_in, v_ffn1_w_out, v_ffn2_w_in, v_ffn2_w_out, v_mix_w_in, v_mix_w_out, v_hgrn_lb_logits, v_hgrn_norm_g, v_mla_q_norm_g, v_mla_kv_norm_g, v_mla_w_uq, v_mla_w_ukv, v_fox_b_f, v_gmlp_ln_g, v_gmlp_ln_b, v_gmlp_w_s, v_gmlp_b_s)))
    Bl, S, _ = x.shape
    T = Bl * S
    core = lax.axis_index("c")
    chip = 2 * lax.axis_index("x") + lax.axis_index("y")

    def shard(key):
        n, l = key
        if n == "ln":
            return jnp.concatenate([ln_g[l:l + 1], ln_b[l:l + 1], jnp.zeros((1, 2, D // N_CHIPS), F32)], axis=1)
        return w[n][l:l + 1].astype(BF16)

    mixers = ["mix_w_in", "mix_w_out", "mla_w_uq", "mla_w_ukv"]
    groups = [[("ada_w", 0), ("ffn1_w_in", 0), ("ffn1_w_out", 0), ("ln", 0)],
              [(n, 0) for n in mixers + ["ffn2_w_in", "ffn2_w_out"]],
              [(n, 1) for n in GATHERED + ["ln"]]]
    srcs = [[shard(k) for k in grp] for grp in groups]
    lands = [[own_slot(s, chip) for s in grp] for grp in srcs]
    handle0 = ag_start(srcs[0], lands[0], jnp.zeros((8, 128), F32), "ag_start_0")
    first, token = ag_forward(ag_wait(handle0, lands[2][0], "ag_wait_0")[1], "ag_forward_0")
    have = dict(zip(groups[0], first))
    handles = {}
    for gi in (1, 2):
        handles[gi] = ag_start(srcs[gi], lands[gi], token, "ag_start_%d" % gi)
        token = handles[gi][-1]
    c8 = jnp.concatenate([c, jnp.zeros((8 - Bl, D), F32)], axis=0)
    c8 = c8 + token[0, 0]

    def cat_cols(a):
        return jnp.concatenate([a[0, k] for k in range(N_CHIPS)], axis=1)

    def get(l, part, after):
        gi = 2 if l == 1 else (0 if part in ("ada", "ffn1") else 1)
        if gi in handles:
            arrived, _ = ag_forward(ag_wait(handles.pop(gi), after, "ag_wait_%d" % gi)[1], "ag_forward_%d" % gi)
            have.update(zip(groups[gi], arrived))
        if part == "ada":
            return dict(ada_w=have[("ada_w", l)], wl=0, ada_b=ada_b[l][None])
        if part == "ffn1":
            ln_full = jnp.moveaxis(have[("ln", l)][0], 0, 1).reshape(8, D)
            return dict(ffn1_in=have[("ffn1_w_in", l)], ffn1_out=have[("ffn1_w_out", l)], wl=0,
                        ln_g=ln_full[0:3], ln_b=ln_full[3:6])
        if part == "ffn2":
            return dict(ffn2_in=have[("ffn2_w_in", l)], ffn2_out=have[("ffn2_w_out", l)])
        return dict(
            mix_in=mix_in_ext(cat_cols(have[("mix_w_in", l)])), mix_out=mix_out_ext(have[("mix_w_out", l)].reshape(D, D)),
            wq=uq_ext(cat_cols(have[("mla_w_uq", l)])).astype(F32), wkv=ukv_ext(cat_cols(have[("mla_w_ukv", l)])).astype(F32),
            ng=hgrn_norm_g[l][None], gq=mla_q_norm_g[l][None], gkv=mla_kv_norm_g[l][None],
            bcol=jnp.concatenate([fox_b_f[l], jnp.zeros((8 - HEADS,), F32)])[:, None],
            g_lg=gmlp_ln_g[l][None], g_lb=gmlp_ln_b[l][None], ws=gmlp_w_s[l], bs=gmlp_b_s[l][:, :, None])

    pending = []

    def emit(l, part, g):
        if part == "mix":
            names = mixers
            by_chip = [_col_shards(mix_in_unext(g["mix_in"])), mix_out_unext(g["mix_out"]).reshape(N_CHIPS, D // N_CHIPS, D),
                       _col_shards(uq_unext(g["wq"])), _col_shards(ukv_unext(g["wkv"]))]
        else:
            names = [part + "_w_in", part + "_w_out"]
            by_chip = [g[part + "_in"], g[part + "_out"]]
        tag = "%d_%s" % (l, part)
        got = sibling_swap(by_chip, "sibling_swap_" + tag)
        chip_sum = [add_kept_half(a, r, core, "add_sibling") for a, r in zip(by_chip, got)]
        zones = [lax.dynamic_update_slice(lax.empty(h.shape, h.dtype), lax.dynamic_slice_in_dim(h, chip, 1, axis=0), (chip, 0, 0))
                 for h in chip_sum]
        handle = rs_start(chip_sum, zones, chip_sum[0], "rs_start_" + tag)
        pending.append((l, names, handle, tag))
        return handle[-1][0, 0]

    loss_tile, dx, dmods, grads, d_logits = local_step(
        x.reshape(T, D), c8, loss_target.reshape(T, D), get, hgrn_lb_logits, S, emit)
    loss = lax.psum(loss_tile[0, 0], ("x", "y", "c"))

    small_g = {"hgrn_lb_logits": d_logits,
               "hgrn_norm_g": jnp.stack([grads[l]["ng"][0] for l in range(DEPTH)]),
               "mla_q_norm_g": jnp.stack([grads[l]["gq"][0] for l in range(DEPTH)]),
               "mla_kv_norm_g": jnp.stack([grads[l]["gkv"][0] for l in range(DEPTH)]),
               "fox_b_f": jnp.stack([grads[l]["bcol"][0:HEADS, 0] for l in range(DEPTH)]),
               "gmlp_ln_g": jnp.stack([grads[l]["g_lg"][0] for l in range(DEPTH)]),
               "gmlp_ln_b": jnp.stack([grads[l]["g_lb"][0] for l in range(DEPTH)]),
               "gmlp_w_s": jnp.stack([grads[l]["ws"] for l in range(DEPTH)]),
               "gmlp_b_s": jnp.stack([grads[l]["bs"][:, :, 0] for l in range(DEPTH)])}
    small_g["ln_g"] = jnp.stack([grads[l]["ln_g"] for l in range(DEPTH)])
    small_g["ln_b"] = jnp.stack([grads[l]["ln_b"] for l in range(DEPTH)])
    pk_small = pack_small(small_g)
    n_small = pk_small.shape[0]
    extras = [dmods[l] for l in range(DEPTH)] + [c]
    n_extra = _round_up(-(-sum(int(np.prod(e.shape)) for e in extras) // D), 8)
    gathered = ag_all(jnp.concatenate([pk_small, _rows(extras, n_extra, F32)], axis=0))
    g_small = unpack_small(sum_slots(gathered[:, 0:n_small], 8, "sum_small"), small_g)
    ext = gathered[:, n_small:].reshape(8, -1)
    n_dmod = DEPTH * Bl * N_MOD * D
    dmod_all = ext[:, 0:n_dmod].reshape(8, DEPTH, Bl, N_MOD * D)
    c_all = ext[:, n_dmod:n_dmod + Bl * D].reshape(8 * Bl, D)
    g_ada_w, g_ada_b = [], []
    ncol = N_MOD * D // N_CHIPS
    for l in range(DEPTH):
        dm = dmod_all[:, l].reshape(8 * Bl, N_MOD * D)
        g_ada_w.append(ada_grad(c_all, lax.dynamic_slice_in_dim(dm, chip * ncol, ncol, axis=1)))
        g_ada_b.append(sum_slots(dm.reshape(8 * Bl, N_MOD, D), 8 * Bl, "sum_ada_b").reshape(N_MOD * D))
    g_ada_w, g_ada_b = jnp.stack(g_ada_w), jnp.stack(g_ada_b)

    red = {n: lax.empty(w[n].shape, F32) for n in REDUCED}

    def arrive(entry, after):
        l, names, handle, tag = entry
        for n, land in zip(names, rs_wait(handle, after, "rs_wait_" + tag)[1]):
            red[n] = sum_into(land, red[n], l, core, "sum_chips")

    for entry in pending[:-1]:
        arrive(entry, dx)
    late = pending[-1][1]
    early = [n for n in REDUCED if n not in late]
    grad = dict(zip(early, sibling_join([red[n] for n in early], "sibling_join_a")))
    grad.update(g_small)
    grad["ada_w"], grad["ada_b"] = g_ada_w, g_ada_b
    for n in ("ln_g", "ln_b"):
        grad[n] = lax.dynamic_slice_in_dim(g_small[n], chip * (D // N_CHIPS), D // N_CHIPS, axis=2)
    out = {"grad": grad, "delta": {}, "new_m": {}, "new_v": {}}

    def update(n):
        shp = w[n].shape
        two_d = (-1, shp[-1])
        res = adamw(w[n].reshape(two_d), grad[n].reshape(two_d), m[n].reshape(two_d), v[n].reshape(two_d), "adamw_" + n)
        grad[n] = grad[n].reshape(shp)
        for key, r in zip(("delta", "new_m", "new_v"), res):
            out[key][n] = r.reshape(shp)

    for n in WEIGHTS:
        if n not in late:
            update(n)
    arrive(pending[-1], out["delta"]["ffn2_w_in"])
    grad.update(zip(late, sibling_join([red[n] for n in late], "sibling_join_b")))
    for n in late:
        update(n)
    outs = [loss, dx.reshape(Bl, S, D)]
    for key in ("grad", "delta", "new_m", "new_v"):
        outs += [out[key][n] for n in WEIGHTS]
    return tuple(outs)
```

```python
import functools

import jax
import jax.numpy as jnp
import numpy as np
from jax import lax
from jax.experimental import pallas as pl
from jax.experimental.pallas import tpu as pltpu

F32, BF16 = jnp.float32, jnp.bfloat16
MESH = pl.DeviceIdType.MESH

N_CHIPS = 4
D = 1024
DEPTH = 2
FF = 2816
N_MOD = 9
GW = 256
HEADS = 4
HD = 64
HP = 128
A_CHUNK = 16
LB_FLOOR = 1e-30
B_Q_LORA, B_KV_LORA, B_NOPE, B_ROPE = 256, 128, 64, 32
ROPE_THETA = 10000.0
D_CHUNK = 128
MIX_COLS = 2724
ALPHA = (2 * DEPTH) ** 0.25
LN_EPS = 1e-5
RMS_EPS = 1e-6
ADAM_LR, ADAM_B1, ADAM_B2, ADAM_EPS, ADAM_WD, ADAM_STEP = 0.001, 0.9, 0.999, 1e-08, 0.01, 10

NP = 3840
NP_TILE = 1920
C_A, C_B, C_CQ, C_CK, C_CV, C_D, C_CF = 0, 1024, 1536, 2048, 2560, 3072, 3584
NCAT = 1536
O_A, O_B, O_C, O_D = 0, 256, 768, 1280

VMEM_BIG = 48 << 20


def _cparams(vmem=None):
    return pltpu.CompilerParams(vmem_limit_bytes=vmem) if vmem else pltpu.CompilerParams()


def _sds(shape, dtype):
    return jax.ShapeDtypeStruct(tuple(shape), dtype)


@jax.custom_vjp
def _mm(a, w):
    return jnp.dot(a.astype(BF16), w.astype(BF16), preferred_element_type=F32)


def _mm_f(a, w):
    return _mm(a, w), (a, w)


def _mm_b(res, g):
    a, w = res
    gb = g.astype(BF16)
    da = lax.dot_general(gb, w.astype(BF16), (((1,), (1,)), ((), ())), preferred_element_type=F32)
    dw = lax.dot_general(a.astype(BF16), gb, (((0,), (0,)), ((), ())), preferred_element_type=F32)
    return da.astype(a.dtype), dw.astype(w.dtype)


_mm.defvjp(_mm_f, _mm_b)


@jax.custom_vjp
def _mm_nt(a, b):
    return lax.dot_general(a.astype(BF16), b.astype(BF16), (((1,), (1,)), ((), ())), preferred_element_type=F32)


def _mm_nt_f(a, b):
    return _mm_nt(a, b), (a, b)


def _mm_nt_b(res, g):
    a, b = res
    gb = g.astype(BF16)
    da = jnp.dot(gb, b.astype(BF16), preferred_element_type=F32)
    db = lax.dot_general(gb, a.astype(BF16), (((0,), (0,)), ((), ())), preferred_element_type=F32)
    return da.astype(a.dtype), db.astype(b.dtype)


_mm_nt.defvjp(_mm_nt_f, _mm_nt_b)


@jax.custom_vjp
def _mm_tn(a, b):
    return lax.dot_general(a.astype(BF16), b.astype(BF16), (((0,), (0,)), ((), ())), preferred_element_type=F32)


def _mm_tn_f(a, b):
    return _mm_tn(a, b), (a, b)


def _mm_tn_b(res, g):
    a, b = res
    gb = g.astype(BF16)
    da = lax.dot_general(b.astype(BF16), gb, (((1,), (1,)), ((), ())), preferred_element_type=F32)
    db = jnp.dot(a.astype(BF16), gb, preferred_element_type=F32)
    return da.astype(a.dtype), db.astype(b.dtype)


_mm_tn.defvjp(_mm_tn_f, _mm_tn_b)


def _mm_hi(a, w):
    return jnp.dot(a, w, precision=lax.Precision.HIGHEST, preferred_element_type=F32)


def _iota(shape, dim):
    return lax.broadcasted_iota(jnp.int32, shape, dim)


def _head_sum_mats():
    e = (_iota((GW, HP), 0) // HD == _iota((GW, HP), 1)).astype(F32)
    et = (_iota((HP, GW), 1) // HD == _iota((HP, GW), 0)).astype(F32)
    return e, et


def _modulate(x, mod_ref, sh, sc):
    return x * (1.0 + mod_ref[sc:sc + 1, :]) + mod_ref[sh:sh + 1, :]


def _ln_res(x, y, gate, lg, lb, gs):
    r = ALPHA * x + gs * (1.0 + gate) * y
    mu = jnp.mean(r, axis=-1, keepdims=True)
    var = jnp.mean(jnp.square(r - mu), axis=-1, keepdims=True)
    return (r - mu) * lax.rsqrt(var + LN_EPS) * lg + lb


def _rms(x, g):
    return x * lax.rsqrt(jnp.mean(x * x, axis=-1, keepdims=True) + RMS_EPS) * g


def _tile(n, pref):
    return pref if n % pref == 0 else n


def mod_fwd(c8, w, l, b):
    tn = w.shape[3]
    n = N_CHIPS * tn

    def body(c_ref, w_ref, b_ref, o_ref):
        h = jax.nn.silu(c_ref[...]).astype(BF16)
        o_ref[...] = jnp.dot(h, w_ref[...], preferred_element_type=F32) + b_ref[...]

    return pl.pallas_call(
        body, name="mod_fwd", grid=(N_CHIPS,),
        in_specs=[pl.BlockSpec((8, D), lambda j: (0, 0)), pl.BlockSpec((None, None, D, tn), lambda j: (l, j, 0, 0)),
                  pl.BlockSpec((1, tn), lambda j: (0, j))],
        out_specs=pl.BlockSpec((8, tn), lambda j: (0, j)), out_shape=_sds((8, n), F32),
        compiler_params=_cparams(VMEM_BIG))(c8, w, b)


def ffn_in_fwd(x, mod, w_in, l, sh, sc, S):
    T = x.shape[0]
    tm, tn = _tile(S, 512), FF // 2
    tpb, nj = S // tm, 2

    def body(x_ref, mod_ref, wg_ref, wu_ref, zg_ref, zu_ref, act_ref, h_ref):
        @pl.when(pl.program_id(1) == 0)
        def _():
            h_ref[...] = _modulate(x_ref[...], mod_ref, sh, sc).astype(BF16)
        g = jnp.dot(h_ref[...], wg_ref[...], preferred_element_type=F32)
        u = jnp.dot(h_ref[...], wu_ref[...], preferred_element_type=F32)
        zg_ref[...] = g.astype(BF16)
        zu_ref[...] = u.astype(BF16)
        act_ref[...] = (jax.nn.silu(g) * u).astype(BF16)

    return pl.pallas_call(
        body, name="ffn_in_fwd", grid=(T // tm, nj),
        in_specs=[pl.BlockSpec((tm, D), lambda i, j: (i, 0)),
                  pl.BlockSpec((None, N_MOD, D), lambda i, j: (i // tpb, 0, 0)),
                  pl.BlockSpec((None, None, D, tn), lambda i, j: (l, j, 0, 0)),
                  pl.BlockSpec((None, None, D, tn), lambda i, j: (l, j + nj, 0, 0))],
        out_specs=[pl.BlockSpec((tm, tn), lambda i, j: (i, j))] * 3,
        out_shape=[_sds((T, FF), BF16)] * 3,
        scratch_shapes=[pltpu.VMEM((tm, D), BF16)],
        compiler_params=_cparams(VMEM_BIG))(x, mod, w_in, w_in)


def mix_in_fwd(x, mod, w, sh, sc, S):
    T = x.shape[0]
    n = w.shape[1]
    tm, tn = _tile(S, 512), NP_TILE
    tpb = S // tm

    def body(x_ref, mod_ref, w_ref, o_ref, h_ref):
        @pl.when(pl.program_id(1) == 0)
        def _():
            h_ref[...] = _modulate(x_ref[...], mod_ref, sh, sc).astype(BF16)
        o_ref[...] = jnp.dot(h_ref[...], w_ref[...], preferred_element_type=F32)

    return pl.pallas_call(
        body, name="mix_in_fwd", grid=(T // tm, n // tn),
        in_specs=[pl.BlockSpec((tm, D), lambda i, j: (i, 0)),
                  pl.BlockSpec((None, N_MOD, D), lambda i, j: (i // tpb, 0, 0)),
                  pl.BlockSpec((D, tn), lambda i, j: (0, j))],
        out_specs=pl.BlockSpec((tm, tn), lambda i, j: (i, j)), out_shape=_sds((T, n), F32),
        scratch_shapes=[pltpu.VMEM((tm, D), BF16)],
        compiler_params=_cparams(VMEM_BIG))(x, mod, w)


def out_ln_fwd(act, w_out, x, mod, lg, lb, gate, gs, S, l=None):
    T, K = act.shape
    tm = _tile(S, 512)
    tpb = S // tm

    def body(a_ref, w_ref, x_ref, mod_ref, lg_ref, lb_ref, y_ref, xn_ref):
        y = jnp.dot(a_ref[...], w_ref[...].reshape(K, D), preferred_element_type=F32)
        y_ref[...] = y
        xn_ref[...] = _ln_res(x_ref[...], y, mod_ref[gate:gate + 1, :], lg_ref[...], lb_ref[...], gs)

    if l is None:
        w_spec = pl.BlockSpec((K, D), lambda i: (0, 0))
    else:
        w_spec = pl.BlockSpec((None, N_CHIPS, K // N_CHIPS, D), lambda i: (l, 0, 0, 0))
    return pl.pallas_call(
        body, name="out_ln_fwd", grid=(T // tm,),
        in_specs=[pl.BlockSpec((tm, K), lambda i: (i, 0)), w_spec,
                  pl.BlockSpec((tm, D), lambda i: (i, 0)),
                  pl.BlockSpec((None, N_MOD, D), lambda i: (i // tpb, 0, 0)),
                  pl.BlockSpec((1, D), lambda i: (0, 0)), pl.BlockSpec((1, D), lambda i: (0, 0))],
        out_specs=[pl.BlockSpec((tm, D), lambda i: (i, 0))] * 2,
        out_shape=[_sds((T, D), F32), _sds((T, D), F32)],
        compiler_params=_cparams(VMEM_BIG))(act, w_out, x, mod, lg, lb)


def ln_res_bwd(dxn, x, y, mod, lg, lb, gate, gs, S):
    T = x.shape[0]
    B = T // S
    tm = _tile(S, 512)
    tpb = S // tm

    def body(d_ref, x_ref, y_ref, mod_ref, lg_ref, lb_ref, dx_ref, dy_ref, dg_ref, dlg_ref, dlb_ref):
        i = pl.program_id(0)
        f = functools.partial(_ln_res, gs=gs)
        _, vjp = jax.vjp(f, x_ref[...], y_ref[...], mod_ref[gate:gate + 1, :], lg_ref[...], lb_ref[...])
        dx, dy, dg, dlg, dlb = vjp(d_ref[...])
        dx_ref[...] = dx
        dy_ref[...] = dy.astype(BF16)

        @pl.when(i % tpb == 0)
        def _():
            dg_ref[...] = jnp.zeros_like(dg_ref)

        @pl.when(i == 0)
        def _():
            dlg_ref[...] = jnp.zeros_like(dlg_ref)
            dlb_ref[...] = jnp.zeros_like(dlb_ref)

        dg_ref[...] += dg
        dlg_ref[...] += dlg
        dlb_ref[...] += dlb

    tok = pl.BlockSpec((tm, D), lambda i: (i, 0))
    vec = pl.BlockSpec((1, D), lambda i: (0, 0))
    return pl.pallas_call(
        body, name="ln_res_bwd", grid=(T // tm,),
        in_specs=[tok, tok, tok, pl.BlockSpec((None, N_MOD, D), lambda i: (i // tpb, 0, 0)), vec, vec],
        out_specs=[tok, tok, pl.BlockSpec((None, 1, D), lambda i: (i // tpb, 0, 0)), vec, vec],
        out_shape=[_sds((T, D), F32), _sds((T, D), BF16), _sds((B, 1, D), F32), _sds((1, D), F32), _sds((1, D), F32)],
        compiler_params=_cparams(VMEM_BIG))(dxn, x, y, mod, lg, lb)


def swiglu_bwd(dy, w_out, l, zg, zu, S):
    T = dy.shape[0]
    tm, tn = _tile(S, 512), FF // 2

    def body(dy_ref, w_ref, zg_ref, zu_ref, dg_ref, du_ref):
        da = lax.dot_general(dy_ref[...], w_ref[...].reshape(tn, D), (((1,), (1,)), ((), ())), preferred_element_type=F32)
        g, u = zg_ref[...].astype(F32), zu_ref[...].astype(F32)
        sg = jax.nn.sigmoid(g)
        dg_ref[...] = (da * u * (sg * (1.0 + g * (1.0 - sg)))).astype(BF16)
        du_ref[...] = (da * (g * sg)).astype(BF16)

    zt = pl.BlockSpec((tm, tn), lambda i, j: (i, j))
    return pl.pallas_call(
        body, name="swiglu_bwd", grid=(T // tm, FF // tn),
        in_specs=[pl.BlockSpec((tm, D), lambda i, j: (i, 0)),
                  pl.BlockSpec((None, 2, FF // N_CHIPS, D), lambda i, j: (l, j, 0, 0)), zt, zt],
        out_specs=[zt, zt], out_shape=[_sds((T, FF), BF16), _sds((T, FF), BF16)],
        compiler_params=_cparams(VMEM_BIG))(dy, w_out, zg, zu)


def nt_plain(dy, w):
    T = dy.shape[0]
    K = w.shape[0]
    tm = _tile(T, 512)

    def body(dy_ref, w_ref, o_ref):
        o_ref[...] = lax.dot_general(dy_ref[...], w_ref[...], (((1,), (1,)), ((), ())), preferred_element_type=F32)

    return pl.pallas_call(
        body, name="nt_plain", grid=(T // tm,),
        in_specs=[pl.BlockSpec((tm, D), lambda i: (i, 0)), pl.BlockSpec((K, D), lambda i: (0, 0))],
        out_specs=pl.BlockSpec((tm, K), lambda i: (i, 0)), out_shape=_sds((T, K), F32),
        compiler_params=_cparams(VMEM_BIG))(dy, w)


def _tn_step(acc, o_ref, lhs, rhs, t, nt):
    part = lax.dot_general(lhs, rhs, (((0,), (0,)), ((), ())), preferred_element_type=F32)

    @pl.when(t == 0)
    def _():
        acc[...] = part

    @pl.when(t > 0)
    def _():
        acc[...] += part

    @pl.when(t == nt - 1)
    def _():
        o_ref[...] = acc[...].astype(o_ref.dtype)


def tn_mm(a, b, tk):
    T, K = a.shape
    N = b.shape[1]
    tt = _tile(T, 512)
    nt = T // tt

    def body(a_ref, b_ref, o_ref, acc):
        _tn_step(acc, o_ref, a_ref[...], b_ref[...], pl.program_id(1), nt)

    return pl.pallas_call(
        body, name="tn_mm", grid=(K // tk, nt),
        in_specs=[pl.BlockSpec((tt, tk), lambda k, t: (t, k)), pl.BlockSpec((tt, N), lambda k, t: (t, 0))],
        out_specs=pl.BlockSpec((tk, N), lambda k, t: (k, 0)), out_shape=_sds((K, N), BF16),
        scratch_shapes=[pltpu.VMEM((tk, N), F32)], compiler_params=_cparams(VMEM_BIG))(a, b)


def tn_mm_mod(x, mod, b, sh, sc, S, tn):
    T = x.shape[0]
    N = b.shape[1]
    tt = _tile(S, 512)
    tpb = S // tt
    nt = T // tt

    def body(x_ref, mod_ref, b_ref, o_ref, acc):
        h = _modulate(x_ref[...], mod_ref, sh, sc).astype(BF16)
        _tn_step(acc, o_ref, h, b_ref[...], pl.program_id(1), nt)

    return pl.pallas_call(
        body, name="tn_mm_mod", grid=(N // tn, nt),
        in_specs=[pl.BlockSpec((tt, D), lambda j, t: (t, 0)),
                  pl.BlockSpec((None, N_MOD, D), lambda j, t: (t // tpb, 0, 0)),
                  pl.BlockSpec((tt, tn), lambda j, t: (t, j))],
        out_specs=pl.BlockSpec((D, tn), lambda j, t: (0, j)), out_shape=_sds((D, N), BF16),
        scratch_shapes=[pltpu.VMEM((D, tn), F32)], compiler_params=_cparams(VMEM_BIG))(x, mod, b)


def tn_mm_mod_shards(x, mod, bg, bu, sh, sc, S):
    T = x.shape[0]
    tn = FF // 2
    tt = _tile(S, 512)
    tpb = S // tt
    nt = T // tt

    def body(x_ref, mod_ref, bg_ref, bu_ref, o_ref, acc):
        j, t = pl.program_id(0), pl.program_id(1)
        h = _modulate(x_ref[...], mod_ref, sh, sc).astype(BF16)

        @pl.when(j < 2)
        def _():
            _tn_step(acc, o_ref, h, bg_ref[...], t, nt)

        @pl.when(j >= 2)
        def _():
            _tn_step(acc, o_ref, h, bu_ref[...], t, nt)

    return pl.pallas_call(
        body, name="tn_mm_mod_shards", grid=(N_CHIPS, T // tt),
        in_specs=[pl.BlockSpec((tt, D), lambda j, t: (t, 0)),
                  pl.BlockSpec((None, N_MOD, D), lambda j, t: (t // tpb, 0, 0)),
                  pl.BlockSpec((tt, tn), lambda j, t: (jnp.where(j < 2, t, 0), jnp.minimum(j, 1))),
                  pl.BlockSpec((tt, tn), lambda j, t: (jnp.where(j < 2, 0, t), jnp.maximum(j - 2, 0)))],
        out_specs=pl.BlockSpec((None, D, tn), lambda j, t: (j, 0, 0)), out_shape=_sds((N_CHIPS, D, tn), BF16),
        scratch_shapes=[pltpu.VMEM((D, tn), F32)], compiler_params=_cparams(VMEM_BIG))(x, mod, bg, bu)


def nt_mod_bwd(ds, w, offs, x, mod, dres, sc, S, tk, l=None):
    T = x.shape[0]
    B = T // S
    tm = _tile(S, 512)
    tpb = S // tm
    Kd = ds[0].shape[1]
    nk = Kd // tk
    n_in = len(ds)

    def body(*refs):
        d_refs, w_refs = refs[:n_in], refs[n_in:2 * n_in]
        x_ref, mod_ref, r_ref, dx_ref, dsh_ref, dsc_ref, acc = refs[2 * n_in:]
        i, k = pl.program_id(0), pl.program_id(1)

        part = sum(lax.dot_general(d_ref[...], w_ref[...], (((1,), (1,)), ((), ())), preferred_element_type=F32)
                   for d_ref, w_ref in zip(d_refs, w_refs))

        @pl.when(k == 0)
        def _():
            acc[...] = part

        @pl.when(k > 0)
        def _():
            acc[...] += part

        @pl.when(k == nk - 1)
        def _():
            dh = acc[...]
            dx_ref[...] = dh * (1.0 + mod_ref[sc:sc + 1, :]) + r_ref[...]

            @pl.when(i % tpb == 0)
            def _():
                dsh_ref[...] = jnp.zeros_like(dsh_ref)
                dsc_ref[...] = jnp.zeros_like(dsc_ref)

            dsh_ref[...] += jnp.sum(dh, axis=0, keepdims=True)
            dsc_ref[...] += jnp.sum(dh * x_ref[...], axis=0, keepdims=True)

    tok = pl.BlockSpec((tm, D), lambda i, k: (i, 0))
    vec = pl.BlockSpec((None, 1, D), lambda i, k: (i // tpb, 0, 0))
    in_specs = [pl.BlockSpec((tm, tk), lambda i, k: (i, k)) for _ in ds]
    if l is None:
        in_specs += [pl.BlockSpec((D, tk), functools.partial(lambda i, k, o: (0, k + o), o=off // tk)) for off in offs]
    else:
        in_specs += [pl.BlockSpec((None, None, D, tk), functools.partial(lambda i, k, o: (l, k + o, 0, 0), o=off)) for off in offs]
    in_specs += [tok, pl.BlockSpec((None, N_MOD, D), lambda i, k: (i // tpb, 0, 0)), tok]
    return pl.pallas_call(
        body, name="nt_mod_bwd", grid=(T // tm, nk), in_specs=in_specs,
        out_specs=[tok, vec, vec],
        out_shape=[_sds((T, D), F32), _sds((B, 1, D), F32), _sds((B, 1, D), F32)],
        scratch_shapes=[pltpu.VMEM((tm, D), F32)],
        compiler_params=_cparams(VMEM_BIG))(*ds, *([w] * n_in), x, mod, dres)


def loss_head(y, tgt):
    T = y.shape[0]
    tm = _tile(T, 512)

    def body(y_ref, t_ref, l_ref, d_ref):
        @pl.when(pl.program_id(0) == 0)
        def _():
            l_ref[...] = jnp.zeros_like(l_ref)
        e = y_ref[...] - t_ref[...]
        d_ref[...] = e * (1.0 / D)
        l_ref[...] += 0.5 * jnp.sum(jnp.sum(e * e, axis=1, keepdims=True) * (1.0 / D))

    tok = pl.BlockSpec((tm, D), lambda i: (i, 0))
    return pl.pallas_call(
        body, name="loss_head", grid=(T // tm,), in_specs=[tok, tok],
        out_specs=[pl.BlockSpec((8, 128), lambda i: (0, 0)), tok],
        out_shape=[_sds((8, 128), F32), _sds((T, D), F32)],
        compiler_params=_cparams(VMEM_BIG))(y, tgt)


def _hgrn_block(q, fz, inp, go, st, lb, ng, blk):
    nc = blk // A_CHUNK
    lb_eff = jnp.maximum(lb, LB_FLOOR)
    log_f = jnp.logaddexp(jnp.log(lb_eff), jnp.log1p(-lb) + jax.nn.log_sigmoid(fz))
    k = (1.0 - lb) * jax.nn.sigmoid(-fz) - (lb_eff - lb)
    qf = jax.nn.silu(q)
    same_chunk = _iota((blk, blk), 0) // A_CHUNK == _iota((blk, blk), 1) // A_CHUNK
    tril = (same_chunk & (_iota((blk, blk), 1) <= _iota((blk, blk), 0))).astype(F32)
    G = _mm_hi(tril, log_f)
    e_mat, et_mat = _head_sum_mats()
    G4, q4, k4, v4 = (z.reshape(nc, A_CHUNK, GW) for z in (G, qf, k, inp))
    shp = (nc, A_CHUNK, A_CHUNK, GW)
    one = (1, A_CHUNK, A_CHUNK, GW)
    mask = jnp.where(_iota(one, 2) <= _iota(one, 1), 0.0, -jnp.inf)
    decay = jnp.exp((G4[:, :, None, :] - G4[:, None, :, :]) + mask)
    prod = q4[:, :, None, :] * k4[:, None, :, :] * decay
    scores = _mm(prod.reshape(nc * A_CHUNK * A_CHUNK, GW), e_mat.astype(BF16))
    spread = _mm(scores, et_mat.astype(BF16)).reshape(shp)
    o_intra = jnp.sum(spread * v4[:, None, :, :], axis=2).reshape(blk, GW)
    head_diag = (_iota((GW, GW), 0) // HD == _iota((GW, GW), 1) // HD).astype(F32)
    g_last = [jnp.sum(log_f[c * A_CHUNK:(c + 1) * A_CHUNK], axis=0, keepdims=True) for c in range(nc)]
    g_last_b = jnp.concatenate([jnp.broadcast_to(g, (A_CHUNK, GW)) for g in g_last], axis=0)
    q_dec = qf * jnp.exp(G)
    k_end = k * jnp.exp(g_last_b - G)
    outs = []
    for c in range(nc):
        rows = slice(c * A_CHUNK, (c + 1) * A_CHUNK)
        outs.append(_mm_nt(q_dec[rows], st))
        st = st * jnp.exp(g_last[c]) + _mm_tn(inp[rows], k_end[rows]) * head_diag
    o = o_intra + jnp.concatenate(outs, axis=0)
    ms = _mm_hi(o * o, e_mat) * (1.0 / HD)
    o = o * _mm_hi(lax.rsqrt(ms + RMS_EPS), et_mat) * ng
    return o * jax.nn.silu(go), st


HGRN_BLK = 128


def hgrn_fwd(proj, lb, ng, S):
    T = proj.shape[0]
    B = T // S
    blk = min(HGRN_BLK, S)
    nb = S // blk

    def body(p_ref, lb_ref, ng_ref, o_ref, st_out_ref, st_ref):
        @pl.when(pl.program_id(1) == 0)
        def _():
            st_ref[...] = jnp.zeros_like(st_ref)
        st_out_ref[...] = st_ref[...]
        p = p_ref[...]
        o, st = _hgrn_block(p[:, 0:GW], p[:, GW:2 * GW], p[:, 2 * GW:3 * GW], p[:, 3 * GW:4 * GW],
                            st_ref[...], lb_ref[...], ng_ref[...], blk)
        o_ref[...] = o.astype(BF16)
        st_ref[...] = st

    vec = pl.BlockSpec((1, GW), lambda b, j: (0, 0))
    return pl.pallas_call(
        body, name="hgrn_fwd", grid=(B, nb),
        in_specs=[pl.BlockSpec((blk, 4 * GW), lambda b, j: (b * nb + j, C_A // (4 * GW))), vec, vec],
        out_specs=[pl.BlockSpec((blk, GW), lambda b, j: (b * nb + j, 0)),
                   pl.BlockSpec((None, GW, GW), lambda b, j: (b * nb + j, 0, 0))],
        out_shape=[_sds((T, GW), BF16), _sds((B * nb, GW, GW), F32)],
        scratch_shapes=[pltpu.VMEM((GW, GW), F32)],
        compiler_params=_cparams(VMEM_BIG))(proj, lb, ng)


def hgrn_bwd(proj, states, dcat, lb, ng, S):
    T = proj.shape[0]
    B = T // S
    blk = min(HGRN_BLK, S)
    nb = S // blk

    def body(p_ref, st_in_ref, do_ref, lb_ref, ng_ref, dp_ref, dlb_ref, dng_ref, dst_ref):
        b, j = pl.program_id(0), pl.program_id(1)

        @pl.when(j == 0)
        def _():
            dst_ref[...] = jnp.zeros_like(dst_ref)

        @pl.when((b == 0) & (j == 0))
        def _():
            dlb_ref[...] = jnp.zeros_like(dlb_ref)
            dng_ref[...] = jnp.zeros_like(dng_ref)

        p = p_ref[...]
        f = functools.partial(_hgrn_block, blk=blk)
        _, vjp = jax.vjp(f, p[:, 0:GW], p[:, GW:2 * GW], p[:, 2 * GW:3 * GW], p[:, 3 * GW:4 * GW],
                         st_in_ref[...], lb_ref[...], ng_ref[...])
        dq, df, di, dg, dst, dlb, dng = vjp((do_ref[...], dst_ref[...]))
        dp_ref[...] = jnp.concatenate([dq, df, di, dg], axis=1).astype(BF16)
        dst_ref[...] = dst
        dlb_ref[...] += dlb
        dng_ref[...] += dng

    def rev(b, j):
        return b * nb + (nb - 1 - j)

    vec = pl.BlockSpec((1, GW), lambda b, j: (0, 0))
    return pl.pallas_call(
        body, name="hgrn_bwd", grid=(B, nb),
        in_specs=[pl.BlockSpec((blk, 4 * GW), lambda b, j: (rev(b, j), C_A // (4 * GW))),
                  pl.BlockSpec((None, GW, GW), lambda b, j: (rev(b, j), 0, 0)),
                  pl.BlockSpec((blk, GW), lambda b, j: (rev(b, j), O_A // GW)), vec, vec],
        out_specs=[pl.BlockSpec((blk, 4 * GW), lambda b, j: (rev(b, j), 0)), vec, vec],
        out_shape=[_sds((T, 4 * GW), BF16), _sds((1, GW), F32), _sds((1, GW), F32)],
        scratch_shapes=[pltpu.VMEM((GW, GW), F32)],
        compiler_params=_cparams(VMEM_BIG))(proj, states, dcat, lb, ng)


ATT_TQ = 256


ATT_BANDS = 8


def _attn_block(q, k, v, cum, qpos0, scale, use_cum, n_free):
    s = _mm_nt(q, k) * scale
    if use_cum:
        s = s - cum
    band = s[:, n_free:]
    visible = _iota(band.shape, 1) <= (qpos0 - n_free) + _iota(band.shape, 0)
    band = jnp.where(visible, band, -jnp.inf)
    m = jnp.max(band, axis=-1, keepdims=True)
    if n_free:
        free = s[:, :n_free]
        m = jnp.maximum(m, jnp.max(free, axis=-1, keepdims=True))
    if not use_cum:
        m = lax.stop_gradient(m)
    e = jnp.exp(band - m)
    denom = jnp.sum(e, axis=-1, keepdims=True)
    o = _mm(e, v[n_free:])
    if n_free:
        e = jnp.exp(free - m)
        denom = denom + jnp.sum(e, axis=-1, keepdims=True)
        o = o + _mm(e, v[:n_free])
    return o * (1.0 / denom)


def _bands(S, tq):
    nq = S // tq
    nb = min(ATT_BANDS, nq)
    per = nq // nb
    return [(r * per, (r + 1) * per, (r + 1) * per * tq) for r in range(nb)]


def attn_fwd(qa, qo, ka, ko, va, vo, cum, scale, S):
    T = qa.shape[0]
    B = T // S
    tq = min(ATT_TQ, S)
    nq = S // tq
    use_cum = cum is not None

    def body(*refs):
        if use_cum:
            q_ref, k_ref, v_ref, c_ref, o_ref = refs
        else:
            (q_ref, k_ref, v_ref, o_ref), c_ref = refs, None
        h, i = pl.program_id(1), pl.program_id(2)
        for lo, hi, kw in _bands(S, tq):
            @pl.when((i >= lo) & (i < hi))
            def _():
                crow = c_ref[pl.ds(h, 1), 0:kw] if use_cum else None
                o = _attn_block(q_ref[...], k_ref[0:kw, :], v_ref[0:kw, :], crow, i * tq, scale, use_cum, lo * tq)
                o_ref[...] = o.astype(BF16)

    in_specs = [pl.BlockSpec((tq, HP), lambda b, h, i: (b * nq + i, qo + h)),
                pl.BlockSpec((S, HP), lambda b, h, i: (b, ko + h)),
                pl.BlockSpec((S, HP), lambda b, h, i: (b, vo + h))]
    args = [qa, ka, va]
    if use_cum:
        in_specs.append(pl.BlockSpec((None, 8, S), lambda b, h, i: (b, 0, 0)))
        args.append(cum)
    return pl.pallas_call(
        body, name="attn_fwd", grid=(B, HEADS, nq), in_specs=in_specs,
        out_specs=pl.BlockSpec((tq, HP), lambda b, h, i: (b * nq + i, h)),
        out_shape=_sds((T, HEADS * HP), BF16),
        compiler_params=_cparams(VMEM_BIG))(*args)


def attn_bwd(qa, qo, ka, ko, va, vo, cum, dcat, do_off, scale, S, out_dtype):
    T = qa.shape[0]
    B = T // S
    tq = min(ATT_TQ, S)
    nq = S // tq
    use_cum = cum is not None

    def body(*refs):
        if use_cum:
            q_ref, k_ref, v_ref, do_ref, c_ref, dq_ref, dk_ref, dv_ref, dc_ref, dk_acc, dv_acc = refs
        else:
            q_ref, k_ref, v_ref, do_ref, dq_ref, dk_ref, dv_ref, dk_acc, dv_acc = refs
        h, i = pl.program_id(1), pl.program_id(2)

        @pl.when(i == 0)
        def _():
            dk_acc[...] = jnp.zeros_like(dk_acc)
            dv_acc[...] = jnp.zeros_like(dv_acc)
            if use_cum:
                dc_ref[...] = jnp.zeros_like(dc_ref)

        for lo, hi, kw in _bands(S, tq):
            @pl.when((i >= lo) & (i < hi))
            def _():
                crow = c_ref[pl.ds(h, 1), 0:kw] if use_cum else jnp.zeros((1, kw), F32)
                f = functools.partial(_attn_block, qpos0=i * tq, scale=scale, use_cum=use_cum, n_free=lo * tq)
                _, vjp = jax.vjp(f, q_ref[...], k_ref[0:kw, :], v_ref[0:kw, :], crow)
                dq, dk, dv, dc = vjp(do_ref[...])
                dq_ref[...] = dq.astype(out_dtype)
                dk_acc[0:kw, :] += dk
                dv_acc[0:kw, :] += dv
                if use_cum:
                    dc_ref[:, 0:kw] += dc

        @pl.when(i == nq - 1)
        def _():
            dk_ref[...] = dk_acc[...].astype(out_dtype)
            dv_ref[...] = dv_acc[...].astype(out_dtype)

    qspec = pl.BlockSpec((tq, HP), lambda b, h, i: (b * nq + i, qo + h))
    in_specs = [qspec, pl.BlockSpec((S, HP), lambda b, h, i: (b, ko + h)),
                pl.BlockSpec((S, HP), lambda b, h, i: (b, vo + h)),
                pl.BlockSpec((tq, HP), lambda b, h, i: (b * nq + i, do_off + h))]
    args = [qa, ka, va, dcat]
    kv_out = pl.BlockSpec((S, HP), lambda b, h, i: (b, h))
    out_specs = [pl.BlockSpec((tq, HP), lambda b, h, i: (b * nq + i, h)), kv_out, kv_out]
    out_shape = [_sds((T, HEADS * HP), out_dtype)] * 3
    if use_cum:
        in_specs.append(pl.BlockSpec((None, 8, S), lambda b, h, i: (b, 0, 0)))
        args.append(cum)
        out_specs.append(pl.BlockSpec((None, 1, S), lambda b, h, i: (b * HEADS + h, 0, 0)))
        out_shape.append(_sds((B * HEADS, 1, S), F32))
    return pl.pallas_call(
        body, name="attn_bwd", grid=(B, HEADS, nq), in_specs=in_specs, out_specs=out_specs, out_shape=out_shape,
        scratch_shapes=[pltpu.VMEM((S, HP), F32), pltpu.VMEM((S, HP), F32)],
        compiler_params=_cparams(VMEM_BIG))(*args)


def _tri(n, upper):
    r, c = _iota((n, n), 0), _iota((n, n), 1)
    return ((r <= c) if upper else (r >= c)).astype(F32)


def fox_gate_fwd(proj, bcol, S):
    T = proj.shape[0]
    B = T // S
    ts = _tile(S, 512)
    nt = S // ts

    def body(p_ref, b_ref, o_ref, carry):
        @pl.when(pl.program_id(1) == 0)
        def _():
            carry[...] = jnp.zeros_like(carry)
        cf = jnp.transpose(p_ref[...])[0:8, :]
        lf = jax.nn.log_sigmoid(cf + b_ref[...])
        cum = _mm_hi(lf, _tri(ts, True)) + carry[...]
        o_ref[...] = cum
        carry[...] += jnp.sum(lf, axis=1, keepdims=True)

    return pl.pallas_call(
        body, name="fox_gate_fwd", grid=(B, nt),
        in_specs=[pl.BlockSpec((ts, HP), lambda b, j: (b * nt + j, C_CF // HP)), pl.BlockSpec((8, 1), lambda b, j: (0, 0))],
        out_specs=pl.BlockSpec((None, 8, ts), lambda b, j: (b, 0, j)), out_shape=_sds((B, 8, S), F32),
        scratch_shapes=[pltpu.VMEM((8, 1), F32)],
        compiler_params=_cparams(VMEM_BIG))(proj, bcol)


def fox_gate_bwd(proj, bcol, dcum, S):
    T = proj.shape[0]
    B = T // S
    ts = _tile(S, 512)
    nt = S // ts

    def body(p_ref, b_ref, dc_ref, dp_ref, db_ref, carry):
        b, j = pl.program_id(0), pl.program_id(1)

        @pl.when(j == 0)
        def _():
            carry[...] = jnp.zeros_like(carry)

        @pl.when((b == 0) & (j == 0))
        def _():
            db_ref[...] = jnp.zeros_like(db_ref)

        cf = jnp.transpose(p_ref[...])[0:8, :]
        dc = dc_ref[...]
        dlf = _mm_hi(dc, _tri(ts, False)) + carry[...]
        carry[...] += jnp.sum(dc, axis=1, keepdims=True)
        dcf = dlf * jax.nn.sigmoid(-(cf + b_ref[...]))
        db_ref[...] += jnp.sum(dcf, axis=1, keepdims=True)
        full = jnp.concatenate([dcf, jnp.zeros((HP - 8, ts), F32)], axis=0)
        dp_ref[...] = jnp.transpose(full).astype(BF16)

    def rev(b, j):
        return nt - 1 - j

    return pl.pallas_call(
        body, name="fox_gate_bwd", grid=(B, nt),
        in_specs=[pl.BlockSpec((ts, HP), lambda b, j: (b * nt + rev(b, j), C_CF // HP)),
                  pl.BlockSpec((8, 1), lambda b, j: (0, 0)),
                  pl.BlockSpec((None, 8, ts), lambda b, j: (b, 0, rev(b, j)))],
        out_specs=[pl.BlockSpec((ts, HP), lambda b, j: (b * nt + rev(b, j), 0)), pl.BlockSpec((8, 1), lambda b, j: (0, 0))],
        out_shape=[_sds((T, HP), BF16), _sds((8, 1), F32)],
        scratch_shapes=[pltpu.VMEM((8, 1), F32)],
        compiler_params=_cparams(VMEM_BIG))(proj, bcol, dcum)


def _mla_pre(blk, gq, gkv, wq, wkv, place, cos_q, sin_q, cs_k):
    nq = _rms(blk[:, 0:B_Q_LORA], gq)
    nkv = _rms(blk[:, B_Q_LORA:B_Q_LORA + B_KV_LORA], gkv)
    qq = _mm(nq, wq)
    q = qq[:, 0:HEADS * HP] * cos_q + qq[:, HEADS * HP:] * sin_q
    kv = _mm(nkv, wkv)
    k = kv[:, 0:HEADS * HP] + _mm(blk[:, B_Q_LORA + B_KV_LORA:] * cs_k, place)
    return q, k, kv[:, HEADS * HP:]


def mla_pre_fwd(proj, gq, gkv, wq, wkv, place, cos_q, sin_q, cs_k, S):
    T = proj.shape[0]
    tm = _tile(S, 512)
    tpb = S // tm
    W = HEADS * HP

    def body(p_ref, gq_ref, gkv_ref, wq_ref, wkv_ref, pl_ref, cq_ref, sq_ref, ck_ref, q_ref, k_ref, v_ref):
        q, k, v = _mla_pre(p_ref[...], gq_ref[...], gkv_ref[...], wq_ref[...], wkv_ref[...], pl_ref[...],
                           cq_ref[...], sq_ref[...], ck_ref[...])
        q_ref[...] = q
        k_ref[...] = k
        v_ref[...] = v

    def full(a):
        return pl.BlockSpec(a.shape, lambda i: (0,) * a.ndim)

    tok = pl.BlockSpec((tm, W), lambda i: (i, 0))
    return pl.pallas_call(
        body, name="mla_pre_fwd", grid=(T // tm,),
        in_specs=[pl.BlockSpec((tm, W), lambda i: (i, C_B // W)), full(gq), full(gkv), full(wq), full(wkv), full(place),
                  pl.BlockSpec((tm, W), lambda i: (i % tpb, 0)), pl.BlockSpec((tm, W), lambda i: (i % tpb, 0)),
                  pl.BlockSpec((tm, HP), lambda i: (i % tpb, 0))],
        out_specs=[tok] * 3, out_shape=[_sds((T, W), F32)] * 3,
        compiler_params=_cparams(VMEM_BIG))(proj, gq, gkv, wq, wkv, place, cos_q, sin_q, cs_k)


def mla_pre_bwd(proj, gq, gkv, wq, wkv, place, cos_q, sin_q, cs_k, dq, dk, dv, S):
    T = proj.shape[0]
    tm = _tile(S, 512)
    tpb = S // tm
    W = HEADS * HP

    def body(p_ref, gq_ref, gkv_ref, wq_ref, wkv_ref, pl_ref, cq_ref, sq_ref, ck_ref, dq_ref, dk_ref, dv_ref,
             dp_ref, dgq_ref, dgkv_ref, dwq_ref, dwkv_ref):
        @pl.when(pl.program_id(0) == 0)
        def _():
            for r in (dgq_ref, dgkv_ref, dwq_ref, dwkv_ref):
                r[...] = jnp.zeros_like(r)

        f = functools.partial(_mla_pre, place=pl_ref[...], cos_q=cq_ref[...], sin_q=sq_ref[...], cs_k=ck_ref[...])
        _, vjp = jax.vjp(f, p_ref[...], gq_ref[...], gkv_ref[...], wq_ref[...], wkv_ref[...])
        dp, dgq, dgkv, dwq, dwkv = vjp((dq_ref[...], dk_ref[...], dv_ref[...]))
        dp_ref[...] = dp.astype(BF16)
        dgq_ref[...] += dgq
        dgkv_ref[...] += dgkv
        dwq_ref[...] += dwq
        dwkv_ref[...] += dwkv

    def full(a):
        return pl.BlockSpec(a.shape, lambda i: (0,) * a.ndim)

    tok = pl.BlockSpec((tm, W), lambda i: (i, 0))
    return pl.pallas_call(
        body, name="mla_pre_bwd", grid=(T // tm,),
        in_specs=[pl.BlockSpec((tm, W), lambda i: (i, C_B // W)), full(gq), full(gkv), full(wq), full(wkv), full(place),
                  pl.BlockSpec((tm, W), lambda i: (i % tpb, 0)), pl.BlockSpec((tm, W), lambda i: (i % tpb, 0)),
                  pl.BlockSpec((tm, HP), lambda i: (i % tpb, 0)), tok, tok, tok],
        out_specs=[tok, full(gq), full(gkv), full(wq), full(wkv)],
        out_shape=[_sds((T, W), BF16), _sds(gq.shape, F32), _sds(gkv.shape, F32), _sds(wq.shape, F32), _sds(wkv.shape, F32)],
        compiler_params=_cparams(VMEM_BIG))(proj, gq, gkv, wq, wkv, place, cos_q, sin_q, cs_k, dq, dk, dv)


def _gmlp_block(blk, lg, lb, ws, bs):
    u = jax.nn.gelu(blk[:, 0:GW])
    v = jax.nn.gelu(blk[:, GW:2 * GW])
    mu = jnp.mean(v, axis=-1, keepdims=True)
    var = jnp.mean(jnp.square(v - mu), axis=-1, keepdims=True)
    vn = (v - mu) * lax.rsqrt(var + LN_EPS) * lg + lb
    causal = _iota((D_CHUNK, D_CHUNK), 1) <= _iota((D_CHUNK, D_CHUNK), 0)
    group = _iota((1, GW), 1) // HD
    mixed = jnp.zeros((D_CHUNK, GW), F32)
    for g in range(HEADS):
        part = _mm(jnp.where(causal, ws[g], 0.0), vn) + bs[g]
        mixed = mixed + jnp.where(group == g, part, 0.0)
    return u * mixed


def gmlp_fwd(proj, lg, lb, ws, bs):
    T = proj.shape[0]

    def body(p_ref, lg_ref, lb_ref, ws_ref, bs_ref, o_ref):
        o_ref[...] = _gmlp_block(p_ref[...], lg_ref[...], lb_ref[...], ws_ref[...], bs_ref[...]).astype(BF16)

    def full(a):
        return pl.BlockSpec(a.shape, lambda i: (0,) * a.ndim)

    return pl.pallas_call(
        body, name="gmlp_fwd", grid=(T // D_CHUNK,),
        in_specs=[pl.BlockSpec((D_CHUNK, 2 * GW), lambda i: (i, C_D // (2 * GW))), full(lg), full(lb), full(ws), full(bs)],
        out_specs=pl.BlockSpec((D_CHUNK, GW), lambda i: (i, 0)), out_shape=_sds((T, GW), BF16),
        compiler_params=_cparams(VMEM_BIG))(proj, lg, lb, ws, bs)


def gmlp_bwd(proj, lg, lb, ws, bs, dcat):
    T = proj.shape[0]

    def body(p_ref, lg_ref, lb_ref, ws_ref, bs_ref, do_ref, dp_ref, dlg_ref, dlb_ref, dws_ref, dbs_ref):
        @pl.when(pl.program_id(0) == 0)
        def _():
            for r in (dlg_ref, dlb_ref, dws_ref, dbs_ref):
                r[...] = jnp.zeros_like(r)

        _, vjp = jax.vjp(_gmlp_block, p_ref[...], lg_ref[...], lb_ref[...], ws_ref[...], bs_ref[...])
        dp, dlg, dlb, dws, dbs = vjp(do_ref[...])
        dp_ref[...] = dp.astype(BF16)
        dlg_ref[...] += dlg
        dlb_ref[...] += dlb
        dws_ref[...] += dws
        dbs_ref[...] += dbs

    def full(a):
        return pl.BlockSpec(a.shape, lambda i: (0,) * a.ndim)

    return pl.pallas_call(
        body, name="gmlp_bwd", grid=(T // D_CHUNK,),
        in_specs=[pl.BlockSpec((D_CHUNK, 2 * GW), lambda i: (i, C_D // (2 * GW))), full(lg), full(lb), full(ws), full(bs),
                  pl.BlockSpec((D_CHUNK, GW), lambda i: (i, O_D // GW))],
        out_specs=[pl.BlockSpec((D_CHUNK, 2 * GW), lambda i: (i, 0)), full(lg), full(lb), full(ws), full(bs)],
        out_shape=[_sds((T, 2 * GW), BF16), _sds(lg.shape, F32), _sds(lb.shape, F32), _sds(ws.shape, F32), _sds(bs.shape, F32)],
        compiler_params=_cparams(VMEM_BIG))(proj, lg, lb, ws, bs, dcat)


def _lb_all(logits):
    m = jnp.max(logits, axis=0, keepdims=True)
    e = jnp.exp(logits - m)
    sm = e / jnp.sum(e, axis=0, keepdims=True)
    return jnp.concatenate([sm[0:1] - sm[0:1], (sm[0:1] + sm[1:2]) - sm[0:1]], axis=0)


def lb_fwd(logits):
    def body(l_ref, o_ref):
        o_ref[...] = _lb_all(l_ref[...])

    return pl.pallas_call(body, name="lb_fwd", out_shape=_sds(logits.shape, F32))(logits)


def lb_bwd(logits, dlb):
    def body(l_ref, d_ref, o_ref):
        _, vjp = jax.vjp(_lb_all, l_ref[...])
        o_ref[...] = vjp(d_ref[...])[0]

    return pl.pallas_call(body, name="lb_bwd", out_shape=_sds(logits.shape, F32))(logits, dlb)


def ada_grad(c_all, dmod_cols):
    N = dmod_cols.shape[1]
    tn = _tile(N, 1152)

    def body(c_ref, d_ref, o_ref):
        h = jax.nn.silu(c_ref[...]).astype(BF16)
        o_ref[...] = lax.dot_general(h, d_ref[...].astype(BF16), (((0,), (0,)), ((), ())), preferred_element_type=F32)

    nb = c_all.shape[0]
    return pl.pallas_call(
        body, name="ada_grad", grid=(N // tn,),
        in_specs=[pl.BlockSpec((nb, D), lambda j: (0, 0)), pl.BlockSpec((nb, tn), lambda j: (0, j))],
        out_specs=pl.BlockSpec((D, tn), lambda j: (0, j)), out_shape=_sds((D, N), F32),
        compiler_params=_cparams(VMEM_BIG))(c_all, dmod_cols)


def sum_slots(a, n, name):
    _, R, C = a.shape
    tr = _row_tile(R, C, n)

    def body(a_ref, o_ref):
        acc = a_ref[0]
        for k in range(1, n):
            acc = acc + a_ref[k]
        o_ref[...] = acc

    return pl.pallas_call(
        body, name=name, grid=(R // tr,),
        in_specs=[pl.BlockSpec((n, tr, C), lambda i: (0, i, 0))],
        out_specs=pl.BlockSpec((tr, C), lambda i: (i, 0)), out_shape=_sds((R, C), F32),
        compiler_params=_cparams(VMEM_BIG))(a)


def add2(a, b, name):
    shp = a.shape
    C = shp[-1]
    a2, b2 = a.reshape(-1, C), b.reshape(-1, C)
    R = a2.shape[0]
    tr = _row_tile(R, C)

    def body(a_ref, b_ref, o_ref):
        o_ref[...] = a_ref[...] + b_ref[...]

    spec = pl.BlockSpec((tr, C), lambda i: (i, 0))
    return pl.pallas_call(body, name=name, grid=(R // tr,), in_specs=[spec, spec], out_specs=spec,
                          out_shape=_sds((R, C), F32), compiler_params=_cparams(VMEM_BIG))(a2, b2).reshape(shp)


def _row_tile(R, C=D, n=1, mult=8, elems=1 << 18):
    limit = max(mult, elems // (C * n))
    for t in range(limit - limit % mult, mult - 1, -mult):
        if R % t == 0:
            return t
    return R


def adamw(w, g, m, v, name, echo=False):
    R, C = w.shape
    tr = _row_tile(R, C, elems=1 << 19)
    c1 = 1.0 - ADAM_B1 ** ADAM_STEP
    c2 = 1.0 - ADAM_B2 ** ADAM_STEP
    n_out = 4 if echo else 3

    def body(w_ref, g_ref, m_ref, v_ref, d_ref, nm_ref, nv_ref, *g_out):
        g_ = g_ref[...]
        nm = ADAM_B1 * m_ref[...] + (1.0 - ADAM_B1) * g_
        nv = ADAM_B2 * v_ref[...] + (1.0 - ADAM_B2) * jnp.square(g_)
        d_ref[...] = -ADAM_LR * ((nm / c1) / (jnp.sqrt(nv / c2) + ADAM_EPS) + ADAM_WD * w_ref[...])
        nm_ref[...] = nm
        nv_ref[...] = nv
        if echo:
            g_out[0][...] = g_

    spec = pl.BlockSpec((tr, C), lambda i: (i, 0))
    return pl.pallas_call(body, name=name, grid=(R // tr,), in_specs=[spec] * 4, out_specs=[spec] * n_out,
                          out_shape=[_sds((R, C), F32)] * n_out, compiler_params=_cparams(VMEM_BIG))(w, g, m, v)


def _rot_cols(w):
    return jnp.concatenate([-w[:, 16:32], w[:, 0:16]], axis=1)


def _fold_rot(d):
    return jnp.concatenate([d[:, 16:32], -d[:, 0:16]], axis=1)


def _pad_heads(w, off, axis):
    parts = []
    for h in range(HEADS):
        piece = lax.slice_in_dim(w, off + HD * h, off + HD * (h + 1), axis=axis)
        parts += [piece, jnp.zeros_like(piece)]
    return parts


def _unpad_heads(d, off, axis):
    return [lax.slice_in_dim(d, off + HP * h, off + HP * h + HD, axis=axis) for h in range(HEADS)]


def mix_in_ext(w):
    z = lambda n: jnp.zeros((w.shape[0], n), w.dtype)
    kr = w[:, 1408:1440]
    cols = [w[:, 0:1408], kr, _rot_cols(kr), z(64)]
    cols += _pad_heads(w, 1440, 1) + _pad_heads(w, 1696, 1) + _pad_heads(w, 1952, 1)
    cols += [w[:, 2212:2724], w[:, 2208:2212], z(NP - C_CF - HEADS)]
    return jnp.concatenate(cols, axis=1)


def mix_in_unext(d):
    kr = d[:, 1408:1440] + _fold_rot(d[:, 1440:1472])
    cols = [d[:, 0:1408], kr] + _unpad_heads(d, C_CQ, 1) + _unpad_heads(d, C_CK, 1) + _unpad_heads(d, C_CV, 1)
    cols += [d[:, C_CF:C_CF + HEADS], d[:, C_D:C_D + 2 * GW]]
    return jnp.concatenate(cols, axis=1)


def mix_out_ext(w):
    return jnp.concatenate([w[0:GW]] + _pad_heads(w, GW, 0) + _pad_heads(w, 2 * GW, 0) + [w[3 * GW:4 * GW]], axis=0)


def mix_out_unext(d):
    return jnp.concatenate([d[0:GW]] + _unpad_heads(d, O_B, 0) + _unpad_heads(d, O_C, 0) + [d[O_D:O_D + GW]], axis=0)


def uq_ext(w):
    z = lambda n: jnp.zeros((w.shape[0], n), w.dtype)
    a, b = [], []
    for h in range(HEADS):
        o = (B_NOPE + B_ROPE) * h
        a += [w[:, o:o + B_NOPE + B_ROPE], z(32)]
        b += [z(B_NOPE), _rot_cols(w[:, o + B_NOPE:o + B_NOPE + B_ROPE]), z(32)]
    return jnp.concatenate(a + b, axis=1)


def uq_unext(d):
    cols = []
    for h in range(HEADS):
        o = HP * h
        cols += [d[:, o:o + B_NOPE], d[:, o + B_NOPE:o + B_NOPE + B_ROPE]
                 + _fold_rot(d[:, HEADS * HP + o + B_NOPE:HEADS * HP + o + B_NOPE + B_ROPE])]
    return jnp.concatenate(cols, axis=1)


def ukv_ext(w):
    z = jnp.zeros((w.shape[0], HD), w.dtype)
    k, v = [], []
    for h in range(HEADS):
        k += [w[:, 2 * HD * h:2 * HD * h + HD], z]
        v += [w[:, 2 * HD * h + HD:2 * HD * (h + 1)], z]
    return jnp.concatenate(k + v, axis=1)


def ukv_unext(d):
    cols = []
    for h in range(HEADS):
        cols += [d[:, HP * h:HP * h + HD], d[:, HEADS * HP + HP * h:HEADS * HP + HP * h + HD]]
    return jnp.concatenate(cols, axis=1)


def rope_tables(S):
    half = B_ROPE // 2
    inv_freq = ROPE_THETA ** (-jnp.arange(half, dtype=F32) / half)
    ang = jnp.arange(S).astype(F32)[:, None] * inv_freq[None, :]
    cos = jnp.tile(jnp.cos(ang), (1, 2))
    sin = jnp.tile(jnp.sin(ang), (1, 2))
    one, zero = jnp.ones((S, B_NOPE), F32), jnp.zeros((S, B_NOPE), F32)
    z32 = jnp.zeros((S, 32), F32)
    cos_q = jnp.tile(jnp.concatenate([one, cos, z32], axis=1), (1, HEADS))
    sin_q = jnp.tile(jnp.concatenate([zero, sin, z32], axis=1), (1, HEADS))
    cs_k = jnp.concatenate([cos, sin, zero], axis=1)
    place = np.zeros((HP, HEADS * HP), np.float32)
    for h in range(HEADS):
        for j in range(B_ROPE):
            place[j, h * HP + B_NOPE + j] = 1.0
            place[B_ROPE + j, h * HP + B_NOPE + j] = 1.0
    return cos_q, sin_q, cs_k, jnp.asarray(place, BF16)


def layer_fwd(x, mod, get, tabs, S):
    cos_q, sin_q, cs_k, place = tabs
    p = dict(get("ffn1", x))
    l = p["wl"]
    zg1, zu1, act1 = ffn_in_fwd(x, mod, p["ffn1_in"], l, 0, 1, S)
    y1, x1 = out_ln_fwd(act1, p["ffn1_out"], x, mod, p["ln_g"][0:1], p["ln_b"][0:1], 2, 0.5, S, l)
    p.update(get("mix", x1))
    proj = mix_in_fwd(x1, mod, p["mix_in"], 3, 4, S)
    o_a, states = hgrn_fwd(proj, p["lb"], p["ng"], S)
    q_b, k_b, v_b = mla_pre_fwd(proj, p["gq"], p["gkv"], p["wq"], p["wkv"], place, cos_q, sin_q, cs_k, S)
    o_b = attn_fwd(q_b, 0, k_b, 0, v_b, 0, None, (B_NOPE + B_ROPE) ** -0.5, S)
    cum = fox_gate_fwd(proj, p["bcol"], S)
    o_c = attn_fwd(proj, C_CQ // HP, proj, C_CK // HP, proj, C_CV // HP, cum, HD ** -0.5, S)
    o_d = gmlp_fwd(proj, p["g_lg"], p["g_lb"], p["ws"], p["bs"])
    cat = jnp.concatenate([o_a, o_b, o_c, o_d], axis=1)
    y2, x2 = out_ln_fwd(cat, p["mix_out"], x1, mod, p["ln_g"][1:2], p["ln_b"][1:2], 5, 1.0, S)
    p.update(get("ffn2", x2))
    zg3, zu3, act3 = ffn_in_fwd(x2, mod, p["ffn2_in"], l, 6, 7, S)
    y3, x3 = out_ln_fwd(act3, p["ffn2_out"], x2, mod, p["ln_g"][2:3], p["ln_b"][2:3], 8, 0.5, S, l)
    saved = dict(x=x, zg1=zg1, zu1=zu1, act1=act1, y1=y1, x1=x1, proj=proj, states=states, q_b=q_b, k_b=k_b, v_b=v_b,
                 cum=cum, cat=cat, y2=y2, x2=x2, zg3=zg3, zu3=zu3, act3=act3, y3=y3, p=p)
    return x3, saved


def _ffn_bwd(dxn, x_in, y, zg, zu, act, mod, w_in, w_out, l, lg, lb, idx, S, emit):
    sh, sc, gate = idx
    dres, dy, dgate, dlg, dlb = ln_res_bwd(dxn, x_in, y, mod, lg, lb, gate, 0.5, S)
    dzg, dzu = swiglu_bwd(dy, w_out, l, zg, zu, S)
    dw_out = tn_mm(act, dy, FF // 2).reshape(N_CHIPS, FF // N_CHIPS, D)
    dw_in = tn_mm_mod_shards(x_in, mod, dzg, dzu, sh, sc, S)
    mod = mod + emit(dw_in, dw_out)
    dx, dsh, dsc = nt_mod_bwd([dzg, dzu], w_in, [0, 2], x_in, mod, dres, sc, S, FF // 2, l)
    return dx, dw_in, dw_out, dlg, dlb, {sh: dsh, sc: dsc, gate: dgate}, mod


def layer_bwd(dx3, mod, sv, tabs, S, emit):
    cos_q, sin_q, cs_k, place = tabs
    p = sv["p"]
    l = p["wl"]
    g = {}
    dm = {}

    def emit_ffn(part):
        def f(dw_in, dw_out):
            g[part + "_in"], g[part + "_out"] = dw_in, dw_out
            return emit(part, g)
        return f

    dx2, _, _, dlg2, dlb2, d, mod = _ffn_bwd(
        dx3, sv["x2"], sv["y3"], sv["zg3"], sv["zu3"], sv["act3"], mod, p["ffn2_in"], p["ffn2_out"], l,
        p["ln_g"][2:3], p["ln_b"][2:3], (6, 7, 8), S, emit_ffn("ffn2"))
    dm.update(d)
    dres, dy2, dm[5], dlg1, dlb1 = ln_res_bwd(dx2, sv["x1"], sv["y2"], mod, p["ln_g"][1:2], p["ln_b"][1:2], 5, 1.0, S)
    dcat = nt_plain(dy2, p["mix_out"])
    g["mix_out"] = tn_mm(sv["cat"], dy2, 768)
    proj = sv["proj"]
    d_a, g["lb"], g["ng"] = hgrn_bwd(proj, sv["states"], dcat, p["lb"], p["ng"], S)
    dq_c, dk_c, dv_c, dcum = attn_bwd(proj, C_CQ // HP, proj, C_CK // HP, proj, C_CV // HP, sv["cum"], dcat,
                                      O_C // HP, HD ** -0.5, S, BF16)
    B = proj.shape[0] // S
    dcum = jnp.concatenate([dcum.reshape(B, HEADS, S), jnp.zeros((B, 8 - HEADS, S), F32)], axis=1)
    d_cf, g["bcol"] = fox_gate_bwd(proj, p["bcol"], dcum, S)
    dq_b, dk_b, dv_b = attn_bwd(sv["q_b"], 0, sv["k_b"], 0, sv["v_b"], 0, None, dcat, O_B // HP,
                                (B_NOPE + B_ROPE) ** -0.5, S, F32)
    d_b, g["gq"], g["gkv"], g["wq"], g["wkv"] = mla_pre_bwd(
        proj, p["gq"], p["gkv"], p["wq"], p["wkv"], place, cos_q, sin_q, cs_k, dq_b, dk_b, dv_b, S)
    d_d, g["g_lg"], g["g_lb"], g["ws"], g["bs"] = gmlp_bwd(proj, p["g_lg"], p["g_lb"], p["ws"], p["bs"], dcat)
    dproj = jnp.concatenate([d_a, d_b, dq_c, dk_c, dv_c, d_d, d_cf, jnp.zeros_like(d_cf)], axis=1)
    g["mix_in"] = tn_mm_mod(sv["x1"], mod, dproj, 3, 4, S, NP_TILE)
    mod = mod + emit("mix", g)
    dx1, dm[3], dm[4] = nt_mod_bwd([dproj], p["mix_in"], [0], sv["x1"], mod, dres, 4, S, NP_TILE)
    last = []

    def emit_last(dw_in, dw_out):
        last.append(emit_ffn("ffn1")(dw_in, dw_out))
        return last[0]

    dx0, _, _, dlg0, dlb0, d, mod = _ffn_bwd(
        dx1, sv["x"], sv["y1"], sv["zg1"], sv["zu1"], sv["act1"], mod, p["ffn1_in"], p["ffn1_out"], l,
        p["ln_g"][0:1], p["ln_b"][0:1], (0, 1, 2), S, emit_last)
    dm.update(d)
    g["ln_g"] = jnp.concatenate([dlg0, dlg1, dlg2], axis=0)
    g["ln_b"] = jnp.concatenate([dlb0, dlb1, dlb2], axis=0)
    dmod = jnp.concatenate([dm[i] for i in range(N_MOD)], axis=1)
    return dx0, dmod, g, last[0]


def local_step(x, c8, tgt, get, lb_logits, S, emit=None):
    B = x.shape[0] // S
    tabs = rope_tables(S)
    lb_all = lb_fwd(lb_logits)
    mods, saved = [], []
    h = x
    for l in range(DEPTH):
        pa = get(l, "ada", h)
        mod = mod_fwd(c8, pa["ada_w"], pa["wl"], pa["ada_b"])[0:B].reshape(B, N_MOD, D)

        def get_l(part, after, l=l):
            p = dict(get(l, part, after))
            if part == "mix":
                p["lb"] = lb_all[l:l + 1]
            return p

        h, sv = layer_fwd(h, mod, get_l, tabs, S)
        mods.append(mod)
        saved.append(sv)
    loss_tile, dh = loss_head(h, tgt)
    grads, dmods, dlb = [None] * DEPTH, [None] * DEPTH, [None] * DEPTH
    tie = jnp.zeros((), F32)
    for l in reversed(range(DEPTH)):
        emit_l = (lambda part, g: jnp.zeros((), F32)) if emit is None else functools.partial(emit, l)
        dh, dmods[l], grads[l], tie = layer_bwd(dh, mods[l] + tie, saved[l], tabs, S, emit_l)
        dlb[l] = grads[l].pop("lb")
    d_logits = lb_bwd(lb_logits, jnp.concatenate(dlb, axis=0))
    return loss_tile, dh, dmods, grads, d_logits


ANY = pl.BlockSpec(memory_space=pl.ANY)


def _place():
    x, y, c = lax.axis_index("x"), lax.axis_index("y"), lax.axis_index("c")
    chips = [(1 - x, y), (x, 1 - y), (1 - x, 1 - y)]
    return x, y, c, chips


def _rcopy(src, dst, sems, k, to):
    send_sems, recv_sems = sems
    return pltpu.make_async_remote_copy(src_ref=src, dst_ref=dst, send_sem=send_sems.at[k], recv_sem=recv_sems.at[k],
                                        device_id=to, device_id_type=MESH)


def _dma_sems(n_remote, n_local):
    return [pltpu.SemaphoreType.DMA((n_remote,)), pltpu.SemaphoreType.DMA((n_remote,)), pltpu.SemaphoreType.DMA((n_local,))]


def own_slot(src, chip):
    L = src.shape[0]
    return lax.dynamic_update_slice(lax.empty((L, N_CHIPS) + src.shape[1:], src.dtype), src[:, None], (0, chip, 0, 0))


def ag_shards(arrs, lands):
    n = len(arrs)
    rh = [a.shape[1] // 2 for a in arrs]

    def body(*refs):
        srcs, outs, token = refs[:n], refs[2 * n:3 * n], refs[3 * n]
        send_sems, recv_sems = refs[3 * n + 1:]
        x, y, c, chips = _place()
        sems = (send_sems, recv_sems)
        me = 2 * x + y
        sibling = (x, y, 1 - c)
        token[...] = jnp.zeros_like(token)

        def part(i, k, hc):
            return outs[i].at[:, k, pl.ds(hc * rh[i], rh[i]), :]

        started = []
        for j, (px, py) in enumerate(chips):
            for i in range(n):
                cp = _rcopy(srcs[i].at[:, pl.ds(c * rh[i], rh[i]), :], part(i, me, c), sems, 6 * i + j, (px, py, c))
                cp.start()
                started.append(cp)
        for j, (px, py) in enumerate(chips):
            k = 2 * px + py
            for i in range(n):
                _rcopy(part(i, k, c), part(i, k, c), sems, 6 * i + j, (px, py, c)).wait_recv()
                cp = _rcopy(part(i, k, c), part(i, k, c), sems, 6 * i + 3 + j, sibling)
                cp.start()
                started.append(cp)
        for j, (px, py) in enumerate(chips):
            k = 2 * px + py
            for i in range(n):
                _rcopy(part(i, k, 1 - c), part(i, k, 1 - c), sems, 6 * i + 3 + j, sibling).wait_recv()
        for cp in started:
            cp.wait_send()

    outs = pl.pallas_call(
        body, name="ag_shards", out_shape=[_sds(a.shape, a.dtype) for a in lands] + [_sds((8, 128), F32)],
        in_specs=[ANY] * (2 * n), out_specs=[ANY] * n + [pl.BlockSpec(memory_space=pltpu.VMEM)],
        input_output_aliases={n + i: i for i in range(n)}, scratch_shapes=_dma_sems(6 * n, 1)[:2])(*arrs, *lands)
    return list(outs[:n]), outs[n]


HBM_SPEC = pl.BlockSpec(memory_space=pltpu.HBM)
SEM_SPEC = pl.BlockSpec(memory_space=pltpu.SEMAPHORE)
DATAFLOW = pltpu.SideEffectType.DATAFLOW_SIDE_EFFECTING


def _after(x, dep):
    return lax.optimization_barrier((x, dep))[0]


def _split_start(srcs, lands, copies, n_copies, dep, name):
    n, m = len(srcs), len(lands)

    def body(*refs):
        ins = refs[:n + m]
        send_sems, recv_sems = refs[n + m + 1], refs[n + m + 2]
        token = refs[-1]
        for k, (src, dst, to) in enumerate(copies(ins[:n], ins[n:], _place())):
            pltpu.make_async_remote_copy(src_ref=src, dst_ref=dst, send_sem=send_sems.at[k], recv_sem=recv_sems.at[k],
                                         device_id=to, device_id_type=MESH).start()
        token[...] = jnp.zeros_like(token)

    arrs = list(srcs) + list(lands)
    outs = pl.pallas_call(
        body, name=name,
        out_shape=(pltpu.SemaphoreType.DMA((n_copies,)), pltpu.SemaphoreType.DMA((n_copies,)),
                   *[pltpu.HBM(a.shape, a.dtype) for a in arrs], _sds((8, 128), F32)),
        in_specs=[HBM_SPEC] * (n + m) + [ANY],
        out_specs=(SEM_SPEC, SEM_SPEC, *[HBM_SPEC] * (n + m), pl.BlockSpec(memory_space=pltpu.VMEM)),
        input_output_aliases={i: 2 + i for i in range(n + m)},
        compiler_params=pltpu.CompilerParams(has_side_effects=DATAFLOW),
    )(*[pltpu.with_memory_space_constraint(a, pltpu.HBM) for a in arrs], dep)
    return outs[0], outs[1], list(outs[2:2 + n]), list(outs[2 + n:2 + n + m]), outs[-1]


def _split_wait(handle, arrivals, after, name):
    send_sems, recv_sems, srcs, lands, _ = handle
    n, m = len(srcs), len(lands)

    def body(*refs):
        ins = refs[:n + m]
        send_sems, recv_sems = refs[n + m], refs[n + m + 1]
        x, y, c, chips = place = _place()
        for k, (src, dst) in enumerate(arrivals(ins[:n], ins[n:], place)):
            cp = pltpu.make_async_remote_copy(src_ref=src, dst_ref=dst, send_sem=send_sems.at[k], recv_sem=recv_sems.at[k],
                                              device_id=(x, y, 1 - c), device_id_type=MESH)
            cp.wait_send()
            cp.wait_recv()

    arrs = list(srcs) + list(lands)
    outs = pl.pallas_call(
        body, name=name, out_shape=[pltpu.HBM(a.shape, a.dtype) for a in arrs],
        in_specs=[HBM_SPEC] * (n + m) + [SEM_SPEC, SEM_SPEC, ANY], out_specs=[HBM_SPEC] * (n + m),
        input_output_aliases={i: i for i in range(n + m)},
        compiler_params=pltpu.CompilerParams(has_side_effects=DATAFLOW),
    )(*arrs, send_sems, recv_sems, after)
    return list(outs[:n]), list(outs[n:])


def _ag_part(ref, k, hc):
    rh = ref.shape[2] // 2
    return ref.at[:, k, pl.ds(hc * rh, rh), :]


def ag_start(srcs, lands, dep, name):
    def copies(s, d, place):
        x, y, c, chips = place
        out = []
        for j, (px, py) in enumerate(chips):
            for i in range(len(s)):
                rh = s[i].shape[1] // 2
                out.append((s[i].at[:, pl.ds(c * rh, rh), :], _ag_part(d[i], 2 * x + y, c), (px, py, c)))
        return out

    return _split_start(srcs, lands, copies, 3 * len(srcs), dep, name)


def ag_wait(handle, after, name):
    def arrivals(s, d, place):
        x, y, c, chips = place
        out = []
        for j, (px, py) in enumerate(chips):
            for i in range(len(s)):
                rh = s[i].shape[1] // 2
                out.append((s[i].at[:, pl.ds(c * rh, rh), :], _ag_part(d[i], 2 * px + py, c)))
        return out

    return _split_wait(handle, arrivals, after, name)


def ag_forward(lands, name):
    n = len(lands)

    def body(*refs):
        bufs, token = refs[n:2 * n], refs[2 * n]
        send_sems, recv_sems = refs[2 * n + 1:]
        x, y, c, chips = _place()
        sems = (send_sems, recv_sems)
        token[...] = jnp.zeros_like(token)
        cps = []
        for j, (px, py) in enumerate(chips):
            for i in range(n):
                part = _ag_part(bufs[i], 2 * px + py, c)
                cps.append(_rcopy(part, part, sems, 3 * i + j, (x, y, 1 - c)))
        for cp in cps:
            cp.start()
        for j, (px, py) in enumerate(chips):
            for i in range(n):
                part = _ag_part(bufs[i], 2 * px + py, 1 - c)
                _rcopy(part, part, sems, 3 * i + j, (x, y, 1 - c)).wait_recv()
        for cp in cps:
            cp.wait_send()

    outs = pl.pallas_call(
        body, name=name, out_shape=[_sds(a.shape, a.dtype) for a in lands] + [_sds((8, 128), F32)],
        in_specs=[ANY] * n, out_specs=[ANY] * n + [pl.BlockSpec(memory_space=pltpu.VMEM)],
        input_output_aliases={i: i for i in range(n)}, scratch_shapes=_dma_sems(3 * n, 1)[:2])(*lands)
    return list(outs[:n]), outs[n]


def rs_start(hs, lands, dep, name):
    def copies(s, d, place):
        x, y, c, chips = place
        return [(s[i].at[2 * px + py], d[i].at[2 * x + y], (px, py, c)) for j, (px, py) in enumerate(chips) for i in range(len(s))]

    return _split_start(hs, lands, copies, 3 * len(hs), dep, name)


def _kept_out(ref, c):
    rh = ref.shape[1] // 2
    return ref.at[:, pl.ds((1 - c) * rh, rh), :]


def swap_start(arrs, lands, dep, name):
    def copies(s, d, place):
        x, y, c, _ = place
        return [(_kept_out(s[i], c), d[i], (x, y, 1 - c)) for i in range(len(s))]

    return _split_start(arrs, lands, copies, len(arrs), dep, name)


def swap_wait(handle, after, name):
    def arrivals(s, d, place):
        x, y, c, _ = place
        return [(_kept_out(s[i], c), d[i]) for i in range(len(s))]

    return _split_wait(handle, arrivals, after, name)


def rs_wait(handle, after, name):
    def arrivals(s, d, place):
        x, y, c, chips = place
        return [(s[i].at[2 * px + py], d[i].at[2 * px + py]) for j, (px, py) in enumerate(chips) for i in range(len(s))]

    return _split_wait(handle, arrivals, after, name)


def sibling_swap(arrs, name):
    n = len(arrs)
    rh = [a.shape[1] // 2 for a in arrs]

    def body(*refs):
        srcs, outs = refs[:n], refs[n:2 * n]
        send_sems, recv_sems = refs[2 * n:]
        x, y, c, _ = _place()
        cps = [_rcopy(srcs[i].at[:, pl.ds((1 - c) * rh[i], rh[i]), :], outs[i], (send_sems, recv_sems), i, (x, y, 1 - c))
               for i in range(n)]
        for cp in cps:
            cp.start()
        for cp in cps:
            cp.wait()

    return pl.pallas_call(
        body, name=name, out_shape=[_sds((N_CHIPS, r, a.shape[2]), a.dtype) for a, r in zip(arrs, rh)],
        in_specs=[ANY] * n, out_specs=[ANY] * n, scratch_shapes=_dma_sems(n, 1)[:2])(*arrs)


def chip_exchange(hs):
    n = len(hs)

    def body(*refs):
        srcs, outs = refs[:n], refs[n:2 * n]
        send_sems, recv_sems, loc_sems = refs[2 * n:]
        x, y, c, chips = _place()
        sems = (send_sems, recv_sems)
        me = 2 * x + y
        mine = [pltpu.make_async_copy(srcs[i].at[me], outs[i].at[me], loc_sems.at[i]) for i in range(n)]
        for cp in mine:
            cp.start()
        sends = []
        for j, (px, py) in enumerate(chips):
            for i in range(n):
                cp = _rcopy(srcs[i].at[2 * px + py], outs[i].at[me], sems, 3 * i + j, (px, py, c))
                cp.start()
                sends.append(cp)
        for j, (px, py) in enumerate(chips):
            for i in range(n):
                _rcopy(srcs[i].at[2 * px + py], outs[i].at[2 * px + py], sems, 3 * i + j, (px, py, c)).wait_recv()
        for cp in sends:
            cp.wait_send()
        for cp in mine:
            cp.wait()

    return pl.pallas_call(
        body, name="chip_exchange", out_shape=[_sds(h.shape, h.dtype) for h in hs],
        in_specs=[ANY] * n, out_specs=[ANY] * n, scratch_shapes=_dma_sems(3 * n, n))(*hs)


def sum_into(land, base, l, core, name):
    _, rh, C = land.shape
    tr = _row_tile(rh, C, N_CHIPS, mult=16)
    nr = rh // tr

    def body(core_ref, land_ref, base_ref, o_ref):
        acc = land_ref[0].astype(F32)
        for k in range(1, N_CHIPS):
            acc = acc + land_ref[k].astype(F32)
        o_ref[...] = acc

    grid_spec = pltpu.PrefetchScalarGridSpec(
        num_scalar_prefetch=1, grid=(nr,),
        in_specs=[pl.BlockSpec((N_CHIPS, tr, C), lambda r, core_ref: (0, r, 0)), ANY],
        out_specs=pl.BlockSpec((None, tr, C), lambda r, core_ref: (l, core_ref[0] * nr + r, 0)))
    return pl.pallas_call(body, name=name, grid_spec=grid_spec, out_shape=_sds(base.shape, base.dtype),
                          input_output_aliases={2: 0}, compiler_params=_cparams(VMEM_BIG))(
        core.reshape(1).astype(jnp.int32), land, base)


def sibling_join(bases, name):
    n = len(bases)

    def body(*refs):
        bufs = refs[n:2 * n]
        send_sems, recv_sems = refs[2 * n:]
        x, y, c, _ = _place()
        sems = (send_sems, recv_sems)

        def half(i, hc):
            rh = bufs[i].shape[1] // 2
            return bufs[i].at[:, pl.ds(hc * rh, rh), :]

        sends = [_rcopy(half(i, c), half(i, c), sems, i, (x, y, 1 - c)) for i in range(n)]
        for cp in sends:
            cp.start()
        for i in range(n):
            _rcopy(half(i, 1 - c), half(i, 1 - c), sems, i, (x, y, 1 - c)).wait_recv()
        for cp in sends:
            cp.wait_send()

    return pl.pallas_call(
        body, name=name, out_shape=[_sds(b.shape, b.dtype) for b in bases], in_specs=[ANY] * n, out_specs=[ANY] * n,
        input_output_aliases={i: i for i in range(n)}, scratch_shapes=_dma_sems(n, 1)[:2])(*bases)


def ag_all(blk):
    M, C = blk.shape

    def body(x_ref, out_ref, send_sems, recv_sems, loc_sem):
        x, y, c, chips = _place()
        sems = (send_sems, recv_sems)
        me, sibling = (x, y, c), (x, y, 1 - c)

        def slot(px, py, pc):
            return out_ref.at[4 * px + 2 * py + pc]

        mine = pltpu.make_async_copy(x_ref, slot(*me), loc_sem)
        mine.start()
        first = [_rcopy(x_ref, slot(*me), sems, 0, sibling)]
        first += [_rcopy(x_ref, slot(*me), sems, 1 + j, (*chip, c)) for j, chip in enumerate(chips)]
        for cp in first:
            cp.start()
        passed = [_rcopy(slot(*chip, c), slot(*chip, c), sems, 4 + j, sibling) for j, chip in enumerate(chips)]
        for j, chip in enumerate(chips):
            _rcopy(slot(*chip, c), slot(*chip, c), sems, 1 + j, me).wait_recv()
            passed[j].start()
        _rcopy(slot(*sibling), slot(*sibling), sems, 0, me).wait_recv()
        for j, chip in enumerate(chips):
            _rcopy(slot(*chip, 1 - c), slot(*chip, 1 - c), sems, 4 + j, me).wait_recv()
        for cp in first + passed:
            cp.wait_send()
        mine.wait()

    return pl.pallas_call(
        body, name="ag_all", out_shape=_sds((8, M, C), blk.dtype),
        in_specs=[pl.BlockSpec(memory_space=pltpu.VMEM)], out_specs=pl.BlockSpec(memory_space=pltpu.VMEM),
        scratch_shapes=[pltpu.SemaphoreType.DMA((7,)), pltpu.SemaphoreType.DMA((7,)), pltpu.SemaphoreType.DMA(())],
        compiler_params=_cparams(VMEM_BIG))(blk)


WEIGHTS = ["ada_w", "ada_b", "ln_g", "ln_b", "ffn1_w_in", "ffn1_w_out", "ffn2_w_in", "ffn2_w_out", "mix_w_in", "mix_w_out",
           "hgrn_lb_logits", "hgrn_norm_g", "mla_q_norm_g", "mla_kv_norm_g", "mla_w_uq", "mla_w_ukv", "fox_b_f",
           "gmlp_ln_g", "gmlp_ln_b", "gmlp_w_s", "gmlp_b_s"]
SHARDED = {"ffn1_w_in": 1, "ffn1_w_out": 0, "ffn2_w_in": 1, "ffn2_w_out": 0, "mix_w_in": 1, "mix_w_out": 0,
           "mla_w_uq": 1, "mla_w_ukv": 1}
SMALL = ["hgrn_lb_logits", "hgrn_norm_g", "mla_q_norm_g", "mla_kv_norm_g", "fox_b_f", "gmlp_ln_g", "gmlp_ln_b",
         "gmlp_w_s", "gmlp_b_s", "ln_g", "ln_b"]
GATHERED = ["ada_w", "ffn1_w_in", "ffn1_w_out", "ffn2_w_in", "ffn2_w_out", "mix_w_in", "mix_w_out", "mla_w_uq", "mla_w_ukv"]
REDUCED = GATHERED[1:]


def _col_shards(a):
    cols = a.shape[1] // N_CHIPS
    return jnp.stack([a[:, k * cols:(k + 1) * cols] for k in range(N_CHIPS)])


def add_kept_half(a, got, core, name):
    _, R, C = a.shape
    rh = R // 2
    tr = _row_tile(rh, C, mult=16)
    nr = rh // tr

    def body(core_ref, a_ref, b_ref, o_ref):
        o_ref[...] = (a_ref[...].astype(F32) + b_ref[...].astype(F32)).astype(o_ref.dtype)

    half = pl.BlockSpec((None, tr, C), lambda k, r, core_ref: (k, r, 0))
    grid_spec = pltpu.PrefetchScalarGridSpec(
        num_scalar_prefetch=1, grid=(N_CHIPS, nr),
        in_specs=[pl.BlockSpec((None, tr, C), lambda k, r, core_ref: (k, core_ref[0] * nr + r, 0)), half],
        out_specs=half)
    return pl.pallas_call(body, name=name, grid_spec=grid_spec, out_shape=_sds((N_CHIPS, rh, C), BF16),
                          compiler_params=_cparams(VMEM_BIG))(core.reshape(1).astype(jnp.int32), a, got)


def _rows(parts, n_rows, dtype):
    flat = jnp.concatenate([p.reshape(-1) for p in parts])
    pad = n_rows * D - flat.shape[0]
    return jnp.concatenate([flat, jnp.zeros((pad,), dtype)]).reshape(n_rows, D)


def _take(flat, shapes):
    out, o = [], 0
    for shp in shapes:
        n = int(np.prod(shp))
        out.append(flat[o:o + n].reshape(shp))
        o += n
    return out


def _round_up(n, m):
    return -(-n // m) * m


def pack_shard(w):
    parts = [w[n][l] for l in range(DEPTH) for n in SHARDED] + [w[n][l] for l in range(DEPTH) for n in ("ln_g", "ln_b")]
    n = sum(int(np.prod(p.shape)) for p in parts)
    return _rows(parts, _round_up(-(-n // D), 16), F32)


def unpack_shard(pk, like):
    shapes = [like[n].shape[1:] for l in range(DEPTH) for n in SHARDED] + [like[n].shape[1:] for l in range(DEPTH) for n in ("ln_g", "ln_b")]
    pieces = _take(pk.reshape(-1), shapes)
    names = [n for l in range(DEPTH) for n in SHARDED] + [n for l in range(DEPTH) for n in ("ln_g", "ln_b")]
    out = {}
    for n in list(SHARDED) + ["ln_g", "ln_b"]:
        out[n] = jnp.stack([p for p, m in zip(pieces, names) if m == n])
    return out


def pack_small(w):
    parts = [w[n][l] for l in range(DEPTH) for n in SMALL]
    n = sum(int(np.prod(p.shape)) for p in parts)
    return _rows(parts, _round_up(-(-n // D), 8), F32)


def unpack_small(pk, like):
    shapes = [like[n].shape[1:] for l in range(DEPTH) for n in SMALL]
    pieces = _take(pk.reshape(-1), shapes)
    names = [n for l in range(DEPTH) for n in SMALL]
    return {n: jnp.stack([p for p, m in zip(pieces, names) if m == n]) for n in SMALL}


def pack_gather(w):
    parts = [w[n][l].astype(BF16) for l in range(DEPTH) for n in ["ada_w"] + list(SHARDED)]
    ln = jnp.concatenate([w[n][l].reshape(-1) for l in range(DEPTH) for n in ("ln_g", "ln_b")])
    parts.append(lax.bitcast_convert_type(ln, BF16))
    n = sum(int(np.prod(p.shape)) for p in parts)
    return _rows(parts, _round_up(-(-n // D), 16), BF16)


def unpack_gather(g, w):
    names = ["ada_w"] + list(SHARDED)
    shapes = [w[n].shape[1:] for l in range(DEPTH) for n in names]
    n_ln = DEPTH * 2 * 3 * (D // N_CHIPS)
    flat = g.reshape(N_CHIPS, -1)
    per_chip = [_take(flat[k], shapes + [(n_ln, 2)]) for k in range(N_CHIPS)]
    layers = [dict() for _ in range(DEPTH)]
    i = 0
    for l in range(DEPTH):
        for n in names:
            axis = 1 if n == "ada_w" else SHARDED[n]
            layers[l][n] = jnp.concatenate([per_chip[k][i] for k in range(N_CHIPS)], axis=axis)
            i += 1
    ln = [lax.bitcast_convert_type(per_chip[k][i], F32).reshape(DEPTH, 2, 3, D // N_CHIPS) for k in range(N_CHIPS)]
    ln = jnp.concatenate(ln, axis=3)
    for l in range(DEPTH):
        layers[l]["ln_g"], layers[l]["ln_b"] = ln[l, 0], ln[l, 1]
    return layers


def pack_grads(grads, k):
    parts = []
    for l in range(DEPTH):
        g = grads[l]
        full = {"ffn1_w_out": g["ffn1_out"], "ffn2_w_out": g["ffn2_out"], "mix_w_in": mix_in_unext(g["mix_in"]),
                "mix_w_out": mix_out_unext(g["mix_out"]), "mla_w_uq": uq_unext(g["wq"]), "mla_w_ukv": ukv_unext(g["wkv"])}
        for n, axis in SHARDED.items():
            if n in ("ffn1_w_in", "ffn2_w_in"):
                half = g[n.replace("_w_in", "_in")][k // 2]
                parts.append(half[:, (k % 2) * (FF // 2):(k % 2 + 1) * (FF // 2)])
            else:
                sz = full[n].shape[axis] // N_CHIPS
                parts.append(lax.slice_in_dim(full[n], k * sz, (k + 1) * sz, axis=axis))
    for l in range(DEPTH):
        for n in ("ln_g", "ln_b"):
            parts.append(grads[l][n][:, k * (D // N_CHIPS):(k + 1) * (D // N_CHIPS)])
    n = sum(int(np.prod(p.shape)) for p in parts)
    return _rows(parts, _round_up(-(-n // D), 16), F32)


def kernel(x, c, ada_w, ada_b, ln_g, ln_b, ffn1_w_in, ffn1_w_out, ffn2_w_in, ffn2_w_out, mix_w_in, mix_w_out, hgrn_lb_logits, hgrn_norm_g, mla_q_norm_g, mla_kv_norm_g, mla_w_uq, mla_w_ukv, fox_b_f, gmlp_ln_g, gmlp_ln_b, gmlp_w_s, gmlp_b_s, loss_target, m_ada_w, m_ada_b, m_ln_g, m_ln_b, m_ffn1_w_in, m_ffn1_w_out, m_ffn2_w_in, m_ffn2_w_out, m_mix_w_in, m_mix_w_out, m_hgrn_lb_logits, m_hgrn_norm_g, m_mla_q_norm_g, m_mla_kv_norm_g, m_mla_w_uq, m_mla_w_ukv, m_fox_b_f, m_gmlp_ln_g, m_gmlp_ln_b, m_gmlp_w_s, m_gmlp_b_s, v_ada_w, v_ada_b, v_ln_g, v_ln_b, v_ffn1_w_in, v_ffn1_w_out, v_ffn2_w_in, v_ffn2_w_out, v_mix_w_in, v_mix_w_out, v_hgrn_lb_logits, v_hgrn_norm_g, v_mla_q_norm_g, v_mla_kv_norm_g, v_mla_w_uq, v_mla_w_ukv, v_fox_b_f, v_gmlp_ln_g, v_gmlp_ln_b, v_gmlp_w_s, v_gmlp_b_s):
    w = dict(zip(WEIGHTS, (ada_w, ada_b, ln_g, ln_b, ffn1_w_in, ffn1_w_out, ffn2_w_in, ffn2_w_out, mix_w_in, mix_w_out, hgrn_lb_logits, hgrn_norm_g, mla_q_norm_g, mla_kv_norm_g, mla_w_uq, mla_w_ukv, fox_b_f, gmlp_ln_g, gmlp_ln_b, gmlp_w_s, gmlp_b_s)))
    m = dict(zip(WEIGHTS, (m_ada_w, m_ada_b, m_ln_g, m_ln_b, m_ffn1_w_in, m_ffn1_w_out, m_ffn2_w_in, m_ffn2_w_out, m_mix_w_in, m_mix_w_out, m_hgrn_lb_logits, m_hgrn_norm_g, m_mla_q_norm_g, m_mla_kv_norm_g, m_mla_w_uq, m_mla_w_ukv, m_fox_b_f, m_gmlp_ln_g, m_gmlp_ln_b, m_gmlp_w_s, m_gmlp_b_s)))
    v = dict(zip(WEIGHTS, (v_ada_w, v_ada_b, v_ln_g, v_ln_b, v_ffn1_w_in, v_ffn1_w_out, v_ffn2_w_in, v_ffn2_w_out, v_mix_w_in, v_mix_w_out, v_hgrn_lb_logits, v_hgrn_norm_g, v_mla_q_norm_g, v_mla_kv_norm_g, v_mla_w_uq, v_mla_w_ukv, v_fox_b_f, v_gmlp_ln_g, v_gmlp_ln_b, v_gmlp_w_s, v_gmlp_b_s)))
    Bl, S, _ = x.shape
    T = Bl * S
    core = lax.axis_index("c")
    chip = 2 * lax.axis_index("x") + lax.axis_index("y")

    def shard(key):
        n, l = key
        if n == "ln":
            return jnp.concatenate([ln_g[l:l + 1], ln_b[l:l + 1], jnp.zeros((1, 2, D // N_CHIPS), F32)], axis=1)
        return w[n][l:l + 1].astype(BF16)

    mixers = ["mix_w_in", "mix_w_out", "mla_w_uq", "mla_w_ukv"]
    groups = [[("ada_w", 0), ("ffn1_w_in", 0), ("ffn1_w_out", 0), ("ln", 0)],
              [(n, 0) for n in mixers + ["ffn2_w_in", "ffn2_w_out"]],
              [(n, 1) for n in GATHERED + ["ln"]]]
    srcs = [[shard(k) for k in grp] for grp in groups]
    lands = [[own_slot(s, chip) for s in grp] for grp in srcs]
    handle0 = ag_start(srcs[0], lands[0], jnp.zeros((8, 128), F32), "ag_start_0")
    first, token = ag_forward(ag_wait(handle0, lands[2][0], "ag_wait_0")[1], "ag_forward_0")
    have = dict(zip(groups[0], first))
    handles = {}
    for gi in (1, 2):
        handles[gi] = ag_start(srcs[gi], lands[gi], token, "ag_start_%d" % gi)
        token = handles[gi][-1]
    c8 = jnp.concatenate([c, jnp.zeros((8 - Bl, D), F32)], axis=0)
    c8 = c8 + token[0, 0]

    def cat_cols(a):
        return jnp.concatenate([a[0, k] for k in range(N_CHIPS)], axis=1)

    def get(l, part, after):
        gi = 2 if l == 1 else (0 if part in ("ada", "ffn1") else 1)
        if gi in handles:
            arrived, _ = ag_forward(ag_wait(handles.pop(gi), after, "ag_wait_%d" % gi)[1], "ag_forward_%d" % gi)
            have.update(zip(groups[gi], arrived))
        if part == "ada":
            return dict(ada_w=have[("ada_w", l)], wl=0, ada_b=ada_b[l][None])
        if part == "ffn1":
            ln_full = jnp.moveaxis(have[("ln", l)][0], 0, 1).reshape(8, D)
            return dict(ffn1_in=have[("ffn1_w_in", l)], ffn1_out=have[("ffn1_w_out", l)], wl=0,
                        ln_g=ln_full[0:3], ln_b=ln_full[3:6])
        if part == "ffn2":
            return dict(ffn2_in=have[("ffn2_w_in", l)], ffn2_out=have[("ffn2_w_out", l)])
        return dict(
            mix_in=mix_in_ext(cat_cols(have[("mix_w_in", l)])), mix_out=mix_out_ext(have[("mix_w_out", l)].reshape(D, D)),
            wq=uq_ext(cat_cols(have[("mla_w_uq", l)])).astype(F32), wkv=ukv_ext(cat_cols(have[("mla_w_ukv", l)])).astype(F32),
            ng=hgrn_norm_g[l][None], gq=mla_q_norm_g[l][None], gkv=mla_kv_norm_g[l][None],
            bcol=jnp.concatenate([fox_b_f[l], jnp.zeros((8 - HEADS,), F32)])[:, None],
            g_lg=gmlp_ln_g[l][None], g_lb=gmlp_ln_b[l][None], ws=gmlp_w_s[l], bs=gmlp_b_s[l][:, :, None])

    pending = []

    def emit(l, part, g):
        if part == "mix":
            names = mixers
            by_chip = [_col_shards(mix_in_unext(g["mix_in"])), mix_out_unext(g["mix_out"]).reshape(N_CHIPS, D // N_CHIPS, D),
                       _col_shards(uq_unext(g["wq"])).astype(BF16), _col_shards(ukv_unext(g["wkv"])).astype(BF16)]
        else:
            names = [part + "_w_in", part + "_w_out"]
            by_chip = [g[part + "_in"], g[part + "_out"]]
        tag = "%d_%s" % (l, part)
        got = sibling_swap(by_chip, "sibling_swap_" + tag)
        chip_sum = [add_kept_half(a, r, core, "add_sibling") for a, r in zip(by_chip, got)]
        zones = [lax.dynamic_update_slice(lax.empty(h.shape, h.dtype), lax.dynamic_slice_in_dim(h, chip, 1, axis=0), (chip, 0, 0))
                 for h in chip_sum]
        handle = rs_start(chip_sum, zones, chip_sum[0], "rs_start_" + tag)
        pending.append((l, names, handle, tag))
        return handle[-1][0, 0]

    loss_tile, dx, dmods, grads, d_logits = local_step(
        x.reshape(T, D), c8, loss_target.reshape(T, D), get, hgrn_lb_logits, S, emit)
    loss = lax.psum(loss_tile[0, 0], ("x", "y", "c"))

    small_g = {"hgrn_lb_logits": d_logits,
               "hgrn_norm_g": jnp.stack([grads[l]["ng"][0] for l in range(DEPTH)]),
               "mla_q_norm_g": jnp.stack([grads[l]["gq"][0] for l in range(DEPTH)]),
               "mla_kv_norm_g": jnp.stack([grads[l]["gkv"][0] for l in range(DEPTH)]),
               "fox_b_f": jnp.stack([grads[l]["bcol"][0:HEADS, 0] for l in range(DEPTH)]),
               "gmlp_ln_g": jnp.stack([grads[l]["g_lg"][0] for l in range(DEPTH)]),
               "gmlp_ln_b": jnp.stack([grads[l]["g_lb"][0] for l in range(DEPTH)]),
               "gmlp_w_s": jnp.stack([grads[l]["ws"] for l in range(DEPTH)]),
               "gmlp_b_s": jnp.stack([grads[l]["bs"][:, :, 0] for l in range(DEPTH)])}
    small_g["ln_g"] = jnp.stack([grads[l]["ln_g"] for l in range(DEPTH)])
    small_g["ln_b"] = jnp.stack([grads[l]["ln_b"] for l in range(DEPTH)])
    pk_small = pack_small(small_g)
    n_small = pk_small.shape[0]
    extras = [dmods[l] for l in range(DEPTH)] + [c]
    n_extra = _round_up(-(-sum(int(np.prod(e.shape)) for e in extras) // D), 8)
    gathered = ag_all(jnp.concatenate([pk_small, _rows(extras, n_extra, F32)], axis=0))
    g_small = unpack_small(sum_slots(gathered[:, 0:n_small], 8, "sum_small"), small_g)
    ext = gathered[:, n_small:].reshape(8, -1)
    n_dmod = DEPTH * Bl * N_MOD * D
    dmod_all = ext[:, 0:n_dmod].reshape(8, DEPTH, Bl, N_MOD * D)
    c_all = ext[:, n_dmod:n_dmod + Bl * D].reshape(8 * Bl, D)
    g_ada_w, g_ada_b = [], []
    ncol = N_MOD * D // N_CHIPS
    for l in range(DEPTH):
        dm = dmod_all[:, l].reshape(8 * Bl, N_MOD * D)
        g_ada_w.append(ada_grad(c_all, lax.dynamic_slice_in_dim(dm, chip * ncol, ncol, axis=1)))
        g_ada_b.append(sum_slots(dm.reshape(8 * Bl, N_MOD, D), 8 * Bl, "sum_ada_b").reshape(N_MOD * D))
    g_ada_w, g_ada_b = jnp.stack(g_ada_w), jnp.stack(g_ada_b)

    red = {n: lax.empty(w[n].shape, F32) for n in REDUCED}

    def arrive(entry, after):
        l, names, handle, tag = entry
        for n, land in zip(names, rs_wait(handle, after, "rs_wait_" + tag)[1]):
            red[n] = sum_into(land, red[n], l, core, "sum_chips")

    for entry in pending[:-1]:
        arrive(entry, dx)
    late = pending[-1][1]
    early = [n for n in REDUCED if n not in late]
    grad = dict(zip(early, sibling_join([red[n] for n in early], "sibling_join_a")))
    grad.update(g_small)
    grad["ada_w"], grad["ada_b"] = g_ada_w, g_ada_b
    for n in ("ln_g", "ln_b"):
        grad[n] = lax.dynamic_slice_in_dim(g_small[n], chip * (D // N_CHIPS), D // N_CHIPS, axis=2)
    out = {"grad": grad, "delta": {}, "new_m": {}, "new_v": {}}

    def update(n):
        shp = w[n].shape
        two_d = (-1, shp[-1])
        res = adamw(w[n].reshape(two_d), grad[n].reshape(two_d), m[n].reshape(two_d), v[n].reshape(two_d), "adamw_" + n,
                    echo=n in REDUCED)
        grad[n] = (res[3] if n in REDUCED else grad[n]).reshape(shp)
        for key, r in zip(("delta", "new_m", "new_v"), res):
            out[key][n] = r.reshape(shp)

    for n in WEIGHTS:
        if n not in late:
            update(n)
    arrive(pending[-1], out["delta"]["ffn2_w_in"])
    grad.update(zip(late, sibling_join([red[n] for n in late], "sibling_join_b")))
    for n in late:
        update(n)
    outs = [loss, dx.reshape(Bl, S, D)]
    for key in ("grad", "delta", "new_m", "new_v"):
        outs += [out[key][n] for n in WEIGHTS]
    return tuple(outs)
```

```python
import functools

import jax
import jax.numpy as jnp
import numpy as np
from jax import lax
from jax.experimental import pallas as pl
from jax.experimental.pallas import tpu as pltpu

F32, BF16 = jnp.float32, jnp.bfloat16
MESH = pl.DeviceIdType.MESH

N_CHIPS = 4
D = 1024
DEPTH = 2
FF = 2816
N_MOD = 9
GW = 256
HEADS = 4
HD = 64
HP = 128
A_CHUNK = 16
LB_FLOOR = 1e-30
B_Q_LORA, B_KV_LORA, B_NOPE, B_ROPE = 256, 128, 64, 32
ROPE_THETA = 10000.0
D_CHUNK = 128
MIX_COLS = 2724
ALPHA = (2 * DEPTH) ** 0.25
LN_EPS = 1e-5
RMS_EPS = 1e-6
ADAM_LR, ADAM_B1, ADAM_B2, ADAM_EPS, ADAM_WD, ADAM_STEP = 0.001, 0.9, 0.999, 1e-08, 0.01, 10

NP = 3840
NP_TILE = 1920
C_A, C_B, C_CQ, C_CK, C_CV, C_D, C_CF = 0, 1024, 1536, 2048, 2560, 3072, 3584
NCAT = 1536
O_A, O_B, O_C, O_D = 0, 256, 768, 1280

VMEM_BIG = 48 << 20


def _cparams(vmem=None):
    return pltpu.CompilerParams(vmem_limit_bytes=vmem) if vmem else pltpu.CompilerParams()


def _sds(shape, dtype):
    return jax.ShapeDtypeStruct(tuple(shape), dtype)


@jax.custom_vjp
def _mm(a, w):
    return jnp.dot(a.astype(BF16), w.astype(BF16), preferred_element_type=F32)


def _mm_f(a, w):
    return _mm(a, w), (a, w)


def _mm_b(res, g):
    a, w = res
    gb = g.astype(BF16)
    da = lax.dot_general(gb, w.astype(BF16), (((1,), (1,)), ((), ())), preferred_element_type=F32)
    dw = lax.dot_general(a.astype(BF16), gb, (((0,), (0,)), ((), ())), preferred_element_type=F32)
    return da.astype(a.dtype), dw.astype(w.dtype)


_mm.defvjp(_mm_f, _mm_b)


@jax.custom_vjp
def _mm_nt(a, b):
    return lax.dot_general(a.astype(BF16), b.astype(BF16), (((1,), (1,)), ((), ())), preferred_element_type=F32)


def _mm_nt_f(a, b):
    return _mm_nt(a, b), (a, b)


def _mm_nt_b(res, g):
    a, b = res
    gb = g.astype(BF16)
    da = jnp.dot(gb, b.astype(BF16), preferred_element_type=F32)
    db = lax.dot_general(gb, a.astype(BF16), (((0,), (0,)), ((), ())), preferred_element_type=F32)
    return da.astype(a.dtype), db.astype(b.dtype)


_mm_nt.defvjp(_mm_nt_f, _mm_nt_b)


@jax.custom_vjp
def _mm_tn(a, b):
    return lax.dot_general(a.astype(BF16), b.astype(BF16), (((0,), (0,)), ((), ())), preferred_element_type=F32)


def _mm_tn_f(a, b):
    return _mm_tn(a, b), (a, b)


def _mm_tn_b(res, g):
    a, b = res
    gb = g.astype(BF16)
    da = lax.dot_general(b.astype(BF16), gb, (((1,), (1,)), ((), ())), preferred_element_type=F32)
    db = jnp.dot(a.astype(BF16), gb, preferred_element_type=F32)
    return da.astype(a.dtype), db.astype(b.dtype)


_mm_tn.defvjp(_mm_tn_f, _mm_tn_b)


def _mm_hi(a, w):
    return jnp.dot(a, w, precision=lax.Precision.HIGHEST, preferred_element_type=F32)


def _iota(shape, dim):
    return lax.broadcasted_iota(jnp.int32, shape, dim)


def _head_sum_mats():
    e = (_iota((GW, HP), 0) // HD == _iota((GW, HP), 1)).astype(F32)
    et = (_iota((HP, GW), 1) // HD == _iota((HP, GW), 0)).astype(F32)
    return e, et


def _modulate(x, mod_ref, sh, sc):
    return x * (1.0 + mod_ref[sc:sc + 1, :]) + mod_ref[sh:sh + 1, :]


def _ln_res(x, y, gate, lg, lb, gs):
    r = ALPHA * x + gs * (1.0 + gate) * y
    mu = jnp.mean(r, axis=-1, keepdims=True)
    var = jnp.mean(jnp.square(r - mu), axis=-1, keepdims=True)
    return (r - mu) * lax.rsqrt(var + LN_EPS) * lg + lb


def _rms(x, g):
    return x * lax.rsqrt(jnp.mean(x * x, axis=-1, keepdims=True) + RMS_EPS) * g


def _tile(n, pref):
    return pref if n % pref == 0 else n


def mod_fwd(c8, w, l, b):
    tn = w.shape[3]
    n = N_CHIPS * tn

    def body(c_ref, w_ref, b_ref, o_ref):
        h = jax.nn.silu(c_ref[...]).astype(BF16)
        o_ref[...] = jnp.dot(h, w_ref[...], preferred_element_type=F32) + b_ref[...]

    return pl.pallas_call(
        body, name="mod_fwd", grid=(N_CHIPS,),
        in_specs=[pl.BlockSpec((8, D), lambda j: (0, 0)), pl.BlockSpec((None, None, D, tn), lambda j: (l, j, 0, 0)),
                  pl.BlockSpec((1, tn), lambda j: (0, j))],
        out_specs=pl.BlockSpec((8, tn), lambda j: (0, j)), out_shape=_sds((8, n), F32),
        compiler_params=_cparams(VMEM_BIG))(c8, w, b)


def ffn_in_fwd(x, mod, w_in, l, sh, sc, S):
    T = x.shape[0]
    tm, tn = _tile(S, 512), FF // 2
    tpb, nj = S // tm, 2

    def body(x_ref, mod_ref, wg_ref, wu_ref, zg_ref, zu_ref, act_ref, ht_ref, actt_ref, h_ref):
        @pl.when(pl.program_id(1) == 0)
        def _():
            h_ref[...] = _modulate(x_ref[...], mod_ref, sh, sc).astype(BF16)
            ht_ref[...] = h_ref[...].T
        g = jnp.dot(h_ref[...], wg_ref[...], preferred_element_type=F32)
        u = jnp.dot(h_ref[...], wu_ref[...], preferred_element_type=F32)
        zg_ref[...] = g.astype(BF16)
        zu_ref[...] = u.astype(BF16)
        act = (jax.nn.silu(g) * u).astype(BF16)
        act_ref[...] = act
        actt_ref[...] = act.T

    return pl.pallas_call(
        body, name="ffn_in_fwd", grid=(T // tm, nj),
        in_specs=[pl.BlockSpec((tm, D), lambda i, j: (i, 0)),
                  pl.BlockSpec((None, N_MOD, D), lambda i, j: (i // tpb, 0, 0)),
                  pl.BlockSpec((None, None, D, tn), lambda i, j: (l, j, 0, 0)),
                  pl.BlockSpec((None, None, D, tn), lambda i, j: (l, j + nj, 0, 0))],
        out_specs=[pl.BlockSpec((tm, tn), lambda i, j: (i, j))] * 3
        + [pl.BlockSpec((D, tm), lambda i, j: (0, i)), pl.BlockSpec((tn, tm), lambda i, j: (j, i))],
        out_shape=[_sds((T, FF), BF16)] * 3 + [_sds((D, T), BF16), _sds((FF, T), BF16)],
        scratch_shapes=[pltpu.VMEM((tm, D), BF16)],
        compiler_params=_cparams(VMEM_BIG))(x, mod, w_in, w_in)


def mix_in_fwd(x, mod, w, sh, sc, S):
    T = x.shape[0]
    n = w.shape[1]
    tm, tn = _tile(S, 512), NP_TILE
    tpb = S // tm

    def body(x_ref, mod_ref, w_ref, o_ref, ht_ref, h_ref):
        @pl.when(pl.program_id(1) == 0)
        def _():
            h_ref[...] = _modulate(x_ref[...], mod_ref, sh, sc).astype(BF16)
            ht_ref[...] = h_ref[...].T
        o_ref[...] = jnp.dot(h_ref[...], w_ref[...], preferred_element_type=F32)

    return pl.pallas_call(
        body, name="mix_in_fwd", grid=(T // tm, n // tn),
        in_specs=[pl.BlockSpec((tm, D), lambda i, j: (i, 0)),
                  pl.BlockSpec((None, N_MOD, D), lambda i, j: (i // tpb, 0, 0)),
                  pl.BlockSpec((D, tn), lambda i, j: (0, j))],
        out_specs=[pl.BlockSpec((tm, tn), lambda i, j: (i, j)), pl.BlockSpec((D, tm), lambda i, j: (0, i))],
        out_shape=[_sds((T, n), F32), _sds((D, T), BF16)],
        scratch_shapes=[pltpu.VMEM((tm, D), BF16)],
        compiler_params=_cparams(VMEM_BIG))(x, mod, w)


def out_ln_fwd(act, w_out, x, mod, lg, lb, gate, gs, S, l=None):
    T, K = act.shape
    tm = _tile(S, 512)
    tpb = S // tm

    def body(a_ref, w_ref, x_ref, mod_ref, lg_ref, lb_ref, y_ref, xn_ref):
        y = jnp.dot(a_ref[...], w_ref[...].reshape(K, D), preferred_element_type=F32)
        y_ref[...] = y
        xn_ref[...] = _ln_res(x_ref[...], y, mod_ref[gate:gate + 1, :], lg_ref[...], lb_ref[...], gs)

    if l is None:
        w_spec = pl.BlockSpec((K, D), lambda i: (0, 0))
    else:
        w_spec = pl.BlockSpec((None, N_CHIPS, K // N_CHIPS, D), lambda i: (l, 0, 0, 0))
    return pl.pallas_call(
        body, name="out_ln_fwd", grid=(T // tm,),
        in_specs=[pl.BlockSpec((tm, K), lambda i: (i, 0)), w_spec,
                  pl.BlockSpec((tm, D), lambda i: (i, 0)),
                  pl.BlockSpec((None, N_MOD, D), lambda i: (i // tpb, 0, 0)),
                  pl.BlockSpec((1, D), lambda i: (0, 0)), pl.BlockSpec((1, D), lambda i: (0, 0))],
        out_specs=[pl.BlockSpec((tm, D), lambda i: (i, 0))] * 2,
        out_shape=[_sds((T, D), F32), _sds((T, D), F32)],
        compiler_params=_cparams(VMEM_BIG))(act, w_out, x, mod, lg, lb)


def ln_res_bwd(dxn, x, y, mod, lg, lb, gate, gs, S):
    T = x.shape[0]
    B = T // S
    tm = _tile(S, 512)
    tpb = S // tm

    def body(d_ref, x_ref, y_ref, mod_ref, lg_ref, lb_ref, dx_ref, dy_ref, dg_ref, dlg_ref, dlb_ref):
        i = pl.program_id(0)
        f = functools.partial(_ln_res, gs=gs)
        _, vjp = jax.vjp(f, x_ref[...], y_ref[...], mod_ref[gate:gate + 1, :], lg_ref[...], lb_ref[...])
        dx, dy, dg, dlg, dlb = vjp(d_ref[...])
        dx_ref[...] = dx
        dy_ref[...] = dy.astype(BF16)

        @pl.when(i % tpb == 0)
        def _():
            dg_ref[...] = jnp.zeros_like(dg_ref)

        @pl.when(i == 0)
        def _():
            dlg_ref[...] = jnp.zeros_like(dlg_ref)
            dlb_ref[...] = jnp.zeros_like(dlb_ref)

        dg_ref[...] += dg
        dlg_ref[...] += dlg
        dlb_ref[...] += dlb

    tok = pl.BlockSpec((tm, D), lambda i: (i, 0))
    vec = pl.BlockSpec((1, D), lambda i: (0, 0))
    return pl.pallas_call(
        body, name="ln_res_bwd", grid=(T // tm,),
        in_specs=[tok, tok, tok, pl.BlockSpec((None, N_MOD, D), lambda i: (i // tpb, 0, 0)), vec, vec],
        out_specs=[tok, tok, pl.BlockSpec((None, 1, D), lambda i: (i // tpb, 0, 0)), vec, vec],
        out_shape=[_sds((T, D), F32), _sds((T, D), BF16), _sds((B, 1, D), F32), _sds((1, D), F32), _sds((1, D), F32)],
        compiler_params=_cparams(VMEM_BIG))(dxn, x, y, mod, lg, lb)


def swiglu_bwd(dy, w_out, l, zg, zu, S):
    T = dy.shape[0]
    tm, tn = _tile(S, 512), FF // 2

    def body(dy_ref, w_ref, zg_ref, zu_ref, dg_ref, du_ref):
        da = lax.dot_general(dy_ref[...], w_ref[...].reshape(tn, D), (((1,), (1,)), ((), ())), preferred_element_type=F32)
        g, u = zg_ref[...].astype(F32), zu_ref[...].astype(F32)
        sg = jax.nn.sigmoid(g)
        dg_ref[...] = (da * u * (sg * (1.0 + g * (1.0 - sg)))).astype(BF16)
        du_ref[...] = (da * (g * sg)).astype(BF16)

    zt = pl.BlockSpec((tm, tn), lambda i, j: (i, j))
    return pl.pallas_call(
        body, name="swiglu_bwd", grid=(T // tm, FF // tn),
        in_specs=[pl.BlockSpec((tm, D), lambda i, j: (i, 0)),
                  pl.BlockSpec((None, 2, FF // N_CHIPS, D), lambda i, j: (l, j, 0, 0)), zt, zt],
        out_specs=[zt, zt], out_shape=[_sds((T, FF), BF16), _sds((T, FF), BF16)],
        compiler_params=_cparams(VMEM_BIG))(dy, w_out, zg, zu)


def nt_plain(dy, w):
    T = dy.shape[0]
    K = w.shape[0]
    tm = _tile(T, 512)

    def body(dy_ref, w_ref, o_ref):
        o_ref[...] = lax.dot_general(dy_ref[...], w_ref[...], (((1,), (1,)), ((), ())), preferred_element_type=F32)

    return pl.pallas_call(
        body, name="nt_plain", grid=(T // tm,),
        in_specs=[pl.BlockSpec((tm, D), lambda i: (i, 0)), pl.BlockSpec((K, D), lambda i: (0, 0))],
        out_specs=pl.BlockSpec((tm, K), lambda i: (i, 0)), out_shape=_sds((T, K), F32),
        compiler_params=_cparams(VMEM_BIG))(dy, w)


def _tn_step(acc, o_ref, lhs_t, rhs, t, nt):
    part = jnp.dot(lhs_t, rhs, preferred_element_type=F32)
    if nt == 1:
        o_ref[...] = part.astype(o_ref.dtype)
        return

    @pl.when(t == 0)
    def _():
        acc[...] = part

    @pl.when((t > 0) & (t < nt - 1))
    def _():
        acc[...] += part

    @pl.when(t == nt - 1)
    def _():
        o_ref[...] = (acc[...] + part).astype(o_ref.dtype)


def tn_mm(a_t, b, tk, tn):
    K, T = a_t.shape
    N = b.shape[1]
    tt = _tile(T, 512)
    nt = T // tt

    def body(a_ref, b_ref, o_ref, acc):
        _tn_step(acc, o_ref, a_ref[...], b_ref[...], pl.program_id(2), nt)

    return pl.pallas_call(
        body, name="tn_mm", grid=(K // tk, N // tn, nt),
        in_specs=[pl.BlockSpec((tk, tt), lambda k, j, t: (k, t)), pl.BlockSpec((tt, tn), lambda k, j, t: (t, j))],
        out_specs=pl.BlockSpec((tk, tn), lambda k, j, t: (k, j)), out_shape=_sds((K, N), BF16),
        scratch_shapes=[pltpu.VMEM((tk, tn), F32)], compiler_params=_cparams(VMEM_BIG))(a_t, b)


def tn_mm_shards(h_t, bg, bu):
    T = h_t.shape[1]
    tn = FF // 2
    tt = _tile(T, 512)
    nt = T // tt

    def body(h_ref, bg_ref, bu_ref, o_ref, acc):
        j, t = pl.program_id(0), pl.program_id(1)

        @pl.when(j < 2)
        def _():
            _tn_step(acc, o_ref, h_ref[...], bg_ref[...], t, nt)

        @pl.when(j >= 2)
        def _():
            _tn_step(acc, o_ref, h_ref[...], bu_ref[...], t, nt)

    return pl.pallas_call(
        body, name="tn_mm_shards", grid=(N_CHIPS, nt),
        in_specs=[pl.BlockSpec((D, tt), lambda j, t: (0, t)),
                  pl.BlockSpec((tt, tn), lambda j, t: (jnp.where(j < 2, t, 0), jnp.minimum(j, 1))),
                  pl.BlockSpec((tt, tn), lambda j, t: (jnp.where(j < 2, 0, t), jnp.maximum(j - 2, 0)))],
        out_specs=pl.BlockSpec((None, D, tn), lambda j, t: (j, 0, 0)), out_shape=_sds((N_CHIPS, D, tn), BF16),
        scratch_shapes=[pltpu.VMEM((D, tn), F32)], compiler_params=_cparams(VMEM_BIG))(h_t, bg, bu)


def nt_mod_bwd(ds, w, offs, x, mod, dres, sc, S, tk, l=None):
    T = x.shape[0]
    B = T // S
    tm = _tile(S, 512)
    tpb = S // tm
    Kd = ds[0].shape[1]
    nk = Kd // tk
    n_in = len(ds)

    def body(*refs):
        d_refs, w_refs = refs[:n_in], refs[n_in:2 * n_in]
        x_ref, mod_ref, r_ref, dx_ref, dsh_ref, dsc_ref, acc = refs[2 * n_in:]
        i, k = pl.program_id(0), pl.program_id(1)

        part = sum(lax.dot_general(d_ref[...], w_ref[...], (((1,), (1,)), ((), ())), preferred_element_type=F32)
                   for d_ref, w_ref in zip(d_refs, w_refs))

        @pl.when(k == 0)
        def _():
            acc[...] = part

        @pl.when(k > 0)
        def _():
            acc[...] += part

        @pl.when(k == nk - 1)
        def _():
            dh = acc[...]
            dx_ref[...] = dh * (1.0 + mod_ref[sc:sc + 1, :]) + r_ref[...]

            @pl.when(i % tpb == 0)
            def _():
                dsh_ref[...] = jnp.zeros_like(dsh_ref)
                dsc_ref[...] = jnp.zeros_like(dsc_ref)

            dsh_ref[...] += jnp.sum(dh, axis=0, keepdims=True)
            dsc_ref[...] += jnp.sum(dh * x_ref[...], axis=0, keepdims=True)

    tok = pl.BlockSpec((tm, D), lambda i, k: (i, 0))
    vec = pl.BlockSpec((None, 1, D), lambda i, k: (i // tpb, 0, 0))
    in_specs = [pl.BlockSpec((tm, tk), lambda i, k: (i, k)) for _ in ds]
    if l is None:
        in_specs += [pl.BlockSpec((D, tk), functools.partial(lambda i, k, o: (0, k + o), o=off // tk)) for off in offs]
    else:
        in_specs += [pl.BlockSpec((None, None, D, tk), functools.partial(lambda i, k, o: (l, k + o, 0, 0), o=off)) for off in offs]
    in_specs += [tok, pl.BlockSpec((None, N_MOD, D), lambda i, k: (i // tpb, 0, 0)), tok]
    return pl.pallas_call(
        body, name="nt_mod_bwd", grid=(T // tm, nk), in_specs=in_specs,
        out_specs=[tok, vec, vec],
        out_shape=[_sds((T, D), F32), _sds((B, 1, D), F32), _sds((B, 1, D), F32)],
        scratch_shapes=[pltpu.VMEM((tm, D), F32)],
        compiler_params=_cparams(VMEM_BIG))(*ds, *([w] * n_in), x, mod, dres)


def loss_head(y, tgt):
    T = y.shape[0]
    tm = _tile(T, 512)

    def body(y_ref, t_ref, l_ref, d_ref):
        @pl.when(pl.program_id(0) == 0)
        def _():
            l_ref[...] = jnp.zeros_like(l_ref)
        e = y_ref[...] - t_ref[...]
        d_ref[...] = e * (1.0 / D)
        l_ref[...] += 0.5 * jnp.sum(jnp.sum(e * e, axis=1, keepdims=True) * (1.0 / D))

    tok = pl.BlockSpec((tm, D), lambda i: (i, 0))
    return pl.pallas_call(
        body, name="loss_head", grid=(T // tm,), in_specs=[tok, tok],
        out_specs=[pl.BlockSpec((8, 128), lambda i: (0, 0)), tok],
        out_shape=[_sds((8, 128), F32), _sds((T, D), F32)],
        compiler_params=_cparams(VMEM_BIG))(y, tgt)


def _hgrn_block(q, fz, inp, go, st, lb, ng, blk):
    nc = blk // A_CHUNK
    lb_eff = jnp.maximum(lb, LB_FLOOR)
    log_f = jnp.logaddexp(jnp.log(lb_eff), jnp.log1p(-lb) + jax.nn.log_sigmoid(fz))
    k = (1.0 - lb) * jax.nn.sigmoid(-fz) - (lb_eff - lb)
    qf = jax.nn.silu(q)
    same_chunk = _iota((blk, blk), 0) // A_CHUNK == _iota((blk, blk), 1) // A_CHUNK
    tril = (same_chunk & (_iota((blk, blk), 1) <= _iota((blk, blk), 0))).astype(F32)
    G = _mm_hi(tril, log_f)
    e_mat, et_mat = _head_sum_mats()
    G4, q4, k4, v4 = (z.reshape(nc, A_CHUNK, GW) for z in (G, qf, k, inp))
    shp = (nc, A_CHUNK, A_CHUNK, GW)
    one = (1, A_CHUNK, A_CHUNK, GW)
    mask = jnp.where(_iota(one, 2) <= _iota(one, 1), 0.0, -jnp.inf)
    decay = jnp.exp((G4[:, :, None, :] - G4[:, None, :, :]) + mask)
    prod = q4[:, :, None, :] * k4[:, None, :, :] * decay
    scores = _mm(prod.reshape(nc * A_CHUNK * A_CHUNK, GW), e_mat.astype(BF16))
    spread = _mm(scores, et_mat.astype(BF16)).reshape(shp)
    o_intra = jnp.sum(spread * v4[:, None, :, :], axis=2).reshape(blk, GW)
    head_diag = (_iota((GW, GW), 0) // HD == _iota((GW, GW), 1) // HD).astype(F32)
    g_last = [jnp.sum(log_f[c * A_CHUNK:(c + 1) * A_CHUNK], axis=0, keepdims=True) for c in range(nc)]
    g_last_b = jnp.concatenate([jnp.broadcast_to(g, (A_CHUNK, GW)) for g in g_last], axis=0)
    q_dec = qf * jnp.exp(G)
    k_end = k * jnp.exp(g_last_b - G)
    outs = []
    for c in range(nc):
        rows = slice(c * A_CHUNK, (c + 1) * A_CHUNK)
        outs.append(_mm_nt(q_dec[rows], st))
        st = st * jnp.exp(g_last[c]) + _mm_tn(inp[rows], k_end[rows]) * head_diag
    o = o_intra + jnp.concatenate(outs, axis=0)
    ms = _mm_hi(o * o, e_mat) * (1.0 / HD)
    o = o * _mm_hi(lax.rsqrt(ms + RMS_EPS), et_mat) * ng
    return o * jax.nn.silu(go), st


HGRN_BLK = 128


def hgrn_fwd(proj, lb, ng, S):
    T = proj.shape[0]
    B = T // S
    blk = min(HGRN_BLK, S)
    nb = S // blk

    def body(p_ref, lb_ref, ng_ref, o_ref, st_out_ref, st_ref):
        @pl.when(pl.program_id(1) == 0)
        def _():
            st_ref[...] = jnp.zeros_like(st_ref)
        st_out_ref[...] = st_ref[...]
        p = p_ref[...]
        o, st = _hgrn_block(p[:, 0:GW], p[:, GW:2 * GW], p[:, 2 * GW:3 * GW], p[:, 3 * GW:4 * GW],
                            st_ref[...], lb_ref[...], ng_ref[...], blk)
        o_ref[...] = o.astype(BF16)
        st_ref[...] = st

    vec = pl.BlockSpec((1, GW), lambda b, j: (0, 0))
    return pl.pallas_call(
        body, name="hgrn_fwd", grid=(B, nb),
        in_specs=[pl.BlockSpec((blk, 4 * GW), lambda b, j: (b * nb + j, C_A // (4 * GW))), vec, vec],
        out_specs=[pl.BlockSpec((blk, GW), lambda b, j: (b * nb + j, 0)),
                   pl.BlockSpec((None, GW, GW), lambda b, j: (b * nb + j, 0, 0))],
        out_shape=[_sds((T, GW), BF16), _sds((B * nb, GW, GW), F32)],
        scratch_shapes=[pltpu.VMEM((GW, GW), F32)],
        compiler_params=_cparams(VMEM_BIG))(proj, lb, ng)


def hgrn_bwd(proj, states, dcat, lb, ng, S):
    T = proj.shape[0]
    B = T // S
    blk = min(HGRN_BLK, S)
    nb = S // blk

    def body(p_ref, st_in_ref, do_ref, lb_ref, ng_ref, dp_ref, dlb_ref, dng_ref, dst_ref):
        b, j = pl.program_id(0), pl.program_id(1)

        @pl.when(j == 0)
        def _():
            dst_ref[...] = jnp.zeros_like(dst_ref)

        @pl.when((b == 0) & (j == 0))
        def _():
            dlb_ref[...] = jnp.zeros_like(dlb_ref)
            dng_ref[...] = jnp.zeros_like(dng_ref)

        p = p_ref[...]
        f = functools.partial(_hgrn_block, blk=blk)
        _, vjp = jax.vjp(f, p[:, 0:GW], p[:, GW:2 * GW], p[:, 2 * GW:3 * GW], p[:, 3 * GW:4 * GW],
                         st_in_ref[...], lb_ref[...], ng_ref[...])
        dq, df, di, dg, dst, dlb, dng = vjp((do_ref[...], dst_ref[...]))
        dp_ref[...] = jnp.concatenate([dq, df, di, dg], axis=1).astype(BF16)
        dst_ref[...] = dst
        dlb_ref[...] += dlb
        dng_ref[...] += dng

    def rev(b, j):
        return b * nb + (nb - 1 - j)

    vec = pl.BlockSpec((1, GW), lambda b, j: (0, 0))
    return pl.pallas_call(
        body, name="hgrn_bwd", grid=(B, nb),
        in_specs=[pl.BlockSpec((blk, 4 * GW), lambda b, j: (rev(b, j), C_A // (4 * GW))),
                  pl.BlockSpec((None, GW, GW), lambda b, j: (rev(b, j), 0, 0)),
                  pl.BlockSpec((blk, GW), lambda b, j: (rev(b, j), O_A // GW)), vec, vec],
        out_specs=[pl.BlockSpec((blk, 4 * GW), lambda b, j: (rev(b, j), 0)), vec, vec],
        out_shape=[_sds((T, 4 * GW), BF16), _sds((1, GW), F32), _sds((1, GW), F32)],
        scratch_shapes=[pltpu.VMEM((GW, GW), F32)],
        compiler_params=_cparams(VMEM_BIG))(proj, states, dcat, lb, ng)


ATT_TQ = 256


ATT_BANDS = 8


def _attn_block(q, k, v, cum, qpos0, scale, use_cum, n_free):
    s = _mm_nt(q, k) * scale
    if use_cum:
        s = s - cum
    band = s[:, n_free:]
    visible = _iota(band.shape, 1) <= (qpos0 - n_free) + _iota(band.shape, 0)
    band = jnp.where(visible, band, -jnp.inf)
    m = jnp.max(band, axis=-1, keepdims=True)
    if n_free:
        free = s[:, :n_free]
        m = jnp.maximum(m, jnp.max(free, axis=-1, keepdims=True))
    if not use_cum:
        m = lax.stop_gradient(m)
    e = jnp.exp(band - m)
    denom = jnp.sum(e, axis=-1, keepdims=True)
    o = _mm(e, v[n_free:])
    if n_free:
        e = jnp.exp(free - m)
        denom = denom + jnp.sum(e, axis=-1, keepdims=True)
        o = o + _mm(e, v[:n_free])
    return o * (1.0 / denom)


def _bands(S, tq):
    nq = S // tq
    nb = min(ATT_BANDS, nq)
    per = nq // nb
    return [(r * per, (r + 1) * per, (r + 1) * per * tq) for r in range(nb)]


def attn_fwd(qa, qo, ka, ko, va, vo, cum, scale, S):
    T = qa.shape[0]
    B = T // S
    tq = min(ATT_TQ, S)
    nq = S // tq
    use_cum = cum is not None

    def body(*refs):
        if use_cum:
            q_ref, k_ref, v_ref, c_ref, o_ref = refs
        else:
            (q_ref, k_ref, v_ref, o_ref), c_ref = refs, None
        h, i = pl.program_id(1), pl.program_id(2)
        for lo, hi, kw in _bands(S, tq):
            @pl.when((i >= lo) & (i < hi))
            def _():
                crow = c_ref[pl.ds(h, 1), 0:kw] if use_cum else None
                o = _attn_block(q_ref[...], k_ref[0:kw, :], v_ref[0:kw, :], crow, i * tq, scale, use_cum, lo * tq)
                o_ref[...] = o.astype(BF16)

    in_specs = [pl.BlockSpec((tq, HP), lambda b, h, i: (b * nq + i, qo + h)),
                pl.BlockSpec((S, HP), lambda b, h, i: (b, ko + h)),
                pl.BlockSpec((S, HP), lambda b, h, i: (b, vo + h))]
    args = [qa, ka, va]
    if use_cum:
        in_specs.append(pl.BlockSpec((None, 8, S), lambda b, h, i: (b, 0, 0)))
        args.append(cum)
    return pl.pallas_call(
        body, name="attn_fwd", grid=(B, HEADS, nq), in_specs=in_specs,
        out_specs=pl.BlockSpec((tq, HP), lambda b, h, i: (b * nq + i, h)),
        out_shape=_sds((T, HEADS * HP), BF16),
        compiler_params=_cparams(VMEM_BIG))(*args)


def attn_bwd(qa, qo, ka, ko, va, vo, cum, dcat, do_off, scale, S, out_dtype):
    T = qa.shape[0]
    B = T // S
    tq = min(ATT_TQ, S)
    nq = S // tq
    use_cum = cum is not None

    def body(*refs):
        if use_cum:
            q_ref, k_ref, v_ref, do_ref, c_ref, dq_ref, dk_ref, dv_ref, dc_ref, dk_acc, dv_acc = refs
        else:
            q_ref, k_ref, v_ref, do_ref, dq_ref, dk_ref, dv_ref, dk_acc, dv_acc = refs
        h, i = pl.program_id(1), pl.program_id(2)

        @pl.when(i == 0)
        def _():
            dk_acc[...] = jnp.zeros_like(dk_acc)
            dv_acc[...] = jnp.zeros_like(dv_acc)
            if use_cum:
                dc_ref[...] = jnp.zeros_like(dc_ref)

        for lo, hi, kw in _bands(S, tq):
            @pl.when((i >= lo) & (i < hi))
            def _():
                crow = c_ref[pl.ds(h, 1), 0:kw] if use_cum else jnp.zeros((1, kw), F32)
                f = functools.partial(_attn_block, qpos0=i * tq, scale=scale, use_cum=use_cum, n_free=lo * tq)
                _, vjp = jax.vjp(f, q_ref[...], k_ref[0:kw, :], v_ref[0:kw, :], crow)
                dq, dk, dv, dc = vjp(do_ref[...])
                dq_ref[...] = dq.astype(out_dtype)
                dk_acc[0:kw, :] += dk
                dv_acc[0:kw, :] += dv
                if use_cum:
                    dc_ref[:, 0:kw] += dc

        @pl.when(i == nq - 1)
        def _():
            dk_ref[...] = dk_acc[...].astype(out_dtype)
            dv_ref[...] = dv_acc[...].astype(out_dtype)

    qspec = pl.BlockSpec((tq, HP), lambda b, h, i: (b * nq + i, qo + h))
    in_specs = [qspec, pl.BlockSpec((S, HP), lambda b, h, i: (b, ko + h)),
                pl.BlockSpec((S, HP), lambda b, h, i: (b, vo + h)),
                pl.BlockSpec((tq, HP), lambda b, h, i: (b * nq + i, do_off + h))]
    args = [qa, ka, va, dcat]
    kv_out = pl.BlockSpec((S, HP), lambda b, h, i: (b, h))
    out_specs = [pl.BlockSpec((tq, HP), lambda b, h, i: (b * nq + i, h)), kv_out, kv_out]
    out_shape = [_sds((T, HEADS * HP), out_dtype)] * 3
    if use_cum:
        in_specs.append(pl.BlockSpec((None, 8, S), lambda b, h, i: (b, 0, 0)))
        args.append(cum)
        out_specs.append(pl.BlockSpec((None, 1, S), lambda b, h, i: (b * HEADS + h, 0, 0)))
        out_shape.append(_sds((B * HEADS, 1, S), F32))
    return pl.pallas_call(
        body, name="attn_bwd", grid=(B, HEADS, nq), in_specs=in_specs, out_specs=out_specs, out_shape=out_shape,
        scratch_shapes=[pltpu.VMEM((S, HP), F32), pltpu.VMEM((S, HP), F32)],
        compiler_params=_cparams(VMEM_BIG))(*args)


def _tri(n, upper):
    r, c = _iota((n, n), 0), _iota((n, n), 1)
    return ((r <= c) if upper else (r >= c)).astype(F32)


def fox_gate_fwd(proj, bcol, S):
    T = proj.shape[0]
    B = T // S
    ts = _tile(S, 512)
    nt = S // ts

    def body(p_ref, b_ref, o_ref, carry):
        @pl.when(pl.program_id(1) == 0)
        def _():
            carry[...] = jnp.zeros_like(carry)
        cf = jnp.transpose(p_ref[...])[0:8, :]
        lf = jax.nn.log_sigmoid(cf + b_ref[...])
        cum = _mm_hi(lf, _tri(ts, True)) + carry[...]
        o_ref[...] = cum
        carry[...] += jnp.sum(lf, axis=1, keepdims=True)

    return pl.pallas_call(
        body, name="fox_gate_fwd", grid=(B, nt),
        in_specs=[pl.BlockSpec((ts, HP), lambda b, j: (b * nt + j, C_CF // HP)), pl.BlockSpec((8, 1), lambda b, j: (0, 0))],
        out_specs=pl.BlockSpec((None, 8, ts), lambda b, j: (b, 0, j)), out_shape=_sds((B, 8, S), F32),
        scratch_shapes=[pltpu.VMEM((8, 1), F32)],
        compiler_params=_cparams(VMEM_BIG))(proj, bcol)


def fox_gate_bwd(proj, bcol, dcum, S):
    T = proj.shape[0]
    B = T // S
    ts = _tile(S, 512)
    nt = S // ts

    def body(p_ref, b_ref, dc_ref, dp_ref, db_ref, carry):
        b, j = pl.program_id(0), pl.program_id(1)

        @pl.when(j == 0)
        def _():
            carry[...] = jnp.zeros_like(carry)

        @pl.when((b == 0) & (j == 0))
        def _():
            db_ref[...] = jnp.zeros_like(db_ref)

        cf = jnp.transpose(p_ref[...])[0:8, :]
        dc = dc_ref[...]
        dlf = _mm_hi(dc, _tri(ts, False)) + carry[...]
        carry[...] += jnp.sum(dc, axis=1, keepdims=True)
        dcf = dlf * jax.nn.sigmoid(-(cf + b_ref[...]))
        db_ref[...] += jnp.sum(dcf, axis=1, keepdims=True)
        full = jnp.concatenate([dcf, jnp.zeros((HP - 8, ts), F32)], axis=0)
        dp_ref[...] = jnp.transpose(full).astype(BF16)

    def rev(b, j):
        return nt - 1 - j

    return pl.pallas_call(
        body, name="fox_gate_bwd", grid=(B, nt),
        in_specs=[pl.BlockSpec((ts, HP), lambda b, j: (b * nt + rev(b, j), C_CF // HP)),
                  pl.BlockSpec((8, 1), lambda b, j: (0, 0)),
                  pl.BlockSpec((None, 8, ts), lambda b, j: (b, 0, rev(b, j)))],
        out_specs=[pl.BlockSpec((ts, HP), lambda b, j: (b * nt + rev(b, j), 0)), pl.BlockSpec((8, 1), lambda b, j: (0, 0))],
        out_shape=[_sds((T, HP), BF16), _sds((8, 1), F32)],
        scratch_shapes=[pltpu.VMEM((8, 1), F32)],
        compiler_params=_cparams(VMEM_BIG))(proj, bcol, dcum)


def _mla_pre(blk, gq, gkv, wq, wkv, place, cos_q, sin_q, cs_k):
    nq = _rms(blk[:, 0:B_Q_LORA], gq)
    nkv = _rms(blk[:, B_Q_LORA:B_Q_LORA + B_KV_LORA], gkv)
    qq = _mm(nq, wq)
    q = qq[:, 0:HEADS * HP] * cos_q + qq[:, HEADS * HP:] * sin_q
    kv = _mm(nkv, wkv)
    k = kv[:, 0:HEADS * HP] + _mm(blk[:, B_Q_LORA + B_KV_LORA:] * cs_k, place)
    return q, k, kv[:, HEADS * HP:]


def mla_pre_fwd(proj, gq, gkv, wq, wkv, place, cos_q, sin_q, cs_k, S):
    T = proj.shape[0]
    tm = _tile(S, 512)
    tpb = S // tm
    W = HEADS * HP

    def body(p_ref, gq_ref, gkv_ref, wq_ref, wkv_ref, pl_ref, cq_ref, sq_ref, ck_ref, q_ref, k_ref, v_ref):
        q, k, v = _mla_pre(p_ref[...], gq_ref[...], gkv_ref[...], wq_ref[...], wkv_ref[...], pl_ref[...],
                           cq_ref[...], sq_ref[...], ck_ref[...])
        q_ref[...] = q
        k_ref[...] = k
        v_ref[...] = v

    def full(a):
        return pl.BlockSpec(a.shape, lambda i: (0,) * a.ndim)

    tok = pl.BlockSpec((tm, W), lambda i: (i, 0))
    return pl.pallas_call(
        body, name="mla_pre_fwd", grid=(T // tm,),
        in_specs=[pl.BlockSpec((tm, W), lambda i: (i, C_B // W)), full(gq), full(gkv), full(wq), full(wkv), full(place),
                  pl.BlockSpec((tm, W), lambda i: (i % tpb, 0)), pl.BlockSpec((tm, W), lambda i: (i % tpb, 0)),
                  pl.BlockSpec((tm, HP), lambda i: (i % tpb, 0))],
        out_specs=[tok] * 3, out_shape=[_sds((T, W), F32)] * 3,
        compiler_params=_cparams(VMEM_BIG))(proj, gq, gkv, wq, wkv, place, cos_q, sin_q, cs_k)


def mla_pre_bwd(proj, gq, gkv, wq, wkv, place, cos_q, sin_q, cs_k, dq, dk, dv, S):
    T = proj.shape[0]
    tm = _tile(S, 512)
    tpb = S // tm
    W = HEADS * HP

    def body(p_ref, gq_ref, gkv_ref, wq_ref, wkv_ref, pl_ref, cq_ref, sq_ref, ck_ref, dq_ref, dk_ref, dv_ref,
             dp_ref, dgq_ref, dgkv_ref, dwq_ref, dwkv_ref):
        @pl.when(pl.program_id(0) == 0)
        def _():
            for r in (dgq_ref, dgkv_ref, dwq_ref, dwkv_ref):
                r[...] = jnp.zeros_like(r)

        f = functools.partial(_mla_pre, place=pl_ref[...], cos_q=cq_ref[...], sin_q=sq_ref[...], cs_k=ck_ref[...])
        _, vjp = jax.vjp(f, p_ref[...], gq_ref[...], gkv_ref[...], wq_ref[...], wkv_ref[...])
        dp, dgq, dgkv, dwq, dwkv = vjp((dq_ref[...], dk_ref[...], dv_ref[...]))
        dp_ref[...] = dp.astype(BF16)
        dgq_ref[...] += dgq
        dgkv_ref[...] += dgkv
        dwq_ref[...] += dwq
        dwkv_ref[...] += dwkv

    def full(a):
        return pl.BlockSpec(a.shape, lambda i: (0,) * a.ndim)

    tok = pl.BlockSpec((tm, W), lambda i: (i, 0))
    return pl.pallas_call(
        body, name="mla_pre_bwd", grid=(T // tm,),
        in_specs=[pl.BlockSpec((tm, W), lambda i: (i, C_B // W)), full(gq), full(gkv), full(wq), full(wkv), full(place),
                  pl.BlockSpec((tm, W), lambda i: (i % tpb, 0)), pl.BlockSpec((tm, W), lambda i: (i % tpb, 0)),
                  pl.BlockSpec((tm, HP), lambda i: (i % tpb, 0)), tok, tok, tok],
        out_specs=[tok, full(gq), full(gkv), full(wq), full(wkv)],
        out_shape=[_sds((T, W), BF16), _sds(gq.shape, F32), _sds(gkv.shape, F32), _sds(wq.shape, F32), _sds(wkv.shape, F32)],
        compiler_params=_cparams(VMEM_BIG))(proj, gq, gkv, wq, wkv, place, cos_q, sin_q, cs_k, dq, dk, dv)


def _gmlp_block(blk, lg, lb, ws, bs):
    u = jax.nn.gelu(blk[:, 0:GW])
    v = jax.nn.gelu(blk[:, GW:2 * GW])
    mu = jnp.mean(v, axis=-1, keepdims=True)
    var = jnp.mean(jnp.square(v - mu), axis=-1, keepdims=True)
    vn = (v - mu) * lax.rsqrt(var + LN_EPS) * lg + lb
    causal = _iota((D_CHUNK, D_CHUNK), 1) <= _iota((D_CHUNK, D_CHUNK), 0)
    group = _iota((1, GW), 1) // HD
    mixed = jnp.zeros((D_CHUNK, GW), F32)
    for g in range(HEADS):
        part = _mm(jnp.where(causal, ws[g], 0.0), vn) + bs[g]
        mixed = mixed + jnp.where(group == g, part, 0.0)
    return u * mixed


def gmlp_fwd(proj, lg, lb, ws, bs):
    T = proj.shape[0]

    def body(p_ref, lg_ref, lb_ref, ws_ref, bs_ref, o_ref):
        o_ref[...] = _gmlp_block(p_ref[...], lg_ref[...], lb_ref[...], ws_ref[...], bs_ref[...]).astype(BF16)

    def full(a):
        return pl.BlockSpec(a.shape, lambda i: (0,) * a.ndim)

    return pl.pallas_call(
        body, name="gmlp_fwd", grid=(T // D_CHUNK,),
        in_specs=[pl.BlockSpec((D_CHUNK, 2 * GW), lambda i: (i, C_D // (2 * GW))), full(lg), full(lb), full(ws), full(bs)],
        out_specs=pl.BlockSpec((D_CHUNK, GW), lambda i: (i, 0)), out_shape=_sds((T, GW), BF16),
        compiler_params=_cparams(VMEM_BIG))(proj, lg, lb, ws, bs)


def gmlp_bwd(proj, lg, lb, ws, bs, dcat):
    T = proj.shape[0]

    def body(p_ref, lg_ref, lb_ref, ws_ref, bs_ref, do_ref, dp_ref, dlg_ref, dlb_ref, dws_ref, dbs_ref):
        @pl.when(pl.program_id(0) == 0)
        def _():
            for r in (dlg_ref, dlb_ref, dws_ref, dbs_ref):
                r[...] = jnp.zeros_like(r)

        _, vjp = jax.vjp(_gmlp_block, p_ref[...], lg_ref[...], lb_ref[...], ws_ref[...], bs_ref[...])
        dp, dlg, dlb, dws, dbs = vjp(do_ref[...])
        dp_ref[...] = dp.astype(BF16)
        dlg_ref[...] += dlg
        dlb_ref[...] += dlb
        dws_ref[...] += dws
        dbs_ref[...] += dbs

    def full(a):
        return pl.BlockSpec(a.shape, lambda i: (0,) * a.ndim)

    return pl.pallas_call(
        body, name="gmlp_bwd", grid=(T // D_CHUNK,),
        in_specs=[pl.BlockSpec((D_CHUNK, 2 * GW), lambda i: (i, C_D // (2 * GW))), full(lg), full(lb), full(ws), full(bs),
                  pl.BlockSpec((D_CHUNK, GW), lambda i: (i, O_D // GW))],
        out_specs=[pl.BlockSpec((D_CHUNK, 2 * GW), lambda i: (i, 0)), full(lg), full(lb), full(ws), full(bs)],
        out_shape=[_sds((T, 2 * GW), BF16), _sds(lg.shape, F32), _sds(lb.shape, F32), _sds(ws.shape, F32), _sds(bs.shape, F32)],
        compiler_params=_cparams(VMEM_BIG))(proj, lg, lb, ws, bs, dcat)


def _lb_all(logits):
    m = jnp.max(logits, axis=0, keepdims=True)
    e = jnp.exp(logits - m)
    sm = e / jnp.sum(e, axis=0, keepdims=True)
    return jnp.concatenate([sm[0:1] - sm[0:1], (sm[0:1] + sm[1:2]) - sm[0:1]], axis=0)


def lb_fwd(logits):
    def body(l_ref, o_ref):
        o_ref[...] = _lb_all(l_ref[...])

    return pl.pallas_call(body, name="lb_fwd", out_shape=_sds(logits.shape, F32))(logits)


def lb_bwd(logits, dlb):
    def body(l_ref, d_ref, o_ref):
        _, vjp = jax.vjp(_lb_all, l_ref[...])
        o_ref[...] = vjp(d_ref[...])[0]

    return pl.pallas_call(body, name="lb_bwd", out_shape=_sds(logits.shape, F32))(logits, dlb)


def ada_grad(c_all, dmod_cols):
    N = dmod_cols.shape[1]
    tn = _tile(N, 1152)

    def body(c_ref, d_ref, o_ref):
        h = jax.nn.silu(c_ref[...]).astype(BF16)
        o_ref[...] = lax.dot_general(h, d_ref[...].astype(BF16), (((0,), (0,)), ((), ())), preferred_element_type=F32)

    nb = c_all.shape[0]
    return pl.pallas_call(
        body, name="ada_grad", grid=(N // tn,),
        in_specs=[pl.BlockSpec((nb, D), lambda j: (0, 0)), pl.BlockSpec((nb, tn), lambda j: (0, j))],
        out_specs=pl.BlockSpec((D, tn), lambda j: (0, j)), out_shape=_sds((D, N), F32),
        compiler_params=_cparams(VMEM_BIG))(c_all, dmod_cols)


def sum_slots(a, n, name):
    _, R, C = a.shape
    tr = _row_tile(R, C, n)

    def body(a_ref, o_ref):
        acc = a_ref[0]
        for k in range(1, n):
            acc = acc + a_ref[k]
        o_ref[...] = acc

    return pl.pallas_call(
        body, name=name, grid=(R // tr,),
        in_specs=[pl.BlockSpec((n, tr, C), lambda i: (0, i, 0))],
        out_specs=pl.BlockSpec((tr, C), lambda i: (i, 0)), out_shape=_sds((R, C), F32),
        compiler_params=_cparams(VMEM_BIG))(a)


def add2(a, b, name):
    shp = a.shape
    C = shp[-1]
    a2, b2 = a.reshape(-1, C), b.reshape(-1, C)
    R = a2.shape[0]
    tr = _row_tile(R, C)

    def body(a_ref, b_ref, o_ref):
        o_ref[...] = a_ref[...] + b_ref[...]

    spec = pl.BlockSpec((tr, C), lambda i: (i, 0))
    return pl.pallas_call(body, name=name, grid=(R // tr,), in_specs=[spec, spec], out_specs=spec,
                          out_shape=_sds((R, C), F32), compiler_params=_cparams(VMEM_BIG))(a2, b2).reshape(shp)


def _row_tile(R, C=D, n=1, mult=8, elems=1 << 18):
    limit = max(mult, elems // (C * n))
    for t in range(limit - limit % mult, mult - 1, -mult):
        if R % t == 0:
            return t
    return R


def adamw(w, g, m, v, name, echo=False):
    R, C = w.shape
    tr = _row_tile(R, C, elems=1 << 19)
    c1 = 1.0 - ADAM_B1 ** ADAM_STEP
    c2 = 1.0 - ADAM_B2 ** ADAM_STEP
    n_out = 4 if echo else 3

    def body(w_ref, g_ref, m_ref, v_ref, d_ref, nm_ref, nv_ref, *g_out):
        g_ = g_ref[...]
        nm = ADAM_B1 * m_ref[...] + (1.0 - ADAM_B1) * g_
        nv = ADAM_B2 * v_ref[...] + (1.0 - ADAM_B2) * jnp.square(g_)
        d_ref[...] = -ADAM_LR * ((nm / c1) / (jnp.sqrt(nv / c2) + ADAM_EPS) + ADAM_WD * w_ref[...])
        nm_ref[...] = nm
        nv_ref[...] = nv
        if echo:
            g_out[0][...] = g_

    spec = pl.BlockSpec((tr, C), lambda i: (i, 0))
    return pl.pallas_call(body, name=name, grid=(R // tr,), in_specs=[spec] * 4, out_specs=[spec] * n_out,
                          out_shape=[_sds((R, C), F32)] * n_out, compiler_params=_cparams(VMEM_BIG))(w, g, m, v)


def _rot_cols(w):
    return jnp.concatenate([-w[:, 16:32], w[:, 0:16]], axis=1)


def _fold_rot(d):
    return jnp.concatenate([d[:, 16:32], -d[:, 0:16]], axis=1)


def _pad_heads(w, off, axis):
    parts = []
    for h in range(HEADS):
        piece = lax.slice_in_dim(w, off + HD * h, off + HD * (h + 1), axis=axis)
        parts += [piece, jnp.zeros_like(piece)]
    return parts


def _unpad_heads(d, off, axis):
    return [lax.slice_in_dim(d, off + HP * h, off + HP * h + HD, axis=axis) for h in range(HEADS)]


def mix_in_ext(w):
    z = lambda n: jnp.zeros((w.shape[0], n), w.dtype)
    kr = w[:, 1408:1440]
    cols = [w[:, 0:1408], kr, _rot_cols(kr), z(64)]
    cols += _pad_heads(w, 1440, 1) + _pad_heads(w, 1696, 1) + _pad_heads(w, 1952, 1)
    cols += [w[:, 2212:2724], w[:, 2208:2212], z(NP - C_CF - HEADS)]
    return jnp.concatenate(cols, axis=1)


def mix_in_unext(d):
    kr = d[:, 1408:1440] + _fold_rot(d[:, 1440:1472])
    cols = [d[:, 0:1408], kr] + _unpad_heads(d, C_CQ, 1) + _unpad_heads(d, C_CK, 1) + _unpad_heads(d, C_CV, 1)
    cols += [d[:, C_CF:C_CF + HEADS], d[:, C_D:C_D + 2 * GW]]
    return jnp.concatenate(cols, axis=1)


def mix_out_ext(w):
    return jnp.concatenate([w[0:GW]] + _pad_heads(w, GW, 0) + _pad_heads(w, 2 * GW, 0) + [w[3 * GW:4 * GW]], axis=0)


def mix_out_unext(d):
    return jnp.concatenate([d[0:GW]] + _unpad_heads(d, O_B, 0) + _unpad_heads(d, O_C, 0) + [d[O_D:O_D + GW]], axis=0)


def uq_ext(w):
    z = lambda n: jnp.zeros((w.shape[0], n), w.dtype)
    a, b = [], []
    for h in range(HEADS):
        o = (B_NOPE + B_ROPE) * h
        a += [w[:, o:o + B_NOPE + B_ROPE], z(32)]
        b += [z(B_NOPE), _rot_cols(w[:, o + B_NOPE:o + B_NOPE + B_ROPE]), z(32)]
    return jnp.concatenate(a + b, axis=1)


def uq_unext(d):
    cols = []
    for h in range(HEADS):
        o = HP * h
        cols += [d[:, o:o + B_NOPE], d[:, o + B_NOPE:o + B_NOPE + B_ROPE]
                 + _fold_rot(d[:, HEADS * HP + o + B_NOPE:HEADS * HP + o + B_NOPE + B_ROPE])]
    return jnp.concatenate(cols, axis=1)


def ukv_ext(w):
    z = jnp.zeros((w.shape[0], HD), w.dtype)
    k, v = [], []
    for h in range(HEADS):
        k += [w[:, 2 * HD * h:2 * HD * h + HD], z]
        v += [w[:, 2 * HD * h + HD:2 * HD * (h + 1)], z]
    return jnp.concatenate(k + v, axis=1)


def ukv_unext(d):
    cols = []
    for h in range(HEADS):
        cols += [d[:, HP * h:HP * h + HD], d[:, HEADS * HP + HP * h:HEADS * HP + HP * h + HD]]
    return jnp.concatenate(cols, axis=1)


def rope_tables(S):
    half = B_ROPE // 2
    inv_freq = ROPE_THETA ** (-jnp.arange(half, dtype=F32) / half)
    ang = jnp.arange(S).astype(F32)[:, None] * inv_freq[None, :]
    cos = jnp.tile(jnp.cos(ang), (1, 2))
    sin = jnp.tile(jnp.sin(ang), (1, 2))
    one, zero = jnp.ones((S, B_NOPE), F32), jnp.zeros((S, B_NOPE), F32)
    z32 = jnp.zeros((S, 32), F32)
    cos_q = jnp.tile(jnp.concatenate([one, cos, z32], axis=1), (1, HEADS))
    sin_q = jnp.tile(jnp.concatenate([zero, sin, z32], axis=1), (1, HEADS))
    cs_k = jnp.concatenate([cos, sin, zero], axis=1)
    place = np.zeros((HP, HEADS * HP), np.float32)
    for h in range(HEADS):
        for j in range(B_ROPE):
            place[j, h * HP + B_NOPE + j] = 1.0
            place[B_ROPE + j, h * HP + B_NOPE + j] = 1.0
    return cos_q, sin_q, cs_k, jnp.asarray(place, BF16)


def layer_fwd(x, mod, get, tabs, S):
    cos_q, sin_q, cs_k, place = tabs
    p = dict(get("ffn1", x))
    l = p["wl"]
    zg1, zu1, act1, *t1 = ffn_in_fwd(x, mod, p["ffn1_in"], l, 0, 1, S)
    y1, x1 = out_ln_fwd(act1, p["ffn1_out"], x, mod, p["ln_g"][0:1], p["ln_b"][0:1], 2, 0.5, S, l)
    p.update(get("mix", x1))
    proj, h2_t = mix_in_fwd(x1, mod, p["mix_in"], 3, 4, S)
    o_a, states = hgrn_fwd(proj, p["lb"], p["ng"], S)
    q_b, k_b, v_b = mla_pre_fwd(proj, p["gq"], p["gkv"], p["wq"], p["wkv"], place, cos_q, sin_q, cs_k, S)
    o_b = attn_fwd(q_b, 0, k_b, 0, v_b, 0, None, (B_NOPE + B_ROPE) ** -0.5, S)
    cum = fox_gate_fwd(proj, p["bcol"], S)
    o_c = attn_fwd(proj, C_CQ // HP, proj, C_CK // HP, proj, C_CV // HP, cum, HD ** -0.5, S)
    o_d = gmlp_fwd(proj, p["g_lg"], p["g_lb"], p["ws"], p["bs"])
    cat = jnp.concatenate([o_a, o_b, o_c, o_d], axis=1)
    y2, x2 = out_ln_fwd(cat, p["mix_out"], x1, mod, p["ln_g"][1:2], p["ln_b"][1:2], 5, 1.0, S)
    p.update(get("ffn2", x2))
    zg3, zu3, act3, *t3 = ffn_in_fwd(x2, mod, p["ffn2_in"], l, 6, 7, S)
    y3, x3 = out_ln_fwd(act3, p["ffn2_out"], x2, mod, p["ln_g"][2:3], p["ln_b"][2:3], 8, 0.5, S, l)
    saved = dict(x=x, zg1=zg1, zu1=zu1, t1=t1, y1=y1, x1=x1, proj=proj, h2_t=h2_t, states=states, q_b=q_b, k_b=k_b,
                 v_b=v_b, cum=cum, cat=cat, y2=y2, x2=x2, zg3=zg3, zu3=zu3, t3=t3, y3=y3, p=p)
    return x3, saved


def _ffn_bwd(dxn, x_in, y, zg, zu, transposed, mod, w_in, w_out, l, lg, lb, idx, S, emit):
    sh, sc, gate = idx
    h_t, act_t = transposed
    dres, dy, dgate, dlg, dlb = ln_res_bwd(dxn, x_in, y, mod, lg, lb, gate, 0.5, S)
    dzg, dzu = swiglu_bwd(dy, w_out, l, zg, zu, S)
    dw_out = tn_mm(act_t, dy, FF // 2, D).reshape(N_CHIPS, FF // N_CHIPS, D)
    dw_in = tn_mm_shards(h_t, dzg, dzu)
    mod = mod + emit(dw_in, dw_out)
    dx, dsh, dsc = nt_mod_bwd([dzg, dzu], w_in, [0, 2], x_in, mod, dres, sc, S, FF // 2, l)
    return dx, dw_in, dw_out, dlg, dlb, {sh: dsh, sc: dsc, gate: dgate}, mod


def layer_bwd(dx3, mod, sv, tabs, S, emit):
    cos_q, sin_q, cs_k, place = tabs
    p = sv["p"]
    l = p["wl"]
    g = {}
    dm = {}

    def emit_ffn(part):
        def f(dw_in, dw_out):
            g[part + "_in"], g[part + "_out"] = dw_in, dw_out
            return emit(part, g)
        return f

    dx2, _, _, dlg2, dlb2, d, mod = _ffn_bwd(
        dx3, sv["x2"], sv["y3"], sv["zg3"], sv["zu3"], sv["t3"], mod, p["ffn2_in"], p["ffn2_out"], l,
        p["ln_g"][2:3], p["ln_b"][2:3], (6, 7, 8), S, emit_ffn("ffn2"))
    dm.update(d)
    dres, dy2, dm[5], dlg1, dlb1 = ln_res_bwd(dx2, sv["x1"], sv["y2"], mod, p["ln_g"][1:2], p["ln_b"][1:2], 5, 1.0, S)
    dcat = nt_plain(dy2, p["mix_out"])
    g["mix_out"] = tn_mm(sv["cat"].T, dy2, NCAT // 2, D)
    proj = sv["proj"]
    d_a, g["lb"], g["ng"] = hgrn_bwd(proj, sv["states"], dcat, p["lb"], p["ng"], S)
    dq_c, dk_c, dv_c, dcum = attn_bwd(proj, C_CQ // HP, proj, C_CK // HP, proj, C_CV // HP, sv["cum"], dcat,
                                      O_C // HP, HD ** -0.5, S, BF16)
    B = proj.shape[0] // S
    dcum = jnp.concatenate([dcum.reshape(B, HEADS, S), jnp.zeros((B, 8 - HEADS, S), F32)], axis=1)
    d_cf, g["bcol"] = fox_gate_bwd(proj, p["bcol"], dcum, S)
    dq_b, dk_b, dv_b = attn_bwd(sv["q_b"], 0, sv["k_b"], 0, sv["v_b"], 0, None, dcat, O_B // HP,
                                (B_NOPE + B_ROPE) ** -0.5, S, F32)
    d_b, g["gq"], g["gkv"], g["wq"], g["wkv"] = mla_pre_bwd(
        proj, p["gq"], p["gkv"], p["wq"], p["wkv"], place, cos_q, sin_q, cs_k, dq_b, dk_b, dv_b, S)
    d_d, g["g_lg"], g["g_lb"], g["ws"], g["bs"] = gmlp_bwd(proj, p["g_lg"], p["g_lb"], p["ws"], p["bs"], dcat)
    dproj = jnp.concatenate([d_a, d_b, dq_c, dk_c, dv_c, d_d, d_cf, jnp.zeros_like(d_cf)], axis=1)
    g["mix_in"] = tn_mm(sv["h2_t"], dproj, D, NP_TILE)
    mod = mod + emit("mix", g)
    dx1, dm[3], dm[4] = nt_mod_bwd([dproj], p["mix_in"], [0], sv["x1"], mod, dres, 4, S, NP_TILE)
    last = []

    def emit_last(dw_in, dw_out):
        last.append(emit_ffn("ffn1")(dw_in, dw_out))
        return last[0]

    dx0, _, _, dlg0, dlb0, d, mod = _ffn_bwd(
        dx1, sv["x"], sv["y1"], sv["zg1"], sv["zu1"], sv["t1"], mod, p["ffn1_in"], p["ffn1_out"], l,
        p["ln_g"][0:1], p["ln_b"][0:1], (0, 1, 2), S, emit_last)
    dm.update(d)
    g["ln_g"] = jnp.concatenate([dlg0, dlg1, dlg2], axis=0)
    g["ln_b"] = jnp.concatenate([dlb0, dlb1, dlb2], axis=0)
    dmod = jnp.concatenate([dm[i] for i in range(N_MOD)], axis=1)
    return dx0, dmod, g, last[0]


def local_step(x, c8, tgt, get, lb_logits, S, emit=None):
    B = x.shape[0] // S
    tabs = rope_tables(S)
    lb_all = lb_fwd(lb_logits)
    mods, saved = [], []
    h = x
    for l in range(DEPTH):
        pa = get(l, "ada", h)
        mod = mod_fwd(c8, pa["ada_w"], pa["wl"], pa["ada_b"])[0:B].reshape(B, N_MOD, D)

        def get_l(part, after, l=l):
            p = dict(get(l, part, after))
            if part == "mix":
                p["lb"] = lb_all[l:l + 1]
            return p

        h, sv = layer_fwd(h, mod, get_l, tabs, S)
        mods.append(mod)
        saved.append(sv)
    loss_tile, dh = loss_head(h, tgt)
    grads, dmods, dlb = [None] * DEPTH, [None] * DEPTH, [None] * DEPTH
    tie = jnp.zeros((), F32)
    for l in reversed(range(DEPTH)):
        emit_l = (lambda part, g: jnp.zeros((), F32)) if emit is None else functools.partial(emit, l)
        dh, dmods[l], grads[l], tie = layer_bwd(dh, mods[l] + tie, saved[l], tabs, S, emit_l)
        dlb[l] = grads[l].pop("lb")
    d_logits = lb_bwd(lb_logits, jnp.concatenate(dlb, axis=0))
    return loss_tile, dh, dmods, grads, d_logits


ANY = pl.BlockSpec(memory_space=pl.ANY)


def _place():
    x, y, c = lax.axis_index("x"), lax.axis_index("y"), lax.axis_index("c")
    chips = [(1 - x, y), (x, 1 - y), (1 - x, 1 - y)]
    return x, y, c, chips


def _rcopy(src, dst, sems, k, to):
    send_sems, recv_sems = sems
    return pltpu.make_async_remote_copy(src_ref=src, dst_ref=dst, send_sem=send_sems.at[k], recv_sem=recv_sems.at[k],
                                        device_id=to, device_id_type=MESH)


def _dma_sems(n_remote, n_local):
    return [pltpu.SemaphoreType.DMA((n_remote,)), pltpu.SemaphoreType.DMA((n_remote,)), pltpu.SemaphoreType.DMA((n_local,))]


def own_slot(src, chip):
    L = src.shape[0]
    return lax.dynamic_update_slice(lax.empty((L, N_CHIPS) + src.shape[1:], src.dtype), src[:, None], (0, chip, 0, 0))


def ag_shards(arrs, lands):
    n = len(arrs)
    rh = [a.shape[1] // 2 for a in arrs]

    def body(*refs):
        srcs, outs, token = refs[:n], refs[2 * n:3 * n], refs[3 * n]
        send_sems, recv_sems = refs[3 * n + 1:]
        x, y, c, chips = _place()
        sems = (send_sems, recv_sems)
        me = 2 * x + y
        sibling = (x, y, 1 - c)
        token[...] = jnp.zeros_like(token)

        def part(i, k, hc):
            return outs[i].at[:, k, pl.ds(hc * rh[i], rh[i]), :]

        started = []
        for j, (px, py) in enumerate(chips):
            for i in range(n):
                cp = _rcopy(srcs[i].at[:, pl.ds(c * rh[i], rh[i]), :], part(i, me, c), sems, 6 * i + j, (px, py, c))
                cp.start()
                started.append(cp)
        for j, (px, py) in enumerate(chips):
            k = 2 * px + py
            for i in range(n):
                _rcopy(part(i, k, c), part(i, k, c), sems, 6 * i + j, (px, py, c)).wait_recv()
                cp = _rcopy(part(i, k, c), part(i, k, c), sems, 6 * i + 3 + j, sibling)
                cp.start()
                started.append(cp)
        for j, (px, py) in enumerate(chips):
            k = 2 * px + py
            for i in range(n):
                _rcopy(part(i, k, 1 - c), part(i, k, 1 - c), sems, 6 * i + 3 + j, sibling).wait_recv()
        for cp in started:
            cp.wait_send()

    outs = pl.pallas_call(
        body, name="ag_shards", out_shape=[_sds(a.shape, a.dtype) for a in lands] + [_sds((8, 128), F32)],
        in_specs=[ANY] * (2 * n), out_specs=[ANY] * n + [pl.BlockSpec(memory_space=pltpu.VMEM)],
        input_output_aliases={n + i: i for i in range(n)}, scratch_shapes=_dma_sems(6 * n, 1)[:2])(*arrs, *lands)
    return list(outs[:n]), outs[n]


HBM_SPEC = pl.BlockSpec(memory_space=pltpu.HBM)
SEM_SPEC = pl.BlockSpec(memory_space=pltpu.SEMAPHORE)
DATAFLOW = pltpu.SideEffectType.DATAFLOW_SIDE_EFFECTING


def _after(x, dep):
    return lax.optimization_barrier((x, dep))[0]


def _split_start(srcs, lands, copies, n_copies, dep, name):
    n, m = len(srcs), len(lands)

    def body(*refs):
        ins = refs[:n + m]
        send_sems, recv_sems = refs[n + m + 1], refs[n + m + 2]
        token = refs[-1]
        for k, (src, dst, to) in enumerate(copies(ins[:n], ins[n:], _place())):
            pltpu.make_async_remote_copy(src_ref=src, dst_ref=dst, send_sem=send_sems.at[k], recv_sem=recv_sems.at[k],
                                         device_id=to, device_id_type=MESH).start()
        token[...] = jnp.zeros_like(token)

    arrs = list(srcs) + list(lands)
    outs = pl.pallas_call(
        body, name=name,
        out_shape=(pltpu.SemaphoreType.DMA((n_copies,)), pltpu.SemaphoreType.DMA((n_copies,)),
                   *[pltpu.HBM(a.shape, a.dtype) for a in arrs], _sds((8, 128), F32)),
        in_specs=[HBM_SPEC] * (n + m) + [ANY],
        out_specs=(SEM_SPEC, SEM_SPEC, *[HBM_SPEC] * (n + m), pl.BlockSpec(memory_space=pltpu.VMEM)),
        input_output_aliases={i: 2 + i for i in range(n + m)},
        compiler_params=pltpu.CompilerParams(has_side_effects=DATAFLOW),
    )(*[pltpu.with_memory_space_constraint(a, pltpu.HBM) for a in arrs], dep)
    return outs[0], outs[1], list(outs[2:2 + n]), list(outs[2 + n:2 + n + m]), outs[-1]


def _split_wait(handle, arrivals, after, name):
    send_sems, recv_sems, srcs, lands, _ = handle
    n, m = len(srcs), len(lands)

    def body(*refs):
        ins = refs[:n + m]
        send_sems, recv_sems = refs[n + m], refs[n + m + 1]
        x, y, c, chips = place = _place()
        for k, (src, dst) in enumerate(arrivals(ins[:n], ins[n:], place)):
            cp = pltpu.make_async_remote_copy(src_ref=src, dst_ref=dst, send_sem=send_sems.at[k], recv_sem=recv_sems.at[k],
                                              device_id=(x, y, 1 - c), device_id_type=MESH)
            cp.wait_send()
            cp.wait_recv()

    arrs = list(srcs) + list(lands)
    outs = pl.pallas_call(
        body, name=name, out_shape=[pltpu.HBM(a.shape, a.dtype) for a in arrs],
        in_specs=[HBM_SPEC] * (n + m) + [SEM_SPEC, SEM_SPEC, ANY], out_specs=[HBM_SPEC] * (n + m),
        input_output_aliases={i: i for i in range(n + m)},
        compiler_params=pltpu.CompilerParams(has_side_effects=DATAFLOW),
    )(*arrs, send_sems, recv_sems, after)
    return list(outs[:n]), list(outs[n:])


def _ag_part(ref, k, hc):
    rh = ref.shape[2] // 2
    return ref.at[:, k, pl.ds(hc * rh, rh), :]


def ag_start(srcs, lands, dep, name):
    def copies(s, d, place):
        x, y, c, chips = place
        out = []
        for j, (px, py) in enumerate(chips):
            for i in range(len(s)):
                rh = s[i].shape[1] // 2
                out.append((s[i].at[:, pl.ds(c * rh, rh), :], _ag_part(d[i], 2 * x + y, c), (px, py, c)))
        return out

    return _split_start(srcs, lands, copies, 3 * len(srcs), dep, name)


def ag_wait(handle, after, name):
    def arrivals(s, d, place):
        x, y, c, chips = place
        out = []
        for j, (px, py) in enumerate(chips):
            for i in range(len(s)):
                rh = s[i].shape[1] // 2
                out.append((s[i].at[:, pl.ds(c * rh, rh), :], _ag_part(d[i], 2 * px + py, c)))
        return out

    return _split_wait(handle, arrivals, after, name)


def ag_forward(lands, name):
    n = len(lands)

    def body(*refs):
        bufs, token = refs[n:2 * n], refs[2 * n]
        send_sems, recv_sems = refs[2 * n + 1:]
        x, y, c, chips = _place()
        sems = (send_sems, recv_sems)
        token[...] = jnp.zeros_like(token)
        cps = []
        for j, (px, py) in enumerate(chips):
            for i in range(n):
                part = _ag_part(bufs[i], 2 * px + py, c)
                cps.append(_rcopy(part, part, sems, 3 * i + j, (x, y, 1 - c)))
        for cp in cps:
            cp.start()
        for j, (px, py) in enumerate(chips):
            for i in range(n):
                part = _ag_part(bufs[i], 2 * px + py, 1 - c)
                _rcopy(part, part, sems, 3 * i + j, (x, y, 1 - c)).wait_recv()
        for cp in cps:
            cp.wait_send()

    outs = pl.pallas_call(
        body, name=name, out_shape=[_sds(a.shape, a.dtype) for a in lands] + [_sds((8, 128), F32)],
        in_specs=[ANY] * n, out_specs=[ANY] * n + [pl.BlockSpec(memory_space=pltpu.VMEM)],
        input_output_aliases={i: i for i in range(n)}, scratch_shapes=_dma_sems(3 * n, 1)[:2])(*lands)
    return list(outs[:n]), outs[n]


def rs_start(hs, lands, dep, name):
    def copies(s, d, place):
        x, y, c, chips = place
        return [(s[i].at[2 * px + py], d[i].at[2 * x + y], (px, py, c)) for j, (px, py) in enumerate(chips) for i in range(len(s))]

    return _split_start(hs, lands, copies, 3 * len(hs), dep, name)


def _kept_out(ref, c):
    rh = ref.shape[1] // 2
    return ref.at[:, pl.ds((1 - c) * rh, rh), :]


def swap_start(arrs, lands, dep, name):
    def copies(s, d, place):
        x, y, c, _ = place
        return [(_kept_out(s[i], c), d[i], (x, y, 1 - c)) for i in range(len(s))]

    return _split_start(arrs, lands, copies, len(arrs), dep, name)


def swap_wait(handle, after, name):
    def arrivals(s, d, place):
        x, y, c, _ = place
        return [(_kept_out(s[i], c), d[i]) for i in range(len(s))]

    return _split_wait(handle, arrivals, after, name)


def rs_wait(handle, after, name):
    def arrivals(s, d, place):
        x, y, c, chips = place
        return [(s[i].at[2 * px + py], d[i].at[2 * px + py]) for j, (px, py) in enumerate(chips) for i in range(len(s))]

    return _split_wait(handle, arrivals, after, name)


def sibling_swap(arrs, name):
    n = len(arrs)
    rh = [a.shape[1] // 2 for a in arrs]

    def body(*refs):
        srcs, outs = refs[:n], refs[n:2 * n]
        send_sems, recv_sems = refs[2 * n:]
        x, y, c, _ = _place()
        cps = [_rcopy(srcs[i].at[:, pl.ds((1 - c) * rh[i], rh[i]), :], outs[i], (send_sems, recv_sems), i, (x, y, 1 - c))
               for i in range(n)]
        for cp in cps:
            cp.start()
        for cp in cps:
            cp.wait()

    return pl.pallas_call(
        body, name=name, out_shape=[_sds((N_CHIPS, r, a.shape[2]), a.dtype) for a, r in zip(arrs, rh)],
        in_specs=[ANY] * n, out_specs=[ANY] * n, scratch_shapes=_dma_sems(n, 1)[:2])(*arrs)


def chip_exchange(hs):
    n = len(hs)

    def body(*refs):
        srcs, outs = refs[:n], refs[n:2 * n]
        send_sems, recv_sems, loc_sems = refs[2 * n:]
        x, y, c, chips = _place()
        sems = (send_sems, recv_sems)
        me = 2 * x + y
        mine = [pltpu.make_async_copy(srcs[i].at[me], outs[i].at[me], loc_sems.at[i]) for i in range(n)]
        for cp in mine:
            cp.start()
        sends = []
        for j, (px, py) in enumerate(chips):
            for i in range(n):
                cp = _rcopy(srcs[i].at[2 * px + py], outs[i].at[me], sems, 3 * i + j, (px, py, c))
                cp.start()
                sends.append(cp)
        for j, (px, py) in enumerate(chips):
            for i in range(n):
                _rcopy(srcs[i].at[2 * px + py], outs[i].at[2 * px + py], sems, 3 * i + j, (px, py, c)).wait_recv()
        for cp in sends:
            cp.wait_send()
        for cp in mine:
            cp.wait()

    return pl.pallas_call(
        body, name="chip_exchange", out_shape=[_sds(h.shape, h.dtype) for h in hs],
        in_specs=[ANY] * n, out_specs=[ANY] * n, scratch_shapes=_dma_sems(3 * n, n))(*hs)


def sum_into(land, base, l, core, name):
    _, rh, C = land.shape
    tr = _row_tile(rh, C, N_CHIPS, mult=16)
    nr = rh // tr

    def body(core_ref, land_ref, base_ref, o_ref):
        acc = land_ref[0].astype(F32)
        for k in range(1, N_CHIPS):
            acc = acc + land_ref[k].astype(F32)
        o_ref[...] = acc

    grid_spec = pltpu.PrefetchScalarGridSpec(
        num_scalar_prefetch=1, grid=(nr,),
        in_specs=[pl.BlockSpec((N_CHIPS, tr, C), lambda r, core_ref: (0, r, 0)), ANY],
        out_specs=pl.BlockSpec((None, tr, C), lambda r, core_ref: (l, core_ref[0] * nr + r, 0)))
    return pl.pallas_call(body, name=name, grid_spec=grid_spec, out_shape=_sds(base.shape, base.dtype),
                          input_output_aliases={2: 0}, compiler_params=_cparams(VMEM_BIG))(
        core.reshape(1).astype(jnp.int32), land, base)


def sibling_join(bases, name):
    n = len(bases)

    def body(*refs):
        bufs = refs[n:2 * n]
        send_sems, recv_sems = refs[2 * n:]
        x, y, c, _ = _place()
        sems = (send_sems, recv_sems)

        def half(i, hc):
            rh = bufs[i].shape[1] // 2
            return bufs[i].at[:, pl.ds(hc * rh, rh), :]

        sends = [_rcopy(half(i, c), half(i, c), sems, i, (x, y, 1 - c)) for i in range(n)]
        for cp in sends:
            cp.start()
        for i in range(n):
            _rcopy(half(i, 1 - c), half(i, 1 - c), sems, i, (x, y, 1 - c)).wait_recv()
        for cp in sends:
            cp.wait_send()

    return pl.pallas_call(
        body, name=name, out_shape=[_sds(b.shape, b.dtype) for b in bases], in_specs=[ANY] * n, out_specs=[ANY] * n,
        input_output_aliases={i: i for i in range(n)}, scratch_shapes=_dma_sems(n, 1)[:2])(*bases)


def ag_all(blk):
    M, C = blk.shape

    def body(x_ref, out_ref, send_sems, recv_sems, loc_sem):
        x, y, c, chips = _place()
        sems = (send_sems, recv_sems)
        me, sibling = (x, y, c), (x, y, 1 - c)

        def slot(px, py, pc):
            return out_ref.at[4 * px + 2 * py + pc]

        mine = pltpu.make_async_copy(x_ref, slot(*me), loc_sem)
        mine.start()
        first = [_rcopy(x_ref, slot(*me), sems, 0, sibling)]
        first += [_rcopy(x_ref, slot(*me), sems, 1 + j, (*chip, c)) for j, chip in enumerate(chips)]
        for cp in first:
            cp.start()
        passed = [_rcopy(slot(*chip, c), slot(*chip, c), sems, 4 + j, sibling) for j, chip in enumerate(chips)]
        for j, chip in enumerate(chips):
            _rcopy(slot(*chip, c), slot(*chip, c), sems, 1 + j, me).wait_recv()
            passed[j].start()
        _rcopy(slot(*sibling), slot(*sibling), sems, 0, me).wait_recv()
        for j, chip in enumerate(chips):
            _rcopy(slot(*chip, 1 - c), slot(*chip, 1 - c), sems, 4 + j, me).wait_recv()
        for cp in first + passed:
            cp.wait_send()
        mine.wait()

    return pl.pallas_call(
        body, name="ag_all", out_shape=_sds((8, M, C), blk.dtype),
        in_specs=[pl.BlockSpec(memory_space=pltpu.VMEM)], out_specs=pl.BlockSpec(memory_space=pltpu.VMEM),
        scratch_shapes=[pltpu.SemaphoreType.DMA((7,)), pltpu.SemaphoreType.DMA((7,)), pltpu.SemaphoreType.DMA(())],
        compiler_params=_cparams(VMEM_BIG))(blk)


WEIGHTS = ["ada_w", "ada_b", "ln_g", "ln_b", "ffn1_w_in", "ffn1_w_out", "ffn2_w_in", "ffn2_w_out", "mix_w_in", "mix_w_out",
           "hgrn_lb_logits", "hgrn_norm_g", "mla_q_norm_g", "mla_kv_norm_g", "mla_w_uq", "mla_w_ukv", "fox_b_f",
           "gmlp_ln_g", "gmlp_ln_b", "gmlp_w_s", "gmlp_b_s"]
SHARDED = {"ffn1_w_in": 1, "ffn1_w_out": 0, "ffn2_w_in": 1, "ffn2_w_out": 0, "mix_w_in": 1, "mix_w_out": 0,
           "mla_w_uq": 1, "mla_w_ukv": 1}
SMALL = ["hgrn_lb_logits", "hgrn_norm_g", "mla_q_norm_g", "mla_kv_norm_g", "fox_b_f", "gmlp_ln_g", "gmlp_ln_b",
         "gmlp_w_s", "gmlp_b_s", "ln_g", "ln_b"]
GATHERED = ["ada_w", "ffn1_w_in", "ffn1_w_out", "ffn2_w_in", "ffn2_w_out", "mix_w_in", "mix_w_out", "mla_w_uq", "mla_w_ukv"]
REDUCED = GATHERED[1:]


def _col_shards(a):
    cols = a.shape[1] // N_CHIPS
    return jnp.stack([a[:, k * cols:(k + 1) * cols] for k in range(N_CHIPS)])


def add_kept_half(a, got, core, name):
    _, R, C = a.shape
    rh = R // 2
    tr = _row_tile(rh, C, mult=16)
    nr = rh // tr

    def body(core_ref, a_ref, b_ref, o_ref):
        o_ref[...] = (a_ref[...].astype(F32) + b_ref[...].astype(F32)).astype(o_ref.dtype)

    half = pl.BlockSpec((None, tr, C), lambda k, r, core_ref: (k, r, 0))
    grid_spec = pltpu.PrefetchScalarGridSpec(
        num_scalar_prefetch=1, grid=(N_CHIPS, nr),
        in_specs=[pl.BlockSpec((None, tr, C), lambda k, r, core_ref: (k, core_ref[0] * nr + r, 0)), half],
        out_specs=half)
    return pl.pallas_call(body, name=name, grid_spec=grid_spec, out_shape=_sds((N_CHIPS, rh, C), BF16),
                          compiler_params=_cparams(VMEM_BIG))(core.reshape(1).astype(jnp.int32), a, got)


def _rows(parts, n_rows, dtype):
    flat = jnp.concatenate([p.reshape(-1) for p in parts])
    pad = n_rows * D - flat.shape[0]
    return jnp.concatenate([flat, jnp.zeros((pad,), dtype)]).reshape(n_rows, D)


def _take(flat, shapes):
    out, o = [], 0
    for shp in shapes:
        n = int(np.prod(shp))
        out.append(flat[o:o + n].reshape(shp))
        o += n
    return out


def _round_up(n, m):
    return -(-n // m) * m


def pack_shard(w):
    parts = [w[n][l] for l in range(DEPTH) for n in SHARDED] + [w[n][l] for l in range(DEPTH) for n in ("ln_g", "ln_b")]
    n = sum(int(np.prod(p.shape)) for p in parts)
    return _rows(parts, _round_up(-(-n // D), 16), F32)


def unpack_shard(pk, like):
    shapes = [like[n].shape[1:] for l in range(DEPTH) for n in SHARDED] + [like[n].shape[1:] for l in range(DEPTH) for n in ("ln_g", "ln_b")]
    pieces = _take(pk.reshape(-1), shapes)
    names = [n for l in range(DEPTH) for n in SHARDED] + [n for l in range(DEPTH) for n in ("ln_g", "ln_b")]
    out = {}
    for n in list(SHARDED) + ["ln_g", "ln_b"]:
        out[n] = jnp.stack([p for p, m in zip(pieces, names) if m == n])
    return out


def pack_small(w):
    parts = [w[n][l] for l in range(DEPTH) for n in SMALL]
    n = sum(int(np.prod(p.shape)) for p in parts)
    return _rows(parts, _round_up(-(-n // D), 8), F32)


def unpack_small(pk, like):
    shapes = [like[n].shape[1:] for l in range(DEPTH) for n in SMALL]
    pieces = _take(pk.reshape(-1), shapes)
    names = [n for l in range(DEPTH) for n in SMALL]
    return {n: jnp.stack([p for p, m in zip(pieces, names) if m == n]) for n in SMALL}


def pack_gather(w):
    parts = [w[n][l].astype(BF16) for l in range(DEPTH) for n in ["ada_w"] + list(SHARDED)]
    ln = jnp.concatenate([w[n][l].reshape(-1) for l in range(DEPTH) for n in ("ln_g", "ln_b")])
    parts.append(lax.bitcast_convert_type(ln, BF16))
    n = sum(int(np.prod(p.shape)) for p in parts)
    return _rows(parts, _round_up(-(-n // D), 16), BF16)


def unpack_gather(g, w):
    names = ["ada_w"] + list(SHARDED)
    shapes = [w[n].shape[1:] for l in range(DEPTH) for n in names]
    n_ln = DEPTH * 2 * 3 * (D // N_CHIPS)
    flat = g.reshape(N_CHIPS, -1)
    per_chip = [_take(flat[k], shapes + [(n_ln, 2)]) for k in range(N_CHIPS)]
    layers = [dict() for _ in range(DEPTH)]
    i = 0
    for l in range(DEPTH):
        for n in names:
            axis = 1 if n == "ada_w" else SHARDED[n]
            layers[l][n] = jnp.concatenate([per_chip[k][i] for k in range(N_CHIPS)], axis=axis)
            i += 1
    ln = [lax.bitcast_convert_type(per_chip[k][i], F32).reshape(DEPTH, 2, 3, D // N_CHIPS) for k in range(N_CHIPS)]
    ln = jnp.concatenate(ln, axis=3)
    for l in range(DEPTH):
        layers[l]["ln_g"], layers[l]["ln_b"] = ln[l, 0], ln[l, 1]
    return layers


def pack_grads(grads, k):
    parts = []
    for l in range(DEPTH):
        g = grads[l]
        full = {"ffn1_w_out": g["ffn1_out"], "ffn2_w_out": g["ffn2_out"], "mix_w_in": mix_in_unext(g["mix_in"]),
                "mix_w_out": mix_out_unext(g["mix_out"]), "mla_w_uq": uq_unext(g["wq"]), "mla_w_ukv": ukv_unext(g["wkv"])}
        for n, axis in SHARDED.items():
            if n in ("ffn1_w_in", "ffn2_w_in"):
                half = g[n.replace("_w_in", "_in")][k // 2]
                parts.append(half[:, (k % 2) * (FF // 2):(k % 2 + 1) * (FF // 2)])
            else:
                sz = full[n].shape[axis] // N_CHIPS
                parts.append(lax.slice_in_dim(full[n], k * sz, (k + 1) * sz, axis=axis))
    for l in range(DEPTH):
        for n in ("ln_g", "ln_b"):
            parts.append(grads[l][n][:, k * (D // N_CHIPS):(k + 1) * (D // N_CHIPS)])
    n = sum(int(np.prod(p.shape)) for p in parts)
    return _rows(parts, _round_up(-(-n // D), 16), F32)


def kernel(x, c, ada_w, ada_b, ln_g, ln_b, ffn1_w_in, ffn1_w_out, ffn2_w_in, ffn2_w_out, mix_w_in, mix_w_out, hgrn_lb_logits, hgrn_norm_g, mla_q_norm_g, mla_kv_norm_g, mla_w_uq, mla_w_ukv, fox_b_f, gmlp_ln_g, gmlp_ln_b, gmlp_w_s, gmlp_b_s, loss_target, m_ada_w, m_ada_b, m_ln_g, m_ln_b, m_ffn1_w_in, m_ffn1_w_out, m_ffn2_w_in, m_ffn2_w_out, m_mix_w_in, m_mix_w_out, m_hgrn_lb_logits, m_hgrn_norm_g, m_mla_q_norm_g, m_mla_kv_norm_g, m_mla_w_uq, m_mla_w_ukv, m_fox_b_f, m_gmlp_ln_g, m_gmlp_ln_b, m_gmlp_w_s, m_gmlp_b_s, v_ada_w, v_ada_b, v_ln_g, v_ln_b, v_ffn1_w_in, v_ffn1_w_out, v_ffn2_w_in, v_ffn2_w_out, v_mix_w_in, v_mix_w_out, v_hgrn_lb_logits, v_hgrn_norm_g, v_mla_q_norm_g, v_mla_kv_norm_g, v_mla_w_uq, v_mla_w_ukv, v_fox_b_f, v_gmlp_ln_g, v_gmlp_ln_b, v_gmlp_w_s, v_gmlp_b_s):
    w = dict(zip(WEIGHTS, (ada_w, ada_b, ln_g, ln_b, ffn1_w_in, ffn1_w_out, ffn2_w_in, ffn2_w_out, mix_w_in, mix_w_out, hgrn_lb_logits, hgrn_norm_g, mla_q_norm_g, mla_kv_norm_g, mla_w_uq, mla_w_ukv, fox_b_f, gmlp_ln_g, gmlp_ln_b, gmlp_w_s, gmlp_b_s)))
    m = dict(zip(WEIGHTS, (m_ada_w, m_ada_b, m_ln_g, m_ln_b, m_ffn1_w_in, m_ffn1_w_out, m_ffn2_w_in, m_ffn2_w_out, m_mix_w_in, m_mix_w_out, m_hgrn_lb_logits, m_hgrn_norm_g, m_mla_q_norm_g, m_mla_kv_norm_g, m_mla_w_uq, m_mla_w_ukv, m_fox_b_f, m_gmlp_ln_g, m_gmlp_ln_b, m_gmlp_w_s, m_gmlp_b_s)))
    v = dict(zip(WEIGHTS, (v_ada_w, v_ada_b, v_ln_g, v_ln_b, v_ffn1_w_in, v_ffn1_w_out, v_ffn2_w_in, v_ffn2_w_out, v_mix_w_in, v_mix_w_out, v_hgrn_lb_logits, v_hgrn_norm_g, v_mla_q_norm_g, v_mla_kv_norm_g, v_mla_w_uq, v_mla_w_ukv, v_fox_b_f, v_gmlp_ln_g, v_gmlp_ln_b, v_gmlp_w_s, v_gmlp_b_s)))
    Bl, S, _ = x.shape
    T = Bl * S
    core = lax.axis_index("c")
    chip = 2 * lax.axis_index("x") + lax.axis_index("y")

    def shard(key):
        n, l = key
        if n == "ln":
            return jnp.concatenate([ln_g[l:l + 1], ln_b[l:l + 1], jnp.zeros((1, 2, D // N_CHIPS), F32)], axis=1)
        return w[n][l:l + 1].astype(BF16)

    mixers = ["mix_w_in", "mix_w_out", "mla_w_uq", "mla_w_ukv"]
    groups = [[("ada_w", 0), ("ffn1_w_in", 0), ("ffn1_w_out", 0), ("ln", 0)],
              [(n, 0) for n in mixers + ["ffn2_w_in", "ffn2_w_out"]],
              [(n, 1) for n in GATHERED + ["ln"]]]
    srcs = [[shard(k) for k in grp] for grp in groups]
    lands = [[own_slot(s, chip) for s in grp] for grp in srcs]
    handle0 = ag_start(srcs[0], lands[0], jnp.zeros((8, 128), F32), "ag_start_0")
    first, token = ag_forward(ag_wait(handle0, lands[2][0], "ag_wait_0")[1], "ag_forward_0")
    have = dict(zip(groups[0], first))
    handles = {}
    for gi in (1, 2):
        handles[gi] = ag_start(srcs[gi], lands[gi], token, "ag_start_%d" % gi)
        token = handles[gi][-1]
    c8 = jnp.concatenate([c, jnp.zeros((8 - Bl, D), F32)], axis=0)
    c8 = c8 + token[0, 0]

    def cat_cols(a):
        return jnp.concatenate([a[0, k] for k in range(N_CHIPS)], axis=1)

    def get(l, part, after):
        gi = 2 if l == 1 else (0 if part in ("ada", "ffn1") else 1)
        if gi in handles:
            arrived, _ = ag_forward(ag_wait(handles.pop(gi), after, "ag_wait_%d" % gi)[1], "ag_forward_%d" % gi)
            have.update(zip(groups[gi], arrived))
        if part == "ada":
            return dict(ada_w=have[("ada_w", l)], wl=0, ada_b=ada_b[l][None])
        if part == "ffn1":
            ln_full = jnp.moveaxis(have[("ln", l)][0], 0, 1).reshape(8, D)
            return dict(ffn1_in=have[("ffn1_w_in", l)], ffn1_out=have[("ffn1_w_out", l)], wl=0,
                        ln_g=ln_full[0:3], ln_b=ln_full[3:6])
        if part == "ffn2":
            return dict(ffn2_in=have[("ffn2_w_in", l)], ffn2_out=have[("ffn2_w_out", l)])
        return dict(
            mix_in=mix_in_ext(cat_cols(have[("mix_w_in", l)])), mix_out=mix_out_ext(have[("mix_w_out", l)].reshape(D, D)),
            wq=uq_ext(cat_cols(have[("mla_w_uq", l)])).astype(F32), wkv=ukv_ext(cat_cols(have[("mla_w_ukv", l)])).astype(F32),
            ng=hgrn_norm_g[l][None], gq=mla_q_norm_g[l][None], gkv=mla_kv_norm_g[l][None],
            bcol=jnp.concatenate([fox_b_f[l], jnp.zeros((8 - HEADS,), F32)])[:, None],
            g_lg=gmlp_ln_g[l][None], g_lb=gmlp_ln_b[l][None], ws=gmlp_w_s[l], bs=gmlp_b_s[l][:, :, None])

    pending = []

    def emit(l, part, g):
        if part == "mix":
            names = mixers
            by_chip = [_col_shards(mix_in_unext(g["mix_in"])), mix_out_unext(g["mix_out"]).reshape(N_CHIPS, D // N_CHIPS, D),
                       _col_shards(uq_unext(g["wq"])).astype(BF16), _col_shards(ukv_unext(g["wkv"])).astype(BF16)]
        else:
            names = [part + "_w_in", part + "_w_out"]
            by_chip = [g[part + "_in"], g[part + "_out"]]
        tag = "%d_%s" % (l, part)
        got = sibling_swap(by_chip, "sibling_swap_" + tag)
        chip_sum = [add_kept_half(a, r, core, "add_sibling") for a, r in zip(by_chip, got)]
        zones = [lax.dynamic_update_slice(lax.empty(h.shape, h.dtype), lax.dynamic_slice_in_dim(h, chip, 1, axis=0), (chip, 0, 0))
                 for h in chip_sum]
        handle = rs_start(chip_sum, zones, chip_sum[0], "rs_start_" + tag)
        pending.append((l, names, handle, tag))
        return handle[-1][0, 0]

    loss_tile, dx, dmods, grads, d_logits = local_step(
        x.reshape(T, D), c8, loss_target.reshape(T, D), get, hgrn_lb_logits, S, emit)
    loss = lax.psum(loss_tile[0, 0], ("x", "y", "c"))

    small_g = {"hgrn_lb_logits": d_logits,
               "hgrn_norm_g": jnp.stack([grads[l]["ng"][0] for l in range(DEPTH)]),
               "mla_q_norm_g": jnp.stack([grads[l]["gq"][0] for l in range(DEPTH)]),
               "mla_kv_norm_g": jnp.stack([grads[l]["gkv"][0] for l in range(DEPTH)]),
               "fox_b_f": jnp.stack([grads[l]["bcol"][0:HEADS, 0] for l in range(DEPTH)]),
               "gmlp_ln_g": jnp.stack([grads[l]["g_lg"][0] for l in range(DEPTH)]),
               "gmlp_ln_b": jnp.stack([grads[l]["g_lb"][0] for l in range(DEPTH)]),
               "gmlp_w_s": jnp.stack([grads[l]["ws"] for l in range(DEPTH)]),
               "gmlp_b_s": jnp.stack([grads[l]["bs"][:, :, 0] for l in range(DEPTH)])}
    small_g["ln_g"] = jnp.stack([grads[l]["ln_g"] for l in range(DEPTH)])
    small_g["ln_b"] = jnp.stack([grads[l]["ln_b"] for l in range(DEPTH)])
    pk_small = pack_small(small_g)
    n_small = pk_small.shape[0]
    extras = [dmods[l] for l in range(DEPTH)] + [c]
    n_extra = _round_up(-(-sum(int(np.prod(e.shape)) for e in extras) // D), 8)
    gathered = ag_all(jnp.concatenate([pk_small, _rows(extras, n_extra, F32)], axis=0))
    g_small = unpack_small(sum_slots(gathered[:, 0:n_small], 8, "sum_small"), small_g)
    ext = gathered[:, n_small:].reshape(8, -1)
    n_dmod = DEPTH * Bl * N_MOD * D
    dmod_all = ext[:, 0:n_dmod].reshape(8, DEPTH, Bl, N_MOD * D)
    c_all = ext[:, n_dmod:n_dmod + Bl * D].reshape(8 * Bl, D)
    g_ada_w, g_ada_b = [], []
    ncol = N_MOD * D // N_CHIPS
    for l in range(DEPTH):
        dm = dmod_all[:, l].reshape(8 * Bl, N_MOD * D)
        g_ada_w.append(ada_grad(c_all, lax.dynamic_slice_in_dim(dm, chip * ncol, ncol, axis=1)))
        g_ada_b.append(sum_slots(dm.reshape(8 * Bl, N_MOD, D), 8 * Bl, "sum_ada_b").reshape(N_MOD * D))
    g_ada_w, g_ada_b = jnp.stack(g_ada_w), jnp.stack(g_ada_b)

    red = {n: lax.empty(w[n].shape, F32) for n in REDUCED}

    def arrive(entry, after):
        l, names, handle, tag = entry
        for n, land in zip(names, rs_wait(handle, after, "rs_wait_" + tag)[1]):
            red[n] = sum_into(land, red[n], l, core, "sum_chips")

    for entry in pending[:-1]:
        arrive(entry, dx)
    late = pending[-1][1]
    early = [n for n in REDUCED if n not in late]
    grad = dict(zip(early, sibling_join([red[n] for n in early], "sibling_join_a")))
    grad.update(g_small)
    grad["ada_w"], grad["ada_b"] = g_ada_w, g_ada_b
    for n in ("ln_g", "ln_b"):
        grad[n] = lax.dynamic_slice_in_dim(g_small[n], chip * (D // N_CHIPS), D // N_CHIPS, axis=2)
    out = {"grad": grad, "delta": {}, "new_m": {}, "new_v": {}}

    def update(n):
        shp = w[n].shape
        two_d = (-1, shp[-1])
        res = adamw(w[n].reshape(two_d), grad[n].reshape(two_d), m[n].reshape(two_d), v[n].reshape(two_d), "adamw_" + n,
                    echo=n in REDUCED)
        grad[n] = (res[3] if n in REDUCED else grad[n]).reshape(shp)
        for key, r in zip(("delta", "new_m", "new_v"), res):
            out[key][n] = r.reshape(shp)

    for n in WEIGHTS:
        if n not in late:
            update(n)
    arrive(pending[-1], out["delta"]["ffn2_w_in"])
    grad.update(zip(late, sibling_join([red[n] for n in late], "sibling_join_b")))
    for n in late:
        update(n)
    outs = [loss, dx.reshape(Bl, S, D)]
    for key in ("grad", "delta", "new_m", "new_v"):
        outs += [out[key][n] for n in WEIGHTS]
    return tuple(outs)
```

```python
import functools

import jax
import jax.numpy as jnp
import numpy as np
from jax import lax
from jax.experimental import pallas as pl
from jax.experimental.pallas import tpu as pltpu

F32, BF16 = jnp.float32, jnp.bfloat16
MESH = pl.DeviceIdType.MESH

N_CHIPS = 4
D = 1024
DEPTH = 2
FF = 2816
N_MOD = 9
GW = 256
HEADS = 4
HD = 64
HP = 128
A_CHUNK = 16
LB_FLOOR = 1e-30
B_Q_LORA, B_KV_LORA, B_NOPE, B_ROPE = 256, 128, 64, 32
ROPE_THETA = 10000.0
D_CHUNK = 128
MIX_COLS = 2724
ALPHA = (2 * DEPTH) ** 0.25
LN_EPS = 1e-5
RMS_EPS = 1e-6
ADAM_LR, ADAM_B1, ADAM_B2, ADAM_EPS, ADAM_WD, ADAM_STEP = 0.001, 0.9, 0.999, 1e-08, 0.01, 10

NP = 3840
NP_TILE = 1920
C_A, C_B, C_CQ, C_CK, C_CV, C_D, C_CF = 0, 1024, 1536, 2048, 2560, 3072, 3584
NCAT = 1536
O_A, O_B, O_C, O_D = 0, 256, 768, 1280

VMEM_BIG = 48 << 20


def _cparams(vmem=None):
    return pltpu.CompilerParams(vmem_limit_bytes=vmem) if vmem else pltpu.CompilerParams()


def _sds(shape, dtype):
    return jax.ShapeDtypeStruct(tuple(shape), dtype)


@jax.custom_vjp
def _mm(a, w):
    return jnp.dot(a.astype(BF16), w.astype(BF16), preferred_element_type=F32)


def _mm_f(a, w):
    return _mm(a, w), (a, w)


def _mm_b(res, g):
    a, w = res
    gb = g.astype(BF16)
    da = lax.dot_general(gb, w.astype(BF16), (((1,), (1,)), ((), ())), preferred_element_type=F32)
    dw = lax.dot_general(a.astype(BF16), gb, (((0,), (0,)), ((), ())), preferred_element_type=F32)
    return da.astype(a.dtype), dw.astype(w.dtype)


_mm.defvjp(_mm_f, _mm_b)


@jax.custom_vjp
def _mm_nt(a, b):
    return lax.dot_general(a.astype(BF16), b.astype(BF16), (((1,), (1,)), ((), ())), preferred_element_type=F32)


def _mm_nt_f(a, b):
    return _mm_nt(a, b), (a, b)


def _mm_nt_b(res, g):
    a, b = res
    gb = g.astype(BF16)
    da = jnp.dot(gb, b.astype(BF16), preferred_element_type=F32)
    db = lax.dot_general(gb, a.astype(BF16), (((0,), (0,)), ((), ())), preferred_element_type=F32)
    return da.astype(a.dtype), db.astype(b.dtype)


_mm_nt.defvjp(_mm_nt_f, _mm_nt_b)


@jax.custom_vjp
def _mm_tn(a, b):
    return lax.dot_general(a.astype(BF16), b.astype(BF16), (((0,), (0,)), ((), ())), preferred_element_type=F32)


def _mm_tn_f(a, b):
    return _mm_tn(a, b), (a, b)


def _mm_tn_b(res, g):
    a, b = res
    gb = g.astype(BF16)
    da = lax.dot_general(b.astype(BF16), gb, (((1,), (1,)), ((), ())), preferred_element_type=F32)
    db = jnp.dot(a.astype(BF16), gb, preferred_element_type=F32)
    return da.astype(a.dtype), db.astype(b.dtype)


_mm_tn.defvjp(_mm_tn_f, _mm_tn_b)


def _split3(x):
    p1 = x.astype(BF16)
    r = x - p1.astype(F32)
    p2 = r.astype(BF16)
    return p1, p2, (r - p2.astype(F32)).astype(BF16)


@jax.custom_vjp
def _sel_r(x, sel):
    s = sel.astype(BF16)
    return sum(jnp.dot(p, s, preferred_element_type=F32) for p in _split3(x))


def _sel_r_f(x, sel):
    return _sel_r(x, sel), sel


def _sel_r_b(sel, g):
    s = sel.astype(BF16)
    dx = sum(lax.dot_general(p, s, (((1,), (1,)), ((), ())), preferred_element_type=F32) for p in _split3(g))
    return dx, jnp.zeros_like(sel)


_sel_r.defvjp(_sel_r_f, _sel_r_b)


@jax.custom_vjp
def _sel_l(sel, x):
    s = sel.astype(BF16)
    return sum(jnp.dot(s, p, preferred_element_type=F32) for p in _split3(x))


def _sel_l_f(sel, x):
    return _sel_l(sel, x), sel


def _sel_l_b(sel, g):
    s = sel.astype(BF16)
    dx = sum(lax.dot_general(s, p, (((0,), (0,)), ((), ())), preferred_element_type=F32) for p in _split3(g))
    return jnp.zeros_like(sel), dx


_sel_l.defvjp(_sel_l_f, _sel_l_b)


def _iota(shape, dim):
    return lax.broadcasted_iota(jnp.int32, shape, dim)


def _head_sum_mats():
    e = (_iota((GW, HP), 0) // HD == _iota((GW, HP), 1)).astype(F32)
    et = (_iota((HP, GW), 1) // HD == _iota((HP, GW), 0)).astype(F32)
    return e, et


def _modulate(x, mod_ref, sh, sc):
    return x * (1.0 + mod_ref[sc:sc + 1, :]) + mod_ref[sh:sh + 1, :]


def _ln_res(x, y, gate, lg, lb, gs):
    r = ALPHA * x + gs * (1.0 + gate) * y
    mu = jnp.mean(r, axis=-1, keepdims=True)
    var = jnp.mean(jnp.square(r - mu), axis=-1, keepdims=True)
    return (r - mu) * lax.rsqrt(var + LN_EPS) * lg + lb


def _rms(x, g):
    return x * lax.rsqrt(jnp.mean(x * x, axis=-1, keepdims=True) + RMS_EPS) * g


def _tile(n, pref):
    return pref if n % pref == 0 else n


def mod_fwd(c8, w, l, b):
    tn = w.shape[3]
    n = N_CHIPS * tn

    def body(c_ref, w_ref, b_ref, o_ref):
        h = jax.nn.silu(c_ref[...]).astype(BF16)
        o_ref[...] = jnp.dot(h, w_ref[...], preferred_element_type=F32) + b_ref[...]

    return pl.pallas_call(
        body, name="mod_fwd", grid=(N_CHIPS,),
        in_specs=[pl.BlockSpec((8, D), lambda j: (0, 0)), pl.BlockSpec((None, None, D, tn), lambda j: (l, j, 0, 0)),
                  pl.BlockSpec((1, tn), lambda j: (0, j))],
        out_specs=pl.BlockSpec((8, tn), lambda j: (0, j)), out_shape=_sds((8, n), F32),
        compiler_params=_cparams(VMEM_BIG))(c8, w, b)


def ffn_in_fwd(x, mod, w_in, l, sh, sc, S):
    T = x.shape[0]
    tm, tn = _tile(S, 512), FF // 2
    tpb, nj = S // tm, 2

    def body(x_ref, mod_ref, wg_ref, wu_ref, zg_ref, zu_ref, act_ref, h_ref):
        @pl.when(pl.program_id(1) == 0)
        def _():
            h_ref[...] = _modulate(x_ref[...], mod_ref, sh, sc).astype(BF16)
        g = jnp.dot(h_ref[...], wg_ref[...], preferred_element_type=F32)
        u = jnp.dot(h_ref[...], wu_ref[...], preferred_element_type=F32)
        zg_ref[...] = g.astype(BF16)
        zu_ref[...] = u.astype(BF16)
        act_ref[...] = (jax.nn.silu(g) * u).astype(BF16)

    return pl.pallas_call(
        body, name="ffn_in_fwd", grid=(T // tm, nj),
        in_specs=[pl.BlockSpec((tm, D), lambda i, j: (i, 0)),
                  pl.BlockSpec((None, N_MOD, D), lambda i, j: (i // tpb, 0, 0)),
                  pl.BlockSpec((None, None, D, tn), lambda i, j: (l, j, 0, 0)),
                  pl.BlockSpec((None, None, D, tn), lambda i, j: (l, j + nj, 0, 0))],
        out_specs=[pl.BlockSpec((tm, tn), lambda i, j: (i, j))] * 3,
        out_shape=[_sds((T, FF), BF16)] * 3,
        scratch_shapes=[pltpu.VMEM((tm, D), BF16)],
        compiler_params=_cparams(VMEM_BIG))(x, mod, w_in, w_in)


def mix_in_fwd(x, mod, w, sh, sc, S):
    T = x.shape[0]
    n = w.shape[1]
    tm, tn = _tile(S, 512), NP_TILE
    tpb = S // tm

    def body(x_ref, mod_ref, w_ref, o_ref, h_ref):
        @pl.when(pl.program_id(1) == 0)
        def _():
            h_ref[...] = _modulate(x_ref[...], mod_ref, sh, sc).astype(BF16)
        o_ref[...] = jnp.dot(h_ref[...], w_ref[...], preferred_element_type=F32)

    return pl.pallas_call(
        body, name="mix_in_fwd", grid=(T // tm, n // tn),
        in_specs=[pl.BlockSpec((tm, D), lambda i, j: (i, 0)),
                  pl.BlockSpec((None, N_MOD, D), lambda i, j: (i // tpb, 0, 0)),
                  pl.BlockSpec((D, tn), lambda i, j: (0, j))],
        out_specs=pl.BlockSpec((tm, tn), lambda i, j: (i, j)), out_shape=_sds((T, n), F32),
        scratch_shapes=[pltpu.VMEM((tm, D), BF16)],
        compiler_params=_cparams(VMEM_BIG))(x, mod, w)


def out_ln_fwd(act, w_out, x, mod, lg, lb, gate, gs, S, l=None):
    T, K = act.shape
    tm = _tile(S, 512)
    tpb = S // tm

    def body(a_ref, w_ref, x_ref, mod_ref, lg_ref, lb_ref, y_ref, xn_ref):
        y = jnp.dot(a_ref[...], w_ref[...].reshape(K, D), preferred_element_type=F32)
        y_ref[...] = y
        xn_ref[...] = _ln_res(x_ref[...], y, mod_ref[gate:gate + 1, :], lg_ref[...], lb_ref[...], gs)

    if l is None:
        w_spec = pl.BlockSpec((K, D), lambda i: (0, 0))
    else:
        w_spec = pl.BlockSpec((None, N_CHIPS, K // N_CHIPS, D), lambda i: (l, 0, 0, 0))
    return pl.pallas_call(
        body, name="out_ln_fwd", grid=(T // tm,),
        in_specs=[pl.BlockSpec((tm, K), lambda i: (i, 0)), w_spec,
                  pl.BlockSpec((tm, D), lambda i: (i, 0)),
                  pl.BlockSpec((None, N_MOD, D), lambda i: (i // tpb, 0, 0)),
                  pl.BlockSpec((1, D), lambda i: (0, 0)), pl.BlockSpec((1, D), lambda i: (0, 0))],
        out_specs=[pl.BlockSpec((tm, D), lambda i: (i, 0))] * 2,
        out_shape=[_sds((T, D), F32), _sds((T, D), F32)],
        compiler_params=_cparams(VMEM_BIG))(act, w_out, x, mod, lg, lb)


def ln_res_bwd(dxn, x, y, mod, lg, lb, gate, gs, S):
    T = x.shape[0]
    B = T // S
    tm = _tile(S, 512)
    tpb = S // tm

    def body(d_ref, x_ref, y_ref, mod_ref, lg_ref, lb_ref, dx_ref, dy_ref, dg_ref, dlg_ref, dlb_ref):
        i = pl.program_id(0)
        f = functools.partial(_ln_res, gs=gs)
        _, vjp = jax.vjp(f, x_ref[...], y_ref[...], mod_ref[gate:gate + 1, :], lg_ref[...], lb_ref[...])
        dx, dy, dg, dlg, dlb = vjp(d_ref[...])
        dx_ref[...] = dx
        dy_ref[...] = dy.astype(BF16)

        @pl.when(i % tpb == 0)
        def _():
            dg_ref[...] = jnp.zeros_like(dg_ref)

        @pl.when(i == 0)
        def _():
            dlg_ref[...] = jnp.zeros_like(dlg_ref)
            dlb_ref[...] = jnp.zeros_like(dlb_ref)

        dg_ref[...] += dg
        dlg_ref[...] += dlg
        dlb_ref[...] += dlb

    tok = pl.BlockSpec((tm, D), lambda i: (i, 0))
    vec = pl.BlockSpec((1, D), lambda i: (0, 0))
    return pl.pallas_call(
        body, name="ln_res_bwd", grid=(T // tm,),
        in_specs=[tok, tok, tok, pl.BlockSpec((None, N_MOD, D), lambda i: (i // tpb, 0, 0)), vec, vec],
        out_specs=[tok, tok, pl.BlockSpec((None, 1, D), lambda i: (i // tpb, 0, 0)), vec, vec],
        out_shape=[_sds((T, D), F32), _sds((T, D), BF16), _sds((B, 1, D), F32), _sds((1, D), F32), _sds((1, D), F32)],
        compiler_params=_cparams(VMEM_BIG))(dxn, x, y, mod, lg, lb)


def swiglu_bwd(dy, w_out, l, zg, zu, S):
    T = dy.shape[0]
    tm, tn = _tile(S, 512), FF // 2

    def body(dy_ref, w_ref, zg_ref, zu_ref, dg_ref, du_ref):
        da = lax.dot_general(dy_ref[...], w_ref[...].reshape(tn, D), (((1,), (1,)), ((), ())), preferred_element_type=F32)
        g, u = zg_ref[...].astype(F32), zu_ref[...].astype(F32)
        sg = jax.nn.sigmoid(g)
        dg_ref[...] = (da * u * (sg * (1.0 + g * (1.0 - sg)))).astype(BF16)
        du_ref[...] = (da * (g * sg)).astype(BF16)

    zt = pl.BlockSpec((tm, tn), lambda i, j: (i, j))
    return pl.pallas_call(
        body, name="swiglu_bwd", grid=(T // tm, FF // tn),
        in_specs=[pl.BlockSpec((tm, D), lambda i, j: (i, 0)),
                  pl.BlockSpec((None, 2, FF // N_CHIPS, D), lambda i, j: (l, j, 0, 0)), zt, zt],
        out_specs=[zt, zt], out_shape=[_sds((T, FF), BF16), _sds((T, FF), BF16)],
        compiler_params=_cparams(VMEM_BIG))(dy, w_out, zg, zu)


def nt_plain(dy, w):
    T = dy.shape[0]
    K = w.shape[0]
    tm = _tile(T, 512)

    def body(dy_ref, w_ref, o_ref):
        o_ref[...] = lax.dot_general(dy_ref[...], w_ref[...], (((1,), (1,)), ((), ())), preferred_element_type=F32)

    return pl.pallas_call(
        body, name="nt_plain", grid=(T // tm,),
        in_specs=[pl.BlockSpec((tm, D), lambda i: (i, 0)), pl.BlockSpec((K, D), lambda i: (0, 0))],
        out_specs=pl.BlockSpec((tm, K), lambda i: (i, 0)), out_shape=_sds((T, K), F32),
        compiler_params=_cparams(VMEM_BIG))(dy, w)


def _tn_step(acc, o_ref, lhs, rhs, t, nt):
    part = lax.dot_general(lhs, rhs, (((0,), (0,)), ((), ())), preferred_element_type=F32)
    if nt == 1:
        o_ref[...] = part.astype(o_ref.dtype)
        return

    @pl.when(t == 0)
    def _():
        acc[...] = part

    @pl.when((t > 0) & (t < nt - 1))
    def _():
        acc[...] += part

    @pl.when(t == nt - 1)
    def _():
        o_ref[...] = (acc[...] + part).astype(o_ref.dtype)


def tn_mm(a, b, tk):
    T, K = a.shape
    N = b.shape[1]
    tt = _tile(T, 512)
    nt = T // tt

    def body(a_ref, b_ref, o_ref, acc):
        _tn_step(acc, o_ref, a_ref[...], b_ref[...], pl.program_id(1), nt)

    return pl.pallas_call(
        body, name="tn_mm", grid=(K // tk, nt),
        in_specs=[pl.BlockSpec((tt, tk), lambda k, t: (t, k)), pl.BlockSpec((tt, N), lambda k, t: (t, 0))],
        out_specs=pl.BlockSpec((tk, N), lambda k, t: (k, 0)), out_shape=_sds((K, N), BF16),
        scratch_shapes=[pltpu.VMEM((tk, N), F32)], compiler_params=_cparams(VMEM_BIG))(a, b)


def tn_mm_mod(x, mod, b, sh, sc, S, tn):
    T = x.shape[0]
    N = b.shape[1]
    tt = _tile(S, 512)
    tpb = S // tt
    nt = T // tt

    def body(x_ref, mod_ref, b_ref, o_ref, acc):
        h = _modulate(x_ref[...], mod_ref, sh, sc).astype(BF16)
        _tn_step(acc, o_ref, h, b_ref[...], pl.program_id(1), nt)

    return pl.pallas_call(
        body, name="tn_mm_mod", grid=(N // tn, nt),
        in_specs=[pl.BlockSpec((tt, D), lambda j, t: (t, 0)),
                  pl.BlockSpec((None, N_MOD, D), lambda j, t: (t // tpb, 0, 0)),
                  pl.BlockSpec((tt, tn), lambda j, t: (t, j))],
        out_specs=pl.BlockSpec((D, tn), lambda j, t: (0, j)), out_shape=_sds((D, N), BF16),
        scratch_shapes=[pltpu.VMEM((D, tn), F32)], compiler_params=_cparams(VMEM_BIG))(x, mod, b)


def tn_mm_mod_shards(x, mod, bg, bu, sh, sc, S):
    T = x.shape[0]
    tn = FF // 2
    tt = _tile(S, 512)
    tpb = S // tt
    nt = T // tt

    def body(x_ref, mod_ref, bg_ref, bu_ref, o_ref, acc):
        j, t = pl.program_id(0), pl.program_id(1)
        h = _modulate(x_ref[...], mod_ref, sh, sc).astype(BF16)

        @pl.when(j < 2)
        def _():
            _tn_step(acc, o_ref, h, bg_ref[...], t, nt)

        @pl.when(j >= 2)
        def _():
            _tn_step(acc, o_ref, h, bu_ref[...], t, nt)

    return pl.pallas_call(
        body, name="tn_mm_mod_shards", grid=(N_CHIPS, nt),
        in_specs=[pl.BlockSpec((tt, D), lambda j, t: (t, 0)),
                  pl.BlockSpec((None, N_MOD, D), lambda j, t: (t // tpb, 0, 0)),
                  pl.BlockSpec((tt, tn), lambda j, t: (jnp.where(j < 2, t, 0), jnp.minimum(j, 1))),
                  pl.BlockSpec((tt, tn), lambda j, t: (jnp.where(j < 2, 0, t), jnp.maximum(j - 2, 0)))],
        out_specs=pl.BlockSpec((None, D, tn), lambda j, t: (j, 0, 0)), out_shape=_sds((N_CHIPS, D, tn), BF16),
        scratch_shapes=[pltpu.VMEM((D, tn), F32)], compiler_params=_cparams(VMEM_BIG))(x, mod, bg, bu)


def nt_mod_bwd(ds, w, offs, x, mod, dres, sc, S, tk, l=None):
    T = x.shape[0]
    B = T // S
    tm = _tile(S, 512)
    tpb = S // tm
    Kd = ds[0].shape[1]
    nk = Kd // tk
    n_in = len(ds)

    def body(*refs):
        d_refs, w_refs = refs[:n_in], refs[n_in:2 * n_in]
        x_ref, mod_ref, r_ref, dx_ref, dsh_ref, dsc_ref, acc = refs[2 * n_in:]
        i, k = pl.program_id(0), pl.program_id(1)

        part = sum(lax.dot_general(d_ref[...], w_ref[...], (((1,), (1,)), ((), ())), preferred_element_type=F32)
                   for d_ref, w_ref in zip(d_refs, w_refs))

        @pl.when(k == 0)
        def _():
            acc[...] = part

        @pl.when(k > 0)
        def _():
            acc[...] += part

        @pl.when(k == nk - 1)
        def _():
            dh = acc[...]
            dx_ref[...] = dh * (1.0 + mod_ref[sc:sc + 1, :]) + r_ref[...]

            @pl.when(i % tpb == 0)
            def _():
                dsh_ref[...] = jnp.zeros_like(dsh_ref)
                dsc_ref[...] = jnp.zeros_like(dsc_ref)

            dsh_ref[...] += jnp.sum(dh, axis=0, keepdims=True)
            dsc_ref[...] += jnp.sum(dh * x_ref[...], axis=0, keepdims=True)

    tok = pl.BlockSpec((tm, D), lambda i, k: (i, 0))
    vec = pl.BlockSpec((None, 1, D), lambda i, k: (i // tpb, 0, 0))
    in_specs = [pl.BlockSpec((tm, tk), lambda i, k: (i, k)) for _ in ds]
    if l is None:
        in_specs += [pl.BlockSpec((D, tk), functools.partial(lambda i, k, o: (0, k + o), o=off // tk)) for off in offs]
    else:
        in_specs += [pl.BlockSpec((None, None, D, tk), functools.partial(lambda i, k, o: (l, k + o, 0, 0), o=off)) for off in offs]
    in_specs += [tok, pl.BlockSpec((None, N_MOD, D), lambda i, k: (i // tpb, 0, 0)), tok]
    return pl.pallas_call(
        body, name="nt_mod_bwd", grid=(T // tm, nk), in_specs=in_specs,
        out_specs=[tok, vec, vec],
        out_shape=[_sds((T, D), F32), _sds((B, 1, D), F32), _sds((B, 1, D), F32)],
        scratch_shapes=[pltpu.VMEM((tm, D), F32)],
        compiler_params=_cparams(VMEM_BIG))(*ds, *([w] * n_in), x, mod, dres)


def loss_head(y, tgt):
    T = y.shape[0]
    tm = _tile(T, 512)

    def body(y_ref, t_ref, l_ref, d_ref):
        @pl.when(pl.program_id(0) == 0)
        def _():
            l_ref[...] = jnp.zeros_like(l_ref)
        e = y_ref[...] - t_ref[...]
        d_ref[...] = e * (1.0 / D)
        l_ref[...] += 0.5 * jnp.sum(jnp.sum(e * e, axis=1, keepdims=True) * (1.0 / D))

    tok = pl.BlockSpec((tm, D), lambda i: (i, 0))
    return pl.pallas_call(
        body, name="loss_head", grid=(T // tm,), in_specs=[tok, tok],
        out_specs=[pl.BlockSpec((8, 128), lambda i: (0, 0)), tok],
        out_shape=[_sds((8, 128), F32), _sds((T, D), F32)],
        compiler_params=_cparams(VMEM_BIG))(y, tgt)


def _hgrn_block(q, fz, inp, go, st, lb, ng, blk):
    nc = blk // A_CHUNK
    lb_eff = jnp.maximum(lb, LB_FLOOR)
    log_f = jnp.logaddexp(jnp.log(lb_eff), jnp.log1p(-lb) + jax.nn.log_sigmoid(fz))
    k = (1.0 - lb) * jax.nn.sigmoid(-fz) - (lb_eff - lb)
    qf = jax.nn.silu(q)
    same_chunk = _iota((blk, blk), 0) // A_CHUNK == _iota((blk, blk), 1) // A_CHUNK
    tril = (same_chunk & (_iota((blk, blk), 1) <= _iota((blk, blk), 0))).astype(F32)
    G = _sel_l(tril, log_f)
    e_mat, et_mat = _head_sum_mats()
    G4, q4, k4, v4 = (z.reshape(nc, A_CHUNK, GW) for z in (G, qf, k, inp))
    shp = (nc, A_CHUNK, A_CHUNK, GW)
    one = (1, A_CHUNK, A_CHUNK, GW)
    mask = jnp.where(_iota(one, 2) <= _iota(one, 1), 0.0, -jnp.inf)
    decay = jnp.exp((G4[:, :, None, :] - G4[:, None, :, :]) + mask)
    prod = q4[:, :, None, :] * k4[:, None, :, :] * decay
    scores = _mm(prod.reshape(nc * A_CHUNK * A_CHUNK, GW), e_mat.astype(BF16))
    spread = _mm(scores, et_mat.astype(BF16)).reshape(shp)
    o_intra = jnp.sum(spread * v4[:, None, :, :], axis=2).reshape(blk, GW)
    head_diag = (_iota((GW, GW), 0) // HD == _iota((GW, GW), 1) // HD).astype(F32)
    g_last = [jnp.sum(log_f[c * A_CHUNK:(c + 1) * A_CHUNK], axis=0, keepdims=True) for c in range(nc)]
    g_last_b = jnp.concatenate([jnp.broadcast_to(g, (A_CHUNK, GW)) for g in g_last], axis=0)
    q_dec = qf * jnp.exp(G)
    k_end = k * jnp.exp(g_last_b - G)
    outs = []
    for c in range(nc):
        rows = slice(c * A_CHUNK, (c + 1) * A_CHUNK)
        outs.append(_mm_nt(q_dec[rows], st))
        st = st * jnp.exp(g_last[c]) + _mm_tn(inp[rows], k_end[rows]) * head_diag
    o = o_intra + jnp.concatenate(outs, axis=0)
    ms = _sel_r(o * o, e_mat) * (1.0 / HD)
    o = o * _sel_r(lax.rsqrt(ms + RMS_EPS), et_mat) * ng
    return o * jax.nn.silu(go), st


HGRN_BLK = 128


def hgrn_fwd(proj, lb, ng, S):
    T = proj.shape[0]
    B = T // S
    blk = min(HGRN_BLK, S)
    nb = S // blk

    def body(p_ref, lb_ref, ng_ref, o_ref, st_out_ref, st_ref):
        @pl.when(pl.program_id(1) == 0)
        def _():
            st_ref[...] = jnp.zeros_like(st_ref)
        st_out_ref[...] = st_ref[...]
        p = p_ref[...]
        o, st = _hgrn_block(p[:, 0:GW], p[:, GW:2 * GW], p[:, 2 * GW:3 * GW], p[:, 3 * GW:4 * GW],
                            st_ref[...], lb_ref[...], ng_ref[...], blk)
        o_ref[...] = o.astype(BF16)
        st_ref[...] = st

    vec = pl.BlockSpec((1, GW), lambda b, j: (0, 0))
    return pl.pallas_call(
        body, name="hgrn_fwd", grid=(B, nb),
        in_specs=[pl.BlockSpec((blk, 4 * GW), lambda b, j: (b * nb + j, C_A // (4 * GW))), vec, vec],
        out_specs=[pl.BlockSpec((blk, GW), lambda b, j: (b * nb + j, 0)),
                   pl.BlockSpec((None, GW, GW), lambda b, j: (b * nb + j, 0, 0))],
        out_shape=[_sds((T, GW), BF16), _sds((B * nb, GW, GW), F32)],
        scratch_shapes=[pltpu.VMEM((GW, GW), F32)],
        compiler_params=_cparams(VMEM_BIG))(proj, lb, ng)


def hgrn_bwd(proj, states, dcat, lb, ng, S):
    T = proj.shape[0]
    B = T // S
    blk = min(HGRN_BLK, S)
    nb = S // blk

    def body(p_ref, st_in_ref, do_ref, lb_ref, ng_ref, dp_ref, dlb_ref, dng_ref, dst_ref):
        b, j = pl.program_id(0), pl.program_id(1)

        @pl.when(j == 0)
        def _():
            dst_ref[...] = jnp.zeros_like(dst_ref)

        @pl.when((b == 0) & (j == 0))
        def _():
            dlb_ref[...] = jnp.zeros_like(dlb_ref)
            dng_ref[...] = jnp.zeros_like(dng_ref)

        p = p_ref[...]
        f = functools.partial(_hgrn_block, blk=blk)
        _, vjp = jax.vjp(f, p[:, 0:GW], p[:, GW:2 * GW], p[:, 2 * GW:3 * GW], p[:, 3 * GW:4 * GW],
                         st_in_ref[...], lb_ref[...], ng_ref[...])
        dq, df, di, dg, dst, dlb, dng = vjp((do_ref[...], dst_ref[...]))
        dp_ref[...] = jnp.concatenate([dq, df, di, dg], axis=1).astype(BF16)
        dst_ref[...] = dst
        dlb_ref[...] += dlb
        dng_ref[...] += dng

    def rev(b, j):
        return b * nb + (nb - 1 - j)

    vec = pl.BlockSpec((1, GW), lambda b, j: (0, 0))
    return pl.pallas_call(
        body, name="hgrn_bwd", grid=(B, nb),
        in_specs=[pl.BlockSpec((blk, 4 * GW), lambda b, j: (rev(b, j), C_A // (4 * GW))),
                  pl.BlockSpec((None, GW, GW), lambda b, j: (rev(b, j), 0, 0)),
                  pl.BlockSpec((blk, GW), lambda b, j: (rev(b, j), O_A // GW)), vec, vec],
        out_specs=[pl.BlockSpec((blk, 4 * GW), lambda b, j: (rev(b, j), 0)), vec, vec],
        out_shape=[_sds((T, 4 * GW), BF16), _sds((1, GW), F32), _sds((1, GW), F32)],
        scratch_shapes=[pltpu.VMEM((GW, GW), F32)],
        compiler_params=_cparams(VMEM_BIG))(proj, states, dcat, lb, ng)


ATT_TQ = 256


ATT_BANDS = 8


def _attn_block(q, k, v, cum, qpos0, scale, use_cum, n_free):
    s = _mm_nt(q, k) * scale
    if use_cum:
        s = s - cum
    band = s[:, n_free:]
    visible = _iota(band.shape, 1) <= (qpos0 - n_free) + _iota(band.shape, 0)
    band = jnp.where(visible, band, -jnp.inf)
    m = jnp.max(band, axis=-1, keepdims=True)
    if n_free:
        free = s[:, :n_free]
        m = jnp.maximum(m, jnp.max(free, axis=-1, keepdims=True))
    if not use_cum:
        m = lax.stop_gradient(m)
    e = jnp.exp(band - m)
    denom = jnp.sum(e, axis=-1, keepdims=True)
    o = _mm(e, v[n_free:])
    if n_free:
        e = jnp.exp(free - m)
        denom = denom + jnp.sum(e, axis=-1, keepdims=True)
        o = o + _mm(e, v[:n_free])
    return o * (1.0 / denom)


def _bands(S, tq):
    nq = S // tq
    nb = min(ATT_BANDS, nq)
    per = nq // nb
    return [(r * per, (r + 1) * per, (r + 1) * per * tq) for r in range(nb)]


def attn_fwd(qa, qo, ka, ko, va, vo, cum, scale, S):
    T = qa.shape[0]
    B = T // S
    tq = min(ATT_TQ, S)
    nq = S // tq
    use_cum = cum is not None

    def body(*refs):
        if use_cum:
            q_ref, k_ref, v_ref, c_ref, o_ref = refs
        else:
            (q_ref, k_ref, v_ref, o_ref), c_ref = refs, None
        h, i = pl.program_id(1), pl.program_id(2)
        for lo, hi, kw in _bands(S, tq):
            @pl.when((i >= lo) & (i < hi))
            def _():
                crow = c_ref[pl.ds(h, 1), 0:kw] if use_cum else None
                o = _attn_block(q_ref[...], k_ref[0:kw, :], v_ref[0:kw, :], crow, i * tq, scale, use_cum, lo * tq)
                o_ref[...] = o.astype(BF16)

    in_specs = [pl.BlockSpec((tq, HP), lambda b, h, i: (b * nq + i, qo + h)),
                pl.BlockSpec((S, HP), lambda b, h, i: (b, ko + h)),
                pl.BlockSpec((S, HP), lambda b, h, i: (b, vo + h))]
    args = [qa, ka, va]
    if use_cum:
        in_specs.append(pl.BlockSpec((None, 8, S), lambda b, h, i: (b, 0, 0)))
        args.append(cum)
    return pl.pallas_call(
        body, name="attn_fwd", grid=(B, HEADS, nq), in_specs=in_specs,
        out_specs=pl.BlockSpec((tq, HP), lambda b, h, i: (b * nq + i, h)),
        out_shape=_sds((T, HEADS * HP), BF16),
        compiler_params=_cparams(VMEM_BIG))(*args)


def attn_bwd(qa, qo, ka, ko, va, vo, cum, dcat, do_off, scale, S, out_dtype):
    T = qa.shape[0]
    B = T // S
    tq = min(ATT_TQ, S)
    nq = S // tq
    use_cum = cum is not None

    def body(*refs):
        if use_cum:
            q_ref, k_ref, v_ref, do_ref, c_ref, dq_ref, dk_ref, dv_ref, dc_ref, dk_acc, dv_acc = refs
        else:
            q_ref, k_ref, v_ref, do_ref, dq_ref, dk_ref, dv_ref, dk_acc, dv_acc = refs
        h, i = pl.program_id(1), pl.program_id(2)

        @pl.when(i == 0)
        def _():
            dk_acc[...] = jnp.zeros_like(dk_acc)
            dv_acc[...] = jnp.zeros_like(dv_acc)
            if use_cum:
                dc_ref[...] = jnp.zeros_like(dc_ref)

        for lo, hi, kw in _bands(S, tq):
            @pl.when((i >= lo) & (i < hi))
            def _():
                crow = c_ref[pl.ds(h, 1), 0:kw] if use_cum else jnp.zeros((1, kw), F32)
                f = functools.partial(_attn_block, qpos0=i * tq, scale=scale, use_cum=use_cum, n_free=lo * tq)
                _, vjp = jax.vjp(f, q_ref[...], k_ref[0:kw, :], v_ref[0:kw, :], crow)
                dq, dk, dv, dc = vjp(do_ref[...])
                dq_ref[...] = dq.astype(out_dtype)
                dk_acc[0:kw, :] += dk
                dv_acc[0:kw, :] += dv
                if use_cum:
                    dc_ref[:, 0:kw] += dc

        @pl.when(i == nq - 1)
        def _():
            dk_ref[...] = dk_acc[...].astype(out_dtype)
            dv_ref[...] = dv_acc[...].astype(out_dtype)

    qspec = pl.BlockSpec((tq, HP), lambda b, h, i: (b * nq + i, qo + h))
    in_specs = [qspec, pl.BlockSpec((S, HP), lambda b, h, i: (b, ko + h)),
                pl.BlockSpec((S, HP), lambda b, h, i: (b, vo + h)),
                pl.BlockSpec((tq, HP), lambda b, h, i: (b * nq + i, do_off + h))]
    args = [qa, ka, va, dcat]
    kv_out = pl.BlockSpec((S, HP), lambda b, h, i: (b, h))
    out_specs = [pl.BlockSpec((tq, HP), lambda b, h, i: (b * nq + i, h)), kv_out, kv_out]
    out_shape = [_sds((T, HEADS * HP), out_dtype)] * 3
    if use_cum:
        in_specs.append(pl.BlockSpec((None, 8, S), lambda b, h, i: (b, 0, 0)))
        args.append(cum)
        out_specs.append(pl.BlockSpec((None, 1, S), lambda b, h, i: (b * HEADS + h, 0, 0)))
        out_shape.append(_sds((B * HEADS, 1, S), F32))
    return pl.pallas_call(
        body, name="attn_bwd", grid=(B, HEADS, nq), in_specs=in_specs, out_specs=out_specs, out_shape=out_shape,
        scratch_shapes=[pltpu.VMEM((S, HP), F32), pltpu.VMEM((S, HP), F32)],
        compiler_params=_cparams(VMEM_BIG))(*args)


def _tri(n, upper):
    r, c = _iota((n, n), 0), _iota((n, n), 1)
    return ((r <= c) if upper else (r >= c)).astype(F32)


def fox_gate_fwd(proj, bcol, S):
    T = proj.shape[0]
    B = T // S
    ts = _tile(S, 512)
    nt = S // ts

    def body(p_ref, b_ref, o_ref, carry):
        @pl.when(pl.program_id(1) == 0)
        def _():
            carry[...] = jnp.zeros_like(carry)
        cf = jnp.transpose(p_ref[...])[0:8, :]
        lf = jax.nn.log_sigmoid(cf + b_ref[...])
        cum = _sel_r(lf, _tri(ts, True)) + carry[...]
        o_ref[...] = cum
        carry[...] += jnp.sum(lf, axis=1, keepdims=True)

    return pl.pallas_call(
        body, name="fox_gate_fwd", grid=(B, nt),
        in_specs=[pl.BlockSpec((ts, HP), lambda b, j: (b * nt + j, C_CF // HP)), pl.BlockSpec((8, 1), lambda b, j: (0, 0))],
        out_specs=pl.BlockSpec((None, 8, ts), lambda b, j: (b, 0, j)), out_shape=_sds((B, 8, S), F32),
        scratch_shapes=[pltpu.VMEM((8, 1), F32)],
        compiler_params=_cparams(VMEM_BIG))(proj, bcol)


def fox_gate_bwd(proj, bcol, dcum, S):
    T = proj.shape[0]
    B = T // S
    ts = _tile(S, 512)
    nt = S // ts

    def body(p_ref, b_ref, dc_ref, dp_ref, db_ref, carry):
        b, j = pl.program_id(0), pl.program_id(1)

        @pl.when(j == 0)
        def _():
            carry[...] = jnp.zeros_like(carry)

        @pl.when((b == 0) & (j == 0))
        def _():
            db_ref[...] = jnp.zeros_like(db_ref)

        cf = jnp.transpose(p_ref[...])[0:8, :]
        dc = dc_ref[...]
        dlf = _sel_r(dc, _tri(ts, False)) + carry[...]
        carry[...] += jnp.sum(dc, axis=1, keepdims=True)
        dcf = dlf * jax.nn.sigmoid(-(cf + b_ref[...]))
        db_ref[...] += jnp.sum(dcf, axis=1, keepdims=True)
        full = jnp.concatenate([dcf, jnp.zeros((HP - 8, ts), F32)], axis=0)
        dp_ref[...] = jnp.transpose(full).astype(BF16)

    def rev(b, j):
        return nt - 1 - j

    return pl.pallas_call(
        body, name="fox_gate_bwd", grid=(B, nt),
        in_specs=[pl.BlockSpec((ts, HP), lambda b, j: (b * nt + rev(b, j), C_CF // HP)),
                  pl.BlockSpec((8, 1), lambda b, j: (0, 0)),
                  pl.BlockSpec((None, 8, ts), lambda b, j: (b, 0, rev(b, j)))],
        out_specs=[pl.BlockSpec((ts, HP), lambda b, j: (b * nt + rev(b, j), 0)), pl.BlockSpec((8, 1), lambda b, j: (0, 0))],
        out_shape=[_sds((T, HP), BF16), _sds((8, 1), F32)],
        scratch_shapes=[pltpu.VMEM((8, 1), F32)],
        compiler_params=_cparams(VMEM_BIG))(proj, bcol, dcum)


def _mla_pre(blk, gq, gkv, wq, wkv, place, cos_q, sin_q, cs_k):
    nq = _rms(blk[:, 0:B_Q_LORA], gq)
    nkv = _rms(blk[:, B_Q_LORA:B_Q_LORA + B_KV_LORA], gkv)
    qq = _mm(nq, wq)
    q = qq[:, 0:HEADS * HP] * cos_q + qq[:, HEADS * HP:] * sin_q
    kv = _mm(nkv, wkv)
    k = kv[:, 0:HEADS * HP] + _mm(blk[:, B_Q_LORA + B_KV_LORA:] * cs_k, place)
    return q, k, kv[:, HEADS * HP:]


def mla_pre_fwd(proj, gq, gkv, wq, wkv, place, cos_q, sin_q, cs_k, S):
    T = proj.shape[0]
    tm = _tile(S, 512)
    tpb = S // tm
    W = HEADS * HP

    def body(p_ref, gq_ref, gkv_ref, wq_ref, wkv_ref, pl_ref, cq_ref, sq_ref, ck_ref, q_ref, k_ref, v_ref):
        q, k, v = _mla_pre(p_ref[...], gq_ref[...], gkv_ref[...], wq_ref[...], wkv_ref[...], pl_ref[...],
                           cq_ref[...], sq_ref[...], ck_ref[...])
        q_ref[...] = q
        k_ref[...] = k
        v_ref[...] = v

    def full(a):
        return pl.BlockSpec(a.shape, lambda i: (0,) * a.ndim)

    tok = pl.BlockSpec((tm, W), lambda i: (i, 0))
    return pl.pallas_call(
        body, name="mla_pre_fwd", grid=(T // tm,),
        in_specs=[pl.BlockSpec((tm, W), lambda i: (i, C_B // W)), full(gq), full(gkv), full(wq), full(wkv), full(place),
                  pl.BlockSpec((tm, W), lambda i: (i % tpb, 0)), pl.BlockSpec((tm, W), lambda i: (i % tpb, 0)),
                  pl.BlockSpec((tm, HP), lambda i: (i % tpb, 0))],
        out_specs=[tok] * 3, out_shape=[_sds((T, W), F32)] * 3,
        compiler_params=_cparams(VMEM_BIG))(proj, gq, gkv, wq, wkv, place, cos_q, sin_q, cs_k)


def mla_pre_bwd(proj, gq, gkv, wq, wkv, place, cos_q, sin_q, cs_k, dq, dk, dv, S):
    T = proj.shape[0]
    tm = _tile(S, 512)
    tpb = S // tm
    W = HEADS * HP

    def body(p_ref, gq_ref, gkv_ref, wq_ref, wkv_ref, pl_ref, cq_ref, sq_ref, ck_ref, dq_ref, dk_ref, dv_ref,
             dp_ref, dgq_ref, dgkv_ref, dwq_ref, dwkv_ref):
        @pl.when(pl.program_id(0) == 0)
        def _():
            for r in (dgq_ref, dgkv_ref, dwq_ref, dwkv_ref):
                r[...] = jnp.zeros_like(r)

        f = functools.partial(_mla_pre, place=pl_ref[...], cos_q=cq_ref[...], sin_q=sq_ref[...], cs_k=ck_ref[...])
        _, vjp = jax.vjp(f, p_ref[...], gq_ref[...], gkv_ref[...], wq_ref[...], wkv_ref[...])
        dp, dgq, dgkv, dwq, dwkv = vjp((dq_ref[...], dk_ref[...], dv_ref[...]))
        dp_ref[...] = dp.astype(BF16)
        dgq_ref[...] += dgq
        dgkv_ref[...] += dgkv
        dwq_ref[...] += dwq
        dwkv_ref[...] += dwkv

    def full(a):
        return pl.BlockSpec(a.shape, lambda i: (0,) * a.ndim)

    tok = pl.BlockSpec((tm, W), lambda i: (i, 0))
    return pl.pallas_call(
        body, name="mla_pre_bwd", grid=(T // tm,),
        in_specs=[pl.BlockSpec((tm, W), lambda i: (i, C_B // W)), full(gq), full(gkv), full(wq), full(wkv), full(place),
                  pl.BlockSpec((tm, W), lambda i: (i % tpb, 0)), pl.BlockSpec((tm, W), lambda i: (i % tpb, 0)),
                  pl.BlockSpec((tm, HP), lambda i: (i % tpb, 0)), tok, tok, tok],
        out_specs=[tok, full(gq), full(gkv), full(wq), full(wkv)],
        out_shape=[_sds((T, W), BF16), _sds(gq.shape, F32), _sds(gkv.shape, F32), _sds(wq.shape, F32), _sds(wkv.shape, F32)],
        compiler_params=_cparams(VMEM_BIG))(proj, gq, gkv, wq, wkv, place, cos_q, sin_q, cs_k, dq, dk, dv)


def _gmlp_block(blk, lg, lb, ws, bs):
    u = jax.nn.gelu(blk[:, 0:GW])
    v = jax.nn.gelu(blk[:, GW:2 * GW])
    mu = jnp.mean(v, axis=-1, keepdims=True)
    var = jnp.mean(jnp.square(v - mu), axis=-1, keepdims=True)
    vn = (v - mu) * lax.rsqrt(var + LN_EPS) * lg + lb
    causal = _iota((D_CHUNK, D_CHUNK), 1) <= _iota((D_CHUNK, D_CHUNK), 0)
    group = _iota((1, GW), 1) // HD
    mixed = jnp.zeros((D_CHUNK, GW), F32)
    for g in range(HEADS):
        part = _mm(jnp.where(causal, ws[g], 0.0), vn) + bs[g]
        mixed = mixed + jnp.where(group == g, part, 0.0)
    return u * mixed


def gmlp_fwd(proj, lg, lb, ws, bs):
    T = proj.shape[0]

    def body(p_ref, lg_ref, lb_ref, ws_ref, bs_ref, o_ref):
        o_ref[...] = _gmlp_block(p_ref[...], lg_ref[...], lb_ref[...], ws_ref[...], bs_ref[...]).astype(BF16)

    def full(a):
        return pl.BlockSpec(a.shape, lambda i: (0,) * a.ndim)

    return pl.pallas_call(
        body, name="gmlp_fwd", grid=(T // D_CHUNK,),
        in_specs=[pl.BlockSpec((D_CHUNK, 2 * GW), lambda i: (i, C_D // (2 * GW))), full(lg), full(lb), full(ws), full(bs)],
        out_specs=pl.BlockSpec((D_CHUNK, GW), lambda i: (i, 0)), out_shape=_sds((T, GW), BF16),
        compiler_params=_cparams(VMEM_BIG))(proj, lg, lb, ws, bs)


def gmlp_bwd(proj, lg, lb, ws, bs, dcat):
    T = proj.shape[0]

    def body(p_ref, lg_ref, lb_ref, ws_ref, bs_ref, do_ref, dp_ref, dlg_ref, dlb_ref, dws_ref, dbs_ref):
        @pl.when(pl.program_id(0) == 0)
        def _():
            for r in (dlg_ref, dlb_ref, dws_ref, dbs_ref):
                r[...] = jnp.zeros_like(r)

        _, vjp = jax.vjp(_gmlp_block, p_ref[...], lg_ref[...], lb_ref[...], ws_ref[...], bs_ref[...])
        dp, dlg, dlb, dws, dbs = vjp(do_ref[...])
        dp_ref[...] = dp.astype(BF16)
        dlg_ref[...] += dlg
        dlb_ref[...] += dlb
        dws_ref[...] += dws
        dbs_ref[...] += dbs

    def full(a):
        return pl.BlockSpec(a.shape, lambda i: (0,) * a.ndim)

    return pl.pallas_call(
        body, name="gmlp_bwd", grid=(T // D_CHUNK,),
        in_specs=[pl.BlockSpec((D_CHUNK, 2 * GW), lambda i: (i, C_D // (2 * GW))), full(lg), full(lb), full(ws), full(bs),
                  pl.BlockSpec((D_CHUNK, GW), lambda i: (i, O_D // GW))],
        out_specs=[pl.BlockSpec((D_CHUNK, 2 * GW), lambda i: (i, 0)), full(lg), full(lb), full(ws), full(bs)],
        out_shape=[_sds((T, 2 * GW), BF16), _sds(lg.shape, F32), _sds(lb.shape, F32), _sds(ws.shape, F32), _sds(bs.shape, F32)],
        compiler_params=_cparams(VMEM_BIG))(proj, lg, lb, ws, bs, dcat)


def _lb_all(logits):
    m = jnp.max(logits, axis=0, keepdims=True)
    e = jnp.exp(logits - m)
    sm = e / jnp.sum(e, axis=0, keepdims=True)
    return jnp.concatenate([sm[0:1] - sm[0:1], (sm[0:1] + sm[1:2]) - sm[0:1]], axis=0)


def lb_fwd(logits):
    def body(l_ref, o_ref):
        o_ref[...] = _lb_all(l_ref[...])

    return pl.pallas_call(body, name="lb_fwd", out_shape=_sds(logits.shape, F32))(logits)


def lb_bwd(logits, dlb):
    def body(l_ref, d_ref, o_ref):
        _, vjp = jax.vjp(_lb_all, l_ref[...])
        o_ref[...] = vjp(d_ref[...])[0]

    return pl.pallas_call(body, name="lb_bwd", out_shape=_sds(logits.shape, F32))(logits, dlb)


def ada_grad(c_all, dmod_cols):
    N = dmod_cols.shape[1]
    tn = _tile(N, 1152)

    def body(c_ref, d_ref, o_ref):
        h = jax.nn.silu(c_ref[...]).astype(BF16)
        o_ref[...] = lax.dot_general(h, d_ref[...].astype(BF16), (((0,), (0,)), ((), ())), preferred_element_type=F32)

    nb = c_all.shape[0]
    return pl.pallas_call(
        body, name="ada_grad", grid=(N // tn,),
        in_specs=[pl.BlockSpec((nb, D), lambda j: (0, 0)), pl.BlockSpec((nb, tn), lambda j: (0, j))],
        out_specs=pl.BlockSpec((D, tn), lambda j: (0, j)), out_shape=_sds((D, N), F32),
        compiler_params=_cparams(VMEM_BIG))(c_all, dmod_cols)


def sum_slots(a, n, name):
    _, R, C = a.shape
    tr = _row_tile(R, C, n)

    def body(a_ref, o_ref):
        acc = a_ref[0]
        for k in range(1, n):
            acc = acc + a_ref[k]
        o_ref[...] = acc

    return pl.pallas_call(
        body, name=name, grid=(R // tr,),
        in_specs=[pl.BlockSpec((n, tr, C), lambda i: (0, i, 0))],
        out_specs=pl.BlockSpec((tr, C), lambda i: (i, 0)), out_shape=_sds((R, C), F32),
        compiler_params=_cparams(VMEM_BIG))(a)


def add2(a, b, name):
    shp = a.shape
    C = shp[-1]
    a2, b2 = a.reshape(-1, C), b.reshape(-1, C)
    R = a2.shape[0]
    tr = _row_tile(R, C)

    def body(a_ref, b_ref, o_ref):
        o_ref[...] = a_ref[...] + b_ref[...]

    spec = pl.BlockSpec((tr, C), lambda i: (i, 0))
    return pl.pallas_call(body, name=name, grid=(R // tr,), in_specs=[spec, spec], out_specs=spec,
                          out_shape=_sds((R, C), F32), compiler_params=_cparams(VMEM_BIG))(a2, b2).reshape(shp)


def _row_tile(R, C=D, n=1, mult=8, elems=1 << 18):
    limit = max(mult, elems // (C * n))
    for t in range(limit - limit % mult, mult - 1, -mult):
        if R % t == 0:
            return t
    return R


def adamw(w, g, m, v, name, echo=False):
    R, C = w.shape
    tr = _row_tile(R, C, elems=1 << 19)
    c1 = 1.0 - ADAM_B1 ** ADAM_STEP
    c2 = 1.0 - ADAM_B2 ** ADAM_STEP
    n_out = 4 if echo else 3

    def body(w_ref, g_ref, m_ref, v_ref, d_ref, nm_ref, nv_ref, *g_out):
        g_ = g_ref[...]
        nm = ADAM_B1 * m_ref[...] + (1.0 - ADAM_B1) * g_
        nv = ADAM_B2 * v_ref[...] + (1.0 - ADAM_B2) * jnp.square(g_)
        d_ref[...] = -ADAM_LR * ((nm / c1) / (jnp.sqrt(nv / c2) + ADAM_EPS) + ADAM_WD * w_ref[...])
        nm_ref[...] = nm
        nv_ref[...] = nv
        if echo:
            g_out[0][...] = g_

    spec = pl.BlockSpec((tr, C), lambda i: (i, 0))
    return pl.pallas_call(body, name=name, grid=(R // tr,), in_specs=[spec] * 4, out_specs=[spec] * n_out,
                          out_shape=[_sds((R, C), F32)] * n_out, compiler_params=_cparams(VMEM_BIG))(w, g, m, v)


def _rot_cols(w):
    return jnp.concatenate([-w[:, 16:32], w[:, 0:16]], axis=1)


def _fold_rot(d):
    return jnp.concatenate([d[:, 16:32], -d[:, 0:16]], axis=1)


def _pad_heads(w, off, axis):
    parts = []
    for h in range(HEADS):
        piece = lax.slice_in_dim(w, off + HD * h, off + HD * (h + 1), axis=axis)
        parts += [piece, jnp.zeros_like(piece)]
    return parts


def _unpad_heads(d, off, axis):
    return [lax.slice_in_dim(d, off + HP * h, off + HP * h + HD, axis=axis) for h in range(HEADS)]


def mix_in_ext(w):
    z = lambda n: jnp.zeros((w.shape[0], n), w.dtype)
    kr = w[:, 1408:1440]
    cols = [w[:, 0:1408], kr, _rot_cols(kr), z(64)]
    cols += _pad_heads(w, 1440, 1) + _pad_heads(w, 1696, 1) + _pad_heads(w, 1952, 1)
    cols += [w[:, 2212:2724], w[:, 2208:2212], z(NP - C_CF - HEADS)]
    return jnp.concatenate(cols, axis=1)


def mix_in_unext(d):
    kr = d[:, 1408:1440] + _fold_rot(d[:, 1440:1472])
    cols = [d[:, 0:1408], kr] + _unpad_heads(d, C_CQ, 1) + _unpad_heads(d, C_CK, 1) + _unpad_heads(d, C_CV, 1)
    cols += [d[:, C_CF:C_CF + HEADS], d[:, C_D:C_D + 2 * GW]]
    return jnp.concatenate(cols, axis=1)


def mix_out_ext(w):
    return jnp.concatenate([w[0:GW]] + _pad_heads(w, GW, 0) + _pad_heads(w, 2 * GW, 0) + [w[3 * GW:4 * GW]], axis=0)


def mix_out_unext(d):
    return jnp.concatenate([d[0:GW]] + _unpad_heads(d, O_B, 0) + _unpad_heads(d, O_C, 0) + [d[O_D:O_D + GW]], axis=0)


def uq_ext(w):
    z = lambda n: jnp.zeros((w.shape[0], n), w.dtype)
    a, b = [], []
    for h in range(HEADS):
        o = (B_NOPE + B_ROPE) * h
        a += [w[:, o:o + B_NOPE + B_ROPE], z(32)]
        b += [z(B_NOPE), _rot_cols(w[:, o + B_NOPE:o + B_NOPE + B_ROPE]), z(32)]
    return jnp.concatenate(a + b, axis=1)


def uq_unext(d):
    cols = []
    for h in range(HEADS):
        o = HP * h
        cols += [d[:, o:o + B_NOPE], d[:, o + B_NOPE:o + B_NOPE + B_ROPE]
                 + _fold_rot(d[:, HEADS * HP + o + B_NOPE:HEADS * HP + o + B_NOPE + B_ROPE])]
    return jnp.concatenate(cols, axis=1)


def ukv_ext(w):
    z = jnp.zeros((w.shape[0], HD), w.dtype)
    k, v = [], []
    for h in range(HEADS):
        k += [w[:, 2 * HD * h:2 * HD * h + HD], z]
        v += [w[:, 2 * HD * h + HD:2 * HD * (h + 1)], z]
    return jnp.concatenate(k + v, axis=1)


def ukv_unext(d):
    cols = []
    for h in range(HEADS):
        cols += [d[:, HP * h:HP * h + HD], d[:, HEADS * HP + HP * h:HEADS * HP + HP * h + HD]]
    return jnp.concatenate(cols, axis=1)


def rope_tables(S):
    half = B_ROPE // 2
    inv_freq = ROPE_THETA ** (-jnp.arange(half, dtype=F32) / half)
    ang = jnp.arange(S).astype(F32)[:, None] * inv_freq[None, :]
    cos = jnp.tile(jnp.cos(ang), (1, 2))
    sin = jnp.tile(jnp.sin(ang), (1, 2))
    one, zero = jnp.ones((S, B_NOPE), F32), jnp.zeros((S, B_NOPE), F32)
    z32 = jnp.zeros((S, 32), F32)
    cos_q = jnp.tile(jnp.concatenate([one, cos, z32], axis=1), (1, HEADS))
    sin_q = jnp.tile(jnp.concatenate([zero, sin, z32], axis=1), (1, HEADS))
    cs_k = jnp.concatenate([cos, sin, zero], axis=1)
    place = np.zeros((HP, HEADS * HP), np.float32)
    for h in range(HEADS):
        for j in range(B_ROPE):
            place[j, h * HP + B_NOPE + j] = 1.0
            place[B_ROPE + j, h * HP + B_NOPE + j] = 1.0
    return cos_q, sin_q, cs_k, jnp.asarray(place, BF16)


def layer_fwd(x, mod, get, tabs, S):
    cos_q, sin_q, cs_k, place = tabs
    p = dict(get("ffn1", x))
    l = p["wl"]
    zg1, zu1, act1 = ffn_in_fwd(x, mod, p["ffn1_in"], l, 0, 1, S)
    y1, x1 = out_ln_fwd(act1, p["ffn1_out"], x, mod, p["ln_g"][0:1], p["ln_b"][0:1], 2, 0.5, S, l)
    p.update(get("mix", x1))
    proj = mix_in_fwd(x1, mod, p["mix_in"], 3, 4, S)
    o_a, states = hgrn_fwd(proj, p["lb"], p["ng"], S)
    q_b, k_b, v_b = mla_pre_fwd(proj, p["gq"], p["gkv"], p["wq"], p["wkv"], place, cos_q, sin_q, cs_k, S)
    o_b = attn_fwd(q_b, 0, k_b, 0, v_b, 0, None, (B_NOPE + B_ROPE) ** -0.5, S)
    cum = fox_gate_fwd(proj, p["bcol"], S)
    o_c = attn_fwd(proj, C_CQ // HP, proj, C_CK // HP, proj, C_CV // HP, cum, HD ** -0.5, S)
    o_d = gmlp_fwd(proj, p["g_lg"], p["g_lb"], p["ws"], p["bs"])
    cat = jnp.concatenate([o_a, o_b, o_c, o_d], axis=1)
    y2, x2 = out_ln_fwd(cat, p["mix_out"], x1, mod, p["ln_g"][1:2], p["ln_b"][1:2], 5, 1.0, S)
    p.update(get("ffn2", x2))
    zg3, zu3, act3 = ffn_in_fwd(x2, mod, p["ffn2_in"], l, 6, 7, S)
    y3, x3 = out_ln_fwd(act3, p["ffn2_out"], x2, mod, p["ln_g"][2:3], p["ln_b"][2:3], 8, 0.5, S, l)
    saved = dict(x=x, zg1=zg1, zu1=zu1, act1=act1, y1=y1, x1=x1, proj=proj, states=states, q_b=q_b, k_b=k_b, v_b=v_b,
                 cum=cum, cat=cat, y2=y2, x2=x2, zg3=zg3, zu3=zu3, act3=act3, y3=y3, p=p)
    return x3, saved


def _ffn_bwd(dxn, x_in, y, zg, zu, act, mod, w_in, w_out, l, lg, lb, idx, S, emit):
    sh, sc, gate = idx
    dres, dy, dgate, dlg, dlb = ln_res_bwd(dxn, x_in, y, mod, lg, lb, gate, 0.5, S)
    dzg, dzu = swiglu_bwd(dy, w_out, l, zg, zu, S)
    dw_out = tn_mm(act, dy, FF // 2).reshape(N_CHIPS, FF // N_CHIPS, D)
    dw_in = tn_mm_mod_shards(x_in, mod, dzg, dzu, sh, sc, S)
    mod = mod + emit(dw_in, dw_out)
    dx, dsh, dsc = nt_mod_bwd([dzg, dzu], w_in, [0, 2], x_in, mod, dres, sc, S, FF // 2, l)
    return dx, dw_in, dw_out, dlg, dlb, {sh: dsh, sc: dsc, gate: dgate}, mod


def layer_bwd(dx3, mod, sv, tabs, S, emit):
    cos_q, sin_q, cs_k, place = tabs
    p = sv["p"]
    l = p["wl"]
    g = {}
    dm = {}

    def emit_ffn(part):
        def f(dw_in, dw_out):
            g[part + "_in"], g[part + "_out"] = dw_in, dw_out
            return emit(part, g)
        return f

    dx2, _, _, dlg2, dlb2, d, mod = _ffn_bwd(
        dx3, sv["x2"], sv["y3"], sv["zg3"], sv["zu3"], sv["act3"], mod, p["ffn2_in"], p["ffn2_out"], l,
        p["ln_g"][2:3], p["ln_b"][2:3], (6, 7, 8), S, emit_ffn("ffn2"))
    dm.update(d)
    dres, dy2, dm[5], dlg1, dlb1 = ln_res_bwd(dx2, sv["x1"], sv["y2"], mod, p["ln_g"][1:2], p["ln_b"][1:2], 5, 1.0, S)
    dcat = nt_plain(dy2, p["mix_out"])
    g["mix_out"] = tn_mm(sv["cat"], dy2, NCAT // 2)
    proj = sv["proj"]
    d_a, g["lb"], g["ng"] = hgrn_bwd(proj, sv["states"], dcat, p["lb"], p["ng"], S)
    dq_c, dk_c, dv_c, dcum = attn_bwd(proj, C_CQ // HP, proj, C_CK // HP, proj, C_CV // HP, sv["cum"], dcat,
                                      O_C // HP, HD ** -0.5, S, BF16)
    B = proj.shape[0] // S
    dcum = jnp.concatenate([dcum.reshape(B, HEADS, S), jnp.zeros((B, 8 - HEADS, S), F32)], axis=1)
    d_cf, g["bcol"] = fox_gate_bwd(proj, p["bcol"], dcum, S)
    dq_b, dk_b, dv_b = attn_bwd(sv["q_b"], 0, sv["k_b"], 0, sv["v_b"], 0, None, dcat, O_B // HP,
                                (B_NOPE + B_ROPE) ** -0.5, S, F32)
    d_b, g["gq"], g["gkv"], g["wq"], g["wkv"] = mla_pre_bwd(
        proj, p["gq"], p["gkv"], p["wq"], p["wkv"], place, cos_q, sin_q, cs_k, dq_b, dk_b, dv_b, S)
    d_d, g["g_lg"], g["g_lb"], g["ws"], g["bs"] = gmlp_bwd(proj, p["g_lg"], p["g_lb"], p["ws"], p["bs"], dcat)
    dproj = jnp.concatenate([d_a, d_b, dq_c, dk_c, dv_c, d_d, d_cf, jnp.zeros_like(d_cf)], axis=1)
    g["mix_in"] = tn_mm_mod(sv["x1"], mod, dproj, 3, 4, S, NP_TILE)
    mod = mod + emit("mix", g)
    dx1, dm[3], dm[4] = nt_mod_bwd([dproj], p["mix_in"], [0], sv["x1"], mod, dres, 4, S, NP_TILE)
    last = []

    def emit_last(dw_in, dw_out):
        last.append(emit_ffn("ffn1")(dw_in, dw_out))
        return last[0]

    dx0, _, _, dlg0, dlb0, d, mod = _ffn_bwd(
        dx1, sv["x"], sv["y1"], sv["zg1"], sv["zu1"], sv["act1"], mod, p["ffn1_in"], p["ffn1_out"], l,
        p["ln_g"][0:1], p["ln_b"][0:1], (0, 1, 2), S, emit_last)
    dm.update(d)
    g["ln_g"] = jnp.concatenate([dlg0, dlg1, dlg2], axis=0)
    g["ln_b"] = jnp.concatenate([dlb0, dlb1, dlb2], axis=0)
    dmod = jnp.concatenate([dm[i] for i in range(N_MOD)], axis=1)
    return dx0, dmod, g, last[0]


def local_step(x, c8, tgt, get, lb_logits, S, emit=None):
    B = x.shape[0] // S
    tabs = rope_tables(S)
    lb_all = lb_fwd(lb_logits)
    mods, saved = [], []
    h = x
    for l in range(DEPTH):
        pa = get(l, "ada", h)
        mod = mod_fwd(c8, pa["ada_w"], pa["wl"], pa["ada_b"])[0:B].reshape(B, N_MOD, D)

        def get_l(part, after, l=l):
            p = dict(get(l, part, after))
            if part == "mix":
                p["lb"] = lb_all[l:l + 1]
            return p

        h, sv = layer_fwd(h, mod, get_l, tabs, S)
        mods.append(mod)
        saved.append(sv)
    loss_tile, dh = loss_head(h, tgt)
    grads, dmods, dlb = [None] * DEPTH, [None] * DEPTH, [None] * DEPTH
    tie = jnp.zeros((), F32)
    for l in reversed(range(DEPTH)):
        emit_l = (lambda part, g: jnp.zeros((), F32)) if emit is None else functools.partial(emit, l)
        dh, dmods[l], grads[l], tie = layer_bwd(dh, mods[l] + tie, saved[l], tabs, S, emit_l)
        dlb[l] = grads[l].pop("lb")
    d_logits = lb_bwd(lb_logits, jnp.concatenate(dlb, axis=0))
    return loss_tile, dh, dmods, grads, d_logits


ANY = pl.BlockSpec(memory_space=pl.ANY)


def _place():
    x, y, c = lax.axis_index("x"), lax.axis_index("y"), lax.axis_index("c")
    chips = [(1 - x, y), (x, 1 - y), (1 - x, 1 - y)]
    return x, y, c, chips


def _rcopy(src, dst, sems, k, to):
    send_sems, recv_sems = sems
    return pltpu.make_async_remote_copy(src_ref=src, dst_ref=dst, send_sem=send_sems.at[k], recv_sem=recv_sems.at[k],
                                        device_id=to, device_id_type=MESH)


def _dma_sems(n_remote, n_local):
    return [pltpu.SemaphoreType.DMA((n_remote,)), pltpu.SemaphoreType.DMA((n_remote,)), pltpu.SemaphoreType.DMA((n_local,))]


def own_slot(src, chip):
    L = src.shape[0]
    return lax.dynamic_update_slice(lax.empty((L, N_CHIPS) + src.shape[1:], src.dtype), src[:, None], (0, chip, 0, 0))


def ag_shards(arrs, lands):
    n = len(arrs)
    rh = [a.shape[1] // 2 for a in arrs]

    def body(*refs):
        srcs, outs, token = refs[:n], refs[2 * n:3 * n], refs[3 * n]
        send_sems, recv_sems = refs[3 * n + 1:]
        x, y, c, chips = _place()
        sems = (send_sems, recv_sems)
        me = 2 * x + y
        sibling = (x, y, 1 - c)
        token[...] = jnp.zeros_like(token)

        def part(i, k, hc):
            return outs[i].at[:, k, pl.ds(hc * rh[i], rh[i]), :]

        started = []
        for j, (px, py) in enumerate(chips):
            for i in range(n):
                cp = _rcopy(srcs[i].at[:, pl.ds(c * rh[i], rh[i]), :], part(i, me, c), sems, 6 * i + j, (px, py, c))
                cp.start()
                started.append(cp)
        for j, (px, py) in enumerate(chips):
            k = 2 * px + py
            for i in range(n):
                _rcopy(part(i, k, c), part(i, k, c), sems, 6 * i + j, (px, py, c)).wait_recv()
                cp = _rcopy(part(i, k, c), part(i, k, c), sems, 6 * i + 3 + j, sibling)
                cp.start()
                started.append(cp)
        for j, (px, py) in enumerate(chips):
            k = 2 * px + py
            for i in range(n):
                _rcopy(part(i, k, 1 - c), part(i, k, 1 - c), sems, 6 * i + 3 + j, sibling).wait_recv()
        for cp in started:
            cp.wait_send()

    outs = pl.pallas_call(
        body, name="ag_shards", out_shape=[_sds(a.shape, a.dtype) for a in lands] + [_sds((8, 128), F32)],
        in_specs=[ANY] * (2 * n), out_specs=[ANY] * n + [pl.BlockSpec(memory_space=pltpu.VMEM)],
        input_output_aliases={n + i: i for i in range(n)}, scratch_shapes=_dma_sems(6 * n, 1)[:2])(*arrs, *lands)
    return list(outs[:n]), outs[n]


HBM_SPEC = pl.BlockSpec(memory_space=pltpu.HBM)
SEM_SPEC = pl.BlockSpec(memory_space=pltpu.SEMAPHORE)
DATAFLOW = pltpu.SideEffectType.DATAFLOW_SIDE_EFFECTING


def _after(x, dep):
    return lax.optimization_barrier((x, dep))[0]


def _split_start(srcs, lands, copies, n_copies, dep, name):
    n, m = len(srcs), len(lands)

    def body(*refs):
        ins = refs[:n + m]
        send_sems, recv_sems = refs[n + m + 1], refs[n + m + 2]
        token = refs[-1]
        for k, (src, dst, to) in enumerate(copies(ins[:n], ins[n:], _place())):
            pltpu.make_async_remote_copy(src_ref=src, dst_ref=dst, send_sem=send_sems.at[k], recv_sem=recv_sems.at[k],
                                         device_id=to, device_id_type=MESH).start()
        token[...] = jnp.zeros_like(token)

    arrs = list(srcs) + list(lands)
    outs = pl.pallas_call(
        body, name=name,
        out_shape=(pltpu.SemaphoreType.DMA((n_copies,)), pltpu.SemaphoreType.DMA((n_copies,)),
                   *[pltpu.HBM(a.shape, a.dtype) for a in arrs], _sds((8, 128), F32)),
        in_specs=[HBM_SPEC] * (n + m) + [ANY],
        out_specs=(SEM_SPEC, SEM_SPEC, *[HBM_SPEC] * (n + m), pl.BlockSpec(memory_space=pltpu.VMEM)),
        input_output_aliases={i: 2 + i for i in range(n + m)},
        compiler_params=pltpu.CompilerParams(has_side_effects=DATAFLOW),
    )(*[pltpu.with_memory_space_constraint(a, pltpu.HBM) for a in arrs], dep)
    return outs[0], outs[1], list(outs[2:2 + n]), list(outs[2 + n:2 + n + m]), outs[-1]


def _split_wait(handle, arrivals, after, name):
    send_sems, recv_sems, srcs, lands, _ = handle
    n, m = len(srcs), len(lands)

    def body(*refs):
        ins = refs[:n + m]
        send_sems, recv_sems = refs[n + m], refs[n + m + 1]
        x, y, c, chips = place = _place()
        for k, (src, dst) in enumerate(arrivals(ins[:n], ins[n:], place)):
            cp = pltpu.make_async_remote_copy(src_ref=src, dst_ref=dst, send_sem=send_sems.at[k], recv_sem=recv_sems.at[k],
                                              device_id=(x, y, 1 - c), device_id_type=MESH)
            cp.wait_send()
            cp.wait_recv()

    arrs = list(srcs) + list(lands)
    outs = pl.pallas_call(
        body, name=name, out_shape=[pltpu.HBM(a.shape, a.dtype) for a in arrs],
        in_specs=[HBM_SPEC] * (n + m) + [SEM_SPEC, SEM_SPEC, ANY], out_specs=[HBM_SPEC] * (n + m),
        input_output_aliases={i: i for i in range(n + m)},
        compiler_params=pltpu.CompilerParams(has_side_effects=DATAFLOW),
    )(*arrs, send_sems, recv_sems, after)
    return list(outs[:n]), list(outs[n:])


def _ag_part(ref, k, hc):
    rh = ref.shape[2] // 2
    return ref.at[:, k, pl.ds(hc * rh, rh), :]


def ag_start(srcs, lands, dep, name):
    def copies(s, d, place):
        x, y, c, chips = place
        out = []
        for j, (px, py) in enumerate(chips):
            for i in range(len(s)):
                rh = s[i].shape[1] // 2
                out.append((s[i].at[:, pl.ds(c * rh, rh), :], _ag_part(d[i], 2 * x + y, c), (px, py, c)))
        return out

    return _split_start(srcs, lands, copies, 3 * len(srcs), dep, name)


def ag_wait(handle, after, name):
    def arrivals(s, d, place):
        x, y, c, chips = place
        out = []
        for j, (px, py) in enumerate(chips):
            for i in range(len(s)):
                rh = s[i].shape[1] // 2
                out.append((s[i].at[:, pl.ds(c * rh, rh), :], _ag_part(d[i], 2 * px + py, c)))
        return out

    return _split_wait(handle, arrivals, after, name)


def ag_forward(lands, name):
    n = len(lands)

    def body(*refs):
        bufs, token = refs[n:2 * n], refs[2 * n]
        send_sems, recv_sems = refs[2 * n + 1:]
        x, y, c, chips = _place()
        sems = (send_sems, recv_sems)
        token[...] = jnp.zeros_like(token)
        cps = []
        for j, (px, py) in enumerate(chips):
            for i in range(n):
                part = _ag_part(bufs[i], 2 * px + py, c)
                cps.append(_rcopy(part, part, sems, 3 * i + j, (x, y, 1 - c)))
        for cp in cps:
            cp.start()
        for j, (px, py) in enumerate(chips):
            for i in range(n):
                part = _ag_part(bufs[i], 2 * px + py, 1 - c)
                _rcopy(part, part, sems, 3 * i + j, (x, y, 1 - c)).wait_recv()
        for cp in cps:
            cp.wait_send()

    outs = pl.pallas_call(
        body, name=name, out_shape=[_sds(a.shape, a.dtype) for a in lands] + [_sds((8, 128), F32)],
        in_specs=[ANY] * n, out_specs=[ANY] * n + [pl.BlockSpec(memory_space=pltpu.VMEM)],
        input_output_aliases={i: i for i in range(n)}, scratch_shapes=_dma_sems(3 * n, 1)[:2])(*lands)
    return list(outs[:n]), outs[n]


def rs_start(hs, lands, dep, name):
    def copies(s, d, place):
        x, y, c, chips = place
        return [(s[i].at[2 * px + py], d[i].at[2 * x + y], (px, py, c)) for j, (px, py) in enumerate(chips) for i in range(len(s))]

    return _split_start(hs, lands, copies, 3 * len(hs), dep, name)


def _kept_out(ref, c):
    rh = ref.shape[1] // 2
    return ref.at[:, pl.ds((1 - c) * rh, rh), :]


def swap_start(arrs, lands, dep, name):
    def copies(s, d, place):
        x, y, c, _ = place
        return [(_kept_out(s[i], c), d[i], (x, y, 1 - c)) for i in range(len(s))]

    return _split_start(arrs, lands, copies, len(arrs), dep, name)


def swap_wait(handle, after, name):
    def arrivals(s, d, place):
        x, y, c, _ = place
        return [(_kept_out(s[i], c), d[i]) for i in range(len(s))]

    return _split_wait(handle, arrivals, after, name)


def rs_wait(handle, after, name):
    def arrivals(s, d, place):
        x, y, c, chips = place
        return [(s[i].at[2 * px + py], d[i].at[2 * px + py]) for j, (px, py) in enumerate(chips) for i in range(len(s))]

    return _split_wait(handle, arrivals, after, name)


def sibling_swap(arrs, name):
    n = len(arrs)
    rh = [a.shape[1] // 2 for a in arrs]

    def body(*refs):
        srcs, outs = refs[:n], refs[n:2 * n]
        send_sems, recv_sems = refs[2 * n:]
        x, y, c, _ = _place()
        cps = [_rcopy(srcs[i].at[:, pl.ds((1 - c) * rh[i], rh[i]), :], outs[i], (send_sems, recv_sems), i, (x, y, 1 - c))
               for i in range(n)]
        for cp in cps:
            cp.start()
        for cp in cps:
            cp.wait()

    return pl.pallas_call(
        body, name=name, out_shape=[_sds((N_CHIPS, r, a.shape[2]), a.dtype) for a, r in zip(arrs, rh)],
        in_specs=[ANY] * n, out_specs=[ANY] * n, scratch_shapes=_dma_sems(n, 1)[:2])(*arrs)


def chip_exchange(hs):
    n = len(hs)

    def body(*refs):
        srcs, outs = refs[:n], refs[n:2 * n]
        send_sems, recv_sems, loc_sems = refs[2 * n:]
        x, y, c, chips = _place()
        sems = (send_sems, recv_sems)
        me = 2 * x + y
        mine = [pltpu.make_async_copy(srcs[i].at[me], outs[i].at[me], loc_sems.at[i]) for i in range(n)]
        for cp in mine:
            cp.start()
        sends = []
        for j, (px, py) in enumerate(chips):
            for i in range(n):
                cp = _rcopy(srcs[i].at[2 * px + py], outs[i].at[me], sems, 3 * i + j, (px, py, c))
                cp.start()
                sends.append(cp)
        for j, (px, py) in enumerate(chips):
            for i in range(n):
                _rcopy(srcs[i].at[2 * px + py], outs[i].at[2 * px + py], sems, 3 * i + j, (px, py, c)).wait_recv()
        for cp in sends:
            cp.wait_send()
        for cp in mine:
            cp.wait()

    return pl.pallas_call(
        body, name="chip_exchange", out_shape=[_sds(h.shape, h.dtype) for h in hs],
        in_specs=[ANY] * n, out_specs=[ANY] * n, scratch_shapes=_dma_sems(3 * n, n))(*hs)


def sum_into(land, base, l, core, name):
    _, rh, C = land.shape
    tr = _row_tile(rh, C, N_CHIPS, mult=16)
    nr = rh // tr

    def body(core_ref, land_ref, base_ref, o_ref):
        acc = land_ref[0].astype(F32)
        for k in range(1, N_CHIPS):
            acc = acc + land_ref[k].astype(F32)
        o_ref[...] = acc

    grid_spec = pltpu.PrefetchScalarGridSpec(
        num_scalar_prefetch=1, grid=(nr,),
        in_specs=[pl.BlockSpec((N_CHIPS, tr, C), lambda r, core_ref: (0, r, 0)), ANY],
        out_specs=pl.BlockSpec((None, tr, C), lambda r, core_ref: (l, core_ref[0] * nr + r, 0)))
    return pl.pallas_call(body, name=name, grid_spec=grid_spec, out_shape=_sds(base.shape, base.dtype),
                          input_output_aliases={2: 0}, compiler_params=_cparams(VMEM_BIG))(
        core.reshape(1).astype(jnp.int32), land, base)


def sibling_join(bases, name):
    n = len(bases)

    def body(*refs):
        bufs = refs[n:2 * n]
        send_sems, recv_sems = refs[2 * n:]
        x, y, c, _ = _place()
        sems = (send_sems, recv_sems)

        def half(i, hc):
            rh = bufs[i].shape[1] // 2
            return bufs[i].at[:, pl.ds(hc * rh, rh), :]

        sends = [_rcopy(half(i, c), half(i, c), sems, i, (x, y, 1 - c)) for i in range(n)]
        for cp in sends:
            cp.start()
        for i in range(n):
            _rcopy(half(i, 1 - c), half(i, 1 - c), sems, i, (x, y, 1 - c)).wait_recv()
        for cp in sends:
            cp.wait_send()

    return pl.pallas_call(
        body, name=name, out_shape=[_sds(b.shape, b.dtype) for b in bases], in_specs=[ANY] * n, out_specs=[ANY] * n,
        input_output_aliases={i: i for i in range(n)}, scratch_shapes=_dma_sems(n, 1)[:2])(*bases)


def ag_all(blk):
    M, C = blk.shape

    def body(x_ref, out_ref, send_sems, recv_sems, loc_sem):
        x, y, c, chips = _place()
        sems = (send_sems, recv_sems)
        me, sibling = (x, y, c), (x, y, 1 - c)

        def slot(px, py, pc):
            return out_ref.at[4 * px + 2 * py + pc]

        mine = pltpu.make_async_copy(x_ref, slot(*me), loc_sem)
        mine.start()
        first = [_rcopy(x_ref, slot(*me), sems, 0, sibling)]
        first += [_rcopy(x_ref, slot(*me), sems, 1 + j, (*chip, c)) for j, chip in enumerate(chips)]
        for cp in first:
            cp.start()
        passed = [_rcopy(slot(*chip, c), slot(*chip, c), sems, 4 + j, sibling) for j, chip in enumerate(chips)]
        for j, chip in enumerate(chips):
            _rcopy(slot(*chip, c), slot(*chip, c), sems, 1 + j, me).wait_recv()
            passed[j].start()
        _rcopy(slot(*sibling), slot(*sibling), sems, 0, me).wait_recv()
        for j, chip in enumerate(chips):
            _rcopy(slot(*chip, 1 - c), slot(*chip, 1 - c), sems, 4 + j, me).wait_recv()
        for cp in first + passed:
            cp.wait_send()
        mine.wait()

    return pl.pallas_call(
        body, name="ag_all", out_shape=_sds((8, M, C), blk.dtype),
        in_specs=[pl.BlockSpec(memory_space=pltpu.VMEM)], out_specs=pl.BlockSpec(memory_space=pltpu.VMEM),
        scratch_shapes=[pltpu.SemaphoreType.DMA((7,)), pltpu.SemaphoreType.DMA((7,)), pltpu.SemaphoreType.DMA(())],
        compiler_params=_cparams(VMEM_BIG))(blk)


WEIGHTS = ["ada_w", "ada_b", "ln_g", "ln_b", "ffn1_w_in", "ffn1_w_out", "ffn2_w_in", "ffn2_w_out", "mix_w_in", "mix_w_out",
           "hgrn_lb_logits", "hgrn_norm_g", "mla_q_norm_g", "mla_kv_norm_g", "mla_w_uq", "mla_w_ukv", "fox_b_f",
           "gmlp_ln_g", "gmlp_ln_b", "gmlp_w_s", "gmlp_b_s"]
SHARDED = {"ffn1_w_in": 1, "ffn1_w_out": 0, "ffn2_w_in": 1, "ffn2_w_out": 0, "mix_w_in": 1, "mix_w_out": 0,
           "mla_w_uq": 1, "mla_w_ukv": 1}
SMALL = ["hgrn_lb_logits", "hgrn_norm_g", "mla_q_norm_g", "mla_kv_norm_g", "fox_b_f", "gmlp_ln_g", "gmlp_ln_b",
         "gmlp_w_s", "gmlp_b_s", "ln_g", "ln_b"]
GATHERED = ["ada_w", "ffn1_w_in", "ffn1_w_out", "ffn2_w_in", "ffn2_w_out", "mix_w_in", "mix_w_out", "mla_w_uq", "mla_w_ukv"]
REDUCED = GATHERED[1:]


def _col_shards(a):
    cols = a.shape[1] // N_CHIPS
    return jnp.stack([a[:, k * cols:(k + 1) * cols] for k in range(N_CHIPS)])


def add_kept_half(a, got, core, name):
    _, R, C = a.shape
    rh = R // 2
    tr = _row_tile(rh, C, mult=16)
    nr = rh // tr

    def body(core_ref, a_ref, b_ref, o_ref):
        o_ref[...] = (a_ref[...].astype(F32) + b_ref[...].astype(F32)).astype(o_ref.dtype)

    half = pl.BlockSpec((None, tr, C), lambda k, r, core_ref: (k, r, 0))
    grid_spec = pltpu.PrefetchScalarGridSpec(
        num_scalar_prefetch=1, grid=(N_CHIPS, nr),
        in_specs=[pl.BlockSpec((None, tr, C), lambda k, r, core_ref: (k, core_ref[0] * nr + r, 0)), half],
        out_specs=half)
    return pl.pallas_call(body, name=name, grid_spec=grid_spec, out_shape=_sds((N_CHIPS, rh, C), BF16),
                          compiler_params=_cparams(VMEM_BIG))(core.reshape(1).astype(jnp.int32), a, got)


def _rows(parts, n_rows, dtype):
    flat = jnp.concatenate([p.reshape(-1) for p in parts])
    pad = n_rows * D - flat.shape[0]
    return jnp.concatenate([flat, jnp.zeros((pad,), dtype)]).reshape(n_rows, D)


def _take(flat, shapes):
    out, o = [], 0
    for shp in shapes:
        n = int(np.prod(shp))
        out.append(flat[o:o + n].reshape(shp))
        o += n
    return out


def _round_up(n, m):
    return -(-n // m) * m


def pack_shard(w):
    parts = [w[n][l] for l in range(DEPTH) for n in SHARDED] + [w[n][l] for l in range(DEPTH) for n in ("ln_g", "ln_b")]
    n = sum(int(np.prod(p.shape)) for p in parts)
    return _rows(parts, _round_up(-(-n // D), 16), F32)


def unpack_shard(pk, like):
    shapes = [like[n].shape[1:] for l in range(DEPTH) for n in SHARDED] + [like[n].shape[1:] for l in range(DEPTH) for n in ("ln_g", "ln_b")]
    pieces = _take(pk.reshape(-1), shapes)
    names = [n for l in range(DEPTH) for n in SHARDED] + [n for l in range(DEPTH) for n in ("ln_g", "ln_b")]
    out = {}
    for n in list(SHARDED) + ["ln_g", "ln_b"]:
        out[n] = jnp.stack([p for p, m in zip(pieces, names) if m == n])
    return out


def pack_small(w):
    parts = [w[n][l] for l in range(DEPTH) for n in SMALL]
    n = sum(int(np.prod(p.shape)) for p in parts)
    return _rows(parts, _round_up(-(-n // D), 8), F32)


def unpack_small(pk, like):
    shapes = [like[n].shape[1:] for l in range(DEPTH) for n in SMALL]
    pieces = _take(pk.reshape(-1), shapes)
    names = [n for l in range(DEPTH) for n in SMALL]
    return {n: jnp.stack([p for p, m in zip(pieces, names) if m == n]) for n in SMALL}


def pack_gather(w):
    parts = [w[n][l].astype(BF16) for l in range(DEPTH) for n in ["ada_w"] + list(SHARDED)]
    ln = jnp.concatenate([w[n][l].reshape(-1) for l in range(DEPTH) for n in ("ln_g", "ln_b")])
    parts.append(lax.bitcast_convert_type(ln, BF16))
    n = sum(int(np.prod(p.shape)) for p in parts)
    return _rows(parts, _round_up(-(-n // D), 16), BF16)


def unpack_gather(g, w):
    names = ["ada_w"] + list(SHARDED)
    shapes = [w[n].shape[1:] for l in range(DEPTH) for n in names]
    n_ln = DEPTH * 2 * 3 * (D // N_CHIPS)
    flat = g.reshape(N_CHIPS, -1)
    per_chip = [_take(flat[k], shapes + [(n_ln, 2)]) for k in range(N_CHIPS)]
    layers = [dict() for _ in range(DEPTH)]
    i = 0
    for l in range(DEPTH):
        for n in names:
            axis = 1 if n == "ada_w" else SHARDED[n]
            layers[l][n] = jnp.concatenate([per_chip[k][i] for k in range(N_CHIPS)], axis=axis)
            i += 1
    ln = [lax.bitcast_convert_type(per_chip[k][i], F32).reshape(DEPTH, 2, 3, D // N_CHIPS) for k in range(N_CHIPS)]
    ln = jnp.concatenate(ln, axis=3)
    for l in range(DEPTH):
        layers[l]["ln_g"], layers[l]["ln_b"] = ln[l, 0], ln[l, 1]
    return layers


def pack_grads(grads, k):
    parts = []
    for l in range(DEPTH):
        g = grads[l]
        full = {"ffn1_w_out": g["ffn1_out"], "ffn2_w_out": g["ffn2_out"], "mix_w_in": mix_in_unext(g["mix_in"]),
                "mix_w_out": mix_out_unext(g["mix_out"]), "mla_w_uq": uq_unext(g["wq"]), "mla_w_ukv": ukv_unext(g["wkv"])}
        for n, axis in SHARDED.items():
            if n in ("ffn1_w_in", "ffn2_w_in"):
                half = g[n.replace("_w_in", "_in")][k // 2]
                parts.append(half[:, (k % 2) * (FF // 2):(k % 2 + 1) * (FF // 2)])
            else:
                sz = full[n].shape[axis] // N_CHIPS
                parts.append(lax.slice_in_dim(full[n], k * sz, (k + 1) * sz, axis=axis))
    for l in range(DEPTH):
        for n in ("ln_g", "ln_b"):
            parts.append(grads[l][n][:, k * (D // N_CHIPS):(k + 1) * (D // N_CHIPS)])
    n = sum(int(np.prod(p.shape)) for p in parts)
    return _rows(parts, _round_up(-(-n // D), 16), F32)


def kernel(x, c, ada_w, ada_b, ln_g, ln_b, ffn1_w_in, ffn1_w_out, ffn2_w_in, ffn2_w_out, mix_w_in, mix_w_out, hgrn_lb_logits, hgrn_norm_g, mla_q_norm_g, mla_kv_norm_g, mla_w_uq, mla_w_ukv, fox_b_f, gmlp_ln_g, gmlp_ln_b, gmlp_w_s, gmlp_b_s, loss_target, m_ada_w, m_ada_b, m_ln_g, m_ln_b, m_ffn1_w_in, m_ffn1_w_out, m_ffn2_w_in, m_ffn2_w_out, m_mix_w_in, m_mix_w_out, m_hgrn_lb_logits, m_hgrn_norm_g, m_mla_q_norm_g, m_mla_kv_norm_g, m_mla_w_uq, m_mla_w_ukv, m_fox_b_f, m_gmlp_ln_g, m_gmlp_ln_b, m_gmlp_w_s, m_gmlp_b_s, v_ada_w, v_ada_b, v_ln_g, v_ln_b, v_ffn1_w_in, v_ffn1_w_out, v_ffn2_w_in, v_ffn2_w_out, v_mix_w_in, v_mix_w_out, v_hgrn_lb_logits, v_hgrn_norm_g, v_mla_q_norm_g, v_mla_kv_norm_g, v_mla_w_uq, v_mla_w_ukv, v_fox_b_f, v_gmlp_ln_g, v_gmlp_ln_b, v_gmlp_w_s, v_gmlp_b_s):
    w = dict(zip(WEIGHTS, (ada_w, ada_b, ln_g, ln_b, ffn1_w_in, ffn1_w_out, ffn2_w_in, ffn2_w_out, mix_w_in, mix_w_out, hgrn_lb_logits, hgrn_norm_g, mla_q_norm_g, mla_kv_norm_g, mla_w_uq, mla_w_ukv, fox_b_f, gmlp_ln_g, gmlp_ln_b, gmlp_w_s, gmlp_b_s)))
    m = dict(zip(WEIGHTS, (m_ada_w, m_ada_b, m_ln_g, m_ln_b, m_ffn1_w_in, m_ffn1_w_out, m_ffn2_w_in, m_ffn2_w_out, m_mix_w_in, m_mix_w_out, m_hgrn_lb_logits, m_hgrn_norm_g, m_mla_q_norm_g, m_mla_kv_norm_g, m_mla_w_uq, m_mla_w_ukv, m_fox_b_f, m_gmlp_ln_g, m_gmlp_ln_b, m_gmlp_w_s, m_gmlp_b_s)))
    v = dict(zip(WEIGHTS, (v_ada_w, v_ada_b, v_ln_g, v_ln_b, v_ffn1_w_in, v_ffn1_w_out, v_ffn2_w_in, v_ffn2_w_out, v_mix_w_in, v_mix_w_out, v_hgrn_lb_logits, v_hgrn_norm_g, v_mla_q_norm_g, v_mla_kv_norm_g, v_mla_w_uq, v_mla_w_ukv, v_fox_b_f, v_gmlp_ln_g, v_gmlp_ln_b, v_gmlp_w_s, v_gmlp_b_s)))
    Bl, S, _ = x.shape
    T = Bl * S
    core = lax.axis_index("c")
    chip = 2 * lax.axis_index("x") + lax.axis_index("y")

    def shard(key):
        n, l = key
        if n == "ln":
            return jnp.concatenate([ln_g[l:l + 1], ln_b[l:l + 1], jnp.zeros((1, 2, D // N_CHIPS), F32)], axis=1)
        return w[n][l:l + 1].astype(BF16)

    mixers = ["mix_w_in", "mix_w_out", "mla_w_uq", "mla_w_ukv"]
    groups = [[("ada_w", 0), ("ffn1_w_in", 0), ("ffn1_w_out", 0), ("ln", 0)],
              [(n, 0) for n in mixers + ["ffn2_w_in", "ffn2_w_out"]],
              [(n, 1) for n in GATHERED + ["ln"]]]
    srcs = [[shard(k) for k in grp] for grp in groups]
    lands = [[own_slot(s, chip) for s in grp] for grp in srcs]
    handle0 = ag_start(srcs[0], lands[0], jnp.zeros((8, 128), F32), "ag_start_0")
    first, token = ag_forward(ag_wait(handle0, lands[2][0], "ag_wait_0")[1], "ag_forward_0")
    have = dict(zip(groups[0], first))
    handles = {}
    for gi in (1, 2):
        handles[gi] = ag_start(srcs[gi], lands[gi], token, "ag_start_%d" % gi)
        token = handles[gi][-1]
    c8 = jnp.concatenate([c, jnp.zeros((8 - Bl, D), F32)], axis=0)
    c8 = c8 + token[0, 0]

    def cat_cols(a):
        return jnp.concatenate([a[0, k] for k in range(N_CHIPS)], axis=1)

    def get(l, part, after):
        gi = 2 if l == 1 else (0 if part in ("ada", "ffn1") else 1)
        if gi in handles:
            arrived, _ = ag_forward(ag_wait(handles.pop(gi), after, "ag_wait_%d" % gi)[1], "ag_forward_%d" % gi)
            have.update(zip(groups[gi], arrived))
        if part == "ada":
            return dict(ada_w=have[("ada_w", l)], wl=0, ada_b=ada_b[l][None])
        if part == "ffn1":
            ln_full = jnp.moveaxis(have[("ln", l)][0], 0, 1).reshape(8, D)
            return dict(ffn1_in=have[("ffn1_w_in", l)], ffn1_out=have[("ffn1_w_out", l)], wl=0,
                        ln_g=ln_full[0:3], ln_b=ln_full[3:6])
        if part == "ffn2":
            return dict(ffn2_in=have[("ffn2_w_in", l)], ffn2_out=have[("ffn2_w_out", l)])
        return dict(
            mix_in=mix_in_ext(cat_cols(have[("mix_w_in", l)])), mix_out=mix_out_ext(have[("mix_w_out", l)].reshape(D, D)),
            wq=uq_ext(cat_cols(have[("mla_w_uq", l)])).astype(F32), wkv=ukv_ext(cat_cols(have[("mla_w_ukv", l)])).astype(F32),
            ng=hgrn_norm_g[l][None], gq=mla_q_norm_g[l][None], gkv=mla_kv_norm_g[l][None],
            bcol=jnp.concatenate([fox_b_f[l], jnp.zeros((8 - HEADS,), F32)])[:, None],
            g_lg=gmlp_ln_g[l][None], g_lb=gmlp_ln_b[l][None], ws=gmlp_w_s[l], bs=gmlp_b_s[l][:, :, None])

    pending = []

    def emit(l, part, g):
        if part == "mix":
            names = mixers
            by_chip = [_col_shards(mix_in_unext(g["mix_in"])), mix_out_unext(g["mix_out"]).reshape(N_CHIPS, D // N_CHIPS, D),
                       _col_shards(uq_unext(g["wq"])).astype(BF16), _col_shards(ukv_unext(g["wkv"])).astype(BF16)]
        else:
            names = [part + "_w_in", part + "_w_out"]
            by_chip = [g[part + "_in"], g[part + "_out"]]
        tag = "%d_%s" % (l, part)
        got = sibling_swap(by_chip, "sibling_swap_" + tag)
        chip_sum = [add_kept_half(a, r, core, "add_sibling") for a, r in zip(by_chip, got)]
        zones = [lax.dynamic_update_slice(lax.empty(h.shape, h.dtype), lax.dynamic_slice_in_dim(h, chip, 1, axis=0), (chip, 0, 0))
                 for h in chip_sum]
        handle = rs_start(chip_sum, zones, chip_sum[0], "rs_start_" + tag)
        pending.append((l, names, handle, tag))
        return handle[-1][0, 0]

    loss_tile, dx, dmods, grads, d_logits = local_step(
        x.reshape(T, D), c8, loss_target.reshape(T, D), get, hgrn_lb_logits, S, emit)
    loss = lax.psum(loss_tile[0, 0], ("x", "y", "c"))

    small_g = {"hgrn_lb_logits": d_logits,
               "hgrn_norm_g": jnp.stack([grads[l]["ng"][0] for l in range(DEPTH)]),
               "mla_q_norm_g": jnp.stack([grads[l]["gq"][0] for l in range(DEPTH)]),
               "mla_kv_norm_g": jnp.stack([grads[l]["gkv"][0] for l in range(DEPTH)]),
               "fox_b_f": jnp.stack([grads[l]["bcol"][0:HEADS, 0] for l in range(DEPTH)]),
               "gmlp_ln_g": jnp.stack([grads[l]["g_lg"][0] for l in range(DEPTH)]),
               "gmlp_ln_b": jnp.stack([grads[l]["g_lb"][0] for l in range(DEPTH)]),
               "gmlp_w_s": jnp.stack([grads[l]["ws"] for l in range(DEPTH)]),
               "gmlp_b_s": jnp.stack([grads[l]["bs"][:, :, 0] for l in range(DEPTH)])}
    small_g["ln_g"] = jnp.stack([grads[l]["ln_g"] for l in range(DEPTH)])
    small_g["ln_b"] = jnp.stack([grads[l]["ln_b"] for l in range(DEPTH)])
    pk_small = pack_small(small_g)
    n_small = pk_small.shape[0]
    extras = [dmods[l] for l in range(DEPTH)] + [c]
    n_extra = _round_up(-(-sum(int(np.prod(e.shape)) for e in extras) // D), 8)
    gathered = ag_all(jnp.concatenate([pk_small, _rows(extras, n_extra, F32)], axis=0))
    g_small = unpack_small(sum_slots(gathered[:, 0:n_small], 8, "sum_small"), small_g)
    ext = gathered[:, n_small:].reshape(8, -1)
    n_dmod = DEPTH * Bl * N_MOD * D
    dmod_all = ext[:, 0:n_dmod].reshape(8, DEPTH, Bl, N_MOD * D)
    c_all = ext[:, n_dmod:n_dmod + Bl * D].reshape(8 * Bl, D)
    g_ada_w, g_ada_b = [], []
    ncol = N_MOD * D // N_CHIPS
    for l in range(DEPTH):
        dm = dmod_all[:, l].reshape(8 * Bl, N_MOD * D)
        g_ada_w.append(ada_grad(c_all, lax.dynamic_slice_in_dim(dm, chip * ncol, ncol, axis=1)))
        g_ada_b.append(sum_slots(dm.reshape(8 * Bl, N_MOD, D), 8 * Bl, "sum_ada_b").reshape(N_MOD * D))
    g_ada_w, g_ada_b = jnp.stack(g_ada_w), jnp.stack(g_ada_b)

    red = {n: lax.empty(w[n].shape, F32) for n in REDUCED}

    def arrive(entry, after):
        l, names, handle, tag = entry
        for n, land in zip(names, rs_wait(handle, after, "rs_wait_" + tag)[1]):
            red[n] = sum_into(land, red[n], l, core, "sum_chips")

    for entry in pending[:-1]:
        arrive(entry, dx)
    late = pending[-1][1]
    early = [n for n in REDUCED if n not in late]
    grad = dict(zip(early, sibling_join([red[n] for n in early], "sibling_join_a")))
    grad.update(g_small)
    grad["ada_w"], grad["ada_b"] = g_ada_w, g_ada_b
    for n in ("ln_g", "ln_b"):
        grad[n] = lax.dynamic_slice_in_dim(g_small[n], chip * (D // N_CHIPS), D // N_CHIPS, axis=2)
    out = {"grad": grad, "delta": {}, "new_m": {}, "new_v": {}}

    def update(n):
        shp = w[n].shape
        two_d = (-1, shp[-1])
        res = adamw(w[n].reshape(two_d), grad[n].reshape(two_d), m[n].reshape(two_d), v[n].reshape(two_d), "adamw_" + n,
                    echo=n in REDUCED)
        grad[n] = (res[3] if n in REDUCED else grad[n]).reshape(shp)
        for key, r in zip(("delta", "new_m", "new_v"), res):
            out[key][n] = r.reshape(shp)

    for n in WEIGHTS:
        if n not in late:
            update(n)
    arrive(pending[-1], out["delta"]["ffn2_w_in"])
    grad.update(zip(late, sibling_join([red[n] for n in late], "sibling_join_b")))
    for n in late:
        update(n)
    outs = [loss, dx.reshape(Bl, S, D)]
    for key in ("grad", "delta", "new_m", "new_v"):
        outs += [out[key][n] for n in WEIGHTS]
    return tuple(outs)
```

```python
import functools

import jax
import jax.numpy as jnp
import numpy as np
from jax import lax
from jax.experimental import pallas as pl
from jax.experimental.pallas import tpu as pltpu

F32, BF16 = jnp.float32, jnp.bfloat16
MESH = pl.DeviceIdType.MESH

N_CHIPS = 4
D = 1024
DEPTH = 2
FF = 2816
N_MOD = 9
GW = 256
HEADS = 4
HD = 64
HP = 128
A_CHUNK = 16
LB_FLOOR = 1e-30
B_Q_LORA, B_KV_LORA, B_NOPE, B_ROPE = 256, 128, 64, 32
ROPE_THETA = 10000.0
D_CHUNK = 128
ALPHA = (2 * DEPTH) ** 0.25
LN_EPS = 1e-5
RMS_EPS = 1e-6
ADAM_LR, ADAM_B1, ADAM_B2, ADAM_EPS, ADAM_WD, ADAM_STEP = 0.001, 0.9, 0.999, 1e-08, 0.01, 10

NP = 3840
NP_TILE = 1920
C_A, C_B, C_CQ, C_CK, C_CV, C_D, C_CF = 0, 1024, 1536, 2048, 2560, 3072, 3584
NCAT = 1536
O_A, O_B, O_C, O_D = 0, 256, 768, 1280

VMEM_BIG = 48 << 20


def _cparams(vmem=None):
    return pltpu.CompilerParams(vmem_limit_bytes=vmem) if vmem else pltpu.CompilerParams()


def _sds(shape, dtype):
    return jax.ShapeDtypeStruct(tuple(shape), dtype)


@jax.custom_vjp
def _mm(a, w):
    return jnp.dot(a.astype(BF16), w.astype(BF16), preferred_element_type=F32)


def _mm_f(a, w):
    return _mm(a, w), (a, w)


def _mm_b(res, g):
    a, w = res
    gb = g.astype(BF16)
    da = lax.dot_general(gb, w.astype(BF16), (((1,), (1,)), ((), ())), preferred_element_type=F32)
    dw = lax.dot_general(a.astype(BF16), gb, (((0,), (0,)), ((), ())), preferred_element_type=F32)
    return da.astype(a.dtype), dw.astype(w.dtype)


_mm.defvjp(_mm_f, _mm_b)


@jax.custom_vjp
def _mm_nt(a, b):
    return lax.dot_general(a.astype(BF16), b.astype(BF16), (((1,), (1,)), ((), ())), preferred_element_type=F32)


def _mm_nt_f(a, b):
    return _mm_nt(a, b), (a, b)


def _mm_nt_b(res, g):
    a, b = res
    gb = g.astype(BF16)
    da = jnp.dot(gb, b.astype(BF16), preferred_element_type=F32)
    db = lax.dot_general(gb, a.astype(BF16), (((0,), (0,)), ((), ())), preferred_element_type=F32)
    return da.astype(a.dtype), db.astype(b.dtype)


_mm_nt.defvjp(_mm_nt_f, _mm_nt_b)


@jax.custom_vjp
def _mm_tn(a, b):
    return lax.dot_general(a.astype(BF16), b.astype(BF16), (((0,), (0,)), ((), ())), preferred_element_type=F32)


def _mm_tn_f(a, b):
    return _mm_tn(a, b), (a, b)


def _mm_tn_b(res, g):
    a, b = res
    gb = g.astype(BF16)
    da = lax.dot_general(b.astype(BF16), gb, (((1,), (1,)), ((), ())), preferred_element_type=F32)
    db = jnp.dot(a.astype(BF16), gb, preferred_element_type=F32)
    return da.astype(a.dtype), db.astype(b.dtype)


_mm_tn.defvjp(_mm_tn_f, _mm_tn_b)


def _split3(x):
    p1 = x.astype(BF16)
    r = x - p1.astype(F32)
    p2 = r.astype(BF16)
    return p1, p2, (r - p2.astype(F32)).astype(BF16)


@jax.custom_vjp
def _sel_r(x, sel):
    s = sel.astype(BF16)
    return sum(jnp.dot(p, s, preferred_element_type=F32) for p in _split3(x))


def _sel_r_f(x, sel):
    return _sel_r(x, sel), sel


def _sel_r_b(sel, g):
    s = sel.astype(BF16)
    dx = sum(lax.dot_general(p, s, (((1,), (1,)), ((), ())), preferred_element_type=F32) for p in _split3(g))
    return dx, jnp.zeros_like(sel)


_sel_r.defvjp(_sel_r_f, _sel_r_b)


@jax.custom_vjp
def _sel_l(sel, x):
    s = sel.astype(BF16)
    return sum(jnp.dot(s, p, preferred_element_type=F32) for p in _split3(x))


def _sel_l_f(sel, x):
    return _sel_l(sel, x), sel


def _sel_l_b(sel, g):
    s = sel.astype(BF16)
    dx = sum(lax.dot_general(s, p, (((0,), (0,)), ((), ())), preferred_element_type=F32) for p in _split3(g))
    return jnp.zeros_like(sel), dx


_sel_l.defvjp(_sel_l_f, _sel_l_b)


def _iota(shape, dim):
    return lax.broadcasted_iota(jnp.int32, shape, dim)


def _head_sum_mats():
    e = (_iota((GW, HP), 0) // HD == _iota((GW, HP), 1)).astype(F32)
    et = (_iota((HP, GW), 1) // HD == _iota((HP, GW), 0)).astype(F32)
    return e, et


def _modulate(x, mod_ref, sh, sc):
    return x * (1.0 + mod_ref[sc:sc + 1, :]) + mod_ref[sh:sh + 1, :]


def _ln_res(x, y, gate, lg, lb, gs):
    r = ALPHA * x + gs * (1.0 + gate) * y
    mu = jnp.mean(r, axis=-1, keepdims=True)
    var = jnp.mean(jnp.square(r - mu), axis=-1, keepdims=True)
    return (r - mu) * lax.rsqrt(var + LN_EPS) * lg + lb


def _rms(x, g):
    return x * lax.rsqrt(jnp.mean(x * x, axis=-1, keepdims=True) + RMS_EPS) * g


def _tile(n, pref):
    return pref if n % pref == 0 else n


def mod_fwd(c8, w, l, b):
    tn = w.shape[3]
    n = N_CHIPS * tn

    def body(c_ref, w_ref, b_ref, o_ref):
        h = jax.nn.silu(c_ref[...]).astype(BF16)
        o_ref[...] = jnp.dot(h, w_ref[...], preferred_element_type=F32) + b_ref[...]

    return pl.pallas_call(
        body, name="mod_fwd", grid=(N_CHIPS,),
        in_specs=[pl.BlockSpec((8, D), lambda j: (0, 0)), pl.BlockSpec((None, None, D, tn), lambda j: (l, j, 0, 0)),
                  pl.BlockSpec((1, tn), lambda j: (0, j))],
        out_specs=pl.BlockSpec((8, tn), lambda j: (0, j)), out_shape=_sds((8, n), F32),
        compiler_params=_cparams(VMEM_BIG))(c8, w, b)


def ffn_in_fwd(x, mod, w_in, l, sh, sc, S):
    T = x.shape[0]
    tm, tn = _tile(S, 512), FF // 2
    tpb, nj = S // tm, 2

    def body(x_ref, mod_ref, wg_ref, wu_ref, zg_ref, zu_ref, act_ref, h_ref):
        @pl.when(pl.program_id(1) == 0)
        def _():
            h_ref[...] = _modulate(x_ref[...], mod_ref, sh, sc).astype(BF16)
        g = jnp.dot(h_ref[...], wg_ref[...], preferred_element_type=F32)
        u = jnp.dot(h_ref[...], wu_ref[...], preferred_element_type=F32)
        zg_ref[...] = g.astype(BF16)
        zu_ref[...] = u.astype(BF16)
        act_ref[...] = (jax.nn.silu(g) * u).astype(BF16)

    return pl.pallas_call(
        body, name="ffn_in_fwd", grid=(T // tm, nj),
        in_specs=[pl.BlockSpec((tm, D), lambda i, j: (i, 0)),
                  pl.BlockSpec((None, N_MOD, D), lambda i, j: (i // tpb, 0, 0)),
                  pl.BlockSpec((None, None, D, tn), lambda i, j: (l, j, 0, 0)),
                  pl.BlockSpec((None, None, D, tn), lambda i, j: (l, j + nj, 0, 0))],
        out_specs=[pl.BlockSpec((tm, tn), lambda i, j: (i, j))] * 3,
        out_shape=[_sds((T, FF), BF16)] * 3,
        scratch_shapes=[pltpu.VMEM((tm, D), BF16)],
        compiler_params=_cparams(VMEM_BIG))(x, mod, w_in, w_in)


def mix_in_fwd(x, mod, w, sh, sc, S):
    T = x.shape[0]
    n = w.shape[1]
    tm, tn = _tile(S, 512), NP_TILE
    tpb = S // tm

    def body(x_ref, mod_ref, w_ref, o_ref, h_ref):
        @pl.when(pl.program_id(1) == 0)
        def _():
            h_ref[...] = _modulate(x_ref[...], mod_ref, sh, sc).astype(BF16)
        o_ref[...] = jnp.dot(h_ref[...], w_ref[...], preferred_element_type=F32)

    return pl.pallas_call(
        body, name="mix_in_fwd", grid=(T // tm, n // tn),
        in_specs=[pl.BlockSpec((tm, D), lambda i, j: (i, 0)),
                  pl.BlockSpec((None, N_MOD, D), lambda i, j: (i // tpb, 0, 0)),
                  pl.BlockSpec((D, tn), lambda i, j: (0, j))],
        out_specs=pl.BlockSpec((tm, tn), lambda i, j: (i, j)), out_shape=_sds((T, n), F32),
        scratch_shapes=[pltpu.VMEM((tm, D), BF16)],
        compiler_params=_cparams(VMEM_BIG))(x, mod, w)


def out_ln_fwd(act, w_out, x, mod, lg, lb, gate, gs, S, l=None):
    T, K = act.shape
    tm = _tile(S, 512)
    tpb = S // tm

    def body(a_ref, w_ref, x_ref, mod_ref, lg_ref, lb_ref, y_ref, xn_ref):
        y = jnp.dot(a_ref[...], w_ref[...].reshape(K, D), preferred_element_type=F32)
        y_ref[...] = y
        xn_ref[...] = _ln_res(x_ref[...], y, mod_ref[gate:gate + 1, :], lg_ref[...], lb_ref[...], gs)

    if l is None:
        w_spec = pl.BlockSpec((K, D), lambda i: (0, 0))
    else:
        w_spec = pl.BlockSpec((None, N_CHIPS, K // N_CHIPS, D), lambda i: (l, 0, 0, 0))
    return pl.pallas_call(
        body, name="out_ln_fwd", grid=(T // tm,),
        in_specs=[pl.BlockSpec((tm, K), lambda i: (i, 0)), w_spec,
                  pl.BlockSpec((tm, D), lambda i: (i, 0)),
                  pl.BlockSpec((None, N_MOD, D), lambda i: (i // tpb, 0, 0)),
                  pl.BlockSpec((1, D), lambda i: (0, 0)), pl.BlockSpec((1, D), lambda i: (0, 0))],
        out_specs=[pl.BlockSpec((tm, D), lambda i: (i, 0))] * 2,
        out_shape=[_sds((T, D), F32), _sds((T, D), F32)],
        compiler_params=_cparams(VMEM_BIG))(act, w_out, x, mod, lg, lb)


def ln_res_bwd(dxn, x, y, mod, lg, lb, gate, gs, S):
    T = x.shape[0]
    B = T // S
    tm = _tile(S, 512)
    tpb = S // tm

    def body(d_ref, x_ref, y_ref, mod_ref, lg_ref, lb_ref, dx_ref, dy_ref, dg_ref, dlg_ref, dlb_ref):
        i = pl.program_id(0)
        f = functools.partial(_ln_res, gs=gs)
        _, vjp = jax.vjp(f, x_ref[...], y_ref[...], mod_ref[gate:gate + 1, :], lg_ref[...], lb_ref[...])
        dx, dy, dg, dlg, dlb = vjp(d_ref[...])
        dx_ref[...] = dx
        dy_ref[...] = dy.astype(BF16)

        @pl.when(i % tpb == 0)
        def _():
            dg_ref[...] = jnp.zeros_like(dg_ref)

        @pl.when(i == 0)
        def _():
            dlg_ref[...] = jnp.zeros_like(dlg_ref)
            dlb_ref[...] = jnp.zeros_like(dlb_ref)

        dg_ref[...] += dg
        dlg_ref[...] += dlg
        dlb_ref[...] += dlb

    tok = pl.BlockSpec((tm, D), lambda i: (i, 0))
    vec = pl.BlockSpec((1, D), lambda i: (0, 0))
    return pl.pallas_call(
        body, name="ln_res_bwd", grid=(T // tm,),
        in_specs=[tok, tok, tok, pl.BlockSpec((None, N_MOD, D), lambda i: (i // tpb, 0, 0)), vec, vec],
        out_specs=[tok, tok, pl.BlockSpec((None, 1, D), lambda i: (i // tpb, 0, 0)), vec, vec],
        out_shape=[_sds((T, D), F32), _sds((T, D), BF16), _sds((B, 1, D), F32), _sds((1, D), F32), _sds((1, D), F32)],
        compiler_params=_cparams(VMEM_BIG))(dxn, x, y, mod, lg, lb)


def swiglu_bwd(dy, w_out, l, zg, zu, S):
    T = dy.shape[0]
    tm, tn = _tile(S, 512), FF // 2

    def body(dy_ref, w_ref, zg_ref, zu_ref, dg_ref, du_ref):
        da = lax.dot_general(dy_ref[...], w_ref[...].reshape(tn, D), (((1,), (1,)), ((), ())), preferred_element_type=F32)
        g, u = zg_ref[...].astype(F32), zu_ref[...].astype(F32)
        sg = jax.nn.sigmoid(g)
        dg_ref[...] = (da * u * (sg * (1.0 + g * (1.0 - sg)))).astype(BF16)
        du_ref[...] = (da * (g * sg)).astype(BF16)

    zt = pl.BlockSpec((tm, tn), lambda i, j: (i, j))
    return pl.pallas_call(
        body, name="swiglu_bwd", grid=(T // tm, FF // tn),
        in_specs=[pl.BlockSpec((tm, D), lambda i, j: (i, 0)),
                  pl.BlockSpec((None, 2, FF // N_CHIPS, D), lambda i, j: (l, j, 0, 0)), zt, zt],
        out_specs=[zt, zt], out_shape=[_sds((T, FF), BF16), _sds((T, FF), BF16)],
        compiler_params=_cparams(VMEM_BIG))(dy, w_out, zg, zu)


def nt_plain(dy, w):
    T = dy.shape[0]
    K = w.shape[0]
    tm = _tile(T, 512)

    def body(dy_ref, w_ref, o_ref):
        o_ref[...] = lax.dot_general(dy_ref[...], w_ref[...], (((1,), (1,)), ((), ())), preferred_element_type=F32)

    return pl.pallas_call(
        body, name="nt_plain", grid=(T // tm,),
        in_specs=[pl.BlockSpec((tm, D), lambda i: (i, 0)), pl.BlockSpec((K, D), lambda i: (0, 0))],
        out_specs=pl.BlockSpec((tm, K), lambda i: (i, 0)), out_shape=_sds((T, K), F32),
        compiler_params=_cparams(VMEM_BIG))(dy, w)


def _tn_step(acc, o_ref, lhs, rhs, t, nt):
    part = lax.dot_general(lhs, rhs, (((0,), (0,)), ((), ())), preferred_element_type=F32)
    if nt == 1:
        o_ref[...] = part.astype(o_ref.dtype)
        return

    @pl.when(t == 0)
    def _():
        acc[...] = part

    @pl.when((t > 0) & (t < nt - 1))
    def _():
        acc[...] += part

    @pl.when(t == nt - 1)
    def _():
        o_ref[...] = (acc[...] + part).astype(o_ref.dtype)


def tn_mm(a, b, tk):
    T, K = a.shape
    N = b.shape[1]
    tt = _tile(T, 512)
    nt = T // tt

    def body(a_ref, b_ref, o_ref, acc):
        _tn_step(acc, o_ref, a_ref[...], b_ref[...], pl.program_id(1), nt)

    return pl.pallas_call(
        body, name="tn_mm", grid=(K // tk, nt),
        in_specs=[pl.BlockSpec((tt, tk), lambda k, t: (t, k)), pl.BlockSpec((tt, N), lambda k, t: (t, 0))],
        out_specs=pl.BlockSpec((tk, N), lambda k, t: (k, 0)), out_shape=_sds((K, N), BF16),
        scratch_shapes=[pltpu.VMEM((tk, N), F32)], compiler_params=_cparams(VMEM_BIG))(a, b)


def tn_mm_mod(x, mod, b, sh, sc, S, tn):
    T = x.shape[0]
    N = b.shape[1]
    tt = _tile(S, 512)
    tpb = S // tt
    nt = T // tt

    def body(x_ref, mod_ref, b_ref, o_ref, acc):
        h = _modulate(x_ref[...], mod_ref, sh, sc).astype(BF16)
        _tn_step(acc, o_ref, h, b_ref[...], pl.program_id(1), nt)

    return pl.pallas_call(
        body, name="tn_mm_mod", grid=(N // tn, nt),
        in_specs=[pl.BlockSpec((tt, D), lambda j, t: (t, 0)),
                  pl.BlockSpec((None, N_MOD, D), lambda j, t: (t // tpb, 0, 0)),
                  pl.BlockSpec((tt, tn), lambda j, t: (t, j))],
        out_specs=pl.BlockSpec((D, tn), lambda j, t: (0, j)), out_shape=_sds((D, N), BF16),
        scratch_shapes=[pltpu.VMEM((D, tn), F32)], compiler_params=_cparams(VMEM_BIG))(x, mod, b)


def tn_mm_mod_shards(x, mod, bg, bu, sh, sc, S):
    T = x.shape[0]
    tn = FF // 2
    tt = _tile(S, 512)
    tpb = S // tt
    nt = T // tt

    def body(x_ref, mod_ref, bg_ref, bu_ref, o_ref, acc):
        j, t = pl.program_id(0), pl.program_id(1)
        h = _modulate(x_ref[...], mod_ref, sh, sc).astype(BF16)

        @pl.when(j < 2)
        def _():
            _tn_step(acc, o_ref, h, bg_ref[...], t, nt)

        @pl.when(j >= 2)
        def _():
            _tn_step(acc, o_ref, h, bu_ref[...], t, nt)

    return pl.pallas_call(
        body, name="tn_mm_mod_shards", grid=(N_CHIPS, nt),
        in_specs=[pl.BlockSpec((tt, D), lambda j, t: (t, 0)),
                  pl.BlockSpec((None, N_MOD, D), lambda j, t: (t // tpb, 0, 0)),
                  pl.BlockSpec((tt, tn), lambda j, t: (jnp.where(j < 2, t, 0), jnp.minimum(j, 1))),
                  pl.BlockSpec((tt, tn), lambda j, t: (jnp.where(j < 2, 0, t), jnp.maximum(j - 2, 0)))],
        out_specs=pl.BlockSpec((None, D, tn), lambda j, t: (j, 0, 0)), out_shape=_sds((N_CHIPS, D, tn), BF16),
        scratch_shapes=[pltpu.VMEM((D, tn), F32)], compiler_params=_cparams(VMEM_BIG))(x, mod, bg, bu)


def nt_mod_bwd(ds, w, offs, x, mod, dres, sc, S, tk, l=None):
    T = x.shape[0]
    B = T // S
    tm = _tile(S, 512)
    tpb = S // tm
    Kd = ds[0].shape[1]
    nk = Kd // tk
    n_in = len(ds)

    def body(*refs):
        d_refs, w_refs = refs[:n_in], refs[n_in:2 * n_in]
        x_ref, mod_ref, r_ref, dx_ref, dsh_ref, dsc_ref, acc = refs[2 * n_in:]
        i, k = pl.program_id(0), pl.program_id(1)

        part = sum(lax.dot_general(d_ref[...], w_ref[...], (((1,), (1,)), ((), ())), preferred_element_type=F32)
                   for d_ref, w_ref in zip(d_refs, w_refs))

        @pl.when(k == 0)
        def _():
            acc[...] = part

        @pl.when(k > 0)
        def _():
            acc[...] += part

        @pl.when(k == nk - 1)
        def _():
            dh = acc[...]
            dx_ref[...] = dh * (1.0 + mod_ref[sc:sc + 1, :]) + r_ref[...]

            @pl.when(i % tpb == 0)
            def _():
                dsh_ref[...] = jnp.zeros_like(dsh_ref)
                dsc_ref[...] = jnp.zeros_like(dsc_ref)

            dsh_ref[...] += jnp.sum(dh, axis=0, keepdims=True)
            dsc_ref[...] += jnp.sum(dh * x_ref[...], axis=0, keepdims=True)

    tok = pl.BlockSpec((tm, D), lambda i, k: (i, 0))
    vec = pl.BlockSpec((None, 1, D), lambda i, k: (i // tpb, 0, 0))
    in_specs = [pl.BlockSpec((tm, tk), lambda i, k: (i, k)) for _ in ds]
    if l is None:
        in_specs += [pl.BlockSpec((D, tk), functools.partial(lambda i, k, o: (0, k + o), o=off // tk)) for off in offs]
    else:
        in_specs += [pl.BlockSpec((None, None, D, tk), functools.partial(lambda i, k, o: (l, k + o, 0, 0), o=off)) for off in offs]
    in_specs += [tok, pl.BlockSpec((None, N_MOD, D), lambda i, k: (i // tpb, 0, 0)), tok]
    return pl.pallas_call(
        body, name="nt_mod_bwd", grid=(T // tm, nk), in_specs=in_specs,
        out_specs=[tok, vec, vec],
        out_shape=[_sds((T, D), F32), _sds((B, 1, D), F32), _sds((B, 1, D), F32)],
        scratch_shapes=[pltpu.VMEM((tm, D), F32)],
        compiler_params=_cparams(VMEM_BIG))(*ds, *([w] * n_in), x, mod, dres)


def loss_head(y, tgt):
    T = y.shape[0]
    tm = _tile(T, 512)

    def body(y_ref, t_ref, l_ref, d_ref):
        @pl.when(pl.program_id(0) == 0)
        def _():
            l_ref[...] = jnp.zeros_like(l_ref)
        e = y_ref[...] - t_ref[...]
        d_ref[...] = e * (1.0 / D)
        l_ref[...] += 0.5 * jnp.sum(jnp.sum(e * e, axis=1, keepdims=True) * (1.0 / D))

    tok = pl.BlockSpec((tm, D), lambda i: (i, 0))
    return pl.pallas_call(
        body, name="loss_head", grid=(T // tm,), in_specs=[tok, tok],
        out_specs=[pl.BlockSpec((8, 128), lambda i: (0, 0)), tok],
        out_shape=[_sds((8, 128), F32), _sds((T, D), F32)],
        compiler_params=_cparams(VMEM_BIG))(y, tgt)


def _hgrn_block(q, fz, inp, go, st, lb, ng, blk):
    nc = blk // A_CHUNK
    lb_eff = jnp.maximum(lb, LB_FLOOR)
    log_f = jnp.logaddexp(jnp.log(lb_eff), jnp.log1p(-lb) + jax.nn.log_sigmoid(fz))
    k = (1.0 - lb) * jax.nn.sigmoid(-fz) - (lb_eff - lb)
    qf = jax.nn.silu(q)
    same_chunk = _iota((blk, blk), 0) // A_CHUNK == _iota((blk, blk), 1) // A_CHUNK
    tril = (same_chunk & (_iota((blk, blk), 1) <= _iota((blk, blk), 0))).astype(F32)
    G = _sel_l(tril, log_f)
    e_mat, et_mat = _head_sum_mats()
    G4, q4, k4, v4 = (z.reshape(nc, A_CHUNK, GW) for z in (G, qf, k, inp))
    shp = (nc, A_CHUNK, A_CHUNK, GW)
    one = (1, A_CHUNK, A_CHUNK, GW)
    mask = jnp.where(_iota(one, 2) <= _iota(one, 1), 0.0, -jnp.inf)
    decay = jnp.exp((G4[:, :, None, :] - G4[:, None, :, :]) + mask)
    prod = q4[:, :, None, :] * k4[:, None, :, :] * decay
    scores = _mm(prod.reshape(nc * A_CHUNK * A_CHUNK, GW), e_mat.astype(BF16))
    spread = _mm(scores, et_mat.astype(BF16)).reshape(shp)
    o_intra = jnp.sum(spread * v4[:, None, :, :], axis=2).reshape(blk, GW)
    head_diag = (_iota((GW, GW), 0) // HD == _iota((GW, GW), 1) // HD).astype(F32)
    g_last = [jnp.sum(log_f[c * A_CHUNK:(c + 1) * A_CHUNK], axis=0, keepdims=True) for c in range(nc)]
    g_last_b = jnp.concatenate([jnp.broadcast_to(g, (A_CHUNK, GW)) for g in g_last], axis=0)
    q_dec = qf * jnp.exp(G)
    k_end = k * jnp.exp(g_last_b - G)
    outs = []
    for c in range(nc):
        rows = slice(c * A_CHUNK, (c + 1) * A_CHUNK)
        outs.append(_mm_nt(q_dec[rows], st))
        st = st * jnp.exp(g_last[c]) + _mm_tn(inp[rows], k_end[rows]) * head_diag
    o = o_intra + jnp.concatenate(outs, axis=0)
    ms = _sel_r(o * o, e_mat) * (1.0 / HD)
    o = o * _sel_r(lax.rsqrt(ms + RMS_EPS), et_mat) * ng
    return o * jax.nn.silu(go), st


HGRN_BLK = 128


def hgrn_fwd(proj, lb, ng, S):
    T = proj.shape[0]
    B = T // S
    blk = min(HGRN_BLK, S)
    nb = S // blk

    def body(p_ref, lb_ref, ng_ref, o_ref, st_out_ref, st_ref):
        @pl.when(pl.program_id(1) == 0)
        def _():
            st_ref[...] = jnp.zeros_like(st_ref)
        st_out_ref[...] = st_ref[...]
        p = p_ref[...]
        o, st = _hgrn_block(p[:, 0:GW], p[:, GW:2 * GW], p[:, 2 * GW:3 * GW], p[:, 3 * GW:4 * GW],
                            st_ref[...], lb_ref[...], ng_ref[...], blk)
        o_ref[...] = o.astype(BF16)
        st_ref[...] = st

    vec = pl.BlockSpec((1, GW), lambda b, j: (0, 0))
    return pl.pallas_call(
        body, name="hgrn_fwd", grid=(B, nb),
        in_specs=[pl.BlockSpec((blk, 4 * GW), lambda b, j: (b * nb + j, C_A // (4 * GW))), vec, vec],
        out_specs=[pl.BlockSpec((blk, GW), lambda b, j: (b * nb + j, 0)),
                   pl.BlockSpec((None, GW, GW), lambda b, j: (b * nb + j, 0, 0))],
        out_shape=[_sds((T, GW), BF16), _sds((B * nb, GW, GW), F32)],
        scratch_shapes=[pltpu.VMEM((GW, GW), F32)],
        compiler_params=_cparams(VMEM_BIG))(proj, lb, ng)


def hgrn_bwd(proj, states, dcat, lb, ng, S):
    T = proj.shape[0]
    B = T // S
    blk = min(HGRN_BLK, S)
    nb = S // blk

    def body(p_ref, st_in_ref, do_ref, lb_ref, ng_ref, dp_ref, dlb_ref, dng_ref, dst_ref):
        b, j = pl.program_id(0), pl.program_id(1)

        @pl.when(j == 0)
        def _():
            dst_ref[...] = jnp.zeros_like(dst_ref)

        @pl.when((b == 0) & (j == 0))
        def _():
            dlb_ref[...] = jnp.zeros_like(dlb_ref)
            dng_ref[...] = jnp.zeros_like(dng_ref)

        p = p_ref[...]
        f = functools.partial(_hgrn_block, blk=blk)
        _, vjp = jax.vjp(f, p[:, 0:GW], p[:, GW:2 * GW], p[:, 2 * GW:3 * GW], p[:, 3 * GW:4 * GW],
                         st_in_ref[...], lb_ref[...], ng_ref[...])
        dq, df, di, dg, dst, dlb, dng = vjp((do_ref[...], dst_ref[...]))
        dp_ref[...] = jnp.concatenate([dq, df, di, dg], axis=1).astype(BF16)
        dst_ref[...] = dst
        dlb_ref[...] += dlb
        dng_ref[...] += dng

    def rev(b, j):
        return b * nb + (nb - 1 - j)

    vec = pl.BlockSpec((1, GW), lambda b, j: (0, 0))
    return pl.pallas_call(
        body, name="hgrn_bwd", grid=(B, nb),
        in_specs=[pl.BlockSpec((blk, 4 * GW), lambda b, j: (rev(b, j), C_A // (4 * GW))),
                  pl.BlockSpec((None, GW, GW), lambda b, j: (rev(b, j), 0, 0)),
                  pl.BlockSpec((blk, GW), lambda b, j: (rev(b, j), O_A // GW)), vec, vec],
        out_specs=[pl.BlockSpec((blk, 4 * GW), lambda b, j: (rev(b, j), 0)), vec, vec],
        out_shape=[_sds((T, 4 * GW), BF16), _sds((1, GW), F32), _sds((1, GW), F32)],
        scratch_shapes=[pltpu.VMEM((GW, GW), F32)],
        compiler_params=_cparams(VMEM_BIG))(proj, states, dcat, lb, ng)


ATT_TQ = 256


ATT_BANDS = 8


def _attn_block(q, k, v, cum, qpos0, scale, use_cum, n_free):
    s = _mm_nt(q, k) * scale
    if use_cum:
        s = s - cum
    band = s[:, n_free:]
    visible = _iota(band.shape, 1) <= (qpos0 - n_free) + _iota(band.shape, 0)
    band = jnp.where(visible, band, -jnp.inf)
    m = jnp.max(band, axis=-1, keepdims=True)
    if n_free:
        free = s[:, :n_free]
        m = jnp.maximum(m, jnp.max(free, axis=-1, keepdims=True))
    if not use_cum:
        m = lax.stop_gradient(m)
    e = jnp.exp(band - m)
    denom = jnp.sum(e, axis=-1, keepdims=True)
    o = _mm(e, v[n_free:])
    if n_free:
        e = jnp.exp(free - m)
        denom = denom + jnp.sum(e, axis=-1, keepdims=True)
        o = o + _mm(e, v[:n_free])
    return o * (1.0 / denom)


def _bands(S, tq):
    nq = S // tq
    nb = min(ATT_BANDS, nq)
    per = nq // nb
    return [(r * per, (r + 1) * per, (r + 1) * per * tq) for r in range(nb)]


def attn_fwd(qa, qo, ka, ko, va, vo, cum, scale, S):
    T = qa.shape[0]
    B = T // S
    tq = min(ATT_TQ, S)
    nq = S // tq
    use_cum = cum is not None

    def body(*refs):
        if use_cum:
            q_ref, k_ref, v_ref, c_ref, o_ref = refs
        else:
            (q_ref, k_ref, v_ref, o_ref), c_ref = refs, None
        h, i = pl.program_id(1), pl.program_id(2)
        for lo, hi, kw in _bands(S, tq):
            @pl.when((i >= lo) & (i < hi))
            def _():
                crow = c_ref[pl.ds(h, 1), 0:kw] if use_cum else None
                o = _attn_block(q_ref[...], k_ref[0:kw, :], v_ref[0:kw, :], crow, i * tq, scale, use_cum, lo * tq)
                o_ref[...] = o.astype(BF16)

    in_specs = [pl.BlockSpec((tq, HP), lambda b, h, i: (b * nq + i, qo + h)),
                pl.BlockSpec((S, HP), lambda b, h, i: (b, ko + h)),
                pl.BlockSpec((S, HP), lambda b, h, i: (b, vo + h))]
    args = [qa, ka, va]
    if use_cum:
        in_specs.append(pl.BlockSpec((None, 8, S), lambda b, h, i: (b, 0, 0)))
        args.append(cum)
    return pl.pallas_call(
        body, name="attn_fwd", grid=(B, HEADS, nq), in_specs=in_specs,
        out_specs=pl.BlockSpec((tq, HP), lambda b, h, i: (b * nq + i, h)),
        out_shape=_sds((T, HEADS * HP), BF16),
        compiler_params=_cparams(VMEM_BIG))(*args)


def attn_bwd(qa, qo, ka, ko, va, vo, cum, dcat, do_off, scale, S, out_dtype):
    T = qa.shape[0]
    B = T // S
    tq = min(ATT_TQ, S)
    nq = S // tq
    use_cum = cum is not None

    def body(*refs):
        if use_cum:
            q_ref, k_ref, v_ref, do_ref, c_ref, dq_ref, dk_ref, dv_ref, dc_ref, dk_acc, dv_acc = refs
        else:
            q_ref, k_ref, v_ref, do_ref, dq_ref, dk_ref, dv_ref, dk_acc, dv_acc = refs
        h, i = pl.program_id(1), pl.program_id(2)

        @pl.when(i == 0)
        def _():
            dk_acc[...] = jnp.zeros_like(dk_acc)
            dv_acc[...] = jnp.zeros_like(dv_acc)
            if use_cum:
                dc_ref[...] = jnp.zeros_like(dc_ref)

        for lo, hi, kw in _bands(S, tq):
            @pl.when((i >= lo) & (i < hi))
            def _():
                crow = c_ref[pl.ds(h, 1), 0:kw] if use_cum else jnp.zeros((1, kw), F32)
                f = functools.partial(_attn_block, qpos0=i * tq, scale=scale, use_cum=use_cum, n_free=lo * tq)
                _, vjp = jax.vjp(f, q_ref[...], k_ref[0:kw, :], v_ref[0:kw, :], crow)
                dq, dk, dv, dc = vjp(do_ref[...])
                dq_ref[...] = dq.astype(out_dtype)
                dk_acc[0:kw, :] += dk
                dv_acc[0:kw, :] += dv
                if use_cum:
                    dc_ref[:, 0:kw] += dc

        @pl.when(i == nq - 1)
        def _():
            dk_ref[...] = dk_acc[...].astype(out_dtype)
            dv_ref[...] = dv_acc[...].astype(out_dtype)

    qspec = pl.BlockSpec((tq, HP), lambda b, h, i: (b * nq + i, qo + h))
    in_specs = [qspec, pl.BlockSpec((S, HP), lambda b, h, i: (b, ko + h)),
                pl.BlockSpec((S, HP), lambda b, h, i: (b, vo + h)),
                pl.BlockSpec((tq, HP), lambda b, h, i: (b * nq + i, do_off + h))]
    args = [qa, ka, va, dcat]
    kv_out = pl.BlockSpec((S, HP), lambda b, h, i: (b, h))
    out_specs = [pl.BlockSpec((tq, HP), lambda b, h, i: (b * nq + i, h)), kv_out, kv_out]
    out_shape = [_sds((T, HEADS * HP), out_dtype)] * 3
    if use_cum:
        in_specs.append(pl.BlockSpec((None, 8, S), lambda b, h, i: (b, 0, 0)))
        args.append(cum)
        out_specs.append(pl.BlockSpec((None, 1, S), lambda b, h, i: (b * HEADS + h, 0, 0)))
        out_shape.append(_sds((B * HEADS, 1, S), F32))
    return pl.pallas_call(
        body, name="attn_bwd", grid=(B, HEADS, nq), in_specs=in_specs, out_specs=out_specs, out_shape=out_shape,
        scratch_shapes=[pltpu.VMEM((S, HP), F32), pltpu.VMEM((S, HP), F32)],
        compiler_params=_cparams(VMEM_BIG))(*args)


def _tri(n, upper):
    r, c = _iota((n, n), 0), _iota((n, n), 1)
    return ((r <= c) if upper else (r >= c)).astype(F32)


def fox_gate_fwd(proj, bcol, S):
    T = proj.shape[0]
    B = T // S
    ts = _tile(S, 512)
    nt = S // ts

    def body(p_ref, b_ref, o_ref, carry):
        @pl.when(pl.program_id(1) == 0)
        def _():
            carry[...] = jnp.zeros_like(carry)
        cf = jnp.transpose(p_ref[...])[0:8, :]
        lf = jax.nn.log_sigmoid(cf + b_ref[...])
        cum = _sel_r(lf, _tri(ts, True)) + carry[...]
        o_ref[...] = cum
        carry[...] += jnp.sum(lf, axis=1, keepdims=True)

    return pl.pallas_call(
        body, name="fox_gate_fwd", grid=(B, nt),
        in_specs=[pl.BlockSpec((ts, HP), lambda b, j: (b * nt + j, C_CF // HP)), pl.BlockSpec((8, 1), lambda b, j: (0, 0))],
        out_specs=pl.BlockSpec((None, 8, ts), lambda b, j: (b, 0, j)), out_shape=_sds((B, 8, S), F32),
        scratch_shapes=[pltpu.VMEM((8, 1), F32)],
        compiler_params=_cparams(VMEM_BIG))(proj, bcol)


def fox_gate_bwd(proj, bcol, dcum, S):
    T = proj.shape[0]
    B = T // S
    ts = _tile(S, 512)
    nt = S // ts

    def body(p_ref, b_ref, dc_ref, dp_ref, db_ref, carry):
        b, j = pl.program_id(0), pl.program_id(1)

        @pl.when(j == 0)
        def _():
            carry[...] = jnp.zeros_like(carry)

        @pl.when((b == 0) & (j == 0))
        def _():
            db_ref[...] = jnp.zeros_like(db_ref)

        cf = jnp.transpose(p_ref[...])[0:8, :]
        dc = dc_ref[...]
        dlf = _sel_r(dc, _tri(ts, False)) + carry[...]
        carry[...] += jnp.sum(dc, axis=1, keepdims=True)
        dcf = dlf * jax.nn.sigmoid(-(cf + b_ref[...]))
        db_ref[...] += jnp.sum(dcf, axis=1, keepdims=True)
        full = jnp.concatenate([dcf, jnp.zeros((HP - 8, ts), F32)], axis=0)
        dp_ref[...] = jnp.transpose(full).astype(BF16)

    def rev(b, j):
        return nt - 1 - j

    return pl.pallas_call(
        body, name="fox_gate_bwd", grid=(B, nt),
        in_specs=[pl.BlockSpec((ts, HP), lambda b, j: (b * nt + rev(b, j), C_CF // HP)),
                  pl.BlockSpec((8, 1), lambda b, j: (0, 0)),
                  pl.BlockSpec((None, 8, ts), lambda b, j: (b, 0, rev(b, j)))],
        out_specs=[pl.BlockSpec((ts, HP), lambda b, j: (b * nt + rev(b, j), 0)), pl.BlockSpec((8, 1), lambda b, j: (0, 0))],
        out_shape=[_sds((T, HP), BF16), _sds((8, 1), F32)],
        scratch_shapes=[pltpu.VMEM((8, 1), F32)],
        compiler_params=_cparams(VMEM_BIG))(proj, bcol, dcum)


def _mla_pre(blk, gq, gkv, wq, wkv, place, cos_q, sin_q, cs_k):
    nq = _rms(blk[:, 0:B_Q_LORA], gq)
    nkv = _rms(blk[:, B_Q_LORA:B_Q_LORA + B_KV_LORA], gkv)
    qq = _mm(nq, wq)
    q = qq[:, 0:HEADS * HP] * cos_q + qq[:, HEADS * HP:] * sin_q
    kv = _mm(nkv, wkv)
    k = kv[:, 0:HEADS * HP] + _mm(blk[:, B_Q_LORA + B_KV_LORA:] * cs_k, place)
    return q, k, kv[:, HEADS * HP:]


def mla_pre_fwd(proj, gq, gkv, wq, wkv, place, cos_q, sin_q, cs_k, S):
    T = proj.shape[0]
    tm = _tile(S, 512)
    tpb = S // tm
    W = HEADS * HP

    def body(p_ref, gq_ref, gkv_ref, wq_ref, wkv_ref, pl_ref, cq_ref, sq_ref, ck_ref, q_ref, k_ref, v_ref):
        q, k, v = _mla_pre(p_ref[...], gq_ref[...], gkv_ref[...], wq_ref[...], wkv_ref[...], pl_ref[...],
                           cq_ref[...], sq_ref[...], ck_ref[...])
        q_ref[...] = q
        k_ref[...] = k
        v_ref[...] = v

    def full(a):
        return pl.BlockSpec(a.shape, lambda i: (0,) * a.ndim)

    tok = pl.BlockSpec((tm, W), lambda i: (i, 0))
    return pl.pallas_call(
        body, name="mla_pre_fwd", grid=(T // tm,),
        in_specs=[pl.BlockSpec((tm, W), lambda i: (i, C_B // W)), full(gq), full(gkv), full(wq), full(wkv), full(place),
                  pl.BlockSpec((tm, W), lambda i: (i % tpb, 0)), pl.BlockSpec((tm, W), lambda i: (i % tpb, 0)),
                  pl.BlockSpec((tm, HP), lambda i: (i % tpb, 0))],
        out_specs=[tok] * 3, out_shape=[_sds((T, W), F32)] * 3,
        compiler_params=_cparams(VMEM_BIG))(proj, gq, gkv, wq, wkv, place, cos_q, sin_q, cs_k)


def mla_pre_bwd(proj, gq, gkv, wq, wkv, place, cos_q, sin_q, cs_k, dq, dk, dv, S):
    T = proj.shape[0]
    tm = _tile(S, 512)
    tpb = S // tm
    W = HEADS * HP

    def body(p_ref, gq_ref, gkv_ref, wq_ref, wkv_ref, pl_ref, cq_ref, sq_ref, ck_ref, dq_ref, dk_ref, dv_ref,
             dp_ref, dgq_ref, dgkv_ref, dwq_ref, dwkv_ref):
        @pl.when(pl.program_id(0) == 0)
        def _():
            for r in (dgq_ref, dgkv_ref, dwq_ref, dwkv_ref):
                r[...] = jnp.zeros_like(r)

        f = functools.partial(_mla_pre, place=pl_ref[...], cos_q=cq_ref[...], sin_q=sq_ref[...], cs_k=ck_ref[...])
        _, vjp = jax.vjp(f, p_ref[...], gq_ref[...], gkv_ref[...], wq_ref[...], wkv_ref[...])
        dp, dgq, dgkv, dwq, dwkv = vjp((dq_ref[...], dk_ref[...], dv_ref[...]))
        dp_ref[...] = dp.astype(BF16)
        dgq_ref[...] += dgq
        dgkv_ref[...] += dgkv
        dwq_ref[...] += dwq
        dwkv_ref[...] += dwkv

    def full(a):
        return pl.BlockSpec(a.shape, lambda i: (0,) * a.ndim)

    tok = pl.BlockSpec((tm, W), lambda i: (i, 0))
    return pl.pallas_call(
        body, name="mla_pre_bwd", grid=(T // tm,),
        in_specs=[pl.BlockSpec((tm, W), lambda i: (i, C_B // W)), full(gq), full(gkv), full(wq), full(wkv), full(place),
                  pl.BlockSpec((tm, W), lambda i: (i % tpb, 0)), pl.BlockSpec((tm, W), lambda i: (i % tpb, 0)),
                  pl.BlockSpec((tm, HP), lambda i: (i % tpb, 0)), tok, tok, tok],
        out_specs=[tok, full(gq), full(gkv), full(wq), full(wkv)],
        out_shape=[_sds((T, W), BF16), _sds(gq.shape, F32), _sds(gkv.shape, F32), _sds(wq.shape, F32), _sds(wkv.shape, F32)],
        compiler_params=_cparams(VMEM_BIG))(proj, gq, gkv, wq, wkv, place, cos_q, sin_q, cs_k, dq, dk, dv)


def _gmlp_block(blk, lg, lb, ws, bs):
    u = jax.nn.gelu(blk[:, 0:GW])
    v = jax.nn.gelu(blk[:, GW:2 * GW])
    mu = jnp.mean(v, axis=-1, keepdims=True)
    var = jnp.mean(jnp.square(v - mu), axis=-1, keepdims=True)
    vn = (v - mu) * lax.rsqrt(var + LN_EPS) * lg + lb
    causal = _iota((D_CHUNK, D_CHUNK), 1) <= _iota((D_CHUNK, D_CHUNK), 0)
    group = _iota((1, GW), 1) // HD
    mixed = jnp.zeros((D_CHUNK, GW), F32)
    for g in range(HEADS):
        part = _mm(jnp.where(causal, ws[g], 0.0), vn) + bs[g]
        mixed = mixed + jnp.where(group == g, part, 0.0)
    return u * mixed


def gmlp_fwd(proj, lg, lb, ws, bs):
    T = proj.shape[0]

    def body(p_ref, lg_ref, lb_ref, ws_ref, bs_ref, o_ref):
        o_ref[...] = _gmlp_block(p_ref[...], lg_ref[...], lb_ref[...], ws_ref[...], bs_ref[...]).astype(BF16)

    def full(a):
        return pl.BlockSpec(a.shape, lambda i: (0,) * a.ndim)

    return pl.pallas_call(
        body, name="gmlp_fwd", grid=(T // D_CHUNK,),
        in_specs=[pl.BlockSpec((D_CHUNK, 2 * GW), lambda i: (i, C_D // (2 * GW))), full(lg), full(lb), full(ws), full(bs)],
        out_specs=pl.BlockSpec((D_CHUNK, GW), lambda i: (i, 0)), out_shape=_sds((T, GW), BF16),
        compiler_params=_cparams(VMEM_BIG))(proj, lg, lb, ws, bs)


def gmlp_bwd(proj, lg, lb, ws, bs, dcat):
    T = proj.shape[0]

    def body(p_ref, lg_ref, lb_ref, ws_ref, bs_ref, do_ref, dp_ref, dlg_ref, dlb_ref, dws_ref, dbs_ref):
        @pl.when(pl.program_id(0) == 0)
        def _():
            for r in (dlg_ref, dlb_ref, dws_ref, dbs_ref):
                r[...] = jnp.zeros_like(r)

        _, vjp = jax.vjp(_gmlp_block, p_ref[...], lg_ref[...], lb_ref[...], ws_ref[...], bs_ref[...])
        dp, dlg, dlb, dws, dbs = vjp(do_ref[...])
        dp_ref[...] = dp.astype(BF16)
        dlg_ref[...] += dlg
        dlb_ref[...] += dlb
        dws_ref[...] += dws
        dbs_ref[...] += dbs

    def full(a):
        return pl.BlockSpec(a.shape, lambda i: (0,) * a.ndim)

    return pl.pallas_call(
        body, name="gmlp_bwd", grid=(T // D_CHUNK,),
        in_specs=[pl.BlockSpec((D_CHUNK, 2 * GW), lambda i: (i, C_D // (2 * GW))), full(lg), full(lb), full(ws), full(bs),
                  pl.BlockSpec((D_CHUNK, GW), lambda i: (i, O_D // GW))],
        out_specs=[pl.BlockSpec((D_CHUNK, 2 * GW), lambda i: (i, 0)), full(lg), full(lb), full(ws), full(bs)],
        out_shape=[_sds((T, 2 * GW), BF16), _sds(lg.shape, F32), _sds(lb.shape, F32), _sds(ws.shape, F32), _sds(bs.shape, F32)],
        compiler_params=_cparams(VMEM_BIG))(proj, lg, lb, ws, bs, dcat)


def _lb_all(logits):
    m = jnp.max(logits, axis=0, keepdims=True)
    e = jnp.exp(logits - m)
    sm = e / jnp.sum(e, axis=0, keepdims=True)
    return jnp.concatenate([sm[0:1] - sm[0:1], (sm[0:1] + sm[1:2]) - sm[0:1]], axis=0)


def lb_fwd(logits):
    def body(l_ref, o_ref):
        o_ref[...] = _lb_all(l_ref[...])

    return pl.pallas_call(body, name="lb_fwd", out_shape=_sds(logits.shape, F32))(logits)


def lb_bwd(logits, dlb):
    def body(l_ref, d_ref, o_ref):
        _, vjp = jax.vjp(_lb_all, l_ref[...])
        o_ref[...] = vjp(d_ref[...])[0]

    return pl.pallas_call(body, name="lb_bwd", out_shape=_sds(logits.shape, F32))(logits, dlb)


def ada_grad(c_all, dmod_cols):
    N = dmod_cols.shape[1]
    tn = _tile(N, 1152)

    def body(c_ref, d_ref, o_ref):
        h = jax.nn.silu(c_ref[...]).astype(BF16)
        o_ref[...] = lax.dot_general(h, d_ref[...].astype(BF16), (((0,), (0,)), ((), ())), preferred_element_type=F32)

    nb = c_all.shape[0]
    return pl.pallas_call(
        body, name="ada_grad", grid=(N // tn,),
        in_specs=[pl.BlockSpec((nb, D), lambda j: (0, 0)), pl.BlockSpec((nb, tn), lambda j: (0, j))],
        out_specs=pl.BlockSpec((D, tn), lambda j: (0, j)), out_shape=_sds((D, N), F32),
        compiler_params=_cparams(VMEM_BIG))(c_all, dmod_cols)


def sum_slots(a, n, name):
    _, R, C = a.shape
    tr = _row_tile(R, C, n)

    def body(a_ref, o_ref):
        acc = a_ref[0]
        for k in range(1, n):
            acc = acc + a_ref[k]
        o_ref[...] = acc

    return pl.pallas_call(
        body, name=name, grid=(R // tr,),
        in_specs=[pl.BlockSpec((n, tr, C), lambda i: (0, i, 0))],
        out_specs=pl.BlockSpec((tr, C), lambda i: (i, 0)), out_shape=_sds((R, C), F32),
        compiler_params=_cparams(VMEM_BIG))(a)


def _row_tile(R, C=D, n=1, mult=8, elems=1 << 18):
    limit = max(mult, elems // (C * n))
    for t in range(limit - limit % mult, mult - 1, -mult):
        if R % t == 0:
            return t
    return R


def adamw(w, g, m, v, name, echo=False):
    R, C = w.shape
    tr = _row_tile(R, C, elems=1 << 19)
    c1 = 1.0 - ADAM_B1 ** ADAM_STEP
    c2 = 1.0 - ADAM_B2 ** ADAM_STEP
    n_out = 4 if echo else 3

    def body(w_ref, g_ref, m_ref, v_ref, d_ref, nm_ref, nv_ref, *g_out):
        g_ = g_ref[...]
        nm = ADAM_B1 * m_ref[...] + (1.0 - ADAM_B1) * g_
        nv = ADAM_B2 * v_ref[...] + (1.0 - ADAM_B2) * jnp.square(g_)
        d_ref[...] = -ADAM_LR * ((nm / c1) / (jnp.sqrt(nv / c2) + ADAM_EPS) + ADAM_WD * w_ref[...])
        nm_ref[...] = nm
        nv_ref[...] = nv
        if echo:
            g_out[0][...] = g_

    spec = pl.BlockSpec((tr, C), lambda i: (i, 0))
    return pl.pallas_call(body, name=name, grid=(R // tr,), in_specs=[spec] * 4, out_specs=[spec] * n_out,
                          out_shape=[_sds((R, C), F32)] * n_out, compiler_params=_cparams(VMEM_BIG))(w, g, m, v)


def _rot_cols(w):
    return jnp.concatenate([-w[:, 16:32], w[:, 0:16]], axis=1)


def _fold_rot(d):
    return jnp.concatenate([d[:, 16:32], -d[:, 0:16]], axis=1)


def _pad_heads(w, off, axis):
    parts = []
    for h in range(HEADS):
        piece = lax.slice_in_dim(w, off + HD * h, off + HD * (h + 1), axis=axis)
        parts += [piece, jnp.zeros_like(piece)]
    return parts


def _unpad_heads(d, off, axis):
    return [lax.slice_in_dim(d, off + HP * h, off + HP * h + HD, axis=axis) for h in range(HEADS)]


def mix_in_ext(w):
    z = lambda n: jnp.zeros((w.shape[0], n), w.dtype)
    kr = w[:, 1408:1440]
    cols = [w[:, 0:1408], kr, _rot_cols(kr), z(64)]
    cols += _pad_heads(w, 1440, 1) + _pad_heads(w, 1696, 1) + _pad_heads(w, 1952, 1)
    cols += [w[:, 2212:2724], w[:, 2208:2212], z(NP - C_CF - HEADS)]
    return jnp.concatenate(cols, axis=1)


def mix_in_unext(d):
    kr = d[:, 1408:1440] + _fold_rot(d[:, 1440:1472])
    cols = [d[:, 0:1408], kr] + _unpad_heads(d, C_CQ, 1) + _unpad_heads(d, C_CK, 1) + _unpad_heads(d, C_CV, 1)
    cols += [d[:, C_CF:C_CF + HEADS], d[:, C_D:C_D + 2 * GW]]
    return jnp.concatenate(cols, axis=1)


def mix_out_ext(w):
    return jnp.concatenate([w[0:GW]] + _pad_heads(w, GW, 0) + _pad_heads(w, 2 * GW, 0) + [w[3 * GW:4 * GW]], axis=0)


def mix_out_unext(d):
    return jnp.concatenate([d[0:GW]] + _unpad_heads(d, O_B, 0) + _unpad_heads(d, O_C, 0) + [d[O_D:O_D + GW]], axis=0)


def uq_ext(w):
    z = lambda n: jnp.zeros((w.shape[0], n), w.dtype)
    a, b = [], []
    for h in range(HEADS):
        o = (B_NOPE + B_ROPE) * h
        a += [w[:, o:o + B_NOPE + B_ROPE], z(32)]
        b += [z(B_NOPE), _rot_cols(w[:, o + B_NOPE:o + B_NOPE + B_ROPE]), z(32)]
    return jnp.concatenate(a + b, axis=1)


def uq_unext(d):
    cols = []
    for h in range(HEADS):
        o = HP * h
        cols += [d[:, o:o + B_NOPE], d[:, o + B_NOPE:o + B_NOPE + B_ROPE]
                 + _fold_rot(d[:, HEADS * HP + o + B_NOPE:HEADS * HP + o + B_NOPE + B_ROPE])]
    return jnp.concatenate(cols, axis=1)


def ukv_ext(w):
    z = jnp.zeros((w.shape[0], HD), w.dtype)
    k, v = [], []
    for h in range(HEADS):
        k += [w[:, 2 * HD * h:2 * HD * h + HD], z]
        v += [w[:, 2 * HD * h + HD:2 * HD * (h + 1)], z]
    return jnp.concatenate(k + v, axis=1)


def ukv_unext(d):
    cols = []
    for h in range(HEADS):
        cols += [d[:, HP * h:HP * h + HD], d[:, HEADS * HP + HP * h:HEADS * HP + HP * h + HD]]
    return jnp.concatenate(cols, axis=1)


def rope_tables(S):
    half = B_ROPE // 2
    inv_freq = ROPE_THETA ** (-jnp.arange(half, dtype=F32) / half)
    ang = jnp.arange(S).astype(F32)[:, None] * inv_freq[None, :]
    cos = jnp.tile(jnp.cos(ang), (1, 2))
    sin = jnp.tile(jnp.sin(ang), (1, 2))
    one, zero = jnp.ones((S, B_NOPE), F32), jnp.zeros((S, B_NOPE), F32)
    z32 = jnp.zeros((S, 32), F32)
    cos_q = jnp.tile(jnp.concatenate([one, cos, z32], axis=1), (1, HEADS))
    sin_q = jnp.tile(jnp.concatenate([zero, sin, z32], axis=1), (1, HEADS))
    cs_k = jnp.concatenate([cos, sin, zero], axis=1)
    place = np.zeros((HP, HEADS * HP), np.float32)
    for h in range(HEADS):
        for j in range(B_ROPE):
            place[j, h * HP + B_NOPE + j] = 1.0
            place[B_ROPE + j, h * HP + B_NOPE + j] = 1.0
    return cos_q, sin_q, cs_k, jnp.asarray(place, BF16)


def layer_fwd(x, mod, get, tabs, S):
    cos_q, sin_q, cs_k, place = tabs
    p = dict(get("ffn1", x))
    l = p["wl"]
    zg1, zu1, act1 = ffn_in_fwd(x, mod, p["ffn1_in"], l, 0, 1, S)
    y1, x1 = out_ln_fwd(act1, p["ffn1_out"], x, mod, p["ln_g"][0:1], p["ln_b"][0:1], 2, 0.5, S, l)
    p.update(get("mix", x1))
    proj = mix_in_fwd(x1, mod, p["mix_in"], 3, 4, S)
    o_a, states = hgrn_fwd(proj, p["lb"], p["ng"], S)
    q_b, k_b, v_b = mla_pre_fwd(proj, p["gq"], p["gkv"], p["wq"], p["wkv"], place, cos_q, sin_q, cs_k, S)
    o_b = attn_fwd(q_b, 0, k_b, 0, v_b, 0, None, (B_NOPE + B_ROPE) ** -0.5, S)
    cum = fox_gate_fwd(proj, p["bcol"], S)
    o_c = attn_fwd(proj, C_CQ // HP, proj, C_CK // HP, proj, C_CV // HP, cum, HD ** -0.5, S)
    o_d = gmlp_fwd(proj, p["g_lg"], p["g_lb"], p["ws"], p["bs"])
    cat = jnp.concatenate([o_a, o_b, o_c, o_d], axis=1)
    y2, x2 = out_ln_fwd(cat, p["mix_out"], x1, mod, p["ln_g"][1:2], p["ln_b"][1:2], 5, 1.0, S)
    p.update(get("ffn2", x2))
    zg3, zu3, act3 = ffn_in_fwd(x2, mod, p["ffn2_in"], l, 6, 7, S)
    y3, x3 = out_ln_fwd(act3, p["ffn2_out"], x2, mod, p["ln_g"][2:3], p["ln_b"][2:3], 8, 0.5, S, l)
    saved = dict(x=x, zg1=zg1, zu1=zu1, act1=act1, y1=y1, x1=x1, proj=proj, states=states, q_b=q_b, k_b=k_b, v_b=v_b,
                 cum=cum, cat=cat, y2=y2, x2=x2, zg3=zg3, zu3=zu3, act3=act3, y3=y3, p=p)
    return x3, saved


def _ffn_bwd(dxn, x_in, y, zg, zu, act, mod, w_in, w_out, l, lg, lb, idx, S, emit):
    sh, sc, gate = idx
    dres, dy, dgate, dlg, dlb = ln_res_bwd(dxn, x_in, y, mod, lg, lb, gate, 0.5, S)
    dzg, dzu = swiglu_bwd(dy, w_out, l, zg, zu, S)
    dw_out = tn_mm(act, dy, FF // 2).reshape(N_CHIPS, FF // N_CHIPS, D)
    dw_in = tn_mm_mod_shards(x_in, mod, dzg, dzu, sh, sc, S)
    mod = mod + emit(dw_in, dw_out)
    dx, dsh, dsc = nt_mod_bwd([dzg, dzu], w_in, [0, 2], x_in, mod, dres, sc, S, FF // 2, l)
    return dx, dw_in, dw_out, dlg, dlb, {sh: dsh, sc: dsc, gate: dgate}, mod


def layer_bwd(dx3, mod, sv, tabs, S, emit):
    cos_q, sin_q, cs_k, place = tabs
    p = sv["p"]
    l = p["wl"]
    g = {}
    dm = {}

    def emit_ffn(part):
        def f(dw_in, dw_out):
            g[part + "_in"], g[part + "_out"] = dw_in, dw_out
            return emit(part, g)
        return f

    dx2, _, _, dlg2, dlb2, d, mod = _ffn_bwd(
        dx3, sv["x2"], sv["y3"], sv["zg3"], sv["zu3"], sv["act3"], mod, p["ffn2_in"], p["ffn2_out"], l,
        p["ln_g"][2:3], p["ln_b"][2:3], (6, 7, 8), S, emit_ffn("ffn2"))
    dm.update(d)
    dres, dy2, dm[5], dlg1, dlb1 = ln_res_bwd(dx2, sv["x1"], sv["y2"], mod, p["ln_g"][1:2], p["ln_b"][1:2], 5, 1.0, S)
    dcat = nt_plain(dy2, p["mix_out"])
    g["mix_out"] = tn_mm(sv["cat"], dy2, NCAT // 2)
    proj = sv["proj"]
    d_a, g["lb"], g["ng"] = hgrn_bwd(proj, sv["states"], dcat, p["lb"], p["ng"], S)
    dq_c, dk_c, dv_c, dcum = attn_bwd(proj, C_CQ // HP, proj, C_CK // HP, proj, C_CV // HP, sv["cum"], dcat,
                                      O_C // HP, HD ** -0.5, S, BF16)
    B = proj.shape[0] // S
    dcum = jnp.concatenate([dcum.reshape(B, HEADS, S), jnp.zeros((B, 8 - HEADS, S), F32)], axis=1)
    d_cf, g["bcol"] = fox_gate_bwd(proj, p["bcol"], dcum, S)
    dq_b, dk_b, dv_b = attn_bwd(sv["q_b"], 0, sv["k_b"], 0, sv["v_b"], 0, None, dcat, O_B // HP,
                                (B_NOPE + B_ROPE) ** -0.5, S, F32)
    d_b, g["gq"], g["gkv"], g["wq"], g["wkv"] = mla_pre_bwd(
        proj, p["gq"], p["gkv"], p["wq"], p["wkv"], place, cos_q, sin_q, cs_k, dq_b, dk_b, dv_b, S)
    d_d, g["g_lg"], g["g_lb"], g["ws"], g["bs"] = gmlp_bwd(proj, p["g_lg"], p["g_lb"], p["ws"], p["bs"], dcat)
    dproj = jnp.concatenate([d_a, d_b, dq_c, dk_c, dv_c, d_d, d_cf, jnp.zeros_like(d_cf)], axis=1)
    g["mix_in"] = tn_mm_mod(sv["x1"], mod, dproj, 3, 4, S, NP_TILE)
    mod = mod + emit("mix", g)
    dx1, dm[3], dm[4] = nt_mod_bwd([dproj], p["mix_in"], [0], sv["x1"], mod, dres, 4, S, NP_TILE)
    last = []

    def emit_last(dw_in, dw_out):
        last.append(emit_ffn("ffn1")(dw_in, dw_out))
        return last[0]

    dx0, _, _, dlg0, dlb0, d, mod = _ffn_bwd(
        dx1, sv["x"], sv["y1"], sv["zg1"], sv["zu1"], sv["act1"], mod, p["ffn1_in"], p["ffn1_out"], l,
        p["ln_g"][0:1], p["ln_b"][0:1], (0, 1, 2), S, emit_last)
    dm.update(d)
    g["ln_g"] = jnp.concatenate([dlg0, dlg1, dlg2], axis=0)
    g["ln_b"] = jnp.concatenate([dlb0, dlb1, dlb2], axis=0)
    dmod = jnp.concatenate([dm[i] for i in range(N_MOD)], axis=1)
    return dx0, dmod, g, last[0]


def local_step(x, c8, tgt, get, lb_logits, S, emit=None):
    B = x.shape[0] // S
    tabs = rope_tables(S)
    lb_all = lb_fwd(lb_logits)
    mods, saved = [], []
    h = x
    for l in range(DEPTH):
        pa = get(l, "ada", h)
        mod = mod_fwd(c8, pa["ada_w"], pa["wl"], pa["ada_b"])[0:B].reshape(B, N_MOD, D)

        def get_l(part, after, l=l):
            p = dict(get(l, part, after))
            if part == "mix":
                p["lb"] = lb_all[l:l + 1]
            return p

        h, sv = layer_fwd(h, mod, get_l, tabs, S)
        mods.append(mod)
        saved.append(sv)
    loss_tile, dh = loss_head(h, tgt)
    grads, dmods, dlb = [None] * DEPTH, [None] * DEPTH, [None] * DEPTH
    tie = jnp.zeros((), F32)
    for l in reversed(range(DEPTH)):
        emit_l = (lambda part, g: jnp.zeros((), F32)) if emit is None else functools.partial(emit, l)
        dh, dmods[l], grads[l], tie = layer_bwd(dh, mods[l] + tie, saved[l], tabs, S, emit_l)
        dlb[l] = grads[l].pop("lb")
    d_logits = lb_bwd(lb_logits, jnp.concatenate(dlb, axis=0))
    return loss_tile, dh, dmods, grads, d_logits


ANY = pl.BlockSpec(memory_space=pl.ANY)


def _place():
    x, y, c = lax.axis_index("x"), lax.axis_index("y"), lax.axis_index("c")
    chips = [(1 - x, y), (x, 1 - y), (1 - x, 1 - y)]
    return x, y, c, chips


def _rcopy(src, dst, sems, k, to):
    send_sems, recv_sems = sems
    return pltpu.make_async_remote_copy(src_ref=src, dst_ref=dst, send_sem=send_sems.at[k], recv_sem=recv_sems.at[k],
                                        device_id=to, device_id_type=MESH)


def _dma_sems(n_remote, n_local):
    return [pltpu.SemaphoreType.DMA((n_remote,)), pltpu.SemaphoreType.DMA((n_remote,)), pltpu.SemaphoreType.DMA((n_local,))]


def own_slot(src, chip):
    L = src.shape[0]
    return lax.dynamic_update_slice(lax.empty((L, N_CHIPS) + src.shape[1:], src.dtype), src[:, None], (0, chip, 0, 0))


HBM_SPEC = pl.BlockSpec(memory_space=pltpu.HBM)
SEM_SPEC = pl.BlockSpec(memory_space=pltpu.SEMAPHORE)
DATAFLOW = pltpu.SideEffectType.DATAFLOW_SIDE_EFFECTING


def _split_start(srcs, lands, copies, n_copies, dep, name):
    n, m = len(srcs), len(lands)

    def body(*refs):
        ins = refs[:n + m]
        send_sems, recv_sems = refs[n + m + 1], refs[n + m + 2]
        token = refs[-1]
        for k, (src, dst, to) in enumerate(copies(ins[:n], ins[n:], _place())):
            pltpu.make_async_remote_copy(src_ref=src, dst_ref=dst, send_sem=send_sems.at[k], recv_sem=recv_sems.at[k],
                                         device_id=to, device_id_type=MESH).start()
        token[...] = jnp.zeros_like(token)

    arrs = list(srcs) + list(lands)
    outs = pl.pallas_call(
        body, name=name,
        out_shape=(pltpu.SemaphoreType.DMA((n_copies,)), pltpu.SemaphoreType.DMA((n_copies,)),
                   *[pltpu.HBM(a.shape, a.dtype) for a in arrs], _sds((8, 128), F32)),
        in_specs=[HBM_SPEC] * (n + m) + [ANY],
        out_specs=(SEM_SPEC, SEM_SPEC, *[HBM_SPEC] * (n + m), pl.BlockSpec(memory_space=pltpu.VMEM)),
        input_output_aliases={i: 2 + i for i in range(n + m)},
        compiler_params=pltpu.CompilerParams(has_side_effects=DATAFLOW),
    )(*[pltpu.with_memory_space_constraint(a, pltpu.HBM) for a in arrs], dep)
    return outs[0], outs[1], list(outs[2:2 + n]), list(outs[2 + n:2 + n + m]), outs[-1]


def _split_wait(handle, arrivals, after, name):
    send_sems, recv_sems, srcs, lands, _ = handle
    n, m = len(srcs), len(lands)

    def body(*refs):
        ins = refs[:n + m]
        send_sems, recv_sems = refs[n + m], refs[n + m + 1]
        x, y, c, chips = place = _place()
        for k, (src, dst) in enumerate(arrivals(ins[:n], ins[n:], place)):
            cp = pltpu.make_async_remote_copy(src_ref=src, dst_ref=dst, send_sem=send_sems.at[k], recv_sem=recv_sems.at[k],
                                              device_id=(x, y, 1 - c), device_id_type=MESH)
            cp.wait_send()
            cp.wait_recv()

    arrs = list(srcs) + list(lands)
    outs = pl.pallas_call(
        body, name=name, out_shape=[pltpu.HBM(a.shape, a.dtype) for a in arrs],
        in_specs=[HBM_SPEC] * (n + m) + [SEM_SPEC, SEM_SPEC, ANY], out_specs=[HBM_SPEC] * (n + m),
        input_output_aliases={i: i for i in range(n + m)},
        compiler_params=pltpu.CompilerParams(has_side_effects=DATAFLOW),
    )(*arrs, send_sems, recv_sems, after)
    return list(outs[:n]), list(outs[n:])


def _ag_part(ref, k, hc):
    rh = ref.shape[2] // 2
    return ref.at[:, k, pl.ds(hc * rh, rh), :]


def ag_start(srcs, lands, dep, name):
    def copies(s, d, place):
        x, y, c, chips = place
        out = []
        for j, (px, py) in enumerate(chips):
            for i in range(len(s)):
                rh = s[i].shape[1] // 2
                out.append((s[i].at[:, pl.ds(c * rh, rh), :], _ag_part(d[i], 2 * x + y, c), (px, py, c)))
        return out

    return _split_start(srcs, lands, copies, 3 * len(srcs), dep, name)


def ag_wait(handle, after, name):
    def arrivals(s, d, place):
        x, y, c, chips = place
        out = []
        for j, (px, py) in enumerate(chips):
            for i in range(len(s)):
                rh = s[i].shape[1] // 2
                out.append((s[i].at[:, pl.ds(c * rh, rh), :], _ag_part(d[i], 2 * px + py, c)))
        return out

    return _split_wait(handle, arrivals, after, name)


def ag_forward(lands, name):
    n = len(lands)

    def body(*refs):
        bufs, token = refs[n:2 * n], refs[2 * n]
        send_sems, recv_sems = refs[2 * n + 1:]
        x, y, c, chips = _place()
        sems = (send_sems, recv_sems)
        token[...] = jnp.zeros_like(token)
        cps = []
        for j, (px, py) in enumerate(chips):
            for i in range(n):
                part = _ag_part(bufs[i], 2 * px + py, c)
                cps.append(_rcopy(part, part, sems, 3 * i + j, (x, y, 1 - c)))
        for cp in cps:
            cp.start()
        for j, (px, py) in enumerate(chips):
            for i in range(n):
                part = _ag_part(bufs[i], 2 * px + py, 1 - c)
                _rcopy(part, part, sems, 3 * i + j, (x, y, 1 - c)).wait_recv()
        for cp in cps:
            cp.wait_send()

    outs = pl.pallas_call(
        body, name=name, out_shape=[_sds(a.shape, a.dtype) for a in lands] + [_sds((8, 128), F32)],
        in_specs=[ANY] * n, out_specs=[ANY] * n + [pl.BlockSpec(memory_space=pltpu.VMEM)],
        input_output_aliases={i: i for i in range(n)}, scratch_shapes=_dma_sems(3 * n, 1)[:2])(*lands)
    return list(outs[:n]), outs[n]


def rs_start(hs, lands, dep, name):
    def copies(s, d, place):
        x, y, c, chips = place
        return [(s[i].at[2 * px + py], d[i].at[2 * x + y], (px, py, c)) for j, (px, py) in enumerate(chips) for i in range(len(s))]

    return _split_start(hs, lands, copies, 3 * len(hs), dep, name)


def rs_wait(handle, after, name):
    def arrivals(s, d, place):
        x, y, c, chips = place
        return [(s[i].at[2 * px + py], d[i].at[2 * px + py]) for j, (px, py) in enumerate(chips) for i in range(len(s))]

    return _split_wait(handle, arrivals, after, name)


def sibling_swap(arrs, name):
    n = len(arrs)
    rh = [a.shape[1] // 2 for a in arrs]

    def body(*refs):
        srcs, outs = refs[:n], refs[n:2 * n]
        send_sems, recv_sems = refs[2 * n:]
        x, y, c, _ = _place()
        cps = [_rcopy(srcs[i].at[:, pl.ds((1 - c) * rh[i], rh[i]), :], outs[i], (send_sems, recv_sems), i, (x, y, 1 - c))
               for i in range(n)]
        for cp in cps:
            cp.start()
        for cp in cps:
            cp.wait()

    return pl.pallas_call(
        body, name=name, out_shape=[_sds((N_CHIPS, r, a.shape[2]), a.dtype) for a, r in zip(arrs, rh)],
        in_specs=[ANY] * n, out_specs=[ANY] * n, scratch_shapes=_dma_sems(n, 1)[:2])(*arrs)


def sum_into(land, base, l, core, name):
    _, rh, C = land.shape
    tr = _row_tile(rh, C, N_CHIPS, mult=16)
    nr = rh // tr

    def body(core_ref, land_ref, base_ref, o_ref):
        acc = land_ref[0].astype(F32)
        for k in range(1, N_CHIPS):
            acc = acc + land_ref[k].astype(F32)
        o_ref[...] = acc

    grid_spec = pltpu.PrefetchScalarGridSpec(
        num_scalar_prefetch=1, grid=(nr,),
        in_specs=[pl.BlockSpec((N_CHIPS, tr, C), lambda r, core_ref: (0, r, 0)), ANY],
        out_specs=pl.BlockSpec((None, tr, C), lambda r, core_ref: (l, core_ref[0] * nr + r, 0)))
    return pl.pallas_call(body, name=name, grid_spec=grid_spec, out_shape=_sds(base.shape, base.dtype),
                          input_output_aliases={2: 0}, compiler_params=_cparams(VMEM_BIG))(
        core.reshape(1).astype(jnp.int32), land, base)


def sibling_join(bases, name):
    n = len(bases)

    def body(*refs):
        bufs = refs[n:2 * n]
        send_sems, recv_sems = refs[2 * n:]
        x, y, c, _ = _place()
        sems = (send_sems, recv_sems)

        def half(i, hc):
            rh = bufs[i].shape[1] // 2
            return bufs[i].at[:, pl.ds(hc * rh, rh), :]

        sends = [_rcopy(half(i, c), half(i, c), sems, i, (x, y, 1 - c)) for i in range(n)]
        for cp in sends:
            cp.start()
        for i in range(n):
            _rcopy(half(i, 1 - c), half(i, 1 - c), sems, i, (x, y, 1 - c)).wait_recv()
        for cp in sends:
            cp.wait_send()

    return pl.pallas_call(
        body, name=name, out_shape=[_sds(b.shape, b.dtype) for b in bases], in_specs=[ANY] * n, out_specs=[ANY] * n,
        input_output_aliases={i: i for i in range(n)}, scratch_shapes=_dma_sems(n, 1)[:2])(*bases)


def ag_all(blk):
    M, C = blk.shape

    def body(x_ref, out_ref, send_sems, recv_sems, loc_sem):
        x, y, c, chips = _place()
        sems = (send_sems, recv_sems)
        me, sibling = (x, y, c), (x, y, 1 - c)

        def slot(px, py, pc):
            return out_ref.at[4 * px + 2 * py + pc]

        mine = pltpu.make_async_copy(x_ref, slot(*me), loc_sem)
        mine.start()
        first = [_rcopy(x_ref, slot(*me), sems, 0, sibling)]
        first += [_rcopy(x_ref, slot(*me), sems, 1 + j, (*chip, c)) for j, chip in enumerate(chips)]
        for cp in first:
            cp.start()
        passed = [_rcopy(slot(*chip, c), slot(*chip, c), sems, 4 + j, sibling) for j, chip in enumerate(chips)]
        for j, chip in enumerate(chips):
            _rcopy(slot(*chip, c), slot(*chip, c), sems, 1 + j, me).wait_recv()
            passed[j].start()
        _rcopy(slot(*sibling), slot(*sibling), sems, 0, me).wait_recv()
        for j, chip in enumerate(chips):
            _rcopy(slot(*chip, 1 - c), slot(*chip, 1 - c), sems, 4 + j, me).wait_recv()
        for cp in first + passed:
            cp.wait_send()
        mine.wait()

    return pl.pallas_call(
        body, name="ag_all", out_shape=_sds((8, M, C), blk.dtype),
        in_specs=[pl.BlockSpec(memory_space=pltpu.VMEM)], out_specs=pl.BlockSpec(memory_space=pltpu.VMEM),
        scratch_shapes=[pltpu.SemaphoreType.DMA((7,)), pltpu.SemaphoreType.DMA((7,)), pltpu.SemaphoreType.DMA(())],
        compiler_params=_cparams(VMEM_BIG))(blk)


WEIGHTS = ["ada_w", "ada_b", "ln_g", "ln_b", "ffn1_w_in", "ffn1_w_out", "ffn2_w_in", "ffn2_w_out", "mix_w_in", "mix_w_out",
           "hgrn_lb_logits", "hgrn_norm_g", "mla_q_norm_g", "mla_kv_norm_g", "mla_w_uq", "mla_w_ukv", "fox_b_f",
           "gmlp_ln_g", "gmlp_ln_b", "gmlp_w_s", "gmlp_b_s"]
SMALL = ["hgrn_lb_logits", "hgrn_norm_g", "mla_q_norm_g", "mla_kv_norm_g", "fox_b_f", "gmlp_ln_g", "gmlp_ln_b",
         "gmlp_w_s", "gmlp_b_s", "ln_g", "ln_b"]
GATHERED = ["ada_w", "ffn1_w_in", "ffn1_w_out", "ffn2_w_in", "ffn2_w_out", "mix_w_in", "mix_w_out", "mla_w_uq", "mla_w_ukv"]
REDUCED = GATHERED[1:]


def _col_shards(a):
    cols = a.shape[1] // N_CHIPS
    return jnp.stack([a[:, k * cols:(k + 1) * cols] for k in range(N_CHIPS)])


def add_kept_half(a, got, core, name):
    _, R, C = a.shape
    rh = R // 2
    tr = _row_tile(rh, C, mult=16)
    nr = rh // tr

    def body(core_ref, a_ref, b_ref, o_ref):
        o_ref[...] = (a_ref[...].astype(F32) + b_ref[...].astype(F32)).astype(o_ref.dtype)

    half = pl.BlockSpec((None, tr, C), lambda k, r, core_ref: (k, r, 0))
    grid_spec = pltpu.PrefetchScalarGridSpec(
        num_scalar_prefetch=1, grid=(N_CHIPS, nr),
        in_specs=[pl.BlockSpec((None, tr, C), lambda k, r, core_ref: (k, core_ref[0] * nr + r, 0)), half],
        out_specs=half)
    return pl.pallas_call(body, name=name, grid_spec=grid_spec, out_shape=_sds((N_CHIPS, rh, C), BF16),
                          compiler_params=_cparams(VMEM_BIG))(core.reshape(1).astype(jnp.int32), a, got)


def _rows(parts, n_rows, dtype):
    flat = jnp.concatenate([p.reshape(-1) for p in parts])
    pad = n_rows * D - flat.shape[0]
    return jnp.concatenate([flat, jnp.zeros((pad,), dtype)]).reshape(n_rows, D)


def _take(flat, shapes):
    out, o = [], 0
    for shp in shapes:
        n = int(np.prod(shp))
        out.append(flat[o:o + n].reshape(shp))
        o += n
    return out


def _round_up(n, m):
    return -(-n // m) * m


def pack_small(w):
    parts = [w[n][l] for l in range(DEPTH) for n in SMALL]
    n = sum(int(np.prod(p.shape)) for p in parts)
    return _rows(parts, _round_up(-(-n // D), 8), F32)


def unpack_small(pk, like):
    shapes = [like[n].shape[1:] for l in range(DEPTH) for n in SMALL]
    pieces = _take(pk.reshape(-1), shapes)
    names = [n for l in range(DEPTH) for n in SMALL]
    return {n: jnp.stack([p for p, m in zip(pieces, names) if m == n]) for n in SMALL}


def kernel(x, c, ada_w, ada_b, ln_g, ln_b, ffn1_w_in, ffn1_w_out, ffn2_w_in, ffn2_w_out, mix_w_in, mix_w_out, hgrn_lb_logits, hgrn_norm_g, mla_q_norm_g, mla_kv_norm_g, mla_w_uq, mla_w_ukv, fox_b_f, gmlp_ln_g, gmlp_ln_b, gmlp_w_s, gmlp_b_s, loss_target, m_ada_w, m_ada_b, m_ln_g, m_ln_b, m_ffn1_w_in, m_ffn1_w_out, m_ffn2_w_in, m_ffn2_w_out, m_mix_w_in, m_mix_w_out, m_hgrn_lb_logits, m_hgrn_norm_g, m_mla_q_norm_g, m_mla_kv_norm_g, m_mla_w_uq, m_mla_w_ukv, m_fox_b_f, m_gmlp_ln_g, m_gmlp_ln_b, m_gmlp_w_s, m_gmlp_b_s, v_ada_w, v_ada_b, v_ln_g, v_ln_b, v_ffn1_w_in, v_ffn1_w_out, v_ffn2_w_in, v_ffn2_w_out, v_mix_w_in, v_mix_w_out, v_hgrn_lb_logits, v_hgrn_norm_g, v_mla_q_norm_g, v_mla_kv_norm_g, v_mla_w_uq, v_mla_w_ukv, v_fox_b_f, v_gmlp_ln_g, v_gmlp_ln_b, v_gmlp_w_s, v_gmlp_b_s):
    w = dict(zip(WEIGHTS, (ada_w, ada_b, ln_g, ln_b, ffn1_w_in, ffn1_w_out, ffn2_w_in, ffn2_w_out, mix_w_in, mix_w_out, hgrn_lb_logits, hgrn_norm_g, mla_q_norm_g, mla_kv_norm_g, mla_w_uq, mla_w_ukv, fox_b_f, gmlp_ln_g, gmlp_ln_b, gmlp_w_s, gmlp_b_s)))
    m = dict(zip(WEIGHTS, (m_ada_w, m_ada_b, m_ln_g, m_ln_b, m_ffn1_w_in, m_ffn1_w_out, m_ffn2_w_in, m_ffn2_w_out, m_mix_w_in, m_mix_w_out, m_hgrn_lb_logits, m_hgrn_norm_g, m_mla_q_norm_g, m_mla_kv_norm_g, m_mla_w_uq, m_mla_w_ukv, m_fox_b_f, m_gmlp_ln_g, m_gmlp_ln_b, m_gmlp_w_s, m_gmlp_b_s)))
    v = dict(zip(WEIGHTS, (v_ada_w, v_ada_b, v_ln_g, v_ln_b, v_ffn1_w_in, v_ffn1_w_out, v_ffn2_w_in, v_ffn2_w_out, v_mix_w_in, v_mix_w_out, v_hgrn_lb_logits, v_hgrn_norm_g, v_mla_q_norm_g, v_mla_kv_norm_g, v_mla_w_uq, v_mla_w_ukv, v_fox_b_f, v_gmlp_ln_g, v_gmlp_ln_b, v_gmlp_w_s, v_gmlp_b_s)))
    Bl, S, _ = x.shape
    T = Bl * S
    core = lax.axis_index("c")
    chip = 2 * lax.axis_index("x") + lax.axis_index("y")

    def shard(key):
        n, l = key
        if n == "ln":
            return jnp.concatenate([ln_g[l:l + 1], ln_b[l:l + 1], jnp.zeros((1, 2, D // N_CHIPS), F32)], axis=1)
        return w[n][l:l + 1].astype(BF16)

    mixers = ["mix_w_in", "mix_w_out", "mla_w_uq", "mla_w_ukv"]
    groups = [[("ada_w", 0), ("ffn1_w_in", 0), ("ffn1_w_out", 0), ("ln", 0)],
              [(n, 0) for n in mixers + ["ffn2_w_in", "ffn2_w_out"]],
              [(n, 1) for n in GATHERED + ["ln"]]]
    srcs = [[shard(k) for k in grp] for grp in groups]
    lands = [[own_slot(s, chip) for s in srcs[0]]]
    handle0 = ag_start(srcs[0], lands[0], jnp.zeros((8, 128), F32), "ag_start_0")
    chip_later = chip + handle0[-1][0, 0].astype(jnp.int32)
    lands += [[own_slot(s, chip_later) for s in grp] for grp in srcs[1:]]
    first, token = ag_forward(ag_wait(handle0, lands[2][0], "ag_wait_0")[1], "ag_forward_0")
    have = dict(zip(groups[0], first))
    handles = {}
    for gi in (1, 2):
        handles[gi] = ag_start(srcs[gi], lands[gi], token, "ag_start_%d" % gi)
        token = handles[gi][-1]
    c8 = jnp.concatenate([c, jnp.zeros((8 - Bl, D), F32)], axis=0)
    c8 = c8 + token[0, 0]

    def cat_cols(a):
        return jnp.concatenate([a[0, k] for k in range(N_CHIPS)], axis=1)

    def get(l, part, after):
        gi = 2 if l == 1 else (0 if part in ("ada", "ffn1") else 1)
        if gi in handles:
            arrived, _ = ag_forward(ag_wait(handles.pop(gi), after, "ag_wait_%d" % gi)[1], "ag_forward_%d" % gi)
            have.update(zip(groups[gi], arrived))
        if part == "ada":
            return dict(ada_w=have[("ada_w", l)], wl=0, ada_b=ada_b[l][None])
        if part == "ffn1":
            ln_full = jnp.moveaxis(have[("ln", l)][0], 0, 1).reshape(8, D)
            return dict(ffn1_in=have[("ffn1_w_in", l)], ffn1_out=have[("ffn1_w_out", l)], wl=0,
                        ln_g=ln_full[0:3], ln_b=ln_full[3:6])
        if part == "ffn2":
            return dict(ffn2_in=have[("ffn2_w_in", l)], ffn2_out=have[("ffn2_w_out", l)])
        return dict(
            mix_in=mix_in_ext(cat_cols(have[("mix_w_in", l)])), mix_out=mix_out_ext(have[("mix_w_out", l)].reshape(D, D)),
            wq=uq_ext(cat_cols(have[("mla_w_uq", l)])).astype(F32), wkv=ukv_ext(cat_cols(have[("mla_w_ukv", l)])).astype(F32),
            ng=hgrn_norm_g[l][None], gq=mla_q_norm_g[l][None], gkv=mla_kv_norm_g[l][None],
            bcol=jnp.concatenate([fox_b_f[l], jnp.zeros((8 - HEADS,), F32)])[:, None],
            g_lg=gmlp_ln_g[l][None], g_lb=gmlp_ln_b[l][None], ws=gmlp_w_s[l], bs=gmlp_b_s[l][:, :, None])

    pending = []

    def emit(l, part, g):
        if part == "mix":
            names = mixers
            by_chip = [_col_shards(mix_in_unext(g["mix_in"])), mix_out_unext(g["mix_out"]).reshape(N_CHIPS, D // N_CHIPS, D),
                       _col_shards(uq_unext(g["wq"])).astype(BF16), _col_shards(ukv_unext(g["wkv"])).astype(BF16)]
        else:
            names = [part + "_w_in", part + "_w_out"]
            by_chip = [g[part + "_in"], g[part + "_out"]]
        tag = "%d_%s" % (l, part)
        got = sibling_swap(by_chip, "sibling_swap_" + tag)
        chip_sum = [add_kept_half(a, r, core, "add_sibling") for a, r in zip(by_chip, got)]
        zones = [lax.dynamic_update_slice(lax.empty(h.shape, h.dtype), lax.dynamic_slice_in_dim(h, chip, 1, axis=0), (chip, 0, 0))
                 for h in chip_sum]
        handle = rs_start(chip_sum, zones, chip_sum[0], "rs_start_" + tag)
        pending.append((l, names, handle, tag))
        return handle[-1][0, 0]

    loss_tile, dx, dmods, grads, d_logits = local_step(
        x.reshape(T, D), c8, loss_target.reshape(T, D), get, hgrn_lb_logits, S, emit)
    loss = lax.psum(loss_tile[0, 0], ("x", "y", "c"))

    small_g = {"hgrn_lb_logits": d_logits,
               "hgrn_norm_g": jnp.stack([grads[l]["ng"][0] for l in range(DEPTH)]),
               "mla_q_norm_g": jnp.stack([grads[l]["gq"][0] for l in range(DEPTH)]),
               "mla_kv_norm_g": jnp.stack([grads[l]["gkv"][0] for l in range(DEPTH)]),
               "fox_b_f": jnp.stack([grads[l]["bcol"][0:HEADS, 0] for l in range(DEPTH)]),
               "gmlp_ln_g": jnp.stack([grads[l]["g_lg"][0] for l in range(DEPTH)]),
               "gmlp_ln_b": jnp.stack([grads[l]["g_lb"][0] for l in range(DEPTH)]),
               "gmlp_w_s": jnp.stack([grads[l]["ws"] for l in range(DEPTH)]),
               "gmlp_b_s": jnp.stack([grads[l]["bs"][:, :, 0] for l in range(DEPTH)])}
    small_g["ln_g"] = jnp.stack([grads[l]["ln_g"] for l in range(DEPTH)])
    small_g["ln_b"] = jnp.stack([grads[l]["ln_b"] for l in range(DEPTH)])
    pk_small = pack_small(small_g)
    n_small = pk_small.shape[0]
    extras = [dmods[l] for l in range(DEPTH)] + [c]
    n_extra = _round_up(-(-sum(int(np.prod(e.shape)) for e in extras) // D), 8)
    gathered = ag_all(jnp.concatenate([pk_small, _rows(extras, n_extra, F32)], axis=0))
    g_small = unpack_small(sum_slots(gathered[:, 0:n_small], 8, "sum_small"), small_g)
    ext = gathered[:, n_small:].reshape(8, -1)
    n_dmod = DEPTH * Bl * N_MOD * D
    dmod_all = ext[:, 0:n_dmod].reshape(8, DEPTH, Bl, N_MOD * D)
    c_all = ext[:, n_dmod:n_dmod + Bl * D].reshape(8 * Bl, D)
    g_ada_w, g_ada_b = [], []
    ncol = N_MOD * D // N_CHIPS
    for l in range(DEPTH):
        dm = dmod_all[:, l].reshape(8 * Bl, N_MOD * D)
        g_ada_w.append(ada_grad(c_all, lax.dynamic_slice_in_dim(dm, chip * ncol, ncol, axis=1)))
        g_ada_b.append(sum_slots(dm.reshape(8 * Bl, N_MOD, D), 8 * Bl, "sum_ada_b").reshape(N_MOD * D))
    g_ada_w, g_ada_b = jnp.stack(g_ada_w), jnp.stack(g_ada_b)

    red = {n: lax.empty(w[n].shape, F32) for n in REDUCED}

    def arrive(entry, after):
        l, names, handle, tag = entry
        for n, land in zip(names, rs_wait(handle, after, "rs_wait_" + tag)[1]):
            red[n] = sum_into(land, red[n], l, core, "sum_chips")

    for entry in pending[:-1]:
        arrive(entry, dx)
    late = pending[-1][1]
    early = [n for n in REDUCED if n not in late]
    grad = dict(zip(early, sibling_join([red[n] for n in early], "sibling_join_a")))
    grad.update(g_small)
    grad["ada_w"], grad["ada_b"] = g_ada_w, g_ada_b
    for n in ("ln_g", "ln_b"):
        grad[n] = lax.dynamic_slice_in_dim(g_small[n], chip * (D // N_CHIPS), D // N_CHIPS, axis=2)
    out = {"grad": grad, "delta": {}, "new_m": {}, "new_v": {}}

    def update(n):
        shp = w[n].shape
        two_d = (-1, shp[-1])
        res = adamw(w[n].reshape(two_d), grad[n].reshape(two_d), m[n].reshape(two_d), v[n].reshape(two_d), "adamw_" + n,
                    echo=n in REDUCED)
        grad[n] = (res[3] if n in REDUCED else grad[n]).reshape(shp)
        for key, r in zip(("delta", "new_m", "new_v"), res):
            out[key][n] = r.reshape(shp)

    for n in WEIGHTS:
        if n not in late:
            update(n)
    arrive(pending[-1], out["delta"]["ffn2_w_in"])
    grad.update(zip(late, sibling_join([red[n] for n in late], "sibling_join_b")))
    for n in late:
        update(n)
    outs = [loss, dx.reshape(Bl, S, D)]
    for key in ("grad", "delta", "new_m", "new_v"):
        outs += [out[key][n] for n in WEIGHTS]
    return tuple(outs)
```

```python
import functools

import jax
import jax.numpy as jnp
import numpy as np
from jax import lax
from jax.experimental import pallas as pl
from jax.experimental.pallas import tpu as pltpu

F32, BF16 = jnp.float32, jnp.bfloat16
MESH = pl.DeviceIdType.MESH

N_CHIPS = 4
D = 1024
DEPTH = 2
FF = 2816
N_MOD = 9
GW = 256
HEADS = 4
HD = 64
HP = 128
A_CHUNK = 16
LB_FLOOR = 1e-30
B_Q_LORA, B_KV_LORA, B_NOPE, B_ROPE = 256, 128, 64, 32
ROPE_THETA = 10000.0
D_CHUNK = 128
ALPHA = (2 * DEPTH) ** 0.25
LN_EPS = 1e-5
RMS_EPS = 1e-6
ADAM_LR, ADAM_B1, ADAM_B2, ADAM_EPS, ADAM_WD, ADAM_STEP = 0.001, 0.9, 0.999, 1e-08, 0.01, 10

NP = 3840
NP_TILE = 1920
C_A, C_B, C_CQ, C_CK, C_CV, C_D, C_CF = 0, 1024, 1536, 2048, 2560, 3072, 3584
NCAT = 1536
O_A, O_B, O_C, O_D = 0, 256, 768, 1280

VMEM_BIG = 48 << 20


def _cparams(vmem=None):
    return pltpu.CompilerParams(vmem_limit_bytes=vmem) if vmem else pltpu.CompilerParams()


def _sds(shape, dtype):
    return jax.ShapeDtypeStruct(tuple(shape), dtype)


@jax.custom_vjp
def _mm(a, w):
    return jnp.dot(a.astype(BF16), w.astype(BF16), preferred_element_type=F32)


def _mm_f(a, w):
    return _mm(a, w), (a, w)


def _mm_b(res, g):
    a, w = res
    gb = g.astype(BF16)
    da = lax.dot_general(gb, w.astype(BF16), (((1,), (1,)), ((), ())), preferred_element_type=F32)
    dw = lax.dot_general(a.astype(BF16), gb, (((0,), (0,)), ((), ())), preferred_element_type=F32)
    return da.astype(a.dtype), dw.astype(w.dtype)


_mm.defvjp(_mm_f, _mm_b)


@jax.custom_vjp
def _mm_nt(a, b):
    return lax.dot_general(a.astype(BF16), b.astype(BF16), (((1,), (1,)), ((), ())), preferred_element_type=F32)


def _mm_nt_f(a, b):
    return _mm_nt(a, b), (a, b)


def _mm_nt_b(res, g):
    a, b = res
    gb = g.astype(BF16)
    da = jnp.dot(gb, b.astype(BF16), preferred_element_type=F32)
    db = lax.dot_general(gb, a.astype(BF16), (((0,), (0,)), ((), ())), preferred_element_type=F32)
    return da.astype(a.dtype), db.astype(b.dtype)


_mm_nt.defvjp(_mm_nt_f, _mm_nt_b)


@jax.custom_vjp
def _mm_tn(a, b):
    return lax.dot_general(a.astype(BF16), b.astype(BF16), (((0,), (0,)), ((), ())), preferred_element_type=F32)


def _mm_tn_f(a, b):
    return _mm_tn(a, b), (a, b)


def _mm_tn_b(res, g):
    a, b = res
    gb = g.astype(BF16)
    da = lax.dot_general(b.astype(BF16), gb, (((1,), (1,)), ((), ())), preferred_element_type=F32)
    db = jnp.dot(a.astype(BF16), gb, preferred_element_type=F32)
    return da.astype(a.dtype), db.astype(b.dtype)


_mm_tn.defvjp(_mm_tn_f, _mm_tn_b)


def _split3(x):
    p1 = x.astype(BF16)
    r = x - p1.astype(F32)
    p2 = r.astype(BF16)
    return p1, p2, (r - p2.astype(F32)).astype(BF16)


@jax.custom_vjp
def _sel_r(x, sel):
    s = sel.astype(BF16)
    return sum(jnp.dot(p, s, preferred_element_type=F32) for p in _split3(x))


def _sel_r_f(x, sel):
    return _sel_r(x, sel), sel


def _sel_r_b(sel, g):
    s = sel.astype(BF16)
    dx = sum(lax.dot_general(p, s, (((1,), (1,)), ((), ())), preferred_element_type=F32) for p in _split3(g))
    return dx, jnp.zeros_like(sel)


_sel_r.defvjp(_sel_r_f, _sel_r_b)


@jax.custom_vjp
def _sel_l(sel, x):
    s = sel.astype(BF16)
    return sum(jnp.dot(s, p, preferred_element_type=F32) for p in _split3(x))


def _sel_l_f(sel, x):
    return _sel_l(sel, x), sel


def _sel_l_b(sel, g):
    s = sel.astype(BF16)
    dx = sum(lax.dot_general(s, p, (((0,), (0,)), ((), ())), preferred_element_type=F32) for p in _split3(g))
    return jnp.zeros_like(sel), dx


_sel_l.defvjp(_sel_l_f, _sel_l_b)


def _iota(shape, dim):
    return lax.broadcasted_iota(jnp.int32, shape, dim)


def _head_sum_mats():
    e = (_iota((GW, HP), 0) // HD == _iota((GW, HP), 1)).astype(F32)
    et = (_iota((HP, GW), 1) // HD == _iota((HP, GW), 0)).astype(F32)
    return e, et


def _modulate(x, mod_ref, sh, sc):
    return x * (1.0 + mod_ref[sc:sc + 1, :]) + mod_ref[sh:sh + 1, :]


def _ln_res(x, y, gate, lg, lb, gs):
    r = ALPHA * x + gs * (1.0 + gate) * y
    mu = jnp.mean(r, axis=-1, keepdims=True)
    var = jnp.mean(jnp.square(r - mu), axis=-1, keepdims=True)
    return (r - mu) * lax.rsqrt(var + LN_EPS) * lg + lb


def _rms(x, g):
    return x * lax.rsqrt(jnp.mean(x * x, axis=-1, keepdims=True) + RMS_EPS) * g


def _tile(n, pref):
    return pref if n % pref == 0 else n


def mod_fwd(c8, w, l, b):
    tn = w.shape[3]
    n = N_CHIPS * tn

    def body(c_ref, w_ref, b_ref, o_ref):
        h = jax.nn.silu(c_ref[...]).astype(BF16)
        o_ref[...] = jnp.dot(h, w_ref[...], preferred_element_type=F32) + b_ref[...]

    return pl.pallas_call(
        body, name="mod_fwd", grid=(N_CHIPS,),
        in_specs=[pl.BlockSpec((8, D), lambda j: (0, 0)), pl.BlockSpec((None, None, D, tn), lambda j: (l, j, 0, 0)),
                  pl.BlockSpec((1, tn), lambda j: (0, j))],
        out_specs=pl.BlockSpec((8, tn), lambda j: (0, j)), out_shape=_sds((8, n), F32),
        compiler_params=_cparams(VMEM_BIG))(c8, w, b)


def ffn_in_fwd(x, mod, w_in, l, sh, sc, S):
    T = x.shape[0]
    tm, tn = _tile(S, 512), FF // 2
    tpb, nj = S // tm, 2

    def body(x_ref, mod_ref, wg_ref, wu_ref, zg_ref, zu_ref, act_ref, h_ref):
        @pl.when(pl.program_id(1) == 0)
        def _():
            h_ref[...] = _modulate(x_ref[...], mod_ref, sh, sc).astype(BF16)
        g = jnp.dot(h_ref[...], wg_ref[...], preferred_element_type=F32)
        u = jnp.dot(h_ref[...], wu_ref[...], preferred_element_type=F32)
        zg_ref[...] = g.astype(BF16)
        zu_ref[...] = u.astype(BF16)
        act_ref[...] = (jax.nn.silu(g) * u).astype(BF16)

    return pl.pallas_call(
        body, name="ffn_in_fwd", grid=(T // tm, nj),
        in_specs=[pl.BlockSpec((tm, D), lambda i, j: (i, 0)),
                  pl.BlockSpec((None, N_MOD, D), lambda i, j: (i // tpb, 0, 0)),
                  pl.BlockSpec((None, None, D, tn), lambda i, j: (l, j, 0, 0)),
                  pl.BlockSpec((None, None, D, tn), lambda i, j: (l, j + nj, 0, 0))],
        out_specs=[pl.BlockSpec((tm, tn), lambda i, j: (i, j))] * 3,
        out_shape=[_sds((T, FF), BF16)] * 3,
        scratch_shapes=[pltpu.VMEM((tm, D), BF16)],
        compiler_params=_cparams(VMEM_BIG))(x, mod, w_in, w_in)


def mix_in_fwd(x, mod, w, sh, sc, S):
    T = x.shape[0]
    n = w.shape[1]
    tm, tn = _tile(S, 512), NP_TILE
    tpb = S // tm

    def body(x_ref, mod_ref, w_ref, o_ref, h_ref):
        @pl.when(pl.program_id(1) == 0)
        def _():
            h_ref[...] = _modulate(x_ref[...], mod_ref, sh, sc).astype(BF16)
        o_ref[...] = jnp.dot(h_ref[...], w_ref[...], preferred_element_type=F32)

    return pl.pallas_call(
        body, name="mix_in_fwd", grid=(T // tm, n // tn),
        in_specs=[pl.BlockSpec((tm, D), lambda i, j: (i, 0)),
                  pl.BlockSpec((None, N_MOD, D), lambda i, j: (i // tpb, 0, 0)),
                  pl.BlockSpec((D, tn), lambda i, j: (0, j))],
        out_specs=pl.BlockSpec((tm, tn), lambda i, j: (i, j)), out_shape=_sds((T, n), F32),
        scratch_shapes=[pltpu.VMEM((tm, D), BF16)],
        compiler_params=_cparams(VMEM_BIG))(x, mod, w)


def out_ln_fwd(act, w_out, x, mod, lg, lb, gate, gs, S, l=None):
    T, K = act.shape
    tm = _tile(S, 512)
    tpb = S // tm

    def body(a_ref, w_ref, x_ref, mod_ref, lg_ref, lb_ref, y_ref, xn_ref):
        y = jnp.dot(a_ref[...], w_ref[...].reshape(K, D), preferred_element_type=F32)
        y_ref[...] = y
        xn_ref[...] = _ln_res(x_ref[...], y, mod_ref[gate:gate + 1, :], lg_ref[...], lb_ref[...], gs)

    if l is None:
        w_spec = pl.BlockSpec((K, D), lambda i: (0, 0))
    else:
        w_spec = pl.BlockSpec((None, N_CHIPS, K // N_CHIPS, D), lambda i: (l, 0, 0, 0))
    return pl.pallas_call(
        body, name="out_ln_fwd", grid=(T // tm,),
        in_specs=[pl.BlockSpec((tm, K), lambda i: (i, 0)), w_spec,
                  pl.BlockSpec((tm, D), lambda i: (i, 0)),
                  pl.BlockSpec((None, N_MOD, D), lambda i: (i // tpb, 0, 0)),
                  pl.BlockSpec((1, D), lambda i: (0, 0)), pl.BlockSpec((1, D), lambda i: (0, 0))],
        out_specs=[pl.BlockSpec((tm, D), lambda i: (i, 0))] * 2,
        out_shape=[_sds((T, D), F32), _sds((T, D), F32)],
        compiler_params=_cparams(VMEM_BIG))(act, w_out, x, mod, lg, lb)


def ln_res_bwd(dxn, x, y, mod, lg, lb, gate, gs, S):
    T = x.shape[0]
    B = T // S
    tm = _tile(S, 512)
    tpb = S // tm

    def body(d_ref, x_ref, y_ref, mod_ref, lg_ref, lb_ref, dx_ref, dy_ref, dg_ref, dlg_ref, dlb_ref):
        i = pl.program_id(0)
        f = functools.partial(_ln_res, gs=gs)
        _, vjp = jax.vjp(f, x_ref[...], y_ref[...], mod_ref[gate:gate + 1, :], lg_ref[...], lb_ref[...])
        dx, dy, dg, dlg, dlb = vjp(d_ref[...])
        dx_ref[...] = dx
        dy_ref[...] = dy.astype(BF16)

        @pl.when(i % tpb == 0)
        def _():
            dg_ref[...] = jnp.zeros_like(dg_ref)

        @pl.when(i == 0)
        def _():
            dlg_ref[...] = jnp.zeros_like(dlg_ref)
            dlb_ref[...] = jnp.zeros_like(dlb_ref)

        dg_ref[...] += dg
        dlg_ref[...] += dlg
        dlb_ref[...] += dlb

    tok = pl.BlockSpec((tm, D), lambda i: (i, 0))
    vec = pl.BlockSpec((1, D), lambda i: (0, 0))
    return pl.pallas_call(
        body, name="ln_res_bwd", grid=(T // tm,),
        in_specs=[tok, tok, tok, pl.BlockSpec((None, N_MOD, D), lambda i: (i // tpb, 0, 0)), vec, vec],
        out_specs=[tok, tok, pl.BlockSpec((None, 1, D), lambda i: (i // tpb, 0, 0)), vec, vec],
        out_shape=[_sds((T, D), F32), _sds((T, D), BF16), _sds((B, 1, D), F32), _sds((1, D), F32), _sds((1, D), F32)],
        compiler_params=_cparams(VMEM_BIG))(dxn, x, y, mod, lg, lb)


def swiglu_bwd(dy, w_out, l, zg, zu, S):
    T = dy.shape[0]
    tm, tn = _tile(S, 512), FF // 2

    def body(dy_ref, w_ref, zg_ref, zu_ref, dg_ref, du_ref):
        da = lax.dot_general(dy_ref[...], w_ref[...].reshape(tn, D), (((1,), (1,)), ((), ())), preferred_element_type=F32)
        g, u = zg_ref[...].astype(F32), zu_ref[...].astype(F32)
        sg = jax.nn.sigmoid(g)
        dg_ref[...] = (da * u * (sg * (1.0 + g * (1.0 - sg)))).astype(BF16)
        du_ref[...] = (da * (g * sg)).astype(BF16)

    zt = pl.BlockSpec((tm, tn), lambda i, j: (i, j))
    return pl.pallas_call(
        body, name="swiglu_bwd", grid=(T // tm, FF // tn),
        in_specs=[pl.BlockSpec((tm, D), lambda i, j: (i, 0)),
                  pl.BlockSpec((None, 2, FF // N_CHIPS, D), lambda i, j: (l, j, 0, 0)), zt, zt],
        out_specs=[zt, zt], out_shape=[_sds((T, FF), BF16), _sds((T, FF), BF16)],
        compiler_params=_cparams(VMEM_BIG))(dy, w_out, zg, zu)


def nt_plain(dy, w):
    T = dy.shape[0]
    K = w.shape[0]
    tm = _tile(T, 512)

    def body(dy_ref, w_ref, o_ref):
        o_ref[...] = lax.dot_general(dy_ref[...], w_ref[...], (((1,), (1,)), ((), ())), preferred_element_type=F32)

    return pl.pallas_call(
        body, name="nt_plain", grid=(T // tm,),
        in_specs=[pl.BlockSpec((tm, D), lambda i: (i, 0)), pl.BlockSpec((K, D), lambda i: (0, 0))],
        out_specs=pl.BlockSpec((tm, K), lambda i: (i, 0)), out_shape=_sds((T, K), F32),
        compiler_params=_cparams(VMEM_BIG))(dy, w)


def _tn_step(acc, o_ref, lhs, rhs, t, nt):
    part = lax.dot_general(lhs, rhs, (((0,), (0,)), ((), ())), preferred_element_type=F32)
    if nt == 1:
        o_ref[...] = part.astype(o_ref.dtype)
        return

    @pl.when(t == 0)
    def _():
        acc[...] = part

    @pl.when((t > 0) & (t < nt - 1))
    def _():
        acc[...] += part

    @pl.when(t == nt - 1)
    def _():
        o_ref[...] = (acc[...] + part).astype(o_ref.dtype)


def tn_mm(a, b, tk):
    T, K = a.shape
    N = b.shape[1]
    tt = _tile(T, 512)
    nt = T // tt

    def body(a_ref, b_ref, o_ref, acc):
        _tn_step(acc, o_ref, a_ref[...], b_ref[...], pl.program_id(1), nt)

    return pl.pallas_call(
        body, name="tn_mm", grid=(K // tk, nt),
        in_specs=[pl.BlockSpec((tt, tk), lambda k, t: (t, k)), pl.BlockSpec((tt, N), lambda k, t: (t, 0))],
        out_specs=pl.BlockSpec((tk, N), lambda k, t: (k, 0)), out_shape=_sds((K, N), BF16),
        scratch_shapes=[pltpu.VMEM((tk, N), F32)], compiler_params=_cparams(VMEM_BIG))(a, b)


def tn_mm_mod(x, mod, b, sh, sc, S, tn):
    T = x.shape[0]
    N = b.shape[1]
    tt = _tile(S, 512)
    tpb = S // tt
    nt = T // tt

    def body(x_ref, mod_ref, b_ref, o_ref, acc):
        h = _modulate(x_ref[...], mod_ref, sh, sc).astype(BF16)
        _tn_step(acc, o_ref, h, b_ref[...], pl.program_id(1), nt)

    return pl.pallas_call(
        body, name="tn_mm_mod", grid=(N // tn, nt),
        in_specs=[pl.BlockSpec((tt, D), lambda j, t: (t, 0)),
                  pl.BlockSpec((None, N_MOD, D), lambda j, t: (t // tpb, 0, 0)),
                  pl.BlockSpec((tt, tn), lambda j, t: (t, j))],
        out_specs=pl.BlockSpec((D, tn), lambda j, t: (0, j)), out_shape=_sds((D, N), BF16),
        scratch_shapes=[pltpu.VMEM((D, tn), F32)], compiler_params=_cparams(VMEM_BIG))(x, mod, b)


def tn_mm_mod_shards(x, mod, bg, bu, sh, sc, S):
    T = x.shape[0]
    tn = FF // 2
    tt = _tile(S, 512)
    tpb = S // tt
    nt = T // tt

    def body(x_ref, mod_ref, bg_ref, bu_ref, o_ref, acc):
        j, t = pl.program_id(0), pl.program_id(1)
        h = _modulate(x_ref[...], mod_ref, sh, sc).astype(BF16)

        @pl.when(j < 2)
        def _():
            _tn_step(acc, o_ref, h, bg_ref[...], t, nt)

        @pl.when(j >= 2)
        def _():
            _tn_step(acc, o_ref, h, bu_ref[...], t, nt)

    return pl.pallas_call(
        body, name="tn_mm_mod_shards", grid=(N_CHIPS, nt),
        in_specs=[pl.BlockSpec((tt, D), lambda j, t: (t, 0)),
                  pl.BlockSpec((None, N_MOD, D), lambda j, t: (t // tpb, 0, 0)),
                  pl.BlockSpec((tt, tn), lambda j, t: (jnp.where(j < 2, t, 0), jnp.minimum(j, 1))),
                  pl.BlockSpec((tt, tn), lambda j, t: (jnp.where(j < 2, 0, t), jnp.maximum(j - 2, 0)))],
        out_specs=pl.BlockSpec((None, D, tn), lambda j, t: (j, 0, 0)), out_shape=_sds((N_CHIPS, D, tn), BF16),
        scratch_shapes=[pltpu.VMEM((D, tn), F32)], compiler_params=_cparams(VMEM_BIG))(x, mod, bg, bu)


def nt_mod_bwd(ds, w, offs, x, mod, dres, sc, S, tk, l=None):
    T = x.shape[0]
    B = T // S
    tm = _tile(S, 512)
    tpb = S // tm
    Kd = ds[0].shape[1]
    nk = Kd // tk
    n_in = len(ds)

    def body(*refs):
        d_refs, w_refs = refs[:n_in], refs[n_in:2 * n_in]
        x_ref, mod_ref, r_ref, dx_ref, dsh_ref, dsc_ref, acc = refs[2 * n_in:]
        i, k = pl.program_id(0), pl.program_id(1)

        part = sum(lax.dot_general(d_ref[...], w_ref[...], (((1,), (1,)), ((), ())), preferred_element_type=F32)
                   for d_ref, w_ref in zip(d_refs, w_refs))

        @pl.when(k == 0)
        def _():
            acc[...] = part

        @pl.when(k > 0)
        def _():
            acc[...] += part

        @pl.when(k == nk - 1)
        def _():
            dh = acc[...]
            dx_ref[...] = dh * (1.0 + mod_ref[sc:sc + 1, :]) + r_ref[...]

            @pl.when(i % tpb == 0)
            def _():
                dsh_ref[...] = jnp.zeros_like(dsh_ref)
                dsc_ref[...] = jnp.zeros_like(dsc_ref)

            dsh_ref[...] += jnp.sum(dh, axis=0, keepdims=True)
            dsc_ref[...] += jnp.sum(dh * x_ref[...], axis=0, keepdims=True)

    tok = pl.BlockSpec((tm, D), lambda i, k: (i, 0))
    vec = pl.BlockSpec((None, 1, D), lambda i, k: (i // tpb, 0, 0))
    in_specs = [pl.BlockSpec((tm, tk), lambda i, k: (i, k)) for _ in ds]
    if l is None:
        in_specs += [pl.BlockSpec((D, tk), functools.partial(lambda i, k, o: (0, k + o), o=off // tk)) for off in offs]
    else:
        in_specs += [pl.BlockSpec((None, None, D, tk), functools.partial(lambda i, k, o: (l, k + o, 0, 0), o=off)) for off in offs]
    in_specs += [tok, pl.BlockSpec((None, N_MOD, D), lambda i, k: (i // tpb, 0, 0)), tok]
    return pl.pallas_call(
        body, name="nt_mod_bwd", grid=(T // tm, nk), in_specs=in_specs,
        out_specs=[tok, vec, vec],
        out_shape=[_sds((T, D), F32), _sds((B, 1, D), F32), _sds((B, 1, D), F32)],
        scratch_shapes=[pltpu.VMEM((tm, D), F32)],
        compiler_params=_cparams(VMEM_BIG))(*ds, *([w] * n_in), x, mod, dres)


def loss_head(y, tgt):
    T = y.shape[0]
    tm = _tile(T, 512)

    def body(y_ref, t_ref, l_ref, d_ref):
        @pl.when(pl.program_id(0) == 0)
        def _():
            l_ref[...] = jnp.zeros_like(l_ref)
        e = y_ref[...] - t_ref[...]
        d_ref[...] = e * (1.0 / D)
        l_ref[...] += 0.5 * jnp.sum(jnp.sum(e * e, axis=1, keepdims=True) * (1.0 / D))

    tok = pl.BlockSpec((tm, D), lambda i: (i, 0))
    return pl.pallas_call(
        body, name="loss_head", grid=(T // tm,), in_specs=[tok, tok],
        out_specs=[pl.BlockSpec((8, 128), lambda i: (0, 0)), tok],
        out_shape=[_sds((8, 128), F32), _sds((T, D), F32)],
        compiler_params=_cparams(VMEM_BIG))(y, tgt)


def _hgrn_block(q, fz, inp, go, st, lb, ng, blk):
    nc = blk // A_CHUNK
    lb_eff = jnp.maximum(lb, LB_FLOOR)
    log_f = jnp.logaddexp(jnp.log(lb_eff), jnp.log1p(-lb) + jax.nn.log_sigmoid(fz))
    k = (1.0 - lb) * jax.nn.sigmoid(-fz) - (lb_eff - lb)
    qf = jax.nn.silu(q)
    same_chunk = _iota((blk, blk), 0) // A_CHUNK == _iota((blk, blk), 1) // A_CHUNK
    tril = (same_chunk & (_iota((blk, blk), 1) <= _iota((blk, blk), 0))).astype(F32)
    G = _sel_l(tril, log_f)
    e_mat, et_mat = _head_sum_mats()
    G4, q4, k4, v4 = (z.reshape(nc, A_CHUNK, GW) for z in (G, qf, k, inp))
    shp = (nc, A_CHUNK, A_CHUNK, GW)
    one = (1, A_CHUNK, A_CHUNK, GW)
    mask = jnp.where(_iota(one, 2) <= _iota(one, 1), 0.0, -jnp.inf)
    decay = jnp.exp((G4[:, :, None, :] - G4[:, None, :, :]) + mask)
    prod = q4[:, :, None, :] * k4[:, None, :, :] * decay
    scores = _mm(prod.reshape(nc * A_CHUNK * A_CHUNK, GW), e_mat.astype(BF16))
    spread = _mm(scores, et_mat.astype(BF16)).reshape(shp)
    o_intra = jnp.sum(spread * v4[:, None, :, :], axis=2).reshape(blk, GW)
    head_diag = (_iota((GW, GW), 0) // HD == _iota((GW, GW), 1) // HD).astype(F32)
    g_last = [jnp.sum(log_f[c * A_CHUNK:(c + 1) * A_CHUNK], axis=0, keepdims=True) for c in range(nc)]
    g_last_b = jnp.concatenate([jnp.broadcast_to(g, (A_CHUNK, GW)) for g in g_last], axis=0)
    q_dec = qf * jnp.exp(G)
    k_end = k * jnp.exp(g_last_b - G)
    outs = []
    for c in range(nc):
        rows = slice(c * A_CHUNK, (c + 1) * A_CHUNK)
        outs.append(_mm_nt(q_dec[rows], st))
        st = st * jnp.exp(g_last[c]) + _mm_tn(inp[rows], k_end[rows]) * head_diag
    o = o_intra + jnp.concatenate(outs, axis=0)
    ms = _sel_r(o * o, e_mat) * (1.0 / HD)
    o = o * _sel_r(lax.rsqrt(ms + RMS_EPS), et_mat) * ng
    return o * jax.nn.silu(go), st


HGRN_BLK = 128


def hgrn_fwd(proj, lb, ng, S):
    T = proj.shape[0]
    B = T // S
    blk = min(HGRN_BLK, S)
    nb = S // blk

    def body(p_ref, lb_ref, ng_ref, o_ref, st_out_ref, st_ref):
        @pl.when(pl.program_id(1) == 0)
        def _():
            st_ref[...] = jnp.zeros_like(st_ref)
        st_out_ref[...] = st_ref[...]
        p = p_ref[...]
        o, st = _hgrn_block(p[:, 0:GW], p[:, GW:2 * GW], p[:, 2 * GW:3 * GW], p[:, 3 * GW:4 * GW],
                            st_ref[...], lb_ref[...], ng_ref[...], blk)
        o_ref[...] = o.astype(BF16)
        st_ref[...] = st

    vec = pl.BlockSpec((1, GW), lambda b, j: (0, 0))
    return pl.pallas_call(
        body, name="hgrn_fwd", grid=(B, nb),
        in_specs=[pl.BlockSpec((blk, 4 * GW), lambda b, j: (b * nb + j, C_A // (4 * GW))), vec, vec],
        out_specs=[pl.BlockSpec((blk, GW), lambda b, j: (b * nb + j, 0)),
                   pl.BlockSpec((None, GW, GW), lambda b, j: (b * nb + j, 0, 0))],
        out_shape=[_sds((T, GW), BF16), _sds((B * nb, GW, GW), F32)],
        scratch_shapes=[pltpu.VMEM((GW, GW), F32)],
        compiler_params=_cparams(VMEM_BIG))(proj, lb, ng)


def hgrn_bwd(proj, states, dcat, lb, ng, S):
    T = proj.shape[0]
    B = T // S
    blk = min(HGRN_BLK, S)
    nb = S // blk

    def body(p_ref, st_in_ref, do_ref, lb_ref, ng_ref, dp_ref, dlb_ref, dng_ref, dst_ref):
        b, j = pl.program_id(0), pl.program_id(1)

        @pl.when(j == 0)
        def _():
            dst_ref[...] = jnp.zeros_like(dst_ref)

        @pl.when((b == 0) & (j == 0))
        def _():
            dlb_ref[...] = jnp.zeros_like(dlb_ref)
            dng_ref[...] = jnp.zeros_like(dng_ref)

        p = p_ref[...]
        f = functools.partial(_hgrn_block, blk=blk)
        _, vjp = jax.vjp(f, p[:, 0:GW], p[:, GW:2 * GW], p[:, 2 * GW:3 * GW], p[:, 3 * GW:4 * GW],
                         st_in_ref[...], lb_ref[...], ng_ref[...])
        dq, df, di, dg, dst, dlb, dng = vjp((do_ref[...], dst_ref[...]))
        dp_ref[...] = jnp.concatenate([dq, df, di, dg], axis=1).astype(BF16)
        dst_ref[...] = dst
        dlb_ref[...] += dlb
        dng_ref[...] += dng

    def rev(b, j):
        return b * nb + (nb - 1 - j)

    vec = pl.BlockSpec((1, GW), lambda b, j: (0, 0))
    return pl.pallas_call(
        body, name="hgrn_bwd", grid=(B, nb),
        in_specs=[pl.BlockSpec((blk, 4 * GW), lambda b, j: (rev(b, j), C_A // (4 * GW))),
                  pl.BlockSpec((None, GW, GW), lambda b, j: (rev(b, j), 0, 0)),
                  pl.BlockSpec((blk, GW), lambda b, j: (rev(b, j), O_A // GW)), vec, vec],
        out_specs=[pl.BlockSpec((blk, 4 * GW), lambda b, j: (rev(b, j), 0)), vec, vec],
        out_shape=[_sds((T, 4 * GW), BF16), _sds((1, GW), F32), _sds((1, GW), F32)],
        scratch_shapes=[pltpu.VMEM((GW, GW), F32)],
        compiler_params=_cparams(VMEM_BIG))(proj, states, dcat, lb, ng)


ATT_TQ = 256


ATT_BANDS = 8


def _attn_block(q, k, v, cum, qpos0, scale, use_cum, n_free):
    s = _mm_nt(q, k) * scale
    if use_cum:
        s = s - cum
    band = s[:, n_free:]
    visible = _iota(band.shape, 1) <= (qpos0 - n_free) + _iota(band.shape, 0)
    band = jnp.where(visible, band, -jnp.inf)
    m = jnp.max(band, axis=-1, keepdims=True)
    if n_free:
        free = s[:, :n_free]
        m = jnp.maximum(m, jnp.max(free, axis=-1, keepdims=True))
    e = jnp.exp(band - m)
    denom = jnp.sum(e, axis=-1, keepdims=True)
    o = _mm(e, v[n_free:])
    if n_free:
        e = jnp.exp(free - m)
        denom = denom + jnp.sum(e, axis=-1, keepdims=True)
        o = o + _mm(e, v[:n_free])
    return o * (1.0 / denom)


def _bands(S, tq):
    nq = S // tq
    nb = min(ATT_BANDS, nq)
    per = nq // nb
    return [(r * per, (r + 1) * per, (r + 1) * per * tq) for r in range(nb)]


def attn_fwd(qa, qo, ka, ko, va, vo, cum, scale, S):
    T = qa.shape[0]
    B = T // S
    tq = min(ATT_TQ, S)
    nq = S // tq
    use_cum = cum is not None

    def body(*refs):
        if use_cum:
            q_ref, k_ref, v_ref, c_ref, o_ref = refs
        else:
            (q_ref, k_ref, v_ref, o_ref), c_ref = refs, None
        h, i = pl.program_id(1), pl.program_id(2)
        for lo, hi, kw in _bands(S, tq):
            @pl.when((i >= lo) & (i < hi))
            def _():
                crow = c_ref[pl.ds(h, 1), 0:kw] if use_cum else None
                o = _attn_block(q_ref[...], k_ref[0:kw, :], v_ref[0:kw, :], crow, i * tq, scale, use_cum, lo * tq)
                o_ref[...] = o.astype(BF16)

    in_specs = [pl.BlockSpec((tq, HP), lambda b, h, i: (b * nq + i, qo + h)),
                pl.BlockSpec((S, HP), lambda b, h, i: (b, ko + h)),
                pl.BlockSpec((S, HP), lambda b, h, i: (b, vo + h))]
    args = [qa, ka, va]
    if use_cum:
        in_specs.append(pl.BlockSpec((None, 8, S), lambda b, h, i: (b, 0, 0)))
        args.append(cum)
    return pl.pallas_call(
        body, name="attn_fwd", grid=(B, HEADS, nq), in_specs=in_specs,
        out_specs=pl.BlockSpec((tq, HP), lambda b, h, i: (b * nq + i, h)),
        out_shape=_sds((T, HEADS * HP), BF16),
        compiler_params=_cparams(VMEM_BIG))(*args)


def _attn_block_bwd(q, k, v, cum, do, qpos0, scale, use_cum, n_free):
    tn = (((0,), (0,)), ((), ()))
    nt = (((1,), (1,)), ((), ()))
    qb, dob = q.astype(BF16), do.astype(BF16)
    kb, vb = k.astype(BF16), v.astype(BF16)
    s = lax.dot_general(qb, kb, nt, preferred_element_type=F32) * scale
    if use_cum:
        s = s - cum
    band = s[:, n_free:]
    visible = _iota(band.shape, 1) <= (qpos0 - n_free) + _iota(band.shape, 0)
    parts = [(jnp.where(visible, band, -jnp.inf), n_free, s.shape[1])]
    if n_free:
        parts.append((s[:, :n_free], 0, n_free))
    m = functools.reduce(jnp.maximum, [jnp.max(sp, axis=-1, keepdims=True) for sp, _, _ in parts])
    es = [jnp.exp(sp - m) for sp, _, _ in parts]
    rinv = 1.0 / sum(jnp.sum(e, axis=-1, keepdims=True) for e in es)
    ps = [e * rinv for e in es]
    dps = [lax.dot_general(dob, vb[a:b], nt, preferred_element_type=F32) for _, a, b in parts]
    delta = sum(jnp.sum(p * dp, axis=-1, keepdims=True) for p, dp in zip(ps, dps))
    dq = jnp.zeros(q.shape, F32)
    out = []
    for p, dp, (_, a, b) in zip(ps, dps, parts):
        ds = p * (dp - delta)
        dsb = ds.astype(BF16)
        dq = dq + jnp.dot(dsb, kb[a:b], preferred_element_type=F32)
        out.append((a, b, lax.dot_general(dsb, qb, tn, preferred_element_type=F32) * scale,
                    lax.dot_general(p.astype(BF16), dob, tn, preferred_element_type=F32),
                    -jnp.sum(ds, axis=0, keepdims=True) if use_cum else None))
    return dq * scale, out


def attn_bwd(qa, qo, ka, ko, va, vo, cum, dcat, do_off, scale, S, out_dtype):
    T = qa.shape[0]
    B = T // S
    tq = min(ATT_TQ, S)
    nq = S // tq
    use_cum = cum is not None

    def body(*refs):
        if use_cum:
            q_ref, k_ref, v_ref, do_ref, c_ref, dq_ref, dk_ref, dv_ref, dc_ref, dk_acc, dv_acc = refs
        else:
            q_ref, k_ref, v_ref, do_ref, dq_ref, dk_ref, dv_ref, dk_acc, dv_acc = refs
        h, i = pl.program_id(1), pl.program_id(2)

        @pl.when(i == 0)
        def _():
            dk_acc[...] = jnp.zeros_like(dk_acc)
            dv_acc[...] = jnp.zeros_like(dv_acc)
            if use_cum:
                dc_ref[...] = jnp.zeros_like(dc_ref)

        for lo, hi, kw in _bands(S, tq):
            @pl.when((i >= lo) & (i < hi))
            def _():
                crow = c_ref[pl.ds(h, 1), 0:kw] if use_cum else None
                dq, pieces = _attn_block_bwd(q_ref[...], k_ref[0:kw, :], v_ref[0:kw, :], crow, do_ref[...], i * tq,
                                             scale, use_cum, lo * tq)
                dq_ref[...] = dq.astype(out_dtype)
                for a, b, dk, dv, dc in pieces:
                    dk_acc[a:b, :] += dk
                    dv_acc[a:b, :] += dv
                    if use_cum:
                        dc_ref[:, a:b] += dc

        @pl.when(i == nq - 1)
        def _():
            dk_ref[...] = dk_acc[...].astype(out_dtype)
            dv_ref[...] = dv_acc[...].astype(out_dtype)

    qspec = pl.BlockSpec((tq, HP), lambda b, h, i: (b * nq + i, qo + h))
    in_specs = [qspec, pl.BlockSpec((S, HP), lambda b, h, i: (b, ko + h)),
                pl.BlockSpec((S, HP), lambda b, h, i: (b, vo + h)),
                pl.BlockSpec((tq, HP), lambda b, h, i: (b * nq + i, do_off + h))]
    args = [qa, ka, va, dcat]
    kv_out = pl.BlockSpec((S, HP), lambda b, h, i: (b, h))
    out_specs = [pl.BlockSpec((tq, HP), lambda b, h, i: (b * nq + i, h)), kv_out, kv_out]
    out_shape = [_sds((T, HEADS * HP), out_dtype)] * 3
    if use_cum:
        in_specs.append(pl.BlockSpec((None, 8, S), lambda b, h, i: (b, 0, 0)))
        args.append(cum)
        out_specs.append(pl.BlockSpec((None, 1, S), lambda b, h, i: (b * HEADS + h, 0, 0)))
        out_shape.append(_sds((B * HEADS, 1, S), F32))
    return pl.pallas_call(
        body, name="attn_bwd", grid=(B, HEADS, nq), in_specs=in_specs, out_specs=out_specs, out_shape=out_shape,
        scratch_shapes=[pltpu.VMEM((S, HP), F32), pltpu.VMEM((S, HP), F32)],
        compiler_params=_cparams(VMEM_BIG))(*args)


def _tri(n, upper):
    r, c = _iota((n, n), 0), _iota((n, n), 1)
    return ((r <= c) if upper else (r >= c)).astype(F32)


def fox_gate_fwd(proj, bcol, S):
    T = proj.shape[0]
    B = T // S
    ts = _tile(S, 512)
    nt = S // ts

    def body(p_ref, b_ref, o_ref, carry):
        @pl.when(pl.program_id(1) == 0)
        def _():
            carry[...] = jnp.zeros_like(carry)
        cf = jnp.transpose(p_ref[...])[0:8, :]
        lf = jax.nn.log_sigmoid(cf + b_ref[...])
        cum = _sel_r(lf, _tri(ts, True)) + carry[...]
        o_ref[...] = cum
        carry[...] += jnp.sum(lf, axis=1, keepdims=True)

    return pl.pallas_call(
        body, name="fox_gate_fwd", grid=(B, nt),
        in_specs=[pl.BlockSpec((ts, HP), lambda b, j: (b * nt + j, C_CF // HP)), pl.BlockSpec((8, 1), lambda b, j: (0, 0))],
        out_specs=pl.BlockSpec((None, 8, ts), lambda b, j: (b, 0, j)), out_shape=_sds((B, 8, S), F32),
        scratch_shapes=[pltpu.VMEM((8, 1), F32)],
        compiler_params=_cparams(VMEM_BIG))(proj, bcol)


def fox_gate_bwd(proj, bcol, dcum, S):
    T = proj.shape[0]
    B = T // S
    ts = _tile(S, 512)
    nt = S // ts

    def body(p_ref, b_ref, dc_ref, dp_ref, db_ref, carry):
        b, j = pl.program_id(0), pl.program_id(1)

        @pl.when(j == 0)
        def _():
            carry[...] = jnp.zeros_like(carry)

        @pl.when((b == 0) & (j == 0))
        def _():
            db_ref[...] = jnp.zeros_like(db_ref)

        cf = jnp.transpose(p_ref[...])[0:8, :]
        dc = dc_ref[...]
        dlf = _sel_r(dc, _tri(ts, False)) + carry[...]
        carry[...] += jnp.sum(dc, axis=1, keepdims=True)
        dcf = dlf * jax.nn.sigmoid(-(cf + b_ref[...]))
        db_ref[...] += jnp.sum(dcf, axis=1, keepdims=True)
        full = jnp.concatenate([dcf, jnp.zeros((HP - 8, ts), F32)], axis=0)
        dp_ref[...] = jnp.transpose(full).astype(BF16)

    def rev(b, j):
        return nt - 1 - j

    return pl.pallas_call(
        body, name="fox_gate_bwd", grid=(B, nt),
        in_specs=[pl.BlockSpec((ts, HP), lambda b, j: (b * nt + rev(b, j), C_CF // HP)),
                  pl.BlockSpec((8, 1), lambda b, j: (0, 0)),
                  pl.BlockSpec((None, 8, ts), lambda b, j: (b, 0, rev(b, j)))],
        out_specs=[pl.BlockSpec((ts, HP), lambda b, j: (b * nt + rev(b, j), 0)), pl.BlockSpec((8, 1), lambda b, j: (0, 0))],
        out_shape=[_sds((T, HP), BF16), _sds((8, 1), F32)],
        scratch_shapes=[pltpu.VMEM((8, 1), F32)],
        compiler_params=_cparams(VMEM_BIG))(proj, bcol, dcum)


def _mla_pre(blk, gq, gkv, wq, wkv, place, cos_q, sin_q, cs_k):
    nq = _rms(blk[:, 0:B_Q_LORA], gq)
    nkv = _rms(blk[:, B_Q_LORA:B_Q_LORA + B_KV_LORA], gkv)
    qq = _mm(nq, wq)
    q = qq[:, 0:HEADS * HP] * cos_q + qq[:, HEADS * HP:] * sin_q
    kv = _mm(nkv, wkv)
    k = kv[:, 0:HEADS * HP] + _mm(blk[:, B_Q_LORA + B_KV_LORA:] * cs_k, place)
    return q, k, kv[:, HEADS * HP:]


def mla_pre_fwd(proj, gq, gkv, wq, wkv, place, cos_q, sin_q, cs_k, S):
    T = proj.shape[0]
    tm = _tile(S, 512)
    tpb = S // tm
    W = HEADS * HP

    def body(p_ref, gq_ref, gkv_ref, wq_ref, wkv_ref, pl_ref, cq_ref, sq_ref, ck_ref, q_ref, k_ref, v_ref):
        q, k, v = _mla_pre(p_ref[...], gq_ref[...], gkv_ref[...], wq_ref[...], wkv_ref[...], pl_ref[...],
                           cq_ref[...], sq_ref[...], ck_ref[...])
        q_ref[...] = q
        k_ref[...] = k
        v_ref[...] = v

    def full(a):
        return pl.BlockSpec(a.shape, lambda i: (0,) * a.ndim)

    tok = pl.BlockSpec((tm, W), lambda i: (i, 0))
    return pl.pallas_call(
        body, name="mla_pre_fwd", grid=(T // tm,),
        in_specs=[pl.BlockSpec((tm, W), lambda i: (i, C_B // W)), full(gq), full(gkv), full(wq), full(wkv), full(place),
                  pl.BlockSpec((tm, W), lambda i: (i % tpb, 0)), pl.BlockSpec((tm, W), lambda i: (i % tpb, 0)),
                  pl.BlockSpec((tm, HP), lambda i: (i % tpb, 0))],
        out_specs=[tok] * 3, out_shape=[_sds((T, W), F32)] * 3,
        compiler_params=_cparams(VMEM_BIG))(proj, gq, gkv, wq, wkv, place, cos_q, sin_q, cs_k)


def mla_pre_bwd(proj, gq, gkv, wq, wkv, place, cos_q, sin_q, cs_k, dq, dk, dv, S):
    T = proj.shape[0]
    tm = _tile(S, 512)
    tpb = S // tm
    W = HEADS * HP

    def body(p_ref, gq_ref, gkv_ref, wq_ref, wkv_ref, pl_ref, cq_ref, sq_ref, ck_ref, dq_ref, dk_ref, dv_ref,
             dp_ref, dgq_ref, dgkv_ref, dwq_ref, dwkv_ref):
        @pl.when(pl.program_id(0) == 0)
        def _():
            for r in (dgq_ref, dgkv_ref, dwq_ref, dwkv_ref):
                r[...] = jnp.zeros_like(r)

        f = functools.partial(_mla_pre, place=pl_ref[...], cos_q=cq_ref[...], sin_q=sq_ref[...], cs_k=ck_ref[...])
        _, vjp = jax.vjp(f, p_ref[...], gq_ref[...], gkv_ref[...], wq_ref[...], wkv_ref[...])
        dp, dgq, dgkv, dwq, dwkv = vjp((dq_ref[...], dk_ref[...], dv_ref[...]))
        dp_ref[...] = dp.astype(BF16)
        dgq_ref[...] += dgq
        dgkv_ref[...] += dgkv
        dwq_ref[...] += dwq
        dwkv_ref[...] += dwkv

    def full(a):
        return pl.BlockSpec(a.shape, lambda i: (0,) * a.ndim)

    tok = pl.BlockSpec((tm, W), lambda i: (i, 0))
    return pl.pallas_call(
        body, name="mla_pre_bwd", grid=(T // tm,),
        in_specs=[pl.BlockSpec((tm, W), lambda i: (i, C_B // W)), full(gq), full(gkv), full(wq), full(wkv), full(place),
                  pl.BlockSpec((tm, W), lambda i: (i % tpb, 0)), pl.BlockSpec((tm, W), lambda i: (i % tpb, 0)),
                  pl.BlockSpec((tm, HP), lambda i: (i % tpb, 0)), tok, tok, tok],
        out_specs=[tok, full(gq), full(gkv), full(wq), full(wkv)],
        out_shape=[_sds((T, W), BF16), _sds(gq.shape, F32), _sds(gkv.shape, F32), _sds(wq.shape, F32), _sds(wkv.shape, F32)],
        compiler_params=_cparams(VMEM_BIG))(proj, gq, gkv, wq, wkv, place, cos_q, sin_q, cs_k, dq, dk, dv)


def _gmlp_block(blk, lg, lb, ws, bs):
    u = jax.nn.gelu(blk[:, 0:GW])
    v = jax.nn.gelu(blk[:, GW:2 * GW])
    mu = jnp.mean(v, axis=-1, keepdims=True)
    var = jnp.mean(jnp.square(v - mu), axis=-1, keepdims=True)
    vn = (v - mu) * lax.rsqrt(var + LN_EPS) * lg + lb
    causal = _iota((D_CHUNK, D_CHUNK), 1) <= _iota((D_CHUNK, D_CHUNK), 0)
    group = _iota((1, GW), 1) // HD
    mixed = jnp.zeros((D_CHUNK, GW), F32)
    for g in range(HEADS):
        part = _mm(jnp.where(causal, ws[g], 0.0), vn) + bs[g]
        mixed = mixed + jnp.where(group == g, part, 0.0)
    return u * mixed


def gmlp_fwd(proj, lg, lb, ws, bs):
    T = proj.shape[0]

    def body(p_ref, lg_ref, lb_ref, ws_ref, bs_ref, o_ref):
        o_ref[...] = _gmlp_block(p_ref[...], lg_ref[...], lb_ref[...], ws_ref[...], bs_ref[...]).astype(BF16)

    def full(a):
        return pl.BlockSpec(a.shape, lambda i: (0,) * a.ndim)

    return pl.pallas_call(
        body, name="gmlp_fwd", grid=(T // D_CHUNK,),
        in_specs=[pl.BlockSpec((D_CHUNK, 2 * GW), lambda i: (i, C_D // (2 * GW))), full(lg), full(lb), full(ws), full(bs)],
        out_specs=pl.BlockSpec((D_CHUNK, GW), lambda i: (i, 0)), out_shape=_sds((T, GW), BF16),
        compiler_params=_cparams(VMEM_BIG))(proj, lg, lb, ws, bs)


def gmlp_bwd(proj, lg, lb, ws, bs, dcat):
    T = proj.shape[0]

    def body(p_ref, lg_ref, lb_ref, ws_ref, bs_ref, do_ref, dp_ref, dlg_ref, dlb_ref, dws_ref, dbs_ref):
        @pl.when(pl.program_id(0) == 0)
        def _():
            for r in (dlg_ref, dlb_ref, dws_ref, dbs_ref):
                r[...] = jnp.zeros_like(r)

        _, vjp = jax.vjp(_gmlp_block, p_ref[...], lg_ref[...], lb_ref[...], ws_ref[...], bs_ref[...])
        dp, dlg, dlb, dws, dbs = vjp(do_ref[...])
        dp_ref[...] = dp.astype(BF16)
        dlg_ref[...] += dlg
        dlb_ref[...] += dlb
        dws_ref[...] += dws
        dbs_ref[...] += dbs

    def full(a):
        return pl.BlockSpec(a.shape, lambda i: (0,) * a.ndim)

    return pl.pallas_call(
        body, name="gmlp_bwd", grid=(T // D_CHUNK,),
        in_specs=[pl.BlockSpec((D_CHUNK, 2 * GW), lambda i: (i, C_D // (2 * GW))), full(lg), full(lb), full(ws), full(bs),
                  pl.BlockSpec((D_CHUNK, GW), lambda i: (i, O_D // GW))],
        out_specs=[pl.BlockSpec((D_CHUNK, 2 * GW), lambda i: (i, 0)), full(lg), full(lb), full(ws), full(bs)],
        out_shape=[_sds((T, 2 * GW), BF16), _sds(lg.shape, F32), _sds(lb.shape, F32), _sds(ws.shape, F32), _sds(bs.shape, F32)],
        compiler_params=_cparams(VMEM_BIG))(proj, lg, lb, ws, bs, dcat)


def _lb_all(logits):
    m = jnp.max(logits, axis=0, keepdims=True)
    e = jnp.exp(logits - m)
    sm = e / jnp.sum(e, axis=0, keepdims=True)
    return jnp.concatenate([sm[0:1] - sm[0:1], (sm[0:1] + sm[1:2]) - sm[0:1]], axis=0)


def lb_fwd(logits):
    def body(l_ref, o_ref):
        o_ref[...] = _lb_all(l_ref[...])

    return pl.pallas_call(body, name="lb_fwd", out_shape=_sds(logits.shape, F32))(logits)


def lb_bwd(logits, dlb):
    def body(l_ref, d_ref, o_ref):
        _, vjp = jax.vjp(_lb_all, l_ref[...])
        o_ref[...] = vjp(d_ref[...])[0]

    return pl.pallas_call(body, name="lb_bwd", out_shape=_sds(logits.shape, F32))(logits, dlb)


def ada_grad(c_all, dmod_cols):
    N = dmod_cols.shape[1]
    tn = _tile(N, 1152)

    def body(c_ref, d_ref, o_ref):
        h = jax.nn.silu(c_ref[...]).astype(BF16)
        o_ref[...] = lax.dot_general(h, d_ref[...].astype(BF16), (((0,), (0,)), ((), ())), preferred_element_type=F32)

    nb = c_all.shape[0]
    return pl.pallas_call(
        body, name="ada_grad", grid=(N // tn,),
        in_specs=[pl.BlockSpec((nb, D), lambda j: (0, 0)), pl.BlockSpec((nb, tn), lambda j: (0, j))],
        out_specs=pl.BlockSpec((D, tn), lambda j: (0, j)), out_shape=_sds((D, N), F32),
        compiler_params=_cparams(VMEM_BIG))(c_all, dmod_cols)


def sum_slots(a, n, name):
    _, R, C = a.shape
    tr = _row_tile(R, C, n)

    def body(a_ref, o_ref):
        acc = a_ref[0]
        for k in range(1, n):
            acc = acc + a_ref[k]
        o_ref[...] = acc

    return pl.pallas_call(
        body, name=name, grid=(R // tr,),
        in_specs=[pl.BlockSpec((n, tr, C), lambda i: (0, i, 0))],
        out_specs=pl.BlockSpec((tr, C), lambda i: (i, 0)), out_shape=_sds((R, C), F32),
        compiler_params=_cparams(VMEM_BIG))(a)


def _row_tile(R, C=D, n=1, mult=8, elems=1 << 18):
    limit = max(mult, elems // (C * n))
    for t in range(limit - limit % mult, mult - 1, -mult):
        if R % t == 0:
            return t
    return R


def adamw(w, g, m, v, name, echo=False):
    R, C = w.shape
    tr = _row_tile(R, C, elems=1 << 19)
    c1 = 1.0 - ADAM_B1 ** ADAM_STEP
    c2 = 1.0 - ADAM_B2 ** ADAM_STEP
    n_out = 4 if echo else 3

    def body(w_ref, g_ref, m_ref, v_ref, d_ref, nm_ref, nv_ref, *g_out):
        g_ = g_ref[...]
        nm = ADAM_B1 * m_ref[...] + (1.0 - ADAM_B1) * g_
        nv = ADAM_B2 * v_ref[...] + (1.0 - ADAM_B2) * jnp.square(g_)
        d_ref[...] = -ADAM_LR * ((nm / c1) / (jnp.sqrt(nv / c2) + ADAM_EPS) + ADAM_WD * w_ref[...])
        nm_ref[...] = nm
        nv_ref[...] = nv
        if echo:
            g_out[0][...] = g_

    spec = pl.BlockSpec((tr, C), lambda i: (i, 0))
    return pl.pallas_call(body, name=name, grid=(R // tr,), in_specs=[spec] * 4, out_specs=[spec] * n_out,
                          out_shape=[_sds((R, C), F32)] * n_out, compiler_params=_cparams(VMEM_BIG))(w, g, m, v)


def _rot_cols(w):
    return jnp.concatenate([-w[:, 16:32], w[:, 0:16]], axis=1)


def _fold_rot(d):
    return jnp.concatenate([d[:, 16:32], -d[:, 0:16]], axis=1)


def _pad_heads(w, off, axis):
    parts = []
    for h in range(HEADS):
        piece = lax.slice_in_dim(w, off + HD * h, off + HD * (h + 1), axis=axis)
        parts += [piece, jnp.zeros_like(piece)]
    return parts


def _unpad_heads(d, off, axis):
    return [lax.slice_in_dim(d, off + HP * h, off + HP * h + HD, axis=axis) for h in range(HEADS)]


def mix_in_ext(w):
    z = lambda n: jnp.zeros((w.shape[0], n), w.dtype)
    kr = w[:, 1408:1440]
    cols = [w[:, 0:1408], kr, _rot_cols(kr), z(64)]
    cols += _pad_heads(w, 1440, 1) + _pad_heads(w, 1696, 1) + _pad_heads(w, 1952, 1)
    cols += [w[:, 2212:2724], w[:, 2208:2212], z(NP - C_CF - HEADS)]
    return jnp.concatenate(cols, axis=1)


def mix_in_unext(d):
    kr = d[:, 1408:1440] + _fold_rot(d[:, 1440:1472])
    cols = [d[:, 0:1408], kr] + _unpad_heads(d, C_CQ, 1) + _unpad_heads(d, C_CK, 1) + _unpad_heads(d, C_CV, 1)
    cols += [d[:, C_CF:C_CF + HEADS], d[:, C_D:C_D + 2 * GW]]
    return jnp.concatenate(cols, axis=1)


def mix_out_ext(w):
    return jnp.concatenate([w[0:GW]] + _pad_heads(w, GW, 0) + _pad_heads(w, 2 * GW, 0) + [w[3 * GW:4 * GW]], axis=0)


def mix_out_unext(d):
    return jnp.concatenate([d[0:GW]] + _unpad_heads(d, O_B, 0) + _unpad_heads(d, O_C, 0) + [d[O_D:O_D + GW]], axis=0)


def uq_ext(w):
    z = lambda n: jnp.zeros((w.shape[0], n), w.dtype)
    a, b = [], []
    for h in range(HEADS):
        o = (B_NOPE + B_ROPE) * h
        a += [w[:, o:o + B_NOPE + B_ROPE], z(32)]
        b += [z(B_NOPE), _rot_cols(w[:, o + B_NOPE:o + B_NOPE + B_ROPE]), z(32)]
    return jnp.concatenate(a + b, axis=1)


def uq_unext(d):
    cols = []
    for h in range(HEADS):
        o = HP * h
        cols += [d[:, o:o + B_NOPE], d[:, o + B_NOPE:o + B_NOPE + B_ROPE]
                 + _fold_rot(d[:, HEADS * HP + o + B_NOPE:HEADS * HP + o + B_NOPE + B_ROPE])]
    return jnp.concatenate(cols, axis=1)


def ukv_ext(w):
    z = jnp.zeros((w.shape[0], HD), w.dtype)
    k, v = [], []
    for h in range(HEADS):
        k += [w[:, 2 * HD * h:2 * HD * h + HD], z]
        v += [w[:, 2 * HD * h + HD:2 * HD * (h + 1)], z]
    return jnp.concatenate(k + v, axis=1)


def ukv_unext(d):
    cols = []
    for h in range(HEADS):
        cols += [d[:, HP * h:HP * h + HD], d[:, HEADS * HP + HP * h:HEADS * HP + HP * h + HD]]
    return jnp.concatenate(cols, axis=1)


def rope_tables(S):
    half = B_ROPE // 2
    inv_freq = ROPE_THETA ** (-jnp.arange(half, dtype=F32) / half)
    ang = jnp.arange(S).astype(F32)[:, None] * inv_freq[None, :]
    cos = jnp.tile(jnp.cos(ang), (1, 2))
    sin = jnp.tile(jnp.sin(ang), (1, 2))
    one, zero = jnp.ones((S, B_NOPE), F32), jnp.zeros((S, B_NOPE), F32)
    z32 = jnp.zeros((S, 32), F32)
    cos_q = jnp.tile(jnp.concatenate([one, cos, z32], axis=1), (1, HEADS))
    sin_q = jnp.tile(jnp.concatenate([zero, sin, z32], axis=1), (1, HEADS))
    cs_k = jnp.concatenate([cos, sin, zero], axis=1)
    place = np.zeros((HP, HEADS * HP), np.float32)
    for h in range(HEADS):
        for j in range(B_ROPE):
            place[j, h * HP + B_NOPE + j] = 1.0
            place[B_ROPE + j, h * HP + B_NOPE + j] = 1.0
    return cos_q, sin_q, cs_k, jnp.asarray(place, BF16)


def layer_fwd(x, mod, get, tabs, S):
    cos_q, sin_q, cs_k, place = tabs
    p = dict(get("ffn1", x))
    l = p["wl"]
    zg1, zu1, act1 = ffn_in_fwd(x, mod, p["ffn1_in"], l, 0, 1, S)
    y1, x1 = out_ln_fwd(act1, p["ffn1_out"], x, mod, p["ln_g"][0:1], p["ln_b"][0:1], 2, 0.5, S, l)
    p.update(get("mix", x1))
    proj = mix_in_fwd(x1, mod, p["mix_in"], 3, 4, S)
    o_a, states = hgrn_fwd(proj, p["lb"], p["ng"], S)
    q_b, k_b, v_b = mla_pre_fwd(proj, p["gq"], p["gkv"], p["wq"], p["wkv"], place, cos_q, sin_q, cs_k, S)
    o_b = attn_fwd(q_b, 0, k_b, 0, v_b, 0, None, (B_NOPE + B_ROPE) ** -0.5, S)
    cum = fox_gate_fwd(proj, p["bcol"], S)
    o_c = attn_fwd(proj, C_CQ // HP, proj, C_CK // HP, proj, C_CV // HP, cum, HD ** -0.5, S)
    o_d = gmlp_fwd(proj, p["g_lg"], p["g_lb"], p["ws"], p["bs"])
    cat = jnp.concatenate([o_a, o_b, o_c, o_d], axis=1)
    y2, x2 = out_ln_fwd(cat, p["mix_out"], x1, mod, p["ln_g"][1:2], p["ln_b"][1:2], 5, 1.0, S)
    p.update(get("ffn2", x2))
    zg3, zu3, act3 = ffn_in_fwd(x2, mod, p["ffn2_in"], l, 6, 7, S)
    y3, x3 = out_ln_fwd(act3, p["ffn2_out"], x2, mod, p["ln_g"][2:3], p["ln_b"][2:3], 8, 0.5, S, l)
    saved = dict(x=x, zg1=zg1, zu1=zu1, act1=act1, y1=y1, x1=x1, proj=proj, states=states, q_b=q_b, k_b=k_b, v_b=v_b,
                 cum=cum, cat=cat, y2=y2, x2=x2, zg3=zg3, zu3=zu3, act3=act3, y3=y3, p=p)
    return x3, saved


def _ffn_bwd(dxn, x_in, y, zg, zu, act, mod, w_in, w_out, l, lg, lb, idx, S, emit):
    sh, sc, gate = idx
    dres, dy, dgate, dlg, dlb = ln_res_bwd(dxn, x_in, y, mod, lg, lb, gate, 0.5, S)
    dzg, dzu = swiglu_bwd(dy, w_out, l, zg, zu, S)
    dw_out = tn_mm(act, dy, FF // 2).reshape(N_CHIPS, FF // N_CHIPS, D)
    dw_in = tn_mm_mod_shards(x_in, mod, dzg, dzu, sh, sc, S)
    mod = mod + emit(dw_in, dw_out)
    dx, dsh, dsc = nt_mod_bwd([dzg, dzu], w_in, [0, 2], x_in, mod, dres, sc, S, FF // 2, l)
    return dx, dw_in, dw_out, dlg, dlb, {sh: dsh, sc: dsc, gate: dgate}, mod


def layer_bwd(dx3, mod, sv, tabs, S, emit):
    cos_q, sin_q, cs_k, place = tabs
    p = sv["p"]
    l = p["wl"]
    g = {}
    dm = {}

    def emit_ffn(part):
        def f(dw_in, dw_out):
            g[part + "_in"], g[part + "_out"] = dw_in, dw_out
            return emit(part, g)
        return f

    dx2, _, _, dlg2, dlb2, d, mod = _ffn_bwd(
        dx3, sv["x2"], sv["y3"], sv["zg3"], sv["zu3"], sv["act3"], mod, p["ffn2_in"], p["ffn2_out"], l,
        p["ln_g"][2:3], p["ln_b"][2:3], (6, 7, 8), S, emit_ffn("ffn2"))
    dm.update(d)
    dres, dy2, dm[5], dlg1, dlb1 = ln_res_bwd(dx2, sv["x1"], sv["y2"], mod, p["ln_g"][1:2], p["ln_b"][1:2], 5, 1.0, S)
    dcat = nt_plain(dy2, p["mix_out"])
    g["mix_out"] = tn_mm(sv["cat"], dy2, NCAT // 2)
    proj = sv["proj"]
    d_a, g["lb"], g["ng"] = hgrn_bwd(proj, sv["states"], dcat, p["lb"], p["ng"], S)
    dq_c, dk_c, dv_c, dcum = attn_bwd(proj, C_CQ // HP, proj, C_CK // HP, proj, C_CV // HP, sv["cum"], dcat,
                                      O_C // HP, HD ** -0.5, S, BF16)
    B = proj.shape[0] // S
    dcum = jnp.concatenate([dcum.reshape(B, HEADS, S), jnp.zeros((B, 8 - HEADS, S), F32)], axis=1)
    d_cf, g["bcol"] = fox_gate_bwd(proj, p["bcol"], dcum, S)
    dq_b, dk_b, dv_b = attn_bwd(sv["q_b"], 0, sv["k_b"], 0, sv["v_b"], 0, None, dcat, O_B // HP,
                                (B_NOPE + B_ROPE) ** -0.5, S, F32)
    d_b, g["gq"], g["gkv"], g["wq"], g["wkv"] = mla_pre_bwd(
        proj, p["gq"], p["gkv"], p["wq"], p["wkv"], place, cos_q, sin_q, cs_k, dq_b, dk_b, dv_b, S)
    d_d, g["g_lg"], g["g_lb"], g["ws"], g["bs"] = gmlp_bwd(proj, p["g_lg"], p["g_lb"], p["ws"], p["bs"], dcat)
    dproj = jnp.concatenate([d_a, d_b, dq_c, dk_c, dv_c, d_d, d_cf, jnp.zeros_like(d_cf)], axis=1)
    g["mix_in"] = tn_mm_mod(sv["x1"], mod, dproj, 3, 4, S, NP_TILE)
    mod = mod + emit("mix", g)
    dx1, dm[3], dm[4] = nt_mod_bwd([dproj], p["mix_in"], [0], sv["x1"], mod, dres, 4, S, NP_TILE)
    last = []

    def emit_last(dw_in, dw_out):
        last.append(emit_ffn("ffn1")(dw_in, dw_out))
        return last[0]

    dx0, _, _, dlg0, dlb0, d, mod = _ffn_bwd(
        dx1, sv["x"], sv["y1"], sv["zg1"], sv["zu1"], sv["act1"], mod, p["ffn1_in"], p["ffn1_out"], l,
        p["ln_g"][0:1], p["ln_b"][0:1], (0, 1, 2), S, emit_last)
    dm.update(d)
    g["ln_g"] = jnp.concatenate([dlg0, dlg1, dlg2], axis=0)
    g["ln_b"] = jnp.concatenate([dlb0, dlb1, dlb2], axis=0)
    dmod = jnp.concatenate([dm[i] for i in range(N_MOD)], axis=1)
    return dx0, dmod, g, last[0]


def local_step(x, c8, tgt, get, lb_logits, S, emit=None):
    B = x.shape[0] // S
    tabs = rope_tables(S)
    lb_all = lb_fwd(lb_logits)
    mods, saved = [], []
    h = x
    for l in range(DEPTH):
        pa = get(l, "ada", h)
        mod = mod_fwd(c8, pa["ada_w"], pa["wl"], pa["ada_b"])[0:B].reshape(B, N_MOD, D)

        def get_l(part, after, l=l):
            p = dict(get(l, part, after))
            if part == "mix":
                p["lb"] = lb_all[l:l + 1]
            return p

        h, sv = layer_fwd(h, mod, get_l, tabs, S)
        mods.append(mod)
        saved.append(sv)
    loss_tile, dh = loss_head(h, tgt)
    grads, dmods, dlb = [None] * DEPTH, [None] * DEPTH, [None] * DEPTH
    tie = jnp.zeros((), F32)
    for l in reversed(range(DEPTH)):
        emit_l = (lambda part, g: jnp.zeros((), F32)) if emit is None else functools.partial(emit, l)
        dh, dmods[l], grads[l], tie = layer_bwd(dh, mods[l] + tie, saved[l], tabs, S, emit_l)
        dlb[l] = grads[l].pop("lb")
    d_logits = lb_bwd(lb_logits, jnp.concatenate(dlb, axis=0))
    return loss_tile, dh, dmods, grads, d_logits


ANY = pl.BlockSpec(memory_space=pl.ANY)


def _place():
    x, y, c = lax.axis_index("x"), lax.axis_index("y"), lax.axis_index("c")
    chips = [(1 - x, y), (x, 1 - y), (1 - x, 1 - y)]
    return x, y, c, chips


def _rcopy(src, dst, sems, k, to):
    send_sems, recv_sems = sems
    return pltpu.make_async_remote_copy(src_ref=src, dst_ref=dst, send_sem=send_sems.at[k], recv_sem=recv_sems.at[k],
                                        device_id=to, device_id_type=MESH)


def _dma_sems(n_remote, n_local):
    return [pltpu.SemaphoreType.DMA((n_remote,)), pltpu.SemaphoreType.DMA((n_remote,)), pltpu.SemaphoreType.DMA((n_local,))]


def own_slot(src, chip):
    L = src.shape[0]
    return lax.dynamic_update_slice(lax.empty((L, N_CHIPS) + src.shape[1:], src.dtype), src[:, None], (0, chip, 0, 0))


HBM_SPEC = pl.BlockSpec(memory_space=pltpu.HBM)
SEM_SPEC = pl.BlockSpec(memory_space=pltpu.SEMAPHORE)
DATAFLOW = pltpu.SideEffectType.DATAFLOW_SIDE_EFFECTING


def _split_start(srcs, lands, copies, n_copies, dep, name):
    n, m = len(srcs), len(lands)

    def body(*refs):
        ins = refs[:n + m]
        send_sems, recv_sems = refs[n + m + 1], refs[n + m + 2]
        token = refs[-1]
        for k, (src, dst, to) in enumerate(copies(ins[:n], ins[n:], _place())):
            pltpu.make_async_remote_copy(src_ref=src, dst_ref=dst, send_sem=send_sems.at[k], recv_sem=recv_sems.at[k],
                                         device_id=to, device_id_type=MESH).start()
        token[...] = jnp.zeros_like(token)

    arrs = list(srcs) + list(lands)
    outs = pl.pallas_call(
        body, name=name,
        out_shape=(pltpu.SemaphoreType.DMA((n_copies,)), pltpu.SemaphoreType.DMA((n_copies,)),
                   *[pltpu.HBM(a.shape, a.dtype) for a in arrs], _sds((8, 128), F32)),
        in_specs=[HBM_SPEC] * (n + m) + [ANY],
        out_specs=(SEM_SPEC, SEM_SPEC, *[HBM_SPEC] * (n + m), pl.BlockSpec(memory_space=pltpu.VMEM)),
        input_output_aliases={i: 2 + i for i in range(n + m)},
        compiler_params=pltpu.CompilerParams(has_side_effects=DATAFLOW),
    )(*[pltpu.with_memory_space_constraint(a, pltpu.HBM) for a in arrs], dep)
    return outs[0], outs[1], list(outs[2:2 + n]), list(outs[2 + n:2 + n + m]), outs[-1]


def _split_wait(handle, arrivals, after, name):
    send_sems, recv_sems, srcs, lands, _ = handle
    n, m = len(srcs), len(lands)

    def body(*refs):
        ins = refs[:n + m]
        send_sems, recv_sems = refs[n + m], refs[n + m + 1]
        x, y, c, chips = place = _place()
        for k, (src, dst) in enumerate(arrivals(ins[:n], ins[n:], place)):
            cp = pltpu.make_async_remote_copy(src_ref=src, dst_ref=dst, send_sem=send_sems.at[k], recv_sem=recv_sems.at[k],
                                              device_id=(x, y, 1 - c), device_id_type=MESH)
            cp.wait_send()
            cp.wait_recv()

    arrs = list(srcs) + list(lands)
    outs = pl.pallas_call(
        body, name=name, out_shape=[pltpu.HBM(a.shape, a.dtype) for a in arrs],
        in_specs=[HBM_SPEC] * (n + m) + [SEM_SPEC, SEM_SPEC, ANY], out_specs=[HBM_SPEC] * (n + m),
        input_output_aliases={i: i for i in range(n + m)},
        compiler_params=pltpu.CompilerParams(has_side_effects=DATAFLOW),
    )(*arrs, send_sems, recv_sems, after)
    return list(outs[:n]), list(outs[n:])


def _ag_part(ref, k, hc):
    rh = ref.shape[2] // 2
    return ref.at[:, k, pl.ds(hc * rh, rh), :]


def ag_start(srcs, lands, dep, name):
    def copies(s, d, place):
        x, y, c, chips = place
        out = []
        for j, (px, py) in enumerate(chips):
            for i in range(len(s)):
                rh = s[i].shape[1] // 2
                out.append((s[i].at[:, pl.ds(c * rh, rh), :], _ag_part(d[i], 2 * x + y, c), (px, py, c)))
        return out

    return _split_start(srcs, lands, copies, 3 * len(srcs), dep, name)


def ag_wait(handle, after, name):
    def arrivals(s, d, place):
        x, y, c, chips = place
        out = []
        for j, (px, py) in enumerate(chips):
            for i in range(len(s)):
                rh = s[i].shape[1] // 2
                out.append((s[i].at[:, pl.ds(c * rh, rh), :], _ag_part(d[i], 2 * px + py, c)))
        return out

    return _split_wait(handle, arrivals, after, name)


def ag_forward(lands, name):
    n = len(lands)

    def body(*refs):
        bufs, token = refs[n:2 * n], refs[2 * n]
        send_sems, recv_sems = refs[2 * n + 1:]
        x, y, c, chips = _place()
        sems = (send_sems, recv_sems)
        token[...] = jnp.zeros_like(token)
        cps = []
        for j, (px, py) in enumerate(chips):
            for i in range(n):
                part = _ag_part(bufs[i], 2 * px + py, c)
                cps.append(_rcopy(part, part, sems, 3 * i + j, (x, y, 1 - c)))
        for cp in cps:
            cp.start()
        for j, (px, py) in enumerate(chips):
            for i in range(n):
                part = _ag_part(bufs[i], 2 * px + py, 1 - c)
                _rcopy(part, part, sems, 3 * i + j, (x, y, 1 - c)).wait_recv()
        for cp in cps:
            cp.wait_send()

    outs = pl.pallas_call(
        body, name=name, out_shape=[_sds(a.shape, a.dtype) for a in lands] + [_sds((8, 128), F32)],
        in_specs=[ANY] * n, out_specs=[ANY] * n + [pl.BlockSpec(memory_space=pltpu.VMEM)],
        input_output_aliases={i: i for i in range(n)}, scratch_shapes=_dma_sems(3 * n, 1)[:2])(*lands)
    return list(outs[:n]), outs[n]


def rs_start(hs, lands, dep, name):
    def copies(s, d, place):
        x, y, c, chips = place
        return [(s[i].at[2 * px + py], d[i].at[2 * x + y], (px, py, c)) for j, (px, py) in enumerate(chips) for i in range(len(s))]

    return _split_start(hs, lands, copies, 3 * len(hs), dep, name)


def rs_wait(handle, after, name):
    def arrivals(s, d, place):
        x, y, c, chips = place
        return [(s[i].at[2 * px + py], d[i].at[2 * px + py]) for j, (px, py) in enumerate(chips) for i in range(len(s))]

    return _split_wait(handle, arrivals, after, name)


def sibling_swap(arrs, name):
    n = len(arrs)
    rh = [a.shape[1] // 2 for a in arrs]

    def body(*refs):
        srcs, outs = refs[:n], refs[n:2 * n]
        send_sems, recv_sems = refs[2 * n:]
        x, y, c, _ = _place()
        cps = [_rcopy(srcs[i].at[:, pl.ds((1 - c) * rh[i], rh[i]), :], outs[i], (send_sems, recv_sems), i, (x, y, 1 - c))
               for i in range(n)]
        for cp in cps:
            cp.start()
        for cp in cps:
            cp.wait()

    return pl.pallas_call(
        body, name=name, out_shape=[_sds((N_CHIPS, r, a.shape[2]), a.dtype) for a, r in zip(arrs, rh)],
        in_specs=[ANY] * n, out_specs=[ANY] * n, scratch_shapes=_dma_sems(n, 1)[:2])(*arrs)


def sum_into(land, base, l, core, name):
    _, rh, C = land.shape
    tr = _row_tile(rh, C, N_CHIPS, mult=16)
    nr = rh // tr

    def body(core_ref, land_ref, base_ref, o_ref):
        acc = land_ref[0].astype(F32)
        for k in range(1, N_CHIPS):
            acc = acc + land_ref[k].astype(F32)
        o_ref[...] = acc

    grid_spec = pltpu.PrefetchScalarGridSpec(
        num_scalar_prefetch=1, grid=(nr,),
        in_specs=[pl.BlockSpec((N_CHIPS, tr, C), lambda r, core_ref: (0, r, 0)), ANY],
        out_specs=pl.BlockSpec((None, tr, C), lambda r, core_ref: (l, core_ref[0] * nr + r, 0)))
    return pl.pallas_call(body, name=name, grid_spec=grid_spec, out_shape=_sds(base.shape, base.dtype),
                          input_output_aliases={2: 0}, compiler_params=_cparams(VMEM_BIG))(
        core.reshape(1).astype(jnp.int32), land, base)


def sibling_join(bases, name):
    n = len(bases)

    def body(*refs):
        bufs = refs[n:2 * n]
        send_sems, recv_sems = refs[2 * n:]
        x, y, c, _ = _place()
        sems = (send_sems, recv_sems)

        def half(i, hc):
            rh = bufs[i].shape[1] // 2
            return bufs[i].at[:, pl.ds(hc * rh, rh), :]

        sends = [_rcopy(half(i, c), half(i, c), sems, i, (x, y, 1 - c)) for i in range(n)]
        for cp in sends:
            cp.start()
        for i in range(n):
            _rcopy(half(i, 1 - c), half(i, 1 - c), sems, i, (x, y, 1 - c)).wait_recv()
        for cp in sends:
            cp.wait_send()

    return pl.pallas_call(
        body, name=name, out_shape=[_sds(b.shape, b.dtype) for b in bases], in_specs=[ANY] * n, out_specs=[ANY] * n,
        input_output_aliases={i: i for i in range(n)}, scratch_shapes=_dma_sems(n, 1)[:2])(*bases)


def ag_all(blk):
    M, C = blk.shape

    def body(x_ref, out_ref, send_sems, recv_sems, loc_sem):
        x, y, c, chips = _place()
        sems = (send_sems, recv_sems)
        me, sibling = (x, y, c), (x, y, 1 - c)

        def slot(px, py, pc):
            return out_ref.at[4 * px + 2 * py + pc]

        mine = pltpu.make_async_copy(x_ref, slot(*me), loc_sem)
        mine.start()
        first = [_rcopy(x_ref, slot(*me), sems, 0, sibling)]
        first += [_rcopy(x_ref, slot(*me), sems, 1 + j, (*chip, c)) for j, chip in enumerate(chips)]
        for cp in first:
            cp.start()
        passed = [_rcopy(slot(*chip, c), slot(*chip, c), sems, 4 + j, sibling) for j, chip in enumerate(chips)]
        for j, chip in enumerate(chips):
            _rcopy(slot(*chip, c), slot(*chip, c), sems, 1 + j, me).wait_recv()
            passed[j].start()
        _rcopy(slot(*sibling), slot(*sibling), sems, 0, me).wait_recv()
        for j, chip in enumerate(chips):
            _rcopy(slot(*chip, 1 - c), slot(*chip, 1 - c), sems, 4 + j, me).wait_recv()
        for cp in first + passed:
            cp.wait_send()
        mine.wait()

    return pl.pallas_call(
        body, name="ag_all", out_shape=_sds((8, M, C), blk.dtype),
        in_specs=[pl.BlockSpec(memory_space=pltpu.VMEM)], out_specs=pl.BlockSpec(memory_space=pltpu.VMEM),
        scratch_shapes=[pltpu.SemaphoreType.DMA((7,)), pltpu.SemaphoreType.DMA((7,)), pltpu.SemaphoreType.DMA(())],
        compiler_params=_cparams(VMEM_BIG))(blk)


WEIGHTS = ["ada_w", "ada_b", "ln_g", "ln_b", "ffn1_w_in", "ffn1_w_out", "ffn2_w_in", "ffn2_w_out", "mix_w_in", "mix_w_out",
           "hgrn_lb_logits", "hgrn_norm_g", "mla_q_norm_g", "mla_kv_norm_g", "mla_w_uq", "mla_w_ukv", "fox_b_f",
           "gmlp_ln_g", "gmlp_ln_b", "gmlp_w_s", "gmlp_b_s"]
SMALL = ["hgrn_lb_logits", "hgrn_norm_g", "mla_q_norm_g", "mla_kv_norm_g", "fox_b_f", "gmlp_ln_g", "gmlp_ln_b",
         "gmlp_w_s", "gmlp_b_s", "ln_g", "ln_b"]
GATHERED = ["ada_w", "ffn1_w_in", "ffn1_w_out", "ffn2_w_in", "ffn2_w_out", "mix_w_in", "mix_w_out", "mla_w_uq", "mla_w_ukv"]
REDUCED = GATHERED[1:]


def _col_shards(a):
    cols = a.shape[1] // N_CHIPS
    return jnp.stack([a[:, k * cols:(k + 1) * cols] for k in range(N_CHIPS)])


def add_kept_half(a, got, core, name):
    _, R, C = a.shape
    rh = R // 2
    tr = _row_tile(rh, C, mult=16)
    nr = rh // tr

    def body(core_ref, a_ref, b_ref, o_ref):
        o_ref[...] = (a_ref[...].astype(F32) + b_ref[...].astype(F32)).astype(o_ref.dtype)

    half = pl.BlockSpec((None, tr, C), lambda k, r, core_ref: (k, r, 0))
    grid_spec = pltpu.PrefetchScalarGridSpec(
        num_scalar_prefetch=1, grid=(N_CHIPS, nr),
        in_specs=[pl.BlockSpec((None, tr, C), lambda k, r, core_ref: (k, core_ref[0] * nr + r, 0)), half],
        out_specs=half)
    return pl.pallas_call(body, name=name, grid_spec=grid_spec, out_shape=_sds((N_CHIPS, rh, C), BF16),
                          compiler_params=_cparams(VMEM_BIG))(core.reshape(1).astype(jnp.int32), a, got)


def _rows(parts, n_rows, dtype):
    flat = jnp.concatenate([p.reshape(-1) for p in parts])
    pad = n_rows * D - flat.shape[0]
    return jnp.concatenate([flat, jnp.zeros((pad,), dtype)]).reshape(n_rows, D)


def _take(flat, shapes):
    out, o = [], 0
    for shp in shapes:
        n = int(np.prod(shp))
        out.append(flat[o:o + n].reshape(shp))
        o += n
    return out


def _round_up(n, m):
    return -(-n // m) * m


def pack_small(w):
    parts = [w[n][l] for l in range(DEPTH) for n in SMALL]
    n = sum(int(np.prod(p.shape)) for p in parts)
    return _rows(parts, _round_up(-(-n // D), 8), F32)


def unpack_small(pk, like):
    shapes = [like[n].shape[1:] for l in range(DEPTH) for n in SMALL]
    pieces = _take(pk.reshape(-1), shapes)
    names = [n for l in range(DEPTH) for n in SMALL]
    return {n: jnp.stack([p for p, m in zip(pieces, names) if m == n]) for n in SMALL}


def kernel(x, c, ada_w, ada_b, ln_g, ln_b, ffn1_w_in, ffn1_w_out, ffn2_w_in, ffn2_w_out, mix_w_in, mix_w_out, hgrn_lb_logits, hgrn_norm_g, mla_q_norm_g, mla_kv_norm_g, mla_w_uq, mla_w_ukv, fox_b_f, gmlp_ln_g, gmlp_ln_b, gmlp_w_s, gmlp_b_s, loss_target, m_ada_w, m_ada_b, m_ln_g, m_ln_b, m_ffn1_w_in, m_ffn1_w_out, m_ffn2_w_in, m_ffn2_w_out, m_mix_w_in, m_mix_w_out, m_hgrn_lb_logits, m_hgrn_norm_g, m_mla_q_norm_g, m_mla_kv_norm_g, m_mla_w_uq, m_mla_w_ukv, m_fox_b_f, m_gmlp_ln_g, m_gmlp_ln_b, m_gmlp_w_s, m_gmlp_b_s, v_ada_w, v_ada_b, v_ln_g, v_ln_b, v_ffn1_w_in, v_ffn1_w_out, v_ffn2_w_in, v_ffn2_w_out, v_mix_w_in, v_mix_w_out, v_hgrn_lb_logits, v_hgrn_norm_g, v_mla_q_norm_g, v_mla_kv_norm_g, v_mla_w_uq, v_mla_w_ukv, v_fox_b_f, v_gmlp_ln_g, v_gmlp_ln_b, v_gmlp_w_s, v_gmlp_b_s):
    w = dict(zip(WEIGHTS, (ada_w, ada_b, ln_g, ln_b, ffn1_w_in, ffn1_w_out, ffn2_w_in, ffn2_w_out, mix_w_in, mix_w_out, hgrn_lb_logits, hgrn_norm_g, mla_q_norm_g, mla_kv_norm_g, mla_w_uq, mla_w_ukv, fox_b_f, gmlp_ln_g, gmlp_ln_b, gmlp_w_s, gmlp_b_s)))
    m = dict(zip(WEIGHTS, (m_ada_w, m_ada_b, m_ln_g, m_ln_b, m_ffn1_w_in, m_ffn1_w_out, m_ffn2_w_in, m_ffn2_w_out, m_mix_w_in, m_mix_w_out, m_hgrn_lb_logits, m_hgrn_norm_g, m_mla_q_norm_g, m_mla_kv_norm_g, m_mla_w_uq, m_mla_w_ukv, m_fox_b_f, m_gmlp_ln_g, m_gmlp_ln_b, m_gmlp_w_s, m_gmlp_b_s)))
    v = dict(zip(WEIGHTS, (v_ada_w, v_ada_b, v_ln_g, v_ln_b, v_ffn1_w_in, v_ffn1_w_out, v_ffn2_w_in, v_ffn2_w_out, v_mix_w_in, v_mix_w_out, v_hgrn_lb_logits, v_hgrn_norm_g, v_mla_q_norm_g, v_mla_kv_norm_g, v_mla_w_uq, v_mla_w_ukv, v_fox_b_f, v_gmlp_ln_g, v_gmlp_ln_b, v_gmlp_w_s, v_gmlp_b_s)))
    Bl, S, _ = x.shape
    T = Bl * S
    core = lax.axis_index("c")
    chip = 2 * lax.axis_index("x") + lax.axis_index("y")

    def shard(key):
        n, l = key
        if n == "ln":
            return jnp.concatenate([ln_g[l:l + 1], ln_b[l:l + 1], jnp.zeros((1, 2, D // N_CHIPS), F32)], axis=1)
        return w[n][l:l + 1].astype(BF16)

    mixers = ["mix_w_in", "mix_w_out", "mla_w_uq", "mla_w_ukv"]
    groups = [[("ada_w", 0), ("ffn1_w_in", 0), ("ffn1_w_out", 0), ("ln", 0)],
              [(n, 0) for n in mixers + ["ffn2_w_in", "ffn2_w_out"]],
              [(n, 1) for n in GATHERED + ["ln"]]]
    srcs = [[shard(k) for k in grp] for grp in groups]
    lands = [[own_slot(s, chip) for s in srcs[0]]]
    handle0 = ag_start(srcs[0], lands[0], jnp.zeros((8, 128), F32), "ag_start_0")
    chip_later = chip + handle0[-1][0, 0].astype(jnp.int32)
    lands += [[own_slot(s, chip_later) for s in grp] for grp in srcs[1:]]
    first, token = ag_forward(ag_wait(handle0, lands[2][0], "ag_wait_0")[1], "ag_forward_0")
    have = dict(zip(groups[0], first))
    handles = {}
    for gi in (1, 2):
        handles[gi] = ag_start(srcs[gi], lands[gi], token, "ag_start_%d" % gi)
        token = handles[gi][-1]
    c8 = jnp.concatenate([c, jnp.zeros((8 - Bl, D), F32)], axis=0)
    c8 = c8 + token[0, 0]

    def cat_cols(a):
        return jnp.concatenate([a[0, k] for k in range(N_CHIPS)], axis=1)

    def get(l, part, after):
        gi = 2 if l == 1 else (0 if part in ("ada", "ffn1") else 1)
        if gi in handles:
            arrived, _ = ag_forward(ag_wait(handles.pop(gi), after, "ag_wait_%d" % gi)[1], "ag_forward_%d" % gi)
            have.update(zip(groups[gi], arrived))
        if part == "ada":
            return dict(ada_w=have[("ada_w", l)], wl=0, ada_b=ada_b[l][None])
        if part == "ffn1":
            ln_full = jnp.moveaxis(have[("ln", l)][0], 0, 1).reshape(8, D)
            return dict(ffn1_in=have[("ffn1_w_in", l)], ffn1_out=have[("ffn1_w_out", l)], wl=0,
                        ln_g=ln_full[0:3], ln_b=ln_full[3:6])
        if part == "ffn2":
            return dict(ffn2_in=have[("ffn2_w_in", l)], ffn2_out=have[("ffn2_w_out", l)])
        return dict(
            mix_in=mix_in_ext(cat_cols(have[("mix_w_in", l)])), mix_out=mix_out_ext(have[("mix_w_out", l)].reshape(D, D)),
            wq=uq_ext(cat_cols(have[("mla_w_uq", l)])).astype(F32), wkv=ukv_ext(cat_cols(have[("mla_w_ukv", l)])).astype(F32),
            ng=hgrn_norm_g[l][None], gq=mla_q_norm_g[l][None], gkv=mla_kv_norm_g[l][None],
            bcol=jnp.concatenate([fox_b_f[l], jnp.zeros((8 - HEADS,), F32)])[:, None],
            g_lg=gmlp_ln_g[l][None], g_lb=gmlp_ln_b[l][None], ws=gmlp_w_s[l], bs=gmlp_b_s[l][:, :, None])

    pending = []

    def emit(l, part, g):
        if part == "mix":
            names = mixers
            by_chip = [_col_shards(mix_in_unext(g["mix_in"])), mix_out_unext(g["mix_out"]).reshape(N_CHIPS, D // N_CHIPS, D),
                       _col_shards(uq_unext(g["wq"])).astype(BF16), _col_shards(ukv_unext(g["wkv"])).astype(BF16)]
        else:
            names = [part + "_w_in", part + "_w_out"]
            by_chip = [g[part + "_in"], g[part + "_out"]]
        tag = "%d_%s" % (l, part)
        got = sibling_swap(by_chip, "sibling_swap_" + tag)
        chip_sum = [add_kept_half(a, r, core, "add_sibling") for a, r in zip(by_chip, got)]
        zones = [lax.dynamic_update_slice(lax.empty(h.shape, h.dtype), lax.dynamic_slice_in_dim(h, chip, 1, axis=0), (chip, 0, 0))
                 for h in chip_sum]
        handle = rs_start(chip_sum, zones, chip_sum[0], "rs_start_" + tag)
        pending.append((l, names, handle, tag))
        return handle[-1][0, 0]

    loss_tile, dx, dmods, grads, d_logits = local_step(
        x.reshape(T, D), c8, loss_target.reshape(T, D), get, hgrn_lb_logits, S, emit)
    loss = lax.psum(loss_tile[0, 0], ("x", "y", "c"))

    small_g = {"hgrn_lb_logits": d_logits,
               "hgrn_norm_g": jnp.stack([grads[l]["ng"][0] for l in range(DEPTH)]),
               "mla_q_norm_g": jnp.stack([grads[l]["gq"][0] for l in range(DEPTH)]),
               "mla_kv_norm_g": jnp.stack([grads[l]["gkv"][0] for l in range(DEPTH)]),
               "fox_b_f": jnp.stack([grads[l]["bcol"][0:HEADS, 0] for l in range(DEPTH)]),
               "gmlp_ln_g": jnp.stack([grads[l]["g_lg"][0] for l in range(DEPTH)]),
               "gmlp_ln_b": jnp.stack([grads[l]["g_lb"][0] for l in range(DEPTH)]),
               "gmlp_w_s": jnp.stack([grads[l]["ws"] for l in range(DEPTH)]),
               "gmlp_b_s": jnp.stack([grads[l]["bs"][:, :, 0] for l in range(DEPTH)])}
    small_g["ln_g"] = jnp.stack([grads[l]["ln_g"] for l in range(DEPTH)])
    small_g["ln_b"] = jnp.stack([grads[l]["ln_b"] for l in range(DEPTH)])
    pk_small = pack_small(small_g)
    n_small = pk_small.shape[0]
    extras = [dmods[l] for l in range(DEPTH)] + [c]
    n_extra = _round_up(-(-sum(int(np.prod(e.shape)) for e in extras) // D), 8)
    gathered = ag_all(jnp.concatenate([pk_small, _rows(extras, n_extra, F32)], axis=0))
    g_small = unpack_small(sum_slots(gathered[:, 0:n_small], 8, "sum_small"), small_g)
    ext = gathered[:, n_small:].reshape(8, -1)
    n_dmod = DEPTH * Bl * N_MOD * D
    dmod_all = ext[:, 0:n_dmod].reshape(8, DEPTH, Bl, N_MOD * D)
    c_all = ext[:, n_dmod:n_dmod + Bl * D].reshape(8 * Bl, D)
    g_ada_w, g_ada_b = [], []
    ncol = N_MOD * D // N_CHIPS
    for l in range(DEPTH):
        dm = dmod_all[:, l].reshape(8 * Bl, N_MOD * D)
        g_ada_w.append(ada_grad(c_all, lax.dynamic_slice_in_dim(dm, chip * ncol, ncol, axis=1)))
        g_ada_b.append(sum_slots(dm.reshape(8 * Bl, N_MOD, D), 8 * Bl, "sum_ada_b").reshape(N_MOD * D))
    g_ada_w, g_ada_b = jnp.stack(g_ada_w), jnp.stack(g_ada_b)

    red = {n: lax.empty(w[n].shape, F32) for n in REDUCED}

    def arrive(entry, after):
        l, names, handle, tag = entry
        for n, land in zip(names, rs_wait(handle, after, "rs_wait_" + tag)[1]):
            red[n] = sum_into(land, red[n], l, core, "sum_chips")

    for entry in pending[:-1]:
        arrive(entry, dx)
    late = pending[-1][1]
    early = [n for n in REDUCED if n not in late]
    grad = dict(zip(early, sibling_join([red[n] for n in early], "sibling_join_a")))
    grad.update(g_small)
    grad["ada_w"], grad["ada_b"] = g_ada_w, g_ada_b
    for n in ("ln_g", "ln_b"):
        grad[n] = lax.dynamic_slice_in_dim(g_small[n], chip * (D // N_CHIPS), D // N_CHIPS, axis=2)
    out = {"grad": grad, "delta": {}, "new_m": {}, "new_v": {}}

    def update(n):
        shp = w[n].shape
        two_d = (-1, shp[-1])
        res = adamw(w[n].reshape(two_d), grad[n].reshape(two_d), m[n].reshape(two_d), v[n].reshape(two_d), "adamw_" + n,
                    echo=n in REDUCED)
        grad[n] = (res[3] if n in REDUCED else grad[n]).reshape(shp)
        for key, r in zip(("delta", "new_m", "new_v"), res):
            out[key][n] = r.reshape(shp)

    for n in WEIGHTS:
        if n not in late:
            update(n)
    arrive(pending[-1], out["delta"]["ffn2_w_in"])
    grad.update(zip(late, sibling_join([red[n] for n in late], "sibling_join_b")))
    for n in late:
        update(n)
    outs = [loss, dx.reshape(Bl, S, D)]
    for key in ("grad", "delta", "new_m", "new_v"):
        outs += [out[key][n] for n in WEIGHTS]
    return tuple(outs)
```

```python
import functools

import jax
import jax.numpy as jnp
import numpy as np
from jax import lax
from jax.experimental import pallas as pl
from jax.experimental.pallas import tpu as pltpu

F32, BF16 = jnp.float32, jnp.bfloat16
MESH = pl.DeviceIdType.MESH

N_CHIPS = 4
D = 1024
DEPTH = 2
FF = 2816
N_MOD = 9
GW = 256
HEADS = 4
HD = 64
HP = 128
A_CHUNK = 16
LB_FLOOR = 1e-30
B_Q_LORA, B_KV_LORA, B_NOPE, B_ROPE = 256, 128, 64, 32
ROPE_THETA = 10000.0
D_CHUNK = 128
ALPHA = (2 * DEPTH) ** 0.25
LN_EPS = 1e-5
RMS_EPS = 1e-6
ADAM_LR, ADAM_B1, ADAM_B2, ADAM_EPS, ADAM_WD, ADAM_STEP = 0.001, 0.9, 0.999, 1e-08, 0.01, 10

NP = 3840
NP_TILE = 1920
C_A, C_B, C_CQ, C_CK, C_CV, C_D, C_CF = 0, 1024, 1536, 2048, 2560, 3072, 3584
NCAT = 1536
O_A, O_B, O_C, O_D = 0, 256, 768, 1280

VMEM_BIG = 48 << 20


def _cparams(vmem=None):
    return pltpu.CompilerParams(vmem_limit_bytes=vmem) if vmem else pltpu.CompilerParams()


def _sds(shape, dtype):
    return jax.ShapeDtypeStruct(tuple(shape), dtype)


@jax.custom_vjp
def _mm(a, w):
    return jnp.dot(a.astype(BF16), w.astype(BF16), preferred_element_type=F32)


def _mm_f(a, w):
    return _mm(a, w), (a, w)


def _mm_b(res, g):
    a, w = res
    gb = g.astype(BF16)
    da = lax.dot_general(gb, w.astype(BF16), (((1,), (1,)), ((), ())), preferred_element_type=F32)
    dw = lax.dot_general(a.astype(BF16), gb, (((0,), (0,)), ((), ())), preferred_element_type=F32)
    return da.astype(a.dtype), dw.astype(w.dtype)


_mm.defvjp(_mm_f, _mm_b)


@jax.custom_vjp
def _mm_nt(a, b):
    return lax.dot_general(a.astype(BF16), b.astype(BF16), (((1,), (1,)), ((), ())), preferred_element_type=F32)


def _mm_nt_f(a, b):
    return _mm_nt(a, b), (a, b)


def _mm_nt_b(res, g):
    a, b = res
    gb = g.astype(BF16)
    da = jnp.dot(gb, b.astype(BF16), preferred_element_type=F32)
    db = lax.dot_general(gb, a.astype(BF16), (((0,), (0,)), ((), ())), preferred_element_type=F32)
    return da.astype(a.dtype), db.astype(b.dtype)


_mm_nt.defvjp(_mm_nt_f, _mm_nt_b)


@jax.custom_vjp
def _mm_tn(a, b):
    return lax.dot_general(a.astype(BF16), b.astype(BF16), (((0,), (0,)), ((), ())), preferred_element_type=F32)


def _mm_tn_f(a, b):
    return _mm_tn(a, b), (a, b)


def _mm_tn_b(res, g):
    a, b = res
    gb = g.astype(BF16)
    da = lax.dot_general(b.astype(BF16), gb, (((1,), (1,)), ((), ())), preferred_element_type=F32)
    db = jnp.dot(a.astype(BF16), gb, preferred_element_type=F32)
    return da.astype(a.dtype), db.astype(b.dtype)


_mm_tn.defvjp(_mm_tn_f, _mm_tn_b)


def _split3(x):
    p1 = x.astype(BF16)
    r = x - p1.astype(F32)
    p2 = r.astype(BF16)
    return p1, p2, (r - p2.astype(F32)).astype(BF16)


@jax.custom_vjp
def _sel_r(x, sel):
    s = sel.astype(BF16)
    return sum(jnp.dot(p, s, preferred_element_type=F32) for p in _split3(x))


def _sel_r_f(x, sel):
    return _sel_r(x, sel), sel


def _sel_r_b(sel, g):
    s = sel.astype(BF16)
    dx = sum(lax.dot_general(p, s, (((1,), (1,)), ((), ())), preferred_element_type=F32) for p in _split3(g))
    return dx, jnp.zeros_like(sel)


_sel_r.defvjp(_sel_r_f, _sel_r_b)


@jax.custom_vjp
def _sel_l(sel, x):
    s = sel.astype(BF16)
    return sum(jnp.dot(s, p, preferred_element_type=F32) for p in _split3(x))


def _sel_l_f(sel, x):
    return _sel_l(sel, x), sel


def _sel_l_b(sel, g):
    s = sel.astype(BF16)
    dx = sum(lax.dot_general(s, p, (((0,), (0,)), ((), ())), preferred_element_type=F32) for p in _split3(g))
    return jnp.zeros_like(sel), dx


_sel_l.defvjp(_sel_l_f, _sel_l_b)


def _iota(shape, dim):
    return lax.broadcasted_iota(jnp.int32, shape, dim)


def _head_sum_mats():
    e = (_iota((GW, HP), 0) // HD == _iota((GW, HP), 1)).astype(F32)
    et = (_iota((HP, GW), 1) // HD == _iota((HP, GW), 0)).astype(F32)
    return e, et


def _modulate(x, mod_ref, sh, sc):
    return x * (1.0 + mod_ref[sc:sc + 1, :]) + mod_ref[sh:sh + 1, :]


def _ln_res(x, y, gate, lg, lb, gs):
    r = ALPHA * x + gs * (1.0 + gate) * y
    mu = jnp.mean(r, axis=-1, keepdims=True)
    var = jnp.mean(jnp.square(r - mu), axis=-1, keepdims=True)
    return (r - mu) * lax.rsqrt(var + LN_EPS) * lg + lb


def _rms(x, g):
    return x * lax.rsqrt(jnp.mean(x * x, axis=-1, keepdims=True) + RMS_EPS) * g


def _tile(n, pref):
    return pref if n % pref == 0 else n


def mod_fwd(c8, w, l, b):
    tn = w.shape[3]
    n = N_CHIPS * tn

    def body(c_ref, w_ref, b_ref, o_ref):
        h = jax.nn.silu(c_ref[...]).astype(BF16)
        o_ref[...] = jnp.dot(h, w_ref[...], preferred_element_type=F32) + b_ref[...]

    return pl.pallas_call(
        body, name="mod_fwd", grid=(N_CHIPS,),
        in_specs=[pl.BlockSpec((8, D), lambda j: (0, 0)), pl.BlockSpec((None, None, D, tn), lambda j: (l, j, 0, 0)),
                  pl.BlockSpec((1, tn), lambda j: (0, j))],
        out_specs=pl.BlockSpec((8, tn), lambda j: (0, j)), out_shape=_sds((8, n), F32),
        compiler_params=_cparams(VMEM_BIG))(c8, w, b)


def ffn_in_fwd(x, mod, w_in, l, sh, sc, S):
    T = x.shape[0]
    tm, tn = _tile(S, 512), FF // 2
    tpb, nj = S // tm, 2

    def body(x_ref, mod_ref, wg_ref, wu_ref, zg_ref, zu_ref, act_ref, h_ref):
        @pl.when(pl.program_id(1) == 0)
        def _():
            h_ref[...] = _modulate(x_ref[...], mod_ref, sh, sc).astype(BF16)
        g = jnp.dot(h_ref[...], wg_ref[...], preferred_element_type=F32)
        u = jnp.dot(h_ref[...], wu_ref[...], preferred_element_type=F32)
        zg_ref[...] = g.astype(BF16)
        zu_ref[...] = u.astype(BF16)
        act_ref[...] = (jax.nn.silu(g) * u).astype(BF16)

    return pl.pallas_call(
        body, name="ffn_in_fwd", grid=(T // tm, nj),
        in_specs=[pl.BlockSpec((tm, D), lambda i, j: (i, 0)),
                  pl.BlockSpec((None, N_MOD, D), lambda i, j: (i // tpb, 0, 0)),
                  pl.BlockSpec((None, None, D, tn), lambda i, j: (l, j, 0, 0)),
                  pl.BlockSpec((None, None, D, tn), lambda i, j: (l, j + nj, 0, 0))],
        out_specs=[pl.BlockSpec((tm, tn), lambda i, j: (i, j))] * 3,
        out_shape=[_sds((T, FF), BF16)] * 3,
        scratch_shapes=[pltpu.VMEM((tm, D), BF16)],
        compiler_params=_cparams(VMEM_BIG))(x, mod, w_in, w_in)


def mix_in_fwd(x, mod, w, sh, sc, S):
    T = x.shape[0]
    n = w.shape[1]
    tm, tn = _tile(S, 512), NP_TILE
    tpb = S // tm

    def body(x_ref, mod_ref, w_ref, o_ref, h_ref):
        @pl.when(pl.program_id(1) == 0)
        def _():
            h_ref[...] = _modulate(x_ref[...], mod_ref, sh, sc).astype(BF16)
        o_ref[...] = jnp.dot(h_ref[...], w_ref[...], preferred_element_type=F32)

    return pl.pallas_call(
        body, name="mix_in_fwd", grid=(T // tm, n // tn),
        in_specs=[pl.BlockSpec((tm, D), lambda i, j: (i, 0)),
                  pl.BlockSpec((None, N_MOD, D), lambda i, j: (i // tpb, 0, 0)),
                  pl.BlockSpec((D, tn), lambda i, j: (0, j))],
        out_specs=pl.BlockSpec((tm, tn), lambda i, j: (i, j)), out_shape=_sds((T, n), F32),
        scratch_shapes=[pltpu.VMEM((tm, D), BF16)],
        compiler_params=_cparams(VMEM_BIG))(x, mod, w)


def out_ln_fwd(act, w_out, x, mod, lg, lb, gate, gs, S, l=None):
    T, K = act.shape
    tm = _tile(S, 512)
    tpb = S // tm

    def body(a_ref, w_ref, x_ref, mod_ref, lg_ref, lb_ref, y_ref, xn_ref):
        y = jnp.dot(a_ref[...], w_ref[...].reshape(K, D), preferred_element_type=F32)
        y_ref[...] = y
        xn_ref[...] = _ln_res(x_ref[...], y, mod_ref[gate:gate + 1, :], lg_ref[...], lb_ref[...], gs)

    if l is None:
        w_spec = pl.BlockSpec((K, D), lambda i: (0, 0))
    else:
        w_spec = pl.BlockSpec((None, N_CHIPS, K // N_CHIPS, D), lambda i: (l, 0, 0, 0))
    return pl.pallas_call(
        body, name="out_ln_fwd", grid=(T // tm,),
        in_specs=[pl.BlockSpec((tm, K), lambda i: (i, 0)), w_spec,
                  pl.BlockSpec((tm, D), lambda i: (i, 0)),
                  pl.BlockSpec((None, N_MOD, D), lambda i: (i // tpb, 0, 0)),
                  pl.BlockSpec((1, D), lambda i: (0, 0)), pl.BlockSpec((1, D), lambda i: (0, 0))],
        out_specs=[pl.BlockSpec((tm, D), lambda i: (i, 0))] * 2,
        out_shape=[_sds((T, D), F32), _sds((T, D), F32)],
        compiler_params=_cparams(VMEM_BIG))(act, w_out, x, mod, lg, lb)


def ln_res_bwd(dxn, x, y, mod, lg, lb, gate, gs, S):
    T = x.shape[0]
    B = T // S
    tm = _tile(S, 512)
    tpb = S // tm

    def body(d_ref, x_ref, y_ref, mod_ref, lg_ref, lb_ref, dx_ref, dy_ref, dg_ref, dlg_ref, dlb_ref):
        i = pl.program_id(0)
        f = functools.partial(_ln_res, gs=gs)
        _, vjp = jax.vjp(f, x_ref[...], y_ref[...], mod_ref[gate:gate + 1, :], lg_ref[...], lb_ref[...])
        dx, dy, dg, dlg, dlb = vjp(d_ref[...])
        dx_ref[...] = dx
        dy_ref[...] = dy.astype(BF16)

        @pl.when(i % tpb == 0)
        def _():
            dg_ref[...] = jnp.zeros_like(dg_ref)

        @pl.when(i == 0)
        def _():
            dlg_ref[...] = jnp.zeros_like(dlg_ref)
            dlb_ref[...] = jnp.zeros_like(dlb_ref)

        dg_ref[...] += dg
        dlg_ref[...] += dlg
        dlb_ref[...] += dlb

    tok = pl.BlockSpec((tm, D), lambda i: (i, 0))
    vec = pl.BlockSpec((1, D), lambda i: (0, 0))
    return pl.pallas_call(
        body, name="ln_res_bwd", grid=(T // tm,),
        in_specs=[tok, tok, tok, pl.BlockSpec((None, N_MOD, D), lambda i: (i // tpb, 0, 0)), vec, vec],
        out_specs=[tok, tok, pl.BlockSpec((None, 1, D), lambda i: (i // tpb, 0, 0)), vec, vec],
        out_shape=[_sds((T, D), F32), _sds((T, D), BF16), _sds((B, 1, D), F32), _sds((1, D), F32), _sds((1, D), F32)],
        compiler_params=_cparams(VMEM_BIG))(dxn, x, y, mod, lg, lb)


def swiglu_bwd(dy, w_out, l, zg, zu, S):
    T = dy.shape[0]
    tm, tn = _tile(S, 512), FF // 2

    def body(dy_ref, w_ref, zg_ref, zu_ref, dg_ref, du_ref):
        da = lax.dot_general(dy_ref[...], w_ref[...].reshape(tn, D), (((1,), (1,)), ((), ())), preferred_element_type=F32)
        g, u = zg_ref[...].astype(F32), zu_ref[...].astype(F32)
        sg = jax.nn.sigmoid(g)
        dg_ref[...] = (da * u * (sg * (1.0 + g * (1.0 - sg)))).astype(BF16)
        du_ref[...] = (da * (g * sg)).astype(BF16)

    zt = pl.BlockSpec((tm, tn), lambda i, j: (i, j))
    return pl.pallas_call(
        body, name="swiglu_bwd", grid=(T // tm, FF // tn),
        in_specs=[pl.BlockSpec((tm, D), lambda i, j: (i, 0)),
                  pl.BlockSpec((None, 2, FF // N_CHIPS, D), lambda i, j: (l, j, 0, 0)), zt, zt],
        out_specs=[zt, zt], out_shape=[_sds((T, FF), BF16), _sds((T, FF), BF16)],
        compiler_params=_cparams(VMEM_BIG))(dy, w_out, zg, zu)


def nt_plain(dy, w):
    T = dy.shape[0]
    K = w.shape[0]
    tm = _tile(T, 512)

    def body(dy_ref, w_ref, o_ref):
        o_ref[...] = lax.dot_general(dy_ref[...], w_ref[...], (((1,), (1,)), ((), ())), preferred_element_type=F32)

    return pl.pallas_call(
        body, name="nt_plain", grid=(T // tm,),
        in_specs=[pl.BlockSpec((tm, D), lambda i: (i, 0)), pl.BlockSpec((K, D), lambda i: (0, 0))],
        out_specs=pl.BlockSpec((tm, K), lambda i: (i, 0)), out_shape=_sds((T, K), F32),
        compiler_params=_cparams(VMEM_BIG))(dy, w)


def _tn_step(acc, o_ref, lhs, rhs, t, nt):
    part = lax.dot_general(lhs, rhs, (((0,), (0,)), ((), ())), preferred_element_type=F32)
    if nt == 1:
        o_ref[...] = part.astype(o_ref.dtype)
        return

    @pl.when(t == 0)
    def _():
        acc[...] = part

    @pl.when((t > 0) & (t < nt - 1))
    def _():
        acc[...] += part

    @pl.when(t == nt - 1)
    def _():
        o_ref[...] = (acc[...] + part).astype(o_ref.dtype)


def tn_mm(a, b, tk):
    T, K = a.shape
    N = b.shape[1]
    tt = _tile(T, 1024)
    nt = T // tt

    def body(a_ref, b_ref, o_ref, acc):
        _tn_step(acc, o_ref, a_ref[...], b_ref[...], pl.program_id(1), nt)

    return pl.pallas_call(
        body, name="tn_mm", grid=(K // tk, nt),
        in_specs=[pl.BlockSpec((tt, tk), lambda k, t: (t, k)), pl.BlockSpec((tt, N), lambda k, t: (t, 0))],
        out_specs=pl.BlockSpec((tk, N), lambda k, t: (k, 0)), out_shape=_sds((K, N), BF16),
        scratch_shapes=[pltpu.VMEM((tk, N), F32)], compiler_params=_cparams(VMEM_BIG))(a, b)


def tn_mm_mod(x, mod, b, sh, sc, S, tn):
    T = x.shape[0]
    N = b.shape[1]
    tt = _tile(S, 1024)
    tpb = S // tt
    nt = T // tt

    def body(x_ref, mod_ref, b_ref, o_ref, acc):
        h = _modulate(x_ref[...], mod_ref, sh, sc).astype(BF16)
        _tn_step(acc, o_ref, h, b_ref[...], pl.program_id(1), nt)

    return pl.pallas_call(
        body, name="tn_mm_mod", grid=(N // tn, nt),
        in_specs=[pl.BlockSpec((tt, D), lambda j, t: (t, 0)),
                  pl.BlockSpec((None, N_MOD, D), lambda j, t: (t // tpb, 0, 0)),
                  pl.BlockSpec((tt, tn), lambda j, t: (t, j))],
        out_specs=pl.BlockSpec((D, tn), lambda j, t: (0, j)), out_shape=_sds((D, N), BF16),
        scratch_shapes=[pltpu.VMEM((D, tn), F32)], compiler_params=_cparams(VMEM_BIG))(x, mod, b)


def tn_mm_mod_shards(x, mod, bg, bu, sh, sc, S):
    T = x.shape[0]
    tn = FF // 2
    tt = _tile(S, 1024)
    tpb = S // tt
    nt = T // tt

    def body(x_ref, mod_ref, bg_ref, bu_ref, o_ref, acc):
        j, t = pl.program_id(0), pl.program_id(1)
        h = _modulate(x_ref[...], mod_ref, sh, sc).astype(BF16)

        @pl.when(j < 2)
        def _():
            _tn_step(acc, o_ref, h, bg_ref[...], t, nt)

        @pl.when(j >= 2)
        def _():
            _tn_step(acc, o_ref, h, bu_ref[...], t, nt)

    return pl.pallas_call(
        body, name="tn_mm_mod_shards", grid=(N_CHIPS, nt),
        in_specs=[pl.BlockSpec((tt, D), lambda j, t: (t, 0)),
                  pl.BlockSpec((None, N_MOD, D), lambda j, t: (t // tpb, 0, 0)),
                  pl.BlockSpec((tt, tn), lambda j, t: (jnp.where(j < 2, t, 0), jnp.minimum(j, 1))),
                  pl.BlockSpec((tt, tn), lambda j, t: (jnp.where(j < 2, 0, t), jnp.maximum(j - 2, 0)))],
        out_specs=pl.BlockSpec((None, D, tn), lambda j, t: (j, 0, 0)), out_shape=_sds((N_CHIPS, D, tn), BF16),
        scratch_shapes=[pltpu.VMEM((D, tn), F32)], compiler_params=_cparams(VMEM_BIG))(x, mod, bg, bu)


def nt_mod_bwd(ds, w, offs, x, mod, dres, sc, S, tk, l=None):
    T = x.shape[0]
    B = T // S
    tm = _tile(S, 512)
    tpb = S // tm
    Kd = ds[0].shape[1]
    nk = Kd // tk
    n_in = len(ds)

    def body(*refs):
        d_refs, w_refs = refs[:n_in], refs[n_in:2 * n_in]
        x_ref, mod_ref, r_ref, dx_ref, dsh_ref, dsc_ref, acc = refs[2 * n_in:]
        i, k = pl.program_id(0), pl.program_id(1)

        part = sum(lax.dot_general(d_ref[...], w_ref[...], (((1,), (1,)), ((), ())), preferred_element_type=F32)
                   for d_ref, w_ref in zip(d_refs, w_refs))

        @pl.when(k == 0)
        def _():
            acc[...] = part

        @pl.when(k > 0)
        def _():
            acc[...] += part

        @pl.when(k == nk - 1)
        def _():
            dh = acc[...]
            dx_ref[...] = dh * (1.0 + mod_ref[sc:sc + 1, :]) + r_ref[...]

            @pl.when(i % tpb == 0)
            def _():
                dsh_ref[...] = jnp.zeros_like(dsh_ref)
                dsc_ref[...] = jnp.zeros_like(dsc_ref)

            dsh_ref[...] += jnp.sum(dh, axis=0, keepdims=True)
            dsc_ref[...] += jnp.sum(dh * x_ref[...], axis=0, keepdims=True)

    tok = pl.BlockSpec((tm, D), lambda i, k: (i, 0))
    vec = pl.BlockSpec((None, 1, D), lambda i, k: (i // tpb, 0, 0))
    in_specs = [pl.BlockSpec((tm, tk), lambda i, k: (i, k)) for _ in ds]
    if l is None:
        in_specs += [pl.BlockSpec((D, tk), functools.partial(lambda i, k, o: (0, k + o), o=off // tk)) for off in offs]
    else:
        in_specs += [pl.BlockSpec((None, None, D, tk), functools.partial(lambda i, k, o: (l, k + o, 0, 0), o=off)) for off in offs]
    in_specs += [tok, pl.BlockSpec((None, N_MOD, D), lambda i, k: (i // tpb, 0, 0)), tok]
    return pl.pallas_call(
        body, name="nt_mod_bwd", grid=(T // tm, nk), in_specs=in_specs,
        out_specs=[tok, vec, vec],
        out_shape=[_sds((T, D), F32), _sds((B, 1, D), F32), _sds((B, 1, D), F32)],
        scratch_shapes=[pltpu.VMEM((tm, D), F32)],
        compiler_params=_cparams(VMEM_BIG))(*ds, *([w] * n_in), x, mod, dres)


def loss_head(y, tgt):
    T = y.shape[0]
    tm = _tile(T, 512)

    def body(y_ref, t_ref, l_ref, d_ref):
        @pl.when(pl.program_id(0) == 0)
        def _():
            l_ref[...] = jnp.zeros_like(l_ref)
        e = y_ref[...] - t_ref[...]
        d_ref[...] = e * (1.0 / D)
        l_ref[...] += 0.5 * jnp.sum(jnp.sum(e * e, axis=1, keepdims=True) * (1.0 / D))

    tok = pl.BlockSpec((tm, D), lambda i: (i, 0))
    return pl.pallas_call(
        body, name="loss_head", grid=(T // tm,), in_specs=[tok, tok],
        out_specs=[pl.BlockSpec((8, 128), lambda i: (0, 0)), tok],
        out_shape=[_sds((8, 128), F32), _sds((T, D), F32)],
        compiler_params=_cparams(VMEM_BIG))(y, tgt)


def _hgrn_block(q, fz, inp, go, st, lb, ng, blk):
    nc = blk // A_CHUNK
    lb_eff = jnp.maximum(lb, LB_FLOOR)
    log_f = jnp.logaddexp(jnp.log(lb_eff), jnp.log1p(-lb) + jax.nn.log_sigmoid(fz))
    k = (1.0 - lb) * jax.nn.sigmoid(-fz) - (lb_eff - lb)
    qf = jax.nn.silu(q)
    same_chunk = _iota((blk, blk), 0) // A_CHUNK == _iota((blk, blk), 1) // A_CHUNK
    tril = (same_chunk & (_iota((blk, blk), 1) <= _iota((blk, blk), 0))).astype(F32)
    G = _sel_l(tril, log_f)
    e_mat, et_mat = _head_sum_mats()
    G4, q4, k4, v4 = (z.reshape(nc, A_CHUNK, GW) for z in (G, qf, k, inp))
    shp = (nc, A_CHUNK, A_CHUNK, GW)
    one = (1, A_CHUNK, A_CHUNK, GW)
    mask = jnp.where(_iota(one, 2) <= _iota(one, 1), 0.0, -jnp.inf)
    decay = jnp.exp((G4[:, :, None, :] - G4[:, None, :, :]) + mask)
    prod = q4[:, :, None, :] * k4[:, None, :, :] * decay
    scores = _mm(prod.reshape(nc * A_CHUNK * A_CHUNK, GW), e_mat.astype(BF16))
    spread = _mm(scores, et_mat.astype(BF16)).reshape(shp)
    o_intra = jnp.sum(spread * v4[:, None, :, :], axis=2).reshape(blk, GW)
    head_diag = (_iota((GW, GW), 0) // HD == _iota((GW, GW), 1) // HD).astype(F32)
    g_last = [jnp.sum(log_f[c * A_CHUNK:(c + 1) * A_CHUNK], axis=0, keepdims=True) for c in range(nc)]
    g_last_b = jnp.concatenate([jnp.broadcast_to(g, (A_CHUNK, GW)) for g in g_last], axis=0)
    q_dec = qf * jnp.exp(G)
    k_end = k * jnp.exp(g_last_b - G)
    outs = []
    for c in range(nc):
        rows = slice(c * A_CHUNK, (c + 1) * A_CHUNK)
        outs.append(_mm_nt(q_dec[rows], st))
        st = st * jnp.exp(g_last[c]) + _mm_tn(inp[rows], k_end[rows]) * head_diag
    o = o_intra + jnp.concatenate(outs, axis=0)
    ms = _sel_r(o * o, e_mat) * (1.0 / HD)
    o = o * _sel_r(lax.rsqrt(ms + RMS_EPS), et_mat) * ng
    return o * jax.nn.silu(go), st


HGRN_BLK = 128


def hgrn_fwd(proj, lb, ng, S):
    T = proj.shape[0]
    B = T // S
    blk = min(HGRN_BLK, S)
    nb = S // blk

    def body(p_ref, lb_ref, ng_ref, o_ref, st_out_ref, st_ref):
        @pl.when(pl.program_id(1) == 0)
        def _():
            st_ref[...] = jnp.zeros_like(st_ref)
        st_out_ref[...] = st_ref[...]
        p = p_ref[...]
        o, st = _hgrn_block(p[:, 0:GW], p[:, GW:2 * GW], p[:, 2 * GW:3 * GW], p[:, 3 * GW:4 * GW],
                            st_ref[...], lb_ref[...], ng_ref[...], blk)
        o_ref[...] = o.astype(BF16)
        st_ref[...] = st

    vec = pl.BlockSpec((1, GW), lambda b, j: (0, 0))
    return pl.pallas_call(
        body, name="hgrn_fwd", grid=(B, nb),
        in_specs=[pl.BlockSpec((blk, 4 * GW), lambda b, j: (b * nb + j, C_A // (4 * GW))), vec, vec],
        out_specs=[pl.BlockSpec((blk, GW), lambda b, j: (b * nb + j, 0)),
                   pl.BlockSpec((None, GW, GW), lambda b, j: (b * nb + j, 0, 0))],
        out_shape=[_sds((T, GW), BF16), _sds((B * nb, GW, GW), F32)],
        scratch_shapes=[pltpu.VMEM((GW, GW), F32)],
        compiler_params=_cparams(VMEM_BIG))(proj, lb, ng)


def hgrn_bwd(proj, states, dcat, lb, ng, S):
    T = proj.shape[0]
    B = T // S
    blk = min(HGRN_BLK, S)
    nb = S // blk

    def body(p_ref, st_in_ref, do_ref, lb_ref, ng_ref, dp_ref, dlb_ref, dng_ref, dst_ref):
        b, j = pl.program_id(0), pl.program_id(1)

        @pl.when(j == 0)
        def _():
            dst_ref[...] = jnp.zeros_like(dst_ref)

        @pl.when((b == 0) & (j == 0))
        def _():
            dlb_ref[...] = jnp.zeros_like(dlb_ref)
            dng_ref[...] = jnp.zeros_like(dng_ref)

        p = p_ref[...]
        f = functools.partial(_hgrn_block, blk=blk)
        _, vjp = jax.vjp(f, p[:, 0:GW], p[:, GW:2 * GW], p[:, 2 * GW:3 * GW], p[:, 3 * GW:4 * GW],
                         st_in_ref[...], lb_ref[...], ng_ref[...])
        dq, df, di, dg, dst, dlb, dng = vjp((do_ref[...], dst_ref[...]))
        dp_ref[...] = jnp.concatenate([dq, df, di, dg], axis=1).astype(BF16)
        dst_ref[...] = dst
        dlb_ref[...] += dlb
        dng_ref[...] += dng

    def rev(b, j):
        return b * nb + (nb - 1 - j)

    vec = pl.BlockSpec((1, GW), lambda b, j: (0, 0))
    return pl.pallas_call(
        body, name="hgrn_bwd", grid=(B, nb),
        in_specs=[pl.BlockSpec((blk, 4 * GW), lambda b, j: (rev(b, j), C_A // (4 * GW))),
                  pl.BlockSpec((None, GW, GW), lambda b, j: (rev(b, j), 0, 0)),
                  pl.BlockSpec((blk, GW), lambda b, j: (rev(b, j), O_A // GW)), vec, vec],
        out_specs=[pl.BlockSpec((blk, 4 * GW), lambda b, j: (rev(b, j), 0)), vec, vec],
        out_shape=[_sds((T, 4 * GW), BF16), _sds((1, GW), F32), _sds((1, GW), F32)],
        scratch_shapes=[pltpu.VMEM((GW, GW), F32)],
        compiler_params=_cparams(VMEM_BIG))(proj, states, dcat, lb, ng)


ATT_TQ = 256


ATT_BANDS = 8


def _attn_block(q, k, v, cum, qpos0, scale, use_cum, n_free):
    s = _mm_nt(q, k) * scale
    if use_cum:
        s = s - cum
    band = s[:, n_free:]
    visible = _iota(band.shape, 1) <= (qpos0 - n_free) + _iota(band.shape, 0)
    band = jnp.where(visible, band, -jnp.inf)
    m = jnp.max(band, axis=-1, keepdims=True)
    if n_free:
        free = s[:, :n_free]
        m = jnp.maximum(m, jnp.max(free, axis=-1, keepdims=True))
    e = jnp.exp(band - m)
    denom = jnp.sum(e, axis=-1, keepdims=True)
    o = _mm(e, v[n_free:])
    if n_free:
        e = jnp.exp(free - m)
        denom = denom + jnp.sum(e, axis=-1, keepdims=True)
        o = o + _mm(e, v[:n_free])
    return o * (1.0 / denom)


def _bands(S, tq):
    nq = S // tq
    nb = min(ATT_BANDS, nq)
    per = nq // nb
    return [(r * per, (r + 1) * per, (r + 1) * per * tq) for r in range(nb)]


def attn_fwd(qa, qo, ka, ko, va, vo, cum, scale, S):
    T = qa.shape[0]
    B = T // S
    tq = min(ATT_TQ, S)
    nq = S // tq
    use_cum = cum is not None

    def body(*refs):
        if use_cum:
            q_ref, k_ref, v_ref, c_ref, o_ref = refs
        else:
            (q_ref, k_ref, v_ref, o_ref), c_ref = refs, None
        h, i = pl.program_id(1), pl.program_id(2)
        for lo, hi, kw in _bands(S, tq):
            @pl.when((i >= lo) & (i < hi))
            def _():
                crow = c_ref[pl.ds(h, 1), 0:kw] if use_cum else None
                o = _attn_block(q_ref[...], k_ref[0:kw, :], v_ref[0:kw, :], crow, i * tq, scale, use_cum, lo * tq)
                o_ref[...] = o.astype(BF16)

    in_specs = [pl.BlockSpec((tq, HP), lambda b, h, i: (b * nq + i, qo + h)),
                pl.BlockSpec((S, HP), lambda b, h, i: (b, ko + h)),
                pl.BlockSpec((S, HP), lambda b, h, i: (b, vo + h))]
    args = [qa, ka, va]
    if use_cum:
        in_specs.append(pl.BlockSpec((None, 8, S), lambda b, h, i: (b, 0, 0)))
        args.append(cum)
    return pl.pallas_call(
        body, name="attn_fwd", grid=(B, HEADS, nq), in_specs=in_specs,
        out_specs=pl.BlockSpec((tq, HP), lambda b, h, i: (b * nq + i, h)),
        out_shape=_sds((T, HEADS * HP), BF16),
        compiler_params=_cparams(VMEM_BIG))(*args)


def _attn_block_bwd(q, k, v, cum, do, qpos0, scale, use_cum, n_free):
    tn = (((0,), (0,)), ((), ()))
    nt = (((1,), (1,)), ((), ()))
    qb, dob = q.astype(BF16), do.astype(BF16)
    kb, vb = k.astype(BF16), v.astype(BF16)
    s = lax.dot_general(qb, kb, nt, preferred_element_type=F32) * scale
    if use_cum:
        s = s - cum
    band = s[:, n_free:]
    visible = _iota(band.shape, 1) <= (qpos0 - n_free) + _iota(band.shape, 0)
    parts = [(jnp.where(visible, band, -jnp.inf), n_free, s.shape[1])]
    if n_free:
        parts.append((s[:, :n_free], 0, n_free))
    m = functools.reduce(jnp.maximum, [jnp.max(sp, axis=-1, keepdims=True) for sp, _, _ in parts])
    es = [jnp.exp(sp - m) for sp, _, _ in parts]
    rinv = 1.0 / sum(jnp.sum(e, axis=-1, keepdims=True) for e in es)
    ps = [e * rinv for e in es]
    dps = [lax.dot_general(dob, vb[a:b], nt, preferred_element_type=F32) for _, a, b in parts]
    delta = sum(jnp.sum(p * dp, axis=-1, keepdims=True) for p, dp in zip(ps, dps))
    dq = jnp.zeros(q.shape, F32)
    out = []
    for p, dp, (_, a, b) in zip(ps, dps, parts):
        ds = p * (dp - delta)
        dsb = ds.astype(BF16)
        dq = dq + jnp.dot(dsb, kb[a:b], preferred_element_type=F32)
        out.append((a, b, lax.dot_general(dsb, qb, tn, preferred_element_type=F32) * scale,
                    lax.dot_general(p.astype(BF16), dob, tn, preferred_element_type=F32),
                    -jnp.sum(ds, axis=0, keepdims=True) if use_cum else None))
    return dq * scale, out


def attn_bwd(qa, qo, ka, ko, va, vo, cum, dcat, do_off, scale, S, out_dtype):
    T = qa.shape[0]
    B = T // S
    tq = min(ATT_TQ, S)
    nq = S // tq
    use_cum = cum is not None

    def body(*refs):
        if use_cum:
            q_ref, k_ref, v_ref, do_ref, c_ref, dq_ref, dk_ref, dv_ref, dc_ref, dk_acc, dv_acc = refs
        else:
            q_ref, k_ref, v_ref, do_ref, dq_ref, dk_ref, dv_ref, dk_acc, dv_acc = refs
        h, i = pl.program_id(1), pl.program_id(2)

        @pl.when(i == 0)
        def _():
            dk_acc[...] = jnp.zeros_like(dk_acc)
            dv_acc[...] = jnp.zeros_like(dv_acc)
            if use_cum:
                dc_ref[...] = jnp.zeros_like(dc_ref)

        for lo, hi, kw in _bands(S, tq):
            @pl.when((i >= lo) & (i < hi))
            def _():
                crow = c_ref[pl.ds(h, 1), 0:kw] if use_cum else None
                dq, pieces = _attn_block_bwd(q_ref[...], k_ref[0:kw, :], v_ref[0:kw, :], crow, do_ref[...], i * tq,
                                             scale, use_cum, lo * tq)
                dq_ref[...] = dq.astype(out_dtype)
                for a, b, dk, dv, dc in pieces:
                    dk_acc[a:b, :] += dk
                    dv_acc[a:b, :] += dv
                    if use_cum:
                        dc_ref[:, a:b] += dc

        @pl.when(i == nq - 1)
        def _():
            dk_ref[...] = dk_acc[...].astype(out_dtype)
            dv_ref[...] = dv_acc[...].astype(out_dtype)

    qspec = pl.BlockSpec((tq, HP), lambda b, h, i: (b * nq + i, qo + h))
    in_specs = [qspec, pl.BlockSpec((S, HP), lambda b, h, i: (b, ko + h)),
                pl.BlockSpec((S, HP), lambda b, h, i: (b, vo + h)),
                pl.BlockSpec((tq, HP), lambda b, h, i: (b * nq + i, do_off + h))]
    args = [qa, ka, va, dcat]
    kv_out = pl.BlockSpec((S, HP), lambda b, h, i: (b, h))
    out_specs = [pl.BlockSpec((tq, HP), lambda b, h, i: (b * nq + i, h)), kv_out, kv_out]
    out_shape = [_sds((T, HEADS * HP), out_dtype)] * 3
    if use_cum:
        in_specs.append(pl.BlockSpec((None, 8, S), lambda b, h, i: (b, 0, 0)))
        args.append(cum)
        out_specs.append(pl.BlockSpec((None, 1, S), lambda b, h, i: (b * HEADS + h, 0, 0)))
        out_shape.append(_sds((B * HEADS, 1, S), F32))
    return pl.pallas_call(
        body, name="attn_bwd", grid=(B, HEADS, nq), in_specs=in_specs, out_specs=out_specs, out_shape=out_shape,
        scratch_shapes=[pltpu.VMEM((S, HP), F32), pltpu.VMEM((S, HP), F32)],
        compiler_params=_cparams(VMEM_BIG))(*args)


def _tri(n, upper):
    r, c = _iota((n, n), 0), _iota((n, n), 1)
    return ((r <= c) if upper else (r >= c)).astype(F32)


def fox_gate_fwd(proj, bcol, S):
    T = proj.shape[0]
    B = T // S
    ts = _tile(S, 512)
    nt = S // ts

    def body(p_ref, b_ref, o_ref, carry):
        @pl.when(pl.program_id(1) == 0)
        def _():
            carry[...] = jnp.zeros_like(carry)
        cf = jnp.transpose(p_ref[...])[0:8, :]
        lf = jax.nn.log_sigmoid(cf + b_ref[...])
        cum = _sel_r(lf, _tri(ts, True)) + carry[...]
        o_ref[...] = cum
        carry[...] += jnp.sum(lf, axis=1, keepdims=True)

    return pl.pallas_call(
        body, name="fox_gate_fwd", grid=(B, nt),
        in_specs=[pl.BlockSpec((ts, HP), lambda b, j: (b * nt + j, C_CF // HP)), pl.BlockSpec((8, 1), lambda b, j: (0, 0))],
        out_specs=pl.BlockSpec((None, 8, ts), lambda b, j: (b, 0, j)), out_shape=_sds((B, 8, S), F32),
        scratch_shapes=[pltpu.VMEM((8, 1), F32)],
        compiler_params=_cparams(VMEM_BIG))(proj, bcol)


def fox_gate_bwd(proj, bcol, dcum, S):
    T = proj.shape[0]
    B = T // S
    ts = _tile(S, 512)
    nt = S // ts

    def body(p_ref, b_ref, dc_ref, dp_ref, db_ref, carry):
        b, j = pl.program_id(0), pl.program_id(1)

        @pl.when(j == 0)
        def _():
            carry[...] = jnp.zeros_like(carry)

        @pl.when((b == 0) & (j == 0))
        def _():
            db_ref[...] = jnp.zeros_like(db_ref)

        cf = jnp.transpose(p_ref[...])[0:8, :]
        dc = dc_ref[...]
        dlf = _sel_r(dc, _tri(ts, False)) + carry[...]
        carry[...] += jnp.sum(dc, axis=1, keepdims=True)
        dcf = dlf * jax.nn.sigmoid(-(cf + b_ref[...]))
        db_ref[...] += jnp.sum(dcf, axis=1, keepdims=True)
        full = jnp.concatenate([dcf, jnp.zeros((HP - 8, ts), F32)], axis=0)
        dp_ref[...] = jnp.transpose(full).astype(BF16)

    def rev(b, j):
        return nt - 1 - j

    return pl.pallas_call(
        body, name="fox_gate_bwd", grid=(B, nt),
        in_specs=[pl.BlockSpec((ts, HP), lambda b, j: (b * nt + rev(b, j), C_CF // HP)),
                  pl.BlockSpec((8, 1), lambda b, j: (0, 0)),
                  pl.BlockSpec((None, 8, ts), lambda b, j: (b, 0, rev(b, j)))],
        out_specs=[pl.BlockSpec((ts, HP), lambda b, j: (b * nt + rev(b, j), 0)), pl.BlockSpec((8, 1), lambda b, j: (0, 0))],
        out_shape=[_sds((T, HP), BF16), _sds((8, 1), F32)],
        scratch_shapes=[pltpu.VMEM((8, 1), F32)],
        compiler_params=_cparams(VMEM_BIG))(proj, bcol, dcum)


def _mla_pre(blk, gq, gkv, wq, wkv, place, cos_q, sin_q, cs_k):
    nq = _rms(blk[:, 0:B_Q_LORA], gq)
    nkv = _rms(blk[:, B_Q_LORA:B_Q_LORA + B_KV_LORA], gkv)
    qq = _mm(nq, wq)
    q = qq[:, 0:HEADS * HP] * cos_q + qq[:, HEADS * HP:] * sin_q
    kv = _mm(nkv, wkv)
    k = kv[:, 0:HEADS * HP] + _mm(blk[:, B_Q_LORA + B_KV_LORA:] * cs_k, place)
    return q, k, kv[:, HEADS * HP:]


def mla_pre_fwd(proj, gq, gkv, wq, wkv, place, cos_q, sin_q, cs_k, S):
    T = proj.shape[0]
    tm = _tile(S, 512)
    tpb = S // tm
    W = HEADS * HP

    def body(p_ref, gq_ref, gkv_ref, wq_ref, wkv_ref, pl_ref, cq_ref, sq_ref, ck_ref, q_ref, k_ref, v_ref):
        q, k, v = _mla_pre(p_ref[...], gq_ref[...], gkv_ref[...], wq_ref[...], wkv_ref[...], pl_ref[...],
                           cq_ref[...], sq_ref[...], ck_ref[...])
        q_ref[...] = q
        k_ref[...] = k
        v_ref[...] = v

    def full(a):
        return pl.BlockSpec(a.shape, lambda i: (0,) * a.ndim)

    tok = pl.BlockSpec((tm, W), lambda i: (i, 0))
    return pl.pallas_call(
        body, name="mla_pre_fwd", grid=(T // tm,),
        in_specs=[pl.BlockSpec((tm, W), lambda i: (i, C_B // W)), full(gq), full(gkv), full(wq), full(wkv), full(place),
                  pl.BlockSpec((tm, W), lambda i: (i % tpb, 0)), pl.BlockSpec((tm, W), lambda i: (i % tpb, 0)),
                  pl.BlockSpec((tm, HP), lambda i: (i % tpb, 0))],
        out_specs=[tok] * 3, out_shape=[_sds((T, W), F32)] * 3,
        compiler_params=_cparams(VMEM_BIG))(proj, gq, gkv, wq, wkv, place, cos_q, sin_q, cs_k)


def mla_pre_bwd(proj, gq, gkv, wq, wkv, place, cos_q, sin_q, cs_k, dq, dk, dv, S):
    T = proj.shape[0]
    tm = _tile(S, 512)
    tpb = S // tm
    W = HEADS * HP

    def body(p_ref, gq_ref, gkv_ref, wq_ref, wkv_ref, pl_ref, cq_ref, sq_ref, ck_ref, dq_ref, dk_ref, dv_ref,
             dp_ref, dgq_ref, dgkv_ref, dwq_ref, dwkv_ref):
        @pl.when(pl.program_id(0) == 0)
        def _():
            for r in (dgq_ref, dgkv_ref, dwq_ref, dwkv_ref):
                r[...] = jnp.zeros_like(r)

        f = functools.partial(_mla_pre, place=pl_ref[...], cos_q=cq_ref[...], sin_q=sq_ref[...], cs_k=ck_ref[...])
        _, vjp = jax.vjp(f, p_ref[...], gq_ref[...], gkv_ref[...], wq_ref[...], wkv_ref[...])
        dp, dgq, dgkv, dwq, dwkv = vjp((dq_ref[...], dk_ref[...], dv_ref[...]))
        dp_ref[...] = dp.astype(BF16)
        dgq_ref[...] += dgq
        dgkv_ref[...] += dgkv
        dwq_ref[...] += dwq
        dwkv_ref[...] += dwkv

    def full(a):
        return pl.BlockSpec(a.shape, lambda i: (0,) * a.ndim)

    tok = pl.BlockSpec((tm, W), lambda i: (i, 0))
    return pl.pallas_call(
        body, name="mla_pre_bwd", grid=(T // tm,),
        in_specs=[pl.BlockSpec((tm, W), lambda i: (i, C_B // W)), full(gq), full(gkv), full(wq), full(wkv), full(place),
                  pl.BlockSpec((tm, W), lambda i: (i % tpb, 0)), pl.BlockSpec((tm, W), lambda i: (i % tpb, 0)),
                  pl.BlockSpec((tm, HP), lambda i: (i % tpb, 0)), tok, tok, tok],
        out_specs=[tok, full(gq), full(gkv), full(wq), full(wkv)],
        out_shape=[_sds((T, W), BF16), _sds(gq.shape, F32), _sds(gkv.shape, F32), _sds(wq.shape, F32), _sds(wkv.shape, F32)],
        compiler_params=_cparams(VMEM_BIG))(proj, gq, gkv, wq, wkv, place, cos_q, sin_q, cs_k, dq, dk, dv)


def _gmlp_block(blk, lg, lb, ws, bs):
    u = jax.nn.gelu(blk[:, 0:GW])
    v = jax.nn.gelu(blk[:, GW:2 * GW])
    mu = jnp.mean(v, axis=-1, keepdims=True)
    var = jnp.mean(jnp.square(v - mu), axis=-1, keepdims=True)
    vn = (v - mu) * lax.rsqrt(var + LN_EPS) * lg + lb
    causal = _iota((D_CHUNK, D_CHUNK), 1) <= _iota((D_CHUNK, D_CHUNK), 0)
    group = _iota((1, GW), 1) // HD
    mixed = jnp.zeros((D_CHUNK, GW), F32)
    for g in range(HEADS):
        part = _mm(jnp.where(causal, ws[g], 0.0), vn) + bs[g]
        mixed = mixed + jnp.where(group == g, part, 0.0)
    return u * mixed


def gmlp_fwd(proj, lg, lb, ws, bs):
    T = proj.shape[0]

    def body(p_ref, lg_ref, lb_ref, ws_ref, bs_ref, o_ref):
        o_ref[...] = _gmlp_block(p_ref[...], lg_ref[...], lb_ref[...], ws_ref[...], bs_ref[...]).astype(BF16)

    def full(a):
        return pl.BlockSpec(a.shape, lambda i: (0,) * a.ndim)

    return pl.pallas_call(
        body, name="gmlp_fwd", grid=(T // D_CHUNK,),
        in_specs=[pl.BlockSpec((D_CHUNK, 2 * GW), lambda i: (i, C_D // (2 * GW))), full(lg), full(lb), full(ws), full(bs)],
        out_specs=pl.BlockSpec((D_CHUNK, GW), lambda i: (i, 0)), out_shape=_sds((T, GW), BF16),
        compiler_params=_cparams(VMEM_BIG))(proj, lg, lb, ws, bs)


def gmlp_bwd(proj, lg, lb, ws, bs, dcat):
    T = proj.shape[0]

    def body(p_ref, lg_ref, lb_ref, ws_ref, bs_ref, do_ref, dp_ref, dlg_ref, dlb_ref, dws_ref, dbs_ref):
        @pl.when(pl.program_id(0) == 0)
        def _():
            for r in (dlg_ref, dlb_ref, dws_ref, dbs_ref):
                r[...] = jnp.zeros_like(r)

        _, vjp = jax.vjp(_gmlp_block, p_ref[...], lg_ref[...], lb_ref[...], ws_ref[...], bs_ref[...])
        dp, dlg, dlb, dws, dbs = vjp(do_ref[...])
        dp_ref[...] = dp.astype(BF16)
        dlg_ref[...] += dlg
        dlb_ref[...] += dlb
        dws_ref[...] += dws
        dbs_ref[...] += dbs

    def full(a):
        return pl.BlockSpec(a.shape, lambda i: (0,) * a.ndim)

    return pl.pallas_call(
        body, name="gmlp_bwd", grid=(T // D_CHUNK,),
        in_specs=[pl.BlockSpec((D_CHUNK, 2 * GW), lambda i: (i, C_D // (2 * GW))), full(lg), full(lb), full(ws), full(bs),
                  pl.BlockSpec((D_CHUNK, GW), lambda i: (i, O_D // GW))],
        out_specs=[pl.BlockSpec((D_CHUNK, 2 * GW), lambda i: (i, 0)), full(lg), full(lb), full(ws), full(bs)],
        out_shape=[_sds((T, 2 * GW), BF16), _sds(lg.shape, F32), _sds(lb.shape, F32), _sds(ws.shape, F32), _sds(bs.shape, F32)],
        compiler_params=_cparams(VMEM_BIG))(proj, lg, lb, ws, bs, dcat)


def _lb_all(logits):
    m = jnp.max(logits, axis=0, keepdims=True)
    e = jnp.exp(logits - m)
    sm = e / jnp.sum(e, axis=0, keepdims=True)
    return jnp.concatenate([sm[0:1] - sm[0:1], (sm[0:1] + sm[1:2]) - sm[0:1]], axis=0)


def lb_fwd(logits):
    def body(l_ref, o_ref):
        o_ref[...] = _lb_all(l_ref[...])

    return pl.pallas_call(body, name="lb_fwd", out_shape=_sds(logits.shape, F32))(logits)


def lb_bwd(logits, dlb):
    def body(l_ref, d_ref, o_ref):
        _, vjp = jax.vjp(_lb_all, l_ref[...])
        o_ref[...] = vjp(d_ref[...])[0]

    return pl.pallas_call(body, name="lb_bwd", out_shape=_sds(logits.shape, F32))(logits, dlb)


def ada_grad(c_all, dmod_cols):
    N = dmod_cols.shape[1]
    tn = _tile(N, 1152)

    def body(c_ref, d_ref, o_ref):
        h = jax.nn.silu(c_ref[...]).astype(BF16)
        o_ref[...] = lax.dot_general(h, d_ref[...].astype(BF16), (((0,), (0,)), ((), ())), preferred_element_type=F32)

    nb = c_all.shape[0]
    return pl.pallas_call(
        body, name="ada_grad", grid=(N // tn,),
        in_specs=[pl.BlockSpec((nb, D), lambda j: (0, 0)), pl.BlockSpec((nb, tn), lambda j: (0, j))],
        out_specs=pl.BlockSpec((D, tn), lambda j: (0, j)), out_shape=_sds((D, N), F32),
        compiler_params=_cparams(VMEM_BIG))(c_all, dmod_cols)


def sum_slots(a, n, name):
    _, R, C = a.shape
    tr = _row_tile(R, C, n)

    def body(a_ref, o_ref):
        acc = a_ref[0]
        for k in range(1, n):
            acc = acc + a_ref[k]
        o_ref[...] = acc

    return pl.pallas_call(
        body, name=name, grid=(R // tr,),
        in_specs=[pl.BlockSpec((n, tr, C), lambda i: (0, i, 0))],
        out_specs=pl.BlockSpec((tr, C), lambda i: (i, 0)), out_shape=_sds((R, C), F32),
        compiler_params=_cparams(VMEM_BIG))(a)


def _row_tile(R, C=D, n=1, mult=8, elems=1 << 18):
    limit = max(mult, elems // (C * n))
    for t in range(limit - limit % mult, mult - 1, -mult):
        if R % t == 0:
            return t
    return R


def adamw(w, g, m, v, name, echo=False):
    R, C = w.shape
    tr = _row_tile(R, C, elems=1 << 19)
    c1 = 1.0 - ADAM_B1 ** ADAM_STEP
    c2 = 1.0 - ADAM_B2 ** ADAM_STEP
    n_out = 4 if echo else 3

    def body(w_ref, g_ref, m_ref, v_ref, d_ref, nm_ref, nv_ref, *g_out):
        g_ = g_ref[...]
        nm = ADAM_B1 * m_ref[...] + (1.0 - ADAM_B1) * g_
        nv = ADAM_B2 * v_ref[...] + (1.0 - ADAM_B2) * jnp.square(g_)
        d_ref[...] = -ADAM_LR * ((nm / c1) / (jnp.sqrt(nv / c2) + ADAM_EPS) + ADAM_WD * w_ref[...])
        nm_ref[...] = nm
        nv_ref[...] = nv
        if echo:
            g_out[0][...] = g_

    spec = pl.BlockSpec((tr, C), lambda i: (i, 0))
    return pl.pallas_call(body, name=name, grid=(R // tr,), in_specs=[spec] * 4, out_specs=[spec] * n_out,
                          out_shape=[_sds((R, C), F32)] * n_out, compiler_params=_cparams(VMEM_BIG))(w, g, m, v)


def _rot_cols(w):
    return jnp.concatenate([-w[:, 16:32], w[:, 0:16]], axis=1)


def _fold_rot(d):
    return jnp.concatenate([d[:, 16:32], -d[:, 0:16]], axis=1)


def _pad_heads(w, off, axis):
    parts = []
    for h in range(HEADS):
        piece = lax.slice_in_dim(w, off + HD * h, off + HD * (h + 1), axis=axis)
        parts += [piece, jnp.zeros_like(piece)]
    return parts


def _unpad_heads(d, off, axis):
    return [lax.slice_in_dim(d, off + HP * h, off + HP * h + HD, axis=axis) for h in range(HEADS)]


def mix_in_ext(w):
    z = lambda n: jnp.zeros((w.shape[0], n), w.dtype)
    kr = w[:, 1408:1440]
    cols = [w[:, 0:1408], kr, _rot_cols(kr), z(64)]
    cols += _pad_heads(w, 1440, 1) + _pad_heads(w, 1696, 1) + _pad_heads(w, 1952, 1)
    cols += [w[:, 2212:2724], w[:, 2208:2212], z(NP - C_CF - HEADS)]
    return jnp.concatenate(cols, axis=1)


def mix_in_unext(d):
    kr = d[:, 1408:1440] + _fold_rot(d[:, 1440:1472])
    cols = [d[:, 0:1408], kr] + _unpad_heads(d, C_CQ, 1) + _unpad_heads(d, C_CK, 1) + _unpad_heads(d, C_CV, 1)
    cols += [d[:, C_CF:C_CF + HEADS], d[:, C_D:C_D + 2 * GW]]
    return jnp.concatenate(cols, axis=1)


def mix_out_ext(w):
    return jnp.concatenate([w[0:GW]] + _pad_heads(w, GW, 0) + _pad_heads(w, 2 * GW, 0) + [w[3 * GW:4 * GW]], axis=0)


def mix_out_unext(d):
    return jnp.concatenate([d[0:GW]] + _unpad_heads(d, O_B, 0) + _unpad_heads(d, O_C, 0) + [d[O_D:O_D + GW]], axis=0)


def uq_ext(w):
    z = lambda n: jnp.zeros((w.shape[0], n), w.dtype)
    a, b = [], []
    for h in range(HEADS):
        o = (B_NOPE + B_ROPE) * h
        a += [w[:, o:o + B_NOPE + B_ROPE], z(32)]
        b += [z(B_NOPE), _rot_cols(w[:, o + B_NOPE:o + B_NOPE + B_ROPE]), z(32)]
    return jnp.concatenate(a + b, axis=1)


def uq_unext(d):
    cols = []
    for h in range(HEADS):
        o = HP * h
        cols += [d[:, o:o + B_NOPE], d[:, o + B_NOPE:o + B_NOPE + B_ROPE]
                 + _fold_rot(d[:, HEADS * HP + o + B_NOPE:HEADS * HP + o + B_NOPE + B_ROPE])]
    return jnp.concatenate(cols, axis=1)


def ukv_ext(w):
    z = jnp.zeros((w.shape[0], HD), w.dtype)
    k, v = [], []
    for h in range(HEADS):
        k += [w[:, 2 * HD * h:2 * HD * h + HD], z]
        v += [w[:, 2 * HD * h + HD:2 * HD * (h + 1)], z]
    return jnp.concatenate(k + v, axis=1)


def ukv_unext(d):
    cols = []
    for h in range(HEADS):
        cols += [d[:, HP * h:HP * h + HD], d[:, HEADS * HP + HP * h:HEADS * HP + HP * h + HD]]
    return jnp.concatenate(cols, axis=1)


def rope_tables(S):
    half = B_ROPE // 2
    inv_freq = ROPE_THETA ** (-jnp.arange(half, dtype=F32) / half)
    ang = jnp.arange(S).astype(F32)[:, None] * inv_freq[None, :]
    cos = jnp.tile(jnp.cos(ang), (1, 2))
    sin = jnp.tile(jnp.sin(ang), (1, 2))
    one, zero = jnp.ones((S, B_NOPE), F32), jnp.zeros((S, B_NOPE), F32)
    z32 = jnp.zeros((S, 32), F32)
    cos_q = jnp.tile(jnp.concatenate([one, cos, z32], axis=1), (1, HEADS))
    sin_q = jnp.tile(jnp.concatenate([zero, sin, z32], axis=1), (1, HEADS))
    cs_k = jnp.concatenate([cos, sin, zero], axis=1)
    place = np.zeros((HP, HEADS * HP), np.float32)
    for h in range(HEADS):
        for j in range(B_ROPE):
            place[j, h * HP + B_NOPE + j] = 1.0
            place[B_ROPE + j, h * HP + B_NOPE + j] = 1.0
    return cos_q, sin_q, cs_k, jnp.asarray(place, BF16)


def layer_fwd(x, mod, get, tabs, S):
    cos_q, sin_q, cs_k, place = tabs
    p = dict(get("ffn1", x))
    l = p["wl"]
    zg1, zu1, act1 = ffn_in_fwd(x, mod, p["ffn1_in"], l, 0, 1, S)
    y1, x1 = out_ln_fwd(act1, p["ffn1_out"], x, mod, p["ln_g"][0:1], p["ln_b"][0:1], 2, 0.5, S, l)
    p.update(get("mix", x1))
    proj = mix_in_fwd(x1, mod, p["mix_in"], 3, 4, S)
    o_a, states = hgrn_fwd(proj, p["lb"], p["ng"], S)
    q_b, k_b, v_b = mla_pre_fwd(proj, p["gq"], p["gkv"], p["wq"], p["wkv"], place, cos_q, sin_q, cs_k, S)
    o_b = attn_fwd(q_b, 0, k_b, 0, v_b, 0, None, (B_NOPE + B_ROPE) ** -0.5, S)
    cum = fox_gate_fwd(proj, p["bcol"], S)
    o_c = attn_fwd(proj, C_CQ // HP, proj, C_CK // HP, proj, C_CV // HP, cum, HD ** -0.5, S)
    o_d = gmlp_fwd(proj, p["g_lg"], p["g_lb"], p["ws"], p["bs"])
    cat = jnp.concatenate([o_a, o_b, o_c, o_d], axis=1)
    y2, x2 = out_ln_fwd(cat, p["mix_out"], x1, mod, p["ln_g"][1:2], p["ln_b"][1:2], 5, 1.0, S)
    p.update(get("ffn2", x2))
    zg3, zu3, act3 = ffn_in_fwd(x2, mod, p["ffn2_in"], l, 6, 7, S)
    y3, x3 = out_ln_fwd(act3, p["ffn2_out"], x2, mod, p["ln_g"][2:3], p["ln_b"][2:3], 8, 0.5, S, l)
    saved = dict(x=x, zg1=zg1, zu1=zu1, act1=act1, y1=y1, x1=x1, proj=proj, states=states, q_b=q_b, k_b=k_b, v_b=v_b,
                 cum=cum, cat=cat, y2=y2, x2=x2, zg3=zg3, zu3=zu3, act3=act3, y3=y3, p=p)
    return x3, saved


def _ffn_bwd(dxn, x_in, y, zg, zu, act, mod, w_in, w_out, l, lg, lb, idx, S, emit):
    sh, sc, gate = idx
    dres, dy, dgate, dlg, dlb = ln_res_bwd(dxn, x_in, y, mod, lg, lb, gate, 0.5, S)
    dzg, dzu = swiglu_bwd(dy, w_out, l, zg, zu, S)
    dw_out = tn_mm(act, dy, FF // 2).reshape(N_CHIPS, FF // N_CHIPS, D)
    dw_in = tn_mm_mod_shards(x_in, mod, dzg, dzu, sh, sc, S)
    mod = mod + emit(dw_in, dw_out)
    dx, dsh, dsc = nt_mod_bwd([dzg, dzu], w_in, [0, 2], x_in, mod, dres, sc, S, FF // 2, l)
    return dx, dw_in, dw_out, dlg, dlb, {sh: dsh, sc: dsc, gate: dgate}, mod


def layer_bwd(dx3, mod, sv, tabs, S, emit):
    cos_q, sin_q, cs_k, place = tabs
    p = sv["p"]
    l = p["wl"]
    g = {}
    dm = {}

    def emit_ffn(part):
        def f(dw_in, dw_out):
            g[part + "_in"], g[part + "_out"] = dw_in, dw_out
            return emit(part, g)
        return f

    dx2, _, _, dlg2, dlb2, d, mod = _ffn_bwd(
        dx3, sv["x2"], sv["y3"], sv["zg3"], sv["zu3"], sv["act3"], mod, p["ffn2_in"], p["ffn2_out"], l,
        p["ln_g"][2:3], p["ln_b"][2:3], (6, 7, 8), S, emit_ffn("ffn2"))
    dm.update(d)
    dres, dy2, dm[5], dlg1, dlb1 = ln_res_bwd(dx2, sv["x1"], sv["y2"], mod, p["ln_g"][1:2], p["ln_b"][1:2], 5, 1.0, S)
    dcat = nt_plain(dy2, p["mix_out"])
    g["mix_out"] = tn_mm(sv["cat"], dy2, NCAT // 2)
    proj = sv["proj"]
    d_a, g["lb"], g["ng"] = hgrn_bwd(proj, sv["states"], dcat, p["lb"], p["ng"], S)
    dq_c, dk_c, dv_c, dcum = attn_bwd(proj, C_CQ // HP, proj, C_CK // HP, proj, C_CV // HP, sv["cum"], dcat,
                                      O_C // HP, HD ** -0.5, S, BF16)
    B = proj.shape[0] // S
    dcum = jnp.concatenate([dcum.reshape(B, HEADS, S), jnp.zeros((B, 8 - HEADS, S), F32)], axis=1)
    d_cf, g["bcol"] = fox_gate_bwd(proj, p["bcol"], dcum, S)
    dq_b, dk_b, dv_b = attn_bwd(sv["q_b"], 0, sv["k_b"], 0, sv["v_b"], 0, None, dcat, O_B // HP,
                                (B_NOPE + B_ROPE) ** -0.5, S, F32)
    d_b, g["gq"], g["gkv"], g["wq"], g["wkv"] = mla_pre_bwd(
        proj, p["gq"], p["gkv"], p["wq"], p["wkv"], place, cos_q, sin_q, cs_k, dq_b, dk_b, dv_b, S)
    d_d, g["g_lg"], g["g_lb"], g["ws"], g["bs"] = gmlp_bwd(proj, p["g_lg"], p["g_lb"], p["ws"], p["bs"], dcat)
    dproj = jnp.concatenate([d_a, d_b, dq_c, dk_c, dv_c, d_d, d_cf, jnp.zeros_like(d_cf)], axis=1)
    g["mix_in"] = tn_mm_mod(sv["x1"], mod, dproj, 3, 4, S, NP_TILE)
    mod = mod + emit("mix", g)
    dx1, dm[3], dm[4] = nt_mod_bwd([dproj], p["mix_in"], [0], sv["x1"], mod, dres, 4, S, NP_TILE)
    last = []

    def emit_last(dw_in, dw_out):
        last.append(emit_ffn("ffn1")(dw_in, dw_out))
        return last[0]

    dx0, _, _, dlg0, dlb0, d, mod = _ffn_bwd(
        dx1, sv["x"], sv["y1"], sv["zg1"], sv["zu1"], sv["act1"], mod, p["ffn1_in"], p["ffn1_out"], l,
        p["ln_g"][0:1], p["ln_b"][0:1], (0, 1, 2), S, emit_last)
    dm.update(d)
    g["ln_g"] = jnp.concatenate([dlg0, dlg1, dlg2], axis=0)
    g["ln_b"] = jnp.concatenate([dlb0, dlb1, dlb2], axis=0)
    dmod = jnp.concatenate([dm[i] for i in range(N_MOD)], axis=1)
    return dx0, dmod, g, last[0]


def local_step(x, c8, tgt, get, lb_logits, S, emit=None):
    B = x.shape[0] // S
    tabs = rope_tables(S)
    lb_all = lb_fwd(lb_logits)
    mods, saved = [], []
    h = x
    for l in range(DEPTH):
        pa = get(l, "ada", h)
        mod = mod_fwd(c8, pa["ada_w"], pa["wl"], pa["ada_b"])[0:B].reshape(B, N_MOD, D)

        def get_l(part, after, l=l):
            p = dict(get(l, part, after))
            if part == "mix":
                p["lb"] = lb_all[l:l + 1]
            return p

        h, sv = layer_fwd(h, mod, get_l, tabs, S)
        mods.append(mod)
        saved.append(sv)
    loss_tile, dh = loss_head(h, tgt)
    grads, dmods, dlb = [None] * DEPTH, [None] * DEPTH, [None] * DEPTH
    tie = jnp.zeros((), F32)
    for l in reversed(range(DEPTH)):
        emit_l = (lambda part, g: jnp.zeros((), F32)) if emit is None else functools.partial(emit, l)
        dh, dmods[l], grads[l], tie = layer_bwd(dh, mods[l] + tie, saved[l], tabs, S, emit_l)
        dlb[l] = grads[l].pop("lb")
    d_logits = lb_bwd(lb_logits, jnp.concatenate(dlb, axis=0))
    return loss_tile, dh, dmods, grads, d_logits


ANY = pl.BlockSpec(memory_space=pl.ANY)


def _place():
    x, y, c = lax.axis_index("x"), lax.axis_index("y"), lax.axis_index("c")
    chips = [(1 - x, y), (x, 1 - y), (1 - x, 1 - y)]
    return x, y, c, chips


def _rcopy(src, dst, sems, k, to):
    send_sems, recv_sems = sems
    return pltpu.make_async_remote_copy(src_ref=src, dst_ref=dst, send_sem=send_sems.at[k], recv_sem=recv_sems.at[k],
                                        device_id=to, device_id_type=MESH)


def _dma_sems(n_remote, n_local):
    return [pltpu.SemaphoreType.DMA((n_remote,)), pltpu.SemaphoreType.DMA((n_remote,)), pltpu.SemaphoreType.DMA((n_local,))]


def own_slot(src, chip):
    L = src.shape[0]
    return lax.dynamic_update_slice(lax.empty((L, N_CHIPS) + src.shape[1:], src.dtype), src[:, None], (0, chip, 0, 0))


HBM_SPEC = pl.BlockSpec(memory_space=pltpu.HBM)
SEM_SPEC = pl.BlockSpec(memory_space=pltpu.SEMAPHORE)
DATAFLOW = pltpu.SideEffectType.DATAFLOW_SIDE_EFFECTING


def _split_start(srcs, lands, copies, n_copies, dep, name):
    n, m = len(srcs), len(lands)

    def body(*refs):
        ins = refs[:n + m]
        send_sems, recv_sems = refs[n + m + 1], refs[n + m + 2]
        token = refs[-1]
        for k, (src, dst, to) in enumerate(copies(ins[:n], ins[n:], _place())):
            pltpu.make_async_remote_copy(src_ref=src, dst_ref=dst, send_sem=send_sems.at[k], recv_sem=recv_sems.at[k],
                                         device_id=to, device_id_type=MESH).start()
        token[...] = jnp.zeros_like(token)

    arrs = list(srcs) + list(lands)
    outs = pl.pallas_call(
        body, name=name,
        out_shape=(pltpu.SemaphoreType.DMA((n_copies,)), pltpu.SemaphoreType.DMA((n_copies,)),
                   *[pltpu.HBM(a.shape, a.dtype) for a in arrs], _sds((8, 128), F32)),
        in_specs=[HBM_SPEC] * (n + m) + [ANY],
        out_specs=(SEM_SPEC, SEM_SPEC, *[HBM_SPEC] * (n + m), pl.BlockSpec(memory_space=pltpu.VMEM)),
        input_output_aliases={i: 2 + i for i in range(n + m)},
        compiler_params=pltpu.CompilerParams(has_side_effects=DATAFLOW),
    )(*[pltpu.with_memory_space_constraint(a, pltpu.HBM) for a in arrs], dep)
    return outs[0], outs[1], list(outs[2:2 + n]), list(outs[2 + n:2 + n + m]), outs[-1]


def _split_wait(handle, arrivals, after, name):
    send_sems, recv_sems, srcs, lands, _ = handle
    n, m = len(srcs), len(lands)

    def body(*refs):
        ins = refs[:n + m]
        send_sems, recv_sems = refs[n + m], refs[n + m + 1]
        x, y, c, chips = place = _place()
        for k, (src, dst) in enumerate(arrivals(ins[:n], ins[n:], place)):
            cp = pltpu.make_async_remote_copy(src_ref=src, dst_ref=dst, send_sem=send_sems.at[k], recv_sem=recv_sems.at[k],
                                              device_id=(x, y, 1 - c), device_id_type=MESH)
            cp.wait_send()
            cp.wait_recv()

    arrs = list(srcs) + list(lands)
    outs = pl.pallas_call(
        body, name=name, out_shape=[pltpu.HBM(a.shape, a.dtype) for a in arrs],
        in_specs=[HBM_SPEC] * (n + m) + [SEM_SPEC, SEM_SPEC, ANY], out_specs=[HBM_SPEC] * (n + m),
        input_output_aliases={i: i for i in range(n + m)},
        compiler_params=pltpu.CompilerParams(has_side_effects=DATAFLOW),
    )(*arrs, send_sems, recv_sems, after)
    return list(outs[:n]), list(outs[n:])


def _ag_part(ref, k, hc):
    rh = ref.shape[2] // 2
    return ref.at[:, k, pl.ds(hc * rh, rh), :]


def ag_start(srcs, lands, dep, name):
    def copies(s, d, place):
        x, y, c, chips = place
        out = []
        for j, (px, py) in enumerate(chips):
            for i in range(len(s)):
                rh = s[i].shape[1] // 2
                out.append((s[i].at[:, pl.ds(c * rh, rh), :], _ag_part(d[i], 2 * x + y, c), (px, py, c)))
        return out

    return _split_start(srcs, lands, copies, 3 * len(srcs), dep, name)


def ag_wait(handle, after, name):
    def arrivals(s, d, place):
        x, y, c, chips = place
        out = []
        for j, (px, py) in enumerate(chips):
            for i in range(len(s)):
                rh = s[i].shape[1] // 2
                out.append((s[i].at[:, pl.ds(c * rh, rh), :], _ag_part(d[i], 2 * px + py, c)))
        return out

    return _split_wait(handle, arrivals, after, name)


def ag_forward(lands, name):
    n = len(lands)

    def body(*refs):
        bufs, token = refs[n:2 * n], refs[2 * n]
        send_sems, recv_sems = refs[2 * n + 1:]
        x, y, c, chips = _place()
        sems = (send_sems, recv_sems)
        token[...] = jnp.zeros_like(token)
        cps = []
        for j, (px, py) in enumerate(chips):
            for i in range(n):
                part = _ag_part(bufs[i], 2 * px + py, c)
                cps.append(_rcopy(part, part, sems, 3 * i + j, (x, y, 1 - c)))
        for cp in cps:
            cp.start()
        for j, (px, py) in enumerate(chips):
            for i in range(n):
                part = _ag_part(bufs[i], 2 * px + py, 1 - c)
                _rcopy(part, part, sems, 3 * i + j, (x, y, 1 - c)).wait_recv()
        for cp in cps:
            cp.wait_send()

    outs = pl.pallas_call(
        body, name=name, out_shape=[_sds(a.shape, a.dtype) for a in lands] + [_sds((8, 128), F32)],
        in_specs=[ANY] * n, out_specs=[ANY] * n + [pl.BlockSpec(memory_space=pltpu.VMEM)],
        input_output_aliases={i: i for i in range(n)}, scratch_shapes=_dma_sems(3 * n, 1)[:2])(*lands)
    return list(outs[:n]), outs[n]


def rs_start(hs, lands, dep, name):
    def copies(s, d, place):
        x, y, c, chips = place
        return [(s[i].at[2 * px + py], d[i].at[2 * x + y], (px, py, c)) for j, (px, py) in enumerate(chips) for i in range(len(s))]

    return _split_start(hs, lands, copies, 3 * len(hs), dep, name)


def rs_wait(handle, after, name):
    def arrivals(s, d, place):
        x, y, c, chips = place
        return [(s[i].at[2 * px + py], d[i].at[2 * px + py]) for j, (px, py) in enumerate(chips) for i in range(len(s))]

    return _split_wait(handle, arrivals, after, name)


def sibling_swap(arrs, name):
    n = len(arrs)
    rh = [a.shape[1] // 2 for a in arrs]

    def body(*refs):
        srcs, outs = refs[:n], refs[n:2 * n]
        send_sems, recv_sems = refs[2 * n:]
        x, y, c, _ = _place()
        cps = [_rcopy(srcs[i].at[:, pl.ds((1 - c) * rh[i], rh[i]), :], outs[i], (send_sems, recv_sems), i, (x, y, 1 - c))
               for i in range(n)]
        for cp in cps:
            cp.start()
        for cp in cps:
            cp.wait()

    return pl.pallas_call(
        body, name=name, out_shape=[_sds((N_CHIPS, r, a.shape[2]), a.dtype) for a, r in zip(arrs, rh)],
        in_specs=[ANY] * n, out_specs=[ANY] * n, scratch_shapes=_dma_sems(n, 1)[:2])(*arrs)


def sum_into(land, base, l, core, name):
    _, rh, C = land.shape
    tr = _row_tile(rh, C, N_CHIPS, mult=16)
    nr = rh // tr

    def body(core_ref, land_ref, base_ref, o_ref):
        acc = land_ref[0].astype(F32)
        for k in range(1, N_CHIPS):
            acc = acc + land_ref[k].astype(F32)
        o_ref[...] = acc

    grid_spec = pltpu.PrefetchScalarGridSpec(
        num_scalar_prefetch=1, grid=(nr,),
        in_specs=[pl.BlockSpec((N_CHIPS, tr, C), lambda r, core_ref: (0, r, 0)), ANY],
        out_specs=pl.BlockSpec((None, tr, C), lambda r, core_ref: (l, core_ref[0] * nr + r, 0)))
    return pl.pallas_call(body, name=name, grid_spec=grid_spec, out_shape=_sds(base.shape, base.dtype),
                          input_output_aliases={2: 0}, compiler_params=_cparams(VMEM_BIG))(
        core.reshape(1).astype(jnp.int32), land, base)


def sibling_join(bases, name):
    n = len(bases)

    def body(*refs):
        bufs = refs[n:2 * n]
        send_sems, recv_sems = refs[2 * n:]
        x, y, c, _ = _place()
        sems = (send_sems, recv_sems)

        def half(i, hc):
            rh = bufs[i].shape[1] // 2
            return bufs[i].at[:, pl.ds(hc * rh, rh), :]

        sends = [_rcopy(half(i, c), half(i, c), sems, i, (x, y, 1 - c)) for i in range(n)]
        for cp in sends:
            cp.start()
        for i in range(n):
            _rcopy(half(i, 1 - c), half(i, 1 - c), sems, i, (x, y, 1 - c)).wait_recv()
        for cp in sends:
            cp.wait_send()

    return pl.pallas_call(
        body, name=name, out_shape=[_sds(b.shape, b.dtype) for b in bases], in_specs=[ANY] * n, out_specs=[ANY] * n,
        input_output_aliases={i: i for i in range(n)}, scratch_shapes=_dma_sems(n, 1)[:2])(*bases)


def ag_all(blk):
    M, C = blk.shape

    def body(x_ref, out_ref, send_sems, recv_sems, loc_sem):
        x, y, c, chips = _place()
        sems = (send_sems, recv_sems)
        me, sibling = (x, y, c), (x, y, 1 - c)

        def slot(px, py, pc):
            return out_ref.at[4 * px + 2 * py + pc]

        mine = pltpu.make_async_copy(x_ref, slot(*me), loc_sem)
        mine.start()
        first = [_rcopy(x_ref, slot(*me), sems, 0, sibling)]
        first += [_rcopy(x_ref, slot(*me), sems, 1 + j, (*chip, c)) for j, chip in enumerate(chips)]
        for cp in first:
            cp.start()
        passed = [_rcopy(slot(*chip, c), slot(*chip, c), sems, 4 + j, sibling) for j, chip in enumerate(chips)]
        for j, chip in enumerate(chips):
            _rcopy(slot(*chip, c), slot(*chip, c), sems, 1 + j, me).wait_recv()
            passed[j].start()
        _rcopy(slot(*sibling), slot(*sibling), sems, 0, me).wait_recv()
        for j, chip in enumerate(chips):
            _rcopy(slot(*chip, 1 - c), slot(*chip, 1 - c), sems, 4 + j, me).wait_recv()
        for cp in first + passed:
            cp.wait_send()
        mine.wait()

    return pl.pallas_call(
        body, name="ag_all", out_shape=_sds((8, M, C), blk.dtype),
        in_specs=[pl.BlockSpec(memory_space=pltpu.VMEM)], out_specs=pl.BlockSpec(memory_space=pltpu.VMEM),
        scratch_shapes=[pltpu.SemaphoreType.DMA((7,)), pltpu.SemaphoreType.DMA((7,)), pltpu.SemaphoreType.DMA(())],
        compiler_params=_cparams(VMEM_BIG))(blk)


WEIGHTS = ["ada_w", "ada_b", "ln_g", "ln_b", "ffn1_w_in", "ffn1_w_out", "ffn2_w_in", "ffn2_w_out", "mix_w_in", "mix_w_out",
           "hgrn_lb_logits", "hgrn_norm_g", "mla_q_norm_g", "mla_kv_norm_g", "mla_w_uq", "mla_w_ukv", "fox_b_f",
           "gmlp_ln_g", "gmlp_ln_b", "gmlp_w_s", "gmlp_b_s"]
SMALL = ["hgrn_lb_logits", "hgrn_norm_g", "mla_q_norm_g", "mla_kv_norm_g", "fox_b_f", "gmlp_ln_g", "gmlp_ln_b",
         "gmlp_w_s", "gmlp_b_s", "ln_g", "ln_b"]
GATHERED = ["ada_w", "ffn1_w_in", "ffn1_w_out", "ffn2_w_in", "ffn2_w_out", "mix_w_in", "mix_w_out", "mla_w_uq", "mla_w_ukv"]
REDUCED = GATHERED[1:]


def _col_shards(a):
    cols = a.shape[1] // N_CHIPS
    return jnp.stack([a[:, k * cols:(k + 1) * cols] for k in range(N_CHIPS)])


def add_kept_half(a, got, core, name):
    _, R, C = a.shape
    rh = R // 2
    tr = _row_tile(rh, C, mult=16)
    nr = rh // tr

    def body(core_ref, a_ref, b_ref, o_ref):
        o_ref[...] = (a_ref[...].astype(F32) + b_ref[...].astype(F32)).astype(o_ref.dtype)

    half = pl.BlockSpec((None, tr, C), lambda k, r, core_ref: (k, r, 0))
    grid_spec = pltpu.PrefetchScalarGridSpec(
        num_scalar_prefetch=1, grid=(N_CHIPS, nr),
        in_specs=[pl.BlockSpec((None, tr, C), lambda k, r, core_ref: (k, core_ref[0] * nr + r, 0)), half],
        out_specs=half)
    return pl.pallas_call(body, name=name, grid_spec=grid_spec, out_shape=_sds((N_CHIPS, rh, C), BF16),
                          compiler_params=_cparams(VMEM_BIG))(core.reshape(1).astype(jnp.int32), a, got)


def _rows(parts, n_rows, dtype):
    flat = jnp.concatenate([p.reshape(-1) for p in parts])
    pad = n_rows * D - flat.shape[0]
    return jnp.concatenate([flat, jnp.zeros((pad,), dtype)]).reshape(n_rows, D)


def _take(flat, shapes):
    out, o = [], 0
    for shp in shapes:
        n = int(np.prod(shp))
        out.append(flat[o:o + n].reshape(shp))
        o += n
    return out


def _round_up(n, m):
    return -(-n // m) * m


def pack_small(w):
    parts = [w[n][l] for l in range(DEPTH) for n in SMALL]
    n = sum(int(np.prod(p.shape)) for p in parts)
    return _rows(parts, _round_up(-(-n // D), 8), F32)


def unpack_small(pk, like):
    shapes = [like[n].shape[1:] for l in range(DEPTH) for n in SMALL]
    pieces = _take(pk.reshape(-1), shapes)
    names = [n for l in range(DEPTH) for n in SMALL]
    return {n: jnp.stack([p for p, m in zip(pieces, names) if m == n]) for n in SMALL}


def kernel(x, c, ada_w, ada_b, ln_g, ln_b, ffn1_w_in, ffn1_w_out, ffn2_w_in, ffn2_w_out, mix_w_in, mix_w_out, hgrn_lb_logits, hgrn_norm_g, mla_q_norm_g, mla_kv_norm_g, mla_w_uq, mla_w_ukv, fox_b_f, gmlp_ln_g, gmlp_ln_b, gmlp_w_s, gmlp_b_s, loss_target, m_ada_w, m_ada_b, m_ln_g, m_ln_b, m_ffn1_w_in, m_ffn1_w_out, m_ffn2_w_in, m_ffn2_w_out, m_mix_w_in, m_mix_w_out, m_hgrn_lb_logits, m_hgrn_norm_g, m_mla_q_norm_g, m_mla_kv_norm_g, m_mla_w_uq, m_mla_w_ukv, m_fox_b_f, m_gmlp_ln_g, m_gmlp_ln_b, m_gmlp_w_s, m_gmlp_b_s, v_ada_w, v_ada_b, v_ln_g, v_ln_b, v_ffn1_w_in, v_ffn1_w_out, v_ffn2_w_in, v_ffn2_w_out, v_mix_w_in, v_mix_w_out, v_hgrn_lb_logits, v_hgrn_norm_g, v_mla_q_norm_g, v_mla_kv_norm_g, v_mla_w_uq, v_mla_w_ukv, v_fox_b_f, v_gmlp_ln_g, v_gmlp_ln_b, v_gmlp_w_s, v_gmlp_b_s):
    w = dict(zip(WEIGHTS, (ada_w, ada_b, ln_g, ln_b, ffn1_w_in, ffn1_w_out, ffn2_w_in, ffn2_w_out, mix_w_in, mix_w_out, hgrn_lb_logits, hgrn_norm_g, mla_q_norm_g, mla_kv_norm_g, mla_w_uq, mla_w_ukv, fox_b_f, gmlp_ln_g, gmlp_ln_b, gmlp_w_s, gmlp_b_s)))
    m = dict(zip(WEIGHTS, (m_ada_w, m_ada_b, m_ln_g, m_ln_b, m_ffn1_w_in, m_ffn1_w_out, m_ffn2_w_in, m_ffn2_w_out, m_mix_w_in, m_mix_w_out, m_hgrn_lb_logits, m_hgrn_norm_g, m_mla_q_norm_g, m_mla_kv_norm_g, m_mla_w_uq, m_mla_w_ukv, m_fox_b_f, m_gmlp_ln_g, m_gmlp_ln_b, m_gmlp_w_s, m_gmlp_b_s)))
    v = dict(zip(WEIGHTS, (v_ada_w, v_ada_b, v_ln_g, v_ln_b, v_ffn1_w_in, v_ffn1_w_out, v_ffn2_w_in, v_ffn2_w_out, v_mix_w_in, v_mix_w_out, v_hgrn_lb_logits, v_hgrn_norm_g, v_mla_q_norm_g, v_mla_kv_norm_g, v_mla_w_uq, v_mla_w_ukv, v_fox_b_f, v_gmlp_ln_g, v_gmlp_ln_b, v_gmlp_w_s, v_gmlp_b_s)))
    Bl, S, _ = x.shape
    T = Bl * S
    core = lax.axis_index("c")
    chip = 2 * lax.axis_index("x") + lax.axis_index("y")

    def shard(key):
        n, l = key
        if n == "ln":
            return jnp.concatenate([ln_g[l:l + 1], ln_b[l:l + 1], jnp.zeros((1, 2, D // N_CHIPS), F32)], axis=1)
        return w[n][l:l + 1].astype(BF16)

    mixers = ["mix_w_in", "mix_w_out", "mla_w_uq", "mla_w_ukv"]
    groups = [[("ada_w", 0), ("ffn1_w_in", 0), ("ffn1_w_out", 0), ("ln", 0)],
              [(n, 0) for n in mixers + ["ffn2_w_in", "ffn2_w_out"]],
              [(n, 1) for n in GATHERED + ["ln"]]]
    srcs = [[shard(k) for k in grp] for grp in groups]
    lands = [[own_slot(s, chip) for s in srcs[0]]]
    handle0 = ag_start(srcs[0], lands[0], jnp.zeros((8, 128), F32), "ag_start_0")
    chip_later = chip + handle0[-1][0, 0].astype(jnp.int32)
    lands += [[own_slot(s, chip_later) for s in grp] for grp in srcs[1:]]
    first, token = ag_forward(ag_wait(handle0, lands[2][0], "ag_wait_0")[1], "ag_forward_0")
    have = dict(zip(groups[0], first))
    handles = {}
    for gi in (1, 2):
        handles[gi] = ag_start(srcs[gi], lands[gi], token, "ag_start_%d" % gi)
        token = handles[gi][-1]
    c8 = jnp.concatenate([c, jnp.zeros((8 - Bl, D), F32)], axis=0)
    c8 = c8 + token[0, 0]

    def cat_cols(a):
        return jnp.concatenate([a[0, k] for k in range(N_CHIPS)], axis=1)

    def get(l, part, after):
        gi = 2 if l == 1 else (0 if part in ("ada", "ffn1") else 1)
        if gi in handles:
            arrived, _ = ag_forward(ag_wait(handles.pop(gi), after, "ag_wait_%d" % gi)[1], "ag_forward_%d" % gi)
            have.update(zip(groups[gi], arrived))
        if part == "ada":
            return dict(ada_w=have[("ada_w", l)], wl=0, ada_b=ada_b[l][None])
        if part == "ffn1":
            ln_full = jnp.moveaxis(have[("ln", l)][0], 0, 1).reshape(8, D)
            return dict(ffn1_in=have[("ffn1_w_in", l)], ffn1_out=have[("ffn1_w_out", l)], wl=0,
                        ln_g=ln_full[0:3], ln_b=ln_full[3:6])
        if part == "ffn2":
            return dict(ffn2_in=have[("ffn2_w_in", l)], ffn2_out=have[("ffn2_w_out", l)])
        return dict(
            mix_in=mix_in_ext(cat_cols(have[("mix_w_in", l)])), mix_out=mix_out_ext(have[("mix_w_out", l)].reshape(D, D)),
            wq=uq_ext(cat_cols(have[("mla_w_uq", l)])).astype(F32), wkv=ukv_ext(cat_cols(have[("mla_w_ukv", l)])).astype(F32),
            ng=hgrn_norm_g[l][None], gq=mla_q_norm_g[l][None], gkv=mla_kv_norm_g[l][None],
            bcol=jnp.concatenate([fox_b_f[l], jnp.zeros((8 - HEADS,), F32)])[:, None],
            g_lg=gmlp_ln_g[l][None], g_lb=gmlp_ln_b[l][None], ws=gmlp_w_s[l], bs=gmlp_b_s[l][:, :, None])

    pending = []

    def emit(l, part, g):
        if part == "mix":
            names = mixers
            by_chip = [_col_shards(mix_in_unext(g["mix_in"])), mix_out_unext(g["mix_out"]).reshape(N_CHIPS, D // N_CHIPS, D),
                       _col_shards(uq_unext(g["wq"])).astype(BF16), _col_shards(ukv_unext(g["wkv"])).astype(BF16)]
        else:
            names = [part + "_w_in", part + "_w_out"]
            by_chip = [g[part + "_in"], g[part + "_out"]]
        tag = "%d_%s" % (l, part)
        got = sibling_swap(by_chip, "sibling_swap_" + tag)
        chip_sum = [add_kept_half(a, r, core, "add_sibling") for a, r in zip(by_chip, got)]
        zones = [lax.dynamic_update_slice(lax.empty(h.shape, h.dtype), lax.dynamic_slice_in_dim(h, chip, 1, axis=0), (chip, 0, 0))
                 for h in chip_sum]
        handle = rs_start(chip_sum, zones, chip_sum[0], "rs_start_" + tag)
        pending.append((l, names, handle, tag))
        return handle[-1][0, 0]

    loss_tile, dx, dmods, grads, d_logits = local_step(
        x.reshape(T, D), c8, loss_target.reshape(T, D), get, hgrn_lb_logits, S, emit)
    loss = lax.psum(loss_tile[0, 0], ("x", "y", "c"))

    small_g = {"hgrn_lb_logits": d_logits,
               "hgrn_norm_g": jnp.stack([grads[l]["ng"][0] for l in range(DEPTH)]),
               "mla_q_norm_g": jnp.stack([grads[l]["gq"][0] for l in range(DEPTH)]),
               "mla_kv_norm_g": jnp.stack([grads[l]["gkv"][0] for l in range(DEPTH)]),
               "fox_b_f": jnp.stack([grads[l]["bcol"][0:HEADS, 0] for l in range(DEPTH)]),
               "gmlp_ln_g": jnp.stack([grads[l]["g_lg"][0] for l in range(DEPTH)]),
               "gmlp_ln_b": jnp.stack([grads[l]["g_lb"][0] for l in range(DEPTH)]),
               "gmlp_w_s": jnp.stack([grads[l]["ws"] for l in range(DEPTH)]),
               "gmlp_b_s": jnp.stack([grads[l]["bs"][:, :, 0] for l in range(DEPTH)])}
    small_g["ln_g"] = jnp.stack([grads[l]["ln_g"] for l in range(DEPTH)])
    small_g["ln_b"] = jnp.stack([grads[l]["ln_b"] for l in range(DEPTH)])
    pk_small = pack_small(small_g)
    n_small = pk_small.shape[0]
    extras = [dmods[l] for l in range(DEPTH)] + [c]
    n_extra = _round_up(-(-sum(int(np.prod(e.shape)) for e in extras) // D), 8)
    gathered = ag_all(jnp.concatenate([pk_small, _rows(extras, n_extra, F32)], axis=0))
    g_small = unpack_small(sum_slots(gathered[:, 0:n_small], 8, "sum_small"), small_g)
    ext = gathered[:, n_small:].reshape(8, -1)
    n_dmod = DEPTH * Bl * N_MOD * D
    dmod_all = ext[:, 0:n_dmod].reshape(8, DEPTH, Bl, N_MOD * D)
    c_all = ext[:, n_dmod:n_dmod + Bl * D].reshape(8 * Bl, D)
    g_ada_w, g_ada_b = [], []
    ncol = N_MOD * D // N_CHIPS
    for l in range(DEPTH):
        dm = dmod_all[:, l].reshape(8 * Bl, N_MOD * D)
        g_ada_w.append(ada_grad(c_all, lax.dynamic_slice_in_dim(dm, chip * ncol, ncol, axis=1)))
        g_ada_b.append(sum_slots(dm.reshape(8 * Bl, N_MOD, D), 8 * Bl, "sum_ada_b").reshape(N_MOD * D))
    g_ada_w, g_ada_b = jnp.stack(g_ada_w), jnp.stack(g_ada_b)

    red = {n: lax.empty(w[n].shape, F32) for n in REDUCED}

    def arrive(entry, after):
        l, names, handle, tag = entry
        for n, land in zip(names, rs_wait(handle, after, "rs_wait_" + tag)[1]):
            red[n] = sum_into(land, red[n], l, core, "sum_chips")

    for entry in pending[:-1]:
        arrive(entry, dx)
    late = pending[-1][1]
    early = [n for n in REDUCED if n not in late]
    grad = dict(zip(early, sibling_join([red[n] for n in early], "sibling_join_a")))
    grad.update(g_small)
    grad["ada_w"], grad["ada_b"] = g_ada_w, g_ada_b
    for n in ("ln_g", "ln_b"):
        grad[n] = lax.dynamic_slice_in_dim(g_small[n], chip * (D // N_CHIPS), D // N_CHIPS, axis=2)
    out = {"grad": grad, "delta": {}, "new_m": {}, "new_v": {}}

    def update(n):
        shp = w[n].shape
        two_d = (-1, shp[-1])
        res = adamw(w[n].reshape(two_d), grad[n].reshape(two_d), m[n].reshape(two_d), v[n].reshape(two_d), "adamw_" + n,
                    echo=n in REDUCED)
        grad[n] = (res[3] if n in REDUCED else grad[n]).reshape(shp)
        for key, r in zip(("delta", "new_m", "new_v"), res):
            out[key][n] = r.reshape(shp)

    for n in WEIGHTS:
        if n not in late:
            update(n)
    arrive(pending[-1], out["delta"]["ffn2_w_in"])
    grad.update(zip(late, sibling_join([red[n] for n in late], "sibling_join_b")))
    for n in late:
        update(n)
    outs = [loss, dx.reshape(Bl, S, D)]
    for key in ("grad", "delta", "new_m", "new_v"):
        outs += [out[key][n] for n in WEIGHTS]
    return tuple(outs)
```

```python
import functools

import jax
import jax.numpy as jnp
import numpy as np
from jax import lax
from jax.experimental import pallas as pl
from jax.experimental.pallas import tpu as pltpu

F32, BF16 = jnp.float32, jnp.bfloat16
MESH = pl.DeviceIdType.MESH

N_CHIPS = 4
D = 1024
DEPTH = 2
FF = 2816
N_MOD = 9
GW = 256
HEADS = 4
HD = 64
HP = 128
A_CHUNK = 16
LB_FLOOR = 1e-30
B_Q_LORA, B_KV_LORA, B_NOPE, B_ROPE = 256, 128, 64, 32
ROPE_THETA = 10000.0
D_CHUNK = 128
ALPHA = (2 * DEPTH) ** 0.25
LN_EPS = 1e-5
RMS_EPS = 1e-6
ADAM_LR, ADAM_B1, ADAM_B2, ADAM_EPS, ADAM_WD, ADAM_STEP = 0.001, 0.9, 0.999, 1e-08, 0.01, 10

NP = 3840
NP_TILE = 1920
C_A, C_B, C_CQ, C_CK, C_CV, C_D, C_CF = 0, 1024, 1536, 2048, 2560, 3072, 3584
NCAT = 1536
O_A, O_B, O_C, O_D = 0, 256, 768, 1280

VMEM_BIG = 48 << 20
VMEM_MOST = 58 << 20


def _cparams(vmem=None):
    return pltpu.CompilerParams(vmem_limit_bytes=vmem) if vmem else pltpu.CompilerParams()


def _sds(shape, dtype):
    return jax.ShapeDtypeStruct(tuple(shape), dtype)


@jax.custom_vjp
def _mm(a, w):
    return jnp.dot(a.astype(BF16), w.astype(BF16), preferred_element_type=F32)


def _mm_f(a, w):
    return _mm(a, w), (a, w)


def _mm_b(res, g):
    a, w = res
    gb = g.astype(BF16)
    da = lax.dot_general(gb, w.astype(BF16), (((1,), (1,)), ((), ())), preferred_element_type=F32)
    dw = lax.dot_general(a.astype(BF16), gb, (((0,), (0,)), ((), ())), preferred_element_type=F32)
    return da.astype(a.dtype), dw.astype(w.dtype)


_mm.defvjp(_mm_f, _mm_b)


@jax.custom_vjp
def _mm_nt(a, b):
    return lax.dot_general(a.astype(BF16), b.astype(BF16), (((1,), (1,)), ((), ())), preferred_element_type=F32)


def _mm_nt_f(a, b):
    return _mm_nt(a, b), (a, b)


def _mm_nt_b(res, g):
    a, b = res
    gb = g.astype(BF16)
    da = jnp.dot(gb, b.astype(BF16), preferred_element_type=F32)
    db = lax.dot_general(gb, a.astype(BF16), (((0,), (0,)), ((), ())), preferred_element_type=F32)
    return da.astype(a.dtype), db.astype(b.dtype)


_mm_nt.defvjp(_mm_nt_f, _mm_nt_b)


@jax.custom_vjp
def _mm_tn(a, b):
    return lax.dot_general(a.astype(BF16), b.astype(BF16), (((0,), (0,)), ((), ())), preferred_element_type=F32)


def _mm_tn_f(a, b):
    return _mm_tn(a, b), (a, b)


def _mm_tn_b(res, g):
    a, b = res
    gb = g.astype(BF16)
    da = lax.dot_general(b.astype(BF16), gb, (((1,), (1,)), ((), ())), preferred_element_type=F32)
    db = jnp.dot(a.astype(BF16), gb, preferred_element_type=F32)
    return da.astype(a.dtype), db.astype(b.dtype)


_mm_tn.defvjp(_mm_tn_f, _mm_tn_b)


def _split3(x):
    p1 = x.astype(BF16)
    r = x - p1.astype(F32)
    p2 = r.astype(BF16)
    return p1, p2, (r - p2.astype(F32)).astype(BF16)


@jax.custom_vjp
def _sel_r(x, sel):
    s = sel.astype(BF16)
    return sum(jnp.dot(p, s, preferred_element_type=F32) for p in _split3(x))


def _sel_r_f(x, sel):
    return _sel_r(x, sel), sel


def _sel_r_b(sel, g):
    s = sel.astype(BF16)
    dx = sum(lax.dot_general(p, s, (((1,), (1,)), ((), ())), preferred_element_type=F32) for p in _split3(g))
    return dx, jnp.zeros_like(sel)


_sel_r.defvjp(_sel_r_f, _sel_r_b)


@jax.custom_vjp
def _sel_l(sel, x):
    s = sel.astype(BF16)
    return sum(jnp.dot(s, p, preferred_element_type=F32) for p in _split3(x))


def _sel_l_f(sel, x):
    return _sel_l(sel, x), sel


def _sel_l_b(sel, g):
    s = sel.astype(BF16)
    dx = sum(lax.dot_general(s, p, (((0,), (0,)), ((), ())), preferred_element_type=F32) for p in _split3(g))
    return jnp.zeros_like(sel), dx


_sel_l.defvjp(_sel_l_f, _sel_l_b)


def _iota(shape, dim):
    return lax.broadcasted_iota(jnp.int32, shape, dim)


def _head_sum_mats():
    e = (_iota((GW, HP), 0) // HD == _iota((GW, HP), 1)).astype(F32)
    et = (_iota((HP, GW), 1) // HD == _iota((HP, GW), 0)).astype(F32)
    return e, et


def _modulate(x, mod_ref, sh, sc):
    return x * (1.0 + mod_ref[sc:sc + 1, :]) + mod_ref[sh:sh + 1, :]


def _ln_res(x, y, gate, lg, lb, gs):
    r = ALPHA * x + gs * (1.0 + gate) * y
    mu = jnp.mean(r, axis=-1, keepdims=True)
    var = jnp.mean(jnp.square(r - mu), axis=-1, keepdims=True)
    return (r - mu) * lax.rsqrt(var + LN_EPS) * lg + lb


def _rms(x, g):
    return x * lax.rsqrt(jnp.mean(x * x, axis=-1, keepdims=True) + RMS_EPS) * g


def _tile(n, pref):
    return pref if n % pref == 0 else n


def mod_fwd(c8, w, l, b):
    tn = w.shape[3]
    n = N_CHIPS * tn

    def body(c_ref, w_ref, b_ref, o_ref):
        h = jax.nn.silu(c_ref[...]).astype(BF16)
        o_ref[...] = jnp.dot(h, w_ref[...], preferred_element_type=F32) + b_ref[...]

    return pl.pallas_call(
        body, name="mod_fwd", grid=(N_CHIPS,),
        in_specs=[pl.BlockSpec((8, D), lambda j: (0, 0)), pl.BlockSpec((None, None, D, tn), lambda j: (l, j, 0, 0)),
                  pl.BlockSpec((1, tn), lambda j: (0, j))],
        out_specs=pl.BlockSpec((8, tn), lambda j: (0, j)), out_shape=_sds((8, n), F32),
        compiler_params=_cparams(VMEM_BIG))(c8, w, b)


def ffn_in_fwd(x, mod, w_in, l, sh, sc, S):
    T = x.shape[0]
    tm, tn = _tile(S, 1024), FF // 2
    tpb, nj = S // tm, 2

    def body(x_ref, mod_ref, wg_ref, wu_ref, zg_ref, zu_ref, act_ref, h_ref):
        @pl.when(pl.program_id(1) == 0)
        def _():
            h_ref[...] = _modulate(x_ref[...], mod_ref, sh, sc).astype(BF16)
        g = jnp.dot(h_ref[...], wg_ref[...], preferred_element_type=F32)
        u = jnp.dot(h_ref[...], wu_ref[...], preferred_element_type=F32)
        zg_ref[...] = g.astype(BF16)
        zu_ref[...] = u.astype(BF16)
        act_ref[...] = (jax.nn.silu(g) * u).astype(BF16)

    return pl.pallas_call(
        body, name="ffn_in_fwd", grid=(T // tm, nj),
        in_specs=[pl.BlockSpec((tm, D), lambda i, j: (i, 0)),
                  pl.BlockSpec((None, N_MOD, D), lambda i, j: (i // tpb, 0, 0)),
                  pl.BlockSpec((None, None, D, tn), lambda i, j: (l, j, 0, 0)),
                  pl.BlockSpec((None, None, D, tn), lambda i, j: (l, j + nj, 0, 0))],
        out_specs=[pl.BlockSpec((tm, tn), lambda i, j: (i, j))] * 3,
        out_shape=[_sds((T, FF), BF16)] * 3,
        scratch_shapes=[pltpu.VMEM((tm, D), BF16)],
        compiler_params=_cparams(VMEM_BIG))(x, mod, w_in, w_in)


def mix_in_fwd(x, mod, w, sh, sc, S):
    T = x.shape[0]
    n = w.shape[1]
    tm, tn = _tile(S, 1024), NP_TILE
    tpb = S // tm

    def body(x_ref, mod_ref, w_ref, o_ref, h_ref):
        @pl.when(pl.program_id(1) == 0)
        def _():
            h_ref[...] = _modulate(x_ref[...], mod_ref, sh, sc).astype(BF16)
        o_ref[...] = jnp.dot(h_ref[...], w_ref[...], preferred_element_type=F32)

    return pl.pallas_call(
        body, name="mix_in_fwd", grid=(T // tm, n // tn),
        in_specs=[pl.BlockSpec((tm, D), lambda i, j: (i, 0)),
                  pl.BlockSpec((None, N_MOD, D), lambda i, j: (i // tpb, 0, 0)),
                  pl.BlockSpec((D, tn), lambda i, j: (0, j))],
        out_specs=pl.BlockSpec((tm, tn), lambda i, j: (i, j)), out_shape=_sds((T, n), F32),
        scratch_shapes=[pltpu.VMEM((tm, D), BF16)],
        compiler_params=_cparams(VMEM_BIG))(x, mod, w)


def out_ln_fwd(act, w_out, x, mod, lg, lb, gate, gs, S, l=None):
    T, K = act.shape
    tm = _tile(S, 512)
    tpb = S // tm

    def body(a_ref, w_ref, x_ref, mod_ref, lg_ref, lb_ref, y_ref, xn_ref):
        y = jnp.dot(a_ref[...], w_ref[...].reshape(K, D), preferred_element_type=F32)
        y_ref[...] = y
        xn_ref[...] = _ln_res(x_ref[...], y, mod_ref[gate:gate + 1, :], lg_ref[...], lb_ref[...], gs)

    if l is None:
        w_spec = pl.BlockSpec((K, D), lambda i: (0, 0))
    else:
        w_spec = pl.BlockSpec((None, N_CHIPS, K // N_CHIPS, D), lambda i: (l, 0, 0, 0))
    return pl.pallas_call(
        body, name="out_ln_fwd", grid=(T // tm,),
        in_specs=[pl.BlockSpec((tm, K), lambda i: (i, 0)), w_spec,
                  pl.BlockSpec((tm, D), lambda i: (i, 0)),
                  pl.BlockSpec((None, N_MOD, D), lambda i: (i // tpb, 0, 0)),
                  pl.BlockSpec((1, D), lambda i: (0, 0)), pl.BlockSpec((1, D), lambda i: (0, 0))],
        out_specs=[pl.BlockSpec((tm, D), lambda i: (i, 0))] * 2,
        out_shape=[_sds((T, D), F32), _sds((T, D), F32)],
        compiler_params=_cparams(VMEM_BIG))(act, w_out, x, mod, lg, lb)


def ln_res_bwd(dxn, x, y, mod, lg, lb, gate, gs, S):
    T = x.shape[0]
    B = T // S
    tm = _tile(S, 512)
    tpb = S // tm

    def body(d_ref, x_ref, y_ref, mod_ref, lg_ref, lb_ref, dx_ref, dy_ref, dg_ref, dlg_ref, dlb_ref):
        i = pl.program_id(0)
        f = functools.partial(_ln_res, gs=gs)
        _, vjp = jax.vjp(f, x_ref[...], y_ref[...], mod_ref[gate:gate + 1, :], lg_ref[...], lb_ref[...])
        dx, dy, dg, dlg, dlb = vjp(d_ref[...])
        dx_ref[...] = dx
        dy_ref[...] = dy.astype(BF16)

        @pl.when(i % tpb == 0)
        def _():
            dg_ref[...] = jnp.zeros_like(dg_ref)

        @pl.when(i == 0)
        def _():
            dlg_ref[...] = jnp.zeros_like(dlg_ref)
            dlb_ref[...] = jnp.zeros_like(dlb_ref)

        dg_ref[...] += dg
        dlg_ref[...] += dlg
        dlb_ref[...] += dlb

    tok = pl.BlockSpec((tm, D), lambda i: (i, 0))
    vec = pl.BlockSpec((1, D), lambda i: (0, 0))
    return pl.pallas_call(
        body, name="ln_res_bwd", grid=(T // tm,),
        in_specs=[tok, tok, tok, pl.BlockSpec((None, N_MOD, D), lambda i: (i // tpb, 0, 0)), vec, vec],
        out_specs=[tok, tok, pl.BlockSpec((None, 1, D), lambda i: (i // tpb, 0, 0)), vec, vec],
        out_shape=[_sds((T, D), F32), _sds((T, D), BF16), _sds((B, 1, D), F32), _sds((1, D), F32), _sds((1, D), F32)],
        compiler_params=_cparams(VMEM_BIG))(dxn, x, y, mod, lg, lb)


def swiglu_bwd(dy, w_out, l, zg, zu, S):
    T = dy.shape[0]
    tm, tn = _tile(S, 1024), FF // 2

    def body(dy_ref, w_ref, zg_ref, zu_ref, dg_ref, du_ref):
        da = lax.dot_general(dy_ref[...], w_ref[...].reshape(tn, D), (((1,), (1,)), ((), ())), preferred_element_type=F32)
        g, u = zg_ref[...].astype(F32), zu_ref[...].astype(F32)
        sg = jax.nn.sigmoid(g)
        dg_ref[...] = (da * u * (sg * (1.0 + g * (1.0 - sg)))).astype(BF16)
        du_ref[...] = (da * (g * sg)).astype(BF16)

    zt = pl.BlockSpec((tm, tn), lambda i, j: (i, j))
    return pl.pallas_call(
        body, name="swiglu_bwd", grid=(T // tm, FF // tn),
        in_specs=[pl.BlockSpec((tm, D), lambda i, j: (i, 0)),
                  pl.BlockSpec((None, 2, FF // N_CHIPS, D), lambda i, j: (l, j, 0, 0)), zt, zt],
        out_specs=[zt, zt], out_shape=[_sds((T, FF), BF16), _sds((T, FF), BF16)],
        compiler_params=_cparams(VMEM_BIG))(dy, w_out, zg, zu)


def nt_plain(dy, w):
    T = dy.shape[0]
    K = w.shape[0]
    tm = _tile(T, 1024)

    def body(dy_ref, w_ref, o_ref):
        o_ref[...] = lax.dot_general(dy_ref[...], w_ref[...], (((1,), (1,)), ((), ())), preferred_element_type=F32)

    return pl.pallas_call(
        body, name="nt_plain", grid=(T // tm,),
        in_specs=[pl.BlockSpec((tm, D), lambda i: (i, 0)), pl.BlockSpec((K, D), lambda i: (0, 0))],
        out_specs=pl.BlockSpec((tm, K), lambda i: (i, 0)), out_shape=_sds((T, K), F32),
        compiler_params=_cparams(VMEM_BIG))(dy, w)


def _tn_step(acc, o_ref, lhs, rhs, t, nt):
    part = lax.dot_general(lhs, rhs, (((0,), (0,)), ((), ())), preferred_element_type=F32)
    if nt == 1:
        o_ref[...] = part.astype(o_ref.dtype)
        return

    @pl.when(t == 0)
    def _():
        acc[...] = part

    @pl.when((t > 0) & (t < nt - 1))
    def _():
        acc[...] += part

    @pl.when(t == nt - 1)
    def _():
        o_ref[...] = (acc[...] + part).astype(o_ref.dtype)


def tn_mm(a, b, tk):
    T, K = a.shape
    N = b.shape[1]
    tt = _tile(T, 1024)
    nt = T // tt

    def body(a_ref, b_ref, o_ref, acc):
        _tn_step(acc, o_ref, a_ref[...], b_ref[...], pl.program_id(1), nt)

    return pl.pallas_call(
        body, name="tn_mm", grid=(K // tk, nt),
        in_specs=[pl.BlockSpec((tt, tk), lambda k, t: (t, k)), pl.BlockSpec((tt, N), lambda k, t: (t, 0))],
        out_specs=pl.BlockSpec((tk, N), lambda k, t: (k, 0)), out_shape=_sds((K, N), BF16),
        scratch_shapes=[pltpu.VMEM((tk, N), F32)], compiler_params=_cparams(VMEM_BIG))(a, b)


def tn_mm_mod(x, mod, b, sh, sc, S, tn):
    T = x.shape[0]
    N = b.shape[1]
    tt = _tile(S, 1024)
    tpb = S // tt
    nt = T // tt

    def body(x_ref, mod_ref, b_ref, o_ref, acc):
        h = _modulate(x_ref[...], mod_ref, sh, sc).astype(BF16)
        _tn_step(acc, o_ref, h, b_ref[...], pl.program_id(1), nt)

    return pl.pallas_call(
        body, name="tn_mm_mod", grid=(N // tn, nt),
        in_specs=[pl.BlockSpec((tt, D), lambda j, t: (t, 0)),
                  pl.BlockSpec((None, N_MOD, D), lambda j, t: (t // tpb, 0, 0)),
                  pl.BlockSpec((tt, tn), lambda j, t: (t, j))],
        out_specs=pl.BlockSpec((D, tn), lambda j, t: (0, j)), out_shape=_sds((D, N), BF16),
        scratch_shapes=[pltpu.VMEM((D, tn), F32)], compiler_params=_cparams(VMEM_BIG))(x, mod, b)


def tn_mm_mod_shards(x, mod, bg, bu, sh, sc, S):
    T = x.shape[0]
    tn = FF // 2
    tt = _tile(S, 1024)
    tpb = S // tt
    nt = T // tt

    def body(x_ref, mod_ref, bg_ref, bu_ref, o_ref, acc):
        j, t = pl.program_id(0), pl.program_id(1)
        h = _modulate(x_ref[...], mod_ref, sh, sc).astype(BF16)

        @pl.when(j < 2)
        def _():
            _tn_step(acc, o_ref, h, bg_ref[...], t, nt)

        @pl.when(j >= 2)
        def _():
            _tn_step(acc, o_ref, h, bu_ref[...], t, nt)

    return pl.pallas_call(
        body, name="tn_mm_mod_shards", grid=(N_CHIPS, nt),
        in_specs=[pl.BlockSpec((tt, D), lambda j, t: (t, 0)),
                  pl.BlockSpec((None, N_MOD, D), lambda j, t: (t // tpb, 0, 0)),
                  pl.BlockSpec((tt, tn), lambda j, t: (jnp.where(j < 2, t, 0), jnp.minimum(j, 1))),
                  pl.BlockSpec((tt, tn), lambda j, t: (jnp.where(j < 2, 0, t), jnp.maximum(j - 2, 0)))],
        out_specs=pl.BlockSpec((None, D, tn), lambda j, t: (j, 0, 0)), out_shape=_sds((N_CHIPS, D, tn), BF16),
        scratch_shapes=[pltpu.VMEM((D, tn), F32)], compiler_params=_cparams(VMEM_BIG))(x, mod, bg, bu)


def nt_mod_bwd(ds, w, offs, x, mod, dres, sc, S, tk, l=None):
    T = x.shape[0]
    B = T // S
    tm = _tile(S, 1024)
    tpb = S // tm
    Kd = ds[0].shape[1]
    nk = Kd // tk
    n_in = len(ds)

    def body(*refs):
        d_refs, w_refs = refs[:n_in], refs[n_in:2 * n_in]
        x_ref, mod_ref, r_ref, dx_ref, dsh_ref, dsc_ref, acc = refs[2 * n_in:]
        i, k = pl.program_id(0), pl.program_id(1)

        part = sum(lax.dot_general(d_ref[...], w_ref[...], (((1,), (1,)), ((), ())), preferred_element_type=F32)
                   for d_ref, w_ref in zip(d_refs, w_refs))

        @pl.when(k == 0)
        def _():
            acc[...] = part

        @pl.when(k > 0)
        def _():
            acc[...] += part

        @pl.when(k == nk - 1)
        def _():
            dh = acc[...]
            dx_ref[...] = dh * (1.0 + mod_ref[sc:sc + 1, :]) + r_ref[...]

            @pl.when(i % tpb == 0)
            def _():
                dsh_ref[...] = jnp.zeros_like(dsh_ref)
                dsc_ref[...] = jnp.zeros_like(dsc_ref)

            dsh_ref[...] += jnp.sum(dh, axis=0, keepdims=True)
            dsc_ref[...] += jnp.sum(dh * x_ref[...], axis=0, keepdims=True)

    tok = pl.BlockSpec((tm, D), lambda i, k: (i, 0))
    vec = pl.BlockSpec((None, 1, D), lambda i, k: (i // tpb, 0, 0))
    in_specs = [pl.BlockSpec((tm, tk), lambda i, k: (i, k)) for _ in ds]
    if l is None:
        in_specs += [pl.BlockSpec((D, tk), functools.partial(lambda i, k, o: (0, k + o), o=off // tk)) for off in offs]
    else:
        in_specs += [pl.BlockSpec((None, None, D, tk), functools.partial(lambda i, k, o: (l, k + o, 0, 0), o=off)) for off in offs]
    in_specs += [tok, pl.BlockSpec((None, N_MOD, D), lambda i, k: (i // tpb, 0, 0)), tok]
    return pl.pallas_call(
        body, name="nt_mod_bwd", grid=(T // tm, nk), in_specs=in_specs,
        out_specs=[tok, vec, vec],
        out_shape=[_sds((T, D), F32), _sds((B, 1, D), F32), _sds((B, 1, D), F32)],
        scratch_shapes=[pltpu.VMEM((tm, D), F32)],
        compiler_params=_cparams(VMEM_MOST))(*ds, *([w] * n_in), x, mod, dres)


def loss_head(y, tgt):
    T = y.shape[0]
    tm = _tile(T, 512)

    def body(y_ref, t_ref, l_ref, d_ref):
        @pl.when(pl.program_id(0) == 0)
        def _():
            l_ref[...] = jnp.zeros_like(l_ref)
        e = y_ref[...] - t_ref[...]
        d_ref[...] = e * (1.0 / D)
        l_ref[...] += 0.5 * jnp.sum(jnp.sum(e * e, axis=1, keepdims=True) * (1.0 / D))

    tok = pl.BlockSpec((tm, D), lambda i: (i, 0))
    return pl.pallas_call(
        body, name="loss_head", grid=(T // tm,), in_specs=[tok, tok],
        out_specs=[pl.BlockSpec((8, 128), lambda i: (0, 0)), tok],
        out_shape=[_sds((8, 128), F32), _sds((T, D), F32)],
        compiler_params=_cparams(VMEM_BIG))(y, tgt)


def _hgrn_block(q, fz, inp, go, st, lb, ng, blk):
    nc = blk // A_CHUNK
    lb_eff = jnp.maximum(lb, LB_FLOOR)
    log_f = jnp.logaddexp(jnp.log(lb_eff), jnp.log1p(-lb) + jax.nn.log_sigmoid(fz))
    k = (1.0 - lb) * jax.nn.sigmoid(-fz) - (lb_eff - lb)
    qf = jax.nn.silu(q)
    same_chunk = _iota((blk, blk), 0) // A_CHUNK == _iota((blk, blk), 1) // A_CHUNK
    tril = (same_chunk & (_iota((blk, blk), 1) <= _iota((blk, blk), 0))).astype(F32)
    G = _sel_l(tril, log_f)
    e_mat, et_mat = _head_sum_mats()
    G4, q4, k4, v4 = (z.reshape(nc, A_CHUNK, GW) for z in (G, qf, k, inp))
    shp = (nc, A_CHUNK, A_CHUNK, GW)
    one = (1, A_CHUNK, A_CHUNK, GW)
    mask = jnp.where(_iota(one, 2) <= _iota(one, 1), 0.0, -jnp.inf)
    decay = jnp.exp((G4[:, :, None, :] - G4[:, None, :, :]) + mask)
    prod = q4[:, :, None, :] * k4[:, None, :, :] * decay
    scores = _mm(prod.reshape(nc * A_CHUNK * A_CHUNK, GW), e_mat.astype(BF16))
    spread = _mm(scores, et_mat.astype(BF16)).reshape(shp)
    o_intra = jnp.sum(spread * v4[:, None, :, :], axis=2).reshape(blk, GW)
    head_diag = (_iota((GW, GW), 0) // HD == _iota((GW, GW), 1) // HD).astype(F32)
    g_last = [jnp.sum(log_f[c * A_CHUNK:(c + 1) * A_CHUNK], axis=0, keepdims=True) for c in range(nc)]
    g_last_b = jnp.concatenate([jnp.broadcast_to(g, (A_CHUNK, GW)) for g in g_last], axis=0)
    q_dec = qf * jnp.exp(G)
    k_end = k * jnp.exp(g_last_b - G)
    outs = []
    for c in range(nc):
        rows = slice(c * A_CHUNK, (c + 1) * A_CHUNK)
        outs.append(_mm_nt(q_dec[rows], st))
        st = st * jnp.exp(g_last[c]) + _mm_tn(inp[rows], k_end[rows]) * head_diag
    o = o_intra + jnp.concatenate(outs, axis=0)
    ms = _sel_r(o * o, e_mat) * (1.0 / HD)
    o = o * _sel_r(lax.rsqrt(ms + RMS_EPS), et_mat) * ng
    return o * jax.nn.silu(go), st


HGRN_BLK = 128


def hgrn_fwd(proj, lb, ng, S):
    T = proj.shape[0]
    B = T // S
    blk = min(HGRN_BLK, S)
    nb = S // blk

    def body(p_ref, lb_ref, ng_ref, o_ref, st_out_ref, st_ref):
        @pl.when(pl.program_id(1) == 0)
        def _():
            st_ref[...] = jnp.zeros_like(st_ref)
        st_out_ref[...] = st_ref[...]
        p = p_ref[...]
        o, st = _hgrn_block(p[:, 0:GW], p[:, GW:2 * GW], p[:, 2 * GW:3 * GW], p[:, 3 * GW:4 * GW],
                            st_ref[...], lb_ref[...], ng_ref[...], blk)
        o_ref[...] = o.astype(BF16)
        st_ref[...] = st

    vec = pl.BlockSpec((1, GW), lambda b, j: (0, 0))
    return pl.pallas_call(
        body, name="hgrn_fwd", grid=(B, nb),
        in_specs=[pl.BlockSpec((blk, 4 * GW), lambda b, j: (b * nb + j, C_A // (4 * GW))), vec, vec],
        out_specs=[pl.BlockSpec((blk, GW), lambda b, j: (b * nb + j, 0)),
                   pl.BlockSpec((None, GW, GW), lambda b, j: (b * nb + j, 0, 0))],
        out_shape=[_sds((T, GW), BF16), _sds((B * nb, GW, GW), F32)],
        scratch_shapes=[pltpu.VMEM((GW, GW), F32)],
        compiler_params=_cparams(VMEM_BIG))(proj, lb, ng)


def hgrn_bwd(proj, states, dcat, lb, ng, S):
    T = proj.shape[0]
    B = T // S
    blk = min(HGRN_BLK, S)
    nb = S // blk

    def body(p_ref, st_in_ref, do_ref, lb_ref, ng_ref, dp_ref, dlb_ref, dng_ref, dst_ref):
        b, j = pl.program_id(0), pl.program_id(1)

        @pl.when(j == 0)
        def _():
            dst_ref[...] = jnp.zeros_like(dst_ref)

        @pl.when((b == 0) & (j == 0))
        def _():
            dlb_ref[...] = jnp.zeros_like(dlb_ref)
            dng_ref[...] = jnp.zeros_like(dng_ref)

        p = p_ref[...]
        f = functools.partial(_hgrn_block, blk=blk)
        _, vjp = jax.vjp(f, p[:, 0:GW], p[:, GW:2 * GW], p[:, 2 * GW:3 * GW], p[:, 3 * GW:4 * GW],
                         st_in_ref[...], lb_ref[...], ng_ref[...])
        dq, df, di, dg, dst, dlb, dng = vjp((do_ref[...], dst_ref[...]))
        dp_ref[...] = jnp.concatenate([dq, df, di, dg], axis=1).astype(BF16)
        dst_ref[...] = dst
        dlb_ref[...] += dlb
        dng_ref[...] += dng

    def rev(b, j):
        return b * nb + (nb - 1 - j)

    vec = pl.BlockSpec((1, GW), lambda b, j: (0, 0))
    return pl.pallas_call(
        body, name="hgrn_bwd", grid=(B, nb),
        in_specs=[pl.BlockSpec((blk, 4 * GW), lambda b, j: (rev(b, j), C_A // (4 * GW))),
                  pl.BlockSpec((None, GW, GW), lambda b, j: (rev(b, j), 0, 0)),
                  pl.BlockSpec((blk, GW), lambda b, j: (rev(b, j), O_A // GW)), vec, vec],
        out_specs=[pl.BlockSpec((blk, 4 * GW), lambda b, j: (rev(b, j), 0)), vec, vec],
        out_shape=[_sds((T, 4 * GW), BF16), _sds((1, GW), F32), _sds((1, GW), F32)],
        scratch_shapes=[pltpu.VMEM((GW, GW), F32)],
        compiler_params=_cparams(VMEM_BIG))(proj, states, dcat, lb, ng)


ATT_TQ = 256


ATT_BANDS = 8


def _attn_block(q, k, v, cum, qpos0, scale, use_cum, n_free):
    s = _mm_nt(q, k) * scale
    if use_cum:
        s = s - cum
    band = s[:, n_free:]
    visible = _iota(band.shape, 1) <= (qpos0 - n_free) + _iota(band.shape, 0)
    band = jnp.where(visible, band, -jnp.inf)
    m = jnp.max(band, axis=-1, keepdims=True)
    if n_free:
        free = s[:, :n_free]
        m = jnp.maximum(m, jnp.max(free, axis=-1, keepdims=True))
    e = jnp.exp(band - m)
    denom = jnp.sum(e, axis=-1, keepdims=True)
    o = _mm(e, v[n_free:])
    if n_free:
        e = jnp.exp(free - m)
        denom = denom + jnp.sum(e, axis=-1, keepdims=True)
        o = o + _mm(e, v[:n_free])
    return o * (1.0 / denom)


def _bands(S, tq):
    nq = S // tq
    nb = min(ATT_BANDS, nq)
    per = nq // nb
    return [(r * per, (r + 1) * per, (r + 1) * per * tq) for r in range(nb)]


def attn_fwd(qa, qo, ka, ko, va, vo, cum, scale, S):
    T = qa.shape[0]
    B = T // S
    tq = min(ATT_TQ, S)
    nq = S // tq
    use_cum = cum is not None

    def body(*refs):
        if use_cum:
            q_ref, k_ref, v_ref, c_ref, o_ref = refs
        else:
            (q_ref, k_ref, v_ref, o_ref), c_ref = refs, None
        h, i = pl.program_id(1), pl.program_id(2)
        for lo, hi, kw in _bands(S, tq):
            @pl.when((i >= lo) & (i < hi))
            def _():
                crow = c_ref[pl.ds(h, 1), 0:kw] if use_cum else None
                o = _attn_block(q_ref[...], k_ref[0:kw, :], v_ref[0:kw, :], crow, i * tq, scale, use_cum, lo * tq)
                o_ref[...] = o.astype(BF16)

    in_specs = [pl.BlockSpec((tq, HP), lambda b, h, i: (b * nq + i, qo + h)),
                pl.BlockSpec((S, HP), lambda b, h, i: (b, ko + h)),
                pl.BlockSpec((S, HP), lambda b, h, i: (b, vo + h))]
    args = [qa, ka, va]
    if use_cum:
        in_specs.append(pl.BlockSpec((None, 8, S), lambda b, h, i: (b, 0, 0)))
        args.append(cum)
    return pl.pallas_call(
        body, name="attn_fwd", grid=(B, HEADS, nq), in_specs=in_specs,
        out_specs=pl.BlockSpec((tq, HP), lambda b, h, i: (b * nq + i, h)),
        out_shape=_sds((T, HEADS * HP), BF16),
        compiler_params=_cparams(VMEM_BIG))(*args)


def _attn_block_bwd(q, k, v, cum, do, qpos0, scale, use_cum, n_free):
    tn = (((0,), (0,)), ((), ()))
    nt = (((1,), (1,)), ((), ()))
    qb, dob = q.astype(BF16), do.astype(BF16)
    kb, vb = k.astype(BF16), v.astype(BF16)
    s = lax.dot_general(qb, kb, nt, preferred_element_type=F32) * scale
    if use_cum:
        s = s - cum
    band = s[:, n_free:]
    visible = _iota(band.shape, 1) <= (qpos0 - n_free) + _iota(band.shape, 0)
    parts = [(jnp.where(visible, band, -jnp.inf), n_free, s.shape[1])]
    if n_free:
        parts.append((s[:, :n_free], 0, n_free))
    m = functools.reduce(jnp.maximum, [jnp.max(sp, axis=-1, keepdims=True) for sp, _, _ in parts])
    es = [jnp.exp(sp - m) for sp, _, _ in parts]
    rinv = 1.0 / sum(jnp.sum(e, axis=-1, keepdims=True) for e in es)
    ps = [e * rinv for e in es]
    dps = [lax.dot_general(dob, vb[a:b], nt, preferred_element_type=F32) for _, a, b in parts]
    delta = sum(jnp.sum(p * dp, axis=-1, keepdims=True) for p, dp in zip(ps, dps))
    dq = jnp.zeros(q.shape, F32)
    out = []
    for p, dp, (_, a, b) in zip(ps, dps, parts):
        ds = p * (dp - delta)
        dsb = ds.astype(BF16)
        dq = dq + jnp.dot(dsb, kb[a:b], preferred_element_type=F32)
        out.append((a, b, lax.dot_general(dsb, qb, tn, preferred_element_type=F32) * scale,
                    lax.dot_general(p.astype(BF16), dob, tn, preferred_element_type=F32),
                    -jnp.sum(ds, axis=0, keepdims=True) if use_cum else None))
    return dq * scale, out


def attn_bwd(qa, qo, ka, ko, va, vo, cum, dcat, do_off, scale, S, out_dtype):
    T = qa.shape[0]
    B = T // S
    tq = min(ATT_TQ, S)
    nq = S // tq
    use_cum = cum is not None

    def body(*refs):
        if use_cum:
            q_ref, k_ref, v_ref, do_ref, c_ref, dq_ref, dk_ref, dv_ref, dc_ref, dk_acc, dv_acc = refs
        else:
            q_ref, k_ref, v_ref, do_ref, dq_ref, dk_ref, dv_ref, dk_acc, dv_acc = refs
        h, i = pl.program_id(1), pl.program_id(2)

        @pl.when(i == 0)
        def _():
            dk_acc[...] = jnp.zeros_like(dk_acc)
            dv_acc[...] = jnp.zeros_like(dv_acc)
            if use_cum:
                dc_ref[...] = jnp.zeros_like(dc_ref)

        for lo, hi, kw in _bands(S, tq):
            @pl.when((i >= lo) & (i < hi))
            def _():
                crow = c_ref[pl.ds(h, 1), 0:kw] if use_cum else None
                dq, pieces = _attn_block_bwd(q_ref[...], k_ref[0:kw, :], v_ref[0:kw, :], crow, do_ref[...], i * tq,
                                             scale, use_cum, lo * tq)
                dq_ref[...] = dq.astype(out_dtype)
                for a, b, dk, dv, dc in pieces:
                    dk_acc[a:b, :] += dk
                    dv_acc[a:b, :] += dv
                    if use_cum:
                        dc_ref[:, a:b] += dc

        @pl.when(i == nq - 1)
        def _():
            dk_ref[...] = dk_acc[...].astype(out_dtype)
            dv_ref[...] = dv_acc[...].astype(out_dtype)

    qspec = pl.BlockSpec((tq, HP), lambda b, h, i: (b * nq + i, qo + h))
    in_specs = [qspec, pl.BlockSpec((S, HP), lambda b, h, i: (b, ko + h)),
                pl.BlockSpec((S, HP), lambda b, h, i: (b, vo + h)),
                pl.BlockSpec((tq, HP), lambda b, h, i: (b * nq + i, do_off + h))]
    args = [qa, ka, va, dcat]
    kv_out = pl.BlockSpec((S, HP), lambda b, h, i: (b, h))
    out_specs = [pl.BlockSpec((tq, HP), lambda b, h, i: (b * nq + i, h)), kv_out, kv_out]
    out_shape = [_sds((T, HEADS * HP), out_dtype)] * 3
    if use_cum:
        in_specs.append(pl.BlockSpec((None, 8, S), lambda b, h, i: (b, 0, 0)))
        args.append(cum)
        out_specs.append(pl.BlockSpec((None, 1, S), lambda b, h, i: (b * HEADS + h, 0, 0)))
        out_shape.append(_sds((B * HEADS, 1, S), F32))
    return pl.pallas_call(
        body, name="attn_bwd", grid=(B, HEADS, nq), in_specs=in_specs, out_specs=out_specs, out_shape=out_shape,
        scratch_shapes=[pltpu.VMEM((S, HP), F32), pltpu.VMEM((S, HP), F32)],
        compiler_params=_cparams(VMEM_BIG))(*args)


def _tri(n, upper):
    r, c = _iota((n, n), 0), _iota((n, n), 1)
    return ((r <= c) if upper else (r >= c)).astype(F32)


def fox_gate_fwd(proj, bcol, S):
    T = proj.shape[0]
    B = T // S
    ts = _tile(S, 512)
    nt = S // ts

    def body(p_ref, b_ref, o_ref, carry):
        @pl.when(pl.program_id(1) == 0)
        def _():
            carry[...] = jnp.zeros_like(carry)
        cf = jnp.transpose(p_ref[...])[0:8, :]
        lf = jax.nn.log_sigmoid(cf + b_ref[...])
        cum = _sel_r(lf, _tri(ts, True)) + carry[...]
        o_ref[...] = cum
        carry[...] += jnp.sum(lf, axis=1, keepdims=True)

    return pl.pallas_call(
        body, name="fox_gate_fwd", grid=(B, nt),
        in_specs=[pl.BlockSpec((ts, HP), lambda b, j: (b * nt + j, C_CF // HP)), pl.BlockSpec((8, 1), lambda b, j: (0, 0))],
        out_specs=pl.BlockSpec((None, 8, ts), lambda b, j: (b, 0, j)), out_shape=_sds((B, 8, S), F32),
        scratch_shapes=[pltpu.VMEM((8, 1), F32)],
        compiler_params=_cparams(VMEM_BIG))(proj, bcol)


def fox_gate_bwd(proj, bcol, dcum, S):
    T = proj.shape[0]
    B = T // S
    ts = _tile(S, 512)
    nt = S // ts

    def body(p_ref, b_ref, dc_ref, dp_ref, db_ref, carry):
        b, j = pl.program_id(0), pl.program_id(1)

        @pl.when(j == 0)
        def _():
            carry[...] = jnp.zeros_like(carry)

        @pl.when((b == 0) & (j == 0))
        def _():
            db_ref[...] = jnp.zeros_like(db_ref)

        cf = jnp.transpose(p_ref[...])[0:8, :]
        dc = dc_ref[...]
        dlf = _sel_r(dc, _tri(ts, False)) + carry[...]
        carry[...] += jnp.sum(dc, axis=1, keepdims=True)
        dcf = dlf * jax.nn.sigmoid(-(cf + b_ref[...]))
        db_ref[...] += jnp.sum(dcf, axis=1, keepdims=True)
        full = jnp.concatenate([dcf, jnp.zeros((HP - 8, ts), F32)], axis=0)
        dp_ref[...] = jnp.transpose(full).astype(BF16)

    def rev(b, j):
        return nt - 1 - j

    return pl.pallas_call(
        body, name="fox_gate_bwd", grid=(B, nt),
        in_specs=[pl.BlockSpec((ts, HP), lambda b, j: (b * nt + rev(b, j), C_CF // HP)),
                  pl.BlockSpec((8, 1), lambda b, j: (0, 0)),
                  pl.BlockSpec((None, 8, ts), lambda b, j: (b, 0, rev(b, j)))],
        out_specs=[pl.BlockSpec((ts, HP), lambda b, j: (b * nt + rev(b, j), 0)), pl.BlockSpec((8, 1), lambda b, j: (0, 0))],
        out_shape=[_sds((T, HP), BF16), _sds((8, 1), F32)],
        scratch_shapes=[pltpu.VMEM((8, 1), F32)],
        compiler_params=_cparams(VMEM_BIG))(proj, bcol, dcum)


def _mla_pre(blk, gq, gkv, wq, wkv, place, cos_q, sin_q, cs_k):
    nq = _rms(blk[:, 0:B_Q_LORA], gq)
    nkv = _rms(blk[:, B_Q_LORA:B_Q_LORA + B_KV_LORA], gkv)
    qq = _mm(nq, wq)
    q = qq[:, 0:HEADS * HP] * cos_q + qq[:, HEADS * HP:] * sin_q
    kv = _mm(nkv, wkv)
    k = kv[:, 0:HEADS * HP] + _mm(blk[:, B_Q_LORA + B_KV_LORA:] * cs_k, place)
    return q, k, kv[:, HEADS * HP:]


def mla_pre_fwd(proj, gq, gkv, wq, wkv, place, cos_q, sin_q, cs_k, S):
    T = proj.shape[0]
    tm = _tile(S, 512)
    tpb = S // tm
    W = HEADS * HP

    def body(p_ref, gq_ref, gkv_ref, wq_ref, wkv_ref, pl_ref, cq_ref, sq_ref, ck_ref, q_ref, k_ref, v_ref):
        q, k, v = _mla_pre(p_ref[...], gq_ref[...], gkv_ref[...], wq_ref[...], wkv_ref[...], pl_ref[...],
                           cq_ref[...], sq_ref[...], ck_ref[...])
        q_ref[...] = q
        k_ref[...] = k
        v_ref[...] = v

    def full(a):
        return pl.BlockSpec(a.shape, lambda i: (0,) * a.ndim)

    tok = pl.BlockSpec((tm, W), lambda i: (i, 0))
    return pl.pallas_call(
        body, name="mla_pre_fwd", grid=(T // tm,),
        in_specs=[pl.BlockSpec((tm, W), lambda i: (i, C_B // W)), full(gq), full(gkv), full(wq), full(wkv), full(place),
                  pl.BlockSpec((tm, W), lambda i: (i % tpb, 0)), pl.BlockSpec((tm, W), lambda i: (i % tpb, 0)),
                  pl.BlockSpec((tm, HP), lambda i: (i % tpb, 0))],
        out_specs=[tok] * 3, out_shape=[_sds((T, W), F32)] * 3,
        compiler_params=_cparams(VMEM_BIG))(proj, gq, gkv, wq, wkv, place, cos_q, sin_q, cs_k)


def mla_pre_bwd(proj, gq, gkv, wq, wkv, place, cos_q, sin_q, cs_k, dq, dk, dv, S):
    T = proj.shape[0]
    tm = _tile(S, 512)
    tpb = S // tm
    W = HEADS * HP

    def body(p_ref, gq_ref, gkv_ref, wq_ref, wkv_ref, pl_ref, cq_ref, sq_ref, ck_ref, dq_ref, dk_ref, dv_ref,
             dp_ref, dgq_ref, dgkv_ref, dwq_ref, dwkv_ref):
        @pl.when(pl.program_id(0) == 0)
        def _():
            for r in (dgq_ref, dgkv_ref, dwq_ref, dwkv_ref):
                r[...] = jnp.zeros_like(r)

        f = functools.partial(_mla_pre, place=pl_ref[...], cos_q=cq_ref[...], sin_q=sq_ref[...], cs_k=ck_ref[...])
        _, vjp = jax.vjp(f, p_ref[...], gq_ref[...], gkv_ref[...], wq_ref[...], wkv_ref[...])
        dp, dgq, dgkv, dwq, dwkv = vjp((dq_ref[...], dk_ref[...], dv_ref[...]))
        dp_ref[...] = dp.astype(BF16)
        dgq_ref[...] += dgq
        dgkv_ref[...] += dgkv
        dwq_ref[...] += dwq
        dwkv_ref[...] += dwkv

    def full(a):
        return pl.BlockSpec(a.shape, lambda i: (0,) * a.ndim)

    tok = pl.BlockSpec((tm, W), lambda i: (i, 0))
    return pl.pallas_call(
        body, name="mla_pre_bwd", grid=(T // tm,),
        in_specs=[pl.BlockSpec((tm, W), lambda i: (i, C_B // W)), full(gq), full(gkv), full(wq), full(wkv), full(place),
                  pl.BlockSpec((tm, W), lambda i: (i % tpb, 0)), pl.BlockSpec((tm, W), lambda i: (i % tpb, 0)),
                  pl.BlockSpec((tm, HP), lambda i: (i % tpb, 0)), tok, tok, tok],
        out_specs=[tok, full(gq), full(gkv), full(wq), full(wkv)],
        out_shape=[_sds((T, W), BF16), _sds(gq.shape, F32), _sds(gkv.shape, F32), _sds(wq.shape, F32), _sds(wkv.shape, F32)],
        compiler_params=_cparams(VMEM_BIG))(proj, gq, gkv, wq, wkv, place, cos_q, sin_q, cs_k, dq, dk, dv)


def _gmlp_block(blk, lg, lb, ws, bs):
    u = jax.nn.gelu(blk[:, 0:GW])
    v = jax.nn.gelu(blk[:, GW:2 * GW])
    mu = jnp.mean(v, axis=-1, keepdims=True)
    var = jnp.mean(jnp.square(v - mu), axis=-1, keepdims=True)
    vn = (v - mu) * lax.rsqrt(var + LN_EPS) * lg + lb
    causal = _iota((D_CHUNK, D_CHUNK), 1) <= _iota((D_CHUNK, D_CHUNK), 0)
    group = _iota((1, GW), 1) // HD
    mixed = jnp.zeros((D_CHUNK, GW), F32)
    for g in range(HEADS):
        part = _mm(jnp.where(causal, ws[g], 0.0), vn) + bs[g]
        mixed = mixed + jnp.where(group == g, part, 0.0)
    return u * mixed


def gmlp_fwd(proj, lg, lb, ws, bs):
    T = proj.shape[0]

    def body(p_ref, lg_ref, lb_ref, ws_ref, bs_ref, o_ref):
        o_ref[...] = _gmlp_block(p_ref[...], lg_ref[...], lb_ref[...], ws_ref[...], bs_ref[...]).astype(BF16)

    def full(a):
        return pl.BlockSpec(a.shape, lambda i: (0,) * a.ndim)

    return pl.pallas_call(
        body, name="gmlp_fwd", grid=(T // D_CHUNK,),
        in_specs=[pl.BlockSpec((D_CHUNK, 2 * GW), lambda i: (i, C_D // (2 * GW))), full(lg), full(lb), full(ws), full(bs)],
        out_specs=pl.BlockSpec((D_CHUNK, GW), lambda i: (i, 0)), out_shape=_sds((T, GW), BF16),
        compiler_params=_cparams(VMEM_BIG))(proj, lg, lb, ws, bs)


def gmlp_bwd(proj, lg, lb, ws, bs, dcat):
    T = proj.shape[0]

    def body(p_ref, lg_ref, lb_ref, ws_ref, bs_ref, do_ref, dp_ref, dlg_ref, dlb_ref, dws_ref, dbs_ref):
        @pl.when(pl.program_id(0) == 0)
        def _():
            for r in (dlg_ref, dlb_ref, dws_ref, dbs_ref):
                r[...] = jnp.zeros_like(r)

        _, vjp = jax.vjp(_gmlp_block, p_ref[...], lg_ref[...], lb_ref[...], ws_ref[...], bs_ref[...])
        dp, dlg, dlb, dws, dbs = vjp(do_ref[...])
        dp_ref[...] = dp.astype(BF16)
        dlg_ref[...] += dlg
        dlb_ref[...] += dlb
        dws_ref[...] += dws
        dbs_ref[...] += dbs

    def full(a):
        return pl.BlockSpec(a.shape, lambda i: (0,) * a.ndim)

    return pl.pallas_call(
        body, name="gmlp_bwd", grid=(T // D_CHUNK,),
        in_specs=[pl.BlockSpec((D_CHUNK, 2 * GW), lambda i: (i, C_D // (2 * GW))), full(lg), full(lb), full(ws), full(bs),
                  pl.BlockSpec((D_CHUNK, GW), lambda i: (i, O_D // GW))],
        out_specs=[pl.BlockSpec((D_CHUNK, 2 * GW), lambda i: (i, 0)), full(lg), full(lb), full(ws), full(bs)],
        out_shape=[_sds((T, 2 * GW), BF16), _sds(lg.shape, F32), _sds(lb.shape, F32), _sds(ws.shape, F32), _sds(bs.shape, F32)],
        compiler_params=_cparams(VMEM_BIG))(proj, lg, lb, ws, bs, dcat)


def _lb_all(logits):
    m = jnp.max(logits, axis=0, keepdims=True)
    e = jnp.exp(logits - m)
    sm = e / jnp.sum(e, axis=0, keepdims=True)
    return jnp.concatenate([sm[0:1] - sm[0:1], (sm[0:1] + sm[1:2]) - sm[0:1]], axis=0)


def lb_fwd(logits):
    def body(l_ref, o_ref):
        o_ref[...] = _lb_all(l_ref[...])

    return pl.pallas_call(body, name="lb_fwd", out_shape=_sds(logits.shape, F32))(logits)


def lb_bwd(logits, dlb):
    def body(l_ref, d_ref, o_ref):
        _, vjp = jax.vjp(_lb_all, l_ref[...])
        o_ref[...] = vjp(d_ref[...])[0]

    return pl.pallas_call(body, name="lb_bwd", out_shape=_sds(logits.shape, F32))(logits, dlb)


def ada_grad(c_all, dmod_cols):
    N = dmod_cols.shape[1]
    tn = _tile(N, 1152)

    def body(c_ref, d_ref, o_ref):
        h = jax.nn.silu(c_ref[...]).astype(BF16)
        o_ref[...] = lax.dot_general(h, d_ref[...].astype(BF16), (((0,), (0,)), ((), ())), preferred_element_type=F32)

    nb = c_all.shape[0]
    return pl.pallas_call(
        body, name="ada_grad", grid=(N // tn,),
        in_specs=[pl.BlockSpec((nb, D), lambda j: (0, 0)), pl.BlockSpec((nb, tn), lambda j: (0, j))],
        out_specs=pl.BlockSpec((D, tn), lambda j: (0, j)), out_shape=_sds((D, N), F32),
        compiler_params=_cparams(VMEM_BIG))(c_all, dmod_cols)


def sum_slots(a, n, name):
    _, R, C = a.shape
    tr = _row_tile(R, C, n)

    def body(a_ref, o_ref):
        acc = a_ref[0]
        for k in range(1, n):
            acc = acc + a_ref[k]
        o_ref[...] = acc

    return pl.pallas_call(
        body, name=name, grid=(R // tr,),
        in_specs=[pl.BlockSpec((n, tr, C), lambda i: (0, i, 0))],
        out_specs=pl.BlockSpec((tr, C), lambda i: (i, 0)), out_shape=_sds((R, C), F32),
        compiler_params=_cparams(VMEM_BIG))(a)


def _row_tile(R, C=D, n=1, mult=8, elems=1 << 18):
    limit = max(mult, elems // (C * n))
    for t in range(limit - limit % mult, mult - 1, -mult):
        if R % t == 0:
            return t
    return R


def adamw(w, g, m, v, name, echo=False):
    R, C = w.shape
    tr = _row_tile(R, C, elems=1 << 19)
    c1 = 1.0 - ADAM_B1 ** ADAM_STEP
    c2 = 1.0 - ADAM_B2 ** ADAM_STEP
    n_out = 4 if echo else 3

    def body(w_ref, g_ref, m_ref, v_ref, d_ref, nm_ref, nv_ref, *g_out):
        g_ = g_ref[...]
        nm = ADAM_B1 * m_ref[...] + (1.0 - ADAM_B1) * g_
        nv = ADAM_B2 * v_ref[...] + (1.0 - ADAM_B2) * jnp.square(g_)
        d_ref[...] = -ADAM_LR * ((nm / c1) / (jnp.sqrt(nv / c2) + ADAM_EPS) + ADAM_WD * w_ref[...])
        nm_ref[...] = nm
        nv_ref[...] = nv
        if echo:
            g_out[0][...] = g_

    spec = pl.BlockSpec((tr, C), lambda i: (i, 0))
    return pl.pallas_call(body, name=name, grid=(R // tr,), in_specs=[spec] * 4, out_specs=[spec] * n_out,
                          out_shape=[_sds((R, C), F32)] * n_out, compiler_params=_cparams(VMEM_BIG))(w, g, m, v)


def _rot_cols(w):
    return jnp.concatenate([-w[:, 16:32], w[:, 0:16]], axis=1)


def _fold_rot(d):
    return jnp.concatenate([d[:, 16:32], -d[:, 0:16]], axis=1)


def _pad_heads(w, off, axis):
    parts = []
    for h in range(HEADS):
        piece = lax.slice_in_dim(w, off + HD * h, off + HD * (h + 1), axis=axis)
        parts += [piece, jnp.zeros_like(piece)]
    return parts


def _unpad_heads(d, off, axis):
    return [lax.slice_in_dim(d, off + HP * h, off + HP * h + HD, axis=axis) for h in range(HEADS)]


def mix_in_ext(w):
    z = lambda n: jnp.zeros((w.shape[0], n), w.dtype)
    kr = w[:, 1408:1440]
    cols = [w[:, 0:1408], kr, _rot_cols(kr), z(64)]
    cols += _pad_heads(w, 1440, 1) + _pad_heads(w, 1696, 1) + _pad_heads(w, 1952, 1)
    cols += [w[:, 2212:2724], w[:, 2208:2212], z(NP - C_CF - HEADS)]
    return jnp.concatenate(cols, axis=1)


def mix_in_unext(d):
    kr = d[:, 1408:1440] + _fold_rot(d[:, 1440:1472])
    cols = [d[:, 0:1408], kr] + _unpad_heads(d, C_CQ, 1) + _unpad_heads(d, C_CK, 1) + _unpad_heads(d, C_CV, 1)
    cols += [d[:, C_CF:C_CF + HEADS], d[:, C_D:C_D + 2 * GW]]
    return jnp.concatenate(cols, axis=1)


def mix_out_ext(w):
    return jnp.concatenate([w[0:GW]] + _pad_heads(w, GW, 0) + _pad_heads(w, 2 * GW, 0) + [w[3 * GW:4 * GW]], axis=0)


def mix_out_unext(d):
    return jnp.concatenate([d[0:GW]] + _unpad_heads(d, O_B, 0) + _unpad_heads(d, O_C, 0) + [d[O_D:O_D + GW]], axis=0)


def uq_ext(w):
    z = lambda n: jnp.zeros((w.shape[0], n), w.dtype)
    a, b = [], []
    for h in range(HEADS):
        o = (B_NOPE + B_ROPE) * h
        a += [w[:, o:o + B_NOPE + B_ROPE], z(32)]
        b += [z(B_NOPE), _rot_cols(w[:, o + B_NOPE:o + B_NOPE + B_ROPE]), z(32)]
    return jnp.concatenate(a + b, axis=1)


def uq_unext(d):
    cols = []
    for h in range(HEADS):
        o = HP * h
        cols += [d[:, o:o + B_NOPE], d[:, o + B_NOPE:o + B_NOPE + B_ROPE]
                 + _fold_rot(d[:, HEADS * HP + o + B_NOPE:HEADS * HP + o + B_NOPE + B_ROPE])]
    return jnp.concatenate(cols, axis=1)


def ukv_ext(w):
    z = jnp.zeros((w.shape[0], HD), w.dtype)
    k, v = [], []
    for h in range(HEADS):
        k += [w[:, 2 * HD * h:2 * HD * h + HD], z]
        v += [w[:, 2 * HD * h + HD:2 * HD * (h + 1)], z]
    return jnp.concatenate(k + v, axis=1)


def ukv_unext(d):
    cols = []
    for h in range(HEADS):
        cols += [d[:, HP * h:HP * h + HD], d[:, HEADS * HP + HP * h:HEADS * HP + HP * h + HD]]
    return jnp.concatenate(cols, axis=1)


def rope_tables(S):
    half = B_ROPE // 2
    inv_freq = ROPE_THETA ** (-jnp.arange(half, dtype=F32) / half)
    ang = jnp.arange(S).astype(F32)[:, None] * inv_freq[None, :]
    cos = jnp.tile(jnp.cos(ang), (1, 2))
    sin = jnp.tile(jnp.sin(ang), (1, 2))
    one, zero = jnp.ones((S, B_NOPE), F32), jnp.zeros((S, B_NOPE), F32)
    z32 = jnp.zeros((S, 32), F32)
    cos_q = jnp.tile(jnp.concatenate([one, cos, z32], axis=1), (1, HEADS))
    sin_q = jnp.tile(jnp.concatenate([zero, sin, z32], axis=1), (1, HEADS))
    cs_k = jnp.concatenate([cos, sin, zero], axis=1)
    place = np.zeros((HP, HEADS * HP), np.float32)
    for h in range(HEADS):
        for j in range(B_ROPE):
            place[j, h * HP + B_NOPE + j] = 1.0
            place[B_ROPE + j, h * HP + B_NOPE + j] = 1.0
    return cos_q, sin_q, cs_k, jnp.asarray(place, BF16)


def layer_fwd(x, mod, get, tabs, S):
    cos_q, sin_q, cs_k, place = tabs
    p = dict(get("ffn1", x))
    l = p["wl"]
    zg1, zu1, act1 = ffn_in_fwd(x, mod, p["ffn1_in"], l, 0, 1, S)
    y1, x1 = out_ln_fwd(act1, p["ffn1_out"], x, mod, p["ln_g"][0:1], p["ln_b"][0:1], 2, 0.5, S, l)
    p.update(get("mix", x1))
    proj = mix_in_fwd(x1, mod, p["mix_in"], 3, 4, S)
    o_a, states = hgrn_fwd(proj, p["lb"], p["ng"], S)
    q_b, k_b, v_b = mla_pre_fwd(proj, p["gq"], p["gkv"], p["wq"], p["wkv"], place, cos_q, sin_q, cs_k, S)
    o_b = attn_fwd(q_b, 0, k_b, 0, v_b, 0, None, (B_NOPE + B_ROPE) ** -0.5, S)
    cum = fox_gate_fwd(proj, p["bcol"], S)
    o_c = attn_fwd(proj, C_CQ // HP, proj, C_CK // HP, proj, C_CV // HP, cum, HD ** -0.5, S)
    o_d = gmlp_fwd(proj, p["g_lg"], p["g_lb"], p["ws"], p["bs"])
    cat = jnp.concatenate([o_a, o_b, o_c, o_d], axis=1)
    y2, x2 = out_ln_fwd(cat, p["mix_out"], x1, mod, p["ln_g"][1:2], p["ln_b"][1:2], 5, 1.0, S)
    p.update(get("ffn2", x2))
    zg3, zu3, act3 = ffn_in_fwd(x2, mod, p["ffn2_in"], l, 6, 7, S)
    y3, x3 = out_ln_fwd(act3, p["ffn2_out"], x2, mod, p["ln_g"][2:3], p["ln_b"][2:3], 8, 0.5, S, l)
    saved = dict(x=x, zg1=zg1, zu1=zu1, act1=act1, y1=y1, x1=x1, proj=proj, states=states, q_b=q_b, k_b=k_b, v_b=v_b,
                 cum=cum, cat=cat, y2=y2, x2=x2, zg3=zg3, zu3=zu3, act3=act3, y3=y3, p=p)
    return x3, saved


def _ffn_bwd(dxn, x_in, y, zg, zu, act, mod, w_in, w_out, l, lg, lb, idx, S, emit):
    sh, sc, gate = idx
    dres, dy, dgate, dlg, dlb = ln_res_bwd(dxn, x_in, y, mod, lg, lb, gate, 0.5, S)
    dzg, dzu = swiglu_bwd(dy, w_out, l, zg, zu, S)
    dw_out = tn_mm(act, dy, FF // 2).reshape(N_CHIPS, FF // N_CHIPS, D)
    dw_in = tn_mm_mod_shards(x_in, mod, dzg, dzu, sh, sc, S)
    mod = mod + emit(dw_in, dw_out)
    dx, dsh, dsc = nt_mod_bwd([dzg, dzu], w_in, [0, 2], x_in, mod, dres, sc, S, FF // 2, l)
    return dx, dw_in, dw_out, dlg, dlb, {sh: dsh, sc: dsc, gate: dgate}, mod


def layer_bwd(dx3, mod, sv, tabs, S, emit):
    cos_q, sin_q, cs_k, place = tabs
    p = sv["p"]
    l = p["wl"]
    g = {}
    dm = {}

    def emit_ffn(part):
        def f(dw_in, dw_out):
            g[part + "_in"], g[part + "_out"] = dw_in, dw_out
            return emit(part, g)
        return f

    dx2, _, _, dlg2, dlb2, d, mod = _ffn_bwd(
        dx3, sv["x2"], sv["y3"], sv["zg3"], sv["zu3"], sv["act3"], mod, p["ffn2_in"], p["ffn2_out"], l,
        p["ln_g"][2:3], p["ln_b"][2:3], (6, 7, 8), S, emit_ffn("ffn2"))
    dm.update(d)
    dres, dy2, dm[5], dlg1, dlb1 = ln_res_bwd(dx2, sv["x1"], sv["y2"], mod, p["ln_g"][1:2], p["ln_b"][1:2], 5, 1.0, S)
    dcat = nt_plain(dy2, p["mix_out"])
    g["mix_out"] = tn_mm(sv["cat"], dy2, NCAT // 2)
    proj = sv["proj"]
    d_a, g["lb"], g["ng"] = hgrn_bwd(proj, sv["states"], dcat, p["lb"], p["ng"], S)
    dq_c, dk_c, dv_c, dcum = attn_bwd(proj, C_CQ // HP, proj, C_CK // HP, proj, C_CV // HP, sv["cum"], dcat,
                                      O_C // HP, HD ** -0.5, S, BF16)
    B = proj.shape[0] // S
    dcum = jnp.concatenate([dcum.reshape(B, HEADS, S), jnp.zeros((B, 8 - HEADS, S), F32)], axis=1)
    d_cf, g["bcol"] = fox_gate_bwd(proj, p["bcol"], dcum, S)
    dq_b, dk_b, dv_b = attn_bwd(sv["q_b"], 0, sv["k_b"], 0, sv["v_b"], 0, None, dcat, O_B // HP,
                                (B_NOPE + B_ROPE) ** -0.5, S, F32)
    d_b, g["gq"], g["gkv"], g["wq"], g["wkv"] = mla_pre_bwd(
        proj, p["gq"], p["gkv"], p["wq"], p["wkv"], place, cos_q, sin_q, cs_k, dq_b, dk_b, dv_b, S)
    d_d, g["g_lg"], g["g_lb"], g["ws"], g["bs"] = gmlp_bwd(proj, p["g_lg"], p["g_lb"], p["ws"], p["bs"], dcat)
    dproj = jnp.concatenate([d_a, d_b, dq_c, dk_c, dv_c, d_d, d_cf, jnp.zeros_like(d_cf)], axis=1)
    g["mix_in"] = tn_mm_mod(sv["x1"], mod, dproj, 3, 4, S, NP_TILE)
    mod = mod + emit("mix", g)
    dx1, dm[3], dm[4] = nt_mod_bwd([dproj], p["mix_in"], [0], sv["x1"], mod, dres, 4, S, NP_TILE)
    last = []

    def emit_last(dw_in, dw_out):
        last.append(emit_ffn("ffn1")(dw_in, dw_out))
        return last[0]

    dx0, _, _, dlg0, dlb0, d, mod = _ffn_bwd(
        dx1, sv["x"], sv["y1"], sv["zg1"], sv["zu1"], sv["act1"], mod, p["ffn1_in"], p["ffn1_out"], l,
        p["ln_g"][0:1], p["ln_b"][0:1], (0, 1, 2), S, emit_last)
    dm.update(d)
    g["ln_g"] = jnp.concatenate([dlg0, dlg1, dlg2], axis=0)
    g["ln_b"] = jnp.concatenate([dlb0, dlb1, dlb2], axis=0)
    dmod = jnp.concatenate([dm[i] for i in range(N_MOD)], axis=1)
    return dx0, dmod, g, last[0]


def local_step(x, c8, tgt, get, lb_logits, S, emit=None):
    B = x.shape[0] // S
    tabs = rope_tables(S)
    lb_all = lb_fwd(lb_logits)
    mods, saved = [], []
    h = x
    for l in range(DEPTH):
        pa = get(l, "ada", h)
        mod = mod_fwd(c8, pa["ada_w"], pa["wl"], pa["ada_b"])[0:B].reshape(B, N_MOD, D)

        def get_l(part, after, l=l):
            p = dict(get(l, part, after))
            if part == "mix":
                p["lb"] = lb_all[l:l + 1]
            return p

        h, sv = layer_fwd(h, mod, get_l, tabs, S)
        mods.append(mod)
        saved.append(sv)
    loss_tile, dh = loss_head(h, tgt)
    grads, dmods, dlb = [None] * DEPTH, [None] * DEPTH, [None] * DEPTH
    tie = jnp.zeros((), F32)
    for l in reversed(range(DEPTH)):
        emit_l = (lambda part, g: jnp.zeros((), F32)) if emit is None else functools.partial(emit, l)
        dh, dmods[l], grads[l], tie = layer_bwd(dh, mods[l] + tie, saved[l], tabs, S, emit_l)
        dlb[l] = grads[l].pop("lb")
    d_logits = lb_bwd(lb_logits, jnp.concatenate(dlb, axis=0))
    return loss_tile, dh, dmods, grads, d_logits


ANY = pl.BlockSpec(memory_space=pl.ANY)


def _place():
    x, y, c = lax.axis_index("x"), lax.axis_index("y"), lax.axis_index("c")
    chips = [(1 - x, y), (x, 1 - y), (1 - x, 1 - y)]
    return x, y, c, chips


def _rcopy(src, dst, sems, k, to):
    send_sems, recv_sems = sems
    return pltpu.make_async_remote_copy(src_ref=src, dst_ref=dst, send_sem=send_sems.at[k], recv_sem=recv_sems.at[k],
                                        device_id=to, device_id_type=MESH)


def _dma_sems(n_remote, n_local):
    return [pltpu.SemaphoreType.DMA((n_remote,)), pltpu.SemaphoreType.DMA((n_remote,)), pltpu.SemaphoreType.DMA((n_local,))]


def own_slot(src, chip):
    L = src.shape[0]
    return lax.dynamic_update_slice(lax.empty((L, N_CHIPS) + src.shape[1:], src.dtype), src[:, None], (0, chip, 0, 0))


HBM_SPEC = pl.BlockSpec(memory_space=pltpu.HBM)
SEM_SPEC = pl.BlockSpec(memory_space=pltpu.SEMAPHORE)
DATAFLOW = pltpu.SideEffectType.DATAFLOW_SIDE_EFFECTING


def _split_start(srcs, lands, copies, n_copies, dep, name):
    n, m = len(srcs), len(lands)

    def body(*refs):
        ins = refs[:n + m]
        send_sems, recv_sems = refs[n + m + 1], refs[n + m + 2]
        token = refs[-1]
        for k, (src, dst, to) in enumerate(copies(ins[:n], ins[n:], _place())):
            pltpu.make_async_remote_copy(src_ref=src, dst_ref=dst, send_sem=send_sems.at[k], recv_sem=recv_sems.at[k],
                                         device_id=to, device_id_type=MESH).start()
        token[...] = jnp.zeros_like(token)

    arrs = list(srcs) + list(lands)
    outs = pl.pallas_call(
        body, name=name,
        out_shape=(pltpu.SemaphoreType.DMA((n_copies,)), pltpu.SemaphoreType.DMA((n_copies,)),
                   *[pltpu.HBM(a.shape, a.dtype) for a in arrs], _sds((8, 128), F32)),
        in_specs=[HBM_SPEC] * (n + m) + [ANY],
        out_specs=(SEM_SPEC, SEM_SPEC, *[HBM_SPEC] * (n + m), pl.BlockSpec(memory_space=pltpu.VMEM)),
        input_output_aliases={i: 2 + i for i in range(n + m)},
        compiler_params=pltpu.CompilerParams(has_side_effects=DATAFLOW),
    )(*[pltpu.with_memory_space_constraint(a, pltpu.HBM) for a in arrs], dep)
    return outs[0], outs[1], list(outs[2:2 + n]), list(outs[2 + n:2 + n + m]), outs[-1]


def _split_wait(handle, arrivals, after, name):
    send_sems, recv_sems, srcs, lands, _ = handle
    n, m = len(srcs), len(lands)

    def body(*refs):
        ins = refs[:n + m]
        send_sems, recv_sems = refs[n + m], refs[n + m + 1]
        x, y, c, chips = place = _place()
        for k, (src, dst) in enumerate(arrivals(ins[:n], ins[n:], place)):
            cp = pltpu.make_async_remote_copy(src_ref=src, dst_ref=dst, send_sem=send_sems.at[k], recv_sem=recv_sems.at[k],
                                              device_id=(x, y, 1 - c), device_id_type=MESH)
            cp.wait_send()
            cp.wait_recv()

    arrs = list(srcs) + list(lands)
    outs = pl.pallas_call(
        body, name=name, out_shape=[pltpu.HBM(a.shape, a.dtype) for a in arrs],
        in_specs=[HBM_SPEC] * (n + m) + [SEM_SPEC, SEM_SPEC, ANY], out_specs=[HBM_SPEC] * (n + m),
        input_output_aliases={i: i for i in range(n + m)},
        compiler_params=pltpu.CompilerParams(has_side_effects=DATAFLOW),
    )(*arrs, send_sems, recv_sems, after)
    return list(outs[:n]), list(outs[n:])


def _ag_part(ref, k, hc):
    rh = ref.shape[2] // 2
    return ref.at[:, k, pl.ds(hc * rh, rh), :]


def ag_start(srcs, lands, dep, name):
    def copies(s, d, place):
        x, y, c, chips = place
        out = []
        for j, (px, py) in enumerate(chips):
            for i in range(len(s)):
                rh = s[i].shape[1] // 2
                out.append((s[i].at[:, pl.ds(c * rh, rh), :], _ag_part(d[i], 2 * x + y, c), (px, py, c)))
        return out

    return _split_start(srcs, lands, copies, 3 * len(srcs), dep, name)


def ag_wait(handle, after, name):
    def arrivals(s, d, place):
        x, y, c, chips = place
        out = []
        for j, (px, py) in enumerate(chips):
            for i in range(len(s)):
                rh = s[i].shape[1] // 2
                out.append((s[i].at[:, pl.ds(c * rh, rh), :], _ag_part(d[i], 2 * px + py, c)))
        return out

    return _split_wait(handle, arrivals, after, name)


def ag_forward(lands, name):
    n = len(lands)

    def body(*refs):
        bufs, token = refs[n:2 * n], refs[2 * n]
        send_sems, recv_sems = refs[2 * n + 1:]
        x, y, c, chips = _place()
        sems = (send_sems, recv_sems)
        token[...] = jnp.zeros_like(token)
        cps = []
        for j, (px, py) in enumerate(chips):
            for i in range(n):
                part = _ag_part(bufs[i], 2 * px + py, c)
                cps.append(_rcopy(part, part, sems, 3 * i + j, (x, y, 1 - c)))
        for cp in cps:
            cp.start()
        for j, (px, py) in enumerate(chips):
            for i in range(n):
                part = _ag_part(bufs[i], 2 * px + py, 1 - c)
                _rcopy(part, part, sems, 3 * i + j, (x, y, 1 - c)).wait_recv()
        for cp in cps:
            cp.wait_send()

    outs = pl.pallas_call(
        body, name=name, out_shape=[_sds(a.shape, a.dtype) for a in lands] + [_sds((8, 128), F32)],
        in_specs=[ANY] * n, out_specs=[ANY] * n + [pl.BlockSpec(memory_space=pltpu.VMEM)],
        input_output_aliases={i: i for i in range(n)}, scratch_shapes=_dma_sems(3 * n, 1)[:2])(*lands)
    return list(outs[:n]), outs[n]


def rs_start(hs, lands, dep, name):
    def copies(s, d, place):
        x, y, c, chips = place
        return [(s[i].at[2 * px + py], d[i].at[2 * x + y], (px, py, c)) for j, (px, py) in enumerate(chips) for i in range(len(s))]

    return _split_start(hs, lands, copies, 3 * len(hs), dep, name)


def rs_wait(handle, after, name):
    def arrivals(s, d, place):
        x, y, c, chips = place
        return [(s[i].at[2 * px + py], d[i].at[2 * px + py]) for j, (px, py) in enumerate(chips) for i in range(len(s))]

    return _split_wait(handle, arrivals, after, name)


def sibling_swap(arrs, name):
    n = len(arrs)
    rh = [a.shape[1] // 2 for a in arrs]

    def body(*refs):
        srcs, outs = refs[:n], refs[n:2 * n]
        send_sems, recv_sems = refs[2 * n:]
        x, y, c, _ = _place()
        cps = [_rcopy(srcs[i].at[:, pl.ds((1 - c) * rh[i], rh[i]), :], outs[i], (send_sems, recv_sems), i, (x, y, 1 - c))
               for i in range(n)]
        for cp in cps:
            cp.start()
        for cp in cps:
            cp.wait()

    return pl.pallas_call(
        body, name=name, out_shape=[_sds((N_CHIPS, r, a.shape[2]), a.dtype) for a, r in zip(arrs, rh)],
        in_specs=[ANY] * n, out_specs=[ANY] * n, scratch_shapes=_dma_sems(n, 1)[:2])(*arrs)


def sum_into(land, base, l, core, name):
    _, rh, C = land.shape
    tr = _row_tile(rh, C, N_CHIPS, mult=16)
    nr = rh // tr

    def body(core_ref, land_ref, base_ref, o_ref):
        acc = land_ref[0].astype(F32)
        for k in range(1, N_CHIPS):
            acc = acc + land_ref[k].astype(F32)
        o_ref[...] = acc

    grid_spec = pltpu.PrefetchScalarGridSpec(
        num_scalar_prefetch=1, grid=(nr,),
        in_specs=[pl.BlockSpec((N_CHIPS, tr, C), lambda r, core_ref: (0, r, 0)), ANY],
        out_specs=pl.BlockSpec((None, tr, C), lambda r, core_ref: (l, core_ref[0] * nr + r, 0)))
    return pl.pallas_call(body, name=name, grid_spec=grid_spec, out_shape=_sds(base.shape, base.dtype),
                          input_output_aliases={2: 0}, compiler_params=_cparams(VMEM_BIG))(
        core.reshape(1).astype(jnp.int32), land, base)


def sibling_join(bases, name):
    n = len(bases)

    def body(*refs):
        bufs = refs[n:2 * n]
        send_sems, recv_sems = refs[2 * n:]
        x, y, c, _ = _place()
        sems = (send_sems, recv_sems)

        def half(i, hc):
            rh = bufs[i].shape[1] // 2
            return bufs[i].at[:, pl.ds(hc * rh, rh), :]

        sends = [_rcopy(half(i, c), half(i, c), sems, i, (x, y, 1 - c)) for i in range(n)]
        for cp in sends:
            cp.start()
        for i in range(n):
            _rcopy(half(i, 1 - c), half(i, 1 - c), sems, i, (x, y, 1 - c)).wait_recv()
        for cp in sends:
            cp.wait_send()

    return pl.pallas_call(
        body, name=name, out_shape=[_sds(b.shape, b.dtype) for b in bases], in_specs=[ANY] * n, out_specs=[ANY] * n,
        input_output_aliases={i: i for i in range(n)}, scratch_shapes=_dma_sems(n, 1)[:2])(*bases)


def ag_all(blk):
    M, C = blk.shape

    def body(x_ref, out_ref, send_sems, recv_sems, loc_sem):
        x, y, c, chips = _place()
        sems = (send_sems, recv_sems)
        me, sibling = (x, y, c), (x, y, 1 - c)

        def slot(px, py, pc):
            return out_ref.at[4 * px + 2 * py + pc]

        mine = pltpu.make_async_copy(x_ref, slot(*me), loc_sem)
        mine.start()
        first = [_rcopy(x_ref, slot(*me), sems, 0, sibling)]
        first += [_rcopy(x_ref, slot(*me), sems, 1 + j, (*chip, c)) for j, chip in enumerate(chips)]
        for cp in first:
            cp.start()
        passed = [_rcopy(slot(*chip, c), slot(*chip, c), sems, 4 + j, sibling) for j, chip in enumerate(chips)]
        for j, chip in enumerate(chips):
            _rcopy(slot(*chip, c), slot(*chip, c), sems, 1 + j, me).wait_recv()
            passed[j].start()
        _rcopy(slot(*sibling), slot(*sibling), sems, 0, me).wait_recv()
        for j, chip in enumerate(chips):
            _rcopy(slot(*chip, 1 - c), slot(*chip, 1 - c), sems, 4 + j, me).wait_recv()
        for cp in first + passed:
            cp.wait_send()
        mine.wait()

    return pl.pallas_call(
        body, name="ag_all", out_shape=_sds((8, M, C), blk.dtype),
        in_specs=[pl.BlockSpec(memory_space=pltpu.VMEM)], out_specs=pl.BlockSpec(memory_space=pltpu.VMEM),
        scratch_shapes=[pltpu.SemaphoreType.DMA((7,)), pltpu.SemaphoreType.DMA((7,)), pltpu.SemaphoreType.DMA(())],
        compiler_params=_cparams(VMEM_BIG))(blk)


WEIGHTS = ["ada_w", "ada_b", "ln_g", "ln_b", "ffn1_w_in", "ffn1_w_out", "ffn2_w_in", "ffn2_w_out", "mix_w_in", "mix_w_out",
           "hgrn_lb_logits", "hgrn_norm_g", "mla_q_norm_g", "mla_kv_norm_g", "mla_w_uq", "mla_w_ukv", "fox_b_f",
           "gmlp_ln_g", "gmlp_ln_b", "gmlp_w_s", "gmlp_b_s"]
SMALL = ["hgrn_lb_logits", "hgrn_norm_g", "mla_q_norm_g", "mla_kv_norm_g", "fox_b_f", "gmlp_ln_g", "gmlp_ln_b",
         "gmlp_w_s", "gmlp_b_s", "ln_g", "ln_b"]
GATHERED = ["ada_w", "ffn1_w_in", "ffn1_w_out", "ffn2_w_in", "ffn2_w_out", "mix_w_in", "mix_w_out", "mla_w_uq", "mla_w_ukv"]
REDUCED = GATHERED[1:]


def _col_shards(a):
    cols = a.shape[1] // N_CHIPS
    return jnp.stack([a[:, k * cols:(k + 1) * cols] for k in range(N_CHIPS)])


def add_kept_half(a, got, core, name):
    _, R, C = a.shape
    rh = R // 2
    tr = _row_tile(rh, C, mult=16)
    nr = rh // tr

    def body(core_ref, a_ref, b_ref, o_ref):
        o_ref[...] = (a_ref[...].astype(F32) + b_ref[...].astype(F32)).astype(o_ref.dtype)

    half = pl.BlockSpec((None, tr, C), lambda k, r, core_ref: (k, r, 0))
    grid_spec = pltpu.PrefetchScalarGridSpec(
        num_scalar_prefetch=1, grid=(N_CHIPS, nr),
        in_specs=[pl.BlockSpec((None, tr, C), lambda k, r, core_ref: (k, core_ref[0] * nr + r, 0)), half],
        out_specs=half)
    return pl.pallas_call(body, name=name, grid_spec=grid_spec, out_shape=_sds((N_CHIPS, rh, C), BF16),
                          compiler_params=_cparams(VMEM_BIG))(core.reshape(1).astype(jnp.int32), a, got)


def _rows(parts, n_rows, dtype):
    flat = jnp.concatenate([p.reshape(-1) for p in parts])
    pad = n_rows * D - flat.shape[0]
    return jnp.concatenate([flat, jnp.zeros((pad,), dtype)]).reshape(n_rows, D)


def _take(flat, shapes):
    out, o = [], 0
    for shp in shapes:
        n = int(np.prod(shp))
        out.append(flat[o:o + n].reshape(shp))
        o += n
    return out


def _round_up(n, m):
    return -(-n // m) * m


def pack_small(w):
    parts = [w[n][l] for l in range(DEPTH) for n in SMALL]
    n = sum(int(np.prod(p.shape)) for p in parts)
    return _rows(parts, _round_up(-(-n // D), 8), F32)


def unpack_small(pk, like):
    shapes = [like[n].shape[1:] for l in range(DEPTH) for n in SMALL]
    pieces = _take(pk.reshape(-1), shapes)
    names = [n for l in range(DEPTH) for n in SMALL]
    return {n: jnp.stack([p for p, m in zip(pieces, names) if m == n]) for n in SMALL}


def kernel(x, c, ada_w, ada_b, ln_g, ln_b, ffn1_w_in, ffn1_w_out, ffn2_w_in, ffn2_w_out, mix_w_in, mix_w_out, hgrn_lb_logits, hgrn_norm_g, mla_q_norm_g, mla_kv_norm_g, mla_w_uq, mla_w_ukv, fox_b_f, gmlp_ln_g, gmlp_ln_b, gmlp_w_s, gmlp_b_s, loss_target, m_ada_w, m_ada_b, m_ln_g, m_ln_b, m_ffn1_w_in, m_ffn1_w_out, m_ffn2_w_in, m_ffn2_w_out, m_mix_w_in, m_mix_w_out, m_hgrn_lb_logits, m_hgrn_norm_g, m_mla_q_norm_g, m_mla_kv_norm_g, m_mla_w_uq, m_mla_w_ukv, m_fox_b_f, m_gmlp_ln_g, m_gmlp_ln_b, m_gmlp_w_s, m_gmlp_b_s, v_ada_w, v_ada_b, v_ln_g, v_ln_b, v_ffn1_w_in, v_ffn1_w_out, v_ffn2_w_in, v_ffn2_w_out, v_mix_w_in, v_mix_w_out, v_hgrn_lb_logits, v_hgrn_norm_g, v_mla_q_norm_g, v_mla_kv_norm_g, v_mla_w_uq, v_mla_w_ukv, v_fox_b_f, v_gmlp_ln_g, v_gmlp_ln_b, v_gmlp_w_s, v_gmlp_b_s):
    w = dict(zip(WEIGHTS, (ada_w, ada_b, ln_g, ln_b, ffn1_w_in, ffn1_w_out, ffn2_w_in, ffn2_w_out, mix_w_in, mix_w_out, hgrn_lb_logits, hgrn_norm_g, mla_q_norm_g, mla_kv_norm_g, mla_w_uq, mla_w_ukv, fox_b_f, gmlp_ln_g, gmlp_ln_b, gmlp_w_s, gmlp_b_s)))
    m = dict(zip(WEIGHTS, (m_ada_w, m_ada_b, m_ln_g, m_ln_b, m_ffn1_w_in, m_ffn1_w_out, m_ffn2_w_in, m_ffn2_w_out, m_mix_w_in, m_mix_w_out, m_hgrn_lb_logits, m_hgrn_norm_g, m_mla_q_norm_g, m_mla_kv_norm_g, m_mla_w_uq, m_mla_w_ukv, m_fox_b_f, m_gmlp_ln_g, m_gmlp_ln_b, m_gmlp_w_s, m_gmlp_b_s)))
    v = dict(zip(WEIGHTS, (v_ada_w, v_ada_b, v_ln_g, v_ln_b, v_ffn1_w_in, v_ffn1_w_out, v_ffn2_w_in, v_ffn2_w_out, v_mix_w_in, v_mix_w_out, v_hgrn_lb_logits, v_hgrn_norm_g, v_mla_q_norm_g, v_mla_kv_norm_g, v_mla_w_uq, v_mla_w_ukv, v_fox_b_f, v_gmlp_ln_g, v_gmlp_ln_b, v_gmlp_w_s, v_gmlp_b_s)))
    Bl, S, _ = x.shape
    T = Bl * S
    core = lax.axis_index("c")
    chip = 2 * lax.axis_index("x") + lax.axis_index("y")

    def shard(key):
        n, l = key
        if n == "ln":
            return jnp.concatenate([ln_g[l:l + 1], ln_b[l:l + 1], jnp.zeros((1, 2, D // N_CHIPS), F32)], axis=1)
        return w[n][l:l + 1].astype(BF16)

    mixers = ["mix_w_in", "mix_w_out", "mla_w_uq", "mla_w_ukv"]
    groups = [[("ada_w", 0), ("ffn1_w_in", 0), ("ffn1_w_out", 0), ("ln", 0)],
              [(n, 0) for n in mixers + ["ffn2_w_in", "ffn2_w_out"]],
              [(n, 1) for n in GATHERED + ["ln"]]]
    srcs = [[shard(k) for k in grp] for grp in groups]
    lands = [[own_slot(s, chip) for s in srcs[0]]]
    handle0 = ag_start(srcs[0], lands[0], jnp.zeros((8, 128), F32), "ag_start_0")
    chip_later = chip + handle0[-1][0, 0].astype(jnp.int32)
    lands += [[own_slot(s, chip_later) for s in grp] for grp in srcs[1:]]
    first, token = ag_forward(ag_wait(handle0, lands[2][0], "ag_wait_0")[1], "ag_forward_0")
    have = dict(zip(groups[0], first))
    handles = {}
    for gi in (1, 2):
        handles[gi] = ag_start(srcs[gi], lands[gi], token, "ag_start_%d" % gi)
        token = handles[gi][-1]
    c8 = jnp.concatenate([c, jnp.zeros((8 - Bl, D), F32)], axis=0)
    c8 = c8 + token[0, 0]

    def cat_cols(a):
        return jnp.concatenate([a[0, k] for k in range(N_CHIPS)], axis=1)

    def get(l, part, after):
        gi = 2 if l == 1 else (0 if part in ("ada", "ffn1") else 1)
        if gi in handles:
            arrived, _ = ag_forward(ag_wait(handles.pop(gi), after, "ag_wait_%d" % gi)[1], "ag_forward_%d" % gi)
            have.update(zip(groups[gi], arrived))
        if part == "ada":
            return dict(ada_w=have[("ada_w", l)], wl=0, ada_b=ada_b[l][None])
        if part == "ffn1":
            ln_full = jnp.moveaxis(have[("ln", l)][0], 0, 1).reshape(8, D)
            return dict(ffn1_in=have[("ffn1_w_in", l)], ffn1_out=have[("ffn1_w_out", l)], wl=0,
                        ln_g=ln_full[0:3], ln_b=ln_full[3:6])
        if part == "ffn2":
            return dict(ffn2_in=have[("ffn2_w_in", l)], ffn2_out=have[("ffn2_w_out", l)])
        return dict(
            mix_in=mix_in_ext(cat_cols(have[("mix_w_in", l)])), mix_out=mix_out_ext(have[("mix_w_out", l)].reshape(D, D)),
            wq=uq_ext(cat_cols(have[("mla_w_uq", l)])).astype(F32), wkv=ukv_ext(cat_cols(have[("mla_w_ukv", l)])).astype(F32),
            ng=hgrn_norm_g[l][None], gq=mla_q_norm_g[l][None], gkv=mla_kv_norm_g[l][None],
            bcol=jnp.concatenate([fox_b_f[l], jnp.zeros((8 - HEADS,), F32)])[:, None],
            g_lg=gmlp_ln_g[l][None], g_lb=gmlp_ln_b[l][None], ws=gmlp_w_s[l], bs=gmlp_b_s[l][:, :, None])

    pending = []

    def emit(l, part, g):
        if part == "mix":
            names = mixers
            by_chip = [_col_shards(mix_in_unext(g["mix_in"])), mix_out_unext(g["mix_out"]).reshape(N_CHIPS, D // N_CHIPS, D),
                       _col_shards(uq_unext(g["wq"])).astype(BF16), _col_shards(ukv_unext(g["wkv"])).astype(BF16)]
        else:
            names = [part + "_w_in", part + "_w_out"]
            by_chip = [g[part + "_in"], g[part + "_out"]]
        tag = "%d_%s" % (l, part)
        got = sibling_swap(by_chip, "sibling_swap_" + tag)
        chip_sum = [add_kept_half(a, r, core, "add_sibling") for a, r in zip(by_chip, got)]
        zones = [lax.dynamic_update_slice(lax.empty(h.shape, h.dtype), lax.dynamic_slice_in_dim(h, chip, 1, axis=0), (chip, 0, 0))
                 for h in chip_sum]
        handle = rs_start(chip_sum, zones, chip_sum[0], "rs_start_" + tag)
        pending.append((l, names, handle, tag))
        return handle[-1][0, 0]

    loss_tile, dx, dmods, grads, d_logits = local_step(
        x.reshape(T, D), c8, loss_target.reshape(T, D), get, hgrn_lb_logits, S, emit)
    loss = lax.psum(loss_tile[0, 0], ("x", "y", "c"))

    small_g = {"hgrn_lb_logits": d_logits,
               "hgrn_norm_g": jnp.stack([grads[l]["ng"][0] for l in range(DEPTH)]),
               "mla_q_norm_g": jnp.stack([grads[l]["gq"][0] for l in range(DEPTH)]),
               "mla_kv_norm_g": jnp.stack([grads[l]["gkv"][0] for l in range(DEPTH)]),
               "fox_b_f": jnp.stack([grads[l]["bcol"][0:HEADS, 0] for l in range(DEPTH)]),
               "gmlp_ln_g": jnp.stack([grads[l]["g_lg"][0] for l in range(DEPTH)]),
               "gmlp_ln_b": jnp.stack([grads[l]["g_lb"][0] for l in range(DEPTH)]),
               "gmlp_w_s": jnp.stack([grads[l]["ws"] for l in range(DEPTH)]),
               "gmlp_b_s": jnp.stack([grads[l]["bs"][:, :, 0] for l in range(DEPTH)])}
    small_g["ln_g"] = jnp.stack([grads[l]["ln_g"] for l in range(DEPTH)])
    small_g["ln_b"] = jnp.stack([grads[l]["ln_b"] for l in range(DEPTH)])
    pk_small = pack_small(small_g)
    n_small = pk_small.shape[0]
    extras = [dmods[l] for l in range(DEPTH)] + [c]
    n_extra = _round_up(-(-sum(int(np.prod(e.shape)) for e in extras) // D), 8)
    gathered = ag_all(jnp.concatenate([pk_small, _rows(extras, n_extra, F32)], axis=0))
    g_small = unpack_small(sum_slots(gathered[:, 0:n_small], 8, "sum_small"), small_g)
    ext = gathered[:, n_small:].reshape(8, -1)
    n_dmod = DEPTH * Bl * N_MOD * D
    dmod_all = ext[:, 0:n_dmod].reshape(8, DEPTH, Bl, N_MOD * D)
    c_all = ext[:, n_dmod:n_dmod + Bl * D].reshape(8 * Bl, D)
    g_ada_w, g_ada_b = [], []
    ncol = N_MOD * D // N_CHIPS
    for l in range(DEPTH):
        dm = dmod_all[:, l].reshape(8 * Bl, N_MOD * D)
        g_ada_w.append(ada_grad(c_all, lax.dynamic_slice_in_dim(dm, chip * ncol, ncol, axis=1)))
        g_ada_b.append(sum_slots(dm.reshape(8 * Bl, N_MOD, D), 8 * Bl, "sum_ada_b").reshape(N_MOD * D))
    g_ada_w, g_ada_b = jnp.stack(g_ada_w), jnp.stack(g_ada_b)

    red = {n: lax.empty(w[n].shape, F32) for n in REDUCED}

    def arrive(entry, after):
        l, names, handle, tag = entry
        for n, land in zip(names, rs_wait(handle, after, "rs_wait_" + tag)[1]):
            red[n] = sum_into(land, red[n], l, core, "sum_chips")

    for entry in pending[:-1]:
        arrive(entry, dx)
    late = pending[-1][1]
    early = [n for n in REDUCED if n not in late]
    grad = dict(zip(early, sibling_join([red[n] for n in early], "sibling_join_a")))
    grad.update(g_small)
    grad["ada_w"], grad["ada_b"] = g_ada_w, g_ada_b
    for n in ("ln_g", "ln_b"):
        grad[n] = lax.dynamic_slice_in_dim(g_small[n], chip * (D // N_CHIPS), D // N_CHIPS, axis=2)
    out = {"grad": grad, "delta": {}, "new_m": {}, "new_v": {}}

    def update(n):
        shp = w[n].shape
        two_d = (-1, shp[-1])
        res = adamw(w[n].reshape(two_d), grad[n].reshape(two_d), m[n].reshape(two_d), v[n].reshape(two_d), "adamw_" + n,
                    echo=n in REDUCED)
        grad[n] = (res[3] if n in REDUCED else grad[n]).reshape(shp)
        for key, r in zip(("delta", "new_m", "new_v"), res):
            out[key][n] = r.reshape(shp)

    for n in WEIGHTS:
        if n not in late:
            update(n)
    arrive(pending[-1], out["delta"]["ffn2_w_in"])
    grad.update(zip(late, sibling_join([red[n] for n in late], "sibling_join_b")))
    for n in late:
        update(n)
    outs = [loss, dx.reshape(Bl, S, D)]
    for key in ("grad", "delta", "new_m", "new_v"):
        outs += [out[key][n] for n in WEIGHTS]
    return tuple(outs)
```

```python
import functools

import jax
import jax.numpy as jnp
import numpy as np
from jax import lax
from jax.experimental import pallas as pl
from jax.experimental.pallas import tpu as pltpu

F32, BF16 = jnp.float32, jnp.bfloat16
MESH = pl.DeviceIdType.MESH

N_CHIPS = 4
D = 1024
DEPTH = 2
FF = 2816
N_MOD = 9
GW = 256
HEADS = 4
HD = 64
HP = 128
A_CHUNK = 16
LB_FLOOR = 1e-30
B_Q_LORA, B_KV_LORA, B_NOPE, B_ROPE = 256, 128, 64, 32
ROPE_THETA = 10000.0
D_CHUNK = 128
ALPHA = (2 * DEPTH) ** 0.25
LN_EPS = 1e-5
RMS_EPS = 1e-6
ADAM_LR, ADAM_B1, ADAM_B2, ADAM_EPS, ADAM_WD, ADAM_STEP = 0.001, 0.9, 0.999, 1e-08, 0.01, 10

NP = 3840
NP_TILE = 1920
C_A, C_B, C_CQ, C_CK, C_CV, C_D, C_CF = 0, 1024, 1536, 2048, 2560, 3072, 3584
NCAT = 1536
O_A, O_B, O_C, O_D = 0, 256, 768, 1280

VMEM_BIG = 48 << 20
VMEM_MOST = 58 << 20


def _cparams(vmem=None):
    return pltpu.CompilerParams(vmem_limit_bytes=vmem) if vmem else pltpu.CompilerParams()


def _sds(shape, dtype):
    return jax.ShapeDtypeStruct(tuple(shape), dtype)


@jax.custom_vjp
def _mm(a, w):
    return jnp.dot(a.astype(BF16), w.astype(BF16), preferred_element_type=F32)


def _mm_f(a, w):
    return _mm(a, w), (a, w)


def _mm_b(res, g):
    a, w = res
    gb = g.astype(BF16)
    da = lax.dot_general(gb, w.astype(BF16), (((1,), (1,)), ((), ())), preferred_element_type=F32)
    dw = lax.dot_general(a.astype(BF16), gb, (((0,), (0,)), ((), ())), preferred_element_type=F32)
    return da.astype(a.dtype), dw.astype(w.dtype)


_mm.defvjp(_mm_f, _mm_b)


@jax.custom_vjp
def _mm_nt(a, b):
    return lax.dot_general(a.astype(BF16), b.astype(BF16), (((1,), (1,)), ((), ())), preferred_element_type=F32)


def _mm_nt_f(a, b):
    return _mm_nt(a, b), (a, b)


def _mm_nt_b(res, g):
    a, b = res
    gb = g.astype(BF16)
    da = jnp.dot(gb, b.astype(BF16), preferred_element_type=F32)
    db = lax.dot_general(gb, a.astype(BF16), (((0,), (0,)), ((), ())), preferred_element_type=F32)
    return da.astype(a.dtype), db.astype(b.dtype)


_mm_nt.defvjp(_mm_nt_f, _mm_nt_b)


@jax.custom_vjp
def _mm_tn(a, b):
    return lax.dot_general(a.astype(BF16), b.astype(BF16), (((0,), (0,)), ((), ())), preferred_element_type=F32)


def _mm_tn_f(a, b):
    return _mm_tn(a, b), (a, b)


def _mm_tn_b(res, g):
    a, b = res
    gb = g.astype(BF16)
    da = lax.dot_general(b.astype(BF16), gb, (((1,), (1,)), ((), ())), preferred_element_type=F32)
    db = jnp.dot(a.astype(BF16), gb, preferred_element_type=F32)
    return da.astype(a.dtype), db.astype(b.dtype)


_mm_tn.defvjp(_mm_tn_f, _mm_tn_b)


def _split3(x):
    p1 = x.astype(BF16)
    r = x - p1.astype(F32)
    p2 = r.astype(BF16)
    return p1, p2, (r - p2.astype(F32)).astype(BF16)


@jax.custom_vjp
def _sel_r(x, sel):
    s = sel.astype(BF16)
    return sum(jnp.dot(p, s, preferred_element_type=F32) for p in _split3(x))


def _sel_r_f(x, sel):
    return _sel_r(x, sel), sel


def _sel_r_b(sel, g):
    s = sel.astype(BF16)
    dx = sum(lax.dot_general(p, s, (((1,), (1,)), ((), ())), preferred_element_type=F32) for p in _split3(g))
    return dx, jnp.zeros_like(sel)


_sel_r.defvjp(_sel_r_f, _sel_r_b)


@jax.custom_vjp
def _sel_l(sel, x):
    s = sel.astype(BF16)
    return sum(jnp.dot(s, p, preferred_element_type=F32) for p in _split3(x))


def _sel_l_f(sel, x):
    return _sel_l(sel, x), sel


def _sel_l_b(sel, g):
    s = sel.astype(BF16)
    dx = sum(lax.dot_general(s, p, (((0,), (0,)), ((), ())), preferred_element_type=F32) for p in _split3(g))
    return jnp.zeros_like(sel), dx


_sel_l.defvjp(_sel_l_f, _sel_l_b)


def _iota(shape, dim):
    return lax.broadcasted_iota(jnp.int32, shape, dim)


def _head_sum_mats():
    e = (_iota((GW, HP), 0) // HD == _iota((GW, HP), 1)).astype(F32)
    et = (_iota((HP, GW), 1) // HD == _iota((HP, GW), 0)).astype(F32)
    return e, et


def _modulate(x, mod_ref, sh, sc):
    return x * (1.0 + mod_ref[sc:sc + 1, :]) + mod_ref[sh:sh + 1, :]


def _ln_res(x, y, gate, lg, lb, gs):
    r = ALPHA * x + gs * (1.0 + gate) * y
    mu = jnp.mean(r, axis=-1, keepdims=True)
    var = jnp.mean(jnp.square(r - mu), axis=-1, keepdims=True)
    return (r - mu) * lax.rsqrt(var + LN_EPS) * lg + lb


def _rms(x, g):
    return x * lax.rsqrt(jnp.mean(x * x, axis=-1, keepdims=True) + RMS_EPS) * g


def _tile(n, pref):
    return pref if n % pref == 0 else n


def mod_fwd(c8, w, l, b):
    tn = w.shape[3]
    n = N_CHIPS * tn

    def body(c_ref, w_ref, b_ref, o_ref):
        h = jax.nn.silu(c_ref[...]).astype(BF16)
        o_ref[...] = jnp.dot(h, w_ref[...], preferred_element_type=F32) + b_ref[...]

    return pl.pallas_call(
        body, name="mod_fwd", grid=(N_CHIPS,),
        in_specs=[pl.BlockSpec((8, D), lambda j: (0, 0)), pl.BlockSpec((None, None, D, tn), lambda j: (l, j, 0, 0)),
                  pl.BlockSpec((1, tn), lambda j: (0, j))],
        out_specs=pl.BlockSpec((8, tn), lambda j: (0, j)), out_shape=_sds((8, n), F32),
        compiler_params=_cparams(VMEM_BIG))(c8, w, b)


def ffn_in_fwd(x, mod, w_in, l, sh, sc, S):
    T = x.shape[0]
    tm, tn = _tile(S, 1024), FF // 2
    tpb, nj = S // tm, 2

    def body(x_ref, mod_ref, wg_ref, wu_ref, zg_ref, zu_ref, act_ref, h_ref):
        @pl.when(pl.program_id(1) == 0)
        def _():
            h_ref[...] = _modulate(x_ref[...], mod_ref, sh, sc).astype(BF16)
        g = jnp.dot(h_ref[...], wg_ref[...], preferred_element_type=F32)
        u = jnp.dot(h_ref[...], wu_ref[...], preferred_element_type=F32)
        zg_ref[...] = g.astype(BF16)
        zu_ref[...] = u.astype(BF16)
        act_ref[...] = (jax.nn.silu(g) * u).astype(BF16)

    return pl.pallas_call(
        body, name="ffn_in_fwd", grid=(T // tm, nj),
        in_specs=[pl.BlockSpec((tm, D), lambda i, j: (i, 0)),
                  pl.BlockSpec((None, N_MOD, D), lambda i, j: (i // tpb, 0, 0)),
                  pl.BlockSpec((None, None, D, tn), lambda i, j: (l, j, 0, 0)),
                  pl.BlockSpec((None, None, D, tn), lambda i, j: (l, j + nj, 0, 0))],
        out_specs=[pl.BlockSpec((tm, tn), lambda i, j: (i, j))] * 3,
        out_shape=[_sds((T, FF), BF16)] * 3,
        scratch_shapes=[pltpu.VMEM((tm, D), BF16)],
        compiler_params=_cparams(VMEM_BIG))(x, mod, w_in, w_in)


def mix_in_fwd(x, mod, w, sh, sc, S):
    T = x.shape[0]
    n = w.shape[1]
    tm, tn = _tile(S, 1024), NP_TILE
    tpb = S // tm

    def body(x_ref, mod_ref, w_ref, o_ref, h_ref):
        @pl.when(pl.program_id(1) == 0)
        def _():
            h_ref[...] = _modulate(x_ref[...], mod_ref, sh, sc).astype(BF16)
        o_ref[...] = jnp.dot(h_ref[...], w_ref[...], preferred_element_type=F32)

    return pl.pallas_call(
        body, name="mix_in_fwd", grid=(T // tm, n // tn),
        in_specs=[pl.BlockSpec((tm, D), lambda i, j: (i, 0)),
                  pl.BlockSpec((None, N_MOD, D), lambda i, j: (i // tpb, 0, 0)),
                  pl.BlockSpec((D, tn), lambda i, j: (0, j))],
        out_specs=pl.BlockSpec((tm, tn), lambda i, j: (i, j)), out_shape=_sds((T, n), F32),
        scratch_shapes=[pltpu.VMEM((tm, D), BF16)],
        compiler_params=_cparams(VMEM_BIG))(x, mod, w)


def out_ln_fwd(act, w_out, x, mod, lg, lb, gate, gs, S, l=None):
    T, K = act.shape
    tm = _tile(S, 1024)
    tpb = S // tm

    def body(a_ref, w_ref, x_ref, mod_ref, lg_ref, lb_ref, y_ref, xn_ref):
        y = jnp.dot(a_ref[...], w_ref[...].reshape(K, D), preferred_element_type=F32)
        y_ref[...] = y
        xn_ref[...] = _ln_res(x_ref[...], y, mod_ref[gate:gate + 1, :], lg_ref[...], lb_ref[...], gs)

    if l is None:
        w_spec = pl.BlockSpec((K, D), lambda i: (0, 0))
    else:
        w_spec = pl.BlockSpec((None, N_CHIPS, K // N_CHIPS, D), lambda i: (l, 0, 0, 0))
    return pl.pallas_call(
        body, name="out_ln_fwd", grid=(T // tm,),
        in_specs=[pl.BlockSpec((tm, K), lambda i: (i, 0)), w_spec,
                  pl.BlockSpec((tm, D), lambda i: (i, 0)),
                  pl.BlockSpec((None, N_MOD, D), lambda i: (i // tpb, 0, 0)),
                  pl.BlockSpec((1, D), lambda i: (0, 0)), pl.BlockSpec((1, D), lambda i: (0, 0))],
        out_specs=[pl.BlockSpec((tm, D), lambda i: (i, 0))] * 2,
        out_shape=[_sds((T, D), F32), _sds((T, D), F32)],
        compiler_params=_cparams(VMEM_MOST))(act, w_out, x, mod, lg, lb)


def ln_res_bwd(dxn, x, y, mod, lg, lb, gate, gs, S):
    T = x.shape[0]
    B = T // S
    tm = _tile(S, 512)
    tpb = S // tm

    def body(d_ref, x_ref, y_ref, mod_ref, lg_ref, lb_ref, dx_ref, dy_ref, dg_ref, dlg_ref, dlb_ref):
        i = pl.program_id(0)
        f = functools.partial(_ln_res, gs=gs)
        _, vjp = jax.vjp(f, x_ref[...], y_ref[...], mod_ref[gate:gate + 1, :], lg_ref[...], lb_ref[...])
        dx, dy, dg, dlg, dlb = vjp(d_ref[...])
        dx_ref[...] = dx
        dy_ref[...] = dy.astype(BF16)

        @pl.when(i % tpb == 0)
        def _():
            dg_ref[...] = jnp.zeros_like(dg_ref)

        @pl.when(i == 0)
        def _():
            dlg_ref[...] = jnp.zeros_like(dlg_ref)
            dlb_ref[...] = jnp.zeros_like(dlb_ref)

        dg_ref[...] += dg
        dlg_ref[...] += dlg
        dlb_ref[...] += dlb

    tok = pl.BlockSpec((tm, D), lambda i: (i, 0))
    vec = pl.BlockSpec((1, D), lambda i: (0, 0))
    return pl.pallas_call(
        body, name="ln_res_bwd", grid=(T // tm,),
        in_specs=[tok, tok, tok, pl.BlockSpec((None, N_MOD, D), lambda i: (i // tpb, 0, 0)), vec, vec],
        out_specs=[tok, tok, pl.BlockSpec((None, 1, D), lambda i: (i // tpb, 0, 0)), vec, vec],
        out_shape=[_sds((T, D), F32), _sds((T, D), BF16), _sds((B, 1, D), F32), _sds((1, D), F32), _sds((1, D), F32)],
        compiler_params=_cparams(VMEM_BIG))(dxn, x, y, mod, lg, lb)


def swiglu_bwd(dy, w_out, l, zg, zu, S):
    T = dy.shape[0]
    tm, tn = _tile(S, 1024), FF // 2

    def body(dy_ref, w_ref, zg_ref, zu_ref, dg_ref, du_ref):
        da = lax.dot_general(dy_ref[...], w_ref[...].reshape(tn, D), (((1,), (1,)), ((), ())), preferred_element_type=F32)
        g, u = zg_ref[...].astype(F32), zu_ref[...].astype(F32)
        sg = jax.nn.sigmoid(g)
        dg_ref[...] = (da * u * (sg * (1.0 + g * (1.0 - sg)))).astype(BF16)
        du_ref[...] = (da * (g * sg)).astype(BF16)

    zt = pl.BlockSpec((tm, tn), lambda i, j: (i, j))
    return pl.pallas_call(
        body, name="swiglu_bwd", grid=(T // tm, FF // tn),
        in_specs=[pl.BlockSpec((tm, D), lambda i, j: (i, 0)),
                  pl.BlockSpec((None, 2, FF // N_CHIPS, D), lambda i, j: (l, j, 0, 0)), zt, zt],
        out_specs=[zt, zt], out_shape=[_sds((T, FF), BF16), _sds((T, FF), BF16)],
        compiler_params=_cparams(VMEM_BIG))(dy, w_out, zg, zu)


def nt_plain(dy, w):
    T = dy.shape[0]
    K = w.shape[0]
    tm = _tile(T, 1024)

    def body(dy_ref, w_ref, o_ref):
        o_ref[...] = lax.dot_general(dy_ref[...], w_ref[...], (((1,), (1,)), ((), ())), preferred_element_type=F32)

    return pl.pallas_call(
        body, name="nt_plain", grid=(T // tm,),
        in_specs=[pl.BlockSpec((tm, D), lambda i: (i, 0)), pl.BlockSpec((K, D), lambda i: (0, 0))],
        out_specs=pl.BlockSpec((tm, K), lambda i: (i, 0)), out_shape=_sds((T, K), F32),
        compiler_params=_cparams(VMEM_BIG))(dy, w)


def _tn_step(acc, o_ref, lhs, rhs, t, nt):
    part = lax.dot_general(lhs, rhs, (((0,), (0,)), ((), ())), preferred_element_type=F32)
    if nt == 1:
        o_ref[...] = part.astype(o_ref.dtype)
        return

    @pl.when(t == 0)
    def _():
        acc[...] = part

    @pl.when((t > 0) & (t < nt - 1))
    def _():
        acc[...] += part

    @pl.when(t == nt - 1)
    def _():
        o_ref[...] = (acc[...] + part).astype(o_ref.dtype)


def tn_mm(a, b, tk):
    T, K = a.shape
    N = b.shape[1]
    tt = _tile(T, 1024)
    nt = T // tt

    def body(a_ref, b_ref, o_ref, acc):
        _tn_step(acc, o_ref, a_ref[...], b_ref[...], pl.program_id(1), nt)

    return pl.pallas_call(
        body, name="tn_mm", grid=(K // tk, nt),
        in_specs=[pl.BlockSpec((tt, tk), lambda k, t: (t, k)), pl.BlockSpec((tt, N), lambda k, t: (t, 0))],
        out_specs=pl.BlockSpec((tk, N), lambda k, t: (k, 0)), out_shape=_sds((K, N), BF16),
        scratch_shapes=[pltpu.VMEM((tk, N), F32)], compiler_params=_cparams(VMEM_BIG))(a, b)


def tn_mm_mod(x, mod, b, sh, sc, S, tn):
    T = x.shape[0]
    N = b.shape[1]
    tt = _tile(S, 1024)
    tpb = S // tt
    nt = T // tt

    def body(x_ref, mod_ref, b_ref, o_ref, acc):
        h = _modulate(x_ref[...], mod_ref, sh, sc).astype(BF16)
        _tn_step(acc, o_ref, h, b_ref[...], pl.program_id(1), nt)

    return pl.pallas_call(
        body, name="tn_mm_mod", grid=(N // tn, nt),
        in_specs=[pl.BlockSpec((tt, D), lambda j, t: (t, 0)),
                  pl.BlockSpec((None, N_MOD, D), lambda j, t: (t // tpb, 0, 0)),
                  pl.BlockSpec((tt, tn), lambda j, t: (t, j))],
        out_specs=pl.BlockSpec((D, tn), lambda j, t: (0, j)), out_shape=_sds((D, N), BF16),
        scratch_shapes=[pltpu.VMEM((D, tn), F32)], compiler_params=_cparams(VMEM_BIG))(x, mod, b)


def tn_mm_mod_shards(x, mod, bg, bu, sh, sc, S):
    T = x.shape[0]
    tn = FF // 2
    tt = _tile(S, 1024)
    tpb = S // tt
    nt = T // tt

    def body(x_ref, mod_ref, bg_ref, bu_ref, o_ref, acc):
        j, t = pl.program_id(0), pl.program_id(1)
        h = _modulate(x_ref[...], mod_ref, sh, sc).astype(BF16)

        @pl.when(j < 2)
        def _():
            _tn_step(acc, o_ref, h, bg_ref[...], t, nt)

        @pl.when(j >= 2)
        def _():
            _tn_step(acc, o_ref, h, bu_ref[...], t, nt)

    return pl.pallas_call(
        body, name="tn_mm_mod_shards", grid=(N_CHIPS, nt),
        in_specs=[pl.BlockSpec((tt, D), lambda j, t: (t, 0)),
                  pl.BlockSpec((None, N_MOD, D), lambda j, t: (t // tpb, 0, 0)),
                  pl.BlockSpec((tt, tn), lambda j, t: (jnp.where(j < 2, t, 0), jnp.minimum(j, 1))),
                  pl.BlockSpec((tt, tn), lambda j, t: (jnp.where(j < 2, 0, t), jnp.maximum(j - 2, 0)))],
        out_specs=pl.BlockSpec((None, D, tn), lambda j, t: (j, 0, 0)), out_shape=_sds((N_CHIPS, D, tn), BF16),
        scratch_shapes=[pltpu.VMEM((D, tn), F32)], compiler_params=_cparams(VMEM_BIG))(x, mod, bg, bu)


def nt_mod_bwd(ds, w, offs, x, mod, dres, sc, S, tk, l=None):
    T = x.shape[0]
    B = T // S
    tm = _tile(S, 1024)
    tpb = S // tm
    Kd = ds[0].shape[1]
    nk = Kd // tk
    n_in = len(ds)

    def body(*refs):
        d_refs, w_refs = refs[:n_in], refs[n_in:2 * n_in]
        x_ref, mod_ref, r_ref, dx_ref, dsh_ref, dsc_ref, acc = refs[2 * n_in:]
        i, k = pl.program_id(0), pl.program_id(1)

        part = sum(lax.dot_general(d_ref[...], w_ref[...], (((1,), (1,)), ((), ())), preferred_element_type=F32)
                   for d_ref, w_ref in zip(d_refs, w_refs))

        @pl.when(k == 0)
        def _():
            acc[...] = part

        @pl.when(k > 0)
        def _():
            acc[...] += part

        @pl.when(k == nk - 1)
        def _():
            dh = acc[...]
            dx_ref[...] = dh * (1.0 + mod_ref[sc:sc + 1, :]) + r_ref[...]

            @pl.when(i % tpb == 0)
            def _():
                dsh_ref[...] = jnp.zeros_like(dsh_ref)
                dsc_ref[...] = jnp.zeros_like(dsc_ref)

            dsh_ref[...] += jnp.sum(dh, axis=0, keepdims=True)
            dsc_ref[...] += jnp.sum(dh * x_ref[...], axis=0, keepdims=True)

    tok = pl.BlockSpec((tm, D), lambda i, k: (i, 0))
    vec = pl.BlockSpec((None, 1, D), lambda i, k: (i // tpb, 0, 0))
    in_specs = [pl.BlockSpec((tm, tk), lambda i, k: (i, k)) for _ in ds]
    if l is None:
        in_specs += [pl.BlockSpec((D, tk), functools.partial(lambda i, k, o: (0, k + o), o=off // tk)) for off in offs]
    else:
        in_specs += [pl.BlockSpec((None, None, D, tk), functools.partial(lambda i, k, o: (l, k + o, 0, 0), o=off)) for off in offs]
    in_specs += [tok, pl.BlockSpec((None, N_MOD, D), lambda i, k: (i // tpb, 0, 0)), tok]
    return pl.pallas_call(
        body, name="nt_mod_bwd", grid=(T // tm, nk), in_specs=in_specs,
        out_specs=[tok, vec, vec],
        out_shape=[_sds((T, D), F32), _sds((B, 1, D), F32), _sds((B, 1, D), F32)],
        scratch_shapes=[pltpu.VMEM((tm, D), F32)],
        compiler_params=_cparams(VMEM_MOST))(*ds, *([w] * n_in), x, mod, dres)


def loss_head(y, tgt):
    T = y.shape[0]
    tm = _tile(T, 512)

    def body(y_ref, t_ref, l_ref, d_ref):
        @pl.when(pl.program_id(0) == 0)
        def _():
            l_ref[...] = jnp.zeros_like(l_ref)
        e = y_ref[...] - t_ref[...]
        d_ref[...] = e * (1.0 / D)
        l_ref[...] += 0.5 * jnp.sum(jnp.sum(e * e, axis=1, keepdims=True) * (1.0 / D))

    tok = pl.BlockSpec((tm, D), lambda i: (i, 0))
    return pl.pallas_call(
        body, name="loss_head", grid=(T // tm,), in_specs=[tok, tok],
        out_specs=[pl.BlockSpec((8, 128), lambda i: (0, 0)), tok],
        out_shape=[_sds((8, 128), F32), _sds((T, D), F32)],
        compiler_params=_cparams(VMEM_BIG))(y, tgt)


def _hgrn_block(q, fz, inp, go, st, lb, ng, blk):
    nc = blk // A_CHUNK
    lb_eff = jnp.maximum(lb, LB_FLOOR)
    log_f = jnp.logaddexp(jnp.log(lb_eff), jnp.log1p(-lb) + jax.nn.log_sigmoid(fz))
    k = (1.0 - lb) * jax.nn.sigmoid(-fz) - (lb_eff - lb)
    qf = jax.nn.silu(q)
    same_chunk = _iota((blk, blk), 0) // A_CHUNK == _iota((blk, blk), 1) // A_CHUNK
    tril = (same_chunk & (_iota((blk, blk), 1) <= _iota((blk, blk), 0))).astype(F32)
    G = _sel_l(tril, log_f)
    e_mat, et_mat = _head_sum_mats()
    G4, q4, k4, v4 = (z.reshape(nc, A_CHUNK, GW) for z in (G, qf, k, inp))
    shp = (nc, A_CHUNK, A_CHUNK, GW)
    one = (1, A_CHUNK, A_CHUNK, GW)
    mask = jnp.where(_iota(one, 2) <= _iota(one, 1), 0.0, -jnp.inf)
    decay = jnp.exp((G4[:, :, None, :] - G4[:, None, :, :]) + mask)
    prod = q4[:, :, None, :] * k4[:, None, :, :] * decay
    scores = _mm(prod.reshape(nc * A_CHUNK * A_CHUNK, GW), e_mat.astype(BF16))
    spread = _mm(scores, et_mat.astype(BF16)).reshape(shp)
    o_intra = jnp.sum(spread * v4[:, None, :, :], axis=2).reshape(blk, GW)
    head_diag = (_iota((GW, GW), 0) // HD == _iota((GW, GW), 1) // HD).astype(F32)
    g_last = [jnp.sum(log_f[c * A_CHUNK:(c + 1) * A_CHUNK], axis=0, keepdims=True) for c in range(nc)]
    g_last_b = jnp.concatenate([jnp.broadcast_to(g, (A_CHUNK, GW)) for g in g_last], axis=0)
    q_dec = qf * jnp.exp(G)
    k_end = k * jnp.exp(g_last_b - G)
    outs = []
    for c in range(nc):
        rows = slice(c * A_CHUNK, (c + 1) * A_CHUNK)
        outs.append(_mm_nt(q_dec[rows], st))
        st = st * jnp.exp(g_last[c]) + _mm_tn(inp[rows], k_end[rows]) * head_diag
    o = o_intra + jnp.concatenate(outs, axis=0)
    ms = _sel_r(o * o, e_mat) * (1.0 / HD)
    o = o * _sel_r(lax.rsqrt(ms + RMS_EPS), et_mat) * ng
    return o * jax.nn.silu(go), st


HGRN_BLK = 128


def hgrn_fwd(proj, lb, ng, S):
    T = proj.shape[0]
    B = T // S
    blk = min(HGRN_BLK, S)
    nb = S // blk

    def body(p_ref, lb_ref, ng_ref, o_ref, st_out_ref, st_ref):
        @pl.when(pl.program_id(1) == 0)
        def _():
            st_ref[...] = jnp.zeros_like(st_ref)
        st_out_ref[...] = st_ref[...]
        p = p_ref[...]
        o, st = _hgrn_block(p[:, 0:GW], p[:, GW:2 * GW], p[:, 2 * GW:3 * GW], p[:, 3 * GW:4 * GW],
                            st_ref[...], lb_ref[...], ng_ref[...], blk)
        o_ref[...] = o.astype(BF16)
        st_ref[...] = st

    vec = pl.BlockSpec((1, GW), lambda b, j: (0, 0))
    return pl.pallas_call(
        body, name="hgrn_fwd", grid=(B, nb),
        in_specs=[pl.BlockSpec((blk, 4 * GW), lambda b, j: (b * nb + j, C_A // (4 * GW))), vec, vec],
        out_specs=[pl.BlockSpec((blk, GW), lambda b, j: (b * nb + j, 0)),
                   pl.BlockSpec((None, GW, GW), lambda b, j: (b * nb + j, 0, 0))],
        out_shape=[_sds((T, GW), BF16), _sds((B * nb, GW, GW), F32)],
        scratch_shapes=[pltpu.VMEM((GW, GW), F32)],
        compiler_params=_cparams(VMEM_BIG))(proj, lb, ng)


def hgrn_bwd(proj, states, dcat, lb, ng, S):
    T = proj.shape[0]
    B = T // S
    blk = min(HGRN_BLK, S)
    nb = S // blk

    def body(p_ref, st_in_ref, do_ref, lb_ref, ng_ref, dp_ref, dlb_ref, dng_ref, dst_ref):
        b, j = pl.program_id(0), pl.program_id(1)

        @pl.when(j == 0)
        def _():
            dst_ref[...] = jnp.zeros_like(dst_ref)

        @pl.when((b == 0) & (j == 0))
        def _():
            dlb_ref[...] = jnp.zeros_like(dlb_ref)
            dng_ref[...] = jnp.zeros_like(dng_ref)

        p = p_ref[...]
        f = functools.partial(_hgrn_block, blk=blk)
        _, vjp = jax.vjp(f, p[:, 0:GW], p[:, GW:2 * GW], p[:, 2 * GW:3 * GW], p[:, 3 * GW:4 * GW],
                         st_in_ref[...], lb_ref[...], ng_ref[...])
        dq, df, di, dg, dst, dlb, dng = vjp((do_ref[...], dst_ref[...]))
        dp_ref[...] = jnp.concatenate([dq, df, di, dg], axis=1).astype(BF16)
        dst_ref[...] = dst
        dlb_ref[...] += dlb
        dng_ref[...] += dng

    def rev(b, j):
        return b * nb + (nb - 1 - j)

    vec = pl.BlockSpec((1, GW), lambda b, j: (0, 0))
    return pl.pallas_call(
        body, name="hgrn_bwd", grid=(B, nb),
        in_specs=[pl.BlockSpec((blk, 4 * GW), lambda b, j: (rev(b, j), C_A // (4 * GW))),
                  pl.BlockSpec((None, GW, GW), lambda b, j: (rev(b, j), 0, 0)),
                  pl.BlockSpec((blk, GW), lambda b, j: (rev(b, j), O_A // GW)), vec, vec],
        out_specs=[pl.BlockSpec((blk, 4 * GW), lambda b, j: (rev(b, j), 0)), vec, vec],
        out_shape=[_sds((T, 4 * GW), BF16), _sds((1, GW), F32), _sds((1, GW), F32)],
        scratch_shapes=[pltpu.VMEM((GW, GW), F32)],
        compiler_params=_cparams(VMEM_BIG))(proj, states, dcat, lb, ng)


ATT_TQ = 256


ATT_BANDS = 8


def _attn_block(q, k, v, cum, qpos0, scale, use_cum, n_free):
    s = _mm_nt(q, k) * scale
    if use_cum:
        s = s - cum
    band = s[:, n_free:]
    visible = _iota(band.shape, 1) <= (qpos0 - n_free) + _iota(band.shape, 0)
    band = jnp.where(visible, band, -jnp.inf)
    m = jnp.max(band, axis=-1, keepdims=True)
    if n_free:
        free = s[:, :n_free]
        m = jnp.maximum(m, jnp.max(free, axis=-1, keepdims=True))
    e = jnp.exp(band - m)
    denom = jnp.sum(e, axis=-1, keepdims=True)
    o = _mm(e, v[n_free:])
    if n_free:
        e = jnp.exp(free - m)
        denom = denom + jnp.sum(e, axis=-1, keepdims=True)
        o = o + _mm(e, v[:n_free])
    return o * (1.0 / denom)


def _bands(S, tq):
    nq = S // tq
    nb = min(ATT_BANDS, nq)
    per = nq // nb
    return [(r * per, (r + 1) * per, (r + 1) * per * tq) for r in range(nb)]


def attn_fwd(qa, qo, ka, ko, va, vo, cum, scale, S):
    T = qa.shape[0]
    B = T // S
    tq = min(ATT_TQ, S)
    nq = S // tq
    use_cum = cum is not None

    def body(*refs):
        if use_cum:
            q_ref, k_ref, v_ref, c_ref, o_ref = refs
        else:
            (q_ref, k_ref, v_ref, o_ref), c_ref = refs, None
        h, i = pl.program_id(1), pl.program_id(2)
        for lo, hi, kw in _bands(S, tq):
            @pl.when((i >= lo) & (i < hi))
            def _():
                crow = c_ref[pl.ds(h, 1), 0:kw] if use_cum else None
                o = _attn_block(q_ref[...], k_ref[0:kw, :], v_ref[0:kw, :], crow, i * tq, scale, use_cum, lo * tq)
                o_ref[...] = o.astype(BF16)

    in_specs = [pl.BlockSpec((tq, HP), lambda b, h, i: (b * nq + i, qo + h)),
                pl.BlockSpec((S, HP), lambda b, h, i: (b, ko + h)),
                pl.BlockSpec((S, HP), lambda b, h, i: (b, vo + h))]
    args = [qa, ka, va]
    if use_cum:
        in_specs.append(pl.BlockSpec((None, 8, S), lambda b, h, i: (b, 0, 0)))
        args.append(cum)
    return pl.pallas_call(
        body, name="attn_fwd", grid=(B, HEADS, nq), in_specs=in_specs,
        out_specs=pl.BlockSpec((tq, HP), lambda b, h, i: (b * nq + i, h)),
        out_shape=_sds((T, HEADS * HP), BF16),
        compiler_params=_cparams(VMEM_BIG))(*args)


def _attn_block_bwd(q, k, v, cum, do, qpos0, scale, use_cum, n_free):
    tn = (((0,), (0,)), ((), ()))
    nt = (((1,), (1,)), ((), ()))
    qb, dob = q.astype(BF16), do.astype(BF16)
    kb, vb = k.astype(BF16), v.astype(BF16)
    s = lax.dot_general(qb, kb, nt, preferred_element_type=F32) * scale
    if use_cum:
        s = s - cum
    band = s[:, n_free:]
    visible = _iota(band.shape, 1) <= (qpos0 - n_free) + _iota(band.shape, 0)
    parts = [(jnp.where(visible, band, -jnp.inf), n_free, s.shape[1])]
    if n_free:
        parts.append((s[:, :n_free], 0, n_free))
    m = functools.reduce(jnp.maximum, [jnp.max(sp, axis=-1, keepdims=True) for sp, _, _ in parts])
    es = [jnp.exp(sp - m) for sp, _, _ in parts]
    rinv = 1.0 / sum(jnp.sum(e, axis=-1, keepdims=True) for e in es)
    ps = [e * rinv for e in es]
    dps = [lax.dot_general(dob, vb[a:b], nt, preferred_element_type=F32) for _, a, b in parts]
    delta = sum(jnp.sum(p * dp, axis=-1, keepdims=True) for p, dp in zip(ps, dps))
    dq = jnp.zeros(q.shape, F32)
    out = []
    for p, dp, (_, a, b) in zip(ps, dps, parts):
        ds = p * (dp - delta)
        dsb = ds.astype(BF16)
        dq = dq + jnp.dot(dsb, kb[a:b], preferred_element_type=F32)
        out.append((a, b, lax.dot_general(dsb, qb, tn, preferred_element_type=F32) * scale,
                    lax.dot_general(p.astype(BF16), dob, tn, preferred_element_type=F32),
                    -jnp.sum(ds, axis=0, keepdims=True) if use_cum else None))
    return dq * scale, out


def attn_bwd(qa, qo, ka, ko, va, vo, cum, dcat, do_off, scale, S, out_dtype):
    T = qa.shape[0]
    B = T // S
    tq = min(ATT_TQ, S)
    nq = S // tq
    use_cum = cum is not None

    def body(*refs):
        if use_cum:
            q_ref, k_ref, v_ref, do_ref, c_ref, dq_ref, dk_ref, dv_ref, dc_ref, dk_acc, dv_acc = refs
        else:
            q_ref, k_ref, v_ref, do_ref, dq_ref, dk_ref, dv_ref, dk_acc, dv_acc = refs
        h, i = pl.program_id(1), pl.program_id(2)

        @pl.when(i == 0)
        def _():
            dk_acc[...] = jnp.zeros_like(dk_acc)
            dv_acc[...] = jnp.zeros_like(dv_acc)
            if use_cum:
                dc_ref[...] = jnp.zeros_like(dc_ref)

        for lo, hi, kw in _bands(S, tq):
            @pl.when((i >= lo) & (i < hi))
            def _():
                crow = c_ref[pl.ds(h, 1), 0:kw] if use_cum else None
                dq, pieces = _attn_block_bwd(q_ref[...], k_ref[0:kw, :], v_ref[0:kw, :], crow, do_ref[...], i * tq,
                                             scale, use_cum, lo * tq)
                dq_ref[...] = dq.astype(out_dtype)
                for a, b, dk, dv, dc in pieces:
                    dk_acc[a:b, :] += dk
                    dv_acc[a:b, :] += dv
                    if use_cum:
                        dc_ref[:, a:b] += dc

        @pl.when(i == nq - 1)
        def _():
            dk_ref[...] = dk_acc[...].astype(out_dtype)
            dv_ref[...] = dv_acc[...].astype(out_dtype)

    qspec = pl.BlockSpec((tq, HP), lambda b, h, i: (b * nq + i, qo + h))
    in_specs = [qspec, pl.BlockSpec((S, HP), lambda b, h, i: (b, ko + h)),
                pl.BlockSpec((S, HP), lambda b, h, i: (b, vo + h)),
                pl.BlockSpec((tq, HP), lambda b, h, i: (b * nq + i, do_off + h))]
    args = [qa, ka, va, dcat]
    kv_out = pl.BlockSpec((S, HP), lambda b, h, i: (b, h))
    out_specs = [pl.BlockSpec((tq, HP), lambda b, h, i: (b * nq + i, h)), kv_out, kv_out]
    out_shape = [_sds((T, HEADS * HP), out_dtype)] * 3
    if use_cum:
        in_specs.append(pl.BlockSpec((None, 8, S), lambda b, h, i: (b, 0, 0)))
        args.append(cum)
        out_specs.append(pl.BlockSpec((None, 1, S), lambda b, h, i: (b * HEADS + h, 0, 0)))
        out_shape.append(_sds((B * HEADS, 1, S), F32))
    return pl.pallas_call(
        body, name="attn_bwd", grid=(B, HEADS, nq), in_specs=in_specs, out_specs=out_specs, out_shape=out_shape,
        scratch_shapes=[pltpu.VMEM((S, HP), F32), pltpu.VMEM((S, HP), F32)],
        compiler_params=_cparams(VMEM_BIG))(*args)


def _tri(n, upper):
    r, c = _iota((n, n), 0), _iota((n, n), 1)
    return ((r <= c) if upper else (r >= c)).astype(F32)


def fox_gate_fwd(proj, bcol, S):
    T = proj.shape[0]
    B = T // S
    ts = _tile(S, 512)
    nt = S // ts

    def body(p_ref, b_ref, o_ref, carry):
        @pl.when(pl.program_id(1) == 0)
        def _():
            carry[...] = jnp.zeros_like(carry)
        cf = jnp.transpose(p_ref[...])[0:8, :]
        lf = jax.nn.log_sigmoid(cf + b_ref[...])
        cum = _sel_r(lf, _tri(ts, True)) + carry[...]
        o_ref[...] = cum
        carry[...] += jnp.sum(lf, axis=1, keepdims=True)

    return pl.pallas_call(
        body, name="fox_gate_fwd", grid=(B, nt),
        in_specs=[pl.BlockSpec((ts, HP), lambda b, j: (b * nt + j, C_CF // HP)), pl.BlockSpec((8, 1), lambda b, j: (0, 0))],
        out_specs=pl.BlockSpec((None, 8, ts), lambda b, j: (b, 0, j)), out_shape=_sds((B, 8, S), F32),
        scratch_shapes=[pltpu.VMEM((8, 1), F32)],
        compiler_params=_cparams(VMEM_BIG))(proj, bcol)


def fox_gate_bwd(proj, bcol, dcum, S):
    T = proj.shape[0]
    B = T // S
    ts = _tile(S, 512)
    nt = S // ts

    def body(p_ref, b_ref, dc_ref, dp_ref, db_ref, carry):
        b, j = pl.program_id(0), pl.program_id(1)

        @pl.when(j == 0)
        def _():
            carry[...] = jnp.zeros_like(carry)

        @pl.when((b == 0) & (j == 0))
        def _():
            db_ref[...] = jnp.zeros_like(db_ref)

        cf = jnp.transpose(p_ref[...])[0:8, :]
        dc = dc_ref[...]
        dlf = _sel_r(dc, _tri(ts, False)) + carry[...]
        carry[...] += jnp.sum(dc, axis=1, keepdims=True)
        dcf = dlf * jax.nn.sigmoid(-(cf + b_ref[...]))
        db_ref[...] += jnp.sum(dcf, axis=1, keepdims=True)
        full = jnp.concatenate([dcf, jnp.zeros((HP - 8, ts), F32)], axis=0)
        dp_ref[...] = jnp.transpose(full).astype(BF16)

    def rev(b, j):
        return nt - 1 - j

    return pl.pallas_call(
        body, name="fox_gate_bwd", grid=(B, nt),
        in_specs=[pl.BlockSpec((ts, HP), lambda b, j: (b * nt + rev(b, j), C_CF // HP)),
                  pl.BlockSpec((8, 1), lambda b, j: (0, 0)),
                  pl.BlockSpec((None, 8, ts), lambda b, j: (b, 0, rev(b, j)))],
        out_specs=[pl.BlockSpec((ts, HP), lambda b, j: (b * nt + rev(b, j), 0)), pl.BlockSpec((8, 1), lambda b, j: (0, 0))],
        out_shape=[_sds((T, HP), BF16), _sds((8, 1), F32)],
        scratch_shapes=[pltpu.VMEM((8, 1), F32)],
        compiler_params=_cparams(VMEM_BIG))(proj, bcol, dcum)


def _mla_pre(blk, gq, gkv, wq, wkv, place, cos_q, sin_q, cs_k):
    nq = _rms(blk[:, 0:B_Q_LORA], gq)
    nkv = _rms(blk[:, B_Q_LORA:B_Q_LORA + B_KV_LORA], gkv)
    qq = _mm(nq, wq)
    q = qq[:, 0:HEADS * HP] * cos_q + qq[:, HEADS * HP:] * sin_q
    kv = _mm(nkv, wkv)
    k = kv[:, 0:HEADS * HP] + _mm(blk[:, B_Q_LORA + B_KV_LORA:] * cs_k, place)
    return q, k, kv[:, HEADS * HP:]


def mla_pre_fwd(proj, gq, gkv, wq, wkv, place, cos_q, sin_q, cs_k, S):
    T = proj.shape[0]
    tm = _tile(S, 512)
    tpb = S // tm
    W = HEADS * HP

    def body(p_ref, gq_ref, gkv_ref, wq_ref, wkv_ref, pl_ref, cq_ref, sq_ref, ck_ref, q_ref, k_ref, v_ref):
        q, k, v = _mla_pre(p_ref[...], gq_ref[...], gkv_ref[...], wq_ref[...], wkv_ref[...], pl_ref[...],
                           cq_ref[...], sq_ref[...], ck_ref[...])
        q_ref[...] = q
        k_ref[...] = k
        v_ref[...] = v

    def full(a):
        return pl.BlockSpec(a.shape, lambda i: (0,) * a.ndim)

    tok = pl.BlockSpec((tm, W), lambda i: (i, 0))
    return pl.pallas_call(
        body, name="mla_pre_fwd", grid=(T // tm,),
        in_specs=[pl.BlockSpec((tm, W), lambda i: (i, C_B // W)), full(gq), full(gkv), full(wq), full(wkv), full(place),
                  pl.BlockSpec((tm, W), lambda i: (i % tpb, 0)), pl.BlockSpec((tm, W), lambda i: (i % tpb, 0)),
                  pl.BlockSpec((tm, HP), lambda i: (i % tpb, 0))],
        out_specs=[tok] * 3, out_shape=[_sds((T, W), F32)] * 3,
        compiler_params=_cparams(VMEM_BIG))(proj, gq, gkv, wq, wkv, place, cos_q, sin_q, cs_k)


def mla_pre_bwd(proj, gq, gkv, wq, wkv, place, cos_q, sin_q, cs_k, dq, dk, dv, S):
    T = proj.shape[0]
    tm = _tile(S, 512)
    tpb = S // tm
    W = HEADS * HP

    def body(p_ref, gq_ref, gkv_ref, wq_ref, wkv_ref, pl_ref, cq_ref, sq_ref, ck_ref, dq_ref, dk_ref, dv_ref,
             dp_ref, dgq_ref, dgkv_ref, dwq_ref, dwkv_ref):
        @pl.when(pl.program_id(0) == 0)
        def _():
            for r in (dgq_ref, dgkv_ref, dwq_ref, dwkv_ref):
                r[...] = jnp.zeros_like(r)

        f = functools.partial(_mla_pre, place=pl_ref[...], cos_q=cq_ref[...], sin_q=sq_ref[...], cs_k=ck_ref[...])
        _, vjp = jax.vjp(f, p_ref[...], gq_ref[...], gkv_ref[...], wq_ref[...], wkv_ref[...])
        dp, dgq, dgkv, dwq, dwkv = vjp((dq_ref[...], dk_ref[...], dv_ref[...]))
        dp_ref[...] = dp.astype(BF16)
        dgq_ref[...] += dgq
        dgkv_ref[...] += dgkv
        dwq_ref[...] += dwq
        dwkv_ref[...] += dwkv

    def full(a):
        return pl.BlockSpec(a.shape, lambda i: (0,) * a.ndim)

    tok = pl.BlockSpec((tm, W), lambda i: (i, 0))
    return pl.pallas_call(
        body, name="mla_pre_bwd", grid=(T // tm,),
        in_specs=[pl.BlockSpec((tm, W), lambda i: (i, C_B // W)), full(gq), full(gkv), full(wq), full(wkv), full(place),
                  pl.BlockSpec((tm, W), lambda i: (i % tpb, 0)), pl.BlockSpec((tm, W), lambda i: (i % tpb, 0)),
                  pl.BlockSpec((tm, HP), lambda i: (i % tpb, 0)), tok, tok, tok],
        out_specs=[tok, full(gq), full(gkv), full(wq), full(wkv)],
        out_shape=[_sds((T, W), BF16), _sds(gq.shape, F32), _sds(gkv.shape, F32), _sds(wq.shape, F32), _sds(wkv.shape, F32)],
        compiler_params=_cparams(VMEM_BIG))(proj, gq, gkv, wq, wkv, place, cos_q, sin_q, cs_k, dq, dk, dv)


GMLP_CHUNKS = 4


def _gmlp_block(blk, lg, lb, ws, bs):
    u = jax.nn.gelu(blk[:, 0:GW])
    v = jax.nn.gelu(blk[:, GW:2 * GW])
    mu = jnp.mean(v, axis=-1, keepdims=True)
    var = jnp.mean(jnp.square(v - mu), axis=-1, keepdims=True)
    vn = (v - mu) * lax.rsqrt(var + LN_EPS) * lg + lb
    causal = _iota((D_CHUNK, D_CHUNK), 1) <= _iota((D_CHUNK, D_CHUNK), 0)
    group = _iota((1, GW), 1) // HD
    w = [jnp.where(causal, ws[g], 0.0) for g in range(HEADS)]
    chunks = []
    for c in range(blk.shape[0] // D_CHUNK):
        vc = vn[c * D_CHUNK:(c + 1) * D_CHUNK]
        mixed = jnp.zeros((D_CHUNK, GW), F32)
        for g in range(HEADS):
            mixed = mixed + jnp.where(group == g, _mm(w[g], vc) + bs[g], 0.0)
        chunks.append(mixed)
    return u * jnp.concatenate(chunks, axis=0)


def _gmlp_tile(T):
    return _tile(T, GMLP_CHUNKS * D_CHUNK) if T % (GMLP_CHUNKS * D_CHUNK) == 0 else D_CHUNK


def gmlp_fwd(proj, lg, lb, ws, bs):
    T = proj.shape[0]
    tm = _gmlp_tile(T)

    def body(p_ref, lg_ref, lb_ref, ws_ref, bs_ref, o_ref):
        o_ref[...] = _gmlp_block(p_ref[...], lg_ref[...], lb_ref[...], ws_ref[...], bs_ref[...]).astype(BF16)

    def full(a):
        return pl.BlockSpec(a.shape, lambda i: (0,) * a.ndim)

    return pl.pallas_call(
        body, name="gmlp_fwd", grid=(T // tm,),
        in_specs=[pl.BlockSpec((tm, 2 * GW), lambda i: (i, C_D // (2 * GW))), full(lg), full(lb), full(ws), full(bs)],
        out_specs=pl.BlockSpec((tm, GW), lambda i: (i, 0)), out_shape=_sds((T, GW), BF16),
        compiler_params=_cparams(VMEM_BIG))(proj, lg, lb, ws, bs)


def gmlp_bwd(proj, lg, lb, ws, bs, dcat):
    T = proj.shape[0]
    tm = _gmlp_tile(T)

    def body(p_ref, lg_ref, lb_ref, ws_ref, bs_ref, do_ref, dp_ref, dlg_ref, dlb_ref, dws_ref, dbs_ref):
        @pl.when(pl.program_id(0) == 0)
        def _():
            for r in (dlg_ref, dlb_ref, dws_ref, dbs_ref):
                r[...] = jnp.zeros_like(r)

        _, vjp = jax.vjp(_gmlp_block, p_ref[...], lg_ref[...], lb_ref[...], ws_ref[...], bs_ref[...])
        dp, dlg, dlb, dws, dbs = vjp(do_ref[...])
        dp_ref[...] = dp.astype(BF16)
        dlg_ref[...] += dlg
        dlb_ref[...] += dlb
        dws_ref[...] += dws
        dbs_ref[...] += dbs

    def full(a):
        return pl.BlockSpec(a.shape, lambda i: (0,) * a.ndim)

    return pl.pallas_call(
        body, name="gmlp_bwd", grid=(T // tm,),
        in_specs=[pl.BlockSpec((tm, 2 * GW), lambda i: (i, C_D // (2 * GW))), full(lg), full(lb), full(ws), full(bs),
                  pl.BlockSpec((tm, GW), lambda i: (i, O_D // GW))],
        out_specs=[pl.BlockSpec((tm, 2 * GW), lambda i: (i, 0)), full(lg), full(lb), full(ws), full(bs)],
        out_shape=[_sds((T, 2 * GW), BF16), _sds(lg.shape, F32), _sds(lb.shape, F32), _sds(ws.shape, F32), _sds(bs.shape, F32)],
        compiler_params=_cparams(VMEM_BIG))(proj, lg, lb, ws, bs, dcat)


def _lb_all(logits):
    m = jnp.max(logits, axis=0, keepdims=True)
    e = jnp.exp(logits - m)
    sm = e / jnp.sum(e, axis=0, keepdims=True)
    return jnp.concatenate([sm[0:1] - sm[0:1], (sm[0:1] + sm[1:2]) - sm[0:1]], axis=0)


def lb_fwd(logits):
    def body(l_ref, o_ref):
        o_ref[...] = _lb_all(l_ref[...])

    return pl.pallas_call(body, name="lb_fwd", out_shape=_sds(logits.shape, F32))(logits)


def lb_bwd(logits, dlb):
    def body(l_ref, d_ref, o_ref):
        _, vjp = jax.vjp(_lb_all, l_ref[...])
        o_ref[...] = vjp(d_ref[...])[0]

    return pl.pallas_call(body, name="lb_bwd", out_shape=_sds(logits.shape, F32))(logits, dlb)


def ada_grad(c_all, dmod_cols):
    N = dmod_cols.shape[1]
    tn = _tile(N, 1152)

    def body(c_ref, d_ref, o_ref):
        h = jax.nn.silu(c_ref[...]).astype(BF16)
        o_ref[...] = lax.dot_general(h, d_ref[...].astype(BF16), (((0,), (0,)), ((), ())), preferred_element_type=F32)

    nb = c_all.shape[0]
    return pl.pallas_call(
        body, name="ada_grad", grid=(N // tn,),
        in_specs=[pl.BlockSpec((nb, D), lambda j: (0, 0)), pl.BlockSpec((nb, tn), lambda j: (0, j))],
        out_specs=pl.BlockSpec((D, tn), lambda j: (0, j)), out_shape=_sds((D, N), F32),
        compiler_params=_cparams(VMEM_BIG))(c_all, dmod_cols)


def sum_slots(a, n, name):
    _, R, C = a.shape
    tr = _row_tile(R, C, n)

    def body(a_ref, o_ref):
        acc = a_ref[0]
        for k in range(1, n):
            acc = acc + a_ref[k]
        o_ref[...] = acc

    return pl.pallas_call(
        body, name=name, grid=(R // tr,),
        in_specs=[pl.BlockSpec((n, tr, C), lambda i: (0, i, 0))],
        out_specs=pl.BlockSpec((tr, C), lambda i: (i, 0)), out_shape=_sds((R, C), F32),
        compiler_params=_cparams(VMEM_BIG))(a)


def _row_tile(R, C=D, n=1, mult=8, elems=1 << 18):
    limit = max(mult, elems // (C * n))
    for t in range(limit - limit % mult, mult - 1, -mult):
        if R % t == 0:
            return t
    return R


def adamw(w, g, m, v, name, echo=False):
    R, C = w.shape
    tr = _row_tile(R, C, elems=1 << 19)
    c1 = 1.0 - ADAM_B1 ** ADAM_STEP
    c2 = 1.0 - ADAM_B2 ** ADAM_STEP
    n_out = 4 if echo else 3

    def body(w_ref, g_ref, m_ref, v_ref, d_ref, nm_ref, nv_ref, *g_out):
        g_ = g_ref[...]
        nm = ADAM_B1 * m_ref[...] + (1.0 - ADAM_B1) * g_
        nv = ADAM_B2 * v_ref[...] + (1.0 - ADAM_B2) * jnp.square(g_)
        d_ref[...] = -ADAM_LR * ((nm / c1) / (jnp.sqrt(nv / c2) + ADAM_EPS) + ADAM_WD * w_ref[...])
        nm_ref[...] = nm
        nv_ref[...] = nv
        if echo:
            g_out[0][...] = g_

    spec = pl.BlockSpec((tr, C), lambda i: (i, 0))
    return pl.pallas_call(body, name=name, grid=(R // tr,), in_specs=[spec] * 4, out_specs=[spec] * n_out,
                          out_shape=[_sds((R, C), F32)] * n_out, compiler_params=_cparams(VMEM_BIG))(w, g, m, v)


def _rot_cols(w):
    return jnp.concatenate([-w[:, 16:32], w[:, 0:16]], axis=1)


def _fold_rot(d):
    return jnp.concatenate([d[:, 16:32], -d[:, 0:16]], axis=1)


def _pad_heads(w, off, axis):
    parts = []
    for h in range(HEADS):
        piece = lax.slice_in_dim(w, off + HD * h, off + HD * (h + 1), axis=axis)
        parts += [piece, jnp.zeros_like(piece)]
    return parts


def _unpad_heads(d, off, axis):
    return [lax.slice_in_dim(d, off + HP * h, off + HP * h + HD, axis=axis) for h in range(HEADS)]


def mix_in_ext(w):
    z = lambda n: jnp.zeros((w.shape[0], n), w.dtype)
    kr = w[:, 1408:1440]
    cols = [w[:, 0:1408], kr, _rot_cols(kr), z(64)]
    cols += _pad_heads(w, 1440, 1) + _pad_heads(w, 1696, 1) + _pad_heads(w, 1952, 1)
    cols += [w[:, 2212:2724], w[:, 2208:2212], z(NP - C_CF - HEADS)]
    return jnp.concatenate(cols, axis=1)


def mix_in_unext(d):
    kr = d[:, 1408:1440] + _fold_rot(d[:, 1440:1472])
    cols = [d[:, 0:1408], kr] + _unpad_heads(d, C_CQ, 1) + _unpad_heads(d, C_CK, 1) + _unpad_heads(d, C_CV, 1)
    cols += [d[:, C_CF:C_CF + HEADS], d[:, C_D:C_D + 2 * GW]]
    return jnp.concatenate(cols, axis=1)


def mix_out_ext(w):
    return jnp.concatenate([w[0:GW]] + _pad_heads(w, GW, 0) + _pad_heads(w, 2 * GW, 0) + [w[3 * GW:4 * GW]], axis=0)


def mix_out_unext(d):
    return jnp.concatenate([d[0:GW]] + _unpad_heads(d, O_B, 0) + _unpad_heads(d, O_C, 0) + [d[O_D:O_D + GW]], axis=0)


def uq_ext(w):
    z = lambda n: jnp.zeros((w.shape[0], n), w.dtype)
    a, b = [], []
    for h in range(HEADS):
        o = (B_NOPE + B_ROPE) * h
        a += [w[:, o:o + B_NOPE + B_ROPE], z(32)]
        b += [z(B_NOPE), _rot_cols(w[:, o + B_NOPE:o + B_NOPE + B_ROPE]), z(32)]
    return jnp.concatenate(a + b, axis=1)


def uq_unext(d):
    cols = []
    for h in range(HEADS):
        o = HP * h
        cols += [d[:, o:o + B_NOPE], d[:, o + B_NOPE:o + B_NOPE + B_ROPE]
                 + _fold_rot(d[:, HEADS * HP + o + B_NOPE:HEADS * HP + o + B_NOPE + B_ROPE])]
    return jnp.concatenate(cols, axis=1)


def ukv_ext(w):
    z = jnp.zeros((w.shape[0], HD), w.dtype)
    k, v = [], []
    for h in range(HEADS):
        k += [w[:, 2 * HD * h:2 * HD * h + HD], z]
        v += [w[:, 2 * HD * h + HD:2 * HD * (h + 1)], z]
    return jnp.concatenate(k + v, axis=1)


def ukv_unext(d):
    cols = []
    for h in range(HEADS):
        cols += [d[:, HP * h:HP * h + HD], d[:, HEADS * HP + HP * h:HEADS * HP + HP * h + HD]]
    return jnp.concatenate(cols, axis=1)


def rope_tables(S):
    half = B_ROPE // 2
    inv_freq = ROPE_THETA ** (-jnp.arange(half, dtype=F32) / half)
    ang = jnp.arange(S).astype(F32)[:, None] * inv_freq[None, :]
    cos = jnp.tile(jnp.cos(ang), (1, 2))
    sin = jnp.tile(jnp.sin(ang), (1, 2))
    one, zero = jnp.ones((S, B_NOPE), F32), jnp.zeros((S, B_NOPE), F32)
    z32 = jnp.zeros((S, 32), F32)
    cos_q = jnp.tile(jnp.concatenate([one, cos, z32], axis=1), (1, HEADS))
    sin_q = jnp.tile(jnp.concatenate([zero, sin, z32], axis=1), (1, HEADS))
    cs_k = jnp.concatenate([cos, sin, zero], axis=1)
    place = np.zeros((HP, HEADS * HP), np.float32)
    for h in range(HEADS):
        for j in range(B_ROPE):
            place[j, h * HP + B_NOPE + j] = 1.0
            place[B_ROPE + j, h * HP + B_NOPE + j] = 1.0
    return cos_q, sin_q, cs_k, jnp.asarray(place, BF16)


def layer_fwd(x, mod, get, tabs, S):
    cos_q, sin_q, cs_k, place = tabs
    p = dict(get("ffn1", x))
    l = p["wl"]
    zg1, zu1, act1 = ffn_in_fwd(x, mod, p["ffn1_in"], l, 0, 1, S)
    y1, x1 = out_ln_fwd(act1, p["ffn1_out"], x, mod, p["ln_g"][0:1], p["ln_b"][0:1], 2, 0.5, S, l)
    p.update(get("mix", x1))
    proj = mix_in_fwd(x1, mod, p["mix_in"], 3, 4, S)
    o_a, states = hgrn_fwd(proj, p["lb"], p["ng"], S)
    q_b, k_b, v_b = mla_pre_fwd(proj, p["gq"], p["gkv"], p["wq"], p["wkv"], place, cos_q, sin_q, cs_k, S)
    o_b = attn_fwd(q_b, 0, k_b, 0, v_b, 0, None, (B_NOPE + B_ROPE) ** -0.5, S)
    cum = fox_gate_fwd(proj, p["bcol"], S)
    o_c = attn_fwd(proj, C_CQ // HP, proj, C_CK // HP, proj, C_CV // HP, cum, HD ** -0.5, S)
    o_d = gmlp_fwd(proj, p["g_lg"], p["g_lb"], p["ws"], p["bs"])
    cat = jnp.concatenate([o_a, o_b, o_c, o_d], axis=1)
    y2, x2 = out_ln_fwd(cat, p["mix_out"], x1, mod, p["ln_g"][1:2], p["ln_b"][1:2], 5, 1.0, S)
    p.update(get("ffn2", x2))
    zg3, zu3, act3 = ffn_in_fwd(x2, mod, p["ffn2_in"], l, 6, 7, S)
    y3, x3 = out_ln_fwd(act3, p["ffn2_out"], x2, mod, p["ln_g"][2:3], p["ln_b"][2:3], 8, 0.5, S, l)
    saved = dict(x=x, zg1=zg1, zu1=zu1, act1=act1, y1=y1, x1=x1, proj=proj, states=states, q_b=q_b, k_b=k_b, v_b=v_b,
                 cum=cum, cat=cat, y2=y2, x2=x2, zg3=zg3, zu3=zu3, act3=act3, y3=y3, p=p)
    return x3, saved


def _ffn_bwd(dxn, x_in, y, zg, zu, act, mod, w_in, w_out, l, lg, lb, idx, S, emit):
    sh, sc, gate = idx
    dres, dy, dgate, dlg, dlb = ln_res_bwd(dxn, x_in, y, mod, lg, lb, gate, 0.5, S)
    dzg, dzu = swiglu_bwd(dy, w_out, l, zg, zu, S)
    dw_out = tn_mm(act, dy, FF // 2).reshape(N_CHIPS, FF // N_CHIPS, D)
    dw_in = tn_mm_mod_shards(x_in, mod, dzg, dzu, sh, sc, S)
    mod = mod + emit(dw_in, dw_out)
    dx, dsh, dsc = nt_mod_bwd([dzg, dzu], w_in, [0, 2], x_in, mod, dres, sc, S, FF // 2, l)
    return dx, dw_in, dw_out, dlg, dlb, {sh: dsh, sc: dsc, gate: dgate}, mod


def layer_bwd(dx3, mod, sv, tabs, S, emit):
    cos_q, sin_q, cs_k, place = tabs
    p = sv["p"]
    l = p["wl"]
    g = {}
    dm = {}

    def emit_ffn(part):
        def f(dw_in, dw_out):
            g[part + "_in"], g[part + "_out"] = dw_in, dw_out
            return emit(part, g)
        return f

    dx2, _, _, dlg2, dlb2, d, mod = _ffn_bwd(
        dx3, sv["x2"], sv["y3"], sv["zg3"], sv["zu3"], sv["act3"], mod, p["ffn2_in"], p["ffn2_out"], l,
        p["ln_g"][2:3], p["ln_b"][2:3], (6, 7, 8), S, emit_ffn("ffn2"))
    dm.update(d)
    dres, dy2, dm[5], dlg1, dlb1 = ln_res_bwd(dx2, sv["x1"], sv["y2"], mod, p["ln_g"][1:2], p["ln_b"][1:2], 5, 1.0, S)
    dcat = nt_plain(dy2, p["mix_out"])
    g["mix_out"] = tn_mm(sv["cat"], dy2, NCAT // 2)
    proj = sv["proj"]
    d_a, g["lb"], g["ng"] = hgrn_bwd(proj, sv["states"], dcat, p["lb"], p["ng"], S)
    dq_c, dk_c, dv_c, dcum = attn_bwd(proj, C_CQ // HP, proj, C_CK // HP, proj, C_CV // HP, sv["cum"], dcat,
                                      O_C // HP, HD ** -0.5, S, BF16)
    B = proj.shape[0] // S
    dcum = jnp.concatenate([dcum.reshape(B, HEADS, S), jnp.zeros((B, 8 - HEADS, S), F32)], axis=1)
    d_cf, g["bcol"] = fox_gate_bwd(proj, p["bcol"], dcum, S)
    dq_b, dk_b, dv_b = attn_bwd(sv["q_b"], 0, sv["k_b"], 0, sv["v_b"], 0, None, dcat, O_B // HP,
                                (B_NOPE + B_ROPE) ** -0.5, S, F32)
    d_b, g["gq"], g["gkv"], g["wq"], g["wkv"] = mla_pre_bwd(
        proj, p["gq"], p["gkv"], p["wq"], p["wkv"], place, cos_q, sin_q, cs_k, dq_b, dk_b, dv_b, S)
    d_d, g["g_lg"], g["g_lb"], g["ws"], g["bs"] = gmlp_bwd(proj, p["g_lg"], p["g_lb"], p["ws"], p["bs"], dcat)
    dproj = jnp.concatenate([d_a, d_b, dq_c, dk_c, dv_c, d_d, d_cf, jnp.zeros_like(d_cf)], axis=1)
    g["mix_in"] = tn_mm_mod(sv["x1"], mod, dproj, 3, 4, S, NP_TILE)
    mod = mod + emit("mix", g)
    dx1, dm[3], dm[4] = nt_mod_bwd([dproj], p["mix_in"], [0], sv["x1"], mod, dres, 4, S, NP_TILE)
    last = []

    def emit_last(dw_in, dw_out):
        last.append(emit_ffn("ffn1")(dw_in, dw_out))
        return last[0]

    dx0, _, _, dlg0, dlb0, d, mod = _ffn_bwd(
        dx1, sv["x"], sv["y1"], sv["zg1"], sv["zu1"], sv["act1"], mod, p["ffn1_in"], p["ffn1_out"], l,
        p["ln_g"][0:1], p["ln_b"][0:1], (0, 1, 2), S, emit_last)
    dm.update(d)
    g["ln_g"] = jnp.concatenate([dlg0, dlg1, dlg2], axis=0)
    g["ln_b"] = jnp.concatenate([dlb0, dlb1, dlb2], axis=0)
    dmod = jnp.concatenate([dm[i] for i in range(N_MOD)], axis=1)
    return dx0, dmod, g, last[0]


def local_step(x, c8, tgt, get, lb_logits, S, emit=None):
    B = x.shape[0] // S
    tabs = rope_tables(S)
    lb_all = lb_fwd(lb_logits)
    mods, saved = [], []
    h = x
    for l in range(DEPTH):
        pa = get(l, "ada", h)
        mod = mod_fwd(c8, pa["ada_w"], pa["wl"], pa["ada_b"])[0:B].reshape(B, N_MOD, D)

        def get_l(part, after, l=l):
            p = dict(get(l, part, after))
            if part == "mix":
                p["lb"] = lb_all[l:l + 1]
            return p

        h, sv = layer_fwd(h, mod, get_l, tabs, S)
        mods.append(mod)
        saved.append(sv)
    loss_tile, dh = loss_head(h, tgt)
    grads, dmods, dlb = [None] * DEPTH, [None] * DEPTH, [None] * DEPTH
    tie = jnp.zeros((), F32)
    for l in reversed(range(DEPTH)):
        emit_l = (lambda part, g: jnp.zeros((), F32)) if emit is None else functools.partial(emit, l)
        dh, dmods[l], grads[l], tie = layer_bwd(dh, mods[l] + tie, saved[l], tabs, S, emit_l)
        dlb[l] = grads[l].pop("lb")
    d_logits = lb_bwd(lb_logits, jnp.concatenate(dlb, axis=0))
    return loss_tile, dh, dmods, grads, d_logits


ANY = pl.BlockSpec(memory_space=pl.ANY)


def _place():
    x, y, c = lax.axis_index("x"), lax.axis_index("y"), lax.axis_index("c")
    chips = [(1 - x, y), (x, 1 - y), (1 - x, 1 - y)]
    return x, y, c, chips


def _rcopy(src, dst, sems, k, to):
    send_sems, recv_sems = sems
    return pltpu.make_async_remote_copy(src_ref=src, dst_ref=dst, send_sem=send_sems.at[k], recv_sem=recv_sems.at[k],
                                        device_id=to, device_id_type=MESH)


def _dma_sems(n_remote, n_local):
    return [pltpu.SemaphoreType.DMA((n_remote,)), pltpu.SemaphoreType.DMA((n_remote,)), pltpu.SemaphoreType.DMA((n_local,))]


def own_slot(src, chip):
    L = src.shape[0]
    return lax.dynamic_update_slice(lax.empty((L, N_CHIPS) + src.shape[1:], src.dtype), src[:, None], (0, chip, 0, 0))


HBM_SPEC = pl.BlockSpec(memory_space=pltpu.HBM)
SEM_SPEC = pl.BlockSpec(memory_space=pltpu.SEMAPHORE)
DATAFLOW = pltpu.SideEffectType.DATAFLOW_SIDE_EFFECTING


def _split_start(srcs, lands, copies, n_copies, dep, name):
    n, m = len(srcs), len(lands)

    def body(*refs):
        ins = refs[:n + m]
        send_sems, recv_sems = refs[n + m + 1], refs[n + m + 2]
        token = refs[-1]
        for k, (src, dst, to) in enumerate(copies(ins[:n], ins[n:], _place())):
            pltpu.make_async_remote_copy(src_ref=src, dst_ref=dst, send_sem=send_sems.at[k], recv_sem=recv_sems.at[k],
                                         device_id=to, device_id_type=MESH).start()
        token[...] = jnp.zeros_like(token)

    arrs = list(srcs) + list(lands)
    outs = pl.pallas_call(
        body, name=name,
        out_shape=(pltpu.SemaphoreType.DMA((n_copies,)), pltpu.SemaphoreType.DMA((n_copies,)),
                   *[pltpu.HBM(a.shape, a.dtype) for a in arrs], _sds((8, 128), F32)),
        in_specs=[HBM_SPEC] * (n + m) + [ANY],
        out_specs=(SEM_SPEC, SEM_SPEC, *[HBM_SPEC] * (n + m), pl.BlockSpec(memory_space=pltpu.VMEM)),
        input_output_aliases={i: 2 + i for i in range(n + m)},
        compiler_params=pltpu.CompilerParams(has_side_effects=DATAFLOW),
    )(*[pltpu.with_memory_space_constraint(a, pltpu.HBM) for a in arrs], dep)
    return outs[0], outs[1], list(outs[2:2 + n]), list(outs[2 + n:2 + n + m]), outs[-1]


def _split_wait(handle, arrivals, after, name):
    send_sems, recv_sems, srcs, lands, _ = handle
    n, m = len(srcs), len(lands)

    def body(*refs):
        ins = refs[:n + m]
        send_sems, recv_sems = refs[n + m], refs[n + m + 1]
        x, y, c, chips = place = _place()
        for k, (src, dst) in enumerate(arrivals(ins[:n], ins[n:], place)):
            cp = pltpu.make_async_remote_copy(src_ref=src, dst_ref=dst, send_sem=send_sems.at[k], recv_sem=recv_sems.at[k],
                                              device_id=(x, y, 1 - c), device_id_type=MESH)
            cp.wait_send()
            cp.wait_recv()

    arrs = list(srcs) + list(lands)
    outs = pl.pallas_call(
        body, name=name, out_shape=[pltpu.HBM(a.shape, a.dtype) for a in arrs],
        in_specs=[HBM_SPEC] * (n + m) + [SEM_SPEC, SEM_SPEC, ANY], out_specs=[HBM_SPEC] * (n + m),
        input_output_aliases={i: i for i in range(n + m)},
        compiler_params=pltpu.CompilerParams(has_side_effects=DATAFLOW),
    )(*arrs, send_sems, recv_sems, after)
    return list(outs[:n]), list(outs[n:])


def _ag_part(ref, k, hc):
    rh = ref.shape[2] // 2
    return ref.at[:, k, pl.ds(hc * rh, rh), :]


def ag_start(srcs, lands, dep, name):
    def copies(s, d, place):
        x, y, c, chips = place
        out = []
        for j, (px, py) in enumerate(chips):
            for i in range(len(s)):
                rh = s[i].shape[1] // 2
                out.append((s[i].at[:, pl.ds(c * rh, rh), :], _ag_part(d[i], 2 * x + y, c), (px, py, c)))
        return out

    return _split_start(srcs, lands, copies, 3 * len(srcs), dep, name)


def ag_wait(handle, after, name):
    def arrivals(s, d, place):
        x, y, c, chips = place
        out = []
        for j, (px, py) in enumerate(chips):
            for i in range(len(s)):
                rh = s[i].shape[1] // 2
                out.append((s[i].at[:, pl.ds(c * rh, rh), :], _ag_part(d[i], 2 * px + py, c)))
        return out

    return _split_wait(handle, arrivals, after, name)


def ag_forward(lands, name):
    n = len(lands)

    def body(*refs):
        bufs, token = refs[n:2 * n], refs[2 * n]
        send_sems, recv_sems = refs[2 * n + 1:]
        x, y, c, chips = _place()
        sems = (send_sems, recv_sems)
        token[...] = jnp.zeros_like(token)
        cps = []
        for j, (px, py) in enumerate(chips):
            for i in range(n):
                part = _ag_part(bufs[i], 2 * px + py, c)
                cps.append(_rcopy(part, part, sems, 3 * i + j, (x, y, 1 - c)))
        for cp in cps:
            cp.start()
        for j, (px, py) in enumerate(chips):
            for i in range(n):
                part = _ag_part(bufs[i], 2 * px + py, 1 - c)
                _rcopy(part, part, sems, 3 * i + j, (x, y, 1 - c)).wait_recv()
        for cp in cps:
            cp.wait_send()

    outs = pl.pallas_call(
        body, name=name, out_shape=[_sds(a.shape, a.dtype) for a in lands] + [_sds((8, 128), F32)],
        in_specs=[ANY] * n, out_specs=[ANY] * n + [pl.BlockSpec(memory_space=pltpu.VMEM)],
        input_output_aliases={i: i for i in range(n)}, scratch_shapes=_dma_sems(3 * n, 1)[:2])(*lands)
    return list(outs[:n]), outs[n]


def rs_start(hs, lands, dep, name):
    def copies(s, d, place):
        x, y, c, chips = place
        return [(s[i].at[2 * px + py], d[i].at[2 * x + y], (px, py, c)) for j, (px, py) in enumerate(chips) for i in range(len(s))]

    return _split_start(hs, lands, copies, 3 * len(hs), dep, name)


def rs_wait(handle, after, name):
    def arrivals(s, d, place):
        x, y, c, chips = place
        return [(s[i].at[2 * px + py], d[i].at[2 * px + py]) for j, (px, py) in enumerate(chips) for i in range(len(s))]

    return _split_wait(handle, arrivals, after, name)


def sibling_swap(arrs, name):
    n = len(arrs)
    rh = [a.shape[1] // 2 for a in arrs]

    def body(*refs):
        srcs, outs = refs[:n], refs[n:2 * n]
        send_sems, recv_sems = refs[2 * n:]
        x, y, c, _ = _place()
        cps = [_rcopy(srcs[i].at[:, pl.ds((1 - c) * rh[i], rh[i]), :], outs[i], (send_sems, recv_sems), i, (x, y, 1 - c))
               for i in range(n)]
        for cp in cps:
            cp.start()
        for cp in cps:
            cp.wait()

    return pl.pallas_call(
        body, name=name, out_shape=[_sds((N_CHIPS, r, a.shape[2]), a.dtype) for a, r in zip(arrs, rh)],
        in_specs=[ANY] * n, out_specs=[ANY] * n, scratch_shapes=_dma_sems(n, 1)[:2])(*arrs)


def sum_into(land, base, l, core, name):
    _, rh, C = land.shape
    tr = _row_tile(rh, C, N_CHIPS, mult=16)
    nr = rh // tr

    def body(core_ref, land_ref, base_ref, o_ref):
        acc = land_ref[0].astype(F32)
        for k in range(1, N_CHIPS):
            acc = acc + land_ref[k].astype(F32)
        o_ref[...] = acc

    grid_spec = pltpu.PrefetchScalarGridSpec(
        num_scalar_prefetch=1, grid=(nr,),
        in_specs=[pl.BlockSpec((N_CHIPS, tr, C), lambda r, core_ref: (0, r, 0)), ANY],
        out_specs=pl.BlockSpec((None, tr, C), lambda r, core_ref: (l, core_ref[0] * nr + r, 0)))
    return pl.pallas_call(body, name=name, grid_spec=grid_spec, out_shape=_sds(base.shape, base.dtype),
                          input_output_aliases={2: 0}, compiler_params=_cparams(VMEM_BIG))(
        core.reshape(1).astype(jnp.int32), land, base)


def sibling_join(bases, name):
    n = len(bases)

    def body(*refs):
        bufs = refs[n:2 * n]
        send_sems, recv_sems = refs[2 * n:]
        x, y, c, _ = _place()
        sems = (send_sems, recv_sems)

        def half(i, hc):
            rh = bufs[i].shape[1] // 2
            return bufs[i].at[:, pl.ds(hc * rh, rh), :]

        sends = [_rcopy(half(i, c), half(i, c), sems, i, (x, y, 1 - c)) for i in range(n)]
        for cp in sends:
            cp.start()
        for i in range(n):
            _rcopy(half(i, 1 - c), half(i, 1 - c), sems, i, (x, y, 1 - c)).wait_recv()
        for cp in sends:
            cp.wait_send()

    return pl.pallas_call(
        body, name=name, out_shape=[_sds(b.shape, b.dtype) for b in bases], in_specs=[ANY] * n, out_specs=[ANY] * n,
        input_output_aliases={i: i for i in range(n)}, scratch_shapes=_dma_sems(n, 1)[:2])(*bases)


def ag_all(blk):
    M, C = blk.shape

    def body(x_ref, out_ref, send_sems, recv_sems, loc_sem):
        x, y, c, chips = _place()
        sems = (send_sems, recv_sems)
        me, sibling = (x, y, c), (x, y, 1 - c)

        def slot(px, py, pc):
            return out_ref.at[4 * px + 2 * py + pc]

        mine = pltpu.make_async_copy(x_ref, slot(*me), loc_sem)
        mine.start()
        first = [_rcopy(x_ref, slot(*me), sems, 0, sibling)]
        first += [_rcopy(x_ref, slot(*me), sems, 1 + j, (*chip, c)) for j, chip in enumerate(chips)]
        for cp in first:
            cp.start()
        passed = [_rcopy(slot(*chip, c), slot(*chip, c), sems, 4 + j, sibling) for j, chip in enumerate(chips)]
        for j, chip in enumerate(chips):
            _rcopy(slot(*chip, c), slot(*chip, c), sems, 1 + j, me).wait_recv()
            passed[j].start()
        _rcopy(slot(*sibling), slot(*sibling), sems, 0, me).wait_recv()
        for j, chip in enumerate(chips):
            _rcopy(slot(*chip, 1 - c), slot(*chip, 1 - c), sems, 4 + j, me).wait_recv()
        for cp in first + passed:
            cp.wait_send()
        mine.wait()

    return pl.pallas_call(
        body, name="ag_all", out_shape=_sds((8, M, C), blk.dtype),
        in_specs=[pl.BlockSpec(memory_space=pltpu.VMEM)], out_specs=pl.BlockSpec(memory_space=pltpu.VMEM),
        scratch_shapes=[pltpu.SemaphoreType.DMA((7,)), pltpu.SemaphoreType.DMA((7,)), pltpu.SemaphoreType.DMA(())],
        compiler_params=_cparams(VMEM_BIG))(blk)


WEIGHTS = ["ada_w", "ada_b", "ln_g", "ln_b", "ffn1_w_in", "ffn1_w_out", "ffn2_w_in", "ffn2_w_out", "mix_w_in", "mix_w_out",
           "hgrn_lb_logits", "hgrn_norm_g", "mla_q_norm_g", "mla_kv_norm_g", "mla_w_uq", "mla_w_ukv", "fox_b_f",
           "gmlp_ln_g", "gmlp_ln_b", "gmlp_w_s", "gmlp_b_s"]
SMALL = ["hgrn_lb_logits", "hgrn_norm_g", "mla_q_norm_g", "mla_kv_norm_g", "fox_b_f", "gmlp_ln_g", "gmlp_ln_b",
         "gmlp_w_s", "gmlp_b_s", "ln_g", "ln_b"]
GATHERED = ["ada_w", "ffn1_w_in", "ffn1_w_out", "ffn2_w_in", "ffn2_w_out", "mix_w_in", "mix_w_out", "mla_w_uq", "mla_w_ukv"]
REDUCED = GATHERED[1:]


def _col_shards(a):
    cols = a.shape[1] // N_CHIPS
    return jnp.stack([a[:, k * cols:(k + 1) * cols] for k in range(N_CHIPS)])


def add_kept_half(a, got, core, name):
    _, R, C = a.shape
    rh = R // 2
    tr = _row_tile(rh, C, mult=16)
    nr = rh // tr

    def body(core_ref, a_ref, b_ref, o_ref):
        o_ref[...] = (a_ref[...].astype(F32) + b_ref[...].astype(F32)).astype(o_ref.dtype)

    half = pl.BlockSpec((None, tr, C), lambda k, r, core_ref: (k, r, 0))
    grid_spec = pltpu.PrefetchScalarGridSpec(
        num_scalar_prefetch=1, grid=(N_CHIPS, nr),
        in_specs=[pl.BlockSpec((None, tr, C), lambda k, r, core_ref: (k, core_ref[0] * nr + r, 0)), half],
        out_specs=half)
    return pl.pallas_call(body, name=name, grid_spec=grid_spec, out_shape=_sds((N_CHIPS, rh, C), BF16),
                          compiler_params=_cparams(VMEM_BIG))(core.reshape(1).astype(jnp.int32), a, got)


def _rows(parts, n_rows, dtype):
    flat = jnp.concatenate([p.reshape(-1) for p in parts])
    pad = n_rows * D - flat.shape[0]
    return jnp.concatenate([flat, jnp.zeros((pad,), dtype)]).reshape(n_rows, D)


def _take(flat, shapes):
    out, o = [], 0
    for shp in shapes:
        n = int(np.prod(shp))
        out.append(flat[o:o + n].reshape(shp))
        o += n
    return out


def _round_up(n, m):
    return -(-n // m) * m


def pack_small(w):
    parts = [w[n][l] for l in range(DEPTH) for n in SMALL]
    n = sum(int(np.prod(p.shape)) for p in parts)
    return _rows(parts, _round_up(-(-n // D), 8), F32)


def unpack_small(pk, like):
    shapes = [like[n].shape[1:] for l in range(DEPTH) for n in SMALL]
    pieces = _take(pk.reshape(-1), shapes)
    names = [n for l in range(DEPTH) for n in SMALL]
    return {n: jnp.stack([p for p, m in zip(pieces, names) if m == n]) for n in SMALL}


def kernel(x, c, ada_w, ada_b, ln_g, ln_b, ffn1_w_in, ffn1_w_out, ffn2_w_in, ffn2_w_out, mix_w_in, mix_w_out, hgrn_lb_logits, hgrn_norm_g, mla_q_norm_g, mla_kv_norm_g, mla_w_uq, mla_w_ukv, fox_b_f, gmlp_ln_g, gmlp_ln_b, gmlp_w_s, gmlp_b_s, loss_target, m_ada_w, m_ada_b, m_ln_g, m_ln_b, m_ffn1_w_in, m_ffn1_w_out, m_ffn2_w_in, m_ffn2_w_out, m_mix_w_in, m_mix_w_out, m_hgrn_lb_logits, m_hgrn_norm_g, m_mla_q_norm_g, m_mla_kv_norm_g, m_mla_w_uq, m_mla_w_ukv, m_fox_b_f, m_gmlp_ln_g, m_gmlp_ln_b, m_gmlp_w_s, m_gmlp_b_s, v_ada_w, v_ada_b, v_ln_g, v_ln_b, v_ffn1_w_in, v_ffn1_w_out, v_ffn2_w_in, v_ffn2_w_out, v_mix_w_in, v_mix_w_out, v_hgrn_lb_logits, v_hgrn_norm_g, v_mla_q_norm_g, v_mla_kv_norm_g, v_mla_w_uq, v_mla_w_ukv, v_fox_b_f, v_gmlp_ln_g, v_gmlp_ln_b, v_gmlp_w_s, v_gmlp_b_s):
    w = dict(zip(WEIGHTS, (ada_w, ada_b, ln_g, ln_b, ffn1_w_in, ffn1_w_out, ffn2_w_in, ffn2_w_out, mix_w_in, mix_w_out, hgrn_lb_logits, hgrn_norm_g, mla_q_norm_g, mla_kv_norm_g, mla_w_uq, mla_w_ukv, fox_b_f, gmlp_ln_g, gmlp_ln_b, gmlp_w_s, gmlp_b_s)))
    m = dict(zip(WEIGHTS, (m_ada_w, m_ada_b, m_ln_g, m_ln_b, m_ffn1_w_in, m_ffn1_w_out, m_ffn2_w_in, m_ffn2_w_out, m_mix_w_in, m_mix_w_out, m_hgrn_lb_logits, m_hgrn_norm_g, m_mla_q_norm_g, m_mla_kv_norm_g, m_mla_w_uq, m_mla_w_ukv, m_fox_b_f, m_gmlp_ln_g, m_gmlp_ln_b, m_gmlp_w_s, m_gmlp_b_s)))
    v = dict(zip(WEIGHTS, (v_ada_w, v_ada_b, v_ln_g, v_ln_b, v_ffn1_w_in, v_ffn1_w_out, v_ffn2_w_in, v_ffn2_w_out, v_mix_w_in, v_mix_w_out, v_hgrn_lb_logits, v_hgrn_norm_g, v_mla_q_norm_g, v_mla_kv_norm_g, v_mla_w_uq, v_mla_w_ukv, v_fox_b_f, v_gmlp_ln_g, v_gmlp_ln_b, v_gmlp_w_s, v_gmlp_b_s)))
    Bl, S, _ = x.shape
    T = Bl * S
    core = lax.axis_index("c")
    chip = 2 * lax.axis_index("x") + lax.axis_index("y")

    def shard(key):
        n, l = key
        if n == "ln":
            return jnp.concatenate([ln_g[l:l + 1], ln_b[l:l + 1], jnp.zeros((1, 2, D // N_CHIPS), F32)], axis=1)
        return w[n][l:l + 1].astype(BF16)

    mixers = ["mix_w_in", "mix_w_out", "mla_w_uq", "mla_w_ukv"]
    groups = [[("ada_w", 0), ("ffn1_w_in", 0), ("ffn1_w_out", 0), ("ln", 0)],
              [(n, 0) for n in mixers + ["ffn2_w_in", "ffn2_w_out"]],
              [(n, 1) for n in GATHERED + ["ln"]]]
    srcs = [[shard(k) for k in grp] for grp in groups]
    lands = [[own_slot(s, chip) for s in srcs[0]]]
    handle0 = ag_start(srcs[0], lands[0], jnp.zeros((8, 128), F32), "ag_start_0")
    chip_later = chip + handle0[-1][0, 0].astype(jnp.int32)
    lands += [[own_slot(s, chip_later) for s in grp] for grp in srcs[1:]]
    first, token = ag_forward(ag_wait(handle0, lands[2][0], "ag_wait_0")[1], "ag_forward_0")
    have = dict(zip(groups[0], first))
    handles = {}
    for gi in (1, 2):
        handles[gi] = ag_start(srcs[gi], lands[gi], token, "ag_start_%d" % gi)
        token = handles[gi][-1]
    c8 = jnp.concatenate([c, jnp.zeros((8 - Bl, D), F32)], axis=0)
    c8 = c8 + token[0, 0]

    def cat_cols(a):
        return jnp.concatenate([a[0, k] for k in range(N_CHIPS)], axis=1)

    def get(l, part, after):
        gi = 2 if l == 1 else (0 if part in ("ada", "ffn1") else 1)
        if gi in handles:
            arrived, _ = ag_forward(ag_wait(handles.pop(gi), after, "ag_wait_%d" % gi)[1], "ag_forward_%d" % gi)
            have.update(zip(groups[gi], arrived))
        if part == "ada":
            return dict(ada_w=have[("ada_w", l)], wl=0, ada_b=ada_b[l][None])
        if part == "ffn1":
            ln_full = jnp.moveaxis(have[("ln", l)][0], 0, 1).reshape(8, D)
            return dict(ffn1_in=have[("ffn1_w_in", l)], ffn1_out=have[("ffn1_w_out", l)], wl=0,
                        ln_g=ln_full[0:3], ln_b=ln_full[3:6])
        if part == "ffn2":
            return dict(ffn2_in=have[("ffn2_w_in", l)], ffn2_out=have[("ffn2_w_out", l)])
        return dict(
            mix_in=mix_in_ext(cat_cols(have[("mix_w_in", l)])), mix_out=mix_out_ext(have[("mix_w_out", l)].reshape(D, D)),
            wq=uq_ext(cat_cols(have[("mla_w_uq", l)])).astype(F32), wkv=ukv_ext(cat_cols(have[("mla_w_ukv", l)])).astype(F32),
            ng=hgrn_norm_g[l][None], gq=mla_q_norm_g[l][None], gkv=mla_kv_norm_g[l][None],
            bcol=jnp.concatenate([fox_b_f[l], jnp.zeros((8 - HEADS,), F32)])[:, None],
            g_lg=gmlp_ln_g[l][None], g_lb=gmlp_ln_b[l][None], ws=gmlp_w_s[l], bs=gmlp_b_s[l][:, :, None])

    pending = []

    def emit(l, part, g):
        if part == "mix":
            names = mixers
            by_chip = [_col_shards(mix_in_unext(g["mix_in"])), mix_out_unext(g["mix_out"]).reshape(N_CHIPS, D // N_CHIPS, D),
                       _col_shards(uq_unext(g["wq"])).astype(BF16), _col_shards(ukv_unext(g["wkv"])).astype(BF16)]
        else:
            names = [part + "_w_in", part + "_w_out"]
            by_chip = [g[part + "_in"], g[part + "_out"]]
        tag = "%d_%s" % (l, part)
        got = sibling_swap(by_chip, "sibling_swap_" + tag)
        chip_sum = [add_kept_half(a, r, core, "add_sibling") for a, r in zip(by_chip, got)]
        zones = [lax.dynamic_update_slice(lax.empty(h.shape, h.dtype), lax.dynamic_slice_in_dim(h, chip, 1, axis=0), (chip, 0, 0))
                 for h in chip_sum]
        handle = rs_start(chip_sum, zones, chip_sum[0], "rs_start_" + tag)
        pending.append((l, names, handle, tag))
        return handle[-1][0, 0]

    loss_tile, dx, dmods, grads, d_logits = local_step(
        x.reshape(T, D), c8, loss_target.reshape(T, D), get, hgrn_lb_logits, S, emit)
    loss = lax.psum(loss_tile[0, 0], ("x", "y", "c"))

    small_g = {"hgrn_lb_logits": d_logits,
               "hgrn_norm_g": jnp.stack([grads[l]["ng"][0] for l in range(DEPTH)]),
               "mla_q_norm_g": jnp.stack([grads[l]["gq"][0] for l in range(DEPTH)]),
               "mla_kv_norm_g": jnp.stack([grads[l]["gkv"][0] for l in range(DEPTH)]),
               "fox_b_f": jnp.stack([grads[l]["bcol"][0:HEADS, 0] for l in range(DEPTH)]),
               "gmlp_ln_g": jnp.stack([grads[l]["g_lg"][0] for l in range(DEPTH)]),
               "gmlp_ln_b": jnp.stack([grads[l]["g_lb"][0] for l in range(DEPTH)]),
               "gmlp_w_s": jnp.stack([grads[l]["ws"] for l in range(DEPTH)]),
               "gmlp_b_s": jnp.stack([grads[l]["bs"][:, :, 0] for l in range(DEPTH)])}
    small_g["ln_g"] = jnp.stack([grads[l]["ln_g"] for l in range(DEPTH)])
    small_g["ln_b"] = jnp.stack([grads[l]["ln_b"] for l in range(DEPTH)])
    pk_small = pack_small(small_g)
    n_small = pk_small.shape[0]
    extras = [dmods[l] for l in range(DEPTH)] + [c]
    n_extra = _round_up(-(-sum(int(np.prod(e.shape)) for e in extras) // D), 8)
    gathered = ag_all(jnp.concatenate([pk_small, _rows(extras, n_extra, F32)], axis=0))
    g_small = unpack_small(sum_slots(gathered[:, 0:n_small], 8, "sum_small"), small_g)
    ext = gathered[:, n_small:].reshape(8, -1)
    n_dmod = DEPTH * Bl * N_MOD * D
    dmod_all = ext[:, 0:n_dmod].reshape(8, DEPTH, Bl, N_MOD * D)
    c_all = ext[:, n_dmod:n_dmod + Bl * D].reshape(8 * Bl, D)
    g_ada_w, g_ada_b = [], []
    ncol = N_MOD * D // N_CHIPS
    for l in range(DEPTH):
        dm = dmod_all[:, l].reshape(8 * Bl, N_MOD * D)
        g_ada_w.append(ada_grad(c_all, lax.dynamic_slice_in_dim(dm, chip * ncol, ncol, axis=1)))
        g_ada_b.append(sum_slots(dm.reshape(8 * Bl, N_MOD, D), 8 * Bl, "sum_ada_b").reshape(N_MOD * D))
    g_ada_w, g_ada_b = jnp.stack(g_ada_w), jnp.stack(g_ada_b)

    red = {n: lax.empty(w[n].shape, F32) for n in REDUCED}

    def arrive(entry, after):
        l, names, handle, tag = entry
        for n, land in zip(names, rs_wait(handle, after, "rs_wait_" + tag)[1]):
            red[n] = sum_into(land, red[n], l, core, "sum_chips")

    for entry in pending[:-1]:
        arrive(entry, dx)
    late = pending[-1][1]
    early = [n for n in REDUCED if n not in late]
    grad = dict(zip(early, sibling_join([red[n] for n in early], "sibling_join_a")))
    grad.update(g_small)
    grad["ada_w"], grad["ada_b"] = g_ada_w, g_ada_b
    for n in ("ln_g", "ln_b"):
        grad[n] = lax.dynamic_slice_in_dim(g_small[n], chip * (D // N_CHIPS), D // N_CHIPS, axis=2)
    out = {"grad": grad, "delta": {}, "new_m": {}, "new_v": {}}

    def update(n):
        shp = w[n].shape
        two_d = (-1, shp[-1])
        res = adamw(w[n].reshape(two_d), grad[n].reshape(two_d), m[n].reshape(two_d), v[n].reshape(two_d), "adamw_" + n,
                    echo=n in REDUCED)
        grad[n] = (res[3] if n in REDUCED else grad[n]).reshape(shp)
        for key, r in zip(("delta", "new_m", "new_v"), res):
            out[key][n] = r.reshape(shp)

    for n in WEIGHTS:
        if n not in late:
            update(n)
    arrive(pending[-1], out["delta"]["ffn2_w_in"])
    grad.update(zip(late, sibling_join([red[n] for n in late], "sibling_join_b")))
    for n in late:
        update(n)
    outs = [loss, dx.reshape(Bl, S, D)]
    for key in ("grad", "delta", "new_m", "new_v"):
        outs += [out[key][n] for n in WEIGHTS]
    return tuple(outs)
```

```python
import functools

import jax
import jax.numpy as jnp
import numpy as np
from jax import lax
from jax.experimental import pallas as pl
from jax.experimental.pallas import tpu as pltpu

F32, BF16 = jnp.float32, jnp.bfloat16
MESH = pl.DeviceIdType.MESH

N_CHIPS = 4
D = 1024
DEPTH = 2
FF = 2816
N_MOD = 9
GW = 256
HEADS = 4
HD = 64
HP = 128
A_CHUNK = 16
LB_FLOOR = 1e-30
B_Q_LORA, B_KV_LORA, B_NOPE, B_ROPE = 256, 128, 64, 32
ROPE_THETA = 10000.0
D_CHUNK = 128
ALPHA = (2 * DEPTH) ** 0.25
LN_EPS = 1e-5
RMS_EPS = 1e-6
ADAM_LR, ADAM_B1, ADAM_B2, ADAM_EPS, ADAM_WD, ADAM_STEP = 0.001, 0.9, 0.999, 1e-08, 0.01, 10

NP = 3840
NP_TILE = 1920
C_A, C_B, C_CQ, C_CK, C_CV, C_D, C_CF = 0, 1024, 1536, 2048, 2560, 3072, 3584
NCAT = 1536
O_A, O_B, O_C, O_D = 0, 256, 768, 1280

VMEM_BIG = 48 << 20
VMEM_MOST = 58 << 20


def _cparams(vmem=None):
    return pltpu.CompilerParams(vmem_limit_bytes=vmem) if vmem else pltpu.CompilerParams()


def _sds(shape, dtype):
    return jax.ShapeDtypeStruct(tuple(shape), dtype)


@jax.custom_vjp
def _mm(a, w):
    return jnp.dot(a.astype(BF16), w.astype(BF16), preferred_element_type=F32)


def _mm_f(a, w):
    return _mm(a, w), (a, w)


def _mm_b(res, g):
    a, w = res
    gb = g.astype(BF16)
    da = lax.dot_general(gb, w.astype(BF16), (((1,), (1,)), ((), ())), preferred_element_type=F32)
    dw = lax.dot_general(a.astype(BF16), gb, (((0,), (0,)), ((), ())), preferred_element_type=F32)
    return da.astype(a.dtype), dw.astype(w.dtype)


_mm.defvjp(_mm_f, _mm_b)


@jax.custom_vjp
def _mm_nt(a, b):
    return lax.dot_general(a.astype(BF16), b.astype(BF16), (((1,), (1,)), ((), ())), preferred_element_type=F32)


def _mm_nt_f(a, b):
    return _mm_nt(a, b), (a, b)


def _mm_nt_b(res, g):
    a, b = res
    gb = g.astype(BF16)
    da = jnp.dot(gb, b.astype(BF16), preferred_element_type=F32)
    db = lax.dot_general(gb, a.astype(BF16), (((0,), (0,)), ((), ())), preferred_element_type=F32)
    return da.astype(a.dtype), db.astype(b.dtype)


_mm_nt.defvjp(_mm_nt_f, _mm_nt_b)


@jax.custom_vjp
def _mm_tn(a, b):
    return lax.dot_general(a.astype(BF16), b.astype(BF16), (((0,), (0,)), ((), ())), preferred_element_type=F32)


def _mm_tn_f(a, b):
    return _mm_tn(a, b), (a, b)


def _mm_tn_b(res, g):
    a, b = res
    gb = g.astype(BF16)
    da = lax.dot_general(b.astype(BF16), gb, (((1,), (1,)), ((), ())), preferred_element_type=F32)
    db = jnp.dot(a.astype(BF16), gb, preferred_element_type=F32)
    return da.astype(a.dtype), db.astype(b.dtype)


_mm_tn.defvjp(_mm_tn_f, _mm_tn_b)


def _split3(x):
    p1 = x.astype(BF16)
    r = x - p1.astype(F32)
    p2 = r.astype(BF16)
    return p1, p2, (r - p2.astype(F32)).astype(BF16)


@jax.custom_vjp
def _sel_r(x, sel):
    s = sel.astype(BF16)
    return sum(jnp.dot(p, s, preferred_element_type=F32) for p in _split3(x))


def _sel_r_f(x, sel):
    return _sel_r(x, sel), sel


def _sel_r_b(sel, g):
    s = sel.astype(BF16)
    dx = sum(lax.dot_general(p, s, (((1,), (1,)), ((), ())), preferred_element_type=F32) for p in _split3(g))
    return dx, jnp.zeros_like(sel)


_sel_r.defvjp(_sel_r_f, _sel_r_b)


@jax.custom_vjp
def _sel_l(sel, x):
    s = sel.astype(BF16)
    return sum(jnp.dot(s, p, preferred_element_type=F32) for p in _split3(x))


def _sel_l_f(sel, x):
    return _sel_l(sel, x), sel


def _sel_l_b(sel, g):
    s = sel.astype(BF16)
    dx = sum(lax.dot_general(s, p, (((0,), (0,)), ((), ())), preferred_element_type=F32) for p in _split3(g))
    return jnp.zeros_like(sel), dx


_sel_l.defvjp(_sel_l_f, _sel_l_b)


def _iota(shape, dim):
    return lax.broadcasted_iota(jnp.int32, shape, dim)


def _head_sum_mats():
    e = (_iota((GW, HP), 0) // HD == _iota((GW, HP), 1)).astype(F32)
    et = (_iota((HP, GW), 1) // HD == _iota((HP, GW), 0)).astype(F32)
    return e, et


def _modulate(x, mod_ref, sh, sc):
    return x * (1.0 + mod_ref[sc:sc + 1, :]) + mod_ref[sh:sh + 1, :]


def _ln_res(x, y, gate, lg, lb, gs):
    r = ALPHA * x + gs * (1.0 + gate) * y
    mu = jnp.mean(r, axis=-1, keepdims=True)
    var = jnp.mean(jnp.square(r - mu), axis=-1, keepdims=True)
    return (r - mu) * lax.rsqrt(var + LN_EPS) * lg + lb


def _rms(x, g):
    return x * lax.rsqrt(jnp.mean(x * x, axis=-1, keepdims=True) + RMS_EPS) * g


def _tile(n, pref):
    return pref if n % pref == 0 else n


def mod_fwd(c8, w, l, b):
    tn = w.shape[3]
    n = N_CHIPS * tn

    def body(c_ref, w_ref, b_ref, o_ref):
        h = jax.nn.silu(c_ref[...]).astype(BF16)
        o_ref[...] = jnp.dot(h, w_ref[...], preferred_element_type=F32) + b_ref[...]

    return pl.pallas_call(
        body, name="mod_fwd", grid=(N_CHIPS,),
        in_specs=[pl.BlockSpec((8, D), lambda j: (0, 0)), pl.BlockSpec((None, None, D, tn), lambda j: (l, j, 0, 0)),
                  pl.BlockSpec((1, tn), lambda j: (0, j))],
        out_specs=pl.BlockSpec((8, tn), lambda j: (0, j)), out_shape=_sds((8, n), F32),
        compiler_params=_cparams(VMEM_BIG))(c8, w, b)


def ffn_in_fwd(x, mod, w_in, l, sh, sc, S):
    T = x.shape[0]
    tm, tn = _tile(S, 1024), FF // 2
    tpb, nj = S // tm, 2

    def body(x_ref, mod_ref, wg_ref, wu_ref, zg_ref, zu_ref, act_ref, h_ref):
        @pl.when(pl.program_id(1) == 0)
        def _():
            h_ref[...] = _modulate(x_ref[...], mod_ref, sh, sc).astype(BF16)
        g = jnp.dot(h_ref[...], wg_ref[...], preferred_element_type=F32)
        u = jnp.dot(h_ref[...], wu_ref[...], preferred_element_type=F32)
        zg_ref[...] = g.astype(BF16)
        zu_ref[...] = u.astype(BF16)
        act_ref[...] = (jax.nn.silu(g) * u).astype(BF16)

    return pl.pallas_call(
        body, name="ffn_in_fwd", grid=(T // tm, nj),
        in_specs=[pl.BlockSpec((tm, D), lambda i, j: (i, 0)),
                  pl.BlockSpec((None, N_MOD, D), lambda i, j: (i // tpb, 0, 0)),
                  pl.BlockSpec((None, None, D, tn), lambda i, j: (l, j, 0, 0)),
                  pl.BlockSpec((None, None, D, tn), lambda i, j: (l, j + nj, 0, 0))],
        out_specs=[pl.BlockSpec((tm, tn), lambda i, j: (i, j))] * 3,
        out_shape=[_sds((T, FF), BF16)] * 3,
        scratch_shapes=[pltpu.VMEM((tm, D), BF16)],
        compiler_params=_cparams(VMEM_BIG))(x, mod, w_in, w_in)


def mix_in_fwd(x, mod, w, sh, sc, S):
    T = x.shape[0]
    n = w.shape[1]
    tm, tn = _tile(S, 1024), NP_TILE
    tpb = S // tm

    def body(x_ref, mod_ref, w_ref, o_ref, h_ref):
        @pl.when(pl.program_id(1) == 0)
        def _():
            h_ref[...] = _modulate(x_ref[...], mod_ref, sh, sc).astype(BF16)
        o_ref[...] = jnp.dot(h_ref[...], w_ref[...], preferred_element_type=F32)

    return pl.pallas_call(
        body, name="mix_in_fwd", grid=(T // tm, n // tn),
        in_specs=[pl.BlockSpec((tm, D), lambda i, j: (i, 0)),
                  pl.BlockSpec((None, N_MOD, D), lambda i, j: (i // tpb, 0, 0)),
                  pl.BlockSpec((D, tn), lambda i, j: (0, j))],
        out_specs=pl.BlockSpec((tm, tn), lambda i, j: (i, j)), out_shape=_sds((T, n), F32),
        scratch_shapes=[pltpu.VMEM((tm, D), BF16)],
        compiler_params=_cparams(VMEM_BIG))(x, mod, w)


def out_ln_fwd(act, w_out, x, mod, lg, lb, gate, gs, S, l=None):
    T, K = act.shape
    tm = _tile(S, 512)
    tpb = S // tm

    def body(a_ref, w_ref, x_ref, mod_ref, lg_ref, lb_ref, y_ref, xn_ref):
        y = jnp.dot(a_ref[...], w_ref[...].reshape(K, D), preferred_element_type=F32)
        y_ref[...] = y
        xn_ref[...] = _ln_res(x_ref[...], y, mod_ref[gate:gate + 1, :], lg_ref[...], lb_ref[...], gs)

    if l is None:
        w_spec = pl.BlockSpec((K, D), lambda i: (0, 0))
    else:
        w_spec = pl.BlockSpec((None, N_CHIPS, K // N_CHIPS, D), lambda i: (l, 0, 0, 0))
    return pl.pallas_call(
        body, name="out_ln_fwd", grid=(T // tm,),
        in_specs=[pl.BlockSpec((tm, K), lambda i: (i, 0)), w_spec,
                  pl.BlockSpec((tm, D), lambda i: (i, 0)),
                  pl.BlockSpec((None, N_MOD, D), lambda i: (i // tpb, 0, 0)),
                  pl.BlockSpec((1, D), lambda i: (0, 0)), pl.BlockSpec((1, D), lambda i: (0, 0))],
        out_specs=[pl.BlockSpec((tm, D), lambda i: (i, 0))] * 2,
        out_shape=[_sds((T, D), F32), _sds((T, D), F32)],
        compiler_params=_cparams(VMEM_BIG))(act, w_out, x, mod, lg, lb)


def ln_res_bwd(dxn, x, y, mod, lg, lb, gate, gs, S):
    T = x.shape[0]
    B = T // S
    tm = _tile(S, 512)
    tpb = S // tm

    def body(d_ref, x_ref, y_ref, mod_ref, lg_ref, lb_ref, dx_ref, dy_ref, dg_ref, dlg_ref, dlb_ref):
        i = pl.program_id(0)
        f = functools.partial(_ln_res, gs=gs)
        _, vjp = jax.vjp(f, x_ref[...], y_ref[...], mod_ref[gate:gate + 1, :], lg_ref[...], lb_ref[...])
        dx, dy, dg, dlg, dlb = vjp(d_ref[...])
        dx_ref[...] = dx
        dy_ref[...] = dy.astype(BF16)

        @pl.when(i % tpb == 0)
        def _():
            dg_ref[...] = jnp.zeros_like(dg_ref)

        @pl.when(i == 0)
        def _():
            dlg_ref[...] = jnp.zeros_like(dlg_ref)
            dlb_ref[...] = jnp.zeros_like(dlb_ref)

        dg_ref[...] += dg
        dlg_ref[...] += dlg
        dlb_ref[...] += dlb

    tok = pl.BlockSpec((tm, D), lambda i: (i, 0))
    vec = pl.BlockSpec((1, D), lambda i: (0, 0))
    return pl.pallas_call(
        body, name="ln_res_bwd", grid=(T // tm,),
        in_specs=[tok, tok, tok, pl.BlockSpec((None, N_MOD, D), lambda i: (i // tpb, 0, 0)), vec, vec],
        out_specs=[tok, tok, pl.BlockSpec((None, 1, D), lambda i: (i // tpb, 0, 0)), vec, vec],
        out_shape=[_sds((T, D), F32), _sds((T, D), BF16), _sds((B, 1, D), F32), _sds((1, D), F32), _sds((1, D), F32)],
        compiler_params=_cparams(VMEM_BIG))(dxn, x, y, mod, lg, lb)


def swiglu_bwd(dy, w_out, l, zg, zu, S):
    T = dy.shape[0]
    tm, tn = _tile(S, 1024), FF // 2

    def body(dy_ref, w_ref, zg_ref, zu_ref, dg_ref, du_ref):
        da = lax.dot_general(dy_ref[...], w_ref[...].reshape(tn, D), (((1,), (1,)), ((), ())), preferred_element_type=F32)
        g, u = zg_ref[...].astype(F32), zu_ref[...].astype(F32)
        sg = jax.nn.sigmoid(g)
        dg_ref[...] = (da * u * (sg * (1.0 + g * (1.0 - sg)))).astype(BF16)
        du_ref[...] = (da * (g * sg)).astype(BF16)

    zt = pl.BlockSpec((tm, tn), lambda i, j: (i, j))
    return pl.pallas_call(
        body, name="swiglu_bwd", grid=(T // tm, FF // tn),
        in_specs=[pl.BlockSpec((tm, D), lambda i, j: (i, 0)),
                  pl.BlockSpec((None, 2, FF // N_CHIPS, D), lambda i, j: (l, j, 0, 0)), zt, zt],
        out_specs=[zt, zt], out_shape=[_sds((T, FF), BF16), _sds((T, FF), BF16)],
        compiler_params=_cparams(VMEM_BIG))(dy, w_out, zg, zu)


def nt_plain(dy, w):
    T = dy.shape[0]
    K = w.shape[0]
    tm = _tile(T, 1024)

    def body(dy_ref, w_ref, o_ref):
        o_ref[...] = lax.dot_general(dy_ref[...], w_ref[...], (((1,), (1,)), ((), ())), preferred_element_type=F32)

    return pl.pallas_call(
        body, name="nt_plain", grid=(T // tm,),
        in_specs=[pl.BlockSpec((tm, D), lambda i: (i, 0)), pl.BlockSpec((K, D), lambda i: (0, 0))],
        out_specs=pl.BlockSpec((tm, K), lambda i: (i, 0)), out_shape=_sds((T, K), F32),
        compiler_params=_cparams(VMEM_BIG))(dy, w)


def _tn_step(acc, o_ref, lhs, rhs, t, nt):
    part = lax.dot_general(lhs, rhs, (((0,), (0,)), ((), ())), preferred_element_type=F32)
    if nt == 1:
        o_ref[...] = part.astype(o_ref.dtype)
        return

    @pl.when(t == 0)
    def _():
        acc[...] = part

    @pl.when((t > 0) & (t < nt - 1))
    def _():
        acc[...] += part

    @pl.when(t == nt - 1)
    def _():
        o_ref[...] = (acc[...] + part).astype(o_ref.dtype)


def tn_mm(a, b, tk):
    T, K = a.shape
    N = b.shape[1]
    tt = _tile(T, 1024)
    nt = T // tt

    def body(a_ref, b_ref, o_ref, acc):
        _tn_step(acc, o_ref, a_ref[...], b_ref[...], pl.program_id(1), nt)

    return pl.pallas_call(
        body, name="tn_mm", grid=(K // tk, nt),
        in_specs=[pl.BlockSpec((tt, tk), lambda k, t: (t, k)), pl.BlockSpec((tt, N), lambda k, t: (t, 0))],
        out_specs=pl.BlockSpec((tk, N), lambda k, t: (k, 0)), out_shape=_sds((K, N), BF16),
        scratch_shapes=[pltpu.VMEM((tk, N), F32)], compiler_params=_cparams(VMEM_BIG))(a, b)


def tn_mm_mod(x, mod, b, sh, sc, S, tn):
    T = x.shape[0]
    N = b.shape[1]
    tt = _tile(S, 1024)
    tpb = S // tt
    nt = T // tt

    def body(x_ref, mod_ref, b_ref, o_ref, acc):
        h = _modulate(x_ref[...], mod_ref, sh, sc).astype(BF16)
        _tn_step(acc, o_ref, h, b_ref[...], pl.program_id(1), nt)

    return pl.pallas_call(
        body, name="tn_mm_mod", grid=(N // tn, nt),
        in_specs=[pl.BlockSpec((tt, D), lambda j, t: (t, 0)),
                  pl.BlockSpec((None, N_MOD, D), lambda j, t: (t // tpb, 0, 0)),
                  pl.BlockSpec((tt, tn), lambda j, t: (t, j))],
        out_specs=pl.BlockSpec((D, tn), lambda j, t: (0, j)), out_shape=_sds((D, N), BF16),
        scratch_shapes=[pltpu.VMEM((D, tn), F32)], compiler_params=_cparams(VMEM_BIG))(x, mod, b)


def tn_mm_mod_shards(x, mod, bg, bu, sh, sc, S):
    T = x.shape[0]
    tn = FF // 2
    tt = _tile(S, 1024)
    tpb = S // tt
    nt = T // tt

    def body(x_ref, mod_ref, bg_ref, bu_ref, o_ref, acc):
        j, t = pl.program_id(0), pl.program_id(1)
        h = _modulate(x_ref[...], mod_ref, sh, sc).astype(BF16)

        @pl.when(j < 2)
        def _():
            _tn_step(acc, o_ref, h, bg_ref[...], t, nt)

        @pl.when(j >= 2)
        def _():
            _tn_step(acc, o_ref, h, bu_ref[...], t, nt)

    return pl.pallas_call(
        body, name="tn_mm_mod_shards", grid=(N_CHIPS, nt),
        in_specs=[pl.BlockSpec((tt, D), lambda j, t: (t, 0)),
                  pl.BlockSpec((None, N_MOD, D), lambda j, t: (t // tpb, 0, 0)),
                  pl.BlockSpec((tt, tn), lambda j, t: (jnp.where(j < 2, t, 0), jnp.minimum(j, 1))),
                  pl.BlockSpec((tt, tn), lambda j, t: (jnp.where(j < 2, 0, t), jnp.maximum(j - 2, 0)))],
        out_specs=pl.BlockSpec((None, D, tn), lambda j, t: (j, 0, 0)), out_shape=_sds((N_CHIPS, D, tn), BF16),
        scratch_shapes=[pltpu.VMEM((D, tn), F32)], compiler_params=_cparams(VMEM_BIG))(x, mod, bg, bu)


def nt_mod_bwd(ds, w, offs, x, mod, dres, sc, S, tk, l=None):
    T = x.shape[0]
    B = T // S
    tm = _tile(S, 1024)
    tpb = S // tm
    Kd = ds[0].shape[1]
    nk = Kd // tk
    n_in = len(ds)

    def body(*refs):
        d_refs, w_refs = refs[:n_in], refs[n_in:2 * n_in]
        x_ref, mod_ref, r_ref, dx_ref, dsh_ref, dsc_ref, acc = refs[2 * n_in:]
        i, k = pl.program_id(0), pl.program_id(1)

        part = sum(lax.dot_general(d_ref[...], w_ref[...], (((1,), (1,)), ((), ())), preferred_element_type=F32)
                   for d_ref, w_ref in zip(d_refs, w_refs))

        @pl.when(k == 0)
        def _():
            acc[...] = part

        @pl.when(k > 0)
        def _():
            acc[...] += part

        @pl.when(k == nk - 1)
        def _():
            dh = acc[...]
            dx_ref[...] = dh * (1.0 + mod_ref[sc:sc + 1, :]) + r_ref[...]

            @pl.when(i % tpb == 0)
            def _():
                dsh_ref[...] = jnp.zeros_like(dsh_ref)
                dsc_ref[...] = jnp.zeros_like(dsc_ref)

            dsh_ref[...] += jnp.sum(dh, axis=0, keepdims=True)
            dsc_ref[...] += jnp.sum(dh * x_ref[...], axis=0, keepdims=True)

    tok = pl.BlockSpec((tm, D), lambda i, k: (i, 0))
    vec = pl.BlockSpec((None, 1, D), lambda i, k: (i // tpb, 0, 0))
    in_specs = [pl.BlockSpec((tm, tk), lambda i, k: (i, k)) for _ in ds]
    if l is None:
        in_specs += [pl.BlockSpec((D, tk), functools.partial(lambda i, k, o: (0, k + o), o=off // tk)) for off in offs]
    else:
        in_specs += [pl.BlockSpec((None, None, D, tk), functools.partial(lambda i, k, o: (l, k + o, 0, 0), o=off)) for off in offs]
    in_specs += [tok, pl.BlockSpec((None, N_MOD, D), lambda i, k: (i // tpb, 0, 0)), tok]
    return pl.pallas_call(
        body, name="nt_mod_bwd", grid=(T // tm, nk), in_specs=in_specs,
        out_specs=[tok, vec, vec],
        out_shape=[_sds((T, D), F32), _sds((B, 1, D), F32), _sds((B, 1, D), F32)],
        scratch_shapes=[pltpu.VMEM((tm, D), F32)],
        compiler_params=_cparams(VMEM_MOST))(*ds, *([w] * n_in), x, mod, dres)


def loss_head(y, tgt):
    T = y.shape[0]
    tm = _tile(T, 512)

    def body(y_ref, t_ref, l_ref, d_ref):
        @pl.when(pl.program_id(0) == 0)
        def _():
            l_ref[...] = jnp.zeros_like(l_ref)
        e = y_ref[...] - t_ref[...]
        d_ref[...] = e * (1.0 / D)
        l_ref[...] += 0.5 * jnp.sum(jnp.sum(e * e, axis=1, keepdims=True) * (1.0 / D))

    tok = pl.BlockSpec((tm, D), lambda i: (i, 0))
    return pl.pallas_call(
        body, name="loss_head", grid=(T // tm,), in_specs=[tok, tok],
        out_specs=[pl.BlockSpec((8, 128), lambda i: (0, 0)), tok],
        out_shape=[_sds((8, 128), F32), _sds((T, D), F32)],
        compiler_params=_cparams(VMEM_BIG))(y, tgt)


def _hgrn_block(q, fz, inp, go, st, lb, ng, blk):
    nc = blk // A_CHUNK
    lb_eff = jnp.maximum(lb, LB_FLOOR)
    log_f = jnp.logaddexp(jnp.log(lb_eff), jnp.log1p(-lb) + jax.nn.log_sigmoid(fz))
    k = (1.0 - lb) * jax.nn.sigmoid(-fz) - (lb_eff - lb)
    qf = jax.nn.silu(q)
    same_chunk = _iota((blk, blk), 0) // A_CHUNK == _iota((blk, blk), 1) // A_CHUNK
    tril = (same_chunk & (_iota((blk, blk), 1) <= _iota((blk, blk), 0))).astype(F32)
    G = _sel_l(tril, log_f)
    e_mat, et_mat = _head_sum_mats()
    G4, q4, k4, v4 = (z.reshape(nc, A_CHUNK, GW) for z in (G, qf, k, inp))
    shp = (nc, A_CHUNK, A_CHUNK, GW)
    one = (1, A_CHUNK, A_CHUNK, GW)
    mask = jnp.where(_iota(one, 2) <= _iota(one, 1), 0.0, -jnp.inf)
    decay = jnp.exp((G4[:, :, None, :] - G4[:, None, :, :]) + mask)
    prod = q4[:, :, None, :] * k4[:, None, :, :] * decay
    scores = _mm(prod.reshape(nc * A_CHUNK * A_CHUNK, GW), e_mat.astype(BF16))
    spread = _mm(scores, et_mat.astype(BF16)).reshape(shp)
    o_intra = jnp.sum(spread * v4[:, None, :, :], axis=2).reshape(blk, GW)
    head_diag = (_iota((GW, GW), 0) // HD == _iota((GW, GW), 1) // HD).astype(F32)
    g_last = [jnp.sum(log_f[c * A_CHUNK:(c + 1) * A_CHUNK], axis=0, keepdims=True) for c in range(nc)]
    g_last_b = jnp.concatenate([jnp.broadcast_to(g, (A_CHUNK, GW)) for g in g_last], axis=0)
    q_dec = qf * jnp.exp(G)
    k_end = k * jnp.exp(g_last_b - G)
    outs = []
    for c in range(nc):
        rows = slice(c * A_CHUNK, (c + 1) * A_CHUNK)
        outs.append(_mm_nt(q_dec[rows], st))
        st = st * jnp.exp(g_last[c]) + _mm_tn(inp[rows], k_end[rows]) * head_diag
    o = o_intra + jnp.concatenate(outs, axis=0)
    ms = _sel_r(o * o, e_mat) * (1.0 / HD)
    o = o * _sel_r(lax.rsqrt(ms + RMS_EPS), et_mat) * ng
    return o * jax.nn.silu(go), st


HGRN_BLK = 128


def hgrn_fwd(proj, lb, ng, S):
    T = proj.shape[0]
    B = T // S
    blk = min(HGRN_BLK, S)
    nb = S // blk

    def body(p_ref, lb_ref, ng_ref, o_ref, st_out_ref, st_ref):
        @pl.when(pl.program_id(1) == 0)
        def _():
            st_ref[...] = jnp.zeros_like(st_ref)
        st_out_ref[...] = st_ref[...]
        p = p_ref[...]
        o, st = _hgrn_block(p[:, 0:GW], p[:, GW:2 * GW], p[:, 2 * GW:3 * GW], p[:, 3 * GW:4 * GW],
                            st_ref[...], lb_ref[...], ng_ref[...], blk)
        o_ref[...] = o.astype(BF16)
        st_ref[...] = st

    vec = pl.BlockSpec((1, GW), lambda b, j: (0, 0))
    return pl.pallas_call(
        body, name="hgrn_fwd", grid=(B, nb),
        in_specs=[pl.BlockSpec((blk, 4 * GW), lambda b, j: (b * nb + j, C_A // (4 * GW))), vec, vec],
        out_specs=[pl.BlockSpec((blk, GW), lambda b, j: (b * nb + j, 0)),
                   pl.BlockSpec((None, GW, GW), lambda b, j: (b * nb + j, 0, 0))],
        out_shape=[_sds((T, GW), BF16), _sds((B * nb, GW, GW), F32)],
        scratch_shapes=[pltpu.VMEM((GW, GW), F32)],
        compiler_params=_cparams(VMEM_BIG))(proj, lb, ng)


def hgrn_bwd(proj, states, dcat, lb, ng, S):
    T = proj.shape[0]
    B = T // S
    blk = min(HGRN_BLK, S)
    nb = S // blk

    def body(p_ref, st_in_ref, do_ref, lb_ref, ng_ref, dp_ref, dlb_ref, dng_ref, dst_ref):
        b, j = pl.program_id(0), pl.program_id(1)

        @pl.when(j == 0)
        def _():
            dst_ref[...] = jnp.zeros_like(dst_ref)

        @pl.when((b == 0) & (j == 0))
        def _():
            dlb_ref[...] = jnp.zeros_like(dlb_ref)
            dng_ref[...] = jnp.zeros_like(dng_ref)

        p = p_ref[...]
        f = functools.partial(_hgrn_block, blk=blk)
        _, vjp = jax.vjp(f, p[:, 0:GW], p[:, GW:2 * GW], p[:, 2 * GW:3 * GW], p[:, 3 * GW:4 * GW],
                         st_in_ref[...], lb_ref[...], ng_ref[...])
        dq, df, di, dg, dst, dlb, dng = vjp((do_ref[...], dst_ref[...]))
        dp_ref[...] = jnp.concatenate([dq, df, di, dg], axis=1).astype(BF16)
        dst_ref[...] = dst
        dlb_ref[...] += dlb
        dng_ref[...] += dng

    def rev(b, j):
        return b * nb + (nb - 1 - j)

    vec = pl.BlockSpec((1, GW), lambda b, j: (0, 0))
    return pl.pallas_call(
        body, name="hgrn_bwd", grid=(B, nb),
        in_specs=[pl.BlockSpec((blk, 4 * GW), lambda b, j: (rev(b, j), C_A // (4 * GW))),
                  pl.BlockSpec((None, GW, GW), lambda b, j: (rev(b, j), 0, 0)),
                  pl.BlockSpec((blk, GW), lambda b, j: (rev(b, j), O_A // GW)), vec, vec],
        out_specs=[pl.BlockSpec((blk, 4 * GW), lambda b, j: (rev(b, j), 0)), vec, vec],
        out_shape=[_sds((T, 4 * GW), BF16), _sds((1, GW), F32), _sds((1, GW), F32)],
        scratch_shapes=[pltpu.VMEM((GW, GW), F32)],
        compiler_params=_cparams(VMEM_BIG))(proj, states, dcat, lb, ng)


ATT_TQ = 256


ATT_BANDS = 8


def _attn_block(q, k, v, cum, qpos0, scale, use_cum, n_free):
    s = _mm_nt(q, k) * scale
    if use_cum:
        s = s - cum
    band = s[:, n_free:]
    visible = _iota(band.shape, 1) <= (qpos0 - n_free) + _iota(band.shape, 0)
    band = jnp.where(visible, band, -jnp.inf)
    m = jnp.max(band, axis=-1, keepdims=True)
    if n_free:
        free = s[:, :n_free]
        m = jnp.maximum(m, jnp.max(free, axis=-1, keepdims=True))
    e = jnp.exp(band - m)
    denom = jnp.sum(e, axis=-1, keepdims=True)
    o = _mm(e, v[n_free:])
    if n_free:
        e = jnp.exp(free - m)
        denom = denom + jnp.sum(e, axis=-1, keepdims=True)
        o = o + _mm(e, v[:n_free])
    return o * (1.0 / denom)


def _bands(S, tq):
    nq = S // tq
    nb = min(ATT_BANDS, nq)
    per = nq // nb
    return [(r * per, (r + 1) * per, (r + 1) * per * tq) for r in range(nb)]


def attn_fwd(qa, qo, ka, ko, va, vo, cum, scale, S):
    T = qa.shape[0]
    B = T // S
    tq = min(ATT_TQ, S)
    nq = S // tq
    use_cum = cum is not None

    def body(*refs):
        if use_cum:
            q_ref, k_ref, v_ref, c_ref, o_ref = refs
        else:
            (q_ref, k_ref, v_ref, o_ref), c_ref = refs, None
        h, i = pl.program_id(1), pl.program_id(2)
        for lo, hi, kw in _bands(S, tq):
            @pl.when((i >= lo) & (i < hi))
            def _():
                crow = c_ref[pl.ds(h, 1), 0:kw] if use_cum else None
                o = _attn_block(q_ref[...], k_ref[0:kw, :], v_ref[0:kw, :], crow, i * tq, scale, use_cum, lo * tq)
                o_ref[...] = o.astype(BF16)

    in_specs = [pl.BlockSpec((tq, HP), lambda b, h, i: (b * nq + i, qo + h)),
                pl.BlockSpec((S, HP), lambda b, h, i: (b, ko + h)),
                pl.BlockSpec((S, HP), lambda b, h, i: (b, vo + h))]
    args = [qa, ka, va]
    if use_cum:
        in_specs.append(pl.BlockSpec((None, 8, S), lambda b, h, i: (b, 0, 0)))
        args.append(cum)
    return pl.pallas_call(
        body, name="attn_fwd", grid=(B, HEADS, nq), in_specs=in_specs,
        out_specs=pl.BlockSpec((tq, HP), lambda b, h, i: (b * nq + i, h)),
        out_shape=_sds((T, HEADS * HP), BF16),
        compiler_params=_cparams(VMEM_BIG))(*args)


def _attn_block_bwd(q, k, v, cum, do, qpos0, scale, use_cum, n_free):
    tn = (((0,), (0,)), ((), ()))
    nt = (((1,), (1,)), ((), ()))
    qb, dob = q.astype(BF16), do.astype(BF16)
    kb, vb = k.astype(BF16), v.astype(BF16)
    s = lax.dot_general(qb, kb, nt, preferred_element_type=F32) * scale
    if use_cum:
        s = s - cum
    band = s[:, n_free:]
    visible = _iota(band.shape, 1) <= (qpos0 - n_free) + _iota(band.shape, 0)
    parts = [(jnp.where(visible, band, -jnp.inf), n_free, s.shape[1])]
    if n_free:
        parts.append((s[:, :n_free], 0, n_free))
    m = functools.reduce(jnp.maximum, [jnp.max(sp, axis=-1, keepdims=True) for sp, _, _ in parts])
    es = [jnp.exp(sp - m) for sp, _, _ in parts]
    rinv = 1.0 / sum(jnp.sum(e, axis=-1, keepdims=True) for e in es)
    ps = [e * rinv for e in es]
    dps = [lax.dot_general(dob, vb[a:b], nt, preferred_element_type=F32) for _, a, b in parts]
    delta = sum(jnp.sum(p * dp, axis=-1, keepdims=True) for p, dp in zip(ps, dps))
    dq = jnp.zeros(q.shape, F32)
    out = []
    for p, dp, (_, a, b) in zip(ps, dps, parts):
        ds = p * (dp - delta)
        dsb = ds.astype(BF16)
        dq = dq + jnp.dot(dsb, kb[a:b], preferred_element_type=F32)
        out.append((a, b, lax.dot_general(dsb, qb, tn, preferred_element_type=F32) * scale,
                    lax.dot_general(p.astype(BF16), dob, tn, preferred_element_type=F32),
                    -jnp.sum(ds, axis=0, keepdims=True) if use_cum else None))
    return dq * scale, out


def attn_bwd(qa, qo, ka, ko, va, vo, cum, dcat, do_off, scale, S, out_dtype):
    T = qa.shape[0]
    B = T // S
    tq = min(ATT_TQ, S)
    nq = S // tq
    use_cum = cum is not None

    def body(*refs):
        if use_cum:
            q_ref, k_ref, v_ref, do_ref, c_ref, dq_ref, dk_ref, dv_ref, dc_ref, dk_acc, dv_acc = refs
        else:
            q_ref, k_ref, v_ref, do_ref, dq_ref, dk_ref, dv_ref, dk_acc, dv_acc = refs
        h, i = pl.program_id(1), pl.program_id(2)

        @pl.when(i == 0)
        def _():
            dk_acc[...] = jnp.zeros_like(dk_acc)
            dv_acc[...] = jnp.zeros_like(dv_acc)
            if use_cum:
                dc_ref[...] = jnp.zeros_like(dc_ref)

        for lo, hi, kw in _bands(S, tq):
            @pl.when((i >= lo) & (i < hi))
            def _():
                crow = c_ref[pl.ds(h, 1), 0:kw] if use_cum else None
                dq, pieces = _attn_block_bwd(q_ref[...], k_ref[0:kw, :], v_ref[0:kw, :], crow, do_ref[...], i * tq,
                                             scale, use_cum, lo * tq)
                dq_ref[...] = dq.astype(out_dtype)
                for a, b, dk, dv, dc in pieces:
                    dk_acc[a:b, :] += dk
                    dv_acc[a:b, :] += dv
                    if use_cum:
                        dc_ref[:, a:b] += dc

        @pl.when(i == nq - 1)
        def _():
            dk_ref[...] = dk_acc[...].astype(out_dtype)
            dv_ref[...] = dv_acc[...].astype(out_dtype)

    qspec = pl.BlockSpec((tq, HP), lambda b, h, i: (b * nq + i, qo + h))
    in_specs = [qspec, pl.BlockSpec((S, HP), lambda b, h, i: (b, ko + h)),
                pl.BlockSpec((S, HP), lambda b, h, i: (b, vo + h)),
                pl.BlockSpec((tq, HP), lambda b, h, i: (b * nq + i, do_off + h))]
    args = [qa, ka, va, dcat]
    kv_out = pl.BlockSpec((S, HP), lambda b, h, i: (b, h))
    out_specs = [pl.BlockSpec((tq, HP), lambda b, h, i: (b * nq + i, h)), kv_out, kv_out]
    out_shape = [_sds((T, HEADS * HP), out_dtype)] * 3
    if use_cum:
        in_specs.append(pl.BlockSpec((None, 8, S), lambda b, h, i: (b, 0, 0)))
        args.append(cum)
        out_specs.append(pl.BlockSpec((None, 1, S), lambda b, h, i: (b * HEADS + h, 0, 0)))
        out_shape.append(_sds((B * HEADS, 1, S), F32))
    return pl.pallas_call(
        body, name="attn_bwd", grid=(B, HEADS, nq), in_specs=in_specs, out_specs=out_specs, out_shape=out_shape,
        scratch_shapes=[pltpu.VMEM((S, HP), F32), pltpu.VMEM((S, HP), F32)],
        compiler_params=_cparams(VMEM_BIG))(*args)


def _tri(n, upper):
    r, c = _iota((n, n), 0), _iota((n, n), 1)
    return ((r <= c) if upper else (r >= c)).astype(F32)


def fox_gate_fwd(proj, bcol, S):
    T = proj.shape[0]
    B = T // S
    ts = _tile(S, 512)
    nt = S // ts

    def body(p_ref, b_ref, o_ref, carry):
        @pl.when(pl.program_id(1) == 0)
        def _():
            carry[...] = jnp.zeros_like(carry)
        cf = jnp.transpose(p_ref[...])[0:8, :]
        lf = jax.nn.log_sigmoid(cf + b_ref[...])
        cum = _sel_r(lf, _tri(ts, True)) + carry[...]
        o_ref[...] = cum
        carry[...] += jnp.sum(lf, axis=1, keepdims=True)

    return pl.pallas_call(
        body, name="fox_gate_fwd", grid=(B, nt),
        in_specs=[pl.BlockSpec((ts, HP), lambda b, j: (b * nt + j, C_CF // HP)), pl.BlockSpec((8, 1), lambda b, j: (0, 0))],
        out_specs=pl.BlockSpec((None, 8, ts), lambda b, j: (b, 0, j)), out_shape=_sds((B, 8, S), F32),
        scratch_shapes=[pltpu.VMEM((8, 1), F32)],
        compiler_params=_cparams(VMEM_BIG))(proj, bcol)


def fox_gate_bwd(proj, bcol, dcum, S):
    T = proj.shape[0]
    B = T // S
    ts = _tile(S, 512)
    nt = S // ts

    def body(p_ref, b_ref, dc_ref, dp_ref, db_ref, carry):
        b, j = pl.program_id(0), pl.program_id(1)

        @pl.when(j == 0)
        def _():
            carry[...] = jnp.zeros_like(carry)

        @pl.when((b == 0) & (j == 0))
        def _():
            db_ref[...] = jnp.zeros_like(db_ref)

        cf = jnp.transpose(p_ref[...])[0:8, :]
        dc = dc_ref[...]
        dlf = _sel_r(dc, _tri(ts, False)) + carry[...]
        carry[...] += jnp.sum(dc, axis=1, keepdims=True)
        dcf = dlf * jax.nn.sigmoid(-(cf + b_ref[...]))
        db_ref[...] += jnp.sum(dcf, axis=1, keepdims=True)
        full = jnp.concatenate([dcf, jnp.zeros((HP - 8, ts), F32)], axis=0)
        dp_ref[...] = jnp.transpose(full).astype(BF16)

    def rev(b, j):
        return nt - 1 - j

    return pl.pallas_call(
        body, name="fox_gate_bwd", grid=(B, nt),
        in_specs=[pl.BlockSpec((ts, HP), lambda b, j: (b * nt + rev(b, j), C_CF // HP)),
                  pl.BlockSpec((8, 1), lambda b, j: (0, 0)),
                  pl.BlockSpec((None, 8, ts), lambda b, j: (b, 0, rev(b, j)))],
        out_specs=[pl.BlockSpec((ts, HP), lambda b, j: (b * nt + rev(b, j), 0)), pl.BlockSpec((8, 1), lambda b, j: (0, 0))],
        out_shape=[_sds((T, HP), BF16), _sds((8, 1), F32)],
        scratch_shapes=[pltpu.VMEM((8, 1), F32)],
        compiler_params=_cparams(VMEM_BIG))(proj, bcol, dcum)


def _mla_pre(blk, gq, gkv, wq, wkv, place, cos_q, sin_q, cs_k):
    nq = _rms(blk[:, 0:B_Q_LORA], gq)
    nkv = _rms(blk[:, B_Q_LORA:B_Q_LORA + B_KV_LORA], gkv)
    qq = _mm(nq, wq)
    q = qq[:, 0:HEADS * HP] * cos_q + qq[:, HEADS * HP:] * sin_q
    kv = _mm(nkv, wkv)
    k = kv[:, 0:HEADS * HP] + _mm(blk[:, B_Q_LORA + B_KV_LORA:] * cs_k, place)
    return q, k, kv[:, HEADS * HP:]


def mla_pre_fwd(proj, gq, gkv, wq, wkv, place, cos_q, sin_q, cs_k, S):
    T = proj.shape[0]
    tm = _tile(S, 512)
    tpb = S // tm
    W = HEADS * HP

    def body(p_ref, gq_ref, gkv_ref, wq_ref, wkv_ref, pl_ref, cq_ref, sq_ref, ck_ref, q_ref, k_ref, v_ref):
        q, k, v = _mla_pre(p_ref[...], gq_ref[...], gkv_ref[...], wq_ref[...], wkv_ref[...], pl_ref[...],
                           cq_ref[...], sq_ref[...], ck_ref[...])
        q_ref[...] = q.astype(BF16)
        k_ref[...] = k.astype(BF16)
        v_ref[...] = v.astype(BF16)

    def full(a):
        return pl.BlockSpec(a.shape, lambda i: (0,) * a.ndim)

    tok = pl.BlockSpec((tm, W), lambda i: (i, 0))
    return pl.pallas_call(
        body, name="mla_pre_fwd", grid=(T // tm,),
        in_specs=[pl.BlockSpec((tm, W), lambda i: (i, C_B // W)), full(gq), full(gkv), full(wq), full(wkv), full(place),
                  pl.BlockSpec((tm, W), lambda i: (i % tpb, 0)), pl.BlockSpec((tm, W), lambda i: (i % tpb, 0)),
                  pl.BlockSpec((tm, HP), lambda i: (i % tpb, 0))],
        out_specs=[tok] * 3, out_shape=[_sds((T, W), BF16)] * 3,
        compiler_params=_cparams(VMEM_BIG))(proj, gq, gkv, wq, wkv, place, cos_q, sin_q, cs_k)


def mla_pre_bwd(proj, gq, gkv, wq, wkv, place, cos_q, sin_q, cs_k, dq, dk, dv, S):
    T = proj.shape[0]
    tm = _tile(S, 512)
    tpb = S // tm
    W = HEADS * HP

    def body(p_ref, gq_ref, gkv_ref, wq_ref, wkv_ref, pl_ref, cq_ref, sq_ref, ck_ref, dq_ref, dk_ref, dv_ref,
             dp_ref, dgq_ref, dgkv_ref, dwq_ref, dwkv_ref):
        @pl.when(pl.program_id(0) == 0)
        def _():
            for r in (dgq_ref, dgkv_ref, dwq_ref, dwkv_ref):
                r[...] = jnp.zeros_like(r)

        f = functools.partial(_mla_pre, place=pl_ref[...], cos_q=cq_ref[...], sin_q=sq_ref[...], cs_k=ck_ref[...])
        _, vjp = jax.vjp(f, p_ref[...], gq_ref[...], gkv_ref[...], wq_ref[...], wkv_ref[...])
        dp, dgq, dgkv, dwq, dwkv = vjp((dq_ref[...], dk_ref[...], dv_ref[...]))
        dp_ref[...] = dp.astype(BF16)
        dgq_ref[...] += dgq
        dgkv_ref[...] += dgkv
        dwq_ref[...] += dwq
        dwkv_ref[...] += dwkv

    def full(a):
        return pl.BlockSpec(a.shape, lambda i: (0,) * a.ndim)

    tok = pl.BlockSpec((tm, W), lambda i: (i, 0))
    return pl.pallas_call(
        body, name="mla_pre_bwd", grid=(T // tm,),
        in_specs=[pl.BlockSpec((tm, W), lambda i: (i, C_B // W)), full(gq), full(gkv), full(wq), full(wkv), full(place),
                  pl.BlockSpec((tm, W), lambda i: (i % tpb, 0)), pl.BlockSpec((tm, W), lambda i: (i % tpb, 0)),
                  pl.BlockSpec((tm, HP), lambda i: (i % tpb, 0)), tok, tok, tok],
        out_specs=[tok, full(gq), full(gkv), full(wq), full(wkv)],
        out_shape=[_sds((T, W), BF16), _sds(gq.shape, F32), _sds(gkv.shape, F32), _sds(wq.shape, F32), _sds(wkv.shape, F32)],
        compiler_params=_cparams(VMEM_BIG))(proj, gq, gkv, wq, wkv, place, cos_q, sin_q, cs_k, dq, dk, dv)


GMLP_CHUNKS = 4


def _gmlp_block(blk, lg, lb, ws, bs):
    u = jax.nn.gelu(blk[:, 0:GW])
    v = jax.nn.gelu(blk[:, GW:2 * GW])
    mu = jnp.mean(v, axis=-1, keepdims=True)
    var = jnp.mean(jnp.square(v - mu), axis=-1, keepdims=True)
    vn = (v - mu) * lax.rsqrt(var + LN_EPS) * lg + lb
    causal = _iota((D_CHUNK, D_CHUNK), 1) <= _iota((D_CHUNK, D_CHUNK), 0)
    group = _iota((1, GW), 1) // HD
    w = [jnp.where(causal, ws[g], 0.0) for g in range(HEADS)]
    chunks = []
    for c in range(blk.shape[0] // D_CHUNK):
        vc = vn[c * D_CHUNK:(c + 1) * D_CHUNK]
        mixed = jnp.zeros((D_CHUNK, GW), F32)
        for g in range(HEADS):
            mixed = mixed + jnp.where(group == g, _mm(w[g], vc) + bs[g], 0.0)
        chunks.append(mixed)
    return u * jnp.concatenate(chunks, axis=0)


def _gmlp_tile(T):
    return _tile(T, GMLP_CHUNKS * D_CHUNK) if T % (GMLP_CHUNKS * D_CHUNK) == 0 else D_CHUNK


def gmlp_fwd(proj, lg, lb, ws, bs):
    T = proj.shape[0]
    tm = _gmlp_tile(T)

    def body(p_ref, lg_ref, lb_ref, ws_ref, bs_ref, o_ref):
        o_ref[...] = _gmlp_block(p_ref[...], lg_ref[...], lb_ref[...], ws_ref[...], bs_ref[...]).astype(BF16)

    def full(a):
        return pl.BlockSpec(a.shape, lambda i: (0,) * a.ndim)

    return pl.pallas_call(
        body, name="gmlp_fwd", grid=(T // tm,),
        in_specs=[pl.BlockSpec((tm, 2 * GW), lambda i: (i, C_D // (2 * GW))), full(lg), full(lb), full(ws), full(bs)],
        out_specs=pl.BlockSpec((tm, GW), lambda i: (i, 0)), out_shape=_sds((T, GW), BF16),
        compiler_params=_cparams(VMEM_BIG))(proj, lg, lb, ws, bs)


def gmlp_bwd(proj, lg, lb, ws, bs, dcat):
    T = proj.shape[0]
    tm = _gmlp_tile(T)

    def body(p_ref, lg_ref, lb_ref, ws_ref, bs_ref, do_ref, dp_ref, dlg_ref, dlb_ref, dws_ref, dbs_ref):
        @pl.when(pl.program_id(0) == 0)
        def _():
            for r in (dlg_ref, dlb_ref, dws_ref, dbs_ref):
                r[...] = jnp.zeros_like(r)

        _, vjp = jax.vjp(_gmlp_block, p_ref[...], lg_ref[...], lb_ref[...], ws_ref[...], bs_ref[...])
        dp, dlg, dlb, dws, dbs = vjp(do_ref[...])
        dp_ref[...] = dp.astype(BF16)
        dlg_ref[...] += dlg
        dlb_ref[...] += dlb
        dws_ref[...] += dws
        dbs_ref[...] += dbs

    def full(a):
        return pl.BlockSpec(a.shape, lambda i: (0,) * a.ndim)

    return pl.pallas_call(
        body, name="gmlp_bwd", grid=(T // tm,),
        in_specs=[pl.BlockSpec((tm, 2 * GW), lambda i: (i, C_D // (2 * GW))), full(lg), full(lb), full(ws), full(bs),
                  pl.BlockSpec((tm, GW), lambda i: (i, O_D // GW))],
        out_specs=[pl.BlockSpec((tm, 2 * GW), lambda i: (i, 0)), full(lg), full(lb), full(ws), full(bs)],
        out_shape=[_sds((T, 2 * GW), BF16), _sds(lg.shape, F32), _sds(lb.shape, F32), _sds(ws.shape, F32), _sds(bs.shape, F32)],
        compiler_params=_cparams(VMEM_BIG))(proj, lg, lb, ws, bs, dcat)


def _lb_all(logits):
    m = jnp.max(logits, axis=0, keepdims=True)
    e = jnp.exp(logits - m)
    sm = e / jnp.sum(e, axis=0, keepdims=True)
    return jnp.concatenate([sm[0:1] - sm[0:1], (sm[0:1] + sm[1:2]) - sm[0:1]], axis=0)


def lb_fwd(logits):
    def body(l_ref, o_ref):
        o_ref[...] = _lb_all(l_ref[...])

    return pl.pallas_call(body, name="lb_fwd", out_shape=_sds(logits.shape, F32))(logits)


def lb_bwd(logits, dlb):
    def body(l_ref, d_ref, o_ref):
        _, vjp = jax.vjp(_lb_all, l_ref[...])
        o_ref[...] = vjp(d_ref[...])[0]

    return pl.pallas_call(body, name="lb_bwd", out_shape=_sds(logits.shape, F32))(logits, dlb)


def ada_grad(c_all, dmod_cols):
    N = dmod_cols.shape[1]
    tn = _tile(N, 1152)

    def body(c_ref, d_ref, o_ref):
        h = jax.nn.silu(c_ref[...]).astype(BF16)
        o_ref[...] = lax.dot_general(h, d_ref[...].astype(BF16), (((0,), (0,)), ((), ())), preferred_element_type=F32)

    nb = c_all.shape[0]
    return pl.pallas_call(
        body, name="ada_grad", grid=(N // tn,),
        in_specs=[pl.BlockSpec((nb, D), lambda j: (0, 0)), pl.BlockSpec((nb, tn), lambda j: (0, j))],
        out_specs=pl.BlockSpec((D, tn), lambda j: (0, j)), out_shape=_sds((D, N), F32),
        compiler_params=_cparams(VMEM_BIG))(c_all, dmod_cols)


def sum_slots(a, n, name):
    _, R, C = a.shape
    tr = _row_tile(R, C, n)

    def body(a_ref, o_ref):
        acc = a_ref[0]
        for k in range(1, n):
            acc = acc + a_ref[k]
        o_ref[...] = acc

    return pl.pallas_call(
        body, name=name, grid=(R // tr,),
        in_specs=[pl.BlockSpec((n, tr, C), lambda i: (0, i, 0))],
        out_specs=pl.BlockSpec((tr, C), lambda i: (i, 0)), out_shape=_sds((R, C), F32),
        compiler_params=_cparams(VMEM_BIG))(a)


def _row_tile(R, C=D, n=1, mult=8, elems=1 << 18):
    limit = max(mult, elems // (C * n))
    for t in range(limit - limit % mult, mult - 1, -mult):
        if R % t == 0:
            return t
    return R


def adamw(w, g, m, v, name, echo=False):
    R, C = w.shape
    tr = _row_tile(R, C, elems=1 << 19)
    c1 = 1.0 - ADAM_B1 ** ADAM_STEP
    c2 = 1.0 - ADAM_B2 ** ADAM_STEP
    n_out = 4 if echo else 3

    def body(w_ref, g_ref, m_ref, v_ref, d_ref, nm_ref, nv_ref, *g_out):
        g_ = g_ref[...]
        nm = ADAM_B1 * m_ref[...] + (1.0 - ADAM_B1) * g_
        nv = ADAM_B2 * v_ref[...] + (1.0 - ADAM_B2) * jnp.square(g_)
        d_ref[...] = -ADAM_LR * ((nm / c1) / (jnp.sqrt(nv / c2) + ADAM_EPS) + ADAM_WD * w_ref[...])
        nm_ref[...] = nm
        nv_ref[...] = nv
        if echo:
            g_out[0][...] = g_

    spec = pl.BlockSpec((tr, C), lambda i: (i, 0))
    return pl.pallas_call(body, name=name, grid=(R // tr,), in_specs=[spec] * 4, out_specs=[spec] * n_out,
                          out_shape=[_sds((R, C), F32)] * n_out, compiler_params=_cparams(VMEM_BIG))(w, g, m, v)


def _rot_cols(w):
    return jnp.concatenate([-w[:, 16:32], w[:, 0:16]], axis=1)


def _fold_rot(d):
    return jnp.concatenate([d[:, 16:32], -d[:, 0:16]], axis=1)


def _pad_heads(w, off, axis):
    parts = []
    for h in range(HEADS):
        piece = lax.slice_in_dim(w, off + HD * h, off + HD * (h + 1), axis=axis)
        parts += [piece, jnp.zeros_like(piece)]
    return parts


def _unpad_heads(d, off, axis):
    return [lax.slice_in_dim(d, off + HP * h, off + HP * h + HD, axis=axis) for h in range(HEADS)]


def mix_in_ext(w):
    z = lambda n: jnp.zeros((w.shape[0], n), w.dtype)
    kr = w[:, 1408:1440]
    cols = [w[:, 0:1408], kr, _rot_cols(kr), z(64)]
    cols += _pad_heads(w, 1440, 1) + _pad_heads(w, 1696, 1) + _pad_heads(w, 1952, 1)
    cols += [w[:, 2212:2724], w[:, 2208:2212], z(NP - C_CF - HEADS)]
    return jnp.concatenate(cols, axis=1)


def mix_in_unext(d):
    kr = d[:, 1408:1440] + _fold_rot(d[:, 1440:1472])
    cols = [d[:, 0:1408], kr] + _unpad_heads(d, C_CQ, 1) + _unpad_heads(d, C_CK, 1) + _unpad_heads(d, C_CV, 1)
    cols += [d[:, C_CF:C_CF + HEADS], d[:, C_D:C_D + 2 * GW]]
    return jnp.concatenate(cols, axis=1)


def mix_out_ext(w):
    return jnp.concatenate([w[0:GW]] + _pad_heads(w, GW, 0) + _pad_heads(w, 2 * GW, 0) + [w[3 * GW:4 * GW]], axis=0)


def mix_out_unext(d):
    return jnp.concatenate([d[0:GW]] + _unpad_heads(d, O_B, 0) + _unpad_heads(d, O_C, 0) + [d[O_D:O_D + GW]], axis=0)


def uq_ext(w):
    z = lambda n: jnp.zeros((w.shape[0], n), w.dtype)
    a, b = [], []
    for h in range(HEADS):
        o = (B_NOPE + B_ROPE) * h
        a += [w[:, o:o + B_NOPE + B_ROPE], z(32)]
        b += [z(B_NOPE), _rot_cols(w[:, o + B_NOPE:o + B_NOPE + B_ROPE]), z(32)]
    return jnp.concatenate(a + b, axis=1)


def uq_unext(d):
    cols = []
    for h in range(HEADS):
        o = HP * h
        cols += [d[:, o:o + B_NOPE], d[:, o + B_NOPE:o + B_NOPE + B_ROPE]
                 + _fold_rot(d[:, HEADS * HP + o + B_NOPE:HEADS * HP + o + B_NOPE + B_ROPE])]
    return jnp.concatenate(cols, axis=1)


def ukv_ext(w):
    z = jnp.zeros((w.shape[0], HD), w.dtype)
    k, v = [], []
    for h in range(HEADS):
        k += [w[:, 2 * HD * h:2 * HD * h + HD], z]
        v += [w[:, 2 * HD * h + HD:2 * HD * (h + 1)], z]
    return jnp.concatenate(k + v, axis=1)


def ukv_unext(d):
    cols = []
    for h in range(HEADS):
        cols += [d[:, HP * h:HP * h + HD], d[:, HEADS * HP + HP * h:HEADS * HP + HP * h + HD]]
    return jnp.concatenate(cols, axis=1)


def rope_tables(S):
    half = B_ROPE // 2
    inv_freq = ROPE_THETA ** (-jnp.arange(half, dtype=F32) / half)
    ang = jnp.arange(S).astype(F32)[:, None] * inv_freq[None, :]
    cos = jnp.tile(jnp.cos(ang), (1, 2))
    sin = jnp.tile(jnp.sin(ang), (1, 2))
    one, zero = jnp.ones((S, B_NOPE), F32), jnp.zeros((S, B_NOPE), F32)
    z32 = jnp.zeros((S, 32), F32)
    cos_q = jnp.tile(jnp.concatenate([one, cos, z32], axis=1), (1, HEADS))
    sin_q = jnp.tile(jnp.concatenate([zero, sin, z32], axis=1), (1, HEADS))
    cs_k = jnp.concatenate([cos, sin, zero], axis=1)
    place = np.zeros((HP, HEADS * HP), np.float32)
    for h in range(HEADS):
        for j in range(B_ROPE):
            place[j, h * HP + B_NOPE + j] = 1.0
            place[B_ROPE + j, h * HP + B_NOPE + j] = 1.0
    return cos_q, sin_q, cs_k, jnp.asarray(place, BF16)


def layer_fwd(x, mod, get, tabs, S):
    cos_q, sin_q, cs_k, place = tabs
    p = dict(get("ffn1", x))
    l = p["wl"]
    zg1, zu1, act1 = ffn_in_fwd(x, mod, p["ffn1_in"], l, 0, 1, S)
    y1, x1 = out_ln_fwd(act1, p["ffn1_out"], x, mod, p["ln_g"][0:1], p["ln_b"][0:1], 2, 0.5, S, l)
    p.update(get("mix", x1))
    proj = mix_in_fwd(x1, mod, p["mix_in"], 3, 4, S)
    o_a, states = hgrn_fwd(proj, p["lb"], p["ng"], S)
    q_b, k_b, v_b = mla_pre_fwd(proj, p["gq"], p["gkv"], p["wq"], p["wkv"], place, cos_q, sin_q, cs_k, S)
    o_b = attn_fwd(q_b, 0, k_b, 0, v_b, 0, None, (B_NOPE + B_ROPE) ** -0.5, S)
    cum = fox_gate_fwd(proj, p["bcol"], S)
    o_c = attn_fwd(proj, C_CQ // HP, proj, C_CK // HP, proj, C_CV // HP, cum, HD ** -0.5, S)
    o_d = gmlp_fwd(proj, p["g_lg"], p["g_lb"], p["ws"], p["bs"])
    cat = jnp.concatenate([o_a, o_b, o_c, o_d], axis=1)
    y2, x2 = out_ln_fwd(cat, p["mix_out"], x1, mod, p["ln_g"][1:2], p["ln_b"][1:2], 5, 1.0, S)
    p.update(get("ffn2", x2))
    zg3, zu3, act3 = ffn_in_fwd(x2, mod, p["ffn2_in"], l, 6, 7, S)
    y3, x3 = out_ln_fwd(act3, p["ffn2_out"], x2, mod, p["ln_g"][2:3], p["ln_b"][2:3], 8, 0.5, S, l)
    saved = dict(x=x, zg1=zg1, zu1=zu1, act1=act1, y1=y1, x1=x1, proj=proj, states=states, q_b=q_b, k_b=k_b, v_b=v_b,
                 cum=cum, cat=cat, y2=y2, x2=x2, zg3=zg3, zu3=zu3, act3=act3, y3=y3, p=p)
    return x3, saved


def _ffn_bwd(dxn, x_in, y, zg, zu, act, mod, w_in, w_out, l, lg, lb, idx, S, emit):
    sh, sc, gate = idx
    dres, dy, dgate, dlg, dlb = ln_res_bwd(dxn, x_in, y, mod, lg, lb, gate, 0.5, S)
    dzg, dzu = swiglu_bwd(dy, w_out, l, zg, zu, S)
    dw_out = tn_mm(act, dy, FF // 2).reshape(N_CHIPS, FF // N_CHIPS, D)
    dw_in = tn_mm_mod_shards(x_in, mod, dzg, dzu, sh, sc, S)
    mod = mod + emit(dw_in, dw_out)
    dx, dsh, dsc = nt_mod_bwd([dzg, dzu], w_in, [0, 2], x_in, mod, dres, sc, S, FF // 2, l)
    return dx, dw_in, dw_out, dlg, dlb, {sh: dsh, sc: dsc, gate: dgate}, mod


def layer_bwd(dx3, mod, sv, tabs, S, emit):
    cos_q, sin_q, cs_k, place = tabs
    p = sv["p"]
    l = p["wl"]
    g = {}
    dm = {}

    def emit_ffn(part):
        def f(dw_in, dw_out):
            g[part + "_in"], g[part + "_out"] = dw_in, dw_out
            return emit(part, g)
        return f

    dx2, _, _, dlg2, dlb2, d, mod = _ffn_bwd(
        dx3, sv["x2"], sv["y3"], sv["zg3"], sv["zu3"], sv["act3"], mod, p["ffn2_in"], p["ffn2_out"], l,
        p["ln_g"][2:3], p["ln_b"][2:3], (6, 7, 8), S, emit_ffn("ffn2"))
    dm.update(d)
    dres, dy2, dm[5], dlg1, dlb1 = ln_res_bwd(dx2, sv["x1"], sv["y2"], mod, p["ln_g"][1:2], p["ln_b"][1:2], 5, 1.0, S)
    dcat = nt_plain(dy2, p["mix_out"])
    g["mix_out"] = tn_mm(sv["cat"], dy2, NCAT // 2)
    proj = sv["proj"]
    d_a, g["lb"], g["ng"] = hgrn_bwd(proj, sv["states"], dcat, p["lb"], p["ng"], S)
    dq_c, dk_c, dv_c, dcum = attn_bwd(proj, C_CQ // HP, proj, C_CK // HP, proj, C_CV // HP, sv["cum"], dcat,
                                      O_C // HP, HD ** -0.5, S, BF16)
    B = proj.shape[0] // S
    dcum = jnp.concatenate([dcum.reshape(B, HEADS, S), jnp.zeros((B, 8 - HEADS, S), F32)], axis=1)
    d_cf, g["bcol"] = fox_gate_bwd(proj, p["bcol"], dcum, S)
    dq_b, dk_b, dv_b = attn_bwd(sv["q_b"], 0, sv["k_b"], 0, sv["v_b"], 0, None, dcat, O_B // HP,
                                (B_NOPE + B_ROPE) ** -0.5, S, F32)
    d_b, g["gq"], g["gkv"], g["wq"], g["wkv"] = mla_pre_bwd(
        proj, p["gq"], p["gkv"], p["wq"], p["wkv"], place, cos_q, sin_q, cs_k, dq_b, dk_b, dv_b, S)
    d_d, g["g_lg"], g["g_lb"], g["ws"], g["bs"] = gmlp_bwd(proj, p["g_lg"], p["g_lb"], p["ws"], p["bs"], dcat)
    dproj = jnp.concatenate([d_a, d_b, dq_c, dk_c, dv_c, d_d, d_cf, jnp.zeros_like(d_cf)], axis=1)
    g["mix_in"] = tn_mm_mod(sv["x1"], mod, dproj, 3, 4, S, NP_TILE)
    mod = mod + emit("mix", g)
    dx1, dm[3], dm[4] = nt_mod_bwd([dproj], p["mix_in"], [0], sv["x1"], mod, dres, 4, S, NP_TILE)
    last = []

    def emit_last(dw_in, dw_out):
        last.append(emit_ffn("ffn1")(dw_in, dw_out))
        return last[0]

    dx0, _, _, dlg0, dlb0, d, mod = _ffn_bwd(
        dx1, sv["x"], sv["y1"], sv["zg1"], sv["zu1"], sv["act1"], mod, p["ffn1_in"], p["ffn1_out"], l,
        p["ln_g"][0:1], p["ln_b"][0:1], (0, 1, 2), S, emit_last)
    dm.update(d)
    g["ln_g"] = jnp.concatenate([dlg0, dlg1, dlg2], axis=0)
    g["ln_b"] = jnp.concatenate([dlb0, dlb1, dlb2], axis=0)
    dmod = jnp.concatenate([dm[i] for i in range(N_MOD)], axis=1)
    return dx0, dmod, g, last[0]


def local_step(x, c8, tgt, get, lb_logits, S, emit=None):
    B = x.shape[0] // S
    tabs = rope_tables(S)
    lb_all = lb_fwd(lb_logits)
    mods, saved = [], []
    h = x
    for l in range(DEPTH):
        pa = get(l, "ada", h)
        mod = mod_fwd(c8, pa["ada_w"], pa["wl"], pa["ada_b"])[0:B].reshape(B, N_MOD, D)

        def get_l(part, after, l=l):
            p = dict(get(l, part, after))
            if part == "mix":
                p["lb"] = lb_all[l:l + 1]
            return p

        h, sv = layer_fwd(h, mod, get_l, tabs, S)
        mods.append(mod)
        saved.append(sv)
    loss_tile, dh = loss_head(h, tgt)
    grads, dmods, dlb = [None] * DEPTH, [None] * DEPTH, [None] * DEPTH
    tie = jnp.zeros((), F32)
    for l in reversed(range(DEPTH)):
        emit_l = (lambda part, g: jnp.zeros((), F32)) if emit is None else functools.partial(emit, l)
        dh, dmods[l], grads[l], tie = layer_bwd(dh, mods[l] + tie, saved[l], tabs, S, emit_l)
        dlb[l] = grads[l].pop("lb")
    d_logits = lb_bwd(lb_logits, jnp.concatenate(dlb, axis=0))
    return loss_tile, dh, dmods, grads, d_logits


ANY = pl.BlockSpec(memory_space=pl.ANY)


def _place():
    x, y, c = lax.axis_index("x"), lax.axis_index("y"), lax.axis_index("c")
    chips = [(1 - x, y), (x, 1 - y), (1 - x, 1 - y)]
    return x, y, c, chips


def _rcopy(src, dst, sems, k, to):
    send_sems, recv_sems = sems
    return pltpu.make_async_remote_copy(src_ref=src, dst_ref=dst, send_sem=send_sems.at[k], recv_sem=recv_sems.at[k],
                                        device_id=to, device_id_type=MESH)


def _dma_sems(n_remote, n_local):
    return [pltpu.SemaphoreType.DMA((n_remote,)), pltpu.SemaphoreType.DMA((n_remote,)), pltpu.SemaphoreType.DMA((n_local,))]


def own_slot(src, chip):
    L = src.shape[0]
    return lax.dynamic_update_slice(lax.empty((L, N_CHIPS) + src.shape[1:], src.dtype), src[:, None], (0, chip, 0, 0))


HBM_SPEC = pl.BlockSpec(memory_space=pltpu.HBM)
SEM_SPEC = pl.BlockSpec(memory_space=pltpu.SEMAPHORE)
DATAFLOW = pltpu.SideEffectType.DATAFLOW_SIDE_EFFECTING


def _split_start(srcs, lands, copies, n_copies, dep, name):
    n, m = len(srcs), len(lands)

    def body(*refs):
        ins = refs[:n + m]
        send_sems, recv_sems = refs[n + m + 1], refs[n + m + 2]
        token = refs[-1]
        for k, (src, dst, to) in enumerate(copies(ins[:n], ins[n:], _place())):
            pltpu.make_async_remote_copy(src_ref=src, dst_ref=dst, send_sem=send_sems.at[k], recv_sem=recv_sems.at[k],
                                         device_id=to, device_id_type=MESH).start()
        token[...] = jnp.zeros_like(token)

    arrs = list(srcs) + list(lands)
    outs = pl.pallas_call(
        body, name=name,
        out_shape=(pltpu.SemaphoreType.DMA((n_copies,)), pltpu.SemaphoreType.DMA((n_copies,)),
                   *[pltpu.HBM(a.shape, a.dtype) for a in arrs], _sds((8, 128), F32)),
        in_specs=[HBM_SPEC] * (n + m) + [ANY],
        out_specs=(SEM_SPEC, SEM_SPEC, *[HBM_SPEC] * (n + m), pl.BlockSpec(memory_space=pltpu.VMEM)),
        input_output_aliases={i: 2 + i for i in range(n + m)},
        compiler_params=pltpu.CompilerParams(has_side_effects=DATAFLOW),
    )(*[pltpu.with_memory_space_constraint(a, pltpu.HBM) for a in arrs], dep)
    return outs[0], outs[1], list(outs[2:2 + n]), list(outs[2 + n:2 + n + m]), outs[-1]


def _split_wait(handle, arrivals, after, name):
    send_sems, recv_sems, srcs, lands, _ = handle
    n, m = len(srcs), len(lands)

    def body(*refs):
        ins = refs[:n + m]
        send_sems, recv_sems = refs[n + m], refs[n + m + 1]
        x, y, c, chips = place = _place()
        for k, (src, dst) in enumerate(arrivals(ins[:n], ins[n:], place)):
            cp = pltpu.make_async_remote_copy(src_ref=src, dst_ref=dst, send_sem=send_sems.at[k], recv_sem=recv_sems.at[k],
                                              device_id=(x, y, 1 - c), device_id_type=MESH)
            cp.wait_send()
            cp.wait_recv()

    arrs = list(srcs) + list(lands)
    outs = pl.pallas_call(
        body, name=name, out_shape=[pltpu.HBM(a.shape, a.dtype) for a in arrs],
        in_specs=[HBM_SPEC] * (n + m) + [SEM_SPEC, SEM_SPEC, ANY], out_specs=[HBM_SPEC] * (n + m),
        input_output_aliases={i: i for i in range(n + m)},
        compiler_params=pltpu.CompilerParams(has_side_effects=DATAFLOW),
    )(*arrs, send_sems, recv_sems, after)
    return list(outs[:n]), list(outs[n:])


def _ag_part(ref, k, hc):
    rh = ref.shape[2] // 2
    return ref.at[:, k, pl.ds(hc * rh, rh), :]


def ag_start(srcs, lands, dep, name):
    def copies(s, d, place):
        x, y, c, chips = place
        out = []
        for j, (px, py) in enumerate(chips):
            for i in range(len(s)):
                rh = s[i].shape[1] // 2
                out.append((s[i].at[:, pl.ds(c * rh, rh), :], _ag_part(d[i], 2 * x + y, c), (px, py, c)))
        return out

    return _split_start(srcs, lands, copies, 3 * len(srcs), dep, name)


def ag_wait(handle, after, name):
    def arrivals(s, d, place):
        x, y, c, chips = place
        out = []
        for j, (px, py) in enumerate(chips):
            for i in range(len(s)):
                rh = s[i].shape[1] // 2
                out.append((s[i].at[:, pl.ds(c * rh, rh), :], _ag_part(d[i], 2 * px + py, c)))
        return out

    return _split_wait(handle, arrivals, after, name)


def ag_forward(lands, name):
    n = len(lands)

    def body(*refs):
        bufs, token = refs[n:2 * n], refs[2 * n]
        send_sems, recv_sems = refs[2 * n + 1:]
        x, y, c, chips = _place()
        sems = (send_sems, recv_sems)
        token[...] = jnp.zeros_like(token)
        cps = []
        for j, (px, py) in enumerate(chips):
            for i in range(n):
                part = _ag_part(bufs[i], 2 * px + py, c)
                cps.append(_rcopy(part, part, sems, 3 * i + j, (x, y, 1 - c)))
        for cp in cps:
            cp.start()
        for j, (px, py) in enumerate(chips):
            for i in range(n):
                part = _ag_part(bufs[i], 2 * px + py, 1 - c)
                _rcopy(part, part, sems, 3 * i + j, (x, y, 1 - c)).wait_recv()
        for cp in cps:
            cp.wait_send()

    outs = pl.pallas_call(
        body, name=name, out_shape=[_sds(a.shape, a.dtype) for a in lands] + [_sds((8, 128), F32)],
        in_specs=[ANY] * n, out_specs=[ANY] * n + [pl.BlockSpec(memory_space=pltpu.VMEM)],
        input_output_aliases={i: i for i in range(n)}, scratch_shapes=_dma_sems(3 * n, 1)[:2])(*lands)
    return list(outs[:n]), outs[n]


def rs_start(hs, lands, dep, name):
    def copies(s, d, place):
        x, y, c, chips = place
        return [(s[i].at[2 * px + py], d[i].at[2 * x + y], (px, py, c)) for j, (px, py) in enumerate(chips) for i in range(len(s))]

    return _split_start(hs, lands, copies, 3 * len(hs), dep, name)


def rs_wait(handle, after, name):
    def arrivals(s, d, place):
        x, y, c, chips = place
        return [(s[i].at[2 * px + py], d[i].at[2 * px + py]) for j, (px, py) in enumerate(chips) for i in range(len(s))]

    return _split_wait(handle, arrivals, after, name)


def sibling_swap(arrs, name):
    n = len(arrs)
    rh = [a.shape[1] // 2 for a in arrs]

    def body(*refs):
        srcs, outs = refs[:n], refs[n:2 * n]
        send_sems, recv_sems = refs[2 * n:]
        x, y, c, _ = _place()
        cps = [_rcopy(srcs[i].at[:, pl.ds((1 - c) * rh[i], rh[i]), :], outs[i], (send_sems, recv_sems), i, (x, y, 1 - c))
               for i in range(n)]
        for cp in cps:
            cp.start()
        for cp in cps:
            cp.wait()

    return pl.pallas_call(
        body, name=name, out_shape=[_sds((N_CHIPS, r, a.shape[2]), a.dtype) for a, r in zip(arrs, rh)],
        in_specs=[ANY] * n, out_specs=[ANY] * n, scratch_shapes=_dma_sems(n, 1)[:2])(*arrs)


def sum_into(land, base, l, core, name):
    _, rh, C = land.shape
    tr = _row_tile(rh, C, N_CHIPS, mult=16)
    nr = rh // tr

    def body(core_ref, land_ref, base_ref, o_ref):
        acc = land_ref[0].astype(F32)
        for k in range(1, N_CHIPS):
            acc = acc + land_ref[k].astype(F32)
        o_ref[...] = acc

    grid_spec = pltpu.PrefetchScalarGridSpec(
        num_scalar_prefetch=1, grid=(nr,),
        in_specs=[pl.BlockSpec((N_CHIPS, tr, C), lambda r, core_ref: (0, r, 0)), ANY],
        out_specs=pl.BlockSpec((None, tr, C), lambda r, core_ref: (l, core_ref[0] * nr + r, 0)))
    return pl.pallas_call(body, name=name, grid_spec=grid_spec, out_shape=_sds(base.shape, base.dtype),
                          input_output_aliases={2: 0}, compiler_params=_cparams(VMEM_BIG))(
        core.reshape(1).astype(jnp.int32), land, base)


def sibling_join(bases, name):
    n = len(bases)

    def body(*refs):
        bufs = refs[n:2 * n]
        send_sems, recv_sems = refs[2 * n:]
        x, y, c, _ = _place()
        sems = (send_sems, recv_sems)

        def half(i, hc):
            rh = bufs[i].shape[1] // 2
            return bufs[i].at[:, pl.ds(hc * rh, rh), :]

        sends = [_rcopy(half(i, c), half(i, c), sems, i, (x, y, 1 - c)) for i in range(n)]
        for cp in sends:
            cp.start()
        for i in range(n):
            _rcopy(half(i, 1 - c), half(i, 1 - c), sems, i, (x, y, 1 - c)).wait_recv()
        for cp in sends:
            cp.wait_send()

    return pl.pallas_call(
        body, name=name, out_shape=[_sds(b.shape, b.dtype) for b in bases], in_specs=[ANY] * n, out_specs=[ANY] * n,
        input_output_aliases={i: i for i in range(n)}, scratch_shapes=_dma_sems(n, 1)[:2])(*bases)


def ag_all(blk):
    M, C = blk.shape

    def body(x_ref, out_ref, send_sems, recv_sems, loc_sem):
        x, y, c, chips = _place()
        sems = (send_sems, recv_sems)
        me, sibling = (x, y, c), (x, y, 1 - c)

        def slot(px, py, pc):
            return out_ref.at[4 * px + 2 * py + pc]

        mine = pltpu.make_async_copy(x_ref, slot(*me), loc_sem)
        mine.start()
        first = [_rcopy(x_ref, slot(*me), sems, 0, sibling)]
        first += [_rcopy(x_ref, slot(*me), sems, 1 + j, (*chip, c)) for j, chip in enumerate(chips)]
        for cp in first:
            cp.start()
        passed = [_rcopy(slot(*chip, c), slot(*chip, c), sems, 4 + j, sibling) for j, chip in enumerate(chips)]
        for j, chip in enumerate(chips):
            _rcopy(slot(*chip, c), slot(*chip, c), sems, 1 + j, me).wait_recv()
            passed[j].start()
        _rcopy(slot(*sibling), slot(*sibling), sems, 0, me).wait_recv()
        for j, chip in enumerate(chips):
            _rcopy(slot(*chip, 1 - c), slot(*chip, 1 - c), sems, 4 + j, me).wait_recv()
        for cp in first + passed:
            cp.wait_send()
        mine.wait()

    return pl.pallas_call(
        body, name="ag_all", out_shape=_sds((8, M, C), blk.dtype),
        in_specs=[pl.BlockSpec(memory_space=pltpu.VMEM)], out_specs=pl.BlockSpec(memory_space=pltpu.VMEM),
        scratch_shapes=[pltpu.SemaphoreType.DMA((7,)), pltpu.SemaphoreType.DMA((7,)), pltpu.SemaphoreType.DMA(())],
        compiler_params=_cparams(VMEM_BIG))(blk)


WEIGHTS = ["ada_w", "ada_b", "ln_g", "ln_b", "ffn1_w_in", "ffn1_w_out", "ffn2_w_in", "ffn2_w_out", "mix_w_in", "mix_w_out",
           "hgrn_lb_logits", "hgrn_norm_g", "mla_q_norm_g", "mla_kv_norm_g", "mla_w_uq", "mla_w_ukv", "fox_b_f",
           "gmlp_ln_g", "gmlp_ln_b", "gmlp_w_s", "gmlp_b_s"]
SMALL = ["hgrn_lb_logits", "hgrn_norm_g", "mla_q_norm_g", "mla_kv_norm_g", "fox_b_f", "gmlp_ln_g", "gmlp_ln_b",
         "gmlp_w_s", "gmlp_b_s", "ln_g", "ln_b"]
GATHERED = ["ada_w", "ffn1_w_in", "ffn1_w_out", "ffn2_w_in", "ffn2_w_out", "mix_w_in", "mix_w_out", "mla_w_uq", "mla_w_ukv"]
REDUCED = GATHERED[1:]


def _col_shards(a):
    cols = a.shape[1] // N_CHIPS
    return jnp.stack([a[:, k * cols:(k + 1) * cols] for k in range(N_CHIPS)])


def add_kept_half(a, got, core, name):
    _, R, C = a.shape
    rh = R // 2
    tr = _row_tile(rh, C, mult=16)
    nr = rh // tr

    def body(core_ref, a_ref, b_ref, o_ref):
        o_ref[...] = (a_ref[...].astype(F32) + b_ref[...].astype(F32)).astype(o_ref.dtype)

    half = pl.BlockSpec((None, tr, C), lambda k, r, core_ref: (k, r, 0))
    grid_spec = pltpu.PrefetchScalarGridSpec(
        num_scalar_prefetch=1, grid=(N_CHIPS, nr),
        in_specs=[pl.BlockSpec((None, tr, C), lambda k, r, core_ref: (k, core_ref[0] * nr + r, 0)), half],
        out_specs=half)
    return pl.pallas_call(body, name=name, grid_spec=grid_spec, out_shape=_sds((N_CHIPS, rh, C), BF16),
                          compiler_params=_cparams(VMEM_BIG))(core.reshape(1).astype(jnp.int32), a, got)


def _rows(parts, n_rows, dtype):
    flat = jnp.concatenate([p.reshape(-1) for p in parts])
    pad = n_rows * D - flat.shape[0]
    return jnp.concatenate([flat, jnp.zeros((pad,), dtype)]).reshape(n_rows, D)


def _take(flat, shapes):
    out, o = [], 0
    for shp in shapes:
        n = int(np.prod(shp))
        out.append(flat[o:o + n].reshape(shp))
        o += n
    return out


def _round_up(n, m):
    return -(-n // m) * m


def pack_small(w):
    parts = [w[n][l] for l in range(DEPTH) for n in SMALL]
    n = sum(int(np.prod(p.shape)) for p in parts)
    return _rows(parts, _round_up(-(-n // D), 8), F32)


def unpack_small(pk, like):
    shapes = [like[n].shape[1:] for l in range(DEPTH) for n in SMALL]
    pieces = _take(pk.reshape(-1), shapes)
    names = [n for l in range(DEPTH) for n in SMALL]
    return {n: jnp.stack([p for p, m in zip(pieces, names) if m == n]) for n in SMALL}


def kernel(x, c, ada_w, ada_b, ln_g, ln_b, ffn1_w_in, ffn1_w_out, ffn2_w_in, ffn2_w_out, mix_w_in, mix_w_out, hgrn_lb_logits, hgrn_norm_g, mla_q_norm_g, mla_kv_norm_g, mla_w_uq, mla_w_ukv, fox_b_f, gmlp_ln_g, gmlp_ln_b, gmlp_w_s, gmlp_b_s, loss_target, m_ada_w, m_ada_b, m_ln_g, m_ln_b, m_ffn1_w_in, m_ffn1_w_out, m_ffn2_w_in, m_ffn2_w_out, m_mix_w_in, m_mix_w_out, m_hgrn_lb_logits, m_hgrn_norm_g, m_mla_q_norm_g, m_mla_kv_norm_g, m_mla_w_uq, m_mla_w_ukv, m_fox_b_f, m_gmlp_ln_g, m_gmlp_ln_b, m_gmlp_w_s, m_gmlp_b_s, v_ada_w, v_ada_b, v_ln_g, v_ln_b, v_ffn1_w_in, v_ffn1_w_out, v_ffn2_w_in, v_ffn2_w_out, v_mix_w_in, v_mix_w_out, v_hgrn_lb_logits, v_hgrn_norm_g, v_mla_q_norm_g, v_mla_kv_norm_g, v_mla_w_uq, v_mla_w_ukv, v_fox_b_f, v_gmlp_ln_g, v_gmlp_ln_b, v_gmlp_w_s, v_gmlp_b_s):
    w = dict(zip(WEIGHTS, (ada_w, ada_b, ln_g, ln_b, ffn1_w_in, ffn1_w_out, ffn2_w_in, ffn2_w_out, mix_w_in, mix_w_out, hgrn_lb_logits, hgrn_norm_g, mla_q_norm_g, mla_kv_norm_g, mla_w_uq, mla_w_ukv, fox_b_f, gmlp_ln_g, gmlp_ln_b, gmlp_w_s, gmlp_b_s)))
    m = dict(zip(WEIGHTS, (m_ada_w, m_ada_b, m_ln_g, m_ln_b, m_ffn1_w_in, m_ffn1_w_out, m_ffn2_w_in, m_ffn2_w_out, m_mix_w_in, m_mix_w_out, m_hgrn_lb_logits, m_hgrn_norm_g, m_mla_q_norm_g, m_mla_kv_norm_g, m_mla_w_uq, m_mla_w_ukv, m_fox_b_f, m_gmlp_ln_g, m_gmlp_ln_b, m_gmlp_w_s, m_gmlp_b_s)))
    v = dict(zip(WEIGHTS, (v_ada_w, v_ada_b, v_ln_g, v_ln_b, v_ffn1_w_in, v_ffn1_w_out, v_ffn2_w_in, v_ffn2_w_out, v_mix_w_in, v_mix_w_out, v_hgrn_lb_logits, v_hgrn_norm_g, v_mla_q_norm_g, v_mla_kv_norm_g, v_mla_w_uq, v_mla_w_ukv, v_fox_b_f, v_gmlp_ln_g, v_gmlp_ln_b, v_gmlp_w_s, v_gmlp_b_s)))
    Bl, S, _ = x.shape
    T = Bl * S
    core = lax.axis_index("c")
    chip = 2 * lax.axis_index("x") + lax.axis_index("y")

    def shard(key):
        n, l = key
        if n == "ln":
            return jnp.concatenate([ln_g[l:l + 1], ln_b[l:l + 1], jnp.zeros((1, 2, D // N_CHIPS), F32)], axis=1)
        return w[n][l:l + 1].astype(BF16)

    mixers = ["mix_w_in", "mix_w_out", "mla_w_uq", "mla_w_ukv"]
    groups = [[("ada_w", 0), ("ffn1_w_in", 0), ("ffn1_w_out", 0), ("ln", 0)],
              [(n, 0) for n in mixers + ["ffn2_w_in", "ffn2_w_out"]],
              [(n, 1) for n in GATHERED + ["ln"]]]
    srcs = [[shard(k) for k in grp] for grp in groups]
    lands = [[own_slot(s, chip) for s in srcs[0]]]
    handle0 = ag_start(srcs[0], lands[0], jnp.zeros((8, 128), F32), "ag_start_0")
    chip_later = chip + handle0[-1][0, 0].astype(jnp.int32)
    lands += [[own_slot(s, chip_later) for s in grp] for grp in srcs[1:]]
    first, token = ag_forward(ag_wait(handle0, lands[2][0], "ag_wait_0")[1], "ag_forward_0")
    have = dict(zip(groups[0], first))
    handles = {}
    for gi in (1, 2):
        handles[gi] = ag_start(srcs[gi], lands[gi], token, "ag_start_%d" % gi)
        token = handles[gi][-1]
    c8 = jnp.concatenate([c, jnp.zeros((8 - Bl, D), F32)], axis=0)
    c8 = c8 + token[0, 0]

    def cat_cols(a):
        return jnp.concatenate([a[0, k] for k in range(N_CHIPS)], axis=1)

    def get(l, part, after):
        gi = 2 if l == 1 else (0 if part in ("ada", "ffn1") else 1)
        if gi in handles:
            arrived, _ = ag_forward(ag_wait(handles.pop(gi), after, "ag_wait_%d" % gi)[1], "ag_forward_%d" % gi)
            have.update(zip(groups[gi], arrived))
        if part == "ada":
            return dict(ada_w=have[("ada_w", l)], wl=0, ada_b=ada_b[l][None])
        if part == "ffn1":
            ln_full = jnp.moveaxis(have[("ln", l)][0], 0, 1).reshape(8, D)
            return dict(ffn1_in=have[("ffn1_w_in", l)], ffn1_out=have[("ffn1_w_out", l)], wl=0,
                        ln_g=ln_full[0:3], ln_b=ln_full[3:6])
        if part == "ffn2":
            return dict(ffn2_in=have[("ffn2_w_in", l)], ffn2_out=have[("ffn2_w_out", l)])
        return dict(
            mix_in=mix_in_ext(cat_cols(have[("mix_w_in", l)])), mix_out=mix_out_ext(have[("mix_w_out", l)].reshape(D, D)),
            wq=uq_ext(cat_cols(have[("mla_w_uq", l)])).astype(F32), wkv=ukv_ext(cat_cols(have[("mla_w_ukv", l)])).astype(F32),
            ng=hgrn_norm_g[l][None], gq=mla_q_norm_g[l][None], gkv=mla_kv_norm_g[l][None],
            bcol=jnp.concatenate([fox_b_f[l], jnp.zeros((8 - HEADS,), F32)])[:, None],
            g_lg=gmlp_ln_g[l][None], g_lb=gmlp_ln_b[l][None], ws=gmlp_w_s[l], bs=gmlp_b_s[l][:, :, None])

    pending = []

    def emit(l, part, g):
        if part == "mix":
            names = mixers
            by_chip = [_col_shards(mix_in_unext(g["mix_in"])), mix_out_unext(g["mix_out"]).reshape(N_CHIPS, D // N_CHIPS, D),
                       _col_shards(uq_unext(g["wq"])).astype(BF16), _col_shards(ukv_unext(g["wkv"])).astype(BF16)]
        else:
            names = [part + "_w_in", part + "_w_out"]
            by_chip = [g[part + "_in"], g[part + "_out"]]
        tag = "%d_%s" % (l, part)
        got = sibling_swap(by_chip, "sibling_swap_" + tag)
        chip_sum = [add_kept_half(a, r, core, "add_sibling") for a, r in zip(by_chip, got)]
        zones = [lax.dynamic_update_slice(lax.empty(h.shape, h.dtype), lax.dynamic_slice_in_dim(h, chip, 1, axis=0), (chip, 0, 0))
                 for h in chip_sum]
        handle = rs_start(chip_sum, zones, chip_sum[0], "rs_start_" + tag)
        pending.append((l, names, handle, tag))
        return handle[-1][0, 0]

    loss_tile, dx, dmods, grads, d_logits = local_step(
        x.reshape(T, D), c8, loss_target.reshape(T, D), get, hgrn_lb_logits, S, emit)
    loss = lax.psum(loss_tile[0, 0], ("x", "y", "c"))

    small_g = {"hgrn_lb_logits": d_logits,
               "hgrn_norm_g": jnp.stack([grads[l]["ng"][0] for l in range(DEPTH)]),
               "mla_q_norm_g": jnp.stack([grads[l]["gq"][0] for l in range(DEPTH)]),
               "mla_kv_norm_g": jnp.stack([grads[l]["gkv"][0] for l in range(DEPTH)]),
               "fox_b_f": jnp.stack([grads[l]["bcol"][0:HEADS, 0] for l in range(DEPTH)]),
               "gmlp_ln_g": jnp.stack([grads[l]["g_lg"][0] for l in range(DEPTH)]),
               "gmlp_ln_b": jnp.stack([grads[l]["g_lb"][0] for l in range(DEPTH)]),
               "gmlp_w_s": jnp.stack([grads[l]["ws"] for l in range(DEPTH)]),
               "gmlp_b_s": jnp.stack([grads[l]["bs"][:, :, 0] for l in range(DEPTH)])}
    small_g["ln_g"] = jnp.stack([grads[l]["ln_g"] for l in range(DEPTH)])
    small_g["ln_b"] = jnp.stack([grads[l]["ln_b"] for l in range(DEPTH)])
    pk_small = pack_small(small_g)
    n_small = pk_small.shape[0]
    extras = [dmods[l] for l in range(DEPTH)] + [c]
    n_extra = _round_up(-(-sum(int(np.prod(e.shape)) for e in extras) // D), 8)
    gathered = ag_all(jnp.concatenate([pk_small, _rows(extras, n_extra, F32)], axis=0))
    g_small = unpack_small(sum_slots(gathered[:, 0:n_small], 8, "sum_small"), small_g)
    ext = gathered[:, n_small:].reshape(8, -1)
    n_dmod = DEPTH * Bl * N_MOD * D
    dmod_all = ext[:, 0:n_dmod].reshape(8, DEPTH, Bl, N_MOD * D)
    c_all = ext[:, n_dmod:n_dmod + Bl * D].reshape(8 * Bl, D)
    g_ada_w, g_ada_b = [], []
    ncol = N_MOD * D // N_CHIPS
    for l in range(DEPTH):
        dm = dmod_all[:, l].reshape(8 * Bl, N_MOD * D)
        g_ada_w.append(ada_grad(c_all, lax.dynamic_slice_in_dim(dm, chip * ncol, ncol, axis=1)))
        g_ada_b.append(sum_slots(dm.reshape(8 * Bl, N_MOD, D), 8 * Bl, "sum_ada_b").reshape(N_MOD * D))
    g_ada_w, g_ada_b = jnp.stack(g_ada_w), jnp.stack(g_ada_b)

    red = {n: lax.empty(w[n].shape, F32) for n in REDUCED}

    def arrive(entry, after):
        l, names, handle, tag = entry
        for n, land in zip(names, rs_wait(handle, after, "rs_wait_" + tag)[1]):
            red[n] = sum_into(land, red[n], l, core, "sum_chips")

    for entry in pending[:-1]:
        arrive(entry, dx)
    late = pending[-1][1]
    early = [n for n in REDUCED if n not in late]
    grad = dict(zip(early, sibling_join([red[n] for n in early], "sibling_join_a")))
    grad.update(g_small)
    grad["ada_w"], grad["ada_b"] = g_ada_w, g_ada_b
    for n in ("ln_g", "ln_b"):
        grad[n] = lax.dynamic_slice_in_dim(g_small[n], chip * (D // N_CHIPS), D // N_CHIPS, axis=2)
    out = {"grad": grad, "delta": {}, "new_m": {}, "new_v": {}}

    def update(n):
        shp = w[n].shape
        two_d = (-1, shp[-1])
        res = adamw(w[n].reshape(two_d), grad[n].reshape(two_d), m[n].reshape(two_d), v[n].reshape(two_d), "adamw_" + n,
                    echo=n in REDUCED)
        grad[n] = (res[3] if n in REDUCED else grad[n]).reshape(shp)
        for key, r in zip(("delta", "new_m", "new_v"), res):
            out[key][n] = r.reshape(shp)

    for n in WEIGHTS:
        if n not in late:
            update(n)
    arrive(pending[-1], out["delta"]["ffn2_w_in"])
    grad.update(zip(late, sibling_join([red[n] for n in late], "sibling_join_b")))
    for n in late:
        update(n)
    outs = [loss, dx.reshape(Bl, S, D)]
    for key in ("grad", "delta", "new_m", "new_v"):
        outs += [out[key][n] for n in WEIGHTS]
    return tuple(outs)
```

```python
import functools

import jax
import jax.numpy as jnp
import numpy as np
from jax import lax
from jax.experimental import pallas as pl
from jax.experimental.pallas import tpu as pltpu

F32, BF16 = jnp.float32, jnp.bfloat16
MESH = pl.DeviceIdType.MESH

N_CHIPS = 4
D = 1024
DEPTH = 2
FF = 2816
N_MOD = 9
GW = 256
HEADS = 4
HD = 64
HP = 128
A_CHUNK = 16
LB_FLOOR = 1e-30
B_Q_LORA, B_KV_LORA, B_NOPE, B_ROPE = 256, 128, 64, 32
ROPE_THETA = 10000.0
D_CHUNK = 128
ALPHA = (2 * DEPTH) ** 0.25
LN_EPS = 1e-5
RMS_EPS = 1e-6
ADAM_LR, ADAM_B1, ADAM_B2, ADAM_EPS, ADAM_WD, ADAM_STEP = 0.001, 0.9, 0.999, 1e-08, 0.01, 10

NP = 3840
NP_TILE = 1920
C_A, C_B, C_CQ, C_CK, C_CV, C_D, C_CF = 0, 1024, 1536, 2048, 2560, 3072, 3584
NCAT = 1536
O_A, O_B, O_C, O_D = 0, 256, 768, 1280

VMEM_BIG = 48 << 20
VMEM_MOST = 58 << 20


def _cparams(vmem=None):
    return pltpu.CompilerParams(vmem_limit_bytes=vmem) if vmem else pltpu.CompilerParams()


def _sds(shape, dtype):
    return jax.ShapeDtypeStruct(tuple(shape), dtype)


@jax.custom_vjp
def _mm(a, w):
    return jnp.dot(a.astype(BF16), w.astype(BF16), preferred_element_type=F32)


def _mm_f(a, w):
    return _mm(a, w), (a, w)


def _mm_b(res, g):
    a, w = res
    gb = g.astype(BF16)
    da = lax.dot_general(gb, w.astype(BF16), (((1,), (1,)), ((), ())), preferred_element_type=F32)
    dw = lax.dot_general(a.astype(BF16), gb, (((0,), (0,)), ((), ())), preferred_element_type=F32)
    return da.astype(a.dtype), dw.astype(w.dtype)


_mm.defvjp(_mm_f, _mm_b)


@jax.custom_vjp
def _mm_nt(a, b):
    return lax.dot_general(a.astype(BF16), b.astype(BF16), (((1,), (1,)), ((), ())), preferred_element_type=F32)


def _mm_nt_f(a, b):
    return _mm_nt(a, b), (a, b)


def _mm_nt_b(res, g):
    a, b = res
    gb = g.astype(BF16)
    da = jnp.dot(gb, b.astype(BF16), preferred_element_type=F32)
    db = lax.dot_general(gb, a.astype(BF16), (((0,), (0,)), ((), ())), preferred_element_type=F32)
    return da.astype(a.dtype), db.astype(b.dtype)


_mm_nt.defvjp(_mm_nt_f, _mm_nt_b)


@jax.custom_vjp
def _mm_tn(a, b):
    return lax.dot_general(a.astype(BF16), b.astype(BF16), (((0,), (0,)), ((), ())), preferred_element_type=F32)


def _mm_tn_f(a, b):
    return _mm_tn(a, b), (a, b)


def _mm_tn_b(res, g):
    a, b = res
    gb = g.astype(BF16)
    da = lax.dot_general(b.astype(BF16), gb, (((1,), (1,)), ((), ())), preferred_element_type=F32)
    db = jnp.dot(a.astype(BF16), gb, preferred_element_type=F32)
    return da.astype(a.dtype), db.astype(b.dtype)


_mm_tn.defvjp(_mm_tn_f, _mm_tn_b)


def _split3(x):
    p1 = x.astype(BF16)
    r = x - p1.astype(F32)
    p2 = r.astype(BF16)
    return p1, p2, (r - p2.astype(F32)).astype(BF16)


@jax.custom_vjp
def _sel_r(x, sel):
    s = sel.astype(BF16)
    return sum(jnp.dot(p, s, preferred_element_type=F32) for p in _split3(x))


def _sel_r_f(x, sel):
    return _sel_r(x, sel), sel


def _sel_r_b(sel, g):
    s = sel.astype(BF16)
    dx = sum(lax.dot_general(p, s, (((1,), (1,)), ((), ())), preferred_element_type=F32) for p in _split3(g))
    return dx, jnp.zeros_like(sel)


_sel_r.defvjp(_sel_r_f, _sel_r_b)


@jax.custom_vjp
def _sel_l(sel, x):
    s = sel.astype(BF16)
    return sum(jnp.dot(s, p, preferred_element_type=F32) for p in _split3(x))


def _sel_l_f(sel, x):
    return _sel_l(sel, x), sel


def _sel_l_b(sel, g):
    s = sel.astype(BF16)
    dx = sum(lax.dot_general(s, p, (((0,), (0,)), ((), ())), preferred_element_type=F32) for p in _split3(g))
    return jnp.zeros_like(sel), dx


_sel_l.defvjp(_sel_l_f, _sel_l_b)


def _iota(shape, dim):
    return lax.broadcasted_iota(jnp.int32, shape, dim)


def _head_sum_mats():
    e = (_iota((GW, HP), 0) // HD == _iota((GW, HP), 1)).astype(F32)
    et = (_iota((HP, GW), 1) // HD == _iota((HP, GW), 0)).astype(F32)
    return e, et


def _modulate(x, mod_ref, sh, sc):
    return x * (1.0 + mod_ref[sc:sc + 1, :]) + mod_ref[sh:sh + 1, :]


def _ln_res(x, y, gate, lg, lb, gs):
    r = ALPHA * x + gs * (1.0 + gate) * y
    mu = jnp.mean(r, axis=-1, keepdims=True)
    var = jnp.mean(jnp.square(r - mu), axis=-1, keepdims=True)
    return (r - mu) * lax.rsqrt(var + LN_EPS) * lg + lb


def _rms(x, g):
    return x * lax.rsqrt(jnp.mean(x * x, axis=-1, keepdims=True) + RMS_EPS) * g


def _tile(n, pref):
    return pref if n % pref == 0 else n


def mod_fwd(c8, w, l, b):
    tn = w.shape[3]
    n = N_CHIPS * tn

    def body(c_ref, w_ref, b_ref, o_ref):
        h = jax.nn.silu(c_ref[...]).astype(BF16)
        o_ref[...] = jnp.dot(h, w_ref[...], preferred_element_type=F32) + b_ref[...]

    return pl.pallas_call(
        body, name="mod_fwd", grid=(N_CHIPS,),
        in_specs=[pl.BlockSpec((8, D), lambda j: (0, 0)), pl.BlockSpec((None, None, D, tn), lambda j: (l, j, 0, 0)),
                  pl.BlockSpec((1, tn), lambda j: (0, j))],
        out_specs=pl.BlockSpec((8, tn), lambda j: (0, j)), out_shape=_sds((8, n), F32),
        compiler_params=_cparams(VMEM_BIG))(c8, w, b)


def ffn_in_fwd(x, mod, w_in, l, sh, sc, S):
    T = x.shape[0]
    tm, tn = _tile(S, 1024), FF // 2
    tpb, nj = S // tm, 2

    def body(x_ref, mod_ref, wg_ref, wu_ref, zg_ref, zu_ref, act_ref, h_ref):
        @pl.when(pl.program_id(1) == 0)
        def _():
            h_ref[...] = _modulate(x_ref[...], mod_ref, sh, sc).astype(BF16)
        g = jnp.dot(h_ref[...], wg_ref[...], preferred_element_type=F32)
        u = jnp.dot(h_ref[...], wu_ref[...], preferred_element_type=F32)
        zg_ref[...] = g.astype(BF16)
        zu_ref[...] = u.astype(BF16)
        act_ref[...] = (jax.nn.silu(g) * u).astype(BF16)

    return pl.pallas_call(
        body, name="ffn_in_fwd", grid=(T // tm, nj),
        in_specs=[pl.BlockSpec((tm, D), lambda i, j: (i, 0)),
                  pl.BlockSpec((None, N_MOD, D), lambda i, j: (i // tpb, 0, 0)),
                  pl.BlockSpec((None, None, D, tn), lambda i, j: (l, j, 0, 0)),
                  pl.BlockSpec((None, None, D, tn), lambda i, j: (l, j + nj, 0, 0))],
        out_specs=[pl.BlockSpec((tm, tn), lambda i, j: (i, j))] * 3,
        out_shape=[_sds((T, FF), BF16)] * 3,
        scratch_shapes=[pltpu.VMEM((tm, D), BF16)],
        compiler_params=_cparams(VMEM_BIG))(x, mod, w_in, w_in)


def mix_in_fwd(x, mod, w, sh, sc, S):
    T = x.shape[0]
    n = w.shape[1]
    tm, tn = _tile(S, 1024), NP_TILE
    tpb = S // tm

    def body(x_ref, mod_ref, w_ref, o_ref, h_ref):
        @pl.when(pl.program_id(1) == 0)
        def _():
            h_ref[...] = _modulate(x_ref[...], mod_ref, sh, sc).astype(BF16)
        o_ref[...] = jnp.dot(h_ref[...], w_ref[...], preferred_element_type=F32)

    return pl.pallas_call(
        body, name="mix_in_fwd", grid=(T // tm, n // tn),
        in_specs=[pl.BlockSpec((tm, D), lambda i, j: (i, 0)),
                  pl.BlockSpec((None, N_MOD, D), lambda i, j: (i // tpb, 0, 0)),
                  pl.BlockSpec((D, tn), lambda i, j: (0, j))],
        out_specs=pl.BlockSpec((tm, tn), lambda i, j: (i, j)), out_shape=_sds((T, n), F32),
        scratch_shapes=[pltpu.VMEM((tm, D), BF16)],
        compiler_params=_cparams(VMEM_BIG))(x, mod, w)


def out_ln_fwd(act, w_out, x, mod, lg, lb, gate, gs, S, l=None):
    T, K = act.shape
    tm = _tile(S, 512)
    tpb = S // tm

    def body(a_ref, w_ref, x_ref, mod_ref, lg_ref, lb_ref, y_ref, xn_ref):
        y = jnp.dot(a_ref[...], w_ref[...].reshape(K, D), preferred_element_type=F32)
        y_ref[...] = y
        xn_ref[...] = _ln_res(x_ref[...], y, mod_ref[gate:gate + 1, :], lg_ref[...], lb_ref[...], gs)

    if l is None:
        w_spec = pl.BlockSpec((K, D), lambda i: (0, 0))
    else:
        w_spec = pl.BlockSpec((None, N_CHIPS, K // N_CHIPS, D), lambda i: (l, 0, 0, 0))
    return pl.pallas_call(
        body, name="out_ln_fwd", grid=(T // tm,),
        in_specs=[pl.BlockSpec((tm, K), lambda i: (i, 0)), w_spec,
                  pl.BlockSpec((tm, D), lambda i: (i, 0)),
                  pl.BlockSpec((None, N_MOD, D), lambda i: (i // tpb, 0, 0)),
                  pl.BlockSpec((1, D), lambda i: (0, 0)), pl.BlockSpec((1, D), lambda i: (0, 0))],
        out_specs=[pl.BlockSpec((tm, D), lambda i: (i, 0))] * 2,
        out_shape=[_sds((T, D), F32), _sds((T, D), F32)],
        compiler_params=_cparams(VMEM_BIG))(act, w_out, x, mod, lg, lb)


def ln_res_bwd(dxn, x, y, mod, lg, lb, gate, gs, S):
    T = x.shape[0]
    B = T // S
    tm = _tile(S, 512)
    tpb = S // tm

    def body(d_ref, x_ref, y_ref, mod_ref, lg_ref, lb_ref, dx_ref, dy_ref, dg_ref, dlg_ref, dlb_ref):
        i = pl.program_id(0)
        f = functools.partial(_ln_res, gs=gs)
        _, vjp = jax.vjp(f, x_ref[...], y_ref[...], mod_ref[gate:gate + 1, :], lg_ref[...], lb_ref[...])
        dx, dy, dg, dlg, dlb = vjp(d_ref[...])
        dx_ref[...] = dx
        dy_ref[...] = dy.astype(BF16)

        @pl.when(i % tpb == 0)
        def _():
            dg_ref[...] = jnp.zeros_like(dg_ref)

        @pl.when(i == 0)
        def _():
            dlg_ref[...] = jnp.zeros_like(dlg_ref)
            dlb_ref[...] = jnp.zeros_like(dlb_ref)

        dg_ref[...] += dg
        dlg_ref[...] += dlg
        dlb_ref[...] += dlb

    tok = pl.BlockSpec((tm, D), lambda i: (i, 0))
    vec = pl.BlockSpec((1, D), lambda i: (0, 0))
    return pl.pallas_call(
        body, name="ln_res_bwd", grid=(T // tm,),
        in_specs=[tok, tok, tok, pl.BlockSpec((None, N_MOD, D), lambda i: (i // tpb, 0, 0)), vec, vec],
        out_specs=[tok, tok, pl.BlockSpec((None, 1, D), lambda i: (i // tpb, 0, 0)), vec, vec],
        out_shape=[_sds((T, D), F32), _sds((T, D), BF16), _sds((B, 1, D), F32), _sds((1, D), F32), _sds((1, D), F32)],
        compiler_params=_cparams(VMEM_BIG))(dxn, x, y, mod, lg, lb)


def swiglu_bwd(dy, w_out, l, zg, zu, S):
    T = dy.shape[0]
    tm, tn = _tile(S, 1024), FF // 2

    def body(dy_ref, w_ref, zg_ref, zu_ref, dg_ref, du_ref):
        da = lax.dot_general(dy_ref[...], w_ref[...].reshape(tn, D), (((1,), (1,)), ((), ())), preferred_element_type=F32)
        g, u = zg_ref[...].astype(F32), zu_ref[...].astype(F32)
        sg = jax.nn.sigmoid(g)
        dg_ref[...] = (da * u * (sg * (1.0 + g * (1.0 - sg)))).astype(BF16)
        du_ref[...] = (da * (g * sg)).astype(BF16)

    zt = pl.BlockSpec((tm, tn), lambda i, j: (i, j))
    return pl.pallas_call(
        body, name="swiglu_bwd", grid=(T // tm, FF // tn),
        in_specs=[pl.BlockSpec((tm, D), lambda i, j: (i, 0)),
                  pl.BlockSpec((None, 2, FF // N_CHIPS, D), lambda i, j: (l, j, 0, 0)), zt, zt],
        out_specs=[zt, zt], out_shape=[_sds((T, FF), BF16), _sds((T, FF), BF16)],
        compiler_params=_cparams(VMEM_BIG))(dy, w_out, zg, zu)


def nt_plain(dy, w):
    T = dy.shape[0]
    K = w.shape[0]
    tm = _tile(T, 1024)

    def body(dy_ref, w_ref, o_ref):
        o_ref[...] = lax.dot_general(dy_ref[...], w_ref[...], (((1,), (1,)), ((), ())), preferred_element_type=F32)

    return pl.pallas_call(
        body, name="nt_plain", grid=(T // tm,),
        in_specs=[pl.BlockSpec((tm, D), lambda i: (i, 0)), pl.BlockSpec((K, D), lambda i: (0, 0))],
        out_specs=pl.BlockSpec((tm, K), lambda i: (i, 0)), out_shape=_sds((T, K), F32),
        compiler_params=_cparams(VMEM_BIG))(dy, w)


def _tn_step(acc, o_ref, lhs, rhs, t, nt):
    part = lax.dot_general(lhs, rhs, (((0,), (0,)), ((), ())), preferred_element_type=F32)
    if nt == 1:
        o_ref[...] = part.astype(o_ref.dtype)
        return

    @pl.when(t == 0)
    def _():
        acc[...] = part

    @pl.when((t > 0) & (t < nt - 1))
    def _():
        acc[...] += part

    @pl.when(t == nt - 1)
    def _():
        o_ref[...] = (acc[...] + part).astype(o_ref.dtype)


def tn_mm(a, b, tk):
    T, K = a.shape
    N = b.shape[1]
    tt = _tile(T, 2048)
    nt = T // tt

    def body(a_ref, b_ref, o_ref, acc):
        _tn_step(acc, o_ref, a_ref[...], b_ref[...], pl.program_id(1), nt)

    return pl.pallas_call(
        body, name="tn_mm", grid=(K // tk, nt),
        in_specs=[pl.BlockSpec((tt, tk), lambda k, t: (t, k)), pl.BlockSpec((tt, N), lambda k, t: (t, 0))],
        out_specs=pl.BlockSpec((tk, N), lambda k, t: (k, 0)), out_shape=_sds((K, N), BF16),
        scratch_shapes=[pltpu.VMEM((tk, N), F32)], compiler_params=_cparams(VMEM_BIG))(a, b)


def tn_mm_mod(x, mod, b, sh, sc, S, tn):
    T = x.shape[0]
    N = b.shape[1]
    tt = _tile(S, 1024)
    tpb = S // tt
    nt = T // tt

    def body(x_ref, mod_ref, b_ref, o_ref, acc):
        h = _modulate(x_ref[...], mod_ref, sh, sc).astype(BF16)
        _tn_step(acc, o_ref, h, b_ref[...], pl.program_id(1), nt)

    return pl.pallas_call(
        body, name="tn_mm_mod", grid=(N // tn, nt),
        in_specs=[pl.BlockSpec((tt, D), lambda j, t: (t, 0)),
                  pl.BlockSpec((None, N_MOD, D), lambda j, t: (t // tpb, 0, 0)),
                  pl.BlockSpec((tt, tn), lambda j, t: (t, j))],
        out_specs=pl.BlockSpec((D, tn), lambda j, t: (0, j)), out_shape=_sds((D, N), BF16),
        scratch_shapes=[pltpu.VMEM((D, tn), F32)], compiler_params=_cparams(VMEM_BIG))(x, mod, b)


def tn_mm_mod_shards(x, mod, bg, bu, sh, sc, S):
    T = x.shape[0]
    tn = FF // 2
    tt = _tile(S, 1024)
    tpb = S // tt
    nt = T // tt

    def body(x_ref, mod_ref, bg_ref, bu_ref, o_ref, acc):
        j, t = pl.program_id(0), pl.program_id(1)
        h = _modulate(x_ref[...], mod_ref, sh, sc).astype(BF16)

        @pl.when(j < 2)
        def _():
            _tn_step(acc, o_ref, h, bg_ref[...], t, nt)

        @pl.when(j >= 2)
        def _():
            _tn_step(acc, o_ref, h, bu_ref[...], t, nt)

    return pl.pallas_call(
        body, name="tn_mm_mod_shards", grid=(N_CHIPS, nt),
        in_specs=[pl.BlockSpec((tt, D), lambda j, t: (t, 0)),
                  pl.BlockSpec((None, N_MOD, D), lambda j, t: (t // tpb, 0, 0)),
                  pl.BlockSpec((tt, tn), lambda j, t: (jnp.where(j < 2, t, 0), jnp.minimum(j, 1))),
                  pl.BlockSpec((tt, tn), lambda j, t: (jnp.where(j < 2, 0, t), jnp.maximum(j - 2, 0)))],
        out_specs=pl.BlockSpec((None, D, tn), lambda j, t: (j, 0, 0)), out_shape=_sds((N_CHIPS, D, tn), BF16),
        scratch_shapes=[pltpu.VMEM((D, tn), F32)], compiler_params=_cparams(VMEM_BIG))(x, mod, bg, bu)


def nt_mod_bwd(ds, w, offs, x, mod, dres, sc, S, tk, l=None):
    T = x.shape[0]
    B = T // S
    tm = _tile(S, 1024)
    tpb = S // tm
    Kd = ds[0].shape[1]
    nk = Kd // tk
    n_in = len(ds)

    def body(*refs):
        d_refs, w_refs = refs[:n_in], refs[n_in:2 * n_in]
        x_ref, mod_ref, r_ref, dx_ref, dsh_ref, dsc_ref, acc = refs[2 * n_in:]
        i, k = pl.program_id(0), pl.program_id(1)

        part = sum(lax.dot_general(d_ref[...], w_ref[...], (((1,), (1,)), ((), ())), preferred_element_type=F32)
                   for d_ref, w_ref in zip(d_refs, w_refs))

        @pl.when(k == 0)
        def _():
            acc[...] = part

        @pl.when(k > 0)
        def _():
            acc[...] += part

        @pl.when(k == nk - 1)
        def _():
            dh = acc[...]
            dx_ref[...] = dh * (1.0 + mod_ref[sc:sc + 1, :]) + r_ref[...]

            @pl.when(i % tpb == 0)
            def _():
                dsh_ref[...] = jnp.zeros_like(dsh_ref)
                dsc_ref[...] = jnp.zeros_like(dsc_ref)

            dsh_ref[...] += jnp.sum(dh, axis=0, keepdims=True)
            dsc_ref[...] += jnp.sum(dh * x_ref[...], axis=0, keepdims=True)

    tok = pl.BlockSpec((tm, D), lambda i, k: (i, 0))
    vec = pl.BlockSpec((None, 1, D), lambda i, k: (i // tpb, 0, 0))
    in_specs = [pl.BlockSpec((tm, tk), lambda i, k: (i, k)) for _ in ds]
    if l is None:
        in_specs += [pl.BlockSpec((D, tk), functools.partial(lambda i, k, o: (0, k + o), o=off // tk)) for off in offs]
    else:
        in_specs += [pl.BlockSpec((None, None, D, tk), functools.partial(lambda i, k, o: (l, k + o, 0, 0), o=off)) for off in offs]
    in_specs += [tok, pl.BlockSpec((None, N_MOD, D), lambda i, k: (i // tpb, 0, 0)), tok]
    return pl.pallas_call(
        body, name="nt_mod_bwd", grid=(T // tm, nk), in_specs=in_specs,
        out_specs=[tok, vec, vec],
        out_shape=[_sds((T, D), F32), _sds((B, 1, D), F32), _sds((B, 1, D), F32)],
        scratch_shapes=[pltpu.VMEM((tm, D), F32)],
        compiler_params=_cparams(VMEM_MOST))(*ds, *([w] * n_in), x, mod, dres)


def loss_head(y, tgt):
    T = y.shape[0]
    tm = _tile(T, 512)

    def body(y_ref, t_ref, l_ref, d_ref):
        @pl.when(pl.program_id(0) == 0)
        def _():
            l_ref[...] = jnp.zeros_like(l_ref)
        e = y_ref[...] - t_ref[...]
        d_ref[...] = e * (1.0 / D)
        l_ref[...] += 0.5 * jnp.sum(jnp.sum(e * e, axis=1, keepdims=True) * (1.0 / D))

    tok = pl.BlockSpec((tm, D), lambda i: (i, 0))
    return pl.pallas_call(
        body, name="loss_head", grid=(T // tm,), in_specs=[tok, tok],
        out_specs=[pl.BlockSpec((8, 128), lambda i: (0, 0)), tok],
        out_shape=[_sds((8, 128), F32), _sds((T, D), F32)],
        compiler_params=_cparams(VMEM_BIG))(y, tgt)


def _hgrn_block(q, fz, inp, go, st, lb, ng, blk):
    nc = blk // A_CHUNK
    lb_eff = jnp.maximum(lb, LB_FLOOR)
    log_f = jnp.logaddexp(jnp.log(lb_eff), jnp.log1p(-lb) + jax.nn.log_sigmoid(fz))
    k = (1.0 - lb) * jax.nn.sigmoid(-fz) - (lb_eff - lb)
    qf = jax.nn.silu(q)
    same_chunk = _iota((blk, blk), 0) // A_CHUNK == _iota((blk, blk), 1) // A_CHUNK
    tril = (same_chunk & (_iota((blk, blk), 1) <= _iota((blk, blk), 0))).astype(F32)
    G = _sel_l(tril, log_f)
    e_mat, et_mat = _head_sum_mats()
    G4, q4, k4, v4 = (z.reshape(nc, A_CHUNK, GW) for z in (G, qf, k, inp))
    shp = (nc, A_CHUNK, A_CHUNK, GW)
    one = (1, A_CHUNK, A_CHUNK, GW)
    mask = jnp.where(_iota(one, 2) <= _iota(one, 1), 0.0, -jnp.inf)
    decay = jnp.exp((G4[:, :, None, :] - G4[:, None, :, :]) + mask)
    prod = q4[:, :, None, :] * k4[:, None, :, :] * decay
    scores = _mm(prod.reshape(nc * A_CHUNK * A_CHUNK, GW), e_mat.astype(BF16))
    spread = _mm(scores, et_mat.astype(BF16)).reshape(shp)
    o_intra = jnp.sum(spread * v4[:, None, :, :], axis=2).reshape(blk, GW)
    head_diag = (_iota((GW, GW), 0) // HD == _iota((GW, GW), 1) // HD).astype(F32)
    g_last = [jnp.sum(log_f[c * A_CHUNK:(c + 1) * A_CHUNK], axis=0, keepdims=True) for c in range(nc)]
    g_last_b = jnp.concatenate([jnp.broadcast_to(g, (A_CHUNK, GW)) for g in g_last], axis=0)
    q_dec = qf * jnp.exp(G)
    k_end = k * jnp.exp(g_last_b - G)
    outs = []
    for c in range(nc):
        rows = slice(c * A_CHUNK, (c + 1) * A_CHUNK)
        outs.append(_mm_nt(q_dec[rows], st))
        st = st * jnp.exp(g_last[c]) + _mm_tn(inp[rows], k_end[rows]) * head_diag
    o = o_intra + jnp.concatenate(outs, axis=0)
    ms = _sel_r(o * o, e_mat) * (1.0 / HD)
    o = o * _sel_r(lax.rsqrt(ms + RMS_EPS), et_mat) * ng
    return o * jax.nn.silu(go), st


HGRN_BLK = 128


def hgrn_fwd(proj, lb, ng, S):
    T = proj.shape[0]
    B = T // S
    blk = min(HGRN_BLK, S)
    nb = S // blk

    def body(p_ref, lb_ref, ng_ref, o_ref, st_out_ref, st_ref):
        @pl.when(pl.program_id(1) == 0)
        def _():
            st_ref[...] = jnp.zeros_like(st_ref)
        st_out_ref[...] = st_ref[...]
        p = p_ref[...]
        o, st = _hgrn_block(p[:, 0:GW], p[:, GW:2 * GW], p[:, 2 * GW:3 * GW], p[:, 3 * GW:4 * GW],
                            st_ref[...], lb_ref[...], ng_ref[...], blk)
        o_ref[...] = o.astype(BF16)
        st_ref[...] = st

    vec = pl.BlockSpec((1, GW), lambda b, j: (0, 0))
    return pl.pallas_call(
        body, name="hgrn_fwd", grid=(B, nb),
        in_specs=[pl.BlockSpec((blk, 4 * GW), lambda b, j: (b * nb + j, C_A // (4 * GW))), vec, vec],
        out_specs=[pl.BlockSpec((blk, GW), lambda b, j: (b * nb + j, 0)),
                   pl.BlockSpec((None, GW, GW), lambda b, j: (b * nb + j, 0, 0))],
        out_shape=[_sds((T, GW), BF16), _sds((B * nb, GW, GW), F32)],
        scratch_shapes=[pltpu.VMEM((GW, GW), F32)],
        compiler_params=_cparams(VMEM_BIG))(proj, lb, ng)


def hgrn_bwd(proj, states, dcat, lb, ng, S):
    T = proj.shape[0]
    B = T // S
    blk = min(HGRN_BLK, S)
    nb = S // blk

    def body(p_ref, st_in_ref, do_ref, lb_ref, ng_ref, dp_ref, dlb_ref, dng_ref, dst_ref):
        b, j = pl.program_id(0), pl.program_id(1)

        @pl.when(j == 0)
        def _():
            dst_ref[...] = jnp.zeros_like(dst_ref)

        @pl.when((b == 0) & (j == 0))
        def _():
            dlb_ref[...] = jnp.zeros_like(dlb_ref)
            dng_ref[...] = jnp.zeros_like(dng_ref)

        p = p_ref[...]
        f = functools.partial(_hgrn_block, blk=blk)
        _, vjp = jax.vjp(f, p[:, 0:GW], p[:, GW:2 * GW], p[:, 2 * GW:3 * GW], p[:, 3 * GW:4 * GW],
                         st_in_ref[...], lb_ref[...], ng_ref[...])
        dq, df, di, dg, dst, dlb, dng = vjp((do_ref[...], dst_ref[...]))
        dp_ref[...] = jnp.concatenate([dq, df, di, dg], axis=1).astype(BF16)
        dst_ref[...] = dst
        dlb_ref[...] += dlb
        dng_ref[...] += dng

    def rev(b, j):
        return b * nb + (nb - 1 - j)

    vec = pl.BlockSpec((1, GW), lambda b, j: (0, 0))
    return pl.pallas_call(
        body, name="hgrn_bwd", grid=(B, nb),
        in_specs=[pl.BlockSpec((blk, 4 * GW), lambda b, j: (rev(b, j), C_A // (4 * GW))),
                  pl.BlockSpec((None, GW, GW), lambda b, j: (rev(b, j), 0, 0)),
                  pl.BlockSpec((blk, GW), lambda b, j: (rev(b, j), O_A // GW)), vec, vec],
        out_specs=[pl.BlockSpec((blk, 4 * GW), lambda b, j: (rev(b, j), 0)), vec, vec],
        out_shape=[_sds((T, 4 * GW), BF16), _sds((1, GW), F32), _sds((1, GW), F32)],
        scratch_shapes=[pltpu.VMEM((GW, GW), F32)],
        compiler_params=_cparams(VMEM_BIG))(proj, states, dcat, lb, ng)


ATT_TQ = 256


ATT_BANDS = 8


def _attn_block(q, k, v, cum, qpos0, scale, use_cum, n_free):
    s = _mm_nt(q, k) * scale
    if use_cum:
        s = s - cum
    band = s[:, n_free:]
    visible = _iota(band.shape, 1) <= (qpos0 - n_free) + _iota(band.shape, 0)
    band = jnp.where(visible, band, -jnp.inf)
    m = jnp.max(band, axis=-1, keepdims=True)
    if n_free:
        free = s[:, :n_free]
        m = jnp.maximum(m, jnp.max(free, axis=-1, keepdims=True))
    e = jnp.exp(band - m)
    denom = jnp.sum(e, axis=-1, keepdims=True)
    o = _mm(e, v[n_free:])
    if n_free:
        e = jnp.exp(free - m)
        denom = denom + jnp.sum(e, axis=-1, keepdims=True)
        o = o + _mm(e, v[:n_free])
    return o * (1.0 / denom)


def _bands(S, tq):
    nq = S // tq
    nb = min(ATT_BANDS, nq)
    per = nq // nb
    return [(r * per, (r + 1) * per, (r + 1) * per * tq) for r in range(nb)]


def attn_fwd(qa, qo, ka, ko, va, vo, cum, scale, S):
    T = qa.shape[0]
    B = T // S
    tq = min(ATT_TQ, S)
    nq = S // tq
    use_cum = cum is not None

    def body(*refs):
        if use_cum:
            q_ref, k_ref, v_ref, c_ref, o_ref = refs
        else:
            (q_ref, k_ref, v_ref, o_ref), c_ref = refs, None
        h, i = pl.program_id(1), pl.program_id(2)
        for lo, hi, kw in _bands(S, tq):
            @pl.when((i >= lo) & (i < hi))
            def _():
                crow = c_ref[pl.ds(h, 1), 0:kw] if use_cum else None
                o = _attn_block(q_ref[...], k_ref[0:kw, :], v_ref[0:kw, :], crow, i * tq, scale, use_cum, lo * tq)
                o_ref[...] = o.astype(BF16)

    in_specs = [pl.BlockSpec((tq, HP), lambda b, h, i: (b * nq + i, qo + h)),
                pl.BlockSpec((S, HP), lambda b, h, i: (b, ko + h)),
                pl.BlockSpec((S, HP), lambda b, h, i: (b, vo + h))]
    args = [qa, ka, va]
    if use_cum:
        in_specs.append(pl.BlockSpec((None, 8, S), lambda b, h, i: (b, 0, 0)))
        args.append(cum)
    return pl.pallas_call(
        body, name="attn_fwd", grid=(B, HEADS, nq), in_specs=in_specs,
        out_specs=pl.BlockSpec((tq, HP), lambda b, h, i: (b * nq + i, h)),
        out_shape=_sds((T, HEADS * HP), BF16),
        compiler_params=_cparams(VMEM_BIG))(*args)


def _attn_block_bwd(q, k, v, cum, do, qpos0, scale, use_cum, n_free):
    tn = (((0,), (0,)), ((), ()))
    nt = (((1,), (1,)), ((), ()))
    qb, dob = q.astype(BF16), do.astype(BF16)
    kb, vb = k.astype(BF16), v.astype(BF16)
    s = lax.dot_general(qb, kb, nt, preferred_element_type=F32) * scale
    if use_cum:
        s = s - cum
    band = s[:, n_free:]
    visible = _iota(band.shape, 1) <= (qpos0 - n_free) + _iota(band.shape, 0)
    parts = [(jnp.where(visible, band, -jnp.inf), n_free, s.shape[1])]
    if n_free:
        parts.append((s[:, :n_free], 0, n_free))
    m = functools.reduce(jnp.maximum, [jnp.max(sp, axis=-1, keepdims=True) for sp, _, _ in parts])
    es = [jnp.exp(sp - m) for sp, _, _ in parts]
    rinv = 1.0 / sum(jnp.sum(e, axis=-1, keepdims=True) for e in es)
    ps = [e * rinv for e in es]
    dps = [lax.dot_general(dob, vb[a:b], nt, preferred_element_type=F32) for _, a, b in parts]
    delta = sum(jnp.sum(p * dp, axis=-1, keepdims=True) for p, dp in zip(ps, dps))
    dq = jnp.zeros(q.shape, F32)
    out = []
    for p, dp, (_, a, b) in zip(ps, dps, parts):
        ds = p * (dp - delta)
        dsb = ds.astype(BF16)
        dq = dq + jnp.dot(dsb, kb[a:b], preferred_element_type=F32)
        out.append((a, b, lax.dot_general(dsb, qb, tn, preferred_element_type=F32) * scale,
                    lax.dot_general(p.astype(BF16), dob, tn, preferred_element_type=F32),
                    -jnp.sum(ds, axis=0, keepdims=True) if use_cum else None))
    return dq * scale, out


def attn_bwd(qa, qo, ka, ko, va, vo, cum, dcat, do_off, scale, S, out_dtype):
    T = qa.shape[0]
    B = T // S
    tq = min(ATT_TQ, S)
    nq = S // tq
    use_cum = cum is not None

    def body(*refs):
        if use_cum:
            q_ref, k_ref, v_ref, do_ref, c_ref, dq_ref, dk_ref, dv_ref, dc_ref, dk_acc, dv_acc = refs
        else:
            q_ref, k_ref, v_ref, do_ref, dq_ref, dk_ref, dv_ref, dk_acc, dv_acc = refs
        h, i = pl.program_id(1), pl.program_id(2)

        @pl.when(i == 0)
        def _():
            dk_acc[...] = jnp.zeros_like(dk_acc)
            dv_acc[...] = jnp.zeros_like(dv_acc)
            if use_cum:
                dc_ref[...] = jnp.zeros_like(dc_ref)

        for lo, hi, kw in _bands(S, tq):
            @pl.when((i >= lo) & (i < hi))
            def _():
                crow = c_ref[pl.ds(h, 1), 0:kw] if use_cum else None
                dq, pieces = _attn_block_bwd(q_ref[...], k_ref[0:kw, :], v_ref[0:kw, :], crow, do_ref[...], i * tq,
                                             scale, use_cum, lo * tq)
                dq_ref[...] = dq.astype(out_dtype)
                for a, b, dk, dv, dc in pieces:
                    dk_acc[a:b, :] += dk
                    dv_acc[a:b, :] += dv
                    if use_cum:
                        dc_ref[:, a:b] += dc

        @pl.when(i == nq - 1)
        def _():
            dk_ref[...] = dk_acc[...].astype(out_dtype)
            dv_ref[...] = dv_acc[...].astype(out_dtype)

    qspec = pl.BlockSpec((tq, HP), lambda b, h, i: (b * nq + i, qo + h))
    in_specs = [qspec, pl.BlockSpec((S, HP), lambda b, h, i: (b, ko + h)),
                pl.BlockSpec((S, HP), lambda b, h, i: (b, vo + h)),
                pl.BlockSpec((tq, HP), lambda b, h, i: (b * nq + i, do_off + h))]
    args = [qa, ka, va, dcat]
    kv_out = pl.BlockSpec((S, HP), lambda b, h, i: (b, h))
    out_specs = [pl.BlockSpec((tq, HP), lambda b, h, i: (b * nq + i, h)), kv_out, kv_out]
    out_shape = [_sds((T, HEADS * HP), out_dtype)] * 3
    if use_cum:
        in_specs.append(pl.BlockSpec((None, 8, S), lambda b, h, i: (b, 0, 0)))
        args.append(cum)
        out_specs.append(pl.BlockSpec((None, 1, S), lambda b, h, i: (b * HEADS + h, 0, 0)))
        out_shape.append(_sds((B * HEADS, 1, S), F32))
    return pl.pallas_call(
        body, name="attn_bwd", grid=(B, HEADS, nq), in_specs=in_specs, out_specs=out_specs, out_shape=out_shape,
        scratch_shapes=[pltpu.VMEM((S, HP), F32), pltpu.VMEM((S, HP), F32)],
        compiler_params=_cparams(VMEM_BIG))(*args)


def _tri(n, upper):
    r, c = _iota((n, n), 0), _iota((n, n), 1)
    return ((r <= c) if upper else (r >= c)).astype(F32)


def fox_gate_fwd(proj, bcol, S):
    T = proj.shape[0]
    B = T // S
    ts = _tile(S, 512)
    nt = S // ts

    def body(p_ref, b_ref, o_ref, carry):
        @pl.when(pl.program_id(1) == 0)
        def _():
            carry[...] = jnp.zeros_like(carry)
        cf = jnp.transpose(p_ref[...])[0:8, :]
        lf = jax.nn.log_sigmoid(cf + b_ref[...])
        cum = _sel_r(lf, _tri(ts, True)) + carry[...]
        o_ref[...] = cum
        carry[...] += jnp.sum(lf, axis=1, keepdims=True)

    return pl.pallas_call(
        body, name="fox_gate_fwd", grid=(B, nt),
        in_specs=[pl.BlockSpec((ts, HP), lambda b, j: (b * nt + j, C_CF // HP)), pl.BlockSpec((8, 1), lambda b, j: (0, 0))],
        out_specs=pl.BlockSpec((None, 8, ts), lambda b, j: (b, 0, j)), out_shape=_sds((B, 8, S), F32),
        scratch_shapes=[pltpu.VMEM((8, 1), F32)],
        compiler_params=_cparams(VMEM_BIG))(proj, bcol)


def fox_gate_bwd(proj, bcol, dcum, S):
    T = proj.shape[0]
    B = T // S
    ts = _tile(S, 512)
    nt = S // ts

    def body(p_ref, b_ref, dc_ref, dp_ref, db_ref, carry):
        b, j = pl.program_id(0), pl.program_id(1)

        @pl.when(j == 0)
        def _():
            carry[...] = jnp.zeros_like(carry)

        @pl.when((b == 0) & (j == 0))
        def _():
            db_ref[...] = jnp.zeros_like(db_ref)

        cf = jnp.transpose(p_ref[...])[0:8, :]
        dc = dc_ref[...]
        dlf = _sel_r(dc, _tri(ts, False)) + carry[...]
        carry[...] += jnp.sum(dc, axis=1, keepdims=True)
        dcf = dlf * jax.nn.sigmoid(-(cf + b_ref[...]))
        db_ref[...] += jnp.sum(dcf, axis=1, keepdims=True)
        full = jnp.concatenate([dcf, jnp.zeros((HP - 8, ts), F32)], axis=0)
        dp_ref[...] = jnp.transpose(full).astype(BF16)

    def rev(b, j):
        return nt - 1 - j

    return pl.pallas_call(
        body, name="fox_gate_bwd", grid=(B, nt),
        in_specs=[pl.BlockSpec((ts, HP), lambda b, j: (b * nt + rev(b, j), C_CF // HP)),
                  pl.BlockSpec((8, 1), lambda b, j: (0, 0)),
                  pl.BlockSpec((None, 8, ts), lambda b, j: (b, 0, rev(b, j)))],
        out_specs=[pl.BlockSpec((ts, HP), lambda b, j: (b * nt + rev(b, j), 0)), pl.BlockSpec((8, 1), lambda b, j: (0, 0))],
        out_shape=[_sds((T, HP), BF16), _sds((8, 1), F32)],
        scratch_shapes=[pltpu.VMEM((8, 1), F32)],
        compiler_params=_cparams(VMEM_BIG))(proj, bcol, dcum)


def _mla_pre(blk, gq, gkv, wq, wkv, place, cos_q, sin_q, cs_k):
    nq = _rms(blk[:, 0:B_Q_LORA], gq)
    nkv = _rms(blk[:, B_Q_LORA:B_Q_LORA + B_KV_LORA], gkv)
    qq = _mm(nq, wq)
    q = qq[:, 0:HEADS * HP] * cos_q + qq[:, HEADS * HP:] * sin_q
    kv = _mm(nkv, wkv)
    k = kv[:, 0:HEADS * HP] + _mm(blk[:, B_Q_LORA + B_KV_LORA:] * cs_k, place)
    return q, k, kv[:, HEADS * HP:]


def mla_pre_fwd(proj, gq, gkv, wq, wkv, place, cos_q, sin_q, cs_k, S):
    T = proj.shape[0]
    tm = _tile(S, 512)
    tpb = S // tm
    W = HEADS * HP

    def body(p_ref, gq_ref, gkv_ref, wq_ref, wkv_ref, pl_ref, cq_ref, sq_ref, ck_ref, q_ref, k_ref, v_ref):
        q, k, v = _mla_pre(p_ref[...], gq_ref[...], gkv_ref[...], wq_ref[...], wkv_ref[...], pl_ref[...],
                           cq_ref[...], sq_ref[...], ck_ref[...])
        q_ref[...] = q.astype(BF16)
        k_ref[...] = k.astype(BF16)
        v_ref[...] = v.astype(BF16)

    def full(a):
        return pl.BlockSpec(a.shape, lambda i: (0,) * a.ndim)

    tok = pl.BlockSpec((tm, W), lambda i: (i, 0))
    return pl.pallas_call(
        body, name="mla_pre_fwd", grid=(T // tm,),
        in_specs=[pl.BlockSpec((tm, W), lambda i: (i, C_B // W)), full(gq), full(gkv), full(wq), full(wkv), full(place),
                  pl.BlockSpec((tm, W), lambda i: (i % tpb, 0)), pl.BlockSpec((tm, W), lambda i: (i % tpb, 0)),
                  pl.BlockSpec((tm, HP), lambda i: (i % tpb, 0))],
        out_specs=[tok] * 3, out_shape=[_sds((T, W), BF16)] * 3,
        compiler_params=_cparams(VMEM_BIG))(proj, gq, gkv, wq, wkv, place, cos_q, sin_q, cs_k)


def mla_pre_bwd(proj, gq, gkv, wq, wkv, place, cos_q, sin_q, cs_k, dq, dk, dv, S):
    T = proj.shape[0]
    tm = _tile(S, 512)
    tpb = S // tm
    W = HEADS * HP

    def body(p_ref, gq_ref, gkv_ref, wq_ref, wkv_ref, pl_ref, cq_ref, sq_ref, ck_ref, dq_ref, dk_ref, dv_ref,
             dp_ref, dgq_ref, dgkv_ref, dwq_ref, dwkv_ref):
        @pl.when(pl.program_id(0) == 0)
        def _():
            for r in (dgq_ref, dgkv_ref, dwq_ref, dwkv_ref):
                r[...] = jnp.zeros_like(r)

        f = functools.partial(_mla_pre, place=pl_ref[...], cos_q=cq_ref[...], sin_q=sq_ref[...], cs_k=ck_ref[...])
        _, vjp = jax.vjp(f, p_ref[...], gq_ref[...], gkv_ref[...], wq_ref[...], wkv_ref[...])
        dp, dgq, dgkv, dwq, dwkv = vjp((dq_ref[...], dk_ref[...], dv_ref[...]))
        dp_ref[...] = dp.astype(BF16)
        dgq_ref[...] += dgq
        dgkv_ref[...] += dgkv
        dwq_ref[...] += dwq
        dwkv_ref[...] += dwkv

    def full(a):
        return pl.BlockSpec(a.shape, lambda i: (0,) * a.ndim)

    tok = pl.BlockSpec((tm, W), lambda i: (i, 0))
    return pl.pallas_call(
        body, name="mla_pre_bwd", grid=(T // tm,),
        in_specs=[pl.BlockSpec((tm, W), lambda i: (i, C_B // W)), full(gq), full(gkv), full(wq), full(wkv), full(place),
                  pl.BlockSpec((tm, W), lambda i: (i % tpb, 0)), pl.BlockSpec((tm, W), lambda i: (i % tpb, 0)),
                  pl.BlockSpec((tm, HP), lambda i: (i % tpb, 0)), tok, tok, tok],
        out_specs=[tok, full(gq), full(gkv), full(wq), full(wkv)],
        out_shape=[_sds((T, W), BF16), _sds(gq.shape, F32), _sds(gkv.shape, F32), _sds(wq.shape, F32), _sds(wkv.shape, F32)],
        compiler_params=_cparams(VMEM_BIG))(proj, gq, gkv, wq, wkv, place, cos_q, sin_q, cs_k, dq, dk, dv)


GMLP_CHUNKS = 4


def _gmlp_block(blk, lg, lb, ws, bs):
    u = jax.nn.gelu(blk[:, 0:GW])
    v = jax.nn.gelu(blk[:, GW:2 * GW])
    mu = jnp.mean(v, axis=-1, keepdims=True)
    var = jnp.mean(jnp.square(v - mu), axis=-1, keepdims=True)
    vn = (v - mu) * lax.rsqrt(var + LN_EPS) * lg + lb
    causal = _iota((D_CHUNK, D_CHUNK), 1) <= _iota((D_CHUNK, D_CHUNK), 0)
    group = _iota((1, GW), 1) // HD
    w = [jnp.where(causal, ws[g], 0.0) for g in range(HEADS)]
    chunks = []
    for c in range(blk.shape[0] // D_CHUNK):
        vc = vn[c * D_CHUNK:(c + 1) * D_CHUNK]
        mixed = jnp.zeros((D_CHUNK, GW), F32)
        for g in range(HEADS):
            mixed = mixed + jnp.where(group == g, _mm(w[g], vc) + bs[g], 0.0)
        chunks.append(mixed)
    return u * jnp.concatenate(chunks, axis=0)


def _gmlp_tile(T):
    return _tile(T, GMLP_CHUNKS * D_CHUNK) if T % (GMLP_CHUNKS * D_CHUNK) == 0 else D_CHUNK


def gmlp_fwd(proj, lg, lb, ws, bs):
    T = proj.shape[0]
    tm = _gmlp_tile(T)

    def body(p_ref, lg_ref, lb_ref, ws_ref, bs_ref, o_ref):
        o_ref[...] = _gmlp_block(p_ref[...], lg_ref[...], lb_ref[...], ws_ref[...], bs_ref[...]).astype(BF16)

    def full(a):
        return pl.BlockSpec(a.shape, lambda i: (0,) * a.ndim)

    return pl.pallas_call(
        body, name="gmlp_fwd", grid=(T // tm,),
        in_specs=[pl.BlockSpec((tm, 2 * GW), lambda i: (i, C_D // (2 * GW))), full(lg), full(lb), full(ws), full(bs)],
        out_specs=pl.BlockSpec((tm, GW), lambda i: (i, 0)), out_shape=_sds((T, GW), BF16),
        compiler_params=_cparams(VMEM_BIG))(proj, lg, lb, ws, bs)


def gmlp_bwd(proj, lg, lb, ws, bs, dcat):
    T = proj.shape[0]
    tm = _gmlp_tile(T)

    def body(p_ref, lg_ref, lb_ref, ws_ref, bs_ref, do_ref, dp_ref, dlg_ref, dlb_ref, dws_ref, dbs_ref):
        @pl.when(pl.program_id(0) == 0)
        def _():
            for r in (dlg_ref, dlb_ref, dws_ref, dbs_ref):
                r[...] = jnp.zeros_like(r)

        _, vjp = jax.vjp(_gmlp_block, p_ref[...], lg_ref[...], lb_ref[...], ws_ref[...], bs_ref[...])
        dp, dlg, dlb, dws, dbs = vjp(do_ref[...])
        dp_ref[...] = dp.astype(BF16)
        dlg_ref[...] += dlg
        dlb_ref[...] += dlb
        dws_ref[...] += dws
        dbs_ref[...] += dbs

    def full(a):
        return pl.BlockSpec(a.shape, lambda i: (0,) * a.ndim)

    return pl.pallas_call(
        body, name="gmlp_bwd", grid=(T // tm,),
        in_specs=[pl.BlockSpec((tm, 2 * GW), lambda i: (i, C_D // (2 * GW))), full(lg), full(lb), full(ws), full(bs),
                  pl.BlockSpec((tm, GW), lambda i: (i, O_D // GW))],
        out_specs=[pl.BlockSpec((tm, 2 * GW), lambda i: (i, 0)), full(lg), full(lb), full(ws), full(bs)],
        out_shape=[_sds((T, 2 * GW), BF16), _sds(lg.shape, F32), _sds(lb.shape, F32), _sds(ws.shape, F32), _sds(bs.shape, F32)],
        compiler_params=_cparams(VMEM_BIG))(proj, lg, lb, ws, bs, dcat)


def _lb_all(logits):
    m = jnp.max(logits, axis=0, keepdims=True)
    e = jnp.exp(logits - m)
    sm = e / jnp.sum(e, axis=0, keepdims=True)
    return jnp.concatenate([sm[0:1] - sm[0:1], (sm[0:1] + sm[1:2]) - sm[0:1]], axis=0)


def lb_fwd(logits):
    def body(l_ref, o_ref):
        o_ref[...] = _lb_all(l_ref[...])

    return pl.pallas_call(body, name="lb_fwd", out_shape=_sds(logits.shape, F32))(logits)


def lb_bwd(logits, dlb):
    def body(l_ref, d_ref, o_ref):
        _, vjp = jax.vjp(_lb_all, l_ref[...])
        o_ref[...] = vjp(d_ref[...])[0]

    return pl.pallas_call(body, name="lb_bwd", out_shape=_sds(logits.shape, F32))(logits, dlb)


def ada_grad(c_all, dmod_cols):
    N = dmod_cols.shape[1]
    tn = _tile(N, 1152)

    def body(c_ref, d_ref, o_ref):
        h = jax.nn.silu(c_ref[...]).astype(BF16)
        o_ref[...] = lax.dot_general(h, d_ref[...].astype(BF16), (((0,), (0,)), ((), ())), preferred_element_type=F32)

    nb = c_all.shape[0]
    return pl.pallas_call(
        body, name="ada_grad", grid=(N // tn,),
        in_specs=[pl.BlockSpec((nb, D), lambda j: (0, 0)), pl.BlockSpec((nb, tn), lambda j: (0, j))],
        out_specs=pl.BlockSpec((D, tn), lambda j: (0, j)), out_shape=_sds((D, N), F32),
        compiler_params=_cparams(VMEM_BIG))(c_all, dmod_cols)


def sum_slots(a, n, name):
    _, R, C = a.shape
    tr = _row_tile(R, C, n)

    def body(a_ref, o_ref):
        acc = a_ref[0]
        for k in range(1, n):
            acc = acc + a_ref[k]
        o_ref[...] = acc

    return pl.pallas_call(
        body, name=name, grid=(R // tr,),
        in_specs=[pl.BlockSpec((n, tr, C), lambda i: (0, i, 0))],
        out_specs=pl.BlockSpec((tr, C), lambda i: (i, 0)), out_shape=_sds((R, C), F32),
        compiler_params=_cparams(VMEM_BIG))(a)


def _row_tile(R, C=D, n=1, mult=8, elems=1 << 18):
    limit = max(mult, elems // (C * n))
    for t in range(limit - limit % mult, mult - 1, -mult):
        if R % t == 0:
            return t
    return R


def adamw(w, g, m, v, name, echo=False):
    R, C = w.shape
    tr = _row_tile(R, C, elems=1 << 19)
    c1 = 1.0 - ADAM_B1 ** ADAM_STEP
    c2 = 1.0 - ADAM_B2 ** ADAM_STEP
    n_out = 4 if echo else 3

    def body(w_ref, g_ref, m_ref, v_ref, d_ref, nm_ref, nv_ref, *g_out):
        g_ = g_ref[...]
        nm = ADAM_B1 * m_ref[...] + (1.0 - ADAM_B1) * g_
        nv = ADAM_B2 * v_ref[...] + (1.0 - ADAM_B2) * jnp.square(g_)
        d_ref[...] = -ADAM_LR * ((nm / c1) / (jnp.sqrt(nv / c2) + ADAM_EPS) + ADAM_WD * w_ref[...])
        nm_ref[...] = nm
        nv_ref[...] = nv
        if echo:
            g_out[0][...] = g_

    spec = pl.BlockSpec((tr, C), lambda i: (i, 0))
    return pl.pallas_call(body, name=name, grid=(R // tr,), in_specs=[spec] * 4, out_specs=[spec] * n_out,
                          out_shape=[_sds((R, C), F32)] * n_out, compiler_params=_cparams(VMEM_BIG))(w, g, m, v)


def _rot_cols(w):
    return jnp.concatenate([-w[:, 16:32], w[:, 0:16]], axis=1)


def _fold_rot(d):
    return jnp.concatenate([d[:, 16:32], -d[:, 0:16]], axis=1)


def _pad_heads(w, off, axis):
    parts = []
    for h in range(HEADS):
        piece = lax.slice_in_dim(w, off + HD * h, off + HD * (h + 1), axis=axis)
        parts += [piece, jnp.zeros_like(piece)]
    return parts


def _unpad_heads(d, off, axis):
    return [lax.slice_in_dim(d, off + HP * h, off + HP * h + HD, axis=axis) for h in range(HEADS)]


def mix_in_ext(w):
    z = lambda n: jnp.zeros((w.shape[0], n), w.dtype)
    kr = w[:, 1408:1440]
    cols = [w[:, 0:1408], kr, _rot_cols(kr), z(64)]
    cols += _pad_heads(w, 1440, 1) + _pad_heads(w, 1696, 1) + _pad_heads(w, 1952, 1)
    cols += [w[:, 2212:2724], w[:, 2208:2212], z(NP - C_CF - HEADS)]
    return jnp.concatenate(cols, axis=1)


def mix_in_unext(d):
    kr = d[:, 1408:1440] + _fold_rot(d[:, 1440:1472])
    cols = [d[:, 0:1408], kr] + _unpad_heads(d, C_CQ, 1) + _unpad_heads(d, C_CK, 1) + _unpad_heads(d, C_CV, 1)
    cols += [d[:, C_CF:C_CF + HEADS], d[:, C_D:C_D + 2 * GW]]
    return jnp.concatenate(cols, axis=1)


def mix_out_ext(w):
    return jnp.concatenate([w[0:GW]] + _pad_heads(w, GW, 0) + _pad_heads(w, 2 * GW, 0) + [w[3 * GW:4 * GW]], axis=0)


def mix_out_unext(d):
    return jnp.concatenate([d[0:GW]] + _unpad_heads(d, O_B, 0) + _unpad_heads(d, O_C, 0) + [d[O_D:O_D + GW]], axis=0)


def uq_ext(w):
    z = lambda n: jnp.zeros((w.shape[0], n), w.dtype)
    a, b = [], []
    for h in range(HEADS):
        o = (B_NOPE + B_ROPE) * h
        a += [w[:, o:o + B_NOPE + B_ROPE], z(32)]
        b += [z(B_NOPE), _rot_cols(w[:, o + B_NOPE:o + B_NOPE + B_ROPE]), z(32)]
    return jnp.concatenate(a + b, axis=1)


def uq_unext(d):
    cols = []
    for h in range(HEADS):
        o = HP * h
        cols += [d[:, o:o + B_NOPE], d[:, o + B_NOPE:o + B_NOPE + B_ROPE]
                 + _fold_rot(d[:, HEADS * HP + o + B_NOPE:HEADS * HP + o + B_NOPE + B_ROPE])]
    return jnp.concatenate(cols, axis=1)


def ukv_ext(w):
    z = jnp.zeros((w.shape[0], HD), w.dtype)
    k, v = [], []
    for h in range(HEADS):
        k += [w[:, 2 * HD * h:2 * HD * h + HD], z]
        v += [w[:, 2 * HD * h + HD:2 * HD * (h + 1)], z]
    return jnp.concatenate(k + v, axis=1)


def ukv_unext(d):
    cols = []
    for h in range(HEADS):
        cols += [d[:, HP * h:HP * h + HD], d[:, HEADS * HP + HP * h:HEADS * HP + HP * h + HD]]
    return jnp.concatenate(cols, axis=1)


def rope_tables(S):
    half = B_ROPE // 2
    inv_freq = ROPE_THETA ** (-jnp.arange(half, dtype=F32) / half)
    ang = jnp.arange(S).astype(F32)[:, None] * inv_freq[None, :]
    cos = jnp.tile(jnp.cos(ang), (1, 2))
    sin = jnp.tile(jnp.sin(ang), (1, 2))
    one, zero = jnp.ones((S, B_NOPE), F32), jnp.zeros((S, B_NOPE), F32)
    z32 = jnp.zeros((S, 32), F32)
    cos_q = jnp.tile(jnp.concatenate([one, cos, z32], axis=1), (1, HEADS))
    sin_q = jnp.tile(jnp.concatenate([zero, sin, z32], axis=1), (1, HEADS))
    cs_k = jnp.concatenate([cos, sin, zero], axis=1)
    place = np.zeros((HP, HEADS * HP), np.float32)
    for h in range(HEADS):
        for j in range(B_ROPE):
            place[j, h * HP + B_NOPE + j] = 1.0
            place[B_ROPE + j, h * HP + B_NOPE + j] = 1.0
    return cos_q, sin_q, cs_k, jnp.asarray(place, BF16)


def layer_fwd(x, mod, get, tabs, S):
    cos_q, sin_q, cs_k, place = tabs
    p = dict(get("ffn1", x))
    l = p["wl"]
    zg1, zu1, act1 = ffn_in_fwd(x, mod, p["ffn1_in"], l, 0, 1, S)
    y1, x1 = out_ln_fwd(act1, p["ffn1_out"], x, mod, p["ln_g"][0:1], p["ln_b"][0:1], 2, 0.5, S, l)
    p.update(get("mix", x1))
    proj = mix_in_fwd(x1, mod, p["mix_in"], 3, 4, S)
    o_a, states = hgrn_fwd(proj, p["lb"], p["ng"], S)
    q_b, k_b, v_b = mla_pre_fwd(proj, p["gq"], p["gkv"], p["wq"], p["wkv"], place, cos_q, sin_q, cs_k, S)
    o_b = attn_fwd(q_b, 0, k_b, 0, v_b, 0, None, (B_NOPE + B_ROPE) ** -0.5, S)
    cum = fox_gate_fwd(proj, p["bcol"], S)
    o_c = attn_fwd(proj, C_CQ // HP, proj, C_CK // HP, proj, C_CV // HP, cum, HD ** -0.5, S)
    o_d = gmlp_fwd(proj, p["g_lg"], p["g_lb"], p["ws"], p["bs"])
    cat = jnp.concatenate([o_a, o_b, o_c, o_d], axis=1)
    y2, x2 = out_ln_fwd(cat, p["mix_out"], x1, mod, p["ln_g"][1:2], p["ln_b"][1:2], 5, 1.0, S)
    p.update(get("ffn2", x2))
    zg3, zu3, act3 = ffn_in_fwd(x2, mod, p["ffn2_in"], l, 6, 7, S)
    y3, x3 = out_ln_fwd(act3, p["ffn2_out"], x2, mod, p["ln_g"][2:3], p["ln_b"][2:3], 8, 0.5, S, l)
    saved = dict(x=x, zg1=zg1, zu1=zu1, act1=act1, y1=y1, x1=x1, proj=proj, states=states, q_b=q_b, k_b=k_b, v_b=v_b,
                 cum=cum, cat=cat, y2=y2, x2=x2, zg3=zg3, zu3=zu3, act3=act3, y3=y3, p=p)
    return x3, saved


def _ffn_bwd(dxn, x_in, y, zg, zu, act, mod, w_in, w_out, l, lg, lb, idx, S, emit):
    sh, sc, gate = idx
    dres, dy, dgate, dlg, dlb = ln_res_bwd(dxn, x_in, y, mod, lg, lb, gate, 0.5, S)
    dzg, dzu = swiglu_bwd(dy, w_out, l, zg, zu, S)
    dw_out = tn_mm(act, dy, FF // 2).reshape(N_CHIPS, FF // N_CHIPS, D)
    dw_in = tn_mm_mod_shards(x_in, mod, dzg, dzu, sh, sc, S)
    mod = mod + emit(dw_in, dw_out)
    dx, dsh, dsc = nt_mod_bwd([dzg, dzu], w_in, [0, 2], x_in, mod, dres, sc, S, FF // 2, l)
    return dx, dw_in, dw_out, dlg, dlb, {sh: dsh, sc: dsc, gate: dgate}, mod


def layer_bwd(dx3, mod, sv, tabs, S, emit):
    cos_q, sin_q, cs_k, place = tabs
    p = sv["p"]
    l = p["wl"]
    g = {}
    dm = {}

    def emit_ffn(part):
        def f(dw_in, dw_out):
            g[part + "_in"], g[part + "_out"] = dw_in, dw_out
            return emit(part, g)
        return f

    dx2, _, _, dlg2, dlb2, d, mod = _ffn_bwd(
        dx3, sv["x2"], sv["y3"], sv["zg3"], sv["zu3"], sv["act3"], mod, p["ffn2_in"], p["ffn2_out"], l,
        p["ln_g"][2:3], p["ln_b"][2:3], (6, 7, 8), S, emit_ffn("ffn2"))
    dm.update(d)
    dres, dy2, dm[5], dlg1, dlb1 = ln_res_bwd(dx2, sv["x1"], sv["y2"], mod, p["ln_g"][1:2], p["ln_b"][1:2], 5, 1.0, S)
    dcat = nt_plain(dy2, p["mix_out"])
    g["mix_out"] = tn_mm(sv["cat"], dy2, NCAT // 2)
    proj = sv["proj"]
    d_a, g["lb"], g["ng"] = hgrn_bwd(proj, sv["states"], dcat, p["lb"], p["ng"], S)
    dq_c, dk_c, dv_c, dcum = attn_bwd(proj, C_CQ // HP, proj, C_CK // HP, proj, C_CV // HP, sv["cum"], dcat,
                                      O_C // HP, HD ** -0.5, S, BF16)
    B = proj.shape[0] // S
    dcum = jnp.concatenate([dcum.reshape(B, HEADS, S), jnp.zeros((B, 8 - HEADS, S), F32)], axis=1)
    d_cf, g["bcol"] = fox_gate_bwd(proj, p["bcol"], dcum, S)
    dq_b, dk_b, dv_b = attn_bwd(sv["q_b"], 0, sv["k_b"], 0, sv["v_b"], 0, None, dcat, O_B // HP,
                                (B_NOPE + B_ROPE) ** -0.5, S, F32)
    d_b, g["gq"], g["gkv"], g["wq"], g["wkv"] = mla_pre_bwd(
        proj, p["gq"], p["gkv"], p["wq"], p["wkv"], place, cos_q, sin_q, cs_k, dq_b, dk_b, dv_b, S)
    d_d, g["g_lg"], g["g_lb"], g["ws"], g["bs"] = gmlp_bwd(proj, p["g_lg"], p["g_lb"], p["ws"], p["bs"], dcat)
    dproj = jnp.concatenate([d_a, d_b, dq_c, dk_c, dv_c, d_d, d_cf, jnp.zeros_like(d_cf)], axis=1)
    g["mix_in"] = tn_mm_mod(sv["x1"], mod, dproj, 3, 4, S, NP_TILE)
    mod = mod + emit("mix", g)
    dx1, dm[3], dm[4] = nt_mod_bwd([dproj], p["mix_in"], [0], sv["x1"], mod, dres, 4, S, NP_TILE)
    last = []

    def emit_last(dw_in, dw_out):
        last.append(emit_ffn("ffn1")(dw_in, dw_out))
        return last[0]

    dx0, _, _, dlg0, dlb0, d, mod = _ffn_bwd(
        dx1, sv["x"], sv["y1"], sv["zg1"], sv["zu1"], sv["act1"], mod, p["ffn1_in"], p["ffn1_out"], l,
        p["ln_g"][0:1], p["ln_b"][0:1], (0, 1, 2), S, emit_last)
    dm.update(d)
    g["ln_g"] = jnp.concatenate([dlg0, dlg1, dlg2], axis=0)
    g["ln_b"] = jnp.concatenate([dlb0, dlb1, dlb2], axis=0)
    dmod = jnp.concatenate([dm[i] for i in range(N_MOD)], axis=1)
    return dx0, dmod, g, last[0]


def local_step(x, c8, tgt, get, lb_logits, S, emit=None):
    B = x.shape[0] // S
    tabs = rope_tables(S)
    lb_all = lb_fwd(lb_logits)
    mods, saved = [], []
    h = x
    for l in range(DEPTH):
        pa = get(l, "ada", h)
        mod = mod_fwd(c8, pa["ada_w"], pa["wl"], pa["ada_b"])[0:B].reshape(B, N_MOD, D)

        def get_l(part, after, l=l):
            p = dict(get(l, part, after))
            if part == "mix":
                p["lb"] = lb_all[l:l + 1]
            return p

        h, sv = layer_fwd(h, mod, get_l, tabs, S)
        mods.append(mod)
        saved.append(sv)
    loss_tile, dh = loss_head(h, tgt)
    grads, dmods, dlb = [None] * DEPTH, [None] * DEPTH, [None] * DEPTH
    tie = jnp.zeros((), F32)
    for l in reversed(range(DEPTH)):
        emit_l = (lambda part, g: jnp.zeros((), F32)) if emit is None else functools.partial(emit, l)
        dh, dmods[l], grads[l], tie = layer_bwd(dh, mods[l] + tie, saved[l], tabs, S, emit_l)
        dlb[l] = grads[l].pop("lb")
    d_logits = lb_bwd(lb_logits, jnp.concatenate(dlb, axis=0))
    return loss_tile, dh, dmods, grads, d_logits


ANY = pl.BlockSpec(memory_space=pl.ANY)


def _place():
    x, y, c = lax.axis_index("x"), lax.axis_index("y"), lax.axis_index("c")
    chips = [(1 - x, y), (x, 1 - y), (1 - x, 1 - y)]
    return x, y, c, chips


def _rcopy(src, dst, sems, k, to):
    send_sems, recv_sems = sems
    return pltpu.make_async_remote_copy(src_ref=src, dst_ref=dst, send_sem=send_sems.at[k], recv_sem=recv_sems.at[k],
                                        device_id=to, device_id_type=MESH)


def _dma_sems(n_remote, n_local):
    return [pltpu.SemaphoreType.DMA((n_remote,)), pltpu.SemaphoreType.DMA((n_remote,)), pltpu.SemaphoreType.DMA((n_local,))]


def own_slot(src, chip):
    L = src.shape[0]
    return lax.dynamic_update_slice(lax.empty((L, N_CHIPS) + src.shape[1:], src.dtype), src[:, None], (0, chip, 0, 0))


HBM_SPEC = pl.BlockSpec(memory_space=pltpu.HBM)
SEM_SPEC = pl.BlockSpec(memory_space=pltpu.SEMAPHORE)
DATAFLOW = pltpu.SideEffectType.DATAFLOW_SIDE_EFFECTING


def _split_start(srcs, lands, copies, n_copies, dep, name):
    n, m = len(srcs), len(lands)

    def body(*refs):
        ins = refs[:n + m]
        send_sems, recv_sems = refs[n + m + 1], refs[n + m + 2]
        token = refs[-1]
        for k, (src, dst, to) in enumerate(copies(ins[:n], ins[n:], _place())):
            pltpu.make_async_remote_copy(src_ref=src, dst_ref=dst, send_sem=send_sems.at[k], recv_sem=recv_sems.at[k],
                                         device_id=to, device_id_type=MESH).start()
        token[...] = jnp.zeros_like(token)

    arrs = list(srcs) + list(lands)
    outs = pl.pallas_call(
        body, name=name,
        out_shape=(pltpu.SemaphoreType.DMA((n_copies,)), pltpu.SemaphoreType.DMA((n_copies,)),
                   *[pltpu.HBM(a.shape, a.dtype) for a in arrs], _sds((8, 128), F32)),
        in_specs=[HBM_SPEC] * (n + m) + [ANY],
        out_specs=(SEM_SPEC, SEM_SPEC, *[HBM_SPEC] * (n + m), pl.BlockSpec(memory_space=pltpu.VMEM)),
        input_output_aliases={i: 2 + i for i in range(n + m)},
        compiler_params=pltpu.CompilerParams(has_side_effects=DATAFLOW),
    )(*[pltpu.with_memory_space_constraint(a, pltpu.HBM) for a in arrs], dep)
    return outs[0], outs[1], list(outs[2:2 + n]), list(outs[2 + n:2 + n + m]), outs[-1]


def _split_wait(handle, arrivals, after, name):
    send_sems, recv_sems, srcs, lands, _ = handle
    n, m = len(srcs), len(lands)

    def body(*refs):
        ins = refs[:n + m]
        send_sems, recv_sems = refs[n + m], refs[n + m + 1]
        x, y, c, chips = place = _place()
        for k, (src, dst) in enumerate(arrivals(ins[:n], ins[n:], place)):
            cp = pltpu.make_async_remote_copy(src_ref=src, dst_ref=dst, send_sem=send_sems.at[k], recv_sem=recv_sems.at[k],
                                              device_id=(x, y, 1 - c), device_id_type=MESH)
            cp.wait_send()
            cp.wait_recv()

    arrs = list(srcs) + list(lands)
    outs = pl.pallas_call(
        body, name=name, out_shape=[pltpu.HBM(a.shape, a.dtype) for a in arrs],
        in_specs=[HBM_SPEC] * (n + m) + [SEM_SPEC, SEM_SPEC, ANY], out_specs=[HBM_SPEC] * (n + m),
        input_output_aliases={i: i for i in range(n + m)},
        compiler_params=pltpu.CompilerParams(has_side_effects=DATAFLOW),
    )(*arrs, send_sems, recv_sems, after)
    return list(outs[:n]), list(outs[n:])


def _ag_part(ref, k, hc):
    rh = ref.shape[2] // 2
    return ref.at[:, k, pl.ds(hc * rh, rh), :]


def ag_start(srcs, lands, dep, name):
    def copies(s, d, place):
        x, y, c, chips = place
        out = []
        for j, (px, py) in enumerate(chips):
            for i in range(len(s)):
                rh = s[i].shape[1] // 2
                out.append((s[i].at[:, pl.ds(c * rh, rh), :], _ag_part(d[i], 2 * x + y, c), (px, py, c)))
        return out

    return _split_start(srcs, lands, copies, 3 * len(srcs), dep, name)


def ag_wait(handle, after, name):
    def arrivals(s, d, place):
        x, y, c, chips = place
        out = []
        for j, (px, py) in enumerate(chips):
            for i in range(len(s)):
                rh = s[i].shape[1] // 2
                out.append((s[i].at[:, pl.ds(c * rh, rh), :], _ag_part(d[i], 2 * px + py, c)))
        return out

    return _split_wait(handle, arrivals, after, name)


def ag_forward(lands, name):
    n = len(lands)

    def body(*refs):
        bufs, token = refs[n:2 * n], refs[2 * n]
        send_sems, recv_sems = refs[2 * n + 1:]
        x, y, c, chips = _place()
        sems = (send_sems, recv_sems)
        token[...] = jnp.zeros_like(token)
        cps = []
        for j, (px, py) in enumerate(chips):
            for i in range(n):
                part = _ag_part(bufs[i], 2 * px + py, c)
                cps.append(_rcopy(part, part, sems, 3 * i + j, (x, y, 1 - c)))
        for cp in cps:
            cp.start()
        for j, (px, py) in enumerate(chips):
            for i in range(n):
                part = _ag_part(bufs[i], 2 * px + py, 1 - c)
                _rcopy(part, part, sems, 3 * i + j, (x, y, 1 - c)).wait_recv()
        for cp in cps:
            cp.wait_send()

    outs = pl.pallas_call(
        body, name=name, out_shape=[_sds(a.shape, a.dtype) for a in lands] + [_sds((8, 128), F32)],
        in_specs=[ANY] * n, out_specs=[ANY] * n + [pl.BlockSpec(memory_space=pltpu.VMEM)],
        input_output_aliases={i: i for i in range(n)}, scratch_shapes=_dma_sems(3 * n, 1)[:2])(*lands)
    return list(outs[:n]), outs[n]


def rs_start(hs, lands, dep, name):
    def copies(s, d, place):
        x, y, c, chips = place
        return [(s[i].at[2 * px + py], d[i].at[2 * x + y], (px, py, c)) for j, (px, py) in enumerate(chips) for i in range(len(s))]

    return _split_start(hs, lands, copies, 3 * len(hs), dep, name)


def rs_wait(handle, after, name):
    def arrivals(s, d, place):
        x, y, c, chips = place
        return [(s[i].at[2 * px + py], d[i].at[2 * px + py]) for j, (px, py) in enumerate(chips) for i in range(len(s))]

    return _split_wait(handle, arrivals, after, name)


def sibling_swap(arrs, name):
    n = len(arrs)
    rh = [a.shape[1] // 2 for a in arrs]

    def body(*refs):
        srcs, outs = refs[:n], refs[n:2 * n]
        send_sems, recv_sems = refs[2 * n:]
        x, y, c, _ = _place()
        cps = [_rcopy(srcs[i].at[:, pl.ds((1 - c) * rh[i], rh[i]), :], outs[i], (send_sems, recv_sems), i, (x, y, 1 - c))
               for i in range(n)]
        for cp in cps:
            cp.start()
        for cp in cps:
            cp.wait()

    return pl.pallas_call(
        body, name=name, out_shape=[_sds((N_CHIPS, r, a.shape[2]), a.dtype) for a, r in zip(arrs, rh)],
        in_specs=[ANY] * n, out_specs=[ANY] * n, scratch_shapes=_dma_sems(n, 1)[:2])(*arrs)


def sum_into(land, base, l, core, name):
    _, rh, C = land.shape
    tr = _row_tile(rh, C, N_CHIPS, mult=16)
    nr = rh // tr

    def body(core_ref, land_ref, base_ref, o_ref):
        acc = land_ref[0].astype(F32)
        for k in range(1, N_CHIPS):
            acc = acc + land_ref[k].astype(F32)
        o_ref[...] = acc

    grid_spec = pltpu.PrefetchScalarGridSpec(
        num_scalar_prefetch=1, grid=(nr,),
        in_specs=[pl.BlockSpec((N_CHIPS, tr, C), lambda r, core_ref: (0, r, 0)), ANY],
        out_specs=pl.BlockSpec((None, tr, C), lambda r, core_ref: (l, core_ref[0] * nr + r, 0)))
    return pl.pallas_call(body, name=name, grid_spec=grid_spec, out_shape=_sds(base.shape, base.dtype),
                          input_output_aliases={2: 0}, compiler_params=_cparams(VMEM_BIG))(
        core.reshape(1).astype(jnp.int32), land, base)


def sibling_join(bases, name):
    n = len(bases)

    def body(*refs):
        bufs = refs[n:2 * n]
        send_sems, recv_sems = refs[2 * n:]
        x, y, c, _ = _place()
        sems = (send_sems, recv_sems)

        def half(i, hc):
            rh = bufs[i].shape[1] // 2
            return bufs[i].at[:, pl.ds(hc * rh, rh), :]

        sends = [_rcopy(half(i, c), half(i, c), sems, i, (x, y, 1 - c)) for i in range(n)]
        for cp in sends:
            cp.start()
        for i in range(n):
            _rcopy(half(i, 1 - c), half(i, 1 - c), sems, i, (x, y, 1 - c)).wait_recv()
        for cp in sends:
            cp.wait_send()

    return pl.pallas_call(
        body, name=name, out_shape=[_sds(b.shape, b.dtype) for b in bases], in_specs=[ANY] * n, out_specs=[ANY] * n,
        input_output_aliases={i: i for i in range(n)}, scratch_shapes=_dma_sems(n, 1)[:2])(*bases)


def ag_all(blk):
    M, C = blk.shape

    def body(x_ref, out_ref, send_sems, recv_sems, loc_sem):
        x, y, c, chips = _place()
        sems = (send_sems, recv_sems)
        me, sibling = (x, y, c), (x, y, 1 - c)

        def slot(px, py, pc):
            return out_ref.at[4 * px + 2 * py + pc]

        mine = pltpu.make_async_copy(x_ref, slot(*me), loc_sem)
        mine.start()
        first = [_rcopy(x_ref, slot(*me), sems, 0, sibling)]
        first += [_rcopy(x_ref, slot(*me), sems, 1 + j, (*chip, c)) for j, chip in enumerate(chips)]
        for cp in first:
            cp.start()
        passed = [_rcopy(slot(*chip, c), slot(*chip, c), sems, 4 + j, sibling) for j, chip in enumerate(chips)]
        for j, chip in enumerate(chips):
            _rcopy(slot(*chip, c), slot(*chip, c), sems, 1 + j, me).wait_recv()
            passed[j].start()
        _rcopy(slot(*sibling), slot(*sibling), sems, 0, me).wait_recv()
        for j, chip in enumerate(chips):
            _rcopy(slot(*chip, 1 - c), slot(*chip, 1 - c), sems, 4 + j, me).wait_recv()
        for cp in first + passed:
            cp.wait_send()
        mine.wait()

    return pl.pallas_call(
        body, name="ag_all", out_shape=_sds((8, M, C), blk.dtype),
        in_specs=[pl.BlockSpec(memory_space=pltpu.VMEM)], out_specs=pl.BlockSpec(memory_space=pltpu.VMEM),
        scratch_shapes=[pltpu.SemaphoreType.DMA((7,)), pltpu.SemaphoreType.DMA((7,)), pltpu.SemaphoreType.DMA(())],
        compiler_params=_cparams(VMEM_BIG))(blk)


WEIGHTS = ["ada_w", "ada_b", "ln_g", "ln_b", "ffn1_w_in", "ffn1_w_out", "ffn2_w_in", "ffn2_w_out", "mix_w_in", "mix_w_out",
           "hgrn_lb_logits", "hgrn_norm_g", "mla_q_norm_g", "mla_kv_norm_g", "mla_w_uq", "mla_w_ukv", "fox_b_f",
           "gmlp_ln_g", "gmlp_ln_b", "gmlp_w_s", "gmlp_b_s"]
SMALL = ["hgrn_lb_logits", "hgrn_norm_g", "mla_q_norm_g", "mla_kv_norm_g", "fox_b_f", "gmlp_ln_g", "gmlp_ln_b",
         "gmlp_w_s", "gmlp_b_s", "ln_g", "ln_b"]
GATHERED = ["ada_w", "ffn1_w_in", "ffn1_w_out", "ffn2_w_in", "ffn2_w_out", "mix_w_in", "mix_w_out", "mla_w_uq", "mla_w_ukv"]
REDUCED = GATHERED[1:]


def _col_shards(a):
    cols = a.shape[1] // N_CHIPS
    return jnp.stack([a[:, k * cols:(k + 1) * cols] for k in range(N_CHIPS)])


def add_kept_half(a, got, core, name):
    _, R, C = a.shape
    rh = R // 2
    tr = _row_tile(rh, C, mult=16)
    nr = rh // tr

    def body(core_ref, a_ref, b_ref, o_ref):
        o_ref[...] = (a_ref[...].astype(F32) + b_ref[...].astype(F32)).astype(o_ref.dtype)

    half = pl.BlockSpec((None, tr, C), lambda k, r, core_ref: (k, r, 0))
    grid_spec = pltpu.PrefetchScalarGridSpec(
        num_scalar_prefetch=1, grid=(N_CHIPS, nr),
        in_specs=[pl.BlockSpec((None, tr, C), lambda k, r, core_ref: (k, core_ref[0] * nr + r, 0)), half],
        out_specs=half)
    return pl.pallas_call(body, name=name, grid_spec=grid_spec, out_shape=_sds((N_CHIPS, rh, C), BF16),
                          compiler_params=_cparams(VMEM_BIG))(core.reshape(1).astype(jnp.int32), a, got)


def _rows(parts, n_rows, dtype):
    flat = jnp.concatenate([p.reshape(-1) for p in parts])
    pad = n_rows * D - flat.shape[0]
    return jnp.concatenate([flat, jnp.zeros((pad,), dtype)]).reshape(n_rows, D)


def _take(flat, shapes):
    out, o = [], 0
    for shp in shapes:
        n = int(np.prod(shp))
        out.append(flat[o:o + n].reshape(shp))
        o += n
    return out


def _round_up(n, m):
    return -(-n // m) * m


def pack_small(w):
    parts = [w[n][l] for l in range(DEPTH) for n in SMALL]
    n = sum(int(np.prod(p.shape)) for p in parts)
    return _rows(parts, _round_up(-(-n // D), 8), F32)


def unpack_small(pk, like):
    shapes = [like[n].shape[1:] for l in range(DEPTH) for n in SMALL]
    pieces = _take(pk.reshape(-1), shapes)
    names = [n for l in range(DEPTH) for n in SMALL]
    return {n: jnp.stack([p for p, m in zip(pieces, names) if m == n]) for n in SMALL}


def kernel(x, c, ada_w, ada_b, ln_g, ln_b, ffn1_w_in, ffn1_w_out, ffn2_w_in, ffn2_w_out, mix_w_in, mix_w_out, hgrn_lb_logits, hgrn_norm_g, mla_q_norm_g, mla_kv_norm_g, mla_w_uq, mla_w_ukv, fox_b_f, gmlp_ln_g, gmlp_ln_b, gmlp_w_s, gmlp_b_s, loss_target, m_ada_w, m_ada_b, m_ln_g, m_ln_b, m_ffn1_w_in, m_ffn1_w_out, m_ffn2_w_in, m_ffn2_w_out, m_mix_w_in, m_mix_w_out, m_hgrn_lb_logits, m_hgrn_norm_g, m_mla_q_norm_g, m_mla_kv_norm_g, m_mla_w_uq, m_mla_w_ukv, m_fox_b_f, m_gmlp_ln_g, m_gmlp_ln_b, m_gmlp_w_s, m_gmlp_b_s, v_ada_w, v_ada_b, v_ln_g, v_ln_b, v_ffn1_w_in, v_ffn1_w_out, v_ffn2_w_in, v_ffn2_w_out, v_mix_w_in, v_mix_w_out, v_hgrn_lb_logits, v_hgrn_norm_g, v_mla_q_norm_g, v_mla_kv_norm_g, v_mla_w_uq, v_mla_w_ukv, v_fox_b_f, v_gmlp_ln_g, v_gmlp_ln_b, v_gmlp_w_s, v_gmlp_b_s):
    w = dict(zip(WEIGHTS, (ada_w, ada_b, ln_g, ln_b, ffn1_w_in, ffn1_w_out, ffn2_w_in, ffn2_w_out, mix_w_in, mix_w_out, hgrn_lb_logits, hgrn_norm_g, mla_q_norm_g, mla_kv_norm_g, mla_w_uq, mla_w_ukv, fox_b_f, gmlp_ln_g, gmlp_ln_b, gmlp_w_s, gmlp_b_s)))
    m = dict(zip(WEIGHTS, (m_ada_w, m_ada_b, m_ln_g, m_ln_b, m_ffn1_w_in, m_ffn1_w_out, m_ffn2_w_in, m_ffn2_w_out, m_mix_w_in, m_mix_w_out, m_hgrn_lb_logits, m_hgrn_norm_g, m_mla_q_norm_g, m_mla_kv_norm_g, m_mla_w_uq, m_mla_w_ukv, m_fox_b_f, m_gmlp_ln_g, m_gmlp_ln_b, m_gmlp_w_s, m_gmlp_b_s)))
    v = dict(zip(WEIGHTS, (v_ada_w, v_ada_b, v_ln_g, v_ln_b, v_ffn1_w_in, v_ffn1_w_out, v_ffn2_w_in, v_ffn2_w_out, v_mix_w_in, v_mix_w_out, v_hgrn_lb_logits, v_hgrn_norm_g, v_mla_q_norm_g, v_mla_kv_norm_g, v_mla_w_uq, v_mla_w_ukv, v_fox_b_f, v_gmlp_ln_g, v_gmlp_ln_b, v_gmlp_w_s, v_gmlp_b_s)))
    Bl, S, _ = x.shape
    T = Bl * S
    core = lax.axis_index("c")
    chip = 2 * lax.axis_index("x") + lax.axis_index("y")

    def shard(key):
        n, l = key
        if n == "ln":
            return jnp.concatenate([ln_g[l:l + 1], ln_b[l:l + 1], jnp.zeros((1, 2, D // N_CHIPS), F32)], axis=1)
        return w[n][l:l + 1].astype(BF16)

    mixers = ["mix_w_in", "mix_w_out", "mla_w_uq", "mla_w_ukv"]
    groups = [[("ada_w", 0), ("ffn1_w_in", 0), ("ffn1_w_out", 0), ("ln", 0)],
              [(n, 0) for n in mixers + ["ffn2_w_in", "ffn2_w_out"]],
              [(n, 1) for n in GATHERED + ["ln"]]]
    srcs = [[shard(k) for k in grp] for grp in groups]
    lands = [[own_slot(s, chip) for s in srcs[0]]]
    handle0 = ag_start(srcs[0], lands[0], jnp.zeros((8, 128), F32), "ag_start_0")
    chip_later = chip + handle0[-1][0, 0].astype(jnp.int32)
    lands += [[own_slot(s, chip_later) for s in grp] for grp in srcs[1:]]
    first, token = ag_forward(ag_wait(handle0, lands[2][0], "ag_wait_0")[1], "ag_forward_0")
    have = dict(zip(groups[0], first))
    handles = {}
    for gi in (1, 2):
        handles[gi] = ag_start(srcs[gi], lands[gi], token, "ag_start_%d" % gi)
        token = handles[gi][-1]
    c8 = jnp.concatenate([c, jnp.zeros((8 - Bl, D), F32)], axis=0)
    c8 = c8 + token[0, 0]

    def cat_cols(a):
        return jnp.concatenate([a[0, k] for k in range(N_CHIPS)], axis=1)

    def get(l, part, after):
        gi = 2 if l == 1 else (0 if part in ("ada", "ffn1") else 1)
        if gi in handles:
            arrived, _ = ag_forward(ag_wait(handles.pop(gi), after, "ag_wait_%d" % gi)[1], "ag_forward_%d" % gi)
            have.update(zip(groups[gi], arrived))
        if part == "ada":
            return dict(ada_w=have[("ada_w", l)], wl=0, ada_b=ada_b[l][None])
        if part == "ffn1":
            ln_full = jnp.moveaxis(have[("ln", l)][0], 0, 1).reshape(8, D)
            return dict(ffn1_in=have[("ffn1_w_in", l)], ffn1_out=have[("ffn1_w_out", l)], wl=0,
                        ln_g=ln_full[0:3], ln_b=ln_full[3:6])
        if part == "ffn2":
            return dict(ffn2_in=have[("ffn2_w_in", l)], ffn2_out=have[("ffn2_w_out", l)])
        return dict(
            mix_in=mix_in_ext(cat_cols(have[("mix_w_in", l)])), mix_out=mix_out_ext(have[("mix_w_out", l)].reshape(D, D)),
            wq=uq_ext(cat_cols(have[("mla_w_uq", l)])).astype(F32), wkv=ukv_ext(cat_cols(have[("mla_w_ukv", l)])).astype(F32),
            ng=hgrn_norm_g[l][None], gq=mla_q_norm_g[l][None], gkv=mla_kv_norm_g[l][None],
            bcol=jnp.concatenate([fox_b_f[l], jnp.zeros((8 - HEADS,), F32)])[:, None],
            g_lg=gmlp_ln_g[l][None], g_lb=gmlp_ln_b[l][None], ws=gmlp_w_s[l], bs=gmlp_b_s[l][:, :, None])

    pending = []

    def emit(l, part, g):
        if part == "mix":
            names = mixers
            by_chip = [_col_shards(mix_in_unext(g["mix_in"])), mix_out_unext(g["mix_out"]).reshape(N_CHIPS, D // N_CHIPS, D),
                       _col_shards(uq_unext(g["wq"])).astype(BF16), _col_shards(ukv_unext(g["wkv"])).astype(BF16)]
        else:
            names = [part + "_w_in", part + "_w_out"]
            by_chip = [g[part + "_in"], g[part + "_out"]]
        tag = "%d_%s" % (l, part)
        got = sibling_swap(by_chip, "sibling_swap_" + tag)
        chip_sum = [add_kept_half(a, r, core, "add_sibling") for a, r in zip(by_chip, got)]
        zones = [lax.dynamic_update_slice(lax.empty(h.shape, h.dtype), lax.dynamic_slice_in_dim(h, chip, 1, axis=0), (chip, 0, 0))
                 for h in chip_sum]
        handle = rs_start(chip_sum, zones, chip_sum[0], "rs_start_" + tag)
        pending.append((l, names, handle, tag))
        return handle[-1][0, 0]

    loss_tile, dx, dmods, grads, d_logits = local_step(
        x.reshape(T, D), c8, loss_target.reshape(T, D), get, hgrn_lb_logits, S, emit)
    loss = lax.psum(loss_tile[0, 0], ("x", "y", "c"))

    small_g = {"hgrn_lb_logits": d_logits,
               "hgrn_norm_g": jnp.stack([grads[l]["ng"][0] for l in range(DEPTH)]),
               "mla_q_norm_g": jnp.stack([grads[l]["gq"][0] for l in range(DEPTH)]),
               "mla_kv_norm_g": jnp.stack([grads[l]["gkv"][0] for l in range(DEPTH)]),
               "fox_b_f": jnp.stack([grads[l]["bcol"][0:HEADS, 0] for l in range(DEPTH)]),
               "gmlp_ln_g": jnp.stack([grads[l]["g_lg"][0] for l in range(DEPTH)]),
               "gmlp_ln_b": jnp.stack([grads[l]["g_lb"][0] for l in range(DEPTH)]),
               "gmlp_w_s": jnp.stack([grads[l]["ws"] for l in range(DEPTH)]),
               "gmlp_b_s": jnp.stack([grads[l]["bs"][:, :, 0] for l in range(DEPTH)])}
    small_g["ln_g"] = jnp.stack([grads[l]["ln_g"] for l in range(DEPTH)])
    small_g["ln_b"] = jnp.stack([grads[l]["ln_b"] for l in range(DEPTH)])
    pk_small = pack_small(small_g)
    n_small = pk_small.shape[0]
    extras = [dmods[l] for l in range(DEPTH)] + [c]
    n_extra = _round_up(-(-sum(int(np.prod(e.shape)) for e in extras) // D), 8)
    gathered = ag_all(jnp.concatenate([pk_small, _rows(extras, n_extra, F32)], axis=0))
    g_small = unpack_small(sum_slots(gathered[:, 0:n_small], 8, "sum_small"), small_g)
    ext = gathered[:, n_small:].reshape(8, -1)
    n_dmod = DEPTH * Bl * N_MOD * D
    dmod_all = ext[:, 0:n_dmod].reshape(8, DEPTH, Bl, N_MOD * D)
    c_all = ext[:, n_dmod:n_dmod + Bl * D].reshape(8 * Bl, D)
    g_ada_w, g_ada_b = [], []
    ncol = N_MOD * D // N_CHIPS
    for l in range(DEPTH):
        dm = dmod_all[:, l].reshape(8 * Bl, N_MOD * D)
        g_ada_w.append(ada_grad(c_all, lax.dynamic_slice_in_dim(dm, chip * ncol, ncol, axis=1)))
        g_ada_b.append(sum_slots(dm.reshape(8 * Bl, N_MOD, D), 8 * Bl, "sum_ada_b").reshape(N_MOD * D))
    g_ada_w, g_ada_b = jnp.stack(g_ada_w), jnp.stack(g_ada_b)

    red = {n: lax.empty(w[n].shape, F32) for n in REDUCED}

    def arrive(entry, after):
        l, names, handle, tag = entry
        for n, land in zip(names, rs_wait(handle, after, "rs_wait_" + tag)[1]):
            red[n] = sum_into(land, red[n], l, core, "sum_chips")

    for entry in pending[:-1]:
        arrive(entry, dx)
    late = pending[-1][1]
    early = [n for n in REDUCED if n not in late]
    grad = dict(zip(early, sibling_join([red[n] for n in early], "sibling_join_a")))
    grad.update(g_small)
    grad["ada_w"], grad["ada_b"] = g_ada_w, g_ada_b
    for n in ("ln_g", "ln_b"):
        grad[n] = lax.dynamic_slice_in_dim(g_small[n], chip * (D // N_CHIPS), D // N_CHIPS, axis=2)
    out = {"grad": grad, "delta": {}, "new_m": {}, "new_v": {}}

    def update(n):
        shp = w[n].shape
        two_d = (-1, shp[-1])
        res = adamw(w[n].reshape(two_d), grad[n].reshape(two_d), m[n].reshape(two_d), v[n].reshape(two_d), "adamw_" + n,
                    echo=n in REDUCED)
        grad[n] = (res[3] if n in REDUCED else grad[n]).reshape(shp)
        for key, r in zip(("delta", "new_m", "new_v"), res):
            out[key][n] = r.reshape(shp)

    for n in WEIGHTS:
        if n not in late:
            update(n)
    arrive(pending[-1], out["delta"]["ffn2_w_in"])
    grad.update(zip(late, sibling_join([red[n] for n in late], "sibling_join_b")))
    for n in late:
        update(n)
    outs = [loss, dx.reshape(Bl, S, D)]
    for key in ("grad", "delta", "new_m", "new_v"):
        outs += [out[key][n] for n in WEIGHTS]
    return tuple(outs)
```

```python
import functools
import math

import jax
import jax.numpy as jnp
import numpy as np
from jax import lax
from jax.experimental import pallas as pl
from jax.experimental.pallas import tpu as pltpu

F32, BF16 = jnp.float32, jnp.bfloat16
MESH = pl.DeviceIdType.MESH

N_CHIPS = 4
D = 1024
DEPTH = 2
FF = 2816
N_MOD = 9
GW = 256
HEADS = 4
HD = 64
HP = 128
A_CHUNK = 16
LB_FLOOR = 1e-30
B_Q_LORA, B_KV_LORA, B_NOPE, B_ROPE = 256, 128, 64, 32
ROPE_THETA = 10000.0
D_CHUNK = 128
ALPHA = (2 * DEPTH) ** 0.25
LN_EPS = 1e-5
RMS_EPS = 1e-6
ADAM_LR, ADAM_B1, ADAM_B2, ADAM_EPS, ADAM_WD, ADAM_STEP = 0.001, 0.9, 0.999, 1e-08, 0.01, 10

NP = 3840
NP_TILE = 1920
C_A, C_B, C_CQ, C_CK, C_CV, C_D, C_CF = 0, 1024, 1536, 2048, 2560, 3072, 3584
NCAT = 1536
O_A, O_B, O_C, O_D = 0, 256, 768, 1280

VMEM_BIG = 48 << 20
VMEM_MOST = 58 << 20


def _cparams(vmem=None):
    return pltpu.CompilerParams(vmem_limit_bytes=vmem) if vmem else pltpu.CompilerParams()


def _sds(shape, dtype):
    return jax.ShapeDtypeStruct(tuple(shape), dtype)


@jax.custom_vjp
def _mm(a, w):
    return jnp.dot(a.astype(BF16), w.astype(BF16), preferred_element_type=F32)


def _mm_f(a, w):
    return _mm(a, w), (a, w)


def _mm_b(res, g):
    a, w = res
    gb = g.astype(BF16)
    da = lax.dot_general(gb, w.astype(BF16), (((1,), (1,)), ((), ())), preferred_element_type=F32)
    dw = lax.dot_general(a.astype(BF16), gb, (((0,), (0,)), ((), ())), preferred_element_type=F32)
    return da.astype(a.dtype), dw.astype(w.dtype)


_mm.defvjp(_mm_f, _mm_b)


@jax.custom_vjp
def _mm_nt(a, b):
    return lax.dot_general(a.astype(BF16), b.astype(BF16), (((1,), (1,)), ((), ())), preferred_element_type=F32)


def _mm_nt_f(a, b):
    return _mm_nt(a, b), (a, b)


def _mm_nt_b(res, g):
    a, b = res
    gb = g.astype(BF16)
    da = jnp.dot(gb, b.astype(BF16), preferred_element_type=F32)
    db = lax.dot_general(gb, a.astype(BF16), (((0,), (0,)), ((), ())), preferred_element_type=F32)
    return da.astype(a.dtype), db.astype(b.dtype)


_mm_nt.defvjp(_mm_nt_f, _mm_nt_b)


@jax.custom_vjp
def _mm_tn(a, b):
    return lax.dot_general(a.astype(BF16), b.astype(BF16), (((0,), (0,)), ((), ())), preferred_element_type=F32)


def _mm_tn_f(a, b):
    return _mm_tn(a, b), (a, b)


def _mm_tn_b(res, g):
    a, b = res
    gb = g.astype(BF16)
    da = lax.dot_general(b.astype(BF16), gb, (((1,), (1,)), ((), ())), preferred_element_type=F32)
    db = jnp.dot(a.astype(BF16), gb, preferred_element_type=F32)
    return da.astype(a.dtype), db.astype(b.dtype)


_mm_tn.defvjp(_mm_tn_f, _mm_tn_b)


def _split3(x):
    p1 = x.astype(BF16)
    r = x - p1.astype(F32)
    p2 = r.astype(BF16)
    return p1, p2, (r - p2.astype(F32)).astype(BF16)


@jax.custom_vjp
def _sel_r(x, sel):
    s = sel.astype(BF16)
    return sum(jnp.dot(p, s, preferred_element_type=F32) for p in _split3(x))


def _sel_r_f(x, sel):
    return _sel_r(x, sel), sel


def _sel_r_b(sel, g):
    s = sel.astype(BF16)
    dx = sum(lax.dot_general(p, s, (((1,), (1,)), ((), ())), preferred_element_type=F32) for p in _split3(g))
    return dx, jnp.zeros_like(sel)


_sel_r.defvjp(_sel_r_f, _sel_r_b)


@jax.custom_vjp
def _sel_l(sel, x):
    s = sel.astype(BF16)
    return sum(jnp.dot(s, p, preferred_element_type=F32) for p in _split3(x))


def _sel_l_f(sel, x):
    return _sel_l(sel, x), sel


def _sel_l_b(sel, g):
    s = sel.astype(BF16)
    dx = sum(lax.dot_general(s, p, (((0,), (0,)), ((), ())), preferred_element_type=F32) for p in _split3(g))
    return jnp.zeros_like(sel), dx


_sel_l.defvjp(_sel_l_f, _sel_l_b)


def _iota(shape, dim):
    return lax.broadcasted_iota(jnp.int32, shape, dim)


def _head_sum_mats():
    e = (_iota((GW, HP), 0) // HD == _iota((GW, HP), 1)).astype(F32)
    et = (_iota((HP, GW), 1) // HD == _iota((HP, GW), 0)).astype(F32)
    return e, et


def _modulate(x, mod_ref, sh, sc):
    return x * (1.0 + mod_ref[sc:sc + 1, :]) + mod_ref[sh:sh + 1, :]


def _ln_res(x, y, gate, lg, lb, gs):
    r = ALPHA * x + gs * (1.0 + gate) * y
    mu = jnp.mean(r, axis=-1, keepdims=True)
    var = jnp.mean(jnp.square(r - mu), axis=-1, keepdims=True)
    return (r - mu) * lax.rsqrt(var + LN_EPS) * lg + lb


def _rms(x, g):
    return x * lax.rsqrt(jnp.mean(x * x, axis=-1, keepdims=True) + RMS_EPS) * g


def _tile(n, pref):
    return pref if n % pref == 0 else n


def mod_fwd(c8, w, l, b):
    tn = w.shape[3]
    n = N_CHIPS * tn

    def body(c_ref, w_ref, b_ref, o_ref):
        h = jax.nn.silu(c_ref[...]).astype(BF16)
        o_ref[...] = jnp.dot(h, w_ref[...], preferred_element_type=F32) + b_ref[...]

    return pl.pallas_call(
        body, name="mod_fwd", grid=(N_CHIPS,),
        in_specs=[pl.BlockSpec((8, D), lambda j: (0, 0)), pl.BlockSpec((None, None, D, tn), lambda j: (l, j, 0, 0)),
                  pl.BlockSpec((1, tn), lambda j: (0, j))],
        out_specs=pl.BlockSpec((8, tn), lambda j: (0, j)), out_shape=_sds((8, n), F32),
        compiler_params=_cparams(VMEM_BIG))(c8, w, b)


def ffn_in_fwd(x, mod, w_in, l, sh, sc, S):
    T = x.shape[0]
    tm, tn = _tile(S, 1024), FF // 2
    tpb, nj = S // tm, 2

    def body(x_ref, mod_ref, wg_ref, wu_ref, zg_ref, zu_ref, act_ref, h_ref):
        @pl.when(pl.program_id(1) == 0)
        def _():
            h_ref[...] = _modulate(x_ref[...], mod_ref, sh, sc).astype(BF16)
        g = jnp.dot(h_ref[...], wg_ref[...], preferred_element_type=F32)
        u = jnp.dot(h_ref[...], wu_ref[...], preferred_element_type=F32)
        zg_ref[...] = g.astype(BF16)
        zu_ref[...] = u.astype(BF16)
        act_ref[...] = (jax.nn.silu(g) * u).astype(BF16)

    return pl.pallas_call(
        body, name="ffn_in_fwd", grid=(T // tm, nj),
        in_specs=[pl.BlockSpec((tm, D), lambda i, j: (i, 0)),
                  pl.BlockSpec((None, N_MOD, D), lambda i, j: (i // tpb, 0, 0)),
                  pl.BlockSpec((None, None, D, tn), lambda i, j: (l, j, 0, 0)),
                  pl.BlockSpec((None, None, D, tn), lambda i, j: (l, j + nj, 0, 0))],
        out_specs=[pl.BlockSpec((tm, tn), lambda i, j: (i, j))] * 3,
        out_shape=[_sds((T, FF), BF16)] * 3,
        scratch_shapes=[pltpu.VMEM((tm, D), BF16)],
        compiler_params=_cparams(VMEM_BIG))(x, mod, w_in, w_in)


def mix_in_fwd(x, mod, w, sh, sc, S):
    T = x.shape[0]
    n = w.shape[1]
    tm, tn = _tile(S, 1024), NP_TILE
    tpb = S // tm

    def body(x_ref, mod_ref, w_ref, o_ref, h_ref):
        @pl.when(pl.program_id(1) == 0)
        def _():
            h_ref[...] = _modulate(x_ref[...], mod_ref, sh, sc).astype(BF16)
        o_ref[...] = jnp.dot(h_ref[...], w_ref[...], preferred_element_type=F32)

    return pl.pallas_call(
        body, name="mix_in_fwd", grid=(T // tm, n // tn),
        in_specs=[pl.BlockSpec((tm, D), lambda i, j: (i, 0)),
                  pl.BlockSpec((None, N_MOD, D), lambda i, j: (i // tpb, 0, 0)),
                  pl.BlockSpec((D, tn), lambda i, j: (0, j))],
        out_specs=pl.BlockSpec((tm, tn), lambda i, j: (i, j)), out_shape=_sds((T, n), F32),
        scratch_shapes=[pltpu.VMEM((tm, D), BF16)],
        compiler_params=_cparams(VMEM_BIG))(x, mod, w)


def out_ln_fwd(act, w_out, x, mod, lg, lb, gate, gs, S, l=None):
    T, K = act.shape
    tm = _tile(S, 512)
    tpb = S // tm

    def body(a_ref, w_ref, x_ref, mod_ref, lg_ref, lb_ref, y_ref, xn_ref):
        y = jnp.dot(a_ref[...], w_ref[...].reshape(K, D), preferred_element_type=F32)
        y_ref[...] = y
        xn_ref[...] = _ln_res(x_ref[...], y, mod_ref[gate:gate + 1, :], lg_ref[...], lb_ref[...], gs)

    if l is None:
        w_spec = pl.BlockSpec((K, D), lambda i: (0, 0))
    else:
        w_spec = pl.BlockSpec((None, N_CHIPS, K // N_CHIPS, D), lambda i: (l, 0, 0, 0))
    return pl.pallas_call(
        body, name="out_ln_fwd", grid=(T // tm,),
        in_specs=[pl.BlockSpec((tm, K), lambda i: (i, 0)), w_spec,
                  pl.BlockSpec((tm, D), lambda i: (i, 0)),
                  pl.BlockSpec((None, N_MOD, D), lambda i: (i // tpb, 0, 0)),
                  pl.BlockSpec((1, D), lambda i: (0, 0)), pl.BlockSpec((1, D), lambda i: (0, 0))],
        out_specs=[pl.BlockSpec((tm, D), lambda i: (i, 0))] * 2,
        out_shape=[_sds((T, D), F32), _sds((T, D), F32)],
        compiler_params=_cparams(VMEM_BIG))(act, w_out, x, mod, lg, lb)


def ln_res_bwd(dxn, x, y, mod, lg, lb, gate, gs, S):
    T = x.shape[0]
    B = T // S
    tm = _tile(S, 512)
    tpb = S // tm

    def body(d_ref, x_ref, y_ref, mod_ref, lg_ref, lb_ref, dx_ref, dy_ref, dg_ref, dlg_ref, dlb_ref):
        i = pl.program_id(0)
        f = functools.partial(_ln_res, gs=gs)
        _, vjp = jax.vjp(f, x_ref[...], y_ref[...], mod_ref[gate:gate + 1, :], lg_ref[...], lb_ref[...])
        dx, dy, dg, dlg, dlb = vjp(d_ref[...])
        dx_ref[...] = dx
        dy_ref[...] = dy.astype(BF16)

        @pl.when(i % tpb == 0)
        def _():
            dg_ref[...] = jnp.zeros_like(dg_ref)

        @pl.when(i == 0)
        def _():
            dlg_ref[...] = jnp.zeros_like(dlg_ref)
            dlb_ref[...] = jnp.zeros_like(dlb_ref)

        dg_ref[...] += dg
        dlg_ref[...] += dlg
        dlb_ref[...] += dlb

    tok = pl.BlockSpec((tm, D), lambda i: (i, 0))
    vec = pl.BlockSpec((1, D), lambda i: (0, 0))
    return pl.pallas_call(
        body, name="ln_res_bwd", grid=(T // tm,),
        in_specs=[tok, tok, tok, pl.BlockSpec((None, N_MOD, D), lambda i: (i // tpb, 0, 0)), vec, vec],
        out_specs=[tok, tok, pl.BlockSpec((None, 1, D), lambda i: (i // tpb, 0, 0)), vec, vec],
        out_shape=[_sds((T, D), F32), _sds((T, D), BF16), _sds((B, 1, D), F32), _sds((1, D), F32), _sds((1, D), F32)],
        compiler_params=_cparams(VMEM_BIG))(dxn, x, y, mod, lg, lb)


def swiglu_bwd(dy, w_out, l, zg, zu, S):
    T = dy.shape[0]
    tm, tn = _tile(S, 1024), FF // 2

    def body(dy_ref, w_ref, zg_ref, zu_ref, dg_ref, du_ref):
        da = lax.dot_general(dy_ref[...], w_ref[...].reshape(tn, D), (((1,), (1,)), ((), ())), preferred_element_type=F32)
        g, u = zg_ref[...].astype(F32), zu_ref[...].astype(F32)
        sg = jax.nn.sigmoid(g)
        dg_ref[...] = (da * u * (sg * (1.0 + g * (1.0 - sg)))).astype(BF16)
        du_ref[...] = (da * (g * sg)).astype(BF16)

    zt = pl.BlockSpec((tm, tn), lambda i, j: (i, j))
    return pl.pallas_call(
        body, name="swiglu_bwd", grid=(T // tm, FF // tn),
        in_specs=[pl.BlockSpec((tm, D), lambda i, j: (i, 0)),
                  pl.BlockSpec((None, 2, FF // N_CHIPS, D), lambda i, j: (l, j, 0, 0)), zt, zt],
        out_specs=[zt, zt], out_shape=[_sds((T, FF), BF16), _sds((T, FF), BF16)],
        compiler_params=_cparams(VMEM_BIG))(dy, w_out, zg, zu)


def nt_plain(dy, w):
    T = dy.shape[0]
    K = w.shape[0]
    tm = _tile(T, 1024)

    def body(dy_ref, w_ref, o_ref):
        o_ref[...] = lax.dot_general(dy_ref[...], w_ref[...], (((1,), (1,)), ((), ())), preferred_element_type=F32)

    return pl.pallas_call(
        body, name="nt_plain", grid=(T // tm,),
        in_specs=[pl.BlockSpec((tm, D), lambda i: (i, 0)), pl.BlockSpec((K, D), lambda i: (0, 0))],
        out_specs=pl.BlockSpec((tm, K), lambda i: (i, 0)), out_shape=_sds((T, K), F32),
        compiler_params=_cparams(VMEM_BIG))(dy, w)


def _tn_step(acc, o_ref, lhs, rhs, t, nt):
    part = lax.dot_general(lhs, rhs, (((0,), (0,)), ((), ())), preferred_element_type=F32)
    if nt == 1:
        o_ref[...] = part.astype(o_ref.dtype)
        return

    @pl.when(t == 0)
    def _():
        acc[...] = part

    @pl.when((t > 0) & (t < nt - 1))
    def _():
        acc[...] += part

    @pl.when(t == nt - 1)
    def _():
        o_ref[...] = (acc[...] + part).astype(o_ref.dtype)


def tn_mm(a, b, tk):
    T, K = a.shape
    N = b.shape[1]
    tt = _tile(T, 2048)
    nt = T // tt

    def body(a_ref, b_ref, o_ref, acc):
        _tn_step(acc, o_ref, a_ref[...], b_ref[...], pl.program_id(1), nt)

    return pl.pallas_call(
        body, name="tn_mm", grid=(K // tk, nt),
        in_specs=[pl.BlockSpec((tt, tk), lambda k, t: (t, k)), pl.BlockSpec((tt, N), lambda k, t: (t, 0))],
        out_specs=pl.BlockSpec((tk, N), lambda k, t: (k, 0)), out_shape=_sds((K, N), BF16),
        scratch_shapes=[pltpu.VMEM((tk, N), F32)], compiler_params=_cparams(VMEM_BIG))(a, b)


def tn_mm_mod(x, mod, b, sh, sc, S, tn):
    T = x.shape[0]
    N = b.shape[1]
    tt = _tile(S, 1024)
    tpb = S // tt
    nt = T // tt

    def body(x_ref, mod_ref, b_ref, o_ref, acc):
        h = _modulate(x_ref[...], mod_ref, sh, sc).astype(BF16)
        _tn_step(acc, o_ref, h, b_ref[...], pl.program_id(1), nt)

    return pl.pallas_call(
        body, name="tn_mm_mod", grid=(N // tn, nt),
        in_specs=[pl.BlockSpec((tt, D), lambda j, t: (t, 0)),
                  pl.BlockSpec((None, N_MOD, D), lambda j, t: (t // tpb, 0, 0)),
                  pl.BlockSpec((tt, tn), lambda j, t: (t, j))],
        out_specs=pl.BlockSpec((D, tn), lambda j, t: (0, j)), out_shape=_sds((D, N), BF16),
        scratch_shapes=[pltpu.VMEM((D, tn), F32)], compiler_params=_cparams(VMEM_BIG))(x, mod, b)


def tn_mm_mod_shards(x, mod, bg, bu, sh, sc, S):
    T = x.shape[0]
    tn = FF // 2
    tt = _tile(S, 1024)
    tpb = S // tt
    nt = T // tt

    def body(x_ref, mod_ref, bg_ref, bu_ref, o_ref, acc):
        j, t = pl.program_id(0), pl.program_id(1)
        h = _modulate(x_ref[...], mod_ref, sh, sc).astype(BF16)

        @pl.when(j < 2)
        def _():
            _tn_step(acc, o_ref, h, bg_ref[...], t, nt)

        @pl.when(j >= 2)
        def _():
            _tn_step(acc, o_ref, h, bu_ref[...], t, nt)

    return pl.pallas_call(
        body, name="tn_mm_mod_shards", grid=(N_CHIPS, nt),
        in_specs=[pl.BlockSpec((tt, D), lambda j, t: (t, 0)),
                  pl.BlockSpec((None, N_MOD, D), lambda j, t: (t // tpb, 0, 0)),
                  pl.BlockSpec((tt, tn), lambda j, t: (jnp.where(j < 2, t, 0), jnp.minimum(j, 1))),
                  pl.BlockSpec((tt, tn), lambda j, t: (jnp.where(j < 2, 0, t), jnp.maximum(j - 2, 0)))],
        out_specs=pl.BlockSpec((None, D, tn), lambda j, t: (j, 0, 0)), out_shape=_sds((N_CHIPS, D, tn), BF16),
        scratch_shapes=[pltpu.VMEM((D, tn), F32)], compiler_params=_cparams(VMEM_BIG))(x, mod, bg, bu)


def nt_mod_bwd(ds, w, offs, x, mod, dres, sc, S, tk, l=None):
    T = x.shape[0]
    B = T // S
    tm = _tile(S, 1024)
    tpb = S // tm
    Kd = ds[0].shape[1]
    nk = Kd // tk
    n_in = len(ds)

    def body(*refs):
        d_refs, w_refs = refs[:n_in], refs[n_in:2 * n_in]
        x_ref, mod_ref, r_ref, dx_ref, dsh_ref, dsc_ref, acc = refs[2 * n_in:]
        i, k = pl.program_id(0), pl.program_id(1)

        part = sum(lax.dot_general(d_ref[...], w_ref[...], (((1,), (1,)), ((), ())), preferred_element_type=F32)
                   for d_ref, w_ref in zip(d_refs, w_refs))

        @pl.when(k == 0)
        def _():
            acc[...] = part

        @pl.when(k > 0)
        def _():
            acc[...] += part

        @pl.when(k == nk - 1)
        def _():
            dh = acc[...]
            dx_ref[...] = dh * (1.0 + mod_ref[sc:sc + 1, :]) + r_ref[...]

            @pl.when(i % tpb == 0)
            def _():
                dsh_ref[...] = jnp.zeros_like(dsh_ref)
                dsc_ref[...] = jnp.zeros_like(dsc_ref)

            dsh_ref[...] += jnp.sum(dh, axis=0, keepdims=True)
            dsc_ref[...] += jnp.sum(dh * x_ref[...], axis=0, keepdims=True)

    tok = pl.BlockSpec((tm, D), lambda i, k: (i, 0))
    vec = pl.BlockSpec((None, 1, D), lambda i, k: (i // tpb, 0, 0))
    in_specs = [pl.BlockSpec((tm, tk), lambda i, k: (i, k)) for _ in ds]
    if l is None:
        in_specs += [pl.BlockSpec((D, tk), functools.partial(lambda i, k, o: (0, k + o), o=off // tk)) for off in offs]
    else:
        in_specs += [pl.BlockSpec((None, None, D, tk), functools.partial(lambda i, k, o: (l, k + o, 0, 0), o=off)) for off in offs]
    in_specs += [tok, pl.BlockSpec((None, N_MOD, D), lambda i, k: (i // tpb, 0, 0)), tok]
    return pl.pallas_call(
        body, name="nt_mod_bwd", grid=(T // tm, nk), in_specs=in_specs,
        out_specs=[tok, vec, vec],
        out_shape=[_sds((T, D), F32), _sds((B, 1, D), F32), _sds((B, 1, D), F32)],
        scratch_shapes=[pltpu.VMEM((tm, D), F32)],
        compiler_params=_cparams(VMEM_MOST))(*ds, *([w] * n_in), x, mod, dres)


def loss_head(y, tgt):
    T = y.shape[0]
    tm = _tile(T, 512)

    def body(y_ref, t_ref, l_ref, d_ref):
        @pl.when(pl.program_id(0) == 0)
        def _():
            l_ref[...] = jnp.zeros_like(l_ref)
        e = y_ref[...] - t_ref[...]
        d_ref[...] = e * (1.0 / D)
        l_ref[...] += 0.5 * jnp.sum(jnp.sum(e * e, axis=1, keepdims=True) * (1.0 / D))

    tok = pl.BlockSpec((tm, D), lambda i: (i, 0))
    return pl.pallas_call(
        body, name="loss_head", grid=(T // tm,), in_specs=[tok, tok],
        out_specs=[pl.BlockSpec((8, 128), lambda i: (0, 0)), tok],
        out_shape=[_sds((8, 128), F32), _sds((T, D), F32)],
        compiler_params=_cparams(VMEM_BIG))(y, tgt)


def _hgrn_block(q, fz, inp, go, st, lb, ng, blk):
    nc = blk // A_CHUNK
    lb_eff = jnp.maximum(lb, LB_FLOOR)
    log_f = jnp.logaddexp(jnp.log(lb_eff), jnp.log1p(-lb) + jax.nn.log_sigmoid(fz))
    k = (1.0 - lb) * jax.nn.sigmoid(-fz) - (lb_eff - lb)
    qf = jax.nn.silu(q)
    same_chunk = _iota((blk, blk), 0) // A_CHUNK == _iota((blk, blk), 1) // A_CHUNK
    tril = (same_chunk & (_iota((blk, blk), 1) <= _iota((blk, blk), 0))).astype(F32)
    G = _sel_l(tril, log_f)
    e_mat, et_mat = _head_sum_mats()
    G4, q4, k4, v4 = (z.reshape(nc, A_CHUNK, GW) for z in (G, qf, k, inp))
    shp = (nc, A_CHUNK, A_CHUNK, GW)
    one = (1, A_CHUNK, A_CHUNK, GW)
    mask = jnp.where(_iota(one, 2) <= _iota(one, 1), 0.0, -jnp.inf)
    decay = jnp.exp((G4[:, :, None, :] - G4[:, None, :, :]) + mask)
    prod = q4[:, :, None, :] * k4[:, None, :, :] * decay
    scores = _mm(prod.reshape(nc * A_CHUNK * A_CHUNK, GW), e_mat.astype(BF16))
    spread = _mm(scores, et_mat.astype(BF16)).reshape(shp)
    o_intra = jnp.sum(spread * v4[:, None, :, :], axis=2).reshape(blk, GW)
    head_diag = (_iota((GW, GW), 0) // HD == _iota((GW, GW), 1) // HD).astype(F32)
    g_last = [jnp.sum(log_f[c * A_CHUNK:(c + 1) * A_CHUNK], axis=0, keepdims=True) for c in range(nc)]
    g_last_b = jnp.concatenate([jnp.broadcast_to(g, (A_CHUNK, GW)) for g in g_last], axis=0)
    q_dec = qf * jnp.exp(G)
    k_end = k * jnp.exp(g_last_b - G)
    outs = []
    for c in range(nc):
        rows = slice(c * A_CHUNK, (c + 1) * A_CHUNK)
        outs.append(_mm_nt(q_dec[rows], st))
        st = st * jnp.exp(g_last[c]) + _mm_tn(inp[rows], k_end[rows]) * head_diag
    o = o_intra + jnp.concatenate(outs, axis=0)
    ms = _sel_r(o * o, e_mat) * (1.0 / HD)
    o = o * _sel_r(lax.rsqrt(ms + RMS_EPS), et_mat) * ng
    return o * jax.nn.silu(go), st


HGRN_BLK = 128


def hgrn_fwd(proj, lb, ng, S):
    T = proj.shape[0]
    B = T // S
    blk = min(HGRN_BLK, S)
    nb = S // blk

    def body(p_ref, lb_ref, ng_ref, o_ref, st_out_ref, st_ref):
        @pl.when(pl.program_id(1) == 0)
        def _():
            st_ref[...] = jnp.zeros_like(st_ref)
        st_out_ref[...] = st_ref[...]
        p = p_ref[...]
        o, st = _hgrn_block(p[:, 0:GW], p[:, GW:2 * GW], p[:, 2 * GW:3 * GW], p[:, 3 * GW:4 * GW],
                            st_ref[...], lb_ref[...], ng_ref[...], blk)
        o_ref[...] = o.astype(BF16)
        st_ref[...] = st

    vec = pl.BlockSpec((1, GW), lambda b, j: (0, 0))
    return pl.pallas_call(
        body, name="hgrn_fwd", grid=(B, nb),
        in_specs=[pl.BlockSpec((blk, 4 * GW), lambda b, j: (b * nb + j, C_A // (4 * GW))), vec, vec],
        out_specs=[pl.BlockSpec((blk, GW), lambda b, j: (b * nb + j, 0)),
                   pl.BlockSpec((None, GW, GW), lambda b, j: (b * nb + j, 0, 0))],
        out_shape=[_sds((T, GW), BF16), _sds((B * nb, GW, GW), F32)],
        scratch_shapes=[pltpu.VMEM((GW, GW), F32)],
        compiler_params=_cparams(VMEM_BIG))(proj, lb, ng)


def hgrn_bwd(proj, states, dcat, lb, ng, S):
    T = proj.shape[0]
    B = T // S
    blk = min(HGRN_BLK, S)
    nb = S // blk

    def body(p_ref, st_in_ref, do_ref, lb_ref, ng_ref, dp_ref, dlb_ref, dng_ref, dst_ref):
        b, j = pl.program_id(0), pl.program_id(1)

        @pl.when(j == 0)
        def _():
            dst_ref[...] = jnp.zeros_like(dst_ref)

        @pl.when((b == 0) & (j == 0))
        def _():
            dlb_ref[...] = jnp.zeros_like(dlb_ref)
            dng_ref[...] = jnp.zeros_like(dng_ref)

        p = p_ref[...]
        f = functools.partial(_hgrn_block, blk=blk)
        _, vjp = jax.vjp(f, p[:, 0:GW], p[:, GW:2 * GW], p[:, 2 * GW:3 * GW], p[:, 3 * GW:4 * GW],
                         st_in_ref[...], lb_ref[...], ng_ref[...])
        dq, df, di, dg, dst, dlb, dng = vjp((do_ref[...], dst_ref[...]))
        dp_ref[...] = jnp.concatenate([dq, df, di, dg], axis=1).astype(BF16)
        dst_ref[...] = dst
        dlb_ref[...] += dlb
        dng_ref[...] += dng

    def rev(b, j):
        return b * nb + (nb - 1 - j)

    vec = pl.BlockSpec((1, GW), lambda b, j: (0, 0))
    return pl.pallas_call(
        body, name="hgrn_bwd", grid=(B, nb),
        in_specs=[pl.BlockSpec((blk, 4 * GW), lambda b, j: (rev(b, j), C_A // (4 * GW))),
                  pl.BlockSpec((None, GW, GW), lambda b, j: (rev(b, j), 0, 0)),
                  pl.BlockSpec((blk, GW), lambda b, j: (rev(b, j), O_A // GW)), vec, vec],
        out_specs=[pl.BlockSpec((blk, 4 * GW), lambda b, j: (rev(b, j), 0)), vec, vec],
        out_shape=[_sds((T, 4 * GW), BF16), _sds((1, GW), F32), _sds((1, GW), F32)],
        scratch_shapes=[pltpu.VMEM((GW, GW), F32)],
        compiler_params=_cparams(VMEM_BIG))(proj, states, dcat, lb, ng)


ATT_TQ = 256


ATT_BANDS = 8


def _attn_block(q, k, v, cum, qpos0, scale, use_cum, n_free):
    s = _mm_nt(q * scale, k) if _pow2(scale) else _mm_nt(q, k) * scale
    if use_cum:
        s = s - cum
    band = s[:, n_free:]
    visible = _iota(band.shape, 1) <= (qpos0 - n_free) + _iota(band.shape, 0)
    band = jnp.where(visible, band, -jnp.inf)
    m = jnp.max(band, axis=-1, keepdims=True)
    if n_free:
        free = s[:, :n_free]
        m = jnp.maximum(m, jnp.max(free, axis=-1, keepdims=True))
    e = jnp.exp(band - m)
    denom = jnp.sum(e, axis=-1, keepdims=True)
    o = _mm(e, v[n_free:])
    if n_free:
        e = jnp.exp(free - m)
        denom = denom + jnp.sum(e, axis=-1, keepdims=True)
        o = o + _mm(e, v[:n_free])
    return o * (1.0 / denom)


def _pow2(scale):
    return math.frexp(scale)[0] == 0.5


def _bands(S, tq):
    nq = S // tq
    nb = min(ATT_BANDS, nq)
    per = nq // nb
    return [(r * per, (r + 1) * per, (r + 1) * per * tq) for r in range(nb)]


def attn_fwd(qa, qo, ka, ko, va, vo, cum, scale, S):
    T = qa.shape[0]
    B = T // S
    tq = min(ATT_TQ, S)
    nq = S // tq
    use_cum = cum is not None

    def body(*refs):
        if use_cum:
            q_ref, k_ref, v_ref, c_ref, o_ref = refs
        else:
            (q_ref, k_ref, v_ref, o_ref), c_ref = refs, None
        h, i = pl.program_id(1), pl.program_id(2)
        for lo, hi, kw in _bands(S, tq):
            @pl.when((i >= lo) & (i < hi))
            def _():
                crow = c_ref[pl.ds(h, 1), 0:kw] if use_cum else None
                o = _attn_block(q_ref[...], k_ref[0:kw, :], v_ref[0:kw, :], crow, i * tq, scale, use_cum, lo * tq)
                o_ref[...] = o.astype(BF16)

    in_specs = [pl.BlockSpec((tq, HP), lambda b, h, i: (b * nq + i, qo + h)),
                pl.BlockSpec((S, HP), lambda b, h, i: (b, ko + h)),
                pl.BlockSpec((S, HP), lambda b, h, i: (b, vo + h))]
    args = [qa, ka, va]
    if use_cum:
        in_specs.append(pl.BlockSpec((None, 8, S), lambda b, h, i: (b, 0, 0)))
        args.append(cum)
    return pl.pallas_call(
        body, name="attn_fwd", grid=(B, HEADS, nq), in_specs=in_specs,
        out_specs=pl.BlockSpec((tq, HP), lambda b, h, i: (b * nq + i, h)),
        out_shape=_sds((T, HEADS * HP), BF16),
        compiler_params=_cparams(VMEM_BIG))(*args)


def _attn_block_bwd(q, k, v, cum, do, qpos0, scale, use_cum, n_free):
    tn = (((0,), (0,)), ((), ()))
    nt = (((1,), (1,)), ((), ()))
    early = _pow2(scale)
    qb, dob = (q * scale if early else q).astype(BF16), do.astype(BF16)
    kb, vb = k.astype(BF16), v.astype(BF16)
    s = lax.dot_general(qb, kb, nt, preferred_element_type=F32)
    if not early:
        s = s * scale
    if use_cum:
        s = s - cum
    band = s[:, n_free:]
    visible = _iota(band.shape, 1) <= (qpos0 - n_free) + _iota(band.shape, 0)
    parts = [(jnp.where(visible, band, -jnp.inf), n_free, s.shape[1])]
    if n_free:
        parts.append((s[:, :n_free], 0, n_free))
    m = functools.reduce(jnp.maximum, [jnp.max(sp, axis=-1, keepdims=True) for sp, _, _ in parts])
    es = [jnp.exp(sp - m) for sp, _, _ in parts]
    rinv = 1.0 / sum(jnp.sum(e, axis=-1, keepdims=True) for e in es)
    ps = [e * rinv for e in es]
    dps = [lax.dot_general(dob, vb[a:b], nt, preferred_element_type=F32) for _, a, b in parts]
    delta = sum(jnp.sum(p * dp, axis=-1, keepdims=True) for p, dp in zip(ps, dps))
    dq = jnp.zeros(q.shape, F32)
    out = []
    for p, dp, (_, a, b) in zip(ps, dps, parts):
        ds = p * (dp - delta)
        dsb = ds.astype(BF16)
        dq = dq + jnp.dot(dsb, kb[a:b], preferred_element_type=F32)
        dk = lax.dot_general(dsb, qb, tn, preferred_element_type=F32)
        out.append((a, b, dk if early else dk * scale,
                    lax.dot_general(p.astype(BF16), dob, tn, preferred_element_type=F32),
                    -jnp.sum(ds, axis=0, keepdims=True) if use_cum else None))
    return dq * scale, out


def attn_bwd(qa, qo, ka, ko, va, vo, cum, dcat, do_off, scale, S, out_dtype):
    T = qa.shape[0]
    B = T // S
    tq = min(ATT_TQ, S)
    nq = S // tq
    use_cum = cum is not None

    def body(*refs):
        if use_cum:
            q_ref, k_ref, v_ref, do_ref, c_ref, dq_ref, dk_ref, dv_ref, dc_ref, dk_acc, dv_acc = refs
        else:
            q_ref, k_ref, v_ref, do_ref, dq_ref, dk_ref, dv_ref, dk_acc, dv_acc = refs
        h, i = pl.program_id(1), pl.program_id(2)

        @pl.when(i == 0)
        def _():
            dk_acc[...] = jnp.zeros_like(dk_acc)
            dv_acc[...] = jnp.zeros_like(dv_acc)
            if use_cum:
                dc_ref[...] = jnp.zeros_like(dc_ref)

        for lo, hi, kw in _bands(S, tq):
            @pl.when((i >= lo) & (i < hi))
            def _():
                crow = c_ref[pl.ds(h, 1), 0:kw] if use_cum else None
                dq, pieces = _attn_block_bwd(q_ref[...], k_ref[0:kw, :], v_ref[0:kw, :], crow, do_ref[...], i * tq,
                                             scale, use_cum, lo * tq)
                dq_ref[...] = dq.astype(out_dtype)
                for a, b, dk, dv, dc in pieces:
                    dk_acc[a:b, :] += dk
                    dv_acc[a:b, :] += dv
                    if use_cum:
                        dc_ref[:, a:b] += dc

        @pl.when(i == nq - 1)
        def _():
            dk_ref[...] = dk_acc[...].astype(out_dtype)
            dv_ref[...] = dv_acc[...].astype(out_dtype)

    qspec = pl.BlockSpec((tq, HP), lambda b, h, i: (b * nq + i, qo + h))
    in_specs = [qspec, pl.BlockSpec((S, HP), lambda b, h, i: (b, ko + h)),
                pl.BlockSpec((S, HP), lambda b, h, i: (b, vo + h)),
                pl.BlockSpec((tq, HP), lambda b, h, i: (b * nq + i, do_off + h))]
    args = [qa, ka, va, dcat]
    kv_out = pl.BlockSpec((S, HP), lambda b, h, i: (b, h))
    out_specs = [pl.BlockSpec((tq, HP), lambda b, h, i: (b * nq + i, h)), kv_out, kv_out]
    out_shape = [_sds((T, HEADS * HP), out_dtype)] * 3
    if use_cum:
        in_specs.append(pl.BlockSpec((None, 8, S), lambda b, h, i: (b, 0, 0)))
        args.append(cum)
        out_specs.append(pl.BlockSpec((None, 1, S), lambda b, h, i: (b * HEADS + h, 0, 0)))
        out_shape.append(_sds((B * HEADS, 1, S), F32))
    return pl.pallas_call(
        body, name="attn_bwd", grid=(B, HEADS, nq), in_specs=in_specs, out_specs=out_specs, out_shape=out_shape,
        scratch_shapes=[pltpu.VMEM((S, HP), F32), pltpu.VMEM((S, HP), F32)],
        compiler_params=_cparams(VMEM_BIG))(*args)


def _tri(n, upper):
    r, c = _iota((n, n), 0), _iota((n, n), 1)
    return ((r <= c) if upper else (r >= c)).astype(F32)


def fox_gate_fwd(proj, bcol, S):
    T = proj.shape[0]
    B = T // S
    ts = _tile(S, 512)
    nt = S // ts

    def body(p_ref, b_ref, o_ref, carry):
        @pl.when(pl.program_id(1) == 0)
        def _():
            carry[...] = jnp.zeros_like(carry)
        cf = jnp.transpose(p_ref[...])[0:8, :]
        lf = jax.nn.log_sigmoid(cf + b_ref[...])
        cum = _sel_r(lf, _tri(ts, True)) + carry[...]
        o_ref[...] = cum
        carry[...] += jnp.sum(lf, axis=1, keepdims=True)

    return pl.pallas_call(
        body, name="fox_gate_fwd", grid=(B, nt),
        in_specs=[pl.BlockSpec((ts, HP), lambda b, j: (b * nt + j, C_CF // HP)), pl.BlockSpec((8, 1), lambda b, j: (0, 0))],
        out_specs=pl.BlockSpec((None, 8, ts), lambda b, j: (b, 0, j)), out_shape=_sds((B, 8, S), F32),
        scratch_shapes=[pltpu.VMEM((8, 1), F32)],
        compiler_params=_cparams(VMEM_BIG))(proj, bcol)


def fox_gate_bwd(proj, bcol, dcum, S):
    T = proj.shape[0]
    B = T // S
    ts = _tile(S, 512)
    nt = S // ts

    def body(p_ref, b_ref, dc_ref, dp_ref, db_ref, carry):
        b, j = pl.program_id(0), pl.program_id(1)

        @pl.when(j == 0)
        def _():
            carry[...] = jnp.zeros_like(carry)

        @pl.when((b == 0) & (j == 0))
        def _():
            db_ref[...] = jnp.zeros_like(db_ref)

        cf = jnp.transpose(p_ref[...])[0:8, :]
        dc = dc_ref[...]
        dlf = _sel_r(dc, _tri(ts, False)) + carry[...]
        carry[...] += jnp.sum(dc, axis=1, keepdims=True)
        dcf = dlf * jax.nn.sigmoid(-(cf + b_ref[...]))
        db_ref[...] += jnp.sum(dcf, axis=1, keepdims=True)
        full = jnp.concatenate([dcf, jnp.zeros((HP - 8, ts), F32)], axis=0)
        dp_ref[...] = jnp.transpose(full).astype(BF16)

    def rev(b, j):
        return nt - 1 - j

    return pl.pallas_call(
        body, name="fox_gate_bwd", grid=(B, nt),
        in_specs=[pl.BlockSpec((ts, HP), lambda b, j: (b * nt + rev(b, j), C_CF // HP)),
                  pl.BlockSpec((8, 1), lambda b, j: (0, 0)),
                  pl.BlockSpec((None, 8, ts), lambda b, j: (b, 0, rev(b, j)))],
        out_specs=[pl.BlockSpec((ts, HP), lambda b, j: (b * nt + rev(b, j), 0)), pl.BlockSpec((8, 1), lambda b, j: (0, 0))],
        out_shape=[_sds((T, HP), BF16), _sds((8, 1), F32)],
        scratch_shapes=[pltpu.VMEM((8, 1), F32)],
        compiler_params=_cparams(VMEM_BIG))(proj, bcol, dcum)


def _mla_pre(blk, gq, gkv, wq, wkv, place, cos_q, sin_q, cs_k):
    nq = _rms(blk[:, 0:B_Q_LORA], gq)
    nkv = _rms(blk[:, B_Q_LORA:B_Q_LORA + B_KV_LORA], gkv)
    qq = _mm(nq, wq)
    q = qq[:, 0:HEADS * HP] * cos_q + qq[:, HEADS * HP:] * sin_q
    kv = _mm(nkv, wkv)
    k = kv[:, 0:HEADS * HP] + _mm(blk[:, B_Q_LORA + B_KV_LORA:] * cs_k, place)
    return q, k, kv[:, HEADS * HP:]


def mla_pre_fwd(proj, gq, gkv, wq, wkv, place, cos_q, sin_q, cs_k, S):
    T = proj.shape[0]
    tm = _tile(S, 512)
    tpb = S // tm
    W = HEADS * HP

    def body(p_ref, gq_ref, gkv_ref, wq_ref, wkv_ref, pl_ref, cq_ref, sq_ref, ck_ref, q_ref, k_ref, v_ref):
        q, k, v = _mla_pre(p_ref[...], gq_ref[...], gkv_ref[...], wq_ref[...], wkv_ref[...], pl_ref[...],
                           cq_ref[...], sq_ref[...], ck_ref[...])
        q_ref[...] = q.astype(BF16)
        k_ref[...] = k.astype(BF16)
        v_ref[...] = v.astype(BF16)

    def full(a):
        return pl.BlockSpec(a.shape, lambda i: (0,) * a.ndim)

    tok = pl.BlockSpec((tm, W), lambda i: (i, 0))
    return pl.pallas_call(
        body, name="mla_pre_fwd", grid=(T // tm,),
        in_specs=[pl.BlockSpec((tm, W), lambda i: (i, C_B // W)), full(gq), full(gkv), full(wq), full(wkv), full(place),
                  pl.BlockSpec((tm, W), lambda i: (i % tpb, 0)), pl.BlockSpec((tm, W), lambda i: (i % tpb, 0)),
                  pl.BlockSpec((tm, HP), lambda i: (i % tpb, 0))],
        out_specs=[tok] * 3, out_shape=[_sds((T, W), BF16)] * 3,
        compiler_params=_cparams(VMEM_BIG))(proj, gq, gkv, wq, wkv, place, cos_q, sin_q, cs_k)


def mla_pre_bwd(proj, gq, gkv, wq, wkv, place, cos_q, sin_q, cs_k, dq, dk, dv, S):
    T = proj.shape[0]
    tm = _tile(S, 512)
    tpb = S // tm
    W = HEADS * HP

    def body(p_ref, gq_ref, gkv_ref, wq_ref, wkv_ref, pl_ref, cq_ref, sq_ref, ck_ref, dq_ref, dk_ref, dv_ref,
             dp_ref, dgq_ref, dgkv_ref, dwq_ref, dwkv_ref):
        @pl.when(pl.program_id(0) == 0)
        def _():
            for r in (dgq_ref, dgkv_ref, dwq_ref, dwkv_ref):
                r[...] = jnp.zeros_like(r)

        f = functools.partial(_mla_pre, place=pl_ref[...], cos_q=cq_ref[...], sin_q=sq_ref[...], cs_k=ck_ref[...])
        _, vjp = jax.vjp(f, p_ref[...], gq_ref[...], gkv_ref[...], wq_ref[...], wkv_ref[...])
        dp, dgq, dgkv, dwq, dwkv = vjp((dq_ref[...], dk_ref[...], dv_ref[...]))
        dp_ref[...] = dp.astype(BF16)
        dgq_ref[...] += dgq
        dgkv_ref[...] += dgkv
        dwq_ref[...] += dwq
        dwkv_ref[...] += dwkv

    def full(a):
        return pl.BlockSpec(a.shape, lambda i: (0,) * a.ndim)

    tok = pl.BlockSpec((tm, W), lambda i: (i, 0))
    return pl.pallas_call(
        body, name="mla_pre_bwd", grid=(T // tm,),
        in_specs=[pl.BlockSpec((tm, W), lambda i: (i, C_B // W)), full(gq), full(gkv), full(wq), full(wkv), full(place),
                  pl.BlockSpec((tm, W), lambda i: (i % tpb, 0)), pl.BlockSpec((tm, W), lambda i: (i % tpb, 0)),
                  pl.BlockSpec((tm, HP), lambda i: (i % tpb, 0)), tok, tok, tok],
        out_specs=[tok, full(gq), full(gkv), full(wq), full(wkv)],
        out_shape=[_sds((T, W), BF16), _sds(gq.shape, F32), _sds(gkv.shape, F32), _sds(wq.shape, F32), _sds(wkv.shape, F32)],
        compiler_params=_cparams(VMEM_BIG))(proj, gq, gkv, wq, wkv, place, cos_q, sin_q, cs_k, dq, dk, dv)


GMLP_CHUNKS = 4


def _gmlp_block(blk, lg, lb, ws, bs):
    u = jax.nn.gelu(blk[:, 0:GW])
    v = jax.nn.gelu(blk[:, GW:2 * GW])
    mu = jnp.mean(v, axis=-1, keepdims=True)
    var = jnp.mean(jnp.square(v - mu), axis=-1, keepdims=True)
    vn = (v - mu) * lax.rsqrt(var + LN_EPS) * lg + lb
    causal = _iota((D_CHUNK, D_CHUNK), 1) <= _iota((D_CHUNK, D_CHUNK), 0)
    group = _iota((1, GW), 1) // HD
    w = [jnp.where(causal, ws[g], 0.0) for g in range(HEADS)]
    chunks = []
    for c in range(blk.shape[0] // D_CHUNK):
        vc = vn[c * D_CHUNK:(c + 1) * D_CHUNK]
        mixed = jnp.zeros((D_CHUNK, GW), F32)
        for g in range(HEADS):
            mixed = mixed + jnp.where(group == g, _mm(w[g], vc) + bs[g], 0.0)
        chunks.append(mixed)
    return u * jnp.concatenate(chunks, axis=0)


def _gmlp_tile(T):
    return _tile(T, GMLP_CHUNKS * D_CHUNK) if T % (GMLP_CHUNKS * D_CHUNK) == 0 else D_CHUNK


def gmlp_fwd(proj, lg, lb, ws, bs):
    T = proj.shape[0]
    tm = _gmlp_tile(T)

    def body(p_ref, lg_ref, lb_ref, ws_ref, bs_ref, o_ref):
        o_ref[...] = _gmlp_block(p_ref[...], lg_ref[...], lb_ref[...], ws_ref[...], bs_ref[...]).astype(BF16)

    def full(a):
        return pl.BlockSpec(a.shape, lambda i: (0,) * a.ndim)

    return pl.pallas_call(
        body, name="gmlp_fwd", grid=(T // tm,),
        in_specs=[pl.BlockSpec((tm, 2 * GW), lambda i: (i, C_D // (2 * GW))), full(lg), full(lb), full(ws), full(bs)],
        out_specs=pl.BlockSpec((tm, GW), lambda i: (i, 0)), out_shape=_sds((T, GW), BF16),
        compiler_params=_cparams(VMEM_BIG))(proj, lg, lb, ws, bs)


def gmlp_bwd(proj, lg, lb, ws, bs, dcat):
    T = proj.shape[0]
    tm = _gmlp_tile(T)

    def body(p_ref, lg_ref, lb_ref, ws_ref, bs_ref, do_ref, dp_ref, dlg_ref, dlb_ref, dws_ref, dbs_ref):
        @pl.when(pl.program_id(0) == 0)
        def _():
            for r in (dlg_ref, dlb_ref, dws_ref, dbs_ref):
                r[...] = jnp.zeros_like(r)

        _, vjp = jax.vjp(_gmlp_block, p_ref[...], lg_ref[...], lb_ref[...], ws_ref[...], bs_ref[...])
        dp, dlg, dlb, dws, dbs = vjp(do_ref[...])
        dp_ref[...] = dp.astype(BF16)
        dlg_ref[...] += dlg
        dlb_ref[...] += dlb
        dws_ref[...] += dws
        dbs_ref[...] += dbs

    def full(a):
        return pl.BlockSpec(a.shape, lambda i: (0,) * a.ndim)

    return pl.pallas_call(
        body, name="gmlp_bwd", grid=(T // tm,),
        in_specs=[pl.BlockSpec((tm, 2 * GW), lambda i: (i, C_D // (2 * GW))), full(lg), full(lb), full(ws), full(bs),
                  pl.BlockSpec((tm, GW), lambda i: (i, O_D // GW))],
        out_specs=[pl.BlockSpec((tm, 2 * GW), lambda i: (i, 0)), full(lg), full(lb), full(ws), full(bs)],
        out_shape=[_sds((T, 2 * GW), BF16), _sds(lg.shape, F32), _sds(lb.shape, F32), _sds(ws.shape, F32), _sds(bs.shape, F32)],
        compiler_params=_cparams(VMEM_BIG))(proj, lg, lb, ws, bs, dcat)


def _lb_all(logits):
    m = jnp.max(logits, axis=0, keepdims=True)
    e = jnp.exp(logits - m)
    sm = e / jnp.sum(e, axis=0, keepdims=True)
    return jnp.concatenate([sm[0:1] - sm[0:1], (sm[0:1] + sm[1:2]) - sm[0:1]], axis=0)


def lb_fwd(logits):
    def body(l_ref, o_ref):
        o_ref[...] = _lb_all(l_ref[...])

    return pl.pallas_call(body, name="lb_fwd", out_shape=_sds(logits.shape, F32))(logits)


def lb_bwd(logits, dlb):
    def body(l_ref, d_ref, o_ref):
        _, vjp = jax.vjp(_lb_all, l_ref[...])
        o_ref[...] = vjp(d_ref[...])[0]

    return pl.pallas_call(body, name="lb_bwd", out_shape=_sds(logits.shape, F32))(logits, dlb)


def ada_grad(c_all, dmod_cols):
    N = dmod_cols.shape[1]
    tn = _tile(N, 1152)

    def body(c_ref, d_ref, o_ref):
        h = jax.nn.silu(c_ref[...]).astype(BF16)
        o_ref[...] = lax.dot_general(h, d_ref[...].astype(BF16), (((0,), (0,)), ((), ())), preferred_element_type=F32)

    nb = c_all.shape[0]
    return pl.pallas_call(
        body, name="ada_grad", grid=(N // tn,),
        in_specs=[pl.BlockSpec((nb, D), lambda j: (0, 0)), pl.BlockSpec((nb, tn), lambda j: (0, j))],
        out_specs=pl.BlockSpec((D, tn), lambda j: (0, j)), out_shape=_sds((D, N), F32),
        compiler_params=_cparams(VMEM_BIG))(c_all, dmod_cols)


def sum_slots(a, n, name):
    _, R, C = a.shape
    tr = _row_tile(R, C, n)

    def body(a_ref, o_ref):
        acc = a_ref[0]
        for k in range(1, n):
            acc = acc + a_ref[k]
        o_ref[...] = acc

    return pl.pallas_call(
        body, name=name, grid=(R // tr,),
        in_specs=[pl.BlockSpec((n, tr, C), lambda i: (0, i, 0))],
        out_specs=pl.BlockSpec((tr, C), lambda i: (i, 0)), out_shape=_sds((R, C), F32),
        compiler_params=_cparams(VMEM_BIG))(a)


def _row_tile(R, C=D, n=1, mult=8, elems=1 << 18):
    limit = max(mult, elems // (C * n))
    for t in range(limit - limit % mult, mult - 1, -mult):
        if R % t == 0:
            return t
    return R


def adamw(w, g, m, v, name, echo=False):
    R, C = w.shape
    tr = _row_tile(R, C, elems=1 << 19)
    c1 = 1.0 - ADAM_B1 ** ADAM_STEP
    c2 = 1.0 - ADAM_B2 ** ADAM_STEP
    n_out = 4 if echo else 3

    def body(w_ref, g_ref, m_ref, v_ref, d_ref, nm_ref, nv_ref, *g_out):
        g_ = g_ref[...]
        nm = ADAM_B1 * m_ref[...] + (1.0 - ADAM_B1) * g_
        nv = ADAM_B2 * v_ref[...] + (1.0 - ADAM_B2) * jnp.square(g_)
        d_ref[...] = -ADAM_LR * ((nm / c1) / (jnp.sqrt(nv / c2) + ADAM_EPS) + ADAM_WD * w_ref[...])
        nm_ref[...] = nm
        nv_ref[...] = nv
        if echo:
            g_out[0][...] = g_

    spec = pl.BlockSpec((tr, C), lambda i: (i, 0))
    return pl.pallas_call(body, name=name, grid=(R // tr,), in_specs=[spec] * 4, out_specs=[spec] * n_out,
                          out_shape=[_sds((R, C), F32)] * n_out, compiler_params=_cparams(VMEM_BIG))(w, g, m, v)


def _rot_cols(w):
    return jnp.concatenate([-w[:, 16:32], w[:, 0:16]], axis=1)


def _fold_rot(d):
    return jnp.concatenate([d[:, 16:32], -d[:, 0:16]], axis=1)


def _pad_heads(w, off, axis):
    parts = []
    for h in range(HEADS):
        piece = lax.slice_in_dim(w, off + HD * h, off + HD * (h + 1), axis=axis)
        parts += [piece, jnp.zeros_like(piece)]
    return parts


def _unpad_heads(d, off, axis):
    return [lax.slice_in_dim(d, off + HP * h, off + HP * h + HD, axis=axis) for h in range(HEADS)]


def mix_in_ext(w):
    z = lambda n: jnp.zeros((w.shape[0], n), w.dtype)
    kr = w[:, 1408:1440]
    cols = [w[:, 0:1408], kr, _rot_cols(kr), z(64)]
    cols += _pad_heads(w, 1440, 1) + _pad_heads(w, 1696, 1) + _pad_heads(w, 1952, 1)
    cols += [w[:, 2212:2724], w[:, 2208:2212], z(NP - C_CF - HEADS)]
    return jnp.concatenate(cols, axis=1)


def mix_in_unext(d):
    kr = d[:, 1408:1440] + _fold_rot(d[:, 1440:1472])
    cols = [d[:, 0:1408], kr] + _unpad_heads(d, C_CQ, 1) + _unpad_heads(d, C_CK, 1) + _unpad_heads(d, C_CV, 1)
    cols += [d[:, C_CF:C_CF + HEADS], d[:, C_D:C_D + 2 * GW]]
    return jnp.concatenate(cols, axis=1)


def mix_out_ext(w):
    return jnp.concatenate([w[0:GW]] + _pad_heads(w, GW, 0) + _pad_heads(w, 2 * GW, 0) + [w[3 * GW:4 * GW]], axis=0)


def mix_out_unext(d):
    return jnp.concatenate([d[0:GW]] + _unpad_heads(d, O_B, 0) + _unpad_heads(d, O_C, 0) + [d[O_D:O_D + GW]], axis=0)


def uq_ext(w):
    z = lambda n: jnp.zeros((w.shape[0], n), w.dtype)
    a, b = [], []
    for h in range(HEADS):
        o = (B_NOPE + B_ROPE) * h
        a += [w[:, o:o + B_NOPE + B_ROPE], z(32)]
        b += [z(B_NOPE), _rot_cols(w[:, o + B_NOPE:o + B_NOPE + B_ROPE]), z(32)]
    return jnp.concatenate(a + b, axis=1)


def uq_unext(d):
    cols = []
    for h in range(HEADS):
        o = HP * h
        cols += [d[:, o:o + B_NOPE], d[:, o + B_NOPE:o + B_NOPE + B_ROPE]
                 + _fold_rot(d[:, HEADS * HP + o + B_NOPE:HEADS * HP + o + B_NOPE + B_ROPE])]
    return jnp.concatenate(cols, axis=1)


def ukv_ext(w):
    z = jnp.zeros((w.shape[0], HD), w.dtype)
    k, v = [], []
    for h in range(HEADS):
        k += [w[:, 2 * HD * h:2 * HD * h + HD], z]
        v += [w[:, 2 * HD * h + HD:2 * HD * (h + 1)], z]
    return jnp.concatenate(k + v, axis=1)


def ukv_unext(d):
    cols = []
    for h in range(HEADS):
        cols += [d[:, HP * h:HP * h + HD], d[:, HEADS * HP + HP * h:HEADS * HP + HP * h + HD]]
    return jnp.concatenate(cols, axis=1)


def rope_tables(S):
    half = B_ROPE // 2
    inv_freq = ROPE_THETA ** (-jnp.arange(half, dtype=F32) / half)
    ang = jnp.arange(S).astype(F32)[:, None] * inv_freq[None, :]
    cos = jnp.tile(jnp.cos(ang), (1, 2))
    sin = jnp.tile(jnp.sin(ang), (1, 2))
    one, zero = jnp.ones((S, B_NOPE), F32), jnp.zeros((S, B_NOPE), F32)
    z32 = jnp.zeros((S, 32), F32)
    cos_q = jnp.tile(jnp.concatenate([one, cos, z32], axis=1), (1, HEADS))
    sin_q = jnp.tile(jnp.concatenate([zero, sin, z32], axis=1), (1, HEADS))
    cs_k = jnp.concatenate([cos, sin, zero], axis=1)
    place = np.zeros((HP, HEADS * HP), np.float32)
    for h in range(HEADS):
        for j in range(B_ROPE):
            place[j, h * HP + B_NOPE + j] = 1.0
            place[B_ROPE + j, h * HP + B_NOPE + j] = 1.0
    return cos_q, sin_q, cs_k, jnp.asarray(place, BF16)


def layer_fwd(x, mod, get, tabs, S):
    cos_q, sin_q, cs_k, place = tabs
    p = dict(get("ffn1", x))
    l = p["wl"]
    zg1, zu1, act1 = ffn_in_fwd(x, mod, p["ffn1_in"], l, 0, 1, S)
    y1, x1 = out_ln_fwd(act1, p["ffn1_out"], x, mod, p["ln_g"][0:1], p["ln_b"][0:1], 2, 0.5, S, l)
    p.update(get("mix", x1))
    proj = mix_in_fwd(x1, mod, p["mix_in"], 3, 4, S)
    o_a, states = hgrn_fwd(proj, p["lb"], p["ng"], S)
    q_b, k_b, v_b = mla_pre_fwd(proj, p["gq"], p["gkv"], p["wq"], p["wkv"], place, cos_q, sin_q, cs_k, S)
    o_b = attn_fwd(q_b, 0, k_b, 0, v_b, 0, None, (B_NOPE + B_ROPE) ** -0.5, S)
    cum = fox_gate_fwd(proj, p["bcol"], S)
    o_c = attn_fwd(proj, C_CQ // HP, proj, C_CK // HP, proj, C_CV // HP, cum, HD ** -0.5, S)
    o_d = gmlp_fwd(proj, p["g_lg"], p["g_lb"], p["ws"], p["bs"])
    cat = jnp.concatenate([o_a, o_b, o_c, o_d], axis=1)
    y2, x2 = out_ln_fwd(cat, p["mix_out"], x1, mod, p["ln_g"][1:2], p["ln_b"][1:2], 5, 1.0, S)
    p.update(get("ffn2", x2))
    zg3, zu3, act3 = ffn_in_fwd(x2, mod, p["ffn2_in"], l, 6, 7, S)
    y3, x3 = out_ln_fwd(act3, p["ffn2_out"], x2, mod, p["ln_g"][2:3], p["ln_b"][2:3], 8, 0.5, S, l)
    saved = dict(x=x, zg1=zg1, zu1=zu1, act1=act1, y1=y1, x1=x1, proj=proj, states=states, q_b=q_b, k_b=k_b, v_b=v_b,
                 cum=cum, cat=cat, y2=y2, x2=x2, zg3=zg3, zu3=zu3, act3=act3, y3=y3, p=p)
    return x3, saved


def _ffn_bwd(dxn, x_in, y, zg, zu, act, mod, w_in, w_out, l, lg, lb, idx, S, emit):
    sh, sc, gate = idx
    dres, dy, dgate, dlg, dlb = ln_res_bwd(dxn, x_in, y, mod, lg, lb, gate, 0.5, S)
    dzg, dzu = swiglu_bwd(dy, w_out, l, zg, zu, S)
    dw_out = tn_mm(act, dy, FF // 2).reshape(N_CHIPS, FF // N_CHIPS, D)
    dw_in = tn_mm_mod_shards(x_in, mod, dzg, dzu, sh, sc, S)
    mod = mod + emit(dw_in, dw_out)
    dx, dsh, dsc = nt_mod_bwd([dzg, dzu], w_in, [0, 2], x_in, mod, dres, sc, S, FF // 2, l)
    return dx, dw_in, dw_out, dlg, dlb, {sh: dsh, sc: dsc, gate: dgate}, mod


def layer_bwd(dx3, mod, sv, tabs, S, emit):
    cos_q, sin_q, cs_k, place = tabs
    p = sv["p"]
    l = p["wl"]
    g = {}
    dm = {}

    def emit_ffn(part):
        def f(dw_in, dw_out):
            g[part + "_in"], g[part + "_out"] = dw_in, dw_out
            return emit(part, g)
        return f

    dx2, _, _, dlg2, dlb2, d, mod = _ffn_bwd(
        dx3, sv["x2"], sv["y3"], sv["zg3"], sv["zu3"], sv["act3"], mod, p["ffn2_in"], p["ffn2_out"], l,
        p["ln_g"][2:3], p["ln_b"][2:3], (6, 7, 8), S, emit_ffn("ffn2"))
    dm.update(d)
    dres, dy2, dm[5], dlg1, dlb1 = ln_res_bwd(dx2, sv["x1"], sv["y2"], mod, p["ln_g"][1:2], p["ln_b"][1:2], 5, 1.0, S)
    dcat = nt_plain(dy2, p["mix_out"])
    g["mix_out"] = tn_mm(sv["cat"], dy2, NCAT // 2)
    proj = sv["proj"]
    d_a, g["lb"], g["ng"] = hgrn_bwd(proj, sv["states"], dcat, p["lb"], p["ng"], S)
    dq_c, dk_c, dv_c, dcum = attn_bwd(proj, C_CQ // HP, proj, C_CK // HP, proj, C_CV // HP, sv["cum"], dcat,
                                      O_C // HP, HD ** -0.5, S, BF16)
    B = proj.shape[0] // S
    dcum = jnp.concatenate([dcum.reshape(B, HEADS, S), jnp.zeros((B, 8 - HEADS, S), F32)], axis=1)
    d_cf, g["bcol"] = fox_gate_bwd(proj, p["bcol"], dcum, S)
    dq_b, dk_b, dv_b = attn_bwd(sv["q_b"], 0, sv["k_b"], 0, sv["v_b"], 0, None, dcat, O_B // HP,
                                (B_NOPE + B_ROPE) ** -0.5, S, F32)
    d_b, g["gq"], g["gkv"], g["wq"], g["wkv"] = mla_pre_bwd(
        proj, p["gq"], p["gkv"], p["wq"], p["wkv"], place, cos_q, sin_q, cs_k, dq_b, dk_b, dv_b, S)
    d_d, g["g_lg"], g["g_lb"], g["ws"], g["bs"] = gmlp_bwd(proj, p["g_lg"], p["g_lb"], p["ws"], p["bs"], dcat)
    dproj = jnp.concatenate([d_a, d_b, dq_c, dk_c, dv_c, d_d, d_cf, jnp.zeros_like(d_cf)], axis=1)
    g["mix_in"] = tn_mm_mod(sv["x1"], mod, dproj, 3, 4, S, NP_TILE)
    mod = mod + emit("mix", g)
    dx1, dm[3], dm[4] = nt_mod_bwd([dproj], p["mix_in"], [0], sv["x1"], mod, dres, 4, S, NP_TILE)
    last = []

    def emit_last(dw_in, dw_out):
        last.append(emit_ffn("ffn1")(dw_in, dw_out))
        return last[0]

    dx0, _, _, dlg0, dlb0, d, mod = _ffn_bwd(
        dx1, sv["x"], sv["y1"], sv["zg1"], sv["zu1"], sv["act1"], mod, p["ffn1_in"], p["ffn1_out"], l,
        p["ln_g"][0:1], p["ln_b"][0:1], (0, 1, 2), S, emit_last)
    dm.update(d)
    g["ln_g"] = jnp.concatenate([dlg0, dlg1, dlg2], axis=0)
    g["ln_b"] = jnp.concatenate([dlb0, dlb1, dlb2], axis=0)
    dmod = jnp.concatenate([dm[i] for i in range(N_MOD)], axis=1)
    return dx0, dmod, g, last[0]


def local_step(x, c8, tgt, get, lb_logits, S, emit=None):
    B = x.shape[0] // S
    tabs = rope_tables(S)
    lb_all = lb_fwd(lb_logits)
    mods, saved = [], []
    h = x
    for l in range(DEPTH):
        pa = get(l, "ada", h)
        mod = mod_fwd(c8, pa["ada_w"], pa["wl"], pa["ada_b"])[0:B].reshape(B, N_MOD, D)

        def get_l(part, after, l=l):
            p = dict(get(l, part, after))
            if part == "mix":
                p["lb"] = lb_all[l:l + 1]
            return p

        h, sv = layer_fwd(h, mod, get_l, tabs, S)
        mods.append(mod)
        saved.append(sv)
    loss_tile, dh = loss_head(h, tgt)
    grads, dmods, dlb = [None] * DEPTH, [None] * DEPTH, [None] * DEPTH
    tie = jnp.zeros((), F32)
    for l in reversed(range(DEPTH)):
        emit_l = (lambda part, g: jnp.zeros((), F32)) if emit is None else functools.partial(emit, l)
        dh, dmods[l], grads[l], tie = layer_bwd(dh, mods[l] + tie, saved[l], tabs, S, emit_l)
        dlb[l] = grads[l].pop("lb")
    d_logits = lb_bwd(lb_logits, jnp.concatenate(dlb, axis=0))
    return loss_tile, dh, dmods, grads, d_logits


ANY = pl.BlockSpec(memory_space=pl.ANY)


def _place():
    x, y, c = lax.axis_index("x"), lax.axis_index("y"), lax.axis_index("c")
    chips = [(1 - x, y), (x, 1 - y), (1 - x, 1 - y)]
    return x, y, c, chips


def _rcopy(src, dst, sems, k, to):
    send_sems, recv_sems = sems
    return pltpu.make_async_remote_copy(src_ref=src, dst_ref=dst, send_sem=send_sems.at[k], recv_sem=recv_sems.at[k],
                                        device_id=to, device_id_type=MESH)


def _dma_sems(n_remote, n_local):
    return [pltpu.SemaphoreType.DMA((n_remote,)), pltpu.SemaphoreType.DMA((n_remote,)), pltpu.SemaphoreType.DMA((n_local,))]


def own_slot(src, chip):
    L = src.shape[0]
    return lax.dynamic_update_slice(lax.empty((L, N_CHIPS) + src.shape[1:], src.dtype), src[:, None], (0, chip, 0, 0))


HBM_SPEC = pl.BlockSpec(memory_space=pltpu.HBM)
SEM_SPEC = pl.BlockSpec(memory_space=pltpu.SEMAPHORE)
DATAFLOW = pltpu.SideEffectType.DATAFLOW_SIDE_EFFECTING


def _split_start(srcs, lands, copies, n_copies, dep, name):
    n, m = len(srcs), len(lands)

    def body(*refs):
        ins = refs[:n + m]
        send_sems, recv_sems = refs[n + m + 1], refs[n + m + 2]
        token = refs[-1]
        for k, (src, dst, to) in enumerate(copies(ins[:n], ins[n:], _place())):
            pltpu.make_async_remote_copy(src_ref=src, dst_ref=dst, send_sem=send_sems.at[k], recv_sem=recv_sems.at[k],
                                         device_id=to, device_id_type=MESH).start()
        token[...] = jnp.zeros_like(token)

    arrs = list(srcs) + list(lands)
    outs = pl.pallas_call(
        body, name=name,
        out_shape=(pltpu.SemaphoreType.DMA((n_copies,)), pltpu.SemaphoreType.DMA((n_copies,)),
                   *[pltpu.HBM(a.shape, a.dtype) for a in arrs], _sds((8, 128), F32)),
        in_specs=[HBM_SPEC] * (n + m) + [ANY],
        out_specs=(SEM_SPEC, SEM_SPEC, *[HBM_SPEC] * (n + m), pl.BlockSpec(memory_space=pltpu.VMEM)),
        input_output_aliases={i: 2 + i for i in range(n + m)},
        compiler_params=pltpu.CompilerParams(has_side_effects=DATAFLOW),
    )(*[pltpu.with_memory_space_constraint(a, pltpu.HBM) for a in arrs], dep)
    return outs[0], outs[1], list(outs[2:2 + n]), list(outs[2 + n:2 + n + m]), outs[-1]


def _split_wait(handle, arrivals, after, name):
    send_sems, recv_sems, srcs, lands, _ = handle
    n, m = len(srcs), len(lands)

    def body(*refs):
        ins = refs[:n + m]
        send_sems, recv_sems = refs[n + m], refs[n + m + 1]
        x, y, c, chips = place = _place()
        for k, (src, dst) in enumerate(arrivals(ins[:n], ins[n:], place)):
            cp = pltpu.make_async_remote_copy(src_ref=src, dst_ref=dst, send_sem=send_sems.at[k], recv_sem=recv_sems.at[k],
                                              device_id=(x, y, 1 - c), device_id_type=MESH)
            cp.wait_send()
            cp.wait_recv()

    arrs = list(srcs) + list(lands)
    outs = pl.pallas_call(
        body, name=name, out_shape=[pltpu.HBM(a.shape, a.dtype) for a in arrs],
        in_specs=[HBM_SPEC] * (n + m) + [SEM_SPEC, SEM_SPEC, ANY], out_specs=[HBM_SPEC] * (n + m),
        input_output_aliases={i: i for i in range(n + m)},
        compiler_params=pltpu.CompilerParams(has_side_effects=DATAFLOW),
    )(*arrs, send_sems, recv_sems, after)
    return list(outs[:n]), list(outs[n:])


def _ag_part(ref, k, hc):
    rh = ref.shape[2] // 2
    return ref.at[:, k, pl.ds(hc * rh, rh), :]


def ag_start(srcs, lands, dep, name):
    def copies(s, d, place):
        x, y, c, chips = place
        out = []
        for j, (px, py) in enumerate(chips):
            for i in range(len(s)):
                rh = s[i].shape[1] // 2
                out.append((s[i].at[:, pl.ds(c * rh, rh), :], _ag_part(d[i], 2 * x + y, c), (px, py, c)))
        return out

    return _split_start(srcs, lands, copies, 3 * len(srcs), dep, name)


def ag_wait(handle, after, name):
    def arrivals(s, d, place):
        x, y, c, chips = place
        out = []
        for j, (px, py) in enumerate(chips):
            for i in range(len(s)):
                rh = s[i].shape[1] // 2
                out.append((s[i].at[:, pl.ds(c * rh, rh), :], _ag_part(d[i], 2 * px + py, c)))
        return out

    return _split_wait(handle, arrivals, after, name)


def ag_forward(lands, name):
    n = len(lands)

    def body(*refs):
        bufs, token = refs[n:2 * n], refs[2 * n]
        send_sems, recv_sems = refs[2 * n + 1:]
        x, y, c, chips = _place()
        sems = (send_sems, recv_sems)
        token[...] = jnp.zeros_like(token)
        cps = []
        for j, (px, py) in enumerate(chips):
            for i in range(n):
                part = _ag_part(bufs[i], 2 * px + py, c)
                cps.append(_rcopy(part, part, sems, 3 * i + j, (x, y, 1 - c)))
        for cp in cps:
            cp.start()
        for j, (px, py) in enumerate(chips):
            for i in range(n):
                part = _ag_part(bufs[i], 2 * px + py, 1 - c)
                _rcopy(part, part, sems, 3 * i + j, (x, y, 1 - c)).wait_recv()
        for cp in cps:
            cp.wait_send()

    outs = pl.pallas_call(
        body, name=name, out_shape=[_sds(a.shape, a.dtype) for a in lands] + [_sds((8, 128), F32)],
        in_specs=[ANY] * n, out_specs=[ANY] * n + [pl.BlockSpec(memory_space=pltpu.VMEM)],
        input_output_aliases={i: i for i in range(n)}, scratch_shapes=_dma_sems(3 * n, 1)[:2])(*lands)
    return list(outs[:n]), outs[n]


def rs_start(hs, lands, dep, name):
    def copies(s, d, place):
        x, y, c, chips = place
        return [(s[i].at[2 * px + py], d[i].at[2 * x + y], (px, py, c)) for j, (px, py) in enumerate(chips) for i in range(len(s))]

    return _split_start(hs, lands, copies, 3 * len(hs), dep, name)


def rs_wait(handle, after, name):
    def arrivals(s, d, place):
        x, y, c, chips = place
        return [(s[i].at[2 * px + py], d[i].at[2 * px + py]) for j, (px, py) in enumerate(chips) for i in range(len(s))]

    return _split_wait(handle, arrivals, after, name)


def sibling_swap(arrs, name):
    n = len(arrs)
    rh = [a.shape[1] // 2 for a in arrs]

    def body(*refs):
        srcs, outs = refs[:n], refs[n:2 * n]
        send_sems, recv_sems = refs[2 * n:]
        x, y, c, _ = _place()
        cps = [_rcopy(srcs[i].at[:, pl.ds((1 - c) * rh[i], rh[i]), :], outs[i], (send_sems, recv_sems), i, (x, y, 1 - c))
               for i in range(n)]
        for cp in cps:
            cp.start()
        for cp in cps:
            cp.wait()

    return pl.pallas_call(
        body, name=name, out_shape=[_sds((N_CHIPS, r, a.shape[2]), a.dtype) for a, r in zip(arrs, rh)],
        in_specs=[ANY] * n, out_specs=[ANY] * n, scratch_shapes=_dma_sems(n, 1)[:2])(*arrs)


def sum_into(land, base, l, core, name):
    _, rh, C = land.shape
    tr = _row_tile(rh, C, N_CHIPS, mult=16)
    nr = rh // tr

    def body(core_ref, land_ref, base_ref, o_ref):
        acc = land_ref[0].astype(F32)
        for k in range(1, N_CHIPS):
            acc = acc + land_ref[k].astype(F32)
        o_ref[...] = acc

    grid_spec = pltpu.PrefetchScalarGridSpec(
        num_scalar_prefetch=1, grid=(nr,),
        in_specs=[pl.BlockSpec((N_CHIPS, tr, C), lambda r, core_ref: (0, r, 0)), ANY],
        out_specs=pl.BlockSpec((None, tr, C), lambda r, core_ref: (l, core_ref[0] * nr + r, 0)))
    return pl.pallas_call(body, name=name, grid_spec=grid_spec, out_shape=_sds(base.shape, base.dtype),
                          input_output_aliases={2: 0}, compiler_params=_cparams(VMEM_BIG))(
        core.reshape(1).astype(jnp.int32), land, base)


def sibling_join(bases, name):
    n = len(bases)

    def body(*refs):
        bufs = refs[n:2 * n]
        send_sems, recv_sems = refs[2 * n:]
        x, y, c, _ = _place()
        sems = (send_sems, recv_sems)

        def half(i, hc):
            rh = bufs[i].shape[1] // 2
            return bufs[i].at[:, pl.ds(hc * rh, rh), :]

        sends = [_rcopy(half(i, c), half(i, c), sems, i, (x, y, 1 - c)) for i in range(n)]
        for cp in sends:
            cp.start()
        for i in range(n):
            _rcopy(half(i, 1 - c), half(i, 1 - c), sems, i, (x, y, 1 - c)).wait_recv()
        for cp in sends:
            cp.wait_send()

    return pl.pallas_call(
        body, name=name, out_shape=[_sds(b.shape, b.dtype) for b in bases], in_specs=[ANY] * n, out_specs=[ANY] * n,
        input_output_aliases={i: i for i in range(n)}, scratch_shapes=_dma_sems(n, 1)[:2])(*bases)


def ag_all(blk):
    M, C = blk.shape

    def body(x_ref, out_ref, send_sems, recv_sems, loc_sem):
        x, y, c, chips = _place()
        sems = (send_sems, recv_sems)
        me, sibling = (x, y, c), (x, y, 1 - c)

        def slot(px, py, pc):
            return out_ref.at[4 * px + 2 * py + pc]

        mine = pltpu.make_async_copy(x_ref, slot(*me), loc_sem)
        mine.start()
        first = [_rcopy(x_ref, slot(*me), sems, 0, sibling)]
        first += [_rcopy(x_ref, slot(*me), sems, 1 + j, (*chip, c)) for j, chip in enumerate(chips)]
        for cp in first:
            cp.start()
        passed = [_rcopy(slot(*chip, c), slot(*chip, c), sems, 4 + j, sibling) for j, chip in enumerate(chips)]
        for j, chip in enumerate(chips):
            _rcopy(slot(*chip, c), slot(*chip, c), sems, 1 + j, me).wait_recv()
            passed[j].start()
        _rcopy(slot(*sibling), slot(*sibling), sems, 0, me).wait_recv()
        for j, chip in enumerate(chips):
            _rcopy(slot(*chip, 1 - c), slot(*chip, 1 - c), sems, 4 + j, me).wait_recv()
        for cp in first + passed:
            cp.wait_send()
        mine.wait()

    return pl.pallas_call(
        body, name="ag_all", out_shape=_sds((8, M, C), blk.dtype),
        in_specs=[pl.BlockSpec(memory_space=pltpu.VMEM)], out_specs=pl.BlockSpec(memory_space=pltpu.VMEM),
        scratch_shapes=[pltpu.SemaphoreType.DMA((7,)), pltpu.SemaphoreType.DMA((7,)), pltpu.SemaphoreType.DMA(())],
        compiler_params=_cparams(VMEM_BIG))(blk)


WEIGHTS = ["ada_w", "ada_b", "ln_g", "ln_b", "ffn1_w_in", "ffn1_w_out", "ffn2_w_in", "ffn2_w_out", "mix_w_in", "mix_w_out",
           "hgrn_lb_logits", "hgrn_norm_g", "mla_q_norm_g", "mla_kv_norm_g", "mla_w_uq", "mla_w_ukv", "fox_b_f",
           "gmlp_ln_g", "gmlp_ln_b", "gmlp_w_s", "gmlp_b_s"]
SMALL = ["hgrn_lb_logits", "hgrn_norm_g", "mla_q_norm_g", "mla_kv_norm_g", "fox_b_f", "gmlp_ln_g", "gmlp_ln_b",
         "gmlp_w_s", "gmlp_b_s", "ln_g", "ln_b"]
GATHERED = ["ada_w", "ffn1_w_in", "ffn1_w_out", "ffn2_w_in", "ffn2_w_out", "mix_w_in", "mix_w_out", "mla_w_uq", "mla_w_ukv"]
REDUCED = GATHERED[1:]


def _col_shards(a):
    cols = a.shape[1] // N_CHIPS
    return jnp.stack([a[:, k * cols:(k + 1) * cols] for k in range(N_CHIPS)])


def add_kept_half(a, got, core, name):
    _, R, C = a.shape
    rh = R // 2
    tr = _row_tile(rh, C, mult=16)
    nr = rh // tr

    def body(core_ref, a_ref, b_ref, o_ref):
        o_ref[...] = (a_ref[...].astype(F32) + b_ref[...].astype(F32)).astype(o_ref.dtype)

    half = pl.BlockSpec((None, tr, C), lambda k, r, core_ref: (k, r, 0))
    grid_spec = pltpu.PrefetchScalarGridSpec(
        num_scalar_prefetch=1, grid=(N_CHIPS, nr),
        in_specs=[pl.BlockSpec((None, tr, C), lambda k, r, core_ref: (k, core_ref[0] * nr + r, 0)), half],
        out_specs=half)
    return pl.pallas_call(body, name=name, grid_spec=grid_spec, out_shape=_sds((N_CHIPS, rh, C), BF16),
                          compiler_params=_cparams(VMEM_BIG))(core.reshape(1).astype(jnp.int32), a, got)


def _rows(parts, n_rows, dtype):
    flat = jnp.concatenate([p.reshape(-1) for p in parts])
    pad = n_rows * D - flat.shape[0]
    return jnp.concatenate([flat, jnp.zeros((pad,), dtype)]).reshape(n_rows, D)


def _take(flat, shapes):
    out, o = [], 0
    for shp in shapes:
        n = int(np.prod(shp))
        out.append(flat[o:o + n].reshape(shp))
        o += n
    return out


def _round_up(n, m):
    return -(-n // m) * m


def pack_small(w):
    parts = [w[n][l] for l in range(DEPTH) for n in SMALL]
    n = sum(int(np.prod(p.shape)) for p in parts)
    return _rows(parts, _round_up(-(-n // D), 8), F32)


def unpack_small(pk, like):
    shapes = [like[n].shape[1:] for l in range(DEPTH) for n in SMALL]
    pieces = _take(pk.reshape(-1), shapes)
    names = [n for l in range(DEPTH) for n in SMALL]
    return {n: jnp.stack([p for p, m in zip(pieces, names) if m == n]) for n in SMALL}


def kernel(x, c, ada_w, ada_b, ln_g, ln_b, ffn1_w_in, ffn1_w_out, ffn2_w_in, ffn2_w_out, mix_w_in, mix_w_out, hgrn_lb_logits, hgrn_norm_g, mla_q_norm_g, mla_kv_norm_g, mla_w_uq, mla_w_ukv, fox_b_f, gmlp_ln_g, gmlp_ln_b, gmlp_w_s, gmlp_b_s, loss_target, m_ada_w, m_ada_b, m_ln_g, m_ln_b, m_ffn1_w_in, m_ffn1_w_out, m_ffn2_w_in, m_ffn2_w_out, m_mix_w_in, m_mix_w_out, m_hgrn_lb_logits, m_hgrn_norm_g, m_mla_q_norm_g, m_mla_kv_norm_g, m_mla_w_uq, m_mla_w_ukv, m_fox_b_f, m_gmlp_ln_g, m_gmlp_ln_b, m_gmlp_w_s, m_gmlp_b_s, v_ada_w, v_ada_b, v_ln_g, v_ln_b, v_ffn1_w_in, v_ffn1_w_out, v_ffn2_w_in, v_ffn2_w_out, v_mix_w_in, v_mix_w_out, v_hgrn_lb_logits, v_hgrn_norm_g, v_mla_q_norm_g, v_mla_kv_norm_g, v_mla_w_uq, v_mla_w_ukv, v_fox_b_f, v_gmlp_ln_g, v_gmlp_ln_b, v_gmlp_w_s, v_gmlp_b_s):
    w = dict(zip(WEIGHTS, (ada_w, ada_b, ln_g, ln_b, ffn1_w_in, ffn1_w_out, ffn2_w_in, ffn2_w_out, mix_w_in, mix_w_out, hgrn_lb_logits, hgrn_norm_g, mla_q_norm_g, mla_kv_norm_g, mla_w_uq, mla_w_ukv, fox_b_f, gmlp_ln_g, gmlp_ln_b, gmlp_w_s, gmlp_b_s)))
    m = dict(zip(WEIGHTS, (m_ada_w, m_ada_b, m_ln_g, m_ln_b, m_ffn1_w_in, m_ffn1_w_out, m_ffn2_w_in, m_ffn2_w_out, m_mix_w_in, m_mix_w_out, m_hgrn_lb_logits, m_hgrn_norm_g, m_mla_q_norm_g, m_mla_kv_norm_g, m_mla_w_uq, m_mla_w_ukv, m_fox_b_f, m_gmlp_ln_g, m_gmlp_ln_b, m_gmlp_w_s, m_gmlp_b_s)))
    v = dict(zip(WEIGHTS, (v_ada_w, v_ada_b, v_ln_g, v_ln_b, v_ffn1_w_in, v_ffn1_w_out, v_ffn2_w_in, v_ffn2_w_out, v_mix_w_in, v_mix_w_out, v_hgrn_lb_logits, v_hgrn_norm_g, v_mla_q_norm_g, v_mla_kv_norm_g, v_mla_w_uq, v_mla_w_ukv, v_fox_b_f, v_gmlp_ln_g, v_gmlp_ln_b, v_gmlp_w_s, v_gmlp_b_s)))
    Bl, S, _ = x.shape
    T = Bl * S
    core = lax.axis_index("c")
    chip = 2 * lax.axis_index("x") + lax.axis_index("y")

    def shard(key):
        n, l = key
        if n == "ln":
            return jnp.concatenate([ln_g[l:l + 1], ln_b[l:l + 1], jnp.zeros((1, 2, D // N_CHIPS), F32)], axis=1)
        return w[n][l:l + 1].astype(BF16)

    mixers = ["mix_w_in", "mix_w_out", "mla_w_uq", "mla_w_ukv"]
    groups = [[("ada_w", 0), ("ffn1_w_in", 0), ("ffn1_w_out", 0), ("ln", 0)],
              [(n, 0) for n in mixers + ["ffn2_w_in", "ffn2_w_out"]],
              [(n, 1) for n in GATHERED + ["ln"]]]
    srcs = [[shard(k) for k in grp] for grp in groups]
    lands = [[own_slot(s, chip) for s in srcs[0]]]
    handle0 = ag_start(srcs[0], lands[0], jnp.zeros((8, 128), F32), "ag_start_0")
    chip_later = chip + handle0[-1][0, 0].astype(jnp.int32)
    lands += [[own_slot(s, chip_later) for s in grp] for grp in srcs[1:]]
    first, token = ag_forward(ag_wait(handle0, lands[2][0], "ag_wait_0")[1], "ag_forward_0")
    have = dict(zip(groups[0], first))
    handles = {}
    for gi in (1, 2):
        handles[gi] = ag_start(srcs[gi], lands[gi], token, "ag_start_%d" % gi)
        token = handles[gi][-1]
    c8 = jnp.concatenate([c, jnp.zeros((8 - Bl, D), F32)], axis=0)
    c8 = c8 + token[0, 0]

    def cat_cols(a):
        return jnp.concatenate([a[0, k] for k in range(N_CHIPS)], axis=1)

    def get(l, part, after):
        gi = 2 if l == 1 else (0 if part in ("ada", "ffn1") else 1)
        if gi in handles:
            arrived, _ = ag_forward(ag_wait(handles.pop(gi), after, "ag_wait_%d" % gi)[1], "ag_forward_%d" % gi)
            have.update(zip(groups[gi], arrived))
        if part == "ada":
            return dict(ada_w=have[("ada_w", l)], wl=0, ada_b=ada_b[l][None])
        if part == "ffn1":
            ln_full = jnp.moveaxis(have[("ln", l)][0], 0, 1).reshape(8, D)
            return dict(ffn1_in=have[("ffn1_w_in", l)], ffn1_out=have[("ffn1_w_out", l)], wl=0,
                        ln_g=ln_full[0:3], ln_b=ln_full[3:6])
        if part == "ffn2":
            return dict(ffn2_in=have[("ffn2_w_in", l)], ffn2_out=have[("ffn2_w_out", l)])
        return dict(
            mix_in=mix_in_ext(cat_cols(have[("mix_w_in", l)])), mix_out=mix_out_ext(have[("mix_w_out", l)].reshape(D, D)),
            wq=uq_ext(cat_cols(have[("mla_w_uq", l)])).astype(F32), wkv=ukv_ext(cat_cols(have[("mla_w_ukv", l)])).astype(F32),
            ng=hgrn_norm_g[l][None], gq=mla_q_norm_g[l][None], gkv=mla_kv_norm_g[l][None],
            bcol=jnp.concatenate([fox_b_f[l], jnp.zeros((8 - HEADS,), F32)])[:, None],
            g_lg=gmlp_ln_g[l][None], g_lb=gmlp_ln_b[l][None], ws=gmlp_w_s[l], bs=gmlp_b_s[l][:, :, None])

    pending = []

    def emit(l, part, g):
        if part == "mix":
            names = mixers
            by_chip = [_col_shards(mix_in_unext(g["mix_in"])), mix_out_unext(g["mix_out"]).reshape(N_CHIPS, D // N_CHIPS, D),
                       _col_shards(uq_unext(g["wq"])).astype(BF16), _col_shards(ukv_unext(g["wkv"])).astype(BF16)]
        else:
            names = [part + "_w_in", part + "_w_out"]
            by_chip = [g[part + "_in"], g[part + "_out"]]
        tag = "%d_%s" % (l, part)
        got = sibling_swap(by_chip, "sibling_swap_" + tag)
        chip_sum = [add_kept_half(a, r, core, "add_sibling") for a, r in zip(by_chip, got)]
        zones = [lax.dynamic_update_slice(lax.empty(h.shape, h.dtype), lax.dynamic_slice_in_dim(h, chip, 1, axis=0), (chip, 0, 0))
                 for h in chip_sum]
        handle = rs_start(chip_sum, zones, chip_sum[0], "rs_start_" + tag)
        pending.append((l, names, handle, tag))
        return handle[-1][0, 0]

    loss_tile, dx, dmods, grads, d_logits = local_step(
        x.reshape(T, D), c8, loss_target.reshape(T, D), get, hgrn_lb_logits, S, emit)
    loss = lax.psum(loss_tile[0, 0], ("x", "y", "c"))

    small_g = {"hgrn_lb_logits": d_logits,
               "hgrn_norm_g": jnp.stack([grads[l]["ng"][0] for l in range(DEPTH)]),
               "mla_q_norm_g": jnp.stack([grads[l]["gq"][0] for l in range(DEPTH)]),
               "mla_kv_norm_g": jnp.stack([grads[l]["gkv"][0] for l in range(DEPTH)]),
               "fox_b_f": jnp.stack([grads[l]["bcol"][0:HEADS, 0] for l in range(DEPTH)]),
               "gmlp_ln_g": jnp.stack([grads[l]["g_lg"][0] for l in range(DEPTH)]),
               "gmlp_ln_b": jnp.stack([grads[l]["g_lb"][0] for l in range(DEPTH)]),
               "gmlp_w_s": jnp.stack([grads[l]["ws"] for l in range(DEPTH)]),
               "gmlp_b_s": jnp.stack([grads[l]["bs"][:, :, 0] for l in range(DEPTH)])}
    small_g["ln_g"] = jnp.stack([grads[l]["ln_g"] for l in range(DEPTH)])
    small_g["ln_b"] = jnp.stack([grads[l]["ln_b"] for l in range(DEPTH)])
    pk_small = pack_small(small_g)
    n_small = pk_small.shape[0]
    extras = [dmods[l] for l in range(DEPTH)] + [c]
    n_extra = _round_up(-(-sum(int(np.prod(e.shape)) for e in extras) // D), 8)
    gathered = ag_all(jnp.concatenate([pk_small, _rows(extras, n_extra, F32)], axis=0))
    g_small = unpack_small(sum_slots(gathered[:, 0:n_small], 8, "sum_small"), small_g)
    ext = gathered[:, n_small:].reshape(8, -1)
    n_dmod = DEPTH * Bl * N_MOD * D
    dmod_all = ext[:, 0:n_dmod].reshape(8, DEPTH, Bl, N_MOD * D)
    c_all = ext[:, n_dmod:n_dmod + Bl * D].reshape(8 * Bl, D)
    g_ada_w, g_ada_b = [], []
    ncol = N_MOD * D // N_CHIPS
    for l in range(DEPTH):
        dm = dmod_all[:, l].reshape(8 * Bl, N_MOD * D)
        g_ada_w.append(ada_grad(c_all, lax.dynamic_slice_in_dim(dm, chip * ncol, ncol, axis=1)))
        g_ada_b.append(sum_slots(dm.reshape(8 * Bl, N_MOD, D), 8 * Bl, "sum_ada_b").reshape(N_MOD * D))
    g_ada_w, g_ada_b = jnp.stack(g_ada_w), jnp.stack(g_ada_b)

    red = {n: lax.empty(w[n].shape, F32) for n in REDUCED}

    def arrive(entry, after):
        l, names, handle, tag = entry
        for n, land in zip(names, rs_wait(handle, after, "rs_wait_" + tag)[1]):
            red[n] = sum_into(land, red[n], l, core, "sum_chips")

    for entry in pending[:-1]:
        arrive(entry, dx)
    late = pending[-1][1]
    early = [n for n in REDUCED if n not in late]
    grad = dict(zip(early, sibling_join([red[n] for n in early], "sibling_join_a")))
    grad.update(g_small)
    grad["ada_w"], grad["ada_b"] = g_ada_w, g_ada_b
    for n in ("ln_g", "ln_b"):
        grad[n] = lax.dynamic_slice_in_dim(g_small[n], chip * (D // N_CHIPS), D // N_CHIPS, axis=2)
    out = {"grad": grad, "delta": {}, "new_m": {}, "new_v": {}}

    def update(n):
        shp = w[n].shape
        two_d = (-1, shp[-1])
        res = adamw(w[n].reshape(two_d), grad[n].reshape(two_d), m[n].reshape(two_d), v[n].reshape(two_d), "adamw_" + n,
                    echo=n in REDUCED)
        grad[n] = (res[3] if n in REDUCED else grad[n]).reshape(shp)
        for key, r in zip(("delta", "new_m", "new_v"), res):
            out[key][n] = r.reshape(shp)

    for n in WEIGHTS:
        if n not in late:
            update(n)
    arrive(pending[-1], out["delta"]["ffn2_w_in"])
    grad.update(zip(late, sibling_join([red[n] for n in late], "sibling_join_b")))
    for n in late:
        update(n)
    outs = [loss, dx.reshape(Bl, S, D)]
    for key in ("grad", "delta", "new_m", "new_v"):
        outs += [out[key][n] for n in WEIGHTS]
    return tuple(outs)
```

```python
import functools

import jax
import jax.numpy as jnp
import numpy as np
from jax import lax
from jax.experimental import pallas as pl
from jax.experimental.pallas import tpu as pltpu

F32, BF16 = jnp.float32, jnp.bfloat16
MESH = pl.DeviceIdType.MESH

N_CHIPS = 4
D = 1024
DEPTH = 2
FF = 2816
N_MOD = 9
GW = 256
HEADS = 4
HD = 64
HP = 128
A_CHUNK = 16
LB_FLOOR = 1e-30
B_Q_LORA, B_KV_LORA, B_NOPE, B_ROPE = 256, 128, 64, 32
ROPE_THETA = 10000.0
D_CHUNK = 128
ALPHA = (2 * DEPTH) ** 0.25
LN_EPS = 1e-5
RMS_EPS = 1e-6
ADAM_LR, ADAM_B1, ADAM_B2, ADAM_EPS, ADAM_WD, ADAM_STEP = 0.001, 0.9, 0.999, 1e-08, 0.01, 10

NP = 3840
NP_TILE = 1920
C_A, C_B, C_CQ, C_CK, C_CV, C_D, C_CF = 0, 1024, 1536, 2048, 2560, 3072, 3584
NCAT = 1536
O_A, O_B, O_C, O_D = 0, 256, 768, 1280

VMEM_BIG = 48 << 20
VMEM_MOST = 58 << 20


def _cparams(vmem=None):
    return pltpu.CompilerParams(vmem_limit_bytes=vmem) if vmem else pltpu.CompilerParams()


def _sds(shape, dtype):
    return jax.ShapeDtypeStruct(tuple(shape), dtype)


@jax.custom_vjp
def _mm(a, w):
    return jnp.dot(a.astype(BF16), w.astype(BF16), preferred_element_type=F32)


def _mm_f(a, w):
    return _mm(a, w), (a, w)


def _mm_b(res, g):
    a, w = res
    gb = g.astype(BF16)
    da = lax.dot_general(gb, w.astype(BF16), (((1,), (1,)), ((), ())), preferred_element_type=F32)
    dw = lax.dot_general(a.astype(BF16), gb, (((0,), (0,)), ((), ())), preferred_element_type=F32)
    return da.astype(a.dtype), dw.astype(w.dtype)


_mm.defvjp(_mm_f, _mm_b)


@jax.custom_vjp
def _mm_nt(a, b):
    return lax.dot_general(a.astype(BF16), b.astype(BF16), (((1,), (1,)), ((), ())), preferred_element_type=F32)


def _mm_nt_f(a, b):
    return _mm_nt(a, b), (a, b)


def _mm_nt_b(res, g):
    a, b = res
    gb = g.astype(BF16)
    da = jnp.dot(gb, b.astype(BF16), preferred_element_type=F32)
    db = lax.dot_general(gb, a.astype(BF16), (((0,), (0,)), ((), ())), preferred_element_type=F32)
    return da.astype(a.dtype), db.astype(b.dtype)


_mm_nt.defvjp(_mm_nt_f, _mm_nt_b)


@jax.custom_vjp
def _mm_tn(a, b):
    return lax.dot_general(a.astype(BF16), b.astype(BF16), (((0,), (0,)), ((), ())), preferred_element_type=F32)


def _mm_tn_f(a, b):
    return _mm_tn(a, b), (a, b)


def _mm_tn_b(res, g):
    a, b = res
    gb = g.astype(BF16)
    da = lax.dot_general(b.astype(BF16), gb, (((1,), (1,)), ((), ())), preferred_element_type=F32)
    db = jnp.dot(a.astype(BF16), gb, preferred_element_type=F32)
    return da.astype(a.dtype), db.astype(b.dtype)


_mm_tn.defvjp(_mm_tn_f, _mm_tn_b)


def _split3(x):
    p1 = x.astype(BF16)
    r = x - p1.astype(F32)
    p2 = r.astype(BF16)
    return p1, p2, (r - p2.astype(F32)).astype(BF16)


@jax.custom_vjp
def _sel_r(x, sel):
    s = sel.astype(BF16)
    return sum(jnp.dot(p, s, preferred_element_type=F32) for p in _split3(x))


def _sel_r_f(x, sel):
    return _sel_r(x, sel), sel


def _sel_r_b(sel, g):
    s = sel.astype(BF16)
    dx = sum(lax.dot_general(p, s, (((1,), (1,)), ((), ())), preferred_element_type=F32) for p in _split3(g))
    return dx, jnp.zeros_like(sel)


_sel_r.defvjp(_sel_r_f, _sel_r_b)


@jax.custom_vjp
def _sel_l(sel, x):
    s = sel.astype(BF16)
    return sum(jnp.dot(s, p, preferred_element_type=F32) for p in _split3(x))


def _sel_l_f(sel, x):
    return _sel_l(sel, x), sel


def _sel_l_b(sel, g):
    s = sel.astype(BF16)
    dx = sum(lax.dot_general(s, p, (((0,), (0,)), ((), ())), preferred_element_type=F32) for p in _split3(g))
    return jnp.zeros_like(sel), dx


_sel_l.defvjp(_sel_l_f, _sel_l_b)


def _iota(shape, dim):
    return lax.broadcasted_iota(jnp.int32, shape, dim)


def _head_sum_mats():
    e = (_iota((GW, HP), 0) // HD == _iota((GW, HP), 1)).astype(F32)
    et = (_iota((HP, GW), 1) // HD == _iota((HP, GW), 0)).astype(F32)
    return e, et


def _modulate(x, mod_ref, sh, sc):
    return x * (1.0 + mod_ref[sc:sc + 1, :]) + mod_ref[sh:sh + 1, :]


def _ln_res(x, y, gate, lg, lb, gs):
    r = ALPHA * x + gs * (1.0 + gate) * y
    mu = jnp.mean(r, axis=-1, keepdims=True)
    var = jnp.mean(jnp.square(r - mu), axis=-1, keepdims=True)
    return (r - mu) * lax.rsqrt(var + LN_EPS) * lg + lb


def _rms(x, g):
    return x * lax.rsqrt(jnp.mean(x * x, axis=-1, keepdims=True) + RMS_EPS) * g


def _tile(n, pref):
    return pref if n % pref == 0 else n


def mod_fwd(c8, w, l, b):
    tn = w.shape[3]
    n = N_CHIPS * tn

    def body(c_ref, w_ref, b_ref, o_ref):
        h = jax.nn.silu(c_ref[...]).astype(BF16)
        o_ref[...] = jnp.dot(h, w_ref[...], preferred_element_type=F32) + b_ref[...]

    return pl.pallas_call(
        body, name="mod_fwd", grid=(N_CHIPS,),
        in_specs=[pl.BlockSpec((8, D), lambda j: (0, 0)), pl.BlockSpec((None, None, D, tn), lambda j: (l, j, 0, 0)),
                  pl.BlockSpec((1, tn), lambda j: (0, j))],
        out_specs=pl.BlockSpec((8, tn), lambda j: (0, j)), out_shape=_sds((8, n), F32),
        compiler_params=_cparams(VMEM_BIG))(c8, w, b)


def ffn_in_fwd(x, mod, w_in, l, sh, sc, S):
    T = x.shape[0]
    tm, tn = _tile(S, 1024), FF // 2
    tpb, nj = S // tm, 2

    def body(x_ref, mod_ref, wg_ref, wu_ref, zg_ref, zu_ref, act_ref, h_ref):
        @pl.when(pl.program_id(1) == 0)
        def _():
            h_ref[...] = _modulate(x_ref[...], mod_ref, sh, sc).astype(BF16)
        g = jnp.dot(h_ref[...], wg_ref[...], preferred_element_type=F32)
        u = jnp.dot(h_ref[...], wu_ref[...], preferred_element_type=F32)
        zg_ref[...] = g.astype(BF16)
        zu_ref[...] = u.astype(BF16)
        act_ref[...] = (jax.nn.silu(g) * u).astype(BF16)

    return pl.pallas_call(
        body, name="ffn_in_fwd", grid=(T // tm, nj),
        in_specs=[pl.BlockSpec((tm, D), lambda i, j: (i, 0)),
                  pl.BlockSpec((None, N_MOD, D), lambda i, j: (i // tpb, 0, 0)),
                  pl.BlockSpec((None, None, D, tn), lambda i, j: (l, j, 0, 0)),
                  pl.BlockSpec((None, None, D, tn), lambda i, j: (l, j + nj, 0, 0))],
        out_specs=[pl.BlockSpec((tm, tn), lambda i, j: (i, j))] * 3,
        out_shape=[_sds((T, FF), BF16)] * 3,
        scratch_shapes=[pltpu.VMEM((tm, D), BF16)],
        compiler_params=_cparams(VMEM_BIG))(x, mod, w_in, w_in)


def mix_in_fwd(x, mod, w, sh, sc, S):
    T = x.shape[0]
    n = w.shape[1]
    tm, tn = _tile(S, 1024), NP_TILE
    tpb = S // tm

    def body(x_ref, mod_ref, w_ref, o_ref, h_ref):
        @pl.when(pl.program_id(1) == 0)
        def _():
            h_ref[...] = _modulate(x_ref[...], mod_ref, sh, sc).astype(BF16)
        o_ref[...] = jnp.dot(h_ref[...], w_ref[...], preferred_element_type=F32)

    return pl.pallas_call(
        body, name="mix_in_fwd", grid=(T // tm, n // tn),
        in_specs=[pl.BlockSpec((tm, D), lambda i, j: (i, 0)),
                  pl.BlockSpec((None, N_MOD, D), lambda i, j: (i // tpb, 0, 0)),
                  pl.BlockSpec((D, tn), lambda i, j: (0, j))],
        out_specs=pl.BlockSpec((tm, tn), lambda i, j: (i, j)), out_shape=_sds((T, n), F32),
        scratch_shapes=[pltpu.VMEM((tm, D), BF16)],
        compiler_params=_cparams(VMEM_BIG))(x, mod, w)


def out_ln_fwd(act, w_out, x, mod, lg, lb, gate, gs, S, l=None):
    T, K = act.shape
    tm = _tile(S, 512)
    tpb = S // tm

    def body(a_ref, w_ref, x_ref, mod_ref, lg_ref, lb_ref, y_ref, xn_ref):
        y = jnp.dot(a_ref[...], w_ref[...].reshape(K, D), preferred_element_type=F32)
        y_ref[...] = y
        xn_ref[...] = _ln_res(x_ref[...], y, mod_ref[gate:gate + 1, :], lg_ref[...], lb_ref[...], gs)

    if l is None:
        w_spec = pl.BlockSpec((K, D), lambda i: (0, 0))
    else:
        w_spec = pl.BlockSpec((None, N_CHIPS, K // N_CHIPS, D), lambda i: (l, 0, 0, 0))
    return pl.pallas_call(
        body, name="out_ln_fwd", grid=(T // tm,),
        in_specs=[pl.BlockSpec((tm, K), lambda i: (i, 0)), w_spec,
                  pl.BlockSpec((tm, D), lambda i: (i, 0)),
                  pl.BlockSpec((None, N_MOD, D), lambda i: (i // tpb, 0, 0)),
                  pl.BlockSpec((1, D), lambda i: (0, 0)), pl.BlockSpec((1, D), lambda i: (0, 0))],
        out_specs=[pl.BlockSpec((tm, D), lambda i: (i, 0))] * 2,
        out_shape=[_sds((T, D), F32), _sds((T, D), F32)],
        compiler_params=_cparams(VMEM_BIG))(act, w_out, x, mod, lg, lb)


def ln_res_bwd(dxn, x, y, mod, lg, lb, gate, gs, S):
    T = x.shape[0]
    B = T // S
    tm = _tile(S, 512)
    tpb = S // tm

    def body(d_ref, x_ref, y_ref, mod_ref, lg_ref, lb_ref, dx_ref, dy_ref, dg_ref, dlg_ref, dlb_ref):
        i = pl.program_id(0)
        f = functools.partial(_ln_res, gs=gs)
        _, vjp = jax.vjp(f, x_ref[...], y_ref[...], mod_ref[gate:gate + 1, :], lg_ref[...], lb_ref[...])
        dx, dy, dg, dlg, dlb = vjp(d_ref[...])
        dx_ref[...] = dx
        dy_ref[...] = dy.astype(BF16)

        @pl.when(i % tpb == 0)
        def _():
            dg_ref[...] = jnp.zeros_like(dg_ref)

        @pl.when(i == 0)
        def _():
            dlg_ref[...] = jnp.zeros_like(dlg_ref)
            dlb_ref[...] = jnp.zeros_like(dlb_ref)

        dg_ref[...] += dg
        dlg_ref[...] += dlg
        dlb_ref[...] += dlb

    tok = pl.BlockSpec((tm, D), lambda i: (i, 0))
    vec = pl.BlockSpec((1, D), lambda i: (0, 0))
    return pl.pallas_call(
        body, name="ln_res_bwd", grid=(T // tm,),
        in_specs=[tok, tok, tok, pl.BlockSpec((None, N_MOD, D), lambda i: (i // tpb, 0, 0)), vec, vec],
        out_specs=[tok, tok, pl.BlockSpec((None, 1, D), lambda i: (i // tpb, 0, 0)), vec, vec],
        out_shape=[_sds((T, D), F32), _sds((T, D), BF16), _sds((B, 1, D), F32), _sds((1, D), F32), _sds((1, D), F32)],
        compiler_params=_cparams(VMEM_BIG))(dxn, x, y, mod, lg, lb)


def swiglu_bwd(dy, w_out, l, zg, zu, S):
    T = dy.shape[0]
    tm, tn = _tile(S, 1024), FF // 2

    def body(dy_ref, w_ref, zg_ref, zu_ref, dg_ref, du_ref):
        da = lax.dot_general(dy_ref[...], w_ref[...].reshape(tn, D), (((1,), (1,)), ((), ())), preferred_element_type=F32)
        g, u = zg_ref[...].astype(F32), zu_ref[...].astype(F32)
        sg = jax.nn.sigmoid(g)
        dg_ref[...] = (da * u * (sg * (1.0 + g * (1.0 - sg)))).astype(BF16)
        du_ref[...] = (da * (g * sg)).astype(BF16)

    zt = pl.BlockSpec((tm, tn), lambda i, j: (i, j))
    return pl.pallas_call(
        body, name="swiglu_bwd", grid=(T // tm, FF // tn),
        in_specs=[pl.BlockSpec((tm, D), lambda i, j: (i, 0)),
                  pl.BlockSpec((None, 2, FF // N_CHIPS, D), lambda i, j: (l, j, 0, 0)), zt, zt],
        out_specs=[zt, zt], out_shape=[_sds((T, FF), BF16), _sds((T, FF), BF16)],
        compiler_params=_cparams(VMEM_BIG))(dy, w_out, zg, zu)


def nt_plain(dy, w):
    T = dy.shape[0]
    K = w.shape[0]
    tm = _tile(T, 1024)

    def body(dy_ref, w_ref, o_ref):
        o_ref[...] = lax.dot_general(dy_ref[...], w_ref[...], (((1,), (1,)), ((), ())), preferred_element_type=F32)

    return pl.pallas_call(
        body, name="nt_plain", grid=(T // tm,),
        in_specs=[pl.BlockSpec((tm, D), lambda i: (i, 0)), pl.BlockSpec((K, D), lambda i: (0, 0))],
        out_specs=pl.BlockSpec((tm, K), lambda i: (i, 0)), out_shape=_sds((T, K), F32),
        compiler_params=_cparams(VMEM_BIG))(dy, w)


def _tn_step(acc, o_ref, lhs, rhs, t, nt):
    part = lax.dot_general(lhs, rhs, (((0,), (0,)), ((), ())), preferred_element_type=F32)
    if nt == 1:
        o_ref[...] = part.astype(o_ref.dtype)
        return

    @pl.when(t == 0)
    def _():
        acc[...] = part

    @pl.when((t > 0) & (t < nt - 1))
    def _():
        acc[...] += part

    @pl.when(t == nt - 1)
    def _():
        o_ref[...] = (acc[...] + part).astype(o_ref.dtype)


def tn_mm(a, b, tk):
    T, K = a.shape
    N = b.shape[1]
    tt = _tile(T, 2048)
    nt = T // tt

    def body(a_ref, b_ref, o_ref, acc):
        _tn_step(acc, o_ref, a_ref[...], b_ref[...], pl.program_id(1), nt)

    return pl.pallas_call(
        body, name="tn_mm", grid=(K // tk, nt),
        in_specs=[pl.BlockSpec((tt, tk), lambda k, t: (t, k)), pl.BlockSpec((tt, N), lambda k, t: (t, 0))],
        out_specs=pl.BlockSpec((tk, N), lambda k, t: (k, 0)), out_shape=_sds((K, N), BF16),
        scratch_shapes=[pltpu.VMEM((tk, N), F32)], compiler_params=_cparams(VMEM_BIG))(a, b)


def tn_mm_mod(x, mod, b, sh, sc, S, tn):
    T = x.shape[0]
    N = b.shape[1]
    tt = _tile(S, 1024)
    tpb = S // tt
    nt = T // tt

    def body(x_ref, mod_ref, b_ref, o_ref, acc):
        h = _modulate(x_ref[...], mod_ref, sh, sc).astype(BF16)
        _tn_step(acc, o_ref, h, b_ref[...], pl.program_id(1), nt)

    return pl.pallas_call(
        body, name="tn_mm_mod", grid=(N // tn, nt),
        in_specs=[pl.BlockSpec((tt, D), lambda j, t: (t, 0)),
                  pl.BlockSpec((None, N_MOD, D), lambda j, t: (t // tpb, 0, 0)),
                  pl.BlockSpec((tt, tn), lambda j, t: (t, j))],
        out_specs=pl.BlockSpec((D, tn), lambda j, t: (0, j)), out_shape=_sds((D, N), BF16),
        scratch_shapes=[pltpu.VMEM((D, tn), F32)], compiler_params=_cparams(VMEM_BIG))(x, mod, b)


def tn_mm_mod_shards(x, mod, bg, bu, sh, sc, S):
    T = x.shape[0]
    tn = FF // 2
    tt = _tile(S, 1024)
    tpb = S // tt
    nt = T // tt

    def body(x_ref, mod_ref, bg_ref, bu_ref, o_ref, acc):
        j, t = pl.program_id(0), pl.program_id(1)
        h = _modulate(x_ref[...], mod_ref, sh, sc).astype(BF16)

        @pl.when(j < 2)
        def _():
            _tn_step(acc, o_ref, h, bg_ref[...], t, nt)

        @pl.when(j >= 2)
        def _():
            _tn_step(acc, o_ref, h, bu_ref[...], t, nt)

    return pl.pallas_call(
        body, name="tn_mm_mod_shards", grid=(N_CHIPS, nt),
        in_specs=[pl.BlockSpec((tt, D), lambda j, t: (t, 0)),
                  pl.BlockSpec((None, N_MOD, D), lambda j, t: (t // tpb, 0, 0)),
                  pl.BlockSpec((tt, tn), lambda j, t: (jnp.where(j < 2, t, 0), jnp.minimum(j, 1))),
                  pl.BlockSpec((tt, tn), lambda j, t: (jnp.where(j < 2, 0, t), jnp.maximum(j - 2, 0)))],
        out_specs=pl.BlockSpec((None, D, tn), lambda j, t: (j, 0, 0)), out_shape=_sds((N_CHIPS, D, tn), BF16),
        scratch_shapes=[pltpu.VMEM((D, tn), F32)], compiler_params=_cparams(VMEM_BIG))(x, mod, bg, bu)


def nt_mod_bwd(ds, w, offs, x, mod, dres, sc, S, tk, l=None):
    T = x.shape[0]
    B = T // S
    tm = _tile(S, 1024)
    tpb = S // tm
    Kd = ds[0].shape[1]
    nk = Kd // tk
    n_in = len(ds)

    def body(*refs):
        d_refs, w_refs = refs[:n_in], refs[n_in:2 * n_in]
        x_ref, mod_ref, r_ref, dx_ref, dsh_ref, dsc_ref, acc = refs[2 * n_in:]
        i, k = pl.program_id(0), pl.program_id(1)

        part = sum(lax.dot_general(d_ref[...], w_ref[...], (((1,), (1,)), ((), ())), preferred_element_type=F32)
                   for d_ref, w_ref in zip(d_refs, w_refs))

        @pl.when(k == 0)
        def _():
            acc[...] = part

        @pl.when(k > 0)
        def _():
            acc[...] += part

        @pl.when(k == nk - 1)
        def _():
            dh = acc[...]
            dx_ref[...] = dh * (1.0 + mod_ref[sc:sc + 1, :]) + r_ref[...]

            @pl.when(i % tpb == 0)
            def _():
                dsh_ref[...] = jnp.zeros_like(dsh_ref)
                dsc_ref[...] = jnp.zeros_like(dsc_ref)

            dsh_ref[...] += jnp.sum(dh, axis=0, keepdims=True)
            dsc_ref[...] += jnp.sum(dh * x_ref[...], axis=0, keepdims=True)

    tok = pl.BlockSpec((tm, D), lambda i, k: (i, 0))
    vec = pl.BlockSpec((None, 1, D), lambda i, k: (i // tpb, 0, 0))
    in_specs = [pl.BlockSpec((tm, tk), lambda i, k: (i, k)) for _ in ds]
    if l is None:
        in_specs += [pl.BlockSpec((D, tk), functools.partial(lambda i, k, o: (0, k + o), o=off // tk)) for off in offs]
    else:
        in_specs += [pl.BlockSpec((None, None, D, tk), functools.partial(lambda i, k, o: (l, k + o, 0, 0), o=off)) for off in offs]
    in_specs += [tok, pl.BlockSpec((None, N_MOD, D), lambda i, k: (i // tpb, 0, 0)), tok]
    return pl.pallas_call(
        body, name="nt_mod_bwd", grid=(T // tm, nk), in_specs=in_specs,
        out_specs=[tok, vec, vec],
        out_shape=[_sds((T, D), F32), _sds((B, 1, D), F32), _sds((B, 1, D), F32)],
        scratch_shapes=[pltpu.VMEM((tm, D), F32)],
        compiler_params=_cparams(VMEM_MOST))(*ds, *([w] * n_in), x, mod, dres)


def loss_head(y, tgt):
    T = y.shape[0]
    tm = _tile(T, 512)

    def body(y_ref, t_ref, l_ref, d_ref):
        @pl.when(pl.program_id(0) == 0)
        def _():
            l_ref[...] = jnp.zeros_like(l_ref)
        e = y_ref[...] - t_ref[...]
        d_ref[...] = e * (1.0 / D)
        l_ref[...] += 0.5 * jnp.sum(jnp.sum(e * e, axis=1, keepdims=True) * (1.0 / D))

    tok = pl.BlockSpec((tm, D), lambda i: (i, 0))
    return pl.pallas_call(
        body, name="loss_head", grid=(T // tm,), in_specs=[tok, tok],
        out_specs=[pl.BlockSpec((8, 128), lambda i: (0, 0)), tok],
        out_shape=[_sds((8, 128), F32), _sds((T, D), F32)],
        compiler_params=_cparams(VMEM_BIG))(y, tgt)


def _hgrn_block(q, fz, inp, go, st, lb, ng, blk):
    nc = blk // A_CHUNK
    lb_eff = jnp.maximum(lb, LB_FLOOR)
    log_f = jnp.logaddexp(jnp.log(lb_eff), jnp.log1p(-lb) + jax.nn.log_sigmoid(fz))
    k = (1.0 - lb) * jax.nn.sigmoid(-fz) - (lb_eff - lb)
    qf = jax.nn.silu(q)
    same_chunk = _iota((blk, blk), 0) // A_CHUNK == _iota((blk, blk), 1) // A_CHUNK
    tril = (same_chunk & (_iota((blk, blk), 1) <= _iota((blk, blk), 0))).astype(F32)
    G = _sel_l(tril, log_f)
    e_mat, et_mat = _head_sum_mats()
    G4, q4, k4, v4 = (z.reshape(nc, A_CHUNK, GW) for z in (G, qf, k, inp))
    shp = (nc, A_CHUNK, A_CHUNK, GW)
    one = (1, A_CHUNK, A_CHUNK, GW)
    mask = jnp.where(_iota(one, 2) <= _iota(one, 1), 0.0, -jnp.inf)
    decay = jnp.exp((G4[:, :, None, :] - G4[:, None, :, :]) + mask)
    prod = q4[:, :, None, :] * k4[:, None, :, :] * decay
    scores = _mm(prod.reshape(nc * A_CHUNK * A_CHUNK, GW), e_mat.astype(BF16))
    spread = _mm(scores, et_mat.astype(BF16)).reshape(shp)
    o_intra = jnp.sum(spread * v4[:, None, :, :], axis=2).reshape(blk, GW)
    head_diag = (_iota((GW, GW), 0) // HD == _iota((GW, GW), 1) // HD).astype(F32)
    g_last = [jnp.sum(log_f[c * A_CHUNK:(c + 1) * A_CHUNK], axis=0, keepdims=True) for c in range(nc)]
    g_last_b = jnp.concatenate([jnp.broadcast_to(g, (A_CHUNK, GW)) for g in g_last], axis=0)
    q_dec = qf * jnp.exp(G)
    k_end = k * jnp.exp(g_last_b - G)
    outs = []
    for c in range(nc):
        rows = slice(c * A_CHUNK, (c + 1) * A_CHUNK)
        outs.append(_mm_nt(q_dec[rows], st))
        st = st * jnp.exp(g_last[c]) + _mm_tn(inp[rows], k_end[rows]) * head_diag
    o = o_intra + jnp.concatenate(outs, axis=0)
    ms = _sel_r(o * o, e_mat) * (1.0 / HD)
    o = o * _sel_r(lax.rsqrt(ms + RMS_EPS), et_mat) * ng
    return o * jax.nn.silu(go), st


HGRN_BLK = 128


def hgrn_fwd(proj, lb, ng, S):
    T = proj.shape[0]
    B = T // S
    blk = min(HGRN_BLK, S)
    nb = S // blk

    def body(p_ref, lb_ref, ng_ref, o_ref, st_out_ref, st_ref):
        @pl.when(pl.program_id(1) == 0)
        def _():
            st_ref[...] = jnp.zeros_like(st_ref)
        st_out_ref[...] = st_ref[...]
        p = p_ref[...]
        o, st = _hgrn_block(p[:, 0:GW], p[:, GW:2 * GW], p[:, 2 * GW:3 * GW], p[:, 3 * GW:4 * GW],
                            st_ref[...], lb_ref[...], ng_ref[...], blk)
        o_ref[...] = o.astype(BF16)
        st_ref[...] = st

    vec = pl.BlockSpec((1, GW), lambda b, j: (0, 0))
    return pl.pallas_call(
        body, name="hgrn_fwd", grid=(B, nb),
        in_specs=[pl.BlockSpec((blk, 4 * GW), lambda b, j: (b * nb + j, C_A // (4 * GW))), vec, vec],
        out_specs=[pl.BlockSpec((blk, GW), lambda b, j: (b * nb + j, 0)),
                   pl.BlockSpec((None, GW, GW), lambda b, j: (b * nb + j, 0, 0))],
        out_shape=[_sds((T, GW), BF16), _sds((B * nb, GW, GW), F32)],
        scratch_shapes=[pltpu.VMEM((GW, GW), F32)],
        compiler_params=_cparams(VMEM_BIG))(proj, lb, ng)


def hgrn_bwd(proj, states, dcat, lb, ng, S):
    T = proj.shape[0]
    B = T // S
    blk = min(HGRN_BLK, S)
    nb = S // blk

    def body(p_ref, st_in_ref, do_ref, lb_ref, ng_ref, dp_ref, dlb_ref, dng_ref, dst_ref):
        b, j = pl.program_id(0), pl.program_id(1)

        @pl.when(j == 0)
        def _():
            dst_ref[...] = jnp.zeros_like(dst_ref)

        @pl.when((b == 0) & (j == 0))
        def _():
            dlb_ref[...] = jnp.zeros_like(dlb_ref)
            dng_ref[...] = jnp.zeros_like(dng_ref)

        p = p_ref[...]
        f = functools.partial(_hgrn_block, blk=blk)
        _, vjp = jax.vjp(f, p[:, 0:GW], p[:, GW:2 * GW], p[:, 2 * GW:3 * GW], p[:, 3 * GW:4 * GW],
                         st_in_ref[...], lb_ref[...], ng_ref[...])
        dq, df, di, dg, dst, dlb, dng = vjp((do_ref[...], dst_ref[...]))
        dp_ref[...] = jnp.concatenate([dq, df, di, dg], axis=1).astype(BF16)
        dst_ref[...] = dst
        dlb_ref[...] += dlb
        dng_ref[...] += dng

    def rev(b, j):
        return b * nb + (nb - 1 - j)

    vec = pl.BlockSpec((1, GW), lambda b, j: (0, 0))
    return pl.pallas_call(
        body, name="hgrn_bwd", grid=(B, nb),
        in_specs=[pl.BlockSpec((blk, 4 * GW), lambda b, j: (rev(b, j), C_A // (4 * GW))),
                  pl.BlockSpec((None, GW, GW), lambda b, j: (rev(b, j), 0, 0)),
                  pl.BlockSpec((blk, GW), lambda b, j: (rev(b, j), O_A // GW)), vec, vec],
        out_specs=[pl.BlockSpec((blk, 4 * GW), lambda b, j: (rev(b, j), 0)), vec, vec],
        out_shape=[_sds((T, 4 * GW), BF16), _sds((1, GW), F32), _sds((1, GW), F32)],
        scratch_shapes=[pltpu.VMEM((GW, GW), F32)],
        compiler_params=_cparams(VMEM_BIG))(proj, states, dcat, lb, ng)


ATT_TQ = 256


ATT_BANDS = 8


def _attn_block(q, k, v, cum, qpos0, scale, use_cum, n_free):
    s = _mm_nt(q, k) * scale
    if use_cum:
        s = s - cum
    band = s[:, n_free:]
    visible = _iota(band.shape, 1) <= (qpos0 - n_free) + _iota(band.shape, 0)
    band = jnp.where(visible, band, -jnp.inf)
    m = jnp.max(band, axis=-1, keepdims=True)
    if n_free:
        free = s[:, :n_free]
        m = jnp.maximum(m, jnp.max(free, axis=-1, keepdims=True))
    e = jnp.exp(band - m)
    denom = jnp.sum(e, axis=-1, keepdims=True)
    o = _mm(e, v[n_free:])
    if n_free:
        e = jnp.exp(free - m)
        denom = denom + jnp.sum(e, axis=-1, keepdims=True)
        o = o + _mm(e, v[:n_free])
    return o * (1.0 / denom)


def _bands(S, tq):
    nq = S // tq
    nb = min(ATT_BANDS, nq)
    per = nq // nb
    return [(r * per, (r + 1) * per, (r + 1) * per * tq) for r in range(nb)]


def attn_fwd(qa, qo, ka, ko, va, vo, cum, scale, S):
    T = qa.shape[0]
    B = T // S
    tq = min(ATT_TQ, S)
    nq = S // tq
    use_cum = cum is not None

    def body(*refs):
        if use_cum:
            q_ref, k_ref, v_ref, c_ref, o_ref = refs
        else:
            (q_ref, k_ref, v_ref, o_ref), c_ref = refs, None
        h, i = pl.program_id(1), pl.program_id(2)
        for lo, hi, kw in _bands(S, tq):
            @pl.when((i >= lo) & (i < hi))
            def _():
                crow = c_ref[pl.ds(h, 1), 0:kw] if use_cum else None
                o = _attn_block(q_ref[...], k_ref[0:kw, :], v_ref[0:kw, :], crow, i * tq, scale, use_cum, lo * tq)
                o_ref[...] = o.astype(BF16)

    in_specs = [pl.BlockSpec((tq, HP), lambda b, h, i: (b * nq + i, qo + h)),
                pl.BlockSpec((S, HP), lambda b, h, i: (b, ko + h)),
                pl.BlockSpec((S, HP), lambda b, h, i: (b, vo + h))]
    args = [qa, ka, va]
    if use_cum:
        in_specs.append(pl.BlockSpec((None, 8, S), lambda b, h, i: (b, 0, 0)))
        args.append(cum)
    return pl.pallas_call(
        body, name="attn_fwd", grid=(B, HEADS, nq), in_specs=in_specs,
        out_specs=pl.BlockSpec((tq, HP), lambda b, h, i: (b * nq + i, h)),
        out_shape=_sds((T, HEADS * HP), BF16),
        compiler_params=_cparams(VMEM_BIG))(*args)


def _attn_block_bwd(q, k, v, cum, do, qpos0, scale, use_cum, n_free):
    tn = (((0,), (0,)), ((), ()))
    nt = (((1,), (1,)), ((), ()))
    qb, dob = q.astype(BF16), do.astype(BF16)
    kb, vb = k.astype(BF16), v.astype(BF16)
    s = lax.dot_general(qb, kb, nt, preferred_element_type=F32) * scale
    if use_cum:
        s = s - cum
    band = s[:, n_free:]
    visible = _iota(band.shape, 1) <= (qpos0 - n_free) + _iota(band.shape, 0)
    parts = [(jnp.where(visible, band, -jnp.inf), n_free, s.shape[1])]
    if n_free:
        parts.append((s[:, :n_free], 0, n_free))
    m = functools.reduce(jnp.maximum, [jnp.max(sp, axis=-1, keepdims=True) for sp, _, _ in parts])
    es = [jnp.exp(sp - m) for sp, _, _ in parts]
    rinv = 1.0 / sum(jnp.sum(e, axis=-1, keepdims=True) for e in es)
    ps = [e * rinv for e in es]
    dps = [lax.dot_general(dob, vb[a:b], nt, preferred_element_type=F32) for _, a, b in parts]
    delta = sum(jnp.sum(p * dp, axis=-1, keepdims=True) for p, dp in zip(ps, dps))
    dq = jnp.zeros(q.shape, F32)
    out = []
    for p, dp, (_, a, b) in zip(ps, dps, parts):
        ds = p * (dp - delta)
        dsb = ds.astype(BF16)
        dq = dq + jnp.dot(dsb, kb[a:b], preferred_element_type=F32)
        out.append((a, b, lax.dot_general(dsb, qb, tn, preferred_element_type=F32) * scale,
                    lax.dot_general(p.astype(BF16), dob, tn, preferred_element_type=F32),
                    -jnp.sum(ds, axis=0, keepdims=True) if use_cum else None))
    return dq * scale, out


def attn_bwd(qa, qo, ka, ko, va, vo, cum, dcat, do_off, scale, S, out_dtype):
    T = qa.shape[0]
    B = T // S
    tq = min(ATT_TQ, S)
    nq = S // tq
    use_cum = cum is not None

    def body(*refs):
        if use_cum:
            q_ref, k_ref, v_ref, do_ref, c_ref, dq_ref, dk_ref, dv_ref, dc_ref, dk_acc, dv_acc = refs
        else:
            q_ref, k_ref, v_ref, do_ref, dq_ref, dk_ref, dv_ref, dk_acc, dv_acc = refs
        h, i = pl.program_id(1), pl.program_id(2)

        @pl.when(i == 0)
        def _():
            dk_acc[...] = jnp.zeros_like(dk_acc)
            dv_acc[...] = jnp.zeros_like(dv_acc)
            if use_cum:
                dc_ref[...] = jnp.zeros_like(dc_ref)

        for lo, hi, kw in _bands(S, tq):
            @pl.when((i >= lo) & (i < hi))
            def _():
                crow = c_ref[pl.ds(h, 1), 0:kw] if use_cum else None
                dq, pieces = _attn_block_bwd(q_ref[...], k_ref[0:kw, :], v_ref[0:kw, :], crow, do_ref[...], i * tq,
                                             scale, use_cum, lo * tq)
                dq_ref[...] = dq.astype(out_dtype)
                for a, b, dk, dv, dc in pieces:
                    dk_acc[a:b, :] += dk
                    dv_acc[a:b, :] += dv
                    if use_cum:
                        dc_ref[:, a:b] += dc

        @pl.when(i == nq - 1)
        def _():
            dk_ref[...] = dk_acc[...].astype(out_dtype)
            dv_ref[...] = dv_acc[...].astype(out_dtype)

    qspec = pl.BlockSpec((tq, HP), lambda b, h, i: (b * nq + i, qo + h))
    in_specs = [qspec, pl.BlockSpec((S, HP), lambda b, h, i: (b, ko + h)),
                pl.BlockSpec((S, HP), lambda b, h, i: (b, vo + h)),
                pl.BlockSpec((tq, HP), lambda b, h, i: (b * nq + i, do_off + h))]
    args = [qa, ka, va, dcat]
    kv_out = pl.BlockSpec((S, HP), lambda b, h, i: (b, h))
    out_specs = [pl.BlockSpec((tq, HP), lambda b, h, i: (b * nq + i, h)), kv_out, kv_out]
    out_shape = [_sds((T, HEADS * HP), out_dtype)] * 3
    if use_cum:
        in_specs.append(pl.BlockSpec((None, 8, S), lambda b, h, i: (b, 0, 0)))
        args.append(cum)
        out_specs.append(pl.BlockSpec((None, 1, S), lambda b, h, i: (b * HEADS + h, 0, 0)))
        out_shape.append(_sds((B * HEADS, 1, S), F32))
    return pl.pallas_call(
        body, name="attn_bwd", grid=(B, HEADS, nq), in_specs=in_specs, out_specs=out_specs, out_shape=out_shape,
        scratch_shapes=[pltpu.VMEM((S, HP), F32), pltpu.VMEM((S, HP), F32)],
        compiler_params=_cparams(VMEM_BIG))(*args)


def _tri(n, upper):
    r, c = _iota((n, n), 0), _iota((n, n), 1)
    return ((r <= c) if upper else (r >= c)).astype(F32)


def fox_gate_fwd(proj, bcol, S):
    T = proj.shape[0]
    B = T // S
    ts = _tile(S, 512)
    nt = S // ts

    def body(p_ref, b_ref, o_ref, carry):
        @pl.when(pl.program_id(1) == 0)
        def _():
            carry[...] = jnp.zeros_like(carry)
        cf = jnp.transpose(p_ref[...])[0:8, :]
        lf = jax.nn.log_sigmoid(cf + b_ref[...])
        cum = _sel_r(lf, _tri(ts, True)) + carry[...]
        o_ref[...] = cum
        carry[...] += jnp.sum(lf, axis=1, keepdims=True)

    return pl.pallas_call(
        body, name="fox_gate_fwd", grid=(B, nt),
        in_specs=[pl.BlockSpec((ts, HP), lambda b, j: (b * nt + j, C_CF // HP)), pl.BlockSpec((8, 1), lambda b, j: (0, 0))],
        out_specs=pl.BlockSpec((None, 8, ts), lambda b, j: (b, 0, j)), out_shape=_sds((B, 8, S), F32),
        scratch_shapes=[pltpu.VMEM((8, 1), F32)],
        compiler_params=_cparams(VMEM_BIG))(proj, bcol)


def fox_gate_bwd(proj, bcol, dcum, S):
    T = proj.shape[0]
    B = T // S
    ts = _tile(S, 512)
    nt = S // ts

    def body(p_ref, b_ref, dc_ref, dp_ref, db_ref, carry):
        b, j = pl.program_id(0), pl.program_id(1)

        @pl.when(j == 0)
        def _():
            carry[...] = jnp.zeros_like(carry)

        @pl.when((b == 0) & (j == 0))
        def _():
            db_ref[...] = jnp.zeros_like(db_ref)

        cf = jnp.transpose(p_ref[...])[0:8, :]
        dc = dc_ref[...]
        dlf = _sel_r(dc, _tri(ts, False)) + carry[...]
        carry[...] += jnp.sum(dc, axis=1, keepdims=True)
        dcf = dlf * jax.nn.sigmoid(-(cf + b_ref[...]))
        db_ref[...] += jnp.sum(dcf, axis=1, keepdims=True)
        full = jnp.concatenate([dcf, jnp.zeros((HP - 8, ts), F32)], axis=0)
        dp_ref[...] = jnp.transpose(full).astype(BF16)

    def rev(b, j):
        return nt - 1 - j

    return pl.pallas_call(
        body, name="fox_gate_bwd", grid=(B, nt),
        in_specs=[pl.BlockSpec((ts, HP), lambda b, j: (b * nt + rev(b, j), C_CF // HP)),
                  pl.BlockSpec((8, 1), lambda b, j: (0, 0)),
                  pl.BlockSpec((None, 8, ts), lambda b, j: (b, 0, rev(b, j)))],
        out_specs=[pl.BlockSpec((ts, HP), lambda b, j: (b * nt + rev(b, j), 0)), pl.BlockSpec((8, 1), lambda b, j: (0, 0))],
        out_shape=[_sds((T, HP), BF16), _sds((8, 1), F32)],
        scratch_shapes=[pltpu.VMEM((8, 1), F32)],
        compiler_params=_cparams(VMEM_BIG))(proj, bcol, dcum)


def _mla_pre(blk, gq, gkv, wq, wkv, place, cos_q, sin_q, cs_k):
    nq = _rms(blk[:, 0:B_Q_LORA], gq)
    nkv = _rms(blk[:, B_Q_LORA:B_Q_LORA + B_KV_LORA], gkv)
    qq = _mm(nq, wq)
    q = qq[:, 0:HEADS * HP] * cos_q + qq[:, HEADS * HP:] * sin_q
    kv = _mm(nkv, wkv)
    k = kv[:, 0:HEADS * HP] + _mm(blk[:, B_Q_LORA + B_KV_LORA:] * cs_k, place)
    return q, k, kv[:, HEADS * HP:]


def mla_pre_fwd(proj, gq, gkv, wq, wkv, place, cos_q, sin_q, cs_k, S):
    T = proj.shape[0]
    tm = _tile(S, 512)
    tpb = S // tm
    W = HEADS * HP

    def body(p_ref, gq_ref, gkv_ref, wq_ref, wkv_ref, pl_ref, cq_ref, sq_ref, ck_ref, q_ref, k_ref, v_ref):
        q, k, v = _mla_pre(p_ref[...], gq_ref[...], gkv_ref[...], wq_ref[...], wkv_ref[...], pl_ref[...],
                           cq_ref[...], sq_ref[...], ck_ref[...])
        q_ref[...] = q.astype(BF16)
        k_ref[...] = k.astype(BF16)
        v_ref[...] = v.astype(BF16)

    def full(a):
        return pl.BlockSpec(a.shape, lambda i: (0,) * a.ndim)

    tok = pl.BlockSpec((tm, W), lambda i: (i, 0))
    return pl.pallas_call(
        body, name="mla_pre_fwd", grid=(T // tm,),
        in_specs=[pl.BlockSpec((tm, W), lambda i: (i, C_B // W)), full(gq), full(gkv), full(wq), full(wkv), full(place),
                  pl.BlockSpec((tm, W), lambda i: (i % tpb, 0)), pl.BlockSpec((tm, W), lambda i: (i % tpb, 0)),
                  pl.BlockSpec((tm, HP), lambda i: (i % tpb, 0))],
        out_specs=[tok] * 3, out_shape=[_sds((T, W), BF16)] * 3,
        compiler_params=_cparams(VMEM_BIG))(proj, gq, gkv, wq, wkv, place, cos_q, sin_q, cs_k)


def mla_pre_bwd(proj, gq, gkv, wq, wkv, place, cos_q, sin_q, cs_k, dq, dk, dv, S):
    T = proj.shape[0]
    tm = _tile(S, 512)
    tpb = S // tm
    W = HEADS * HP

    def body(p_ref, gq_ref, gkv_ref, wq_ref, wkv_ref, pl_ref, cq_ref, sq_ref, ck_ref, dq_ref, dk_ref, dv_ref,
             dp_ref, dgq_ref, dgkv_ref, dwq_ref, dwkv_ref):
        @pl.when(pl.program_id(0) == 0)
        def _():
            for r in (dgq_ref, dgkv_ref, dwq_ref, dwkv_ref):
                r[...] = jnp.zeros_like(r)

        f = functools.partial(_mla_pre, place=pl_ref[...], cos_q=cq_ref[...], sin_q=sq_ref[...], cs_k=ck_ref[...])
        _, vjp = jax.vjp(f, p_ref[...], gq_ref[...], gkv_ref[...], wq_ref[...], wkv_ref[...])
        dp, dgq, dgkv, dwq, dwkv = vjp((dq_ref[...], dk_ref[...], dv_ref[...]))
        dp_ref[...] = dp.astype(BF16)
        dgq_ref[...] += dgq
        dgkv_ref[...] += dgkv
        dwq_ref[...] += dwq
        dwkv_ref[...] += dwkv

    def full(a):
        return pl.BlockSpec(a.shape, lambda i: (0,) * a.ndim)

    tok = pl.BlockSpec((tm, W), lambda i: (i, 0))
    return pl.pallas_call(
        body, name="mla_pre_bwd", grid=(T // tm,),
        in_specs=[pl.BlockSpec((tm, W), lambda i: (i, C_B // W)), full(gq), full(gkv), full(wq), full(wkv), full(place),
                  pl.BlockSpec((tm, W), lambda i: (i % tpb, 0)), pl.BlockSpec((tm, W), lambda i: (i % tpb, 0)),
                  pl.BlockSpec((tm, HP), lambda i: (i % tpb, 0)), tok, tok, tok],
        out_specs=[tok, full(gq), full(gkv), full(wq), full(wkv)],
        out_shape=[_sds((T, W), BF16), _sds(gq.shape, F32), _sds(gkv.shape, F32), _sds(wq.shape, F32), _sds(wkv.shape, F32)],
        compiler_params=_cparams(VMEM_BIG))(proj, gq, gkv, wq, wkv, place, cos_q, sin_q, cs_k, dq, dk, dv)


GMLP_CHUNKS = 4


def _gmlp_block(blk, lg, lb, ws, bs):
    u = jax.nn.gelu(blk[:, 0:GW])
    v = jax.nn.gelu(blk[:, GW:2 * GW])
    mu = jnp.mean(v, axis=-1, keepdims=True)
    var = jnp.mean(jnp.square(v - mu), axis=-1, keepdims=True)
    vn = (v - mu) * lax.rsqrt(var + LN_EPS) * lg + lb
    causal = _iota((D_CHUNK, D_CHUNK), 1) <= _iota((D_CHUNK, D_CHUNK), 0)
    group = _iota((1, GW), 1) // HD
    w = [jnp.where(causal, ws[g], 0.0) for g in range(HEADS)]
    chunks = []
    for c in range(blk.shape[0] // D_CHUNK):
        vc = vn[c * D_CHUNK:(c + 1) * D_CHUNK]
        mixed = jnp.zeros((D_CHUNK, GW), F32)
        for g in range(HEADS):
            mixed = mixed + jnp.where(group == g, _mm(w[g], vc) + bs[g], 0.0)
        chunks.append(mixed)
    return u * jnp.concatenate(chunks, axis=0)


def _gmlp_tile(T):
    return _tile(T, GMLP_CHUNKS * D_CHUNK) if T % (GMLP_CHUNKS * D_CHUNK) == 0 else D_CHUNK


def gmlp_fwd(proj, lg, lb, ws, bs):
    T = proj.shape[0]
    tm = _gmlp_tile(T)

    def body(p_ref, lg_ref, lb_ref, ws_ref, bs_ref, o_ref):
        o_ref[...] = _gmlp_block(p_ref[...], lg_ref[...], lb_ref[...], ws_ref[...], bs_ref[...]).astype(BF16)

    def full(a):
        return pl.BlockSpec(a.shape, lambda i: (0,) * a.ndim)

    return pl.pallas_call(
        body, name="gmlp_fwd", grid=(T // tm,),
        in_specs=[pl.BlockSpec((tm, 2 * GW), lambda i: (i, C_D // (2 * GW))), full(lg), full(lb), full(ws), full(bs)],
        out_specs=pl.BlockSpec((tm, GW), lambda i: (i, 0)), out_shape=_sds((T, GW), BF16),
        compiler_params=_cparams(VMEM_BIG))(proj, lg, lb, ws, bs)


def gmlp_bwd(proj, lg, lb, ws, bs, dcat):
    T = proj.shape[0]
    tm = _gmlp_tile(T)

    def body(p_ref, lg_ref, lb_ref, ws_ref, bs_ref, do_ref, dp_ref, dlg_ref, dlb_ref, dws_ref, dbs_ref):
        @pl.when(pl.program_id(0) == 0)
        def _():
            for r in (dlg_ref, dlb_ref, dws_ref, dbs_ref):
                r[...] = jnp.zeros_like(r)

        _, vjp = jax.vjp(_gmlp_block, p_ref[...], lg_ref[...], lb_ref[...], ws_ref[...], bs_ref[...])
        dp, dlg, dlb, dws, dbs = vjp(do_ref[...])
        dp_ref[...] = dp.astype(BF16)
        dlg_ref[...] += dlg
        dlb_ref[...] += dlb
        dws_ref[...] += dws
        dbs_ref[...] += dbs

    def full(a):
        return pl.BlockSpec(a.shape, lambda i: (0,) * a.ndim)

    return pl.pallas_call(
        body, name="gmlp_bwd", grid=(T // tm,),
        in_specs=[pl.BlockSpec((tm, 2 * GW), lambda i: (i, C_D // (2 * GW))), full(lg), full(lb), full(ws), full(bs),
                  pl.BlockSpec((tm, GW), lambda i: (i, O_D // GW))],
        out_specs=[pl.BlockSpec((tm, 2 * GW), lambda i: (i, 0)), full(lg), full(lb), full(ws), full(bs)],
        out_shape=[_sds((T, 2 * GW), BF16), _sds(lg.shape, F32), _sds(lb.shape, F32), _sds(ws.shape, F32), _sds(bs.shape, F32)],
        compiler_params=_cparams(VMEM_BIG))(proj, lg, lb, ws, bs, dcat)


def _lb_all(logits):
    m = jnp.max(logits, axis=0, keepdims=True)
    e = jnp.exp(logits - m)
    sm = e / jnp.sum(e, axis=0, keepdims=True)
    return jnp.concatenate([sm[0:1] - sm[0:1], (sm[0:1] + sm[1:2]) - sm[0:1]], axis=0)


def lb_fwd(logits):
    def body(l_ref, o_ref):
        o_ref[...] = _lb_all(l_ref[...])

    return pl.pallas_call(body, name="lb_fwd", out_shape=_sds(logits.shape, F32))(logits)


def lb_bwd(logits, dlb):
    def body(l_ref, d_ref, o_ref):
        _, vjp = jax.vjp(_lb_all, l_ref[...])
        o_ref[...] = vjp(d_ref[...])[0]

    return pl.pallas_call(body, name="lb_bwd", out_shape=_sds(logits.shape, F32))(logits, dlb)


def ada_grad(c_all, dmod_cols):
    N = dmod_cols.shape[1]
    tn = _tile(N, 1152)

    def body(c_ref, d_ref, o_ref):
        h = jax.nn.silu(c_ref[...]).astype(BF16)
        o_ref[...] = lax.dot_general(h, d_ref[...].astype(BF16), (((0,), (0,)), ((), ())), preferred_element_type=F32)

    nb = c_all.shape[0]
    return pl.pallas_call(
        body, name="ada_grad", grid=(N // tn,),
        in_specs=[pl.BlockSpec((nb, D), lambda j: (0, 0)), pl.BlockSpec((nb, tn), lambda j: (0, j))],
        out_specs=pl.BlockSpec((D, tn), lambda j: (0, j)), out_shape=_sds((D, N), F32),
        compiler_params=_cparams(VMEM_BIG))(c_all, dmod_cols)


def sum_slots(a, n, name):
    _, R, C = a.shape
    tr = _row_tile(R, C, n)

    def body(a_ref, o_ref):
        acc = a_ref[0]
        for k in range(1, n):
            acc = acc + a_ref[k]
        o_ref[...] = acc

    return pl.pallas_call(
        body, name=name, grid=(R // tr,),
        in_specs=[pl.BlockSpec((n, tr, C), lambda i: (0, i, 0))],
        out_specs=pl.BlockSpec((tr, C), lambda i: (i, 0)), out_shape=_sds((R, C), F32),
        compiler_params=_cparams(VMEM_BIG))(a)


def _row_tile(R, C=D, n=1, mult=8, elems=1 << 18):
    limit = max(mult, elems // (C * n))
    for t in range(limit - limit % mult, mult - 1, -mult):
        if R % t == 0:
            return t
    return R


def adamw(w, g, m, v, name, echo=False):
    R, C = w.shape
    tr = _row_tile(R, C, elems=1 << 19)
    c1 = 1.0 - ADAM_B1 ** ADAM_STEP
    c2 = 1.0 - ADAM_B2 ** ADAM_STEP
    n_out = 4 if echo else 3

    def body(w_ref, g_ref, m_ref, v_ref, d_ref, nm_ref, nv_ref, *g_out):
        g_ = g_ref[...]
        nm = ADAM_B1 * m_ref[...] + (1.0 - ADAM_B1) * g_
        nv = ADAM_B2 * v_ref[...] + (1.0 - ADAM_B2) * jnp.square(g_)
        d_ref[...] = -ADAM_LR * ((nm / c1) / (jnp.sqrt(nv / c2) + ADAM_EPS) + ADAM_WD * w_ref[...])
        nm_ref[...] = nm
        nv_ref[...] = nv
        if echo:
            g_out[0][...] = g_

    spec = pl.BlockSpec((tr, C), lambda i: (i, 0))
    return pl.pallas_call(body, name=name, grid=(R // tr,), in_specs=[spec] * 4, out_specs=[spec] * n_out,
                          out_shape=[_sds((R, C), F32)] * n_out, compiler_params=_cparams(VMEM_BIG))(w, g, m, v)


def _rot_cols(w):
    return jnp.concatenate([-w[:, 16:32], w[:, 0:16]], axis=1)


def _fold_rot(d):
    return jnp.concatenate([d[:, 16:32], -d[:, 0:16]], axis=1)


def _pad_heads(w, off, axis):
    parts = []
    for h in range(HEADS):
        piece = lax.slice_in_dim(w, off + HD * h, off + HD * (h + 1), axis=axis)
        parts += [piece, jnp.zeros_like(piece)]
    return parts


def _unpad_heads(d, off, axis):
    return [lax.slice_in_dim(d, off + HP * h, off + HP * h + HD, axis=axis) for h in range(HEADS)]


def mix_in_ext(w):
    z = lambda n: jnp.zeros((w.shape[0], n), w.dtype)
    kr = w[:, 1408:1440]
    cols = [w[:, 0:1408], kr, _rot_cols(kr), z(64)]
    cols += _pad_heads(w, 1440, 1) + _pad_heads(w, 1696, 1) + _pad_heads(w, 1952, 1)
    cols += [w[:, 2212:2724], w[:, 2208:2212], z(NP - C_CF - HEADS)]
    return jnp.concatenate(cols, axis=1)


def mix_in_unext(d):
    kr = d[:, 1408:1440] + _fold_rot(d[:, 1440:1472])
    cols = [d[:, 0:1408], kr] + _unpad_heads(d, C_CQ, 1) + _unpad_heads(d, C_CK, 1) + _unpad_heads(d, C_CV, 1)
    cols += [d[:, C_CF:C_CF + HEADS], d[:, C_D:C_D + 2 * GW]]
    return jnp.concatenate(cols, axis=1)


def mix_out_ext(w):
    return jnp.concatenate([w[0:GW]] + _pad_heads(w, GW, 0) + _pad_heads(w, 2 * GW, 0) + [w[3 * GW:4 * GW]], axis=0)


def mix_out_unext(d):
    return jnp.concatenate([d[0:GW]] + _unpad_heads(d, O_B, 0) + _unpad_heads(d, O_C, 0) + [d[O_D:O_D + GW]], axis=0)


def uq_ext(w):
    z = lambda n: jnp.zeros((w.shape[0], n), w.dtype)
    a, b = [], []
    for h in range(HEADS):
        o = (B_NOPE + B_ROPE) * h
        a += [w[:, o:o + B_NOPE + B_ROPE], z(32)]
        b += [z(B_NOPE), _rot_cols(w[:, o + B_NOPE:o + B_NOPE + B_ROPE]), z(32)]
    return jnp.concatenate(a + b, axis=1)


def uq_unext(d):
    cols = []
    for h in range(HEADS):
        o = HP * h
        cols += [d[:, o:o + B_NOPE], d[:, o + B_NOPE:o + B_NOPE + B_ROPE]
                 + _fold_rot(d[:, HEADS * HP + o + B_NOPE:HEADS * HP + o + B_NOPE + B_ROPE])]
    return jnp.concatenate(cols, axis=1)


def ukv_ext(w):
    z = jnp.zeros((w.shape[0], HD), w.dtype)
    k, v = [], []
    for h in range(HEADS):
        k += [w[:, 2 * HD * h:2 * HD * h + HD], z]
        v += [w[:, 2 * HD * h + HD:2 * HD * (h + 1)], z]
    return jnp.concatenate(k + v, axis=1)


def ukv_unext(d):
    cols = []
    for h in range(HEADS):
        cols += [d[:, HP * h:HP * h + HD], d[:, HEADS * HP + HP * h:HEADS * HP + HP * h + HD]]
    return jnp.concatenate(cols, axis=1)


def rope_tables(S):
    half = B_ROPE // 2
    inv_freq = ROPE_THETA ** (-jnp.arange(half, dtype=F32) / half)
    ang = jnp.arange(S).astype(F32)[:, None] * inv_freq[None, :]
    cos = jnp.tile(jnp.cos(ang), (1, 2))
    sin = jnp.tile(jnp.sin(ang), (1, 2))
    one, zero = jnp.ones((S, B_NOPE), F32), jnp.zeros((S, B_NOPE), F32)
    z32 = jnp.zeros((S, 32), F32)
    cos_q = jnp.tile(jnp.concatenate([one, cos, z32], axis=1), (1, HEADS))
    sin_q = jnp.tile(jnp.concatenate([zero, sin, z32], axis=1), (1, HEADS))
    cs_k = jnp.concatenate([cos, sin, zero], axis=1)
    place = np.zeros((HP, HEADS * HP), np.float32)
    for h in range(HEADS):
        for j in range(B_ROPE):
            place[j, h * HP + B_NOPE + j] = 1.0
            place[B_ROPE + j, h * HP + B_NOPE + j] = 1.0
    return cos_q, sin_q, cs_k, jnp.asarray(place, BF16)


def layer_fwd(x, mod, get, tabs, S):
    cos_q, sin_q, cs_k, place = tabs
    p = dict(get("ffn1", x))
    l = p["wl"]
    zg1, zu1, act1 = ffn_in_fwd(x, mod, p["ffn1_in"], l, 0, 1, S)
    y1, x1 = out_ln_fwd(act1, p["ffn1_out"], x, mod, p["ln_g"][0:1], p["ln_b"][0:1], 2, 0.5, S, l)
    p.update(get("mix", x1))
    proj = mix_in_fwd(x1, mod, p["mix_in"], 3, 4, S)
    o_a, states = hgrn_fwd(proj, p["lb"], p["ng"], S)
    q_b, k_b, v_b = mla_pre_fwd(proj, p["gq"], p["gkv"], p["wq"], p["wkv"], place, cos_q, sin_q, cs_k, S)
    o_b = attn_fwd(q_b, 0, k_b, 0, v_b, 0, None, (B_NOPE + B_ROPE) ** -0.5, S)
    cum = fox_gate_fwd(proj, p["bcol"], S)
    o_c = attn_fwd(proj, C_CQ // HP, proj, C_CK // HP, proj, C_CV // HP, cum, HD ** -0.5, S)
    o_d = gmlp_fwd(proj, p["g_lg"], p["g_lb"], p["ws"], p["bs"])
    cat = jnp.concatenate([o_a, o_b, o_c, o_d], axis=1)
    y2, x2 = out_ln_fwd(cat, p["mix_out"], x1, mod, p["ln_g"][1:2], p["ln_b"][1:2], 5, 1.0, S)
    p.update(get("ffn2", x2))
    zg3, zu3, act3 = ffn_in_fwd(x2, mod, p["ffn2_in"], l, 6, 7, S)
    y3, x3 = out_ln_fwd(act3, p["ffn2_out"], x2, mod, p["ln_g"][2:3], p["ln_b"][2:3], 8, 0.5, S, l)
    saved = dict(x=x, zg1=zg1, zu1=zu1, act1=act1, y1=y1, x1=x1, proj=proj, states=states, q_b=q_b, k_b=k_b, v_b=v_b,
                 cum=cum, cat=cat, y2=y2, x2=x2, zg3=zg3, zu3=zu3, act3=act3, y3=y3, p=p)
    return x3, saved


def _ffn_bwd(dxn, x_in, y, zg, zu, act, mod, w_in, w_out, l, lg, lb, idx, S, emit):
    sh, sc, gate = idx
    dres, dy, dgate, dlg, dlb = ln_res_bwd(dxn, x_in, y, mod, lg, lb, gate, 0.5, S)
    dzg, dzu = swiglu_bwd(dy, w_out, l, zg, zu, S)
    dw_out = tn_mm(act, dy, FF // 2).reshape(N_CHIPS, FF // N_CHIPS, D)
    dw_in = tn_mm_mod_shards(x_in, mod, dzg, dzu, sh, sc, S)
    mod = mod + emit(dw_in, dw_out)
    dx, dsh, dsc = nt_mod_bwd([dzg, dzu], w_in, [0, 2], x_in, mod, dres, sc, S, FF // 2, l)
    return dx, dw_in, dw_out, dlg, dlb, {sh: dsh, sc: dsc, gate: dgate}, mod


def layer_bwd(dx3, mod, sv, tabs, S, emit):
    cos_q, sin_q, cs_k, place = tabs
    p = sv["p"]
    l = p["wl"]
    g = {}
    dm = {}

    def emit_ffn(part):
        def f(dw_in, dw_out):
            g[part + "_in"], g[part + "_out"] = dw_in, dw_out
            return emit(part, g)
        return f

    dx2, _, _, dlg2, dlb2, d, mod = _ffn_bwd(
        dx3, sv["x2"], sv["y3"], sv["zg3"], sv["zu3"], sv["act3"], mod, p["ffn2_in"], p["ffn2_out"], l,
        p["ln_g"][2:3], p["ln_b"][2:3], (6, 7, 8), S, emit_ffn("ffn2"))
    dm.update(d)
    dres, dy2, dm[5], dlg1, dlb1 = ln_res_bwd(dx2, sv["x1"], sv["y2"], mod, p["ln_g"][1:2], p["ln_b"][1:2], 5, 1.0, S)
    dcat = nt_plain(dy2, p["mix_out"])
    g["mix_out"] = tn_mm(sv["cat"], dy2, NCAT // 2)
    proj = sv["proj"]
    d_a, g["lb"], g["ng"] = hgrn_bwd(proj, sv["states"], dcat, p["lb"], p["ng"], S)
    dq_c, dk_c, dv_c, dcum = attn_bwd(proj, C_CQ // HP, proj, C_CK // HP, proj, C_CV // HP, sv["cum"], dcat,
                                      O_C // HP, HD ** -0.5, S, BF16)
    B = proj.shape[0] // S
    dcum = jnp.concatenate([dcum.reshape(B, HEADS, S), jnp.zeros((B, 8 - HEADS, S), F32)], axis=1)
    d_cf, g["bcol"] = fox_gate_bwd(proj, p["bcol"], dcum, S)
    dq_b, dk_b, dv_b = attn_bwd(sv["q_b"], 0, sv["k_b"], 0, sv["v_b"], 0, None, dcat, O_B // HP,
                                (B_NOPE + B_ROPE) ** -0.5, S, F32)
    d_b, g["gq"], g["gkv"], g["wq"], g["wkv"] = mla_pre_bwd(
        proj, p["gq"], p["gkv"], p["wq"], p["wkv"], place, cos_q, sin_q, cs_k, dq_b, dk_b, dv_b, S)
    d_d, g["g_lg"], g["g_lb"], g["ws"], g["bs"] = gmlp_bwd(proj, p["g_lg"], p["g_lb"], p["ws"], p["bs"], dcat)
    dproj = jnp.concatenate([d_a, d_b, dq_c, dk_c, dv_c, d_d, d_cf, jnp.zeros_like(d_cf)], axis=1)
    g["mix_in"] = tn_mm_mod(sv["x1"], mod, dproj, 3, 4, S, NP_TILE)
    mod = mod + emit("mix", g)
    dx1, dm[3], dm[4] = nt_mod_bwd([dproj], p["mix_in"], [0], sv["x1"], mod, dres, 4, S, NP_TILE)
    last = []

    def emit_last(dw_in, dw_out):
        last.append(emit_ffn("ffn1")(dw_in, dw_out))
        return last[0]

    dx0, _, _, dlg0, dlb0, d, mod = _ffn_bwd(
        dx1, sv["x"], sv["y1"], sv["zg1"], sv["zu1"], sv["act1"], mod, p["ffn1_in"], p["ffn1_out"], l,
        p["ln_g"][0:1], p["ln_b"][0:1], (0, 1, 2), S, emit_last)
    dm.update(d)
    g["ln_g"] = jnp.concatenate([dlg0, dlg1, dlg2], axis=0)
    g["ln_b"] = jnp.concatenate([dlb0, dlb1, dlb2], axis=0)
    dmod = jnp.concatenate([dm[i] for i in range(N_MOD)], axis=1)
    return dx0, dmod, g, last[0]


def local_step(x, c8, tgt, get, lb_logits, S, emit=None):
    B = x.shape[0] // S
    tabs = rope_tables(S)
    lb_all = lb_fwd(lb_logits)
    mods, saved = [], []
    h = x
    for l in range(DEPTH):
        pa = get(l, "ada", h)
        mod = mod_fwd(c8, pa["ada_w"], pa["wl"], pa["ada_b"])[0:B].reshape(B, N_MOD, D)

        def get_l(part, after, l=l):
            p = dict(get(l, part, after))
            if part == "mix":
                p["lb"] = lb_all[l:l + 1]
            return p

        h, sv = layer_fwd(h, mod, get_l, tabs, S)
        mods.append(mod)
        saved.append(sv)
    loss_tile, dh = loss_head(h, tgt)
    grads, dmods, dlb = [None] * DEPTH, [None] * DEPTH, [None] * DEPTH
    tie = jnp.zeros((), F32)
    for l in reversed(range(DEPTH)):
        emit_l = (lambda part, g: jnp.zeros((), F32)) if emit is None else functools.partial(emit, l)
        dh, dmods[l], grads[l], tie = layer_bwd(dh, mods[l] + tie, saved[l], tabs, S, emit_l)
        dlb[l] = grads[l].pop("lb")
    d_logits = lb_bwd(lb_logits, jnp.concatenate(dlb, axis=0))
    return loss_tile, dh, dmods, grads, d_logits


ANY = pl.BlockSpec(memory_space=pl.ANY)


def _place():
    x, y, c = lax.axis_index("x"), lax.axis_index("y"), lax.axis_index("c")
    chips = [(1 - x, y), (x, 1 - y), (1 - x, 1 - y)]
    return x, y, c, chips


def _rcopy(src, dst, sems, k, to):
    send_sems, recv_sems = sems
    return pltpu.make_async_remote_copy(src_ref=src, dst_ref=dst, send_sem=send_sems.at[k], recv_sem=recv_sems.at[k],
                                        device_id=to, device_id_type=MESH)


def _dma_sems(n_remote, n_local):
    return [pltpu.SemaphoreType.DMA((n_remote,)), pltpu.SemaphoreType.DMA((n_remote,)), pltpu.SemaphoreType.DMA((n_local,))]


def own_slot(src, chip):
    L = src.shape[0]
    return lax.dynamic_update_slice(lax.empty((L, N_CHIPS) + src.shape[1:], src.dtype), src[:, None], (0, chip, 0, 0))


HBM_SPEC = pl.BlockSpec(memory_space=pltpu.HBM)
SEM_SPEC = pl.BlockSpec(memory_space=pltpu.SEMAPHORE)
DATAFLOW = pltpu.SideEffectType.DATAFLOW_SIDE_EFFECTING


def _split_start(srcs, lands, copies, n_copies, dep, name):
    n, m = len(srcs), len(lands)

    def body(*refs):
        ins = refs[:n + m]
        send_sems, recv_sems = refs[n + m + 1], refs[n + m + 2]
        token = refs[-1]
        for k, (src, dst, to) in enumerate(copies(ins[:n], ins[n:], _place())):
            pltpu.make_async_remote_copy(src_ref=src, dst_ref=dst, send_sem=send_sems.at[k], recv_sem=recv_sems.at[k],
                                         device_id=to, device_id_type=MESH).start()
        token[...] = jnp.zeros_like(token)

    arrs = list(srcs) + list(lands)
    outs = pl.pallas_call(
        body, name=name,
        out_shape=(pltpu.SemaphoreType.DMA((n_copies,)), pltpu.SemaphoreType.DMA((n_copies,)),
                   *[pltpu.HBM(a.shape, a.dtype) for a in arrs], _sds((8, 128), F32)),
        in_specs=[HBM_SPEC] * (n + m) + [ANY],
        out_specs=(SEM_SPEC, SEM_SPEC, *[HBM_SPEC] * (n + m), pl.BlockSpec(memory_space=pltpu.VMEM)),
        input_output_aliases={i: 2 + i for i in range(n + m)},
        compiler_params=pltpu.CompilerParams(has_side_effects=DATAFLOW),
    )(*[pltpu.with_memory_space_constraint(a, pltpu.HBM) for a in arrs], dep)
    return outs[0], outs[1], list(outs[2:2 + n]), list(outs[2 + n:2 + n + m]), outs[-1]


def _split_wait(handle, arrivals, after, name):
    send_sems, recv_sems, srcs, lands, _ = handle
    n, m = len(srcs), len(lands)

    def body(*refs):
        ins = refs[:n + m]
        send_sems, recv_sems = refs[n + m], refs[n + m + 1]
        x, y, c, chips = place = _place()
        for k, (src, dst) in enumerate(arrivals(ins[:n], ins[n:], place)):
            cp = pltpu.make_async_remote_copy(src_ref=src, dst_ref=dst, send_sem=send_sems.at[k], recv_sem=recv_sems.at[k],
                                              device_id=(x, y, 1 - c), device_id_type=MESH)
            cp.wait_send()
            cp.wait_recv()

    arrs = list(srcs) + list(lands)
    outs = pl.pallas_call(
        body, name=name, out_shape=[pltpu.HBM(a.shape, a.dtype) for a in arrs],
        in_specs=[HBM_SPEC] * (n + m) + [SEM_SPEC, SEM_SPEC, ANY], out_specs=[HBM_SPEC] * (n + m),
        input_output_aliases={i: i for i in range(n + m)},
        compiler_params=pltpu.CompilerParams(has_side_effects=DATAFLOW),
    )(*arrs, send_sems, recv_sems, after)
    return list(outs[:n]), list(outs[n:])


def _ag_part(ref, k, hc):
    rh = ref.shape[2] // 2
    return ref.at[:, k, pl.ds(hc * rh, rh), :]


def ag_start(srcs, lands, dep, name):
    def copies(s, d, place):
        x, y, c, chips = place
        out = []
        for j, (px, py) in enumerate(chips):
            for i in range(len(s)):
                rh = s[i].shape[1] // 2
                out.append((s[i].at[:, pl.ds(c * rh, rh), :], _ag_part(d[i], 2 * x + y, c), (px, py, c)))
        return out

    return _split_start(srcs, lands, copies, 3 * len(srcs), dep, name)


def ag_wait(handle, after, name):
    def arrivals(s, d, place):
        x, y, c, chips = place
        out = []
        for j, (px, py) in enumerate(chips):
            for i in range(len(s)):
                rh = s[i].shape[1] // 2
                out.append((s[i].at[:, pl.ds(c * rh, rh), :], _ag_part(d[i], 2 * px + py, c)))
        return out

    return _split_wait(handle, arrivals, after, name)


def ag_forward(lands, name):
    n = len(lands)

    def body(*refs):
        bufs, token = refs[n:2 * n], refs[2 * n]
        send_sems, recv_sems = refs[2 * n + 1:]
        x, y, c, chips = _place()
        sems = (send_sems, recv_sems)
        token[...] = jnp.zeros_like(token)
        cps = []
        for j, (px, py) in enumerate(chips):
            for i in range(n):
                part = _ag_part(bufs[i], 2 * px + py, c)
                cps.append(_rcopy(part, part, sems, 3 * i + j, (x, y, 1 - c)))
        for cp in cps:
            cp.start()
        for j, (px, py) in enumerate(chips):
            for i in range(n):
                part = _ag_part(bufs[i], 2 * px + py, 1 - c)
                _rcopy(part, part, sems, 3 * i + j, (x, y, 1 - c)).wait_recv()
        for cp in cps:
            cp.wait_send()

    outs = pl.pallas_call(
        body, name=name, out_shape=[_sds(a.shape, a.dtype) for a in lands] + [_sds((8, 128), F32)],
        in_specs=[ANY] * n, out_specs=[ANY] * n + [pl.BlockSpec(memory_space=pltpu.VMEM)],
        input_output_aliases={i: i for i in range(n)}, scratch_shapes=_dma_sems(3 * n, 1)[:2])(*lands)
    return list(outs[:n]), outs[n]


def rs_start(hs, lands, dep, name):
    def copies(s, d, place):
        x, y, c, chips = place
        return [(s[i].at[2 * px + py], d[i].at[2 * x + y], (px, py, c)) for j, (px, py) in enumerate(chips) for i in range(len(s))]

    return _split_start(hs, lands, copies, 3 * len(hs), dep, name)


def rs_wait(handle, after, name):
    def arrivals(s, d, place):
        x, y, c, chips = place
        return [(s[i].at[2 * px + py], d[i].at[2 * px + py]) for j, (px, py) in enumerate(chips) for i in range(len(s))]

    return _split_wait(handle, arrivals, after, name)


def sibling_swap(arrs, name):
    n = len(arrs)
    rh = [a.shape[1] // 2 for a in arrs]

    def body(*refs):
        srcs, outs = refs[:n], refs[n:2 * n]
        send_sems, recv_sems = refs[2 * n:]
        x, y, c, _ = _place()
        cps = [_rcopy(srcs[i].at[:, pl.ds((1 - c) * rh[i], rh[i]), :], outs[i], (send_sems, recv_sems), i, (x, y, 1 - c))
               for i in range(n)]
        for cp in cps:
            cp.start()
        for cp in cps:
            cp.wait()

    return pl.pallas_call(
        body, name=name, out_shape=[_sds((N_CHIPS, r, a.shape[2]), a.dtype) for a, r in zip(arrs, rh)],
        in_specs=[ANY] * n, out_specs=[ANY] * n, scratch_shapes=_dma_sems(n, 1)[:2])(*arrs)


def sum_into(land, base, l, core, name):
    _, rh, C = land.shape
    tr = _row_tile(rh, C, N_CHIPS, mult=16, elems=1 << 20)
    nr = rh // tr

    def body(core_ref, land_ref, base_ref, o_ref):
        acc = land_ref[0].astype(F32)
        for k in range(1, N_CHIPS):
            acc = acc + land_ref[k].astype(F32)
        o_ref[...] = acc

    grid_spec = pltpu.PrefetchScalarGridSpec(
        num_scalar_prefetch=1, grid=(nr,),
        in_specs=[pl.BlockSpec((N_CHIPS, tr, C), lambda r, core_ref: (0, r, 0)), ANY],
        out_specs=pl.BlockSpec((None, tr, C), lambda r, core_ref: (l, core_ref[0] * nr + r, 0)))
    return pl.pallas_call(body, name=name, grid_spec=grid_spec, out_shape=_sds(base.shape, base.dtype),
                          input_output_aliases={2: 0}, compiler_params=_cparams(VMEM_BIG))(
        core.reshape(1).astype(jnp.int32), land, base)


def sibling_join(bases, name):
    n = len(bases)

    def body(*refs):
        bufs = refs[n:2 * n]
        send_sems, recv_sems = refs[2 * n:]
        x, y, c, _ = _place()
        sems = (send_sems, recv_sems)

        def half(i, hc):
            rh = bufs[i].shape[1] // 2
            return bufs[i].at[:, pl.ds(hc * rh, rh), :]

        sends = [_rcopy(half(i, c), half(i, c), sems, i, (x, y, 1 - c)) for i in range(n)]
        for cp in sends:
            cp.start()
        for i in range(n):
            _rcopy(half(i, 1 - c), half(i, 1 - c), sems, i, (x, y, 1 - c)).wait_recv()
        for cp in sends:
            cp.wait_send()

    return pl.pallas_call(
        body, name=name, out_shape=[_sds(b.shape, b.dtype) for b in bases], in_specs=[ANY] * n, out_specs=[ANY] * n,
        input_output_aliases={i: i for i in range(n)}, scratch_shapes=_dma_sems(n, 1)[:2])(*bases)


def ag_all(blk):
    M, C = blk.shape

    def body(x_ref, out_ref, send_sems, recv_sems, loc_sem):
        x, y, c, chips = _place()
        sems = (send_sems, recv_sems)
        me, sibling = (x, y, c), (x, y, 1 - c)

        def slot(px, py, pc):
            return out_ref.at[4 * px + 2 * py + pc]

        mine = pltpu.make_async_copy(x_ref, slot(*me), loc_sem)
        mine.start()
        first = [_rcopy(x_ref, slot(*me), sems, 0, sibling)]
        first += [_rcopy(x_ref, slot(*me), sems, 1 + j, (*chip, c)) for j, chip in enumerate(chips)]
        for cp in first:
            cp.start()
        passed = [_rcopy(slot(*chip, c), slot(*chip, c), sems, 4 + j, sibling) for j, chip in enumerate(chips)]
        for j, chip in enumerate(chips):
            _rcopy(slot(*chip, c), slot(*chip, c), sems, 1 + j, me).wait_recv()
            passed[j].start()
        _rcopy(slot(*sibling), slot(*sibling), sems, 0, me).wait_recv()
        for j, chip in enumerate(chips):
            _rcopy(slot(*chip, 1 - c), slot(*chip, 1 - c), sems, 4 + j, me).wait_recv()
        for cp in first + passed:
            cp.wait_send()
        mine.wait()

    return pl.pallas_call(
        body, name="ag_all", out_shape=_sds((8, M, C), blk.dtype),
        in_specs=[pl.BlockSpec(memory_space=pltpu.VMEM)], out_specs=pl.BlockSpec(memory_space=pltpu.VMEM),
        scratch_shapes=[pltpu.SemaphoreType.DMA((7,)), pltpu.SemaphoreType.DMA((7,)), pltpu.SemaphoreType.DMA(())],
        compiler_params=_cparams(VMEM_BIG))(blk)


WEIGHTS = ["ada_w", "ada_b", "ln_g", "ln_b", "ffn1_w_in", "ffn1_w_out", "ffn2_w_in", "ffn2_w_out", "mix_w_in", "mix_w_out",
           "hgrn_lb_logits", "hgrn_norm_g", "mla_q_norm_g", "mla_kv_norm_g", "mla_w_uq", "mla_w_ukv", "fox_b_f",
           "gmlp_ln_g", "gmlp_ln_b", "gmlp_w_s", "gmlp_b_s"]
SMALL = ["hgrn_lb_logits", "hgrn_norm_g", "mla_q_norm_g", "mla_kv_norm_g", "fox_b_f", "gmlp_ln_g", "gmlp_ln_b",
         "gmlp_w_s", "gmlp_b_s", "ln_g", "ln_b"]
GATHERED = ["ada_w", "ffn1_w_in", "ffn1_w_out", "ffn2_w_in", "ffn2_w_out", "mix_w_in", "mix_w_out", "mla_w_uq", "mla_w_ukv"]
REDUCED = GATHERED[1:]


def _col_shards(a):
    cols = a.shape[1] // N_CHIPS
    return jnp.stack([a[:, k * cols:(k + 1) * cols] for k in range(N_CHIPS)])


def add_kept_half(a, got, core, name):
    _, R, C = a.shape
    rh = R // 2
    tr = _row_tile(rh, C, mult=16, elems=1 << 19)
    nr = rh // tr

    def body(core_ref, a_ref, b_ref, o_ref):
        o_ref[...] = (a_ref[...].astype(F32) + b_ref[...].astype(F32)).astype(o_ref.dtype)

    half = pl.BlockSpec((None, tr, C), lambda k, r, core_ref: (k, r, 0))
    grid_spec = pltpu.PrefetchScalarGridSpec(
        num_scalar_prefetch=1, grid=(N_CHIPS, nr),
        in_specs=[pl.BlockSpec((None, tr, C), lambda k, r, core_ref: (k, core_ref[0] * nr + r, 0)), half],
        out_specs=half)
    return pl.pallas_call(body, name=name, grid_spec=grid_spec, out_shape=_sds((N_CHIPS, rh, C), BF16),
                          compiler_params=_cparams(VMEM_BIG))(core.reshape(1).astype(jnp.int32), a, got)


def _rows(parts, n_rows, dtype):
    flat = jnp.concatenate([p.reshape(-1) for p in parts])
    pad = n_rows * D - flat.shape[0]
    return jnp.concatenate([flat, jnp.zeros((pad,), dtype)]).reshape(n_rows, D)


def _take(flat, shapes):
    out, o = [], 0
    for shp in shapes:
        n = int(np.prod(shp))
        out.append(flat[o:o + n].reshape(shp))
        o += n
    return out


def _round_up(n, m):
    return -(-n // m) * m


def pack_small(w):
    parts = [w[n][l] for l in range(DEPTH) for n in SMALL]
    n = sum(int(np.prod(p.shape)) for p in parts)
    return _rows(parts, _round_up(-(-n // D), 8), F32)


def unpack_small(pk, like):
    shapes = [like[n].shape[1:] for l in range(DEPTH) for n in SMALL]
    pieces = _take(pk.reshape(-1), shapes)
    names = [n for l in range(DEPTH) for n in SMALL]
    return {n: jnp.stack([p for p, m in zip(pieces, names) if m == n]) for n in SMALL}


def kernel(x, c, ada_w, ada_b, ln_g, ln_b, ffn1_w_in, ffn1_w_out, ffn2_w_in, ffn2_w_out, mix_w_in, mix_w_out, hgrn_lb_logits, hgrn_norm_g, mla_q_norm_g, mla_kv_norm_g, mla_w_uq, mla_w_ukv, fox_b_f, gmlp_ln_g, gmlp_ln_b, gmlp_w_s, gmlp_b_s, loss_target, m_ada_w, m_ada_b, m_ln_g, m_ln_b, m_ffn1_w_in, m_ffn1_w_out, m_ffn2_w_in, m_ffn2_w_out, m_mix_w_in, m_mix_w_out, m_hgrn_lb_logits, m_hgrn_norm_g, m_mla_q_norm_g, m_mla_kv_norm_g, m_mla_w_uq, m_mla_w_ukv, m_fox_b_f, m_gmlp_ln_g, m_gmlp_ln_b, m_gmlp_w_s, m_gmlp_b_s, v_ada_w, v_ada_b, v_ln_g, v_ln_b, v_ffn1_w_in, v_ffn1_w_out, v_ffn2_w_in, v_ffn2_w_out, v_mix_w_in, v_mix_w_out, v_hgrn_lb_logits, v_hgrn_norm_g, v_mla_q_norm_g, v_mla_kv_norm_g, v_mla_w_uq, v_mla_w_ukv, v_fox_b_f, v_gmlp_ln_g, v_gmlp_ln_b, v_gmlp_w_s, v_gmlp_b_s):
    w = dict(zip(WEIGHTS, (ada_w, ada_b, ln_g, ln_b, ffn1_w_in, ffn1_w_out, ffn2_w_in, ffn2_w_out, mix_w_in, mix_w_out, hgrn_lb_logits, hgrn_norm_g, mla_q_norm_g, mla_kv_norm_g, mla_w_uq, mla_w_ukv, fox_b_f, gmlp_ln_g, gmlp_ln_b, gmlp_w_s, gmlp_b_s)))
    m = dict(zip(WEIGHTS, (m_ada_w, m_ada_b, m_ln_g, m_ln_b, m_ffn1_w_in, m_ffn1_w_out, m_ffn2_w_in, m_ffn2_w_out, m_mix_w_in, m_mix_w_out, m_hgrn_lb_logits, m_hgrn_norm_g, m_mla_q_norm_g, m_mla_kv_norm_g, m_mla_w_uq, m_mla_w_ukv, m_fox_b_f, m_gmlp_ln_g, m_gmlp_ln_b, m_gmlp_w_s, m_gmlp_b_s)))
    v = dict(zip(WEIGHTS, (v_ada_w, v_ada_b, v_ln_g, v_ln_b, v_ffn1_w_in, v_ffn1_w_out, v_ffn2_w_in, v_ffn2_w_out, v_mix_w_in, v_mix_w_out, v_hgrn_lb_logits, v_hgrn_norm_g, v_mla_q_norm_g, v_mla_kv_norm_g, v_mla_w_uq, v_mla_w_ukv, v_fox_b_f, v_gmlp_ln_g, v_gmlp_ln_b, v_gmlp_w_s, v_gmlp_b_s)))
    Bl, S, _ = x.shape
    T = Bl * S
    core = lax.axis_index("c")
    chip = 2 * lax.axis_index("x") + lax.axis_index("y")

    def shard(key):
        n, l = key
        if n == "ln":
            return jnp.concatenate([ln_g[l:l + 1], ln_b[l:l + 1], jnp.zeros((1, 2, D // N_CHIPS), F32)], axis=1)
        return w[n][l:l + 1].astype(BF16)

    mixers = ["mix_w_in", "mix_w_out", "mla_w_uq", "mla_w_ukv"]
    groups = [[("ada_w", 0), ("ffn1_w_in", 0), ("ffn1_w_out", 0), ("ln", 0)],
              [(n, 0) for n in mixers + ["ffn2_w_in", "ffn2_w_out"]],
              [(n, 1) for n in GATHERED + ["ln"]]]
    srcs = [[shard(k) for k in grp] for grp in groups]
    lands = [[own_slot(s, chip) for s in srcs[0]]]
    handle0 = ag_start(srcs[0], lands[0], jnp.zeros((8, 128), F32), "ag_start_0")
    chip_later = chip + handle0[-1][0, 0].astype(jnp.int32)
    lands += [[own_slot(s, chip_later) for s in grp] for grp in srcs[1:]]
    first, token = ag_forward(ag_wait(handle0, lands[2][0], "ag_wait_0")[1], "ag_forward_0")
    have = dict(zip(groups[0], first))
    handles = {}
    for gi in (1, 2):
        handles[gi] = ag_start(srcs[gi], lands[gi], token, "ag_start_%d" % gi)
        token = handles[gi][-1]
    c8 = jnp.concatenate([c, jnp.zeros((8 - Bl, D), F32)], axis=0)
    c8 = c8 + token[0, 0]

    def cat_cols(a):
        return jnp.concatenate([a[0, k] for k in range(N_CHIPS)], axis=1)

    def get(l, part, after):
        gi = 2 if l == 1 else (0 if part in ("ada", "ffn1") else 1)
        if gi in handles:
            arrived, _ = ag_forward(ag_wait(handles.pop(gi), after, "ag_wait_%d" % gi)[1], "ag_forward_%d" % gi)
            have.update(zip(groups[gi], arrived))
        if part == "ada":
            return dict(ada_w=have[("ada_w", l)], wl=0, ada_b=ada_b[l][None])
        if part == "ffn1":
            ln_full = jnp.moveaxis(have[("ln", l)][0], 0, 1).reshape(8, D)
            return dict(ffn1_in=have[("ffn1_w_in", l)], ffn1_out=have[("ffn1_w_out", l)], wl=0,
                        ln_g=ln_full[0:3], ln_b=ln_full[3:6])
        if part == "ffn2":
            return dict(ffn2_in=have[("ffn2_w_in", l)], ffn2_out=have[("ffn2_w_out", l)])
        return dict(
            mix_in=mix_in_ext(cat_cols(have[("mix_w_in", l)])), mix_out=mix_out_ext(have[("mix_w_out", l)].reshape(D, D)),
            wq=uq_ext(cat_cols(have[("mla_w_uq", l)])).astype(F32), wkv=ukv_ext(cat_cols(have[("mla_w_ukv", l)])).astype(F32),
            ng=hgrn_norm_g[l][None], gq=mla_q_norm_g[l][None], gkv=mla_kv_norm_g[l][None],
            bcol=jnp.concatenate([fox_b_f[l], jnp.zeros((8 - HEADS,), F32)])[:, None],
            g_lg=gmlp_ln_g[l][None], g_lb=gmlp_ln_b[l][None], ws=gmlp_w_s[l], bs=gmlp_b_s[l][:, :, None])

    pending = []

    def emit(l, part, g):
        if part == "mix":
            names = mixers
            by_chip = [_col_shards(mix_in_unext(g["mix_in"])), mix_out_unext(g["mix_out"]).reshape(N_CHIPS, D // N_CHIPS, D),
                       _col_shards(uq_unext(g["wq"])).astype(BF16), _col_shards(ukv_unext(g["wkv"])).astype(BF16)]
        else:
            names = [part + "_w_in", part + "_w_out"]
            by_chip = [g[part + "_in"], g[part + "_out"]]
        tag = "%d_%s" % (l, part)
        got = sibling_swap(by_chip, "sibling_swap_" + tag)
        chip_sum = [add_kept_half(a, r, core, "add_sibling") for a, r in zip(by_chip, got)]
        zones = [lax.dynamic_update_slice(lax.empty(h.shape, h.dtype), lax.dynamic_slice_in_dim(h, chip, 1, axis=0), (chip, 0, 0))
                 for h in chip_sum]
        handle = rs_start(chip_sum, zones, chip_sum[0], "rs_start_" + tag)
        pending.append((l, names, handle, tag))
        return handle[-1][0, 0]

    loss_tile, dx, dmods, grads, d_logits = local_step(
        x.reshape(T, D), c8, loss_target.reshape(T, D), get, hgrn_lb_logits, S, emit)
    loss = lax.psum(loss_tile[0, 0], ("x", "y", "c"))

    small_g = {"hgrn_lb_logits": d_logits,
               "hgrn_norm_g": jnp.stack([grads[l]["ng"][0] for l in range(DEPTH)]),
               "mla_q_norm_g": jnp.stack([grads[l]["gq"][0] for l in range(DEPTH)]),
               "mla_kv_norm_g": jnp.stack([grads[l]["gkv"][0] for l in range(DEPTH)]),
               "fox_b_f": jnp.stack([grads[l]["bcol"][0:HEADS, 0] for l in range(DEPTH)]),
               "gmlp_ln_g": jnp.stack([grads[l]["g_lg"][0] for l in range(DEPTH)]),
               "gmlp_ln_b": jnp.stack([grads[l]["g_lb"][0] for l in range(DEPTH)]),
               "gmlp_w_s": jnp.stack([grads[l]["ws"] for l in range(DEPTH)]),
               "gmlp_b_s": jnp.stack([grads[l]["bs"][:, :, 0] for l in range(DEPTH)])}
    small_g["ln_g"] = jnp.stack([grads[l]["ln_g"] for l in range(DEPTH)])
    small_g["ln_b"] = jnp.stack([grads[l]["ln_b"] for l in range(DEPTH)])
    pk_small = pack_small(small_g)
    n_small = pk_small.shape[0]
    extras = [dmods[l] for l in range(DEPTH)] + [c]
    n_extra = _round_up(-(-sum(int(np.prod(e.shape)) for e in extras) // D), 8)
    gathered = ag_all(jnp.concatenate([pk_small, _rows(extras, n_extra, F32)], axis=0))
    g_small = unpack_small(sum_slots(gathered[:, 0:n_small], 8, "sum_small"), small_g)
    ext = gathered[:, n_small:].reshape(8, -1)
    n_dmod = DEPTH * Bl * N_MOD * D
    dmod_all = ext[:, 0:n_dmod].reshape(8, DEPTH, Bl, N_MOD * D)
    c_all = ext[:, n_dmod:n_dmod + Bl * D].reshape(8 * Bl, D)
    g_ada_w, g_ada_b = [], []
    ncol = N_MOD * D // N_CHIPS
    for l in range(DEPTH):
        dm = dmod_all[:, l].reshape(8 * Bl, N_MOD * D)
        g_ada_w.append(ada_grad(c_all, lax.dynamic_slice_in_dim(dm, chip * ncol, ncol, axis=1)))
        g_ada_b.append(sum_slots(dm.reshape(8 * Bl, N_MOD, D), 8 * Bl, "sum_ada_b").reshape(N_MOD * D))
    g_ada_w, g_ada_b = jnp.stack(g_ada_w), jnp.stack(g_ada_b)

    red = {n: lax.empty(w[n].shape, F32) for n in REDUCED}

    def arrive(entry, after):
        l, names, handle, tag = entry
        for n, land in zip(names, rs_wait(handle, after, "rs_wait_" + tag)[1]):
            red[n] = sum_into(land, red[n], l, core, "sum_chips")

    for entry in pending[:-1]:
        arrive(entry, dx)
    late = pending[-1][1]
    early = [n for n in REDUCED if n not in late]
    grad = dict(zip(early, sibling_join([red[n] for n in early], "sibling_join_a")))
    grad.update(g_small)
    grad["ada_w"], grad["ada_b"] = g_ada_w, g_ada_b
    for n in ("ln_g", "ln_b"):
        grad[n] = lax.dynamic_slice_in_dim(g_small[n], chip * (D // N_CHIPS), D // N_CHIPS, axis=2)
    out = {"grad": grad, "delta": {}, "new_m": {}, "new_v": {}}

    def update(n):
        shp = w[n].shape
        two_d = (-1, shp[-1])
        res = adamw(w[n].reshape(two_d), grad[n].reshape(two_d), m[n].reshape(two_d), v[n].reshape(two_d), "adamw_" + n,
                    echo=n in REDUCED)
        grad[n] = (res[3] if n in REDUCED else grad[n]).reshape(shp)
        for key, r in zip(("delta", "new_m", "new_v"), res):
            out[key][n] = r.reshape(shp)

    for n in WEIGHTS:
        if n not in late:
            update(n)
    arrive(pending[-1], out["delta"]["ffn2_w_in"])
    grad.update(zip(late, sibling_join([red[n] for n in late], "sibling_join_b")))
    for n in late:
        update(n)
    outs = [loss, dx.reshape(Bl, S, D)]
    for key in ("grad", "delta", "new_m", "new_v"):
        outs += [out[key][n] for n in WEIGHTS]
    return tuple(outs)
```

```python
import functools

import jax
import jax.numpy as jnp
import numpy as np
from jax import lax
from jax.experimental import pallas as pl
from jax.experimental.pallas import tpu as pltpu

F32, BF16 = jnp.float32, jnp.bfloat16
MESH = pl.DeviceIdType.MESH

N_CHIPS = 4
D = 1024
DEPTH = 2
FF = 2816
N_MOD = 9
GW = 256
HEADS = 4
HD = 64
HP = 128
A_CHUNK = 16
LB_FLOOR = 1e-30
B_Q_LORA, B_KV_LORA, B_NOPE, B_ROPE = 256, 128, 64, 32
ROPE_THETA = 10000.0
D_CHUNK = 128
ALPHA = (2 * DEPTH) ** 0.25
LN_EPS = 1e-5
RMS_EPS = 1e-6
ADAM_LR, ADAM_B1, ADAM_B2, ADAM_EPS, ADAM_WD, ADAM_STEP = 0.001, 0.9, 0.999, 1e-08, 0.01, 10

NP = 3840
NP_TILE = 1920
C_A, C_B, C_CQ, C_CK, C_CV, C_D, C_CF = 0, 1024, 1536, 2048, 2560, 3072, 3584
NCAT = 1536
O_A, O_B, O_C, O_D = 0, 256, 768, 1280

VMEM_BIG = 48 << 20
VMEM_MOST = 58 << 20


def _cparams(vmem=None):
    return pltpu.CompilerParams(vmem_limit_bytes=vmem) if vmem else pltpu.CompilerParams()


def _sds(shape, dtype):
    return jax.ShapeDtypeStruct(tuple(shape), dtype)


@jax.custom_vjp
def _mm(a, w):
    return jnp.dot(a.astype(BF16), w.astype(BF16), preferred_element_type=F32)


def _mm_f(a, w):
    return _mm(a, w), (a, w)


def _mm_b(res, g):
    a, w = res
    gb = g.astype(BF16)
    da = lax.dot_general(gb, w.astype(BF16), (((1,), (1,)), ((), ())), preferred_element_type=F32)
    dw = lax.dot_general(a.astype(BF16), gb, (((0,), (0,)), ((), ())), preferred_element_type=F32)
    return da.astype(a.dtype), dw.astype(w.dtype)


_mm.defvjp(_mm_f, _mm_b)


@jax.custom_vjp
def _mm_nt(a, b):
    return lax.dot_general(a.astype(BF16), b.astype(BF16), (((1,), (1,)), ((), ())), preferred_element_type=F32)


def _mm_nt_f(a, b):
    return _mm_nt(a, b), (a, b)


def _mm_nt_b(res, g):
    a, b = res
    gb = g.astype(BF16)
    da = jnp.dot(gb, b.astype(BF16), preferred_element_type=F32)
    db = lax.dot_general(gb, a.astype(BF16), (((0,), (0,)), ((), ())), preferred_element_type=F32)
    return da.astype(a.dtype), db.astype(b.dtype)


_mm_nt.defvjp(_mm_nt_f, _mm_nt_b)


@jax.custom_vjp
def _mm_tn(a, b):
    return lax.dot_general(a.astype(BF16), b.astype(BF16), (((0,), (0,)), ((), ())), preferred_element_type=F32)


def _mm_tn_f(a, b):
    return _mm_tn(a, b), (a, b)


def _mm_tn_b(res, g):
    a, b = res
    gb = g.astype(BF16)
    da = lax.dot_general(b.astype(BF16), gb, (((1,), (1,)), ((), ())), preferred_element_type=F32)
    db = jnp.dot(a.astype(BF16), gb, preferred_element_type=F32)
    return da.astype(a.dtype), db.astype(b.dtype)


_mm_tn.defvjp(_mm_tn_f, _mm_tn_b)


def _split3(x):
    p1 = x.astype(BF16)
    r = x - p1.astype(F32)
    p2 = r.astype(BF16)
    return p1, p2, (r - p2.astype(F32)).astype(BF16)


@jax.custom_vjp
def _sel_r(x, sel):
    s = sel.astype(BF16)
    return sum(jnp.dot(p, s, preferred_element_type=F32) for p in _split3(x))


def _sel_r_f(x, sel):
    return _sel_r(x, sel), sel


def _sel_r_b(sel, g):
    s = sel.astype(BF16)
    dx = sum(lax.dot_general(p, s, (((1,), (1,)), ((), ())), preferred_element_type=F32) for p in _split3(g))
    return dx, jnp.zeros_like(sel)


_sel_r.defvjp(_sel_r_f, _sel_r_b)


@jax.custom_vjp
def _sel_l(sel, x):
    s = sel.astype(BF16)
    return sum(jnp.dot(s, p, preferred_element_type=F32) for p in _split3(x))


def _sel_l_f(sel, x):
    return _sel_l(sel, x), sel


def _sel_l_b(sel, g):
    s = sel.astype(BF16)
    dx = sum(lax.dot_general(s, p, (((0,), (0,)), ((), ())), preferred_element_type=F32) for p in _split3(g))
    return jnp.zeros_like(sel), dx


_sel_l.defvjp(_sel_l_f, _sel_l_b)


def _iota(shape, dim):
    return lax.broadcasted_iota(jnp.int32, shape, dim)


def _head_sum_mats():
    e = (_iota((GW, HP), 0) // HD == _iota((GW, HP), 1)).astype(F32)
    et = (_iota((HP, GW), 1) // HD == _iota((HP, GW), 0)).astype(F32)
    return e, et


def _modulate(x, mod_ref, sh, sc):
    return x * (1.0 + mod_ref[sc:sc + 1, :]) + mod_ref[sh:sh + 1, :]


def _ln_res(x, y, gate, lg, lb, gs):
    r = ALPHA * x + gs * (1.0 + gate) * y
    mu = jnp.mean(r, axis=-1, keepdims=True)
    var = jnp.mean(jnp.square(r - mu), axis=-1, keepdims=True)
    return (r - mu) * lax.rsqrt(var + LN_EPS) * lg + lb


def _rms(x, g):
    return x * lax.rsqrt(jnp.mean(x * x, axis=-1, keepdims=True) + RMS_EPS) * g


def _tile(n, pref):
    return pref if n % pref == 0 else n


def mod_fwd(c8, w, l, b):
    tn = w.shape[3]
    n = N_CHIPS * tn

    def body(c_ref, w_ref, b_ref, o_ref):
        h = jax.nn.silu(c_ref[...]).astype(BF16)
        o_ref[...] = jnp.dot(h, w_ref[...], preferred_element_type=F32) + b_ref[...]

    return pl.pallas_call(
        body, name="mod_fwd", grid=(N_CHIPS,),
        in_specs=[pl.BlockSpec((8, D), lambda j: (0, 0)), pl.BlockSpec((None, None, D, tn), lambda j: (l, j, 0, 0)),
                  pl.BlockSpec((1, tn), lambda j: (0, j))],
        out_specs=pl.BlockSpec((8, tn), lambda j: (0, j)), out_shape=_sds((8, n), F32),
        compiler_params=_cparams(VMEM_BIG))(c8, w, b)


def ffn_in_fwd(x, mod, w_in, l, sh, sc, S):
    T = x.shape[0]
    tm, tn = _tile(S, 1024), FF // 2
    tpb, nj = S // tm, 2

    def body(x_ref, mod_ref, wg_ref, wu_ref, zg_ref, zu_ref, act_ref, h_ref):
        @pl.when(pl.program_id(1) == 0)
        def _():
            h_ref[...] = _modulate(x_ref[...], mod_ref, sh, sc).astype(BF16)
        g = jnp.dot(h_ref[...], wg_ref[...], preferred_element_type=F32)
        u = jnp.dot(h_ref[...], wu_ref[...], preferred_element_type=F32)
        zg_ref[...] = g.astype(BF16)
        zu_ref[...] = u.astype(BF16)
        act_ref[...] = (jax.nn.silu(g) * u).astype(BF16)

    return pl.pallas_call(
        body, name="ffn_in_fwd", grid=(T // tm, nj),
        in_specs=[pl.BlockSpec((tm, D), lambda i, j: (i, 0)),
                  pl.BlockSpec((None, N_MOD, D), lambda i, j: (i // tpb, 0, 0)),
                  pl.BlockSpec((None, None, D, tn), lambda i, j: (l, j, 0, 0)),
                  pl.BlockSpec((None, None, D, tn), lambda i, j: (l, j + nj, 0, 0))],
        out_specs=[pl.BlockSpec((tm, tn), lambda i, j: (i, j))] * 3,
        out_shape=[_sds((T, FF), BF16)] * 3,
        scratch_shapes=[pltpu.VMEM((tm, D), BF16)],
        compiler_params=_cparams(VMEM_BIG))(x, mod, w_in, w_in)


def mix_in_fwd(x, mod, w, sh, sc, S):
    T = x.shape[0]
    n = w.shape[1]
    tm, tn = _tile(S, 1024), NP_TILE
    tpb = S // tm

    def body(x_ref, mod_ref, w_ref, o_ref, h_ref):
        @pl.when(pl.program_id(1) == 0)
        def _():
            h_ref[...] = _modulate(x_ref[...], mod_ref, sh, sc).astype(BF16)
        o_ref[...] = jnp.dot(h_ref[...], w_ref[...], preferred_element_type=F32)

    return pl.pallas_call(
        body, name="mix_in_fwd", grid=(T // tm, n // tn),
        in_specs=[pl.BlockSpec((tm, D), lambda i, j: (i, 0)),
                  pl.BlockSpec((None, N_MOD, D), lambda i, j: (i // tpb, 0, 0)),
                  pl.BlockSpec((D, tn), lambda i, j: (0, j))],
        out_specs=pl.BlockSpec((tm, tn), lambda i, j: (i, j)), out_shape=_sds((T, n), F32),
        scratch_shapes=[pltpu.VMEM((tm, D), BF16)],
        compiler_params=_cparams(VMEM_BIG))(x, mod, w)


def out_ln_fwd(act, w_out, x, mod, lg, lb, gate, gs, S, l=None):
    T, K = act.shape
    tm = _tile(S, 512)
    tpb = S // tm

    def body(a_ref, w_ref, x_ref, mod_ref, lg_ref, lb_ref, y_ref, xn_ref):
        y = jnp.dot(a_ref[...], w_ref[...].reshape(K, D), preferred_element_type=F32)
        y_ref[...] = y
        xn_ref[...] = _ln_res(x_ref[...], y, mod_ref[gate:gate + 1, :], lg_ref[...], lb_ref[...], gs)

    if l is None:
        w_spec = pl.BlockSpec((K, D), lambda i: (0, 0))
    else:
        w_spec = pl.BlockSpec((None, N_CHIPS, K // N_CHIPS, D), lambda i: (l, 0, 0, 0))
    return pl.pallas_call(
        body, name="out_ln_fwd", grid=(T // tm,),
        in_specs=[pl.BlockSpec((tm, K), lambda i: (i, 0)), w_spec,
                  pl.BlockSpec((tm, D), lambda i: (i, 0)),
                  pl.BlockSpec((None, N_MOD, D), lambda i: (i // tpb, 0, 0)),
                  pl.BlockSpec((1, D), lambda i: (0, 0)), pl.BlockSpec((1, D), lambda i: (0, 0))],
        out_specs=[pl.BlockSpec((tm, D), lambda i: (i, 0))] * 2,
        out_shape=[_sds((T, D), F32), _sds((T, D), F32)],
        compiler_params=_cparams(VMEM_BIG))(act, w_out, x, mod, lg, lb)


def ln_res_bwd(dxn, x, y, mod, lg, lb, gate, gs, S):
    T = x.shape[0]
    B = T // S
    tm = _tile(S, 512)
    tpb = S // tm

    def body(d_ref, x_ref, y_ref, mod_ref, lg_ref, lb_ref, dx_ref, dy_ref, dg_ref, dlg_ref, dlb_ref):
        i = pl.program_id(0)
        f = functools.partial(_ln_res, gs=gs)
        _, vjp = jax.vjp(f, x_ref[...], y_ref[...], mod_ref[gate:gate + 1, :], lg_ref[...], lb_ref[...])
        dx, dy, dg, dlg, dlb = vjp(d_ref[...])
        dx_ref[...] = dx
        dy_ref[...] = dy.astype(BF16)

        @pl.when(i % tpb == 0)
        def _():
            dg_ref[...] = jnp.zeros_like(dg_ref)

        @pl.when(i == 0)
        def _():
            dlg_ref[...] = jnp.zeros_like(dlg_ref)
            dlb_ref[...] = jnp.zeros_like(dlb_ref)

        dg_ref[...] += dg
        dlg_ref[...] += dlg
        dlb_ref[...] += dlb

    tok = pl.BlockSpec((tm, D), lambda i: (i, 0))
    vec = pl.BlockSpec((1, D), lambda i: (0, 0))
    return pl.pallas_call(
        body, name="ln_res_bwd", grid=(T // tm,),
        in_specs=[tok, tok, tok, pl.BlockSpec((None, N_MOD, D), lambda i: (i // tpb, 0, 0)), vec, vec],
        out_specs=[tok, tok, pl.BlockSpec((None, 1, D), lambda i: (i // tpb, 0, 0)), vec, vec],
        out_shape=[_sds((T, D), F32), _sds((T, D), BF16), _sds((B, 1, D), F32), _sds((1, D), F32), _sds((1, D), F32)],
        compiler_params=_cparams(VMEM_BIG))(dxn, x, y, mod, lg, lb)


def swiglu_bwd(dy, w_out, l, zg, zu, S):
    T = dy.shape[0]
    tm, tn = _tile(S, 1024), FF // 2

    def body(dy_ref, w_ref, zg_ref, zu_ref, dg_ref, du_ref):
        da = lax.dot_general(dy_ref[...], w_ref[...].reshape(tn, D), (((1,), (1,)), ((), ())), preferred_element_type=F32)
        g, u = zg_ref[...].astype(F32), zu_ref[...].astype(F32)
        sg = jax.nn.sigmoid(g)
        dg_ref[...] = (da * u * (sg * (1.0 + g * (1.0 - sg)))).astype(BF16)
        du_ref[...] = (da * (g * sg)).astype(BF16)

    zt = pl.BlockSpec((tm, tn), lambda i, j: (i, j))
    return pl.pallas_call(
        body, name="swiglu_bwd", grid=(T // tm, FF // tn),
        in_specs=[pl.BlockSpec((tm, D), lambda i, j: (i, 0)),
                  pl.BlockSpec((None, 2, FF // N_CHIPS, D), lambda i, j: (l, j, 0, 0)), zt, zt],
        out_specs=[zt, zt], out_shape=[_sds((T, FF), BF16), _sds((T, FF), BF16)],
        compiler_params=_cparams(VMEM_BIG))(dy, w_out, zg, zu)


def nt_plain(dy, w):
    T = dy.shape[0]
    K = w.shape[0]
    tm = _tile(T, 1024)

    def body(dy_ref, w_ref, o_ref):
        o_ref[...] = lax.dot_general(dy_ref[...], w_ref[...], (((1,), (1,)), ((), ())), preferred_element_type=F32)

    return pl.pallas_call(
        body, name="nt_plain", grid=(T // tm,),
        in_specs=[pl.BlockSpec((tm, D), lambda i: (i, 0)), pl.BlockSpec((K, D), lambda i: (0, 0))],
        out_specs=pl.BlockSpec((tm, K), lambda i: (i, 0)), out_shape=_sds((T, K), F32),
        compiler_params=_cparams(VMEM_BIG))(dy, w)


def _tn_step(acc, o_ref, lhs, rhs, t, nt):
    part = lax.dot_general(lhs, rhs, (((0,), (0,)), ((), ())), preferred_element_type=F32)
    if nt == 1:
        o_ref[...] = part.astype(o_ref.dtype)
        return

    @pl.when(t == 0)
    def _():
        acc[...] = part

    @pl.when((t > 0) & (t < nt - 1))
    def _():
        acc[...] += part

    @pl.when(t == nt - 1)
    def _():
        o_ref[...] = (acc[...] + part).astype(o_ref.dtype)


def tn_mm(a, b, tk):
    T, K = a.shape
    N = b.shape[1]
    tt = _tile(T, 2048)
    nt = T // tt

    def body(a_ref, b_ref, o_ref, acc):
        _tn_step(acc, o_ref, a_ref[...], b_ref[...], pl.program_id(1), nt)

    return pl.pallas_call(
        body, name="tn_mm", grid=(K // tk, nt),
        in_specs=[pl.BlockSpec((tt, tk), lambda k, t: (t, k)), pl.BlockSpec((tt, N), lambda k, t: (t, 0))],
        out_specs=pl.BlockSpec((tk, N), lambda k, t: (k, 0)), out_shape=_sds((K, N), BF16),
        scratch_shapes=[pltpu.VMEM((tk, N), F32)], compiler_params=_cparams(VMEM_BIG))(a, b)


def tn_mm_mod(x, mod, b, sh, sc, S, tn):
    T = x.shape[0]
    N = b.shape[1]
    tt = _tile(S, 1024)
    tpb = S // tt
    nt = T // tt

    def body(x_ref, mod_ref, b_ref, o_ref, acc):
        h = _modulate(x_ref[...], mod_ref, sh, sc).astype(BF16)
        _tn_step(acc, o_ref, h, b_ref[...], pl.program_id(1), nt)

    return pl.pallas_call(
        body, name="tn_mm_mod", grid=(N // tn, nt),
        in_specs=[pl.BlockSpec((tt, D), lambda j, t: (t, 0)),
                  pl.BlockSpec((None, N_MOD, D), lambda j, t: (t // tpb, 0, 0)),
                  pl.BlockSpec((tt, tn), lambda j, t: (t, j))],
        out_specs=pl.BlockSpec((D, tn), lambda j, t: (0, j)), out_shape=_sds((D, N), BF16),
        scratch_shapes=[pltpu.VMEM((D, tn), F32)], compiler_params=_cparams(VMEM_BIG))(x, mod, b)


def tn_mm_mod_shards(x, mod, bg, bu, sh, sc, S):
    T = x.shape[0]
    tn = FF // 2
    tt = _tile(S, 1024)
    tpb = S // tt
    nt = T // tt

    def body(x_ref, mod_ref, bg_ref, bu_ref, o_ref, acc):
        j, t = pl.program_id(0), pl.program_id(1)
        h = _modulate(x_ref[...], mod_ref, sh, sc).astype(BF16)

        @pl.when(j < 2)
        def _():
            _tn_step(acc, o_ref, h, bg_ref[...], t, nt)

        @pl.when(j >= 2)
        def _():
            _tn_step(acc, o_ref, h, bu_ref[...], t, nt)

    return pl.pallas_call(
        body, name="tn_mm_mod_shards", grid=(N_CHIPS, nt),
        in_specs=[pl.BlockSpec((tt, D), lambda j, t: (t, 0)),
                  pl.BlockSpec((None, N_MOD, D), lambda j, t: (t // tpb, 0, 0)),
                  pl.BlockSpec((tt, tn), lambda j, t: (jnp.where(j < 2, t, 0), jnp.minimum(j, 1))),
                  pl.BlockSpec((tt, tn), lambda j, t: (jnp.where(j < 2, 0, t), jnp.maximum(j - 2, 0)))],
        out_specs=pl.BlockSpec((None, D, tn), lambda j, t: (j, 0, 0)), out_shape=_sds((N_CHIPS, D, tn), BF16),
        scratch_shapes=[pltpu.VMEM((D, tn), F32)], compiler_params=_cparams(VMEM_BIG))(x, mod, bg, bu)


def nt_mod_bwd(ds, w, offs, x, mod, dres, sc, S, tk, l=None):
    T = x.shape[0]
    B = T // S
    tm = _tile(S, 1024)
    tpb = S // tm
    Kd = ds[0].shape[1]
    nk = Kd // tk
    n_in = len(ds)

    def body(*refs):
        d_refs, w_refs = refs[:n_in], refs[n_in:2 * n_in]
        x_ref, mod_ref, r_ref, dx_ref, dsh_ref, dsc_ref, acc = refs[2 * n_in:]
        i, k = pl.program_id(0), pl.program_id(1)

        part = sum(lax.dot_general(d_ref[...], w_ref[...], (((1,), (1,)), ((), ())), preferred_element_type=F32)
                   for d_ref, w_ref in zip(d_refs, w_refs))

        @pl.when(k == 0)
        def _():
            acc[...] = part

        @pl.when(k > 0)
        def _():
            acc[...] += part

        @pl.when(k == nk - 1)
        def _():
            dh = acc[...]
            dx_ref[...] = dh * (1.0 + mod_ref[sc:sc + 1, :]) + r_ref[...]

            @pl.when(i % tpb == 0)
            def _():
                dsh_ref[...] = jnp.zeros_like(dsh_ref)
                dsc_ref[...] = jnp.zeros_like(dsc_ref)

            dsh_ref[...] += jnp.sum(dh, axis=0, keepdims=True)
            dsc_ref[...] += jnp.sum(dh * x_ref[...], axis=0, keepdims=True)

    tok = pl.BlockSpec((tm, D), lambda i, k: (i, 0))
    vec = pl.BlockSpec((None, 1, D), lambda i, k: (i // tpb, 0, 0))
    in_specs = [pl.BlockSpec((tm, tk), lambda i, k: (i, k)) for _ in ds]
    if l is None:
        in_specs += [pl.BlockSpec((D, tk), functools.partial(lambda i, k, o: (0, k + o), o=off // tk)) for off in offs]
    else:
        in_specs += [pl.BlockSpec((None, None, D, tk), functools.partial(lambda i, k, o: (l, k + o, 0, 0), o=off)) for off in offs]
    in_specs += [tok, pl.BlockSpec((None, N_MOD, D), lambda i, k: (i // tpb, 0, 0)), tok]
    return pl.pallas_call(
        body, name="nt_mod_bwd", grid=(T // tm, nk), in_specs=in_specs,
        out_specs=[tok, vec, vec],
        out_shape=[_sds((T, D), F32), _sds((B, 1, D), F32), _sds((B, 1, D), F32)],
        scratch_shapes=[pltpu.VMEM((tm, D), F32)],
        compiler_params=_cparams(VMEM_MOST))(*ds, *([w] * n_in), x, mod, dres)


def loss_head(y, tgt):
    T = y.shape[0]
    tm = _tile(T, 512)

    def body(y_ref, t_ref, l_ref, d_ref):
        @pl.when(pl.program_id(0) == 0)
        def _():
            l_ref[...] = jnp.zeros_like(l_ref)
        e = y_ref[...] - t_ref[...]
        d_ref[...] = e * (1.0 / D)
        l_ref[...] += 0.5 * jnp.sum(jnp.sum(e * e, axis=1, keepdims=True) * (1.0 / D))

    tok = pl.BlockSpec((tm, D), lambda i: (i, 0))
    return pl.pallas_call(
        body, name="loss_head", grid=(T // tm,), in_specs=[tok, tok],
        out_specs=[pl.BlockSpec((8, 128), lambda i: (0, 0)), tok],
        out_shape=[_sds((8, 128), F32), _sds((T, D), F32)],
        compiler_params=_cparams(VMEM_BIG))(y, tgt)


def _hgrn_block(q, fz, inp, go, st, lb, ng, blk):
    nc = blk // A_CHUNK
    lb_eff = jnp.maximum(lb, LB_FLOOR)
    log_f = jnp.logaddexp(jnp.log(lb_eff), jnp.log1p(-lb) + jax.nn.log_sigmoid(fz))
    k = (1.0 - lb) * jax.nn.sigmoid(-fz) - (lb_eff - lb)
    qf = jax.nn.silu(q)
    same_chunk = _iota((blk, blk), 0) // A_CHUNK == _iota((blk, blk), 1) // A_CHUNK
    tril = (same_chunk & (_iota((blk, blk), 1) <= _iota((blk, blk), 0))).astype(F32)
    G = _sel_l(tril, log_f)
    e_mat, et_mat = _head_sum_mats()
    G4, q4, k4, v4 = (z.reshape(nc, A_CHUNK, GW) for z in (G, qf, k, inp))
    shp = (nc, A_CHUNK, A_CHUNK, GW)
    one = (1, A_CHUNK, A_CHUNK, GW)
    mask = jnp.where(_iota(one, 2) <= _iota(one, 1), 0.0, -jnp.inf)
    decay = jnp.exp((G4[:, :, None, :] - G4[:, None, :, :]) + mask)
    prod = q4[:, :, None, :] * k4[:, None, :, :] * decay
    scores = _mm(prod.reshape(nc * A_CHUNK * A_CHUNK, GW), e_mat.astype(BF16))
    spread = _mm(scores, et_mat.astype(BF16)).reshape(shp)
    o_intra = jnp.sum(spread * v4[:, None, :, :], axis=2).reshape(blk, GW)
    head_diag = (_iota((GW, GW), 0) // HD == _iota((GW, GW), 1) // HD).astype(F32)
    g_last = [jnp.sum(log_f[c * A_CHUNK:(c + 1) * A_CHUNK], axis=0, keepdims=True) for c in range(nc)]
    g_last_b = jnp.concatenate([jnp.broadcast_to(g, (A_CHUNK, GW)) for g in g_last], axis=0)
    q_dec = qf * jnp.exp(G)
    k_end = k * jnp.exp(g_last_b - G)
    outs = []
    for c in range(nc):
        rows = slice(c * A_CHUNK, (c + 1) * A_CHUNK)
        outs.append(_mm_nt(q_dec[rows], st))
        st = st * jnp.exp(g_last[c]) + _mm_tn(inp[rows], k_end[rows]) * head_diag
    o = o_intra + jnp.concatenate(outs, axis=0)
    ms = _sel_r(o * o, e_mat) * (1.0 / HD)
    o = o * _sel_r(lax.rsqrt(ms + RMS_EPS), et_mat) * ng
    return o * jax.nn.silu(go), st


HGRN_BLK = 128


def hgrn_fwd(proj, lb, ng, S):
    T = proj.shape[0]
    B = T // S
    blk = min(HGRN_BLK, S)
    nb = S // blk

    def body(p_ref, lb_ref, ng_ref, o_ref, st_out_ref, st_ref):
        @pl.when(pl.program_id(1) == 0)
        def _():
            st_ref[...] = jnp.zeros_like(st_ref)
        st_out_ref[...] = st_ref[...]
        p = p_ref[...]
        o, st = _hgrn_block(p[:, 0:GW], p[:, GW:2 * GW], p[:, 2 * GW:3 * GW], p[:, 3 * GW:4 * GW],
                            st_ref[...], lb_ref[...], ng_ref[...], blk)
        o_ref[...] = o.astype(BF16)
        st_ref[...] = st

    vec = pl.BlockSpec((1, GW), lambda b, j: (0, 0))
    return pl.pallas_call(
        body, name="hgrn_fwd", grid=(B, nb),
        in_specs=[pl.BlockSpec((blk, 4 * GW), lambda b, j: (b * nb + j, C_A // (4 * GW))), vec, vec],
        out_specs=[pl.BlockSpec((blk, GW), lambda b, j: (b * nb + j, 0)),
                   pl.BlockSpec((None, GW, GW), lambda b, j: (b * nb + j, 0, 0))],
        out_shape=[_sds((T, GW), BF16), _sds((B * nb, GW, GW), F32)],
        scratch_shapes=[pltpu.VMEM((GW, GW), F32)],
        compiler_params=_cparams(VMEM_BIG))(proj, lb, ng)


def hgrn_bwd(proj, states, dcat, lb, ng, S):
    T = proj.shape[0]
    B = T // S
    blk = min(HGRN_BLK, S)
    nb = S // blk

    def body(p_ref, st_in_ref, do_ref, lb_ref, ng_ref, dp_ref, dlb_ref, dng_ref, dst_ref):
        b, j = pl.program_id(0), pl.program_id(1)

        @pl.when(j == 0)
        def _():
            dst_ref[...] = jnp.zeros_like(dst_ref)

        @pl.when((b == 0) & (j == 0))
        def _():
            dlb_ref[...] = jnp.zeros_like(dlb_ref)
            dng_ref[...] = jnp.zeros_like(dng_ref)

        p = p_ref[...]
        f = functools.partial(_hgrn_block, blk=blk)
        _, vjp = jax.vjp(f, p[:, 0:GW], p[:, GW:2 * GW], p[:, 2 * GW:3 * GW], p[:, 3 * GW:4 * GW],
                         st_in_ref[...], lb_ref[...], ng_ref[...])
        dq, df, di, dg, dst, dlb, dng = vjp((do_ref[...], dst_ref[...]))
        dp_ref[...] = jnp.concatenate([dq, df, di, dg], axis=1).astype(BF16)
        dst_ref[...] = dst
        dlb_ref[...] += dlb
        dng_ref[...] += dng

    def rev(b, j):
        return b * nb + (nb - 1 - j)

    vec = pl.BlockSpec((1, GW), lambda b, j: (0, 0))
    return pl.pallas_call(
        body, name="hgrn_bwd", grid=(B, nb),
        in_specs=[pl.BlockSpec((blk, 4 * GW), lambda b, j: (rev(b, j), C_A // (4 * GW))),
                  pl.BlockSpec((None, GW, GW), lambda b, j: (rev(b, j), 0, 0)),
                  pl.BlockSpec((blk, GW), lambda b, j: (rev(b, j), O_A // GW)), vec, vec],
        out_specs=[pl.BlockSpec((blk, 4 * GW), lambda b, j: (rev(b, j), 0)), vec, vec],
        out_shape=[_sds((T, 4 * GW), BF16), _sds((1, GW), F32), _sds((1, GW), F32)],
        scratch_shapes=[pltpu.VMEM((GW, GW), F32)],
        compiler_params=_cparams(VMEM_BIG))(proj, states, dcat, lb, ng)


ATT_TQ = 256


ATT_BANDS = 8


def _attn_block(q, k, v, cum, qpos0, scale, use_cum, n_free):
    s = _mm_nt(q, k) * scale
    if use_cum:
        s = s - cum
    band = s[:, n_free:]
    visible = _iota(band.shape, 1) <= (qpos0 - n_free) + _iota(band.shape, 0)
    band = jnp.where(visible, band, -jnp.inf)
    m = jnp.max(band, axis=-1, keepdims=True)
    if n_free:
        free = s[:, :n_free]
        m = jnp.maximum(m, jnp.max(free, axis=-1, keepdims=True))
    e = jnp.exp(band - m)
    denom = jnp.sum(e, axis=-1, keepdims=True)
    o = _mm(e, v[n_free:])
    if n_free:
        e = jnp.exp(free - m)
        denom = denom + jnp.sum(e, axis=-1, keepdims=True)
        o = o + _mm(e, v[:n_free])
    return o * (1.0 / denom)


def _bands(S, tq):
    nq = S // tq
    nb = min(ATT_BANDS, nq)
    per = nq // nb
    return [(r * per, (r + 1) * per, (r + 1) * per * tq) for r in range(nb)]


def attn_fwd(qa, qo, ka, ko, va, vo, cum, scale, S):
    T = qa.shape[0]
    B = T // S
    tq = min(ATT_TQ, S)
    nq = S // tq
    use_cum = cum is not None

    def body(*refs):
        if use_cum:
            q_ref, k_ref, v_ref, c_ref, o_ref = refs
        else:
            (q_ref, k_ref, v_ref, o_ref), c_ref = refs, None
        h, i = pl.program_id(1), pl.program_id(2)
        for lo, hi, kw in _bands(S, tq):
            @pl.when((i >= lo) & (i < hi))
            def _():
                crow = c_ref[pl.ds(h, 1), 0:kw] if use_cum else None
                o = _attn_block(q_ref[...], k_ref[0:kw, :], v_ref[0:kw, :], crow, i * tq, scale, use_cum, lo * tq)
                o_ref[...] = o.astype(BF16)

    in_specs = [pl.BlockSpec((tq, HP), lambda b, h, i: (b * nq + i, qo + h)),
                pl.BlockSpec((S, HP), lambda b, h, i: (b, ko + h)),
                pl.BlockSpec((S, HP), lambda b, h, i: (b, vo + h))]
    args = [qa, ka, va]
    if use_cum:
        in_specs.append(pl.BlockSpec((None, 8, S), lambda b, h, i: (b, 0, 0)))
        args.append(cum)
    return pl.pallas_call(
        body, name="attn_fwd", grid=(B, HEADS, nq), in_specs=in_specs,
        out_specs=pl.BlockSpec((tq, HP), lambda b, h, i: (b * nq + i, h)),
        out_shape=_sds((T, HEADS * HP), BF16),
        compiler_params=_cparams(VMEM_BIG))(*args)


def _attn_block_bwd(q, k, v, cum, do, qpos0, scale, use_cum, n_free):
    tn = (((0,), (0,)), ((), ()))
    nt = (((1,), (1,)), ((), ()))
    qb, dob = q.astype(BF16), do.astype(BF16)
    kb, vb = k.astype(BF16), v.astype(BF16)
    s = lax.dot_general(qb, kb, nt, preferred_element_type=F32) * scale
    if use_cum:
        s = s - cum
    band = s[:, n_free:]
    visible = _iota(band.shape, 1) <= (qpos0 - n_free) + _iota(band.shape, 0)
    parts = [(jnp.where(visible, band, -jnp.inf), n_free, s.shape[1])]
    if n_free:
        parts.append((s[:, :n_free], 0, n_free))
    m = functools.reduce(jnp.maximum, [jnp.max(sp, axis=-1, keepdims=True) for sp, _, _ in parts])
    es = [jnp.exp(sp - m) for sp, _, _ in parts]
    rinv = 1.0 / sum(jnp.sum(e, axis=-1, keepdims=True) for e in es)
    ps = [e * rinv for e in es]
    dps = [lax.dot_general(dob, vb[a:b], nt, preferred_element_type=F32) for _, a, b in parts]
    delta = sum(jnp.sum(p * dp, axis=-1, keepdims=True) for p, dp in zip(ps, dps))
    dq = jnp.zeros(q.shape, F32)
    out = []
    for p, dp, (_, a, b) in zip(ps, dps, parts):
        ds = p * (dp - delta)
        dsb = ds.astype(BF16)
        dq = dq + jnp.dot(dsb, kb[a:b], preferred_element_type=F32)
        out.append((a, b, lax.dot_general(dsb, qb, tn, preferred_element_type=F32) * scale,
                    lax.dot_general(p.astype(BF16), dob, tn, preferred_element_type=F32),
                    -jnp.sum(ds, axis=0, keepdims=True) if use_cum else None))
    return dq * scale, out


def attn_bwd(qa, qo, ka, ko, va, vo, cum, dcat, do_off, scale, S, out_dtype):
    T = qa.shape[0]
    B = T // S
    tq = min(ATT_TQ, S)
    nq = S // tq
    use_cum = cum is not None

    def body(*refs):
        if use_cum:
            q_ref, k_ref, v_ref, do_ref, c_ref, dq_ref, dk_ref, dv_ref, dc_ref, dk_acc, dv_acc = refs
        else:
            q_ref, k_ref, v_ref, do_ref, dq_ref, dk_ref, dv_ref, dk_acc, dv_acc = refs
        h, i = pl.program_id(1), pl.program_id(2)

        @pl.when(i == 0)
        def _():
            dk_acc[...] = jnp.zeros_like(dk_acc)
            dv_acc[...] = jnp.zeros_like(dv_acc)
            if use_cum:
                dc_ref[...] = jnp.zeros_like(dc_ref)

        for lo, hi, kw in _bands(S, tq):
            @pl.when((i >= lo) & (i < hi))
            def _():
                crow = c_ref[pl.ds(h, 1), 0:kw] if use_cum else None
                dq, pieces = _attn_block_bwd(q_ref[...], k_ref[0:kw, :], v_ref[0:kw, :], crow, do_ref[...], i * tq,
                                             scale, use_cum, lo * tq)
                dq_ref[...] = dq.astype(out_dtype)
                for a, b, dk, dv, dc in pieces:
                    dk_acc[a:b, :] += dk
                    dv_acc[a:b, :] += dv
                    if use_cum:
                        dc_ref[:, a:b] += dc

        @pl.when(i == nq - 1)
        def _():
            dk_ref[...] = dk_acc[...].astype(out_dtype)
            dv_ref[...] = dv_acc[...].astype(out_dtype)

    qspec = pl.BlockSpec((tq, HP), lambda b, h, i: (b * nq + i, qo + h))
    in_specs = [qspec, pl.BlockSpec((S, HP), lambda b, h, i: (b, ko + h)),
                pl.BlockSpec((S, HP), lambda b, h, i: (b, vo + h)),
                pl.BlockSpec((tq, HP), lambda b, h, i: (b * nq + i, do_off + h))]
    args = [qa, ka, va, dcat]
    kv_out = pl.BlockSpec((S, HP), lambda b, h, i: (b, h))
    out_specs = [pl.BlockSpec((tq, HP), lambda b, h, i: (b * nq + i, h)), kv_out, kv_out]
    out_shape = [_sds((T, HEADS * HP), out_dtype)] * 3
    if use_cum:
        in_specs.append(pl.BlockSpec((None, 8, S), lambda b, h, i: (b, 0, 0)))
        args.append(cum)
        out_specs.append(pl.BlockSpec((None, 1, S), lambda b, h, i: (b * HEADS + h, 0, 0)))
        out_shape.append(_sds((B * HEADS, 1, S), F32))
    return pl.pallas_call(
        body, name="attn_bwd", grid=(B, HEADS, nq), in_specs=in_specs, out_specs=out_specs, out_shape=out_shape,
        scratch_shapes=[pltpu.VMEM((S, HP), F32), pltpu.VMEM((S, HP), F32)],
        compiler_params=_cparams(VMEM_BIG))(*args)


def _tri(n, upper):
    r, c = _iota((n, n), 0), _iota((n, n), 1)
    return ((r <= c) if upper else (r >= c)).astype(F32)


def fox_gate_fwd(proj, bcol, S):
    T = proj.shape[0]
    B = T // S
    ts = _tile(S, 512)
    nt = S // ts

    def body(p_ref, b_ref, o_ref, carry):
        @pl.when(pl.program_id(1) == 0)
        def _():
            carry[...] = jnp.zeros_like(carry)
        cf = jnp.transpose(p_ref[...])[0:8, :]
        lf = jax.nn.log_sigmoid(cf + b_ref[...])
        cum = _sel_r(lf, _tri(ts, True)) + carry[...]
        o_ref[...] = cum
        carry[...] += jnp.sum(lf, axis=1, keepdims=True)

    return pl.pallas_call(
        body, name="fox_gate_fwd", grid=(B, nt),
        in_specs=[pl.BlockSpec((ts, HP), lambda b, j: (b * nt + j, C_CF // HP)), pl.BlockSpec((8, 1), lambda b, j: (0, 0))],
        out_specs=pl.BlockSpec((None, 8, ts), lambda b, j: (b, 0, j)), out_shape=_sds((B, 8, S), F32),
        scratch_shapes=[pltpu.VMEM((8, 1), F32)],
        compiler_params=_cparams(VMEM_BIG))(proj, bcol)


def fox_gate_bwd(proj, bcol, dcum, S):
    T = proj.shape[0]
    B = T // S
    ts = _tile(S, 512)
    nt = S // ts

    def body(p_ref, b_ref, dc_ref, dp_ref, db_ref, carry):
        b, j = pl.program_id(0), pl.program_id(1)

        @pl.when(j == 0)
        def _():
            carry[...] = jnp.zeros_like(carry)

        @pl.when((b == 0) & (j == 0))
        def _():
            db_ref[...] = jnp.zeros_like(db_ref)

        cf = jnp.transpose(p_ref[...])[0:8, :]
        dc = dc_ref[...]
        dlf = _sel_r(dc, _tri(ts, False)) + carry[...]
        carry[...] += jnp.sum(dc, axis=1, keepdims=True)
        dcf = dlf * jax.nn.sigmoid(-(cf + b_ref[...]))
        db_ref[...] += jnp.sum(dcf, axis=1, keepdims=True)
        full = jnp.concatenate([dcf, jnp.zeros((HP - 8, ts), F32)], axis=0)
        dp_ref[...] = jnp.transpose(full).astype(BF16)

    def rev(b, j):
        return nt - 1 - j

    return pl.pallas_call(
        body, name="fox_gate_bwd", grid=(B, nt),
        in_specs=[pl.BlockSpec((ts, HP), lambda b, j: (b * nt + rev(b, j), C_CF // HP)),
                  pl.BlockSpec((8, 1), lambda b, j: (0, 0)),
                  pl.BlockSpec((None, 8, ts), lambda b, j: (b, 0, rev(b, j)))],
        out_specs=[pl.BlockSpec((ts, HP), lambda b, j: (b * nt + rev(b, j), 0)), pl.BlockSpec((8, 1), lambda b, j: (0, 0))],
        out_shape=[_sds((T, HP), BF16), _sds((8, 1), F32)],
        scratch_shapes=[pltpu.VMEM((8, 1), F32)],
        compiler_params=_cparams(VMEM_BIG))(proj, bcol, dcum)


def _mla_pre(blk, gq, gkv, wq, wkv, place, cos_q, sin_q, cs_k):
    nq = _rms(blk[:, 0:B_Q_LORA], gq)
    nkv = _rms(blk[:, B_Q_LORA:B_Q_LORA + B_KV_LORA], gkv)
    qq = _mm(nq, wq)
    q = qq[:, 0:HEADS * HP] * cos_q + qq[:, HEADS * HP:] * sin_q
    kv = _mm(nkv, wkv)
    k = kv[:, 0:HEADS * HP] + _mm(blk[:, B_Q_LORA + B_KV_LORA:] * cs_k, place)
    return q, k, kv[:, HEADS * HP:]


def mla_pre_fwd(proj, gq, gkv, wq, wkv, place, cos_q, sin_q, cs_k, S):
    T = proj.shape[0]
    tm = _tile(S, 512)
    tpb = S // tm
    W = HEADS * HP

    def body(p_ref, gq_ref, gkv_ref, wq_ref, wkv_ref, pl_ref, cq_ref, sq_ref, ck_ref, q_ref, k_ref, v_ref):
        q, k, v = _mla_pre(p_ref[...], gq_ref[...], gkv_ref[...], wq_ref[...], wkv_ref[...], pl_ref[...],
                           cq_ref[...], sq_ref[...], ck_ref[...])
        q_ref[...] = q.astype(BF16)
        k_ref[...] = k.astype(BF16)
        v_ref[...] = v.astype(BF16)

    def full(a):
        return pl.BlockSpec(a.shape, lambda i: (0,) * a.ndim)

    tok = pl.BlockSpec((tm, W), lambda i: (i, 0))
    return pl.pallas_call(
        body, name="mla_pre_fwd", grid=(T // tm,),
        in_specs=[pl.BlockSpec((tm, W), lambda i: (i, C_B // W)), full(gq), full(gkv), full(wq), full(wkv), full(place),
                  pl.BlockSpec((tm, W), lambda i: (i % tpb, 0)), pl.BlockSpec((tm, W), lambda i: (i % tpb, 0)),
                  pl.BlockSpec((tm, HP), lambda i: (i % tpb, 0))],
        out_specs=[tok] * 3, out_shape=[_sds((T, W), BF16)] * 3,
        compiler_params=_cparams(VMEM_BIG))(proj, gq, gkv, wq, wkv, place, cos_q, sin_q, cs_k)


def mla_pre_bwd(proj, gq, gkv, wq, wkv, place, cos_q, sin_q, cs_k, dq, dk, dv, S):
    T = proj.shape[0]
    tm = _tile(S, 512)
    tpb = S // tm
    W = HEADS * HP

    def body(p_ref, gq_ref, gkv_ref, wq_ref, wkv_ref, pl_ref, cq_ref, sq_ref, ck_ref, dq_ref, dk_ref, dv_ref,
             dp_ref, dgq_ref, dgkv_ref, dwq_ref, dwkv_ref):
        @pl.when(pl.program_id(0) == 0)
        def _():
            for r in (dgq_ref, dgkv_ref, dwq_ref, dwkv_ref):
                r[...] = jnp.zeros_like(r)

        f = functools.partial(_mla_pre, place=pl_ref[...], cos_q=cq_ref[...], sin_q=sq_ref[...], cs_k=ck_ref[...])
        _, vjp = jax.vjp(f, p_ref[...], gq_ref[...], gkv_ref[...], wq_ref[...], wkv_ref[...])
        dp, dgq, dgkv, dwq, dwkv = vjp((dq_ref[...], dk_ref[...], dv_ref[...]))
        dp_ref[...] = dp.astype(BF16)
        dgq_ref[...] += dgq
        dgkv_ref[...] += dgkv
        dwq_ref[...] += dwq
        dwkv_ref[...] += dwkv

    def full(a):
        return pl.BlockSpec(a.shape, lambda i: (0,) * a.ndim)

    tok = pl.BlockSpec((tm, W), lambda i: (i, 0))
    return pl.pallas_call(
        body, name="mla_pre_bwd", grid=(T // tm,),
        in_specs=[pl.BlockSpec((tm, W), lambda i: (i, C_B // W)), full(gq), full(gkv), full(wq), full(wkv), full(place),
                  pl.BlockSpec((tm, W), lambda i: (i % tpb, 0)), pl.BlockSpec((tm, W), lambda i: (i % tpb, 0)),
                  pl.BlockSpec((tm, HP), lambda i: (i % tpb, 0)), tok, tok, tok],
        out_specs=[tok, full(gq), full(gkv), full(wq), full(wkv)],
        out_shape=[_sds((T, W), BF16), _sds(gq.shape, F32), _sds(gkv.shape, F32), _sds(wq.shape, F32), _sds(wkv.shape, F32)],
        compiler_params=_cparams(VMEM_BIG))(proj, gq, gkv, wq, wkv, place, cos_q, sin_q, cs_k, dq, dk, dv)


GMLP_CHUNKS = 4


def _gmlp_block(blk, lg, lb, ws, bs):
    u = jax.nn.gelu(blk[:, 0:GW])
    v = jax.nn.gelu(blk[:, GW:2 * GW])
    mu = jnp.mean(v, axis=-1, keepdims=True)
    var = jnp.mean(jnp.square(v - mu), axis=-1, keepdims=True)
    vn = (v - mu) * lax.rsqrt(var + LN_EPS) * lg + lb
    causal = _iota((D_CHUNK, D_CHUNK), 1) <= _iota((D_CHUNK, D_CHUNK), 0)
    group = _iota((1, GW), 1) // HD
    w = [jnp.where(causal, ws[g], 0.0) for g in range(HEADS)]
    chunks = []
    for c in range(blk.shape[0] // D_CHUNK):
        vc = vn[c * D_CHUNK:(c + 1) * D_CHUNK]
        mixed = jnp.zeros((D_CHUNK, GW), F32)
        for g in range(HEADS):
            mixed = mixed + jnp.where(group == g, _mm(w[g], vc) + bs[g], 0.0)
        chunks.append(mixed)
    return u * jnp.concatenate(chunks, axis=0)


def _gmlp_tile(T):
    return _tile(T, GMLP_CHUNKS * D_CHUNK) if T % (GMLP_CHUNKS * D_CHUNK) == 0 else D_CHUNK


def gmlp_fwd(proj, lg, lb, ws, bs):
    T = proj.shape[0]
    tm = _gmlp_tile(T)

    def body(p_ref, lg_ref, lb_ref, ws_ref, bs_ref, o_ref):
        o_ref[...] = _gmlp_block(p_ref[...], lg_ref[...], lb_ref[...], ws_ref[...], bs_ref[...]).astype(BF16)

    def full(a):
        return pl.BlockSpec(a.shape, lambda i: (0,) * a.ndim)

    return pl.pallas_call(
        body, name="gmlp_fwd", grid=(T // tm,),
        in_specs=[pl.BlockSpec((tm, 2 * GW), lambda i: (i, C_D // (2 * GW))), full(lg), full(lb), full(ws), full(bs)],
        out_specs=pl.BlockSpec((tm, GW), lambda i: (i, 0)), out_shape=_sds((T, GW), BF16),
        compiler_params=_cparams(VMEM_BIG))(proj, lg, lb, ws, bs)


def gmlp_bwd(proj, lg, lb, ws, bs, dcat):
    T = proj.shape[0]
    tm = _gmlp_tile(T)

    def body(p_ref, lg_ref, lb_ref, ws_ref, bs_ref, do_ref, dp_ref, dlg_ref, dlb_ref, dws_ref, dbs_ref):
        @pl.when(pl.program_id(0) == 0)
        def _():
            for r in (dlg_ref, dlb_ref, dws_ref, dbs_ref):
                r[...] = jnp.zeros_like(r)

        _, vjp = jax.vjp(_gmlp_block, p_ref[...], lg_ref[...], lb_ref[...], ws_ref[...], bs_ref[...])
        dp, dlg, dlb, dws, dbs = vjp(do_ref[...])
        dp_ref[...] = dp.astype(BF16)
        dlg_ref[...] += dlg
        dlb_ref[...] += dlb
        dws_ref[...] += dws
        dbs_ref[...] += dbs

    def full(a):
        return pl.BlockSpec(a.shape, lambda i: (0,) * a.ndim)

    return pl.pallas_call(
        body, name="gmlp_bwd", grid=(T // tm,),
        in_specs=[pl.BlockSpec((tm, 2 * GW), lambda i: (i, C_D // (2 * GW))), full(lg), full(lb), full(ws), full(bs),
                  pl.BlockSpec((tm, GW), lambda i: (i, O_D // GW))],
        out_specs=[pl.BlockSpec((tm, 2 * GW), lambda i: (i, 0)), full(lg), full(lb), full(ws), full(bs)],
        out_shape=[_sds((T, 2 * GW), BF16), _sds(lg.shape, F32), _sds(lb.shape, F32), _sds(ws.shape, F32), _sds(bs.shape, F32)],
        compiler_params=_cparams(VMEM_BIG))(proj, lg, lb, ws, bs, dcat)


def _lb_all(logits):
    m = jnp.max(logits, axis=0, keepdims=True)
    e = jnp.exp(logits - m)
    sm = e / jnp.sum(e, axis=0, keepdims=True)
    return jnp.concatenate([sm[0:1] - sm[0:1], (sm[0:1] + sm[1:2]) - sm[0:1]], axis=0)


def lb_fwd(logits):
    def body(l_ref, o_ref):
        o_ref[...] = _lb_all(l_ref[...])

    return pl.pallas_call(body, name="lb_fwd", out_shape=_sds(logits.shape, F32))(logits)


def lb_bwd(logits, dlb):
    def body(l_ref, d_ref, o_ref):
        _, vjp = jax.vjp(_lb_all, l_ref[...])
        o_ref[...] = vjp(d_ref[...])[0]

    return pl.pallas_call(body, name="lb_bwd", out_shape=_sds(logits.shape, F32))(logits, dlb)


def ada_grad(c_all, dmod_cols):
    N = dmod_cols.shape[1]
    tn = _tile(N, 1152)

    def body(c_ref, d_ref, o_ref):
        h = jax.nn.silu(c_ref[...]).astype(BF16)
        o_ref[...] = lax.dot_general(h, d_ref[...].astype(BF16), (((0,), (0,)), ((), ())), preferred_element_type=F32)

    nb = c_all.shape[0]
    return pl.pallas_call(
        body, name="ada_grad", grid=(N // tn,),
        in_specs=[pl.BlockSpec((nb, D), lambda j: (0, 0)), pl.BlockSpec((nb, tn), lambda j: (0, j))],
        out_specs=pl.BlockSpec((D, tn), lambda j: (0, j)), out_shape=_sds((D, N), F32),
        compiler_params=_cparams(VMEM_BIG))(c_all, dmod_cols)


def sum_slots(a, n, name):
    _, R, C = a.shape
    tr = _row_tile(R, C, n)

    def body(a_ref, o_ref):
        acc = a_ref[0]
        for k in range(1, n):
            acc = acc + a_ref[k]
        o_ref[...] = acc

    return pl.pallas_call(
        body, name=name, grid=(R // tr,),
        in_specs=[pl.BlockSpec((n, tr, C), lambda i: (0, i, 0))],
        out_specs=pl.BlockSpec((tr, C), lambda i: (i, 0)), out_shape=_sds((R, C), F32),
        compiler_params=_cparams(VMEM_BIG))(a)


def _row_tile(R, C=D, n=1, mult=8, elems=1 << 18):
    limit = max(mult, elems // (C * n))
    for t in range(limit - limit % mult, mult - 1, -mult):
        if R % t == 0:
            return t
    return R


def adamw(w, g, m, v, name, echo=False):
    R, C = w.shape
    tr = _row_tile(R, C, elems=1 << 19)
    c1 = 1.0 - ADAM_B1 ** ADAM_STEP
    c2 = 1.0 - ADAM_B2 ** ADAM_STEP
    n_out = 4 if echo else 3

    def body(w_ref, g_ref, m_ref, v_ref, d_ref, nm_ref, nv_ref, *g_out):
        g_ = g_ref[...]
        nm = ADAM_B1 * m_ref[...] + (1.0 - ADAM_B1) * g_
        nv = ADAM_B2 * v_ref[...] + (1.0 - ADAM_B2) * jnp.square(g_)
        d_ref[...] = -ADAM_LR * ((nm / c1) / (jnp.sqrt(nv / c2) + ADAM_EPS) + ADAM_WD * w_ref[...])
        nm_ref[...] = nm
        nv_ref[...] = nv
        if echo:
            g_out[0][...] = g_

    spec = pl.BlockSpec((tr, C), lambda i: (i, 0))
    return pl.pallas_call(body, name=name, grid=(R // tr,), in_specs=[spec] * 4, out_specs=[spec] * n_out,
                          out_shape=[_sds((R, C), F32)] * n_out, compiler_params=_cparams(VMEM_BIG))(w, g, m, v)


def _rot_cols(w):
    return jnp.concatenate([-w[:, 16:32], w[:, 0:16]], axis=1)


def _fold_rot(d):
    return jnp.concatenate([d[:, 16:32], -d[:, 0:16]], axis=1)


def _pad_heads(w, off, axis):
    parts = []
    for h in range(HEADS):
        piece = lax.slice_in_dim(w, off + HD * h, off + HD * (h + 1), axis=axis)
        parts += [piece, jnp.zeros_like(piece)]
    return parts


def _unpad_heads(d, off, axis):
    return [lax.slice_in_dim(d, off + HP * h, off + HP * h + HD, axis=axis) for h in range(HEADS)]


def mix_in_ext(w):
    z = lambda n: jnp.zeros((w.shape[0], n), w.dtype)
    kr = w[:, 1408:1440]
    cols = [w[:, 0:1408], kr, _rot_cols(kr), z(64)]
    cols += _pad_heads(w, 1440, 1) + _pad_heads(w, 1696, 1) + _pad_heads(w, 1952, 1)
    cols += [w[:, 2212:2724], w[:, 2208:2212], z(NP - C_CF - HEADS)]
    return jnp.concatenate(cols, axis=1)


def mix_in_unext(d):
    kr = d[:, 1408:1440] + _fold_rot(d[:, 1440:1472])
    cols = [d[:, 0:1408], kr] + _unpad_heads(d, C_CQ, 1) + _unpad_heads(d, C_CK, 1) + _unpad_heads(d, C_CV, 1)
    cols += [d[:, C_CF:C_CF + HEADS], d[:, C_D:C_D + 2 * GW]]
    return jnp.concatenate(cols, axis=1)


def mix_out_ext(w):
    return jnp.concatenate([w[0:GW]] + _pad_heads(w, GW, 0) + _pad_heads(w, 2 * GW, 0) + [w[3 * GW:4 * GW]], axis=0)


def mix_out_unext(d):
    return jnp.concatenate([d[0:GW]] + _unpad_heads(d, O_B, 0) + _unpad_heads(d, O_C, 0) + [d[O_D:O_D + GW]], axis=0)


def uq_ext(w):
    z = lambda n: jnp.zeros((w.shape[0], n), w.dtype)
    a, b = [], []
    for h in range(HEADS):
        o = (B_NOPE + B_ROPE) * h
        a += [w[:, o:o + B_NOPE + B_ROPE], z(32)]
        b += [z(B_NOPE), _rot_cols(w[:, o + B_NOPE:o + B_NOPE + B_ROPE]), z(32)]
    return jnp.concatenate(a + b, axis=1)


def uq_unext(d):
    cols = []
    for h in range(HEADS):
        o = HP * h
        cols += [d[:, o:o + B_NOPE], d[:, o + B_NOPE:o + B_NOPE + B_ROPE]
                 + _fold_rot(d[:, HEADS * HP + o + B_NOPE:HEADS * HP + o + B_NOPE + B_ROPE])]
    return jnp.concatenate(cols, axis=1)


def ukv_ext(w):
    z = jnp.zeros((w.shape[0], HD), w.dtype)
    k, v = [], []
    for h in range(HEADS):
        k += [w[:, 2 * HD * h:2 * HD * h + HD], z]
        v += [w[:, 2 * HD * h + HD:2 * HD * (h + 1)], z]
    return jnp.concatenate(k + v, axis=1)


def ukv_unext(d):
    cols = []
    for h in range(HEADS):
        cols += [d[:, HP * h:HP * h + HD], d[:, HEADS * HP + HP * h:HEADS * HP + HP * h + HD]]
    return jnp.concatenate(cols, axis=1)


def rope_tables(S):
    half = B_ROPE // 2
    inv_freq = ROPE_THETA ** (-jnp.arange(half, dtype=F32) / half)
    ang = jnp.arange(S).astype(F32)[:, None] * inv_freq[None, :]
    cos = jnp.tile(jnp.cos(ang), (1, 2))
    sin = jnp.tile(jnp.sin(ang), (1, 2))
    one, zero = jnp.ones((S, B_NOPE), F32), jnp.zeros((S, B_NOPE), F32)
    z32 = jnp.zeros((S, 32), F32)
    cos_q = jnp.tile(jnp.concatenate([one, cos, z32], axis=1), (1, HEADS))
    sin_q = jnp.tile(jnp.concatenate([zero, sin, z32], axis=1), (1, HEADS))
    cs_k = jnp.concatenate([cos, sin, zero], axis=1)
    place = np.zeros((HP, HEADS * HP), np.float32)
    for h in range(HEADS):
        for j in range(B_ROPE):
            place[j, h * HP + B_NOPE + j] = 1.0
            place[B_ROPE + j, h * HP + B_NOPE + j] = 1.0
    return cos_q, sin_q, cs_k, jnp.asarray(place, BF16)


def layer_fwd(x, mod, get, tabs, S):
    cos_q, sin_q, cs_k, place = tabs
    p = dict(get("ffn1", x))
    l = p["wl"]
    zg1, zu1, act1 = ffn_in_fwd(x, mod, p["ffn1_in"], l, 0, 1, S)
    y1, x1 = out_ln_fwd(act1, p["ffn1_out"], x, mod, p["ln_g"][0:1], p["ln_b"][0:1], 2, 0.5, S, l)
    p.update(get("mix", x1))
    proj = mix_in_fwd(x1, mod, p["mix_in"], 3, 4, S)
    o_a, states = hgrn_fwd(proj, p["lb"], p["ng"], S)
    q_b, k_b, v_b = mla_pre_fwd(proj, p["gq"], p["gkv"], p["wq"], p["wkv"], place, cos_q, sin_q, cs_k, S)
    o_b = attn_fwd(q_b, 0, k_b, 0, v_b, 0, None, (B_NOPE + B_ROPE) ** -0.5, S)
    cum = fox_gate_fwd(proj, p["bcol"], S)
    o_c = attn_fwd(proj, C_CQ // HP, proj, C_CK // HP, proj, C_CV // HP, cum, HD ** -0.5, S)
    o_d = gmlp_fwd(proj, p["g_lg"], p["g_lb"], p["ws"], p["bs"])
    cat = jnp.concatenate([o_a, o_b, o_c, o_d], axis=1)
    y2, x2 = out_ln_fwd(cat, p["mix_out"], x1, mod, p["ln_g"][1:2], p["ln_b"][1:2], 5, 1.0, S)
    p.update(get("ffn2", x2))
    zg3, zu3, act3 = ffn_in_fwd(x2, mod, p["ffn2_in"], l, 6, 7, S)
    y3, x3 = out_ln_fwd(act3, p["ffn2_out"], x2, mod, p["ln_g"][2:3], p["ln_b"][2:3], 8, 0.5, S, l)
    saved = dict(x=x, zg1=zg1, zu1=zu1, act1=act1, y1=y1, x1=x1, proj=proj, states=states, q_b=q_b, k_b=k_b, v_b=v_b,
                 cum=cum, cat=cat, y2=y2, x2=x2, zg3=zg3, zu3=zu3, act3=act3, y3=y3, p=p)
    return x3, saved


def _ffn_bwd(dxn, x_in, y, zg, zu, act, mod, w_in, w_out, l, lg, lb, idx, S, emit):
    sh, sc, gate = idx
    dres, dy, dgate, dlg, dlb = ln_res_bwd(dxn, x_in, y, mod, lg, lb, gate, 0.5, S)
    dzg, dzu = swiglu_bwd(dy, w_out, l, zg, zu, S)
    dw_out = tn_mm(act, dy, FF // 2).reshape(N_CHIPS, FF // N_CHIPS, D)
    dw_in = tn_mm_mod_shards(x_in, mod, dzg, dzu, sh, sc, S)
    mod = mod + emit(dw_in, dw_out)
    dx, dsh, dsc = nt_mod_bwd([dzg, dzu], w_in, [0, 2], x_in, mod, dres, sc, S, FF // 2, l)
    return dx, dw_in, dw_out, dlg, dlb, {sh: dsh, sc: dsc, gate: dgate}, mod


def layer_bwd(dx3, mod, sv, tabs, S, emit):
    cos_q, sin_q, cs_k, place = tabs
    p = sv["p"]
    l = p["wl"]
    g = {}
    dm = {}

    def emit_ffn(part):
        def f(dw_in, dw_out):
            g[part + "_in"], g[part + "_out"] = dw_in, dw_out
            return emit(part, g)
        return f

    dx2, _, _, dlg2, dlb2, d, mod = _ffn_bwd(
        dx3, sv["x2"], sv["y3"], sv["zg3"], sv["zu3"], sv["act3"], mod, p["ffn2_in"], p["ffn2_out"], l,
        p["ln_g"][2:3], p["ln_b"][2:3], (6, 7, 8), S, emit_ffn("ffn2"))
    dm.update(d)
    dres, dy2, dm[5], dlg1, dlb1 = ln_res_bwd(dx2, sv["x1"], sv["y2"], mod, p["ln_g"][1:2], p["ln_b"][1:2], 5, 1.0, S)
    dcat = nt_plain(dy2, p["mix_out"])
    g["mix_out"] = tn_mm(sv["cat"], dy2, NCAT // 2)
    proj = sv["proj"]
    d_a, g["lb"], g["ng"] = hgrn_bwd(proj, sv["states"], dcat, p["lb"], p["ng"], S)
    dq_c, dk_c, dv_c, dcum = attn_bwd(proj, C_CQ // HP, proj, C_CK // HP, proj, C_CV // HP, sv["cum"], dcat,
                                      O_C // HP, HD ** -0.5, S, BF16)
    B = proj.shape[0] // S
    dcum = jnp.concatenate([dcum.reshape(B, HEADS, S), jnp.zeros((B, 8 - HEADS, S), F32)], axis=1)
    d_cf, g["bcol"] = fox_gate_bwd(proj, p["bcol"], dcum, S)
    dq_b, dk_b, dv_b = attn_bwd(sv["q_b"], 0, sv["k_b"], 0, sv["v_b"], 0, None, dcat, O_B // HP,
                                (B_NOPE + B_ROPE) ** -0.5, S, F32)
    d_b, g["gq"], g["gkv"], g["wq"], g["wkv"] = mla_pre_bwd(
        proj, p["gq"], p["gkv"], p["wq"], p["wkv"], place, cos_q, sin_q, cs_k, dq_b, dk_b, dv_b, S)
    d_d, g["g_lg"], g["g_lb"], g["ws"], g["bs"] = gmlp_bwd(proj, p["g_lg"], p["g_lb"], p["ws"], p["bs"], dcat)
    dproj = jnp.concatenate([d_a, d_b, dq_c, dk_c, dv_c, d_d, d_cf, jnp.zeros_like(d_cf)], axis=1)
    g["mix_in"] = tn_mm_mod(sv["x1"], mod, dproj, 3, 4, S, NP_TILE)
    mod = mod + emit("mix", g)
    dx1, dm[3], dm[4] = nt_mod_bwd([dproj], p["mix_in"], [0], sv["x1"], mod, dres, 4, S, NP_TILE)
    last = []

    def emit_last(dw_in, dw_out):
        last.append(emit_ffn("ffn1")(dw_in, dw_out))
        return last[0]

    dx0, _, _, dlg0, dlb0, d, mod = _ffn_bwd(
        dx1, sv["x"], sv["y1"], sv["zg1"], sv["zu1"], sv["act1"], mod, p["ffn1_in"], p["ffn1_out"], l,
        p["ln_g"][0:1], p["ln_b"][0:1], (0, 1, 2), S, emit_last)
    dm.update(d)
    g["ln_g"] = jnp.concatenate([dlg0, dlg1, dlg2], axis=0)
    g["ln_b"] = jnp.concatenate([dlb0, dlb1, dlb2], axis=0)
    dmod = jnp.concatenate([dm[i] for i in range(N_MOD)], axis=1)
    return dx0, dmod, g, last[0]


def local_step(x, c8, tgt, get, lb_logits, S, emit=None):
    B = x.shape[0] // S
    tabs = rope_tables(S)
    lb_all = lb_fwd(lb_logits)
    mods, saved = [], []
    h = x
    for l in range(DEPTH):
        pa = get(l, "ada", h)
        mod = mod_fwd(c8, pa["ada_w"], pa["wl"], pa["ada_b"])[0:B].reshape(B, N_MOD, D)

        def get_l(part, after, l=l):
            p = dict(get(l, part, after))
            if part == "mix":
                p["lb"] = lb_all[l:l + 1]
            return p

        h, sv = layer_fwd(h, mod, get_l, tabs, S)
        mods.append(mod)
        saved.append(sv)
    loss_tile, dh = loss_head(h, tgt)
    grads, dmods, dlb = [None] * DEPTH, [None] * DEPTH, [None] * DEPTH
    tie = jnp.zeros((), F32)
    for l in reversed(range(DEPTH)):
        emit_l = (lambda part, g: jnp.zeros((), F32)) if emit is None else functools.partial(emit, l)
        dh, dmods[l], grads[l], tie = layer_bwd(dh, mods[l] + tie, saved[l], tabs, S, emit_l)
        dlb[l] = grads[l].pop("lb")
    d_logits = lb_bwd(lb_logits, jnp.concatenate(dlb, axis=0))
    return loss_tile, dh, dmods, grads, d_logits


ANY = pl.BlockSpec(memory_space=pl.ANY)


def _place():
    x, y, c = lax.axis_index("x"), lax.axis_index("y"), lax.axis_index("c")
    chips = [(1 - x, y), (x, 1 - y), (1 - x, 1 - y)]
    return x, y, c, chips


def _rcopy(src, dst, sems, k, to):
    send_sems, recv_sems = sems
    return pltpu.make_async_remote_copy(src_ref=src, dst_ref=dst, send_sem=send_sems.at[k], recv_sem=recv_sems.at[k],
                                        device_id=to, device_id_type=MESH)


def _dma_sems(n_remote, n_local):
    return [pltpu.SemaphoreType.DMA((n_remote,)), pltpu.SemaphoreType.DMA((n_remote,)), pltpu.SemaphoreType.DMA((n_local,))]


def own_slot(src, chip):
    L = src.shape[0]
    return lax.dynamic_update_slice(lax.empty((L, N_CHIPS) + src.shape[1:], src.dtype), src[:, None], (0, chip, 0, 0))


HBM_SPEC = pl.BlockSpec(memory_space=pltpu.HBM)
SEM_SPEC = pl.BlockSpec(memory_space=pltpu.SEMAPHORE)
DATAFLOW = pltpu.SideEffectType.DATAFLOW_SIDE_EFFECTING


def _split_start(srcs, lands, copies, n_copies, dep, name):
    n, m = len(srcs), len(lands)

    def body(*refs):
        ins = refs[:n + m]
        send_sems, recv_sems = refs[n + m + 1], refs[n + m + 2]
        token = refs[-1]
        for k, (src, dst, to) in enumerate(copies(ins[:n], ins[n:], _place())):
            pltpu.make_async_remote_copy(src_ref=src, dst_ref=dst, send_sem=send_sems.at[k], recv_sem=recv_sems.at[k],
                                         device_id=to, device_id_type=MESH).start()
        token[...] = jnp.zeros_like(token)

    arrs = list(srcs) + list(lands)
    outs = pl.pallas_call(
        body, name=name,
        out_shape=(pltpu.SemaphoreType.DMA((n_copies,)), pltpu.SemaphoreType.DMA((n_copies,)),
                   *[pltpu.HBM(a.shape, a.dtype) for a in arrs], _sds((8, 128), F32)),
        in_specs=[HBM_SPEC] * (n + m) + [ANY],
        out_specs=(SEM_SPEC, SEM_SPEC, *[HBM_SPEC] * (n + m), pl.BlockSpec(memory_space=pltpu.VMEM)),
        input_output_aliases={i: 2 + i for i in range(n + m)},
        compiler_params=pltpu.CompilerParams(has_side_effects=DATAFLOW),
    )(*[pltpu.with_memory_space_constraint(a, pltpu.HBM) for a in arrs], dep)
    return outs[0], outs[1], list(outs[2:2 + n]), list(outs[2 + n:2 + n + m]), outs[-1]


def _split_wait(handle, arrivals, after, name):
    send_sems, recv_sems, srcs, lands, _ = handle
    n, m = len(srcs), len(lands)

    def body(*refs):
        ins = refs[:n + m]
        send_sems, recv_sems = refs[n + m], refs[n + m + 1]
        x, y, c, chips = place = _place()
        for k, (src, dst) in enumerate(arrivals(ins[:n], ins[n:], place)):
            cp = pltpu.make_async_remote_copy(src_ref=src, dst_ref=dst, send_sem=send_sems.at[k], recv_sem=recv_sems.at[k],
                                              device_id=(x, y, 1 - c), device_id_type=MESH)
            cp.wait_send()
            cp.wait_recv()

    arrs = list(srcs) + list(lands)
    outs = pl.pallas_call(
        body, name=name, out_shape=[pltpu.HBM(a.shape, a.dtype) for a in arrs],
        in_specs=[HBM_SPEC] * (n + m) + [SEM_SPEC, SEM_SPEC, ANY], out_specs=[HBM_SPEC] * (n + m),
        input_output_aliases={i: i for i in range(n + m)},
        compiler_params=pltpu.CompilerParams(has_side_effects=DATAFLOW),
    )(*arrs, send_sems, recv_sems, after)
    return list(outs[:n]), list(outs[n:])


def _ag_part(ref, k, hc):
    rh = ref.shape[2] // 2
    return ref.at[:, k, pl.ds(hc * rh, rh), :]


def ag_start(srcs, lands, dep, name):
    def copies(s, d, place):
        x, y, c, chips = place
        out = []
        for j, (px, py) in enumerate(chips):
            for i in range(len(s)):
                rh = s[i].shape[1] // 2
                out.append((s[i].at[:, pl.ds(c * rh, rh), :], _ag_part(d[i], 2 * x + y, c), (px, py, c)))
        return out

    return _split_start(srcs, lands, copies, 3 * len(srcs), dep, name)


def ag_wait(handle, after, name):
    def arrivals(s, d, place):
        x, y, c, chips = place
        out = []
        for j, (px, py) in enumerate(chips):
            for i in range(len(s)):
                rh = s[i].shape[1] // 2
                out.append((s[i].at[:, pl.ds(c * rh, rh), :], _ag_part(d[i], 2 * px + py, c)))
        return out

    return _split_wait(handle, arrivals, after, name)


def ag_forward(lands, name):
    n = len(lands)

    def body(*refs):
        bufs, token = refs[n:2 * n], refs[2 * n]
        send_sems, recv_sems = refs[2 * n + 1:]
        x, y, c, chips = _place()
        sems = (send_sems, recv_sems)
        token[...] = jnp.zeros_like(token)
        cps = []
        for j, (px, py) in enumerate(chips):
            for i in range(n):
                part = _ag_part(bufs[i], 2 * px + py, c)
                cps.append(_rcopy(part, part, sems, 3 * i + j, (x, y, 1 - c)))
        for cp in cps:
            cp.start()
        for j, (px, py) in enumerate(chips):
            for i in range(n):
                part = _ag_part(bufs[i], 2 * px + py, 1 - c)
                _rcopy(part, part, sems, 3 * i + j, (x, y, 1 - c)).wait_recv()
        for cp in cps:
            cp.wait_send()

    outs = pl.pallas_call(
        body, name=name, out_shape=[_sds(a.shape, a.dtype) for a in lands] + [_sds((8, 128), F32)],
        in_specs=[ANY] * n, out_specs=[ANY] * n + [pl.BlockSpec(memory_space=pltpu.VMEM)],
        input_output_aliases={i: i for i in range(n)}, scratch_shapes=_dma_sems(3 * n, 1)[:2])(*lands)
    return list(outs[:n]), outs[n]


def rs_start(hs, lands, dep, name):
    def copies(s, d, place):
        x, y, c, chips = place
        return [(s[i].at[2 * px + py], d[i].at[2 * x + y], (px, py, c)) for j, (px, py) in enumerate(chips) for i in range(len(s))]

    return _split_start(hs, lands, copies, 3 * len(hs), dep, name)


def rs_wait(handle, after, name):
    def arrivals(s, d, place):
        x, y, c, chips = place
        return [(s[i].at[2 * px + py], d[i].at[2 * px + py]) for j, (px, py) in enumerate(chips) for i in range(len(s))]

    return _split_wait(handle, arrivals, after, name)


def sibling_swap(arrs, name):
    n = len(arrs)
    rh = [a.shape[1] // 2 for a in arrs]

    def body(*refs):
        srcs, outs = refs[:n], refs[n:2 * n]
        send_sems, recv_sems = refs[2 * n:]
        x, y, c, _ = _place()
        cps = [_rcopy(srcs[i].at[:, pl.ds((1 - c) * rh[i], rh[i]), :], outs[i], (send_sems, recv_sems), i, (x, y, 1 - c))
               for i in range(n)]
        for cp in cps:
            cp.start()
        for cp in cps:
            cp.wait()

    return pl.pallas_call(
        body, name=name, out_shape=[_sds((N_CHIPS, r, a.shape[2]), a.dtype) for a, r in zip(arrs, rh)],
        in_specs=[ANY] * n, out_specs=[ANY] * n, scratch_shapes=_dma_sems(n, 1)[:2])(*arrs)


def sum_into(land, base, l, core, name):
    _, rh, C = land.shape
    tr = _row_tile(rh, C, N_CHIPS, mult=16, elems=1 << 21)
    nr = rh // tr

    def body(core_ref, land_ref, base_ref, o_ref):
        acc = land_ref[0].astype(F32)
        for k in range(1, N_CHIPS):
            acc = acc + land_ref[k].astype(F32)
        o_ref[...] = acc

    grid_spec = pltpu.PrefetchScalarGridSpec(
        num_scalar_prefetch=1, grid=(nr,),
        in_specs=[pl.BlockSpec((N_CHIPS, tr, C), lambda r, core_ref: (0, r, 0)), ANY],
        out_specs=pl.BlockSpec((None, tr, C), lambda r, core_ref: (l, core_ref[0] * nr + r, 0)))
    return pl.pallas_call(body, name=name, grid_spec=grid_spec, out_shape=_sds(base.shape, base.dtype),
                          input_output_aliases={2: 0}, compiler_params=_cparams(VMEM_BIG))(
        core.reshape(1).astype(jnp.int32), land, base)


def sibling_join(bases, name):
    n = len(bases)

    def body(*refs):
        bufs = refs[n:2 * n]
        send_sems, recv_sems = refs[2 * n:]
        x, y, c, _ = _place()
        sems = (send_sems, recv_sems)

        def half(i, hc):
            rh = bufs[i].shape[1] // 2
            return bufs[i].at[:, pl.ds(hc * rh, rh), :]

        sends = [_rcopy(half(i, c), half(i, c), sems, i, (x, y, 1 - c)) for i in range(n)]
        for cp in sends:
            cp.start()
        for i in range(n):
            _rcopy(half(i, 1 - c), half(i, 1 - c), sems, i, (x, y, 1 - c)).wait_recv()
        for cp in sends:
            cp.wait_send()

    return pl.pallas_call(
        body, name=name, out_shape=[_sds(b.shape, b.dtype) for b in bases], in_specs=[ANY] * n, out_specs=[ANY] * n,
        input_output_aliases={i: i for i in range(n)}, scratch_shapes=_dma_sems(n, 1)[:2])(*bases)


def ag_all(blk):
    M, C = blk.shape

    def body(x_ref, out_ref, send_sems, recv_sems, loc_sem):
        x, y, c, chips = _place()
        sems = (send_sems, recv_sems)
        me, sibling = (x, y, c), (x, y, 1 - c)

        def slot(px, py, pc):
            return out_ref.at[4 * px + 2 * py + pc]

        mine = pltpu.make_async_copy(x_ref, slot(*me), loc_sem)
        mine.start()
        first = [_rcopy(x_ref, slot(*me), sems, 0, sibling)]
        first += [_rcopy(x_ref, slot(*me), sems, 1 + j, (*chip, c)) for j, chip in enumerate(chips)]
        for cp in first:
            cp.start()
        passed = [_rcopy(slot(*chip, c), slot(*chip, c), sems, 4 + j, sibling) for j, chip in enumerate(chips)]
        for j, chip in enumerate(chips):
            _rcopy(slot(*chip, c), slot(*chip, c), sems, 1 + j, me).wait_recv()
            passed[j].start()
        _rcopy(slot(*sibling), slot(*sibling), sems, 0, me).wait_recv()
        for j, chip in enumerate(chips):
            _rcopy(slot(*chip, 1 - c), slot(*chip, 1 - c), sems, 4 + j, me).wait_recv()
        for cp in first + passed:
            cp.wait_send()
        mine.wait()

    return pl.pallas_call(
        body, name="ag_all", out_shape=_sds((8, M, C), blk.dtype),
        in_specs=[pl.BlockSpec(memory_space=pltpu.VMEM)], out_specs=pl.BlockSpec(memory_space=pltpu.VMEM),
        scratch_shapes=[pltpu.SemaphoreType.DMA((7,)), pltpu.SemaphoreType.DMA((7,)), pltpu.SemaphoreType.DMA(())],
        compiler_params=_cparams(VMEM_BIG))(blk)


WEIGHTS = ["ada_w", "ada_b", "ln_g", "ln_b", "ffn1_w_in", "ffn1_w_out", "ffn2_w_in", "ffn2_w_out", "mix_w_in", "mix_w_out",
           "hgrn_lb_logits", "hgrn_norm_g", "mla_q_norm_g", "mla_kv_norm_g", "mla_w_uq", "mla_w_ukv", "fox_b_f",
           "gmlp_ln_g", "gmlp_ln_b", "gmlp_w_s", "gmlp_b_s"]
SMALL = ["hgrn_lb_logits", "hgrn_norm_g", "mla_q_norm_g", "mla_kv_norm_g", "fox_b_f", "gmlp_ln_g", "gmlp_ln_b",
         "gmlp_w_s", "gmlp_b_s", "ln_g", "ln_b"]
GATHERED = ["ada_w", "ffn1_w_in", "ffn1_w_out", "ffn2_w_in", "ffn2_w_out", "mix_w_in", "mix_w_out", "mla_w_uq", "mla_w_ukv"]
REDUCED = GATHERED[1:]


def _col_shards(a):
    cols = a.shape[1] // N_CHIPS
    return jnp.stack([a[:, k * cols:(k + 1) * cols] for k in range(N_CHIPS)])


def add_kept_half(a, got, core, name):
    _, R, C = a.shape
    rh = R // 2
    tr = _row_tile(rh, C, mult=16, elems=1 << 20)
    nr = rh // tr

    def body(core_ref, a_ref, b_ref, o_ref):
        o_ref[...] = (a_ref[...].astype(F32) + b_ref[...].astype(F32)).astype(o_ref.dtype)

    half = pl.BlockSpec((None, tr, C), lambda k, r, core_ref: (k, r, 0))
    grid_spec = pltpu.PrefetchScalarGridSpec(
        num_scalar_prefetch=1, grid=(N_CHIPS, nr),
        in_specs=[pl.BlockSpec((None, tr, C), lambda k, r, core_ref: (k, core_ref[0] * nr + r, 0)), half],
        out_specs=half)
    return pl.pallas_call(body, name=name, grid_spec=grid_spec, out_shape=_sds((N_CHIPS, rh, C), BF16),
                          compiler_params=_cparams(VMEM_BIG))(core.reshape(1).astype(jnp.int32), a, got)


def _rows(parts, n_rows, dtype):
    flat = jnp.concatenate([p.reshape(-1) for p in parts])
    pad = n_rows * D - flat.shape[0]
    return jnp.concatenate([flat, jnp.zeros((pad,), dtype)]).reshape(n_rows, D)


def _take(flat, shapes):
    out, o = [], 0
    for shp in shapes:
        n = int(np.prod(shp))
        out.append(flat[o:o + n].reshape(shp))
        o += n
    return out


def _round_up(n, m):
    return -(-n // m) * m


def pack_small(w):
    parts = [w[n][l] for l in range(DEPTH) for n in SMALL]
    n = sum(int(np.prod(p.shape)) for p in parts)
    return _rows(parts, _round_up(-(-n // D), 8), F32)


def unpack_small(pk, like):
    shapes = [like[n].shape[1:] for l in range(DEPTH) for n in SMALL]
    pieces = _take(pk.reshape(-1), shapes)
    names = [n for l in range(DEPTH) for n in SMALL]
    return {n: jnp.stack([p for p, m in zip(pieces, names) if m == n]) for n in SMALL}


def kernel(x, c, ada_w, ada_b, ln_g, ln_b, ffn1_w_in, ffn1_w_out, ffn2_w_in, ffn2_w_out, mix_w_in, mix_w_out, hgrn_lb_logits, hgrn_norm_g, mla_q_norm_g, mla_kv_norm_g, mla_w_uq, mla_w_ukv, fox_b_f, gmlp_ln_g, gmlp_ln_b, gmlp_w_s, gmlp_b_s, loss_target, m_ada_w, m_ada_b, m_ln_g, m_ln_b, m_ffn1_w_in, m_ffn1_w_out, m_ffn2_w_in, m_ffn2_w_out, m_mix_w_in, m_mix_w_out, m_hgrn_lb_logits, m_hgrn_norm_g, m_mla_q_norm_g, m_mla_kv_norm_g, m_mla_w_uq, m_mla_w_ukv, m_fox_b_f, m_gmlp_ln_g, m_gmlp_ln_b, m_gmlp_w_s, m_gmlp_b_s, v_ada_w, v_ada_b, v_ln_g, v_ln_b, v_ffn1_w_in, v_ffn1_w_out, v_ffn2_w_in, v_ffn2_w_out, v_mix_w_in, v_mix_w_out, v_hgrn_lb_logits, v_hgrn_norm_g, v_mla_q_norm_g, v_mla_kv_norm_g, v_mla_w_uq, v_mla_w_ukv, v_fox_b_f, v_gmlp_ln_g, v_gmlp_ln_b, v_gmlp_w_s, v_gmlp_b_s):
    w = dict(zip(WEIGHTS, (ada_w, ada_b, ln_g, ln_b, ffn1_w_in, ffn1_w_out, ffn2_w_in, ffn2_w_out, mix_w_in, mix_w_out, hgrn_lb_logits, hgrn_norm_g, mla_q_norm_g, mla_kv_norm_g, mla_w_uq, mla_w_ukv, fox_b_f, gmlp_ln_g, gmlp_ln_b, gmlp_w_s, gmlp_b_s)))
    m = dict(zip(WEIGHTS, (m_ada_w, m_ada_b, m_ln_g, m_ln_b, m_ffn1_w_in, m_ffn1_w_out, m_ffn2_w_in, m_ffn2_w_out, m_mix_w_in, m_mix_w_out, m_hgrn_lb_logits, m_hgrn_norm_g, m_mla_q_norm_g, m_mla_kv_norm_g, m_mla_w_uq, m_mla_w_ukv, m_fox_b_f, m_gmlp_ln_g, m_gmlp_ln_b, m_gmlp_w_s, m_gmlp_b_s)))
    v = dict(zip(WEIGHTS, (v_ada_w, v_ada_b, v_ln_g, v_ln_b, v_ffn1_w_in, v_ffn1_w_out, v_ffn2_w_in, v_ffn2_w_out, v_mix_w_in, v_mix_w_out, v_hgrn_lb_logits, v_hgrn_norm_g, v_mla_q_norm_g, v_mla_kv_norm_g, v_mla_w_uq, v_mla_w_ukv, v_fox_b_f, v_gmlp_ln_g, v_gmlp_ln_b, v_gmlp_w_s, v_gmlp_b_s)))
    Bl, S, _ = x.shape
    T = Bl * S
    core = lax.axis_index("c")
    chip = 2 * lax.axis_index("x") + lax.axis_index("y")

    def shard(key):
        n, l = key
        if n == "ln":
            return jnp.concatenate([ln_g[l:l + 1], ln_b[l:l + 1], jnp.zeros((1, 2, D // N_CHIPS), F32)], axis=1)
        return w[n][l:l + 1].astype(BF16)

    mixers = ["mix_w_in", "mix_w_out", "mla_w_uq", "mla_w_ukv"]
    groups = [[("ada_w", 0), ("ffn1_w_in", 0), ("ffn1_w_out", 0), ("ln", 0)],
              [(n, 0) for n in mixers + ["ffn2_w_in", "ffn2_w_out"]],
              [(n, 1) for n in GATHERED + ["ln"]]]
    srcs = [[shard(k) for k in grp] for grp in groups]
    lands = [[own_slot(s, chip) for s in srcs[0]]]
    handle0 = ag_start(srcs[0], lands[0], jnp.zeros((8, 128), F32), "ag_start_0")
    chip_later = chip + handle0[-1][0, 0].astype(jnp.int32)
    lands += [[own_slot(s, chip_later) for s in grp] for grp in srcs[1:]]
    first, token = ag_forward(ag_wait(handle0, lands[2][0], "ag_wait_0")[1], "ag_forward_0")
    have = dict(zip(groups[0], first))
    handles = {}
    for gi in (1, 2):
        handles[gi] = ag_start(srcs[gi], lands[gi], token, "ag_start_%d" % gi)
        token = handles[gi][-1]
    c8 = jnp.concatenate([c, jnp.zeros((8 - Bl, D), F32)], axis=0)
    c8 = c8 + token[0, 0]

    def cat_cols(a):
        return jnp.concatenate([a[0, k] for k in range(N_CHIPS)], axis=1)

    def get(l, part, after):
        gi = 2 if l == 1 else (0 if part in ("ada", "ffn1") else 1)
        if gi in handles:
            arrived, _ = ag_forward(ag_wait(handles.pop(gi), after, "ag_wait_%d" % gi)[1], "ag_forward_%d" % gi)
            have.update(zip(groups[gi], arrived))
        if part == "ada":
            return dict(ada_w=have[("ada_w", l)], wl=0, ada_b=ada_b[l][None])
        if part == "ffn1":
            ln_full = jnp.moveaxis(have[("ln", l)][0], 0, 1).reshape(8, D)
            return dict(ffn1_in=have[("ffn1_w_in", l)], ffn1_out=have[("ffn1_w_out", l)], wl=0,
                        ln_g=ln_full[0:3], ln_b=ln_full[3:6])
        if part == "ffn2":
            return dict(ffn2_in=have[("ffn2_w_in", l)], ffn2_out=have[("ffn2_w_out", l)])
        return dict(
            mix_in=mix_in_ext(cat_cols(have[("mix_w_in", l)])), mix_out=mix_out_ext(have[("mix_w_out", l)].reshape(D, D)),
            wq=uq_ext(cat_cols(have[("mla_w_uq", l)])).astype(F32), wkv=ukv_ext(cat_cols(have[("mla_w_ukv", l)])).astype(F32),
            ng=hgrn_norm_g[l][None], gq=mla_q_norm_g[l][None], gkv=mla_kv_norm_g[l][None],
            bcol=jnp.concatenate([fox_b_f[l], jnp.zeros((8 - HEADS,), F32)])[:, None],
            g_lg=gmlp_ln_g[l][None], g_lb=gmlp_ln_b[l][None], ws=gmlp_w_s[l], bs=gmlp_b_s[l][:, :, None])

    pending = []

    def emit(l, part, g):
        if part == "mix":
            names = mixers
            by_chip = [_col_shards(mix_in_unext(g["mix_in"])), mix_out_unext(g["mix_out"]).reshape(N_CHIPS, D // N_CHIPS, D),
                       _col_shards(uq_unext(g["wq"])).astype(BF16), _col_shards(ukv_unext(g["wkv"])).astype(BF16)]
        else:
            names = [part + "_w_in", part + "_w_out"]
            by_chip = [g[part + "_in"], g[part + "_out"]]
        tag = "%d_%s" % (l, part)
        got = sibling_swap(by_chip, "sibling_swap_" + tag)
        chip_sum = [add_kept_half(a, r, core, "add_sibling") for a, r in zip(by_chip, got)]
        zones = [lax.dynamic_update_slice(lax.empty(h.shape, h.dtype), lax.dynamic_slice_in_dim(h, chip, 1, axis=0), (chip, 0, 0))
                 for h in chip_sum]
        handle = rs_start(chip_sum, zones, chip_sum[0], "rs_start_" + tag)
        pending.append((l, names, handle, tag))
        return handle[-1][0, 0]

    loss_tile, dx, dmods, grads, d_logits = local_step(
        x.reshape(T, D), c8, loss_target.reshape(T, D), get, hgrn_lb_logits, S, emit)
    loss = lax.psum(loss_tile[0, 0], ("x", "y", "c"))

    small_g = {"hgrn_lb_logits": d_logits,
               "hgrn_norm_g": jnp.stack([grads[l]["ng"][0] for l in range(DEPTH)]),
               "mla_q_norm_g": jnp.stack([grads[l]["gq"][0] for l in range(DEPTH)]),
               "mla_kv_norm_g": jnp.stack([grads[l]["gkv"][0] for l in range(DEPTH)]),
               "fox_b_f": jnp.stack([grads[l]["bcol"][0:HEADS, 0] for l in range(DEPTH)]),
               "gmlp_ln_g": jnp.stack([grads[l]["g_lg"][0] for l in range(DEPTH)]),
               "gmlp_ln_b": jnp.stack([grads[l]["g_lb"][0] for l in range(DEPTH)]),
               "gmlp_w_s": jnp.stack([grads[l]["ws"] for l in range(DEPTH)]),
               "gmlp_b_s": jnp.stack([grads[l]["bs"][:, :, 0] for l in range(DEPTH)])}
    small_g["ln_g"] = jnp.stack([grads[l]["ln_g"] for l in range(DEPTH)])
    small_g["ln_b"] = jnp.stack([grads[l]["ln_b"] for l in range(DEPTH)])
    pk_small = pack_small(small_g)
    n_small = pk_small.shape[0]
    extras = [dmods[l] for l in range(DEPTH)] + [c]
    n_extra = _round_up(-(-sum(int(np.prod(e.shape)) for e in extras) // D), 8)
    gathered = ag_all(jnp.concatenate([pk_small, _rows(extras, n_extra, F32)], axis=0))
    g_small = unpack_small(sum_slots(gathered[:, 0:n_small], 8, "sum_small"), small_g)
    ext = gathered[:, n_small:].reshape(8, -1)
    n_dmod = DEPTH * Bl * N_MOD * D
    dmod_all = ext[:, 0:n_dmod].reshape(8, DEPTH, Bl, N_MOD * D)
    c_all = ext[:, n_dmod:n_dmod + Bl * D].reshape(8 * Bl, D)
    g_ada_w, g_ada_b = [], []
    ncol = N_MOD * D // N_CHIPS
    for l in range(DEPTH):
        dm = dmod_all[:, l].reshape(8 * Bl, N_MOD * D)
        g_ada_w.append(ada_grad(c_all, lax.dynamic_slice_in_dim(dm, chip * ncol, ncol, axis=1)))
        g_ada_b.append(sum_slots(dm.reshape(8 * Bl, N_MOD, D), 8 * Bl, "sum_ada_b").reshape(N_MOD * D))
    g_ada_w, g_ada_b = jnp.stack(g_ada_w), jnp.stack(g_ada_b)

    red = {n: lax.empty(w[n].shape, F32) for n in REDUCED}

    def arrive(entry, after):
        l, names, handle, tag = entry
        for n, land in zip(names, rs_wait(handle, after, "rs_wait_" + tag)[1]):
            red[n] = sum_into(land, red[n], l, core, "sum_chips")

    for entry in pending[:-1]:
        arrive(entry, dx)
    late = pending[-1][1]
    early = [n for n in REDUCED if n not in late]
    grad = dict(zip(early, sibling_join([red[n] for n in early], "sibling_join_a")))
    grad.update(g_small)
    grad["ada_w"], grad["ada_b"] = g_ada_w, g_ada_b
    for n in ("ln_g", "ln_b"):
        grad[n] = lax.dynamic_slice_in_dim(g_small[n], chip * (D // N_CHIPS), D // N_CHIPS, axis=2)
    out = {"grad": grad, "delta": {}, "new_m": {}, "new_v": {}}

    def update(n):
        shp = w[n].shape
        two_d = (-1, shp[-1])
        res = adamw(w[n].reshape(two_d), grad[n].reshape(two_d), m[n].reshape(two_d), v[n].reshape(two_d), "adamw_" + n,
                    echo=n in REDUCED)
        grad[n] = (res[3] if n in REDUCED else grad[n]).reshape(shp)
        for key, r in zip(("delta", "new_m", "new_v"), res):
            out[key][n] = r.reshape(shp)

    for n in WEIGHTS:
        if n not in late:
            update(n)
    arrive(pending[-1], out["delta"]["ffn2_w_in"])
    grad.update(zip(late, sibling_join([red[n] for n in late], "sibling_join_b")))
    for n in late:
        update(n)
    outs = [loss, dx.reshape(Bl, S, D)]
    for key in ("grad", "delta", "new_m", "new_v"):
        outs += [out[key][n] for n in WEIGHTS]
    return tuple(outs)
```
